```python
import math
import jax, jax.numpy as jnp
from jax import lax
import numpy as np

D_MODEL = 1024
BATCH = 8
SEQ = 4096
DEPTH = 1

ATT_PATTERNS = ((128, 1), (512, 4), (2048, 16))
ATT_GROUPS = len(ATT_PATTERNS)
ATT_HEADS = 8
ATT_HEAD_DIM = 64
ATT_WIDTH = ATT_HEADS * ATT_HEAD_DIM
ATT_BLOCK = 128
RWKV_WIDTH = D_MODEL
RWKV_HEAD_DIM = 64
RWKV_HEADS = RWKV_WIDTH // RWKV_HEAD_DIM
DECAY_LORA = 64
AAA_LORA = 64
GATE_LORA = 160
D_FF = 2816
CONV_WIDTH = 3
N_BRANCHES = 2
ATT_IN = ATT_GROUPS * 3 * ATT_WIDTH
RWKV_IN = 3 * RWKV_WIDTH + DECAY_LORA + AAA_LORA + GATE_LORA
GATE_IN = N_BRANCHES * D_MODEL
N_IN = ATT_IN + RWKV_IN + GATE_IN
RMS_EPS = 1e-6
GN_EPS = 64e-5

kernel_name = 'hybrid_dilated_attn_rwkv7_convffn_adaln'


def rms_norm(x, w):
    xf = x.astype(jnp.float32)
    y = xf * lax.rsqrt(jnp.mean(xf * xf, axis=-1, keepdims=True) + RMS_EPS)
    return (y * w).astype(x.dtype)


def dilated_window_attention(q, k, v, window, dilation):
    b, s, h, e = q.shape
    back = window // dilation
    sub_len = -(-s // dilation)
    n_blk = -(-sub_len // ATT_BLOCK)
    s_pad = n_blk * ATT_BLOCK * dilation

    def to_blocks(t):
        t = jnp.pad(t, ((0, 0), (0, s_pad - s), (0, 0), (0, 0)))
        return t.reshape(b, n_blk, ATT_BLOCK, dilation, h, e)

    def with_prev(t):
        prev = jnp.concatenate([jnp.zeros_like(t[:, :1]), t[:, :-1]], axis=1)
        return jnp.concatenate([prev, t], axis=2)

    qb = to_blocks(q)
    kc = with_prev(to_blocks(k))
    vc = with_prev(to_blocks(v))
    scores = jnp.einsum('bnqrhe,bnkrhe->bnrhqk', qb, kc).astype(jnp.float32) * (e ** -0.5)
    qi = jnp.arange(ATT_BLOCK)[:, None]
    kj = jnp.arange(2 * ATT_BLOCK)[None, :]
    dist = qi + ATT_BLOCK - kj
    kpos = jnp.arange(n_blk)[:, None] * ATT_BLOCK + kj - ATT_BLOCK
    valid = ((dist >= 0) & (dist <= back))[None] & (kpos >= 0)[:, None, :]
    scores = jnp.where(valid[None, :, None, None], scores, -jnp.inf)
    m = jnp.max(scores, axis=-1, keepdims=True)
    p = jnp.exp(scores - m)
    den = jnp.sum(p, axis=-1)
    num = jnp.einsum('bnrhqk,bnkrhe->bnqrhe', p, vc.astype(jnp.float32))
    den_q = jnp.transpose(den, (0, 1, 4, 2, 3))
    out = (num / den_q[..., None]).reshape(b, s_pad, h, e)[:, :s]
    lse = (jnp.transpose(m[..., 0], (0, 1, 4, 2, 3)) + jnp.log(den_q)).reshape(b, s_pad, h)[:, :s]
    return out, lse


def attention_mixer(z):
    b, s, _ = z.shape
    qkv = z.reshape(b, s, ATT_GROUPS, 3, ATT_HEADS, ATT_HEAD_DIM)
    outs, lses = [], []
    for g, (window, dilation) in enumerate(ATT_PATTERNS):
        o, l = dilated_window_attention(qkv[:, :, g, 0], qkv[:, :, g, 1], qkv[:, :, g, 2], window, dilation)
        outs.append(o)
        lses.append(l)
    wts = jax.nn.softmax(jnp.stack(lses), axis=0)
    out = jnp.sum(wts[..., None] * jnp.stack(outs), axis=0)
    return out.reshape(b, s, ATT_WIDTH).astype(z.dtype)


def rwkv7_mixer(z, mu, w0, w2, a0, a2, g2, k_k, k_a, r_k, lnx_w, lnx_b):
    f32 = jnp.float32
    b, s, _ = z.shape
    z_prev = jnp.pad(z, ((0, 0), (1, 0), (0, 0)))[:, :s]
    z = z + (z_prev - z) * mu
    c = RWKV_WIDTH
    r, k, v, w_low, a_low, g_low = jnp.split(
        z, [c, 2 * c, 3 * c, 3 * c + DECAY_LORA, 3 * c + DECAY_LORA + AAA_LORA], axis=-1)
    w_log = -jax.nn.softplus(-(w0 + jnp.tanh(w_low) @ w2).astype(f32)) - 0.5
    decay = jnp.exp(-jnp.exp(w_log))
    a = jax.nn.sigmoid((a0 + a_low @ a2).astype(f32))
    g = jax.nn.sigmoid(g_low) @ g2
    k_mod = k.astype(f32) * (1.0 + (a - 1.0) * k_a)

    def heads(t):
        return t.astype(f32).reshape(b, s, RWKV_HEADS, RWKV_HEAD_DIM)

    kk = heads(k * k_k)
    kk = kk / jnp.maximum(jnp.sqrt(jnp.sum(kk * kk, axis=-1, keepdims=True)), 1e-12)
    r_h, k_h, v_h, w_h, a_h = heads(r), heads(k_mod), heads(v), heads(decay), heads(a)

    def step(state, inp):
        r_t, w_t, k_t, v_t, aa_t, bb_t = inp
        sa = jnp.einsum('bhvk,bhk->bhv', state, aa_t)
        state = state * w_t[:, :, None, :] + sa[..., None] * bb_t[:, :, None, :] + v_t[..., None] * k_t[:, :, None, :]
        return state, jnp.einsum('bhvk,bhk->bhv', state, r_t)

    tm = lambda t: jnp.swapaxes(t, 0, 1)
    state0 = jnp.zeros((b, RWKV_HEADS, RWKV_HEAD_DIM, RWKV_HEAD_DIM), f32)
    _, y = lax.scan(step, state0, (tm(r_h), tm(w_h), tm(k_h), tm(v_h), tm(-kk), tm(kk * a_h)))
    y = tm(y)
    mean = jnp.mean(y, axis=-1, keepdims=True)
    var = jnp.mean(jnp.square(y - mean), axis=-1, keepdims=True)
    y = ((y - mean) * lax.rsqrt(var + GN_EPS)).reshape(b, s, c) * lnx_w + lnx_b
    bonus = (jnp.sum(r_h * k_h * r_k, axis=-1, keepdims=True) * v_h).reshape(b, s, c)
    return ((y + bonus) * g).astype(z.dtype)


def conv_ffn(h, w_up, conv_w, conv_b, w_down):
    s = h.shape[1]
    u = h @ w_up
    up = jnp.pad(u, ((0, 0), (CONV_WIDTH - 1, 0), (0, 0)))
    u = conv_b + sum(conv_w[j] * up[:, j:j + s] for j in range(CONV_WIDTH))
    gate, val = jnp.split(u, 2, axis=-1)
    return (jax.nn.silu(gate) * val) @ w_down


def _fwd_setup_inputs(seed: int = 0) -> dict:
    key = jax.random.key(seed)
    ks = iter(jax.random.split(key, 32))
    f32 = jnp.float32
    L, D, C = DEPTH, D_MODEL, RWKV_WIDTH

    def nrm(shape, scale):
        return jax.random.normal(next(ks), shape, f32) * scale

    ramp = (jnp.arange(C, dtype=f32) / (C - 1)) ** 0.85
    inputs = {}
    inputs['x'] = nrm((BATCH, SEQ, D), 1.0)
    inputs['c'] = nrm((BATCH, D), 1.0)
    inputs['w_ada'] = nrm((L, D, 6 * D), 0.3 * D ** -0.5)
    inputs['b_ada'] = nrm((L, 6 * D), 0.02)
    inputs['norm1_w'] = 1.0 + nrm((L, D), 0.05)
    inputs['w_in'] = nrm((L, D, N_IN), D ** -0.5)
    inputs['b_gate'] = nrm((L, GATE_IN), 0.1)
    inputs['mu_shift'] = jax.random.uniform(next(ks), (L, RWKV_IN), f32)
    inputs['w0'] = -6.5 + 5.0 * ramp + nrm((L, C), 0.1)
    inputs['w2'] = nrm((L, DECAY_LORA, C), 0.1 * DECAY_LORA ** -0.5)
    inputs['a0'] = nrm((L, C), 0.1)
    inputs['a2'] = nrm((L, AAA_LORA, C), AAA_LORA ** -0.5)
    inputs['g2'] = nrm((L, GATE_LORA, C), GATE_LORA ** -0.5)
    inputs['k_k'] = 0.85 + nrm((L, C), 0.05)
    inputs['k_a'] = 1.0 + nrm((L, C), 0.05)
    inputs['r_k'] = nrm((L, RWKV_HEADS, RWKV_HEAD_DIM), 0.1)
    inputs['lnx_w'] = 1.0 + nrm((L, C), 0.05)
    inputs['lnx_b'] = nrm((L, C), 0.02)
    inputs['w_att_out'] = nrm((L, ATT_WIDTH, D), ATT_WIDTH ** -0.5)
    inputs['w_rwkv_out'] = nrm((L, C, D), C ** -0.5)
    inputs['w_o'] = nrm((L, D, D), D ** -0.5)
    inputs['norm2_w'] = 1.0 + nrm((L, D), 0.05)
    inputs['w_up'] = nrm((L, D, 2 * D_FF), D ** -0.5)
    inputs['conv_w'] = nrm((L, CONV_WIDTH, 2 * D_FF), CONV_WIDTH ** -0.5)
    inputs['conv_b'] = nrm((L, 2 * D_FF), 0.02)
    inputs['w_down'] = nrm((L, D_FF, D), D_FF ** -0.5)
    inputs['norm_f_w'] = 1.0 + nrm((D,), 0.05)
    return inputs


def _fwd_reference(x, c, w_ada, b_ada, norm1_w, w_in, b_gate, mu_shift, w0, w2, a0, a2, g2, k_k, k_a, r_k,
              lnx_w, lnx_b, w_att_out, w_rwkv_out, w_o, norm2_w, w_up, conv_w, conv_b, w_down, norm_f_w):
    for l in range(DEPTH):
        ada = (c @ w_ada[l] + b_ada[l])[:, None, :]
        sh1, sc1, gt1, sh2, sc2, gt2 = jnp.split(ada, 6, axis=-1)
        h = rms_norm(x, norm1_w[l]) * (1.0 + sc1) + sh1
        proj = h @ w_in[l]
        att_in, rwkv_in, gate_in = jnp.split(proj, [ATT_IN, ATT_IN + RWKV_IN], axis=-1)
        y_att = attention_mixer(att_in) @ w_att_out[l]
        y_rwkv = rwkv7_mixer(rwkv_in, mu_shift[l], w0[l], w2[l], a0[l], a2[l], g2[l], k_k[l], k_a[l],
                             r_k[l], lnx_w[l], lnx_b[l]) @ w_rwkv_out[l]
        g_att, g_rwkv = jnp.split(jax.nn.sigmoid(gate_in + b_gate[l]), N_BRANCHES, axis=-1)
        x = x + gt1 * ((g_att * y_att + g_rwkv * y_rwkv) @ w_o[l])
        h = rms_norm(x, norm2_w[l]) * (1.0 + sc2) + sh2
        x = x + gt2 * conv_ffn(h, w_up[l], conv_w[l], conv_b[l], w_down[l])
    return rms_norm(x, norm_f_w)


import jax as _jax
import jax.numpy as _jnp

TWIN_FORMAT = 'train_step'
FWD_PARAMS = ['x', 'c', 'w_ada', 'b_ada', 'norm1_w', 'w_in', 'b_gate', 'mu_shift', 'w0', 'w2', 'a0', 'a2', 'g2', 'k_k', 'k_a', 'r_k', 'lnx_w', 'lnx_b', 'w_att_out', 'w_rwkv_out', 'w_o', 'norm2_w', 'w_up', 'conv_w', 'conv_b', 'w_down', 'norm_f_w']
TWIN_WEIGHTS = ['w_ada', 'b_ada', 'norm1_w', 'w_in', 'b_gate', 'mu_shift', 'w0', 'w2', 'a0', 'a2', 'g2', 'k_k', 'k_a', 'r_k', 'lnx_w', 'lnx_b', 'w_att_out', 'w_rwkv_out', 'w_o', 'norm2_w', 'w_up', 'conv_w', 'conv_b', 'w_down', 'norm_f_w']
TWIN_DIFF_INPUT = 'x'
TWIN_INPUTS = ['x', 'c', 'w_ada', 'b_ada', 'norm1_w', 'w_in', 'b_gate', 'mu_shift', 'w0', 'w2', 'a0', 'a2', 'g2', 'k_k', 'k_a', 'r_k', 'lnx_w', 'lnx_b', 'w_att_out', 'w_rwkv_out', 'w_o', 'norm2_w', 'w_up', 'conv_w', 'conv_b', 'w_down', 'norm_f_w', 'loss_target', 'm_w_ada', 'm_b_ada', 'm_norm1_w', 'm_w_in', 'm_b_gate', 'm_mu_shift', 'm_w0', 'm_w2', 'm_a0', 'm_a2', 'm_g2', 'm_k_k', 'm_k_a', 'm_r_k', 'm_lnx_w', 'm_lnx_b', 'm_w_att_out', 'm_w_rwkv_out', 'm_w_o', 'm_norm2_w', 'm_w_up', 'm_conv_w', 'm_conv_b', 'm_w_down', 'm_norm_f_w', 'v_w_ada', 'v_b_ada', 'v_norm1_w', 'v_w_in', 'v_b_gate', 'v_mu_shift', 'v_w0', 'v_w2', 'v_a0', 'v_a2', 'v_g2', 'v_k_k', 'v_k_a', 'v_r_k', 'v_lnx_w', 'v_lnx_b', 'v_w_att_out', 'v_w_rwkv_out', 'v_w_o', 'v_norm2_w', 'v_w_up', 'v_conv_w', 'v_conv_b', 'v_w_down', 'v_norm_f_w']
TWIN_OUTPUTS = ['loss', 'grad_x', 'grad_w_ada', 'grad_b_ada', 'grad_norm1_w', 'grad_w_in', 'grad_b_gate', 'grad_mu_shift', 'grad_w0', 'grad_w2', 'grad_a0', 'grad_a2', 'grad_g2', 'grad_k_k', 'grad_k_a', 'grad_r_k', 'grad_lnx_w', 'grad_lnx_b', 'grad_w_att_out', 'grad_w_rwkv_out', 'grad_w_o', 'grad_norm2_w', 'grad_w_up', 'grad_conv_w', 'grad_conv_b', 'grad_w_down', 'grad_norm_f_w', 'delta_w_ada', 'delta_b_ada', 'delta_norm1_w', 'delta_w_in', 'delta_b_gate', 'delta_mu_shift', 'delta_w0', 'delta_w2', 'delta_a0', 'delta_a2', 'delta_g2', 'delta_k_k', 'delta_k_a', 'delta_r_k', 'delta_lnx_w', 'delta_lnx_b', 'delta_w_att_out', 'delta_w_rwkv_out', 'delta_w_o', 'delta_norm2_w', 'delta_w_up', 'delta_conv_w', 'delta_conv_b', 'delta_w_down', 'delta_norm_f_w', 'new_m_w_ada', 'new_m_b_ada', 'new_m_norm1_w', 'new_m_w_in', 'new_m_b_gate', 'new_m_mu_shift', 'new_m_w0', 'new_m_w2', 'new_m_a0', 'new_m_a2', 'new_m_g2', 'new_m_k_k', 'new_m_k_a', 'new_m_r_k', 'new_m_lnx_w', 'new_m_lnx_b', 'new_m_w_att_out', 'new_m_w_rwkv_out', 'new_m_w_o', 'new_m_norm2_w', 'new_m_w_up', 'new_m_conv_w', 'new_m_conv_b', 'new_m_w_down', 'new_m_norm_f_w', 'new_v_w_ada', 'new_v_b_ada', 'new_v_norm1_w', 'new_v_w_in', 'new_v_b_gate', 'new_v_mu_shift', 'new_v_w0', 'new_v_w2', 'new_v_a0', 'new_v_a2', 'new_v_g2', 'new_v_k_k', 'new_v_k_a', 'new_v_r_k', 'new_v_lnx_w', 'new_v_lnx_b', 'new_v_w_att_out', 'new_v_w_rwkv_out', 'new_v_w_o', 'new_v_norm2_w', 'new_v_w_up', 'new_v_conv_w', 'new_v_conv_b', 'new_v_w_down', 'new_v_norm_f_w']
TWIN_LEAF_KINDS = {'loss': 'loss', 'grad_x': 'grad_x', 'grad_w_ada': 'grad_w', 'grad_b_ada': 'grad_w', 'grad_norm1_w': 'grad_w', 'grad_w_in': 'grad_w', 'grad_b_gate': 'grad_w', 'grad_mu_shift': 'grad_w', 'grad_w0': 'grad_w', 'grad_w2': 'grad_w', 'grad_a0': 'grad_w', 'grad_a2': 'grad_w', 'grad_g2': 'grad_w', 'grad_k_k': 'grad_w', 'grad_k_a': 'grad_w', 'grad_r_k': 'grad_w', 'grad_lnx_w': 'grad_w', 'grad_lnx_b': 'grad_w', 'grad_w_att_out': 'grad_w', 'grad_w_rwkv_out': 'grad_w', 'grad_w_o': 'grad_w', 'grad_norm2_w': 'grad_w', 'grad_w_up': 'grad_w', 'grad_conv_w': 'grad_w', 'grad_conv_b': 'grad_w', 'grad_w_down': 'grad_w', 'grad_norm_f_w': 'grad_w', 'delta_w_ada': 'delta_w', 'delta_b_ada': 'delta_w', 'delta_norm1_w': 'delta_w', 'delta_w_in': 'delta_w', 'delta_b_gate': 'delta_w', 'delta_mu_shift': 'delta_w', 'delta_w0': 'delta_w', 'delta_w2': 'delta_w', 'delta_a0': 'delta_w', 'delta_a2': 'delta_w', 'delta_g2': 'delta_w', 'delta_k_k': 'delta_w', 'delta_k_a': 'delta_w', 'delta_r_k': 'delta_w', 'delta_lnx_w': 'delta_w', 'delta_lnx_b': 'delta_w', 'delta_w_att_out': 'delta_w', 'delta_w_rwkv_out': 'delta_w', 'delta_w_o': 'delta_w', 'delta_norm2_w': 'delta_w', 'delta_w_up': 'delta_w', 'delta_conv_w': 'delta_w', 'delta_conv_b': 'delta_w', 'delta_w_down': 'delta_w', 'delta_norm_f_w': 'delta_w', 'new_m_w_ada': 'new_m', 'new_m_b_ada': 'new_m', 'new_m_norm1_w': 'new_m', 'new_m_w_in': 'new_m', 'new_m_b_gate': 'new_m', 'new_m_mu_shift': 'new_m', 'new_m_w0': 'new_m', 'new_m_w2': 'new_m', 'new_m_a0': 'new_m', 'new_m_a2': 'new_m', 'new_m_g2': 'new_m', 'new_m_k_k': 'new_m', 'new_m_k_a': 'new_m', 'new_m_r_k': 'new_m', 'new_m_lnx_w': 'new_m', 'new_m_lnx_b': 'new_m', 'new_m_w_att_out': 'new_m', 'new_m_w_rwkv_out': 'new_m', 'new_m_w_o': 'new_m', 'new_m_norm2_w': 'new_m', 'new_m_w_up': 'new_m', 'new_m_conv_w': 'new_m', 'new_m_conv_b': 'new_m', 'new_m_w_down': 'new_m', 'new_m_norm_f_w': 'new_m', 'new_v_w_ada': 'new_v', 'new_v_b_ada': 'new_v', 'new_v_norm1_w': 'new_v', 'new_v_w_in': 'new_v', 'new_v_b_gate': 'new_v', 'new_v_mu_shift': 'new_v', 'new_v_w0': 'new_v', 'new_v_w2': 'new_v', 'new_v_a0': 'new_v', 'new_v_a2': 'new_v', 'new_v_g2': 'new_v', 'new_v_k_k': 'new_v', 'new_v_k_a': 'new_v', 'new_v_r_k': 'new_v', 'new_v_lnx_w': 'new_v', 'new_v_lnx_b': 'new_v', 'new_v_w_att_out': 'new_v', 'new_v_w_rwkv_out': 'new_v', 'new_v_w_o': 'new_v', 'new_v_norm2_w': 'new_v', 'new_v_w_up': 'new_v', 'new_v_conv_w': 'new_v', 'new_v_conv_b': 'new_v', 'new_v_w_down': 'new_v', 'new_v_norm_f_w': 'new_v'}


def _forward(args):
    return _fwd_reference(*[args[k] for k in FWD_PARAMS])


def _output_shape():
    out = _jax.eval_shape(lambda: _forward(_fwd_setup_inputs(0)))
    return out.shape, out.dtype

N_MICROBATCH = 1
ADAM_LR = 0.001
ADAM_B1 = 0.9
ADAM_B2 = 0.999
ADAM_EPS = 1e-08
ADAM_WD = 0.01
ADAM_STEP = 10
PER_EXAMPLE_BATCH_AXIS = {'x': 0, 'c': 0, 'loss_target': 0}
SHARED_INPUTS = []
_WEIGHT_DTYPES = {'w_ada': _jnp.float32, 'b_ada': _jnp.float32, 'norm1_w': _jnp.float32, 'w_in': _jnp.float32, 'b_gate': _jnp.float32, 'mu_shift': _jnp.float32, 'w0': _jnp.float32, 'w2': _jnp.float32, 'a0': _jnp.float32, 'a2': _jnp.float32, 'g2': _jnp.float32, 'k_k': _jnp.float32, 'k_a': _jnp.float32, 'r_k': _jnp.float32, 'lnx_w': _jnp.float32, 'lnx_b': _jnp.float32, 'w_att_out': _jnp.float32, 'w_rwkv_out': _jnp.float32, 'w_o': _jnp.float32, 'norm2_w': _jnp.float32, 'w_up': _jnp.float32, 'conv_w': _jnp.float32, 'conv_b': _jnp.float32, 'w_down': _jnp.float32, 'norm_f_w': _jnp.float32}
MOMENT_SCALE = {'w_ada': 9.341117e-02, 'b_ada': 1.002065e-01, 'norm1_w': 3.671224e-02, 'w_in': 1.197897e-02, 'b_gate': 5.451807e-03, 'mu_shift': 3.162792e-02, 'w0': 1.160874e-02, 'w2': 2.295053e-03, 'a0': 7.018198e-03, 'a2': 6.806741e-03, 'g2': 1.725931e-02, 'k_k': 3.254307e-02, 'k_a': 2.904764e-02, 'r_k': 5.735245e-02, 'lnx_w': 1.652477e-02, 'lnx_b': 2.347615e-02, 'w_att_out': 9.755699e-03, 'w_rwkv_out': 1.763202e-02, 'w_o': 2.021554e-02, 'norm2_w': 5.641170e-02, 'w_up': 2.425293e-02, 'conv_w': 2.422406e-02, 'conv_b': 2.432368e-02, 'w_down': 3.977109e-02, 'norm_f_w': 3.202379e+01}


def _to_microbatches(a, axis):
    t = _jnp.moveaxis(a, axis, 0)
    t = t.reshape((N_MICROBATCH, t.shape[0] // N_MICROBATCH) + t.shape[1:])
    return _jnp.moveaxis(t, 1, axis + 1)


def setup_inputs(seed: int = 0) -> dict:
    inp = _fwd_setup_inputs(seed)
    key = _jax.random.fold_in(_jax.random.key(seed), 7919)
    shape, _ = _output_shape()
    out = dict(inp)
    out["loss_target"] = _jax.random.normal(_jax.random.fold_in(key, 0), shape, _jnp.float32)
    for i, name in enumerate(TWIN_WEIGHTS):
        w = inp[name].astype(_jnp.float32)
        if MOMENT_SCALE is None:
            s = _jnp.sqrt(_jnp.mean(_jnp.square(w)) + 1e-30)
        else:
            s = MOMENT_SCALE[name]
        km, kv = _jax.random.split(_jax.random.fold_in(key, i + 1))
        out[name] = w
        out["m_" + name] = s * _jax.random.normal(km, w.shape, _jnp.float32)
        out["v_" + name] = (s * s) * _jax.random.uniform(kv, w.shape, _jnp.float32, 0.5, 1.5)
    if N_MICROBATCH > 1:
        for name, axis in PER_EXAMPLE_BATCH_AXIS.items():
            out[name] = _to_microbatches(out[name], axis)
    return {'x': out['x'], 'c': out['c'], 'w_ada': out['w_ada'], 'b_ada': out['b_ada'], 'norm1_w': out['norm1_w'], 'w_in': out['w_in'], 'b_gate': out['b_gate'], 'mu_shift': out['mu_shift'], 'w0': out['w0'], 'w2': out['w2'], 'a0': out['a0'], 'a2': out['a2'], 'g2': out['g2'], 'k_k': out['k_k'], 'k_a': out['k_a'], 'r_k': out['r_k'], 'lnx_w': out['lnx_w'], 'lnx_b': out['lnx_b'], 'w_att_out': out['w_att_out'], 'w_rwkv_out': out['w_rwkv_out'], 'w_o': out['w_o'], 'norm2_w': out['norm2_w'], 'w_up': out['w_up'], 'conv_w': out['conv_w'], 'conv_b': out['conv_b'], 'w_down': out['w_down'], 'norm_f_w': out['norm_f_w'], 'loss_target': out['loss_target'], 'm_w_ada': out['m_w_ada'], 'm_b_ada': out['m_b_ada'], 'm_norm1_w': out['m_norm1_w'], 'm_w_in': out['m_w_in'], 'm_b_gate': out['m_b_gate'], 'm_mu_shift': out['m_mu_shift'], 'm_w0': out['m_w0'], 'm_w2': out['m_w2'], 'm_a0': out['m_a0'], 'm_a2': out['m_a2'], 'm_g2': out['m_g2'], 'm_k_k': out['m_k_k'], 'm_k_a': out['m_k_a'], 'm_r_k': out['m_r_k'], 'm_lnx_w': out['m_lnx_w'], 'm_lnx_b': out['m_lnx_b'], 'm_w_att_out': out['m_w_att_out'], 'm_w_rwkv_out': out['m_w_rwkv_out'], 'm_w_o': out['m_w_o'], 'm_norm2_w': out['m_norm2_w'], 'm_w_up': out['m_w_up'], 'm_conv_w': out['m_conv_w'], 'm_conv_b': out['m_conv_b'], 'm_w_down': out['m_w_down'], 'm_norm_f_w': out['m_norm_f_w'], 'v_w_ada': out['v_w_ada'], 'v_b_ada': out['v_b_ada'], 'v_norm1_w': out['v_norm1_w'], 'v_w_in': out['v_w_in'], 'v_b_gate': out['v_b_gate'], 'v_mu_shift': out['v_mu_shift'], 'v_w0': out['v_w0'], 'v_w2': out['v_w2'], 'v_a0': out['v_a0'], 'v_a2': out['v_a2'], 'v_g2': out['v_g2'], 'v_k_k': out['v_k_k'], 'v_k_a': out['v_k_a'], 'v_r_k': out['v_r_k'], 'v_lnx_w': out['v_lnx_w'], 'v_lnx_b': out['v_lnx_b'], 'v_w_att_out': out['v_w_att_out'], 'v_w_rwkv_out': out['v_w_rwkv_out'], 'v_w_o': out['v_w_o'], 'v_norm2_w': out['v_norm2_w'], 'v_w_up': out['v_w_up'], 'v_conv_w': out['v_conv_w'], 'v_conv_b': out['v_conv_b'], 'v_w_down': out['v_w_down'], 'v_norm_f_w': out['v_norm_f_w']}


def _loss(weights, diff, rest, loss_target):
    with _jax.named_scope("forward"):
        args = {**rest, TWIN_DIFF_INPUT: diff, **{k: w.astype(_WEIGHT_DTYPES[k]) for k, w in weights.items()}}
        y = _forward(args)
    with _jax.named_scope("loss_head"):
        err = _jnp.square(y.astype(_jnp.float32) - loss_target)
        return 0.5 * _jnp.sum(_jnp.mean(err, axis=-1)) if err.ndim else 0.5 * err


def _adamw(w, g, m, v):
    m = ADAM_B1 * m + (1.0 - ADAM_B1) * g
    v = ADAM_B2 * v + (1.0 - ADAM_B2) * _jnp.square(g)
    m_hat = m / (1.0 - ADAM_B1 ** ADAM_STEP)
    v_hat = v / (1.0 - ADAM_B2 ** ADAM_STEP)
    delta = -ADAM_LR * (m_hat / (_jnp.sqrt(v_hat) + ADAM_EPS) + ADAM_WD * w)
    return delta, m, v


def reference(x, c, w_ada, b_ada, norm1_w, w_in, b_gate, mu_shift, w0, w2, a0, a2, g2, k_k, k_a, r_k, lnx_w, lnx_b, w_att_out, w_rwkv_out, w_o, norm2_w, w_up, conv_w, conv_b, w_down, norm_f_w, loss_target, m_w_ada, m_b_ada, m_norm1_w, m_w_in, m_b_gate, m_mu_shift, m_w0, m_w2, m_a0, m_a2, m_g2, m_k_k, m_k_a, m_r_k, m_lnx_w, m_lnx_b, m_w_att_out, m_w_rwkv_out, m_w_o, m_norm2_w, m_w_up, m_conv_w, m_conv_b, m_w_down, m_norm_f_w, v_w_ada, v_b_ada, v_norm1_w, v_w_in, v_b_gate, v_mu_shift, v_w0, v_w2, v_a0, v_a2, v_g2, v_k_k, v_k_a, v_r_k, v_lnx_w, v_lnx_b, v_w_att_out, v_w_rwkv_out, v_w_o, v_norm2_w, v_w_up, v_conv_w, v_conv_b, v_w_down, v_norm_f_w):
    given = dict(x=x, c=c, w_ada=w_ada, b_ada=b_ada, norm1_w=norm1_w, w_in=w_in, b_gate=b_gate, mu_shift=mu_shift, w0=w0, w2=w2, a0=a0, a2=a2, g2=g2, k_k=k_k, k_a=k_a, r_k=r_k, lnx_w=lnx_w, lnx_b=lnx_b, w_att_out=w_att_out, w_rwkv_out=w_rwkv_out, w_o=w_o, norm2_w=norm2_w, w_up=w_up, conv_w=conv_w, conv_b=conv_b, w_down=w_down, norm_f_w=norm_f_w, loss_target=loss_target, m_w_ada=m_w_ada, m_b_ada=m_b_ada, m_norm1_w=m_norm1_w, m_w_in=m_w_in, m_b_gate=m_b_gate, m_mu_shift=m_mu_shift, m_w0=m_w0, m_w2=m_w2, m_a0=m_a0, m_a2=m_a2, m_g2=m_g2, m_k_k=m_k_k, m_k_a=m_k_a, m_r_k=m_r_k, m_lnx_w=m_lnx_w, m_lnx_b=m_lnx_b, m_w_att_out=m_w_att_out, m_w_rwkv_out=m_w_rwkv_out, m_w_o=m_w_o, m_norm2_w=m_norm2_w, m_w_up=m_w_up, m_conv_w=m_conv_w, m_conv_b=m_conv_b, m_w_down=m_w_down, m_norm_f_w=m_norm_f_w, v_w_ada=v_w_ada, v_b_ada=v_b_ada, v_norm1_w=v_norm1_w, v_w_in=v_w_in, v_b_gate=v_b_gate, v_mu_shift=v_mu_shift, v_w0=v_w0, v_w2=v_w2, v_a0=v_a0, v_a2=v_a2, v_g2=v_g2, v_k_k=v_k_k, v_k_a=v_k_a, v_r_k=v_r_k, v_lnx_w=v_lnx_w, v_lnx_b=v_lnx_b, v_w_att_out=v_w_att_out, v_w_rwkv_out=v_w_rwkv_out, v_w_o=v_w_o, v_norm2_w=v_norm2_w, v_w_up=v_w_up, v_conv_w=v_conv_w, v_conv_b=v_conv_b, v_w_down=v_w_down, v_norm_f_w=v_norm_f_w)
    weights = {n: given[n] for n in TWIN_WEIGHTS}
    shared = {n: given[n] for n in SHARED_INPUTS}
    per_example = {n: given[n] for n in ['x', 'c']}
    grad_fn = _jax.value_and_grad(_loss, argnums=(0, 1))

    def one_microbatch(ex, loss_target):
        ex = dict(ex)
        diff = ex.pop(TWIN_DIFF_INPUT)
        return grad_fn(weights, diff, {**shared, **ex}, loss_target)

    if N_MICROBATCH == 1:
        loss, (grad_w, grad_x) = one_microbatch(per_example, given["loss_target"])
    else:
        def body(carry, xs):
            loss_sum, grad_sum = carry
            l_k, (gw_k, gx_k) = one_microbatch(xs[0], xs[1])
            with _jax.named_scope("update"):
                return (loss_sum + l_k, _jax.tree.map(_jnp.add, grad_sum, gw_k)), gx_k

        init = (_jnp.zeros((), _jnp.float32), _jax.tree.map(_jnp.zeros_like, weights))
        (loss, grad_w), grad_x = _jax.lax.scan(body, init, (per_example, given["loss_target"]))
    with _jax.named_scope("update"):
        delta_w, new_m, new_v = {}, {}, {}
        for n in TWIN_WEIGHTS:
            delta_w[n], new_m[n], new_v[n] = _adamw(weights[n], grad_w[n], given["m_" + n], given["v_" + n])
    return (loss, grad_x, *[grad_w[n] for n in TWIN_WEIGHTS], *[delta_w[n] for n in TWIN_WEIGHTS],
            *[new_m[n] for n in TWIN_WEIGHTS], *[new_v[n] for n in TWIN_WEIGHTS])
```

```python
import functools

import jax
import jax.numpy as jnp
from jax import lax
from jax.experimental import pallas as pl
from jax.experimental.pallas import tpu as pltpu

F32 = jnp.float32
BF16 = jnp.bfloat16
HI = lax.Precision.HIGHEST
MESH = pl.DeviceIdType.MESH

D = 1024
ATT_PATTERNS = ((128, 1), (512, 4), (2048, 16))
ATT_BLOCK = 128
ATT_WIDTH = 512
N_ATT = 3 * 3 * ATT_WIDTH
N_RW = 3 * D + 64 + 64 + 160
N_RWP = 3456
N_LORA = N_RWP - 3 * D
N_GATE = 2 * D
D_FF = 2816
RMS_EPS = 1e-6
GN_EPS = 64e-5
SCAN_CHUNK = 64
NEG = -1e30
VMEM_LIMIT = 48 * 1024 * 1024

ADAM_LR, ADAM_B1, ADAM_B2, ADAM_EPS, ADAM_WD, ADAM_STEP = 0.001, 0.9, 0.999, 1e-08, 0.01, 10


def _pcall(body, **kw):
    return pl.pallas_call(body, **kw)


def _cparams(sem):
    return pltpu.CompilerParams(dimension_semantics=sem, vmem_limit_bytes=VMEM_LIMIT)


def _div(n, pref, mult):
    best = None
    d = mult
    while d <= min(n, pref):
        if n % d == 0:
            best = d
        d += mult
    return best if best else n


def _dg(a, b, ca, cb):
    return lax.dot_general(a.astype(BF16), b.astype(BF16), (((ca,), (cb,)), ((), ())), preferred_element_type=F32)


@jax.custom_vjp
def _nn(a, b):
    return _dg(a, b, 1, 0)


@jax.custom_vjp
def _nt(a, b):
    return _dg(a, b, 1, 1)


@jax.custom_vjp
def _tn(a, b):
    return _dg(a, b, 0, 0)


_nn.defvjp(lambda a, b: (_nn(a, b), (a, b)), lambda res, g: (_nt(g, res[1]), _tn(res[0], g)))
_nt.defvjp(lambda a, b: (_nt(a, b), (a, b)), lambda res, g: (_nn(g, res[1]), _tn(g, res[0])))
_tn.defvjp(lambda a, b: (_tn(a, b), (a, b)), lambda res, g: (_nt(res[1], g), _nn(res[0], g)))


def _split2(x):
    hi = x.astype(BF16)
    lo = (x - hi.astype(F32)).astype(BF16)
    return hi, lo


def _hsum_impl(x, e, et):
    eb, etb = e.astype(BF16), et.astype(BF16)
    hi, lo = _split2(x)
    s = jnp.dot(hi, eb, preferred_element_type=F32) + jnp.dot(lo, eb, preferred_element_type=F32)
    shi, slo = _split2(s)
    return jnp.dot(shi, etb, preferred_element_type=F32) + jnp.dot(slo, etb, preferred_element_type=F32)


@jax.custom_vjp
def _hsum(x, e, et):
    return _hsum_impl(x, e, et)


_hsum.defvjp(lambda x, e, et: (_hsum_impl(x, e, et), (e, et)),
             lambda res, g: (_hsum_impl(g, res[0], res[1]), jnp.zeros_like(res[0]), jnp.zeros_like(res[1])))


def _mm(a, b, *, ta=False, tb=False, out_dtype=F32, add=None, name):
    if ta:
        kdim, m = a.shape
    else:
        m, kdim = a.shape
    n = b.shape[0] if tb else b.shape[1]
    tm, tn, tk = _div(m, 1024, 128), _div(n, 1024, 128), _div(kdim, 512, 128)
    nk = kdim // tk
    ca, cb = (0 if ta else 1), (1 if tb else 0)

    def body(*refs):
        if add is None:
            a_ref, b_ref, o_ref, acc = refs
        else:
            a_ref, b_ref, add_ref, o_ref, acc = refs
        k = pl.program_id(2)

        @pl.when(k == 0)
        def _():
            acc[...] = jnp.zeros_like(acc)

        acc[...] += lax.dot_general(a_ref[...], b_ref[...], (((ca,), (cb,)), ((), ())), preferred_element_type=F32)

        @pl.when(k == nk - 1)
        def _():
            r = acc[...]
            if add is not None:
                r = r + add_ref[...]
            o_ref[...] = r.astype(o_ref.dtype)

    a_spec = pl.BlockSpec((tk, tm), lambda i, j, k: (k, i)) if ta else pl.BlockSpec((tm, tk), lambda i, j, k: (i, k))
    b_spec = pl.BlockSpec((tn, tk), lambda i, j, k: (j, k)) if tb else pl.BlockSpec((tk, tn), lambda i, j, k: (k, j))
    in_specs = [a_spec, b_spec]
    args = [a, b]
    if add is not None:
        in_specs.append(pl.BlockSpec((tm, tn), lambda i, j, k: (i, j)))
        args.append(add)
    return _pcall(
        body, name=name, grid=(m // tm, n // tn, nk), in_specs=in_specs,
        out_specs=pl.BlockSpec((tm, tn), lambda i, j, k: (i, j)),
        out_shape=jax.ShapeDtypeStruct((m, n), out_dtype),
        scratch_shapes=[pltpu.VMEM((tm, tn), F32)],
        compiler_params=_cparams(("parallel", "parallel", "arbitrary")),
    )(*args)


def _row_spec(br, w, cb):
    return pl.BlockSpec((br, w), lambda i: (i, cb))


def _const_spec(shape):
    return pl.BlockSpec(shape, lambda i: (0,) * len(shape))


def _rows_fwd(fn, rows, consts, outs, *, name, br, acc_shape=None):
    s = rows[0][0].shape[0]
    nr, nc = len(rows), len(consts)
    kept = [k for k, o in enumerate(outs) if o is not None]

    def body(*refs):
        xs = [r[...].astype(F32) for r in refs[:nr]]
        cs = [c[...] for c in refs[nr:nr + nc]]
        res = fn(*xs, *cs)
        orefs = refs[nr + nc:]
        for j, k in enumerate(kept):
            orefs[j][...] = res[k].astype(orefs[j].dtype)
        if acc_shape is not None:
            acc_ref = orefs[len(kept)]

            @pl.when(pl.program_id(0) == 0)
            def _():
                acc_ref[...] = jnp.zeros_like(acc_ref)

            acc_ref[...] += res[len(outs)]

    in_specs = [_row_spec(br, w, cb) for (_, w, cb) in rows] + [_const_spec(c.shape) for c in consts]
    out_specs = [_row_spec(br, outs[k][0], 0) for k in kept]
    out_shape = [jax.ShapeDtypeStruct((s, outs[k][0]), outs[k][1]) for k in kept]
    if acc_shape is not None:
        out_specs.append(_const_spec(acc_shape))
        out_shape.append(jax.ShapeDtypeStruct(acc_shape, F32))
    return _pcall(
        body, name=name, grid=(s // br,), in_specs=in_specs, out_specs=out_specs, out_shape=out_shape,
        compiler_params=_cparams(("arbitrary",)),
    )(*[r[0] for r in rows], *consts)


def _rows_bwd(fn, rows, consts, cots, *, wrt_rows, wrt_consts, drow_dtypes, name, br, unit_cot=False):
    s = rows[0][0].shape[0]
    nr, nc = len(rows), len(consts)
    flat_cots = [c for lst in cots for c in lst]
    ncot = len(flat_cots)

    def body(*refs):
        xs = [r[...].astype(F32) for r in refs[:nr]]
        cs = [c[...] for c in refs[nr:nr + nc]]
        cvals = [c[...].astype(F32) for c in refs[nr + nc:nr + nc + ncot]]
        orefs = refs[nr + nc + ncot:]

        def g(*d):
            xs2, cs2 = list(xs), list(cs)
            for j, k in enumerate(wrt_rows):
                xs2[k] = d[j]
            for j, k in enumerate(wrt_consts):
                cs2[k] = d[len(wrt_rows) + j]
            return tuple(fn(*xs2, *cs2))

        prim = [xs[k] for k in wrt_rows] + [cs[k] for k in wrt_consts]
        outs, vjp = jax.vjp(g, *prim)
        ct = []
        pos = 0
        for o, lst in zip(outs, cots):
            if unit_cot:
                ct.append(jnp.ones_like(o))
                continue
            acc = jnp.zeros_like(o)
            for _ in lst:
                acc = acc + cvals[pos]
                pos += 1
            ct.append(acc)
        grads = vjp(tuple(ct))
        for j in range(len(wrt_rows)):
            orefs[j][...] = grads[j].astype(orefs[j].dtype)

        @pl.when(pl.program_id(0) == 0)
        def _():
            for j in range(len(wrt_consts)):
                oref = orefs[len(wrt_rows) + j]
                oref[...] = jnp.zeros_like(oref)

        for j in range(len(wrt_consts)):
            orefs[len(wrt_rows) + j][...] += grads[len(wrt_rows) + j]

    in_specs = ([_row_spec(br, w, cb) for (_, w, cb) in rows] + [_const_spec(c.shape) for c in consts]
                + [_row_spec(br, w, cb) for (_, w, cb) in flat_cots])
    out_specs = [_row_spec(br, rows[k][1], 0) for k in wrt_rows] + [_const_spec(consts[k].shape) for k in wrt_consts]
    out_shape = ([jax.ShapeDtypeStruct((s, rows[k][1]), dt) for k, dt in zip(wrt_rows, drow_dtypes)]
                 + [jax.ShapeDtypeStruct(consts[k].shape, F32) for k in wrt_consts])
    return _pcall(
        body, name=name, grid=(s // br,), in_specs=in_specs, out_specs=out_specs, out_shape=out_shape,
        compiler_params=_cparams(("arbitrary",)),
    )(*[r[0] for r in rows], *consts, *[c[0] for c in flat_cots])


def _rms(x, w):
    return x * lax.rsqrt(jnp.mean(x * x, axis=-1, keepdims=True) + RMS_EPS) * w


def _softplus(x):
    return jnp.maximum(x, 0.0) + jnp.log(1.0 + jnp.exp(-jnp.abs(x)))


def _f_pre(x, nw, sc, sh):
    return _rms(x, nw) * (1.0 + sc) + sh, x


def _f_pre2(x, o, gt, nw, sc, sh):
    x1 = x + gt * o
    return x1, _rms(x1, nw) * (1.0 + sc) + sh


def _f_fin(x1, f, tgt, gt, nfw):
    y = _rms(x1 + gt * f, nfw)
    return (0.5 * jnp.mean(jnp.square(y - tgt), axis=-1, keepdims=True),)


def _f_comb(o1, o2, o3, l1, l2, l3):
    m = lax.stop_gradient(jnp.maximum(jnp.maximum(l1, l2), l3))
    e1, e2, e3 = jnp.exp(l1 - m), jnp.exp(l2 - m), jnp.exp(l3 - m)
    return ((e1 * o1 + e2 * o2 + e3 * o3) / (e1 + e2 + e3),)


def _f_rwpre(zs, w0, a0, k_k, k_a, wl, e, et):
    r, k, v, zl = zs[:, 0:D], zs[:, D:2 * D], zs[:, 2 * D:3 * D], zs[:, 3 * D:N_RWP]
    lane = lax.broadcasted_iota(jnp.int32, zl.shape, 1)
    t = jnp.where(lane < 64, jnp.tanh(zl), jnp.where(lane < 128, zl, jnp.where(lane < 288, jax.nn.sigmoid(zl), 0.0)))
    lo = _nn(t, wl)
    w_log = -_softplus(-(w0 + lo[:, 0:D])) - 0.5
    lw = -jnp.exp(w_log)
    a = jax.nn.sigmoid(a0 + lo[:, D:2 * D])
    g = lo[:, 2 * D:3 * D]
    k_mod = k * (1.0 + (a - 1.0) * k_a)
    kk = k * k_k
    kk = kk / jnp.maximum(jnp.sqrt(_hsum(kk * kk, e, et)), 1e-12)
    return r, lw, k_mod, v, -kk, kk * a, g


def _f_rwpost(y, r, v, k_mod, g, lnx_w, lnx_b, r_k, e, et):
    mean = _hsum(y, e, et) * (1.0 / 64)
    yc = y - mean
    var = _hsum(yc * yc, e, et) * (1.0 / 64)
    yn = yc * lax.rsqrt(var + GN_EPS) * lnx_w + lnx_b
    bonus = _hsum(r * k_mod * r_k, e, et) * v
    return ((yn + bonus) * g,)


def _f_mix(gia, gir, ya, yr, bga, bgr):
    return (jax.nn.sigmoid(gia + bga) * ya + jax.nn.sigmoid(gir + bgr) * yr,)


def _f_adamw(w, g, m, v):
    m = ADAM_B1 * m + (1.0 - ADAM_B1) * g
    v = ADAM_B2 * v + (1.0 - ADAM_B2) * jnp.square(g)
    m_hat = m / (1.0 - ADAM_B1 ** ADAM_STEP)
    v_hat = v / (1.0 - ADAM_B2 ** ADAM_STEP)
    return -ADAM_LR * (m_hat / (jnp.sqrt(v_hat) + ADAM_EPS) + ADAM_WD * w), m, v


def _down(x, k):
    row = lax.broadcasted_iota(jnp.int32, x.shape, 0)
    return jnp.where(row < k, 0.0, pltpu.roll(x, k, 0))


def _up(x, k):
    n = x.shape[0]
    row = lax.broadcasted_iota(jnp.int32, x.shape, 0)
    return jnp.where(row >= n - k, 0.0, pltpu.roll(x, n - k, 0))


def _col_spec(s, w, off=0):
    return pl.BlockSpec((s, w), lambda j: (0, j + off))


def _shift_fwd(z, mu):
    s, n = z.shape

    def body(z_ref, mu_ref, o_ref):
        zz = z_ref[...]
        o_ref[...] = zz + (_down(zz, 1) - zz) * mu_ref[...]

    return _pcall(
        body, name="shift_fwd", grid=(n // 128,), in_specs=[_col_spec(s, 128), _col_spec(1, 128)],
        out_specs=_col_spec(s, 128), out_shape=jax.ShapeDtypeStruct((s, n), F32),
        compiler_params=_cparams(("parallel",)),
    )(z, mu)


def _shift_bwd(z, mu, dzs):
    s, n = z.shape

    def body(z_ref, mu_ref, d_ref, dz_ref, dmu_ref):
        zz, d, m = z_ref[...], d_ref[...], mu_ref[...]
        dm = d * m
        dz_ref[...] = (d - dm + _up(dm, 1)).astype(dz_ref.dtype)
        dmu_ref[...] = jnp.sum(d * (_down(zz, 1) - zz), axis=0, keepdims=True)

    return _pcall(
        body, name="shift_bwd", grid=(n // 128,), in_specs=[_col_spec(s, 128), _col_spec(1, 128), _col_spec(s, 128)],
        out_specs=[_col_spec(s, 128), _col_spec(1, 128)],
        out_shape=[jax.ShapeDtypeStruct((s, n), BF16), jax.ShapeDtypeStruct((1, n), F32)],
        compiler_params=_cparams(("parallel",)),
    )(z, mu, dzs)


def _conv3(x, w_ref, b_ref):
    return b_ref[...] + w_ref[0:1, :] * _down(x, 2) + w_ref[1:2, :] * _down(x, 1) + w_ref[2:3, :] * x


def _conv_fwd(u, cw, cb):
    s = u.shape[0]
    nb = D_FF // 128

    def body(ug_ref, uv_ref, wg_ref, wv_ref, bg_ref, bv_ref, o_ref):
        gate = _conv3(ug_ref[...], wg_ref, bg_ref)
        val = _conv3(uv_ref[...], wv_ref, bv_ref)
        o_ref[...] = (gate * jax.nn.sigmoid(gate) * val).astype(o_ref.dtype)

    return _pcall(
        body, name="conv_fwd", grid=(nb,),
        in_specs=[_col_spec(s, 128), _col_spec(s, 128, nb), _col_spec(3, 128), _col_spec(3, 128, nb),
                  _col_spec(1, 128), _col_spec(1, 128, nb)],
        out_specs=_col_spec(s, 128), out_shape=jax.ShapeDtypeStruct((s, D_FF), BF16),
        compiler_params=_cparams(("parallel",)),
    )(u, u, cw, cw, cb, cb)


def _conv_bwd(u, cw, cb, dact):
    s = u.shape[0]
    nb = D_FF // 128

    def half(x, d, w_ref, du_ref, dw_ref, db_ref):
        x1, x2 = _down(x, 1), _down(x, 2)
        du_ref[...] = (w_ref[2:3, :] * d + w_ref[1:2, :] * _up(d, 1) + w_ref[0:1, :] * _up(d, 2)).astype(du_ref.dtype)
        dw_ref[0:1, :] = jnp.sum(d * x2, axis=0, keepdims=True)
        dw_ref[1:2, :] = jnp.sum(d * x1, axis=0, keepdims=True)
        dw_ref[2:3, :] = jnp.sum(d * x, axis=0, keepdims=True)
        db_ref[...] = jnp.sum(d, axis=0, keepdims=True)

    def body(ug_ref, uv_ref, wg_ref, wv_ref, bg_ref, bv_ref, da_ref,
             dug_ref, duv_ref, dwg_ref, dwv_ref, dbg_ref, dbv_ref):
        ug, uv, da = ug_ref[...], uv_ref[...], da_ref[...]
        gate = _conv3(ug, wg_ref, bg_ref)
        val = _conv3(uv, wv_ref, bv_ref)
        sg = jax.nn.sigmoid(gate)
        dgate = da * val * sg * (1.0 + gate * (1.0 - sg))
        dval = da * gate * sg
        half(ug, dgate, wg_ref, dug_ref, dwg_ref, dbg_ref)
        half(uv, dval, wv_ref, duv_ref, dwv_ref, dbv_ref)

    dug, duv, dwg, dwv, dbg, dbv = _pcall(
        body, name="conv_bwd", grid=(nb,),
        in_specs=[_col_spec(s, 128), _col_spec(s, 128, nb), _col_spec(3, 128), _col_spec(3, 128, nb),
                  _col_spec(1, 128), _col_spec(1, 128, nb), _col_spec(s, 128)],
        out_specs=[_col_spec(s, 128), _col_spec(s, 128), _col_spec(3, 128), _col_spec(3, 128),
                   _col_spec(1, 128), _col_spec(1, 128)],
        out_shape=[jax.ShapeDtypeStruct((s, D_FF), BF16), jax.ShapeDtypeStruct((s, D_FF), BF16),
                   jax.ShapeDtypeStruct((3, D_FF), F32), jax.ShapeDtypeStruct((3, D_FF), F32),
                   jax.ShapeDtypeStruct((1, D_FF), F32), jax.ShapeDtypeStruct((1, D_FF), F32)],
        compiler_params=_cparams(("parallel",)),
    )(u, u, cw, cw, cb, cb, dact)
    return (jnp.concatenate([dug, duv], axis=1), jnp.concatenate([dwg, dwv], axis=1),
            jnp.concatenate([dbg, dbv], axis=1))


def _att_pair(q, kp, kc, vp, vc, first):
    lane = lax.broadcasted_iota(jnp.int32, (ATT_BLOCK, 128), 1)
    ma = lane < 64
    qs = jnp.concatenate([jnp.where(ma, q, 0.0), jnp.where(ma, 0.0, q)], axis=0)
    qi = lax.broadcasted_iota(jnp.int32, (2 * ATT_BLOCK, ATT_BLOCK), 0) & (ATT_BLOCK - 1)
    kj = lax.broadcasted_iota(jnp.int32, (2 * ATT_BLOCK, ATT_BLOCK), 1)
    okp = kj >= qi + jnp.where(first, 2 * ATT_BLOCK, 0)
    okc = kj <= qi
    sp = jnp.where(okp, _nt(qs, kp) * 0.125, NEG)
    sc = jnp.where(okc, _nt(qs, kc) * 0.125, NEG)
    m = lax.stop_gradient(jnp.maximum(jnp.max(sp, axis=-1, keepdims=True), jnp.max(sc, axis=-1, keepdims=True)))
    pp, pc = jnp.exp(sp - m), jnp.exp(sc - m)
    den = jnp.sum(pp, axis=-1, keepdims=True) + jnp.sum(pc, axis=-1, keepdims=True)
    o_s = (_nn(pp, vp) + _nn(pc, vc)) / den
    l_s = jnp.broadcast_to(m + jnp.log(den), (2 * ATT_BLOCK, 128))
    return (jnp.where(ma, o_s[:ATT_BLOCK], o_s[ATT_BLOCK:]), jnp.where(ma, l_s[:ATT_BLOCK], l_s[ATT_BLOCK:]))


def _att_specs(g, nslots):
    def cur(slot):
        return pl.BlockSpec((ATT_BLOCK, 128), lambda r, n, p: (n, (r * nslots + g * 3 + slot) * 4 + p))

    def prev(slot):
        return pl.BlockSpec((ATT_BLOCK, 128), lambda r, n, p: (jnp.maximum(n - 1, 0), (r * nslots + g * 3 + slot) * 4 + p))

    return [cur(0), prev(1), cur(1), prev(2), cur(2)]


def _att_out_spec():
    return pl.BlockSpec((ATT_BLOCK, 128), lambda r, n, p: (n, r * 4 + p))


def _att_fwd(att_in, g, dil):
    s = att_in.shape[0]
    rows = s // dil
    a2 = att_in.reshape(rows, dil * N_ATT)

    def body(q_ref, kp_ref, kc_ref, vp_ref, vc_ref, o_ref, l_ref):
        vals = [r[...].astype(F32) for r in (q_ref, kp_ref, kc_ref, vp_ref, vc_ref)]
        o, l = _att_pair(*vals, pl.program_id(1) == 0)
        o_ref[...] = o
        l_ref[...] = l

    o, l = _pcall(
        body, name=f"att_fwd{g}", grid=(dil, rows // ATT_BLOCK, 4), in_specs=_att_specs(g, 9),
        out_specs=[_att_out_spec(), _att_out_spec()],
        out_shape=[jax.ShapeDtypeStruct((rows, dil * ATT_WIDTH), F32)] * 2,
        compiler_params=_cparams(("parallel", "parallel", "parallel")),
    )(a2, a2, a2, a2, a2)
    return o.reshape(s, ATT_WIDTH), l.reshape(s, ATT_WIDTH)


def _att_bwd(att_in, g, dil, do, dl):
    s = att_in.shape[0]
    rows = s // dil
    nb = rows // ATT_BLOCK
    a2 = att_in.reshape(rows, dil * N_ATT)
    do2, dl2 = do.reshape(rows, dil * ATT_WIDTH), dl.reshape(rows, dil * ATT_WIDTH)

    def body(q_ref, kp_ref, kc_ref, vp_ref, vc_ref, do_ref, dl_ref, dq_ref, dkp_ref, dkc_ref, dvp_ref, dvc_ref):
        vals = [r[...].astype(F32) for r in (q_ref, kp_ref, kc_ref, vp_ref, vc_ref)]
        first = pl.program_id(1) == 0
        _, vjp = jax.vjp(lambda *a: _att_pair(*a, first), *vals)
        grads = vjp((do_ref[...], dl_ref[...]))
        for ref, gr in zip((dq_ref, dkp_ref, dkc_ref, dvp_ref, dvc_ref), grads):
            ref[...] = gr

    parts = _pcall(
        body, name=f"att_bwd{g}", grid=(dil, nb, 4), in_specs=_att_specs(g, 9) + [_att_out_spec(), _att_out_spec()],
        out_specs=[_att_out_spec()] * 5, out_shape=[jax.ShapeDtypeStruct((rows, dil * ATT_WIDTH), F32)] * 5,
        compiler_params=_cparams(("parallel", "parallel", "parallel")),
    )(a2, a2, a2, a2, a2, do2, dl2)
    dq, dkp, dkc, dvp, dvc = parts

    def cbody(dq_ref, dkc_ref, dkn_ref, dvc_ref, dvn_ref, oq_ref, ok_ref, ov_ref):
        has_next = pl.program_id(1) + 1 < nb
        oq_ref[...] = dq_ref[...].astype(BF16)
        ok_ref[...] = (dkc_ref[...] + jnp.where(has_next, dkn_ref[...], 0.0)).astype(BF16)
        ov_ref[...] = (dvc_ref[...] + jnp.where(has_next, dvn_ref[...], 0.0)).astype(BF16)

    cur = pl.BlockSpec((ATT_BLOCK, ATT_WIDTH), lambda r, n: (n, r))
    nxt = pl.BlockSpec((ATT_BLOCK, ATT_WIDTH), lambda r, n: (jnp.minimum(n + 1, nb - 1), r))
    outs = _pcall(
        cbody, name=f"att_bwd_sum{g}", grid=(dil, nb), in_specs=[cur, cur, nxt, cur, nxt], out_specs=[cur] * 3,
        out_shape=[jax.ShapeDtypeStruct((rows, dil * ATT_WIDTH), BF16)] * 3,
        compiler_params=_cparams(("parallel", "parallel")),
    )(dq, dkc, dkp, dvc, dvp)
    return [t.reshape(s, ATT_WIDTH) for t in outs]


def _scan_chunk(r, lw, k, v, a, b, s0):
    c = SCAN_CHUNK
    ri = lax.broadcasted_iota(jnp.int32, (c, c), 0)
    ci = lax.broadcasted_iota(jnp.int32, (c, c), 1)
    cum = jnp.dot((ci <= ri).astype(F32), lw, precision=HI, preferred_element_type=F32)
    tot = jnp.sum(lw, axis=0, keepdims=True)
    ma = lax.broadcasted_iota(jnp.int32, (c, 128), 1) < 64

    def stack(x):
        return jnp.concatenate([jnp.where(ma, x, 0.0), jnp.where(ma, 0.0, x)], axis=0)

    einv, eend = jnp.exp(-cum), jnp.exp(tot - cum)
    ra, aa = stack(r * jnp.exp(cum)), stack(a * jnp.exp(cum - lw))
    bi, ki, be, ke, vs = stack(b * einv), stack(k * einv), stack(b * eend), stack(k * eend), stack(v)
    r2 = lax.broadcasted_iota(jnp.int32, (2 * c, 2 * c), 0)
    c2 = lax.broadcasted_iota(jnp.int32, (2 * c, 2 * c), 1)
    same = (r2 >= c) == (c2 >= c)
    strict = jnp.logical_and(same, c2 < r2)
    incl = jnp.logical_and(same, c2 <= r2)
    s0 = jnp.where(same, s0, 0.0)
    a_ab = jnp.where(strict, _nt(aa, bi), 0.0)
    a_ak = jnp.where(strict, _nt(aa, ki), 0.0)
    a_rb = jnp.where(incl, _nt(ra, bi), 0.0)
    a_rk = jnp.where(incl, _nt(ra, ki), 0.0)
    t = jnp.where(r2 == c2, 1.0, 0.0) + a_ab
    pw = a_ab
    for _ in range(5):
        pw = _nn(pw, pw)
        t = t + _nn(t, pw)
    u = _nn(t, _nt(aa, s0) + _nn(a_ak, vs))
    ys = _nt(ra, s0) + _nn(a_rb, u) + _nn(a_rk, vs)
    s1 = s0 * jnp.exp(tot) + _tn(u, be) + _tn(vs, ke)
    return ys[:c] + ys[c:], s1


def _scan_specs(rev, n):
    def cm(cb):
        if rev:
            return pl.BlockSpec((SCAN_CHUNK, 128), lambda p, i: (n - 1 - i, cb * 8 + p))
        return pl.BlockSpec((SCAN_CHUNK, 128), lambda p, i: (i, cb * 8 + p))

    if rev:
        st = pl.BlockSpec((1, 1, 128, 128), lambda p, i: (n - 1 - i, p, 0, 0))
    else:
        st = pl.BlockSpec((1, 1, 128, 128), lambda p, i: (i, p, 0, 0))
    return cm, st


def _scan_fwd(zs, lw, km, aa, bb):
    s = zs.shape[0]
    n = s // SCAN_CHUNK
    cm, st = _scan_specs(False, n)

    def body(r_ref, lw_ref, k_ref, v_ref, a_ref, b_ref, y_ref, s0_ref, state):
        @pl.when(pl.program_id(1) == 0)
        def _():
            state[...] = jnp.zeros_like(state)

        s0 = state[...]
        s0_ref[0, 0] = s0
        y, s1 = _scan_chunk(r_ref[...], lw_ref[...], k_ref[...], v_ref[...], a_ref[...], b_ref[...], s0)
        y_ref[...] = y
        state[...] = s1

    return _pcall(
        body, name="scan_fwd", grid=(8, n), in_specs=[cm(0), cm(0), cm(0), cm(2), cm(0), cm(0)],
        out_specs=[cm(0), st],
        out_shape=[jax.ShapeDtypeStruct((s, D), F32), jax.ShapeDtypeStruct((n, 8, 128, 128), F32)],
        scratch_shapes=[pltpu.VMEM((128, 128), F32)],
        compiler_params=_cparams(("parallel", "arbitrary")),
    )(zs, lw, km, zs, aa, bb)


def _scan_bwd(zs, lw, km, aa, bb, s0s, dy):
    s = zs.shape[0]
    n = s // SCAN_CHUNK
    cm, st = _scan_specs(True, n)

    def body(r_ref, lw_ref, k_ref, v_ref, a_ref, b_ref, s0_ref, dy_ref,
             dr_ref, dlw_ref, dk_ref, dv_ref, da_ref, db_ref, dstate):
        @pl.when(pl.program_id(1) == 0)
        def _():
            dstate[...] = jnp.zeros_like(dstate)

        prim = (r_ref[...], lw_ref[...], k_ref[...], v_ref[...], a_ref[...], b_ref[...], s0_ref[0, 0])
        _, vjp = jax.vjp(_scan_chunk, *prim)
        grads = vjp((dy_ref[...], dstate[...]))
        for ref, gr in zip((dr_ref, dlw_ref, dk_ref, dv_ref, da_ref, db_ref), grads[:6]):
            ref[...] = gr
        dstate[...] = grads[6]

    return _pcall(
        body, name="scan_bwd", grid=(8, n), in_specs=[cm(0), cm(0), cm(0), cm(2), cm(0), cm(0), st, cm(0)],
        out_specs=[cm(0)] * 6, out_shape=[jax.ShapeDtypeStruct((s, D), F32)] * 6,
        scratch_shapes=[pltpu.VMEM((128, 128), F32)],
        compiler_params=_cparams(("parallel", "arbitrary")),
    )(zs, lw, km, zs, aa, bb, s0s, dy)


_HBM = pl.BlockSpec(memory_space=pltpu.HBM)


def _me():
    return lax.axis_index("x"), lax.axis_index("y"), lax.axis_index("c")


def _allgather8(src, name):
    def body(src_ref, out_ref, ssem, rsem, lsem):
        x, y, c = _me()
        me = 4 * x + 2 * y + c
        local = pltpu.make_async_copy(src_ref, out_ref.at[me], lsem)
        local.start()
        peers = []
        for k in range(1, 8):
            peers.append(((1 - x) if k & 4 else x, (1 - y) if k & 2 else y, (1 - c) if k & 1 else c))
        sends = []
        for k, peer in enumerate(peers):
            cp = pltpu.make_async_remote_copy(src_ref, out_ref.at[me], ssem.at[k], rsem.at[k], device_id=peer,
                                              device_id_type=MESH)
            cp.start()
            sends.append(cp)
        for k, (px, py, pc) in enumerate(peers):
            pltpu.make_async_remote_copy(src_ref, out_ref.at[4 * px + 2 * py + pc], ssem.at[k], rsem.at[k],
                                         device_id=(px, py, pc), device_id_type=MESH).wait_recv()
        for cp in sends:
            cp.wait_send()
        local.wait()

    return _pcall(
        body, name=name, in_specs=[_HBM], out_specs=_HBM, out_shape=jax.ShapeDtypeStruct((8,) + src.shape, src.dtype),
        scratch_shapes=[pltpu.SemaphoreType.DMA((7,)), pltpu.SemaphoreType.DMA((7,)), pltpu.SemaphoreType.DMA],
    )(src)


def _chip_exchange(src, gather, name):
    shape = src.shape if gather else src.shape[1:]

    def body(src_ref, out_ref, ssem, rsem, lsem):
        x, y, c = _me()
        me = 2 * x + y
        local = pltpu.make_async_copy(src_ref if gather else src_ref.at[me], out_ref.at[me], lsem)
        local.start()
        chips = [(1 - x, y), (x, 1 - y), (1 - x, 1 - y)]
        sends = []
        for k, (px, py) in enumerate(chips):
            cp = pltpu.make_async_remote_copy(src_ref if gather else src_ref.at[2 * px + py], out_ref.at[me],
                                              ssem.at[k], rsem.at[k], device_id=(px, py, c), device_id_type=MESH)
            cp.start()
            sends.append(cp)
        for k, (px, py) in enumerate(chips):
            pltpu.make_async_remote_copy(src_ref if gather else src_ref.at[me], out_ref.at[2 * px + py],
                                         ssem.at[k], rsem.at[k], device_id=(px, py, c), device_id_type=MESH).wait_recv()
        for cp in sends:
            cp.wait_send()
        local.wait()

    return _pcall(
        body, name=name, in_specs=[_HBM], out_specs=_HBM, out_shape=jax.ShapeDtypeStruct((4,) + shape, src.dtype),
        scratch_shapes=[pltpu.SemaphoreType.DMA((3,)), pltpu.SemaphoreType.DMA((3,)), pltpu.SemaphoreType.DMA],
    )(src)


def _sib_swap(src, name):
    def body(src_ref, out_ref, ssem, rsem):
        x, y, c = _me()
        cp = pltpu.make_async_remote_copy(src_ref, out_ref, ssem, rsem, device_id=(x, y, 1 - c), device_id_type=MESH)
        cp.start()
        cp.wait_recv()
        cp.wait_send()

    return _pcall(
        body, name=name, in_specs=[_HBM], out_specs=_HBM, out_shape=jax.ShapeDtypeStruct(src.shape, src.dtype),
        scratch_shapes=[pltpu.SemaphoreType.DMA, pltpu.SemaphoreType.DMA],
    )(src)


def _ada_fwd(c_all, w, b):
    def body(c_ref, w_ref, b_ref, o_ref):
        o_ref[...] = jnp.dot(c_ref[...], w_ref[...], precision=HI, preferred_element_type=F32) + b_ref[...]

    return _pcall(body, name="ada_fwd", out_shape=jax.ShapeDtypeStruct((c_all.shape[0], w.shape[1]), F32),
                  compiler_params=pltpu.CompilerParams(vmem_limit_bytes=VMEM_LIMIT))(c_all, w, b)


def _ada_bwd(c_all_t, d):
    def body(c_ref, d_ref, o_ref):
        o_ref[...] = jnp.dot(c_ref[...], d_ref[...], precision=HI, preferred_element_type=F32)

    return _pcall(body, name="ada_bwd", out_shape=jax.ShapeDtypeStruct((c_all_t.shape[0], d.shape[1]), F32),
                  compiler_params=pltpu.CompilerParams(vmem_limit_bytes=VMEM_LIMIT))(c_all_t, d)


def _sum_lead(x, name):
    p, r, n = x.shape
    br = _div(r, 512, 8)

    def body(x_ref, o_ref):
        acc = x_ref[0]
        for j in range(1, p):
            acc = acc + x_ref[j]
        o_ref[...] = acc

    return _pcall(
        body, name=name, grid=(r // br,), in_specs=[pl.BlockSpec((p, br, n), lambda i: (0, i, 0))],
        out_specs=pl.BlockSpec((br, n), lambda i: (i, 0)), out_shape=jax.ShapeDtypeStruct((r, n), F32),
        compiler_params=_cparams(("parallel",)),
    )(x)


def _adamw(w, g, m, v, name):
    shape = w.shape
    cols = shape[-1]
    w2, g2, m2, v2 = [t.reshape(-1, cols) for t in (w, g, m, v)]
    rows = w2.shape[0]
    br = _div(rows, max(8, (1 << 19) // cols // 8 * 8), 8)
    outs = _rows_fwd(_f_adamw, [(t, cols, 0) for t in (w2, g2, m2, v2)], [], [(cols, F32)] * 3, name=name, br=br)
    return [o.reshape(shape) for o in outs]


_BIG = (("w_in", 1), ("w_up", 1), ("w_down", 0), ("w_o", 0), ("w_rwkv_out", 0), ("w_att_out", 1), ("w2", 1), ("a2", 1),
        ("g2", 1))


def _pack_rows(shards, dtype):
    flat = [t.reshape(-1, D).astype(dtype) for t in shards]
    rows = sum(t.shape[0] for t in flat)
    pad = (-rows) % 32
    if pad:
        flat.append(jnp.zeros((pad, D), dtype))
    return jnp.concatenate(flat, axis=0)


def _unpack_rows(buf, shapes):
    out, pos = [], 0
    for shp in shapes:
        nrow = shp[0] * shp[1] // D
        out.append(buf[pos:pos + nrow].reshape(shp))
        pos += nrow
    return out


def _shard_of(full, axis, j):
    n = full.shape[axis] // 4
    return lax.slice_in_dim(full, j * n, (j + 1) * n, axis=axis)


def _local_step(x, tgt, ada, wts):
    s = x.shape[0]
    sh1, sc1, gt1, sh2, sc2, gt2 = ada
    br = 256
    grp = lax.broadcasted_iota(jnp.int32, (D, 128), 0) // 64 == lax.broadcasted_iota(jnp.int32, (D, 128), 1)
    e = grp.astype(F32)
    et = e.T
    w_in = wts["w_in"]
    w_att = w_in[:, :N_ATT]
    w_rw = jnp.pad(w_in[:, N_ATT:N_ATT + N_RW], ((0, 0), (0, N_RWP - N_RW)))
    w_gate = w_in[:, N_ATT + N_RW:]
    mu = jnp.pad(wts["mu_shift"], ((0, 0), (0, N_RWP - N_RW)))
    wl = jnp.zeros((N_LORA, 3 * D), F32)
    wl = wl.at[0:64, 0:D].set(wts["w2"].astype(F32)).at[64:128, D:2 * D].set(wts["a2"].astype(F32))
    wl = wl.at[128:288, 2 * D:3 * D].set(wts["g2"].astype(F32))
    bga, bgr = wts["b_gate"][:, :D], wts["b_gate"][:, D:]
    rk = wts["r_k"]

    pre1_c = [wts["norm1_w"], sc1, sh1]
    (h1,) = _rows_fwd(_f_pre, [(x, D, 0)], pre1_c, [(D, BF16), None], name="pre1_fwd", br=br)
    att_in = _mm(h1, w_att, out_dtype=BF16, name="mm_att_in")
    z = _mm(h1, w_rw, name="mm_rw_in")
    gate_in = _mm(h1, w_gate, name="mm_gate_in")
    att_o, att_l = [], []
    for g, (_, dil) in enumerate(ATT_PATTERNS):
        o, l = _att_fwd(att_in, g, dil)
        att_o.append(o)
        att_l.append(l)
    comb_rows = [(t, ATT_WIDTH, 0) for t in att_o + att_l]
    (att,) = _rows_fwd(_f_comb, comb_rows, [], [(ATT_WIDTH, BF16)], name="comb_fwd", br=br)
    y_att = _mm(att, wts["w_att_out"], name="mm_att_out")
    zs = _shift_fwd(z, mu)
    rwpre_c = [wts["w0"], wts["a0"], wts["k_k"], wts["k_a"], wl, e, et]
    lw, km, aa, bb, gg = _rows_fwd(_f_rwpre, [(zs, N_RWP, 0)], rwpre_c,
                                   [None, (D, F32), (D, F32), None, (D, F32), (D, F32), (D, F32)],
                                   name="rwpre_fwd", br=br)
    y_raw, s0s = _scan_fwd(zs, lw, km, aa, bb)
    post_rows = [(y_raw, D, 0), (zs, D, 0), (zs, D, 2), (km, D, 0), (gg, D, 0)]
    post_c = [wts["lnx_w"], wts["lnx_b"], rk, e, et]
    (rw_out,) = _rows_fwd(_f_rwpost, post_rows, post_c, [(D, BF16)], name="rwpost_fwd", br=br)
    y_rw = _mm(rw_out, wts["w_rwkv_out"], name="mm_rw_out")
    mix_rows = [(gate_in, D, 0), (gate_in, D, 1), (y_att, D, 0), (y_rw, D, 0)]
    (mix,) = _rows_fwd(_f_mix, mix_rows, [bga, bgr], [(D, BF16)], name="mix_fwd", br=br)
    o = _mm(mix, wts["w_o"], name="mm_o")
    pre2_c = [gt1, wts["norm2_w"], sc2, sh2]
    x1, h2 = _rows_fwd(_f_pre2, [(x, D, 0), (o, D, 0)], pre2_c, [(D, F32), (D, BF16)], name="pre2_fwd", br=br)
    u = _mm(h2, wts["w_up"], name="mm_up")
    act = _conv_fwd(u, wts["conv_w"], wts["conv_b"])
    f = _mm(act, wts["w_down"], name="mm_down")
    fin_rows = [(x1, D, 0), (f, D, 0), (tgt, D, 0)]
    fin_c = [gt2, wts["norm_f_w"]]

    def fin_fwd(*a):
        (l,) = _f_fin(*a)
        return (jnp.broadcast_to(jnp.sum(l, axis=0, keepdims=True), (8, 128)),)

    (loss_acc,) = _rows_fwd(fin_fwd, fin_rows, fin_c, [], name="fin_fwd", br=br, acc_shape=(8, 128))
    loss = loss_acc[0, 0]

    gw = {}
    dx1a, df, d_gt2, gw["norm_f_w"] = _rows_bwd(
        _f_fin, fin_rows, fin_c, [[]], wrt_rows=[0, 1], wrt_consts=[0, 1], drow_dtypes=[F32, BF16],
        name="fin_bwd", br=br, unit_cot=True)
    dact = _mm(df, wts["w_down"], tb=True, name="mm_dact")
    gw["w_down"] = _mm(act, df, ta=True, name="mm_dw_down")
    du, gw["conv_w"], gw["conv_b"] = _conv_bwd(u, wts["conv_w"], wts["conv_b"], dact)
    dh2 = _mm(du, wts["w_up"], tb=True, name="mm_dh2")
    gw["w_up"] = _mm(h2, du, ta=True, name="mm_dw_up")
    dxa, do, d_gt1, gw["norm2_w"], d_sc2, d_sh2 = _rows_bwd(
        _f_pre2, [(x, D, 0), (o, D, 0)], pre2_c, [[(dx1a, D, 0)], [(dh2, D, 0)]], wrt_rows=[0, 1],
        wrt_consts=[0, 1, 2, 3], drow_dtypes=[F32, BF16], name="pre2_bwd", br=br)
    dmix = _mm(do, wts["w_o"], tb=True, name="mm_dmix")
    gw["w_o"] = _mm(mix, do, ta=True, name="mm_dw_o")
    dga, dgr, dya, dyr, d_bga, d_bgr = _rows_bwd(
        _f_mix, mix_rows, [bga, bgr], [[(dmix, D, 0)]], wrt_rows=[0, 1, 2, 3], wrt_consts=[0, 1],
        drow_dtypes=[BF16] * 4, name="mix_bwd", br=br)
    gw["b_gate"] = jnp.concatenate([d_bga, d_bgr], axis=1)
    datt = _mm(dya, wts["w_att_out"], tb=True, name="mm_datt")
    gw["w_att_out"] = _mm(att, dya, ta=True, name="mm_dw_att_out")
    drw = _mm(dyr, wts["w_rwkv_out"], tb=True, name="mm_drw")
    gw["w_rwkv_out"] = _mm(rw_out, dyr, ta=True, name="mm_dw_rw_out")
    dcomb = _rows_bwd(_f_comb, comb_rows, [], [[(datt, ATT_WIDTH, 0)]], wrt_rows=list(range(6)), wrt_consts=[],
                      drow_dtypes=[F32] * 6, name="comb_bwd", br=br)
    datt_in = []
    for g, (_, dil) in enumerate(ATT_PATTERNS):
        datt_in += _att_bwd(att_in, g, dil, dcomb[g], dcomb[3 + g])
    datt_in = jnp.concatenate(datt_in, axis=1)
    dy_raw, dr_p, dv_p, dkm_p, dgg, gw["lnx_w"], gw["lnx_b"], gw["r_k"] = _rows_bwd(
        _f_rwpost, post_rows, post_c, [[(drw, D, 0)]], wrt_rows=[0, 1, 2, 3, 4], wrt_consts=[0, 1, 2],
        drow_dtypes=[F32] * 5, name="rwpost_bwd", br=br)
    dr_s, dlw, dkm_s, dv_s, daa, dbb = _scan_bwd(zs, lw, km, aa, bb, s0s, dy_raw)
    pre_cots = [[(dr_p, D, 0), (dr_s, D, 0)], [(dlw, D, 0)], [(dkm_p, D, 0), (dkm_s, D, 0)],
                [(dv_p, D, 0), (dv_s, D, 0)], [(daa, D, 0)], [(dbb, D, 0)], [(dgg, D, 0)]]
    dzs, gw["w0"], gw["a0"], gw["k_k"], gw["k_a"], dwl = _rows_bwd(
        _f_rwpre, [(zs, N_RWP, 0)], rwpre_c, pre_cots, wrt_rows=[0], wrt_consts=[0, 1, 2, 3, 4], drow_dtypes=[F32],
        name="rwpre_bwd", br=128)
    gw["w2"], gw["a2"], gw["g2"] = dwl[0:64, 0:D], dwl[64:128, D:2 * D], dwl[128:288, 2 * D:3 * D]
    dz, dmu = _shift_bwd(z, mu, dzs)
    gw["mu_shift"] = dmu[:, :N_RW]
    dgate = jnp.concatenate([dga, dgr], axis=1)
    dh1 = _mm(datt_in, w_att, tb=True, name="mm_dh1_att")
    dh1 = _mm(dz, w_rw, tb=True, add=dh1, name="mm_dh1_rw")
    dh1 = _mm(dgate, w_gate, tb=True, add=dh1, name="mm_dh1_gate")
    gw["w_in"] = jnp.concatenate([_mm(h1, datt_in, ta=True, name="mm_dw_att"),
                                  _mm(h1, dz, ta=True, name="mm_dw_rw")[:, :N_RW],
                                  _mm(h1, dgate, ta=True, name="mm_dw_gate")], axis=1)
    grad_x, gw["norm1_w"], d_sc1, d_sh1 = _rows_bwd(
        _f_pre, [(x, D, 0)], pre1_c, [[(dh1, D, 0)], [(dxa, D, 0)]], wrt_rows=[0], wrt_consts=[0, 1, 2],
        drow_dtypes=[F32], name="pre1_bwd", br=br)
    return loss, grad_x, (d_sh1, d_sc1, d_gt1, d_sh2, d_sc2, d_gt2), gw


_SMALL = ("b_ada", "norm1_w", "b_gate", "mu_shift", "w0", "a0", "k_k", "k_a", "r_k", "lnx_w", "lnx_b", "norm2_w",
          "conv_b", "norm_f_w")
_NAMES = ("w_ada", "b_ada", "norm1_w", "w_in", "b_gate", "mu_shift", "w0", "w2", "a0", "a2", "g2", "k_k", "k_a", "r_k",
          "lnx_w", "lnx_b", "w_att_out", "w_rwkv_out", "w_o", "norm2_w", "w_up", "conv_w", "conv_b", "w_down",
          "norm_f_w")


def kernel(x, c, w_ada, b_ada, norm1_w, w_in, b_gate, mu_shift, w0, w2, a0, a2, g2, k_k, k_a, r_k, lnx_w, lnx_b, w_att_out, w_rwkv_out, w_o, norm2_w, w_up, conv_w, conv_b, w_down, norm_f_w, loss_target, m_w_ada, m_b_ada, m_norm1_w, m_w_in, m_b_gate, m_mu_shift, m_w0, m_w2, m_a0, m_a2, m_g2, m_k_k, m_k_a, m_r_k, m_lnx_w, m_lnx_b, m_w_att_out, m_w_rwkv_out, m_w_o, m_norm2_w, m_w_up, m_conv_w, m_conv_b, m_w_down, m_norm_f_w, v_w_ada, v_b_ada, v_norm1_w, v_w_in, v_b_gate, v_mu_shift, v_w0, v_w2, v_a0, v_a2, v_g2, v_k_k, v_k_a, v_r_k, v_lnx_w, v_lnx_b, v_w_att_out, v_w_rwkv_out, v_w_o, v_norm2_w, v_w_up, v_conv_w, v_conv_b, v_w_down, v_norm_f_w):
    args = dict(locals())
    p, pm, pv = {}, {}, {}
    for name in _NAMES:
        for dst, key in ((p, name), (pm, "m_" + name), (pv, "v_" + name)):
            t = args[key]
            dst[name] = t.reshape(1, -1) if name in ("r_k", "norm_f_w") else t.reshape(t.shape[-2], t.shape[-1])
    xi, yi, ci = _me()
    chip = 2 * xi + yi
    dev = 4 * xi + 2 * yi + ci
    x2, tgt = x[0], loss_target[0]

    n_cw = 3 * (2 * D_FF // 4)
    vec = jnp.concatenate([c.reshape(-1), p["conv_w"].reshape(-1), jnp.zeros((8 * D - D - n_cw,), F32)]).reshape(8, D)
    g0 = _allgather8(vec, "gather_c").reshape(8, 8 * D)
    c_all = g0[:, :D]
    conv_w_full = jnp.concatenate([g0[2 * j, D:D + n_cw].reshape(3, -1) for j in range(4)], axis=1)
    n_ada = 6 * D // 4
    b_ada_sh = lax.dynamic_slice(p["b_ada"], (0, chip * n_ada), (1, n_ada))
    ada_sh = _ada_fwd(c_all, p["w_ada"], b_ada_sh)
    ga = _allgather8(ada_sh, "gather_ada")
    ada_all = jnp.concatenate([ga[2 * j] for j in range(4)], axis=1)
    ada_row = lax.dynamic_slice(ada_all, (dev, 0), (1, 6 * D))
    ada = [ada_row[:, j * D:(j + 1) * D] for j in range(6)]

    big_shapes = [p[n].shape for n, _ in _BIG]
    wpack = _pack_rows([p[n] for n, _ in _BIG], BF16)
    rtot = wpack.shape[0]
    rh = rtot // 2
    mine = lax.dynamic_slice(wpack, (ci * rh, 0), (rh, D))
    got = _chip_exchange(mine, True, "gather_w")
    other = _sib_swap(got, "gather_w_sib")
    lo_half = jnp.where(ci == 0, got, other)
    hi_half = jnp.where(ci == 0, other, got)
    wall = jnp.concatenate([lo_half, hi_half], axis=1)
    per_chip = [_unpack_rows(wall[j], big_shapes) for j in range(4)]
    wts = {n: jnp.concatenate([per_chip[j][k] for j in range(4)], axis=ax) for k, (n, ax) in enumerate(_BIG)}
    for n in _SMALL:
        wts[n] = p[n]
    wts["conv_w"] = conv_w_full

    loss_part, grad_x, d_ada, gw = _local_step(x2, tgt, ada, wts)

    small = [jnp.concatenate(d_ada, axis=1)] + [gw[n] for n in _SMALL[1:]] + [gw["conv_w"], loss_part.reshape(1, 1)]
    sizes = [t.size for t in small]
    flat = jnp.concatenate([t.reshape(-1) for t in small])
    npad = (-flat.shape[0]) % (8 * D)
    srows = (flat.shape[0] + npad) // D
    flat = jnp.concatenate([flat, jnp.zeros((npad,), F32)]).reshape(srows, D)
    parts = _allgather8(flat, "gather_small")
    tot = _sum_lead(parts, "sum_small").reshape(-1)
    pieces, pos = [], 0
    for sz in sizes:
        pieces.append(tot[pos:pos + sz])
        pos += sz
    grads = {}
    for n, piece in zip(_SMALL, pieces[:len(_SMALL)]):
        grads[n] = piece.reshape(p[n].shape)
    conv_w_grad = pieces[len(_SMALL)].reshape(3, 2 * D_FF)
    grads["conv_w"] = lax.dynamic_slice(conv_w_grad, (0, chip * (n_cw // 3)), (3, n_cw // 3))
    loss = pieces[-1][0]
    d_ada_all = parts[:, :6].reshape(8, 6 * D)
    grads["w_ada"] = _ada_bwd(c_all.T, lax.dynamic_slice(d_ada_all, (0, chip * n_ada), (8, n_ada)))

    gpack = jnp.stack([_pack_rows([_shard_of(gw[n], ax, j) for n, ax in _BIG], F32) for j in range(4)])
    halves = gpack.reshape(4, 2, rh, D)
    keep = lax.dynamic_index_in_dim(halves, ci, axis=1, keepdims=False)
    give = lax.dynamic_index_in_dim(halves, 1 - ci, axis=1, keepdims=False)
    recv = _sib_swap(give, "reduce_sib")
    lo_dev = jnp.where(ci == 0, keep, recv)
    hi_dev = jnp.where(ci == 0, recv, keep)
    chip_sum = _sum_lead(jnp.stack([lo_dev, hi_dev]).reshape(2, 4 * rh, D), "reduce_add2").reshape(4, rh, D)
    slots = _chip_exchange(chip_sum, False, "reduce_chips")
    red = _sum_lead(slots, "reduce_add4")
    red_other = _sib_swap(red, "reduce_sib2")
    red_full = jnp.concatenate([jnp.where(ci == 0, red, red_other), jnp.where(ci == 0, red_other, red)], axis=0)
    for (n, _), gshard in zip(_BIG, _unpack_rows(red_full, big_shapes)):
        grads[n] = gshard

    outs_g, outs_d, outs_m, outs_v = [], [], [], []
    for name in _NAMES:
        g = grads[name]
        d, m, v = _adamw(p[name], g, pm[name], pv[name], "adamw_" + name)
        shape = args[name].shape
        outs_g.append(g.reshape(shape))
        outs_d.append(d.reshape(shape))
        outs_m.append(m.reshape(shape))
        outs_v.append(v.reshape(shape))
    return (loss, grad_x.reshape(x.shape), *outs_g, *outs_d, *outs_m, *outs_v)
```

```python
import functools

import jax
import jax.numpy as jnp
from jax import lax
from jax.experimental import pallas as pl
from jax.experimental.pallas import tpu as pltpu

F32 = jnp.float32
BF16 = jnp.bfloat16
HI = lax.Precision.HIGHEST
MESH = pl.DeviceIdType.MESH

D = 1024
ATT_PATTERNS = ((128, 1), (512, 4), (2048, 16))
ATT_BLOCK = 128
ATT_WIDTH = 512
N_ATT = 3 * 3 * ATT_WIDTH
N_RW = 3 * D + 64 + 64 + 160
N_RWP = 3456
N_LORA = N_RWP - 3 * D
N_GATE = 2 * D
D_FF = 2816
RMS_EPS = 1e-6
GN_EPS = 64e-5
SCAN_CHUNK = 64
SCAN_PAIRS = 4
NEG = -1e30
VMEM_LIMIT = 48 * 1024 * 1024

ADAM_LR, ADAM_B1, ADAM_B2, ADAM_EPS, ADAM_WD, ADAM_STEP = 0.001, 0.9, 0.999, 1e-08, 0.01, 10


def _pcall(body, **kw):
    return pl.pallas_call(body, **kw)


def _cparams(sem):
    return pltpu.CompilerParams(dimension_semantics=sem, vmem_limit_bytes=VMEM_LIMIT)


def _div(n, pref, mult):
    best = None
    d = mult
    while d <= min(n, pref):
        if n % d == 0:
            best = d
        d += mult
    return best if best else n


def _dg(a, b, ca, cb):
    return lax.dot_general(a.astype(BF16), b.astype(BF16), (((ca,), (cb,)), ((), ())), preferred_element_type=F32)


@jax.custom_vjp
def _nn(a, b):
    return _dg(a, b, 1, 0)


@jax.custom_vjp
def _nt(a, b):
    return _dg(a, b, 1, 1)


@jax.custom_vjp
def _tn(a, b):
    return _dg(a, b, 0, 0)


_nn.defvjp(lambda a, b: (_nn(a, b), (a, b)), lambda res, g: (_nt(g, res[1]), _tn(res[0], g)))
_nt.defvjp(lambda a, b: (_nt(a, b), (a, b)), lambda res, g: (_nn(g, res[1]), _tn(g, res[0])))
_tn.defvjp(lambda a, b: (_tn(a, b), (a, b)), lambda res, g: (_nt(res[1], g), _nn(res[0], g)))


def _bdg(a, b, ca, cb):
    return lax.dot_general(a.astype(BF16), b.astype(BF16), (((ca,), (cb,)), ((0,), (0,))), preferred_element_type=F32)


@jax.custom_vjp
def _bnn(a, b):
    return _bdg(a, b, 2, 1)


@jax.custom_vjp
def _bnt(a, b):
    return _bdg(a, b, 2, 2)


@jax.custom_vjp
def _btn(a, b):
    return _bdg(a, b, 1, 1)


_bnn.defvjp(lambda a, b: (_bnn(a, b), (a, b)), lambda res, g: (_bnt(g, res[1]), _btn(res[0], g)))
_bnt.defvjp(lambda a, b: (_bnt(a, b), (a, b)), lambda res, g: (_bnn(g, res[1]), _btn(g, res[0])))
_btn.defvjp(lambda a, b: (_btn(a, b), (a, b)), lambda res, g: (_bnt(res[1], g), _bnn(res[0], g)))


def _split2(x):
    hi = x.astype(BF16)
    lo = (x - hi.astype(F32)).astype(BF16)
    return hi, lo


def _hsum_impl(x, e, et):
    eb, etb = e.astype(BF16), et.astype(BF16)
    hi, lo = _split2(x)
    s = jnp.dot(hi, eb, preferred_element_type=F32) + jnp.dot(lo, eb, preferred_element_type=F32)
    shi, slo = _split2(s)
    return jnp.dot(shi, etb, preferred_element_type=F32) + jnp.dot(slo, etb, preferred_element_type=F32)


@jax.custom_vjp
def _hsum(x, e, et):
    return _hsum_impl(x, e, et)


_hsum.defvjp(lambda x, e, et: (_hsum_impl(x, e, et), (e, et)),
             lambda res, g: (_hsum_impl(g, res[0], res[1]), jnp.zeros_like(res[0]), jnp.zeros_like(res[1])))


def _mm(a, b, *, ta=False, tb=False, out_dtype=F32, add=None, name):
    if ta:
        kdim, m = a.shape
    else:
        m, kdim = a.shape
    n = b.shape[0] if tb else b.shape[1]
    tm, tn, tk = _div(m, 1536, 128), _div(n, 1536, 128), _div(kdim, 1408, 128)
    nk = kdim // tk
    ca, cb = (0 if ta else 1), (1 if tb else 0)

    def body(*refs):
        a_ref, b_ref = refs[0], refs[1]
        add_ref = None if add is None else refs[2]
        o_ref = refs[2 if add is None else 3]
        part = lax.dot_general(a_ref[...], b_ref[...], (((ca,), (cb,)), ((), ())), preferred_element_type=F32)

        def finish(r):
            if add_ref is not None:
                r = r + add_ref[...]
            o_ref[...] = r.astype(o_ref.dtype)

        if nk == 1:
            finish(part)
            return
        acc = refs[-1]
        k = pl.program_id(2)

        @pl.when(k == 0)
        def _():
            acc[...] = part

        @pl.when(k > 0)
        def _():
            acc[...] += part

        @pl.when(k == nk - 1)
        def _():
            finish(acc[...])

    a_spec = pl.BlockSpec((tk, tm), lambda i, j, k: (k, i)) if ta else pl.BlockSpec((tm, tk), lambda i, j, k: (i, k))
    b_spec = pl.BlockSpec((tn, tk), lambda i, j, k: (j, k)) if tb else pl.BlockSpec((tk, tn), lambda i, j, k: (k, j))
    in_specs = [a_spec, b_spec]
    args = [a, b]
    if add is not None:
        in_specs.append(pl.BlockSpec((tm, tn), lambda i, j, k: (i, j)))
        args.append(add)
    return _pcall(
        body, name=name, grid=(m // tm, n // tn, nk), in_specs=in_specs,
        out_specs=pl.BlockSpec((tm, tn), lambda i, j, k: (i, j)),
        out_shape=jax.ShapeDtypeStruct((m, n), out_dtype),
        scratch_shapes=[] if nk == 1 else [pltpu.VMEM((tm, tn), F32)],
        compiler_params=_cparams(("parallel", "parallel", "arbitrary")),
    )(*args)


def _row_spec(br, w, cb):
    return pl.BlockSpec((br, w), lambda i: (i, cb))


def _const_spec(shape):
    return pl.BlockSpec(shape, lambda i: (0,) * len(shape))


def _rows_fwd(fn, rows, consts, outs, *, name, br, acc_shape=None):
    s = rows[0][0].shape[0]
    nr, nc = len(rows), len(consts)
    kept = [k for k, o in enumerate(outs) if o is not None]

    def body(*refs):
        xs = [r[...].astype(F32) for r in refs[:nr]]
        cs = [c[...] for c in refs[nr:nr + nc]]
        res = fn(*xs, *cs)
        orefs = refs[nr + nc:]
        for j, k in enumerate(kept):
            orefs[j][...] = res[k].astype(orefs[j].dtype)
        if acc_shape is not None:
            acc_ref = orefs[len(kept)]

            @pl.when(pl.program_id(0) == 0)
            def _():
                acc_ref[...] = jnp.zeros_like(acc_ref)

            acc_ref[...] += res[len(outs)]

    in_specs = [_row_spec(br, w, cb) for (_, w, cb) in rows] + [_const_spec(c.shape) for c in consts]
    out_specs = [_row_spec(br, outs[k][0], 0) for k in kept]
    out_shape = [jax.ShapeDtypeStruct((s, outs[k][0]), outs[k][1]) for k in kept]
    if acc_shape is not None:
        out_specs.append(_const_spec(acc_shape))
        out_shape.append(jax.ShapeDtypeStruct(acc_shape, F32))
    return _pcall(
        body, name=name, grid=(s // br,), in_specs=in_specs, out_specs=out_specs, out_shape=out_shape,
        compiler_params=_cparams(("arbitrary",)),
    )(*[r[0] for r in rows], *consts)


def _rows_bwd(fn, rows, consts, cots, *, wrt_rows, wrt_consts, drow_dtypes, name, br, unit_cot=False):
    s = rows[0][0].shape[0]
    nr, nc = len(rows), len(consts)
    flat_cots = [c for lst in cots for c in lst]
    ncot = len(flat_cots)

    def body(*refs):
        xs = [r[...].astype(F32) for r in refs[:nr]]
        cs = [c[...] for c in refs[nr:nr + nc]]
        cvals = [c[...].astype(F32) for c in refs[nr + nc:nr + nc + ncot]]
        orefs = refs[nr + nc + ncot:]

        def g(*d):
            xs2, cs2 = list(xs), list(cs)
            for j, k in enumerate(wrt_rows):
                xs2[k] = d[j]
            for j, k in enumerate(wrt_consts):
                cs2[k] = d[len(wrt_rows) + j]
            return tuple(fn(*xs2, *cs2))

        prim = [xs[k] for k in wrt_rows] + [cs[k] for k in wrt_consts]
        outs, vjp = jax.vjp(g, *prim)
        ct = []
        pos = 0
        for o, lst in zip(outs, cots):
            if unit_cot:
                ct.append(jnp.ones_like(o))
                continue
            acc = jnp.zeros_like(o)
            for _ in lst:
                acc = acc + cvals[pos]
                pos += 1
            ct.append(acc)
        grads = vjp(tuple(ct))
        for j in range(len(wrt_rows)):
            orefs[j][...] = grads[j].astype(orefs[j].dtype)

        @pl.when(pl.program_id(0) == 0)
        def _():
            for j in range(len(wrt_consts)):
                oref = orefs[len(wrt_rows) + j]
                oref[...] = jnp.zeros_like(oref)

        for j in range(len(wrt_consts)):
            orefs[len(wrt_rows) + j][...] += grads[len(wrt_rows) + j]

    in_specs = ([_row_spec(br, w, cb) for (_, w, cb) in rows] + [_const_spec(c.shape) for c in consts]
                + [_row_spec(br, w, cb) for (_, w, cb) in flat_cots])
    out_specs = [_row_spec(br, rows[k][1], 0) for k in wrt_rows] + [_const_spec(consts[k].shape) for k in wrt_consts]
    out_shape = ([jax.ShapeDtypeStruct((s, rows[k][1]), dt) for k, dt in zip(wrt_rows, drow_dtypes)]
                 + [jax.ShapeDtypeStruct(consts[k].shape, F32) for k in wrt_consts])
    return _pcall(
        body, name=name, grid=(s // br,), in_specs=in_specs, out_specs=out_specs, out_shape=out_shape,
        compiler_params=_cparams(("arbitrary",)),
    )(*[r[0] for r in rows], *consts, *[c[0] for c in flat_cots])


def _rms(x, w):
    return x * lax.rsqrt(jnp.mean(x * x, axis=-1, keepdims=True) + RMS_EPS) * w


def _softplus(x):
    return jnp.maximum(x, 0.0) + jnp.log(1.0 + jnp.exp(-jnp.abs(x)))


def _f_pre(x, nw, sc, sh):
    return _rms(x, nw) * (1.0 + sc) + sh, x


def _f_pre2(x, o, gt, nw, sc, sh):
    x1 = x + gt * o
    return x1, _rms(x1, nw) * (1.0 + sc) + sh


def _f_fin(x1, f, tgt, gt, nfw):
    y = _rms(x1 + gt * f, nfw)
    return (0.5 * jnp.mean(jnp.square(y - tgt), axis=-1, keepdims=True),)


def _f_comb(o1, o2, o3, l1, l2, l3):
    m = lax.stop_gradient(jnp.maximum(jnp.maximum(l1, l2), l3))
    e1, e2, e3 = jnp.exp(l1 - m), jnp.exp(l2 - m), jnp.exp(l3 - m)
    return ((e1 * o1 + e2 * o2 + e3 * o3) / (e1 + e2 + e3),)


def _f_rwpre(zs, w0, a0, k_k, k_a, wl, e, et):
    r, k, v, zl = zs[:, 0:D], zs[:, D:2 * D], zs[:, 2 * D:3 * D], zs[:, 3 * D:N_RWP]
    lane = lax.broadcasted_iota(jnp.int32, zl.shape, 1)
    t = jnp.where(lane < 64, jnp.tanh(zl), jnp.where(lane < 128, zl, jnp.where(lane < 288, jax.nn.sigmoid(zl), 0.0)))
    lo = _nn(t, wl)
    w_log = -_softplus(-(w0 + lo[:, 0:D])) - 0.5
    lw = -jnp.exp(w_log)
    a = jax.nn.sigmoid(a0 + lo[:, D:2 * D])
    g = lo[:, 2 * D:3 * D]
    k_mod = k * (1.0 + (a - 1.0) * k_a)
    kk = k * k_k
    kk = kk / jnp.maximum(jnp.sqrt(_hsum(kk * kk, e, et)), 1e-12)
    return r, lw, k_mod, v, -kk, kk * a, g


def _f_rwpost(y, r, v, k_mod, g, lnx_w, lnx_b, r_k, e, et):
    mean = _hsum(y, e, et) * (1.0 / 64)
    yc = y - mean
    var = _hsum(yc * yc, e, et) * (1.0 / 64)
    yn = yc * lax.rsqrt(var + GN_EPS) * lnx_w + lnx_b
    bonus = _hsum(r * k_mod * r_k, e, et) * v
    return ((yn + bonus) * g,)


def _f_mix(gia, gir, ya, yr, bga, bgr):
    return (jax.nn.sigmoid(gia + bga) * ya + jax.nn.sigmoid(gir + bgr) * yr,)


def _f_adamw(w, g, m, v):
    m = ADAM_B1 * m + (1.0 - ADAM_B1) * g
    v = ADAM_B2 * v + (1.0 - ADAM_B2) * jnp.square(g)
    m_hat = m / (1.0 - ADAM_B1 ** ADAM_STEP)
    v_hat = v / (1.0 - ADAM_B2 ** ADAM_STEP)
    return -ADAM_LR * (m_hat / (jnp.sqrt(v_hat) + ADAM_EPS) + ADAM_WD * w), m, v


def _down(x, k):
    row = lax.broadcasted_iota(jnp.int32, x.shape, 0)
    return jnp.where(row < k, 0.0, pltpu.roll(x, k, 0))


def _up(x, k):
    n = x.shape[0]
    row = lax.broadcasted_iota(jnp.int32, x.shape, 0)
    return jnp.where(row >= n - k, 0.0, pltpu.roll(x, n - k, 0))


def _col_spec(s, w, off=0):
    return pl.BlockSpec((s, w), lambda j: (0, j + off))


def _shift_fwd(z, mu):
    s, n = z.shape

    def body(z_ref, mu_ref, o_ref):
        zz = z_ref[...]
        o_ref[...] = zz + (_down(zz, 1) - zz) * mu_ref[...]

    return _pcall(
        body, name="shift_fwd", grid=(n // 128,), in_specs=[_col_spec(s, 128), _col_spec(1, 128)],
        out_specs=_col_spec(s, 128), out_shape=jax.ShapeDtypeStruct((s, n), F32),
        compiler_params=_cparams(("parallel",)),
    )(z, mu)


def _shift_bwd(z, mu, dzs):
    s, n = z.shape

    def body(z_ref, mu_ref, d_ref, dz_ref, dmu_ref):
        zz, d, m = z_ref[...], d_ref[...], mu_ref[...]
        dm = d * m
        dz_ref[...] = (d - dm + _up(dm, 1)).astype(dz_ref.dtype)
        dmu_ref[...] = jnp.sum(d * (_down(zz, 1) - zz), axis=0, keepdims=True)

    return _pcall(
        body, name="shift_bwd", grid=(n // 128,), in_specs=[_col_spec(s, 128), _col_spec(1, 128), _col_spec(s, 128)],
        out_specs=[_col_spec(s, 128), _col_spec(1, 128)],
        out_shape=[jax.ShapeDtypeStruct((s, n), BF16), jax.ShapeDtypeStruct((1, n), F32)],
        compiler_params=_cparams(("parallel",)),
    )(z, mu, dzs)


def _conv3(x, w_ref, b_ref):
    return b_ref[...] + w_ref[0:1, :] * _down(x, 2) + w_ref[1:2, :] * _down(x, 1) + w_ref[2:3, :] * x


def _conv_fwd(u, cw, cb):
    s = u.shape[0]
    nb = D_FF // 128

    def body(ug_ref, uv_ref, wg_ref, wv_ref, bg_ref, bv_ref, o_ref):
        gate = _conv3(ug_ref[...], wg_ref, bg_ref)
        val = _conv3(uv_ref[...], wv_ref, bv_ref)
        o_ref[...] = (gate * jax.nn.sigmoid(gate) * val).astype(o_ref.dtype)

    return _pcall(
        body, name="conv_fwd", grid=(nb,),
        in_specs=[_col_spec(s, 128), _col_spec(s, 128, nb), _col_spec(3, 128), _col_spec(3, 128, nb),
                  _col_spec(1, 128), _col_spec(1, 128, nb)],
        out_specs=_col_spec(s, 128), out_shape=jax.ShapeDtypeStruct((s, D_FF), BF16),
        compiler_params=_cparams(("parallel",)),
    )(u, u, cw, cw, cb, cb)


def _conv_bwd(u, cw, cb, dact):
    s = u.shape[0]
    nb = D_FF // 128

    def half(x, d, w_ref, du_ref, dw_ref, db_ref):
        x1, x2 = _down(x, 1), _down(x, 2)
        du_ref[...] = (w_ref[2:3, :] * d + w_ref[1:2, :] * _up(d, 1) + w_ref[0:1, :] * _up(d, 2)).astype(du_ref.dtype)
        dw_ref[0:1, :] = jnp.sum(d * x2, axis=0, keepdims=True)
        dw_ref[1:2, :] = jnp.sum(d * x1, axis=0, keepdims=True)
        dw_ref[2:3, :] = jnp.sum(d * x, axis=0, keepdims=True)
        db_ref[...] = jnp.sum(d, axis=0, keepdims=True)

    def body(ug_ref, uv_ref, wg_ref, wv_ref, bg_ref, bv_ref, da_ref,
             dug_ref, duv_ref, dwg_ref, dwv_ref, dbg_ref, dbv_ref):
        ug, uv, da = ug_ref[...], uv_ref[...], da_ref[...]
        gate = _conv3(ug, wg_ref, bg_ref)
        val = _conv3(uv, wv_ref, bv_ref)
        sg = jax.nn.sigmoid(gate)
        dgate = da * val * sg * (1.0 + gate * (1.0 - sg))
        dval = da * gate * sg
        half(ug, dgate, wg_ref, dug_ref, dwg_ref, dbg_ref)
        half(uv, dval, wv_ref, duv_ref, dwv_ref, dbv_ref)

    dug, duv, dwg, dwv, dbg, dbv = _pcall(
        body, name="conv_bwd", grid=(nb,),
        in_specs=[_col_spec(s, 128), _col_spec(s, 128, nb), _col_spec(3, 128), _col_spec(3, 128, nb),
                  _col_spec(1, 128), _col_spec(1, 128, nb), _col_spec(s, 128)],
        out_specs=[_col_spec(s, 128), _col_spec(s, 128), _col_spec(3, 128), _col_spec(3, 128),
                   _col_spec(1, 128), _col_spec(1, 128)],
        out_shape=[jax.ShapeDtypeStruct((s, D_FF), BF16), jax.ShapeDtypeStruct((s, D_FF), BF16),
                   jax.ShapeDtypeStruct((3, D_FF), F32), jax.ShapeDtypeStruct((3, D_FF), F32),
                   jax.ShapeDtypeStruct((1, D_FF), F32), jax.ShapeDtypeStruct((1, D_FF), F32)],
        compiler_params=_cparams(("parallel",)),
    )(u, u, cw, cw, cb, cb, dact)
    return (jnp.concatenate([dug, duv], axis=1), jnp.concatenate([dwg, dwv], axis=1),
            jnp.concatenate([dbg, dbv], axis=1))


ATT_BATCH = 4


def _att_batch(q, kp, kc, vp, vc, first):
    ma = lax.broadcasted_iota(jnp.int32, (1, ATT_BLOCK, 128), 2) < 64
    qs = jnp.concatenate([jnp.where(ma, q, 0.0), jnp.where(ma, 0.0, q)], axis=1)
    qi = lax.broadcasted_iota(jnp.int32, (1, 2 * ATT_BLOCK, ATT_BLOCK), 1) & (ATT_BLOCK - 1)
    kj = lax.broadcasted_iota(jnp.int32, (1, 2 * ATT_BLOCK, ATT_BLOCK), 2)
    okp = kj >= qi + jnp.where(first, 2 * ATT_BLOCK, 0)
    okc = kj <= qi
    sp = jnp.where(okp, _bnt(qs, kp) * 0.125, NEG)
    sc = jnp.where(okc, _bnt(qs, kc) * 0.125, NEG)
    m = lax.stop_gradient(jnp.maximum(jnp.max(sp, axis=-1, keepdims=True), jnp.max(sc, axis=-1, keepdims=True)))
    pp, pc = jnp.exp(sp - m), jnp.exp(sc - m)
    den = jnp.sum(pp, axis=-1, keepdims=True) + jnp.sum(pc, axis=-1, keepdims=True)
    o_s = (_bnn(pp, vp) + _bnn(pc, vc)) / den
    l_s = jnp.broadcast_to(m + jnp.log(den), o_s.shape)
    return (jnp.where(ma, o_s[:, :ATT_BLOCK], o_s[:, ATT_BLOCK:]), jnp.where(ma, l_s[:, :ATT_BLOCK], l_s[:, ATT_BLOCK:]))


def _att_pairs_per_step(dil):
    return ATT_BATCH if dil == 1 else 1


def _att_specs(g, dil):
    rows, pp = ATT_BLOCK * dil, _att_pairs_per_step(dil)

    def cur(slot):
        return pl.BlockSpec((rows, 128 * pp), lambda n, p: (n, (g * 3 + slot) * (4 // pp) + p))

    def prev(slot):
        return pl.BlockSpec((rows, 128 * pp), lambda n, p: (jnp.maximum(n - 1, 0), (g * 3 + slot) * (4 // pp) + p))

    return [cur(0), prev(1), cur(1), prev(2), cur(2)]


def _att_out_spec(dil):
    return pl.BlockSpec((ATT_BLOCK * dil, 128 * _att_pairs_per_step(dil)), lambda n, p: (n, p))


def _att_grid(s, dil):
    return (s // (ATT_BLOCK * dil), 4 // _att_pairs_per_step(dil))


def _att_windows(i, dil):
    if dil == 1:
        return [(pl.ds(0, ATT_BLOCK), pl.ds(128 * j, 128)) for j in range(ATT_BATCH)]
    return [(pl.ds(i * ATT_BATCH + j, ATT_BLOCK, stride=dil), pl.ds(0, 128)) for j in range(ATT_BATCH)]


def _att_fwd(att_in, g, dil):
    s = att_in.shape[0]

    def body(q_ref, kp_ref, kc_ref, vp_ref, vc_ref, o_ref, l_ref):
        first = pl.program_id(0) == 0

        def one(i, carry):
            win = _att_windows(i, dil)
            vals = [jnp.stack([ref[w] for w in win]) for ref in (q_ref, kp_ref, kc_ref, vp_ref, vc_ref)]
            o, l = _att_batch(*vals, first)
            for j, w in enumerate(win):
                o_ref[w] = o[j]
                l_ref[w] = l[j]
            return carry

        lax.fori_loop(0, max(1, dil // ATT_BATCH), one, 0)

    return _pcall(
        body, name=f"att_fwd{g}", grid=_att_grid(s, dil), in_specs=_att_specs(g, dil),
        out_specs=[_att_out_spec(dil)] * 2, out_shape=[jax.ShapeDtypeStruct((s, ATT_WIDTH), F32)] * 2,
        compiler_params=_cparams(("parallel", "parallel")),
    )(att_in, att_in, att_in, att_in, att_in)


def _att_bwd(att_in, g, dil, do, dl):
    s = att_in.shape[0]
    nb = s // (ATT_BLOCK * dil)

    def body(q_ref, kp_ref, kc_ref, vp_ref, vc_ref, do_ref, dl_ref, dq_ref, dkp_ref, dkc_ref, dvp_ref, dvc_ref):
        first = pl.program_id(0) == 0

        def one(i, carry):
            win = _att_windows(i, dil)
            vals = [jnp.stack([ref[w] for w in win]) for ref in (q_ref, kp_ref, kc_ref, vp_ref, vc_ref)]
            _, vjp = jax.vjp(lambda *a: _att_batch(*a, first), *vals)
            grads = vjp((jnp.stack([do_ref[w] for w in win]), jnp.stack([dl_ref[w] for w in win])))
            for ref, gr in zip((dq_ref, dkp_ref, dkc_ref, dvp_ref, dvc_ref), grads):
                for j, w in enumerate(win):
                    ref[w] = gr[j]
            return carry

        lax.fori_loop(0, max(1, dil // ATT_BATCH), one, 0)

    dq, dkp, dkc, dvp, dvc = _pcall(
        body, name=f"att_bwd{g}", grid=_att_grid(s, dil), in_specs=_att_specs(g, dil) + [_att_out_spec(dil)] * 2,
        out_specs=[_att_out_spec(dil)] * 5, out_shape=[jax.ShapeDtypeStruct((s, ATT_WIDTH), F32)] * 5,
        compiler_params=_cparams(("parallel", "parallel")),
    )(att_in, att_in, att_in, att_in, att_in, do, dl)

    def cbody(dq_ref, dkc_ref, dkn_ref, dvc_ref, dvn_ref, oq_ref, ok_ref, ov_ref):
        has_next = pl.program_id(0) + 1 < nb
        oq_ref[...] = dq_ref[...].astype(BF16)
        ok_ref[...] = (dkc_ref[...] + jnp.where(has_next, dkn_ref[...], 0.0)).astype(BF16)
        ov_ref[...] = (dvc_ref[...] + jnp.where(has_next, dvn_ref[...], 0.0)).astype(BF16)

    cur = pl.BlockSpec((ATT_BLOCK * dil, 128), lambda n, p: (n, p))
    nxt = pl.BlockSpec((ATT_BLOCK * dil, 128), lambda n, p: (jnp.minimum(n + 1, nb - 1), p))
    return _pcall(
        cbody, name=f"att_bwd_sum{g}", grid=(nb, 4), in_specs=[cur, cur, nxt, cur, nxt], out_specs=[cur] * 3,
        out_shape=[jax.ShapeDtypeStruct((s, ATT_WIDTH), BF16)] * 3,
        compiler_params=_cparams(("parallel", "parallel")),
    )(dq, dkc, dkp, dvc, dvp)


def _scan_chunk(r, lw, k, v, a, b, s0):
    c = SCAN_CHUNK
    p = s0.shape[0]
    ri = lax.broadcasted_iota(jnp.int32, (c, c), 0)
    ci = lax.broadcasted_iota(jnp.int32, (c, c), 1)
    cum = jnp.dot((ci <= ri).astype(F32), lw, precision=HI, preferred_element_type=F32)
    tot = jnp.sum(lw, axis=0, keepdims=True)
    ma = (lax.broadcasted_iota(jnp.int32, (c, 128 * p), 1) & 127) < 64

    def pairs(x):
        return jnp.concatenate([x[None, :, 128 * j:128 * (j + 1)] for j in range(p)], axis=0)

    def stack(x):
        return jnp.concatenate([pairs(jnp.where(ma, x, 0.0)), pairs(jnp.where(ma, 0.0, x))], axis=1)

    einv, eend = jnp.exp(-cum), jnp.exp(tot - cum)
    ra, aa = stack(r * jnp.exp(cum)), stack(a * jnp.exp(cum - lw))
    bi, ki, be, ke, vs = stack(b * einv), stack(k * einv), stack(b * eend), stack(k * eend), stack(v)
    r2 = lax.broadcasted_iota(jnp.int32, (1, 2 * c, 2 * c), 1)
    c2 = lax.broadcasted_iota(jnp.int32, (1, 2 * c, 2 * c), 2)
    same = (r2 >= c) == (c2 >= c)
    strict = jnp.logical_and(same, c2 < r2)
    incl = jnp.logical_and(same, c2 <= r2)
    s0 = jnp.where(same, s0, 0.0)
    a_ab = jnp.where(strict, _bnt(aa, bi), 0.0)
    a_ak = jnp.where(strict, _bnt(aa, ki), 0.0)
    a_rb = jnp.where(incl, _bnt(ra, bi), 0.0)
    a_rk = jnp.where(incl, _bnt(ra, ki), 0.0)
    t = jnp.where(r2 == c2, 1.0, 0.0) + a_ab
    pw = a_ab
    for _ in range(5):
        pw = _bnn(pw, pw)
        t = t + _bnn(t, pw)
    u = _bnn(t, _bnt(aa, s0) + _bnn(a_ak, vs))
    ys = _bnt(ra, s0) + _bnn(a_rb, u) + _bnn(a_rk, vs)
    s1 = s0 * pairs(jnp.exp(tot)) + _btn(u, be) + _btn(vs, ke)
    y3 = ys[:, :c] + ys[:, c:]
    return jnp.concatenate([y3[j] for j in range(p)], axis=1), s1


def _scan_specs(rev, n):
    w = 128 * SCAN_PAIRS

    def at(i):
        return n - 1 - i if rev else i

    def cm(cb):
        return pl.BlockSpec((SCAN_CHUNK, w), lambda p, i: (at(i), cb * (D // w) + p))

    return cm, pl.BlockSpec((1, SCAN_PAIRS, 128, 128), lambda p, i: (at(i), p, 0, 0))


def _scan_fwd(zs, lw, km, aa, bb):
    s = zs.shape[0]
    n = s // SCAN_CHUNK
    cm, st = _scan_specs(False, n)

    def body(r_ref, lw_ref, k_ref, v_ref, a_ref, b_ref, y_ref, s0_ref, state):
        @pl.when(pl.program_id(1) == 0)
        def _():
            state[...] = jnp.zeros_like(state)

        s0 = state[...]
        s0_ref[0] = s0
        y, s1 = _scan_chunk(*[ref[...] for ref in (r_ref, lw_ref, k_ref, v_ref, a_ref, b_ref)], s0)
        y_ref[...] = y
        state[...] = s1

    return _pcall(
        body, name="scan_fwd", grid=(8 // SCAN_PAIRS, n), in_specs=[cm(0), cm(0), cm(0), cm(2), cm(0), cm(0)],
        out_specs=[cm(0), st],
        out_shape=[jax.ShapeDtypeStruct((s, D), F32), jax.ShapeDtypeStruct((n, 8, 128, 128), F32)],
        scratch_shapes=[pltpu.VMEM((SCAN_PAIRS, 128, 128), F32)],
        compiler_params=_cparams(("parallel", "arbitrary")),
    )(zs, lw, km, zs, aa, bb)


def _scan_bwd(zs, lw, km, aa, bb, s0s, dy):
    s = zs.shape[0]
    n = s // SCAN_CHUNK
    cm, st = _scan_specs(True, n)

    def body(r_ref, lw_ref, k_ref, v_ref, a_ref, b_ref, s0_ref, dy_ref,
             dr_ref, dlw_ref, dk_ref, dv_ref, da_ref, db_ref, dstate):
        @pl.when(pl.program_id(1) == 0)
        def _():
            dstate[...] = jnp.zeros_like(dstate)

        prim = [ref[...] for ref in (r_ref, lw_ref, k_ref, v_ref, a_ref, b_ref)] + [s0_ref[0]]
        _, vjp = jax.vjp(_scan_chunk, *prim)
        grads = vjp((dy_ref[...], dstate[...]))
        for ref, gr in zip((dr_ref, dlw_ref, dk_ref, dv_ref, da_ref, db_ref), grads[:6]):
            ref[...] = gr
        dstate[...] = grads[6]

    return _pcall(
        body, name="scan_bwd", grid=(8 // SCAN_PAIRS, n),
        in_specs=[cm(0), cm(0), cm(0), cm(2), cm(0), cm(0), st, cm(0)],
        out_specs=[cm(0)] * 6, out_shape=[jax.ShapeDtypeStruct((s, D), F32)] * 6,
        scratch_shapes=[pltpu.VMEM((SCAN_PAIRS, 128, 128), F32)],
        compiler_params=_cparams(("parallel", "arbitrary")),
    )(zs, lw, km, zs, aa, bb, s0s, dy)


_HBM = pl.BlockSpec(memory_space=pltpu.HBM)


def _me():
    return lax.axis_index("x"), lax.axis_index("y"), lax.axis_index("c")


def _allgather8(src, name):
    def body(src_ref, out_ref, ssem, rsem, lsem):
        x, y, c = _me()
        me = 4 * x + 2 * y + c
        local = pltpu.make_async_copy(src_ref, out_ref.at[me], lsem)
        local.start()
        peers = []
        for k in range(1, 8):
            peers.append(((1 - x) if k & 4 else x, (1 - y) if k & 2 else y, (1 - c) if k & 1 else c))
        sends = []
        for k, peer in enumerate(peers):
            cp = pltpu.make_async_remote_copy(src_ref, out_ref.at[me], ssem.at[k], rsem.at[k], device_id=peer,
                                              device_id_type=MESH)
            cp.start()
            sends.append(cp)
        for k, (px, py, pc) in enumerate(peers):
            pltpu.make_async_remote_copy(src_ref, out_ref.at[4 * px + 2 * py + pc], ssem.at[k], rsem.at[k],
                                         device_id=(px, py, pc), device_id_type=MESH).wait_recv()
        for cp in sends:
            cp.wait_send()
        local.wait()

    return _pcall(
        body, name=name, in_specs=[_HBM], out_specs=_HBM, out_shape=jax.ShapeDtypeStruct((8,) + src.shape, src.dtype),
        scratch_shapes=[pltpu.SemaphoreType.DMA((7,)), pltpu.SemaphoreType.DMA((7,)), pltpu.SemaphoreType.DMA],
    )(src)


def _chip_exchange(src, gather, name):
    shape = src.shape if gather else src.shape[1:]

    def body(src_ref, out_ref, ssem, rsem, lsem):
        x, y, c = _me()
        me = 2 * x + y
        local = pltpu.make_async_copy(src_ref if gather else src_ref.at[me], out_ref.at[me], lsem)
        local.start()
        chips = [(1 - x, y), (x, 1 - y), (1 - x, 1 - y)]
        sends = []
        for k, (px, py) in enumerate(chips):
            cp = pltpu.make_async_remote_copy(src_ref if gather else src_ref.at[2 * px + py], out_ref.at[me],
                                              ssem.at[k], rsem.at[k], device_id=(px, py, c), device_id_type=MESH)
            cp.start()
            sends.append(cp)
        for k, (px, py) in enumerate(chips):
            pltpu.make_async_remote_copy(src_ref if gather else src_ref.at[me], out_ref.at[2 * px + py],
                                         ssem.at[k], rsem.at[k], device_id=(px, py, c), device_id_type=MESH).wait_recv()
        for cp in sends:
            cp.wait_send()
        local.wait()

    return _pcall(
        body, name=name, in_specs=[_HBM], out_specs=_HBM, out_shape=jax.ShapeDtypeStruct((4,) + shape, src.dtype),
        scratch_shapes=[pltpu.SemaphoreType.DMA((3,)), pltpu.SemaphoreType.DMA((3,)), pltpu.SemaphoreType.DMA],
    )(src)


def _sib_swap(src, name):
    def body(src_ref, out_ref, ssem, rsem):
        x, y, c = _me()
        cp = pltpu.make_async_remote_copy(src_ref, out_ref, ssem, rsem, device_id=(x, y, 1 - c), device_id_type=MESH)
        cp.start()
        cp.wait_recv()
        cp.wait_send()

    return _pcall(
        body, name=name, in_specs=[_HBM], out_specs=_HBM, out_shape=jax.ShapeDtypeStruct(src.shape, src.dtype),
        scratch_shapes=[pltpu.SemaphoreType.DMA, pltpu.SemaphoreType.DMA],
    )(src)


def _ada_fwd(c_all, w, b):
    def body(c_ref, w_ref, b_ref, o_ref):
        o_ref[...] = jnp.dot(c_ref[...], w_ref[...], precision=HI, preferred_element_type=F32) + b_ref[...]

    return _pcall(body, name="ada_fwd", out_shape=jax.ShapeDtypeStruct((c_all.shape[0], w.shape[1]), F32),
                  compiler_params=pltpu.CompilerParams(vmem_limit_bytes=VMEM_LIMIT))(c_all, w, b)


def _ada_bwd(c_all_t, d):
    def body(c_ref, d_ref, o_ref):
        o_ref[...] = jnp.dot(c_ref[...], d_ref[...], precision=HI, preferred_element_type=F32)

    return _pcall(body, name="ada_bwd", out_shape=jax.ShapeDtypeStruct((c_all_t.shape[0], d.shape[1]), F32),
                  compiler_params=pltpu.CompilerParams(vmem_limit_bytes=VMEM_LIMIT))(c_all_t, d)


def _sum_lead(x, name):
    p, r, n = x.shape
    br = _div(r, 512, 8)

    def body(x_ref, o_ref):
        acc = x_ref[0]
        for j in range(1, p):
            acc = acc + x_ref[j]
        o_ref[...] = acc

    return _pcall(
        body, name=name, grid=(r // br,), in_specs=[pl.BlockSpec((p, br, n), lambda i: (0, i, 0))],
        out_specs=pl.BlockSpec((br, n), lambda i: (i, 0)), out_shape=jax.ShapeDtypeStruct((r, n), F32),
        compiler_params=_cparams(("parallel",)),
    )(x)


def _adamw(w, g, m, v, name):
    shape = w.shape
    cols = shape[-1]
    w2, g2, m2, v2 = [t.reshape(-1, cols) for t in (w, g, m, v)]
    rows = w2.shape[0]
    br = _div(rows, max(8, (1 << 19) // cols // 8 * 8), 8)
    outs = _rows_fwd(_f_adamw, [(t, cols, 0) for t in (w2, g2, m2, v2)], [], [(cols, F32)] * 3, name=name, br=br)
    return [o.reshape(shape) for o in outs]


_BIG = (("w_in", 1), ("w_up", 1), ("w_down", 0), ("w_o", 0), ("w_rwkv_out", 0), ("w_att_out", 1), ("w2", 1), ("a2", 1),
        ("g2", 1))


def _pack_rows(shards, dtype):
    flat = [t.reshape(-1, D).astype(dtype) for t in shards]
    rows = sum(t.shape[0] for t in flat)
    pad = (-rows) % 32
    if pad:
        flat.append(jnp.zeros((pad, D), dtype))
    return jnp.concatenate(flat, axis=0)


def _unpack_rows(buf, shapes):
    out, pos = [], 0
    for shp in shapes:
        nrow = shp[0] * shp[1] // D
        out.append(buf[pos:pos + nrow].reshape(shp))
        pos += nrow
    return out


def _shard_of(full, axis, j):
    n = full.shape[axis] // 4
    return lax.slice_in_dim(full, j * n, (j + 1) * n, axis=axis)


def _local_step(x, tgt, ada, wts):
    s = x.shape[0]
    sh1, sc1, gt1, sh2, sc2, gt2 = ada
    br = 256
    grp = lax.broadcasted_iota(jnp.int32, (D, 128), 0) // 64 == lax.broadcasted_iota(jnp.int32, (D, 128), 1)
    e = grp.astype(F32)
    et = e.T
    w_in = wts["w_in"]
    w_att = w_in[:, :N_ATT]
    w_rw = jnp.pad(w_in[:, N_ATT:N_ATT + N_RW], ((0, 0), (0, N_RWP - N_RW)))
    w_gate = w_in[:, N_ATT + N_RW:]
    mu = jnp.pad(wts["mu_shift"], ((0, 0), (0, N_RWP - N_RW)))
    wl = jnp.zeros((N_LORA, 3 * D), F32)
    wl = wl.at[0:64, 0:D].set(wts["w2"].astype(F32)).at[64:128, D:2 * D].set(wts["a2"].astype(F32))
    wl = wl.at[128:288, 2 * D:3 * D].set(wts["g2"].astype(F32))
    bga, bgr = wts["b_gate"][:, :D], wts["b_gate"][:, D:]
    rk = wts["r_k"]

    pre1_c = [wts["norm1_w"], sc1, sh1]
    (h1,) = _rows_fwd(_f_pre, [(x, D, 0)], pre1_c, [(D, BF16), None], name="pre1_fwd", br=br)
    att_in = _mm(h1, w_att, name="mm_att_in")
    z = _mm(h1, w_rw, name="mm_rw_in")
    gate_in = _mm(h1, w_gate, name="mm_gate_in")
    att_o, att_l = [], []
    for g, (_, dil) in enumerate(ATT_PATTERNS):
        o, l = _att_fwd(att_in, g, dil)
        att_o.append(o)
        att_l.append(l)
    comb_rows = [(t, ATT_WIDTH, 0) for t in att_o + att_l]
    (att,) = _rows_fwd(_f_comb, comb_rows, [], [(ATT_WIDTH, BF16)], name="comb_fwd", br=br)
    y_att = _mm(att, wts["w_att_out"], name="mm_att_out")
    zs = _shift_fwd(z, mu)
    rwpre_c = [wts["w0"], wts["a0"], wts["k_k"], wts["k_a"], wl, e, et]
    lw, km, aa, bb, gg = _rows_fwd(_f_rwpre, [(zs, N_RWP, 0)], rwpre_c,
                                   [None, (D, F32), (D, F32), None, (D, F32), (D, F32), (D, F32)],
                                   name="rwpre_fwd", br=br)
    y_raw, s0s = _scan_fwd(zs, lw, km, aa, bb)
    post_rows = [(y_raw, D, 0), (zs, D, 0), (zs, D, 2), (km, D, 0), (gg, D, 0)]
    post_c = [wts["lnx_w"], wts["lnx_b"], rk, e, et]
    (rw_out,) = _rows_fwd(_f_rwpost, post_rows, post_c, [(D, BF16)], name="rwpost_fwd", br=br)
    y_rw = _mm(rw_out, wts["w_rwkv_out"], name="mm_rw_out")
    mix_rows = [(gate_in, D, 0), (gate_in, D, 1), (y_att, D, 0), (y_rw, D, 0)]
    (mix,) = _rows_fwd(_f_mix, mix_rows, [bga, bgr], [(D, BF16)], name="mix_fwd", br=br)
    o = _mm(mix, wts["w_o"], name="mm_o")
    pre2_c = [gt1, wts["norm2_w"], sc2, sh2]
    x1, h2 = _rows_fwd(_f_pre2, [(x, D, 0), (o, D, 0)], pre2_c, [(D, F32), (D, BF16)], name="pre2_fwd", br=br)
    u = _mm(h2, wts["w_up"], name="mm_up")
    act = _conv_fwd(u, wts["conv_w"], wts["conv_b"])
    f = _mm(act, wts["w_down"], name="mm_down")
    fin_rows = [(x1, D, 0), (f, D, 0), (tgt, D, 0)]
    fin_c = [gt2, wts["norm_f_w"]]

    def fin_fwd(*a):
        (l,) = _f_fin(*a)
        return (jnp.broadcast_to(jnp.sum(l, axis=0, keepdims=True), (8, 128)),)

    (loss_acc,) = _rows_fwd(fin_fwd, fin_rows, fin_c, [], name="fin_fwd", br=br, acc_shape=(8, 128))
    loss = loss_acc[0, 0]

    gw = {}
    dx1a, df, d_gt2, gw["norm_f_w"] = _rows_bwd(
        _f_fin, fin_rows, fin_c, [[]], wrt_rows=[0, 1], wrt_consts=[0, 1], drow_dtypes=[F32, BF16],
        name="fin_bwd", br=br, unit_cot=True)
    dact = _mm(df, wts["w_down"], tb=True, name="mm_dact")
    gw["w_down"] = _mm(act, df, ta=True, name="mm_dw_down")
    du, gw["conv_w"], gw["conv_b"] = _conv_bwd(u, wts["conv_w"], wts["conv_b"], dact)
    dh2 = _mm(du, wts["w_up"], tb=True, name="mm_dh2")
    gw["w_up"] = _mm(h2, du, ta=True, name="mm_dw_up")
    dxa, do, d_gt1, gw["norm2_w"], d_sc2, d_sh2 = _rows_bwd(
        _f_pre2, [(x, D, 0), (o, D, 0)], pre2_c, [[(dx1a, D, 0)], [(dh2, D, 0)]], wrt_rows=[0, 1],
        wrt_consts=[0, 1, 2, 3], drow_dtypes=[F32, BF16], name="pre2_bwd", br=br)
    dmix = _mm(do, wts["w_o"], tb=True, name="mm_dmix")
    gw["w_o"] = _mm(mix, do, ta=True, name="mm_dw_o")
    dga, dgr, dya, dyr, d_bga, d_bgr = _rows_bwd(
        _f_mix, mix_rows, [bga, bgr], [[(dmix, D, 0)]], wrt_rows=[0, 1, 2, 3], wrt_consts=[0, 1],
        drow_dtypes=[BF16] * 4, name="mix_bwd", br=br)
    gw["b_gate"] = jnp.concatenate([d_bga, d_bgr], axis=1)
    datt = _mm(dya, wts["w_att_out"], tb=True, name="mm_datt")
    gw["w_att_out"] = _mm(att, dya, ta=True, name="mm_dw_att_out")
    drw = _mm(dyr, wts["w_rwkv_out"], tb=True, name="mm_drw")
    gw["w_rwkv_out"] = _mm(rw_out, dyr, ta=True, name="mm_dw_rw_out")
    dcomb = _rows_bwd(_f_comb, comb_rows, [], [[(datt, ATT_WIDTH, 0)]], wrt_rows=list(range(6)), wrt_consts=[],
                      drow_dtypes=[F32] * 6, name="comb_bwd", br=br)
    datt_in = []
    for g, (_, dil) in enumerate(ATT_PATTERNS):
        datt_in += _att_bwd(att_in, g, dil, dcomb[g], dcomb[3 + g])
    datt_in = jnp.concatenate(datt_in, axis=1)
    dy_raw, dr_p, dv_p, dkm_p, dgg, gw["lnx_w"], gw["lnx_b"], gw["r_k"] = _rows_bwd(
        _f_rwpost, post_rows, post_c, [[(drw, D, 0)]], wrt_rows=[0, 1, 2, 3, 4], wrt_consts=[0, 1, 2],
        drow_dtypes=[F32] * 5, name="rwpost_bwd", br=br)
    dr_s, dlw, dkm_s, dv_s, daa, dbb = _scan_bwd(zs, lw, km, aa, bb, s0s, dy_raw)
    pre_cots = [[(dr_p, D, 0), (dr_s, D, 0)], [(dlw, D, 0)], [(dkm_p, D, 0), (dkm_s, D, 0)],
                [(dv_p, D, 0), (dv_s, D, 0)], [(daa, D, 0)], [(dbb, D, 0)], [(dgg, D, 0)]]
    dzs, gw["w0"], gw["a0"], gw["k_k"], gw["k_a"], dwl = _rows_bwd(
        _f_rwpre, [(zs, N_RWP, 0)], rwpre_c, pre_cots, wrt_rows=[0], wrt_consts=[0, 1, 2, 3, 4], drow_dtypes=[F32],
        name="rwpre_bwd", br=128)
    gw["w2"], gw["a2"], gw["g2"] = dwl[0:64, 0:D], dwl[64:128, D:2 * D], dwl[128:288, 2 * D:3 * D]
    dz, dmu = _shift_bwd(z, mu, dzs)
    gw["mu_shift"] = dmu[:, :N_RW]
    dgate = jnp.concatenate([dga, dgr], axis=1)
    dh1 = _mm(datt_in, w_att, tb=True, name="mm_dh1_att")
    dh1 = _mm(dz, w_rw, tb=True, add=dh1, name="mm_dh1_rw")
    dh1 = _mm(dgate, w_gate, tb=True, add=dh1, name="mm_dh1_gate")
    gw["w_in"] = jnp.concatenate([_mm(h1, datt_in, ta=True, name="mm_dw_att"),
                                  _mm(h1, dz, ta=True, name="mm_dw_rw")[:, :N_RW],
                                  _mm(h1, dgate, ta=True, name="mm_dw_gate")], axis=1)
    grad_x, gw["norm1_w"], d_sc1, d_sh1 = _rows_bwd(
        _f_pre, [(x, D, 0)], pre1_c, [[(dh1, D, 0)], [(dxa, D, 0)]], wrt_rows=[0], wrt_consts=[0, 1, 2],
        drow_dtypes=[F32], name="pre1_bwd", br=br)
    return loss, grad_x, (d_sh1, d_sc1, d_gt1, d_sh2, d_sc2, d_gt2), gw


_SMALL = ("b_ada", "norm1_w", "b_gate", "mu_shift", "w0", "a0", "k_k", "k_a", "r_k", "lnx_w", "lnx_b", "norm2_w",
          "conv_b", "norm_f_w")
_NAMES = ("w_ada", "b_ada", "norm1_w", "w_in", "b_gate", "mu_shift", "w0", "w2", "a0", "a2", "g2", "k_k", "k_a", "r_k",
          "lnx_w", "lnx_b", "w_att_out", "w_rwkv_out", "w_o", "norm2_w", "w_up", "conv_w", "conv_b", "w_down",
          "norm_f_w")


def kernel(x, c, w_ada, b_ada, norm1_w, w_in, b_gate, mu_shift, w0, w2, a0, a2, g2, k_k, k_a, r_k, lnx_w, lnx_b, w_att_out, w_rwkv_out, w_o, norm2_w, w_up, conv_w, conv_b, w_down, norm_f_w, loss_target, m_w_ada, m_b_ada, m_norm1_w, m_w_in, m_b_gate, m_mu_shift, m_w0, m_w2, m_a0, m_a2, m_g2, m_k_k, m_k_a, m_r_k, m_lnx_w, m_lnx_b, m_w_att_out, m_w_rwkv_out, m_w_o, m_norm2_w, m_w_up, m_conv_w, m_conv_b, m_w_down, m_norm_f_w, v_w_ada, v_b_ada, v_norm1_w, v_w_in, v_b_gate, v_mu_shift, v_w0, v_w2, v_a0, v_a2, v_g2, v_k_k, v_k_a, v_r_k, v_lnx_w, v_lnx_b, v_w_att_out, v_w_rwkv_out, v_w_o, v_norm2_w, v_w_up, v_conv_w, v_conv_b, v_w_down, v_norm_f_w):
    args = dict(locals())
    p, pm, pv = {}, {}, {}
    for name in _NAMES:
        for dst, key in ((p, name), (pm, "m_" + name), (pv, "v_" + name)):
            t = args[key]
            dst[name] = t.reshape(1, -1) if name in ("r_k", "norm_f_w") else t.reshape(t.shape[-2], t.shape[-1])
    xi, yi, ci = _me()
    chip = 2 * xi + yi
    dev = 4 * xi + 2 * yi + ci
    x2, tgt = x[0], loss_target[0]

    n_cw = 3 * (2 * D_FF // 4)
    vec = jnp.concatenate([c.reshape(-1), p["conv_w"].reshape(-1), jnp.zeros((8 * D - D - n_cw,), F32)]).reshape(8, D)
    g0 = _allgather8(vec, "gather_c").reshape(8, 8 * D)
    c_all = g0[:, :D]
    conv_w_full = jnp.concatenate([g0[2 * j, D:D + n_cw].reshape(3, -1) for j in range(4)], axis=1)
    n_ada = 6 * D // 4
    b_ada_sh = lax.dynamic_slice(p["b_ada"], (0, chip * n_ada), (1, n_ada))
    ada_sh = _ada_fwd(c_all, p["w_ada"], b_ada_sh)
    ga = _allgather8(ada_sh, "gather_ada")
    ada_all = jnp.concatenate([ga[2 * j] for j in range(4)], axis=1)
    ada_row = lax.dynamic_slice(ada_all, (dev, 0), (1, 6 * D))
    ada = [ada_row[:, j * D:(j + 1) * D] for j in range(6)]

    big_shapes = [p[n].shape for n, _ in _BIG]
    wpack = _pack_rows([p[n] for n, _ in _BIG], BF16)
    rtot = wpack.shape[0]
    rh = rtot // 2
    mine = lax.dynamic_slice(wpack, (ci * rh, 0), (rh, D))
    got = _chip_exchange(mine, True, "gather_w")
    other = _sib_swap(got, "gather_w_sib")
    lo_half = jnp.where(ci == 0, got, other)
    hi_half = jnp.where(ci == 0, other, got)
    wall = jnp.concatenate([lo_half, hi_half], axis=1)
    per_chip = [_unpack_rows(wall[j], big_shapes) for j in range(4)]
    wts = {n: jnp.concatenate([per_chip[j][k] for j in range(4)], axis=ax) for k, (n, ax) in enumerate(_BIG)}
    for n in _SMALL:
        wts[n] = p[n]
    wts["conv_w"] = conv_w_full

    loss_part, grad_x, d_ada, gw = _local_step(x2, tgt, ada, wts)

    small = [jnp.concatenate(d_ada, axis=1)] + [gw[n] for n in _SMALL[1:]] + [gw["conv_w"], loss_part.reshape(1, 1)]
    sizes = [t.size for t in small]
    flat = jnp.concatenate([t.reshape(-1) for t in small])
    npad = (-flat.shape[0]) % (8 * D)
    srows = (flat.shape[0] + npad) // D
    flat = jnp.concatenate([flat, jnp.zeros((npad,), F32)]).reshape(srows, D)
    parts = _allgather8(flat, "gather_small")
    tot = _sum_lead(parts, "sum_small").reshape(-1)
    pieces, pos = [], 0
    for sz in sizes:
        pieces.append(tot[pos:pos + sz])
        pos += sz
    grads = {}
    for n, piece in zip(_SMALL, pieces[:len(_SMALL)]):
        grads[n] = piece.reshape(p[n].shape)
    conv_w_grad = pieces[len(_SMALL)].reshape(3, 2 * D_FF)
    grads["conv_w"] = lax.dynamic_slice(conv_w_grad, (0, chip * (n_cw // 3)), (3, n_cw // 3))
    loss = pieces[-1][0]
    d_ada_all = parts[:, :6].reshape(8, 6 * D)
    grads["w_ada"] = _ada_bwd(c_all.T, lax.dynamic_slice(d_ada_all, (0, chip * n_ada), (8, n_ada)))

    gpack = jnp.stack([_pack_rows([_shard_of(gw[n], ax, j) for n, ax in _BIG], F32) for j in range(4)])
    halves = gpack.reshape(4, 2, rh, D)
    keep = lax.dynamic_index_in_dim(halves, ci, axis=1, keepdims=False)
    give = lax.dynamic_index_in_dim(halves, 1 - ci, axis=1, keepdims=False)
    recv = _sib_swap(give, "reduce_sib")
    lo_dev = jnp.where(ci == 0, keep, recv)
    hi_dev = jnp.where(ci == 0, recv, keep)
    chip_sum = _sum_lead(jnp.stack([lo_dev, hi_dev]).reshape(2, 4 * rh, D), "reduce_add2").reshape(4, rh, D)
    slots = _chip_exchange(chip_sum, False, "reduce_chips")
    red = _sum_lead(slots, "reduce_add4")
    red_other = _sib_swap(red, "reduce_sib2")
    red_full = jnp.concatenate([jnp.where(ci == 0, red, red_other), jnp.where(ci == 0, red_other, red)], axis=0)
    for (n, _), gshard in zip(_BIG, _unpack_rows(red_full, big_shapes)):
        grads[n] = gshard

    outs_g, outs_d, outs_m, outs_v = [], [], [], []
    for name in _NAMES:
        g = grads[name]
        d, m, v = _adamw(p[name], g, pm[name], pv[name], "adamw_" + name)
        shape = args[name].shape
        outs_g.append(g.reshape(shape))
        outs_d.append(d.reshape(shape))
        outs_m.append(m.reshape(shape))
        outs_v.append(v.reshape(shape))
    return (loss, grad_x.reshape(x.shape), *outs_g, *outs_d, *outs_m, *outs_v)
```

```python
import functools

import jax
import jax.numpy as jnp
from jax import lax
from jax.experimental import pallas as pl
from jax.experimental.pallas import tpu as pltpu

F32 = jnp.float32
BF16 = jnp.bfloat16
HI = lax.Precision.HIGHEST
MESH = pl.DeviceIdType.MESH

D = 1024
ATT_PATTERNS = ((128, 1), (512, 4), (2048, 16))
ATT_BLOCK = 128
ATT_WIDTH = 512
N_ATT = 3 * 3 * ATT_WIDTH
N_RW = 3 * D + 64 + 64 + 160
N_RWP = 3456
N_LORA = N_RWP - 3 * D
N_GATE = 2 * D
D_FF = 2816
RMS_EPS = 1e-6
GN_EPS = 64e-5
SCAN_CHUNK = 64
SCAN_PAIRS = 4
NEG = -1e30
VMEM_LIMIT = 48 * 1024 * 1024

ADAM_LR, ADAM_B1, ADAM_B2, ADAM_EPS, ADAM_WD, ADAM_STEP = 0.001, 0.9, 0.999, 1e-08, 0.01, 10


def _pcall(body, **kw):
    return pl.pallas_call(body, **kw)


def _cparams(sem):
    return pltpu.CompilerParams(dimension_semantics=sem, vmem_limit_bytes=VMEM_LIMIT)


def _div(n, pref, mult):
    best = None
    d = mult
    while d <= min(n, pref):
        if n % d == 0:
            best = d
        d += mult
    return best if best else n


def _dg(a, b, ca, cb):
    return lax.dot_general(a.astype(BF16), b.astype(BF16), (((ca,), (cb,)), ((), ())), preferred_element_type=F32)


@jax.custom_vjp
def _nn(a, b):
    return _dg(a, b, 1, 0)


@jax.custom_vjp
def _nt(a, b):
    return _dg(a, b, 1, 1)


@jax.custom_vjp
def _tn(a, b):
    return _dg(a, b, 0, 0)


_nn.defvjp(lambda a, b: (_nn(a, b), (a, b)), lambda res, g: (_nt(g, res[1]), _tn(res[0], g)))
_nt.defvjp(lambda a, b: (_nt(a, b), (a, b)), lambda res, g: (_nn(g, res[1]), _tn(g, res[0])))
_tn.defvjp(lambda a, b: (_tn(a, b), (a, b)), lambda res, g: (_nt(res[1], g), _nn(res[0], g)))


def _bdg(a, b, ca, cb):
    return lax.dot_general(a.astype(BF16), b.astype(BF16), (((ca,), (cb,)), ((0,), (0,))), preferred_element_type=F32)


@jax.custom_vjp
def _bnn(a, b):
    return _bdg(a, b, 2, 1)


@jax.custom_vjp
def _bnt(a, b):
    return _bdg(a, b, 2, 2)


@jax.custom_vjp
def _btn(a, b):
    return _bdg(a, b, 1, 1)


_bnn.defvjp(lambda a, b: (_bnn(a, b), (a, b)), lambda res, g: (_bnt(g, res[1]), _btn(res[0], g)))
_bnt.defvjp(lambda a, b: (_bnt(a, b), (a, b)), lambda res, g: (_bnn(g, res[1]), _btn(g, res[0])))
_btn.defvjp(lambda a, b: (_btn(a, b), (a, b)), lambda res, g: (_bnt(res[1], g), _bnn(res[0], g)))


def _split2(x):
    hi = x.astype(BF16)
    lo = (x - hi.astype(F32)).astype(BF16)
    return hi, lo


def _hsum_impl(x, e, et):
    eb, etb = e.astype(BF16), et.astype(BF16)
    hi, lo = _split2(x)
    s = jnp.dot(hi, eb, preferred_element_type=F32) + jnp.dot(lo, eb, preferred_element_type=F32)
    shi, slo = _split2(s)
    return jnp.dot(shi, etb, preferred_element_type=F32) + jnp.dot(slo, etb, preferred_element_type=F32)


@jax.custom_vjp
def _hsum(x, e, et):
    return _hsum_impl(x, e, et)


_hsum.defvjp(lambda x, e, et: (_hsum_impl(x, e, et), (e, et)),
             lambda res, g: (_hsum_impl(g, res[0], res[1]), jnp.zeros_like(res[0]), jnp.zeros_like(res[1])))


def _mm(a, b, *, ta=False, tb=False, out_dtype=F32, add=None, b_chip=False, out_chip=False, name):
    if ta:
        kdim, m = a.shape
    else:
        m, kdim = a.shape
    if b_chip:
        n = b.shape[1] if tb else 4 * b.shape[2]
    else:
        n = b.shape[0] if tb else b.shape[1]
    tm, tn, tk = _div(m, 1536, 128), _div(n, 1536, 128), _div(kdim, 1408, 128)
    if b_chip and tb:
        tk = kdim // 4
    if (b_chip and not tb) or out_chip:
        tn = n // 4
    nk = kdim // tk
    ca, cb = (0 if ta else 1), (1 if tb else 0)

    def body(*refs):
        a_ref, b_ref = refs[0], refs[1]
        add_ref = None if add is None else refs[2]
        o_ref = refs[2 if add is None else 3]
        part = lax.dot_general(a_ref[...], b_ref[...], (((ca,), (cb,)), ((), ())), preferred_element_type=F32)

        def finish(r):
            if add_ref is not None:
                r = r + add_ref[...]
            o_ref[...] = r.astype(o_ref.dtype)

        if nk == 1:
            finish(part)
            return
        acc = refs[-1]
        k = pl.program_id(2)

        @pl.when(k == 0)
        def _():
            acc[...] = part

        @pl.when(k > 0)
        def _():
            acc[...] += part

        @pl.when(k == nk - 1)
        def _():
            finish(acc[...])

    a_spec = pl.BlockSpec((tk, tm), lambda i, j, k: (k, i)) if ta else pl.BlockSpec((tm, tk), lambda i, j, k: (i, k))
    if b_chip:
        b_spec = (pl.BlockSpec((None, tn, tk), lambda i, j, k: (k, j, 0)) if tb
                  else pl.BlockSpec((None, tk, tn), lambda i, j, k: (j, k, 0)))
    else:
        b_spec = pl.BlockSpec((tn, tk), lambda i, j, k: (j, k)) if tb else pl.BlockSpec((tk, tn), lambda i, j, k: (k, j))
    in_specs = [a_spec, b_spec]
    args = [a, b]
    if add is not None:
        in_specs.append(pl.BlockSpec((tm, tn), lambda i, j, k: (i, j)))
        args.append(add)
    if out_chip:
        out_spec = pl.BlockSpec((None, tm, tn), lambda i, j, k: (j, i, 0))
        out_shape = jax.ShapeDtypeStruct((4, m, tn), out_dtype)
    else:
        out_spec = pl.BlockSpec((tm, tn), lambda i, j, k: (i, j))
        out_shape = jax.ShapeDtypeStruct((m, n), out_dtype)
    return _pcall(
        body, name=name, grid=(m // tm, n // tn, nk), in_specs=in_specs, out_specs=out_spec, out_shape=out_shape,
        scratch_shapes=[] if nk == 1 else [pltpu.VMEM((tm, tn), F32)],
        compiler_params=_cparams(("parallel", "parallel", "arbitrary")),
    )(*args)


def _row_spec(br, w, cb):
    return pl.BlockSpec((br, w), lambda i: (i, cb))


def _const_spec(shape):
    return pl.BlockSpec(shape, lambda i: (0,) * len(shape))


def _rows_fwd(fn, rows, consts, outs, *, name, br, acc_shape=None):
    s = rows[0][0].shape[0]
    nr, nc = len(rows), len(consts)
    kept = [k for k, o in enumerate(outs) if o is not None]

    def body(*refs):
        xs = [r[...].astype(F32) for r in refs[:nr]]
        cs = [c[...] for c in refs[nr:nr + nc]]
        res = fn(*xs, *cs)
        orefs = refs[nr + nc:]
        for j, k in enumerate(kept):
            orefs[j][...] = res[k].astype(orefs[j].dtype)
        if acc_shape is not None:
            acc_ref = orefs[len(kept)]

            @pl.when(pl.program_id(0) == 0)
            def _():
                acc_ref[...] = jnp.zeros_like(acc_ref)

            acc_ref[...] += res[len(outs)]

    in_specs = [_row_spec(br, w, cb) for (_, w, cb) in rows] + [_const_spec(c.shape) for c in consts]
    out_specs = [_row_spec(br, outs[k][0], 0) for k in kept]
    out_shape = [jax.ShapeDtypeStruct((s, outs[k][0]), outs[k][1]) for k in kept]
    if acc_shape is not None:
        out_specs.append(_const_spec(acc_shape))
        out_shape.append(jax.ShapeDtypeStruct(acc_shape, F32))
    return _pcall(
        body, name=name, grid=(s // br,), in_specs=in_specs, out_specs=out_specs, out_shape=out_shape,
        compiler_params=_cparams(("arbitrary",)),
    )(*[r[0] for r in rows], *consts)


def _rows_bwd(fn, rows, consts, cots, *, wrt_rows, wrt_consts, drow_dtypes, name, br, unit_cot=False):
    s = rows[0][0].shape[0]
    nr, nc = len(rows), len(consts)
    flat_cots = [c for lst in cots for c in lst]
    ncot = len(flat_cots)

    def body(*refs):
        xs = [r[...].astype(F32) for r in refs[:nr]]
        cs = [c[...] for c in refs[nr:nr + nc]]
        cvals = [c[...].astype(F32) for c in refs[nr + nc:nr + nc + ncot]]
        orefs = refs[nr + nc + ncot:]

        def g(*d):
            xs2, cs2 = list(xs), list(cs)
            for j, k in enumerate(wrt_rows):
                xs2[k] = d[j]
            for j, k in enumerate(wrt_consts):
                cs2[k] = d[len(wrt_rows) + j]
            return tuple(fn(*xs2, *cs2))

        prim = [xs[k] for k in wrt_rows] + [cs[k] for k in wrt_consts]
        outs, vjp = jax.vjp(g, *prim)
        ct = []
        pos = 0
        for o, lst in zip(outs, cots):
            if unit_cot:
                ct.append(jnp.ones_like(o))
                continue
            acc = jnp.zeros_like(o)
            for _ in lst:
                acc = acc + cvals[pos]
                pos += 1
            ct.append(acc)
        grads = vjp(tuple(ct))
        for j in range(len(wrt_rows)):
            orefs[j][...] = grads[j].astype(orefs[j].dtype)

        @pl.when(pl.program_id(0) == 0)
        def _():
            for j in range(len(wrt_consts)):
                oref = orefs[len(wrt_rows) + j]
                oref[...] = jnp.zeros_like(oref)

        for j in range(len(wrt_consts)):
            orefs[len(wrt_rows) + j][...] += grads[len(wrt_rows) + j]

    in_specs = ([_row_spec(br, w, cb) for (_, w, cb) in rows] + [_const_spec(c.shape) for c in consts]
                + [_row_spec(br, w, cb) for (_, w, cb) in flat_cots])
    out_specs = [_row_spec(br, rows[k][1], 0) for k in wrt_rows] + [_const_spec(consts[k].shape) for k in wrt_consts]
    out_shape = ([jax.ShapeDtypeStruct((s, rows[k][1]), dt) for k, dt in zip(wrt_rows, drow_dtypes)]
                 + [jax.ShapeDtypeStruct(consts[k].shape, F32) for k in wrt_consts])
    return _pcall(
        body, name=name, grid=(s // br,), in_specs=in_specs, out_specs=out_specs, out_shape=out_shape,
        compiler_params=_cparams(("arbitrary",)),
    )(*[r[0] for r in rows], *consts, *[c[0] for c in flat_cots])


def _rms(x, w):
    return x * lax.rsqrt(jnp.mean(x * x, axis=-1, keepdims=True) + RMS_EPS) * w


def _softplus(x):
    return jnp.maximum(x, 0.0) + jnp.log(1.0 + jnp.exp(-jnp.abs(x)))


def _f_pre(x, nw, sc, sh):
    return _rms(x, nw) * (1.0 + sc) + sh, x


def _f_pre2(x, o, gt, nw, sc, sh):
    x1 = x + gt * o
    return x1, _rms(x1, nw) * (1.0 + sc) + sh


def _f_fin(x1, f, tgt, gt, nfw):
    y = _rms(x1 + gt * f, nfw)
    return (0.5 * jnp.mean(jnp.square(y - tgt), axis=-1, keepdims=True),)


def _f_comb(o1, o2, o3, l1, l2, l3):
    m = lax.stop_gradient(jnp.maximum(jnp.maximum(l1, l2), l3))
    e1, e2, e3 = jnp.exp(l1 - m), jnp.exp(l2 - m), jnp.exp(l3 - m)
    return ((e1 * o1 + e2 * o2 + e3 * o3) / (e1 + e2 + e3),)


def _f_rwpre(zs, w0, a0, k_k, k_a, wl, e, et):
    r, k, v, zl = zs[:, 0:D], zs[:, D:2 * D], zs[:, 2 * D:3 * D], zs[:, 3 * D:N_RWP]
    lane = lax.broadcasted_iota(jnp.int32, zl.shape, 1)
    t = jnp.where(lane < 64, jnp.tanh(zl), jnp.where(lane < 128, zl, jnp.where(lane < 288, jax.nn.sigmoid(zl), 0.0)))
    lo = _nn(t, wl)
    w_log = -_softplus(-(w0 + lo[:, 0:D])) - 0.5
    lw = -jnp.exp(w_log)
    a = jax.nn.sigmoid(a0 + lo[:, D:2 * D])
    g = lo[:, 2 * D:3 * D]
    k_mod = k * (1.0 + (a - 1.0) * k_a)
    kk = k * k_k
    kk = kk / jnp.maximum(jnp.sqrt(_hsum(kk * kk, e, et)), 1e-12)
    return r, lw, k_mod, v, -kk, kk * a, g


def _f_rwpost(y, r, v, k_mod, g, lnx_w, lnx_b, r_k, e, et):
    mean = _hsum(y, e, et) * (1.0 / 64)
    yc = y - mean
    var = _hsum(yc * yc, e, et) * (1.0 / 64)
    yn = yc * lax.rsqrt(var + GN_EPS) * lnx_w + lnx_b
    bonus = _hsum(r * k_mod * r_k, e, et) * v
    return ((yn + bonus) * g,)


def _f_mix(gia, gir, ya, yr, bga, bgr):
    return (jax.nn.sigmoid(gia + bga) * ya + jax.nn.sigmoid(gir + bgr) * yr,)


def _f_adamw(w, g, m, v):
    m = ADAM_B1 * m + (1.0 - ADAM_B1) * g
    v = ADAM_B2 * v + (1.0 - ADAM_B2) * jnp.square(g)
    m_hat = m / (1.0 - ADAM_B1 ** ADAM_STEP)
    v_hat = v / (1.0 - ADAM_B2 ** ADAM_STEP)
    return -ADAM_LR * (m_hat / (jnp.sqrt(v_hat) + ADAM_EPS) + ADAM_WD * w), m, v


def _down(x, k):
    row = lax.broadcasted_iota(jnp.int32, x.shape, 0)
    return jnp.where(row < k, 0.0, pltpu.roll(x, k, 0))


def _up(x, k):
    n = x.shape[0]
    row = lax.broadcasted_iota(jnp.int32, x.shape, 0)
    return jnp.where(row >= n - k, 0.0, pltpu.roll(x, n - k, 0))


def _col_spec(s, w, off=0):
    return pl.BlockSpec((s, w), lambda j: (0, j + off))


def _shift_fwd(z, mu):
    s, n = z.shape

    def body(z_ref, mu_ref, o_ref):
        zz = z_ref[...]
        o_ref[...] = zz + (_down(zz, 1) - zz) * mu_ref[...]

    return _pcall(
        body, name="shift_fwd", grid=(n // 128,), in_specs=[_col_spec(s, 128), _col_spec(1, 128)],
        out_specs=_col_spec(s, 128), out_shape=jax.ShapeDtypeStruct((s, n), F32),
        compiler_params=_cparams(("parallel",)),
    )(z, mu)


def _shift_bwd(z, mu, dzs):
    s, n = z.shape

    def body(z_ref, mu_ref, d_ref, dz_ref, dmu_ref):
        zz, d, m = z_ref[...], d_ref[...], mu_ref[...]
        dm = d * m
        dz_ref[...] = (d - dm + _up(dm, 1)).astype(dz_ref.dtype)
        dmu_ref[...] = jnp.sum(d * (_down(zz, 1) - zz), axis=0, keepdims=True)

    return _pcall(
        body, name="shift_bwd", grid=(n // 128,), in_specs=[_col_spec(s, 128), _col_spec(1, 128), _col_spec(s, 128)],
        out_specs=[_col_spec(s, 128), _col_spec(1, 128)],
        out_shape=[jax.ShapeDtypeStruct((s, n), BF16), jax.ShapeDtypeStruct((1, n), F32)],
        compiler_params=_cparams(("parallel",)),
    )(z, mu, dzs)


def _conv3(x, w_ref, b_ref):
    return b_ref[...] + w_ref[0:1, :] * _down(x, 2) + w_ref[1:2, :] * _down(x, 1) + w_ref[2:3, :] * x


def _conv_fwd(u, cw, cb):
    s = u.shape[0]
    nb = D_FF // 128

    def body(ug_ref, uv_ref, wg_ref, wv_ref, bg_ref, bv_ref, o_ref):
        gate = _conv3(ug_ref[...], wg_ref, bg_ref)
        val = _conv3(uv_ref[...], wv_ref, bv_ref)
        o_ref[...] = (gate * jax.nn.sigmoid(gate) * val).astype(o_ref.dtype)

    return _pcall(
        body, name="conv_fwd", grid=(nb,),
        in_specs=[_col_spec(s, 128), _col_spec(s, 128, nb), _col_spec(3, 128), _col_spec(3, 128, nb),
                  _col_spec(1, 128), _col_spec(1, 128, nb)],
        out_specs=_col_spec(s, 128), out_shape=jax.ShapeDtypeStruct((s, D_FF), BF16),
        compiler_params=_cparams(("parallel",)),
    )(u, u, cw, cw, cb, cb)


def _conv_bwd(u, cw, cb, dact):
    s = u.shape[0]
    nb = D_FF // 128

    def half(x, d, w_ref, du_ref, dw_ref, db_ref):
        x1, x2 = _down(x, 1), _down(x, 2)
        du_ref[...] = (w_ref[2:3, :] * d + w_ref[1:2, :] * _up(d, 1) + w_ref[0:1, :] * _up(d, 2)).astype(du_ref.dtype)
        dw_ref[0:1, :] = jnp.sum(d * x2, axis=0, keepdims=True)
        dw_ref[1:2, :] = jnp.sum(d * x1, axis=0, keepdims=True)
        dw_ref[2:3, :] = jnp.sum(d * x, axis=0, keepdims=True)
        db_ref[...] = jnp.sum(d, axis=0, keepdims=True)

    def body(ug_ref, uv_ref, wg_ref, wv_ref, bg_ref, bv_ref, da_ref,
             dug_ref, duv_ref, dwg_ref, dwv_ref, dbg_ref, dbv_ref):
        ug, uv, da = ug_ref[...], uv_ref[...], da_ref[...]
        gate = _conv3(ug, wg_ref, bg_ref)
        val = _conv3(uv, wv_ref, bv_ref)
        sg = jax.nn.sigmoid(gate)
        dgate = da * val * sg * (1.0 + gate * (1.0 - sg))
        dval = da * gate * sg
        half(ug, dgate, wg_ref, dug_ref, dwg_ref, dbg_ref)
        half(uv, dval, wv_ref, duv_ref, dwv_ref, dbv_ref)

    dug, duv, dwg, dwv, dbg, dbv = _pcall(
        body, name="conv_bwd", grid=(nb,),
        in_specs=[_col_spec(s, 128), _col_spec(s, 128, nb), _col_spec(3, 128), _col_spec(3, 128, nb),
                  _col_spec(1, 128), _col_spec(1, 128, nb), _col_spec(s, 128)],
        out_specs=[_col_spec(s, 128), _col_spec(s, 128), _col_spec(3, 128), _col_spec(3, 128),
                   _col_spec(1, 128), _col_spec(1, 128)],
        out_shape=[jax.ShapeDtypeStruct((s, D_FF), BF16), jax.ShapeDtypeStruct((s, D_FF), BF16),
                   jax.ShapeDtypeStruct((3, D_FF), F32), jax.ShapeDtypeStruct((3, D_FF), F32),
                   jax.ShapeDtypeStruct((1, D_FF), F32), jax.ShapeDtypeStruct((1, D_FF), F32)],
        compiler_params=_cparams(("parallel",)),
    )(u, u, cw, cw, cb, cb, dact)
    return (jnp.concatenate([dug, duv], axis=1), jnp.concatenate([dwg, dwv], axis=1),
            jnp.concatenate([dbg, dbv], axis=1))


ATT_BATCH = 4


def _att_batch(q, kp, kc, vp, vc, first):
    ma = lax.broadcasted_iota(jnp.int32, (1, ATT_BLOCK, 128), 2) < 64
    qs = jnp.concatenate([jnp.where(ma, q, 0.0), jnp.where(ma, 0.0, q)], axis=1)
    qi = lax.broadcasted_iota(jnp.int32, (1, 2 * ATT_BLOCK, ATT_BLOCK), 1) & (ATT_BLOCK - 1)
    kj = lax.broadcasted_iota(jnp.int32, (1, 2 * ATT_BLOCK, ATT_BLOCK), 2)
    okp = kj >= qi + jnp.where(first, 2 * ATT_BLOCK, 0)
    okc = kj <= qi
    sp = jnp.where(okp, _bnt(qs, kp) * 0.125, NEG)
    sc = jnp.where(okc, _bnt(qs, kc) * 0.125, NEG)
    m = lax.stop_gradient(jnp.maximum(jnp.max(sp, axis=-1, keepdims=True), jnp.max(sc, axis=-1, keepdims=True)))
    pp, pc = jnp.exp(sp - m), jnp.exp(sc - m)
    den = jnp.sum(pp, axis=-1, keepdims=True) + jnp.sum(pc, axis=-1, keepdims=True)
    o_s = (_bnn(pp, vp) + _bnn(pc, vc)) / den
    l_s = jnp.broadcast_to(m + jnp.log(den), o_s.shape)
    return (jnp.where(ma, o_s[:, :ATT_BLOCK], o_s[:, ATT_BLOCK:]), jnp.where(ma, l_s[:, :ATT_BLOCK], l_s[:, ATT_BLOCK:]))


def _att_pairs_per_step(dil):
    return ATT_BATCH if dil == 1 else 1


def _att_specs(g, dil):
    rows, pp = ATT_BLOCK * dil, _att_pairs_per_step(dil)

    def cur(slot):
        return pl.BlockSpec((rows, 128 * pp), lambda n, p: (n, (g * 3 + slot) * (4 // pp) + p))

    def prev(slot):
        return pl.BlockSpec((rows, 128 * pp), lambda n, p: (jnp.maximum(n - 1, 0), (g * 3 + slot) * (4 // pp) + p))

    return [cur(0), prev(1), cur(1), prev(2), cur(2)]


def _att_out_spec(dil):
    return pl.BlockSpec((ATT_BLOCK * dil, 128 * _att_pairs_per_step(dil)), lambda n, p: (n, p))


def _att_grid(s, dil):
    return (s // (ATT_BLOCK * dil), 4 // _att_pairs_per_step(dil))


def _att_windows(i, dil):
    if dil == 1:
        return [(pl.ds(0, ATT_BLOCK), pl.ds(128 * j, 128)) for j in range(ATT_BATCH)]
    return [(pl.ds(i * ATT_BATCH + j, ATT_BLOCK, stride=dil), pl.ds(0, 128)) for j in range(ATT_BATCH)]


def _att_fwd(att_in, g, dil):
    s = att_in.shape[0]

    def body(q_ref, kp_ref, kc_ref, vp_ref, vc_ref, o_ref, l_ref):
        first = pl.program_id(0) == 0

        def one(i, carry):
            win = _att_windows(i, dil)
            vals = [jnp.stack([ref[w] for w in win]) for ref in (q_ref, kp_ref, kc_ref, vp_ref, vc_ref)]
            o, l = _att_batch(*vals, first)
            for j, w in enumerate(win):
                o_ref[w] = o[j]
                l_ref[w] = l[j]
            return carry

        lax.fori_loop(0, max(1, dil // ATT_BATCH), one, 0)

    return _pcall(
        body, name=f"att_fwd{g}", grid=_att_grid(s, dil), in_specs=_att_specs(g, dil),
        out_specs=[_att_out_spec(dil)] * 2, out_shape=[jax.ShapeDtypeStruct((s, ATT_WIDTH), F32)] * 2,
        compiler_params=_cparams(("parallel", "parallel")),
    )(att_in, att_in, att_in, att_in, att_in)


def _att_bwd(att_in, g, dil, do, dl):
    s = att_in.shape[0]
    nb = s // (ATT_BLOCK * dil)

    def body(q_ref, kp_ref, kc_ref, vp_ref, vc_ref, do_ref, dl_ref, dq_ref, dkp_ref, dkc_ref, dvp_ref, dvc_ref):
        first = pl.program_id(0) == 0

        def one(i, carry):
            win = _att_windows(i, dil)
            vals = [jnp.stack([ref[w] for w in win]) for ref in (q_ref, kp_ref, kc_ref, vp_ref, vc_ref)]
            _, vjp = jax.vjp(lambda *a: _att_batch(*a, first), *vals)
            grads = vjp((jnp.stack([do_ref[w] for w in win]), jnp.stack([dl_ref[w] for w in win])))
            for ref, gr in zip((dq_ref, dkp_ref, dkc_ref, dvp_ref, dvc_ref), grads):
                for j, w in enumerate(win):
                    ref[w] = gr[j]
            return carry

        lax.fori_loop(0, max(1, dil // ATT_BATCH), one, 0)

    dq, dkp, dkc, dvp, dvc = _pcall(
        body, name=f"att_bwd{g}", grid=_att_grid(s, dil), in_specs=_att_specs(g, dil) + [_att_out_spec(dil)] * 2,
        out_specs=[_att_out_spec(dil)] * 5, out_shape=[jax.ShapeDtypeStruct((s, ATT_WIDTH), F32)] * 5,
        compiler_params=_cparams(("parallel", "parallel")),
    )(att_in, att_in, att_in, att_in, att_in, do, dl)

    def cbody(dq_ref, dkc_ref, dkn_ref, dvc_ref, dvn_ref, oq_ref, ok_ref, ov_ref):
        has_next = pl.program_id(0) + 1 < nb
        oq_ref[...] = dq_ref[...].astype(BF16)
        ok_ref[...] = (dkc_ref[...] + jnp.where(has_next, dkn_ref[...], 0.0)).astype(BF16)
        ov_ref[...] = (dvc_ref[...] + jnp.where(has_next, dvn_ref[...], 0.0)).astype(BF16)

    cur = pl.BlockSpec((ATT_BLOCK * dil, 128), lambda n, p: (n, p))
    nxt = pl.BlockSpec((ATT_BLOCK * dil, 128), lambda n, p: (jnp.minimum(n + 1, nb - 1), p))
    return _pcall(
        cbody, name=f"att_bwd_sum{g}", grid=(nb, 4), in_specs=[cur, cur, nxt, cur, nxt], out_specs=[cur] * 3,
        out_shape=[jax.ShapeDtypeStruct((s, ATT_WIDTH), BF16)] * 3,
        compiler_params=_cparams(("parallel", "parallel")),
    )(dq, dkc, dkp, dvc, dvp)


def _scan_chunk(r, lw, k, v, a, b, s0):
    c = SCAN_CHUNK
    p = s0.shape[0]
    ri = lax.broadcasted_iota(jnp.int32, (c, c), 0)
    ci = lax.broadcasted_iota(jnp.int32, (c, c), 1)
    cum = jnp.dot((ci <= ri).astype(F32), lw, precision=HI, preferred_element_type=F32)
    tot = jnp.sum(lw, axis=0, keepdims=True)
    ma = (lax.broadcasted_iota(jnp.int32, (c, 128 * p), 1) & 127) < 64

    def pairs(x):
        return jnp.concatenate([x[None, :, 128 * j:128 * (j + 1)] for j in range(p)], axis=0)

    def stack(x):
        return jnp.concatenate([pairs(jnp.where(ma, x, 0.0)), pairs(jnp.where(ma, 0.0, x))], axis=1)

    einv, eend = jnp.exp(-cum), jnp.exp(tot - cum)
    ra, aa = stack(r * jnp.exp(cum)), stack(a * jnp.exp(cum - lw))
    bi, ki, be, ke, vs = stack(b * einv), stack(k * einv), stack(b * eend), stack(k * eend), stack(v)
    r2 = lax.broadcasted_iota(jnp.int32, (1, 2 * c, 2 * c), 1)
    c2 = lax.broadcasted_iota(jnp.int32, (1, 2 * c, 2 * c), 2)
    same = (r2 >= c) == (c2 >= c)
    strict = jnp.logical_and(same, c2 < r2)
    incl = jnp.logical_and(same, c2 <= r2)
    s0 = jnp.where(same, s0, 0.0)
    a_ab = jnp.where(strict, _bnt(aa, bi), 0.0)
    a_ak = jnp.where(strict, _bnt(aa, ki), 0.0)
    a_rb = jnp.where(incl, _bnt(ra, bi), 0.0)
    a_rk = jnp.where(incl, _bnt(ra, ki), 0.0)
    t = jnp.where(r2 == c2, 1.0, 0.0) + a_ab
    pw = a_ab
    for _ in range(5):
        pw = _bnn(pw, pw)
        t = t + _bnn(t, pw)
    u = _bnn(t, _bnt(aa, s0) + _bnn(a_ak, vs))
    ys = _bnt(ra, s0) + _bnn(a_rb, u) + _bnn(a_rk, vs)
    s1 = s0 * pairs(jnp.exp(tot)) + _btn(u, be) + _btn(vs, ke)
    y3 = ys[:, :c] + ys[:, c:]
    return jnp.concatenate([y3[j] for j in range(p)], axis=1), s1


def _scan_specs(rev, n):
    w = 128 * SCAN_PAIRS

    def at(i):
        return n - 1 - i if rev else i

    def cm(cb):
        return pl.BlockSpec((SCAN_CHUNK, w), lambda p, i: (at(i), cb * (D // w) + p))

    return cm, pl.BlockSpec((1, SCAN_PAIRS, 128, 128), lambda p, i: (at(i), p, 0, 0))


def _scan_fwd(zs, lw, km, aa, bb):
    s = zs.shape[0]
    n = s // SCAN_CHUNK
    cm, st = _scan_specs(False, n)

    def body(r_ref, lw_ref, k_ref, v_ref, a_ref, b_ref, y_ref, s0_ref, state):
        @pl.when(pl.program_id(1) == 0)
        def _():
            state[...] = jnp.zeros_like(state)

        s0 = state[...]
        s0_ref[0] = s0
        y, s1 = _scan_chunk(*[ref[...] for ref in (r_ref, lw_ref, k_ref, v_ref, a_ref, b_ref)], s0)
        y_ref[...] = y
        state[...] = s1

    return _pcall(
        body, name="scan_fwd", grid=(8 // SCAN_PAIRS, n), in_specs=[cm(0), cm(0), cm(0), cm(2), cm(0), cm(0)],
        out_specs=[cm(0), st],
        out_shape=[jax.ShapeDtypeStruct((s, D), F32), jax.ShapeDtypeStruct((n, 8, 128, 128), F32)],
        scratch_shapes=[pltpu.VMEM((SCAN_PAIRS, 128, 128), F32)],
        compiler_params=_cparams(("parallel", "arbitrary")),
    )(zs, lw, km, zs, aa, bb)


def _scan_bwd(zs, lw, km, aa, bb, s0s, dy):
    s = zs.shape[0]
    n = s // SCAN_CHUNK
    cm, st = _scan_specs(True, n)

    def body(r_ref, lw_ref, k_ref, v_ref, a_ref, b_ref, s0_ref, dy_ref,
             dr_ref, dlw_ref, dk_ref, dv_ref, da_ref, db_ref, dstate):
        @pl.when(pl.program_id(1) == 0)
        def _():
            dstate[...] = jnp.zeros_like(dstate)

        prim = [ref[...] for ref in (r_ref, lw_ref, k_ref, v_ref, a_ref, b_ref)] + [s0_ref[0]]
        _, vjp = jax.vjp(_scan_chunk, *prim)
        grads = vjp((dy_ref[...], dstate[...]))
        for ref, gr in zip((dr_ref, dlw_ref, dk_ref, dv_ref, da_ref, db_ref), grads[:6]):
            ref[...] = gr
        dstate[...] = grads[6]

    return _pcall(
        body, name="scan_bwd", grid=(8 // SCAN_PAIRS, n),
        in_specs=[cm(0), cm(0), cm(0), cm(2), cm(0), cm(0), st, cm(0)],
        out_specs=[cm(0)] * 6, out_shape=[jax.ShapeDtypeStruct((s, D), F32)] * 6,
        scratch_shapes=[pltpu.VMEM((SCAN_PAIRS, 128, 128), F32)],
        compiler_params=_cparams(("parallel", "arbitrary")),
    )(zs, lw, km, zs, aa, bb, s0s, dy)


_HBM = pl.BlockSpec(memory_space=pltpu.HBM)


def _me():
    return lax.axis_index("x"), lax.axis_index("y"), lax.axis_index("c")


def _allgather8(src, name):
    def body(src_ref, out_ref, ssem, rsem, lsem):
        x, y, c = _me()
        me = 4 * x + 2 * y + c
        local = pltpu.make_async_copy(src_ref, out_ref.at[me], lsem)
        local.start()
        peers = []
        for k in range(1, 8):
            peers.append(((1 - x) if k & 4 else x, (1 - y) if k & 2 else y, (1 - c) if k & 1 else c))
        sends = []
        for k, peer in enumerate(peers):
            cp = pltpu.make_async_remote_copy(src_ref, out_ref.at[me], ssem.at[k], rsem.at[k], device_id=peer,
                                              device_id_type=MESH)
            cp.start()
            sends.append(cp)
        for k, (px, py, pc) in enumerate(peers):
            pltpu.make_async_remote_copy(src_ref, out_ref.at[4 * px + 2 * py + pc], ssem.at[k], rsem.at[k],
                                         device_id=(px, py, pc), device_id_type=MESH).wait_recv()
        for cp in sends:
            cp.wait_send()
        local.wait()

    return _pcall(
        body, name=name, in_specs=[_HBM], out_specs=_HBM, out_shape=jax.ShapeDtypeStruct((8,) + src.shape, src.dtype),
        scratch_shapes=[pltpu.SemaphoreType.DMA((7,)), pltpu.SemaphoreType.DMA((7,)), pltpu.SemaphoreType.DMA],
    )(src)


def _other_chips(x, y):
    return [(1 - x, y), (x, 1 - y), (1 - x, 1 - y)]


def _remote(src, dst, ssem, rsem, to):
    return pltpu.make_async_remote_copy(src, dst, ssem, rsem, device_id=to, device_id_type=MESH)


def _gather_weights(shards, name):
    n = len(shards)

    def body(*refs):
        srcs, outs = refs[:n], refs[n:2 * n]
        ssem, rsem, lsem = refs[2 * n:]
        x, y, c = _me()
        me = 2 * x + y
        chips = _other_chips(x, y)

        def half(a, core):
            h = shards[a].shape[0] // 2
            return pl.ds(core * h, h)

        own = [pltpu.make_async_copy(srcs[a], outs[a].at[me], lsem.at[a]) for a in range(n)]
        for cp in own:
            cp.start()
        sends = []
        for a in range(n):
            for k, (px, py) in enumerate(chips):
                sends.append(_remote(srcs[a].at[half(a, c)], outs[a].at[me, half(a, c)], ssem.at[6 * a + k],
                                     rsem.at[6 * a + k], (px, py, c)))
                sends[-1].start()
        for a in range(n):
            for k, (px, py) in enumerate(chips):
                got = outs[a].at[2 * px + py, half(a, c)]
                _remote(got, got, ssem.at[6 * a + k], rsem.at[6 * a + k], (px, py, c)).wait_recv()
                sends.append(_remote(got, got, ssem.at[6 * a + 3 + k], rsem.at[6 * a + 3 + k], (x, y, 1 - c)))
                sends[-1].start()
        for a in range(n):
            for k, (px, py) in enumerate(chips):
                dst = outs[a].at[2 * px + py, half(a, 1 - c)]
                _remote(dst, dst, ssem.at[6 * a + 3 + k], rsem.at[6 * a + 3 + k], (x, y, 1 - c)).wait_recv()
        for cp in sends:
            cp.wait_send()
        for cp in own:
            cp.wait()

    return _pcall(
        body, name=name, in_specs=[_HBM] * n, out_specs=[_HBM] * n,
        out_shape=[jax.ShapeDtypeStruct((4,) + t.shape, t.dtype) for t in shards],
        scratch_shapes=[pltpu.SemaphoreType.DMA((6 * n,)), pltpu.SemaphoreType.DMA((6 * n,)),
                        pltpu.SemaphoreType.DMA((n,))],
    )(*shards)


def _reduce_sibling(grads, name):
    n = len(grads)

    def body(*refs):
        srcs, keeps, recvs = refs[:n], refs[n:2 * n], refs[2 * n:3 * n]
        ssem, rsem, lsem = refs[3 * n:]
        x, y, c = _me()
        copies = []
        for a in range(n):
            h = grads[a].shape[1] // 2
            copies.append(pltpu.make_async_copy(srcs[a].at[:, pl.ds(c * h, h)], keeps[a], lsem.at[a]))
            copies.append(_remote(srcs[a].at[:, pl.ds((1 - c) * h, h)], recvs[a], ssem.at[a], rsem.at[a], (x, y, 1 - c)))
        for cp in copies:
            cp.start()
        for cp in copies:
            cp.wait()

    halves = [jax.ShapeDtypeStruct((4, t.shape[1] // 2, t.shape[2]), t.dtype) for t in grads]
    res = _pcall(
        body, name=name, in_specs=[_HBM] * n, out_specs=[_HBM] * (2 * n), out_shape=halves + halves,
        scratch_shapes=[pltpu.SemaphoreType.DMA((n,)), pltpu.SemaphoreType.DMA((n,)), pltpu.SemaphoreType.DMA((n,))],
    )(*grads)
    return res[:n], res[n:]


def _reduce_chips(parts, name):
    n = len(parts)

    def body(*refs):
        srcs, outs = refs[:n], refs[n:2 * n]
        ssem, rsem, lsem = refs[2 * n:]
        x, y, c = _me()
        me = 2 * x + y
        chips = _other_chips(x, y)
        copies = [pltpu.make_async_copy(srcs[a].at[me], outs[a].at[me], lsem.at[a]) for a in range(n)]
        for a in range(n):
            for k, (px, py) in enumerate(chips):
                copies.append(_remote(srcs[a].at[2 * px + py], outs[a].at[me], ssem.at[3 * a + k], rsem.at[3 * a + k],
                                      (px, py, c)))
        for cp in copies:
            cp.start()
        for a in range(n):
            for k, (px, py) in enumerate(chips):
                dst = outs[a].at[2 * px + py]
                _remote(dst, dst, ssem.at[3 * a + k], rsem.at[3 * a + k], (px, py, c)).wait_recv()
        for cp in copies[:n]:
            cp.wait()
        for cp in copies[n:]:
            cp.wait_send()

    return _pcall(
        body, name=name, in_specs=[_HBM] * n, out_specs=[_HBM] * n,
        out_shape=[jax.ShapeDtypeStruct(t.shape, t.dtype) for t in parts],
        scratch_shapes=[pltpu.SemaphoreType.DMA((3 * n,)), pltpu.SemaphoreType.DMA((3 * n,)),
                        pltpu.SemaphoreType.DMA((n,))],
    )(*parts)


def _reduce_finish(reds, name):
    n = len(reds)

    def body(*refs):
        srcs, outs = refs[:n], refs[n:2 * n]
        ssem, rsem, lsem = refs[2 * n:]
        x, y, c = _me()
        copies = []
        for a in range(n):
            h = reds[a].shape[0]
            copies.append(pltpu.make_async_copy(srcs[a], outs[a].at[pl.ds(c * h, h)], lsem.at[a]))
            copies.append(_remote(srcs[a], outs[a].at[pl.ds(c * h, h)], ssem.at[a], rsem.at[a], (x, y, 1 - c)))
        for cp in copies:
            cp.start()
        for a in range(n):
            h = reds[a].shape[0]
            dst = outs[a].at[pl.ds((1 - c) * h, h)]
            _remote(dst, dst, ssem.at[a], rsem.at[a], (x, y, 1 - c)).wait_recv()
        for a in range(n):
            copies[2 * a].wait()
            copies[2 * a + 1].wait_send()

    return _pcall(
        body, name=name, in_specs=[_HBM] * n, out_specs=[_HBM] * n,
        out_shape=[jax.ShapeDtypeStruct((2 * t.shape[0], t.shape[1]), t.dtype) for t in reds],
        scratch_shapes=[pltpu.SemaphoreType.DMA((n,)), pltpu.SemaphoreType.DMA((n,)), pltpu.SemaphoreType.DMA((n,))],
    )(*reds)


def _slabs(fn, ins, out_lead, out_dtype, name):
    p, r, c = ins[0].shape
    br = _div(r, max(16, (1 << 19) // (p * c)), 16)
    spec3 = pl.BlockSpec((p, br, c), lambda i: (0, i, 0))

    def body(*refs):
        refs[-1][...] = fn(*[t[...].astype(F32) for t in refs[:-1]]).astype(out_dtype)

    return _pcall(
        body, name=name, grid=(r // br,), in_specs=[spec3] * len(ins),
        out_specs=spec3 if out_lead else pl.BlockSpec((br, c), lambda i: (i, 0)),
        out_shape=jax.ShapeDtypeStruct((p, r, c) if out_lead else (r, c), out_dtype),
        compiler_params=_cparams(("parallel",)),
    )(*ins)


def _ada_fwd(c_all, w, b):
    def body(c_ref, w_ref, b_ref, o_ref):
        o_ref[...] = jnp.dot(c_ref[...], w_ref[...], precision=HI, preferred_element_type=F32) + b_ref[...]

    return _pcall(body, name="ada_fwd", out_shape=jax.ShapeDtypeStruct((c_all.shape[0], w.shape[1]), F32),
                  compiler_params=pltpu.CompilerParams(vmem_limit_bytes=VMEM_LIMIT))(c_all, w, b)


def _ada_bwd(c_all_t, d):
    def body(c_ref, d_ref, o_ref):
        o_ref[...] = jnp.dot(c_ref[...], d_ref[...], precision=HI, preferred_element_type=F32)

    return _pcall(body, name="ada_bwd", out_shape=jax.ShapeDtypeStruct((c_all_t.shape[0], d.shape[1]), F32),
                  compiler_params=pltpu.CompilerParams(vmem_limit_bytes=VMEM_LIMIT))(c_all_t, d)


def _sum_lead(x, name):
    p, r, n = x.shape
    br = _div(r, 512, 8)

    def body(x_ref, o_ref):
        acc = x_ref[0]
        for j in range(1, p):
            acc = acc + x_ref[j]
        o_ref[...] = acc

    return _pcall(
        body, name=name, grid=(r // br,), in_specs=[pl.BlockSpec((p, br, n), lambda i: (0, i, 0))],
        out_specs=pl.BlockSpec((br, n), lambda i: (i, 0)), out_shape=jax.ShapeDtypeStruct((r, n), F32),
        compiler_params=_cparams(("parallel",)),
    )(x)


def _adamw(w, g, m, v, name):
    shape = w.shape
    cols = shape[-1]
    w2, g2, m2, v2 = [t.reshape(-1, cols) for t in (w, g, m, v)]
    rows = w2.shape[0]
    br = _div(rows, max(8, (1 << 19) // cols // 8 * 8), 8)
    outs = _rows_fwd(_f_adamw, [(t, cols, 0) for t in (w2, g2, m2, v2)], [], [(cols, F32)] * 3, name=name, br=br)
    return [o.reshape(shape) for o in outs]


_BIG = (("w_in", 1), ("w_up", 1), ("w_down", 0), ("w_o", 0), ("w_rwkv_out", 0), ("w_att_out", 1), ("w2", 1), ("a2", 1),
        ("g2", 1))


def _cols_joined(t):
    return jnp.concatenate([t[j] for j in range(4)], axis=1)


def _cols_split(t):
    n = t.shape[1] // 4
    return jnp.stack([t[:, j * n:(j + 1) * n] for j in range(4)])


def _rows_joined(t):
    return t.reshape(4 * t.shape[1], t.shape[2])


def _rows_split(t):
    return t.reshape(4, t.shape[0] // 4, t.shape[1])


def _local_step(x, tgt, ada, wts):
    s = x.shape[0]
    sh1, sc1, gt1, sh2, sc2, gt2 = ada
    br = 256
    grp = lax.broadcasted_iota(jnp.int32, (D, 128), 0) // 64 == lax.broadcasted_iota(jnp.int32, (D, 128), 1)
    e = grp.astype(F32)
    et = e.T
    w_in = _cols_joined(wts["w_in"])
    w_att = w_in[:, :N_ATT]
    w_rw = jnp.pad(w_in[:, N_ATT:N_ATT + N_RW], ((0, 0), (0, N_RWP - N_RW)))
    w_gate = w_in[:, N_ATT + N_RW:]
    w_up, w_ao = wts["w_up"], wts["w_att_out"]
    w_down, w_o, w_ro = _rows_joined(wts["w_down"]), _rows_joined(wts["w_o"]), _rows_joined(wts["w_rwkv_out"])
    mu = jnp.pad(wts["mu_shift"], ((0, 0), (0, N_RWP - N_RW)))
    wl = jnp.zeros((N_LORA, 3 * D), F32)
    wl = wl.at[0:64, 0:D].set(_cols_joined(wts["w2"]).astype(F32))
    wl = wl.at[64:128, D:2 * D].set(_cols_joined(wts["a2"]).astype(F32))
    wl = wl.at[128:288, 2 * D:3 * D].set(_cols_joined(wts["g2"]).astype(F32))
    bga, bgr = wts["b_gate"][:, :D], wts["b_gate"][:, D:]
    rk = wts["r_k"]

    pre1_c = [wts["norm1_w"], sc1, sh1]
    (h1,) = _rows_fwd(_f_pre, [(x, D, 0)], pre1_c, [(D, BF16), None], name="pre1_fwd", br=br)
    att_in = _mm(h1, w_att, name="mm_att_in")
    z = _mm(h1, w_rw, name="mm_rw_in")
    gate_in = _mm(h1, w_gate, name="mm_gate_in")
    att_o, att_l = [], []
    for g, (_, dil) in enumerate(ATT_PATTERNS):
        o, l = _att_fwd(att_in, g, dil)
        att_o.append(o)
        att_l.append(l)
    comb_rows = [(t, ATT_WIDTH, 0) for t in att_o + att_l]
    (att,) = _rows_fwd(_f_comb, comb_rows, [], [(ATT_WIDTH, BF16)], name="comb_fwd", br=br)
    y_att = _mm(att, w_ao, b_chip=True, name="mm_att_out")
    zs = _shift_fwd(z, mu)
    rwpre_c = [wts["w0"], wts["a0"], wts["k_k"], wts["k_a"], wl, e, et]
    lw, km, aa, bb, gg = _rows_fwd(_f_rwpre, [(zs, N_RWP, 0)], rwpre_c,
                                   [None, (D, F32), (D, F32), None, (D, F32), (D, F32), (D, F32)],
                                   name="rwpre_fwd", br=br)
    y_raw, s0s = _scan_fwd(zs, lw, km, aa, bb)
    post_rows = [(y_raw, D, 0), (zs, D, 0), (zs, D, 2), (km, D, 0), (gg, D, 0)]
    post_c = [wts["lnx_w"], wts["lnx_b"], rk, e, et]
    (rw_out,) = _rows_fwd(_f_rwpost, post_rows, post_c, [(D, BF16)], name="rwpost_fwd", br=br)
    y_rw = _mm(rw_out, w_ro, name="mm_rw_out")
    mix_rows = [(gate_in, D, 0), (gate_in, D, 1), (y_att, D, 0), (y_rw, D, 0)]
    (mix,) = _rows_fwd(_f_mix, mix_rows, [bga, bgr], [(D, BF16)], name="mix_fwd", br=br)
    o = _mm(mix, w_o, name="mm_o")
    pre2_c = [gt1, wts["norm2_w"], sc2, sh2]
    x1, h2 = _rows_fwd(_f_pre2, [(x, D, 0), (o, D, 0)], pre2_c, [(D, F32), (D, BF16)], name="pre2_fwd", br=br)
    u = _mm(h2, w_up, b_chip=True, name="mm_up")
    act = _conv_fwd(u, wts["conv_w"], wts["conv_b"])
    f = _mm(act, w_down, name="mm_down")
    fin_rows = [(x1, D, 0), (f, D, 0), (tgt, D, 0)]
    fin_c = [gt2, wts["norm_f_w"]]

    def fin_fwd(*a):
        (l,) = _f_fin(*a)
        return (jnp.broadcast_to(jnp.sum(l, axis=0, keepdims=True), (8, 128)),)

    (loss_acc,) = _rows_fwd(fin_fwd, fin_rows, fin_c, [], name="fin_fwd", br=br, acc_shape=(8, 128))
    loss = loss_acc[0, 0]

    gw = {}
    dx1a, df, d_gt2, gw["norm_f_w"] = _rows_bwd(
        _f_fin, fin_rows, fin_c, [[]], wrt_rows=[0, 1], wrt_consts=[0, 1], drow_dtypes=[F32, BF16],
        name="fin_bwd", br=br, unit_cot=True)
    dact = _mm(df, w_down, tb=True, name="mm_dact")
    gw["w_down"] = _rows_split(_mm(act, df, ta=True, name="mm_dw_down"))
    du, gw["conv_w"], gw["conv_b"] = _conv_bwd(u, wts["conv_w"], wts["conv_b"], dact)
    dh2 = _mm(du, w_up, tb=True, b_chip=True, name="mm_dh2")
    gw["w_up"] = _mm(h2, du, ta=True, out_chip=True, name="mm_dw_up")
    dxa, do, d_gt1, gw["norm2_w"], d_sc2, d_sh2 = _rows_bwd(
        _f_pre2, [(x, D, 0), (o, D, 0)], pre2_c, [[(dx1a, D, 0)], [(dh2, D, 0)]], wrt_rows=[0, 1],
        wrt_consts=[0, 1, 2, 3], drow_dtypes=[F32, BF16], name="pre2_bwd", br=br)
    dmix = _mm(do, w_o, tb=True, name="mm_dmix")
    gw["w_o"] = _rows_split(_mm(mix, do, ta=True, name="mm_dw_o"))
    dga, dgr, dya, dyr, d_bga, d_bgr = _rows_bwd(
        _f_mix, mix_rows, [bga, bgr], [[(dmix, D, 0)]], wrt_rows=[0, 1, 2, 3], wrt_consts=[0, 1],
        drow_dtypes=[BF16] * 4, name="mix_bwd", br=br)
    gw["b_gate"] = jnp.concatenate([d_bga, d_bgr], axis=1)
    datt = _mm(dya, w_ao, tb=True, b_chip=True, name="mm_datt")
    gw["w_att_out"] = _mm(att, dya, ta=True, out_chip=True, name="mm_dw_att_out")
    drw = _mm(dyr, w_ro, tb=True, name="mm_drw")
    gw["w_rwkv_out"] = _rows_split(_mm(rw_out, dyr, ta=True, name="mm_dw_rw_out"))
    dcomb = _rows_bwd(_f_comb, comb_rows, [], [[(datt, ATT_WIDTH, 0)]], wrt_rows=list(range(6)), wrt_consts=[],
                      drow_dtypes=[F32] * 6, name="comb_bwd", br=br)
    datt_in = []
    for g, (_, dil) in enumerate(ATT_PATTERNS):
        datt_in += _att_bwd(att_in, g, dil, dcomb[g], dcomb[3 + g])
    datt_in = jnp.concatenate(datt_in, axis=1)
    dy_raw, dr_p, dv_p, dkm_p, dgg, gw["lnx_w"], gw["lnx_b"], gw["r_k"] = _rows_bwd(
        _f_rwpost, post_rows, post_c, [[(drw, D, 0)]], wrt_rows=[0, 1, 2, 3, 4], wrt_consts=[0, 1, 2],
        drow_dtypes=[F32] * 5, name="rwpost_bwd", br=br)
    dr_s, dlw, dkm_s, dv_s, daa, dbb = _scan_bwd(zs, lw, km, aa, bb, s0s, dy_raw)
    pre_cots = [[(dr_p, D, 0), (dr_s, D, 0)], [(dlw, D, 0)], [(dkm_p, D, 0), (dkm_s, D, 0)],
                [(dv_p, D, 0), (dv_s, D, 0)], [(daa, D, 0)], [(dbb, D, 0)], [(dgg, D, 0)]]
    dzs, gw["w0"], gw["a0"], gw["k_k"], gw["k_a"], dwl = _rows_bwd(
        _f_rwpre, [(zs, N_RWP, 0)], rwpre_c, pre_cots, wrt_rows=[0], wrt_consts=[0, 1, 2, 3, 4], drow_dtypes=[F32],
        name="rwpre_bwd", br=128)
    gw["w2"], gw["a2"] = _cols_split(dwl[0:64, 0:D]), _cols_split(dwl[64:128, D:2 * D])
    gw["g2"] = _cols_split(dwl[128:288, 2 * D:3 * D])
    dz, dmu = _shift_bwd(z, mu, dzs)
    gw["mu_shift"] = dmu[:, :N_RW]
    dgate = jnp.concatenate([dga, dgr], axis=1)
    dh1 = _mm(datt_in, w_att, tb=True, name="mm_dh1_att")
    dh1 = _mm(dz, w_rw, tb=True, add=dh1, name="mm_dh1_rw")
    dh1 = _mm(dgate, w_gate, tb=True, add=dh1, name="mm_dh1_gate")
    gw["w_in"] = _cols_split(jnp.concatenate([_mm(h1, datt_in, ta=True, name="mm_dw_att"),
                                              _mm(h1, dz, ta=True, name="mm_dw_rw")[:, :N_RW],
                                              _mm(h1, dgate, ta=True, name="mm_dw_gate")], axis=1))
    grad_x, gw["norm1_w"], d_sc1, d_sh1 = _rows_bwd(
        _f_pre, [(x, D, 0)], pre1_c, [[(dh1, D, 0)], [(dxa, D, 0)]], wrt_rows=[0], wrt_consts=[0, 1, 2],
        drow_dtypes=[F32], name="pre1_bwd", br=br)
    return loss, grad_x, (d_sh1, d_sc1, d_gt1, d_sh2, d_sc2, d_gt2), gw


_SMALL = ("b_ada", "norm1_w", "b_gate", "mu_shift", "w0", "a0", "k_k", "k_a", "r_k", "lnx_w", "lnx_b", "norm2_w",
          "conv_b", "norm_f_w")
_NAMES = ("w_ada", "b_ada", "norm1_w", "w_in", "b_gate", "mu_shift", "w0", "w2", "a0", "a2", "g2", "k_k", "k_a", "r_k",
          "lnx_w", "lnx_b", "w_att_out", "w_rwkv_out", "w_o", "norm2_w", "w_up", "conv_w", "conv_b", "w_down",
          "norm_f_w")


def kernel(x, c, w_ada, b_ada, norm1_w, w_in, b_gate, mu_shift, w0, w2, a0, a2, g2, k_k, k_a, r_k, lnx_w, lnx_b, w_att_out, w_rwkv_out, w_o, norm2_w, w_up, conv_w, conv_b, w_down, norm_f_w, loss_target, m_w_ada, m_b_ada, m_norm1_w, m_w_in, m_b_gate, m_mu_shift, m_w0, m_w2, m_a0, m_a2, m_g2, m_k_k, m_k_a, m_r_k, m_lnx_w, m_lnx_b, m_w_att_out, m_w_rwkv_out, m_w_o, m_norm2_w, m_w_up, m_conv_w, m_conv_b, m_w_down, m_norm_f_w, v_w_ada, v_b_ada, v_norm1_w, v_w_in, v_b_gate, v_mu_shift, v_w0, v_w2, v_a0, v_a2, v_g2, v_k_k, v_k_a, v_r_k, v_lnx_w, v_lnx_b, v_w_att_out, v_w_rwkv_out, v_w_o, v_norm2_w, v_w_up, v_conv_w, v_conv_b, v_w_down, v_norm_f_w):
    args = dict(locals())
    p, pm, pv = {}, {}, {}
    for name in _NAMES:
        for dst, key in ((p, name), (pm, "m_" + name), (pv, "v_" + name)):
            t = args[key]
            dst[name] = t.reshape(1, -1) if name in ("r_k", "norm_f_w") else t.reshape(t.shape[-2], t.shape[-1])
    xi, yi, ci = _me()
    chip = 2 * xi + yi
    dev = 4 * xi + 2 * yi + ci
    x2, tgt = x[0], loss_target[0]

    n_cw = 3 * (2 * D_FF // 4)
    vec = jnp.concatenate([c.reshape(-1), p["conv_w"].reshape(-1), jnp.zeros((8 * D - D - n_cw,), F32)]).reshape(8, D)
    g0 = _allgather8(vec, "gather_c").reshape(8, 8 * D)
    c_all = g0[:, :D]
    conv_w_full = jnp.concatenate([g0[2 * j, D:D + n_cw].reshape(3, -1) for j in range(4)], axis=1)
    n_ada = 6 * D // 4
    b_ada_sh = lax.dynamic_slice(p["b_ada"], (0, chip * n_ada), (1, n_ada))
    ada_sh = _ada_fwd(c_all, p["w_ada"], b_ada_sh)
    ga = _allgather8(ada_sh, "gather_ada")
    ada_all = jnp.concatenate([ga[2 * j] for j in range(4)], axis=1)
    ada_row = lax.dynamic_slice(ada_all, (dev, 0), (1, 6 * D))
    ada = [ada_row[:, j * D:(j + 1) * D] for j in range(6)]

    big = [n for n, _ in _BIG]
    wts = dict(zip(big, _gather_weights([p[n].astype(BF16) for n in big], "gather_w")))
    for n in _SMALL:
        wts[n] = p[n]
    wts["conv_w"] = conv_w_full

    loss_part, grad_x, d_ada, gw = _local_step(x2, tgt, ada, wts)

    small = [jnp.concatenate(d_ada, axis=1)] + [gw[n] for n in _SMALL[1:]] + [gw["conv_w"], loss_part.reshape(1, 1)]
    sizes = [t.size for t in small]
    flat = jnp.concatenate([t.reshape(-1) for t in small])
    npad = (-flat.shape[0]) % (8 * D)
    srows = (flat.shape[0] + npad) // D
    flat = jnp.concatenate([flat, jnp.zeros((npad,), F32)]).reshape(srows, D)
    parts = _allgather8(flat, "gather_small")
    tot = _sum_lead(parts, "sum_small").reshape(-1)
    pieces, pos = [], 0
    for sz in sizes:
        pieces.append(tot[pos:pos + sz])
        pos += sz
    grads = {}
    for n, piece in zip(_SMALL, pieces[:len(_SMALL)]):
        grads[n] = piece.reshape(p[n].shape)
    conv_w_grad = pieces[len(_SMALL)].reshape(3, 2 * D_FF)
    grads["conv_w"] = lax.dynamic_slice(conv_w_grad, (0, chip * (n_cw // 3)), (3, n_cw // 3))
    loss = pieces[-1][0]
    d_ada_all = parts[:, :6].reshape(8, 6 * D)
    grads["w_ada"] = _ada_bwd(c_all.T, lax.dynamic_slice(d_ada_all, (0, chip * n_ada), (8, n_ada)))

    keep, recv = _reduce_sibling([gw[n] for n in big], "reduce_sib")
    chip_part = [_slabs(lambda a, b: a + b, [k, r], True, BF16, "reduce_add2_" + n) for n, k, r in zip(big, keep, recv)]
    slots = _reduce_chips(chip_part, "reduce_chips")
    reds = [_slabs(lambda t: t[0] + t[1] + t[2] + t[3], [t], False, F32, "reduce_add4_" + n) for n, t in zip(big, slots)]
    for n, g in zip(big, _reduce_finish(reds, "reduce_sib2")):
        grads[n] = g

    outs_g, outs_d, outs_m, outs_v = [], [], [], []
    for name in _NAMES:
        g = grads[name]
        d, m, v = _adamw(p[name], g, pm[name], pv[name], "adamw_" + name)
        shape = args[name].shape
        outs_g.append(g.reshape(shape))
        outs_d.append(d.reshape(shape))
        outs_m.append(m.reshape(shape))
        outs_v.append(v.reshape(shape))
    return (loss, grad_x.reshape(x.shape), *outs_g, *outs_d, *outs_m, *outs_v)
```

```python
import functools

import jax
import jax.numpy as jnp
from jax import lax
from jax.experimental import pallas as pl
from jax.experimental.pallas import tpu as pltpu

F32 = jnp.float32
BF16 = jnp.bfloat16
HI = lax.Precision.HIGHEST
MESH = pl.DeviceIdType.MESH

D = 1024
ATT_PATTERNS = ((128, 1), (512, 4), (2048, 16))
ATT_BLOCK = 128
ATT_WIDTH = 512
N_ATT = 3 * 3 * ATT_WIDTH
N_RW = 3 * D + 64 + 64 + 160
N_RWP = 3456
N_LORA = N_RWP - 3 * D
N_GATE = 2 * D
D_FF = 2816
RMS_EPS = 1e-6
GN_EPS = 64e-5
SCAN_CHUNK = 64
SCAN_PAIRS = 4
NEG = -1e30
VMEM_LIMIT = 48 * 1024 * 1024

ADAM_LR, ADAM_B1, ADAM_B2, ADAM_EPS, ADAM_WD, ADAM_STEP = 0.001, 0.9, 0.999, 1e-08, 0.01, 10


def _pcall(body, **kw):
    return pl.pallas_call(body, **kw)


def _cparams(sem):
    return pltpu.CompilerParams(dimension_semantics=sem, vmem_limit_bytes=VMEM_LIMIT)


def _div(n, pref, mult):
    best = None
    d = mult
    while d <= min(n, pref):
        if n % d == 0:
            best = d
        d += mult
    return best if best else n


def _dg(a, b, ca, cb):
    return lax.dot_general(a.astype(BF16), b.astype(BF16), (((ca,), (cb,)), ((), ())), preferred_element_type=F32)


@jax.custom_vjp
def _nn(a, b):
    return _dg(a, b, 1, 0)


@jax.custom_vjp
def _nt(a, b):
    return _dg(a, b, 1, 1)


@jax.custom_vjp
def _tn(a, b):
    return _dg(a, b, 0, 0)


_nn.defvjp(lambda a, b: (_nn(a, b), (a, b)), lambda res, g: (_nt(g, res[1]), _tn(res[0], g)))
_nt.defvjp(lambda a, b: (_nt(a, b), (a, b)), lambda res, g: (_nn(g, res[1]), _tn(g, res[0])))
_tn.defvjp(lambda a, b: (_tn(a, b), (a, b)), lambda res, g: (_nt(res[1], g), _nn(res[0], g)))


def _bdg(a, b, ca, cb):
    return lax.dot_general(a.astype(BF16), b.astype(BF16), (((ca,), (cb,)), ((0,), (0,))), preferred_element_type=F32)


@jax.custom_vjp
def _bnn(a, b):
    return _bdg(a, b, 2, 1)


@jax.custom_vjp
def _bnt(a, b):
    return _bdg(a, b, 2, 2)


@jax.custom_vjp
def _btn(a, b):
    return _bdg(a, b, 1, 1)


_bnn.defvjp(lambda a, b: (_bnn(a, b), (a, b)), lambda res, g: (_bnt(g, res[1]), _btn(res[0], g)))
_bnt.defvjp(lambda a, b: (_bnt(a, b), (a, b)), lambda res, g: (_bnn(g, res[1]), _btn(g, res[0])))
_btn.defvjp(lambda a, b: (_btn(a, b), (a, b)), lambda res, g: (_bnt(res[1], g), _bnn(res[0], g)))


def _split2(x):
    hi = x.astype(BF16)
    lo = (x - hi.astype(F32)).astype(BF16)
    return hi, lo


def _hsum_impl(x, e, et):
    eb, etb = e.astype(BF16), et.astype(BF16)
    hi, lo = _split2(x)
    s = jnp.dot(hi, eb, preferred_element_type=F32) + jnp.dot(lo, eb, preferred_element_type=F32)
    shi, slo = _split2(s)
    return jnp.dot(shi, etb, preferred_element_type=F32) + jnp.dot(slo, etb, preferred_element_type=F32)


@jax.custom_vjp
def _hsum(x, e, et):
    return _hsum_impl(x, e, et)


_hsum.defvjp(lambda x, e, et: (_hsum_impl(x, e, et), (e, et)),
             lambda res, g: (_hsum_impl(g, res[0], res[1]), jnp.zeros_like(res[0]), jnp.zeros_like(res[1])))


def _mm(a, b, *, ta=False, tb=False, out_dtype=F32, add=None, b_chip=False, out_chip=False, name):
    if ta:
        kdim, m = a.shape
    else:
        m, kdim = a.shape
    if b_chip:
        n = b.shape[1] if tb else 4 * b.shape[2]
    else:
        n = b.shape[0] if tb else b.shape[1]
    tm, tn, tk = _div(m, 1536, 128), _div(n, 1536, 128), _div(kdim, 1408, 128)
    if b_chip and tb:
        tk = kdim // 4
    if (b_chip and not tb) or out_chip:
        tn = n // 4
    nk = kdim // tk
    ca, cb = (0 if ta else 1), (1 if tb else 0)

    def body(*refs):
        a_ref, b_ref = refs[0], refs[1]
        add_ref = None if add is None else refs[2]
        o_ref = refs[2 if add is None else 3]
        part = lax.dot_general(a_ref[...], b_ref[...], (((ca,), (cb,)), ((), ())), preferred_element_type=F32)

        def finish(r):
            if add_ref is not None:
                r = r + add_ref[...]
            o_ref[...] = r.astype(o_ref.dtype)

        if nk == 1:
            finish(part)
            return
        acc = refs[-1]
        k = pl.program_id(2)

        @pl.when(k == 0)
        def _():
            acc[...] = part

        @pl.when(k > 0)
        def _():
            acc[...] += part

        @pl.when(k == nk - 1)
        def _():
            finish(acc[...])

    a_spec = pl.BlockSpec((tk, tm), lambda i, j, k: (k, i)) if ta else pl.BlockSpec((tm, tk), lambda i, j, k: (i, k))
    if b_chip:
        b_spec = (pl.BlockSpec((None, tn, tk), lambda i, j, k: (k, j, 0)) if tb
                  else pl.BlockSpec((None, tk, tn), lambda i, j, k: (j, k, 0)))
    else:
        b_spec = pl.BlockSpec((tn, tk), lambda i, j, k: (j, k)) if tb else pl.BlockSpec((tk, tn), lambda i, j, k: (k, j))
    in_specs = [a_spec, b_spec]
    args = [a, b]
    if add is not None:
        in_specs.append(pl.BlockSpec((tm, tn), lambda i, j, k: (i, j)))
        args.append(add)
    if out_chip:
        out_spec = pl.BlockSpec((None, tm, tn), lambda i, j, k: (j, i, 0))
        out_shape = jax.ShapeDtypeStruct((4, m, tn), out_dtype)
    else:
        out_spec = pl.BlockSpec((tm, tn), lambda i, j, k: (i, j))
        out_shape = jax.ShapeDtypeStruct((m, n), out_dtype)
    return _pcall(
        body, name=name, grid=(m // tm, n // tn, nk), in_specs=in_specs, out_specs=out_spec, out_shape=out_shape,
        scratch_shapes=[] if nk == 1 else [pltpu.VMEM((tm, tn), F32)],
        compiler_params=_cparams(("parallel", "parallel", "arbitrary")),
    )(*args)


def _row_spec(br, w, cb):
    return pl.BlockSpec((br, w), lambda i: (i, cb))


def _const_spec(shape):
    return pl.BlockSpec(shape, lambda i: (0,) * len(shape))


def _rows_fwd(fn, rows, consts, outs, *, name, br, acc_shape=None):
    s = rows[0][0].shape[0]
    nr, nc = len(rows), len(consts)
    kept = [k for k, o in enumerate(outs) if o is not None]

    def body(*refs):
        xs = [r[...].astype(F32) for r in refs[:nr]]
        cs = [c[...] for c in refs[nr:nr + nc]]
        res = fn(*xs, *cs)
        orefs = refs[nr + nc:]
        for j, k in enumerate(kept):
            orefs[j][...] = res[k].astype(orefs[j].dtype)
        if acc_shape is not None:
            acc_ref = orefs[len(kept)]

            @pl.when(pl.program_id(0) == 0)
            def _():
                acc_ref[...] = jnp.zeros_like(acc_ref)

            acc_ref[...] += res[len(outs)]

    in_specs = [_row_spec(br, w, cb) for (_, w, cb) in rows] + [_const_spec(c.shape) for c in consts]
    out_specs = [_row_spec(br, outs[k][0], 0) for k in kept]
    out_shape = [jax.ShapeDtypeStruct((s, outs[k][0]), outs[k][1]) for k in kept]
    if acc_shape is not None:
        out_specs.append(_const_spec(acc_shape))
        out_shape.append(jax.ShapeDtypeStruct(acc_shape, F32))
    return _pcall(
        body, name=name, grid=(s // br,), in_specs=in_specs, out_specs=out_specs, out_shape=out_shape,
        compiler_params=_cparams(("arbitrary",)),
    )(*[r[0] for r in rows], *consts)


def _rows_bwd(fn, rows, consts, cots, *, wrt_rows, wrt_consts, drow_dtypes, name, br, unit_cot=False):
    s = rows[0][0].shape[0]
    nr, nc = len(rows), len(consts)
    flat_cots = [c for lst in cots for c in lst]
    ncot = len(flat_cots)

    def body(*refs):
        xs = [r[...].astype(F32) for r in refs[:nr]]
        cs = [c[...] for c in refs[nr:nr + nc]]
        cvals = [c[...].astype(F32) for c in refs[nr + nc:nr + nc + ncot]]
        orefs = refs[nr + nc + ncot:]

        def g(*d):
            xs2, cs2 = list(xs), list(cs)
            for j, k in enumerate(wrt_rows):
                xs2[k] = d[j]
            for j, k in enumerate(wrt_consts):
                cs2[k] = d[len(wrt_rows) + j]
            return tuple(fn(*xs2, *cs2))

        prim = [xs[k] for k in wrt_rows] + [cs[k] for k in wrt_consts]
        outs, vjp = jax.vjp(g, *prim)
        ct = []
        pos = 0
        for o, lst in zip(outs, cots):
            if unit_cot:
                ct.append(jnp.ones_like(o))
                continue
            acc = jnp.zeros_like(o)
            for _ in lst:
                acc = acc + cvals[pos]
                pos += 1
            ct.append(acc)
        grads = vjp(tuple(ct))
        for j in range(len(wrt_rows)):
            orefs[j][...] = grads[j].astype(orefs[j].dtype)

        @pl.when(pl.program_id(0) == 0)
        def _():
            for j in range(len(wrt_consts)):
                oref = orefs[len(wrt_rows) + j]
                oref[...] = jnp.zeros_like(oref)

        for j in range(len(wrt_consts)):
            orefs[len(wrt_rows) + j][...] += grads[len(wrt_rows) + j]

    in_specs = ([_row_spec(br, w, cb) for (_, w, cb) in rows] + [_const_spec(c.shape) for c in consts]
                + [_row_spec(br, w, cb) for (_, w, cb) in flat_cots])
    out_specs = [_row_spec(br, rows[k][1], 0) for k in wrt_rows] + [_const_spec(consts[k].shape) for k in wrt_consts]
    out_shape = ([jax.ShapeDtypeStruct((s, rows[k][1]), dt) for k, dt in zip(wrt_rows, drow_dtypes)]
                 + [jax.ShapeDtypeStruct(consts[k].shape, F32) for k in wrt_consts])
    return _pcall(
        body, name=name, grid=(s // br,), in_specs=in_specs, out_specs=out_specs, out_shape=out_shape,
        compiler_params=_cparams(("arbitrary",)),
    )(*[r[0] for r in rows], *consts, *[c[0] for c in flat_cots])


def _rms(x, w):
    return x * lax.rsqrt(jnp.mean(x * x, axis=-1, keepdims=True) + RMS_EPS) * w


def _softplus(x):
    return jnp.maximum(x, 0.0) + jnp.log(1.0 + jnp.exp(-jnp.abs(x)))


def _f_pre(x, nw, sc, sh):
    return _rms(x, nw) * (1.0 + sc) + sh, x


def _f_pre2(x, o, gt, nw, sc, sh):
    x1 = x + gt * o
    return x1, _rms(x1, nw) * (1.0 + sc) + sh


def _f_fin(x1, f, tgt, gt, nfw):
    y = _rms(x1 + gt * f, nfw)
    return (0.5 * jnp.mean(jnp.square(y - tgt), axis=-1, keepdims=True),)


def _f_comb(o1, o2, o3, l1, l2, l3):
    m = lax.stop_gradient(jnp.maximum(jnp.maximum(l1, l2), l3))
    e1, e2, e3 = jnp.exp(l1 - m), jnp.exp(l2 - m), jnp.exp(l3 - m)
    return ((e1 * o1 + e2 * o2 + e3 * o3) / (e1 + e2 + e3),)


def _f_rwpre(zs, w0, a0, k_k, k_a, wl, e, et):
    r, k, v, zl = zs[:, 0:D], zs[:, D:2 * D], zs[:, 2 * D:3 * D], zs[:, 3 * D:N_RWP]
    lane = lax.broadcasted_iota(jnp.int32, zl.shape, 1)
    t = jnp.where(lane < 64, jnp.tanh(zl), jnp.where(lane < 128, zl, jnp.where(lane < 288, jax.nn.sigmoid(zl), 0.0)))
    lo = _nn(t, wl)
    w_log = -_softplus(-(w0 + lo[:, 0:D])) - 0.5
    lw = -jnp.exp(w_log)
    a = jax.nn.sigmoid(a0 + lo[:, D:2 * D])
    g = lo[:, 2 * D:3 * D]
    k_mod = k * (1.0 + (a - 1.0) * k_a)
    kk = k * k_k
    kk = kk / jnp.maximum(jnp.sqrt(_hsum(kk * kk, e, et)), 1e-12)
    return r, lw, k_mod, v, -kk, kk * a, g


def _f_rwpost(y, r, v, k_mod, g, lnx_w, lnx_b, r_k, e, et):
    mean = _hsum(y, e, et) * (1.0 / 64)
    yc = y - mean
    var = _hsum(yc * yc, e, et) * (1.0 / 64)
    yn = yc * lax.rsqrt(var + GN_EPS) * lnx_w + lnx_b
    bonus = _hsum(r * k_mod * r_k, e, et) * v
    return ((yn + bonus) * g,)


def _f_mix(gia, gir, ya, yr, bga, bgr):
    return (jax.nn.sigmoid(gia + bga) * ya + jax.nn.sigmoid(gir + bgr) * yr,)


def _f_adamw(w, g, m, v):
    m = ADAM_B1 * m + (1.0 - ADAM_B1) * g
    v = ADAM_B2 * v + (1.0 - ADAM_B2) * jnp.square(g)
    m_hat = m / (1.0 - ADAM_B1 ** ADAM_STEP)
    v_hat = v / (1.0 - ADAM_B2 ** ADAM_STEP)
    return -ADAM_LR * (m_hat / (jnp.sqrt(v_hat) + ADAM_EPS) + ADAM_WD * w), m, v


def _down(x, k):
    row = lax.broadcasted_iota(jnp.int32, x.shape, 0)
    return jnp.where(row < k, 0.0, pltpu.roll(x, k, 0))


def _up(x, k):
    n = x.shape[0]
    row = lax.broadcasted_iota(jnp.int32, x.shape, 0)
    return jnp.where(row >= n - k, 0.0, pltpu.roll(x, n - k, 0))


def _col_spec(s, w, off=0):
    return pl.BlockSpec((s, w), lambda j: (0, j + off))


def _shift_fwd(z, mu):
    s, n = z.shape

    def body(z_ref, mu_ref, o_ref):
        zz = z_ref[...]
        o_ref[...] = zz + (_down(zz, 1) - zz) * mu_ref[...]

    return _pcall(
        body, name="shift_fwd", grid=(n // 128,), in_specs=[_col_spec(s, 128), _col_spec(1, 128)],
        out_specs=_col_spec(s, 128), out_shape=jax.ShapeDtypeStruct((s, n), F32),
        compiler_params=_cparams(("parallel",)),
    )(z, mu)


def _shift_bwd(z, mu, dzs):
    s, n = z.shape

    def body(z_ref, mu_ref, d_ref, dz_ref, dmu_ref):
        zz, d, m = z_ref[...], d_ref[...], mu_ref[...]
        dm = d * m
        dz_ref[...] = (d - dm + _up(dm, 1)).astype(dz_ref.dtype)
        dmu_ref[...] = jnp.sum(d * (_down(zz, 1) - zz), axis=0, keepdims=True)

    return _pcall(
        body, name="shift_bwd", grid=(n // 128,), in_specs=[_col_spec(s, 128), _col_spec(1, 128), _col_spec(s, 128)],
        out_specs=[_col_spec(s, 128), _col_spec(1, 128)],
        out_shape=[jax.ShapeDtypeStruct((s, n), BF16), jax.ShapeDtypeStruct((1, n), F32)],
        compiler_params=_cparams(("parallel",)),
    )(z, mu, dzs)


def _conv3(x, w_ref, b_ref):
    return b_ref[...] + w_ref[0:1, :] * _down(x, 2) + w_ref[1:2, :] * _down(x, 1) + w_ref[2:3, :] * x


def _conv_fwd(u, cw, cb):
    s = u.shape[0]
    nb = D_FF // 128

    def body(ug_ref, uv_ref, wg_ref, wv_ref, bg_ref, bv_ref, o_ref):
        gate = _conv3(ug_ref[...], wg_ref, bg_ref)
        val = _conv3(uv_ref[...], wv_ref, bv_ref)
        o_ref[...] = (gate * jax.nn.sigmoid(gate) * val).astype(o_ref.dtype)

    return _pcall(
        body, name="conv_fwd", grid=(nb,),
        in_specs=[_col_spec(s, 128), _col_spec(s, 128, nb), _col_spec(3, 128), _col_spec(3, 128, nb),
                  _col_spec(1, 128), _col_spec(1, 128, nb)],
        out_specs=_col_spec(s, 128), out_shape=jax.ShapeDtypeStruct((s, D_FF), BF16),
        compiler_params=_cparams(("parallel",)),
    )(u, u, cw, cw, cb, cb)


def _conv_bwd(u, cw, cb, dact):
    s = u.shape[0]
    nb = D_FF // 128

    def half(x, d, w_ref, du_ref, dw_ref, db_ref):
        x1, x2 = _down(x, 1), _down(x, 2)
        du_ref[...] = (w_ref[2:3, :] * d + w_ref[1:2, :] * _up(d, 1) + w_ref[0:1, :] * _up(d, 2)).astype(du_ref.dtype)
        dw_ref[0:1, :] = jnp.sum(d * x2, axis=0, keepdims=True)
        dw_ref[1:2, :] = jnp.sum(d * x1, axis=0, keepdims=True)
        dw_ref[2:3, :] = jnp.sum(d * x, axis=0, keepdims=True)
        db_ref[...] = jnp.sum(d, axis=0, keepdims=True)

    def body(ug_ref, uv_ref, wg_ref, wv_ref, bg_ref, bv_ref, da_ref,
             dug_ref, duv_ref, dwg_ref, dwv_ref, dbg_ref, dbv_ref):
        ug, uv, da = ug_ref[...], uv_ref[...], da_ref[...]
        gate = _conv3(ug, wg_ref, bg_ref)
        val = _conv3(uv, wv_ref, bv_ref)
        sg = jax.nn.sigmoid(gate)
        dgate = da * val * sg * (1.0 + gate * (1.0 - sg))
        dval = da * gate * sg
        half(ug, dgate, wg_ref, dug_ref, dwg_ref, dbg_ref)
        half(uv, dval, wv_ref, duv_ref, dwv_ref, dbv_ref)

    dug, duv, dwg, dwv, dbg, dbv = _pcall(
        body, name="conv_bwd", grid=(nb,),
        in_specs=[_col_spec(s, 128), _col_spec(s, 128, nb), _col_spec(3, 128), _col_spec(3, 128, nb),
                  _col_spec(1, 128), _col_spec(1, 128, nb), _col_spec(s, 128)],
        out_specs=[_col_spec(s, 128), _col_spec(s, 128), _col_spec(3, 128), _col_spec(3, 128),
                   _col_spec(1, 128), _col_spec(1, 128)],
        out_shape=[jax.ShapeDtypeStruct((s, D_FF), BF16), jax.ShapeDtypeStruct((s, D_FF), BF16),
                   jax.ShapeDtypeStruct((3, D_FF), F32), jax.ShapeDtypeStruct((3, D_FF), F32),
                   jax.ShapeDtypeStruct((1, D_FF), F32), jax.ShapeDtypeStruct((1, D_FF), F32)],
        compiler_params=_cparams(("parallel",)),
    )(u, u, cw, cw, cb, cb, dact)
    return (jnp.concatenate([dug, duv], axis=1), jnp.concatenate([dwg, dwv], axis=1),
            jnp.concatenate([dbg, dbv], axis=1))


ATT_BATCH = 4


def _att_batch(q, kp, kc, vp, vc, first):
    ma = lax.broadcasted_iota(jnp.int32, (1, ATT_BLOCK, 128), 2) < 64
    qs = jnp.concatenate([jnp.where(ma, q, 0.0), jnp.where(ma, 0.0, q)], axis=1)
    qi = lax.broadcasted_iota(jnp.int32, (1, 2 * ATT_BLOCK, ATT_BLOCK), 1) & (ATT_BLOCK - 1)
    kj = lax.broadcasted_iota(jnp.int32, (1, 2 * ATT_BLOCK, ATT_BLOCK), 2)
    okp = kj >= qi + jnp.where(first, 2 * ATT_BLOCK, 0)
    okc = kj <= qi
    sp = jnp.where(okp, _bnt(qs, kp) * 0.125, NEG)
    sc = jnp.where(okc, _bnt(qs, kc) * 0.125, NEG)
    m = lax.stop_gradient(jnp.maximum(jnp.max(sp, axis=-1, keepdims=True), jnp.max(sc, axis=-1, keepdims=True)))
    pp, pc = jnp.exp(sp - m), jnp.exp(sc - m)
    den = jnp.sum(pp, axis=-1, keepdims=True) + jnp.sum(pc, axis=-1, keepdims=True)
    o_s = (_bnn(pp, vp) + _bnn(pc, vc)) / den
    l_s = jnp.broadcast_to(m + jnp.log(den), o_s.shape)
    return (jnp.where(ma, o_s[:, :ATT_BLOCK], o_s[:, ATT_BLOCK:]), jnp.where(ma, l_s[:, :ATT_BLOCK], l_s[:, ATT_BLOCK:]))


def _att_pairs_per_step(dil):
    return ATT_BATCH if dil == 1 else 1


def _att_specs(g, dil):
    rows, pp = ATT_BLOCK * dil, _att_pairs_per_step(dil)

    def cur(slot):
        return pl.BlockSpec((rows, 128 * pp), lambda n, p: (n, (g * 3 + slot) * (4 // pp) + p))

    def prev(slot):
        return pl.BlockSpec((rows, 128 * pp), lambda n, p: (jnp.maximum(n - 1, 0), (g * 3 + slot) * (4 // pp) + p))

    return [cur(0), prev(1), cur(1), prev(2), cur(2)]


def _att_out_spec(dil):
    return pl.BlockSpec((ATT_BLOCK * dil, 128 * _att_pairs_per_step(dil)), lambda n, p: (n, p))


def _att_grid(s, dil):
    return (s // (ATT_BLOCK * dil), 4 // _att_pairs_per_step(dil))


def _att_windows(i, dil):
    if dil == 1:
        return [(pl.ds(0, ATT_BLOCK), pl.ds(128 * j, 128)) for j in range(ATT_BATCH)]
    return [(pl.ds(i * ATT_BATCH + j, ATT_BLOCK, stride=dil), pl.ds(0, 128)) for j in range(ATT_BATCH)]


def _att_fwd(att_in, g, dil):
    s = att_in.shape[0]

    def body(q_ref, kp_ref, kc_ref, vp_ref, vc_ref, o_ref, l_ref):
        first = pl.program_id(0) == 0

        def one(i, carry):
            win = _att_windows(i, dil)
            vals = [jnp.stack([ref[w] for w in win]) for ref in (q_ref, kp_ref, kc_ref, vp_ref, vc_ref)]
            o, l = _att_batch(*vals, first)
            for j, w in enumerate(win):
                o_ref[w] = o[j]
                l_ref[w] = l[j]
            return carry

        lax.fori_loop(0, max(1, dil // ATT_BATCH), one, 0)

    return _pcall(
        body, name=f"att_fwd{g}", grid=_att_grid(s, dil), in_specs=_att_specs(g, dil),
        out_specs=[_att_out_spec(dil)] * 2, out_shape=[jax.ShapeDtypeStruct((s, ATT_WIDTH), F32)] * 2,
        compiler_params=_cparams(("parallel", "parallel")),
    )(att_in, att_in, att_in, att_in, att_in)


def _att_bwd(att_in, g, dil, do, dl):
    s = att_in.shape[0]
    nb = s // (ATT_BLOCK * dil)

    def body(q_ref, kp_ref, kc_ref, vp_ref, vc_ref, do_ref, dl_ref, dq_ref, dkp_ref, dkc_ref, dvp_ref, dvc_ref):
        first = pl.program_id(0) == 0

        def one(i, carry):
            win = _att_windows(i, dil)
            vals = [jnp.stack([ref[w] for w in win]) for ref in (q_ref, kp_ref, kc_ref, vp_ref, vc_ref)]
            _, vjp = jax.vjp(lambda *a: _att_batch(*a, first), *vals)
            grads = vjp((jnp.stack([do_ref[w] for w in win]), jnp.stack([dl_ref[w] for w in win])))
            for ref, gr in zip((dq_ref, dkp_ref, dkc_ref, dvp_ref, dvc_ref), grads):
                for j, w in enumerate(win):
                    ref[w] = gr[j]
            return carry

        lax.fori_loop(0, max(1, dil // ATT_BATCH), one, 0)

    dq, dkp, dkc, dvp, dvc = _pcall(
        body, name=f"att_bwd{g}", grid=_att_grid(s, dil), in_specs=_att_specs(g, dil) + [_att_out_spec(dil)] * 2,
        out_specs=[_att_out_spec(dil)] * 5, out_shape=[jax.ShapeDtypeStruct((s, ATT_WIDTH), F32)] * 5,
        compiler_params=_cparams(("parallel", "parallel")),
    )(att_in, att_in, att_in, att_in, att_in, do, dl)

    def cbody(dq_ref, dkc_ref, dkn_ref, dvc_ref, dvn_ref, oq_ref, ok_ref, ov_ref):
        has_next = pl.program_id(0) + 1 < nb
        oq_ref[...] = dq_ref[...].astype(BF16)
        ok_ref[...] = (dkc_ref[...] + jnp.where(has_next, dkn_ref[...], 0.0)).astype(BF16)
        ov_ref[...] = (dvc_ref[...] + jnp.where(has_next, dvn_ref[...], 0.0)).astype(BF16)

    cur = pl.BlockSpec((ATT_BLOCK * dil, 128), lambda n, p: (n, p))
    nxt = pl.BlockSpec((ATT_BLOCK * dil, 128), lambda n, p: (jnp.minimum(n + 1, nb - 1), p))
    return _pcall(
        cbody, name=f"att_bwd_sum{g}", grid=(nb, 4), in_specs=[cur, cur, nxt, cur, nxt], out_specs=[cur] * 3,
        out_shape=[jax.ShapeDtypeStruct((s, ATT_WIDTH), BF16)] * 3,
        compiler_params=_cparams(("parallel", "parallel")),
    )(dq, dkc, dkp, dvc, dvp)


def _scan_chunk(r, lw, k, v, a, b, s0):
    c = SCAN_CHUNK
    p = s0.shape[0]
    ri = lax.broadcasted_iota(jnp.int32, (c, c), 0)
    ci = lax.broadcasted_iota(jnp.int32, (c, c), 1)
    cum = jnp.dot((ci <= ri).astype(F32), lw, precision=HI, preferred_element_type=F32)
    tot = jnp.sum(lw, axis=0, keepdims=True)
    ma = (lax.broadcasted_iota(jnp.int32, (c, 128 * p), 1) & 127) < 64

    def pairs(x):
        return jnp.concatenate([x[None, :, 128 * j:128 * (j + 1)] for j in range(p)], axis=0)

    def stack(x):
        return jnp.concatenate([pairs(jnp.where(ma, x, 0.0)), pairs(jnp.where(ma, 0.0, x))], axis=1)

    einv, eend = jnp.exp(-cum), jnp.exp(tot - cum)
    ra, aa = stack(r * jnp.exp(cum)), stack(a * jnp.exp(cum - lw))
    bi, ki, be, ke, vs = stack(b * einv), stack(k * einv), stack(b * eend), stack(k * eend), stack(v)
    r2 = lax.broadcasted_iota(jnp.int32, (1, 2 * c, 2 * c), 1)
    c2 = lax.broadcasted_iota(jnp.int32, (1, 2 * c, 2 * c), 2)
    same = (r2 >= c) == (c2 >= c)
    strict = jnp.logical_and(same, c2 < r2)
    incl = jnp.logical_and(same, c2 <= r2)
    s0 = jnp.where(same, s0, 0.0)
    a_ab = jnp.where(strict, _bnt(aa, bi), 0.0)
    a_ak = jnp.where(strict, _bnt(aa, ki), 0.0)
    a_rb = jnp.where(incl, _bnt(ra, bi), 0.0)
    a_rk = jnp.where(incl, _bnt(ra, ki), 0.0)
    t = jnp.where(r2 == c2, 1.0, 0.0) + a_ab
    pw = a_ab
    for _ in range(5):
        pw = _bnn(pw, pw)
        t = t + _bnn(t, pw)
    u = _bnn(t, _bnt(aa, s0) + _bnn(a_ak, vs))
    ys = _bnt(ra, s0) + _bnn(a_rb, u) + _bnn(a_rk, vs)
    s1 = s0 * pairs(jnp.exp(tot)) + _btn(u, be) + _btn(vs, ke)
    y3 = ys[:, :c] + ys[:, c:]
    return jnp.concatenate([y3[j] for j in range(p)], axis=1), s1


def _scan_specs(rev, n):
    w = 128 * SCAN_PAIRS

    def at(i):
        return n - 1 - i if rev else i

    def cm(cb):
        return pl.BlockSpec((SCAN_CHUNK, w), lambda p, i: (at(i), cb * (D // w) + p))

    return cm, pl.BlockSpec((1, SCAN_PAIRS, 128, 128), lambda p, i: (at(i), p, 0, 0))


def _scan_fwd(zs, lw, km, aa, bb):
    s = zs.shape[0]
    n = s // SCAN_CHUNK
    cm, st = _scan_specs(False, n)

    def body(r_ref, lw_ref, k_ref, v_ref, a_ref, b_ref, y_ref, s0_ref, state):
        @pl.when(pl.program_id(1) == 0)
        def _():
            state[...] = jnp.zeros_like(state)

        s0 = state[...]
        s0_ref[0] = s0
        y, s1 = _scan_chunk(*[ref[...] for ref in (r_ref, lw_ref, k_ref, v_ref, a_ref, b_ref)], s0)
        y_ref[...] = y
        state[...] = s1

    return _pcall(
        body, name="scan_fwd", grid=(8 // SCAN_PAIRS, n), in_specs=[cm(0), cm(0), cm(0), cm(2), cm(0), cm(0)],
        out_specs=[cm(0), st],
        out_shape=[jax.ShapeDtypeStruct((s, D), F32), jax.ShapeDtypeStruct((n, 8, 128, 128), F32)],
        scratch_shapes=[pltpu.VMEM((SCAN_PAIRS, 128, 128), F32)],
        compiler_params=_cparams(("parallel", "arbitrary")),
    )(zs, lw, km, zs, aa, bb)


def _scan_bwd(zs, lw, km, aa, bb, s0s, dy):
    s = zs.shape[0]
    n = s // SCAN_CHUNK
    cm, st = _scan_specs(True, n)

    def body(r_ref, lw_ref, k_ref, v_ref, a_ref, b_ref, s0_ref, dy_ref,
             dr_ref, dlw_ref, dk_ref, dv_ref, da_ref, db_ref, dstate):
        @pl.when(pl.program_id(1) == 0)
        def _():
            dstate[...] = jnp.zeros_like(dstate)

        prim = [ref[...] for ref in (r_ref, lw_ref, k_ref, v_ref, a_ref, b_ref)] + [s0_ref[0]]
        _, vjp = jax.vjp(_scan_chunk, *prim)
        grads = vjp((dy_ref[...], dstate[...]))
        for ref, gr in zip((dr_ref, dlw_ref, dk_ref, dv_ref, da_ref, db_ref), grads[:6]):
            ref[...] = gr
        dstate[...] = grads[6]

    return _pcall(
        body, name="scan_bwd", grid=(8 // SCAN_PAIRS, n),
        in_specs=[cm(0), cm(0), cm(0), cm(2), cm(0), cm(0), st, cm(0)],
        out_specs=[cm(0)] * 6, out_shape=[jax.ShapeDtypeStruct((s, D), F32)] * 6,
        scratch_shapes=[pltpu.VMEM((SCAN_PAIRS, 128, 128), F32)],
        compiler_params=_cparams(("parallel", "arbitrary")),
    )(zs, lw, km, zs, aa, bb, s0s, dy)


_HBM = pl.BlockSpec(memory_space=pltpu.HBM)


def _me():
    return lax.axis_index("x"), lax.axis_index("y"), lax.axis_index("c")


def _allgather8(src, name):
    def body(src_ref, out_ref, ssem, rsem, lsem):
        x, y, c = _me()
        me = 4 * x + 2 * y + c
        local = pltpu.make_async_copy(src_ref, out_ref.at[me], lsem)
        local.start()
        peers = []
        for k in range(1, 8):
            peers.append(((1 - x) if k & 4 else x, (1 - y) if k & 2 else y, (1 - c) if k & 1 else c))
        sends = []
        for k, peer in enumerate(peers):
            cp = pltpu.make_async_remote_copy(src_ref, out_ref.at[me], ssem.at[k], rsem.at[k], device_id=peer,
                                              device_id_type=MESH)
            cp.start()
            sends.append(cp)
        for k, (px, py, pc) in enumerate(peers):
            pltpu.make_async_remote_copy(src_ref, out_ref.at[4 * px + 2 * py + pc], ssem.at[k], rsem.at[k],
                                         device_id=(px, py, pc), device_id_type=MESH).wait_recv()
        for cp in sends:
            cp.wait_send()
        local.wait()

    return _pcall(
        body, name=name, in_specs=[_HBM], out_specs=_HBM, out_shape=jax.ShapeDtypeStruct((8,) + src.shape, src.dtype),
        scratch_shapes=[pltpu.SemaphoreType.DMA((7,)), pltpu.SemaphoreType.DMA((7,)), pltpu.SemaphoreType.DMA],
    )(src)


def _other_chips(x, y):
    return [(1 - x, y), (x, 1 - y), (1 - x, 1 - y)]


def _remote(src, dst, ssem, rsem, to):
    return pltpu.make_async_remote_copy(src, dst, ssem, rsem, device_id=to, device_id_type=MESH)


def _gather_weights(shards, name):
    n = len(shards)

    def body(*refs):
        srcs, outs = refs[:n], refs[n:2 * n]
        ssem, rsem, lsem, osem = refs[2 * n:]
        x, y, c = _me()
        me = 2 * x + y
        chips = _other_chips(x, y)

        def half(a, core):
            h = shards[a].shape[0] // 2
            return pl.ds(core * h, h)

        own = [_remote(srcs[a], outs[a].at[me], lsem.at[a], osem.at[a], (x, y, 1 - c)) for a in range(n)]
        for cp in own:
            cp.start()
        sends = []
        for a in range(n):
            for k, (px, py) in enumerate(chips):
                sends.append(_remote(srcs[a].at[half(a, c)], outs[a].at[me, half(a, c)], ssem.at[6 * a + k],
                                     rsem.at[6 * a + k], (px, py, c)))
                sends[-1].start()
        for a in range(n):
            for k, (px, py) in enumerate(chips):
                got = outs[a].at[2 * px + py, half(a, c)]
                _remote(got, got, ssem.at[6 * a + k], rsem.at[6 * a + k], (px, py, c)).wait_recv()
                sends.append(_remote(got, got, ssem.at[6 * a + 3 + k], rsem.at[6 * a + 3 + k], (x, y, 1 - c)))
                sends[-1].start()
        for a in range(n):
            for k, (px, py) in enumerate(chips):
                dst = outs[a].at[2 * px + py, half(a, 1 - c)]
                _remote(dst, dst, ssem.at[6 * a + 3 + k], rsem.at[6 * a + 3 + k], (x, y, 1 - c)).wait_recv()
        for cp in sends:
            cp.wait_send()
        for cp in own:
            cp.wait()

    return _pcall(
        body, name=name, in_specs=[_HBM] * n, out_specs=[_HBM] * n,
        out_shape=[jax.ShapeDtypeStruct((4,) + t.shape, t.dtype) for t in shards],
        scratch_shapes=[pltpu.SemaphoreType.DMA((6 * n,)), pltpu.SemaphoreType.DMA((6 * n,)),
                        pltpu.SemaphoreType.DMA((n,)), pltpu.SemaphoreType.DMA((n,))],
    )(*shards)


def _reduce_sibling(grads, name):
    n = len(grads)

    def body(*refs):
        srcs, recvs = refs[:n], refs[n:2 * n]
        ssem, rsem = refs[2 * n:]
        x, y, c = _me()
        copies = []
        for a in range(n):
            h = grads[a].shape[1] // 2
            copies.append(_remote(srcs[a].at[:, pl.ds((1 - c) * h, h)], recvs[a], ssem.at[a], rsem.at[a], (x, y, 1 - c)))
        for cp in copies:
            cp.start()
        for cp in copies:
            cp.wait()

    return _pcall(
        body, name=name, in_specs=[_HBM] * n, out_specs=[_HBM] * n,
        out_shape=[jax.ShapeDtypeStruct((4, t.shape[1] // 2, t.shape[2]), t.dtype) for t in grads],
        scratch_shapes=[pltpu.SemaphoreType.DMA((n,)), pltpu.SemaphoreType.DMA((n,))],
    )(*grads)


def _reduce_chips(parts, name):
    n = len(parts)

    def body(*refs):
        srcs, outs = refs[:n], refs[n:2 * n]
        ssem, rsem, lsem = refs[2 * n:]
        x, y, c = _me()
        me = 2 * x + y
        chips = _other_chips(x, y)
        copies = [pltpu.make_async_copy(srcs[a].at[me], outs[a].at[me], lsem.at[a]) for a in range(n)]
        for a in range(n):
            for k, (px, py) in enumerate(chips):
                copies.append(_remote(srcs[a].at[2 * px + py], outs[a].at[me], ssem.at[3 * a + k], rsem.at[3 * a + k],
                                      (px, py, c)))
        for cp in copies:
            cp.start()
        for a in range(n):
            for k, (px, py) in enumerate(chips):
                dst = outs[a].at[2 * px + py]
                _remote(dst, dst, ssem.at[3 * a + k], rsem.at[3 * a + k], (px, py, c)).wait_recv()
        for cp in copies[:n]:
            cp.wait()
        for cp in copies[n:]:
            cp.wait_send()

    return _pcall(
        body, name=name, in_specs=[_HBM] * n, out_specs=[_HBM] * n,
        out_shape=[jax.ShapeDtypeStruct(t.shape, t.dtype) for t in parts],
        scratch_shapes=[pltpu.SemaphoreType.DMA((3 * n,)), pltpu.SemaphoreType.DMA((3 * n,)),
                        pltpu.SemaphoreType.DMA((n,))],
    )(*parts)


def _reduce_finish(reds, name):
    n = len(reds)

    def body(*refs):
        outs = refs[n:2 * n]
        ssem, rsem = refs[2 * n:]
        x, y, c = _me()
        copies = []
        for a in range(n):
            h = reds[a].shape[0] // 2
            mine = outs[a].at[pl.ds(c * h, h)]
            copies.append(_remote(mine, mine, ssem.at[a], rsem.at[a], (x, y, 1 - c)))
        for cp in copies:
            cp.start()
        for a in range(n):
            h = reds[a].shape[0] // 2
            dst = outs[a].at[pl.ds((1 - c) * h, h)]
            _remote(dst, dst, ssem.at[a], rsem.at[a], (x, y, 1 - c)).wait_recv()
        for cp in copies:
            cp.wait_send()

    return _pcall(
        body, name=name, in_specs=[_HBM] * n, out_specs=[_HBM] * n,
        out_shape=[jax.ShapeDtypeStruct(t.shape, t.dtype) for t in reds],
        input_output_aliases={a: a for a in range(n)},
        scratch_shapes=[pltpu.SemaphoreType.DMA((n,)), pltpu.SemaphoreType.DMA((n,))],
    )(*reds)


def _half_sum(fn, full, halves, out_full, out_dtype, core, name):
    p, h, c = (halves[0].shape if halves else (full[0].shape[0], full[0].shape[1] // 2, full[0].shape[2]))
    br = _div(h, max(16, (1 << 19) // (p * c)), 16)
    nb = h // br
    mine3 = pl.BlockSpec((p, br, c), lambda i, core_ref: (0, core_ref[0] * nb + i, 0))
    half3 = pl.BlockSpec((p, br, c), lambda i, core_ref: (0, i, 0))

    def body(core_ref, *refs):
        refs[-1][...] = fn(*[t[...].astype(F32) for t in refs[:-1]]).astype(out_dtype)

    if out_full:
        out_spec = pl.BlockSpec((br, c), lambda i, core_ref: (core_ref[0] * nb + i, 0))
        out_shape = jax.ShapeDtypeStruct((2 * h, c), out_dtype)
    else:
        out_spec, out_shape = half3, jax.ShapeDtypeStruct((p, h, c), out_dtype)
    return _pcall(
        body, name=name,
        grid_spec=pltpu.PrefetchScalarGridSpec(
            num_scalar_prefetch=1, grid=(nb,), in_specs=[mine3] * len(full) + [half3] * len(halves),
            out_specs=out_spec),
        out_shape=out_shape, compiler_params=_cparams(("parallel",)),
    )(core, *full, *halves)


def _ada_fwd(c_all, w, b):
    def body(c_ref, w_ref, b_ref, o_ref):
        o_ref[...] = jnp.dot(c_ref[...], w_ref[...], precision=HI, preferred_element_type=F32) + b_ref[...]

    return _pcall(body, name="ada_fwd", out_shape=jax.ShapeDtypeStruct((c_all.shape[0], w.shape[1]), F32),
                  compiler_params=pltpu.CompilerParams(vmem_limit_bytes=VMEM_LIMIT))(c_all, w, b)


def _ada_bwd(c_all_t, d):
    def body(c_ref, d_ref, o_ref):
        o_ref[...] = jnp.dot(c_ref[...], d_ref[...], precision=HI, preferred_element_type=F32)

    return _pcall(body, name="ada_bwd", out_shape=jax.ShapeDtypeStruct((c_all_t.shape[0], d.shape[1]), F32),
                  compiler_params=pltpu.CompilerParams(vmem_limit_bytes=VMEM_LIMIT))(c_all_t, d)


def _sum_lead(x, name):
    p, r, n = x.shape
    br = _div(r, 512, 8)

    def body(x_ref, o_ref):
        acc = x_ref[0]
        for j in range(1, p):
            acc = acc + x_ref[j]
        o_ref[...] = acc

    return _pcall(
        body, name=name, grid=(r // br,), in_specs=[pl.BlockSpec((p, br, n), lambda i: (0, i, 0))],
        out_specs=pl.BlockSpec((br, n), lambda i: (i, 0)), out_shape=jax.ShapeDtypeStruct((r, n), F32),
        compiler_params=_cparams(("parallel",)),
    )(x)


def _adamw(w, g, m, v, name):
    shape = w.shape
    cols = shape[-1]
    w2, g2, m2, v2 = [t.reshape(-1, cols) for t in (w, g, m, v)]
    rows = w2.shape[0]
    br = _div(rows, max(8, (1 << 19) // cols // 8 * 8), 8)
    outs = _rows_fwd(_f_adamw, [(t, cols, 0) for t in (w2, g2, m2, v2)], [], [(cols, F32)] * 3, name=name, br=br)
    return [o.reshape(shape) for o in outs]


_BIG = (("w_in", 1), ("w_up", 1), ("w_down", 0), ("w_o", 0), ("w_rwkv_out", 0), ("w_att_out", 1), ("w2", 1), ("a2", 1),
        ("g2", 1))


def _cols_joined(t):
    return jnp.concatenate([t[j] for j in range(4)], axis=1)


def _cols_split(t):
    n = t.shape[1] // 4
    return jnp.stack([t[:, j * n:(j + 1) * n] for j in range(4)])


def _rows_joined(t):
    return t.reshape(4 * t.shape[1], t.shape[2])


def _rows_split(t):
    return t.reshape(4, t.shape[0] // 4, t.shape[1])


def _local_step(x, tgt, ada, wts):
    s = x.shape[0]
    sh1, sc1, gt1, sh2, sc2, gt2 = ada
    br = 256
    grp = lax.broadcasted_iota(jnp.int32, (D, 128), 0) // 64 == lax.broadcasted_iota(jnp.int32, (D, 128), 1)
    e = grp.astype(F32)
    et = e.T
    w_in = _cols_joined(wts["w_in"])
    w_att = w_in[:, :N_ATT]
    w_rw = jnp.pad(w_in[:, N_ATT:N_ATT + N_RW], ((0, 0), (0, N_RWP - N_RW)))
    w_gate = w_in[:, N_ATT + N_RW:]
    w_up, w_ao = wts["w_up"], wts["w_att_out"]
    w_down, w_o, w_ro = _rows_joined(wts["w_down"]), _rows_joined(wts["w_o"]), _rows_joined(wts["w_rwkv_out"])
    mu = jnp.pad(wts["mu_shift"], ((0, 0), (0, N_RWP - N_RW)))
    wl = jnp.zeros((N_LORA, 3 * D), F32)
    wl = wl.at[0:64, 0:D].set(_cols_joined(wts["w2"]).astype(F32))
    wl = wl.at[64:128, D:2 * D].set(_cols_joined(wts["a2"]).astype(F32))
    wl = wl.at[128:288, 2 * D:3 * D].set(_cols_joined(wts["g2"]).astype(F32))
    bga, bgr = wts["b_gate"][:, :D], wts["b_gate"][:, D:]
    rk = wts["r_k"]

    pre1_c = [wts["norm1_w"], sc1, sh1]
    (h1,) = _rows_fwd(_f_pre, [(x, D, 0)], pre1_c, [(D, BF16), None], name="pre1_fwd", br=br)
    att_in = _mm(h1, w_att, name="mm_att_in")
    z = _mm(h1, w_rw, name="mm_rw_in")
    gate_in = _mm(h1, w_gate, name="mm_gate_in")
    att_o, att_l = [], []
    for g, (_, dil) in enumerate(ATT_PATTERNS):
        o, l = _att_fwd(att_in, g, dil)
        att_o.append(o)
        att_l.append(l)
    comb_rows = [(t, ATT_WIDTH, 0) for t in att_o + att_l]
    (att,) = _rows_fwd(_f_comb, comb_rows, [], [(ATT_WIDTH, BF16)], name="comb_fwd", br=br)
    y_att = _mm(att, w_ao, b_chip=True, name="mm_att_out")
    zs = _shift_fwd(z, mu)
    rwpre_c = [wts["w0"], wts["a0"], wts["k_k"], wts["k_a"], wl, e, et]
    lw, km, aa, bb, gg = _rows_fwd(_f_rwpre, [(zs, N_RWP, 0)], rwpre_c,
                                   [None, (D, F32), (D, F32), None, (D, F32), (D, F32), (D, F32)],
                                   name="rwpre_fwd", br=br)
    y_raw, s0s = _scan_fwd(zs, lw, km, aa, bb)
    post_rows = [(y_raw, D, 0), (zs, D, 0), (zs, D, 2), (km, D, 0), (gg, D, 0)]
    post_c = [wts["lnx_w"], wts["lnx_b"], rk, e, et]
    (rw_out,) = _rows_fwd(_f_rwpost, post_rows, post_c, [(D, BF16)], name="rwpost_fwd", br=br)
    y_rw = _mm(rw_out, w_ro, name="mm_rw_out")
    mix_rows = [(gate_in, D, 0), (gate_in, D, 1), (y_att, D, 0), (y_rw, D, 0)]
    (mix,) = _rows_fwd(_f_mix, mix_rows, [bga, bgr], [(D, BF16)], name="mix_fwd", br=br)
    o = _mm(mix, w_o, name="mm_o")
    pre2_c = [gt1, wts["norm2_w"], sc2, sh2]
    x1, h2 = _rows_fwd(_f_pre2, [(x, D, 0), (o, D, 0)], pre2_c, [(D, F32), (D, BF16)], name="pre2_fwd", br=br)
    u = _mm(h2, w_up, b_chip=True, name="mm_up")
    act = _conv_fwd(u, wts["conv_w"], wts["conv_b"])
    f = _mm(act, w_down, name="mm_down")
    fin_rows = [(x1, D, 0), (f, D, 0), (tgt, D, 0)]
    fin_c = [gt2, wts["norm_f_w"]]

    def fin_fwd(*a):
        (l,) = _f_fin(*a)
        return (jnp.broadcast_to(jnp.sum(l, axis=0, keepdims=True), (8, 128)),)

    (loss_acc,) = _rows_fwd(fin_fwd, fin_rows, fin_c, [], name="fin_fwd", br=br, acc_shape=(8, 128))
    loss = loss_acc[0, 0]

    gw = {}
    dx1a, df, d_gt2, gw["norm_f_w"] = _rows_bwd(
        _f_fin, fin_rows, fin_c, [[]], wrt_rows=[0, 1], wrt_consts=[0, 1], drow_dtypes=[F32, BF16],
        name="fin_bwd", br=br, unit_cot=True)
    dact = _mm(df, w_down, tb=True, name="mm_dact")
    gw["w_down"] = _rows_split(_mm(act, df, ta=True, name="mm_dw_down"))
    du, gw["conv_w"], gw["conv_b"] = _conv_bwd(u, wts["conv_w"], wts["conv_b"], dact)
    dh2 = _mm(du, w_up, tb=True, b_chip=True, name="mm_dh2")
    gw["w_up"] = _mm(h2, du, ta=True, out_chip=True, name="mm_dw_up")
    dxa, do, d_gt1, gw["norm2_w"], d_sc2, d_sh2 = _rows_bwd(
        _f_pre2, [(x, D, 0), (o, D, 0)], pre2_c, [[(dx1a, D, 0)], [(dh2, D, 0)]], wrt_rows=[0, 1],
        wrt_consts=[0, 1, 2, 3], drow_dtypes=[F32, BF16], name="pre2_bwd", br=br)
    dmix = _mm(do, w_o, tb=True, name="mm_dmix")
    gw["w_o"] = _rows_split(_mm(mix, do, ta=True, name="mm_dw_o"))
    dga, dgr, dya, dyr, d_bga, d_bgr = _rows_bwd(
        _f_mix, mix_rows, [bga, bgr], [[(dmix, D, 0)]], wrt_rows=[0, 1, 2, 3], wrt_consts=[0, 1],
        drow_dtypes=[BF16] * 4, name="mix_bwd", br=br)
    gw["b_gate"] = jnp.concatenate([d_bga, d_bgr], axis=1)
    datt = _mm(dya, w_ao, tb=True, b_chip=True, name="mm_datt")
    gw["w_att_out"] = _mm(att, dya, ta=True, out_chip=True, name="mm_dw_att_out")
    drw = _mm(dyr, w_ro, tb=True, name="mm_drw")
    gw["w_rwkv_out"] = _rows_split(_mm(rw_out, dyr, ta=True, name="mm_dw_rw_out"))
    dcomb = _rows_bwd(_f_comb, comb_rows, [], [[(datt, ATT_WIDTH, 0)]], wrt_rows=list(range(6)), wrt_consts=[],
                      drow_dtypes=[F32] * 6, name="comb_bwd", br=br)
    datt_in = []
    for g, (_, dil) in enumerate(ATT_PATTERNS):
        datt_in += _att_bwd(att_in, g, dil, dcomb[g], dcomb[3 + g])
    datt_in = jnp.concatenate(datt_in, axis=1)
    dy_raw, dr_p, dv_p, dkm_p, dgg, gw["lnx_w"], gw["lnx_b"], gw["r_k"] = _rows_bwd(
        _f_rwpost, post_rows, post_c, [[(drw, D, 0)]], wrt_rows=[0, 1, 2, 3, 4], wrt_consts=[0, 1, 2],
        drow_dtypes=[F32] * 5, name="rwpost_bwd", br=br)
    dr_s, dlw, dkm_s, dv_s, daa, dbb = _scan_bwd(zs, lw, km, aa, bb, s0s, dy_raw)
    pre_cots = [[(dr_p, D, 0), (dr_s, D, 0)], [(dlw, D, 0)], [(dkm_p, D, 0), (dkm_s, D, 0)],
                [(dv_p, D, 0), (dv_s, D, 0)], [(daa, D, 0)], [(dbb, D, 0)], [(dgg, D, 0)]]
    dzs, gw["w0"], gw["a0"], gw["k_k"], gw["k_a"], dwl = _rows_bwd(
        _f_rwpre, [(zs, N_RWP, 0)], rwpre_c, pre_cots, wrt_rows=[0], wrt_consts=[0, 1, 2, 3, 4], drow_dtypes=[F32],
        name="rwpre_bwd", br=128)
    gw["w2"], gw["a2"] = _cols_split(dwl[0:64, 0:D]), _cols_split(dwl[64:128, D:2 * D])
    gw["g2"] = _cols_split(dwl[128:288, 2 * D:3 * D])
    dz, dmu = _shift_bwd(z, mu, dzs)
    gw["mu_shift"] = dmu[:, :N_RW]
    dgate = jnp.concatenate([dga, dgr], axis=1)
    dh1 = _mm(datt_in, w_att, tb=True, name="mm_dh1_att")
    dh1 = _mm(dz, w_rw, tb=True, add=dh1, name="mm_dh1_rw")
    dh1 = _mm(dgate, w_gate, tb=True, add=dh1, name="mm_dh1_gate")
    gw["w_in"] = _cols_split(jnp.concatenate([_mm(h1, datt_in, ta=True, name="mm_dw_att"),
                                              _mm(h1, dz, ta=True, name="mm_dw_rw")[:, :N_RW],
                                              _mm(h1, dgate, ta=True, name="mm_dw_gate")], axis=1))
    grad_x, gw["norm1_w"], d_sc1, d_sh1 = _rows_bwd(
        _f_pre, [(x, D, 0)], pre1_c, [[(dh1, D, 0)], [(dxa, D, 0)]], wrt_rows=[0], wrt_consts=[0, 1, 2],
        drow_dtypes=[F32], name="pre1_bwd", br=br)
    return loss, grad_x, (d_sh1, d_sc1, d_gt1, d_sh2, d_sc2, d_gt2), gw


_SMALL = ("b_ada", "norm1_w", "b_gate", "mu_shift", "w0", "a0", "k_k", "k_a", "r_k", "lnx_w", "lnx_b", "norm2_w",
          "conv_b", "norm_f_w")
_NAMES = ("w_ada", "b_ada", "norm1_w", "w_in", "b_gate", "mu_shift", "w0", "w2", "a0", "a2", "g2", "k_k", "k_a", "r_k",
          "lnx_w", "lnx_b", "w_att_out", "w_rwkv_out", "w_o", "norm2_w", "w_up", "conv_w", "conv_b", "w_down",
          "norm_f_w")


def kernel(x, c, w_ada, b_ada, norm1_w, w_in, b_gate, mu_shift, w0, w2, a0, a2, g2, k_k, k_a, r_k, lnx_w, lnx_b, w_att_out, w_rwkv_out, w_o, norm2_w, w_up, conv_w, conv_b, w_down, norm_f_w, loss_target, m_w_ada, m_b_ada, m_norm1_w, m_w_in, m_b_gate, m_mu_shift, m_w0, m_w2, m_a0, m_a2, m_g2, m_k_k, m_k_a, m_r_k, m_lnx_w, m_lnx_b, m_w_att_out, m_w_rwkv_out, m_w_o, m_norm2_w, m_w_up, m_conv_w, m_conv_b, m_w_down, m_norm_f_w, v_w_ada, v_b_ada, v_norm1_w, v_w_in, v_b_gate, v_mu_shift, v_w0, v_w2, v_a0, v_a2, v_g2, v_k_k, v_k_a, v_r_k, v_lnx_w, v_lnx_b, v_w_att_out, v_w_rwkv_out, v_w_o, v_norm2_w, v_w_up, v_conv_w, v_conv_b, v_w_down, v_norm_f_w):
    args = dict(locals())
    p, pm, pv = {}, {}, {}
    for name in _NAMES:
        for dst, key in ((p, name), (pm, "m_" + name), (pv, "v_" + name)):
            t = args[key]
            dst[name] = t.reshape(1, -1) if name in ("r_k", "norm_f_w") else t.reshape(t.shape[-2], t.shape[-1])
    xi, yi, ci = _me()
    chip = 2 * xi + yi
    dev = 4 * xi + 2 * yi + ci
    x2, tgt = x[0], loss_target[0]

    n_cw = 3 * (2 * D_FF // 4)
    vec = jnp.concatenate([c.reshape(-1), p["conv_w"].reshape(-1), jnp.zeros((8 * D - D - n_cw,), F32)]).reshape(8, D)
    g0 = _allgather8(vec, "gather_c").reshape(8, 8 * D)
    c_all = g0[:, :D]
    conv_w_full = jnp.concatenate([g0[2 * j, D:D + n_cw].reshape(3, -1) for j in range(4)], axis=1)
    n_ada = 6 * D // 4
    b_ada_sh = lax.dynamic_slice(p["b_ada"], (0, chip * n_ada), (1, n_ada))
    ada_sh = _ada_fwd(c_all, p["w_ada"], b_ada_sh)
    ga = _allgather8(ada_sh, "gather_ada")
    ada_all = jnp.concatenate([ga[2 * j] for j in range(4)], axis=1)
    ada_row = lax.dynamic_slice(ada_all, (dev, 0), (1, 6 * D))
    ada = [ada_row[:, j * D:(j + 1) * D] for j in range(6)]

    big = [n for n, _ in _BIG]
    wts = dict(zip(big, _gather_weights([p[n].astype(BF16) for n in big], "gather_w")))
    for n in _SMALL:
        wts[n] = p[n]
    wts["conv_w"] = conv_w_full

    loss_part, grad_x, d_ada, gw = _local_step(x2, tgt, ada, wts)

    small = [jnp.concatenate(d_ada, axis=1)] + [gw[n] for n in _SMALL[1:]] + [gw["conv_w"], loss_part.reshape(1, 1)]
    sizes = [t.size for t in small]
    flat = jnp.concatenate([t.reshape(-1) for t in small])
    npad = (-flat.shape[0]) % (8 * D)
    srows = (flat.shape[0] + npad) // D
    flat = jnp.concatenate([flat, jnp.zeros((npad,), F32)]).reshape(srows, D)
    parts = _allgather8(flat, "gather_small")
    tot = _sum_lead(parts, "sum_small").reshape(-1)
    pieces, pos = [], 0
    for sz in sizes:
        pieces.append(tot[pos:pos + sz])
        pos += sz
    grads = {}
    for n, piece in zip(_SMALL, pieces[:len(_SMALL)]):
        grads[n] = piece.reshape(p[n].shape)
    conv_w_grad = pieces[len(_SMALL)].reshape(3, 2 * D_FF)
    grads["conv_w"] = lax.dynamic_slice(conv_w_grad, (0, chip * (n_cw // 3)), (3, n_cw // 3))
    loss = pieces[-1][0]
    d_ada_all = parts[:, :6].reshape(8, 6 * D)
    grads["w_ada"] = _ada_bwd(c_all.T, lax.dynamic_slice(d_ada_all, (0, chip * n_ada), (8, n_ada)))

    core = ci.reshape(1).astype(jnp.int32)
    recv = _reduce_sibling([gw[n] for n in big], "reduce_sib")
    chip_part = [_half_sum(lambda a, b: a + b, [gw[n]], [r], False, BF16, core, "reduce_add2_" + n)
                 for n, r in zip(big, recv)]
    slots = _reduce_chips(chip_part, "reduce_chips")
    reds = [_half_sum(lambda t: t[0] + t[1] + t[2] + t[3], [], [t], True, F32, core, "reduce_add4_" + n)
            for n, t in zip(big, slots)]
    for n, g in zip(big, _reduce_finish(reds, "reduce_sib2")):
        grads[n] = g

    outs_g, outs_d, outs_m, outs_v = [], [], [], []
    for name in _NAMES:
        g = grads[name]
        d, m, v = _adamw(p[name], g, pm[name], pv[name], "adamw_" + name)
        shape = args[name].shape
        outs_g.append(g.reshape(shape))
        outs_d.append(d.reshape(shape))
        outs_m.append(m.reshape(shape))
        outs_v.append(v.reshape(shape))
    return (loss, grad_x.reshape(x.shape), *outs_g, *outs_d, *outs_m, *outs_v)
```

```python
import functools

import jax
import jax.numpy as jnp
from jax import lax
from jax.experimental import pallas as pl
from jax.experimental.pallas import tpu as pltpu

F32 = jnp.float32
BF16 = jnp.bfloat16
HI = lax.Precision.HIGHEST
MESH = pl.DeviceIdType.MESH

D = 1024
ATT_PATTERNS = ((128, 1), (512, 4), (2048, 16))
ATT_BLOCK = 128
ATT_WIDTH = 512
N_ATT = 3 * 3 * ATT_WIDTH
N_RW = 3 * D + 64 + 64 + 160
N_RWP = 3456
N_LORA = N_RWP - 3 * D
N_GATE = 2 * D
D_FF = 2816
RMS_EPS = 1e-6
GN_EPS = 64e-5
SCAN_CHUNK = 64
SCAN_PAIRS = 8
NEG = -1e30
VMEM_LIMIT = 48 * 1024 * 1024

ADAM_LR, ADAM_B1, ADAM_B2, ADAM_EPS, ADAM_WD, ADAM_STEP = 0.001, 0.9, 0.999, 1e-08, 0.01, 10


def _pcall(body, **kw):
    return pl.pallas_call(body, **kw)


def _cparams(sem):
    return pltpu.CompilerParams(dimension_semantics=sem, vmem_limit_bytes=VMEM_LIMIT)


def _div(n, pref, mult):
    best = None
    d = mult
    while d <= min(n, pref):
        if n % d == 0:
            best = d
        d += mult
    return best if best else n


def _dg(a, b, ca, cb):
    return lax.dot_general(a.astype(BF16), b.astype(BF16), (((ca,), (cb,)), ((), ())), preferred_element_type=F32)


@jax.custom_vjp
def _nn(a, b):
    return _dg(a, b, 1, 0)


@jax.custom_vjp
def _nt(a, b):
    return _dg(a, b, 1, 1)


@jax.custom_vjp
def _tn(a, b):
    return _dg(a, b, 0, 0)


_nn.defvjp(lambda a, b: (_nn(a, b), (a, b)), lambda res, g: (_nt(g, res[1]), _tn(res[0], g)))
_nt.defvjp(lambda a, b: (_nt(a, b), (a, b)), lambda res, g: (_nn(g, res[1]), _tn(g, res[0])))
_tn.defvjp(lambda a, b: (_tn(a, b), (a, b)), lambda res, g: (_nt(res[1], g), _nn(res[0], g)))


def _bdg(a, b, ca, cb):
    return lax.dot_general(a.astype(BF16), b.astype(BF16), (((ca,), (cb,)), ((0,), (0,))), preferred_element_type=F32)


@jax.custom_vjp
def _bnn(a, b):
    return _bdg(a, b, 2, 1)


@jax.custom_vjp
def _bnt(a, b):
    return _bdg(a, b, 2, 2)


@jax.custom_vjp
def _btn(a, b):
    return _bdg(a, b, 1, 1)


_bnn.defvjp(lambda a, b: (_bnn(a, b), (a, b)), lambda res, g: (_bnt(g, res[1]), _btn(res[0], g)))
_bnt.defvjp(lambda a, b: (_bnt(a, b), (a, b)), lambda res, g: (_bnn(g, res[1]), _btn(g, res[0])))
_btn.defvjp(lambda a, b: (_btn(a, b), (a, b)), lambda res, g: (_bnt(res[1], g), _bnn(res[0], g)))


def _split2(x):
    hi = x.astype(BF16)
    lo = (x - hi.astype(F32)).astype(BF16)
    return hi, lo


def _hsum_impl(x, e, et):
    eb, etb = e.astype(BF16), et.astype(BF16)
    hi, lo = _split2(x)
    s = jnp.dot(hi, eb, preferred_element_type=F32) + jnp.dot(lo, eb, preferred_element_type=F32)
    shi, slo = _split2(s)
    return jnp.dot(shi, etb, preferred_element_type=F32) + jnp.dot(slo, etb, preferred_element_type=F32)


@jax.custom_vjp
def _hsum(x, e, et):
    return _hsum_impl(x, e, et)


_hsum.defvjp(lambda x, e, et: (_hsum_impl(x, e, et), (e, et)),
             lambda res, g: (_hsum_impl(g, res[0], res[1]), jnp.zeros_like(res[0]), jnp.zeros_like(res[1])))


def _mm(a, b, *, ta=False, tb=False, out_dtype=F32, add=None, b_chip=False, out_chip=False, name):
    if ta:
        kdim, m = a.shape
    else:
        m, kdim = a.shape
    if b_chip:
        n = b.shape[1] if tb else 4 * b.shape[2]
    else:
        n = b.shape[0] if tb else b.shape[1]
    tm, tn, tk = _div(m, 1536, 128), _div(n, 1536, 128), _div(kdim, 1408, 128)
    if b_chip and tb:
        tk = kdim // 4
    if (b_chip and not tb) or out_chip:
        tn = n // 4
    nk = kdim // tk
    ca, cb = (0 if ta else 1), (1 if tb else 0)

    def body(*refs):
        a_ref, b_ref = refs[0], refs[1]
        add_ref = None if add is None else refs[2]
        o_ref = refs[2 if add is None else 3]
        part = lax.dot_general(a_ref[...], b_ref[...], (((ca,), (cb,)), ((), ())), preferred_element_type=F32)

        def finish(r):
            if add_ref is not None:
                r = r + add_ref[...]
            o_ref[...] = r.astype(o_ref.dtype)

        if nk == 1:
            finish(part)
            return
        acc = refs[-1]
        k = pl.program_id(2)

        @pl.when(k == 0)
        def _():
            acc[...] = part

        @pl.when(k > 0)
        def _():
            acc[...] += part

        @pl.when(k == nk - 1)
        def _():
            finish(acc[...])

    a_spec = pl.BlockSpec((tk, tm), lambda i, j, k: (k, i)) if ta else pl.BlockSpec((tm, tk), lambda i, j, k: (i, k))
    if b_chip:
        b_spec = (pl.BlockSpec((None, tn, tk), lambda i, j, k: (k, j, 0)) if tb
                  else pl.BlockSpec((None, tk, tn), lambda i, j, k: (j, k, 0)))
    else:
        b_spec = pl.BlockSpec((tn, tk), lambda i, j, k: (j, k)) if tb else pl.BlockSpec((tk, tn), lambda i, j, k: (k, j))
    in_specs = [a_spec, b_spec]
    args = [a, b]
    if add is not None:
        in_specs.append(pl.BlockSpec((tm, tn), lambda i, j, k: (i, j)))
        args.append(add)
    if out_chip:
        out_spec = pl.BlockSpec((None, tm, tn), lambda i, j, k: (j, i, 0))
        out_shape = jax.ShapeDtypeStruct((4, m, tn), out_dtype)
    else:
        out_spec = pl.BlockSpec((tm, tn), lambda i, j, k: (i, j))
        out_shape = jax.ShapeDtypeStruct((m, n), out_dtype)
    return _pcall(
        body, name=name, grid=(m // tm, n // tn, nk), in_specs=in_specs, out_specs=out_spec, out_shape=out_shape,
        scratch_shapes=[] if nk == 1 else [pltpu.VMEM((tm, tn), F32)],
        compiler_params=_cparams(("parallel", "parallel", "arbitrary")),
    )(*args)


def _row_spec(br, w, cb):
    return pl.BlockSpec((br, w), lambda i: (i, cb))


def _const_spec(shape):
    return pl.BlockSpec(shape, lambda i: (0,) * len(shape))


def _rows_fwd(fn, rows, consts, outs, *, name, br, acc_shape=None):
    s = rows[0][0].shape[0]
    nr, nc = len(rows), len(consts)
    kept = [k for k, o in enumerate(outs) if o is not None]

    def body(*refs):
        xs = [r[...].astype(F32) for r in refs[:nr]]
        cs = [c[...] for c in refs[nr:nr + nc]]
        res = fn(*xs, *cs)
        orefs = refs[nr + nc:]
        for j, k in enumerate(kept):
            orefs[j][...] = res[k].astype(orefs[j].dtype)
        if acc_shape is not None:
            acc_ref = orefs[len(kept)]

            @pl.when(pl.program_id(0) == 0)
            def _():
                acc_ref[...] = jnp.zeros_like(acc_ref)

            acc_ref[...] += res[len(outs)]

    in_specs = [_row_spec(br, w, cb) for (_, w, cb) in rows] + [_const_spec(c.shape) for c in consts]
    out_specs = [_row_spec(br, outs[k][0], 0) for k in kept]
    out_shape = [jax.ShapeDtypeStruct((s, outs[k][0]), outs[k][1]) for k in kept]
    if acc_shape is not None:
        out_specs.append(_const_spec(acc_shape))
        out_shape.append(jax.ShapeDtypeStruct(acc_shape, F32))
    return _pcall(
        body, name=name, grid=(s // br,), in_specs=in_specs, out_specs=out_specs, out_shape=out_shape,
        compiler_params=_cparams(("arbitrary",)),
    )(*[r[0] for r in rows], *consts)


def _rows_bwd(fn, rows, consts, cots, *, wrt_rows, wrt_consts, drow_dtypes, name, br, unit_cot=False):
    s = rows[0][0].shape[0]
    nr, nc = len(rows), len(consts)
    flat_cots = [c for lst in cots for c in lst]
    ncot = len(flat_cots)

    def body(*refs):
        xs = [r[...].astype(F32) for r in refs[:nr]]
        cs = [c[...] for c in refs[nr:nr + nc]]
        cvals = [c[...].astype(F32) for c in refs[nr + nc:nr + nc + ncot]]
        orefs = refs[nr + nc + ncot:]

        def g(*d):
            xs2, cs2 = list(xs), list(cs)
            for j, k in enumerate(wrt_rows):
                xs2[k] = d[j]
            for j, k in enumerate(wrt_consts):
                cs2[k] = d[len(wrt_rows) + j]
            return tuple(fn(*xs2, *cs2))

        prim = [xs[k] for k in wrt_rows] + [cs[k] for k in wrt_consts]
        outs, vjp = jax.vjp(g, *prim)
        ct = []
        pos = 0
        for o, lst in zip(outs, cots):
            if unit_cot:
                ct.append(jnp.ones_like(o))
                continue
            acc = jnp.zeros_like(o)
            for _ in lst:
                acc = acc + cvals[pos]
                pos += 1
            ct.append(acc)
        grads = vjp(tuple(ct))
        for j in range(len(wrt_rows)):
            orefs[j][...] = grads[j].astype(orefs[j].dtype)

        @pl.when(pl.program_id(0) == 0)
        def _():
            for j in range(len(wrt_consts)):
                oref = orefs[len(wrt_rows) + j]
                oref[...] = jnp.zeros_like(oref)

        for j in range(len(wrt_consts)):
            orefs[len(wrt_rows) + j][...] += grads[len(wrt_rows) + j]

    in_specs = ([_row_spec(br, w, cb) for (_, w, cb) in rows] + [_const_spec(c.shape) for c in consts]
                + [_row_spec(br, w, cb) for (_, w, cb) in flat_cots])
    out_specs = [_row_spec(br, rows[k][1], 0) for k in wrt_rows] + [_const_spec(consts[k].shape) for k in wrt_consts]
    out_shape = ([jax.ShapeDtypeStruct((s, rows[k][1]), dt) for k, dt in zip(wrt_rows, drow_dtypes)]
                 + [jax.ShapeDtypeStruct(consts[k].shape, F32) for k in wrt_consts])
    return _pcall(
        body, name=name, grid=(s // br,), in_specs=in_specs, out_specs=out_specs, out_shape=out_shape,
        compiler_params=_cparams(("arbitrary",)),
    )(*[r[0] for r in rows], *consts, *[c[0] for c in flat_cots])


def _rms(x, w):
    return x * lax.rsqrt(jnp.mean(x * x, axis=-1, keepdims=True) + RMS_EPS) * w


def _softplus(x):
    return jnp.maximum(x, 0.0) + jnp.log(1.0 + jnp.exp(-jnp.abs(x)))


def _f_pre(x, nw, sc, sh):
    return _rms(x, nw) * (1.0 + sc) + sh, x


def _f_pre2(x, o, gt, nw, sc, sh):
    x1 = x + gt * o
    return x1, _rms(x1, nw) * (1.0 + sc) + sh


def _f_fin(x1, f, tgt, gt, nfw):
    y = _rms(x1 + gt * f, nfw)
    return (0.5 * jnp.mean(jnp.square(y - tgt), axis=-1, keepdims=True),)


def _f_comb(o1, o2, o3, l1, l2, l3):
    m = lax.stop_gradient(jnp.maximum(jnp.maximum(l1, l2), l3))
    e1, e2, e3 = jnp.exp(l1 - m), jnp.exp(l2 - m), jnp.exp(l3 - m)
    return ((e1 * o1 + e2 * o2 + e3 * o3) / (e1 + e2 + e3),)


def _f_rwpre(zs, w0, a0, k_k, k_a, wl, e, et):
    r, k, v, zl = zs[:, 0:D], zs[:, D:2 * D], zs[:, 2 * D:3 * D], zs[:, 3 * D:N_RWP]
    lane = lax.broadcasted_iota(jnp.int32, zl.shape, 1)
    t = jnp.where(lane < 64, jnp.tanh(zl), jnp.where(lane < 128, zl, jnp.where(lane < 288, jax.nn.sigmoid(zl), 0.0)))
    lo = _nn(t, wl)
    w_log = -_softplus(-(w0 + lo[:, 0:D])) - 0.5
    lw = -jnp.exp(w_log)
    a = jax.nn.sigmoid(a0 + lo[:, D:2 * D])
    g = lo[:, 2 * D:3 * D]
    k_mod = k * (1.0 + (a - 1.0) * k_a)
    kk = k * k_k
    kk = kk / jnp.maximum(jnp.sqrt(_hsum(kk * kk, e, et)), 1e-12)
    return r, lw, k_mod, v, -kk, kk * a, g


def _f_rwpost(y, r, v, k_mod, g, lnx_w, lnx_b, r_k, e, et):
    mean = _hsum(y, e, et) * (1.0 / 64)
    yc = y - mean
    var = _hsum(yc * yc, e, et) * (1.0 / 64)
    yn = yc * lax.rsqrt(var + GN_EPS) * lnx_w + lnx_b
    bonus = _hsum(r * k_mod * r_k, e, et) * v
    return ((yn + bonus) * g,)


def _f_mix(gia, gir, ya, yr, bga, bgr):
    return (jax.nn.sigmoid(gia + bga) * ya + jax.nn.sigmoid(gir + bgr) * yr,)


def _f_adamw(w, g, m, v):
    m = ADAM_B1 * m + (1.0 - ADAM_B1) * g
    v = ADAM_B2 * v + (1.0 - ADAM_B2) * jnp.square(g)
    m_hat = m / (1.0 - ADAM_B1 ** ADAM_STEP)
    v_hat = v / (1.0 - ADAM_B2 ** ADAM_STEP)
    return -ADAM_LR * (m_hat / (jnp.sqrt(v_hat) + ADAM_EPS) + ADAM_WD * w), m, v


def _down(x, k):
    row = lax.broadcasted_iota(jnp.int32, x.shape, 0)
    return jnp.where(row < k, 0.0, pltpu.roll(x, k, 0))


def _up(x, k):
    n = x.shape[0]
    row = lax.broadcasted_iota(jnp.int32, x.shape, 0)
    return jnp.where(row >= n - k, 0.0, pltpu.roll(x, n - k, 0))


def _col_spec(s, w, off=0):
    return pl.BlockSpec((s, w), lambda j: (0, j + off))


def _shift_fwd(z, mu):
    s, n = z.shape

    def body(z_ref, mu_ref, o_ref):
        zz = z_ref[...]
        o_ref[...] = zz + (_down(zz, 1) - zz) * mu_ref[...]

    return _pcall(
        body, name="shift_fwd", grid=(n // 128,), in_specs=[_col_spec(s, 128), _col_spec(1, 128)],
        out_specs=_col_spec(s, 128), out_shape=jax.ShapeDtypeStruct((s, n), F32),
        compiler_params=_cparams(("parallel",)),
    )(z, mu)


def _shift_bwd(z, mu, dzs):
    s, n = z.shape

    def body(z_ref, mu_ref, d_ref, dz_ref, dmu_ref):
        zz, d, m = z_ref[...], d_ref[...], mu_ref[...]
        dm = d * m
        dz_ref[...] = (d - dm + _up(dm, 1)).astype(dz_ref.dtype)
        dmu_ref[...] = jnp.sum(d * (_down(zz, 1) - zz), axis=0, keepdims=True)

    return _pcall(
        body, name="shift_bwd", grid=(n // 128,), in_specs=[_col_spec(s, 128), _col_spec(1, 128), _col_spec(s, 128)],
        out_specs=[_col_spec(s, 128), _col_spec(1, 128)],
        out_shape=[jax.ShapeDtypeStruct((s, n), BF16), jax.ShapeDtypeStruct((1, n), F32)],
        compiler_params=_cparams(("parallel",)),
    )(z, mu, dzs)


def _conv3(x, w_ref, b_ref):
    return b_ref[...] + w_ref[0:1, :] * _down(x, 2) + w_ref[1:2, :] * _down(x, 1) + w_ref[2:3, :] * x


def _conv_fwd(u, cw, cb):
    s = u.shape[0]
    nb = D_FF // 128

    def body(ug_ref, uv_ref, wg_ref, wv_ref, bg_ref, bv_ref, o_ref):
        gate = _conv3(ug_ref[...], wg_ref, bg_ref)
        val = _conv3(uv_ref[...], wv_ref, bv_ref)
        o_ref[...] = (gate * jax.nn.sigmoid(gate) * val).astype(o_ref.dtype)

    return _pcall(
        body, name="conv_fwd", grid=(nb,),
        in_specs=[_col_spec(s, 128), _col_spec(s, 128, nb), _col_spec(3, 128), _col_spec(3, 128, nb),
                  _col_spec(1, 128), _col_spec(1, 128, nb)],
        out_specs=_col_spec(s, 128), out_shape=jax.ShapeDtypeStruct((s, D_FF), BF16),
        compiler_params=_cparams(("parallel",)),
    )(u, u, cw, cw, cb, cb)


def _conv_bwd(u, cw, cb, dact):
    s = u.shape[0]
    nb = D_FF // 128

    def half(x, d, w_ref, du_ref, dw_ref, db_ref):
        x1, x2 = _down(x, 1), _down(x, 2)
        du_ref[...] = (w_ref[2:3, :] * d + w_ref[1:2, :] * _up(d, 1) + w_ref[0:1, :] * _up(d, 2)).astype(du_ref.dtype)
        dw_ref[0:1, :] = jnp.sum(d * x2, axis=0, keepdims=True)
        dw_ref[1:2, :] = jnp.sum(d * x1, axis=0, keepdims=True)
        dw_ref[2:3, :] = jnp.sum(d * x, axis=0, keepdims=True)
        db_ref[...] = jnp.sum(d, axis=0, keepdims=True)

    def body(ug_ref, uv_ref, wg_ref, wv_ref, bg_ref, bv_ref, da_ref,
             dug_ref, duv_ref, dwg_ref, dwv_ref, dbg_ref, dbv_ref):
        ug, uv, da = ug_ref[...], uv_ref[...], da_ref[...]
        gate = _conv3(ug, wg_ref, bg_ref)
        val = _conv3(uv, wv_ref, bv_ref)
        sg = jax.nn.sigmoid(gate)
        dgate = da * val * sg * (1.0 + gate * (1.0 - sg))
        dval = da * gate * sg
        half(ug, dgate, wg_ref, dug_ref, dwg_ref, dbg_ref)
        half(uv, dval, wv_ref, duv_ref, dwv_ref, dbv_ref)

    dug, duv, dwg, dwv, dbg, dbv = _pcall(
        body, name="conv_bwd", grid=(nb,),
        in_specs=[_col_spec(s, 128), _col_spec(s, 128, nb), _col_spec(3, 128), _col_spec(3, 128, nb),
                  _col_spec(1, 128), _col_spec(1, 128, nb), _col_spec(s, 128)],
        out_specs=[_col_spec(s, 128), _col_spec(s, 128), _col_spec(3, 128), _col_spec(3, 128),
                   _col_spec(1, 128), _col_spec(1, 128)],
        out_shape=[jax.ShapeDtypeStruct((s, D_FF), BF16), jax.ShapeDtypeStruct((s, D_FF), BF16),
                   jax.ShapeDtypeStruct((3, D_FF), F32), jax.ShapeDtypeStruct((3, D_FF), F32),
                   jax.ShapeDtypeStruct((1, D_FF), F32), jax.ShapeDtypeStruct((1, D_FF), F32)],
        compiler_params=_cparams(("parallel",)),
    )(u, u, cw, cw, cb, cb, dact)
    return (jnp.concatenate([dug, duv], axis=1), jnp.concatenate([dwg, dwv], axis=1),
            jnp.concatenate([dbg, dbv], axis=1))


ATT_BATCH = 4


def _att_batch(q, kp, kc, vp, vc, first):
    ma = lax.broadcasted_iota(jnp.int32, (1, ATT_BLOCK, 128), 2) < 64
    qs = jnp.concatenate([jnp.where(ma, q, 0.0), jnp.where(ma, 0.0, q)], axis=1)
    qi = lax.broadcasted_iota(jnp.int32, (1, 2 * ATT_BLOCK, ATT_BLOCK), 1) & (ATT_BLOCK - 1)
    kj = lax.broadcasted_iota(jnp.int32, (1, 2 * ATT_BLOCK, ATT_BLOCK), 2)
    okp = kj >= qi + jnp.where(first, 2 * ATT_BLOCK, 0)
    okc = kj <= qi
    sp = jnp.where(okp, _bnt(qs, kp) * 0.125, NEG)
    sc = jnp.where(okc, _bnt(qs, kc) * 0.125, NEG)
    m = lax.stop_gradient(jnp.maximum(jnp.max(sp, axis=-1, keepdims=True), jnp.max(sc, axis=-1, keepdims=True)))
    pp, pc = jnp.exp(sp - m), jnp.exp(sc - m)
    den = jnp.sum(pp, axis=-1, keepdims=True) + jnp.sum(pc, axis=-1, keepdims=True)
    o_s = (_bnn(pp, vp) + _bnn(pc, vc)) / den
    l_s = jnp.broadcast_to(m + jnp.log(den), o_s.shape)
    return (jnp.where(ma, o_s[:, :ATT_BLOCK], o_s[:, ATT_BLOCK:]), jnp.where(ma, l_s[:, :ATT_BLOCK], l_s[:, ATT_BLOCK:]))


def _att_pairs_per_step(dil):
    return ATT_BATCH if dil == 1 else 1


def _att_specs(g, dil):
    rows, pp = ATT_BLOCK * dil, _att_pairs_per_step(dil)

    def cur(slot):
        return pl.BlockSpec((rows, 128 * pp), lambda n, p: (n, (g * 3 + slot) * (4 // pp) + p))

    def prev(slot):
        return pl.BlockSpec((rows, 128 * pp), lambda n, p: (jnp.maximum(n - 1, 0), (g * 3 + slot) * (4 // pp) + p))

    return [cur(0), prev(1), cur(1), prev(2), cur(2)]


def _att_out_spec(dil):
    return pl.BlockSpec((ATT_BLOCK * dil, 128 * _att_pairs_per_step(dil)), lambda n, p: (n, p))


def _att_grid(s, dil):
    return (s // (ATT_BLOCK * dil), 4 // _att_pairs_per_step(dil))


def _att_windows(i, dil):
    if dil == 1:
        return [(pl.ds(0, ATT_BLOCK), pl.ds(128 * j, 128)) for j in range(ATT_BATCH)]
    return [(pl.ds(i * ATT_BATCH + j, ATT_BLOCK, stride=dil), pl.ds(0, 128)) for j in range(ATT_BATCH)]


def _att_fwd(att_in, g, dil):
    s = att_in.shape[0]

    def body(q_ref, kp_ref, kc_ref, vp_ref, vc_ref, o_ref, l_ref):
        first = pl.program_id(0) == 0

        def one(i, carry):
            win = _att_windows(i, dil)
            vals = [jnp.stack([ref[w] for w in win]) for ref in (q_ref, kp_ref, kc_ref, vp_ref, vc_ref)]
            o, l = _att_batch(*vals, first)
            for j, w in enumerate(win):
                o_ref[w] = o[j]
                l_ref[w] = l[j]
            return carry

        lax.fori_loop(0, max(1, dil // ATT_BATCH), one, 0)

    return _pcall(
        body, name=f"att_fwd{g}", grid=_att_grid(s, dil), in_specs=_att_specs(g, dil),
        out_specs=[_att_out_spec(dil)] * 2, out_shape=[jax.ShapeDtypeStruct((s, ATT_WIDTH), F32)] * 2,
        compiler_params=_cparams(("parallel", "parallel")),
    )(att_in, att_in, att_in, att_in, att_in)


def _att_bwd(att_in, g, dil, do, dl):
    s = att_in.shape[0]
    nb = s // (ATT_BLOCK * dil)

    def body(q_ref, kp_ref, kc_ref, vp_ref, vc_ref, do_ref, dl_ref, dq_ref, dkp_ref, dkc_ref, dvp_ref, dvc_ref):
        first = pl.program_id(0) == 0

        def one(i, carry):
            win = _att_windows(i, dil)
            vals = [jnp.stack([ref[w] for w in win]) for ref in (q_ref, kp_ref, kc_ref, vp_ref, vc_ref)]
            _, vjp = jax.vjp(lambda *a: _att_batch(*a, first), *vals)
            grads = vjp((jnp.stack([do_ref[w] for w in win]), jnp.stack([dl_ref[w] for w in win])))
            for ref, gr in zip((dq_ref, dkp_ref, dkc_ref, dvp_ref, dvc_ref), grads):
                for j, w in enumerate(win):
                    ref[w] = gr[j]
            return carry

        lax.fori_loop(0, max(1, dil // ATT_BATCH), one, 0)

    dq, dkp, dkc, dvp, dvc = _pcall(
        body, name=f"att_bwd{g}", grid=_att_grid(s, dil), in_specs=_att_specs(g, dil) + [_att_out_spec(dil)] * 2,
        out_specs=[_att_out_spec(dil)] * 5, out_shape=[jax.ShapeDtypeStruct((s, ATT_WIDTH), F32)] * 5,
        compiler_params=_cparams(("parallel", "parallel")),
    )(att_in, att_in, att_in, att_in, att_in, do, dl)

    def cbody(dq_ref, dkc_ref, dkn_ref, dvc_ref, dvn_ref, oq_ref, ok_ref, ov_ref):
        has_next = pl.program_id(0) + 1 < nb
        oq_ref[...] = dq_ref[...].astype(BF16)
        ok_ref[...] = (dkc_ref[...] + jnp.where(has_next, dkn_ref[...], 0.0)).astype(BF16)
        ov_ref[...] = (dvc_ref[...] + jnp.where(has_next, dvn_ref[...], 0.0)).astype(BF16)

    cur = pl.BlockSpec((ATT_BLOCK * dil, 128), lambda n, p: (n, p))
    nxt = pl.BlockSpec((ATT_BLOCK * dil, 128), lambda n, p: (jnp.minimum(n + 1, nb - 1), p))
    return _pcall(
        cbody, name=f"att_bwd_sum{g}", grid=(nb, 4), in_specs=[cur, cur, nxt, cur, nxt], out_specs=[cur] * 3,
        out_shape=[jax.ShapeDtypeStruct((s, ATT_WIDTH), BF16)] * 3,
        compiler_params=_cparams(("parallel", "parallel")),
    )(dq, dkc, dkp, dvc, dvp)


def _scan_chunk(r, lw, k, v, a, b, s0):
    c = SCAN_CHUNK
    p = s0.shape[0]
    ri = lax.broadcasted_iota(jnp.int32, (c, c), 0)
    ci = lax.broadcasted_iota(jnp.int32, (c, c), 1)
    cum = jnp.dot((ci <= ri).astype(F32), lw, precision=HI, preferred_element_type=F32)
    tot = jnp.sum(lw, axis=0, keepdims=True)
    ma = (lax.broadcasted_iota(jnp.int32, (c, 128 * p), 1) & 127) < 64

    def pairs(x):
        return jnp.concatenate([x[None, :, 128 * j:128 * (j + 1)] for j in range(p)], axis=0)

    def stack(x):
        return jnp.concatenate([pairs(jnp.where(ma, x, 0.0)), pairs(jnp.where(ma, 0.0, x))], axis=1)

    einv, eend = jnp.exp(-cum), jnp.exp(tot - cum)
    ra, aa = stack(r * jnp.exp(cum)), stack(a * jnp.exp(cum - lw))
    bi, ki, be, ke, vs = stack(b * einv), stack(k * einv), stack(b * eend), stack(k * eend), stack(v)
    r2 = lax.broadcasted_iota(jnp.int32, (1, 2 * c, 2 * c), 1)
    c2 = lax.broadcasted_iota(jnp.int32, (1, 2 * c, 2 * c), 2)
    same = (r2 >= c) == (c2 >= c)
    strict = jnp.logical_and(same, c2 < r2)
    incl = jnp.logical_and(same, c2 <= r2)
    s0 = jnp.where(same, s0, 0.0)
    prod = _bnt(jnp.concatenate([aa, ra], axis=1), jnp.concatenate([bi, ki], axis=1))
    a_ab = jnp.where(strict, prod[:, :2 * c, :2 * c], 0.0)
    a_ak = jnp.where(strict, prod[:, :2 * c, 2 * c:], 0.0)
    a_rb = jnp.where(incl, prod[:, 2 * c:, :2 * c], 0.0)
    a_rk = jnp.where(incl, prod[:, 2 * c:, 2 * c:], 0.0)
    t = jnp.where(r2 == c2, 1.0, 0.0) + a_ab
    pw = a_ab
    for _ in range(5):
        pw = _bnn(pw, pw)
        t = t + _bnn(t, pw)
    u = _bnn(t, _bnt(aa, s0) + _bnn(a_ak, vs))
    uv = jnp.concatenate([u, vs], axis=1)
    ys = _bnt(ra, s0) + _bnn(jnp.concatenate([a_rb, a_rk], axis=2), uv)
    s1 = s0 * pairs(jnp.exp(tot)) + _btn(uv, jnp.concatenate([be, ke], axis=1))
    y3 = ys[:, :c] + ys[:, c:]
    return jnp.concatenate([y3[j] for j in range(p)], axis=1), s1


def _scan_specs(rev, n):
    def at(i):
        return n - 1 - i if rev else i

    def cm(cb):
        return pl.BlockSpec((SCAN_CHUNK, D), lambda i: (at(i), cb))

    return cm, pl.BlockSpec((1, SCAN_PAIRS, 128, 128), lambda i: (at(i), 0, 0, 0))


def _comm_phases(comm, refs, n):
    k = comm.n
    srcs, outs, sems = refs[:k], refs[k:2 * k], refs[2 * k:]
    i = pl.program_id(0)

    def before():
        @pl.when(i == 0)
        def _():
            comm.first(srcs, outs, sems)

    def after():
        if comm.mid is not None:
            @pl.when(i == (3 * n) // 4)
            def _():
                comm.mid(srcs, outs, sems)

        @pl.when(i == n - 1)
        def _():
            comm.last(srcs, outs, sems)

    return before, after


def _scan_fwd(zs, lw, km, aa, bb, comm):
    s = zs.shape[0]
    n = s // SCAN_CHUNK
    cm, st = _scan_specs(False, n)
    k = comm.n

    def body(*refs):
        r_ref, lw_ref, k_ref, v_ref, a_ref, b_ref = refs[:6]
        y_ref, s0_ref = refs[6 + k:8 + k]
        state = refs[8 + 2 * k]
        before, after = _comm_phases(comm, refs[6:6 + k] + refs[8 + k:8 + 2 * k] + refs[9 + 2 * k:], n)
        before()

        @pl.when(pl.program_id(0) == 0)
        def _():
            state[...] = jnp.zeros_like(state)

        s0 = state[...]
        s0_ref[0] = s0
        y, s1 = _scan_chunk(*[ref[...] for ref in (r_ref, lw_ref, k_ref, v_ref, a_ref, b_ref)], s0)
        y_ref[...] = y
        state[...] = s1
        after()

    res = _pcall(
        body, name="scan_fwd", grid=(n,), in_specs=[cm(0), cm(0), cm(0), cm(2), cm(0), cm(0)] + [_HBM] * k,
        out_specs=[cm(0), st] + [_HBM] * k,
        out_shape=[jax.ShapeDtypeStruct((s, D), F32), jax.ShapeDtypeStruct((n, 8, 128, 128), F32)] + comm.out_shape,
        scratch_shapes=[pltpu.VMEM((SCAN_PAIRS, 128, 128), F32)] + comm.sems,
        compiler_params=_cparams(("arbitrary",)),
    )(zs, lw, km, zs, aa, bb, *comm.ins)
    return res[0], res[1], res[2:]


def _scan_bwd(zs, lw, km, aa, bb, s0s, dy, comm):
    s = zs.shape[0]
    n = s // SCAN_CHUNK
    cm, st = _scan_specs(True, n)
    k = comm.n

    def body(*refs):
        r_ref, lw_ref, k_ref, v_ref, a_ref, b_ref, s0_ref, dy_ref = refs[:8]
        douts = refs[8 + k:14 + k]
        dstate = refs[14 + 2 * k]
        before, after = _comm_phases(comm, refs[8:8 + k] + refs[14 + k:14 + 2 * k] + refs[15 + 2 * k:], n)
        before()

        @pl.when(pl.program_id(0) == 0)
        def _():
            dstate[...] = jnp.zeros_like(dstate)

        prim = [ref[...] for ref in (r_ref, lw_ref, k_ref, v_ref, a_ref, b_ref)] + [s0_ref[0]]
        _, vjp = jax.vjp(_scan_chunk, *prim)
        grads = vjp((dy_ref[...], dstate[...]))
        for ref, gr in zip(douts, grads[:6]):
            ref[...] = gr
        dstate[...] = grads[6]
        after()

    res = _pcall(
        body, name="scan_bwd", grid=(n,),
        in_specs=[cm(0), cm(0), cm(0), cm(2), cm(0), cm(0), st, cm(0)] + [_HBM] * k,
        out_specs=[cm(0)] * 6 + [_HBM] * k, out_shape=[jax.ShapeDtypeStruct((s, D), F32)] * 6 + comm.out_shape,
        scratch_shapes=[pltpu.VMEM((SCAN_PAIRS, 128, 128), F32)] + comm.sems,
        compiler_params=_cparams(("arbitrary",)),
    )(zs, lw, km, zs, aa, bb, s0s, dy, *comm.ins)
    return res[:6], res[6:]


_HBM = pl.BlockSpec(memory_space=pltpu.HBM)


def _me():
    return lax.axis_index("x"), lax.axis_index("y"), lax.axis_index("c")


def _allgather8(src, name):
    def body(src_ref, out_ref, ssem, rsem, lsem):
        x, y, c = _me()
        me = 4 * x + 2 * y + c
        local = pltpu.make_async_copy(src_ref, out_ref.at[me], lsem)
        local.start()
        peers = []
        for k in range(1, 8):
            peers.append(((1 - x) if k & 4 else x, (1 - y) if k & 2 else y, (1 - c) if k & 1 else c))
        sends = []
        for k, peer in enumerate(peers):
            cp = pltpu.make_async_remote_copy(src_ref, out_ref.at[me], ssem.at[k], rsem.at[k], device_id=peer,
                                              device_id_type=MESH)
            cp.start()
            sends.append(cp)
        for k, (px, py, pc) in enumerate(peers):
            pltpu.make_async_remote_copy(src_ref, out_ref.at[4 * px + 2 * py + pc], ssem.at[k], rsem.at[k],
                                         device_id=(px, py, pc), device_id_type=MESH).wait_recv()
        for cp in sends:
            cp.wait_send()
        local.wait()

    return _pcall(
        body, name=name, in_specs=[_HBM], out_specs=_HBM, out_shape=jax.ShapeDtypeStruct((8,) + src.shape, src.dtype),
        scratch_shapes=[pltpu.SemaphoreType.DMA((7,)), pltpu.SemaphoreType.DMA((7,)), pltpu.SemaphoreType.DMA],
    )(src)


def _other_chips(x, y):
    return [(1 - x, y), (x, 1 - y), (1 - x, 1 - y)]


def _remote(src, dst, ssem, rsem, to):
    return pltpu.make_async_remote_copy(src, dst, ssem, rsem, device_id=to, device_id_type=MESH)


class _GatherWeights:
    def __init__(self, shards):
        self.ins = list(shards)
        n = self.n = len(shards)
        self.out_shape = [jax.ShapeDtypeStruct((4,) + t.shape, t.dtype) for t in shards]
        self.sems = [pltpu.SemaphoreType.DMA((6 * n,)), pltpu.SemaphoreType.DMA((6 * n,)),
                     pltpu.SemaphoreType.DMA((n,)), pltpu.SemaphoreType.DMA((n,))]

    def _copies(self, srcs, outs, sems):
        ssem, rsem, lsem, osem = sems
        x, y, c = _me()
        me = 2 * x + y
        own, ici, landed, passed, passed_in = [], [], [], [], []
        for a in range(self.n):
            h = self.ins[a].shape[0] // 2
            mine, other = pl.ds(c * h, h), pl.ds((1 - c) * h, h)
            own.append(_remote(srcs[a], outs[a].at[me], lsem.at[a], osem.at[a], (x, y, 1 - c)))
            for k, (px, py) in enumerate(_other_chips(x, y)):
                s1, r1, s2, r2 = ssem.at[6 * a + k], rsem.at[6 * a + k], ssem.at[6 * a + 3 + k], rsem.at[6 * a + 3 + k]
                got, got_sib = outs[a].at[2 * px + py, mine], outs[a].at[2 * px + py, other]
                ici.append(_remote(srcs[a].at[mine], outs[a].at[me, mine], s1, r1, (px, py, c)))
                landed.append(_remote(got, got, s1, r1, (px, py, c)))
                passed.append(_remote(got, got, s2, r2, (x, y, 1 - c)))
                passed_in.append(_remote(got_sib, got_sib, s2, r2, (x, y, 1 - c)))
        return own, ici, landed, passed, passed_in

    def first(self, srcs, outs, sems):
        own, ici, _, _, _ = self._copies(srcs, outs, sems)
        for cp in own + ici:
            cp.start()

    def mid(self, srcs, outs, sems):
        _, _, landed, passed, _ = self._copies(srcs, outs, sems)
        for arrived, onward in zip(landed, passed):
            arrived.wait_recv()
            onward.start()

    def last(self, srcs, outs, sems):
        own, ici, _, passed, passed_in = self._copies(srcs, outs, sems)
        for cp in passed_in:
            cp.wait_recv()
        for cp in ici + passed:
            cp.wait_send()
        for cp in own:
            cp.wait()


class _ScatterToChips:
    def __init__(self, parts):
        self.ins = list(parts)
        n = self.n = len(parts)
        self.out_shape = [jax.ShapeDtypeStruct(t.shape, t.dtype) for t in parts]
        self.sems = [pltpu.SemaphoreType.DMA((3 * n,)), pltpu.SemaphoreType.DMA((3 * n,)), pltpu.SemaphoreType.DMA((n,))]

    def _copies(self, srcs, outs, sems):
        ssem, rsem, lsem = sems
        x, y, c = _me()
        me = 2 * x + y
        own, out, landed = [], [], []
        for a in range(self.n):
            own.append(pltpu.make_async_copy(srcs[a].at[me], outs[a].at[me], lsem.at[a]))
            for k, (px, py) in enumerate(_other_chips(x, y)):
                dst = outs[a].at[2 * px + py]
                out.append(_remote(srcs[a].at[2 * px + py], outs[a].at[me], ssem.at[3 * a + k], rsem.at[3 * a + k],
                                   (px, py, c)))
                landed.append(_remote(dst, dst, ssem.at[3 * a + k], rsem.at[3 * a + k], (px, py, c)))
        return own, out, landed

    def first(self, srcs, outs, sems):
        own, out, _ = self._copies(srcs, outs, sems)
        for cp in own + out:
            cp.start()

    mid = None

    def last(self, srcs, outs, sems):
        own, out, landed = self._copies(srcs, outs, sems)
        for cp in landed:
            cp.wait_recv()
        for cp in own:
            cp.wait()
        for cp in out:
            cp.wait_send()


def _run_comm(comm, name):
    n = comm.n

    def body(*refs):
        srcs, outs, sems = refs[:n], refs[n:2 * n], refs[2 * n:]
        comm.first(srcs, outs, sems)
        if comm.mid is not None:
            comm.mid(srcs, outs, sems)
        comm.last(srcs, outs, sems)

    return _pcall(body, name=name, in_specs=[_HBM] * n, out_specs=[_HBM] * n, out_shape=comm.out_shape,
                  scratch_shapes=comm.sems)(*comm.ins)


def _reduce_sibling(grads, name):
    n = len(grads)

    def body(*refs):
        srcs, recvs = refs[:n], refs[n:2 * n]
        ssem, rsem = refs[2 * n:]
        x, y, c = _me()
        copies = []
        for a in range(n):
            h = grads[a].shape[1] // 2
            copies.append(_remote(srcs[a].at[:, pl.ds((1 - c) * h, h)], recvs[a], ssem.at[a], rsem.at[a], (x, y, 1 - c)))
        for cp in copies:
            cp.start()
        for cp in copies:
            cp.wait()

    return _pcall(
        body, name=name, in_specs=[_HBM] * n, out_specs=[_HBM] * n,
        out_shape=[jax.ShapeDtypeStruct((4, t.shape[1] // 2, t.shape[2]), t.dtype) for t in grads],
        scratch_shapes=[pltpu.SemaphoreType.DMA((n,)), pltpu.SemaphoreType.DMA((n,))],
    )(*grads)


def _reduce_finish(reds, name):
    n = len(reds)

    def body(*refs):
        outs = refs[n:2 * n]
        ssem, rsem = refs[2 * n:]
        x, y, c = _me()
        copies = []
        for a in range(n):
            h = reds[a].shape[0] // 2
            mine = outs[a].at[pl.ds(c * h, h)]
            copies.append(_remote(mine, mine, ssem.at[a], rsem.at[a], (x, y, 1 - c)))
        for cp in copies:
            cp.start()
        for a in range(n):
            h = reds[a].shape[0] // 2
            dst = outs[a].at[pl.ds((1 - c) * h, h)]
            _remote(dst, dst, ssem.at[a], rsem.at[a], (x, y, 1 - c)).wait_recv()
        for cp in copies:
            cp.wait_send()

    return _pcall(
        body, name=name, in_specs=[_HBM] * n, out_specs=[_HBM] * n,
        out_shape=[jax.ShapeDtypeStruct(t.shape, t.dtype) for t in reds],
        input_output_aliases={a: a for a in range(n)},
        scratch_shapes=[pltpu.SemaphoreType.DMA((n,)), pltpu.SemaphoreType.DMA((n,))],
    )(*reds)


def _half_sum(fn, full, halves, out_full, out_dtype, core, name):
    p, h, c = (halves[0].shape if halves else (full[0].shape[0], full[0].shape[1] // 2, full[0].shape[2]))
    br = _div(h, max(16, (1 << 19) // (p * c)), 16)
    nb = h // br
    mine3 = pl.BlockSpec((p, br, c), lambda i, core_ref: (0, core_ref[0] * nb + i, 0))
    half3 = pl.BlockSpec((p, br, c), lambda i, core_ref: (0, i, 0))

    def body(core_ref, *refs):
        refs[-1][...] = fn(*[t[...].astype(F32) for t in refs[:-1]]).astype(out_dtype)

    if out_full:
        out_spec = pl.BlockSpec((br, c), lambda i, core_ref: (core_ref[0] * nb + i, 0))
        out_shape = jax.ShapeDtypeStruct((2 * h, c), out_dtype)
    else:
        out_spec, out_shape = half3, jax.ShapeDtypeStruct((p, h, c), out_dtype)
    return _pcall(
        body, name=name,
        grid_spec=pltpu.PrefetchScalarGridSpec(
            num_scalar_prefetch=1, grid=(nb,), in_specs=[mine3] * len(full) + [half3] * len(halves),
            out_specs=out_spec),
        out_shape=out_shape, compiler_params=_cparams(("parallel",)),
    )(core, *full, *halves)


def _ada_fwd(c_all, w, b):
    def body(c_ref, w_ref, b_ref, o_ref):
        o_ref[...] = jnp.dot(c_ref[...], w_ref[...], precision=HI, preferred_element_type=F32) + b_ref[...]

    return _pcall(body, name="ada_fwd", out_shape=jax.ShapeDtypeStruct((c_all.shape[0], w.shape[1]), F32),
                  compiler_params=pltpu.CompilerParams(vmem_limit_bytes=VMEM_LIMIT))(c_all, w, b)


def _ada_bwd(c_all_t, d):
    def body(c_ref, d_ref, o_ref):
        o_ref[...] = jnp.dot(c_ref[...], d_ref[...], precision=HI, preferred_element_type=F32)

    return _pcall(body, name="ada_bwd", out_shape=jax.ShapeDtypeStruct((c_all_t.shape[0], d.shape[1]), F32),
                  compiler_params=pltpu.CompilerParams(vmem_limit_bytes=VMEM_LIMIT))(c_all_t, d)


def _sum_lead(x, name):
    p, r, n = x.shape
    br = _div(r, 512, 8)

    def body(x_ref, o_ref):
        acc = x_ref[0]
        for j in range(1, p):
            acc = acc + x_ref[j]
        o_ref[...] = acc

    return _pcall(
        body, name=name, grid=(r // br,), in_specs=[pl.BlockSpec((p, br, n), lambda i: (0, i, 0))],
        out_specs=pl.BlockSpec((br, n), lambda i: (i, 0)), out_shape=jax.ShapeDtypeStruct((r, n), F32),
        compiler_params=_cparams(("parallel",)),
    )(x)


def _adamw(w, g, m, v, name):
    shape = w.shape
    cols = shape[-1]
    w2, g2, m2, v2 = [t.reshape(-1, cols) for t in (w, g, m, v)]
    rows = w2.shape[0]
    br = _div(rows, max(8, (1 << 19) // cols // 8 * 8), 8)
    outs = _rows_fwd(_f_adamw, [(t, cols, 0) for t in (w2, g2, m2, v2)], [], [(cols, F32)] * 3, name=name, br=br)
    return [o.reshape(shape) for o in outs]


_BIG = (("w_in", 1), ("w_up", 1), ("w_down", 0), ("w_o", 0), ("w_rwkv_out", 0), ("w_att_out", 1), ("w2", 1), ("a2", 1),
        ("g2", 1))


_NEEDED_FIRST = ("w_in", "w_att_out", "w2", "a2", "g2")
_NEEDED_LATER = ("w_up", "w_down", "w_o", "w_rwkv_out")
_DONE_EARLY = ("w_up", "w_down", "w_o", "w_rwkv_out", "w_att_out")
_DONE_LATE = ("w_in", "w2", "a2", "g2")


def _cols_joined(t):
    return jnp.concatenate([t[j] for j in range(4)], axis=1)


def _cols_split(t):
    n = t.shape[1] // 4
    return jnp.stack([t[:, j * n:(j + 1) * n] for j in range(4)])


def _rows_joined(t):
    return t.reshape(4 * t.shape[1], t.shape[2])


def _rows_split(t):
    return t.reshape(4, t.shape[0] // 4, t.shape[1])


def _step_to_scan(x, tgt, ada, wts):
    sh1, sc1, gt1, sh2, sc2, gt2 = ada
    br = 256
    grp = lax.broadcasted_iota(jnp.int32, (D, 128), 0) // 64 == lax.broadcasted_iota(jnp.int32, (D, 128), 1)
    e = grp.astype(F32)
    et = e.T
    w_in = _cols_joined(wts["w_in"])
    w_att = w_in[:, :N_ATT]
    w_rw = jnp.pad(w_in[:, N_ATT:N_ATT + N_RW], ((0, 0), (0, N_RWP - N_RW)))
    w_gate = w_in[:, N_ATT + N_RW:]
    mu = jnp.pad(wts["mu_shift"], ((0, 0), (0, N_RWP - N_RW)))
    wl = jnp.zeros((N_LORA, 3 * D), F32)
    wl = wl.at[0:64, 0:D].set(_cols_joined(wts["w2"]).astype(F32))
    wl = wl.at[64:128, D:2 * D].set(_cols_joined(wts["a2"]).astype(F32))
    wl = wl.at[128:288, 2 * D:3 * D].set(_cols_joined(wts["g2"]).astype(F32))
    pre1_c = [wts["norm1_w"], sc1, sh1]
    (h1,) = _rows_fwd(_f_pre, [(x, D, 0)], pre1_c, [(D, BF16), None], name="pre1_fwd", br=br)
    att_in = _mm(h1, w_att, name="mm_att_in")
    z = _mm(h1, w_rw, name="mm_rw_in")
    gate_in = _mm(h1, w_gate, name="mm_gate_in")
    att_o, att_l = [], []
    for g, (_, dil) in enumerate(ATT_PATTERNS):
        o, l = _att_fwd(att_in, g, dil)
        att_o.append(o)
        att_l.append(l)
    comb_rows = [(t, ATT_WIDTH, 0) for t in att_o + att_l]
    (att,) = _rows_fwd(_f_comb, comb_rows, [], [(ATT_WIDTH, BF16)], name="comb_fwd", br=br)
    y_att = _mm(att, wts["w_att_out"], b_chip=True, name="mm_att_out")
    zs = _shift_fwd(z, mu)
    rwpre_c = [wts["w0"], wts["a0"], wts["k_k"], wts["k_a"], wl, e, et]
    lw, km, aa, bb, gg = _rows_fwd(_f_rwpre, [(zs, N_RWP, 0)], rwpre_c,
                                   [None, (D, F32), (D, F32), None, (D, F32), (D, F32), (D, F32)],
                                   name="rwpre_fwd", br=br)
    return dict(x=x, tgt=tgt, wts=wts, br=br, e=e, et=et, gt1=gt1, sc2=sc2, sh2=sh2, gt2=gt2, w_att=w_att, w_rw=w_rw,
                w_gate=w_gate, mu=mu, pre1_c=pre1_c, h1=h1, att_in=att_in, z=z, gate_in=gate_in, comb_rows=comb_rows,
                att=att, y_att=y_att, zs=zs, rwpre_c=rwpre_c, lw=lw, km=km, aa=aa, bb=bb, gg=gg)


def _step_between_scans(st, y_raw, late):
    x, tgt, wts, br, e, et = st["x"], st["tgt"], st["wts"], st["br"], st["e"], st["et"]
    zs, km, gg, gate_in, y_att, att = st["zs"], st["km"], st["gg"], st["gate_in"], st["y_att"], st["att"]
    comb_rows, att_in = st["comb_rows"], st["att_in"]
    gt1, sc2, sh2, gt2 = st["gt1"], st["sc2"], st["sh2"], st["gt2"]
    w_up, w_ao = late["w_up"], wts["w_att_out"]
    w_down, w_o, w_ro = _rows_joined(late["w_down"]), _rows_joined(late["w_o"]), _rows_joined(late["w_rwkv_out"])
    bga, bgr = wts["b_gate"][:, :D], wts["b_gate"][:, D:]
    post_rows = [(y_raw, D, 0), (zs, D, 0), (zs, D, 2), (km, D, 0), (gg, D, 0)]
    post_c = [wts["lnx_w"], wts["lnx_b"], wts["r_k"], e, et]
    (rw_out,) = _rows_fwd(_f_rwpost, post_rows, post_c, [(D, BF16)], name="rwpost_fwd", br=br)
    y_rw = _mm(rw_out, w_ro, name="mm_rw_out")
    mix_rows = [(gate_in, D, 0), (gate_in, D, 1), (y_att, D, 0), (y_rw, D, 0)]
    (mix,) = _rows_fwd(_f_mix, mix_rows, [bga, bgr], [(D, BF16)], name="mix_fwd", br=br)
    o = _mm(mix, w_o, name="mm_o")
    pre2_c = [gt1, wts["norm2_w"], sc2, sh2]
    x1, h2 = _rows_fwd(_f_pre2, [(x, D, 0), (o, D, 0)], pre2_c, [(D, F32), (D, BF16)], name="pre2_fwd", br=br)
    u = _mm(h2, w_up, b_chip=True, name="mm_up")
    act = _conv_fwd(u, wts["conv_w"], wts["conv_b"])
    f = _mm(act, w_down, name="mm_down")
    fin_rows = [(x1, D, 0), (f, D, 0), (tgt, D, 0)]
    fin_c = [gt2, wts["norm_f_w"]]

    def fin_fwd(*a):
        (l,) = _f_fin(*a)
        return (jnp.broadcast_to(jnp.sum(l, axis=0, keepdims=True), (8, 128)),)

    (loss_acc,) = _rows_fwd(fin_fwd, fin_rows, fin_c, [], name="fin_fwd", br=br, acc_shape=(8, 128))

    gw = {}
    dx1a, df, d_gt2, gw["norm_f_w"] = _rows_bwd(
        _f_fin, fin_rows, fin_c, [[]], wrt_rows=[0, 1], wrt_consts=[0, 1], drow_dtypes=[F32, BF16],
        name="fin_bwd", br=br, unit_cot=True)
    dact = _mm(df, w_down, tb=True, name="mm_dact")
    gw["w_down"] = _rows_split(_mm(act, df, ta=True, name="mm_dw_down"))
    du, gw["conv_w"], gw["conv_b"] = _conv_bwd(u, wts["conv_w"], wts["conv_b"], dact)
    dh2 = _mm(du, w_up, tb=True, b_chip=True, name="mm_dh2")
    gw["w_up"] = _mm(h2, du, ta=True, out_chip=True, name="mm_dw_up")
    dxa, do, d_gt1, gw["norm2_w"], d_sc2, d_sh2 = _rows_bwd(
        _f_pre2, [(x, D, 0), (o, D, 0)], pre2_c, [[(dx1a, D, 0)], [(dh2, D, 0)]], wrt_rows=[0, 1],
        wrt_consts=[0, 1, 2, 3], drow_dtypes=[F32, BF16], name="pre2_bwd", br=br)
    dmix = _mm(do, w_o, tb=True, name="mm_dmix")
    gw["w_o"] = _rows_split(_mm(mix, do, ta=True, name="mm_dw_o"))
    dga, dgr, dya, dyr, d_bga, d_bgr = _rows_bwd(
        _f_mix, mix_rows, [bga, bgr], [[(dmix, D, 0)]], wrt_rows=[0, 1, 2, 3], wrt_consts=[0, 1],
        drow_dtypes=[BF16] * 4, name="mix_bwd", br=br)
    gw["b_gate"] = jnp.concatenate([d_bga, d_bgr], axis=1)
    datt = _mm(dya, w_ao, tb=True, b_chip=True, name="mm_datt")
    gw["w_att_out"] = _mm(att, dya, ta=True, out_chip=True, name="mm_dw_att_out")
    drw = _mm(dyr, w_ro, tb=True, name="mm_drw")
    gw["w_rwkv_out"] = _rows_split(_mm(rw_out, dyr, ta=True, name="mm_dw_rw_out"))
    dcomb = _rows_bwd(_f_comb, comb_rows, [], [[(datt, ATT_WIDTH, 0)]], wrt_rows=list(range(6)), wrt_consts=[],
                      drow_dtypes=[F32] * 6, name="comb_bwd", br=br)
    datt_in = []
    for g, (_, dil) in enumerate(ATT_PATTERNS):
        datt_in += _att_bwd(att_in, g, dil, dcomb[g], dcomb[3 + g])
    datt_in = jnp.concatenate(datt_in, axis=1)
    dy_raw, dr_p, dv_p, dkm_p, dgg, gw["lnx_w"], gw["lnx_b"], gw["r_k"] = _rows_bwd(
        _f_rwpost, post_rows, post_c, [[(drw, D, 0)]], wrt_rows=[0, 1, 2, 3, 4], wrt_consts=[0, 1, 2],
        drow_dtypes=[F32] * 5, name="rwpost_bwd", br=br)
    st.update(loss=loss_acc[0, 0], gw=gw, dxa=dxa, dgate=jnp.concatenate([dga, dgr], axis=1), datt_in=datt_in,
              dy_raw=dy_raw, dr_p=dr_p, dv_p=dv_p, dkm_p=dkm_p, dgg=dgg, d_ada_late=(d_gt1, d_sh2, d_sc2, d_gt2))
    return st


def _step_after_scan(st, scan_grads):
    x, br, gw, h1, zs = st["x"], st["br"], st["gw"], st["h1"], st["zs"]
    dr_s, dlw, dkm_s, dv_s, daa, dbb = scan_grads
    pre_cots = [[(st["dr_p"], D, 0), (dr_s, D, 0)], [(dlw, D, 0)], [(st["dkm_p"], D, 0), (dkm_s, D, 0)],
                [(st["dv_p"], D, 0), (dv_s, D, 0)], [(daa, D, 0)], [(dbb, D, 0)], [(st["dgg"], D, 0)]]
    dzs, gw["w0"], gw["a0"], gw["k_k"], gw["k_a"], dwl = _rows_bwd(
        _f_rwpre, [(zs, N_RWP, 0)], st["rwpre_c"], pre_cots, wrt_rows=[0], wrt_consts=[0, 1, 2, 3, 4],
        drow_dtypes=[F32], name="rwpre_bwd", br=128)
    gw["w2"], gw["a2"] = _cols_split(dwl[0:64, 0:D]), _cols_split(dwl[64:128, D:2 * D])
    gw["g2"] = _cols_split(dwl[128:288, 2 * D:3 * D])
    dz, dmu = _shift_bwd(st["z"], st["mu"], dzs)
    gw["mu_shift"] = dmu[:, :N_RW]
    datt_in, dgate = st["datt_in"], st["dgate"]
    dh1 = _mm(datt_in, st["w_att"], tb=True, name="mm_dh1_att")
    dh1 = _mm(dgate, st["w_gate"], tb=True, add=dh1, name="mm_dh1_gate")
    dh1 = _mm(dz, st["w_rw"], tb=True, add=dh1, name="mm_dh1_rw")
    gw["w_in"] = _cols_split(jnp.concatenate([_mm(h1, datt_in, ta=True, name="mm_dw_att"),
                                              _mm(h1, dz, ta=True, name="mm_dw_rw")[:, :N_RW],
                                              _mm(h1, dgate, ta=True, name="mm_dw_gate")], axis=1))
    grad_x, gw["norm1_w"], d_sc1, d_sh1 = _rows_bwd(
        _f_pre, [(x, D, 0)], st["pre1_c"], [[(dh1, D, 0)], [(st["dxa"], D, 0)]], wrt_rows=[0], wrt_consts=[0, 1, 2],
        drow_dtypes=[F32], name="pre1_bwd", br=br)
    d_gt1, d_sh2, d_sc2, d_gt2 = st["d_ada_late"]
    return st["loss"], grad_x, (d_sh1, d_sc1, d_gt1, d_sh2, d_sc2, d_gt2), gw


_SMALL = ("b_ada", "norm1_w", "b_gate", "mu_shift", "w0", "a0", "k_k", "k_a", "r_k", "lnx_w", "lnx_b", "norm2_w",
          "conv_b", "norm_f_w")
_NAMES = ("w_ada", "b_ada", "norm1_w", "w_in", "b_gate", "mu_shift", "w0", "w2", "a0", "a2", "g2", "k_k", "k_a", "r_k",
          "lnx_w", "lnx_b", "w_att_out", "w_rwkv_out", "w_o", "norm2_w", "w_up", "conv_w", "conv_b", "w_down",
          "norm_f_w")


def kernel(x, c, w_ada, b_ada, norm1_w, w_in, b_gate, mu_shift, w0, w2, a0, a2, g2, k_k, k_a, r_k, lnx_w, lnx_b, w_att_out, w_rwkv_out, w_o, norm2_w, w_up, conv_w, conv_b, w_down, norm_f_w, loss_target, m_w_ada, m_b_ada, m_norm1_w, m_w_in, m_b_gate, m_mu_shift, m_w0, m_w2, m_a0, m_a2, m_g2, m_k_k, m_k_a, m_r_k, m_lnx_w, m_lnx_b, m_w_att_out, m_w_rwkv_out, m_w_o, m_norm2_w, m_w_up, m_conv_w, m_conv_b, m_w_down, m_norm_f_w, v_w_ada, v_b_ada, v_norm1_w, v_w_in, v_b_gate, v_mu_shift, v_w0, v_w2, v_a0, v_a2, v_g2, v_k_k, v_k_a, v_r_k, v_lnx_w, v_lnx_b, v_w_att_out, v_w_rwkv_out, v_w_o, v_norm2_w, v_w_up, v_conv_w, v_conv_b, v_w_down, v_norm_f_w):
    args = dict(locals())
    p, pm, pv = {}, {}, {}
    for name in _NAMES:
        for dst, key in ((p, name), (pm, "m_" + name), (pv, "v_" + name)):
            t = args[key]
            dst[name] = t.reshape(1, -1) if name in ("r_k", "norm_f_w") else t.reshape(t.shape[-2], t.shape[-1])
    xi, yi, ci = _me()
    chip = 2 * xi + yi
    dev = 4 * xi + 2 * yi + ci
    x2, tgt = x[0], loss_target[0]

    n_cw = 3 * (2 * D_FF // 4)
    vec = jnp.concatenate([c.reshape(-1), p["conv_w"].reshape(-1), jnp.zeros((8 * D - D - n_cw,), F32)]).reshape(8, D)
    g0 = _allgather8(vec, "gather_c").reshape(8, 8 * D)
    c_all = g0[:, :D]
    conv_w_full = jnp.concatenate([g0[2 * j, D:D + n_cw].reshape(3, -1) for j in range(4)], axis=1)
    n_ada = 6 * D // 4
    b_ada_sh = lax.dynamic_slice(p["b_ada"], (0, chip * n_ada), (1, n_ada))
    ada_sh = _ada_fwd(c_all, p["w_ada"], b_ada_sh)
    ga = _allgather8(ada_sh, "gather_ada")
    ada_all = jnp.concatenate([ga[2 * j] for j in range(4)], axis=1)
    ada_row = lax.dynamic_slice(ada_all, (dev, 0), (1, 6 * D))
    ada = [ada_row[:, j * D:(j + 1) * D] for j in range(6)]

    big = [n for n, _ in _BIG]
    shard = {n: p[n].astype(BF16) for n in big}
    wts = dict(zip(_NEEDED_FIRST, _run_comm(_GatherWeights([shard[n] for n in _NEEDED_FIRST]), "gather_w")))
    for n in _SMALL:
        wts[n] = p[n]
    wts["conv_w"] = conv_w_full
    core = ci.reshape(1).astype(jnp.int32)

    def chip_parts(gw, names):
        recv = _reduce_sibling([gw[n] for n in names], "reduce_sib_" + names[0])
        return [_half_sum(lambda a, b: a + b, [gw[n]], [r], False, BF16, core, "reduce_add2_" + n)
                for n, r in zip(names, recv)]

    st = _step_to_scan(x2, tgt, ada, wts)
    y_raw, s0s, late = _scan_fwd(st["zs"], st["lw"], st["km"], st["aa"], st["bb"],
                                 _GatherWeights([shard[n] for n in _NEEDED_LATER]))
    st = _step_between_scans(st, y_raw, dict(zip(_NEEDED_LATER, late)))
    scan_grads, slots_early = _scan_bwd(st["zs"], st["lw"], st["km"], st["aa"], st["bb"], s0s, st["dy_raw"],
                                        _ScatterToChips(chip_parts(st["gw"], _DONE_EARLY)))
    loss_part, grad_x, d_ada, gw = _step_after_scan(st, scan_grads)

    small = [jnp.concatenate(d_ada, axis=1)] + [gw[n] for n in _SMALL[1:]] + [gw["conv_w"], loss_part.reshape(1, 1)]
    sizes = [t.size for t in small]
    flat = jnp.concatenate([t.reshape(-1) for t in small])
    npad = (-flat.shape[0]) % (8 * D)
    srows = (flat.shape[0] + npad) // D
    flat = jnp.concatenate([flat, jnp.zeros((npad,), F32)]).reshape(srows, D)
    parts = _allgather8(flat, "gather_small")
    tot = _sum_lead(parts, "sum_small").reshape(-1)
    pieces, pos = [], 0
    for sz in sizes:
        pieces.append(tot[pos:pos + sz])
        pos += sz
    grads = {}
    for n, piece in zip(_SMALL, pieces[:len(_SMALL)]):
        grads[n] = piece.reshape(p[n].shape)
    conv_w_grad = pieces[len(_SMALL)].reshape(3, 2 * D_FF)
    grads["conv_w"] = lax.dynamic_slice(conv_w_grad, (0, chip * (n_cw // 3)), (3, n_cw // 3))
    loss = pieces[-1][0]
    d_ada_all = parts[:, :6].reshape(8, 6 * D)
    grads["w_ada"] = _ada_bwd(c_all.T, lax.dynamic_slice(d_ada_all, (0, chip * n_ada), (8, n_ada)))

    slots_late = _run_comm(_ScatterToChips(chip_parts(gw, _DONE_LATE)), "reduce_chips")
    order = _DONE_EARLY + _DONE_LATE
    reds = [_half_sum(lambda t: t[0] + t[1] + t[2] + t[3], [], [t], True, F32, core, "reduce_add4_" + n)
            for n, t in zip(order, list(slots_early) + list(slots_late))]
    for n, g in zip(order, _reduce_finish(reds, "reduce_sib2")):
        grads[n] = g

    outs_g, outs_d, outs_m, outs_v = [], [], [], []
    for name in _NAMES:
        g = grads[name]
        d, m, v = _adamw(p[name], g, pm[name], pv[name], "adamw_" + name)
        shape = args[name].shape
        outs_g.append(g.reshape(shape))
        outs_d.append(d.reshape(shape))
        outs_m.append(m.reshape(shape))
        outs_v.append(v.reshape(shape))
    return (loss, grad_x.reshape(x.shape), *outs_g, *outs_d, *outs_m, *outs_v)
```

```python
import functools

import jax
import jax.numpy as jnp
from jax import lax
from jax.experimental import pallas as pl
from jax.experimental.pallas import tpu as pltpu

F32 = jnp.float32
BF16 = jnp.bfloat16
HI = lax.Precision.HIGHEST
MESH = pl.DeviceIdType.MESH

D = 1024
ATT_PATTERNS = ((128, 1), (512, 4), (2048, 16))
ATT_BLOCK = 128
ATT_WIDTH = 512
N_ATT = 3 * 3 * ATT_WIDTH
N_RW = 3 * D + 64 + 64 + 160
N_RWP = 3456
N_LORA = N_RWP - 3 * D
N_GATE = 2 * D
D_FF = 2816
RMS_EPS = 1e-6
GN_EPS = 64e-5
SCAN_CHUNK = 64
SCAN_PAIRS = 8
NEG = -1e30
VMEM_LIMIT = 48 * 1024 * 1024

ADAM_LR, ADAM_B1, ADAM_B2, ADAM_EPS, ADAM_WD, ADAM_STEP = 0.001, 0.9, 0.999, 1e-08, 0.01, 10


def _pcall(body, **kw):
    return pl.pallas_call(body, **kw)


def _cparams(sem):
    return pltpu.CompilerParams(dimension_semantics=sem, vmem_limit_bytes=VMEM_LIMIT)


def _div(n, pref, mult):
    best = None
    d = mult
    while d <= min(n, pref):
        if n % d == 0:
            best = d
        d += mult
    return best if best else n


def _dg(a, b, ca, cb):
    return lax.dot_general(a.astype(BF16), b.astype(BF16), (((ca,), (cb,)), ((), ())), preferred_element_type=F32)


@jax.custom_vjp
def _nn(a, b):
    return _dg(a, b, 1, 0)


@jax.custom_vjp
def _nt(a, b):
    return _dg(a, b, 1, 1)


@jax.custom_vjp
def _tn(a, b):
    return _dg(a, b, 0, 0)


_nn.defvjp(lambda a, b: (_nn(a, b), (a, b)), lambda res, g: (_nt(g, res[1]), _tn(res[0], g)))
_nt.defvjp(lambda a, b: (_nt(a, b), (a, b)), lambda res, g: (_nn(g, res[1]), _tn(g, res[0])))
_tn.defvjp(lambda a, b: (_tn(a, b), (a, b)), lambda res, g: (_nt(res[1], g), _nn(res[0], g)))


def _bdg(a, b, ca, cb):
    return lax.dot_general(a.astype(BF16), b.astype(BF16), (((ca,), (cb,)), ((0,), (0,))), preferred_element_type=F32)


@jax.custom_vjp
def _bnn(a, b):
    return _bdg(a, b, 2, 1)


@jax.custom_vjp
def _bnt(a, b):
    return _bdg(a, b, 2, 2)


@jax.custom_vjp
def _btn(a, b):
    return _bdg(a, b, 1, 1)


_bnn.defvjp(lambda a, b: (_bnn(a, b), (a, b)), lambda res, g: (_bnt(g, res[1]), _btn(res[0], g)))
_bnt.defvjp(lambda a, b: (_bnt(a, b), (a, b)), lambda res, g: (_bnn(g, res[1]), _btn(g, res[0])))
_btn.defvjp(lambda a, b: (_btn(a, b), (a, b)), lambda res, g: (_bnt(res[1], g), _bnn(res[0], g)))


def _split2(x):
    hi = x.astype(BF16)
    lo = (x - hi.astype(F32)).astype(BF16)
    return hi, lo


def _hsum_impl(x, e, et):
    eb, etb = e.astype(BF16), et.astype(BF16)
    hi, lo = _split2(x)
    s = jnp.dot(hi, eb, preferred_element_type=F32) + jnp.dot(lo, eb, preferred_element_type=F32)
    shi, slo = _split2(s)
    return jnp.dot(shi, etb, preferred_element_type=F32) + jnp.dot(slo, etb, preferred_element_type=F32)


@jax.custom_vjp
def _hsum(x, e, et):
    return _hsum_impl(x, e, et)


_hsum.defvjp(lambda x, e, et: (_hsum_impl(x, e, et), (e, et)),
             lambda res, g: (_hsum_impl(g, res[0], res[1]), jnp.zeros_like(res[0]), jnp.zeros_like(res[1])))


def _mm(a, b, *, ta=False, tb=False, out_dtype=F32, add=None, b_chip=False, out_chip=False, name):
    if ta:
        kdim, m = a.shape
    else:
        m, kdim = a.shape
    if b_chip:
        n = b.shape[1] if tb else 4 * b.shape[2]
    else:
        n = b.shape[0] if tb else b.shape[1]
    tm, tn, tk = _div(m, 1536, 128), _div(n, 1536, 128), _div(kdim, 1408, 128)
    if b_chip and tb:
        tk = kdim // 4
    if (b_chip and not tb) or out_chip:
        tn = n // 4
    nk = kdim // tk
    ca, cb = (0 if ta else 1), (1 if tb else 0)

    def body(*refs):
        a_ref, b_ref = refs[0], refs[1]
        add_ref = None if add is None else refs[2]
        o_ref = refs[2 if add is None else 3]
        part = lax.dot_general(a_ref[...], b_ref[...], (((ca,), (cb,)), ((), ())), preferred_element_type=F32)

        def finish(r):
            if add_ref is not None:
                r = r + add_ref[...]
            o_ref[...] = r.astype(o_ref.dtype)

        if nk == 1:
            finish(part)
            return
        acc = refs[-1]
        k = pl.program_id(2)

        @pl.when(k == 0)
        def _():
            acc[...] = part

        @pl.when(k > 0)
        def _():
            acc[...] += part

        @pl.when(k == nk - 1)
        def _():
            finish(acc[...])

    a_spec = pl.BlockSpec((tk, tm), lambda i, j, k: (k, i)) if ta else pl.BlockSpec((tm, tk), lambda i, j, k: (i, k))
    if b_chip:
        b_spec = (pl.BlockSpec((None, tn, tk), lambda i, j, k: (k, j, 0)) if tb
                  else pl.BlockSpec((None, tk, tn), lambda i, j, k: (j, k, 0)))
    else:
        b_spec = pl.BlockSpec((tn, tk), lambda i, j, k: (j, k)) if tb else pl.BlockSpec((tk, tn), lambda i, j, k: (k, j))
    in_specs = [a_spec, b_spec]
    args = [a, b]
    if add is not None:
        in_specs.append(pl.BlockSpec((tm, tn), lambda i, j, k: (i, j)))
        args.append(add)
    if out_chip:
        out_spec = pl.BlockSpec((None, tm, tn), lambda i, j, k: (j, i, 0))
        out_shape = jax.ShapeDtypeStruct((4, m, tn), out_dtype)
    else:
        out_spec = pl.BlockSpec((tm, tn), lambda i, j, k: (i, j))
        out_shape = jax.ShapeDtypeStruct((m, n), out_dtype)
    return _pcall(
        body, name=name, grid=(m // tm, n // tn, nk), in_specs=in_specs, out_specs=out_spec, out_shape=out_shape,
        scratch_shapes=[] if nk == 1 else [pltpu.VMEM((tm, tn), F32)],
        compiler_params=_cparams(("parallel", "parallel", "arbitrary")),
    )(*args)


def _row_spec(br, w, cb):
    return pl.BlockSpec((br, w), lambda i: (i, cb))


def _const_spec(shape):
    return pl.BlockSpec(shape, lambda i: (0,) * len(shape))


def _rows_fwd(fn, rows, consts, outs, *, name, br, acc_shape=None):
    s = rows[0][0].shape[0]
    nr, nc = len(rows), len(consts)
    kept = [k for k, o in enumerate(outs) if o is not None]

    def body(*refs):
        xs = [r[...].astype(F32) for r in refs[:nr]]
        cs = [c[...] for c in refs[nr:nr + nc]]
        res = fn(*xs, *cs)
        orefs = refs[nr + nc:]
        for j, k in enumerate(kept):
            orefs[j][...] = res[k].astype(orefs[j].dtype)
        if acc_shape is not None:
            acc_ref = orefs[len(kept)]

            @pl.when(pl.program_id(0) == 0)
            def _():
                acc_ref[...] = jnp.zeros_like(acc_ref)

            acc_ref[...] += res[len(outs)]

    in_specs = [_row_spec(br, w, cb) for (_, w, cb) in rows] + [_const_spec(c.shape) for c in consts]
    out_specs = [_row_spec(br, outs[k][0], 0) for k in kept]
    out_shape = [jax.ShapeDtypeStruct((s, outs[k][0]), outs[k][1]) for k in kept]
    if acc_shape is not None:
        out_specs.append(_const_spec(acc_shape))
        out_shape.append(jax.ShapeDtypeStruct(acc_shape, F32))
    return _pcall(
        body, name=name, grid=(s // br,), in_specs=in_specs, out_specs=out_specs, out_shape=out_shape,
        compiler_params=_cparams(("arbitrary",)),
    )(*[r[0] for r in rows], *consts)


def _rows_bwd(fn, rows, consts, cots, *, wrt_rows, wrt_consts, drow_dtypes, name, br, unit_cot=False):
    s = rows[0][0].shape[0]
    nr, nc = len(rows), len(consts)
    flat_cots = [c for lst in cots for c in lst]
    ncot = len(flat_cots)

    def body(*refs):
        xs = [r[...].astype(F32) for r in refs[:nr]]
        cs = [c[...] for c in refs[nr:nr + nc]]
        cvals = [c[...].astype(F32) for c in refs[nr + nc:nr + nc + ncot]]
        orefs = refs[nr + nc + ncot:]

        def g(*d):
            xs2, cs2 = list(xs), list(cs)
            for j, k in enumerate(wrt_rows):
                xs2[k] = d[j]
            for j, k in enumerate(wrt_consts):
                cs2[k] = d[len(wrt_rows) + j]
            return tuple(fn(*xs2, *cs2))

        prim = [xs[k] for k in wrt_rows] + [cs[k] for k in wrt_consts]
        outs, vjp = jax.vjp(g, *prim)
        ct = []
        pos = 0
        for o, lst in zip(outs, cots):
            if unit_cot:
                ct.append(jnp.ones_like(o))
                continue
            acc = jnp.zeros_like(o)
            for _ in lst:
                acc = acc + cvals[pos]
                pos += 1
            ct.append(acc)
        grads = vjp(tuple(ct))
        for j in range(len(wrt_rows)):
            orefs[j][...] = grads[j].astype(orefs[j].dtype)

        @pl.when(pl.program_id(0) == 0)
        def _():
            for j in range(len(wrt_consts)):
                oref = orefs[len(wrt_rows) + j]
                oref[...] = jnp.zeros_like(oref)

        for j in range(len(wrt_consts)):
            orefs[len(wrt_rows) + j][...] += grads[len(wrt_rows) + j]

    in_specs = ([_row_spec(br, w, cb) for (_, w, cb) in rows] + [_const_spec(c.shape) for c in consts]
                + [_row_spec(br, w, cb) for (_, w, cb) in flat_cots])
    out_specs = [_row_spec(br, rows[k][1], 0) for k in wrt_rows] + [_const_spec(consts[k].shape) for k in wrt_consts]
    out_shape = ([jax.ShapeDtypeStruct((s, rows[k][1]), dt) for k, dt in zip(wrt_rows, drow_dtypes)]
                 + [jax.ShapeDtypeStruct(consts[k].shape, F32) for k in wrt_consts])
    return _pcall(
        body, name=name, grid=(s // br,), in_specs=in_specs, out_specs=out_specs, out_shape=out_shape,
        compiler_params=_cparams(("arbitrary",)),
    )(*[r[0] for r in rows], *consts, *[c[0] for c in flat_cots])


def _rms(x, w):
    return x * lax.rsqrt(jnp.mean(x * x, axis=-1, keepdims=True) + RMS_EPS) * w


def _softplus(x):
    return jnp.maximum(x, 0.0) + jnp.log(1.0 + jnp.exp(-jnp.abs(x)))


def _f_pre(x, nw, sc, sh):
    return _rms(x, nw) * (1.0 + sc) + sh, x


def _f_pre2(x, o, gt, nw, sc, sh):
    x1 = x + gt * o
    return x1, _rms(x1, nw) * (1.0 + sc) + sh


def _f_fin(x1, f, tgt, gt, nfw):
    y = _rms(x1 + gt * f, nfw)
    return (0.5 * jnp.mean(jnp.square(y - tgt), axis=-1, keepdims=True),)


def _f_comb(o1, o2, o3, l1, l2, l3):
    m = lax.stop_gradient(jnp.maximum(jnp.maximum(l1, l2), l3))
    e1, e2, e3 = jnp.exp(l1 - m), jnp.exp(l2 - m), jnp.exp(l3 - m)
    return ((e1 * o1 + e2 * o2 + e3 * o3) / (e1 + e2 + e3),)


def _f_rwpre(zs, w0, a0, k_k, k_a, wl, e, et):
    r, k, v, zl = zs[:, 0:D], zs[:, D:2 * D], zs[:, 2 * D:3 * D], zs[:, 3 * D:N_RWP]
    lane = lax.broadcasted_iota(jnp.int32, zl.shape, 1)
    t = jnp.where(lane < 64, jnp.tanh(zl), jnp.where(lane < 128, zl, jnp.where(lane < 288, jax.nn.sigmoid(zl), 0.0)))
    lo = _nn(t, wl)
    w_log = -_softplus(-(w0 + lo[:, 0:D])) - 0.5
    lw = -jnp.exp(w_log)
    a = jax.nn.sigmoid(a0 + lo[:, D:2 * D])
    g = lo[:, 2 * D:3 * D]
    k_mod = k * (1.0 + (a - 1.0) * k_a)
    kk = k * k_k
    kk = kk / jnp.maximum(jnp.sqrt(_hsum(kk * kk, e, et)), 1e-12)
    return r, lw, k_mod, v, -kk, kk * a, g


def _f_rwpost(y, r, v, k_mod, g, lnx_w, lnx_b, r_k, e, et):
    mean = _hsum(y, e, et) * (1.0 / 64)
    yc = y - mean
    var = _hsum(yc * yc, e, et) * (1.0 / 64)
    yn = yc * lax.rsqrt(var + GN_EPS) * lnx_w + lnx_b
    bonus = _hsum(r * k_mod * r_k, e, et) * v
    return ((yn + bonus) * g,)


def _f_mix(gia, gir, ya, yr, bga, bgr):
    return (jax.nn.sigmoid(gia + bga) * ya + jax.nn.sigmoid(gir + bgr) * yr,)


def _f_adamw(w, g, m, v):
    m = ADAM_B1 * m + (1.0 - ADAM_B1) * g
    v = ADAM_B2 * v + (1.0 - ADAM_B2) * jnp.square(g)
    m_hat = m / (1.0 - ADAM_B1 ** ADAM_STEP)
    v_hat = v / (1.0 - ADAM_B2 ** ADAM_STEP)
    return -ADAM_LR * (m_hat / (jnp.sqrt(v_hat) + ADAM_EPS) + ADAM_WD * w), m, v


def _down(x, k):
    row = lax.broadcasted_iota(jnp.int32, x.shape, 0)
    return jnp.where(row < k, 0.0, pltpu.roll(x, k, 0))


def _up(x, k):
    n = x.shape[0]
    row = lax.broadcasted_iota(jnp.int32, x.shape, 0)
    return jnp.where(row >= n - k, 0.0, pltpu.roll(x, n - k, 0))


def _col_spec(s, w, off=0):
    return pl.BlockSpec((s, w), lambda j: (0, j + off))


def _shift_fwd(z, mu):
    s, n = z.shape

    def body(z_ref, mu_ref, o_ref):
        zz = z_ref[...]
        o_ref[...] = zz + (_down(zz, 1) - zz) * mu_ref[...]

    return _pcall(
        body, name="shift_fwd", grid=(n // 128,), in_specs=[_col_spec(s, 128), _col_spec(1, 128)],
        out_specs=_col_spec(s, 128), out_shape=jax.ShapeDtypeStruct((s, n), F32),
        compiler_params=_cparams(("parallel",)),
    )(z, mu)


def _shift_bwd(z, mu, dzs):
    s, n = z.shape

    def body(z_ref, mu_ref, d_ref, dz_ref, dmu_ref):
        zz, d, m = z_ref[...], d_ref[...], mu_ref[...]
        dm = d * m
        dz_ref[...] = (d - dm + _up(dm, 1)).astype(dz_ref.dtype)
        dmu_ref[...] = jnp.sum(d * (_down(zz, 1) - zz), axis=0, keepdims=True)

    return _pcall(
        body, name="shift_bwd", grid=(n // 128,), in_specs=[_col_spec(s, 128), _col_spec(1, 128), _col_spec(s, 128)],
        out_specs=[_col_spec(s, 128), _col_spec(1, 128)],
        out_shape=[jax.ShapeDtypeStruct((s, n), BF16), jax.ShapeDtypeStruct((1, n), F32)],
        compiler_params=_cparams(("parallel",)),
    )(z, mu, dzs)


def _conv3(x, w_ref, b_ref):
    return b_ref[...] + w_ref[0:1, :] * _down(x, 2) + w_ref[1:2, :] * _down(x, 1) + w_ref[2:3, :] * x


def _conv_fwd(u, cw, cb):
    s = u.shape[0]
    nb = D_FF // 128

    def body(ug_ref, uv_ref, wg_ref, wv_ref, bg_ref, bv_ref, o_ref):
        gate = _conv3(ug_ref[...], wg_ref, bg_ref)
        val = _conv3(uv_ref[...], wv_ref, bv_ref)
        o_ref[...] = (gate * jax.nn.sigmoid(gate) * val).astype(o_ref.dtype)

    return _pcall(
        body, name="conv_fwd", grid=(nb,),
        in_specs=[_col_spec(s, 128), _col_spec(s, 128, nb), _col_spec(3, 128), _col_spec(3, 128, nb),
                  _col_spec(1, 128), _col_spec(1, 128, nb)],
        out_specs=_col_spec(s, 128), out_shape=jax.ShapeDtypeStruct((s, D_FF), BF16),
        compiler_params=_cparams(("parallel",)),
    )(u, u, cw, cw, cb, cb)


def _conv_bwd(u, cw, cb, dact):
    s = u.shape[0]
    nb = D_FF // 128

    def half(x, d, w_ref, du_ref, dw_ref, db_ref):
        x1, x2 = _down(x, 1), _down(x, 2)
        du_ref[...] = (w_ref[2:3, :] * d + w_ref[1:2, :] * _up(d, 1) + w_ref[0:1, :] * _up(d, 2)).astype(du_ref.dtype)
        dw_ref[0:1, :] = jnp.sum(d * x2, axis=0, keepdims=True)
        dw_ref[1:2, :] = jnp.sum(d * x1, axis=0, keepdims=True)
        dw_ref[2:3, :] = jnp.sum(d * x, axis=0, keepdims=True)
        db_ref[...] = jnp.sum(d, axis=0, keepdims=True)

    def body(ug_ref, uv_ref, wg_ref, wv_ref, bg_ref, bv_ref, da_ref,
             dug_ref, duv_ref, dwg_ref, dwv_ref, dbg_ref, dbv_ref):
        ug, uv, da = ug_ref[...], uv_ref[...], da_ref[...]
        gate = _conv3(ug, wg_ref, bg_ref)
        val = _conv3(uv, wv_ref, bv_ref)
        sg = jax.nn.sigmoid(gate)
        dgate = da * val * sg * (1.0 + gate * (1.0 - sg))
        dval = da * gate * sg
        half(ug, dgate, wg_ref, dug_ref, dwg_ref, dbg_ref)
        half(uv, dval, wv_ref, duv_ref, dwv_ref, dbv_ref)

    dug, duv, dwg, dwv, dbg, dbv = _pcall(
        body, name="conv_bwd", grid=(nb,),
        in_specs=[_col_spec(s, 128), _col_spec(s, 128, nb), _col_spec(3, 128), _col_spec(3, 128, nb),
                  _col_spec(1, 128), _col_spec(1, 128, nb), _col_spec(s, 128)],
        out_specs=[_col_spec(s, 128), _col_spec(s, 128), _col_spec(3, 128), _col_spec(3, 128),
                   _col_spec(1, 128), _col_spec(1, 128)],
        out_shape=[jax.ShapeDtypeStruct((s, D_FF), BF16), jax.ShapeDtypeStruct((s, D_FF), BF16),
                   jax.ShapeDtypeStruct((3, D_FF), F32), jax.ShapeDtypeStruct((3, D_FF), F32),
                   jax.ShapeDtypeStruct((1, D_FF), F32), jax.ShapeDtypeStruct((1, D_FF), F32)],
        compiler_params=_cparams(("parallel",)),
    )(u, u, cw, cw, cb, cb, dact)
    return (jnp.concatenate([dug, duv], axis=1), jnp.concatenate([dwg, dwv], axis=1),
            jnp.concatenate([dbg, dbv], axis=1))


ATT_BATCH = 4


def _att_batch(q, kp, kc, vp, vc, first):
    ma = lax.broadcasted_iota(jnp.int32, (1, ATT_BLOCK, 128), 2) < 64
    qs = jnp.concatenate([jnp.where(ma, q, 0.0), jnp.where(ma, 0.0, q)], axis=1)
    qi = lax.broadcasted_iota(jnp.int32, (1, 2 * ATT_BLOCK, ATT_BLOCK), 1) & (ATT_BLOCK - 1)
    kj = lax.broadcasted_iota(jnp.int32, (1, 2 * ATT_BLOCK, ATT_BLOCK), 2)
    okp = kj >= qi + jnp.where(first, 2 * ATT_BLOCK, 0)
    okc = kj <= qi
    sp = jnp.where(okp, _bnt(qs, kp) * 0.125, NEG)
    sc = jnp.where(okc, _bnt(qs, kc) * 0.125, NEG)
    m = lax.stop_gradient(jnp.maximum(jnp.max(sp, axis=-1, keepdims=True), jnp.max(sc, axis=-1, keepdims=True)))
    pp, pc = jnp.exp(sp - m), jnp.exp(sc - m)
    den = jnp.sum(pp, axis=-1, keepdims=True) + jnp.sum(pc, axis=-1, keepdims=True)
    o_s = (_bnn(pp, vp) + _bnn(pc, vc)) / den
    l_s = jnp.broadcast_to(m + jnp.log(den), o_s.shape)
    return (jnp.where(ma, o_s[:, :ATT_BLOCK], o_s[:, ATT_BLOCK:]), jnp.where(ma, l_s[:, :ATT_BLOCK], l_s[:, ATT_BLOCK:]))


def _att_pairs_per_step(dil):
    return ATT_BATCH if dil == 1 else 1


def _att_specs(g, dil):
    rows, pp = ATT_BLOCK * dil, _att_pairs_per_step(dil)

    def cur(slot):
        return pl.BlockSpec((rows, 128 * pp), lambda n, p: (n, (g * 3 + slot) * (4 // pp) + p))

    def prev(slot):
        return pl.BlockSpec((rows, 128 * pp), lambda n, p: (jnp.maximum(n - 1, 0), (g * 3 + slot) * (4 // pp) + p))

    return [cur(0), prev(1), cur(1), prev(2), cur(2)]


def _att_out_spec(dil):
    return pl.BlockSpec((ATT_BLOCK * dil, 128 * _att_pairs_per_step(dil)), lambda n, p: (n, p))


def _att_grid(s, dil):
    return (s // (ATT_BLOCK * dil), 4 // _att_pairs_per_step(dil))


def _att_windows(i, dil):
    if dil == 1:
        return [(pl.ds(0, ATT_BLOCK), pl.ds(128 * j, 128)) for j in range(ATT_BATCH)]
    return [(pl.ds(i * ATT_BATCH + j, ATT_BLOCK, stride=dil), pl.ds(0, 128)) for j in range(ATT_BATCH)]


def _att_fwd(att_in, g, dil):
    s = att_in.shape[0]

    def body(q_ref, kp_ref, kc_ref, vp_ref, vc_ref, o_ref, l_ref):
        first = pl.program_id(0) == 0

        def one(i, carry):
            win = _att_windows(i, dil)
            vals = [jnp.stack([ref[w] for w in win]) for ref in (q_ref, kp_ref, kc_ref, vp_ref, vc_ref)]
            o, l = _att_batch(*vals, first)
            for j, w in enumerate(win):
                o_ref[w] = o[j]
                l_ref[w] = l[j]
            return carry

        lax.fori_loop(0, max(1, dil // ATT_BATCH), one, 0)

    return _pcall(
        body, name=f"att_fwd{g}", grid=_att_grid(s, dil), in_specs=_att_specs(g, dil),
        out_specs=[_att_out_spec(dil)] * 2, out_shape=[jax.ShapeDtypeStruct((s, ATT_WIDTH), F32)] * 2,
        compiler_params=_cparams(("parallel", "parallel")),
    )(att_in, att_in, att_in, att_in, att_in)


def _att_bwd(att_in, g, dil, do, dl):
    s = att_in.shape[0]
    nb = s // (ATT_BLOCK * dil)

    def body(q_ref, kp_ref, kc_ref, vp_ref, vc_ref, do_ref, dl_ref, dq_ref, dkp_ref, dkc_ref, dvp_ref, dvc_ref):
        first = pl.program_id(0) == 0

        def one(i, carry):
            win = _att_windows(i, dil)
            vals = [jnp.stack([ref[w] for w in win]) for ref in (q_ref, kp_ref, kc_ref, vp_ref, vc_ref)]
            _, vjp = jax.vjp(lambda *a: _att_batch(*a, first), *vals)
            grads = vjp((jnp.stack([do_ref[w] for w in win]), jnp.stack([dl_ref[w] for w in win])))
            for ref, gr in zip((dq_ref, dkp_ref, dkc_ref, dvp_ref, dvc_ref), grads):
                for j, w in enumerate(win):
                    ref[w] = gr[j]
            return carry

        lax.fori_loop(0, max(1, dil // ATT_BATCH), one, 0)

    dq, dkp, dkc, dvp, dvc = _pcall(
        body, name=f"att_bwd{g}", grid=_att_grid(s, dil), in_specs=_att_specs(g, dil) + [_att_out_spec(dil)] * 2,
        out_specs=[_att_out_spec(dil)] * 5, out_shape=[jax.ShapeDtypeStruct((s, ATT_WIDTH), F32)] * 5,
        compiler_params=_cparams(("parallel", "parallel")),
    )(att_in, att_in, att_in, att_in, att_in, do, dl)

    unit = ATT_BLOCK * dil
    per = max(1, 1024 // unit)
    width = ATT_WIDTH if per > 1 else 128
    steps = nb // per

    def with_next(cur_ref, prev_ref, next_ref, has_next):
        tail = jnp.where(has_next, next_ref[...], 0.0)
        shifted = tail if per == 1 else jnp.concatenate([prev_ref[unit:, :], tail], axis=0)
        return (cur_ref[...] + shifted).astype(BF16)

    def cbody(dq_ref, dkc_ref, dkp_ref, dkn_ref, dvc_ref, dvp_ref, dvn_ref, oq_ref, ok_ref, ov_ref):
        has_next = pl.program_id(0) + 1 < steps
        oq_ref[...] = dq_ref[...].astype(BF16)
        ok_ref[...] = with_next(dkc_ref, dkp_ref, dkn_ref, has_next)
        ov_ref[...] = with_next(dvc_ref, dvp_ref, dvn_ref, has_next)

    cur = pl.BlockSpec((per * unit, width), lambda n, p: (n, p))
    nxt = pl.BlockSpec((unit, width), lambda n, p: (jnp.minimum((n + 1) * per, nb - 1), p))
    return _pcall(
        cbody, name=f"att_bwd_sum{g}", grid=(steps, ATT_WIDTH // width), in_specs=[cur, cur, cur, nxt, cur, cur, nxt],
        out_specs=[cur] * 3, out_shape=[jax.ShapeDtypeStruct((s, ATT_WIDTH), BF16)] * 3,
        compiler_params=_cparams(("parallel", "parallel")),
    )(dq, dkc, dkp, dkp, dvc, dvp, dvp)


def _scan_chunk(r, lw, k, v, a, b, s0):
    c = SCAN_CHUNK
    p = s0.shape[0]
    ri = lax.broadcasted_iota(jnp.int32, (c, c), 0)
    ci = lax.broadcasted_iota(jnp.int32, (c, c), 1)
    cum = jnp.dot((ci <= ri).astype(F32), lw, precision=HI, preferred_element_type=F32)
    tot = jnp.sum(lw, axis=0, keepdims=True)
    ma = (lax.broadcasted_iota(jnp.int32, (c, 128 * p), 1) & 127) < 64

    def pairs(x):
        return jnp.concatenate([x[None, :, 128 * j:128 * (j + 1)] for j in range(p)], axis=0)

    def stack(x):
        return jnp.concatenate([pairs(jnp.where(ma, x, 0.0)), pairs(jnp.where(ma, 0.0, x))], axis=1)

    einv, eend = jnp.exp(-cum), jnp.exp(tot - cum)
    ra, aa = stack(r * jnp.exp(cum)), stack(a * jnp.exp(cum - lw))
    bi, ki, be, ke, vs = stack(b * einv), stack(k * einv), stack(b * eend), stack(k * eend), stack(v)
    r2 = lax.broadcasted_iota(jnp.int32, (1, 2 * c, 2 * c), 1)
    c2 = lax.broadcasted_iota(jnp.int32, (1, 2 * c, 2 * c), 2)
    same = (r2 >= c) == (c2 >= c)
    strict = jnp.logical_and(same, c2 < r2)
    incl = jnp.logical_and(same, c2 <= r2)
    s0 = jnp.where(same, s0, 0.0)
    prod = _bnt(jnp.concatenate([aa, ra], axis=1), jnp.concatenate([bi, ki], axis=1))
    a_ab = jnp.where(strict, prod[:, :2 * c, :2 * c], 0.0)
    a_ak = jnp.where(strict, prod[:, :2 * c, 2 * c:], 0.0)
    a_rb = jnp.where(incl, prod[:, 2 * c:, :2 * c], 0.0)
    a_rk = jnp.where(incl, prod[:, 2 * c:, 2 * c:], 0.0)
    t = jnp.where(r2 == c2, 1.0, 0.0) + a_ab
    pw = a_ab
    for _ in range(5):
        pw = _bnn(pw, pw)
        t = t + _bnn(t, pw)
    u = _bnn(t, _bnt(aa, s0) + _bnn(a_ak, vs))
    uv = jnp.concatenate([u, vs], axis=1)
    ys = _bnt(ra, s0) + _bnn(jnp.concatenate([a_rb, a_rk], axis=2), uv)
    s1 = s0 * pairs(jnp.exp(tot)) + _btn(uv, jnp.concatenate([be, ke], axis=1))
    y3 = ys[:, :c] + ys[:, c:]
    return jnp.concatenate([y3[j] for j in range(p)], axis=1), s1


def _scan_specs(rev, n):
    def at(i):
        return n - 1 - i if rev else i

    def cm(cb):
        return pl.BlockSpec((SCAN_CHUNK, D), lambda i: (at(i), cb))

    return cm, pl.BlockSpec((1, SCAN_PAIRS, 128, 128), lambda i: (at(i), 0, 0, 0))


def _comm_phases(comm, refs, n):
    k = comm.n
    srcs, outs, sems = refs[:k], refs[k:2 * k], refs[2 * k:]
    i = pl.program_id(0)

    def before():
        @pl.when(i == 0)
        def _():
            comm.first(srcs, outs, sems)

    def after():
        if comm.mid is not None:
            @pl.when(i == (3 * n) // 4)
            def _():
                comm.mid(srcs, outs, sems)

        @pl.when(i == n - 1)
        def _():
            comm.last(srcs, outs, sems)

    return before, after


def _scan_fwd(zs, lw, km, aa, bb, comm):
    s = zs.shape[0]
    n = s // SCAN_CHUNK
    cm, st = _scan_specs(False, n)
    k = comm.n

    def body(*refs):
        r_ref, lw_ref, k_ref, v_ref, a_ref, b_ref = refs[:6]
        y_ref, s0_ref = refs[6 + k:8 + k]
        state = refs[8 + 2 * k]
        before, after = _comm_phases(comm, refs[6:6 + k] + refs[8 + k:8 + 2 * k] + refs[9 + 2 * k:], n)
        before()

        @pl.when(pl.program_id(0) == 0)
        def _():
            state[...] = jnp.zeros_like(state)

        s0 = state[...]
        s0_ref[0] = s0
        y, s1 = _scan_chunk(*[ref[...] for ref in (r_ref, lw_ref, k_ref, v_ref, a_ref, b_ref)], s0)
        y_ref[...] = y
        state[...] = s1
        after()

    res = _pcall(
        body, name="scan_fwd", grid=(n,), in_specs=[cm(0), cm(0), cm(0), cm(2), cm(0), cm(0)] + [_HBM] * k,
        out_specs=[cm(0), st] + [_HBM] * k,
        out_shape=[jax.ShapeDtypeStruct((s, D), F32), jax.ShapeDtypeStruct((n, 8, 128, 128), F32)] + comm.out_shape,
        scratch_shapes=[pltpu.VMEM((SCAN_PAIRS, 128, 128), F32)] + comm.sems,
        compiler_params=_cparams(("arbitrary",)),
    )(zs, lw, km, zs, aa, bb, *comm.ins)
    return res[0], res[1], res[2:]


def _scan_bwd(zs, lw, km, aa, bb, s0s, dy, comm):
    s = zs.shape[0]
    n = s // SCAN_CHUNK
    cm, st = _scan_specs(True, n)
    k = comm.n

    def body(*refs):
        r_ref, lw_ref, k_ref, v_ref, a_ref, b_ref, s0_ref, dy_ref = refs[:8]
        douts = refs[8 + k:14 + k]
        dstate = refs[14 + 2 * k]
        before, after = _comm_phases(comm, refs[8:8 + k] + refs[14 + k:14 + 2 * k] + refs[15 + 2 * k:], n)
        before()

        @pl.when(pl.program_id(0) == 0)
        def _():
            dstate[...] = jnp.zeros_like(dstate)

        prim = [ref[...] for ref in (r_ref, lw_ref, k_ref, v_ref, a_ref, b_ref)] + [s0_ref[0]]
        _, vjp = jax.vjp(_scan_chunk, *prim)
        grads = vjp((dy_ref[...], dstate[...]))
        for ref, gr in zip(douts, grads[:6]):
            ref[...] = gr
        dstate[...] = grads[6]
        after()

    res = _pcall(
        body, name="scan_bwd", grid=(n,),
        in_specs=[cm(0), cm(0), cm(0), cm(2), cm(0), cm(0), st, cm(0)] + [_HBM] * k,
        out_specs=[cm(0)] * 6 + [_HBM] * k, out_shape=[jax.ShapeDtypeStruct((s, D), F32)] * 6 + comm.out_shape,
        scratch_shapes=[pltpu.VMEM((SCAN_PAIRS, 128, 128), F32)] + comm.sems,
        compiler_params=_cparams(("arbitrary",)),
    )(zs, lw, km, zs, aa, bb, s0s, dy, *comm.ins)
    return res[:6], res[6:]


_HBM = pl.BlockSpec(memory_space=pltpu.HBM)


def _me():
    return lax.axis_index("x"), lax.axis_index("y"), lax.axis_index("c")


def _allgather8(src, name):
    def body(src_ref, out_ref, ssem, rsem, lsem):
        x, y, c = _me()
        me = 4 * x + 2 * y + c
        local = pltpu.make_async_copy(src_ref, out_ref.at[me], lsem)
        local.start()
        peers = []
        for k in range(1, 8):
            peers.append(((1 - x) if k & 4 else x, (1 - y) if k & 2 else y, (1 - c) if k & 1 else c))
        sends = []
        for k, peer in enumerate(peers):
            cp = pltpu.make_async_remote_copy(src_ref, out_ref.at[me], ssem.at[k], rsem.at[k], device_id=peer,
                                              device_id_type=MESH)
            cp.start()
            sends.append(cp)
        for k, (px, py, pc) in enumerate(peers):
            pltpu.make_async_remote_copy(src_ref, out_ref.at[4 * px + 2 * py + pc], ssem.at[k], rsem.at[k],
                                         device_id=(px, py, pc), device_id_type=MESH).wait_recv()
        for cp in sends:
            cp.wait_send()
        local.wait()

    return _pcall(
        body, name=name, in_specs=[_HBM], out_specs=_HBM, out_shape=jax.ShapeDtypeStruct((8,) + src.shape, src.dtype),
        scratch_shapes=[pltpu.SemaphoreType.DMA((7,)), pltpu.SemaphoreType.DMA((7,)), pltpu.SemaphoreType.DMA],
    )(src)


def _other_chips(x, y):
    return [(1 - x, y), (x, 1 - y), (1 - x, 1 - y)]


def _remote(src, dst, ssem, rsem, to):
    return pltpu.make_async_remote_copy(src, dst, ssem, rsem, device_id=to, device_id_type=MESH)


class _GatherWeights:
    def __init__(self, shards):
        self.ins = list(shards)
        n = self.n = len(shards)
        self.out_shape = [jax.ShapeDtypeStruct((4,) + t.shape, t.dtype) for t in shards]
        self.sems = [pltpu.SemaphoreType.DMA((6 * n,)), pltpu.SemaphoreType.DMA((6 * n,)),
                     pltpu.SemaphoreType.DMA((n,)), pltpu.SemaphoreType.DMA((n,))]

    def _copies(self, srcs, outs, sems):
        ssem, rsem, lsem, osem = sems
        x, y, c = _me()
        me = 2 * x + y
        own, ici, landed, passed, passed_in = [], [], [], [], []
        for a in range(self.n):
            h = self.ins[a].shape[0] // 2
            mine, other = pl.ds(c * h, h), pl.ds((1 - c) * h, h)
            own.append(_remote(srcs[a], outs[a].at[me], lsem.at[a], osem.at[a], (x, y, 1 - c)))
            for k, (px, py) in enumerate(_other_chips(x, y)):
                s1, r1, s2, r2 = ssem.at[6 * a + k], rsem.at[6 * a + k], ssem.at[6 * a + 3 + k], rsem.at[6 * a + 3 + k]
                got, got_sib = outs[a].at[2 * px + py, mine], outs[a].at[2 * px + py, other]
                ici.append(_remote(srcs[a].at[mine], outs[a].at[me, mine], s1, r1, (px, py, c)))
                landed.append(_remote(got, got, s1, r1, (px, py, c)))
                passed.append(_remote(got, got, s2, r2, (x, y, 1 - c)))
                passed_in.append(_remote(got_sib, got_sib, s2, r2, (x, y, 1 - c)))
        return own, ici, landed, passed, passed_in

    def first(self, srcs, outs, sems):
        own, ici, _, _, _ = self._copies(srcs, outs, sems)
        for cp in own + ici:
            cp.start()

    def mid(self, srcs, outs, sems):
        _, _, landed, passed, _ = self._copies(srcs, outs, sems)
        for arrived, onward in zip(landed, passed):
            arrived.wait_recv()
            onward.start()

    def last(self, srcs, outs, sems):
        own, ici, _, passed, passed_in = self._copies(srcs, outs, sems)
        for cp in passed_in:
            cp.wait_recv()
        for cp in ici + passed:
            cp.wait_send()
        for cp in own:
            cp.wait()


class _ScatterToChips:
    def __init__(self, parts):
        self.ins = list(parts)
        n = self.n = len(parts)
        self.out_shape = [jax.ShapeDtypeStruct(t.shape, t.dtype) for t in parts]
        self.sems = [pltpu.SemaphoreType.DMA((3 * n,)), pltpu.SemaphoreType.DMA((3 * n,)), pltpu.SemaphoreType.DMA((n,))]

    def _copies(self, srcs, outs, sems):
        ssem, rsem, lsem = sems
        x, y, c = _me()
        me = 2 * x + y
        own, out, landed = [], [], []
        for a in range(self.n):
            own.append(pltpu.make_async_copy(srcs[a].at[me], outs[a].at[me], lsem.at[a]))
            for k, (px, py) in enumerate(_other_chips(x, y)):
                dst = outs[a].at[2 * px + py]
                out.append(_remote(srcs[a].at[2 * px + py], outs[a].at[me], ssem.at[3 * a + k], rsem.at[3 * a + k],
                                   (px, py, c)))
                landed.append(_remote(dst, dst, ssem.at[3 * a + k], rsem.at[3 * a + k], (px, py, c)))
        return own, out, landed

    def first(self, srcs, outs, sems):
        own, out, _ = self._copies(srcs, outs, sems)
        for cp in own + out:
            cp.start()

    mid = None

    def last(self, srcs, outs, sems):
        own, out, landed = self._copies(srcs, outs, sems)
        for cp in landed:
            cp.wait_recv()
        for cp in own:
            cp.wait()
        for cp in out:
            cp.wait_send()


def _run_comm(comm, name):
    n = comm.n

    def body(*refs):
        srcs, outs, sems = refs[:n], refs[n:2 * n], refs[2 * n:]
        comm.first(srcs, outs, sems)
        if comm.mid is not None:
            comm.mid(srcs, outs, sems)
        comm.last(srcs, outs, sems)

    return _pcall(body, name=name, in_specs=[_HBM] * n, out_specs=[_HBM] * n, out_shape=comm.out_shape,
                  scratch_shapes=comm.sems)(*comm.ins)


def _reduce_sibling(grads, name):
    n = len(grads)

    def body(*refs):
        srcs, recvs = refs[:n], refs[n:2 * n]
        ssem, rsem = refs[2 * n:]
        x, y, c = _me()
        copies = []
        for a in range(n):
            h = grads[a].shape[1] // 2
            copies.append(_remote(srcs[a].at[:, pl.ds((1 - c) * h, h)], recvs[a], ssem.at[a], rsem.at[a], (x, y, 1 - c)))
        for cp in copies:
            cp.start()
        for cp in copies:
            cp.wait()

    return _pcall(
        body, name=name, in_specs=[_HBM] * n, out_specs=[_HBM] * n,
        out_shape=[jax.ShapeDtypeStruct((4, t.shape[1] // 2, t.shape[2]), t.dtype) for t in grads],
        scratch_shapes=[pltpu.SemaphoreType.DMA((n,)), pltpu.SemaphoreType.DMA((n,))],
    )(*grads)


def _reduce_finish(reds, name):
    n = len(reds)

    def body(*refs):
        outs = refs[n:2 * n]
        ssem, rsem = refs[2 * n:]
        x, y, c = _me()
        copies = []
        for a in range(n):
            h = reds[a].shape[0] // 2
            mine = outs[a].at[pl.ds(c * h, h)]
            copies.append(_remote(mine, mine, ssem.at[a], rsem.at[a], (x, y, 1 - c)))
        for cp in copies:
            cp.start()
        for a in range(n):
            h = reds[a].shape[0] // 2
            dst = outs[a].at[pl.ds((1 - c) * h, h)]
            _remote(dst, dst, ssem.at[a], rsem.at[a], (x, y, 1 - c)).wait_recv()
        for cp in copies:
            cp.wait_send()

    return _pcall(
        body, name=name, in_specs=[_HBM] * n, out_specs=[_HBM] * n,
        out_shape=[jax.ShapeDtypeStruct(t.shape, t.dtype) for t in reds],
        input_output_aliases={a: a for a in range(n)},
        scratch_shapes=[pltpu.SemaphoreType.DMA((n,)), pltpu.SemaphoreType.DMA((n,))],
    )(*reds)


def _half_sum(fn, full, halves, out_full, out_dtype, core, name):
    p, h, c = (halves[0].shape if halves else (full[0].shape[0], full[0].shape[1] // 2, full[0].shape[2]))
    br = _div(h, max(16, (1 << 19) // (p * c)), 16)
    nb = h // br
    mine3 = pl.BlockSpec((p, br, c), lambda i, core_ref: (0, core_ref[0] * nb + i, 0))
    half3 = pl.BlockSpec((p, br, c), lambda i, core_ref: (0, i, 0))

    def body(core_ref, *refs):
        refs[-1][...] = fn(*[t[...].astype(F32) for t in refs[:-1]]).astype(out_dtype)

    if out_full:
        out_spec = pl.BlockSpec((br, c), lambda i, core_ref: (core_ref[0] * nb + i, 0))
        out_shape = jax.ShapeDtypeStruct((2 * h, c), out_dtype)
    else:
        out_spec, out_shape = half3, jax.ShapeDtypeStruct((p, h, c), out_dtype)
    return _pcall(
        body, name=name,
        grid_spec=pltpu.PrefetchScalarGridSpec(
            num_scalar_prefetch=1, grid=(nb,), in_specs=[mine3] * len(full) + [half3] * len(halves),
            out_specs=out_spec),
        out_shape=out_shape, compiler_params=_cparams(("parallel",)),
    )(core, *full, *halves)


def _ada_fwd(c_all, w, b):
    def body(c_ref, w_ref, b_ref, o_ref):
        o_ref[...] = jnp.dot(c_ref[...], w_ref[...], precision=HI, preferred_element_type=F32) + b_ref[...]

    return _pcall(body, name="ada_fwd", out_shape=jax.ShapeDtypeStruct((c_all.shape[0], w.shape[1]), F32),
                  compiler_params=pltpu.CompilerParams(vmem_limit_bytes=VMEM_LIMIT))(c_all, w, b)


def _ada_bwd(c_all_t, d):
    def body(c_ref, d_ref, o_ref):
        o_ref[...] = jnp.dot(c_ref[...], d_ref[...], precision=HI, preferred_element_type=F32)

    return _pcall(body, name="ada_bwd", out_shape=jax.ShapeDtypeStruct((c_all_t.shape[0], d.shape[1]), F32),
                  compiler_params=pltpu.CompilerParams(vmem_limit_bytes=VMEM_LIMIT))(c_all_t, d)


def _sum_lead(x, name):
    p, r, n = x.shape
    br = _div(r, 512, 8)

    def body(x_ref, o_ref):
        acc = x_ref[0]
        for j in range(1, p):
            acc = acc + x_ref[j]
        o_ref[...] = acc

    return _pcall(
        body, name=name, grid=(r // br,), in_specs=[pl.BlockSpec((p, br, n), lambda i: (0, i, 0))],
        out_specs=pl.BlockSpec((br, n), lambda i: (i, 0)), out_shape=jax.ShapeDtypeStruct((r, n), F32),
        compiler_params=_cparams(("parallel",)),
    )(x)


def _adamw(w, g, m, v, name):
    shape = w.shape
    cols = shape[-1]
    w2, g2, m2, v2 = [t.reshape(-1, cols) for t in (w, g, m, v)]
    rows = w2.shape[0]
    br = _div(rows, max(8, (1 << 19) // cols // 8 * 8), 8)
    outs = _rows_fwd(_f_adamw, [(t, cols, 0) for t in (w2, g2, m2, v2)], [], [(cols, F32)] * 3, name=name, br=br)
    return [o.reshape(shape) for o in outs]


_BIG = (("w_in", 1), ("w_up", 1), ("w_down", 0), ("w_o", 0), ("w_rwkv_out", 0), ("w_att_out", 1), ("w2", 1), ("a2", 1),
        ("g2", 1))


_NEEDED_FIRST = ("w_in", "w_att_out", "w2", "a2", "g2")
_NEEDED_LATER = ("w_up", "w_down", "w_o", "w_rwkv_out")
_DONE_EARLY = ("w_up", "w_down", "w_o", "w_rwkv_out", "w_att_out")
_DONE_LATE = ("w_in", "w2", "a2", "g2")


def _cols_joined(t):
    return jnp.concatenate([t[j] for j in range(4)], axis=1)


def _cols_split(t):
    n = t.shape[1] // 4
    return jnp.stack([t[:, j * n:(j + 1) * n] for j in range(4)])


def _rows_joined(t):
    return t.reshape(4 * t.shape[1], t.shape[2])


def _rows_split(t):
    return t.reshape(4, t.shape[0] // 4, t.shape[1])


def _step_to_scan(x, tgt, ada, wts):
    sh1, sc1, gt1, sh2, sc2, gt2 = ada
    br = 256
    grp = lax.broadcasted_iota(jnp.int32, (D, 128), 0) // 64 == lax.broadcasted_iota(jnp.int32, (D, 128), 1)
    e = grp.astype(F32)
    et = e.T
    w_in = _cols_joined(wts["w_in"])
    w_att = w_in[:, :N_ATT]
    w_rw = jnp.pad(w_in[:, N_ATT:N_ATT + N_RW], ((0, 0), (0, N_RWP - N_RW)))
    w_gate = w_in[:, N_ATT + N_RW:]
    mu = jnp.pad(wts["mu_shift"], ((0, 0), (0, N_RWP - N_RW)))
    wl = jnp.zeros((N_LORA, 3 * D), F32)
    wl = wl.at[0:64, 0:D].set(_cols_joined(wts["w2"]).astype(F32))
    wl = wl.at[64:128, D:2 * D].set(_cols_joined(wts["a2"]).astype(F32))
    wl = wl.at[128:288, 2 * D:3 * D].set(_cols_joined(wts["g2"]).astype(F32))
    pre1_c = [wts["norm1_w"], sc1, sh1]
    (h1,) = _rows_fwd(_f_pre, [(x, D, 0)], pre1_c, [(D, BF16), None], name="pre1_fwd", br=br)
    att_in = _mm(h1, w_att, name="mm_att_in")
    z = _mm(h1, w_rw, name="mm_rw_in")
    gate_in = _mm(h1, w_gate, name="mm_gate_in")
    att_o, att_l = [], []
    for g, (_, dil) in enumerate(ATT_PATTERNS):
        o, l = _att_fwd(att_in, g, dil)
        att_o.append(o)
        att_l.append(l)
    comb_rows = [(t, ATT_WIDTH, 0) for t in att_o + att_l]
    (att,) = _rows_fwd(_f_comb, comb_rows, [], [(ATT_WIDTH, BF16)], name="comb_fwd", br=br)
    y_att = _mm(att, wts["w_att_out"], b_chip=True, name="mm_att_out")
    zs = _shift_fwd(z, mu)
    rwpre_c = [wts["w0"], wts["a0"], wts["k_k"], wts["k_a"], wl, e, et]
    lw, km, aa, bb, gg = _rows_fwd(_f_rwpre, [(zs, N_RWP, 0)], rwpre_c,
                                   [None, (D, F32), (D, F32), None, (D, F32), (D, F32), (D, F32)],
                                   name="rwpre_fwd", br=br)
    return dict(x=x, tgt=tgt, wts=wts, br=br, e=e, et=et, gt1=gt1, sc2=sc2, sh2=sh2, gt2=gt2, w_att=w_att, w_rw=w_rw,
                w_gate=w_gate, mu=mu, pre1_c=pre1_c, h1=h1, att_in=att_in, z=z, gate_in=gate_in, comb_rows=comb_rows,
                att=att, y_att=y_att, zs=zs, rwpre_c=rwpre_c, lw=lw, km=km, aa=aa, bb=bb, gg=gg)


def _step_between_scans(st, y_raw, late):
    x, tgt, wts, br, e, et = st["x"], st["tgt"], st["wts"], st["br"], st["e"], st["et"]
    zs, km, gg, gate_in, y_att, att = st["zs"], st["km"], st["gg"], st["gate_in"], st["y_att"], st["att"]
    comb_rows, att_in = st["comb_rows"], st["att_in"]
    gt1, sc2, sh2, gt2 = st["gt1"], st["sc2"], st["sh2"], st["gt2"]
    w_up, w_ao = late["w_up"], wts["w_att_out"]
    w_down, w_o, w_ro = _rows_joined(late["w_down"]), _rows_joined(late["w_o"]), _rows_joined(late["w_rwkv_out"])
    bga, bgr = wts["b_gate"][:, :D], wts["b_gate"][:, D:]
    post_rows = [(y_raw, D, 0), (zs, D, 0), (zs, D, 2), (km, D, 0), (gg, D, 0)]
    post_c = [wts["lnx_w"], wts["lnx_b"], wts["r_k"], e, et]
    (rw_out,) = _rows_fwd(_f_rwpost, post_rows, post_c, [(D, BF16)], name="rwpost_fwd", br=br)
    y_rw = _mm(rw_out, w_ro, name="mm_rw_out")
    mix_rows = [(gate_in, D, 0), (gate_in, D, 1), (y_att, D, 0), (y_rw, D, 0)]
    (mix,) = _rows_fwd(_f_mix, mix_rows, [bga, bgr], [(D, BF16)], name="mix_fwd", br=br)
    o = _mm(mix, w_o, name="mm_o")
    pre2_c = [gt1, wts["norm2_w"], sc2, sh2]
    x1, h2 = _rows_fwd(_f_pre2, [(x, D, 0), (o, D, 0)], pre2_c, [(D, F32), (D, BF16)], name="pre2_fwd", br=br)
    u = _mm(h2, w_up, b_chip=True, name="mm_up")
    act = _conv_fwd(u, wts["conv_w"], wts["conv_b"])
    f = _mm(act, w_down, name="mm_down")
    fin_rows = [(x1, D, 0), (f, D, 0), (tgt, D, 0)]
    fin_c = [gt2, wts["norm_f_w"]]

    def fin_fwd(*a):
        (l,) = _f_fin(*a)
        return (jnp.broadcast_to(jnp.sum(l, axis=0, keepdims=True), (8, 128)),)

    (loss_acc,) = _rows_fwd(fin_fwd, fin_rows, fin_c, [], name="fin_fwd", br=br, acc_shape=(8, 128))

    gw = {}
    dx1a, df, d_gt2, gw["norm_f_w"] = _rows_bwd(
        _f_fin, fin_rows, fin_c, [[]], wrt_rows=[0, 1], wrt_consts=[0, 1], drow_dtypes=[F32, BF16],
        name="fin_bwd", br=br, unit_cot=True)
    dact = _mm(df, w_down, tb=True, name="mm_dact")
    gw["w_down"] = _rows_split(_mm(act, df, ta=True, name="mm_dw_down"))
    du, gw["conv_w"], gw["conv_b"] = _conv_bwd(u, wts["conv_w"], wts["conv_b"], dact)
    dh2 = _mm(du, w_up, tb=True, b_chip=True, name="mm_dh2")
    gw["w_up"] = _mm(h2, du, ta=True, out_chip=True, name="mm_dw_up")
    dxa, do, d_gt1, gw["norm2_w"], d_sc2, d_sh2 = _rows_bwd(
        _f_pre2, [(x, D, 0), (o, D, 0)], pre2_c, [[(dx1a, D, 0)], [(dh2, D, 0)]], wrt_rows=[0, 1],
        wrt_consts=[0, 1, 2, 3], drow_dtypes=[F32, BF16], name="pre2_bwd", br=br)
    dmix = _mm(do, w_o, tb=True, name="mm_dmix")
    gw["w_o"] = _rows_split(_mm(mix, do, ta=True, name="mm_dw_o"))
    dga, dgr, dya, dyr, d_bga, d_bgr = _rows_bwd(
        _f_mix, mix_rows, [bga, bgr], [[(dmix, D, 0)]], wrt_rows=[0, 1, 2, 3], wrt_consts=[0, 1],
        drow_dtypes=[BF16] * 4, name="mix_bwd", br=br)
    gw["b_gate"] = jnp.concatenate([d_bga, d_bgr], axis=1)
    datt = _mm(dya, w_ao, tb=True, b_chip=True, name="mm_datt")
    gw["w_att_out"] = _mm(att, dya, ta=True, out_chip=True, name="mm_dw_att_out")
    drw = _mm(dyr, w_ro, tb=True, name="mm_drw")
    gw["w_rwkv_out"] = _rows_split(_mm(rw_out, dyr, ta=True, name="mm_dw_rw_out"))
    dcomb = _rows_bwd(_f_comb, comb_rows, [], [[(datt, ATT_WIDTH, 0)]], wrt_rows=list(range(6)), wrt_consts=[],
                      drow_dtypes=[F32] * 6, name="comb_bwd", br=br)
    datt_in = []
    for g, (_, dil) in enumerate(ATT_PATTERNS):
        datt_in += _att_bwd(att_in, g, dil, dcomb[g], dcomb[3 + g])
    datt_in = jnp.concatenate(datt_in, axis=1)
    dy_raw, dr_p, dv_p, dkm_p, dgg, gw["lnx_w"], gw["lnx_b"], gw["r_k"] = _rows_bwd(
        _f_rwpost, post_rows, post_c, [[(drw, D, 0)]], wrt_rows=[0, 1, 2, 3, 4], wrt_consts=[0, 1, 2],
        drow_dtypes=[F32] * 5, name="rwpost_bwd", br=br)
    st.update(loss=loss_acc[0, 0], gw=gw, dxa=dxa, dgate=jnp.concatenate([dga, dgr], axis=1), datt_in=datt_in,
              dy_raw=dy_raw, dr_p=dr_p, dv_p=dv_p, dkm_p=dkm_p, dgg=dgg, d_ada_late=(d_gt1, d_sh2, d_sc2, d_gt2))
    return st


def _step_after_scan(st, scan_grads):
    x, br, gw, h1, zs = st["x"], st["br"], st["gw"], st["h1"], st["zs"]
    dr_s, dlw, dkm_s, dv_s, daa, dbb = scan_grads
    pre_cots = [[(st["dr_p"], D, 0), (dr_s, D, 0)], [(dlw, D, 0)], [(st["dkm_p"], D, 0), (dkm_s, D, 0)],
                [(st["dv_p"], D, 0), (dv_s, D, 0)], [(daa, D, 0)], [(dbb, D, 0)], [(st["dgg"], D, 0)]]
    dzs, gw["w0"], gw["a0"], gw["k_k"], gw["k_a"], dwl = _rows_bwd(
        _f_rwpre, [(zs, N_RWP, 0)], st["rwpre_c"], pre_cots, wrt_rows=[0], wrt_consts=[0, 1, 2, 3, 4],
        drow_dtypes=[F32], name="rwpre_bwd", br=128)
    gw["w2"], gw["a2"] = _cols_split(dwl[0:64, 0:D]), _cols_split(dwl[64:128, D:2 * D])
    gw["g2"] = _cols_split(dwl[128:288, 2 * D:3 * D])
    dz, dmu = _shift_bwd(st["z"], st["mu"], dzs)
    gw["mu_shift"] = dmu[:, :N_RW]
    datt_in, dgate = st["datt_in"], st["dgate"]
    dh1 = _mm(datt_in, st["w_att"], tb=True, name="mm_dh1_att")
    dh1 = _mm(dgate, st["w_gate"], tb=True, add=dh1, name="mm_dh1_gate")
    dh1 = _mm(dz, st["w_rw"], tb=True, add=dh1, name="mm_dh1_rw")
    gw["w_in"] = _cols_split(jnp.concatenate([_mm(h1, datt_in, ta=True, name="mm_dw_att"),
                                              _mm(h1, dz, ta=True, name="mm_dw_rw")[:, :N_RW],
                                              _mm(h1, dgate, ta=True, name="mm_dw_gate")], axis=1))
    grad_x, gw["norm1_w"], d_sc1, d_sh1 = _rows_bwd(
        _f_pre, [(x, D, 0)], st["pre1_c"], [[(dh1, D, 0)], [(st["dxa"], D, 0)]], wrt_rows=[0], wrt_consts=[0, 1, 2],
        drow_dtypes=[F32], name="pre1_bwd", br=br)
    d_gt1, d_sh2, d_sc2, d_gt2 = st["d_ada_late"]
    return st["loss"], grad_x, (d_sh1, d_sc1, d_gt1, d_sh2, d_sc2, d_gt2), gw


_SMALL = ("b_ada", "norm1_w", "b_gate", "mu_shift", "w0", "a0", "k_k", "k_a", "r_k", "lnx_w", "lnx_b", "norm2_w",
          "conv_b", "norm_f_w")
_NAMES = ("w_ada", "b_ada", "norm1_w", "w_in", "b_gate", "mu_shift", "w0", "w2", "a0", "a2", "g2", "k_k", "k_a", "r_k",
          "lnx_w", "lnx_b", "w_att_out", "w_rwkv_out", "w_o", "norm2_w", "w_up", "conv_w", "conv_b", "w_down",
          "norm_f_w")


def kernel(x, c, w_ada, b_ada, norm1_w, w_in, b_gate, mu_shift, w0, w2, a0, a2, g2, k_k, k_a, r_k, lnx_w, lnx_b, w_att_out, w_rwkv_out, w_o, norm2_w, w_up, conv_w, conv_b, w_down, norm_f_w, loss_target, m_w_ada, m_b_ada, m_norm1_w, m_w_in, m_b_gate, m_mu_shift, m_w0, m_w2, m_a0, m_a2, m_g2, m_k_k, m_k_a, m_r_k, m_lnx_w, m_lnx_b, m_w_att_out, m_w_rwkv_out, m_w_o, m_norm2_w, m_w_up, m_conv_w, m_conv_b, m_w_down, m_norm_f_w, v_w_ada, v_b_ada, v_norm1_w, v_w_in, v_b_gate, v_mu_shift, v_w0, v_w2, v_a0, v_a2, v_g2, v_k_k, v_k_a, v_r_k, v_lnx_w, v_lnx_b, v_w_att_out, v_w_rwkv_out, v_w_o, v_norm2_w, v_w_up, v_conv_w, v_conv_b, v_w_down, v_norm_f_w):
    args = dict(locals())
    p, pm, pv = {}, {}, {}
    for name in _NAMES:
        for dst, key in ((p, name), (pm, "m_" + name), (pv, "v_" + name)):
            t = args[key]
            dst[name] = t.reshape(1, -1) if name in ("r_k", "norm_f_w") else t.reshape(t.shape[-2], t.shape[-1])
    xi, yi, ci = _me()
    chip = 2 * xi + yi
    dev = 4 * xi + 2 * yi + ci
    x2, tgt = x[0], loss_target[0]

    n_cw = 3 * (2 * D_FF // 4)
    vec = jnp.concatenate([c.reshape(-1), p["conv_w"].reshape(-1), jnp.zeros((8 * D - D - n_cw,), F32)]).reshape(8, D)
    g0 = _allgather8(vec, "gather_c").reshape(8, 8 * D)
    c_all = g0[:, :D]
    conv_w_full = jnp.concatenate([g0[2 * j, D:D + n_cw].reshape(3, -1) for j in range(4)], axis=1)
    n_ada = 6 * D // 4
    b_ada_sh = lax.dynamic_slice(p["b_ada"], (0, chip * n_ada), (1, n_ada))
    ada_sh = _ada_fwd(c_all, p["w_ada"], b_ada_sh)
    ga = _allgather8(ada_sh, "gather_ada")
    ada_all = jnp.concatenate([ga[2 * j] for j in range(4)], axis=1)
    ada_row = lax.dynamic_slice(ada_all, (dev, 0), (1, 6 * D))
    ada = [ada_row[:, j * D:(j + 1) * D] for j in range(6)]

    big = [n for n, _ in _BIG]
    shard = {n: p[n].astype(BF16) for n in big}
    wts = dict(zip(_NEEDED_FIRST, _run_comm(_GatherWeights([shard[n] for n in _NEEDED_FIRST]), "gather_w")))
    for n in _SMALL:
        wts[n] = p[n]
    wts["conv_w"] = conv_w_full
    core = ci.reshape(1).astype(jnp.int32)

    def chip_parts(gw, names):
        recv = _reduce_sibling([gw[n] for n in names], "reduce_sib_" + names[0])
        return [_half_sum(lambda a, b: a + b, [gw[n]], [r], False, BF16, core, "reduce_add2_" + n)
                for n, r in zip(names, recv)]

    st = _step_to_scan(x2, tgt, ada, wts)
    y_raw, s0s, late = _scan_fwd(st["zs"], st["lw"], st["km"], st["aa"], st["bb"],
                                 _GatherWeights([shard[n] for n in _NEEDED_LATER]))
    st = _step_between_scans(st, y_raw, dict(zip(_NEEDED_LATER, late)))
    scan_grads, slots_early = _scan_bwd(st["zs"], st["lw"], st["km"], st["aa"], st["bb"], s0s, st["dy_raw"],
                                        _ScatterToChips(chip_parts(st["gw"], _DONE_EARLY)))
    loss_part, grad_x, d_ada, gw = _step_after_scan(st, scan_grads)

    small = [jnp.concatenate(d_ada, axis=1)] + [gw[n] for n in _SMALL[1:]] + [gw["conv_w"], loss_part.reshape(1, 1)]
    sizes = [t.size for t in small]
    flat = jnp.concatenate([t.reshape(-1) for t in small])
    npad = (-flat.shape[0]) % (8 * D)
    srows = (flat.shape[0] + npad) // D
    flat = jnp.concatenate([flat, jnp.zeros((npad,), F32)]).reshape(srows, D)
    parts = _allgather8(flat, "gather_small")
    tot = _sum_lead(parts, "sum_small").reshape(-1)
    pieces, pos = [], 0
    for sz in sizes:
        pieces.append(tot[pos:pos + sz])
        pos += sz
    grads = {}
    for n, piece in zip(_SMALL, pieces[:len(_SMALL)]):
        grads[n] = piece.reshape(p[n].shape)
    conv_w_grad = pieces[len(_SMALL)].reshape(3, 2 * D_FF)
    grads["conv_w"] = lax.dynamic_slice(conv_w_grad, (0, chip * (n_cw // 3)), (3, n_cw // 3))
    loss = pieces[-1][0]
    d_ada_all = parts[:, :6].reshape(8, 6 * D)
    grads["w_ada"] = _ada_bwd(c_all.T, lax.dynamic_slice(d_ada_all, (0, chip * n_ada), (8, n_ada)))

    slots_late = _run_comm(_ScatterToChips(chip_parts(gw, _DONE_LATE)), "reduce_chips")
    order = _DONE_EARLY + _DONE_LATE
    reds = [_half_sum(lambda t: t[0] + t[1] + t[2] + t[3], [], [t], True, F32, core, "reduce_add4_" + n)
            for n, t in zip(order, list(slots_early) + list(slots_late))]
    for n, g in zip(order, _reduce_finish(reds, "reduce_sib2")):
        grads[n] = g

    outs_g, outs_d, outs_m, outs_v = [], [], [], []
    for name in _NAMES:
        g = grads[name]
        d, m, v = _adamw(p[name], g, pm[name], pv[name], "adamw_" + name)
        shape = args[name].shape
        outs_g.append(g.reshape(shape))
        outs_d.append(d.reshape(shape))
        outs_m.append(m.reshape(shape))
        outs_v.append(v.reshape(shape))
    return (loss, grad_x.reshape(x.shape), *outs_g, *outs_d, *outs_m, *outs_v)
```

```python
import functools

import jax
import jax.numpy as jnp
from jax import lax
from jax.experimental import pallas as pl
from jax.experimental.pallas import tpu as pltpu

F32 = jnp.float32
BF16 = jnp.bfloat16
HI = lax.Precision.HIGHEST
MESH = pl.DeviceIdType.MESH

D = 1024
ATT_PATTERNS = ((128, 1), (512, 4), (2048, 16))
ATT_BLOCK = 128
ATT_WIDTH = 512
N_ATT = 3 * 3 * ATT_WIDTH
N_RW = 3 * D + 64 + 64 + 160
N_RWP = 3456
N_LORA = N_RWP - 3 * D
N_GATE = 2 * D
D_FF = 2816
RMS_EPS = 1e-6
GN_EPS = 64e-5
SCAN_CHUNK = 64
SCAN_PAIRS = 8
NEG = -1e30
VMEM_LIMIT = 48 * 1024 * 1024

ADAM_LR, ADAM_B1, ADAM_B2, ADAM_EPS, ADAM_WD, ADAM_STEP = 0.001, 0.9, 0.999, 1e-08, 0.01, 10


def _pcall(body, **kw):
    return pl.pallas_call(body, **kw)


def _cparams(sem):
    return pltpu.CompilerParams(dimension_semantics=sem, vmem_limit_bytes=VMEM_LIMIT)


def _div(n, pref, mult):
    best = None
    d = mult
    while d <= min(n, pref):
        if n % d == 0:
            best = d
        d += mult
    return best if best else n


def _dg(a, b, ca, cb):
    return lax.dot_general(a.astype(BF16), b.astype(BF16), (((ca,), (cb,)), ((), ())), preferred_element_type=F32)


@jax.custom_vjp
def _nn(a, b):
    return _dg(a, b, 1, 0)


@jax.custom_vjp
def _nt(a, b):
    return _dg(a, b, 1, 1)


@jax.custom_vjp
def _tn(a, b):
    return _dg(a, b, 0, 0)


_nn.defvjp(lambda a, b: (_nn(a, b), (a, b)), lambda res, g: (_nt(g, res[1]), _tn(res[0], g)))
_nt.defvjp(lambda a, b: (_nt(a, b), (a, b)), lambda res, g: (_nn(g, res[1]), _tn(g, res[0])))
_tn.defvjp(lambda a, b: (_tn(a, b), (a, b)), lambda res, g: (_nt(res[1], g), _nn(res[0], g)))


def _bdg(a, b, ca, cb):
    return lax.dot_general(a.astype(BF16), b.astype(BF16), (((ca,), (cb,)), ((0,), (0,))), preferred_element_type=F32)


@jax.custom_vjp
def _bnn(a, b):
    return _bdg(a, b, 2, 1)


@jax.custom_vjp
def _bnt(a, b):
    return _bdg(a, b, 2, 2)


@jax.custom_vjp
def _btn(a, b):
    return _bdg(a, b, 1, 1)


_bnn.defvjp(lambda a, b: (_bnn(a, b), (a, b)), lambda res, g: (_bnt(g, res[1]), _btn(res[0], g)))
_bnt.defvjp(lambda a, b: (_bnt(a, b), (a, b)), lambda res, g: (_bnn(g, res[1]), _btn(g, res[0])))
_btn.defvjp(lambda a, b: (_btn(a, b), (a, b)), lambda res, g: (_bnt(res[1], g), _bnn(res[0], g)))


def _split2(x):
    hi = x.astype(BF16)
    lo = (x - hi.astype(F32)).astype(BF16)
    return hi, lo


def _hsum_impl(x, e, et):
    eb, etb = e.astype(BF16), et.astype(BF16)
    hi, lo = _split2(x)
    s = jnp.dot(hi, eb, preferred_element_type=F32) + jnp.dot(lo, eb, preferred_element_type=F32)
    shi, slo = _split2(s)
    return jnp.dot(shi, etb, preferred_element_type=F32) + jnp.dot(slo, etb, preferred_element_type=F32)


@jax.custom_vjp
def _hsum(x, e, et):
    return _hsum_impl(x, e, et)


_hsum.defvjp(lambda x, e, et: (_hsum_impl(x, e, et), (e, et)),
             lambda res, g: (_hsum_impl(g, res[0], res[1]), jnp.zeros_like(res[0]), jnp.zeros_like(res[1])))


def _mm(a, b, *, ta=False, tb=False, out_dtype=F32, add=None, b_chip=False, out_chip=False, name):
    if ta:
        kdim, m = a.shape
    else:
        m, kdim = a.shape
    if b_chip:
        n = b.shape[1] if tb else 4 * b.shape[2]
    else:
        n = b.shape[0] if tb else b.shape[1]
    tm, tn, tk = _div(m, 1536, 128), _div(n, 1536, 128), _div(kdim, 1408, 128)
    if b_chip and tb:
        tk = kdim // 4
    if (b_chip and not tb) or out_chip:
        tn = n // 4
    nk = kdim // tk
    ca, cb = (0 if ta else 1), (1 if tb else 0)

    def body(*refs):
        a_ref, b_ref = refs[0], refs[1]
        add_ref = None if add is None else refs[2]
        o_ref = refs[2 if add is None else 3]
        part = lax.dot_general(a_ref[...], b_ref[...], (((ca,), (cb,)), ((), ())), preferred_element_type=F32)

        def finish(r):
            if add_ref is not None:
                r = r + add_ref[...]
            o_ref[...] = r.astype(o_ref.dtype)

        if nk == 1:
            finish(part)
            return
        acc = refs[-1]
        k = pl.program_id(2)

        @pl.when(k == 0)
        def _():
            acc[...] = part

        @pl.when(k > 0)
        def _():
            acc[...] += part

        @pl.when(k == nk - 1)
        def _():
            finish(acc[...])

    a_spec = pl.BlockSpec((tk, tm), lambda i, j, k: (k, i)) if ta else pl.BlockSpec((tm, tk), lambda i, j, k: (i, k))
    if b_chip:
        b_spec = (pl.BlockSpec((None, tn, tk), lambda i, j, k: (k, j, 0)) if tb
                  else pl.BlockSpec((None, tk, tn), lambda i, j, k: (j, k, 0)))
    else:
        b_spec = pl.BlockSpec((tn, tk), lambda i, j, k: (j, k)) if tb else pl.BlockSpec((tk, tn), lambda i, j, k: (k, j))
    in_specs = [a_spec, b_spec]
    args = [a, b]
    if add is not None:
        in_specs.append(pl.BlockSpec((tm, tn), lambda i, j, k: (i, j)))
        args.append(add)
    if out_chip:
        out_spec = pl.BlockSpec((None, tm, tn), lambda i, j, k: (j, i, 0))
        out_shape = jax.ShapeDtypeStruct((4, m, tn), out_dtype)
    else:
        out_spec = pl.BlockSpec((tm, tn), lambda i, j, k: (i, j))
        out_shape = jax.ShapeDtypeStruct((m, n), out_dtype)
    return _pcall(
        body, name=name, grid=(m // tm, n // tn, nk), in_specs=in_specs, out_specs=out_spec, out_shape=out_shape,
        scratch_shapes=[] if nk == 1 else [pltpu.VMEM((tm, tn), F32)],
        compiler_params=_cparams(("parallel", "parallel", "arbitrary")),
    )(*args)


def _row_spec(br, w, cb):
    return pl.BlockSpec((br, w), lambda i: (i, cb))


def _const_spec(shape):
    return pl.BlockSpec(shape, lambda i: (0,) * len(shape))


def _rows_fwd(fn, rows, consts, outs, *, name, br, acc_shape=None):
    s = rows[0][0].shape[0]
    nr, nc = len(rows), len(consts)
    kept = [k for k, o in enumerate(outs) if o is not None]

    def body(*refs):
        xs = [r[...].astype(F32) for r in refs[:nr]]
        cs = [c[...] for c in refs[nr:nr + nc]]
        res = fn(*xs, *cs)
        orefs = refs[nr + nc:]
        for j, k in enumerate(kept):
            orefs[j][...] = res[k].astype(orefs[j].dtype)
        if acc_shape is not None:
            acc_ref = orefs[len(kept)]

            @pl.when(pl.program_id(0) == 0)
            def _():
                acc_ref[...] = jnp.zeros_like(acc_ref)

            acc_ref[...] += res[len(outs)]

    in_specs = [_row_spec(br, w, cb) for (_, w, cb) in rows] + [_const_spec(c.shape) for c in consts]
    out_specs = [_row_spec(br, outs[k][0], 0) for k in kept]
    out_shape = [jax.ShapeDtypeStruct((s, outs[k][0]), outs[k][1]) for k in kept]
    if acc_shape is not None:
        out_specs.append(_const_spec(acc_shape))
        out_shape.append(jax.ShapeDtypeStruct(acc_shape, F32))
    return _pcall(
        body, name=name, grid=(s // br,), in_specs=in_specs, out_specs=out_specs, out_shape=out_shape,
        compiler_params=_cparams(("arbitrary",)),
    )(*[r[0] for r in rows], *consts)


def _rows_bwd(fn, rows, consts, cots, *, wrt_rows, wrt_consts, drow_dtypes, name, br, unit_cot=False):
    s = rows[0][0].shape[0]
    nr, nc = len(rows), len(consts)
    flat_cots = [c for lst in cots for c in lst]
    ncot = len(flat_cots)

    def body(*refs):
        xs = [r[...].astype(F32) for r in refs[:nr]]
        cs = [c[...] for c in refs[nr:nr + nc]]
        cvals = [c[...].astype(F32) for c in refs[nr + nc:nr + nc + ncot]]
        orefs = refs[nr + nc + ncot:]

        def g(*d):
            xs2, cs2 = list(xs), list(cs)
            for j, k in enumerate(wrt_rows):
                xs2[k] = d[j]
            for j, k in enumerate(wrt_consts):
                cs2[k] = d[len(wrt_rows) + j]
            return tuple(fn(*xs2, *cs2))

        prim = [xs[k] for k in wrt_rows] + [cs[k] for k in wrt_consts]
        outs, vjp = jax.vjp(g, *prim)
        ct = []
        pos = 0
        for o, lst in zip(outs, cots):
            if unit_cot:
                ct.append(jnp.ones_like(o))
                continue
            acc = jnp.zeros_like(o)
            for _ in lst:
                acc = acc + cvals[pos]
                pos += 1
            ct.append(acc)
        grads = vjp(tuple(ct))
        for j in range(len(wrt_rows)):
            orefs[j][...] = grads[j].astype(orefs[j].dtype)

        @pl.when(pl.program_id(0) == 0)
        def _():
            for j in range(len(wrt_consts)):
                oref = orefs[len(wrt_rows) + j]
                oref[...] = jnp.zeros_like(oref)

        for j in range(len(wrt_consts)):
            orefs[len(wrt_rows) + j][...] += grads[len(wrt_rows) + j]

    in_specs = ([_row_spec(br, w, cb) for (_, w, cb) in rows] + [_const_spec(c.shape) for c in consts]
                + [_row_spec(br, w, cb) for (_, w, cb) in flat_cots])
    out_specs = [_row_spec(br, rows[k][1], 0) for k in wrt_rows] + [_const_spec(consts[k].shape) for k in wrt_consts]
    out_shape = ([jax.ShapeDtypeStruct((s, rows[k][1]), dt) for k, dt in zip(wrt_rows, drow_dtypes)]
                 + [jax.ShapeDtypeStruct(consts[k].shape, F32) for k in wrt_consts])
    return _pcall(
        body, name=name, grid=(s // br,), in_specs=in_specs, out_specs=out_specs, out_shape=out_shape,
        compiler_params=_cparams(("arbitrary",)),
    )(*[r[0] for r in rows], *consts, *[c[0] for c in flat_cots])


def _rms(x, w):
    return x * lax.rsqrt(jnp.mean(x * x, axis=-1, keepdims=True) + RMS_EPS) * w


def _softplus(x):
    return jnp.maximum(x, 0.0) + jnp.log(1.0 + jnp.exp(-jnp.abs(x)))


def _f_pre(x, nw, sc, sh):
    return _rms(x, nw) * (1.0 + sc) + sh, x


def _f_pre2(x, o, gt, nw, sc, sh):
    x1 = x + gt * o
    return x1, _rms(x1, nw) * (1.0 + sc) + sh


def _f_fin(x1, f, tgt, gt, nfw):
    y = _rms(x1 + gt * f, nfw)
    return (0.5 * jnp.mean(jnp.square(y - tgt), axis=-1, keepdims=True),)


def _f_comb(o1, o2, o3, l1, l2, l3):
    m = lax.stop_gradient(jnp.maximum(jnp.maximum(l1, l2), l3))
    e1, e2, e3 = jnp.exp(l1 - m), jnp.exp(l2 - m), jnp.exp(l3 - m)
    return ((e1 * o1 + e2 * o2 + e3 * o3) / (e1 + e2 + e3),)


def _f_rwpre(zs, w0, a0, k_k, k_a, wl, e, et):
    r, k, v, zl = zs[:, 0:D], zs[:, D:2 * D], zs[:, 2 * D:3 * D], zs[:, 3 * D:N_RWP]
    lane = lax.broadcasted_iota(jnp.int32, zl.shape, 1)
    t = jnp.where(lane < 64, jnp.tanh(zl), jnp.where(lane < 128, zl, jnp.where(lane < 288, jax.nn.sigmoid(zl), 0.0)))
    lo = _nn(t[:, 0:128], wl[0:128, 0:2 * D])
    g = _nn(t[:, 128:N_LORA], wl[128:N_LORA, 2 * D:3 * D])
    w_log = -_softplus(-(w0 + lo[:, 0:D])) - 0.5
    lw = -jnp.exp(w_log)
    a = jax.nn.sigmoid(a0 + lo[:, D:2 * D])
    k_mod = k * (1.0 + (a - 1.0) * k_a)
    kk = k * k_k
    kk = kk / jnp.maximum(jnp.sqrt(_hsum(kk * kk, e, et)), 1e-12)
    return r, lw, k_mod, v, -kk, kk * a, g


def _f_rwpost(y, r, v, k_mod, g, lnx_w, lnx_b, r_k, e, et):
    mean = _hsum(y, e, et) * (1.0 / 64)
    yc = y - mean
    var = _hsum(yc * yc, e, et) * (1.0 / 64)
    yn = yc * lax.rsqrt(var + GN_EPS) * lnx_w + lnx_b
    bonus = _hsum(r * k_mod * r_k, e, et) * v
    return ((yn + bonus) * g,)


def _f_mix(gia, gir, ya, yr, bga, bgr):
    return (jax.nn.sigmoid(gia + bga) * ya + jax.nn.sigmoid(gir + bgr) * yr,)


def _f_adamw(w, g, m, v):
    m = ADAM_B1 * m + (1.0 - ADAM_B1) * g
    v = ADAM_B2 * v + (1.0 - ADAM_B2) * jnp.square(g)
    m_hat = m / (1.0 - ADAM_B1 ** ADAM_STEP)
    v_hat = v / (1.0 - ADAM_B2 ** ADAM_STEP)
    return -ADAM_LR * (m_hat / (jnp.sqrt(v_hat) + ADAM_EPS) + ADAM_WD * w), m, v


def _down(x, k):
    row = lax.broadcasted_iota(jnp.int32, x.shape, 0)
    return jnp.where(row < k, 0.0, pltpu.roll(x, k, 0))


def _up(x, k):
    n = x.shape[0]
    row = lax.broadcasted_iota(jnp.int32, x.shape, 0)
    return jnp.where(row >= n - k, 0.0, pltpu.roll(x, n - k, 0))


def _col_spec(s, w, off=0):
    return pl.BlockSpec((s, w), lambda j: (0, j + off))


def _shift_fwd(z, mu):
    s, n = z.shape

    def body(z_ref, mu_ref, o_ref):
        zz = z_ref[...]
        o_ref[...] = zz + (_down(zz, 1) - zz) * mu_ref[...]

    return _pcall(
        body, name="shift_fwd", grid=(n // 128,), in_specs=[_col_spec(s, 128), _col_spec(1, 128)],
        out_specs=_col_spec(s, 128), out_shape=jax.ShapeDtypeStruct((s, n), F32),
        compiler_params=_cparams(("parallel",)),
    )(z, mu)


def _shift_bwd(z, mu, dzs):
    s, n = z.shape

    def body(z_ref, mu_ref, d_ref, dz_ref, dmu_ref):
        zz, d, m = z_ref[...], d_ref[...], mu_ref[...]
        dm = d * m
        dz_ref[...] = (d - dm + _up(dm, 1)).astype(dz_ref.dtype)
        dmu_ref[...] = jnp.sum(d * (_down(zz, 1) - zz), axis=0, keepdims=True)

    return _pcall(
        body, name="shift_bwd", grid=(n // 128,), in_specs=[_col_spec(s, 128), _col_spec(1, 128), _col_spec(s, 128)],
        out_specs=[_col_spec(s, 128), _col_spec(1, 128)],
        out_shape=[jax.ShapeDtypeStruct((s, n), BF16), jax.ShapeDtypeStruct((1, n), F32)],
        compiler_params=_cparams(("parallel",)),
    )(z, mu, dzs)


def _conv3(x, w_ref, b_ref):
    return b_ref[...] + w_ref[0:1, :] * _down(x, 2) + w_ref[1:2, :] * _down(x, 1) + w_ref[2:3, :] * x


def _conv_fwd(u, cw, cb):
    s = u.shape[0]
    nb = D_FF // 128

    def body(ug_ref, uv_ref, wg_ref, wv_ref, bg_ref, bv_ref, o_ref):
        gate = _conv3(ug_ref[...], wg_ref, bg_ref)
        val = _conv3(uv_ref[...], wv_ref, bv_ref)
        o_ref[...] = (gate * jax.nn.sigmoid(gate) * val).astype(o_ref.dtype)

    return _pcall(
        body, name="conv_fwd", grid=(nb,),
        in_specs=[_col_spec(s, 128), _col_spec(s, 128, nb), _col_spec(3, 128), _col_spec(3, 128, nb),
                  _col_spec(1, 128), _col_spec(1, 128, nb)],
        out_specs=_col_spec(s, 128), out_shape=jax.ShapeDtypeStruct((s, D_FF), BF16),
        compiler_params=_cparams(("parallel",)),
    )(u, u, cw, cw, cb, cb)


def _conv_bwd(u, cw, cb, dact):
    s = u.shape[0]
    nb = D_FF // 128

    def half(x, d, w_ref, du_ref, dw_ref, db_ref):
        x1, x2 = _down(x, 1), _down(x, 2)
        du_ref[...] = (w_ref[2:3, :] * d + w_ref[1:2, :] * _up(d, 1) + w_ref[0:1, :] * _up(d, 2)).astype(du_ref.dtype)
        dw_ref[0:1, :] = jnp.sum(d * x2, axis=0, keepdims=True)
        dw_ref[1:2, :] = jnp.sum(d * x1, axis=0, keepdims=True)
        dw_ref[2:3, :] = jnp.sum(d * x, axis=0, keepdims=True)
        db_ref[...] = jnp.sum(d, axis=0, keepdims=True)

    def body(ug_ref, uv_ref, wg_ref, wv_ref, bg_ref, bv_ref, da_ref,
             dug_ref, duv_ref, dwg_ref, dwv_ref, dbg_ref, dbv_ref):
        ug, uv, da = ug_ref[...], uv_ref[...], da_ref[...]
        gate = _conv3(ug, wg_ref, bg_ref)
        val = _conv3(uv, wv_ref, bv_ref)
        sg = jax.nn.sigmoid(gate)
        dgate = da * val * sg * (1.0 + gate * (1.0 - sg))
        dval = da * gate * sg
        half(ug, dgate, wg_ref, dug_ref, dwg_ref, dbg_ref)
        half(uv, dval, wv_ref, duv_ref, dwv_ref, dbv_ref)

    dug, duv, dwg, dwv, dbg, dbv = _pcall(
        body, name="conv_bwd", grid=(nb,),
        in_specs=[_col_spec(s, 128), _col_spec(s, 128, nb), _col_spec(3, 128), _col_spec(3, 128, nb),
                  _col_spec(1, 128), _col_spec(1, 128, nb), _col_spec(s, 128)],
        out_specs=[_col_spec(s, 128), _col_spec(s, 128), _col_spec(3, 128), _col_spec(3, 128),
                   _col_spec(1, 128), _col_spec(1, 128)],
        out_shape=[jax.ShapeDtypeStruct((s, D_FF), BF16), jax.ShapeDtypeStruct((s, D_FF), BF16),
                   jax.ShapeDtypeStruct((3, D_FF), F32), jax.ShapeDtypeStruct((3, D_FF), F32),
                   jax.ShapeDtypeStruct((1, D_FF), F32), jax.ShapeDtypeStruct((1, D_FF), F32)],
        compiler_params=_cparams(("parallel",)),
    )(u, u, cw, cw, cb, cb, dact)
    return (jnp.concatenate([dug, duv], axis=1), jnp.concatenate([dwg, dwv], axis=1),
            jnp.concatenate([dbg, dbv], axis=1))


ATT_BATCH = 4


def _att_batch(q, kp, kc, vp, vc, first):
    ma = lax.broadcasted_iota(jnp.int32, (1, ATT_BLOCK, 128), 2) < 64
    qs = jnp.concatenate([jnp.where(ma, q, 0.0), jnp.where(ma, 0.0, q)], axis=1)
    qi = lax.broadcasted_iota(jnp.int32, (1, 2 * ATT_BLOCK, ATT_BLOCK), 1) & (ATT_BLOCK - 1)
    kj = lax.broadcasted_iota(jnp.int32, (1, 2 * ATT_BLOCK, ATT_BLOCK), 2)
    okp = kj >= qi + jnp.where(first, 2 * ATT_BLOCK, 0)
    okc = kj <= qi
    sp = jnp.where(okp, _bnt(qs, kp) * 0.125, NEG)
    sc = jnp.where(okc, _bnt(qs, kc) * 0.125, NEG)
    m = lax.stop_gradient(jnp.maximum(jnp.max(sp, axis=-1, keepdims=True), jnp.max(sc, axis=-1, keepdims=True)))
    pp, pc = jnp.exp(sp - m), jnp.exp(sc - m)
    den = jnp.sum(pp, axis=-1, keepdims=True) + jnp.sum(pc, axis=-1, keepdims=True)
    o_s = (_bnn(pp, vp) + _bnn(pc, vc)) / den
    l_s = jnp.broadcast_to(m + jnp.log(den), o_s.shape)
    return (jnp.where(ma, o_s[:, :ATT_BLOCK], o_s[:, ATT_BLOCK:]), jnp.where(ma, l_s[:, :ATT_BLOCK], l_s[:, ATT_BLOCK:]))


def _att_pairs_per_step(dil):
    return ATT_BATCH if dil == 1 else 1


def _att_specs(g, dil):
    rows, pp = ATT_BLOCK * dil, _att_pairs_per_step(dil)

    def cur(slot):
        return pl.BlockSpec((rows, 128 * pp), lambda n, p: (n, (g * 3 + slot) * (4 // pp) + p))

    def prev(slot):
        return pl.BlockSpec((rows, 128 * pp), lambda n, p: (jnp.maximum(n - 1, 0), (g * 3 + slot) * (4 // pp) + p))

    return [cur(0), prev(1), cur(1), prev(2), cur(2)]


def _att_out_spec(dil):
    return pl.BlockSpec((ATT_BLOCK * dil, 128 * _att_pairs_per_step(dil)), lambda n, p: (n, p))


def _att_grid(s, dil):
    return (s // (ATT_BLOCK * dil), 4 // _att_pairs_per_step(dil))


def _att_windows(i, dil):
    if dil == 1:
        return [(pl.ds(0, ATT_BLOCK), pl.ds(128 * j, 128)) for j in range(ATT_BATCH)]
    return [(pl.ds(i * ATT_BATCH + j, ATT_BLOCK, stride=dil), pl.ds(0, 128)) for j in range(ATT_BATCH)]


def _att_fwd(att_in, g, dil):
    s = att_in.shape[0]

    def body(q_ref, kp_ref, kc_ref, vp_ref, vc_ref, o_ref, l_ref):
        first = pl.program_id(0) == 0

        def one(i, carry):
            win = _att_windows(i, dil)
            vals = [jnp.stack([ref[w] for w in win]) for ref in (q_ref, kp_ref, kc_ref, vp_ref, vc_ref)]
            o, l = _att_batch(*vals, first)
            for j, w in enumerate(win):
                o_ref[w] = o[j]
                l_ref[w] = l[j]
            return carry

        lax.fori_loop(0, max(1, dil // ATT_BATCH), one, 0)

    return _pcall(
        body, name=f"att_fwd{g}", grid=_att_grid(s, dil), in_specs=_att_specs(g, dil),
        out_specs=[_att_out_spec(dil)] * 2, out_shape=[jax.ShapeDtypeStruct((s, ATT_WIDTH), F32)] * 2,
        compiler_params=_cparams(("parallel", "parallel")),
    )(att_in, att_in, att_in, att_in, att_in)


def _att_bwd(att_in, g, dil, do, dl):
    s = att_in.shape[0]
    nb = s // (ATT_BLOCK * dil)

    def body(q_ref, kp_ref, kc_ref, vp_ref, vc_ref, do_ref, dl_ref, dq_ref, dkp_ref, dkc_ref, dvp_ref, dvc_ref):
        first = pl.program_id(0) == 0

        def one(i, carry):
            win = _att_windows(i, dil)
            vals = [jnp.stack([ref[w] for w in win]) for ref in (q_ref, kp_ref, kc_ref, vp_ref, vc_ref)]
            _, vjp = jax.vjp(lambda *a: _att_batch(*a, first), *vals)
            grads = vjp((jnp.stack([do_ref[w] for w in win]), jnp.stack([dl_ref[w] for w in win])))
            for ref, gr in zip((dq_ref, dkp_ref, dkc_ref, dvp_ref, dvc_ref), grads):
                for j, w in enumerate(win):
                    ref[w] = gr[j]
            return carry

        lax.fori_loop(0, max(1, dil // ATT_BATCH), one, 0)

    dq, dkp, dkc, dvp, dvc = _pcall(
        body, name=f"att_bwd{g}", grid=_att_grid(s, dil), in_specs=_att_specs(g, dil) + [_att_out_spec(dil)] * 2,
        out_specs=[_att_out_spec(dil)] * 5, out_shape=[jax.ShapeDtypeStruct((s, ATT_WIDTH), F32)] * 5,
        compiler_params=_cparams(("parallel", "parallel")),
    )(att_in, att_in, att_in, att_in, att_in, do, dl)

    unit = ATT_BLOCK * dil
    per = max(1, 1024 // unit)
    width = ATT_WIDTH if per > 1 else 128
    steps = nb // per

    def with_next(cur_ref, prev_ref, next_ref, has_next):
        tail = jnp.where(has_next, next_ref[...], 0.0)
        shifted = tail if per == 1 else jnp.concatenate([prev_ref[unit:, :], tail], axis=0)
        return (cur_ref[...] + shifted).astype(BF16)

    def cbody(dq_ref, dkc_ref, dkp_ref, dkn_ref, dvc_ref, dvp_ref, dvn_ref, oq_ref, ok_ref, ov_ref):
        has_next = pl.program_id(0) + 1 < steps
        oq_ref[...] = dq_ref[...].astype(BF16)
        ok_ref[...] = with_next(dkc_ref, dkp_ref, dkn_ref, has_next)
        ov_ref[...] = with_next(dvc_ref, dvp_ref, dvn_ref, has_next)

    cur = pl.BlockSpec((per * unit, width), lambda n, p: (n, p))
    nxt = pl.BlockSpec((unit, width), lambda n, p: (jnp.minimum((n + 1) * per, nb - 1), p))
    return _pcall(
        cbody, name=f"att_bwd_sum{g}", grid=(steps, ATT_WIDTH // width), in_specs=[cur, cur, cur, nxt, cur, cur, nxt],
        out_specs=[cur] * 3, out_shape=[jax.ShapeDtypeStruct((s, ATT_WIDTH), BF16)] * 3,
        compiler_params=_cparams(("parallel", "parallel")),
    )(dq, dkc, dkp, dkp, dvc, dvp, dvp)


def _unit_lower_inverse_impl(n):
    eye = (lax.broadcasted_iota(jnp.int32, (1,) + n.shape[1:], 1)
           == lax.broadcasted_iota(jnp.int32, (1,) + n.shape[1:], 2))
    t = jnp.where(eye, 1.0, 0.0) + n
    pw = n
    for _ in range(5):
        pw = _bnn(pw, pw)
        t = t + _bnn(t, pw)
    return t


@jax.custom_vjp
def _unit_lower_inverse(n):
    return _unit_lower_inverse_impl(n)


def _unit_lower_inverse_fwd(n):
    t = _unit_lower_inverse_impl(n)
    return t, t


_unit_lower_inverse.defvjp(_unit_lower_inverse_fwd, lambda t, g: (_bnt(_btn(t, g), t),))


def _scan_chunk(r, lw, k, v, a, b, s0):
    c = SCAN_CHUNK
    p = s0.shape[0]
    ri = lax.broadcasted_iota(jnp.int32, (c, c), 0)
    ci = lax.broadcasted_iota(jnp.int32, (c, c), 1)
    cum = jnp.dot((ci <= ri).astype(F32), lw, precision=HI, preferred_element_type=F32)
    tot = jnp.sum(lw, axis=0, keepdims=True)
    ma = (lax.broadcasted_iota(jnp.int32, (c, 128 * p), 1) & 127) < 64

    def pairs(x):
        return jnp.concatenate([x[None, :, 128 * j:128 * (j + 1)] for j in range(p)], axis=0)

    def stack(x):
        return jnp.concatenate([pairs(jnp.where(ma, x, 0.0)), pairs(jnp.where(ma, 0.0, x))], axis=1)

    einv, eend = jnp.exp(-cum), jnp.exp(tot - cum)
    ra, aa = stack(r * jnp.exp(cum)), stack(a * jnp.exp(cum - lw))
    bi, ki, be, ke, vs = stack(b * einv), stack(k * einv), stack(b * eend), stack(k * eend), stack(v)
    r2 = lax.broadcasted_iota(jnp.int32, (1, 2 * c, 2 * c), 1)
    c2 = lax.broadcasted_iota(jnp.int32, (1, 2 * c, 2 * c), 2)
    same = (r2 >= c) == (c2 >= c)
    strict = jnp.logical_and(same, c2 < r2)
    incl = jnp.logical_and(same, c2 <= r2)
    s0 = jnp.where(same, s0, 0.0)
    prod = _bnt(jnp.concatenate([aa, ra], axis=1), jnp.concatenate([bi, ki], axis=1))
    a_ab = jnp.where(strict, prod[:, :2 * c, :2 * c], 0.0)
    a_ak = jnp.where(strict, prod[:, :2 * c, 2 * c:], 0.0)
    a_rb = jnp.where(incl, prod[:, 2 * c:, :2 * c], 0.0)
    a_rk = jnp.where(incl, prod[:, 2 * c:, 2 * c:], 0.0)
    t = _unit_lower_inverse(a_ab)
    u = _bnn(t, _bnt(aa, s0) + _bnn(a_ak, vs))
    uv = jnp.concatenate([u, vs], axis=1)
    ys = _bnt(ra, s0) + _bnn(jnp.concatenate([a_rb, a_rk], axis=2), uv)
    s1 = s0 * pairs(jnp.exp(tot)) + _btn(uv, jnp.concatenate([be, ke], axis=1))
    y3 = ys[:, :c] + ys[:, c:]
    return jnp.concatenate([y3[j] for j in range(p)], axis=1), s1


def _scan_specs(rev, n):
    def at(i):
        return n - 1 - i if rev else i

    def cm(cb):
        return pl.BlockSpec((SCAN_CHUNK, D), lambda i: (at(i), cb))

    return cm, pl.BlockSpec((1, SCAN_PAIRS, 128, 128), lambda i: (at(i), 0, 0, 0))


def _comm_phases(comm, refs, n):
    k = comm.n
    srcs, outs, sems = refs[:k], refs[k:2 * k], refs[2 * k:]
    i = pl.program_id(0)

    def before():
        @pl.when(i == 0)
        def _():
            comm.first(srcs, outs, sems)

    def after():
        if comm.mid is not None:
            @pl.when(i == (3 * n) // 4)
            def _():
                comm.mid(srcs, outs, sems)

        @pl.when(i == n - 1)
        def _():
            comm.last(srcs, outs, sems)

    return before, after


def _scan_fwd(zs, lw, km, aa, bb, comm):
    s = zs.shape[0]
    n = s // SCAN_CHUNK
    cm, st = _scan_specs(False, n)
    k = comm.n

    def body(*refs):
        r_ref, lw_ref, k_ref, v_ref, a_ref, b_ref = refs[:6]
        y_ref, s0_ref = refs[6 + k:8 + k]
        state = refs[8 + 2 * k]
        before, after = _comm_phases(comm, refs[6:6 + k] + refs[8 + k:8 + 2 * k] + refs[9 + 2 * k:], n)
        before()

        @pl.when(pl.program_id(0) == 0)
        def _():
            state[...] = jnp.zeros_like(state)

        s0 = state[...]
        s0_ref[0] = s0
        y, s1 = _scan_chunk(*[ref[...] for ref in (r_ref, lw_ref, k_ref, v_ref, a_ref, b_ref)], s0)
        y_ref[...] = y
        state[...] = s1
        after()

    res = _pcall(
        body, name="scan_fwd", grid=(n,), in_specs=[cm(0), cm(0), cm(0), cm(2), cm(0), cm(0)] + [_HBM] * k,
        out_specs=[cm(0), st] + [_HBM] * k,
        out_shape=[jax.ShapeDtypeStruct((s, D), F32), jax.ShapeDtypeStruct((n, 8, 128, 128), F32)] + comm.out_shape,
        scratch_shapes=[pltpu.VMEM((SCAN_PAIRS, 128, 128), F32)] + comm.sems,
        compiler_params=_cparams(("arbitrary",)),
    )(zs, lw, km, zs, aa, bb, *comm.ins)
    return res[0], res[1], res[2:]


def _scan_bwd(zs, lw, km, aa, bb, s0s, dy, comm):
    s = zs.shape[0]
    n = s // SCAN_CHUNK
    cm, st = _scan_specs(True, n)
    k = comm.n

    def body(*refs):
        r_ref, lw_ref, k_ref, v_ref, a_ref, b_ref, s0_ref, dy_ref = refs[:8]
        douts = refs[8 + k:14 + k]
        dstate = refs[14 + 2 * k]
        before, after = _comm_phases(comm, refs[8:8 + k] + refs[14 + k:14 + 2 * k] + refs[15 + 2 * k:], n)
        before()

        @pl.when(pl.program_id(0) == 0)
        def _():
            dstate[...] = jnp.zeros_like(dstate)

        prim = [ref[...] for ref in (r_ref, lw_ref, k_ref, v_ref, a_ref, b_ref)] + [s0_ref[0]]
        _, vjp = jax.vjp(_scan_chunk, *prim)
        grads = vjp((dy_ref[...], dstate[...]))
        for ref, gr in zip(douts, grads[:6]):
            ref[...] = gr
        dstate[...] = grads[6]
        after()

    res = _pcall(
        body, name="scan_bwd", grid=(n,),
        in_specs=[cm(0), cm(0), cm(0), cm(2), cm(0), cm(0), st, cm(0)] + [_HBM] * k,
        out_specs=[cm(0)] * 6 + [_HBM] * k, out_shape=[jax.ShapeDtypeStruct((s, D), F32)] * 6 + comm.out_shape,
        scratch_shapes=[pltpu.VMEM((SCAN_PAIRS, 128, 128), F32)] + comm.sems,
        compiler_params=_cparams(("arbitrary",)),
    )(zs, lw, km, zs, aa, bb, s0s, dy, *comm.ins)
    return res[:6], res[6:]


_HBM = pl.BlockSpec(memory_space=pltpu.HBM)


def _me():
    return lax.axis_index("x"), lax.axis_index("y"), lax.axis_index("c")


def _allgather8(src, name):
    def body(src_ref, out_ref, ssem, rsem, lsem):
        x, y, c = _me()
        me = 4 * x + 2 * y + c
        local = pltpu.make_async_copy(src_ref, out_ref.at[me], lsem)
        local.start()
        peers = []
        for k in range(1, 8):
            peers.append(((1 - x) if k & 4 else x, (1 - y) if k & 2 else y, (1 - c) if k & 1 else c))
        sends = []
        for k, peer in enumerate(peers):
            cp = pltpu.make_async_remote_copy(src_ref, out_ref.at[me], ssem.at[k], rsem.at[k], device_id=peer,
                                              device_id_type=MESH)
            cp.start()
            sends.append(cp)
        for k, (px, py, pc) in enumerate(peers):
            pltpu.make_async_remote_copy(src_ref, out_ref.at[4 * px + 2 * py + pc], ssem.at[k], rsem.at[k],
                                         device_id=(px, py, pc), device_id_type=MESH).wait_recv()
        for cp in sends:
            cp.wait_send()
        local.wait()

    return _pcall(
        body, name=name, in_specs=[_HBM], out_specs=_HBM, out_shape=jax.ShapeDtypeStruct((8,) + src.shape, src.dtype),
        scratch_shapes=[pltpu.SemaphoreType.DMA((7,)), pltpu.SemaphoreType.DMA((7,)), pltpu.SemaphoreType.DMA],
    )(src)


def _other_chips(x, y):
    return [(1 - x, y), (x, 1 - y), (1 - x, 1 - y)]


def _remote(src, dst, ssem, rsem, to):
    return pltpu.make_async_remote_copy(src, dst, ssem, rsem, device_id=to, device_id_type=MESH)


class _GatherWeights:
    def __init__(self, shards):
        self.ins = list(shards)
        n = self.n = len(shards)
        self.out_shape = [jax.ShapeDtypeStruct((4,) + t.shape, t.dtype) for t in shards]
        self.sems = [pltpu.SemaphoreType.DMA((6 * n,)), pltpu.SemaphoreType.DMA((6 * n,)),
                     pltpu.SemaphoreType.DMA((n,)), pltpu.SemaphoreType.DMA((n,))]

    def _copies(self, srcs, outs, sems):
        ssem, rsem, lsem, osem = sems
        x, y, c = _me()
        me = 2 * x + y
        own, ici, landed, passed, passed_in = [], [], [], [], []
        for a in range(self.n):
            h = self.ins[a].shape[0] // 2
            mine, other = pl.ds(c * h, h), pl.ds((1 - c) * h, h)
            own.append(_remote(srcs[a], outs[a].at[me], lsem.at[a], osem.at[a], (x, y, 1 - c)))
            for k, (px, py) in enumerate(_other_chips(x, y)):
                s1, r1, s2, r2 = ssem.at[6 * a + k], rsem.at[6 * a + k], ssem.at[6 * a + 3 + k], rsem.at[6 * a + 3 + k]
                got, got_sib = outs[a].at[2 * px + py, mine], outs[a].at[2 * px + py, other]
                ici.append(_remote(srcs[a].at[mine], outs[a].at[me, mine], s1, r1, (px, py, c)))
                landed.append(_remote(got, got, s1, r1, (px, py, c)))
                passed.append(_remote(got, got, s2, r2, (x, y, 1 - c)))
                passed_in.append(_remote(got_sib, got_sib, s2, r2, (x, y, 1 - c)))
        return own, ici, landed, passed, passed_in

    def first(self, srcs, outs, sems):
        own, ici, _, _, _ = self._copies(srcs, outs, sems)
        for cp in own + ici:
            cp.start()

    def mid(self, srcs, outs, sems):
        _, _, landed, passed, _ = self._copies(srcs, outs, sems)
        for arrived, onward in zip(landed, passed):
            arrived.wait_recv()
            onward.start()

    def last(self, srcs, outs, sems):
        own, ici, _, passed, passed_in = self._copies(srcs, outs, sems)
        for cp in passed_in:
            cp.wait_recv()
        for cp in ici + passed:
            cp.wait_send()
        for cp in own:
            cp.wait()


class _ScatterToChips:
    def __init__(self, parts):
        self.ins = list(parts)
        n = self.n = len(parts)
        self.out_shape = [jax.ShapeDtypeStruct(t.shape, t.dtype) for t in parts]
        self.sems = [pltpu.SemaphoreType.DMA((3 * n,)), pltpu.SemaphoreType.DMA((3 * n,)), pltpu.SemaphoreType.DMA((n,))]

    def _copies(self, srcs, outs, sems):
        ssem, rsem, lsem = sems
        x, y, c = _me()
        me = 2 * x + y
        own, out, landed = [], [], []
        for a in range(self.n):
            own.append(pltpu.make_async_copy(srcs[a].at[me], outs[a].at[me], lsem.at[a]))
            for k, (px, py) in enumerate(_other_chips(x, y)):
                dst = outs[a].at[2 * px + py]
                out.append(_remote(srcs[a].at[2 * px + py], outs[a].at[me], ssem.at[3 * a + k], rsem.at[3 * a + k],
                                   (px, py, c)))
                landed.append(_remote(dst, dst, ssem.at[3 * a + k], rsem.at[3 * a + k], (px, py, c)))
        return own, out, landed

    def first(self, srcs, outs, sems):
        own, out, _ = self._copies(srcs, outs, sems)
        for cp in own + out:
            cp.start()

    mid = None

    def last(self, srcs, outs, sems):
        own, out, landed = self._copies(srcs, outs, sems)
        for cp in landed:
            cp.wait_recv()
        for cp in own:
            cp.wait()
        for cp in out:
            cp.wait_send()


def _run_comm(comm, name):
    n = comm.n

    def body(*refs):
        srcs, outs, sems = refs[:n], refs[n:2 * n], refs[2 * n:]
        comm.first(srcs, outs, sems)
        if comm.mid is not None:
            comm.mid(srcs, outs, sems)
        comm.last(srcs, outs, sems)

    return _pcall(body, name=name, in_specs=[_HBM] * n, out_specs=[_HBM] * n, out_shape=comm.out_shape,
                  scratch_shapes=comm.sems)(*comm.ins)


def _reduce_sibling(grads, name):
    n = len(grads)

    def body(*refs):
        srcs, recvs = refs[:n], refs[n:2 * n]
        ssem, rsem = refs[2 * n:]
        x, y, c = _me()
        copies = []
        for a in range(n):
            h = grads[a].shape[1] // 2
            copies.append(_remote(srcs[a].at[:, pl.ds((1 - c) * h, h)], recvs[a], ssem.at[a], rsem.at[a], (x, y, 1 - c)))
        for cp in copies:
            cp.start()
        for cp in copies:
            cp.wait()

    return _pcall(
        body, name=name, in_specs=[_HBM] * n, out_specs=[_HBM] * n,
        out_shape=[jax.ShapeDtypeStruct((4, t.shape[1] // 2, t.shape[2]), t.dtype) for t in grads],
        scratch_shapes=[pltpu.SemaphoreType.DMA((n,)), pltpu.SemaphoreType.DMA((n,))],
    )(*grads)


def _reduce_finish(reds, name):
    n = len(reds)

    def body(*refs):
        outs = refs[n:2 * n]
        ssem, rsem = refs[2 * n:]
        x, y, c = _me()
        copies = []
        for a in range(n):
            h = reds[a].shape[0] // 2
            mine = outs[a].at[pl.ds(c * h, h)]
            copies.append(_remote(mine, mine, ssem.at[a], rsem.at[a], (x, y, 1 - c)))
        for cp in copies:
            cp.start()
        for a in range(n):
            h = reds[a].shape[0] // 2
            dst = outs[a].at[pl.ds((1 - c) * h, h)]
            _remote(dst, dst, ssem.at[a], rsem.at[a], (x, y, 1 - c)).wait_recv()
        for cp in copies:
            cp.wait_send()

    return _pcall(
        body, name=name, in_specs=[_HBM] * n, out_specs=[_HBM] * n,
        out_shape=[jax.ShapeDtypeStruct(t.shape, t.dtype) for t in reds],
        input_output_aliases={a: a for a in range(n)},
        scratch_shapes=[pltpu.SemaphoreType.DMA((n,)), pltpu.SemaphoreType.DMA((n,))],
    )(*reds)


def _half_sum(fn, full, halves, out_full, out_dtype, core, name):
    p, h, c = (halves[0].shape if halves else (full[0].shape[0], full[0].shape[1] // 2, full[0].shape[2]))
    br = _div(h, max(16, (1 << 19) // (p * c)), 16)
    nb = h // br
    mine3 = pl.BlockSpec((p, br, c), lambda i, core_ref: (0, core_ref[0] * nb + i, 0))
    half3 = pl.BlockSpec((p, br, c), lambda i, core_ref: (0, i, 0))

    def body(core_ref, *refs):
        refs[-1][...] = fn(*[t[...].astype(F32) for t in refs[:-1]]).astype(out_dtype)

    if out_full:
        out_spec = pl.BlockSpec((br, c), lambda i, core_ref: (core_ref[0] * nb + i, 0))
        out_shape = jax.ShapeDtypeStruct((2 * h, c), out_dtype)
    else:
        out_spec, out_shape = half3, jax.ShapeDtypeStruct((p, h, c), out_dtype)
    return _pcall(
        body, name=name,
        grid_spec=pltpu.PrefetchScalarGridSpec(
            num_scalar_prefetch=1, grid=(nb,), in_specs=[mine3] * len(full) + [half3] * len(halves),
            out_specs=out_spec),
        out_shape=out_shape, compiler_params=_cparams(("parallel",)),
    )(core, *full, *halves)


def _ada_fwd(c_all, w, b):
    def body(c_ref, w_ref, b_ref, o_ref):
        o_ref[...] = jnp.dot(c_ref[...], w_ref[...], precision=HI, preferred_element_type=F32) + b_ref[...]

    return _pcall(body, name="ada_fwd", out_shape=jax.ShapeDtypeStruct((c_all.shape[0], w.shape[1]), F32),
                  compiler_params=pltpu.CompilerParams(vmem_limit_bytes=VMEM_LIMIT))(c_all, w, b)


def _ada_bwd(c_all_t, d):
    def body(c_ref, d_ref, o_ref):
        o_ref[...] = jnp.dot(c_ref[...], d_ref[...], precision=HI, preferred_element_type=F32)

    return _pcall(body, name="ada_bwd", out_shape=jax.ShapeDtypeStruct((c_all_t.shape[0], d.shape[1]), F32),
                  compiler_params=pltpu.CompilerParams(vmem_limit_bytes=VMEM_LIMIT))(c_all_t, d)


def _sum_lead(x, name):
    p, r, n = x.shape
    br = _div(r, 512, 8)

    def body(x_ref, o_ref):
        acc = x_ref[0]
        for j in range(1, p):
            acc = acc + x_ref[j]
        o_ref[...] = acc

    return _pcall(
        body, name=name, grid=(r // br,), in_specs=[pl.BlockSpec((p, br, n), lambda i: (0, i, 0))],
        out_specs=pl.BlockSpec((br, n), lambda i: (i, 0)), out_shape=jax.ShapeDtypeStruct((r, n), F32),
        compiler_params=_cparams(("parallel",)),
    )(x)


def _adamw(w, g, m, v, name):
    shape = w.shape
    cols = shape[-1]
    w2, g2, m2, v2 = [t.reshape(-1, cols) for t in (w, g, m, v)]
    rows = w2.shape[0]
    br = _div(rows, max(8, (1 << 19) // cols // 8 * 8), 8)
    outs = _rows_fwd(_f_adamw, [(t, cols, 0) for t in (w2, g2, m2, v2)], [], [(cols, F32)] * 3, name=name, br=br)
    return [o.reshape(shape) for o in outs]


_BIG = (("w_in", 1), ("w_up", 1), ("w_down", 0), ("w_o", 0), ("w_rwkv_out", 0), ("w_att_out", 1), ("w2", 1), ("a2", 1),
        ("g2", 1))


_NEEDED_FIRST = ("w_in", "w_att_out", "w2", "a2", "g2")
_NEEDED_LATER = ("w_up", "w_down", "w_o", "w_rwkv_out")
_DONE_EARLY = ("w_up", "w_down", "w_o", "w_rwkv_out", "w_att_out")
_DONE_LATE = ("w_in", "w2", "a2", "g2")


def _cols_joined(t):
    return jnp.concatenate([t[j] for j in range(4)], axis=1)


def _cols_split(t):
    n = t.shape[1] // 4
    return jnp.stack([t[:, j * n:(j + 1) * n] for j in range(4)])


def _rows_joined(t):
    return t.reshape(4 * t.shape[1], t.shape[2])


def _rows_split(t):
    return t.reshape(4, t.shape[0] // 4, t.shape[1])


def _step_to_scan(x, tgt, ada, wts):
    sh1, sc1, gt1, sh2, sc2, gt2 = ada
    br = 256
    grp = lax.broadcasted_iota(jnp.int32, (D, 128), 0) // 64 == lax.broadcasted_iota(jnp.int32, (D, 128), 1)
    e = grp.astype(F32)
    et = e.T
    w_in = _cols_joined(wts["w_in"])
    w_att = w_in[:, :N_ATT]
    w_rw = jnp.pad(w_in[:, N_ATT:N_ATT + N_RW], ((0, 0), (0, N_RWP - N_RW)))
    w_gate = w_in[:, N_ATT + N_RW:]
    mu = jnp.pad(wts["mu_shift"], ((0, 0), (0, N_RWP - N_RW)))
    wl = jnp.zeros((N_LORA, 3 * D), F32)
    wl = wl.at[0:64, 0:D].set(_cols_joined(wts["w2"]).astype(F32))
    wl = wl.at[64:128, D:2 * D].set(_cols_joined(wts["a2"]).astype(F32))
    wl = wl.at[128:288, 2 * D:3 * D].set(_cols_joined(wts["g2"]).astype(F32))
    pre1_c = [wts["norm1_w"], sc1, sh1]
    (h1,) = _rows_fwd(_f_pre, [(x, D, 0)], pre1_c, [(D, BF16), None], name="pre1_fwd", br=br)
    att_in = _mm(h1, w_att, name="mm_att_in")
    z = _mm(h1, w_rw, name="mm_rw_in")
    gate_in = _mm(h1, w_gate, name="mm_gate_in")
    att_o, att_l = [], []
    for g, (_, dil) in enumerate(ATT_PATTERNS):
        o, l = _att_fwd(att_in, g, dil)
        att_o.append(o)
        att_l.append(l)
    comb_rows = [(t, ATT_WIDTH, 0) for t in att_o + att_l]
    (att,) = _rows_fwd(_f_comb, comb_rows, [], [(ATT_WIDTH, BF16)], name="comb_fwd", br=br)
    y_att = _mm(att, wts["w_att_out"], b_chip=True, name="mm_att_out")
    zs = _shift_fwd(z, mu)
    rwpre_c = [wts["w0"], wts["a0"], wts["k_k"], wts["k_a"], wl, e, et]
    lw, km, aa, bb, gg = _rows_fwd(_f_rwpre, [(zs, N_RWP, 0)], rwpre_c,
                                   [None, (D, F32), (D, F32), None, (D, F32), (D, F32), (D, F32)],
                                   name="rwpre_fwd", br=br)
    return dict(x=x, tgt=tgt, wts=wts, br=br, e=e, et=et, gt1=gt1, sc2=sc2, sh2=sh2, gt2=gt2, w_att=w_att, w_rw=w_rw,
                w_gate=w_gate, mu=mu, pre1_c=pre1_c, h1=h1, att_in=att_in, z=z, gate_in=gate_in, comb_rows=comb_rows,
                att=att, y_att=y_att, zs=zs, rwpre_c=rwpre_c, lw=lw, km=km, aa=aa, bb=bb, gg=gg)


def _step_between_scans(st, y_raw, late):
    x, tgt, wts, br, e, et = st["x"], st["tgt"], st["wts"], st["br"], st["e"], st["et"]
    zs, km, gg, gate_in, y_att, att = st["zs"], st["km"], st["gg"], st["gate_in"], st["y_att"], st["att"]
    comb_rows, att_in = st["comb_rows"], st["att_in"]
    gt1, sc2, sh2, gt2 = st["gt1"], st["sc2"], st["sh2"], st["gt2"]
    w_up, w_ao = late["w_up"], wts["w_att_out"]
    w_down, w_o, w_ro = _rows_joined(late["w_down"]), _rows_joined(late["w_o"]), _rows_joined(late["w_rwkv_out"])
    bga, bgr = wts["b_gate"][:, :D], wts["b_gate"][:, D:]
    post_rows = [(y_raw, D, 0), (zs, D, 0), (zs, D, 2), (km, D, 0), (gg, D, 0)]
    post_c = [wts["lnx_w"], wts["lnx_b"], wts["r_k"], e, et]
    (rw_out,) = _rows_fwd(_f_rwpost, post_rows, post_c, [(D, BF16)], name="rwpost_fwd", br=br)
    y_rw = _mm(rw_out, w_ro, name="mm_rw_out")
    mix_rows = [(gate_in, D, 0), (gate_in, D, 1), (y_att, D, 0), (y_rw, D, 0)]
    (mix,) = _rows_fwd(_f_mix, mix_rows, [bga, bgr], [(D, BF16)], name="mix_fwd", br=br)
    o = _mm(mix, w_o, name="mm_o")
    pre2_c = [gt1, wts["norm2_w"], sc2, sh2]
    x1, h2 = _rows_fwd(_f_pre2, [(x, D, 0), (o, D, 0)], pre2_c, [(D, F32), (D, BF16)], name="pre2_fwd", br=br)
    u = _mm(h2, w_up, b_chip=True, name="mm_up")
    act = _conv_fwd(u, wts["conv_w"], wts["conv_b"])
    f = _mm(act, w_down, name="mm_down")
    fin_rows = [(x1, D, 0), (f, D, 0), (tgt, D, 0)]
    fin_c = [gt2, wts["norm_f_w"]]

    def fin_fwd(*a):
        (l,) = _f_fin(*a)
        return (jnp.broadcast_to(jnp.sum(l, axis=0, keepdims=True), (8, 128)),)

    (loss_acc,) = _rows_fwd(fin_fwd, fin_rows, fin_c, [], name="fin_fwd", br=br, acc_shape=(8, 128))

    gw = {}
    dx1a, df, d_gt2, gw["norm_f_w"] = _rows_bwd(
        _f_fin, fin_rows, fin_c, [[]], wrt_rows=[0, 1], wrt_consts=[0, 1], drow_dtypes=[F32, BF16],
        name="fin_bwd", br=br, unit_cot=True)
    dact = _mm(df, w_down, tb=True, name="mm_dact")
    gw["w_down"] = _rows_split(_mm(act, df, ta=True, name="mm_dw_down"))
    du, gw["conv_w"], gw["conv_b"] = _conv_bwd(u, wts["conv_w"], wts["conv_b"], dact)
    dh2 = _mm(du, w_up, tb=True, b_chip=True, name="mm_dh2")
    gw["w_up"] = _mm(h2, du, ta=True, out_chip=True, name="mm_dw_up")
    dxa, do, d_gt1, gw["norm2_w"], d_sc2, d_sh2 = _rows_bwd(
        _f_pre2, [(x, D, 0), (o, D, 0)], pre2_c, [[(dx1a, D, 0)], [(dh2, D, 0)]], wrt_rows=[0, 1],
        wrt_consts=[0, 1, 2, 3], drow_dtypes=[F32, BF16], name="pre2_bwd", br=br)
    dmix = _mm(do, w_o, tb=True, name="mm_dmix")
    gw["w_o"] = _rows_split(_mm(mix, do, ta=True, name="mm_dw_o"))
    dga, dgr, dya, dyr, d_bga, d_bgr = _rows_bwd(
        _f_mix, mix_rows, [bga, bgr], [[(dmix, D, 0)]], wrt_rows=[0, 1, 2, 3], wrt_consts=[0, 1],
        drow_dtypes=[BF16] * 4, name="mix_bwd", br=br)
    gw["b_gate"] = jnp.concatenate([d_bga, d_bgr], axis=1)
    datt = _mm(dya, w_ao, tb=True, b_chip=True, name="mm_datt")
    gw["w_att_out"] = _mm(att, dya, ta=True, out_chip=True, name="mm_dw_att_out")
    drw = _mm(dyr, w_ro, tb=True, name="mm_drw")
    gw["w_rwkv_out"] = _rows_split(_mm(rw_out, dyr, ta=True, name="mm_dw_rw_out"))
    dcomb = _rows_bwd(_f_comb, comb_rows, [], [[(datt, ATT_WIDTH, 0)]], wrt_rows=list(range(6)), wrt_consts=[],
                      drow_dtypes=[F32] * 6, name="comb_bwd", br=br)
    datt_in = []
    for g, (_, dil) in enumerate(ATT_PATTERNS):
        datt_in += _att_bwd(att_in, g, dil, dcomb[g], dcomb[3 + g])
    datt_in = jnp.concatenate(datt_in, axis=1)
    dy_raw, dr_p, dv_p, dkm_p, dgg, gw["lnx_w"], gw["lnx_b"], gw["r_k"] = _rows_bwd(
        _f_rwpost, post_rows, post_c, [[(drw, D, 0)]], wrt_rows=[0, 1, 2, 3, 4], wrt_consts=[0, 1, 2],
        drow_dtypes=[F32] * 5, name="rwpost_bwd", br=br)
    st.update(loss=loss_acc[0, 0], gw=gw, dxa=dxa, dgate=jnp.concatenate([dga, dgr], axis=1), datt_in=datt_in,
              dy_raw=dy_raw, dr_p=dr_p, dv_p=dv_p, dkm_p=dkm_p, dgg=dgg, d_ada_late=(d_gt1, d_sh2, d_sc2, d_gt2))
    return st


def _step_after_scan(st, scan_grads):
    x, br, gw, h1, zs = st["x"], st["br"], st["gw"], st["h1"], st["zs"]
    dr_s, dlw, dkm_s, dv_s, daa, dbb = scan_grads
    pre_cots = [[(st["dr_p"], D, 0), (dr_s, D, 0)], [(dlw, D, 0)], [(st["dkm_p"], D, 0), (dkm_s, D, 0)],
                [(st["dv_p"], D, 0), (dv_s, D, 0)], [(daa, D, 0)], [(dbb, D, 0)], [(st["dgg"], D, 0)]]
    dzs, gw["w0"], gw["a0"], gw["k_k"], gw["k_a"], dwl = _rows_bwd(
        _f_rwpre, [(zs, N_RWP, 0)], st["rwpre_c"], pre_cots, wrt_rows=[0], wrt_consts=[0, 1, 2, 3, 4],
        drow_dtypes=[F32], name="rwpre_bwd", br=128)
    gw["w2"], gw["a2"] = _cols_split(dwl[0:64, 0:D]), _cols_split(dwl[64:128, D:2 * D])
    gw["g2"] = _cols_split(dwl[128:288, 2 * D:3 * D])
    dz, dmu = _shift_bwd(st["z"], st["mu"], dzs)
    gw["mu_shift"] = dmu[:, :N_RW]
    datt_in, dgate = st["datt_in"], st["dgate"]
    dh1 = _mm(datt_in, st["w_att"], tb=True, name="mm_dh1_att")
    dh1 = _mm(dgate, st["w_gate"], tb=True, add=dh1, name="mm_dh1_gate")
    dh1 = _mm(dz, st["w_rw"], tb=True, add=dh1, name="mm_dh1_rw")
    gw["w_in"] = _cols_split(jnp.concatenate([_mm(h1, datt_in, ta=True, name="mm_dw_att"),
                                              _mm(h1, dz, ta=True, name="mm_dw_rw")[:, :N_RW],
                                              _mm(h1, dgate, ta=True, name="mm_dw_gate")], axis=1))
    grad_x, gw["norm1_w"], d_sc1, d_sh1 = _rows_bwd(
        _f_pre, [(x, D, 0)], st["pre1_c"], [[(dh1, D, 0)], [(st["dxa"], D, 0)]], wrt_rows=[0], wrt_consts=[0, 1, 2],
        drow_dtypes=[F32], name="pre1_bwd", br=br)
    d_gt1, d_sh2, d_sc2, d_gt2 = st["d_ada_late"]
    return st["loss"], grad_x, (d_sh1, d_sc1, d_gt1, d_sh2, d_sc2, d_gt2), gw


_SMALL = ("b_ada", "norm1_w", "b_gate", "mu_shift", "w0", "a0", "k_k", "k_a", "r_k", "lnx_w", "lnx_b", "norm2_w",
          "conv_b", "norm_f_w")
_NAMES = ("w_ada", "b_ada", "norm1_w", "w_in", "b_gate", "mu_shift", "w0", "w2", "a0", "a2", "g2", "k_k", "k_a", "r_k",
          "lnx_w", "lnx_b", "w_att_out", "w_rwkv_out", "w_o", "norm2_w", "w_up", "conv_w", "conv_b", "w_down",
          "norm_f_w")


def kernel(x, c, w_ada, b_ada, norm1_w, w_in, b_gate, mu_shift, w0, w2, a0, a2, g2, k_k, k_a, r_k, lnx_w, lnx_b, w_att_out, w_rwkv_out, w_o, norm2_w, w_up, conv_w, conv_b, w_down, norm_f_w, loss_target, m_w_ada, m_b_ada, m_norm1_w, m_w_in, m_b_gate, m_mu_shift, m_w0, m_w2, m_a0, m_a2, m_g2, m_k_k, m_k_a, m_r_k, m_lnx_w, m_lnx_b, m_w_att_out, m_w_rwkv_out, m_w_o, m_norm2_w, m_w_up, m_conv_w, m_conv_b, m_w_down, m_norm_f_w, v_w_ada, v_b_ada, v_norm1_w, v_w_in, v_b_gate, v_mu_shift, v_w0, v_w2, v_a0, v_a2, v_g2, v_k_k, v_k_a, v_r_k, v_lnx_w, v_lnx_b, v_w_att_out, v_w_rwkv_out, v_w_o, v_norm2_w, v_w_up, v_conv_w, v_conv_b, v_w_down, v_norm_f_w):
    args = dict(locals())
    p, pm, pv = {}, {}, {}
    for name in _NAMES:
        for dst, key in ((p, name), (pm, "m_" + name), (pv, "v_" + name)):
            t = args[key]
            dst[name] = t.reshape(1, -1) if name in ("r_k", "norm_f_w") else t.reshape(t.shape[-2], t.shape[-1])
    xi, yi, ci = _me()
    chip = 2 * xi + yi
    dev = 4 * xi + 2 * yi + ci
    x2, tgt = x[0], loss_target[0]

    n_cw = 3 * (2 * D_FF // 4)
    vec = jnp.concatenate([c.reshape(-1), p["conv_w"].reshape(-1), jnp.zeros((8 * D - D - n_cw,), F32)]).reshape(8, D)
    g0 = _allgather8(vec, "gather_c").reshape(8, 8 * D)
    c_all = g0[:, :D]
    conv_w_full = jnp.concatenate([g0[2 * j, D:D + n_cw].reshape(3, -1) for j in range(4)], axis=1)
    n_ada = 6 * D // 4
    b_ada_sh = lax.dynamic_slice(p["b_ada"], (0, chip * n_ada), (1, n_ada))
    ada_sh = _ada_fwd(c_all, p["w_ada"], b_ada_sh)
    ga = _allgather8(ada_sh, "gather_ada")
    ada_all = jnp.concatenate([ga[2 * j] for j in range(4)], axis=1)
    ada_row = lax.dynamic_slice(ada_all, (dev, 0), (1, 6 * D))
    ada = [ada_row[:, j * D:(j + 1) * D] for j in range(6)]

    big = [n for n, _ in _BIG]
    shard = {n: p[n].astype(BF16) for n in big}
    wts = dict(zip(_NEEDED_FIRST, _run_comm(_GatherWeights([shard[n] for n in _NEEDED_FIRST]), "gather_w")))
    for n in _SMALL:
        wts[n] = p[n]
    wts["conv_w"] = conv_w_full
    core = ci.reshape(1).astype(jnp.int32)

    def chip_parts(gw, names):
        recv = _reduce_sibling([gw[n] for n in names], "reduce_sib_" + names[0])
        return [_half_sum(lambda a, b: a + b, [gw[n]], [r], False, BF16, core, "reduce_add2_" + n)
                for n, r in zip(names, recv)]

    st = _step_to_scan(x2, tgt, ada, wts)
    y_raw, s0s, late = _scan_fwd(st["zs"], st["lw"], st["km"], st["aa"], st["bb"],
                                 _GatherWeights([shard[n] for n in _NEEDED_LATER]))
    st = _step_between_scans(st, y_raw, dict(zip(_NEEDED_LATER, late)))
    scan_grads, slots_early = _scan_bwd(st["zs"], st["lw"], st["km"], st["aa"], st["bb"], s0s, st["dy_raw"],
                                        _ScatterToChips(chip_parts(st["gw"], _DONE_EARLY)))
    loss_part, grad_x, d_ada, gw = _step_after_scan(st, scan_grads)

    small = [jnp.concatenate(d_ada, axis=1)] + [gw[n] for n in _SMALL[1:]] + [gw["conv_w"], loss_part.reshape(1, 1)]
    sizes = [t.size for t in small]
    flat = jnp.concatenate([t.reshape(-1) for t in small])
    npad = (-flat.shape[0]) % (8 * D)
    srows = (flat.shape[0] + npad) // D
    flat = jnp.concatenate([flat, jnp.zeros((npad,), F32)]).reshape(srows, D)
    parts = _allgather8(flat, "gather_small")
    tot = _sum_lead(parts, "sum_small").reshape(-1)
    pieces, pos = [], 0
    for sz in sizes:
        pieces.append(tot[pos:pos + sz])
        pos += sz
    grads = {}
    for n, piece in zip(_SMALL, pieces[:len(_SMALL)]):
        grads[n] = piece.reshape(p[n].shape)
    conv_w_grad = pieces[len(_SMALL)].reshape(3, 2 * D_FF)
    grads["conv_w"] = lax.dynamic_slice(conv_w_grad, (0, chip * (n_cw // 3)), (3, n_cw // 3))
    loss = pieces[-1][0]
    d_ada_all = parts[:, :6].reshape(8, 6 * D)
    grads["w_ada"] = _ada_bwd(c_all.T, lax.dynamic_slice(d_ada_all, (0, chip * n_ada), (8, n_ada)))

    slots_late = _run_comm(_ScatterToChips(chip_parts(gw, _DONE_LATE)), "reduce_chips")
    order = _DONE_EARLY + _DONE_LATE
    reds = [_half_sum(lambda t: t[0] + t[1] + t[2] + t[3], [], [t], True, F32, core, "reduce_add4_" + n)
            for n, t in zip(order, list(slots_early) + list(slots_late))]
    for n, g in zip(order, _reduce_finish(reds, "reduce_sib2")):
        grads[n] = g

    outs_g, outs_d, outs_m, outs_v = [], [], [], []
    for name in _NAMES:
        g = grads[name]
        d, m, v = _adamw(p[name], g, pm[name], pv[name], "adamw_" + name)
        shape = args[name].shape
        outs_g.append(g.reshape(shape))
        outs_d.append(d.reshape(shape))
        outs_m.append(m.reshape(shape))
        outs_v.append(v.reshape(shape))
    return (loss, grad_x.reshape(x.shape), *outs_g, *outs_d, *outs_m, *outs_v)
```

```python
import functools

import jax
import jax.numpy as jnp
from jax import lax
from jax.experimental import pallas as pl
from jax.experimental.pallas import tpu as pltpu

F32 = jnp.float32
BF16 = jnp.bfloat16
HI = lax.Precision.HIGHEST
MESH = pl.DeviceIdType.MESH

D = 1024
ATT_PATTERNS = ((128, 1), (512, 4), (2048, 16))
ATT_BLOCK = 128
ATT_WIDTH = 512
N_ATT = 3 * 3 * ATT_WIDTH
N_RW = 3 * D + 64 + 64 + 160
N_RWP = 3456
N_LORA = N_RWP - 3 * D
N_GATE = 2 * D
D_FF = 2816
RMS_EPS = 1e-6
GN_EPS = 64e-5
SCAN_CHUNK = 64
SCAN_PAIRS = 8
NEG = -1e30
VMEM_LIMIT = 48 * 1024 * 1024

ADAM_LR, ADAM_B1, ADAM_B2, ADAM_EPS, ADAM_WD, ADAM_STEP = 0.001, 0.9, 0.999, 1e-08, 0.01, 10


def _pcall(body, **kw):
    return pl.pallas_call(body, **kw)


def _cparams(sem):
    return pltpu.CompilerParams(dimension_semantics=sem, vmem_limit_bytes=VMEM_LIMIT)


def _div(n, pref, mult):
    best = None
    d = mult
    while d <= min(n, pref):
        if n % d == 0:
            best = d
        d += mult
    return best if best else n


def _dg(a, b, ca, cb):
    return lax.dot_general(a.astype(BF16), b.astype(BF16), (((ca,), (cb,)), ((), ())), preferred_element_type=F32)


@jax.custom_vjp
def _nn(a, b):
    return _dg(a, b, 1, 0)


@jax.custom_vjp
def _nt(a, b):
    return _dg(a, b, 1, 1)


@jax.custom_vjp
def _tn(a, b):
    return _dg(a, b, 0, 0)


_nn.defvjp(lambda a, b: (_nn(a, b), (a, b)), lambda res, g: (_nt(g, res[1]), _tn(res[0], g)))
_nt.defvjp(lambda a, b: (_nt(a, b), (a, b)), lambda res, g: (_nn(g, res[1]), _tn(g, res[0])))
_tn.defvjp(lambda a, b: (_tn(a, b), (a, b)), lambda res, g: (_nt(res[1], g), _nn(res[0], g)))


def _bdg(a, b, ca, cb):
    return lax.dot_general(a.astype(BF16), b.astype(BF16), (((ca,), (cb,)), ((0,), (0,))), preferred_element_type=F32)


@jax.custom_vjp
def _bnn(a, b):
    return _bdg(a, b, 2, 1)


@jax.custom_vjp
def _bnt(a, b):
    return _bdg(a, b, 2, 2)


@jax.custom_vjp
def _btn(a, b):
    return _bdg(a, b, 1, 1)


_bnn.defvjp(lambda a, b: (_bnn(a, b), (a, b)), lambda res, g: (_bnt(g, res[1]), _btn(res[0], g)))
_bnt.defvjp(lambda a, b: (_bnt(a, b), (a, b)), lambda res, g: (_bnn(g, res[1]), _btn(g, res[0])))
_btn.defvjp(lambda a, b: (_btn(a, b), (a, b)), lambda res, g: (_bnt(res[1], g), _bnn(res[0], g)))


def _split2(x):
    hi = x.astype(BF16)
    lo = (x - hi.astype(F32)).astype(BF16)
    return hi, lo


def _hsum_impl(x, e, et):
    eb, etb = e.astype(BF16), et.astype(BF16)
    s = jnp.dot(x.astype(BF16), eb, preferred_element_type=F32)
    shi, slo = _split2(s)
    return jnp.dot(shi, etb, preferred_element_type=F32) + jnp.dot(slo, etb, preferred_element_type=F32)


@jax.custom_vjp
def _hsum(x, e, et):
    return _hsum_impl(x, e, et)


_hsum.defvjp(lambda x, e, et: (_hsum_impl(x, e, et), (e, et)),
             lambda res, g: (_hsum_impl(g, res[0], res[1]), jnp.zeros_like(res[0]), jnp.zeros_like(res[1])))


def _mm(a, b, *, ta=False, tb=False, out_dtype=F32, add=None, b_chip=False, out_chip=False, name):
    if ta:
        kdim, m = a.shape
    else:
        m, kdim = a.shape
    if b_chip:
        n = b.shape[1] if tb else 4 * b.shape[2]
    else:
        n = b.shape[0] if tb else b.shape[1]
    tm, tn, tk = _div(m, 1536, 128), _div(n, 1536, 128), _div(kdim, 1408, 128)
    if b_chip and tb:
        tk = kdim // 4
    if (b_chip and not tb) or out_chip:
        tn = n // 4
    nk = kdim // tk
    ca, cb = (0 if ta else 1), (1 if tb else 0)

    def body(*refs):
        a_ref, b_ref = refs[0], refs[1]
        add_ref = None if add is None else refs[2]
        o_ref = refs[2 if add is None else 3]
        part = lax.dot_general(a_ref[...], b_ref[...], (((ca,), (cb,)), ((), ())), preferred_element_type=F32)

        def finish(r):
            if add_ref is not None:
                r = r + add_ref[...]
            o_ref[...] = r.astype(o_ref.dtype)

        if nk == 1:
            finish(part)
            return
        acc = refs[-1]
        k = pl.program_id(2)

        @pl.when(k == 0)
        def _():
            acc[...] = part

        @pl.when(k > 0)
        def _():
            acc[...] += part

        @pl.when(k == nk - 1)
        def _():
            finish(acc[...])

    a_spec = pl.BlockSpec((tk, tm), lambda i, j, k: (k, i)) if ta else pl.BlockSpec((tm, tk), lambda i, j, k: (i, k))
    if b_chip:
        b_spec = (pl.BlockSpec((None, tn, tk), lambda i, j, k: (k, j, 0)) if tb
                  else pl.BlockSpec((None, tk, tn), lambda i, j, k: (j, k, 0)))
    else:
        b_spec = pl.BlockSpec((tn, tk), lambda i, j, k: (j, k)) if tb else pl.BlockSpec((tk, tn), lambda i, j, k: (k, j))
    in_specs = [a_spec, b_spec]
    args = [a, b]
    if add is not None:
        in_specs.append(pl.BlockSpec((tm, tn), lambda i, j, k: (i, j)))
        args.append(add)
    if out_chip:
        out_spec = pl.BlockSpec((None, tm, tn), lambda i, j, k: (j, i, 0))
        out_shape = jax.ShapeDtypeStruct((4, m, tn), out_dtype)
    else:
        out_spec = pl.BlockSpec((tm, tn), lambda i, j, k: (i, j))
        out_shape = jax.ShapeDtypeStruct((m, n), out_dtype)
    return _pcall(
        body, name=name, grid=(m // tm, n // tn, nk), in_specs=in_specs, out_specs=out_spec, out_shape=out_shape,
        scratch_shapes=[] if nk == 1 else [pltpu.VMEM((tm, tn), F32)],
        compiler_params=_cparams(("parallel", "parallel", "arbitrary")),
    )(*args)


def _row_spec(br, w, cb):
    return pl.BlockSpec((br, w), lambda i: (i, cb))


def _const_spec(shape):
    return pl.BlockSpec(shape, lambda i: (0,) * len(shape))


def _rows_fwd(fn, rows, consts, outs, *, name, br, acc_shape=None):
    s = rows[0][0].shape[0]
    nr, nc = len(rows), len(consts)
    kept = [k for k, o in enumerate(outs) if o is not None]

    def body(*refs):
        xs = [r[...].astype(F32) for r in refs[:nr]]
        cs = [c[...] for c in refs[nr:nr + nc]]
        res = fn(*xs, *cs)
        orefs = refs[nr + nc:]
        for j, k in enumerate(kept):
            orefs[j][...] = res[k].astype(orefs[j].dtype)
        if acc_shape is not None:
            acc_ref = orefs[len(kept)]

            @pl.when(pl.program_id(0) == 0)
            def _():
                acc_ref[...] = jnp.zeros_like(acc_ref)

            acc_ref[...] += res[len(outs)]

    in_specs = [_row_spec(br, w, cb) for (_, w, cb) in rows] + [_const_spec(c.shape) for c in consts]
    out_specs = [_row_spec(br, outs[k][0], 0) for k in kept]
    out_shape = [jax.ShapeDtypeStruct((s, outs[k][0]), outs[k][1]) for k in kept]
    if acc_shape is not None:
        out_specs.append(_const_spec(acc_shape))
        out_shape.append(jax.ShapeDtypeStruct(acc_shape, F32))
    return _pcall(
        body, name=name, grid=(s // br,), in_specs=in_specs, out_specs=out_specs, out_shape=out_shape,
        compiler_params=_cparams(("arbitrary",)),
    )(*[r[0] for r in rows], *consts)


def _rows_bwd(fn, rows, consts, cots, *, wrt_rows, wrt_consts, drow_dtypes, name, br, unit_cot=False, comm=None):
    comm = _NOTHING if comm is None else comm
    ncomm = comm.n
    nout = len(wrt_rows) + len(wrt_consts)
    s = rows[0][0].shape[0]
    nr, nc = len(rows), len(consts)
    flat_cots = [c for lst in cots for c in lst]
    ncot = len(flat_cots)

    def body(*refs):
        xs = [r[...].astype(F32) for r in refs[:nr]]
        cs = [c[...] for c in refs[nr:nr + nc]]
        cvals = [c[...].astype(F32) for c in refs[nr + nc:nr + nc + ncot]]
        orefs = refs[nr + nc + ncot + ncomm:]
        before, after = _comm_phases(comm, refs[nr + nc + ncot:nr + nc + ncot + ncomm] + orefs[nout:], s // br)
        before()

        def g(*d):
            xs2, cs2 = list(xs), list(cs)
            for j, k in enumerate(wrt_rows):
                xs2[k] = d[j]
            for j, k in enumerate(wrt_consts):
                cs2[k] = d[len(wrt_rows) + j]
            return tuple(fn(*xs2, *cs2))

        prim = [xs[k] for k in wrt_rows] + [cs[k] for k in wrt_consts]
        outs, vjp = jax.vjp(g, *prim)
        ct = []
        pos = 0
        for o, lst in zip(outs, cots):
            if unit_cot:
                ct.append(jnp.ones_like(o))
                continue
            acc = jnp.zeros_like(o)
            for _ in lst:
                acc = acc + cvals[pos]
                pos += 1
            ct.append(acc)
        grads = vjp(tuple(ct))
        for j in range(len(wrt_rows)):
            orefs[j][...] = grads[j].astype(orefs[j].dtype)

        @pl.when(pl.program_id(0) == 0)
        def _():
            for j in range(len(wrt_consts)):
                oref = orefs[len(wrt_rows) + j]
                oref[...] = jnp.zeros_like(oref)

        for j in range(len(wrt_consts)):
            orefs[len(wrt_rows) + j][...] += grads[len(wrt_rows) + j]
        after()

    in_specs = ([_row_spec(br, w, cb) for (_, w, cb) in rows] + [_const_spec(c.shape) for c in consts]
                + [_row_spec(br, w, cb) for (_, w, cb) in flat_cots] + [_HBM] * ncomm)
    out_specs = ([_row_spec(br, rows[k][1], 0) for k in wrt_rows] + [_const_spec(consts[k].shape) for k in wrt_consts]
                 + [_HBM] * ncomm)
    out_shape = ([jax.ShapeDtypeStruct((s, rows[k][1]), dt) for k, dt in zip(wrt_rows, drow_dtypes)]
                 + [jax.ShapeDtypeStruct(consts[k].shape, F32) for k in wrt_consts] + comm.out_shape)
    return _pcall(
        body, name=name, grid=(s // br,), in_specs=in_specs, out_specs=out_specs, out_shape=out_shape,
        scratch_shapes=comm.sems, compiler_params=_cparams(("arbitrary",)),
    )(*[r[0] for r in rows], *consts, *[c[0] for c in flat_cots], *comm.ins)


def _rms(x, w):
    return x * lax.rsqrt(jnp.mean(x * x, axis=-1, keepdims=True) + RMS_EPS) * w


def _softplus(x):
    return jnp.maximum(x, 0.0) + jnp.log(1.0 + jnp.exp(-jnp.abs(x)))


def _f_pre(x, nw, sc, sh):
    return _rms(x, nw) * (1.0 + sc) + sh, x


def _f_pre2(x, o, gt, nw, sc, sh):
    x1 = x + gt * o
    return x1, _rms(x1, nw) * (1.0 + sc) + sh


def _f_fin(x1, f, tgt, gt, nfw):
    y = _rms(x1 + gt * f, nfw)
    return (0.5 * jnp.mean(jnp.square(y - tgt), axis=-1, keepdims=True),)


def _f_comb(o1, o2, o3, l1, l2, l3):
    m = lax.stop_gradient(jnp.maximum(jnp.maximum(l1, l2), l3))
    e1, e2, e3 = jnp.exp(l1 - m), jnp.exp(l2 - m), jnp.exp(l3 - m)
    return ((e1 * o1 + e2 * o2 + e3 * o3) / (e1 + e2 + e3),)


def _f_rwpre(zs, w0, a0, k_k, k_a, wl, e, et):
    r, k, v, zl = zs[:, 0:D], zs[:, D:2 * D], zs[:, 2 * D:3 * D], zs[:, 3 * D:N_RWP]
    lane = lax.broadcasted_iota(jnp.int32, zl.shape, 1)
    t = jnp.where(lane < 64, jnp.tanh(zl), jnp.where(lane < 128, zl, jnp.where(lane < 288, jax.nn.sigmoid(zl), 0.0)))
    lo = _nn(t[:, 0:128], wl[0:128, 0:2 * D])
    g = _nn(t[:, 128:N_LORA], wl[128:N_LORA, 2 * D:3 * D])
    w_log = -_softplus(-(w0 + lo[:, 0:D])) - 0.5
    lw = -jnp.exp(w_log)
    a = jax.nn.sigmoid(a0 + lo[:, D:2 * D])
    k_mod = k * (1.0 + (a - 1.0) * k_a)
    kk = k * k_k
    kk = kk / jnp.maximum(jnp.sqrt(_hsum(kk * kk, e, et)), 1e-12)
    return r, lw, k_mod, v, -kk, kk * a, g


def _f_rwpost(y, r, v, k_mod, g, lnx_w, lnx_b, r_k, e, et):
    mean = _hsum(y, e, et) * (1.0 / 64)
    yc = y - mean
    var = _hsum(yc * yc, e, et) * (1.0 / 64)
    yn = yc * lax.rsqrt(var + GN_EPS) * lnx_w + lnx_b
    bonus = _hsum(r * k_mod * r_k, e, et) * v
    return ((yn + bonus) * g,)


def _f_mix(gia, gir, ya, yr, bga, bgr):
    return (jax.nn.sigmoid(gia + bga) * ya + jax.nn.sigmoid(gir + bgr) * yr,)


def _f_adamw(w, g, m, v):
    m = ADAM_B1 * m + (1.0 - ADAM_B1) * g
    v = ADAM_B2 * v + (1.0 - ADAM_B2) * jnp.square(g)
    m_hat = m / (1.0 - ADAM_B1 ** ADAM_STEP)
    v_hat = v / (1.0 - ADAM_B2 ** ADAM_STEP)
    return -ADAM_LR * (m_hat / (jnp.sqrt(v_hat) + ADAM_EPS) + ADAM_WD * w), m, v


def _down(x, k):
    row = lax.broadcasted_iota(jnp.int32, x.shape, 0)
    return jnp.where(row < k, 0.0, pltpu.roll(x, k, 0))


def _up(x, k):
    n = x.shape[0]
    row = lax.broadcasted_iota(jnp.int32, x.shape, 0)
    return jnp.where(row >= n - k, 0.0, pltpu.roll(x, n - k, 0))


def _col_spec(s, w, off=0):
    return pl.BlockSpec((s, w), lambda j: (0, j + off))


def _shift_fwd(z, mu):
    s, n = z.shape

    def body(z_ref, mu_ref, o_ref):
        zz = z_ref[...]
        o_ref[...] = zz + (_down(zz, 1) - zz) * mu_ref[...]

    return _pcall(
        body, name="shift_fwd", grid=(n // 128,), in_specs=[_col_spec(s, 128), _col_spec(1, 128)],
        out_specs=_col_spec(s, 128), out_shape=jax.ShapeDtypeStruct((s, n), F32),
        compiler_params=_cparams(("parallel",)),
    )(z, mu)


def _shift_bwd(z, mu, dzs):
    s, n = z.shape

    def body(z_ref, mu_ref, d_ref, dz_ref, dmu_ref):
        zz, d, m = z_ref[...], d_ref[...], mu_ref[...]
        dm = d * m
        dz_ref[...] = (d - dm + _up(dm, 1)).astype(dz_ref.dtype)
        dmu_ref[...] = jnp.sum(d * (_down(zz, 1) - zz), axis=0, keepdims=True)

    return _pcall(
        body, name="shift_bwd", grid=(n // 128,), in_specs=[_col_spec(s, 128), _col_spec(1, 128), _col_spec(s, 128)],
        out_specs=[_col_spec(s, 128), _col_spec(1, 128)],
        out_shape=[jax.ShapeDtypeStruct((s, n), BF16), jax.ShapeDtypeStruct((1, n), F32)],
        compiler_params=_cparams(("parallel",)),
    )(z, mu, dzs)


def _conv3(x, w_ref, b_ref):
    return b_ref[...] + w_ref[0:1, :] * _down(x, 2) + w_ref[1:2, :] * _down(x, 1) + w_ref[2:3, :] * x


def _conv_fwd(u, cw, cb):
    s = u.shape[0]
    nb = D_FF // 128

    def body(ug_ref, uv_ref, wg_ref, wv_ref, bg_ref, bv_ref, o_ref):
        gate = _conv3(ug_ref[...], wg_ref, bg_ref)
        val = _conv3(uv_ref[...], wv_ref, bv_ref)
        o_ref[...] = (gate * jax.nn.sigmoid(gate) * val).astype(o_ref.dtype)

    return _pcall(
        body, name="conv_fwd", grid=(nb,),
        in_specs=[_col_spec(s, 128), _col_spec(s, 128, nb), _col_spec(3, 128), _col_spec(3, 128, nb),
                  _col_spec(1, 128), _col_spec(1, 128, nb)],
        out_specs=_col_spec(s, 128), out_shape=jax.ShapeDtypeStruct((s, D_FF), BF16),
        compiler_params=_cparams(("parallel",)),
    )(u, u, cw, cw, cb, cb)


def _conv_bwd(u, cw, cb, dact):
    s = u.shape[0]
    nb = D_FF // 128

    def half(x, d, w_ref, du_ref, dw_ref, db_ref):
        x1, x2 = _down(x, 1), _down(x, 2)
        du_ref[...] = (w_ref[2:3, :] * d + w_ref[1:2, :] * _up(d, 1) + w_ref[0:1, :] * _up(d, 2)).astype(du_ref.dtype)
        dw_ref[0:1, :] = jnp.sum(d * x2, axis=0, keepdims=True)
        dw_ref[1:2, :] = jnp.sum(d * x1, axis=0, keepdims=True)
        dw_ref[2:3, :] = jnp.sum(d * x, axis=0, keepdims=True)
        db_ref[...] = jnp.sum(d, axis=0, keepdims=True)

    def body(ug_ref, uv_ref, wg_ref, wv_ref, bg_ref, bv_ref, da_ref,
             dug_ref, duv_ref, dwg_ref, dwv_ref, dbg_ref, dbv_ref):
        ug, uv, da = ug_ref[...], uv_ref[...], da_ref[...]
        gate = _conv3(ug, wg_ref, bg_ref)
        val = _conv3(uv, wv_ref, bv_ref)
        sg = jax.nn.sigmoid(gate)
        dgate = da * val * sg * (1.0 + gate * (1.0 - sg))
        dval = da * gate * sg
        half(ug, dgate, wg_ref, dug_ref, dwg_ref, dbg_ref)
        half(uv, dval, wv_ref, duv_ref, dwv_ref, dbv_ref)

    dug, duv, dwg, dwv, dbg, dbv = _pcall(
        body, name="conv_bwd", grid=(nb,),
        in_specs=[_col_spec(s, 128), _col_spec(s, 128, nb), _col_spec(3, 128), _col_spec(3, 128, nb),
                  _col_spec(1, 128), _col_spec(1, 128, nb), _col_spec(s, 128)],
        out_specs=[_col_spec(s, 128), _col_spec(s, 128), _col_spec(3, 128), _col_spec(3, 128),
                   _col_spec(1, 128), _col_spec(1, 128)],
        out_shape=[jax.ShapeDtypeStruct((s, D_FF), BF16), jax.ShapeDtypeStruct((s, D_FF), BF16),
                   jax.ShapeDtypeStruct((3, D_FF), F32), jax.ShapeDtypeStruct((3, D_FF), F32),
                   jax.ShapeDtypeStruct((1, D_FF), F32), jax.ShapeDtypeStruct((1, D_FF), F32)],
        compiler_params=_cparams(("parallel",)),
    )(u, u, cw, cw, cb, cb, dact)
    return (jnp.concatenate([dug, duv], axis=1), jnp.concatenate([dwg, dwv], axis=1),
            jnp.concatenate([dbg, dbv], axis=1))


ATT_BATCH = 4


def _att_batch(q, kp, kc, vp, vc, first):
    ma = lax.broadcasted_iota(jnp.int32, (1, ATT_BLOCK, 128), 2) < 64
    qs = jnp.concatenate([jnp.where(ma, q, 0.0), jnp.where(ma, 0.0, q)], axis=1)
    qi = lax.broadcasted_iota(jnp.int32, (1, 2 * ATT_BLOCK, ATT_BLOCK), 1) & (ATT_BLOCK - 1)
    kj = lax.broadcasted_iota(jnp.int32, (1, 2 * ATT_BLOCK, ATT_BLOCK), 2)
    okp = kj >= qi + jnp.where(first, 2 * ATT_BLOCK, 0)
    okc = kj <= qi
    sp = jnp.where(okp, _bnt(qs, kp) * 0.125, NEG)
    sc = jnp.where(okc, _bnt(qs, kc) * 0.125, NEG)
    m = lax.stop_gradient(jnp.maximum(jnp.max(sp, axis=-1, keepdims=True), jnp.max(sc, axis=-1, keepdims=True)))
    pp, pc = jnp.exp(sp - m), jnp.exp(sc - m)
    den = jnp.sum(pp, axis=-1, keepdims=True) + jnp.sum(pc, axis=-1, keepdims=True)
    o_s = (_bnn(pp, vp) + _bnn(pc, vc)) / den
    l_s = jnp.broadcast_to(m + jnp.log(den), o_s.shape)
    return (jnp.where(ma, o_s[:, :ATT_BLOCK], o_s[:, ATT_BLOCK:]), jnp.where(ma, l_s[:, :ATT_BLOCK], l_s[:, ATT_BLOCK:]))


def _att_pairs_per_step(dil):
    return ATT_BATCH if dil == 1 else 1


def _att_specs(g, dil):
    rows, pp = ATT_BLOCK * dil, _att_pairs_per_step(dil)

    def cur(slot):
        return pl.BlockSpec((rows, 128 * pp), lambda n, p: (n, (g * 3 + slot) * (4 // pp) + p))

    def prev(slot):
        return pl.BlockSpec((rows, 128 * pp), lambda n, p: (jnp.maximum(n - 1, 0), (g * 3 + slot) * (4 // pp) + p))

    return [cur(0), prev(1), cur(1), prev(2), cur(2)]


def _att_out_spec(dil):
    return pl.BlockSpec((ATT_BLOCK * dil, 128 * _att_pairs_per_step(dil)), lambda n, p: (n, p))


def _att_grid(s, dil):
    return (s // (ATT_BLOCK * dil), 4 // _att_pairs_per_step(dil))


def _att_windows(i, dil):
    if dil == 1:
        return [(pl.ds(0, ATT_BLOCK), pl.ds(128 * j, 128)) for j in range(ATT_BATCH)]
    return [(pl.ds(i * ATT_BATCH + j, ATT_BLOCK, stride=dil), pl.ds(0, 128)) for j in range(ATT_BATCH)]


def _att_fwd(att_in, g, dil):
    s = att_in.shape[0]

    def body(q_ref, kp_ref, kc_ref, vp_ref, vc_ref, o_ref, l_ref):
        first = pl.program_id(0) == 0

        def one(i, carry):
            win = _att_windows(i, dil)
            vals = [jnp.stack([ref[w] for w in win]) for ref in (q_ref, kp_ref, kc_ref, vp_ref, vc_ref)]
            o, l = _att_batch(*vals, first)
            for j, w in enumerate(win):
                o_ref[w] = o[j]
                l_ref[w] = l[j]
            return carry

        lax.fori_loop(0, max(1, dil // ATT_BATCH), one, 0)

    return _pcall(
        body, name=f"att_fwd{g}", grid=_att_grid(s, dil), in_specs=_att_specs(g, dil),
        out_specs=[_att_out_spec(dil)] * 2, out_shape=[jax.ShapeDtypeStruct((s, ATT_WIDTH), F32)] * 2,
        compiler_params=_cparams(("parallel", "parallel")),
    )(att_in, att_in, att_in, att_in, att_in)


def _att_bwd(att_in, g, dil, do, dl):
    s = att_in.shape[0]
    nb = s // (ATT_BLOCK * dil)

    def body(q_ref, kp_ref, kc_ref, vp_ref, vc_ref, do_ref, dl_ref, dq_ref, dkp_ref, dkc_ref, dvp_ref, dvc_ref):
        first = pl.program_id(0) == 0

        def one(i, carry):
            win = _att_windows(i, dil)
            vals = [jnp.stack([ref[w] for w in win]) for ref in (q_ref, kp_ref, kc_ref, vp_ref, vc_ref)]
            _, vjp = jax.vjp(lambda *a: _att_batch(*a, first), *vals)
            grads = vjp((jnp.stack([do_ref[w] for w in win]), jnp.stack([dl_ref[w] for w in win])))
            for ref, gr in zip((dq_ref, dkp_ref, dkc_ref, dvp_ref, dvc_ref), grads):
                for j, w in enumerate(win):
                    ref[w] = gr[j]
            return carry

        lax.fori_loop(0, max(1, dil // ATT_BATCH), one, 0)

    dq, dkp, dkc, dvp, dvc = _pcall(
        body, name=f"att_bwd{g}", grid=_att_grid(s, dil), in_specs=_att_specs(g, dil) + [_att_out_spec(dil)] * 2,
        out_specs=[_att_out_spec(dil)] * 5, out_shape=[jax.ShapeDtypeStruct((s, ATT_WIDTH), F32)] * 5,
        compiler_params=_cparams(("parallel", "parallel")),
    )(att_in, att_in, att_in, att_in, att_in, do, dl)

    unit = ATT_BLOCK * dil
    per = max(1, 1024 // unit)
    width = ATT_WIDTH if per > 1 else 128
    steps = nb // per

    def with_next(cur_ref, prev_ref, next_ref, has_next):
        tail = jnp.where(has_next, next_ref[...], 0.0)
        shifted = tail if per == 1 else jnp.concatenate([prev_ref[unit:, :], tail], axis=0)
        return (cur_ref[...] + shifted).astype(BF16)

    def cbody(dq_ref, dkc_ref, dkp_ref, dkn_ref, dvc_ref, dvp_ref, dvn_ref, oq_ref, ok_ref, ov_ref):
        has_next = pl.program_id(0) + 1 < steps
        oq_ref[...] = dq_ref[...].astype(BF16)
        ok_ref[...] = with_next(dkc_ref, dkp_ref, dkn_ref, has_next)
        ov_ref[...] = with_next(dvc_ref, dvp_ref, dvn_ref, has_next)

    cur = pl.BlockSpec((per * unit, width), lambda n, p: (n, p))
    nxt = pl.BlockSpec((unit, width), lambda n, p: (jnp.minimum((n + 1) * per, nb - 1), p))
    return _pcall(
        cbody, name=f"att_bwd_sum{g}", grid=(steps, ATT_WIDTH // width), in_specs=[cur, cur, cur, nxt, cur, cur, nxt],
        out_specs=[cur] * 3, out_shape=[jax.ShapeDtypeStruct((s, ATT_WIDTH), BF16)] * 3,
        compiler_params=_cparams(("parallel", "parallel")),
    )(dq, dkc, dkp, dkp, dvc, dvp, dvp)


def _unit_lower_inverse_impl(n):
    eye = (lax.broadcasted_iota(jnp.int32, (1,) + n.shape[1:], 1)
           == lax.broadcasted_iota(jnp.int32, (1,) + n.shape[1:], 2))
    t = jnp.where(eye, 1.0, 0.0) + n
    pw = n
    for _ in range(5):
        pw = _bnn(pw, pw)
        t = t + _bnn(t, pw)
    return t


@jax.custom_vjp
def _unit_lower_inverse(n):
    return _unit_lower_inverse_impl(n)


def _unit_lower_inverse_fwd(n):
    t = _unit_lower_inverse_impl(n)
    return t, t


_unit_lower_inverse.defvjp(_unit_lower_inverse_fwd, lambda t, g: (_bnt(_btn(t, g), t),))


def _scan_chunk(r, lw, k, v, a, b, s0):
    c = SCAN_CHUNK
    p = s0.shape[0]
    ri = lax.broadcasted_iota(jnp.int32, (c, c), 0)
    ci = lax.broadcasted_iota(jnp.int32, (c, c), 1)
    cum = jnp.dot((ci <= ri).astype(F32), lw, precision=HI, preferred_element_type=F32)
    tot = jnp.sum(lw, axis=0, keepdims=True)
    ma = (lax.broadcasted_iota(jnp.int32, (c, 128 * p), 1) & 127) < 64

    def pairs(x):
        return jnp.concatenate([x[None, :, 128 * j:128 * (j + 1)] for j in range(p)], axis=0)

    def stack(x):
        return jnp.concatenate([pairs(jnp.where(ma, x, 0.0)), pairs(jnp.where(ma, 0.0, x))], axis=1)

    einv, eend = jnp.exp(-cum), jnp.exp(tot - cum)
    ra, aa = stack(r * jnp.exp(cum)), stack(a * jnp.exp(cum - lw))
    bi, ki, be, ke, vs = stack(b * einv), stack(k * einv), stack(b * eend), stack(k * eend), stack(v)
    r2 = lax.broadcasted_iota(jnp.int32, (1, 2 * c, 2 * c), 1)
    c2 = lax.broadcasted_iota(jnp.int32, (1, 2 * c, 2 * c), 2)
    same = (r2 >= c) == (c2 >= c)
    strict = jnp.logical_and(same, c2 < r2)
    incl = jnp.logical_and(same, c2 <= r2)
    s0 = jnp.where(same, s0, 0.0)
    prod = _bnt(jnp.concatenate([aa, ra], axis=1), jnp.concatenate([bi, ki], axis=1))
    a_ab = jnp.where(strict, prod[:, :2 * c, :2 * c], 0.0)
    a_ak = jnp.where(strict, prod[:, :2 * c, 2 * c:], 0.0)
    a_rb = jnp.where(incl, prod[:, 2 * c:, :2 * c], 0.0)
    a_rk = jnp.where(incl, prod[:, 2 * c:, 2 * c:], 0.0)
    t = _unit_lower_inverse(a_ab)
    u = _bnn(t, _bnt(aa, s0) + _bnn(a_ak, vs))
    uv = jnp.concatenate([u, vs], axis=1)
    ys = _bnt(ra, s0) + _bnn(jnp.concatenate([a_rb, a_rk], axis=2), uv)
    s1 = s0 * pairs(jnp.exp(tot)) + _btn(uv, jnp.concatenate([be, ke], axis=1))
    y3 = ys[:, :c] + ys[:, c:]
    return jnp.concatenate([y3[j] for j in range(p)], axis=1), s1


def _scan_specs(rev, n):
    def at(i):
        return n - 1 - i if rev else i

    def cm(cb):
        return pl.BlockSpec((SCAN_CHUNK, D), lambda i: (at(i), cb))

    return cm, pl.BlockSpec((1, SCAN_PAIRS, 128, 128), lambda i: (at(i), 0, 0, 0))


def _comm_phases(comm, refs, n):
    k = comm.n
    srcs, outs, sems = refs[:k], refs[k:2 * k], refs[2 * k:]
    i = pl.program_id(0)

    def before():
        @pl.when(i == 0)
        def _():
            comm.first(srcs, outs, sems)

    def after():
        if comm.mid is not None:
            @pl.when(i == (3 * n) // 4)
            def _():
                comm.mid(srcs, outs, sems)

        @pl.when(i == n - 1)
        def _():
            comm.last(srcs, outs, sems)

    return before, after


def _scan_fwd(zs, lw, km, aa, bb, comm):
    s = zs.shape[0]
    n = s // SCAN_CHUNK
    cm, st = _scan_specs(False, n)
    k = comm.n

    def body(*refs):
        r_ref, lw_ref, k_ref, v_ref, a_ref, b_ref = refs[:6]
        y_ref, s0_ref = refs[6 + k:8 + k]
        state = refs[8 + 2 * k]
        before, after = _comm_phases(comm, refs[6:6 + k] + refs[8 + k:8 + 2 * k] + refs[9 + 2 * k:], n)
        before()

        @pl.when(pl.program_id(0) == 0)
        def _():
            state[...] = jnp.zeros_like(state)

        s0 = state[...]
        s0_ref[0] = s0
        y, s1 = _scan_chunk(*[ref[...] for ref in (r_ref, lw_ref, k_ref, v_ref, a_ref, b_ref)], s0)
        y_ref[...] = y
        state[...] = s1
        after()

    res = _pcall(
        body, name="scan_fwd", grid=(n,), in_specs=[cm(0), cm(0), cm(0), cm(2), cm(0), cm(0)] + [_HBM] * k,
        out_specs=[cm(0), st] + [_HBM] * k,
        out_shape=[jax.ShapeDtypeStruct((s, D), F32), jax.ShapeDtypeStruct((n, 8, 128, 128), F32)] + comm.out_shape,
        scratch_shapes=[pltpu.VMEM((SCAN_PAIRS, 128, 128), F32)] + comm.sems,
        compiler_params=_cparams(("arbitrary",)),
    )(zs, lw, km, zs, aa, bb, *comm.ins)
    return res[0], res[1], res[2:]


def _scan_bwd(zs, lw, km, aa, bb, s0s, dy, comm):
    s = zs.shape[0]
    n = s // SCAN_CHUNK
    cm, st = _scan_specs(True, n)
    k = comm.n

    def body(*refs):
        r_ref, lw_ref, k_ref, v_ref, a_ref, b_ref, s0_ref, dy_ref = refs[:8]
        douts = refs[8 + k:14 + k]
        dstate = refs[14 + 2 * k]
        before, after = _comm_phases(comm, refs[8:8 + k] + refs[14 + k:14 + 2 * k] + refs[15 + 2 * k:], n)
        before()

        @pl.when(pl.program_id(0) == 0)
        def _():
            dstate[...] = jnp.zeros_like(dstate)

        prim = [ref[...] for ref in (r_ref, lw_ref, k_ref, v_ref, a_ref, b_ref)] + [s0_ref[0]]
        _, vjp = jax.vjp(_scan_chunk, *prim)
        grads = vjp((dy_ref[...], dstate[...]))
        for ref, gr in zip(douts, grads[:6]):
            ref[...] = gr
        dstate[...] = grads[6]
        after()

    res = _pcall(
        body, name="scan_bwd", grid=(n,),
        in_specs=[cm(0), cm(0), cm(0), cm(2), cm(0), cm(0), st, cm(0)] + [_HBM] * k,
        out_specs=[cm(0)] * 6 + [_HBM] * k, out_shape=[jax.ShapeDtypeStruct((s, D), F32)] * 6 + comm.out_shape,
        scratch_shapes=[pltpu.VMEM((SCAN_PAIRS, 128, 128), F32)] + comm.sems,
        compiler_params=_cparams(("arbitrary",)),
    )(zs, lw, km, zs, aa, bb, s0s, dy, *comm.ins)
    return res[:6], res[6:]


_HBM = pl.BlockSpec(memory_space=pltpu.HBM)


def _me():
    return lax.axis_index("x"), lax.axis_index("y"), lax.axis_index("c")


def _allgather8(src, name):
    def body(src_ref, out_ref, ssem, rsem, lsem):
        x, y, c = _me()
        me = 4 * x + 2 * y + c
        local = pltpu.make_async_copy(src_ref, out_ref.at[me], lsem)
        local.start()
        peers = []
        for k in range(1, 8):
            peers.append(((1 - x) if k & 4 else x, (1 - y) if k & 2 else y, (1 - c) if k & 1 else c))
        sends = []
        for k, peer in enumerate(peers):
            cp = pltpu.make_async_remote_copy(src_ref, out_ref.at[me], ssem.at[k], rsem.at[k], device_id=peer,
                                              device_id_type=MESH)
            cp.start()
            sends.append(cp)
        for k, (px, py, pc) in enumerate(peers):
            pltpu.make_async_remote_copy(src_ref, out_ref.at[4 * px + 2 * py + pc], ssem.at[k], rsem.at[k],
                                         device_id=(px, py, pc), device_id_type=MESH).wait_recv()
        for cp in sends:
            cp.wait_send()
        local.wait()

    return _pcall(
        body, name=name, in_specs=[_HBM], out_specs=_HBM, out_shape=jax.ShapeDtypeStruct((8,) + src.shape, src.dtype),
        scratch_shapes=[pltpu.SemaphoreType.DMA((7,)), pltpu.SemaphoreType.DMA((7,)), pltpu.SemaphoreType.DMA],
    )(src)


def _other_chips(x, y):
    return [(1 - x, y), (x, 1 - y), (1 - x, 1 - y)]


def _remote(src, dst, ssem, rsem, to):
    return pltpu.make_async_remote_copy(src, dst, ssem, rsem, device_id=to, device_id_type=MESH)


class _GatherWeights:
    def __init__(self, shards):
        self.ins = list(shards)
        n = self.n = len(shards)
        self.out_shape = [jax.ShapeDtypeStruct((4,) + t.shape, t.dtype) for t in shards]
        self.sems = [pltpu.SemaphoreType.DMA((6 * n,)), pltpu.SemaphoreType.DMA((6 * n,)),
                     pltpu.SemaphoreType.DMA((n,)), pltpu.SemaphoreType.DMA((n,))]

    def _copies(self, srcs, outs, sems):
        ssem, rsem, lsem, osem = sems
        x, y, c = _me()
        me = 2 * x + y
        own, ici, landed, passed, passed_in = [], [], [], [], []
        for a in range(self.n):
            h = self.ins[a].shape[0] // 2
            mine, other = pl.ds(c * h, h), pl.ds((1 - c) * h, h)
            own.append(_remote(srcs[a], outs[a].at[me], lsem.at[a], osem.at[a], (x, y, 1 - c)))
            for k, (px, py) in enumerate(_other_chips(x, y)):
                s1, r1, s2, r2 = ssem.at[6 * a + k], rsem.at[6 * a + k], ssem.at[6 * a + 3 + k], rsem.at[6 * a + 3 + k]
                got, got_sib = outs[a].at[2 * px + py, mine], outs[a].at[2 * px + py, other]
                ici.append(_remote(srcs[a].at[mine], outs[a].at[me, mine], s1, r1, (px, py, c)))
                landed.append(_remote(got, got, s1, r1, (px, py, c)))
                passed.append(_remote(got, got, s2, r2, (x, y, 1 - c)))
                passed_in.append(_remote(got_sib, got_sib, s2, r2, (x, y, 1 - c)))
        return own, ici, landed, passed, passed_in

    def first(self, srcs, outs, sems):
        own, ici, _, _, _ = self._copies(srcs, outs, sems)
        for cp in own + ici:
            cp.start()

    def mid(self, srcs, outs, sems):
        _, _, landed, passed, _ = self._copies(srcs, outs, sems)
        for arrived, onward in zip(landed, passed):
            arrived.wait_recv()
            onward.start()

    def last(self, srcs, outs, sems):
        own, ici, _, passed, passed_in = self._copies(srcs, outs, sems)
        for cp in passed_in:
            cp.wait_recv()
        for cp in ici + passed:
            cp.wait_send()
        for cp in own:
            cp.wait()


class _ScatterToChips:
    def __init__(self, parts):
        self.ins = list(parts)
        n = self.n = len(parts)
        self.out_shape = [jax.ShapeDtypeStruct(t.shape, t.dtype) for t in parts]
        self.sems = [pltpu.SemaphoreType.DMA((3 * n,)), pltpu.SemaphoreType.DMA((3 * n,)), pltpu.SemaphoreType.DMA((n,))]

    def _copies(self, srcs, outs, sems):
        ssem, rsem, lsem = sems
        x, y, c = _me()
        me = 2 * x + y
        own, out, landed = [], [], []
        for a in range(self.n):
            own.append(pltpu.make_async_copy(srcs[a].at[me], outs[a].at[me], lsem.at[a]))
            for k, (px, py) in enumerate(_other_chips(x, y)):
                dst = outs[a].at[2 * px + py]
                out.append(_remote(srcs[a].at[2 * px + py], outs[a].at[me], ssem.at[3 * a + k], rsem.at[3 * a + k],
                                   (px, py, c)))
                landed.append(_remote(dst, dst, ssem.at[3 * a + k], rsem.at[3 * a + k], (px, py, c)))
        return own, out, landed

    def first(self, srcs, outs, sems):
        own, out, _ = self._copies(srcs, outs, sems)
        for cp in own + out:
            cp.start()

    mid = None

    def last(self, srcs, outs, sems):
        own, out, landed = self._copies(srcs, outs, sems)
        for cp in landed:
            cp.wait_recv()
        for cp in own:
            cp.wait()
        for cp in out:
            cp.wait_send()


def _run_comm(comm, name):
    n = comm.n

    def body(*refs):
        srcs, outs, sems = refs[:n], refs[n:2 * n], refs[2 * n:]
        comm.first(srcs, outs, sems)
        if comm.mid is not None:
            comm.mid(srcs, outs, sems)
        comm.last(srcs, outs, sems)

    return _pcall(body, name=name, in_specs=[_HBM] * n, out_specs=[_HBM] * n, out_shape=comm.out_shape,
                  scratch_shapes=comm.sems)(*comm.ins)


class _NoComm:
    n, ins, out_shape, sems, mid = 0, [], [], [], None

    def first(self, srcs, outs, sems):
        pass

    def last(self, srcs, outs, sems):
        pass


_NOTHING = _NoComm()


class _SiblingHalves:
    mid = None

    def __init__(self, grads):
        self.ins = list(grads)
        n = self.n = len(grads)
        self.out_shape = [jax.ShapeDtypeStruct((4, t.shape[1] // 2, t.shape[2]), t.dtype) for t in grads]
        self.sems = [pltpu.SemaphoreType.DMA((n,)), pltpu.SemaphoreType.DMA((n,))]

    def _copies(self, srcs, outs, sems):
        ssem, rsem = sems
        x, y, c = _me()
        copies = []
        for a in range(self.n):
            h = self.ins[a].shape[1] // 2
            copies.append(_remote(srcs[a].at[:, pl.ds((1 - c) * h, h)], outs[a], ssem.at[a], rsem.at[a], (x, y, 1 - c)))
        return copies

    def first(self, srcs, outs, sems):
        for cp in self._copies(srcs, outs, sems):
            cp.start()

    def last(self, srcs, outs, sems):
        for cp in self._copies(srcs, outs, sems):
            cp.wait()


def _reduce_finish(reds, name):
    n = len(reds)

    def body(*refs):
        outs = refs[n:2 * n]
        ssem, rsem = refs[2 * n:]
        x, y, c = _me()
        copies = []
        for a in range(n):
            h = reds[a].shape[0] // 2
            mine = outs[a].at[pl.ds(c * h, h)]
            copies.append(_remote(mine, mine, ssem.at[a], rsem.at[a], (x, y, 1 - c)))
        for cp in copies:
            cp.start()
        for a in range(n):
            h = reds[a].shape[0] // 2
            dst = outs[a].at[pl.ds((1 - c) * h, h)]
            _remote(dst, dst, ssem.at[a], rsem.at[a], (x, y, 1 - c)).wait_recv()
        for cp in copies:
            cp.wait_send()

    return _pcall(
        body, name=name, in_specs=[_HBM] * n, out_specs=[_HBM] * n,
        out_shape=[jax.ShapeDtypeStruct(t.shape, t.dtype) for t in reds],
        input_output_aliases={a: a for a in range(n)},
        scratch_shapes=[pltpu.SemaphoreType.DMA((n,)), pltpu.SemaphoreType.DMA((n,))],
    )(*reds)


def _half_sum(fn, full, halves, out_full, out_dtype, core, name):
    p, h, c = (halves[0].shape if halves else (full[0].shape[0], full[0].shape[1] // 2, full[0].shape[2]))
    br = _div(h, max(16, (1 << 19) // (p * c)), 16)
    nb = h // br
    mine3 = pl.BlockSpec((p, br, c), lambda i, core_ref: (0, core_ref[0] * nb + i, 0))
    half3 = pl.BlockSpec((p, br, c), lambda i, core_ref: (0, i, 0))

    def body(core_ref, *refs):
        refs[-1][...] = fn(*[t[...].astype(F32) for t in refs[:-1]]).astype(out_dtype)

    if out_full:
        out_spec = pl.BlockSpec((br, c), lambda i, core_ref: (core_ref[0] * nb + i, 0))
        out_shape = jax.ShapeDtypeStruct((2 * h, c), out_dtype)
    else:
        out_spec, out_shape = half3, jax.ShapeDtypeStruct((p, h, c), out_dtype)
    return _pcall(
        body, name=name,
        grid_spec=pltpu.PrefetchScalarGridSpec(
            num_scalar_prefetch=1, grid=(nb,), in_specs=[mine3] * len(full) + [half3] * len(halves),
            out_specs=out_spec),
        out_shape=out_shape, compiler_params=_cparams(("parallel",)),
    )(core, *full, *halves)


def _ada_fwd(c_all, w, b):
    def body(c_ref, w_ref, b_ref, o_ref):
        o_ref[...] = jnp.dot(c_ref[...], w_ref[...], precision=HI, preferred_element_type=F32) + b_ref[...]

    return _pcall(body, name="ada_fwd", out_shape=jax.ShapeDtypeStruct((c_all.shape[0], w.shape[1]), F32),
                  compiler_params=pltpu.CompilerParams(vmem_limit_bytes=VMEM_LIMIT))(c_all, w, b)


def _ada_bwd(c_all_t, d):
    def body(c_ref, d_ref, o_ref):
        o_ref[...] = jnp.dot(c_ref[...], d_ref[...], precision=HI, preferred_element_type=F32)

    return _pcall(body, name="ada_bwd", out_shape=jax.ShapeDtypeStruct((c_all_t.shape[0], d.shape[1]), F32),
                  compiler_params=pltpu.CompilerParams(vmem_limit_bytes=VMEM_LIMIT))(c_all_t, d)


def _sum_lead(x, name):
    p, r, n = x.shape
    br = _div(r, 512, 8)

    def body(x_ref, o_ref):
        acc = x_ref[0]
        for j in range(1, p):
            acc = acc + x_ref[j]
        o_ref[...] = acc

    return _pcall(
        body, name=name, grid=(r // br,), in_specs=[pl.BlockSpec((p, br, n), lambda i: (0, i, 0))],
        out_specs=pl.BlockSpec((br, n), lambda i: (i, 0)), out_shape=jax.ShapeDtypeStruct((r, n), F32),
        compiler_params=_cparams(("parallel",)),
    )(x)


def _adamw(w, g, m, v, name):
    shape = w.shape
    cols = shape[-1]
    w2, g2, m2, v2 = [t.reshape(-1, cols) for t in (w, g, m, v)]
    rows = w2.shape[0]
    br = _div(rows, max(8, (1 << 19) // cols // 8 * 8), 8)
    outs = _rows_fwd(_f_adamw, [(t, cols, 0) for t in (w2, g2, m2, v2)], [], [(cols, F32)] * 3, name=name, br=br)
    return [o.reshape(shape) for o in outs]


_BIG = (("w_in", 1), ("w_up", 1), ("w_down", 0), ("w_o", 0), ("w_rwkv_out", 0), ("w_att_out", 1), ("w2", 1), ("a2", 1),
        ("g2", 1))


_NEEDED_FIRST = ("w_in", "w_att_out", "w2", "a2", "g2")
_NEEDED_LATER = ("w_up", "w_down", "w_o", "w_rwkv_out")
_DONE_EARLY = ("w_up", "w_down", "w_o", "w_rwkv_out", "w_att_out")
_DONE_LATE = ("w_in", "w2", "a2", "g2")


def _cols_joined(t):
    return jnp.concatenate([t[j] for j in range(4)], axis=1)


def _cols_split(t):
    n = t.shape[1] // 4
    return jnp.stack([t[:, j * n:(j + 1) * n] for j in range(4)])


def _rows_joined(t):
    return t.reshape(4 * t.shape[1], t.shape[2])


def _rows_split(t):
    return t.reshape(4, t.shape[0] // 4, t.shape[1])


def _step_to_scan(x, tgt, ada, wts):
    sh1, sc1, gt1, sh2, sc2, gt2 = ada
    br = 256
    grp = lax.broadcasted_iota(jnp.int32, (D, 128), 0) // 64 == lax.broadcasted_iota(jnp.int32, (D, 128), 1)
    e = grp.astype(F32)
    et = e.T
    w_in = _cols_joined(wts["w_in"])
    w_att = w_in[:, :N_ATT]
    w_rw = jnp.pad(w_in[:, N_ATT:N_ATT + N_RW], ((0, 0), (0, N_RWP - N_RW)))
    w_gate = w_in[:, N_ATT + N_RW:]
    mu = jnp.pad(wts["mu_shift"], ((0, 0), (0, N_RWP - N_RW)))
    wl = jnp.zeros((N_LORA, 3 * D), F32)
    wl = wl.at[0:64, 0:D].set(_cols_joined(wts["w2"]).astype(F32))
    wl = wl.at[64:128, D:2 * D].set(_cols_joined(wts["a2"]).astype(F32))
    wl = wl.at[128:288, 2 * D:3 * D].set(_cols_joined(wts["g2"]).astype(F32))
    pre1_c = [wts["norm1_w"], sc1, sh1]
    (h1,) = _rows_fwd(_f_pre, [(x, D, 0)], pre1_c, [(D, BF16), None], name="pre1_fwd", br=br)
    att_in = _mm(h1, w_att, name="mm_att_in")
    z = _mm(h1, w_rw, name="mm_rw_in")
    gate_in = _mm(h1, w_gate, name="mm_gate_in")
    att_o, att_l = [], []
    for g, (_, dil) in enumerate(ATT_PATTERNS):
        o, l = _att_fwd(att_in, g, dil)
        att_o.append(o)
        att_l.append(l)
    comb_rows = [(t, ATT_WIDTH, 0) for t in att_o + att_l]
    (att,) = _rows_fwd(_f_comb, comb_rows, [], [(ATT_WIDTH, BF16)], name="comb_fwd", br=br)
    y_att = _mm(att, wts["w_att_out"], b_chip=True, name="mm_att_out")
    zs = _shift_fwd(z, mu)
    rwpre_c = [wts["w0"], wts["a0"], wts["k_k"], wts["k_a"], wl, e, et]
    lw, km, aa, bb, gg = _rows_fwd(_f_rwpre, [(zs, N_RWP, 0)], rwpre_c,
                                   [None, (D, F32), (D, F32), None, (D, F32), (D, F32), (D, F32)],
                                   name="rwpre_fwd", br=br)
    return dict(x=x, tgt=tgt, wts=wts, br=br, e=e, et=et, gt1=gt1, sc2=sc2, sh2=sh2, gt2=gt2, w_att=w_att, w_rw=w_rw,
                w_gate=w_gate, mu=mu, pre1_c=pre1_c, h1=h1, att_in=att_in, z=z, gate_in=gate_in, comb_rows=comb_rows,
                att=att, y_att=y_att, zs=zs, rwpre_c=rwpre_c, lw=lw, km=km, aa=aa, bb=bb, gg=gg)


def _step_between_scans(st, y_raw, late):
    x, tgt, wts, br, e, et = st["x"], st["tgt"], st["wts"], st["br"], st["e"], st["et"]
    zs, km, gg, gate_in, y_att, att = st["zs"], st["km"], st["gg"], st["gate_in"], st["y_att"], st["att"]
    comb_rows, att_in = st["comb_rows"], st["att_in"]
    gt1, sc2, sh2, gt2 = st["gt1"], st["sc2"], st["sh2"], st["gt2"]
    w_up, w_ao = late["w_up"], wts["w_att_out"]
    w_down, w_o, w_ro = _rows_joined(late["w_down"]), _rows_joined(late["w_o"]), _rows_joined(late["w_rwkv_out"])
    bga, bgr = wts["b_gate"][:, :D], wts["b_gate"][:, D:]
    post_rows = [(y_raw, D, 0), (zs, D, 0), (zs, D, 2), (km, D, 0), (gg, D, 0)]
    post_c = [wts["lnx_w"], wts["lnx_b"], wts["r_k"], e, et]
    (rw_out,) = _rows_fwd(_f_rwpost, post_rows, post_c, [(D, BF16)], name="rwpost_fwd", br=br)
    y_rw = _mm(rw_out, w_ro, name="mm_rw_out")
    mix_rows = [(gate_in, D, 0), (gate_in, D, 1), (y_att, D, 0), (y_rw, D, 0)]
    (mix,) = _rows_fwd(_f_mix, mix_rows, [bga, bgr], [(D, BF16)], name="mix_fwd", br=br)
    o = _mm(mix, w_o, name="mm_o")
    pre2_c = [gt1, wts["norm2_w"], sc2, sh2]
    x1, h2 = _rows_fwd(_f_pre2, [(x, D, 0), (o, D, 0)], pre2_c, [(D, F32), (D, BF16)], name="pre2_fwd", br=br)
    u = _mm(h2, w_up, b_chip=True, name="mm_up")
    act = _conv_fwd(u, wts["conv_w"], wts["conv_b"])
    f = _mm(act, w_down, name="mm_down")
    fin_rows = [(x1, D, 0), (f, D, 0), (tgt, D, 0)]
    fin_c = [gt2, wts["norm_f_w"]]

    def fin_fwd(*a):
        (l,) = _f_fin(*a)
        return (jnp.broadcast_to(jnp.sum(l, axis=0, keepdims=True), (8, 128)),)

    (loss_acc,) = _rows_fwd(fin_fwd, fin_rows, fin_c, [], name="fin_fwd", br=br, acc_shape=(8, 128))

    gw = {}
    dx1a, df, d_gt2, gw["norm_f_w"] = _rows_bwd(
        _f_fin, fin_rows, fin_c, [[]], wrt_rows=[0, 1], wrt_consts=[0, 1], drow_dtypes=[F32, BF16],
        name="fin_bwd", br=br, unit_cot=True)
    dact = _mm(df, w_down, tb=True, name="mm_dact")
    gw["w_down"] = _rows_split(_mm(act, df, ta=True, name="mm_dw_down"))
    du, gw["conv_w"], gw["conv_b"] = _conv_bwd(u, wts["conv_w"], wts["conv_b"], dact)
    dh2 = _mm(du, w_up, tb=True, b_chip=True, name="mm_dh2")
    gw["w_up"] = _mm(h2, du, ta=True, out_chip=True, name="mm_dw_up")
    dxa, do, d_gt1, gw["norm2_w"], d_sc2, d_sh2 = _rows_bwd(
        _f_pre2, [(x, D, 0), (o, D, 0)], pre2_c, [[(dx1a, D, 0)], [(dh2, D, 0)]], wrt_rows=[0, 1],
        wrt_consts=[0, 1, 2, 3], drow_dtypes=[F32, BF16], name="pre2_bwd", br=br)
    dmix = _mm(do, w_o, tb=True, name="mm_dmix")
    gw["w_o"] = _rows_split(_mm(mix, do, ta=True, name="mm_dw_o"))
    dga, dgr, dya, dyr, d_bga, d_bgr = _rows_bwd(
        _f_mix, mix_rows, [bga, bgr], [[(dmix, D, 0)]], wrt_rows=[0, 1, 2, 3], wrt_consts=[0, 1],
        drow_dtypes=[BF16] * 4, name="mix_bwd", br=br)
    gw["b_gate"] = jnp.concatenate([d_bga, d_bgr], axis=1)
    datt = _mm(dya, w_ao, tb=True, b_chip=True, name="mm_datt")
    gw["w_att_out"] = _mm(att, dya, ta=True, out_chip=True, name="mm_dw_att_out")
    drw = _mm(dyr, w_ro, tb=True, name="mm_drw")
    gw["w_rwkv_out"] = _rows_split(_mm(rw_out, dyr, ta=True, name="mm_dw_rw_out"))
    dcomb = _rows_bwd(_f_comb, comb_rows, [], [[(datt, ATT_WIDTH, 0)]], wrt_rows=list(range(6)), wrt_consts=[],
                      drow_dtypes=[F32] * 6, name="comb_bwd", br=br)
    datt_in = []
    for g, (_, dil) in enumerate(ATT_PATTERNS):
        datt_in += _att_bwd(att_in, g, dil, dcomb[g], dcomb[3 + g])
    datt_in = jnp.concatenate(datt_in, axis=1)
    dy_raw, dr_p, dv_p, dkm_p, dgg, gw["lnx_w"], gw["lnx_b"], gw["r_k"], *recv_early = _rows_bwd(
        _f_rwpost, post_rows, post_c, [[(drw, D, 0)]], wrt_rows=[0, 1, 2, 3, 4], wrt_consts=[0, 1, 2],
        drow_dtypes=[F32] * 5, name="rwpost_bwd", br=br, comm=_SiblingHalves([gw[n] for n in _DONE_EARLY]))
    st.update(loss=loss_acc[0, 0], gw=gw, dxa=dxa, dgate=jnp.concatenate([dga, dgr], axis=1), datt_in=datt_in,
              dy_raw=dy_raw, dr_p=dr_p, dv_p=dv_p, dkm_p=dkm_p, dgg=dgg, d_ada_late=(d_gt1, d_sh2, d_sc2, d_gt2),
              recv_early=recv_early)
    return st


def _step_after_scan(st, scan_grads):
    x, br, gw, h1, zs = st["x"], st["br"], st["gw"], st["h1"], st["zs"]
    dr_s, dlw, dkm_s, dv_s, daa, dbb = scan_grads
    pre_cots = [[(st["dr_p"], D, 0), (dr_s, D, 0)], [(dlw, D, 0)], [(st["dkm_p"], D, 0), (dkm_s, D, 0)],
                [(st["dv_p"], D, 0), (dv_s, D, 0)], [(daa, D, 0)], [(dbb, D, 0)], [(st["dgg"], D, 0)]]
    dzs, gw["w0"], gw["a0"], gw["k_k"], gw["k_a"], dwl = _rows_bwd(
        _f_rwpre, [(zs, N_RWP, 0)], st["rwpre_c"], pre_cots, wrt_rows=[0], wrt_consts=[0, 1, 2, 3, 4],
        drow_dtypes=[F32], name="rwpre_bwd", br=128)
    gw["w2"], gw["a2"] = _cols_split(dwl[0:64, 0:D]), _cols_split(dwl[64:128, D:2 * D])
    gw["g2"] = _cols_split(dwl[128:288, 2 * D:3 * D])
    dz, dmu = _shift_bwd(st["z"], st["mu"], dzs)
    gw["mu_shift"] = dmu[:, :N_RW]
    datt_in, dgate = st["datt_in"], st["dgate"]
    dh1 = _mm(datt_in, st["w_att"], tb=True, name="mm_dh1_att")
    dh1 = _mm(dgate, st["w_gate"], tb=True, add=dh1, name="mm_dh1_gate")
    dh1 = _mm(dz, st["w_rw"], tb=True, add=dh1, name="mm_dh1_rw")
    gw["w_in"] = _cols_split(jnp.concatenate([_mm(h1, datt_in, ta=True, name="mm_dw_att"),
                                              _mm(h1, dz, ta=True, name="mm_dw_rw")[:, :N_RW],
                                              _mm(h1, dgate, ta=True, name="mm_dw_gate")], axis=1))
    grad_x, gw["norm1_w"], d_sc1, d_sh1, *recv_late = _rows_bwd(
        _f_pre, [(x, D, 0)], st["pre1_c"], [[(dh1, D, 0)], [(st["dxa"], D, 0)]], wrt_rows=[0], wrt_consts=[0, 1, 2],
        drow_dtypes=[F32], name="pre1_bwd", br=br, comm=_SiblingHalves([gw[n] for n in _DONE_LATE]))
    d_gt1, d_sh2, d_sc2, d_gt2 = st["d_ada_late"]
    return st["loss"], grad_x, (d_sh1, d_sc1, d_gt1, d_sh2, d_sc2, d_gt2), gw, recv_late


_SMALL = ("b_ada", "norm1_w", "b_gate", "mu_shift", "w0", "a0", "k_k", "k_a", "r_k", "lnx_w", "lnx_b", "norm2_w",
          "conv_b", "norm_f_w")
_NAMES = ("w_ada", "b_ada", "norm1_w", "w_in", "b_gate", "mu_shift", "w0", "w2", "a0", "a2", "g2", "k_k", "k_a", "r_k",
          "lnx_w", "lnx_b", "w_att_out", "w_rwkv_out", "w_o", "norm2_w", "w_up", "conv_w", "conv_b", "w_down",
          "norm_f_w")


def kernel(x, c, w_ada, b_ada, norm1_w, w_in, b_gate, mu_shift, w0, w2, a0, a2, g2, k_k, k_a, r_k, lnx_w, lnx_b, w_att_out, w_rwkv_out, w_o, norm2_w, w_up, conv_w, conv_b, w_down, norm_f_w, loss_target, m_w_ada, m_b_ada, m_norm1_w, m_w_in, m_b_gate, m_mu_shift, m_w0, m_w2, m_a0, m_a2, m_g2, m_k_k, m_k_a, m_r_k, m_lnx_w, m_lnx_b, m_w_att_out, m_w_rwkv_out, m_w_o, m_norm2_w, m_w_up, m_conv_w, m_conv_b, m_w_down, m_norm_f_w, v_w_ada, v_b_ada, v_norm1_w, v_w_in, v_b_gate, v_mu_shift, v_w0, v_w2, v_a0, v_a2, v_g2, v_k_k, v_k_a, v_r_k, v_lnx_w, v_lnx_b, v_w_att_out, v_w_rwkv_out, v_w_o, v_norm2_w, v_w_up, v_conv_w, v_conv_b, v_w_down, v_norm_f_w):
    args = dict(locals())
    p, pm, pv = {}, {}, {}
    for name in _NAMES:
        for dst, key in ((p, name), (pm, "m_" + name), (pv, "v_" + name)):
            t = args[key]
            dst[name] = t.reshape(1, -1) if name in ("r_k", "norm_f_w") else t.reshape(t.shape[-2], t.shape[-1])
    xi, yi, ci = _me()
    chip = 2 * xi + yi
    dev = 4 * xi + 2 * yi + ci
    x2, tgt = x[0], loss_target[0]

    n_cw = 3 * (2 * D_FF // 4)
    vec = jnp.concatenate([c.reshape(-1), p["conv_w"].reshape(-1), jnp.zeros((8 * D - D - n_cw,), F32)]).reshape(8, D)
    g0 = _allgather8(vec, "gather_c").reshape(8, 8 * D)
    c_all = g0[:, :D]
    conv_w_full = jnp.concatenate([g0[2 * j, D:D + n_cw].reshape(3, -1) for j in range(4)], axis=1)
    n_ada = 6 * D // 4
    b_ada_sh = lax.dynamic_slice(p["b_ada"], (0, chip * n_ada), (1, n_ada))
    ada_sh = _ada_fwd(c_all, p["w_ada"], b_ada_sh)
    ga = _allgather8(ada_sh, "gather_ada")
    ada_all = jnp.concatenate([ga[2 * j] for j in range(4)], axis=1)
    ada_row = lax.dynamic_slice(ada_all, (dev, 0), (1, 6 * D))
    ada = [ada_row[:, j * D:(j + 1) * D] for j in range(6)]

    big = [n for n, _ in _BIG]
    shard = {n: p[n].astype(BF16) for n in big}
    wts = dict(zip(_NEEDED_FIRST, _run_comm(_GatherWeights([shard[n] for n in _NEEDED_FIRST]), "gather_w")))
    for n in _SMALL:
        wts[n] = p[n]
    wts["conv_w"] = conv_w_full
    core = ci.reshape(1).astype(jnp.int32)

    def chip_parts(gw, names, recv):
        return [_half_sum(lambda a, b: a + b, [gw[n]], [r], False, BF16, core, "reduce_add2_" + n)
                for n, r in zip(names, recv)]

    st = _step_to_scan(x2, tgt, ada, wts)
    y_raw, s0s, late = _scan_fwd(st["zs"], st["lw"], st["km"], st["aa"], st["bb"],
                                 _GatherWeights([shard[n] for n in _NEEDED_LATER]))
    st = _step_between_scans(st, y_raw, dict(zip(_NEEDED_LATER, late)))
    scan_grads, slots_early = _scan_bwd(st["zs"], st["lw"], st["km"], st["aa"], st["bb"], s0s, st["dy_raw"],
                                        _ScatterToChips(chip_parts(st["gw"], _DONE_EARLY, st["recv_early"])))
    loss_part, grad_x, d_ada, gw, recv_late = _step_after_scan(st, scan_grads)

    small = [jnp.concatenate(d_ada, axis=1)] + [gw[n] for n in _SMALL[1:]] + [gw["conv_w"], loss_part.reshape(1, 1)]
    sizes = [t.size for t in small]
    flat = jnp.concatenate([t.reshape(-1) for t in small])
    npad = (-flat.shape[0]) % (8 * D)
    srows = (flat.shape[0] + npad) // D
    flat = jnp.concatenate([flat, jnp.zeros((npad,), F32)]).reshape(srows, D)
    parts = _allgather8(flat, "gather_small")
    tot = _sum_lead(parts, "sum_small").reshape(-1)
    pieces, pos = [], 0
    for sz in sizes:
        pieces.append(tot[pos:pos + sz])
        pos += sz
    grads = {}
    for n, piece in zip(_SMALL, pieces[:len(_SMALL)]):
        grads[n] = piece.reshape(p[n].shape)
    conv_w_grad = pieces[len(_SMALL)].reshape(3, 2 * D_FF)
    grads["conv_w"] = lax.dynamic_slice(conv_w_grad, (0, chip * (n_cw // 3)), (3, n_cw // 3))
    loss = pieces[-1][0]
    d_ada_all = parts[:, :6].reshape(8, 6 * D)
    grads["w_ada"] = _ada_bwd(c_all.T, lax.dynamic_slice(d_ada_all, (0, chip * n_ada), (8, n_ada)))

    slots_late = _run_comm(_ScatterToChips(chip_parts(gw, _DONE_LATE, recv_late)), "reduce_chips")
    order = _DONE_EARLY + _DONE_LATE
    reds = [_half_sum(lambda t: t[0] + t[1] + t[2] + t[3], [], [t], True, F32, core, "reduce_add4_" + n)
            for n, t in zip(order, list(slots_early) + list(slots_late))]
    for n, g in zip(order, _reduce_finish(reds, "reduce_sib2")):
        grads[n] = g

    outs_g, outs_d, outs_m, outs_v = [], [], [], []
    for name in _NAMES:
        g = grads[name]
        d, m, v = _adamw(p[name], g, pm[name], pv[name], "adamw_" + name)
        shape = args[name].shape
        outs_g.append(g.reshape(shape))
        outs_d.append(d.reshape(shape))
        outs_m.append(m.reshape(shape))
        outs_v.append(v.reshape(shape))
    return (loss, grad_x.reshape(x.shape), *outs_g, *outs_d, *outs_m, *outs_v)
```

```python
import functools

import jax
import jax.numpy as jnp
from jax import lax
from jax.experimental import pallas as pl
from jax.experimental.pallas import tpu as pltpu

F32 = jnp.float32
BF16 = jnp.bfloat16
HI = lax.Precision.HIGHEST
MESH = pl.DeviceIdType.MESH

D = 1024
ATT_PATTERNS = ((128, 1), (512, 4), (2048, 16))
ATT_BLOCK = 128
ATT_WIDTH = 512
N_ATT = 3 * 3 * ATT_WIDTH
N_RW = 3 * D + 64 + 64 + 160
N_RWP = 3456
N_LORA = N_RWP - 3 * D
N_GATE = 2 * D
D_FF = 2816
RMS_EPS = 1e-6
GN_EPS = 64e-5
SCAN_CHUNK = 64
SCAN_PAIRS = 8
NEG = -1e30
VMEM_LIMIT = 48 * 1024 * 1024

ADAM_LR, ADAM_B1, ADAM_B2, ADAM_EPS, ADAM_WD, ADAM_STEP = 0.001, 0.9, 0.999, 1e-08, 0.01, 10


def _pcall(body, **kw):
    return pl.pallas_call(body, **kw)


def _cparams(sem):
    return pltpu.CompilerParams(dimension_semantics=sem, vmem_limit_bytes=VMEM_LIMIT)


def _div(n, pref, mult):
    best = None
    d = mult
    while d <= min(n, pref):
        if n % d == 0:
            best = d
        d += mult
    return best if best else n


def _dg(a, b, ca, cb):
    return lax.dot_general(a.astype(BF16), b.astype(BF16), (((ca,), (cb,)), ((), ())), preferred_element_type=F32)


@jax.custom_vjp
def _nn(a, b):
    return _dg(a, b, 1, 0)


@jax.custom_vjp
def _nt(a, b):
    return _dg(a, b, 1, 1)


@jax.custom_vjp
def _tn(a, b):
    return _dg(a, b, 0, 0)


_nn.defvjp(lambda a, b: (_nn(a, b), (a, b)), lambda res, g: (_nt(g, res[1]), _tn(res[0], g)))
_nt.defvjp(lambda a, b: (_nt(a, b), (a, b)), lambda res, g: (_nn(g, res[1]), _tn(g, res[0])))
_tn.defvjp(lambda a, b: (_tn(a, b), (a, b)), lambda res, g: (_nt(res[1], g), _nn(res[0], g)))


def _bdg(a, b, ca, cb):
    return lax.dot_general(a.astype(BF16), b.astype(BF16), (((ca,), (cb,)), ((0,), (0,))), preferred_element_type=F32)


@jax.custom_vjp
def _bnn(a, b):
    return _bdg(a, b, 2, 1)


@jax.custom_vjp
def _bnt(a, b):
    return _bdg(a, b, 2, 2)


@jax.custom_vjp
def _btn(a, b):
    return _bdg(a, b, 1, 1)


_bnn.defvjp(lambda a, b: (_bnn(a, b), (a, b)), lambda res, g: (_bnt(g, res[1]), _btn(res[0], g)))
_bnt.defvjp(lambda a, b: (_bnt(a, b), (a, b)), lambda res, g: (_bnn(g, res[1]), _btn(g, res[0])))
_btn.defvjp(lambda a, b: (_btn(a, b), (a, b)), lambda res, g: (_bnt(res[1], g), _bnn(res[0], g)))


def _split2(x):
    hi = x.astype(BF16)
    lo = (x - hi.astype(F32)).astype(BF16)
    return hi, lo


def _hsum_impl(x, e, et):
    eb, etb = e.astype(BF16), et.astype(BF16)
    s = jnp.dot(x.astype(BF16), eb, preferred_element_type=F32)
    shi, slo = _split2(s)
    return jnp.dot(shi, etb, preferred_element_type=F32) + jnp.dot(slo, etb, preferred_element_type=F32)


@jax.custom_vjp
def _hsum(x, e, et):
    return _hsum_impl(x, e, et)


_hsum.defvjp(lambda x, e, et: (_hsum_impl(x, e, et), (e, et)),
             lambda res, g: (_hsum_impl(g, res[0], res[1]), jnp.zeros_like(res[0]), jnp.zeros_like(res[1])))


def _mm(a, b, *, ta=False, tb=False, out_dtype=F32, add=None, b_chip=False, out_chip=False, name):
    if ta:
        kdim, m = a.shape
    else:
        m, kdim = a.shape
    if b_chip:
        n = b.shape[1] if tb else 4 * b.shape[2]
    else:
        n = b.shape[0] if tb else b.shape[1]
    tm, tn, tk = _div(m, 1536, 128), _div(n, 1536, 128), _div(kdim, 1408, 128)
    if b_chip and tb:
        tk = kdim // 4
    if (b_chip and not tb) or out_chip:
        tn = n // 4
    nk = kdim // tk
    ca, cb = (0 if ta else 1), (1 if tb else 0)

    def body(*refs):
        a_ref, b_ref = refs[0], refs[1]
        add_ref = None if add is None else refs[2]
        o_ref = refs[2 if add is None else 3]
        part = lax.dot_general(a_ref[...], b_ref[...], (((ca,), (cb,)), ((), ())), preferred_element_type=F32)

        def finish(r):
            if add_ref is not None:
                r = r + add_ref[...]
            o_ref[...] = r.astype(o_ref.dtype)

        if nk == 1:
            finish(part)
            return
        acc = refs[-1]
        k = pl.program_id(2)

        @pl.when(k == 0)
        def _():
            acc[...] = part

        @pl.when(k > 0)
        def _():
            acc[...] += part

        @pl.when(k == nk - 1)
        def _():
            finish(acc[...])

    a_spec = pl.BlockSpec((tk, tm), lambda i, j, k: (k, i)) if ta else pl.BlockSpec((tm, tk), lambda i, j, k: (i, k))
    if b_chip:
        b_spec = (pl.BlockSpec((None, tn, tk), lambda i, j, k: (k, j, 0)) if tb
                  else pl.BlockSpec((None, tk, tn), lambda i, j, k: (j, k, 0)))
    else:
        b_spec = pl.BlockSpec((tn, tk), lambda i, j, k: (j, k)) if tb else pl.BlockSpec((tk, tn), lambda i, j, k: (k, j))
    in_specs = [a_spec, b_spec]
    args = [a, b]
    if add is not None:
        in_specs.append(pl.BlockSpec((tm, tn), lambda i, j, k: (i, j)))
        args.append(add)
    if out_chip:
        out_spec = pl.BlockSpec((None, tm, tn), lambda i, j, k: (j, i, 0))
        out_shape = jax.ShapeDtypeStruct((4, m, tn), out_dtype)
    else:
        out_spec = pl.BlockSpec((tm, tn), lambda i, j, k: (i, j))
        out_shape = jax.ShapeDtypeStruct((m, n), out_dtype)
    return _pcall(
        body, name=name, grid=(m // tm, n // tn, nk), in_specs=in_specs, out_specs=out_spec, out_shape=out_shape,
        scratch_shapes=[] if nk == 1 else [pltpu.VMEM((tm, tn), F32)],
        compiler_params=_cparams(("parallel", "parallel", "arbitrary")),
    )(*args)


def _row_spec(br, w, cb):
    return pl.BlockSpec((br, w), lambda i: (i, cb))


def _const_spec(shape):
    return pl.BlockSpec(shape, lambda i: (0,) * len(shape))


def _rows_fwd(fn, rows, consts, outs, *, name, br, acc_shape=None):
    s = rows[0][0].shape[0]
    nr, nc = len(rows), len(consts)
    kept = [k for k, o in enumerate(outs) if o is not None]

    def body(*refs):
        xs = [r[...].astype(F32) for r in refs[:nr]]
        cs = [c[...] for c in refs[nr:nr + nc]]
        res = fn(*xs, *cs)
        orefs = refs[nr + nc:]
        for j, k in enumerate(kept):
            orefs[j][...] = res[k].astype(orefs[j].dtype)
        if acc_shape is not None:
            acc_ref = orefs[len(kept)]

            @pl.when(pl.program_id(0) == 0)
            def _():
                acc_ref[...] = jnp.zeros_like(acc_ref)

            acc_ref[...] += res[len(outs)]

    in_specs = [_row_spec(br, w, cb) for (_, w, cb) in rows] + [_const_spec(c.shape) for c in consts]
    out_specs = [_row_spec(br, outs[k][0], 0) for k in kept]
    out_shape = [jax.ShapeDtypeStruct((s, outs[k][0]), outs[k][1]) for k in kept]
    if acc_shape is not None:
        out_specs.append(_const_spec(acc_shape))
        out_shape.append(jax.ShapeDtypeStruct(acc_shape, F32))
    return _pcall(
        body, name=name, grid=(s // br,), in_specs=in_specs, out_specs=out_specs, out_shape=out_shape,
        compiler_params=_cparams(("arbitrary",)),
    )(*[r[0] for r in rows], *consts)


def _rows_bwd(fn, rows, consts, cots, *, wrt_rows, wrt_consts, drow_dtypes, name, br, unit_cot=False, comm=None):
    comm = _NOTHING if comm is None else comm
    ncomm = comm.n
    nout = len(wrt_rows) + len(wrt_consts)
    s = rows[0][0].shape[0]
    nr, nc = len(rows), len(consts)
    flat_cots = [c for lst in cots for c in lst]
    ncot = len(flat_cots)

    def body(*refs):
        xs = [r[...].astype(F32) for r in refs[:nr]]
        cs = [c[...] for c in refs[nr:nr + nc]]
        cvals = [c[...].astype(F32) for c in refs[nr + nc:nr + nc + ncot]]
        orefs = refs[nr + nc + ncot + ncomm:]
        before, after = _comm_phases(comm, refs[nr + nc + ncot:nr + nc + ncot + ncomm] + orefs[nout:], s // br)
        before()

        def g(*d):
            xs2, cs2 = list(xs), list(cs)
            for j, k in enumerate(wrt_rows):
                xs2[k] = d[j]
            for j, k in enumerate(wrt_consts):
                cs2[k] = d[len(wrt_rows) + j]
            return tuple(fn(*xs2, *cs2))

        prim = [xs[k] for k in wrt_rows] + [cs[k] for k in wrt_consts]
        outs, vjp = jax.vjp(g, *prim)
        ct = []
        pos = 0
        for o, lst in zip(outs, cots):
            if unit_cot:
                ct.append(jnp.ones_like(o))
                continue
            acc = jnp.zeros_like(o)
            for _ in lst:
                acc = acc + cvals[pos]
                pos += 1
            ct.append(acc)
        grads = vjp(tuple(ct))
        for j in range(len(wrt_rows)):
            orefs[j][...] = grads[j].astype(orefs[j].dtype)

        @pl.when(pl.program_id(0) == 0)
        def _():
            for j in range(len(wrt_consts)):
                oref = orefs[len(wrt_rows) + j]
                oref[...] = jnp.zeros_like(oref)

        for j in range(len(wrt_consts)):
            orefs[len(wrt_rows) + j][...] += grads[len(wrt_rows) + j]
        after()

    in_specs = ([_row_spec(br, w, cb) for (_, w, cb) in rows] + [_const_spec(c.shape) for c in consts]
                + [_row_spec(br, w, cb) for (_, w, cb) in flat_cots] + [_HBM] * ncomm)
    out_specs = ([_row_spec(br, rows[k][1], 0) for k in wrt_rows] + [_const_spec(consts[k].shape) for k in wrt_consts]
                 + [_HBM] * ncomm)
    out_shape = ([jax.ShapeDtypeStruct((s, rows[k][1]), dt) for k, dt in zip(wrt_rows, drow_dtypes)]
                 + [jax.ShapeDtypeStruct(consts[k].shape, F32) for k in wrt_consts] + comm.out_shape)
    return _pcall(
        body, name=name, grid=(s // br,), in_specs=in_specs, out_specs=out_specs, out_shape=out_shape,
        scratch_shapes=comm.sems, compiler_params=_cparams(("arbitrary",)),
    )(*[r[0] for r in rows], *consts, *[c[0] for c in flat_cots], *comm.ins)


def _rms(x, w):
    return x * lax.rsqrt(jnp.mean(x * x, axis=-1, keepdims=True) + RMS_EPS) * w


def _softplus(x):
    return jnp.maximum(x, 0.0) + jnp.log(1.0 + jnp.exp(-jnp.abs(x)))


def _f_pre(x, nw, sc, sh):
    return _rms(x, nw) * (1.0 + sc) + sh, x


def _f_pre2(x, o, gt, nw, sc, sh):
    x1 = x + gt * o
    return x1, _rms(x1, nw) * (1.0 + sc) + sh


def _f_fin(x1, f, tgt, gt, nfw):
    y = _rms(x1 + gt * f, nfw)
    return (0.5 * jnp.mean(jnp.square(y - tgt), axis=-1, keepdims=True),)


def _f_comb(o1, o2, o3, l1, l2, l3):
    m = lax.stop_gradient(jnp.maximum(jnp.maximum(l1, l2), l3))
    e1, e2, e3 = jnp.exp(l1 - m), jnp.exp(l2 - m), jnp.exp(l3 - m)
    return ((e1 * o1 + e2 * o2 + e3 * o3) / (e1 + e2 + e3),)


def _f_rwpre(zs, w0, a0, k_k, k_a, wl, e, et):
    r, k, v, zl = zs[:, 0:D], zs[:, D:2 * D], zs[:, 2 * D:3 * D], zs[:, 3 * D:N_RWP]
    lane = lax.broadcasted_iota(jnp.int32, zl.shape, 1)
    t = jnp.where(lane < 64, jnp.tanh(zl), jnp.where(lane < 128, zl, jnp.where(lane < 288, jax.nn.sigmoid(zl), 0.0)))
    lo = _nn(t[:, 0:128], wl[0:128, 0:2 * D])
    g = _nn(t[:, 128:N_LORA], wl[128:N_LORA, 2 * D:3 * D])
    w_log = -_softplus(-(w0 + lo[:, 0:D])) - 0.5
    lw = -jnp.exp(w_log)
    a = jax.nn.sigmoid(a0 + lo[:, D:2 * D])
    k_mod = k * (1.0 + (a - 1.0) * k_a)
    kk = k * k_k
    kk = kk / jnp.maximum(jnp.sqrt(_hsum(kk * kk, e, et)), 1e-12)
    return r, lw, k_mod, v, -kk, kk * a, g


def _f_rwpost(y, r, v, k_mod, g, lnx_w, lnx_b, r_k, e, et):
    mean = _hsum(y, e, et) * (1.0 / 64)
    yc = y - mean
    var = _hsum(yc * yc, e, et) * (1.0 / 64)
    yn = yc * lax.rsqrt(var + GN_EPS) * lnx_w + lnx_b
    bonus = _hsum(r * k_mod * r_k, e, et) * v
    return ((yn + bonus) * g,)


def _f_mix(gi, ya, yr, bg):
    gate = jax.nn.sigmoid(gi + bg)
    return (gate[:, 0:D] * ya + gate[:, D:2 * D] * yr,)


def _f_adamw(w, g, m, v):
    m = ADAM_B1 * m + (1.0 - ADAM_B1) * g
    v = ADAM_B2 * v + (1.0 - ADAM_B2) * jnp.square(g)
    m_hat = m / (1.0 - ADAM_B1 ** ADAM_STEP)
    v_hat = v / (1.0 - ADAM_B2 ** ADAM_STEP)
    return -ADAM_LR * (m_hat / (jnp.sqrt(v_hat) + ADAM_EPS) + ADAM_WD * w), m, v


def _down(x, k):
    row = lax.broadcasted_iota(jnp.int32, x.shape, 0)
    return jnp.where(row < k, 0.0, pltpu.roll(x, k, 0))


def _up(x, k):
    n = x.shape[0]
    row = lax.broadcasted_iota(jnp.int32, x.shape, 0)
    return jnp.where(row >= n - k, 0.0, pltpu.roll(x, n - k, 0))


def _col_spec(s, w, off=0):
    return pl.BlockSpec((s, w), lambda j: (0, j + off))


def _shift_fwd(z, mu):
    s, n = z.shape

    def body(z_ref, mu_ref, o_ref):
        zz = z_ref[...]
        o_ref[...] = zz + (_down(zz, 1) - zz) * mu_ref[...]

    return _pcall(
        body, name="shift_fwd", grid=(n // 128,), in_specs=[_col_spec(s, 128), _col_spec(1, 128)],
        out_specs=_col_spec(s, 128), out_shape=jax.ShapeDtypeStruct((s, n), F32),
        compiler_params=_cparams(("parallel",)),
    )(z, mu)


def _shift_bwd(z, mu, dzs):
    s, n = z.shape

    def body(z_ref, mu_ref, d_ref, dz_ref, dmu_ref):
        zz, d, m = z_ref[...], d_ref[...], mu_ref[...]
        dm = d * m
        dz_ref[...] = (d - dm + _up(dm, 1)).astype(dz_ref.dtype)
        dmu_ref[...] = jnp.sum(d * (_down(zz, 1) - zz), axis=0, keepdims=True)

    return _pcall(
        body, name="shift_bwd", grid=(n // 128,), in_specs=[_col_spec(s, 128), _col_spec(1, 128), _col_spec(s, 128)],
        out_specs=[_col_spec(s, 128), _col_spec(1, 128)],
        out_shape=[jax.ShapeDtypeStruct((s, n), BF16), jax.ShapeDtypeStruct((1, n), F32)],
        compiler_params=_cparams(("parallel",)),
    )(z, mu, dzs)


def _conv3(x, w_ref, b_ref):
    return b_ref[...] + w_ref[0:1, :] * _down(x, 2) + w_ref[1:2, :] * _down(x, 1) + w_ref[2:3, :] * x


def _conv_fwd(u, cw, cb):
    s = u.shape[0]
    nb = D_FF // 128

    def body(ug_ref, uv_ref, wg_ref, wv_ref, bg_ref, bv_ref, o_ref):
        gate = _conv3(ug_ref[...], wg_ref, bg_ref)
        val = _conv3(uv_ref[...], wv_ref, bv_ref)
        o_ref[...] = (gate * jax.nn.sigmoid(gate) * val).astype(o_ref.dtype)

    return _pcall(
        body, name="conv_fwd", grid=(nb,),
        in_specs=[_col_spec(s, 128), _col_spec(s, 128, nb), _col_spec(3, 128), _col_spec(3, 128, nb),
                  _col_spec(1, 128), _col_spec(1, 128, nb)],
        out_specs=_col_spec(s, 128), out_shape=jax.ShapeDtypeStruct((s, D_FF), BF16),
        compiler_params=_cparams(("parallel",)),
    )(u, u, cw, cw, cb, cb)


def _conv_bwd(u, cw, cb, dact):
    s = u.shape[0]
    nb = D_FF // 128

    def half(x, d, w_ref, du_ref, dw_ref, db_ref):
        x1, x2 = _down(x, 1), _down(x, 2)
        du_ref[...] = (w_ref[2:3, :] * d + w_ref[1:2, :] * _up(d, 1) + w_ref[0:1, :] * _up(d, 2)).astype(du_ref.dtype)
        dw_ref[0:1, :] = jnp.sum(d * x2, axis=0, keepdims=True)
        dw_ref[1:2, :] = jnp.sum(d * x1, axis=0, keepdims=True)
        dw_ref[2:3, :] = jnp.sum(d * x, axis=0, keepdims=True)
        db_ref[...] = jnp.sum(d, axis=0, keepdims=True)

    def body(ug_ref, uv_ref, wg_ref, wv_ref, bg_ref, bv_ref, da_ref,
             dug_ref, duv_ref, dwg_ref, dwv_ref, dbg_ref, dbv_ref):
        ug, uv, da = ug_ref[...], uv_ref[...], da_ref[...]
        gate = _conv3(ug, wg_ref, bg_ref)
        val = _conv3(uv, wv_ref, bv_ref)
        sg = jax.nn.sigmoid(gate)
        dgate = da * val * sg * (1.0 + gate * (1.0 - sg))
        dval = da * gate * sg
        half(ug, dgate, wg_ref, dug_ref, dwg_ref, dbg_ref)
        half(uv, dval, wv_ref, duv_ref, dwv_ref, dbv_ref)

    dug, duv, dwg, dwv, dbg, dbv = _pcall(
        body, name="conv_bwd", grid=(nb,),
        in_specs=[_col_spec(s, 128), _col_spec(s, 128, nb), _col_spec(3, 128), _col_spec(3, 128, nb),
                  _col_spec(1, 128), _col_spec(1, 128, nb), _col_spec(s, 128)],
        out_specs=[_col_spec(s, 128), _col_spec(s, 128), _col_spec(3, 128), _col_spec(3, 128),
                   _col_spec(1, 128), _col_spec(1, 128)],
        out_shape=[jax.ShapeDtypeStruct((s, D_FF), BF16), jax.ShapeDtypeStruct((s, D_FF), BF16),
                   jax.ShapeDtypeStruct((3, D_FF), F32), jax.ShapeDtypeStruct((3, D_FF), F32),
                   jax.ShapeDtypeStruct((1, D_FF), F32), jax.ShapeDtypeStruct((1, D_FF), F32)],
        compiler_params=_cparams(("parallel",)),
    )(u, u, cw, cw, cb, cb, dact)
    return (jnp.concatenate([dug, duv], axis=1), jnp.concatenate([dwg, dwv], axis=1),
            jnp.concatenate([dbg, dbv], axis=1))


ATT_BATCH = 4


def _att_batch(q, kp, kc, vp, vc, first):
    ma = lax.broadcasted_iota(jnp.int32, (1, ATT_BLOCK, 128), 2) < 64
    qs = jnp.concatenate([jnp.where(ma, q, 0.0), jnp.where(ma, 0.0, q)], axis=1)
    qi = lax.broadcasted_iota(jnp.int32, (1, 2 * ATT_BLOCK, ATT_BLOCK), 1) & (ATT_BLOCK - 1)
    kj = lax.broadcasted_iota(jnp.int32, (1, 2 * ATT_BLOCK, ATT_BLOCK), 2)
    okp = kj >= qi + jnp.where(first, 2 * ATT_BLOCK, 0)
    okc = kj <= qi
    sp = jnp.where(okp, _bnt(qs, kp) * 0.125, NEG)
    sc = jnp.where(okc, _bnt(qs, kc) * 0.125, NEG)
    m = lax.stop_gradient(jnp.maximum(jnp.max(sp, axis=-1, keepdims=True), jnp.max(sc, axis=-1, keepdims=True)))
    pp, pc = jnp.exp(sp - m), jnp.exp(sc - m)
    den = jnp.sum(pp, axis=-1, keepdims=True) + jnp.sum(pc, axis=-1, keepdims=True)
    o_s = (_bnn(pp, vp) + _bnn(pc, vc)) / den
    l_s = jnp.broadcast_to(m + jnp.log(den), o_s.shape)
    return (jnp.where(ma, o_s[:, :ATT_BLOCK], o_s[:, ATT_BLOCK:]), jnp.where(ma, l_s[:, :ATT_BLOCK], l_s[:, ATT_BLOCK:]))


def _att_pairs_per_step(dil):
    return ATT_BATCH if dil == 1 else 1


def _att_specs(g, dil):
    rows, pp = ATT_BLOCK * dil, _att_pairs_per_step(dil)

    def cur(slot):
        return pl.BlockSpec((rows, 128 * pp), lambda n, p: (n, (g * 3 + slot) * (4 // pp) + p))

    def prev(slot):
        return pl.BlockSpec((rows, 128 * pp), lambda n, p: (jnp.maximum(n - 1, 0), (g * 3 + slot) * (4 // pp) + p))

    return [cur(0), prev(1), cur(1), prev(2), cur(2)]


def _att_out_spec(dil):
    return pl.BlockSpec((ATT_BLOCK * dil, 128 * _att_pairs_per_step(dil)), lambda n, p: (n, p))


def _att_grid(s, dil):
    return (s // (ATT_BLOCK * dil), 4 // _att_pairs_per_step(dil))


def _att_windows(i, dil):
    if dil == 1:
        return [(pl.ds(0, ATT_BLOCK), pl.ds(128 * j, 128)) for j in range(ATT_BATCH)]
    return [(pl.ds(i * ATT_BATCH + j, ATT_BLOCK, stride=dil), pl.ds(0, 128)) for j in range(ATT_BATCH)]


def _att_fwd(att_in, g, dil):
    s = att_in.shape[0]

    def body(q_ref, kp_ref, kc_ref, vp_ref, vc_ref, o_ref, l_ref):
        first = pl.program_id(0) == 0

        def one(i, carry):
            win = _att_windows(i, dil)
            vals = [jnp.stack([ref[w] for w in win]) for ref in (q_ref, kp_ref, kc_ref, vp_ref, vc_ref)]
            o, l = _att_batch(*vals, first)
            for j, w in enumerate(win):
                o_ref[w] = o[j]
                l_ref[w] = l[j]
            return carry

        lax.fori_loop(0, max(1, dil // ATT_BATCH), one, 0)

    return _pcall(
        body, name=f"att_fwd{g}", grid=_att_grid(s, dil), in_specs=_att_specs(g, dil),
        out_specs=[_att_out_spec(dil)] * 2, out_shape=[jax.ShapeDtypeStruct((s, ATT_WIDTH), F32)] * 2,
        compiler_params=_cparams(("parallel", "parallel")),
    )(att_in, att_in, att_in, att_in, att_in)


def _att_bwd(att_in, g, dil, do, dl, acc):
    s = att_in.shape[0]

    def body(q_ref, kp_ref, kc_ref, vp_ref, vc_ref, do_ref, dl_ref, dq_ref, dkp_ref, dkc_ref, dvp_ref, dvc_ref):
        first = pl.program_id(0) == 0

        def one(i, carry):
            win = _att_windows(i, dil)
            vals = [jnp.stack([ref[w] for w in win]) for ref in (q_ref, kp_ref, kc_ref, vp_ref, vc_ref)]
            _, vjp = jax.vjp(lambda *a: _att_batch(*a, first), *vals)
            grads = vjp((jnp.stack([do_ref[w] for w in win]), jnp.stack([dl_ref[w] for w in win])))
            for ref, gr in zip((dq_ref, dkp_ref, dkc_ref, dvp_ref, dvc_ref), grads):
                for j, w in enumerate(win):
                    ref[w] = gr[j]
            return carry

        lax.fori_loop(0, max(1, dil // ATT_BATCH), one, 0)

    dq, dkp, dkc, dvp, dvc = _pcall(
        body, name=f"att_bwd{g}", grid=_att_grid(s, dil), in_specs=_att_specs(g, dil) + [_att_out_spec(dil)] * 2,
        out_specs=[_att_out_spec(dil)] * 5, out_shape=[jax.ShapeDtypeStruct((s, ATT_WIDTH), F32)] * 5,
        compiler_params=_cparams(("parallel", "parallel")),
    )(att_in, att_in, att_in, att_in, att_in, do, dl)

    unit, rb = ATT_BLOCK * dil, 1024
    steps = s // rb
    within = unit < rb

    def shifted(cur_ref, next_ref, has_next):
        nxt = jnp.where(has_next, next_ref[...], 0.0)
        return jnp.concatenate([cur_ref[unit:, :], nxt], axis=0) if within else nxt

    def cbody(dq_ref, dkc_ref, dkp_ref, dkn_ref, dvc_ref, dvp_ref, dvn_ref, *rest):
        o_ref = rest[-1]
        has_next = pl.program_id(0) + (1 if within else unit // rb) < steps
        o_ref[:, 0:ATT_WIDTH] = dq_ref[...].astype(BF16)
        o_ref[:, ATT_WIDTH:2 * ATT_WIDTH] = (dkc_ref[...] + shifted(dkp_ref, dkn_ref, has_next)).astype(BF16)
        o_ref[:, 2 * ATT_WIDTH:3 * ATT_WIDTH] = (dvc_ref[...] + shifted(dvp_ref, dvn_ref, has_next)).astype(BF16)

    cur = pl.BlockSpec((rb, ATT_WIDTH), lambda i: (i, 0))
    if within:
        nxt = pl.BlockSpec((unit, ATT_WIDTH), lambda i: (jnp.minimum((i + 1) * (rb // unit), s // unit - 1), 0))
    else:
        nxt = pl.BlockSpec((rb, ATT_WIDTH), lambda i: (jnp.minimum(i + unit // rb, steps - 1), 0))
    carried = [] if acc is None else [acc]
    return _pcall(
        cbody, name=f"att_bwd_sum{g}", grid=(steps,),
        in_specs=[cur, cur, cur, nxt, cur, cur, nxt] + [pl.BlockSpec(memory_space=pl.ANY)] * len(carried),
        out_specs=pl.BlockSpec((rb, 3 * ATT_WIDTH), lambda i: (i, g)),
        out_shape=jax.ShapeDtypeStruct((s, N_ATT), BF16), input_output_aliases={7: 0} if carried else {},
        compiler_params=_cparams(("parallel",)),
    )(dq, dkc, dkp, dkp, dvc, dvp, dvp, *carried)


def _unit_lower_inverse_impl(n):
    eye = (lax.broadcasted_iota(jnp.int32, (1,) + n.shape[1:], 1)
           == lax.broadcasted_iota(jnp.int32, (1,) + n.shape[1:], 2))
    t = jnp.where(eye, 1.0, 0.0) + n
    pw = n
    for _ in range(5):
        pw = _bnn(pw, pw)
        t = t + _bnn(t, pw)
    return t


@jax.custom_vjp
def _unit_lower_inverse(n):
    return _unit_lower_inverse_impl(n)


def _unit_lower_inverse_fwd(n):
    t = _unit_lower_inverse_impl(n)
    return t, t


_unit_lower_inverse.defvjp(_unit_lower_inverse_fwd, lambda t, g: (_bnt(_btn(t, g), t),))


def _scan_chunk(r, lw, k, v, a, b, s0):
    c = SCAN_CHUNK
    p = s0.shape[0]
    ri = lax.broadcasted_iota(jnp.int32, (c, c), 0)
    ci = lax.broadcasted_iota(jnp.int32, (c, c), 1)
    cum = jnp.dot((ci <= ri).astype(F32), lw, precision=HI, preferred_element_type=F32)
    tot = jnp.sum(lw, axis=0, keepdims=True)
    ma = (lax.broadcasted_iota(jnp.int32, (c, 128 * p), 1) & 127) < 64

    def pairs(x):
        return jnp.concatenate([x[None, :, 128 * j:128 * (j + 1)] for j in range(p)], axis=0)

    def stack(x):
        return jnp.concatenate([pairs(jnp.where(ma, x, 0.0)), pairs(jnp.where(ma, 0.0, x))], axis=1)

    einv, eend = jnp.exp(-cum), jnp.exp(tot - cum)
    ra, aa = stack(r * jnp.exp(cum)), stack(a * jnp.exp(cum - lw))
    bi, ki, be, ke, vs = stack(b * einv), stack(k * einv), stack(b * eend), stack(k * eend), stack(v)
    r2 = lax.broadcasted_iota(jnp.int32, (1, 2 * c, 2 * c), 1)
    c2 = lax.broadcasted_iota(jnp.int32, (1, 2 * c, 2 * c), 2)
    same = (r2 >= c) == (c2 >= c)
    strict = jnp.logical_and(same, c2 < r2)
    incl = jnp.logical_and(same, c2 <= r2)
    s0 = jnp.where(same, s0, 0.0)
    prod = _bnt(jnp.concatenate([aa, ra], axis=1), jnp.concatenate([bi, ki], axis=1))
    a_ab = jnp.where(strict, prod[:, :2 * c, :2 * c], 0.0)
    a_ak = jnp.where(strict, prod[:, :2 * c, 2 * c:], 0.0)
    a_rb = jnp.where(incl, prod[:, 2 * c:, :2 * c], 0.0)
    a_rk = jnp.where(incl, prod[:, 2 * c:, 2 * c:], 0.0)
    t = _unit_lower_inverse(a_ab)
    u = _bnn(t, _bnt(aa, s0) + _bnn(a_ak, vs))
    uv = jnp.concatenate([u, vs], axis=1)
    ys = _bnt(ra, s0) + _bnn(jnp.concatenate([a_rb, a_rk], axis=2), uv)
    s1 = s0 * pairs(jnp.exp(tot)) + _btn(uv, jnp.concatenate([be, ke], axis=1))
    y3 = ys[:, :c] + ys[:, c:]
    return jnp.concatenate([y3[j] for j in range(p)], axis=1), s1


def _scan_specs(rev, n):
    def at(i):
        return n - 1 - i if rev else i

    def cm(cb):
        return pl.BlockSpec((SCAN_CHUNK, D), lambda i: (at(i), cb))

    return cm, pl.BlockSpec((1, SCAN_PAIRS, 128, 128), lambda i: (at(i), 0, 0, 0))


def _comm_phases(comm, refs, n):
    k = comm.n
    srcs, outs, sems = refs[:k], refs[k:2 * k], refs[2 * k:]
    i = pl.program_id(0)

    def before():
        @pl.when(i == 0)
        def _():
            comm.first(srcs, outs, sems)

    def after():
        if comm.mid is not None:
            @pl.when(i == (3 * n) // 4)
            def _():
                comm.mid(srcs, outs, sems)

        @pl.when(i == n - 1)
        def _():
            comm.last(srcs, outs, sems)

    return before, after


def _scan_fwd(zs, lw, km, aa, bb, comm):
    s = zs.shape[0]
    n = s // SCAN_CHUNK
    cm, st = _scan_specs(False, n)
    k = comm.n

    def body(*refs):
        r_ref, lw_ref, k_ref, v_ref, a_ref, b_ref = refs[:6]
        y_ref, s0_ref = refs[6 + k:8 + k]
        state = refs[8 + 2 * k]
        before, after = _comm_phases(comm, refs[6:6 + k] + refs[8 + k:8 + 2 * k] + refs[9 + 2 * k:], n)
        before()

        @pl.when(pl.program_id(0) == 0)
        def _():
            state[...] = jnp.zeros_like(state)

        s0 = state[...]
        s0_ref[0] = s0
        y, s1 = _scan_chunk(*[ref[...] for ref in (r_ref, lw_ref, k_ref, v_ref, a_ref, b_ref)], s0)
        y_ref[...] = y
        state[...] = s1
        after()

    res = _pcall(
        body, name="scan_fwd", grid=(n,), in_specs=[cm(0), cm(0), cm(0), cm(2), cm(0), cm(0)] + [_HBM] * k,
        out_specs=[cm(0), st] + [_HBM] * k,
        out_shape=[jax.ShapeDtypeStruct((s, D), F32), jax.ShapeDtypeStruct((n, 8, 128, 128), F32)] + comm.out_shape,
        scratch_shapes=[pltpu.VMEM((SCAN_PAIRS, 128, 128), F32)] + comm.sems,
        compiler_params=_cparams(("arbitrary",)),
    )(zs, lw, km, zs, aa, bb, *comm.ins)
    return res[0], res[1], res[2:]


def _scan_bwd(zs, lw, km, aa, bb, s0s, dy, comm):
    s = zs.shape[0]
    n = s // SCAN_CHUNK
    cm, st = _scan_specs(True, n)
    k = comm.n

    def body(*refs):
        r_ref, lw_ref, k_ref, v_ref, a_ref, b_ref, s0_ref, dy_ref = refs[:8]
        douts = refs[8 + k:14 + k]
        dstate = refs[14 + 2 * k]
        before, after = _comm_phases(comm, refs[8:8 + k] + refs[14 + k:14 + 2 * k] + refs[15 + 2 * k:], n)
        before()

        @pl.when(pl.program_id(0) == 0)
        def _():
            dstate[...] = jnp.zeros_like(dstate)

        prim = [ref[...] for ref in (r_ref, lw_ref, k_ref, v_ref, a_ref, b_ref)] + [s0_ref[0]]
        _, vjp = jax.vjp(_scan_chunk, *prim)
        grads = vjp((dy_ref[...], dstate[...]))
        for ref, gr in zip(douts, grads[:6]):
            ref[...] = gr
        dstate[...] = grads[6]
        after()

    res = _pcall(
        body, name="scan_bwd", grid=(n,),
        in_specs=[cm(0), cm(0), cm(0), cm(2), cm(0), cm(0), st, cm(0)] + [_HBM] * k,
        out_specs=[cm(0)] * 6 + [_HBM] * k, out_shape=[jax.ShapeDtypeStruct((s, D), F32)] * 6 + comm.out_shape,
        scratch_shapes=[pltpu.VMEM((SCAN_PAIRS, 128, 128), F32)] + comm.sems,
        compiler_params=_cparams(("arbitrary",)),
    )(zs, lw, km, zs, aa, bb, s0s, dy, *comm.ins)
    return res[:6], res[6:]


_HBM = pl.BlockSpec(memory_space=pltpu.HBM)


def _me():
    return lax.axis_index("x"), lax.axis_index("y"), lax.axis_index("c")


def _allgather8(src, name):
    def body(src_ref, out_ref, ssem, rsem, lsem):
        x, y, c = _me()
        me = 4 * x + 2 * y + c
        local = pltpu.make_async_copy(src_ref, out_ref.at[me], lsem)
        local.start()
        peers = []
        for k in range(1, 8):
            peers.append(((1 - x) if k & 4 else x, (1 - y) if k & 2 else y, (1 - c) if k & 1 else c))
        sends = []
        for k, peer in enumerate(peers):
            cp = pltpu.make_async_remote_copy(src_ref, out_ref.at[me], ssem.at[k], rsem.at[k], device_id=peer,
                                              device_id_type=MESH)
            cp.start()
            sends.append(cp)
        for k, (px, py, pc) in enumerate(peers):
            pltpu.make_async_remote_copy(src_ref, out_ref.at[4 * px + 2 * py + pc], ssem.at[k], rsem.at[k],
                                         device_id=(px, py, pc), device_id_type=MESH).wait_recv()
        for cp in sends:
            cp.wait_send()
        local.wait()

    return _pcall(
        body, name=name, in_specs=[_HBM], out_specs=_HBM, out_shape=jax.ShapeDtypeStruct((8,) + src.shape, src.dtype),
        scratch_shapes=[pltpu.SemaphoreType.DMA((7,)), pltpu.SemaphoreType.DMA((7,)), pltpu.SemaphoreType.DMA],
    )(src)


def _other_chips(x, y):
    return [(1 - x, y), (x, 1 - y), (1 - x, 1 - y)]


def _remote(src, dst, ssem, rsem, to):
    return pltpu.make_async_remote_copy(src, dst, ssem, rsem, device_id=to, device_id_type=MESH)


class _GatherWeights:
    def __init__(self, shards):
        self.ins = list(shards)
        n = self.n = len(shards)
        self.out_shape = [jax.ShapeDtypeStruct((4,) + t.shape, t.dtype) for t in shards]
        self.sems = [pltpu.SemaphoreType.DMA((6 * n,)), pltpu.SemaphoreType.DMA((6 * n,)),
                     pltpu.SemaphoreType.DMA((n,)), pltpu.SemaphoreType.DMA((n,))]

    def _copies(self, srcs, outs, sems):
        ssem, rsem, lsem, osem = sems
        x, y, c = _me()
        me = 2 * x + y
        own, ici, landed, passed, passed_in = [], [], [], [], []
        for a in range(self.n):
            h = self.ins[a].shape[0] // 2
            mine, other = pl.ds(c * h, h), pl.ds((1 - c) * h, h)
            own.append(_remote(srcs[a], outs[a].at[me], lsem.at[a], osem.at[a], (x, y, 1 - c)))
            for k, (px, py) in enumerate(_other_chips(x, y)):
                s1, r1, s2, r2 = ssem.at[6 * a + k], rsem.at[6 * a + k], ssem.at[6 * a + 3 + k], rsem.at[6 * a + 3 + k]
                got, got_sib = outs[a].at[2 * px + py, mine], outs[a].at[2 * px + py, other]
                ici.append(_remote(srcs[a].at[mine], outs[a].at[me, mine], s1, r1, (px, py, c)))
                landed.append(_remote(got, got, s1, r1, (px, py, c)))
                passed.append(_remote(got, got, s2, r2, (x, y, 1 - c)))
                passed_in.append(_remote(got_sib, got_sib, s2, r2, (x, y, 1 - c)))
        return own, ici, landed, passed, passed_in

    def first(self, srcs, outs, sems):
        own, ici, _, _, _ = self._copies(srcs, outs, sems)
        for cp in own + ici:
            cp.start()

    def mid(self, srcs, outs, sems):
        _, _, landed, passed, _ = self._copies(srcs, outs, sems)
        for arrived, onward in zip(landed, passed):
            arrived.wait_recv()
            onward.start()

    def last(self, srcs, outs, sems):
        own, ici, _, passed, passed_in = self._copies(srcs, outs, sems)
        for cp in passed_in:
            cp.wait_recv()
        for cp in ici + passed:
            cp.wait_send()
        for cp in own:
            cp.wait()


class _ScatterToChips:
    def __init__(self, parts):
        self.ins = list(parts)
        n = self.n = len(parts)
        self.out_shape = [jax.ShapeDtypeStruct(t.shape, t.dtype) for t in parts]
        self.sems = [pltpu.SemaphoreType.DMA((3 * n,)), pltpu.SemaphoreType.DMA((3 * n,)), pltpu.SemaphoreType.DMA((n,))]

    def _copies(self, srcs, outs, sems):
        ssem, rsem, lsem = sems
        x, y, c = _me()
        me = 2 * x + y
        own, out, landed = [], [], []
        for a in range(self.n):
            own.append(pltpu.make_async_copy(srcs[a].at[me], outs[a].at[me], lsem.at[a]))
            for k, (px, py) in enumerate(_other_chips(x, y)):
                dst = outs[a].at[2 * px + py]
                out.append(_remote(srcs[a].at[2 * px + py], outs[a].at[me], ssem.at[3 * a + k], rsem.at[3 * a + k],
                                   (px, py, c)))
                landed.append(_remote(dst, dst, ssem.at[3 * a + k], rsem.at[3 * a + k], (px, py, c)))
        return own, out, landed

    def first(self, srcs, outs, sems):
        own, out, _ = self._copies(srcs, outs, sems)
        for cp in own + out:
            cp.start()

    mid = None

    def last(self, srcs, outs, sems):
        own, out, landed = self._copies(srcs, outs, sems)
        for cp in landed:
            cp.wait_recv()
        for cp in own:
            cp.wait()
        for cp in out:
            cp.wait_send()


def _run_comm(comm, name):
    n = comm.n

    def body(*refs):
        srcs, outs, sems = refs[:n], refs[n:2 * n], refs[2 * n:]
        comm.first(srcs, outs, sems)
        if comm.mid is not None:
            comm.mid(srcs, outs, sems)
        comm.last(srcs, outs, sems)

    return _pcall(body, name=name, in_specs=[_HBM] * n, out_specs=[_HBM] * n, out_shape=comm.out_shape,
                  scratch_shapes=comm.sems)(*comm.ins)


class _NoComm:
    n, ins, out_shape, sems, mid = 0, [], [], [], None

    def first(self, srcs, outs, sems):
        pass

    def last(self, srcs, outs, sems):
        pass


_NOTHING = _NoComm()


class _SiblingHalves:
    mid = None

    def __init__(self, grads):
        self.ins = list(grads)
        n = self.n = len(grads)
        self.out_shape = [jax.ShapeDtypeStruct((4, t.shape[1] // 2, t.shape[2]), t.dtype) for t in grads]
        self.sems = [pltpu.SemaphoreType.DMA((n,)), pltpu.SemaphoreType.DMA((n,))]

    def _copies(self, srcs, outs, sems):
        ssem, rsem = sems
        x, y, c = _me()
        copies = []
        for a in range(self.n):
            h = self.ins[a].shape[1] // 2
            copies.append(_remote(srcs[a].at[:, pl.ds((1 - c) * h, h)], outs[a], ssem.at[a], rsem.at[a], (x, y, 1 - c)))
        return copies

    def first(self, srcs, outs, sems):
        for cp in self._copies(srcs, outs, sems):
            cp.start()

    def last(self, srcs, outs, sems):
        for cp in self._copies(srcs, outs, sems):
            cp.wait()


def _reduce_finish(reds, name):
    n = len(reds)

    def body(*refs):
        outs = refs[n:2 * n]
        ssem, rsem = refs[2 * n:]
        x, y, c = _me()
        copies = []
        for a in range(n):
            h = reds[a].shape[0] // 2
            mine = outs[a].at[pl.ds(c * h, h)]
            copies.append(_remote(mine, mine, ssem.at[a], rsem.at[a], (x, y, 1 - c)))
        for cp in copies:
            cp.start()
        for a in range(n):
            h = reds[a].shape[0] // 2
            dst = outs[a].at[pl.ds((1 - c) * h, h)]
            _remote(dst, dst, ssem.at[a], rsem.at[a], (x, y, 1 - c)).wait_recv()
        for cp in copies:
            cp.wait_send()

    return _pcall(
        body, name=name, in_specs=[_HBM] * n, out_specs=[_HBM] * n,
        out_shape=[jax.ShapeDtypeStruct(t.shape, t.dtype) for t in reds],
        input_output_aliases={a: a for a in range(n)},
        scratch_shapes=[pltpu.SemaphoreType.DMA((n,)), pltpu.SemaphoreType.DMA((n,))],
    )(*reds)


def _half_sum(fn, full, halves, out_full, out_dtype, core, name):
    p, h, c = (halves[0].shape if halves else (full[0].shape[0], full[0].shape[1] // 2, full[0].shape[2]))
    br = _div(h, max(16, (1 << 19) // (p * c)), 16)
    nb = h // br
    mine3 = pl.BlockSpec((p, br, c), lambda i, core_ref: (0, core_ref[0] * nb + i, 0))
    half3 = pl.BlockSpec((p, br, c), lambda i, core_ref: (0, i, 0))

    def body(core_ref, *refs):
        refs[-1][...] = fn(*[t[...].astype(F32) for t in refs[:-1]]).astype(out_dtype)

    if out_full:
        out_spec = pl.BlockSpec((br, c), lambda i, core_ref: (core_ref[0] * nb + i, 0))
        out_shape = jax.ShapeDtypeStruct((2 * h, c), out_dtype)
    else:
        out_spec, out_shape = half3, jax.ShapeDtypeStruct((p, h, c), out_dtype)
    return _pcall(
        body, name=name,
        grid_spec=pltpu.PrefetchScalarGridSpec(
            num_scalar_prefetch=1, grid=(nb,), in_specs=[mine3] * len(full) + [half3] * len(halves),
            out_specs=out_spec),
        out_shape=out_shape, compiler_params=_cparams(("parallel",)),
    )(core, *full, *halves)


def _ada_fwd(c_all, w, b):
    def body(c_ref, w_ref, b_ref, o_ref):
        o_ref[...] = jnp.dot(c_ref[...], w_ref[...], precision=HI, preferred_element_type=F32) + b_ref[...]

    return _pcall(body, name="ada_fwd", out_shape=jax.ShapeDtypeStruct((c_all.shape[0], w.shape[1]), F32),
                  compiler_params=pltpu.CompilerParams(vmem_limit_bytes=VMEM_LIMIT))(c_all, w, b)


def _ada_bwd(c_all_t, d):
    def body(c_ref, d_ref, o_ref):
        o_ref[...] = jnp.dot(c_ref[...], d_ref[...], precision=HI, preferred_element_type=F32)

    return _pcall(body, name="ada_bwd", out_shape=jax.ShapeDtypeStruct((c_all_t.shape[0], d.shape[1]), F32),
                  compiler_params=pltpu.CompilerParams(vmem_limit_bytes=VMEM_LIMIT))(c_all_t, d)


def _sum_lead(x, name):
    p, r, n = x.shape
    br = _div(r, 512, 8)

    def body(x_ref, o_ref):
        acc = x_ref[0]
        for j in range(1, p):
            acc = acc + x_ref[j]
        o_ref[...] = acc

    return _pcall(
        body, name=name, grid=(r // br,), in_specs=[pl.BlockSpec((p, br, n), lambda i: (0, i, 0))],
        out_specs=pl.BlockSpec((br, n), lambda i: (i, 0)), out_shape=jax.ShapeDtypeStruct((r, n), F32),
        compiler_params=_cparams(("parallel",)),
    )(x)


def _adamw(w, g, m, v, name):
    shape = w.shape
    cols = shape[-1]
    w2, g2, m2, v2 = [t.reshape(-1, cols) for t in (w, g, m, v)]
    rows = w2.shape[0]
    br = _div(rows, max(8, (1 << 19) // cols // 8 * 8), 8)
    outs = _rows_fwd(_f_adamw, [(t, cols, 0) for t in (w2, g2, m2, v2)], [], [(cols, F32)] * 3, name=name, br=br)
    return [o.reshape(shape) for o in outs]


_BIG = (("w_in", 1), ("w_up", 1), ("w_down", 0), ("w_o", 0), ("w_rwkv_out", 0), ("w_att_out", 1), ("w2", 1), ("a2", 1),
        ("g2", 1))


_NEEDED_FIRST = ("w_in", "w_att_out", "w2", "a2", "g2")
_NEEDED_LATER = ("w_up", "w_down", "w_o", "w_rwkv_out")
_DONE_EARLY = ("w_up", "w_down", "w_o", "w_rwkv_out", "w_att_out")
_DONE_LATE = ("w_in", "w2", "a2", "g2")


def _cols_joined(t):
    return jnp.concatenate([t[j] for j in range(4)], axis=1)


def _cols_split(t):
    n = t.shape[1] // 4
    return jnp.stack([t[:, j * n:(j + 1) * n] for j in range(4)])


def _col_window(parts, lo, hi):
    out, pos = [], 0
    for t, w in parts:
        a, b = max(lo, pos), min(hi, pos + w)
        if a < b:
            out.append(t[:, a - pos:b - pos])
        pos += w
    return out[0] if len(out) == 1 else jnp.concatenate(out, axis=1)


def _rows_joined(t):
    return t.reshape(4 * t.shape[1], t.shape[2])


def _rows_split(t):
    return t.reshape(4, t.shape[0] // 4, t.shape[1])


def _step_to_scan(x, tgt, ada, wts):
    sh1, sc1, gt1, sh2, sc2, gt2 = ada
    br = 256
    grp = lax.broadcasted_iota(jnp.int32, (D, 128), 0) // 64 == lax.broadcasted_iota(jnp.int32, (D, 128), 1)
    e = grp.astype(F32)
    et = e.T
    w_in = [(wts["w_in"][j], wts["w_in"].shape[2]) for j in range(4)]
    w_att = _col_window(w_in, 0, N_ATT)
    w_rw = jnp.pad(_col_window(w_in, N_ATT, N_ATT + N_RW), ((0, 0), (0, N_RWP - N_RW)))
    w_gate = _col_window(w_in, N_ATT + N_RW, N_ATT + N_RW + N_GATE)
    mu = jnp.pad(wts["mu_shift"], ((0, 0), (0, N_RWP - N_RW)))
    wl = jnp.zeros((N_LORA, 3 * D), F32)
    wl = wl.at[0:64, 0:D].set(_cols_joined(wts["w2"]).astype(F32))
    wl = wl.at[64:128, D:2 * D].set(_cols_joined(wts["a2"]).astype(F32))
    wl = wl.at[128:288, 2 * D:3 * D].set(_cols_joined(wts["g2"]).astype(F32))
    pre1_c = [wts["norm1_w"], sc1, sh1]
    (h1,) = _rows_fwd(_f_pre, [(x, D, 0)], pre1_c, [(D, BF16), None], name="pre1_fwd", br=br)
    att_in = _mm(h1, w_att, name="mm_att_in")
    z = _mm(h1, w_rw, name="mm_rw_in")
    gate_in = _mm(h1, w_gate, name="mm_gate_in")
    att_o, att_l = [], []
    for g, (_, dil) in enumerate(ATT_PATTERNS):
        o, l = _att_fwd(att_in, g, dil)
        att_o.append(o)
        att_l.append(l)
    comb_rows = [(t, ATT_WIDTH, 0) for t in att_o + att_l]
    (att,) = _rows_fwd(_f_comb, comb_rows, [], [(ATT_WIDTH, BF16)], name="comb_fwd", br=br)
    y_att = _mm(att, wts["w_att_out"], b_chip=True, name="mm_att_out")
    zs = _shift_fwd(z, mu)
    rwpre_c = [wts["w0"], wts["a0"], wts["k_k"], wts["k_a"], wl, e, et]
    lw, km, aa, bb, gg = _rows_fwd(_f_rwpre, [(zs, N_RWP, 0)], rwpre_c,
                                   [None, (D, F32), (D, F32), None, (D, F32), (D, F32), (D, F32)],
                                   name="rwpre_fwd", br=br)
    return dict(x=x, tgt=tgt, wts=wts, br=br, e=e, et=et, gt1=gt1, sc2=sc2, sh2=sh2, gt2=gt2, w_att=w_att, w_rw=w_rw,
                w_gate=w_gate, mu=mu, pre1_c=pre1_c, h1=h1, att_in=att_in, z=z, gate_in=gate_in, comb_rows=comb_rows,
                att=att, y_att=y_att, zs=zs, rwpre_c=rwpre_c, lw=lw, km=km, aa=aa, bb=bb, gg=gg)


def _step_between_scans(st, y_raw, late):
    x, tgt, wts, br, e, et = st["x"], st["tgt"], st["wts"], st["br"], st["e"], st["et"]
    zs, km, gg, gate_in, y_att, att = st["zs"], st["km"], st["gg"], st["gate_in"], st["y_att"], st["att"]
    comb_rows, att_in = st["comb_rows"], st["att_in"]
    gt1, sc2, sh2, gt2 = st["gt1"], st["sc2"], st["sh2"], st["gt2"]
    w_up, w_ao = late["w_up"], wts["w_att_out"]
    w_down, w_o, w_ro = _rows_joined(late["w_down"]), _rows_joined(late["w_o"]), _rows_joined(late["w_rwkv_out"])
    post_rows = [(y_raw, D, 0), (zs, D, 0), (zs, D, 2), (km, D, 0), (gg, D, 0)]
    post_c = [wts["lnx_w"], wts["lnx_b"], wts["r_k"], e, et]
    (rw_out,) = _rows_fwd(_f_rwpost, post_rows, post_c, [(D, BF16)], name="rwpost_fwd", br=br)
    y_rw = _mm(rw_out, w_ro, name="mm_rw_out")
    mix_rows = [(gate_in, N_GATE, 0), (y_att, D, 0), (y_rw, D, 0)]
    (mix,) = _rows_fwd(_f_mix, mix_rows, [wts["b_gate"]], [(D, BF16)], name="mix_fwd", br=br)
    o = _mm(mix, w_o, name="mm_o")
    pre2_c = [gt1, wts["norm2_w"], sc2, sh2]
    x1, h2 = _rows_fwd(_f_pre2, [(x, D, 0), (o, D, 0)], pre2_c, [(D, F32), (D, BF16)], name="pre2_fwd", br=br)
    u = _mm(h2, w_up, b_chip=True, name="mm_up")
    act = _conv_fwd(u, wts["conv_w"], wts["conv_b"])
    f = _mm(act, w_down, name="mm_down")
    fin_rows = [(x1, D, 0), (f, D, 0), (tgt, D, 0)]
    fin_c = [gt2, wts["norm_f_w"]]

    def fin_fwd(*a):
        (l,) = _f_fin(*a)
        return (jnp.broadcast_to(jnp.sum(l, axis=0, keepdims=True), (8, 128)),)

    (loss_acc,) = _rows_fwd(fin_fwd, fin_rows, fin_c, [], name="fin_fwd", br=br, acc_shape=(8, 128))

    gw = {}
    dx1a, df, d_gt2, gw["norm_f_w"] = _rows_bwd(
        _f_fin, fin_rows, fin_c, [[]], wrt_rows=[0, 1], wrt_consts=[0, 1], drow_dtypes=[F32, BF16],
        name="fin_bwd", br=br, unit_cot=True)
    dact = _mm(df, w_down, tb=True, name="mm_dact")
    gw["w_down"] = _rows_split(_mm(act, df, ta=True, name="mm_dw_down"))
    du, gw["conv_w"], gw["conv_b"] = _conv_bwd(u, wts["conv_w"], wts["conv_b"], dact)
    dh2 = _mm(du, w_up, tb=True, b_chip=True, name="mm_dh2")
    gw["w_up"] = _mm(h2, du, ta=True, out_chip=True, name="mm_dw_up")
    dxa, do, d_gt1, gw["norm2_w"], d_sc2, d_sh2 = _rows_bwd(
        _f_pre2, [(x, D, 0), (o, D, 0)], pre2_c, [[(dx1a, D, 0)], [(dh2, D, 0)]], wrt_rows=[0, 1],
        wrt_consts=[0, 1, 2, 3], drow_dtypes=[F32, BF16], name="pre2_bwd", br=br)
    dmix = _mm(do, w_o, tb=True, name="mm_dmix")
    gw["w_o"] = _rows_split(_mm(mix, do, ta=True, name="mm_dw_o"))
    dgate, dya, dyr, gw["b_gate"] = _rows_bwd(
        _f_mix, mix_rows, [wts["b_gate"]], [[(dmix, D, 0)]], wrt_rows=[0, 1, 2], wrt_consts=[0],
        drow_dtypes=[BF16] * 3, name="mix_bwd", br=br)
    datt = _mm(dya, w_ao, tb=True, b_chip=True, name="mm_datt")
    gw["w_att_out"] = _mm(att, dya, ta=True, out_chip=True, name="mm_dw_att_out")
    drw = _mm(dyr, w_ro, tb=True, name="mm_drw")
    gw["w_rwkv_out"] = _rows_split(_mm(rw_out, dyr, ta=True, name="mm_dw_rw_out"))
    dcomb = _rows_bwd(_f_comb, comb_rows, [], [[(datt, ATT_WIDTH, 0)]], wrt_rows=list(range(6)), wrt_consts=[],
                      drow_dtypes=[F32] * 6, name="comb_bwd", br=br)
    datt_in = None
    for g, (_, dil) in enumerate(ATT_PATTERNS):
        datt_in = _att_bwd(att_in, g, dil, dcomb[g], dcomb[3 + g], datt_in)
    dy_raw, dr_p, dv_p, dkm_p, dgg, gw["lnx_w"], gw["lnx_b"], gw["r_k"], *recv_early = _rows_bwd(
        _f_rwpost, post_rows, post_c, [[(drw, D, 0)]], wrt_rows=[0, 1, 2, 3, 4], wrt_consts=[0, 1, 2],
        drow_dtypes=[F32] * 5, name="rwpost_bwd", br=br, comm=_SiblingHalves([gw[n] for n in _DONE_EARLY]))
    st.update(loss=loss_acc[0, 0], gw=gw, dxa=dxa, dgate=dgate, datt_in=datt_in,
              dy_raw=dy_raw, dr_p=dr_p, dv_p=dv_p, dkm_p=dkm_p, dgg=dgg, d_ada_late=(d_gt1, d_sh2, d_sc2, d_gt2),
              recv_early=recv_early)
    return st


def _step_after_scan(st, scan_grads):
    x, br, gw, h1, zs = st["x"], st["br"], st["gw"], st["h1"], st["zs"]
    dr_s, dlw, dkm_s, dv_s, daa, dbb = scan_grads
    pre_cots = [[(st["dr_p"], D, 0), (dr_s, D, 0)], [(dlw, D, 0)], [(st["dkm_p"], D, 0), (dkm_s, D, 0)],
                [(st["dv_p"], D, 0), (dv_s, D, 0)], [(daa, D, 0)], [(dbb, D, 0)], [(st["dgg"], D, 0)]]
    dzs, gw["w0"], gw["a0"], gw["k_k"], gw["k_a"], dwl = _rows_bwd(
        _f_rwpre, [(zs, N_RWP, 0)], st["rwpre_c"], pre_cots, wrt_rows=[0], wrt_consts=[0, 1, 2, 3, 4],
        drow_dtypes=[F32], name="rwpre_bwd", br=128)
    gw["w2"], gw["a2"] = _cols_split(dwl[0:64, 0:D]), _cols_split(dwl[64:128, D:2 * D])
    gw["g2"] = _cols_split(dwl[128:288, 2 * D:3 * D])
    dz, dmu = _shift_bwd(st["z"], st["mu"], dzs)
    gw["mu_shift"] = dmu[:, :N_RW]
    datt_in, dgate = st["datt_in"], st["dgate"]
    dh1 = _mm(datt_in, st["w_att"], tb=True, name="mm_dh1_att")
    dh1 = _mm(dgate, st["w_gate"], tb=True, add=dh1, name="mm_dh1_gate")
    dh1 = _mm(dz, st["w_rw"], tb=True, add=dh1, name="mm_dh1_rw")
    dw_in = [(_mm(h1, datt_in, ta=True, name="mm_dw_att"), N_ATT), (_mm(h1, dz, ta=True, name="mm_dw_rw"), N_RW),
             (_mm(h1, dgate, ta=True, name="mm_dw_gate"), N_GATE)]
    shard = (N_ATT + N_RW + N_GATE) // 4
    gw["w_in"] = jnp.stack([_col_window(dw_in, j * shard, (j + 1) * shard) for j in range(4)])
    grad_x, gw["norm1_w"], d_sc1, d_sh1, *recv_late = _rows_bwd(
        _f_pre, [(x, D, 0)], st["pre1_c"], [[(dh1, D, 0)], [(st["dxa"], D, 0)]], wrt_rows=[0], wrt_consts=[0, 1, 2],
        drow_dtypes=[F32], name="pre1_bwd", br=br, comm=_SiblingHalves([gw[n] for n in _DONE_LATE]))
    d_gt1, d_sh2, d_sc2, d_gt2 = st["d_ada_late"]
    return st["loss"], grad_x, (d_sh1, d_sc1, d_gt1, d_sh2, d_sc2, d_gt2), gw, recv_late


_SMALL = ("b_ada", "norm1_w", "b_gate", "mu_shift", "w0", "a0", "k_k", "k_a", "r_k", "lnx_w", "lnx_b", "norm2_w",
          "conv_b", "norm_f_w")
_NAMES = ("w_ada", "b_ada", "norm1_w", "w_in", "b_gate", "mu_shift", "w0", "w2", "a0", "a2", "g2", "k_k", "k_a", "r_k",
          "lnx_w", "lnx_b", "w_att_out", "w_rwkv_out", "w_o", "norm2_w", "w_up", "conv_w", "conv_b", "w_down",
          "norm_f_w")


def kernel(x, c, w_ada, b_ada, norm1_w, w_in, b_gate, mu_shift, w0, w2, a0, a2, g2, k_k, k_a, r_k, lnx_w, lnx_b, w_att_out, w_rwkv_out, w_o, norm2_w, w_up, conv_w, conv_b, w_down, norm_f_w, loss_target, m_w_ada, m_b_ada, m_norm1_w, m_w_in, m_b_gate, m_mu_shift, m_w0, m_w2, m_a0, m_a2, m_g2, m_k_k, m_k_a, m_r_k, m_lnx_w, m_lnx_b, m_w_att_out, m_w_rwkv_out, m_w_o, m_norm2_w, m_w_up, m_conv_w, m_conv_b, m_w_down, m_norm_f_w, v_w_ada, v_b_ada, v_norm1_w, v_w_in, v_b_gate, v_mu_shift, v_w0, v_w2, v_a0, v_a2, v_g2, v_k_k, v_k_a, v_r_k, v_lnx_w, v_lnx_b, v_w_att_out, v_w_rwkv_out, v_w_o, v_norm2_w, v_w_up, v_conv_w, v_conv_b, v_w_down, v_norm_f_w):
    args = dict(locals())
    p, pm, pv = {}, {}, {}
    for name in _NAMES:
        for dst, key in ((p, name), (pm, "m_" + name), (pv, "v_" + name)):
            t = args[key]
            dst[name] = t.reshape(1, -1) if name in ("r_k", "norm_f_w") else t.reshape(t.shape[-2], t.shape[-1])
    xi, yi, ci = _me()
    chip = 2 * xi + yi
    dev = 4 * xi + 2 * yi + ci
    x2, tgt = x[0], loss_target[0]

    n_cw = 3 * (2 * D_FF // 4)
    vec = jnp.concatenate([c.reshape(-1), p["conv_w"].reshape(-1), jnp.zeros((8 * D - D - n_cw,), F32)]).reshape(8, D)
    g0 = _allgather8(vec, "gather_c").reshape(8, 8 * D)
    c_all = g0[:, :D]
    conv_w_full = jnp.concatenate([g0[2 * j, D:D + n_cw].reshape(3, -1) for j in range(4)], axis=1)
    n_ada = 6 * D // 4
    b_ada_sh = lax.dynamic_slice(p["b_ada"], (0, chip * n_ada), (1, n_ada))
    ada_sh = _ada_fwd(c_all, p["w_ada"], b_ada_sh)
    ga = _allgather8(ada_sh, "gather_ada")
    ada_all = jnp.concatenate([ga[2 * j] for j in range(4)], axis=1)
    ada_row = lax.dynamic_slice(ada_all, (dev, 0), (1, 6 * D))
    ada = [ada_row[:, j * D:(j + 1) * D] for j in range(6)]

    big = [n for n, _ in _BIG]
    shard = {n: p[n].astype(BF16) for n in big}
    wts = dict(zip(_NEEDED_FIRST, _run_comm(_GatherWeights([shard[n] for n in _NEEDED_FIRST]), "gather_w")))
    for n in _SMALL:
        wts[n] = p[n]
    wts["conv_w"] = conv_w_full
    core = ci.reshape(1).astype(jnp.int32)

    def chip_parts(gw, names, recv):
        return [_half_sum(lambda a, b: a + b, [gw[n]], [r], False, BF16, core, "reduce_add2_" + n)
                for n, r in zip(names, recv)]

    st = _step_to_scan(x2, tgt, ada, wts)
    y_raw, s0s, late = _scan_fwd(st["zs"], st["lw"], st["km"], st["aa"], st["bb"],
                                 _GatherWeights([shard[n] for n in _NEEDED_LATER]))
    st = _step_between_scans(st, y_raw, dict(zip(_NEEDED_LATER, late)))
    scan_grads, slots_early = _scan_bwd(st["zs"], st["lw"], st["km"], st["aa"], st["bb"], s0s, st["dy_raw"],
                                        _ScatterToChips(chip_parts(st["gw"], _DONE_EARLY, st["recv_early"])))
    loss_part, grad_x, d_ada, gw, recv_late = _step_after_scan(st, scan_grads)

    small = [jnp.concatenate(d_ada, axis=1)] + [gw[n] for n in _SMALL[1:]] + [gw["conv_w"], loss_part.reshape(1, 1)]
    sizes = [t.size for t in small]
    flat = jnp.concatenate([t.reshape(-1) for t in small])
    npad = (-flat.shape[0]) % (8 * D)
    srows = (flat.shape[0] + npad) // D
    flat = jnp.concatenate([flat, jnp.zeros((npad,), F32)]).reshape(srows, D)
    parts = _allgather8(flat, "gather_small")
    tot = _sum_lead(parts, "sum_small").reshape(-1)
    pieces, pos = [], 0
    for sz in sizes:
        pieces.append(tot[pos:pos + sz])
        pos += sz
    grads = {}
    for n, piece in zip(_SMALL, pieces[:len(_SMALL)]):
        grads[n] = piece.reshape(p[n].shape)
    conv_w_grad = pieces[len(_SMALL)].reshape(3, 2 * D_FF)
    grads["conv_w"] = lax.dynamic_slice(conv_w_grad, (0, chip * (n_cw // 3)), (3, n_cw // 3))
    loss = pieces[-1][0]
    d_ada_all = parts[:, :6].reshape(8, 6 * D)
    grads["w_ada"] = _ada_bwd(c_all.T, lax.dynamic_slice(d_ada_all, (0, chip * n_ada), (8, n_ada)))

    slots_late = _run_comm(_ScatterToChips(chip_parts(gw, _DONE_LATE, recv_late)), "reduce_chips")
    order = _DONE_EARLY + _DONE_LATE
    reds = [_half_sum(lambda t: t[0] + t[1] + t[2] + t[3], [], [t], True, F32, core, "reduce_add4_" + n)
            for n, t in zip(order, list(slots_early) + list(slots_late))]
    for n, g in zip(order, _reduce_finish(reds, "reduce_sib2")):
        grads[n] = g

    outs_g, outs_d, outs_m, outs_v = [], [], [], []
    for name in _NAMES:
        g = grads[name]
        d, m, v = _adamw(p[name], g, pm[name], pv[name], "adamw_" + name)
        shape = args[name].shape
        outs_g.append(g.reshape(shape))
        outs_d.append(d.reshape(shape))
        outs_m.append(m.reshape(shape))
        outs_v.append(v.reshape(shape))
    return (loss, grad_x.reshape(x.shape), *outs_g, *outs_d, *outs_m, *outs_v)
```

```python
import functools

import jax
import jax.numpy as jnp
from jax import lax
from jax.experimental import pallas as pl
from jax.experimental.pallas import tpu as pltpu

F32 = jnp.float32
BF16 = jnp.bfloat16
HI = lax.Precision.HIGHEST
MESH = pl.DeviceIdType.MESH

D = 1024
ATT_PATTERNS = ((128, 1), (512, 4), (2048, 16))
ATT_BLOCK = 128
ATT_WIDTH = 512
N_ATT = 3 * 3 * ATT_WIDTH
N_RW = 3 * D + 64 + 64 + 160
N_RWP = 3456
N_LORA = N_RWP - 3 * D
N_GATE = 2 * D
D_FF = 2816
RMS_EPS = 1e-6
GN_EPS = 64e-5
SCAN_CHUNK = 64
SCAN_PAIRS = 8
NEG = -1e30
VMEM_LIMIT = 48 * 1024 * 1024

ADAM_LR, ADAM_B1, ADAM_B2, ADAM_EPS, ADAM_WD, ADAM_STEP = 0.001, 0.9, 0.999, 1e-08, 0.01, 10


def _pcall(body, **kw):
    return pl.pallas_call(body, **kw)


def _cparams(sem):
    return pltpu.CompilerParams(dimension_semantics=sem, vmem_limit_bytes=VMEM_LIMIT)


def _div(n, pref, mult):
    best = None
    d = mult
    while d <= min(n, pref):
        if n % d == 0:
            best = d
        d += mult
    return best if best else n


def _dg(a, b, ca, cb):
    return lax.dot_general(a.astype(BF16), b.astype(BF16), (((ca,), (cb,)), ((), ())), preferred_element_type=F32)


@jax.custom_vjp
def _nn(a, b):
    return _dg(a, b, 1, 0)


@jax.custom_vjp
def _nt(a, b):
    return _dg(a, b, 1, 1)


@jax.custom_vjp
def _tn(a, b):
    return _dg(a, b, 0, 0)


_nn.defvjp(lambda a, b: (_nn(a, b), (a, b)), lambda res, g: (_nt(g, res[1]), _tn(res[0], g)))
_nt.defvjp(lambda a, b: (_nt(a, b), (a, b)), lambda res, g: (_nn(g, res[1]), _tn(g, res[0])))
_tn.defvjp(lambda a, b: (_tn(a, b), (a, b)), lambda res, g: (_nt(res[1], g), _nn(res[0], g)))


def _bdg(a, b, ca, cb):
    return lax.dot_general(a.astype(BF16), b.astype(BF16), (((ca,), (cb,)), ((0,), (0,))), preferred_element_type=F32)


@jax.custom_vjp
def _bnn(a, b):
    return _bdg(a, b, 2, 1)


@jax.custom_vjp
def _bnt(a, b):
    return _bdg(a, b, 2, 2)


@jax.custom_vjp
def _btn(a, b):
    return _bdg(a, b, 1, 1)


_bnn.defvjp(lambda a, b: (_bnn(a, b), (a, b)), lambda res, g: (_bnt(g, res[1]), _btn(res[0], g)))
_bnt.defvjp(lambda a, b: (_bnt(a, b), (a, b)), lambda res, g: (_bnn(g, res[1]), _btn(g, res[0])))
_btn.defvjp(lambda a, b: (_btn(a, b), (a, b)), lambda res, g: (_bnt(res[1], g), _bnn(res[0], g)))


def _split2(x):
    hi = x.astype(BF16)
    lo = (x - hi.astype(F32)).astype(BF16)
    return hi, lo


def _hsum_impl(x, e, et):
    eb, etb = e.astype(BF16), et.astype(BF16)
    s = jnp.dot(x.astype(BF16), eb, preferred_element_type=F32)
    shi, slo = _split2(s)
    return jnp.dot(shi, etb, preferred_element_type=F32) + jnp.dot(slo, etb, preferred_element_type=F32)


@jax.custom_vjp
def _hsum(x, e, et):
    return _hsum_impl(x, e, et)


_hsum.defvjp(lambda x, e, et: (_hsum_impl(x, e, et), (e, et)),
             lambda res, g: (_hsum_impl(g, res[0], res[1]), jnp.zeros_like(res[0]), jnp.zeros_like(res[1])))


def _mm(a, b, *, ta=False, tb=False, out_dtype=F32, add=None, b_chip=False, out_chip=False, comm=None, name):
    riding = _NOTHING if comm is None else comm
    nc = riding.n
    if ta:
        kdim, m = a.shape
    else:
        m, kdim = a.shape
    if b_chip:
        n = b.shape[1] if tb else 4 * b.shape[2]
    else:
        n = b.shape[0] if tb else b.shape[1]
    tm, tn, tk = _div(m, 1536, 128), _div(n, 1536, 128), _div(kdim, 1408, 128)
    if b_chip and tb:
        tk = kdim // 4
    if (b_chip and not tb) or out_chip:
        tn = n // 4
    nk = kdim // tk
    ca, cb = (0 if ta else 1), (1 if tb else 0)

    nin = 2 if add is None else 3
    gi, gj = m // tm, n // tn

    def body(*refs):
        a_ref, b_ref = refs[0], refs[1]
        add_ref = None if add is None else refs[2]
        o_ref = refs[nin + nc]
        step = (pl.program_id(0) * gj + pl.program_id(1)) * nk + pl.program_id(2)
        before, after = _comm_phases(riding, refs[nin:nin + nc] + refs[nin + nc + 1:nin + 2 * nc + 1]
                                     + refs[nin + 2 * nc + 1 + (nk > 1):], gi * gj * nk, step)
        before()
        part = lax.dot_general(a_ref[...], b_ref[...], (((ca,), (cb,)), ((), ())), preferred_element_type=F32)

        def finish(r):
            if add_ref is not None:
                r = r + add_ref[...]
            o_ref[...] = r.astype(o_ref.dtype)

        if nk == 1:
            finish(part)
            after()
            return
        acc = refs[nin + 2 * nc + 1]
        k = pl.program_id(2)

        @pl.when(k == 0)
        def _():
            acc[...] = part

        @pl.when(k > 0)
        def _():
            acc[...] += part

        @pl.when(k == nk - 1)
        def _():
            finish(acc[...])

        after()

    a_spec = pl.BlockSpec((tk, tm), lambda i, j, k: (k, i)) if ta else pl.BlockSpec((tm, tk), lambda i, j, k: (i, k))
    if b_chip:
        b_spec = (pl.BlockSpec((None, tn, tk), lambda i, j, k: (k, j, 0)) if tb
                  else pl.BlockSpec((None, tk, tn), lambda i, j, k: (j, k, 0)))
    else:
        b_spec = pl.BlockSpec((tn, tk), lambda i, j, k: (j, k)) if tb else pl.BlockSpec((tk, tn), lambda i, j, k: (k, j))
    in_specs = [a_spec, b_spec]
    args = [a, b]
    if add is not None:
        in_specs.append(pl.BlockSpec((tm, tn), lambda i, j, k: (i, j)))
        args.append(add)
    if out_chip:
        out_spec = pl.BlockSpec((None, tm, tn), lambda i, j, k: (j, i, 0))
        out_shape = jax.ShapeDtypeStruct((4, m, tn), out_dtype)
    else:
        out_spec = pl.BlockSpec((tm, tn), lambda i, j, k: (i, j))
        out_shape = jax.ShapeDtypeStruct((m, n), out_dtype)
    res = _pcall(
        body, name=name, grid=(gi, gj, nk), in_specs=in_specs + [_HBM] * nc, out_specs=[out_spec] + [_HBM] * nc,
        out_shape=[out_shape] + riding.out_shape,
        scratch_shapes=([] if nk == 1 else [pltpu.VMEM((tm, tn), F32)]) + riding.sems,
        compiler_params=_cparams(("arbitrary",) * 3 if nc else ("parallel", "parallel", "arbitrary")),
    )(*args, *riding.ins)
    return res[0] if comm is None else (res[0], res[1:])


def _mm_nt_sum(pairs, *, comm, name):
    m, n = pairs[0][0].shape[0], pairs[0][1].shape[0]
    tm, tn = _div(m, 1024, 128), _div(n, 1024, 128)
    tks = [_div(a.shape[1], 1408, 128) for a, _ in pairs]
    nks = [a.shape[1] // tk for (a, _), tk in zip(pairs, tks)]
    offs = [sum(nks[:p]) for p in range(len(pairs))]
    total, npair, nc = sum(nks), len(pairs), comm.n
    gi, gj = m // tm, n // tn

    def body(*refs):
        o_ref, acc = refs[2 * npair + nc], refs[2 * npair + 2 * nc + 1]
        k = pl.program_id(2)
        step = (pl.program_id(0) * gj + pl.program_id(1)) * total + k
        before, after = _comm_phases(comm, refs[2 * npair:2 * npair + nc]
                                     + refs[2 * npair + nc + 1:2 * npair + 2 * nc + 1]
                                     + refs[2 * npair + 2 * nc + 2:], gi * gj * total, step)
        before()
        for p in range(npair):
            def partial_product(p=p):
                part = lax.dot_general(refs[2 * p][...], refs[2 * p + 1][...], (((1,), (1,)), ((), ())),
                                       preferred_element_type=F32)
                if p == 0:
                    @pl.when(k == 0)
                    def _():
                        acc[...] = part

                    @pl.when(k > 0)
                    def _():
                        acc[...] += part
                else:
                    acc[...] += part

            pl.when(jnp.logical_and(k >= offs[p], k < offs[p] + nks[p]))(partial_product)

        @pl.when(k == total - 1)
        def _():
            o_ref[...] = acc[...]

        after()

    def spec(rows, tk, off, nk, lead):
        def block(i, j, k):
            return (i if lead == 0 else j, jnp.clip(k - off, 0, nk - 1))
        return pl.BlockSpec((rows, tk), block)

    in_specs, args = [], []
    for (a, b), tk, off, nk in zip(pairs, tks, offs, nks):
        in_specs += [spec(tm, tk, off, nk, 0), spec(tn, tk, off, nk, 1)]
        args += [a, b]
    res = _pcall(
        body, name=name, grid=(gi, gj, total), in_specs=in_specs + [_HBM] * nc,
        out_specs=[pl.BlockSpec((tm, tn), lambda i, j, k: (i, j))] + [_HBM] * nc,
        out_shape=[jax.ShapeDtypeStruct((m, n), F32)] + comm.out_shape,
        scratch_shapes=[pltpu.VMEM((tm, tn), F32)] + comm.sems,
        compiler_params=_cparams(("arbitrary",) * 3),
    )(*args, *comm.ins)
    return res[0], res[1:]


def _row_spec(br, w, cb):
    return pl.BlockSpec((br, w), lambda i: (i, cb))


def _const_spec(shape):
    return pl.BlockSpec(shape, lambda i: (0,) * len(shape))


def _rows_fwd(fn, rows, consts, outs, *, name, br, acc_shape=None):
    s = rows[0][0].shape[0]
    nr, nc = len(rows), len(consts)
    kept = [k for k, o in enumerate(outs) if o is not None]

    def body(*refs):
        xs = [r[...].astype(F32) for r in refs[:nr]]
        cs = [c[...] for c in refs[nr:nr + nc]]
        res = fn(*xs, *cs)
        orefs = refs[nr + nc:]
        for j, k in enumerate(kept):
            orefs[j][...] = res[k].astype(orefs[j].dtype)
        if acc_shape is not None:
            acc_ref = orefs[len(kept)]

            @pl.when(pl.program_id(0) == 0)
            def _():
                acc_ref[...] = jnp.zeros_like(acc_ref)

            acc_ref[...] += res[len(outs)]

    in_specs = [_row_spec(br, w, cb) for (_, w, cb) in rows] + [_const_spec(c.shape) for c in consts]
    out_specs = [_row_spec(br, outs[k][0], 0) for k in kept]
    out_shape = [jax.ShapeDtypeStruct((s, outs[k][0]), outs[k][1]) for k in kept]
    if acc_shape is not None:
        out_specs.append(_const_spec(acc_shape))
        out_shape.append(jax.ShapeDtypeStruct(acc_shape, F32))
    return _pcall(
        body, name=name, grid=(s // br,), in_specs=in_specs, out_specs=out_specs, out_shape=out_shape,
        compiler_params=_cparams(("arbitrary",)),
    )(*[r[0] for r in rows], *consts)


def _rows_bwd(fn, rows, consts, cots, *, wrt_rows, wrt_consts, drow_dtypes, name, br, unit_cot=False, comm=None):
    comm = _NOTHING if comm is None else comm
    ncomm = comm.n
    nout = len(wrt_rows) + len(wrt_consts)
    s = rows[0][0].shape[0]
    nr, nc = len(rows), len(consts)
    flat_cots = [c for lst in cots for c in lst]
    ncot = len(flat_cots)

    def body(*refs):
        xs = [r[...].astype(F32) for r in refs[:nr]]
        cs = [c[...] for c in refs[nr:nr + nc]]
        cvals = [c[...].astype(F32) for c in refs[nr + nc:nr + nc + ncot]]
        orefs = refs[nr + nc + ncot + ncomm:]
        before, after = _comm_phases(comm, refs[nr + nc + ncot:nr + nc + ncot + ncomm] + orefs[nout:], s // br)
        before()

        def g(*d):
            xs2, cs2 = list(xs), list(cs)
            for j, k in enumerate(wrt_rows):
                xs2[k] = d[j]
            for j, k in enumerate(wrt_consts):
                cs2[k] = d[len(wrt_rows) + j]
            return tuple(fn(*xs2, *cs2))

        prim = [xs[k] for k in wrt_rows] + [cs[k] for k in wrt_consts]
        outs, vjp = jax.vjp(g, *prim)
        ct = []
        pos = 0
        for o, lst in zip(outs, cots):
            if unit_cot:
                ct.append(jnp.ones_like(o))
                continue
            acc = jnp.zeros_like(o)
            for _ in lst:
                acc = acc + cvals[pos]
                pos += 1
            ct.append(acc)
        grads = vjp(tuple(ct))
        for j in range(len(wrt_rows)):
            orefs[j][...] = grads[j].astype(orefs[j].dtype)

        @pl.when(pl.program_id(0) == 0)
        def _():
            for j in range(len(wrt_consts)):
                oref = orefs[len(wrt_rows) + j]
                oref[...] = jnp.zeros_like(oref)

        for j in range(len(wrt_consts)):
            orefs[len(wrt_rows) + j][...] += grads[len(wrt_rows) + j]
        after()

    in_specs = ([_row_spec(br, w, cb) for (_, w, cb) in rows] + [_const_spec(c.shape) for c in consts]
                + [_row_spec(br, w, cb) for (_, w, cb) in flat_cots] + [_HBM] * ncomm)
    out_specs = ([_row_spec(br, rows[k][1], 0) for k in wrt_rows] + [_const_spec(consts[k].shape) for k in wrt_consts]
                 + [_HBM] * ncomm)
    out_shape = ([jax.ShapeDtypeStruct((s, rows[k][1]), dt) for k, dt in zip(wrt_rows, drow_dtypes)]
                 + [jax.ShapeDtypeStruct(consts[k].shape, F32) for k in wrt_consts] + comm.out_shape)
    return _pcall(
        body, name=name, grid=(s // br,), in_specs=in_specs, out_specs=out_specs, out_shape=out_shape,
        scratch_shapes=comm.sems, compiler_params=_cparams(("arbitrary",)),
    )(*[r[0] for r in rows], *consts, *[c[0] for c in flat_cots], *comm.ins)


def _rms(x, w):
    return x * lax.rsqrt(jnp.mean(x * x, axis=-1, keepdims=True) + RMS_EPS) * w


def _softplus(x):
    return jnp.maximum(x, 0.0) + jnp.log(1.0 + jnp.exp(-jnp.abs(x)))


def _f_pre(x, nw, sc, sh):
    return _rms(x, nw) * (1.0 + sc) + sh, x


def _f_pre2(x, o, gt, nw, sc, sh):
    x1 = x + gt * o
    return x1, _rms(x1, nw) * (1.0 + sc) + sh


def _f_fin(x1, f, tgt, gt, nfw):
    y = _rms(x1 + gt * f, nfw)
    return (0.5 * jnp.mean(jnp.square(y - tgt), axis=-1, keepdims=True),)


def _f_comb(o1, o2, o3, l1, l2, l3):
    m = lax.stop_gradient(jnp.maximum(jnp.maximum(l1, l2), l3))
    e1, e2, e3 = jnp.exp(l1 - m), jnp.exp(l2 - m), jnp.exp(l3 - m)
    return ((e1 * o1 + e2 * o2 + e3 * o3) / (e1 + e2 + e3),)


def _f_rwpre(zs, w0, a0, k_k, k_a, wl, e, et):
    r, k, v, zl = zs[:, 0:D], zs[:, D:2 * D], zs[:, 2 * D:3 * D], zs[:, 3 * D:N_RWP]
    lane = lax.broadcasted_iota(jnp.int32, zl.shape, 1)
    t = jnp.where(lane < 64, jnp.tanh(zl), jnp.where(lane < 128, zl, jnp.where(lane < 288, jax.nn.sigmoid(zl), 0.0)))
    lo = _nn(t[:, 0:128], wl[0:128, 0:2 * D])
    g = _nn(t[:, 128:N_LORA], wl[128:N_LORA, 2 * D:3 * D])
    w_log = -_softplus(-(w0 + lo[:, 0:D])) - 0.5
    lw = -jnp.exp(w_log)
    a = jax.nn.sigmoid(a0 + lo[:, D:2 * D])
    k_mod = k * (1.0 + (a - 1.0) * k_a)
    kk = k * k_k
    kk = kk / jnp.maximum(jnp.sqrt(_hsum(kk * kk, e, et)), 1e-12)
    return r, lw, k_mod, v, -kk, kk * a, g


def _f_rwpost(y, r, v, k_mod, g, lnx_w, lnx_b, r_k, e, et):
    mean = _hsum(y, e, et) * (1.0 / 64)
    yc = y - mean
    var = _hsum(yc * yc, e, et) * (1.0 / 64)
    yn = yc * lax.rsqrt(var + GN_EPS) * lnx_w + lnx_b
    bonus = _hsum(r * k_mod * r_k, e, et) * v
    return ((yn + bonus) * g,)


def _f_mix(gi, ya, yr, bg):
    gate = jax.nn.sigmoid(gi + bg)
    return (gate[:, 0:D] * ya + gate[:, D:2 * D] * yr,)


def _f_adamw(w, g, m, v):
    m = ADAM_B1 * m + (1.0 - ADAM_B1) * g
    v = ADAM_B2 * v + (1.0 - ADAM_B2) * jnp.square(g)
    m_hat = m / (1.0 - ADAM_B1 ** ADAM_STEP)
    v_hat = v / (1.0 - ADAM_B2 ** ADAM_STEP)
    return -ADAM_LR * (m_hat / (jnp.sqrt(v_hat) + ADAM_EPS) + ADAM_WD * w), m, v


def _down(x, k):
    row = lax.broadcasted_iota(jnp.int32, x.shape, 0)
    return jnp.where(row < k, 0.0, pltpu.roll(x, k, 0))


def _up(x, k):
    n = x.shape[0]
    row = lax.broadcasted_iota(jnp.int32, x.shape, 0)
    return jnp.where(row >= n - k, 0.0, pltpu.roll(x, n - k, 0))


def _col_spec(s, w, off=0):
    return pl.BlockSpec((s, w), lambda j: (0, j + off))


def _shift_fwd(z, mu):
    s, n = z.shape

    def body(z_ref, mu_ref, o_ref):
        zz = z_ref[...]
        o_ref[...] = zz + (_down(zz, 1) - zz) * mu_ref[...]

    return _pcall(
        body, name="shift_fwd", grid=(n // 128,), in_specs=[_col_spec(s, 128), _col_spec(1, 128)],
        out_specs=_col_spec(s, 128), out_shape=jax.ShapeDtypeStruct((s, n), F32),
        compiler_params=_cparams(("parallel",)),
    )(z, mu)


def _shift_bwd(z, mu, dzs):
    s, n = z.shape

    def body(z_ref, mu_ref, d_ref, dz_ref, dmu_ref):
        zz, d, m = z_ref[...], d_ref[...], mu_ref[...]
        dm = d * m
        dz_ref[...] = (d - dm + _up(dm, 1)).astype(dz_ref.dtype)
        dmu_ref[...] = jnp.sum(d * (_down(zz, 1) - zz), axis=0, keepdims=True)

    return _pcall(
        body, name="shift_bwd", grid=(n // 128,), in_specs=[_col_spec(s, 128), _col_spec(1, 128), _col_spec(s, 128)],
        out_specs=[_col_spec(s, 128), _col_spec(1, 128)],
        out_shape=[jax.ShapeDtypeStruct((s, n), BF16), jax.ShapeDtypeStruct((1, n), F32)],
        compiler_params=_cparams(("parallel",)),
    )(z, mu, dzs)


def _conv3(x, w_ref, b_ref):
    return b_ref[...] + w_ref[0:1, :] * _down(x, 2) + w_ref[1:2, :] * _down(x, 1) + w_ref[2:3, :] * x


def _conv_fwd(u, cw, cb):
    s = u.shape[0]
    nb = D_FF // 128

    def body(ug_ref, uv_ref, wg_ref, wv_ref, bg_ref, bv_ref, o_ref):
        gate = _conv3(ug_ref[...], wg_ref, bg_ref)
        val = _conv3(uv_ref[...], wv_ref, bv_ref)
        o_ref[...] = (gate * jax.nn.sigmoid(gate) * val).astype(o_ref.dtype)

    return _pcall(
        body, name="conv_fwd", grid=(nb,),
        in_specs=[_col_spec(s, 128), _col_spec(s, 128, nb), _col_spec(3, 128), _col_spec(3, 128, nb),
                  _col_spec(1, 128), _col_spec(1, 128, nb)],
        out_specs=_col_spec(s, 128), out_shape=jax.ShapeDtypeStruct((s, D_FF), BF16),
        compiler_params=_cparams(("parallel",)),
    )(u, u, cw, cw, cb, cb)


def _conv_bwd(u, cw, cb, dact):
    s = u.shape[0]
    nb = D_FF // 128

    def half(x, d, w_ref, du_ref, dw_ref, db_ref):
        x1, x2 = _down(x, 1), _down(x, 2)
        du_ref[...] = (w_ref[2:3, :] * d + w_ref[1:2, :] * _up(d, 1) + w_ref[0:1, :] * _up(d, 2)).astype(du_ref.dtype)
        dw_ref[0:1, :] = jnp.sum(d * x2, axis=0, keepdims=True)
        dw_ref[1:2, :] = jnp.sum(d * x1, axis=0, keepdims=True)
        dw_ref[2:3, :] = jnp.sum(d * x, axis=0, keepdims=True)
        db_ref[...] = jnp.sum(d, axis=0, keepdims=True)

    def body(ug_ref, uv_ref, wg_ref, wv_ref, bg_ref, bv_ref, da_ref,
             dug_ref, duv_ref, dwg_ref, dwv_ref, dbg_ref, dbv_ref):
        ug, uv, da = ug_ref[...], uv_ref[...], da_ref[...]
        gate = _conv3(ug, wg_ref, bg_ref)
        val = _conv3(uv, wv_ref, bv_ref)
        sg = jax.nn.sigmoid(gate)
        dgate = da * val * sg * (1.0 + gate * (1.0 - sg))
        dval = da * gate * sg
        half(ug, dgate, wg_ref, dug_ref, dwg_ref, dbg_ref)
        half(uv, dval, wv_ref, duv_ref, dwv_ref, dbv_ref)

    dug, duv, dwg, dwv, dbg, dbv = _pcall(
        body, name="conv_bwd", grid=(nb,),
        in_specs=[_col_spec(s, 128), _col_spec(s, 128, nb), _col_spec(3, 128), _col_spec(3, 128, nb),
                  _col_spec(1, 128), _col_spec(1, 128, nb), _col_spec(s, 128)],
        out_specs=[_col_spec(s, 128), _col_spec(s, 128), _col_spec(3, 128), _col_spec(3, 128),
                   _col_spec(1, 128), _col_spec(1, 128)],
        out_shape=[jax.ShapeDtypeStruct((s, D_FF), BF16), jax.ShapeDtypeStruct((s, D_FF), BF16),
                   jax.ShapeDtypeStruct((3, D_FF), F32), jax.ShapeDtypeStruct((3, D_FF), F32),
                   jax.ShapeDtypeStruct((1, D_FF), F32), jax.ShapeDtypeStruct((1, D_FF), F32)],
        compiler_params=_cparams(("parallel",)),
    )(u, u, cw, cw, cb, cb, dact)
    return (jnp.concatenate([dug, duv], axis=1), jnp.concatenate([dwg, dwv], axis=1),
            jnp.concatenate([dbg, dbv], axis=1))


ATT_BATCH = 4


def _att_batch(q, kp, kc, vp, vc, first):
    ma = lax.broadcasted_iota(jnp.int32, (1, ATT_BLOCK, 128), 2) < 64
    qs = jnp.concatenate([jnp.where(ma, q, 0.0), jnp.where(ma, 0.0, q)], axis=1)
    qi = lax.broadcasted_iota(jnp.int32, (1, 2 * ATT_BLOCK, ATT_BLOCK), 1) & (ATT_BLOCK - 1)
    kj = lax.broadcasted_iota(jnp.int32, (1, 2 * ATT_BLOCK, ATT_BLOCK), 2)
    okp = kj >= qi + jnp.where(first, 2 * ATT_BLOCK, 0)
    okc = kj <= qi
    sp = jnp.where(okp, _bnt(qs, kp) * 0.125, NEG)
    sc = jnp.where(okc, _bnt(qs, kc) * 0.125, NEG)
    m = lax.stop_gradient(jnp.maximum(jnp.max(sp, axis=-1, keepdims=True), jnp.max(sc, axis=-1, keepdims=True)))
    pp, pc = jnp.exp(sp - m), jnp.exp(sc - m)
    den = jnp.sum(pp, axis=-1, keepdims=True) + jnp.sum(pc, axis=-1, keepdims=True)
    o_s = (_bnn(pp, vp) + _bnn(pc, vc)) / den
    l_s = jnp.broadcast_to(m + jnp.log(den), o_s.shape)
    return (jnp.where(ma, o_s[:, :ATT_BLOCK], o_s[:, ATT_BLOCK:]), jnp.where(ma, l_s[:, :ATT_BLOCK], l_s[:, ATT_BLOCK:]))


def _att_pairs_per_step(dil):
    return ATT_BATCH if dil == 1 else 1


def _att_specs(g, dil):
    rows, pp = ATT_BLOCK * dil, _att_pairs_per_step(dil)

    def cur(slot):
        return pl.BlockSpec((rows, 128 * pp), lambda n, p: (n, (g * 3 + slot) * (4 // pp) + p))

    def prev(slot):
        return pl.BlockSpec((rows, 128 * pp), lambda n, p: (jnp.maximum(n - 1, 0), (g * 3 + slot) * (4 // pp) + p))

    return [cur(0), prev(1), cur(1), prev(2), cur(2)]


def _att_out_spec(dil):
    return pl.BlockSpec((ATT_BLOCK * dil, 128 * _att_pairs_per_step(dil)), lambda n, p: (n, p))


def _att_grid(s, dil):
    return (s // (ATT_BLOCK * dil), 4 // _att_pairs_per_step(dil))


def _att_windows(i, dil):
    if dil == 1:
        return [(pl.ds(0, ATT_BLOCK), pl.ds(128 * j, 128)) for j in range(ATT_BATCH)]
    return [(pl.ds(i * ATT_BATCH + j, ATT_BLOCK, stride=dil), pl.ds(0, 128)) for j in range(ATT_BATCH)]


def _att_fwd(att_in, g, dil):
    s = att_in.shape[0]

    def body(q_ref, kp_ref, kc_ref, vp_ref, vc_ref, o_ref, l_ref):
        first = pl.program_id(0) == 0

        def one(i, carry):
            win = _att_windows(i, dil)
            vals = [jnp.stack([ref[w] for w in win]) for ref in (q_ref, kp_ref, kc_ref, vp_ref, vc_ref)]
            o, l = _att_batch(*vals, first)
            for j, w in enumerate(win):
                o_ref[w] = o[j]
                l_ref[w] = l[j]
            return carry

        lax.fori_loop(0, max(1, dil // ATT_BATCH), one, 0)

    return _pcall(
        body, name=f"att_fwd{g}", grid=_att_grid(s, dil), in_specs=_att_specs(g, dil),
        out_specs=[_att_out_spec(dil)] * 2, out_shape=[jax.ShapeDtypeStruct((s, ATT_WIDTH), F32)] * 2,
        compiler_params=_cparams(("parallel", "parallel")),
    )(att_in, att_in, att_in, att_in, att_in)


def _att_bwd(att_in, g, dil, do, dl, acc):
    s = att_in.shape[0]

    def body(q_ref, kp_ref, kc_ref, vp_ref, vc_ref, do_ref, dl_ref, dq_ref, dkp_ref, dkc_ref, dvp_ref, dvc_ref):
        first = pl.program_id(0) == 0

        def one(i, carry):
            win = _att_windows(i, dil)
            vals = [jnp.stack([ref[w] for w in win]) for ref in (q_ref, kp_ref, kc_ref, vp_ref, vc_ref)]
            _, vjp = jax.vjp(lambda *a: _att_batch(*a, first), *vals)
            grads = vjp((jnp.stack([do_ref[w] for w in win]), jnp.stack([dl_ref[w] for w in win])))
            for ref, gr in zip((dq_ref, dkp_ref, dkc_ref, dvp_ref, dvc_ref), grads):
                for j, w in enumerate(win):
                    ref[w] = gr[j]
            return carry

        lax.fori_loop(0, max(1, dil // ATT_BATCH), one, 0)

    dq, dkp, dkc, dvp, dvc = _pcall(
        body, name=f"att_bwd{g}", grid=_att_grid(s, dil), in_specs=_att_specs(g, dil) + [_att_out_spec(dil)] * 2,
        out_specs=[_att_out_spec(dil)] * 5, out_shape=[jax.ShapeDtypeStruct((s, ATT_WIDTH), F32)] * 5,
        compiler_params=_cparams(("parallel", "parallel")),
    )(att_in, att_in, att_in, att_in, att_in, do, dl)

    unit, rb = ATT_BLOCK * dil, 1024
    steps = s // rb
    within = unit < rb

    def shifted(cur_ref, next_ref, has_next):
        nxt = jnp.where(has_next, next_ref[...], 0.0)
        return jnp.concatenate([cur_ref[unit:, :], nxt], axis=0) if within else nxt

    def cbody(dq_ref, dkc_ref, dkp_ref, dkn_ref, dvc_ref, dvp_ref, dvn_ref, *rest):
        o_ref = rest[-1]
        has_next = pl.program_id(0) + (1 if within else unit // rb) < steps
        o_ref[:, 0:ATT_WIDTH] = dq_ref[...].astype(BF16)
        o_ref[:, ATT_WIDTH:2 * ATT_WIDTH] = (dkc_ref[...] + shifted(dkp_ref, dkn_ref, has_next)).astype(BF16)
        o_ref[:, 2 * ATT_WIDTH:3 * ATT_WIDTH] = (dvc_ref[...] + shifted(dvp_ref, dvn_ref, has_next)).astype(BF16)

    cur = pl.BlockSpec((rb, ATT_WIDTH), lambda i: (i, 0))
    if within:
        nxt = pl.BlockSpec((unit, ATT_WIDTH), lambda i: (jnp.minimum((i + 1) * (rb // unit), s // unit - 1), 0))
    else:
        nxt = pl.BlockSpec((rb, ATT_WIDTH), lambda i: (jnp.minimum(i + unit // rb, steps - 1), 0))
    carried = [] if acc is None else [acc]
    return _pcall(
        cbody, name=f"att_bwd_sum{g}", grid=(steps,),
        in_specs=[cur, cur, cur, nxt, cur, cur, nxt] + [pl.BlockSpec(memory_space=pl.ANY)] * len(carried),
        out_specs=pl.BlockSpec((rb, 3 * ATT_WIDTH), lambda i: (i, g)),
        out_shape=jax.ShapeDtypeStruct((s, N_ATT), BF16), input_output_aliases={7: 0} if carried else {},
        compiler_params=_cparams(("parallel",)),
    )(dq, dkc, dkp, dkp, dvc, dvp, dvp, *carried)


def _unit_lower_inverse_impl(n):
    eye = (lax.broadcasted_iota(jnp.int32, (1,) + n.shape[1:], 1)
           == lax.broadcasted_iota(jnp.int32, (1,) + n.shape[1:], 2))
    t = jnp.where(eye, 1.0, 0.0) + n
    pw = n
    for _ in range(5):
        pw = _bnn(pw, pw)
        t = t + _bnn(t, pw)
    return t


@jax.custom_vjp
def _unit_lower_inverse(n):
    return _unit_lower_inverse_impl(n)


def _unit_lower_inverse_fwd(n):
    t = _unit_lower_inverse_impl(n)
    return t, t


_unit_lower_inverse.defvjp(_unit_lower_inverse_fwd, lambda t, g: (_bnt(_btn(t, g), t),))


def _scan_chunk(r, lw, k, v, a, b, s0):
    c = SCAN_CHUNK
    p = s0.shape[0]
    ri = lax.broadcasted_iota(jnp.int32, (c, c), 0)
    ci = lax.broadcasted_iota(jnp.int32, (c, c), 1)
    cum = jnp.dot((ci <= ri).astype(F32), lw, precision=HI, preferred_element_type=F32)
    tot = jnp.sum(lw, axis=0, keepdims=True)
    ma = (lax.broadcasted_iota(jnp.int32, (c, 128 * p), 1) & 127) < 64

    def pairs(x):
        return jnp.concatenate([x[None, :, 128 * j:128 * (j + 1)] for j in range(p)], axis=0)

    def stack(x):
        return jnp.concatenate([pairs(jnp.where(ma, x, 0.0)), pairs(jnp.where(ma, 0.0, x))], axis=1)

    einv, eend = jnp.exp(-cum), jnp.exp(tot - cum)
    ra, aa = stack(r * jnp.exp(cum)), stack(a * jnp.exp(cum - lw))
    bi, ki, be, ke, vs = stack(b * einv), stack(k * einv), stack(b * eend), stack(k * eend), stack(v)
    r2 = lax.broadcasted_iota(jnp.int32, (1, 2 * c, 2 * c), 1)
    c2 = lax.broadcasted_iota(jnp.int32, (1, 2 * c, 2 * c), 2)
    same = (r2 >= c) == (c2 >= c)
    strict = jnp.logical_and(same, c2 < r2)
    incl = jnp.logical_and(same, c2 <= r2)
    s0 = jnp.where(same, s0, 0.0)
    prod = _bnt(jnp.concatenate([aa, ra], axis=1), jnp.concatenate([bi, ki], axis=1))
    a_ab = jnp.where(strict, prod[:, :2 * c, :2 * c], 0.0)
    a_ak = jnp.where(strict, prod[:, :2 * c, 2 * c:], 0.0)
    a_rb = jnp.where(incl, prod[:, 2 * c:, :2 * c], 0.0)
    a_rk = jnp.where(incl, prod[:, 2 * c:, 2 * c:], 0.0)
    t = _unit_lower_inverse(a_ab)
    u = _bnn(t, _bnt(aa, s0) + _bnn(a_ak, vs))
    uv = jnp.concatenate([u, vs], axis=1)
    ys = _bnt(ra, s0) + _bnn(jnp.concatenate([a_rb, a_rk], axis=2), uv)
    s1 = s0 * pairs(jnp.exp(tot)) + _btn(uv, jnp.concatenate([be, ke], axis=1))
    y3 = ys[:, :c] + ys[:, c:]
    return jnp.concatenate([y3[j] for j in range(p)], axis=1), s1


def _scan_specs(rev, n):
    def at(i):
        return n - 1 - i if rev else i

    def cm(cb):
        return pl.BlockSpec((SCAN_CHUNK, D), lambda i: (at(i), cb))

    return cm, pl.BlockSpec((1, SCAN_PAIRS, 128, 128), lambda i: (at(i), 0, 0, 0))


def _comm_phases(comm, refs, n, step=None):
    k = comm.n
    srcs, outs, sems = refs[:k], refs[k:2 * k], refs[2 * k:]
    i = pl.program_id(0) if step is None else step

    def before():
        @pl.when(i == 0)
        def _():
            comm.first(srcs, outs, sems)

    def after():
        if comm.mid is not None:
            @pl.when(i == (3 * n) // 4)
            def _():
                comm.mid(srcs, outs, sems)

        @pl.when(i == n - 1)
        def _():
            comm.last(srcs, outs, sems)

    return before, after


def _scan_fwd(zs, lw, km, aa, bb, comm):
    s = zs.shape[0]
    n = s // SCAN_CHUNK
    cm, st = _scan_specs(False, n)
    k = comm.n

    def body(*refs):
        r_ref, lw_ref, k_ref, v_ref, a_ref, b_ref = refs[:6]
        y_ref, s0_ref = refs[6 + k:8 + k]
        state = refs[8 + 2 * k]
        before, after = _comm_phases(comm, refs[6:6 + k] + refs[8 + k:8 + 2 * k] + refs[9 + 2 * k:], n)
        before()

        @pl.when(pl.program_id(0) == 0)
        def _():
            state[...] = jnp.zeros_like(state)

        s0 = state[...]
        s0_ref[0] = s0
        y, s1 = _scan_chunk(*[ref[...] for ref in (r_ref, lw_ref, k_ref, v_ref, a_ref, b_ref)], s0)
        y_ref[...] = y
        state[...] = s1
        after()

    res = _pcall(
        body, name="scan_fwd", grid=(n,), in_specs=[cm(0), cm(0), cm(0), cm(2), cm(0), cm(0)] + [_HBM] * k,
        out_specs=[cm(0), st] + [_HBM] * k,
        out_shape=[jax.ShapeDtypeStruct((s, D), F32), jax.ShapeDtypeStruct((n, 8, 128, 128), F32)] + comm.out_shape,
        scratch_shapes=[pltpu.VMEM((SCAN_PAIRS, 128, 128), F32)] + comm.sems,
        compiler_params=_cparams(("arbitrary",)),
    )(zs, lw, km, zs, aa, bb, *comm.ins)
    return res[0], res[1], res[2:]


def _scan_bwd(zs, lw, km, aa, bb, s0s, dy, comm):
    s = zs.shape[0]
    n = s // SCAN_CHUNK
    cm, st = _scan_specs(True, n)
    k = comm.n

    def body(*refs):
        r_ref, lw_ref, k_ref, v_ref, a_ref, b_ref, s0_ref, dy_ref = refs[:8]
        douts = refs[8 + k:14 + k]
        dstate = refs[14 + 2 * k]
        before, after = _comm_phases(comm, refs[8:8 + k] + refs[14 + k:14 + 2 * k] + refs[15 + 2 * k:], n)
        before()

        @pl.when(pl.program_id(0) == 0)
        def _():
            dstate[...] = jnp.zeros_like(dstate)

        prim = [ref[...] for ref in (r_ref, lw_ref, k_ref, v_ref, a_ref, b_ref)] + [s0_ref[0]]
        _, vjp = jax.vjp(_scan_chunk, *prim)
        grads = vjp((dy_ref[...], dstate[...]))
        for ref, gr in zip(douts, grads[:6]):
            ref[...] = gr
        dstate[...] = grads[6]
        after()

    res = _pcall(
        body, name="scan_bwd", grid=(n,),
        in_specs=[cm(0), cm(0), cm(0), cm(2), cm(0), cm(0), st, cm(0)] + [_HBM] * k,
        out_specs=[cm(0)] * 6 + [_HBM] * k, out_shape=[jax.ShapeDtypeStruct((s, D), F32)] * 6 + comm.out_shape,
        scratch_shapes=[pltpu.VMEM((SCAN_PAIRS, 128, 128), F32)] + comm.sems,
        compiler_params=_cparams(("arbitrary",)),
    )(zs, lw, km, zs, aa, bb, s0s, dy, *comm.ins)
    return res[:6], res[6:]


_HBM = pl.BlockSpec(memory_space=pltpu.HBM)


def _me():
    return lax.axis_index("x"), lax.axis_index("y"), lax.axis_index("c")


def _allgather8(src, name):
    def body(src_ref, out_ref, ssem, rsem, lsem):
        x, y, c = _me()
        me = 4 * x + 2 * y + c
        local = pltpu.make_async_copy(src_ref, out_ref.at[me], lsem)
        local.start()
        peers = []
        for k in range(1, 8):
            peers.append(((1 - x) if k & 4 else x, (1 - y) if k & 2 else y, (1 - c) if k & 1 else c))
        sends = []
        for k, peer in enumerate(peers):
            cp = pltpu.make_async_remote_copy(src_ref, out_ref.at[me], ssem.at[k], rsem.at[k], device_id=peer,
                                              device_id_type=MESH)
            cp.start()
            sends.append(cp)
        for k, (px, py, pc) in enumerate(peers):
            pltpu.make_async_remote_copy(src_ref, out_ref.at[4 * px + 2 * py + pc], ssem.at[k], rsem.at[k],
                                         device_id=(px, py, pc), device_id_type=MESH).wait_recv()
        for cp in sends:
            cp.wait_send()
        local.wait()

    return _pcall(
        body, name=name, in_specs=[_HBM], out_specs=_HBM, out_shape=jax.ShapeDtypeStruct((8,) + src.shape, src.dtype),
        scratch_shapes=[pltpu.SemaphoreType.DMA((7,)), pltpu.SemaphoreType.DMA((7,)), pltpu.SemaphoreType.DMA],
    )(src)


def _other_chips(x, y):
    return [(1 - x, y), (x, 1 - y), (1 - x, 1 - y)]


def _remote(src, dst, ssem, rsem, to):
    return pltpu.make_async_remote_copy(src, dst, ssem, rsem, device_id=to, device_id_type=MESH)


class _GatherWeights:
    def __init__(self, shards):
        self.ins = list(shards)
        n = self.n = len(shards)
        self.out_shape = [jax.ShapeDtypeStruct((4,) + t.shape, t.dtype) for t in shards]
        self.sems = [pltpu.SemaphoreType.DMA((6 * n,)), pltpu.SemaphoreType.DMA((6 * n,)),
                     pltpu.SemaphoreType.DMA((n,)), pltpu.SemaphoreType.DMA((n,))]

    def _copies(self, srcs, outs, sems):
        ssem, rsem, lsem, osem = sems
        x, y, c = _me()
        me = 2 * x + y
        own, ici, landed, passed, passed_in = [], [], [], [], []
        for a in range(self.n):
            h = self.ins[a].shape[0] // 2
            mine, other = pl.ds(c * h, h), pl.ds((1 - c) * h, h)
            own.append(_remote(srcs[a], outs[a].at[me], lsem.at[a], osem.at[a], (x, y, 1 - c)))
            for k, (px, py) in enumerate(_other_chips(x, y)):
                s1, r1, s2, r2 = ssem.at[6 * a + k], rsem.at[6 * a + k], ssem.at[6 * a + 3 + k], rsem.at[6 * a + 3 + k]
                got, got_sib = outs[a].at[2 * px + py, mine], outs[a].at[2 * px + py, other]
                ici.append(_remote(srcs[a].at[mine], outs[a].at[me, mine], s1, r1, (px, py, c)))
                landed.append(_remote(got, got, s1, r1, (px, py, c)))
                passed.append(_remote(got, got, s2, r2, (x, y, 1 - c)))
                passed_in.append(_remote(got_sib, got_sib, s2, r2, (x, y, 1 - c)))
        return own, ici, landed, passed, passed_in

    def first(self, srcs, outs, sems):
        own, ici, _, _, _ = self._copies(srcs, outs, sems)
        for cp in own + ici:
            cp.start()

    def mid(self, srcs, outs, sems):
        _, _, landed, passed, _ = self._copies(srcs, outs, sems)
        for arrived, onward in zip(landed, passed):
            arrived.wait_recv()
            onward.start()

    def last(self, srcs, outs, sems):
        own, ici, _, passed, passed_in = self._copies(srcs, outs, sems)
        for cp in passed_in:
            cp.wait_recv()
        for cp in ici + passed:
            cp.wait_send()
        for cp in own:
            cp.wait()


class _ScatterToChips:
    def __init__(self, parts):
        self.ins = list(parts)
        n = self.n = len(parts)
        self.out_shape = [jax.ShapeDtypeStruct(t.shape, t.dtype) for t in parts]
        self.sems = [pltpu.SemaphoreType.DMA((3 * n,)), pltpu.SemaphoreType.DMA((3 * n,)), pltpu.SemaphoreType.DMA((n,))]

    def _copies(self, srcs, outs, sems):
        ssem, rsem, lsem = sems
        x, y, c = _me()
        me = 2 * x + y
        own, out, landed = [], [], []
        for a in range(self.n):
            own.append(pltpu.make_async_copy(srcs[a].at[me], outs[a].at[me], lsem.at[a]))
            for k, (px, py) in enumerate(_other_chips(x, y)):
                dst = outs[a].at[2 * px + py]
                out.append(_remote(srcs[a].at[2 * px + py], outs[a].at[me], ssem.at[3 * a + k], rsem.at[3 * a + k],
                                   (px, py, c)))
                landed.append(_remote(dst, dst, ssem.at[3 * a + k], rsem.at[3 * a + k], (px, py, c)))
        return own, out, landed

    def first(self, srcs, outs, sems):
        own, out, _ = self._copies(srcs, outs, sems)
        for cp in own + out:
            cp.start()

    mid = None

    def last(self, srcs, outs, sems):
        own, out, landed = self._copies(srcs, outs, sems)
        for cp in landed:
            cp.wait_recv()
        for cp in own:
            cp.wait()
        for cp in out:
            cp.wait_send()


def _run_comm(comm, name):
    n = comm.n

    def body(*refs):
        srcs, outs, sems = refs[:n], refs[n:2 * n], refs[2 * n:]
        comm.first(srcs, outs, sems)
        if comm.mid is not None:
            comm.mid(srcs, outs, sems)
        comm.last(srcs, outs, sems)

    return _pcall(body, name=name, in_specs=[_HBM] * n, out_specs=[_HBM] * n, out_shape=comm.out_shape,
                  scratch_shapes=comm.sems)(*comm.ins)


class _NoComm:
    n, ins, out_shape, sems, mid = 0, [], [], [], None

    def first(self, srcs, outs, sems):
        pass

    def last(self, srcs, outs, sems):
        pass


_NOTHING = _NoComm()


class _SiblingHalves:
    mid = None

    def __init__(self, grads):
        self.ins = list(grads)
        n = self.n = len(grads)
        self.out_shape = [jax.ShapeDtypeStruct((4, t.shape[1] // 2, t.shape[2]), t.dtype) for t in grads]
        self.sems = [pltpu.SemaphoreType.DMA((n,)), pltpu.SemaphoreType.DMA((n,))]

    def _copies(self, srcs, outs, sems):
        ssem, rsem = sems
        x, y, c = _me()
        copies = []
        for a in range(self.n):
            h = self.ins[a].shape[1] // 2
            copies.append(_remote(srcs[a].at[:, pl.ds((1 - c) * h, h)], outs[a], ssem.at[a], rsem.at[a], (x, y, 1 - c)))
        return copies

    def first(self, srcs, outs, sems):
        for cp in self._copies(srcs, outs, sems):
            cp.start()

    def last(self, srcs, outs, sems):
        for cp in self._copies(srcs, outs, sems):
            cp.wait()


def _reduce_finish(reds, name):
    n = len(reds)

    def body(*refs):
        outs = refs[n:2 * n]
        ssem, rsem = refs[2 * n:]
        x, y, c = _me()
        copies = []
        for a in range(n):
            h = reds[a].shape[0] // 2
            mine = outs[a].at[pl.ds(c * h, h)]
            copies.append(_remote(mine, mine, ssem.at[a], rsem.at[a], (x, y, 1 - c)))
        for cp in copies:
            cp.start()
        for a in range(n):
            h = reds[a].shape[0] // 2
            dst = outs[a].at[pl.ds((1 - c) * h, h)]
            _remote(dst, dst, ssem.at[a], rsem.at[a], (x, y, 1 - c)).wait_recv()
        for cp in copies:
            cp.wait_send()

    return _pcall(
        body, name=name, in_specs=[_HBM] * n, out_specs=[_HBM] * n,
        out_shape=[jax.ShapeDtypeStruct(t.shape, t.dtype) for t in reds],
        input_output_aliases={a: a for a in range(n)},
        scratch_shapes=[pltpu.SemaphoreType.DMA((n,)), pltpu.SemaphoreType.DMA((n,))],
    )(*reds)


def _half_sum(fn, full, halves, out_full, out_dtype, core, name):
    p, h, c = (halves[0].shape if halves else (full[0].shape[0], full[0].shape[1] // 2, full[0].shape[2]))
    br = _div(h, max(16, (1 << 19) // (p * c)), 16)
    nb = h // br
    mine3 = pl.BlockSpec((p, br, c), lambda i, core_ref: (0, core_ref[0] * nb + i, 0))
    half3 = pl.BlockSpec((p, br, c), lambda i, core_ref: (0, i, 0))

    def body(core_ref, *refs):
        refs[-1][...] = fn(*[t[...].astype(F32) for t in refs[:-1]]).astype(out_dtype)

    if out_full:
        out_spec = pl.BlockSpec((br, c), lambda i, core_ref: (core_ref[0] * nb + i, 0))
        out_shape = jax.ShapeDtypeStruct((2 * h, c), out_dtype)
    else:
        out_spec, out_shape = half3, jax.ShapeDtypeStruct((p, h, c), out_dtype)
    return _pcall(
        body, name=name,
        grid_spec=pltpu.PrefetchScalarGridSpec(
            num_scalar_prefetch=1, grid=(nb,), in_specs=[mine3] * len(full) + [half3] * len(halves),
            out_specs=out_spec),
        out_shape=out_shape, compiler_params=_cparams(("parallel",)),
    )(core, *full, *halves)


def _ada_fwd(c_all, w, b):
    def body(c_ref, w_ref, b_ref, o_ref):
        o_ref[...] = jnp.dot(c_ref[...], w_ref[...], precision=HI, preferred_element_type=F32) + b_ref[...]

    return _pcall(body, name="ada_fwd", out_shape=jax.ShapeDtypeStruct((c_all.shape[0], w.shape[1]), F32),
                  compiler_params=pltpu.CompilerParams(vmem_limit_bytes=VMEM_LIMIT))(c_all, w, b)


def _ada_bwd(c_all_t, d):
    def body(c_ref, d_ref, o_ref):
        o_ref[...] = jnp.dot(c_ref[...], d_ref[...], precision=HI, preferred_element_type=F32)

    return _pcall(body, name="ada_bwd", out_shape=jax.ShapeDtypeStruct((c_all_t.shape[0], d.shape[1]), F32),
                  compiler_params=pltpu.CompilerParams(vmem_limit_bytes=VMEM_LIMIT))(c_all_t, d)


def _sum_lead(x, name):
    p, r, n = x.shape
    br = _div(r, 512, 8)

    def body(x_ref, o_ref):
        acc = x_ref[0]
        for j in range(1, p):
            acc = acc + x_ref[j]
        o_ref[...] = acc

    return _pcall(
        body, name=name, grid=(r // br,), in_specs=[pl.BlockSpec((p, br, n), lambda i: (0, i, 0))],
        out_specs=pl.BlockSpec((br, n), lambda i: (i, 0)), out_shape=jax.ShapeDtypeStruct((r, n), F32),
        compiler_params=_cparams(("parallel",)),
    )(x)


def _adamw(w, g, m, v, name):
    shape = w.shape
    cols = shape[-1]
    w2, g2, m2, v2 = [t.reshape(-1, cols) for t in (w, g, m, v)]
    rows = w2.shape[0]
    br = _div(rows, max(8, (1 << 19) // cols // 8 * 8), 8)
    outs = _rows_fwd(_f_adamw, [(t, cols, 0) for t in (w2, g2, m2, v2)], [], [(cols, F32)] * 3, name=name, br=br)
    return [o.reshape(shape) for o in outs]


_BIG = (("w_in", 1), ("w_up", 1), ("w_down", 0), ("w_o", 0), ("w_rwkv_out", 0), ("w_att_out", 1), ("w2", 1), ("a2", 1),
        ("g2", 1))


_NEEDED_FIRST = ("w_in", "w_att_out", "w2", "a2", "g2")
_NEEDED_LATER = ("w_up", "w_down", "w_o", "w_rwkv_out")
_DONE_EARLY = ("w_up", "w_down", "w_o", "w_rwkv_out", "w_att_out")
_DONE_LATE = ("w_in", "w2", "a2", "g2")


def _cols_joined(t):
    return jnp.concatenate([t[j] for j in range(4)], axis=1)


def _cols_split(t):
    n = t.shape[1] // 4
    return jnp.stack([t[:, j * n:(j + 1) * n] for j in range(4)])


def _col_window(parts, lo, hi):
    out, pos = [], 0
    for t, w in parts:
        a, b = max(lo, pos), min(hi, pos + w)
        if a < b:
            out.append(t[:, a - pos:b - pos])
        pos += w
    return out[0] if len(out) == 1 else jnp.concatenate(out, axis=1)


def _rows_joined(t):
    return t.reshape(4 * t.shape[1], t.shape[2])


def _rows_split(t):
    return t.reshape(4, t.shape[0] // 4, t.shape[1])


def _step_to_scan(x, tgt, ada, wts):
    sh1, sc1, gt1, sh2, sc2, gt2 = ada
    br = 256
    grp = lax.broadcasted_iota(jnp.int32, (D, 128), 0) // 64 == lax.broadcasted_iota(jnp.int32, (D, 128), 1)
    e = grp.astype(F32)
    et = e.T
    w_in = [(wts["w_in"][j], wts["w_in"].shape[2]) for j in range(4)]
    w_att = _col_window(w_in, 0, N_ATT)
    w_rw = jnp.pad(_col_window(w_in, N_ATT, N_ATT + N_RW), ((0, 0), (0, N_RWP - N_RW)))
    w_gate = _col_window(w_in, N_ATT + N_RW, N_ATT + N_RW + N_GATE)
    mu = jnp.pad(wts["mu_shift"], ((0, 0), (0, N_RWP - N_RW)))
    wl = jnp.zeros((N_LORA, 3 * D), F32)
    wl = wl.at[0:64, 0:D].set(_cols_joined(wts["w2"]).astype(F32))
    wl = wl.at[64:128, D:2 * D].set(_cols_joined(wts["a2"]).astype(F32))
    wl = wl.at[128:288, 2 * D:3 * D].set(_cols_joined(wts["g2"]).astype(F32))
    pre1_c = [wts["norm1_w"], sc1, sh1]
    (h1,) = _rows_fwd(_f_pre, [(x, D, 0)], pre1_c, [(D, BF16), None], name="pre1_fwd", br=br)
    att_in = _mm(h1, w_att, name="mm_att_in")
    z = _mm(h1, w_rw, name="mm_rw_in")
    gate_in = _mm(h1, w_gate, name="mm_gate_in")
    att_o, att_l = [], []
    for g, (_, dil) in enumerate(ATT_PATTERNS):
        o, l = _att_fwd(att_in, g, dil)
        att_o.append(o)
        att_l.append(l)
    comb_rows = [(t, ATT_WIDTH, 0) for t in att_o + att_l]
    (att,) = _rows_fwd(_f_comb, comb_rows, [], [(ATT_WIDTH, BF16)], name="comb_fwd", br=br)
    y_att = _mm(att, wts["w_att_out"], b_chip=True, name="mm_att_out")
    zs = _shift_fwd(z, mu)
    rwpre_c = [wts["w0"], wts["a0"], wts["k_k"], wts["k_a"], wl, e, et]
    lw, km, aa, bb, gg = _rows_fwd(_f_rwpre, [(zs, N_RWP, 0)], rwpre_c,
                                   [None, (D, F32), (D, F32), None, (D, F32), (D, F32), (D, F32)],
                                   name="rwpre_fwd", br=br)
    return dict(x=x, tgt=tgt, wts=wts, br=br, e=e, et=et, gt1=gt1, sc2=sc2, sh2=sh2, gt2=gt2, w_att=w_att, w_rw=w_rw,
                w_gate=w_gate, mu=mu, pre1_c=pre1_c, h1=h1, att_in=att_in, z=z, gate_in=gate_in, comb_rows=comb_rows,
                att=att, y_att=y_att, zs=zs, rwpre_c=rwpre_c, lw=lw, km=km, aa=aa, bb=bb, gg=gg)


def _step_between_scans(st, y_raw, late):
    x, tgt, wts, br, e, et = st["x"], st["tgt"], st["wts"], st["br"], st["e"], st["et"]
    zs, km, gg, gate_in, y_att, att = st["zs"], st["km"], st["gg"], st["gate_in"], st["y_att"], st["att"]
    comb_rows, att_in = st["comb_rows"], st["att_in"]
    gt1, sc2, sh2, gt2 = st["gt1"], st["sc2"], st["sh2"], st["gt2"]
    w_up, w_ao = late["w_up"], wts["w_att_out"]
    w_down, w_o, w_ro = _rows_joined(late["w_down"]), _rows_joined(late["w_o"]), _rows_joined(late["w_rwkv_out"])
    post_rows = [(y_raw, D, 0), (zs, D, 0), (zs, D, 2), (km, D, 0), (gg, D, 0)]
    post_c = [wts["lnx_w"], wts["lnx_b"], wts["r_k"], e, et]
    (rw_out,) = _rows_fwd(_f_rwpost, post_rows, post_c, [(D, BF16)], name="rwpost_fwd", br=br)
    y_rw = _mm(rw_out, w_ro, name="mm_rw_out")
    mix_rows = [(gate_in, N_GATE, 0), (y_att, D, 0), (y_rw, D, 0)]
    (mix,) = _rows_fwd(_f_mix, mix_rows, [wts["b_gate"]], [(D, BF16)], name="mix_fwd", br=br)
    o = _mm(mix, w_o, name="mm_o")
    pre2_c = [gt1, wts["norm2_w"], sc2, sh2]
    x1, h2 = _rows_fwd(_f_pre2, [(x, D, 0), (o, D, 0)], pre2_c, [(D, F32), (D, BF16)], name="pre2_fwd", br=br)
    u = _mm(h2, w_up, b_chip=True, name="mm_up")
    act = _conv_fwd(u, wts["conv_w"], wts["conv_b"])
    f = _mm(act, w_down, name="mm_down")
    fin_rows = [(x1, D, 0), (f, D, 0), (tgt, D, 0)]
    fin_c = [gt2, wts["norm_f_w"]]

    def fin_fwd(*a):
        (l,) = _f_fin(*a)
        return (jnp.broadcast_to(jnp.sum(l, axis=0, keepdims=True), (8, 128)),)

    (loss_acc,) = _rows_fwd(fin_fwd, fin_rows, fin_c, [], name="fin_fwd", br=br, acc_shape=(8, 128))

    gw = {}
    dx1a, df, d_gt2, gw["norm_f_w"] = _rows_bwd(
        _f_fin, fin_rows, fin_c, [[]], wrt_rows=[0, 1], wrt_consts=[0, 1], drow_dtypes=[F32, BF16],
        name="fin_bwd", br=br, unit_cot=True)
    dact = _mm(df, w_down, tb=True, name="mm_dact")
    gw["w_down"] = _rows_split(_mm(act, df, ta=True, name="mm_dw_down"))
    du, gw["conv_w"], gw["conv_b"] = _conv_bwd(u, wts["conv_w"], wts["conv_b"], dact)
    dh2 = _mm(du, w_up, tb=True, b_chip=True, name="mm_dh2")
    gw["w_up"] = _mm(h2, du, ta=True, out_chip=True, name="mm_dw_up")
    dxa, do, d_gt1, gw["norm2_w"], d_sc2, d_sh2 = _rows_bwd(
        _f_pre2, [(x, D, 0), (o, D, 0)], pre2_c, [[(dx1a, D, 0)], [(dh2, D, 0)]], wrt_rows=[0, 1],
        wrt_consts=[0, 1, 2, 3], drow_dtypes=[F32, BF16], name="pre2_bwd", br=br)
    dmix = _mm(do, w_o, tb=True, name="mm_dmix")
    gw["w_o"] = _rows_split(_mm(mix, do, ta=True, name="mm_dw_o"))
    dgate, dya, dyr, gw["b_gate"] = _rows_bwd(
        _f_mix, mix_rows, [wts["b_gate"]], [[(dmix, D, 0)]], wrt_rows=[0, 1, 2], wrt_consts=[0],
        drow_dtypes=[BF16] * 3, name="mix_bwd", br=br)
    datt = _mm(dya, w_ao, tb=True, b_chip=True, name="mm_datt")
    gw["w_att_out"] = _mm(att, dya, ta=True, out_chip=True, name="mm_dw_att_out")
    drw = _mm(dyr, w_ro, tb=True, name="mm_drw")
    gw["w_rwkv_out"] = _rows_split(_mm(rw_out, dyr, ta=True, name="mm_dw_rw_out"))
    dcomb = _rows_bwd(_f_comb, comb_rows, [], [[(datt, ATT_WIDTH, 0)]], wrt_rows=list(range(6)), wrt_consts=[],
                      drow_dtypes=[F32] * 6, name="comb_bwd", br=br)
    datt_in = None
    for g, (_, dil) in enumerate(ATT_PATTERNS):
        datt_in = _att_bwd(att_in, g, dil, dcomb[g], dcomb[3 + g], datt_in)
    dy_raw, dr_p, dv_p, dkm_p, dgg, gw["lnx_w"], gw["lnx_b"], gw["r_k"], *recv_early = _rows_bwd(
        _f_rwpost, post_rows, post_c, [[(drw, D, 0)]], wrt_rows=[0, 1, 2, 3, 4], wrt_consts=[0, 1, 2],
        drow_dtypes=[F32] * 5, name="rwpost_bwd", br=br, comm=_SiblingHalves([gw[n] for n in _DONE_EARLY]))
    st.update(loss=loss_acc[0, 0], gw=gw, dxa=dxa, dgate=dgate, datt_in=datt_in,
              dy_raw=dy_raw, dr_p=dr_p, dv_p=dv_p, dkm_p=dkm_p, dgg=dgg, d_ada_late=(d_gt1, d_sh2, d_sc2, d_gt2),
              recv_early=recv_early)
    return st


def _chip_parts(grads, recv, names, core):
    return [_half_sum(lambda a, b: a + b, [g], [r], False, BF16, core, "reduce_add2_" + n)
            for g, r, n in zip(grads, recv, names)]


def _step_after_scan(st, scan_grads, core):
    x, br, gw, h1, zs = st["x"], st["br"], st["gw"], st["h1"], st["zs"]
    dr_s, dlw, dkm_s, dv_s, daa, dbb = scan_grads
    pre_cots = [[(st["dr_p"], D, 0), (dr_s, D, 0)], [(dlw, D, 0)], [(st["dkm_p"], D, 0), (dkm_s, D, 0)],
                [(st["dv_p"], D, 0), (dv_s, D, 0)], [(daa, D, 0)], [(dbb, D, 0)], [(st["dgg"], D, 0)]]
    dzs, gw["w0"], gw["a0"], gw["k_k"], gw["k_a"], dwl = _rows_bwd(
        _f_rwpre, [(zs, N_RWP, 0)], st["rwpre_c"], pre_cots, wrt_rows=[0], wrt_consts=[0, 1, 2, 3, 4],
        drow_dtypes=[F32], name="rwpre_bwd", br=128)
    gw["w2"], gw["a2"] = _cols_split(dwl[0:64, 0:D]), _cols_split(dwl[64:128, D:2 * D])
    gw["g2"] = _cols_split(dwl[128:288, 2 * D:3 * D])
    dz, dmu = _shift_bwd(st["z"], st["mu"], dzs)
    gw["mu_shift"] = dmu[:, :N_RW]
    datt_in, dgate = st["datt_in"], st["dgate"]
    dw_in = [(_mm(h1, datt_in, ta=True, name="mm_dw_att"), N_ATT), (_mm(h1, dz, ta=True, name="mm_dw_rw"), N_RW),
             (_mm(h1, dgate, ta=True, name="mm_dw_gate"), N_GATE)]
    shard = (N_ATT + N_RW + N_GATE) // 4
    gw["w_in"] = jnp.stack([_col_window(dw_in, j * shard, (j + 1) * shard) for j in range(4)])
    late = [gw[n] for n in _DONE_LATE]
    parts = _chip_parts(late, _run_comm(_SiblingHalves(late), "reduce_sib_late"), _DONE_LATE, core)
    dh1, slots_late = _mm_nt_sum([(datt_in, st["w_att"]), (dz, st["w_rw"]), (dgate, st["w_gate"])],
                                 comm=_ScatterToChips(parts), name="mm_dh1")
    grad_x, gw["norm1_w"], d_sc1, d_sh1 = _rows_bwd(
        _f_pre, [(x, D, 0)], st["pre1_c"], [[(dh1, D, 0)], [(st["dxa"], D, 0)]], wrt_rows=[0], wrt_consts=[0, 1, 2],
        drow_dtypes=[F32], name="pre1_bwd", br=br)
    d_gt1, d_sh2, d_sc2, d_gt2 = st["d_ada_late"]
    return st["loss"], grad_x, (d_sh1, d_sc1, d_gt1, d_sh2, d_sc2, d_gt2), gw, slots_late


_SMALL = ("b_ada", "norm1_w", "b_gate", "mu_shift", "w0", "a0", "k_k", "k_a", "r_k", "lnx_w", "lnx_b", "norm2_w",
          "conv_b", "norm_f_w")
_NAMES = ("w_ada", "b_ada", "norm1_w", "w_in", "b_gate", "mu_shift", "w0", "w2", "a0", "a2", "g2", "k_k", "k_a", "r_k",
          "lnx_w", "lnx_b", "w_att_out", "w_rwkv_out", "w_o", "norm2_w", "w_up", "conv_w", "conv_b", "w_down",
          "norm_f_w")


def kernel(x, c, w_ada, b_ada, norm1_w, w_in, b_gate, mu_shift, w0, w2, a0, a2, g2, k_k, k_a, r_k, lnx_w, lnx_b, w_att_out, w_rwkv_out, w_o, norm2_w, w_up, conv_w, conv_b, w_down, norm_f_w, loss_target, m_w_ada, m_b_ada, m_norm1_w, m_w_in, m_b_gate, m_mu_shift, m_w0, m_w2, m_a0, m_a2, m_g2, m_k_k, m_k_a, m_r_k, m_lnx_w, m_lnx_b, m_w_att_out, m_w_rwkv_out, m_w_o, m_norm2_w, m_w_up, m_conv_w, m_conv_b, m_w_down, m_norm_f_w, v_w_ada, v_b_ada, v_norm1_w, v_w_in, v_b_gate, v_mu_shift, v_w0, v_w2, v_a0, v_a2, v_g2, v_k_k, v_k_a, v_r_k, v_lnx_w, v_lnx_b, v_w_att_out, v_w_rwkv_out, v_w_o, v_norm2_w, v_w_up, v_conv_w, v_conv_b, v_w_down, v_norm_f_w):
    args = dict(locals())
    p, pm, pv = {}, {}, {}
    for name in _NAMES:
        for dst, key in ((p, name), (pm, "m_" + name), (pv, "v_" + name)):
            t = args[key]
            dst[name] = t.reshape(1, -1) if name in ("r_k", "norm_f_w") else t.reshape(t.shape[-2], t.shape[-1])
    xi, yi, ci = _me()
    chip = 2 * xi + yi
    dev = 4 * xi + 2 * yi + ci
    x2, tgt = x[0], loss_target[0]

    n_cw = 3 * (2 * D_FF // 4)
    vec = jnp.concatenate([c.reshape(-1), p["conv_w"].reshape(-1), jnp.zeros((8 * D - D - n_cw,), F32)]).reshape(8, D)
    g0 = _allgather8(vec, "gather_c").reshape(8, 8 * D)
    c_all = g0[:, :D]
    conv_w_full = jnp.concatenate([g0[2 * j, D:D + n_cw].reshape(3, -1) for j in range(4)], axis=1)
    n_ada = 6 * D // 4
    b_ada_sh = lax.dynamic_slice(p["b_ada"], (0, chip * n_ada), (1, n_ada))
    ada_sh = _ada_fwd(c_all, p["w_ada"], b_ada_sh)
    ga = _allgather8(ada_sh, "gather_ada")
    ada_all = jnp.concatenate([ga[2 * j] for j in range(4)], axis=1)
    ada_row = lax.dynamic_slice(ada_all, (dev, 0), (1, 6 * D))
    ada = [ada_row[:, j * D:(j + 1) * D] for j in range(6)]

    big = [n for n, _ in _BIG]
    shard = {n: p[n].astype(BF16) for n in big}
    wts = dict(zip(_NEEDED_FIRST, _run_comm(_GatherWeights([shard[n] for n in _NEEDED_FIRST]), "gather_w")))
    for n in _SMALL:
        wts[n] = p[n]
    wts["conv_w"] = conv_w_full
    core = ci.reshape(1).astype(jnp.int32)

    st = _step_to_scan(x2, tgt, ada, wts)
    y_raw, s0s, late = _scan_fwd(st["zs"], st["lw"], st["km"], st["aa"], st["bb"],
                                 _GatherWeights([shard[n] for n in _NEEDED_LATER]))
    st = _step_between_scans(st, y_raw, dict(zip(_NEEDED_LATER, late)))
    early = _chip_parts([st["gw"][n] for n in _DONE_EARLY], st["recv_early"], _DONE_EARLY, core)
    scan_grads, slots_early = _scan_bwd(st["zs"], st["lw"], st["km"], st["aa"], st["bb"], s0s, st["dy_raw"],
                                        _ScatterToChips(early))
    loss_part, grad_x, d_ada, gw, slots_late = _step_after_scan(st, scan_grads, core)

    small = [jnp.concatenate(d_ada, axis=1)] + [gw[n] for n in _SMALL[1:]] + [gw["conv_w"], loss_part.reshape(1, 1)]
    sizes = [t.size for t in small]
    flat = jnp.concatenate([t.reshape(-1) for t in small])
    npad = (-flat.shape[0]) % (8 * D)
    srows = (flat.shape[0] + npad) // D
    flat = jnp.concatenate([flat, jnp.zeros((npad,), F32)]).reshape(srows, D)
    parts = _allgather8(flat, "gather_small")
    tot = _sum_lead(parts, "sum_small").reshape(-1)
    pieces, pos = [], 0
    for sz in sizes:
        pieces.append(tot[pos:pos + sz])
        pos += sz
    grads = {}
    for n, piece in zip(_SMALL, pieces[:len(_SMALL)]):
        grads[n] = piece.reshape(p[n].shape)
    conv_w_grad = pieces[len(_SMALL)].reshape(3, 2 * D_FF)
    grads["conv_w"] = lax.dynamic_slice(conv_w_grad, (0, chip * (n_cw // 3)), (3, n_cw // 3))
    loss = pieces[-1][0]
    d_ada_all = parts[:, :6].reshape(8, 6 * D)
    grads["w_ada"] = _ada_bwd(c_all.T, lax.dynamic_slice(d_ada_all, (0, chip * n_ada), (8, n_ada)))

    order = _DONE_EARLY + _DONE_LATE
    reds = [_half_sum(lambda t: t[0] + t[1] + t[2] + t[3], [], [t], True, F32, core, "reduce_add4_" + n)
            for n, t in zip(order, list(slots_early) + list(slots_late))]
    for n, g in zip(order, _reduce_finish(reds, "reduce_sib2")):
        grads[n] = g

    outs_g, outs_d, outs_m, outs_v = [], [], [], []
    for name in _NAMES:
        g = grads[name]
        d, m, v = _adamw(p[name], g, pm[name], pv[name], "adamw_" + name)
        shape = args[name].shape
        outs_g.append(g.reshape(shape))
        outs_d.append(d.reshape(shape))
        outs_m.append(m.reshape(shape))
        outs_v.append(v.reshape(shape))
    return (loss, grad_x.reshape(x.shape), *outs_g, *outs_d, *outs_m, *outs_v)
```

```python
import functools

import jax
import jax.numpy as jnp
from jax import lax
from jax.experimental import pallas as pl
from jax.experimental.pallas import tpu as pltpu

F32 = jnp.float32
BF16 = jnp.bfloat16
HI = lax.Precision.HIGHEST
MESH = pl.DeviceIdType.MESH

D = 1024
ATT_PATTERNS = ((128, 1), (512, 4), (2048, 16))
ATT_BLOCK = 128
ATT_WIDTH = 512
N_ATT = 3 * 3 * ATT_WIDTH
N_RW = 3 * D + 64 + 64 + 160
N_RWP = 3456
N_LORA = N_RWP - 3 * D
N_GATE = 2 * D
D_FF = 2816
RMS_EPS = 1e-6
GN_EPS = 64e-5
SCAN_CHUNK = 64
SCAN_PAIRS = 8
NEG = -1e30
VMEM_LIMIT = 48 * 1024 * 1024

ADAM_LR, ADAM_B1, ADAM_B2, ADAM_EPS, ADAM_WD, ADAM_STEP = 0.001, 0.9, 0.999, 1e-08, 0.01, 10


def _pcall(body, **kw):
    return pl.pallas_call(body, **kw)


def _cparams(sem):
    return pltpu.CompilerParams(dimension_semantics=sem, vmem_limit_bytes=VMEM_LIMIT)


def _div(n, pref, mult):
    best = None
    d = mult
    while d <= min(n, pref):
        if n % d == 0:
            best = d
        d += mult
    return best if best else n


def _dg(a, b, ca, cb):
    return lax.dot_general(a.astype(BF16), b.astype(BF16), (((ca,), (cb,)), ((), ())), preferred_element_type=F32)


@jax.custom_vjp
def _nn(a, b):
    return _dg(a, b, 1, 0)


@jax.custom_vjp
def _nt(a, b):
    return _dg(a, b, 1, 1)


@jax.custom_vjp
def _tn(a, b):
    return _dg(a, b, 0, 0)


_nn.defvjp(lambda a, b: (_nn(a, b), (a, b)), lambda res, g: (_nt(g, res[1]), _tn(res[0], g)))
_nt.defvjp(lambda a, b: (_nt(a, b), (a, b)), lambda res, g: (_nn(g, res[1]), _tn(g, res[0])))
_tn.defvjp(lambda a, b: (_tn(a, b), (a, b)), lambda res, g: (_nt(res[1], g), _nn(res[0], g)))


def _bdg(a, b, ca, cb):
    return lax.dot_general(a.astype(BF16), b.astype(BF16), (((ca,), (cb,)), ((0,), (0,))), preferred_element_type=F32)


@jax.custom_vjp
def _bnn(a, b):
    return _bdg(a, b, 2, 1)


@jax.custom_vjp
def _bnt(a, b):
    return _bdg(a, b, 2, 2)


@jax.custom_vjp
def _btn(a, b):
    return _bdg(a, b, 1, 1)


_bnn.defvjp(lambda a, b: (_bnn(a, b), (a, b)), lambda res, g: (_bnt(g, res[1]), _btn(res[0], g)))
_bnt.defvjp(lambda a, b: (_bnt(a, b), (a, b)), lambda res, g: (_bnn(g, res[1]), _btn(g, res[0])))
_btn.defvjp(lambda a, b: (_btn(a, b), (a, b)), lambda res, g: (_bnt(res[1], g), _bnn(res[0], g)))


def _split2(x):
    hi = x.astype(BF16)
    lo = (x - hi.astype(F32)).astype(BF16)
    return hi, lo


def _hsum_impl(x, e, et):
    eb, etb = e.astype(BF16), et.astype(BF16)
    s = jnp.dot(x.astype(BF16), eb, preferred_element_type=F32)
    shi, slo = _split2(s)
    return jnp.dot(shi, etb, preferred_element_type=F32) + jnp.dot(slo, etb, preferred_element_type=F32)


@jax.custom_vjp
def _hsum(x, e, et):
    return _hsum_impl(x, e, et)


_hsum.defvjp(lambda x, e, et: (_hsum_impl(x, e, et), (e, et)),
             lambda res, g: (_hsum_impl(g, res[0], res[1]), jnp.zeros_like(res[0]), jnp.zeros_like(res[1])))


def _mm(a, b, *, ta=False, tb=False, out_dtype=F32, add=None, b_chip=False, out_chip=False, comm=None, name):
    riding = _NOTHING if comm is None else comm
    nc = riding.n
    if ta:
        kdim, m = a.shape
    else:
        m, kdim = a.shape
    if b_chip:
        n = b.shape[1] if tb else 4 * b.shape[2]
    else:
        n = b.shape[0] if tb else b.shape[1]
    tm, tn, tk = _div(m, 1536, 128), _div(n, 1536, 128), _div(kdim, 1408, 128)
    if b_chip and tb:
        tk = kdim // 4
    if (b_chip and not tb) or out_chip:
        tn = n // 4
    nk = kdim // tk
    ca, cb = (0 if ta else 1), (1 if tb else 0)

    nin = 2 if add is None else 3
    gi, gj = m // tm, n // tn

    def body(*refs):
        a_ref, b_ref = refs[0], refs[1]
        add_ref = None if add is None else refs[2]
        o_ref = refs[nin + nc]
        step = (pl.program_id(0) * gj + pl.program_id(1)) * nk + pl.program_id(2)
        before, after = _comm_phases(riding, refs[nin:nin + nc] + refs[nin + nc + 1:nin + 2 * nc + 1]
                                     + refs[nin + 2 * nc + 1 + (nk > 1):], gi * gj * nk, step)
        before()
        part = lax.dot_general(a_ref[...], b_ref[...], (((ca,), (cb,)), ((), ())), preferred_element_type=F32)

        def finish(r):
            if add_ref is not None:
                r = r + add_ref[...]
            o_ref[...] = r.astype(o_ref.dtype)

        if nk == 1:
            finish(part)
            after()
            return
        acc = refs[nin + 2 * nc + 1]
        k = pl.program_id(2)

        @pl.when(k == 0)
        def _():
            acc[...] = part

        @pl.when(k > 0)
        def _():
            acc[...] += part

        @pl.when(k == nk - 1)
        def _():
            finish(acc[...])

        after()

    a_spec = pl.BlockSpec((tk, tm), lambda i, j, k: (k, i)) if ta else pl.BlockSpec((tm, tk), lambda i, j, k: (i, k))
    if b_chip:
        b_spec = (pl.BlockSpec((None, tn, tk), lambda i, j, k: (k, j, 0)) if tb
                  else pl.BlockSpec((None, tk, tn), lambda i, j, k: (j, k, 0)))
    else:
        b_spec = pl.BlockSpec((tn, tk), lambda i, j, k: (j, k)) if tb else pl.BlockSpec((tk, tn), lambda i, j, k: (k, j))
    in_specs = [a_spec, b_spec]
    args = [a, b]
    if add is not None:
        in_specs.append(pl.BlockSpec((tm, tn), lambda i, j, k: (i, j)))
        args.append(add)
    if out_chip:
        out_spec = pl.BlockSpec((None, tm, tn), lambda i, j, k: (j, i, 0))
        out_shape = jax.ShapeDtypeStruct((4, m, tn), out_dtype)
    else:
        out_spec = pl.BlockSpec((tm, tn), lambda i, j, k: (i, j))
        out_shape = jax.ShapeDtypeStruct((m, n), out_dtype)
    res = _pcall(
        body, name=name, grid=(gi, gj, nk), in_specs=in_specs + [_HBM] * nc, out_specs=[out_spec] + [_HBM] * nc,
        out_shape=[out_shape] + riding.out_shape,
        scratch_shapes=([] if nk == 1 else [pltpu.VMEM((tm, tn), F32)]) + riding.sems,
        compiler_params=_cparams(("arbitrary",) * 3 if nc else ("parallel", "parallel", "arbitrary")),
    )(*args, *riding.ins)
    return res[0] if comm is None else (res[0], res[1:])


def _mm_nt_sum(pairs, *, comm, name):
    m, n = pairs[0][0].shape[0], pairs[0][1].shape[0]
    tm, tn = _div(m, 1024, 128), _div(n, 1024, 128)
    tks = [_div(a.shape[1], 1408, 128) for a, _ in pairs]
    nks = [a.shape[1] // tk for (a, _), tk in zip(pairs, tks)]
    offs = [sum(nks[:p]) for p in range(len(pairs))]
    total, npair, nc = sum(nks), len(pairs), comm.n
    gi, gj = m // tm, n // tn

    def body(*refs):
        o_ref, acc = refs[2 * npair + nc], refs[2 * npair + 2 * nc + 1]
        k = pl.program_id(2)
        step = (pl.program_id(0) * gj + pl.program_id(1)) * total + k
        before, after = _comm_phases(comm, refs[2 * npair:2 * npair + nc]
                                     + refs[2 * npair + nc + 1:2 * npair + 2 * nc + 1]
                                     + refs[2 * npair + 2 * nc + 2:], gi * gj * total, step)
        before()
        for p in range(npair):
            def partial_product(p=p):
                part = lax.dot_general(refs[2 * p][...], refs[2 * p + 1][...], (((1,), (1,)), ((), ())),
                                       preferred_element_type=F32)
                if p == 0:
                    @pl.when(k == 0)
                    def _():
                        acc[...] = part

                    @pl.when(k > 0)
                    def _():
                        acc[...] += part
                else:
                    acc[...] += part

            pl.when(jnp.logical_and(k >= offs[p], k < offs[p] + nks[p]))(partial_product)

        @pl.when(k == total - 1)
        def _():
            o_ref[...] = acc[...]

        after()

    def spec(rows, tk, off, nk, lead):
        def block(i, j, k):
            return (i if lead == 0 else j, jnp.clip(k - off, 0, nk - 1))
        return pl.BlockSpec((rows, tk), block)

    in_specs, args = [], []
    for (a, b), tk, off, nk in zip(pairs, tks, offs, nks):
        in_specs += [spec(tm, tk, off, nk, 0), spec(tn, tk, off, nk, 1)]
        args += [a, b]
    res = _pcall(
        body, name=name, grid=(gi, gj, total), in_specs=in_specs + [_HBM] * nc,
        out_specs=[pl.BlockSpec((tm, tn), lambda i, j, k: (i, j))] + [_HBM] * nc,
        out_shape=[jax.ShapeDtypeStruct((m, n), F32)] + comm.out_shape,
        scratch_shapes=[pltpu.VMEM((tm, tn), F32)] + comm.sems,
        compiler_params=_cparams(("arbitrary",) * 3),
    )(*args, *comm.ins)
    return res[0], res[1:]


def _row_spec(br, w, cb):
    return pl.BlockSpec((br, w), lambda i: (i, cb))


def _const_spec(shape):
    return pl.BlockSpec(shape, lambda i: (0,) * len(shape))


def _rows_fwd(fn, rows, consts, outs, *, name, br, acc_shape=None):
    s = rows[0][0].shape[0]
    nr, nc = len(rows), len(consts)
    kept = [k for k, o in enumerate(outs) if o is not None]

    def body(*refs):
        xs = [r[...].astype(F32) for r in refs[:nr]]
        cs = [c[...] for c in refs[nr:nr + nc]]
        res = fn(*xs, *cs)
        orefs = refs[nr + nc:]
        for j, k in enumerate(kept):
            orefs[j][...] = res[k].astype(orefs[j].dtype)
        if acc_shape is not None:
            acc_ref = orefs[len(kept)]

            @pl.when(pl.program_id(0) == 0)
            def _():
                acc_ref[...] = jnp.zeros_like(acc_ref)

            acc_ref[...] += res[len(outs)]

    in_specs = [_row_spec(br, w, cb) for (_, w, cb) in rows] + [_const_spec(c.shape) for c in consts]
    out_specs = [_row_spec(br, outs[k][0], 0) for k in kept]
    out_shape = [jax.ShapeDtypeStruct((s, outs[k][0]), outs[k][1]) for k in kept]
    if acc_shape is not None:
        out_specs.append(_const_spec(acc_shape))
        out_shape.append(jax.ShapeDtypeStruct(acc_shape, F32))
    return _pcall(
        body, name=name, grid=(s // br,), in_specs=in_specs, out_specs=out_specs, out_shape=out_shape,
        compiler_params=_cparams(("arbitrary",)),
    )(*[r[0] for r in rows], *consts)


def _rows_bwd(fn, rows, consts, cots, *, wrt_rows, wrt_consts, drow_dtypes, name, br, unit_cot=False, comm=None):
    comm = _NOTHING if comm is None else comm
    ncomm = comm.n
    nout = len(wrt_rows) + len(wrt_consts)
    s = rows[0][0].shape[0]
    nr, nc = len(rows), len(consts)
    flat_cots = [c for lst in cots for c in lst]
    ncot = len(flat_cots)

    def body(*refs):
        xs = [r[...].astype(F32) for r in refs[:nr]]
        cs = [c[...] for c in refs[nr:nr + nc]]
        cvals = [c[...].astype(F32) for c in refs[nr + nc:nr + nc + ncot]]
        orefs = refs[nr + nc + ncot + ncomm:]
        before, after = _comm_phases(comm, refs[nr + nc + ncot:nr + nc + ncot + ncomm] + orefs[nout:], s // br)
        before()

        def g(*d):
            xs2, cs2 = list(xs), list(cs)
            for j, k in enumerate(wrt_rows):
                xs2[k] = d[j]
            for j, k in enumerate(wrt_consts):
                cs2[k] = d[len(wrt_rows) + j]
            return tuple(fn(*xs2, *cs2))

        prim = [xs[k] for k in wrt_rows] + [cs[k] for k in wrt_consts]
        outs, vjp = jax.vjp(g, *prim)
        ct = []
        pos = 0
        for o, lst in zip(outs, cots):
            if unit_cot:
                ct.append(jnp.ones_like(o))
                continue
            acc = jnp.zeros_like(o)
            for _ in lst:
                acc = acc + cvals[pos]
                pos += 1
            ct.append(acc)
        grads = vjp(tuple(ct))
        for j in range(len(wrt_rows)):
            orefs[j][...] = grads[j].astype(orefs[j].dtype)

        @pl.when(pl.program_id(0) == 0)
        def _():
            for j in range(len(wrt_consts)):
                oref = orefs[len(wrt_rows) + j]
                oref[...] = jnp.zeros_like(oref)

        for j in range(len(wrt_consts)):
            orefs[len(wrt_rows) + j][...] += grads[len(wrt_rows) + j]
        after()

    in_specs = ([_row_spec(br, w, cb) for (_, w, cb) in rows] + [_const_spec(c.shape) for c in consts]
                + [_row_spec(br, w, cb) for (_, w, cb) in flat_cots] + [_HBM] * ncomm)
    out_specs = ([_row_spec(br, rows[k][1], 0) for k in wrt_rows] + [_const_spec(consts[k].shape) for k in wrt_consts]
                 + [_HBM] * ncomm)
    out_shape = ([jax.ShapeDtypeStruct((s, rows[k][1]), dt) for k, dt in zip(wrt_rows, drow_dtypes)]
                 + [jax.ShapeDtypeStruct(consts[k].shape, F32) for k in wrt_consts] + comm.out_shape)
    return _pcall(
        body, name=name, grid=(s // br,), in_specs=in_specs, out_specs=out_specs, out_shape=out_shape,
        scratch_shapes=comm.sems, compiler_params=_cparams(("arbitrary",)),
    )(*[r[0] for r in rows], *consts, *[c[0] for c in flat_cots], *comm.ins)


def _rms(x, w):
    return x * lax.rsqrt(jnp.mean(x * x, axis=-1, keepdims=True) + RMS_EPS) * w


def _softplus(x):
    return jnp.maximum(x, 0.0) + jnp.log(1.0 + jnp.exp(-jnp.abs(x)))


def _f_pre(x, nw, sc, sh):
    return _rms(x, nw) * (1.0 + sc) + sh, x


def _f_pre2(x, o, gt, nw, sc, sh):
    x1 = x + gt * o
    return x1, _rms(x1, nw) * (1.0 + sc) + sh


def _f_fin(x1, f, tgt, gt, nfw):
    y = _rms(x1 + gt * f, nfw)
    return (0.5 * jnp.mean(jnp.square(y - tgt), axis=-1, keepdims=True),)


def _f_comb(o1, o2, o3, l1, l2, l3):
    m = lax.stop_gradient(jnp.maximum(jnp.maximum(l1, l2), l3))
    e1, e2, e3 = jnp.exp(l1 - m), jnp.exp(l2 - m), jnp.exp(l3 - m)
    return ((e1 * o1 + e2 * o2 + e3 * o3) / (e1 + e2 + e3),)


def _f_rwpre(zs, w0, a0, k_k, k_a, wl, e, et):
    r, k, v, zl = zs[:, 0:D], zs[:, D:2 * D], zs[:, 2 * D:3 * D], zs[:, 3 * D:N_RWP]
    lane = lax.broadcasted_iota(jnp.int32, zl.shape, 1)
    t = jnp.where(lane < 64, jnp.tanh(zl), jnp.where(lane < 128, zl, jnp.where(lane < 288, jax.nn.sigmoid(zl), 0.0)))
    lo = _nn(t[:, 0:128], wl[0:128, 0:2 * D])
    g = _nn(t[:, 128:N_LORA], wl[128:N_LORA, 2 * D:3 * D])
    w_log = -_softplus(-(w0 + lo[:, 0:D])) - 0.5
    lw = -jnp.exp(w_log)
    a = jax.nn.sigmoid(a0 + lo[:, D:2 * D])
    k_mod = k * (1.0 + (a - 1.0) * k_a)
    kk = k * k_k
    kk = kk / jnp.maximum(jnp.sqrt(_hsum(kk * kk, e, et)), 1e-12)
    return r, lw, k_mod, v, -kk, kk * a, g


def _f_rwpost(y, r, v, k_mod, g, lnx_w, lnx_b, r_k, e, et):
    mean = _hsum(y, e, et) * (1.0 / 64)
    yc = y - mean
    var = _hsum(yc * yc, e, et) * (1.0 / 64)
    yn = yc * lax.rsqrt(var + GN_EPS) * lnx_w + lnx_b
    bonus = _hsum(r * k_mod * r_k, e, et) * v
    return ((yn + bonus) * g,)


def _f_mix(gi, ya, yr, bg):
    gate = jax.nn.sigmoid(gi + bg)
    return (gate[:, 0:D] * ya + gate[:, D:2 * D] * yr,)


def _f_adamw(w, g, m, v):
    m = ADAM_B1 * m + (1.0 - ADAM_B1) * g
    v = ADAM_B2 * v + (1.0 - ADAM_B2) * jnp.square(g)
    m_hat = m / (1.0 - ADAM_B1 ** ADAM_STEP)
    v_hat = v / (1.0 - ADAM_B2 ** ADAM_STEP)
    return -ADAM_LR * (m_hat / (jnp.sqrt(v_hat) + ADAM_EPS) + ADAM_WD * w), m, v


def _down(x, k):
    row = lax.broadcasted_iota(jnp.int32, x.shape, 0)
    return jnp.where(row < k, 0.0, pltpu.roll(x, k, 0))


def _up(x, k):
    n = x.shape[0]
    row = lax.broadcasted_iota(jnp.int32, x.shape, 0)
    return jnp.where(row >= n - k, 0.0, pltpu.roll(x, n - k, 0))


def _col_spec(s, w, off=0):
    return pl.BlockSpec((s, w), lambda j: (0, j + off))


def _shift_fwd(z, mu):
    s, n = z.shape

    def body(z_ref, mu_ref, o_ref):
        zz = z_ref[...]
        o_ref[...] = zz + (_down(zz, 1) - zz) * mu_ref[...]

    return _pcall(
        body, name="shift_fwd", grid=(n // 128,), in_specs=[_col_spec(s, 128), _col_spec(1, 128)],
        out_specs=_col_spec(s, 128), out_shape=jax.ShapeDtypeStruct((s, n), F32),
        compiler_params=_cparams(("parallel",)),
    )(z, mu)


def _shift_bwd(z, mu, dzs):
    s, n = z.shape

    def body(z_ref, mu_ref, d_ref, dz_ref, dmu_ref):
        zz, d, m = z_ref[...], d_ref[...], mu_ref[...]
        dm = d * m
        dz_ref[...] = (d - dm + _up(dm, 1)).astype(dz_ref.dtype)
        dmu_ref[...] = jnp.sum(d * (_down(zz, 1) - zz), axis=0, keepdims=True)

    return _pcall(
        body, name="shift_bwd", grid=(n // 128,), in_specs=[_col_spec(s, 128), _col_spec(1, 128), _col_spec(s, 128)],
        out_specs=[_col_spec(s, 128), _col_spec(1, 128)],
        out_shape=[jax.ShapeDtypeStruct((s, n), BF16), jax.ShapeDtypeStruct((1, n), F32)],
        compiler_params=_cparams(("parallel",)),
    )(z, mu, dzs)


def _conv3(x, w_ref, b_ref):
    return b_ref[...] + w_ref[0:1, :] * _down(x, 2) + w_ref[1:2, :] * _down(x, 1) + w_ref[2:3, :] * x


def _conv_fwd(u, cw, cb):
    s = u.shape[0]
    nb = D_FF // 128

    def body(ug_ref, uv_ref, wg_ref, wv_ref, bg_ref, bv_ref, o_ref):
        gate = _conv3(ug_ref[...], wg_ref, bg_ref)
        val = _conv3(uv_ref[...], wv_ref, bv_ref)
        o_ref[...] = (gate * jax.nn.sigmoid(gate) * val).astype(o_ref.dtype)

    return _pcall(
        body, name="conv_fwd", grid=(nb,),
        in_specs=[_col_spec(s, 128), _col_spec(s, 128, nb), _col_spec(3, 128), _col_spec(3, 128, nb),
                  _col_spec(1, 128), _col_spec(1, 128, nb)],
        out_specs=_col_spec(s, 128), out_shape=jax.ShapeDtypeStruct((s, D_FF), BF16),
        compiler_params=_cparams(("parallel",)),
    )(u, u, cw, cw, cb, cb)


def _conv_bwd(u, cw, cb, dact):
    s = u.shape[0]
    nb = D_FF // 128

    def half(x, d, w_ref, du_ref, dw_ref, db_ref):
        x1, x2 = _down(x, 1), _down(x, 2)
        du_ref[...] = (w_ref[2:3, :] * d + w_ref[1:2, :] * _up(d, 1) + w_ref[0:1, :] * _up(d, 2)).astype(du_ref.dtype)
        dw_ref[0:1, :] = jnp.sum(d * x2, axis=0, keepdims=True)
        dw_ref[1:2, :] = jnp.sum(d * x1, axis=0, keepdims=True)
        dw_ref[2:3, :] = jnp.sum(d * x, axis=0, keepdims=True)
        db_ref[...] = jnp.sum(d, axis=0, keepdims=True)

    def body(ug_ref, uv_ref, wg_ref, wv_ref, bg_ref, bv_ref, da_ref,
             dug_ref, duv_ref, dwg_ref, dwv_ref, dbg_ref, dbv_ref):
        ug, uv, da = ug_ref[...], uv_ref[...], da_ref[...]
        gate = _conv3(ug, wg_ref, bg_ref)
        val = _conv3(uv, wv_ref, bv_ref)
        sg = jax.nn.sigmoid(gate)
        dgate = da * val * sg * (1.0 + gate * (1.0 - sg))
        dval = da * gate * sg
        half(ug, dgate, wg_ref, dug_ref, dwg_ref, dbg_ref)
        half(uv, dval, wv_ref, duv_ref, dwv_ref, dbv_ref)

    dug, duv, dwg, dwv, dbg, dbv = _pcall(
        body, name="conv_bwd", grid=(nb,),
        in_specs=[_col_spec(s, 128), _col_spec(s, 128, nb), _col_spec(3, 128), _col_spec(3, 128, nb),
                  _col_spec(1, 128), _col_spec(1, 128, nb), _col_spec(s, 128)],
        out_specs=[_col_spec(s, 128), _col_spec(s, 128), _col_spec(3, 128), _col_spec(3, 128),
                   _col_spec(1, 128), _col_spec(1, 128)],
        out_shape=[jax.ShapeDtypeStruct((s, D_FF), BF16), jax.ShapeDtypeStruct((s, D_FF), BF16),
                   jax.ShapeDtypeStruct((3, D_FF), F32), jax.ShapeDtypeStruct((3, D_FF), F32),
                   jax.ShapeDtypeStruct((1, D_FF), F32), jax.ShapeDtypeStruct((1, D_FF), F32)],
        compiler_params=_cparams(("parallel",)),
    )(u, u, cw, cw, cb, cb, dact)
    return (jnp.concatenate([dug, duv], axis=1), jnp.concatenate([dwg, dwv], axis=1),
            jnp.concatenate([dbg, dbv], axis=1))


ATT_BATCH = 4


def _att_batch(q, kp, kc, vp, vc, first):
    ma = lax.broadcasted_iota(jnp.int32, (1, ATT_BLOCK, 128), 2) < 64
    qs = jnp.concatenate([jnp.where(ma, q, 0.0), jnp.where(ma, 0.0, q)], axis=1)
    qi = lax.broadcasted_iota(jnp.int32, (1, 2 * ATT_BLOCK, ATT_BLOCK), 1) & (ATT_BLOCK - 1)
    kj = lax.broadcasted_iota(jnp.int32, (1, 2 * ATT_BLOCK, ATT_BLOCK), 2)
    okp = kj >= qi + jnp.where(first, 2 * ATT_BLOCK, 0)
    okc = kj <= qi
    sp = jnp.where(okp, _bnt(qs, kp) * 0.125, NEG)
    sc = jnp.where(okc, _bnt(qs, kc) * 0.125, NEG)
    m = lax.stop_gradient(jnp.maximum(jnp.max(sp, axis=-1, keepdims=True), jnp.max(sc, axis=-1, keepdims=True)))
    pp, pc = jnp.exp(sp - m), jnp.exp(sc - m)
    den = jnp.sum(pp, axis=-1, keepdims=True) + jnp.sum(pc, axis=-1, keepdims=True)
    o_s = (_bnn(pp, vp) + _bnn(pc, vc)) / den
    l_s = jnp.broadcast_to(m + jnp.log(den), o_s.shape)
    return (jnp.where(ma, o_s[:, :ATT_BLOCK], o_s[:, ATT_BLOCK:]), jnp.where(ma, l_s[:, :ATT_BLOCK], l_s[:, ATT_BLOCK:]))


def _att_pairs_per_step(dil):
    return ATT_BATCH if dil == 1 else 1


def _att_specs(g, dil):
    rows, pp = ATT_BLOCK * dil, _att_pairs_per_step(dil)

    def cur(slot):
        return pl.BlockSpec((rows, 128 * pp), lambda n, p: (n, (g * 3 + slot) * (4 // pp) + p))

    def prev(slot):
        return pl.BlockSpec((rows, 128 * pp), lambda n, p: (jnp.maximum(n - 1, 0), (g * 3 + slot) * (4 // pp) + p))

    return [cur(0), prev(1), cur(1), prev(2), cur(2)]


def _att_out_spec(dil):
    return pl.BlockSpec((ATT_BLOCK * dil, 128 * _att_pairs_per_step(dil)), lambda n, p: (n, p))


def _att_grid(s, dil):
    return (s // (ATT_BLOCK * dil), 4 // _att_pairs_per_step(dil))


def _att_windows(i, dil):
    if dil == 1:
        return [(pl.ds(0, ATT_BLOCK), pl.ds(128 * j, 128)) for j in range(ATT_BATCH)]
    return [(pl.ds(i * ATT_BATCH + j, ATT_BLOCK, stride=dil), pl.ds(0, 128)) for j in range(ATT_BATCH)]


def _att_fwd(att_in, g, dil):
    s = att_in.shape[0]

    def body(q_ref, kp_ref, kc_ref, vp_ref, vc_ref, o_ref, l_ref):
        first = pl.program_id(0) == 0

        def one(i, carry):
            win = _att_windows(i, dil)
            vals = [jnp.stack([ref[w] for w in win]) for ref in (q_ref, kp_ref, kc_ref, vp_ref, vc_ref)]
            o, l = _att_batch(*vals, first)
            for j, w in enumerate(win):
                o_ref[w] = o[j]
                l_ref[w] = l[j]
            return carry

        lax.fori_loop(0, max(1, dil // ATT_BATCH), one, 0)

    return _pcall(
        body, name=f"att_fwd{g}", grid=_att_grid(s, dil), in_specs=_att_specs(g, dil),
        out_specs=[_att_out_spec(dil)] * 2, out_shape=[jax.ShapeDtypeStruct((s, ATT_WIDTH), F32)] * 2,
        compiler_params=_cparams(("parallel", "parallel")),
    )(att_in, att_in, att_in, att_in, att_in)


def _att_bwd(att_in, g, dil, do, dl, acc):
    s = att_in.shape[0]

    def body(q_ref, kp_ref, kc_ref, vp_ref, vc_ref, do_ref, dl_ref, dq_ref, dkp_ref, dkc_ref, dvp_ref, dvc_ref):
        first = pl.program_id(0) == 0

        def one(i, carry):
            win = _att_windows(i, dil)
            vals = [jnp.stack([ref[w] for w in win]) for ref in (q_ref, kp_ref, kc_ref, vp_ref, vc_ref)]
            _, vjp = jax.vjp(lambda *a: _att_batch(*a, first), *vals)
            grads = vjp((jnp.stack([do_ref[w] for w in win]), jnp.stack([dl_ref[w] for w in win])))
            for ref, gr in zip((dq_ref, dkp_ref, dkc_ref, dvp_ref, dvc_ref), grads):
                for j, w in enumerate(win):
                    ref[w] = gr[j]
            return carry

        lax.fori_loop(0, max(1, dil // ATT_BATCH), one, 0)

    dq, dkp, dkc, dvp, dvc = _pcall(
        body, name=f"att_bwd{g}", grid=_att_grid(s, dil), in_specs=_att_specs(g, dil) + [_att_out_spec(dil)] * 2,
        out_specs=[_att_out_spec(dil)] * 5, out_shape=[jax.ShapeDtypeStruct((s, ATT_WIDTH), F32)] * 5,
        compiler_params=_cparams(("parallel", "parallel")),
    )(att_in, att_in, att_in, att_in, att_in, do, dl)

    unit, rb = ATT_BLOCK * dil, 1024
    steps = s // rb
    within = unit < rb

    def shifted(cur_ref, next_ref, has_next):
        nxt = jnp.where(has_next, next_ref[...], 0.0)
        return jnp.concatenate([cur_ref[unit:, :], nxt], axis=0) if within else nxt

    def cbody(dq_ref, dkc_ref, dkp_ref, dkn_ref, dvc_ref, dvp_ref, dvn_ref, *rest):
        o_ref = rest[-1]
        has_next = pl.program_id(0) + (1 if within else unit // rb) < steps
        o_ref[:, 0:ATT_WIDTH] = dq_ref[...].astype(BF16)
        o_ref[:, ATT_WIDTH:2 * ATT_WIDTH] = (dkc_ref[...] + shifted(dkp_ref, dkn_ref, has_next)).astype(BF16)
        o_ref[:, 2 * ATT_WIDTH:3 * ATT_WIDTH] = (dvc_ref[...] + shifted(dvp_ref, dvn_ref, has_next)).astype(BF16)

    cur = pl.BlockSpec((rb, ATT_WIDTH), lambda i: (i, 0))
    if within:
        nxt = pl.BlockSpec((unit, ATT_WIDTH), lambda i: (jnp.minimum((i + 1) * (rb // unit), s // unit - 1), 0))
    else:
        nxt = pl.BlockSpec((rb, ATT_WIDTH), lambda i: (jnp.minimum(i + unit // rb, steps - 1), 0))
    carried = [] if acc is None else [acc]
    return _pcall(
        cbody, name=f"att_bwd_sum{g}", grid=(steps,),
        in_specs=[cur, cur, cur, nxt, cur, cur, nxt] + [pl.BlockSpec(memory_space=pl.ANY)] * len(carried),
        out_specs=pl.BlockSpec((rb, 3 * ATT_WIDTH), lambda i: (i, g)),
        out_shape=jax.ShapeDtypeStruct((s, N_ATT), BF16), input_output_aliases={7: 0} if carried else {},
        compiler_params=_cparams(("parallel",)),
    )(dq, dkc, dkp, dkp, dvc, dvp, dvp, *carried)


def _unit_lower_inverse_impl(n):
    eye = (lax.broadcasted_iota(jnp.int32, (1,) + n.shape[1:], 1)
           == lax.broadcasted_iota(jnp.int32, (1,) + n.shape[1:], 2))
    t = jnp.where(eye, 1.0, 0.0) + n
    pw = n
    for _ in range(5):
        pw = _bnn(pw, pw)
        t = t + _bnn(t, pw)
    return t


@jax.custom_vjp
def _unit_lower_inverse(n):
    return _unit_lower_inverse_impl(n)


def _unit_lower_inverse_fwd(n):
    t = _unit_lower_inverse_impl(n)
    return t, t


_unit_lower_inverse.defvjp(_unit_lower_inverse_fwd, lambda t, g: (_bnt(_btn(t, g), t),))


@jax.custom_vjp
def _known_inverse(n, t):
    return t


_known_inverse.defvjp(lambda n, t: (t, t), lambda t, g: (_bnt(_btn(t, g), t), jnp.zeros_like(t)))


def _scan_chunk(r, lw, k, v, a, b, s0, inverse):
    c = SCAN_CHUNK
    p = s0.shape[0]
    ri = lax.broadcasted_iota(jnp.int32, (c, c), 0)
    ci = lax.broadcasted_iota(jnp.int32, (c, c), 1)
    cum = jnp.dot((ci <= ri).astype(F32), lw, precision=HI, preferred_element_type=F32)
    tot = jnp.sum(lw, axis=0, keepdims=True)
    ma = (lax.broadcasted_iota(jnp.int32, (c, 128 * p), 1) & 127) < 64

    def pairs(x):
        return jnp.concatenate([x[None, :, 128 * j:128 * (j + 1)] for j in range(p)], axis=0)

    def stack(x):
        return jnp.concatenate([pairs(jnp.where(ma, x, 0.0)), pairs(jnp.where(ma, 0.0, x))], axis=1)

    einv, eend = jnp.exp(-cum), jnp.exp(tot - cum)
    ra, aa = stack(r * jnp.exp(cum)), stack(a * jnp.exp(cum - lw))
    bi, ki, be, ke, vs = stack(b * einv), stack(k * einv), stack(b * eend), stack(k * eend), stack(v)
    r2 = lax.broadcasted_iota(jnp.int32, (1, 2 * c, 2 * c), 1)
    c2 = lax.broadcasted_iota(jnp.int32, (1, 2 * c, 2 * c), 2)
    same = (r2 >= c) == (c2 >= c)
    strict = jnp.logical_and(same, c2 < r2)
    incl = jnp.logical_and(same, c2 <= r2)
    s0 = jnp.where(same, s0, 0.0)
    prod = _bnt(jnp.concatenate([aa, ra], axis=1), jnp.concatenate([bi, ki], axis=1))
    a_ab = jnp.where(strict, prod[:, :2 * c, :2 * c], 0.0)
    a_ak = jnp.where(strict, prod[:, :2 * c, 2 * c:], 0.0)
    a_rb = jnp.where(incl, prod[:, 2 * c:, :2 * c], 0.0)
    a_rk = jnp.where(incl, prod[:, 2 * c:, 2 * c:], 0.0)
    t = inverse(a_ab)
    u = _bnn(t, _bnt(aa, s0) + _bnn(a_ak, vs))
    uv = jnp.concatenate([u, vs], axis=1)
    ys = _bnt(ra, s0) + _bnn(jnp.concatenate([a_rb, a_rk], axis=2), uv)
    s1 = s0 * pairs(jnp.exp(tot)) + _btn(uv, jnp.concatenate([be, ke], axis=1))
    y3 = ys[:, :c] + ys[:, c:]
    return (jnp.concatenate([y3[j] for j in range(p)], axis=1), s1), t


def _scan_specs(rev, n):
    def at(i):
        return n - 1 - i if rev else i

    def cm(cb):
        return pl.BlockSpec((SCAN_CHUNK, D), lambda i: (at(i), cb))

    return cm, pl.BlockSpec((1, SCAN_PAIRS, 128, 128), lambda i: (at(i), 0, 0, 0))


def _comm_phases(comm, refs, n, step=None):
    k = comm.n
    srcs, outs, sems = refs[:k], refs[k:2 * k], refs[2 * k:]
    i = pl.program_id(0) if step is None else step

    def before():
        @pl.when(i == 0)
        def _():
            comm.first(srcs, outs, sems)

    def after():
        if comm.mid is not None:
            @pl.when(i == (3 * n) // 4)
            def _():
                comm.mid(srcs, outs, sems)

        @pl.when(i == n - 1)
        def _():
            comm.last(srcs, outs, sems)

    return before, after


def _scan_fwd(zs, lw, km, aa, bb, comm):
    s = zs.shape[0]
    n = s // SCAN_CHUNK
    cm, st = _scan_specs(False, n)
    k = comm.n

    def body(*refs):
        r_ref, lw_ref, k_ref, v_ref, a_ref, b_ref = refs[:6]
        y_ref, s0_ref, t_ref = refs[6 + k:9 + k]
        state = refs[9 + 2 * k]
        before, after = _comm_phases(comm, refs[6:6 + k] + refs[9 + k:9 + 2 * k] + refs[10 + 2 * k:], n)
        before()

        @pl.when(pl.program_id(0) == 0)
        def _():
            state[...] = jnp.zeros_like(state)

        s0 = state[...]
        s0_ref[0] = s0
        (y, s1), t = _scan_chunk(*[ref[...] for ref in (r_ref, lw_ref, k_ref, v_ref, a_ref, b_ref)], s0,
                                 _unit_lower_inverse)
        y_ref[...] = y
        t_ref[0] = t.astype(BF16)
        state[...] = s1
        after()

    per_chunk = (n, SCAN_PAIRS, 128, 128)
    res = _pcall(
        body, name="scan_fwd", grid=(n,), in_specs=[cm(0), cm(0), cm(0), cm(2), cm(0), cm(0)] + [_HBM] * k,
        out_specs=[cm(0), st, st] + [_HBM] * k,
        out_shape=[jax.ShapeDtypeStruct((s, D), F32), jax.ShapeDtypeStruct(per_chunk, F32),
                   jax.ShapeDtypeStruct(per_chunk, BF16)] + comm.out_shape,
        scratch_shapes=[pltpu.VMEM((SCAN_PAIRS, 128, 128), F32)] + comm.sems,
        compiler_params=_cparams(("arbitrary",)),
    )(zs, lw, km, zs, aa, bb, *comm.ins)
    return res[0], res[1], res[2], res[3:]


def _scan_bwd(zs, lw, km, aa, bb, s0s, ts, dy, comm):
    s = zs.shape[0]
    n = s // SCAN_CHUNK
    cm, st = _scan_specs(True, n)
    k = comm.n

    def body(*refs):
        r_ref, lw_ref, k_ref, v_ref, a_ref, b_ref, s0_ref, t_ref, dy_ref = refs[:9]
        douts = refs[9 + k:15 + k]
        dstate = refs[15 + 2 * k]
        before, after = _comm_phases(comm, refs[9:9 + k] + refs[15 + k:15 + 2 * k] + refs[16 + 2 * k:], n)
        before()

        @pl.when(pl.program_id(0) == 0)
        def _():
            dstate[...] = jnp.zeros_like(dstate)

        t = t_ref[0].astype(F32)
        prim = [ref[...] for ref in (r_ref, lw_ref, k_ref, v_ref, a_ref, b_ref)] + [s0_ref[0]]
        _, vjp, _ = jax.vjp(lambda *p: _scan_chunk(*p, lambda nil: _known_inverse(nil, t)), *prim, has_aux=True)
        grads = vjp((dy_ref[...], dstate[...]))
        for ref, gr in zip(douts, grads[:6]):
            ref[...] = gr
        dstate[...] = grads[6]
        after()

    res = _pcall(
        body, name="scan_bwd", grid=(n,),
        in_specs=[cm(0), cm(0), cm(0), cm(2), cm(0), cm(0), st, st, cm(0)] + [_HBM] * k,
        out_specs=[cm(0)] * 6 + [_HBM] * k, out_shape=[jax.ShapeDtypeStruct((s, D), F32)] * 6 + comm.out_shape,
        scratch_shapes=[pltpu.VMEM((SCAN_PAIRS, 128, 128), F32)] + comm.sems,
        compiler_params=_cparams(("arbitrary",)),
    )(zs, lw, km, zs, aa, bb, s0s, ts, dy, *comm.ins)
    return res[:6], res[6:]


_HBM = pl.BlockSpec(memory_space=pltpu.HBM)


def _me():
    return lax.axis_index("x"), lax.axis_index("y"), lax.axis_index("c")


def _allgather8(src, name):
    def body(src_ref, out_ref, ssem, rsem, lsem):
        x, y, c = _me()
        me = 4 * x + 2 * y + c
        local = pltpu.make_async_copy(src_ref, out_ref.at[me], lsem)
        local.start()
        peers = []
        for k in range(1, 8):
            peers.append(((1 - x) if k & 4 else x, (1 - y) if k & 2 else y, (1 - c) if k & 1 else c))
        sends = []
        for k, peer in enumerate(peers):
            cp = pltpu.make_async_remote_copy(src_ref, out_ref.at[me], ssem.at[k], rsem.at[k], device_id=peer,
                                              device_id_type=MESH)
            cp.start()
            sends.append(cp)
        for k, (px, py, pc) in enumerate(peers):
            pltpu.make_async_remote_copy(src_ref, out_ref.at[4 * px + 2 * py + pc], ssem.at[k], rsem.at[k],
                                         device_id=(px, py, pc), device_id_type=MESH).wait_recv()
        for cp in sends:
            cp.wait_send()
        local.wait()

    return _pcall(
        body, name=name, in_specs=[_HBM], out_specs=_HBM, out_shape=jax.ShapeDtypeStruct((8,) + src.shape, src.dtype),
        scratch_shapes=[pltpu.SemaphoreType.DMA((7,)), pltpu.SemaphoreType.DMA((7,)), pltpu.SemaphoreType.DMA],
    )(src)


def _other_chips(x, y):
    return [(1 - x, y), (x, 1 - y), (1 - x, 1 - y)]


def _remote(src, dst, ssem, rsem, to):
    return pltpu.make_async_remote_copy(src, dst, ssem, rsem, device_id=to, device_id_type=MESH)


class _GatherWeights:
    def __init__(self, shards):
        self.ins = list(shards)
        n = self.n = len(shards)
        self.out_shape = [jax.ShapeDtypeStruct((4,) + t.shape, t.dtype) for t in shards]
        self.sems = [pltpu.SemaphoreType.DMA((6 * n,)), pltpu.SemaphoreType.DMA((6 * n,)),
                     pltpu.SemaphoreType.DMA((n,)), pltpu.SemaphoreType.DMA((n,))]

    def _copies(self, srcs, outs, sems):
        ssem, rsem, lsem, osem = sems
        x, y, c = _me()
        me = 2 * x + y
        own, ici, landed, passed, passed_in = [], [], [], [], []
        for a in range(self.n):
            h = self.ins[a].shape[0] // 2
            mine, other = pl.ds(c * h, h), pl.ds((1 - c) * h, h)
            own.append(_remote(srcs[a], outs[a].at[me], lsem.at[a], osem.at[a], (x, y, 1 - c)))
            for k, (px, py) in enumerate(_other_chips(x, y)):
                s1, r1, s2, r2 = ssem.at[6 * a + k], rsem.at[6 * a + k], ssem.at[6 * a + 3 + k], rsem.at[6 * a + 3 + k]
                got, got_sib = outs[a].at[2 * px + py, mine], outs[a].at[2 * px + py, other]
                ici.append(_remote(srcs[a].at[mine], outs[a].at[me, mine], s1, r1, (px, py, c)))
                landed.append(_remote(got, got, s1, r1, (px, py, c)))
                passed.append(_remote(got, got, s2, r2, (x, y, 1 - c)))
                passed_in.append(_remote(got_sib, got_sib, s2, r2, (x, y, 1 - c)))
        return own, ici, landed, passed, passed_in

    def first(self, srcs, outs, sems):
        own, ici, _, _, _ = self._copies(srcs, outs, sems)
        for cp in own + ici:
            cp.start()

    def mid(self, srcs, outs, sems):
        _, _, landed, passed, _ = self._copies(srcs, outs, sems)
        for arrived, onward in zip(landed, passed):
            arrived.wait_recv()
            onward.start()

    def last(self, srcs, outs, sems):
        own, ici, _, passed, passed_in = self._copies(srcs, outs, sems)
        for cp in passed_in:
            cp.wait_recv()
        for cp in ici + passed:
            cp.wait_send()
        for cp in own:
            cp.wait()


class _ScatterToChips:
    def __init__(self, parts):
        self.ins = list(parts)
        n = self.n = len(parts)
        self.out_shape = [jax.ShapeDtypeStruct(t.shape, t.dtype) for t in parts]
        self.sems = [pltpu.SemaphoreType.DMA((3 * n,)), pltpu.SemaphoreType.DMA((3 * n,)), pltpu.SemaphoreType.DMA((n,))]

    def _copies(self, srcs, outs, sems):
        ssem, rsem, lsem = sems
        x, y, c = _me()
        me = 2 * x + y
        own, out, landed = [], [], []
        for a in range(self.n):
            own.append(pltpu.make_async_copy(srcs[a].at[me], outs[a].at[me], lsem.at[a]))
            for k, (px, py) in enumerate(_other_chips(x, y)):
                dst = outs[a].at[2 * px + py]
                out.append(_remote(srcs[a].at[2 * px + py], outs[a].at[me], ssem.at[3 * a + k], rsem.at[3 * a + k],
                                   (px, py, c)))
                landed.append(_remote(dst, dst, ssem.at[3 * a + k], rsem.at[3 * a + k], (px, py, c)))
        return own, out, landed

    def first(self, srcs, outs, sems):
        own, out, _ = self._copies(srcs, outs, sems)
        for cp in own + out:
            cp.start()

    mid = None

    def last(self, srcs, outs, sems):
        own, out, landed = self._copies(srcs, outs, sems)
        for cp in landed:
            cp.wait_recv()
        for cp in own:
            cp.wait()
        for cp in out:
            cp.wait_send()


def _run_comm(comm, name):
    n = comm.n

    def body(*refs):
        srcs, outs, sems = refs[:n], refs[n:2 * n], refs[2 * n:]
        comm.first(srcs, outs, sems)
        if comm.mid is not None:
            comm.mid(srcs, outs, sems)
        comm.last(srcs, outs, sems)

    return _pcall(body, name=name, in_specs=[_HBM] * n, out_specs=[_HBM] * n, out_shape=comm.out_shape,
                  scratch_shapes=comm.sems)(*comm.ins)


class _NoComm:
    n, ins, out_shape, sems, mid = 0, [], [], [], None

    def first(self, srcs, outs, sems):
        pass

    def last(self, srcs, outs, sems):
        pass


_NOTHING = _NoComm()


class _SiblingHalves:
    mid = None

    def __init__(self, grads):
        self.ins = list(grads)
        n = self.n = len(grads)
        self.out_shape = [jax.ShapeDtypeStruct((4, t.shape[1] // 2, t.shape[2]), t.dtype) for t in grads]
        self.sems = [pltpu.SemaphoreType.DMA((n,)), pltpu.SemaphoreType.DMA((n,))]

    def _copies(self, srcs, outs, sems):
        ssem, rsem = sems
        x, y, c = _me()
        copies = []
        for a in range(self.n):
            h = self.ins[a].shape[1] // 2
            copies.append(_remote(srcs[a].at[:, pl.ds((1 - c) * h, h)], outs[a], ssem.at[a], rsem.at[a], (x, y, 1 - c)))
        return copies

    def first(self, srcs, outs, sems):
        for cp in self._copies(srcs, outs, sems):
            cp.start()

    def last(self, srcs, outs, sems):
        for cp in self._copies(srcs, outs, sems):
            cp.wait()


def _reduce_finish(reds, name):
    n = len(reds)

    def body(*refs):
        outs = refs[n:2 * n]
        ssem, rsem = refs[2 * n:]
        x, y, c = _me()
        copies = []
        for a in range(n):
            h = reds[a].shape[0] // 2
            mine = outs[a].at[pl.ds(c * h, h)]
            copies.append(_remote(mine, mine, ssem.at[a], rsem.at[a], (x, y, 1 - c)))
        for cp in copies:
            cp.start()
        for a in range(n):
            h = reds[a].shape[0] // 2
            dst = outs[a].at[pl.ds((1 - c) * h, h)]
            _remote(dst, dst, ssem.at[a], rsem.at[a], (x, y, 1 - c)).wait_recv()
        for cp in copies:
            cp.wait_send()

    return _pcall(
        body, name=name, in_specs=[_HBM] * n, out_specs=[_HBM] * n,
        out_shape=[jax.ShapeDtypeStruct(t.shape, t.dtype) for t in reds],
        input_output_aliases={a: a for a in range(n)},
        scratch_shapes=[pltpu.SemaphoreType.DMA((n,)), pltpu.SemaphoreType.DMA((n,))],
    )(*reds)


def _half_sum(fn, full, halves, out_full, out_dtype, core, name):
    p, h, c = (halves[0].shape if halves else (full[0].shape[0], full[0].shape[1] // 2, full[0].shape[2]))
    br = _div(h, max(16, (1 << 19) // (p * c)), 16)
    nb = h // br
    mine3 = pl.BlockSpec((p, br, c), lambda i, core_ref: (0, core_ref[0] * nb + i, 0))
    half3 = pl.BlockSpec((p, br, c), lambda i, core_ref: (0, i, 0))

    def body(core_ref, *refs):
        refs[-1][...] = fn(*[t[...].astype(F32) for t in refs[:-1]]).astype(out_dtype)

    if out_full:
        out_spec = pl.BlockSpec((br, c), lambda i, core_ref: (core_ref[0] * nb + i, 0))
        out_shape = jax.ShapeDtypeStruct((2 * h, c), out_dtype)
    else:
        out_spec, out_shape = half3, jax.ShapeDtypeStruct((p, h, c), out_dtype)
    return _pcall(
        body, name=name,
        grid_spec=pltpu.PrefetchScalarGridSpec(
            num_scalar_prefetch=1, grid=(nb,), in_specs=[mine3] * len(full) + [half3] * len(halves),
            out_specs=out_spec),
        out_shape=out_shape, compiler_params=_cparams(("parallel",)),
    )(core, *full, *halves)


def _ada_fwd(c_all, w, b):
    def body(c_ref, w_ref, b_ref, o_ref):
        o_ref[...] = jnp.dot(c_ref[...], w_ref[...], precision=HI, preferred_element_type=F32) + b_ref[...]

    return _pcall(body, name="ada_fwd", out_shape=jax.ShapeDtypeStruct((c_all.shape[0], w.shape[1]), F32),
                  compiler_params=pltpu.CompilerParams(vmem_limit_bytes=VMEM_LIMIT))(c_all, w, b)


def _ada_bwd(c_all_t, d):
    def body(c_ref, d_ref, o_ref):
        o_ref[...] = jnp.dot(c_ref[...], d_ref[...], precision=HI, preferred_element_type=F32)

    return _pcall(body, name="ada_bwd", out_shape=jax.ShapeDtypeStruct((c_all_t.shape[0], d.shape[1]), F32),
                  compiler_params=pltpu.CompilerParams(vmem_limit_bytes=VMEM_LIMIT))(c_all_t, d)


def _sum_lead(x, name):
    p, r, n = x.shape
    br = _div(r, 512, 8)

    def body(x_ref, o_ref):
        acc = x_ref[0]
        for j in range(1, p):
            acc = acc + x_ref[j]
        o_ref[...] = acc

    return _pcall(
        body, name=name, grid=(r // br,), in_specs=[pl.BlockSpec((p, br, n), lambda i: (0, i, 0))],
        out_specs=pl.BlockSpec((br, n), lambda i: (i, 0)), out_shape=jax.ShapeDtypeStruct((r, n), F32),
        compiler_params=_cparams(("parallel",)),
    )(x)


def _adamw(w, g, m, v, name):
    shape = w.shape
    cols = shape[-1]
    w2, g2, m2, v2 = [t.reshape(-1, cols) for t in (w, g, m, v)]
    rows = w2.shape[0]
    br = _div(rows, max(8, (1 << 19) // cols // 8 * 8), 8)
    outs = _rows_fwd(_f_adamw, [(t, cols, 0) for t in (w2, g2, m2, v2)], [], [(cols, F32)] * 3, name=name, br=br)
    return [o.reshape(shape) for o in outs]


_BIG = (("w_in", 1), ("w_up", 1), ("w_down", 0), ("w_o", 0), ("w_rwkv_out", 0), ("w_att_out", 1), ("w2", 1), ("a2", 1),
        ("g2", 1))


_NEEDED_FIRST = ("w_in", "w_att_out", "w2", "a2", "g2")
_NEEDED_LATER = ("w_up", "w_down", "w_o", "w_rwkv_out")
_DONE_EARLY = ("w_up", "w_down", "w_o", "w_rwkv_out", "w_att_out")
_DONE_LATE = ("w_in", "w2", "a2", "g2")


def _cols_joined(t):
    return jnp.concatenate([t[j] for j in range(4)], axis=1)


def _cols_split(t):
    n = t.shape[1] // 4
    return jnp.stack([t[:, j * n:(j + 1) * n] for j in range(4)])


def _col_window(parts, lo, hi):
    out, pos = [], 0
    for t, w in parts:
        a, b = max(lo, pos), min(hi, pos + w)
        if a < b:
            out.append(t[:, a - pos:b - pos])
        pos += w
    return out[0] if len(out) == 1 else jnp.concatenate(out, axis=1)


def _rows_joined(t):
    return t.reshape(4 * t.shape[1], t.shape[2])


def _rows_split(t):
    return t.reshape(4, t.shape[0] // 4, t.shape[1])


def _step_to_scan(x, tgt, ada, wts):
    sh1, sc1, gt1, sh2, sc2, gt2 = ada
    br = 256
    grp = lax.broadcasted_iota(jnp.int32, (D, 128), 0) // 64 == lax.broadcasted_iota(jnp.int32, (D, 128), 1)
    e = grp.astype(F32)
    et = e.T
    w_in = [(wts["w_in"][j], wts["w_in"].shape[2]) for j in range(4)]
    w_att = _col_window(w_in, 0, N_ATT)
    w_rw = jnp.pad(_col_window(w_in, N_ATT, N_ATT + N_RW), ((0, 0), (0, N_RWP - N_RW)))
    w_gate = _col_window(w_in, N_ATT + N_RW, N_ATT + N_RW + N_GATE)
    mu = jnp.pad(wts["mu_shift"], ((0, 0), (0, N_RWP - N_RW)))
    wl = jnp.zeros((N_LORA, 3 * D), F32)
    wl = wl.at[0:64, 0:D].set(_cols_joined(wts["w2"]).astype(F32))
    wl = wl.at[64:128, D:2 * D].set(_cols_joined(wts["a2"]).astype(F32))
    wl = wl.at[128:288, 2 * D:3 * D].set(_cols_joined(wts["g2"]).astype(F32))
    pre1_c = [wts["norm1_w"], sc1, sh1]
    (h1,) = _rows_fwd(_f_pre, [(x, D, 0)], pre1_c, [(D, BF16), None], name="pre1_fwd", br=br)
    att_in = _mm(h1, w_att, name="mm_att_in")
    z = _mm(h1, w_rw, name="mm_rw_in")
    gate_in = _mm(h1, w_gate, name="mm_gate_in")
    att_o, att_l = [], []
    for g, (_, dil) in enumerate(ATT_PATTERNS):
        o, l = _att_fwd(att_in, g, dil)
        att_o.append(o)
        att_l.append(l)
    comb_rows = [(t, ATT_WIDTH, 0) for t in att_o + att_l]
    (att,) = _rows_fwd(_f_comb, comb_rows, [], [(ATT_WIDTH, BF16)], name="comb_fwd", br=br)
    y_att = _mm(att, wts["w_att_out"], b_chip=True, name="mm_att_out")
    zs = _shift_fwd(z, mu)
    rwpre_c = [wts["w0"], wts["a0"], wts["k_k"], wts["k_a"], wl, e, et]
    lw, km, aa, bb, gg = _rows_fwd(_f_rwpre, [(zs, N_RWP, 0)], rwpre_c,
                                   [None, (D, F32), (D, F32), None, (D, F32), (D, F32), (D, F32)],
                                   name="rwpre_fwd", br=br)
    return dict(x=x, tgt=tgt, wts=wts, br=br, e=e, et=et, gt1=gt1, sc2=sc2, sh2=sh2, gt2=gt2, w_att=w_att, w_rw=w_rw,
                w_gate=w_gate, mu=mu, pre1_c=pre1_c, h1=h1, att_in=att_in, z=z, gate_in=gate_in, comb_rows=comb_rows,
                att=att, y_att=y_att, zs=zs, rwpre_c=rwpre_c, lw=lw, km=km, aa=aa, bb=bb, gg=gg)


def _step_between_scans(st, y_raw, late):
    x, tgt, wts, br, e, et = st["x"], st["tgt"], st["wts"], st["br"], st["e"], st["et"]
    zs, km, gg, gate_in, y_att, att = st["zs"], st["km"], st["gg"], st["gate_in"], st["y_att"], st["att"]
    comb_rows, att_in = st["comb_rows"], st["att_in"]
    gt1, sc2, sh2, gt2 = st["gt1"], st["sc2"], st["sh2"], st["gt2"]
    w_up, w_ao = late["w_up"], wts["w_att_out"]
    w_down, w_o, w_ro = _rows_joined(late["w_down"]), _rows_joined(late["w_o"]), _rows_joined(late["w_rwkv_out"])
    post_rows = [(y_raw, D, 0), (zs, D, 0), (zs, D, 2), (km, D, 0), (gg, D, 0)]
    post_c = [wts["lnx_w"], wts["lnx_b"], wts["r_k"], e, et]
    (rw_out,) = _rows_fwd(_f_rwpost, post_rows, post_c, [(D, BF16)], name="rwpost_fwd", br=br)
    y_rw = _mm(rw_out, w_ro, name="mm_rw_out")
    mix_rows = [(gate_in, N_GATE, 0), (y_att, D, 0), (y_rw, D, 0)]
    (mix,) = _rows_fwd(_f_mix, mix_rows, [wts["b_gate"]], [(D, BF16)], name="mix_fwd", br=br)
    o = _mm(mix, w_o, name="mm_o")
    pre2_c = [gt1, wts["norm2_w"], sc2, sh2]
    x1, h2 = _rows_fwd(_f_pre2, [(x, D, 0), (o, D, 0)], pre2_c, [(D, F32), (D, BF16)], name="pre2_fwd", br=br)
    u = _mm(h2, w_up, b_chip=True, name="mm_up")
    act = _conv_fwd(u, wts["conv_w"], wts["conv_b"])
    f = _mm(act, w_down, name="mm_down")
    fin_rows = [(x1, D, 0), (f, D, 0), (tgt, D, 0)]
    fin_c = [gt2, wts["norm_f_w"]]

    def fin_fwd(*a):
        (l,) = _f_fin(*a)
        return (jnp.broadcast_to(jnp.sum(l, axis=0, keepdims=True), (8, 128)),)

    (loss_acc,) = _rows_fwd(fin_fwd, fin_rows, fin_c, [], name="fin_fwd", br=br, acc_shape=(8, 128))

    gw = {}
    dx1a, df, d_gt2, gw["norm_f_w"] = _rows_bwd(
        _f_fin, fin_rows, fin_c, [[]], wrt_rows=[0, 1], wrt_consts=[0, 1], drow_dtypes=[F32, BF16],
        name="fin_bwd", br=br, unit_cot=True)
    dact = _mm(df, w_down, tb=True, name="mm_dact")
    gw["w_down"] = _rows_split(_mm(act, df, ta=True, name="mm_dw_down"))
    du, gw["conv_w"], gw["conv_b"] = _conv_bwd(u, wts["conv_w"], wts["conv_b"], dact)
    dh2 = _mm(du, w_up, tb=True, b_chip=True, name="mm_dh2")
    gw["w_up"] = _mm(h2, du, ta=True, out_chip=True, name="mm_dw_up")
    dxa, do, d_gt1, gw["norm2_w"], d_sc2, d_sh2 = _rows_bwd(
        _f_pre2, [(x, D, 0), (o, D, 0)], pre2_c, [[(dx1a, D, 0)], [(dh2, D, 0)]], wrt_rows=[0, 1],
        wrt_consts=[0, 1, 2, 3], drow_dtypes=[F32, BF16], name="pre2_bwd", br=br)
    dmix = _mm(do, w_o, tb=True, name="mm_dmix")
    gw["w_o"] = _rows_split(_mm(mix, do, ta=True, name="mm_dw_o"))
    dgate, dya, dyr, gw["b_gate"] = _rows_bwd(
        _f_mix, mix_rows, [wts["b_gate"]], [[(dmix, D, 0)]], wrt_rows=[0, 1, 2], wrt_consts=[0],
        drow_dtypes=[BF16] * 3, name="mix_bwd", br=br)
    datt = _mm(dya, w_ao, tb=True, b_chip=True, name="mm_datt")
    gw["w_att_out"] = _mm(att, dya, ta=True, out_chip=True, name="mm_dw_att_out")
    drw = _mm(dyr, w_ro, tb=True, name="mm_drw")
    gw["w_rwkv_out"] = _rows_split(_mm(rw_out, dyr, ta=True, name="mm_dw_rw_out"))
    dcomb = _rows_bwd(_f_comb, comb_rows, [], [[(datt, ATT_WIDTH, 0)]], wrt_rows=list(range(6)), wrt_consts=[],
                      drow_dtypes=[F32] * 6, name="comb_bwd", br=br)
    datt_in = None
    for g, (_, dil) in enumerate(ATT_PATTERNS):
        datt_in = _att_bwd(att_in, g, dil, dcomb[g], dcomb[3 + g], datt_in)
    dy_raw, dr_p, dv_p, dkm_p, dgg, gw["lnx_w"], gw["lnx_b"], gw["r_k"], *recv_early = _rows_bwd(
        _f_rwpost, post_rows, post_c, [[(drw, D, 0)]], wrt_rows=[0, 1, 2, 3, 4], wrt_consts=[0, 1, 2],
        drow_dtypes=[F32] * 5, name="rwpost_bwd", br=br, comm=_SiblingHalves([gw[n] for n in _DONE_EARLY]))
    st.update(loss=loss_acc[0, 0], gw=gw, dxa=dxa, dgate=dgate, datt_in=datt_in,
              dy_raw=dy_raw, dr_p=dr_p, dv_p=dv_p, dkm_p=dkm_p, dgg=dgg, d_ada_late=(d_gt1, d_sh2, d_sc2, d_gt2),
              recv_early=recv_early)
    return st


def _chip_parts(grads, recv, names, core):
    return [_half_sum(lambda a, b: a + b, [g], [r], False, BF16, core, "reduce_add2_" + n)
            for g, r, n in zip(grads, recv, names)]


def _step_after_scan(st, scan_grads, core):
    x, br, gw, h1, zs = st["x"], st["br"], st["gw"], st["h1"], st["zs"]
    dr_s, dlw, dkm_s, dv_s, daa, dbb = scan_grads
    pre_cots = [[(st["dr_p"], D, 0), (dr_s, D, 0)], [(dlw, D, 0)], [(st["dkm_p"], D, 0), (dkm_s, D, 0)],
                [(st["dv_p"], D, 0), (dv_s, D, 0)], [(daa, D, 0)], [(dbb, D, 0)], [(st["dgg"], D, 0)]]
    dzs, gw["w0"], gw["a0"], gw["k_k"], gw["k_a"], dwl = _rows_bwd(
        _f_rwpre, [(zs, N_RWP, 0)], st["rwpre_c"], pre_cots, wrt_rows=[0], wrt_consts=[0, 1, 2, 3, 4],
        drow_dtypes=[F32], name="rwpre_bwd", br=128)
    gw["w2"], gw["a2"] = _cols_split(dwl[0:64, 0:D]), _cols_split(dwl[64:128, D:2 * D])
    gw["g2"] = _cols_split(dwl[128:288, 2 * D:3 * D])
    dz, dmu = _shift_bwd(st["z"], st["mu"], dzs)
    gw["mu_shift"] = dmu[:, :N_RW]
    datt_in, dgate = st["datt_in"], st["dgate"]
    dw_in = [(_mm(h1, datt_in, ta=True, name="mm_dw_att"), N_ATT), (_mm(h1, dz, ta=True, name="mm_dw_rw"), N_RW),
             (_mm(h1, dgate, ta=True, name="mm_dw_gate"), N_GATE)]
    shard = (N_ATT + N_RW + N_GATE) // 4
    gw["w_in"] = jnp.stack([_col_window(dw_in, j * shard, (j + 1) * shard) for j in range(4)])
    late = [gw[n] for n in _DONE_LATE]
    parts = _chip_parts(late, _run_comm(_SiblingHalves(late), "reduce_sib_late"), _DONE_LATE, core)
    dh1, slots_late = _mm_nt_sum([(datt_in, st["w_att"]), (dz, st["w_rw"]), (dgate, st["w_gate"])],
                                 comm=_ScatterToChips(parts), name="mm_dh1")
    grad_x, gw["norm1_w"], d_sc1, d_sh1 = _rows_bwd(
        _f_pre, [(x, D, 0)], st["pre1_c"], [[(dh1, D, 0)], [(st["dxa"], D, 0)]], wrt_rows=[0], wrt_consts=[0, 1, 2],
        drow_dtypes=[F32], name="pre1_bwd", br=br)
    d_gt1, d_sh2, d_sc2, d_gt2 = st["d_ada_late"]
    return st["loss"], grad_x, (d_sh1, d_sc1, d_gt1, d_sh2, d_sc2, d_gt2), gw, slots_late


_SMALL = ("b_ada", "norm1_w", "b_gate", "mu_shift", "w0", "a0", "k_k", "k_a", "r_k", "lnx_w", "lnx_b", "norm2_w",
          "conv_b", "norm_f_w")
_NAMES = ("w_ada", "b_ada", "norm1_w", "w_in", "b_gate", "mu_shift", "w0", "w2", "a0", "a2", "g2", "k_k", "k_a", "r_k",
          "lnx_w", "lnx_b", "w_att_out", "w_rwkv_out", "w_o", "norm2_w", "w_up", "conv_w", "conv_b", "w_down",
          "norm_f_w")


def kernel(x, c, w_ada, b_ada, norm1_w, w_in, b_gate, mu_shift, w0, w2, a0, a2, g2, k_k, k_a, r_k, lnx_w, lnx_b, w_att_out, w_rwkv_out, w_o, norm2_w, w_up, conv_w, conv_b, w_down, norm_f_w, loss_target, m_w_ada, m_b_ada, m_norm1_w, m_w_in, m_b_gate, m_mu_shift, m_w0, m_w2, m_a0, m_a2, m_g2, m_k_k, m_k_a, m_r_k, m_lnx_w, m_lnx_b, m_w_att_out, m_w_rwkv_out, m_w_o, m_norm2_w, m_w_up, m_conv_w, m_conv_b, m_w_down, m_norm_f_w, v_w_ada, v_b_ada, v_norm1_w, v_w_in, v_b_gate, v_mu_shift, v_w0, v_w2, v_a0, v_a2, v_g2, v_k_k, v_k_a, v_r_k, v_lnx_w, v_lnx_b, v_w_att_out, v_w_rwkv_out, v_w_o, v_norm2_w, v_w_up, v_conv_w, v_conv_b, v_w_down, v_norm_f_w):
    args = dict(locals())
    p, pm, pv = {}, {}, {}
    for name in _NAMES:
        for dst, key in ((p, name), (pm, "m_" + name), (pv, "v_" + name)):
            t = args[key]
            dst[name] = t.reshape(1, -1) if name in ("r_k", "norm_f_w") else t.reshape(t.shape[-2], t.shape[-1])
    xi, yi, ci = _me()
    chip = 2 * xi + yi
    dev = 4 * xi + 2 * yi + ci
    x2, tgt = x[0], loss_target[0]

    n_cw = 3 * (2 * D_FF // 4)
    vec = jnp.concatenate([c.reshape(-1), p["conv_w"].reshape(-1), jnp.zeros((8 * D - D - n_cw,), F32)]).reshape(8, D)
    g0 = _allgather8(vec, "gather_c").reshape(8, 8 * D)
    c_all = g0[:, :D]
    conv_w_full = jnp.concatenate([g0[2 * j, D:D + n_cw].reshape(3, -1) for j in range(4)], axis=1)
    n_ada = 6 * D // 4
    b_ada_sh = lax.dynamic_slice(p["b_ada"], (0, chip * n_ada), (1, n_ada))
    ada_sh = _ada_fwd(c_all, p["w_ada"], b_ada_sh)
    ga = _allgather8(ada_sh, "gather_ada")
    ada_all = jnp.concatenate([ga[2 * j] for j in range(4)], axis=1)
    ada_row = lax.dynamic_slice(ada_all, (dev, 0), (1, 6 * D))
    ada = [ada_row[:, j * D:(j + 1) * D] for j in range(6)]

    big = [n for n, _ in _BIG]
    shard = {n: p[n].astype(BF16) for n in big}
    wts = dict(zip(_NEEDED_FIRST, _run_comm(_GatherWeights([shard[n] for n in _NEEDED_FIRST]), "gather_w")))
    for n in _SMALL:
        wts[n] = p[n]
    wts["conv_w"] = conv_w_full
    core = ci.reshape(1).astype(jnp.int32)

    st = _step_to_scan(x2, tgt, ada, wts)
    y_raw, s0s, inverses, late = _scan_fwd(st["zs"], st["lw"], st["km"], st["aa"], st["bb"],
                                           _GatherWeights([shard[n] for n in _NEEDED_LATER]))
    st = _step_between_scans(st, y_raw, dict(zip(_NEEDED_LATER, late)))
    early = _chip_parts([st["gw"][n] for n in _DONE_EARLY], st["recv_early"], _DONE_EARLY, core)
    scan_grads, slots_early = _scan_bwd(st["zs"], st["lw"], st["km"], st["aa"], st["bb"], s0s, inverses,
                                        st["dy_raw"], _ScatterToChips(early))
    loss_part, grad_x, d_ada, gw, slots_late = _step_after_scan(st, scan_grads, core)

    small = [jnp.concatenate(d_ada, axis=1)] + [gw[n] for n in _SMALL[1:]] + [gw["conv_w"], loss_part.reshape(1, 1)]
    sizes = [t.size for t in small]
    flat = jnp.concatenate([t.reshape(-1) for t in small])
    npad = (-flat.shape[0]) % (8 * D)
    srows = (flat.shape[0] + npad) // D
    flat = jnp.concatenate([flat, jnp.zeros((npad,), F32)]).reshape(srows, D)
    parts = _allgather8(flat, "gather_small")
    tot = _sum_lead(parts, "sum_small").reshape(-1)
    pieces, pos = [], 0
    for sz in sizes:
        pieces.append(tot[pos:pos + sz])
        pos += sz
    grads = {}
    for n, piece in zip(_SMALL, pieces[:len(_SMALL)]):
        grads[n] = piece.reshape(p[n].shape)
    conv_w_grad = pieces[len(_SMALL)].reshape(3, 2 * D_FF)
    grads["conv_w"] = lax.dynamic_slice(conv_w_grad, (0, chip * (n_cw // 3)), (3, n_cw // 3))
    loss = pieces[-1][0]
    d_ada_all = parts[:, :6].reshape(8, 6 * D)
    grads["w_ada"] = _ada_bwd(c_all.T, lax.dynamic_slice(d_ada_all, (0, chip * n_ada), (8, n_ada)))

    order = _DONE_EARLY + _DONE_LATE
    reds = [_half_sum(lambda t: t[0] + t[1] + t[2] + t[3], [], [t], True, F32, core, "reduce_add4_" + n)
            for n, t in zip(order, list(slots_early) + list(slots_late))]
    for n, g in zip(order, _reduce_finish(reds, "reduce_sib2")):
        grads[n] = g

    outs_g, outs_d, outs_m, outs_v = [], [], [], []
    for name in _NAMES:
        g = grads[name]
        d, m, v = _adamw(p[name], g, pm[name], pv[name], "adamw_" + name)
        shape = args[name].shape
        outs_g.append(g.reshape(shape))
        outs_d.append(d.reshape(shape))
        outs_m.append(m.reshape(shape))
        outs_v.append(v.reshape(shape))
    return (loss, grad_x.reshape(x.shape), *outs_g, *outs_d, *outs_m, *outs_v)
```

```python
import functools

import jax
import jax.numpy as jnp
from jax import lax
from jax.experimental import pallas as pl
from jax.experimental.pallas import tpu as pltpu

F32 = jnp.float32
BF16 = jnp.bfloat16
HI = lax.Precision.HIGHEST
MESH = pl.DeviceIdType.MESH

D = 1024
ATT_PATTERNS = ((128, 1), (512, 4), (2048, 16))
ATT_BLOCK = 128
ATT_WIDTH = 512
N_ATT = 3 * 3 * ATT_WIDTH
N_RW = 3 * D + 64 + 64 + 160
N_RWP = 3456
N_LORA = N_RWP - 3 * D
N_GATE = 2 * D
D_FF = 2816
RMS_EPS = 1e-6
GN_EPS = 64e-5
SCAN_CHUNK = 64
SCAN_PAIRS = 8
NEG = -1e30
VMEM_LIMIT = 48 * 1024 * 1024

ADAM_LR, ADAM_B1, ADAM_B2, ADAM_EPS, ADAM_WD, ADAM_STEP = 0.001, 0.9, 0.999, 1e-08, 0.01, 10


def _pcall(body, **kw):
    return pl.pallas_call(body, **kw)


def _cparams(sem):
    return pltpu.CompilerParams(dimension_semantics=sem, vmem_limit_bytes=VMEM_LIMIT)


def _div(n, pref, mult):
    best = None
    d = mult
    while d <= min(n, pref):
        if n % d == 0:
            best = d
        d += mult
    return best if best else n


def _dg(a, b, ca, cb):
    return lax.dot_general(a.astype(BF16), b.astype(BF16), (((ca,), (cb,)), ((), ())), preferred_element_type=F32)


@jax.custom_vjp
def _nn(a, b):
    return _dg(a, b, 1, 0)


@jax.custom_vjp
def _nt(a, b):
    return _dg(a, b, 1, 1)


@jax.custom_vjp
def _tn(a, b):
    return _dg(a, b, 0, 0)


_nn.defvjp(lambda a, b: (_nn(a, b), (a, b)), lambda res, g: (_nt(g, res[1]), _tn(res[0], g)))
_nt.defvjp(lambda a, b: (_nt(a, b), (a, b)), lambda res, g: (_nn(g, res[1]), _tn(g, res[0])))
_tn.defvjp(lambda a, b: (_tn(a, b), (a, b)), lambda res, g: (_nt(res[1], g), _nn(res[0], g)))


def _bdg(a, b, ca, cb):
    return lax.dot_general(a.astype(BF16), b.astype(BF16), (((ca,), (cb,)), ((0,), (0,))), preferred_element_type=F32)


@jax.custom_vjp
def _bnn(a, b):
    return _bdg(a, b, 2, 1)


@jax.custom_vjp
def _bnt(a, b):
    return _bdg(a, b, 2, 2)


@jax.custom_vjp
def _btn(a, b):
    return _bdg(a, b, 1, 1)


_bnn.defvjp(lambda a, b: (_bnn(a, b), (a, b)), lambda res, g: (_bnt(g, res[1]), _btn(res[0], g)))
_bnt.defvjp(lambda a, b: (_bnt(a, b), (a, b)), lambda res, g: (_bnn(g, res[1]), _btn(g, res[0])))
_btn.defvjp(lambda a, b: (_btn(a, b), (a, b)), lambda res, g: (_bnt(res[1], g), _bnn(res[0], g)))


def _split2(x):
    hi = x.astype(BF16)
    lo = (x - hi.astype(F32)).astype(BF16)
    return hi, lo


def _hsum_impl(x, e, et):
    eb, etb = e.astype(BF16), et.astype(BF16)
    s = jnp.dot(x.astype(BF16), eb, preferred_element_type=F32)
    shi, slo = _split2(s)
    return jnp.dot(shi, etb, preferred_element_type=F32) + jnp.dot(slo, etb, preferred_element_type=F32)


@jax.custom_vjp
def _hsum(x, e, et):
    return _hsum_impl(x, e, et)


_hsum.defvjp(lambda x, e, et: (_hsum_impl(x, e, et), (e, et)),
             lambda res, g: (_hsum_impl(g, res[0], res[1]), jnp.zeros_like(res[0]), jnp.zeros_like(res[1])))


def _mm(a, b, *, ta=False, tb=False, out_dtype=F32, add=None, b_chip=False, out_chip=False, comm=None, name):
    riding = _NOTHING if comm is None else comm
    nc = riding.n
    if ta:
        kdim, m = a.shape
    else:
        m, kdim = a.shape
    if b_chip:
        n = b.shape[1] if tb else 4 * b.shape[2]
    else:
        n = b.shape[0] if tb else b.shape[1]
    tm, tn, tk = _div(m, 1536, 128), _div(n, 1536, 128), _div(kdim, 1408, 128)
    if b_chip and tb:
        tk = kdim // 4
    if (b_chip and not tb) or out_chip:
        tn = n // 4
    nk = kdim // tk
    ca, cb = (0 if ta else 1), (1 if tb else 0)

    nin = 2 if add is None else 3
    gi, gj = m // tm, n // tn

    def body(*refs):
        a_ref, b_ref = refs[0], refs[1]
        add_ref = None if add is None else refs[2]
        o_ref = refs[nin + nc]
        step = (pl.program_id(0) * gj + pl.program_id(1)) * nk + pl.program_id(2)
        before, after = _comm_phases(riding, refs[nin:nin + nc] + refs[nin + nc + 1:nin + 2 * nc + 1]
                                     + refs[nin + 2 * nc + 1 + (nk > 1):], gi * gj * nk, step)
        before()
        part = lax.dot_general(a_ref[...], b_ref[...], (((ca,), (cb,)), ((), ())), preferred_element_type=F32)

        def finish(r):
            if add_ref is not None:
                r = r + add_ref[...]
            o_ref[...] = r.astype(o_ref.dtype)

        if nk == 1:
            finish(part)
            after()
            return
        acc = refs[nin + 2 * nc + 1]
        k = pl.program_id(2)

        @pl.when(k == 0)
        def _():
            acc[...] = part

        @pl.when(k > 0)
        def _():
            acc[...] += part

        @pl.when(k == nk - 1)
        def _():
            finish(acc[...])

        after()

    a_spec = pl.BlockSpec((tk, tm), lambda i, j, k: (k, i)) if ta else pl.BlockSpec((tm, tk), lambda i, j, k: (i, k))
    if b_chip:
        b_spec = (pl.BlockSpec((None, tn, tk), lambda i, j, k: (k, j, 0)) if tb
                  else pl.BlockSpec((None, tk, tn), lambda i, j, k: (j, k, 0)))
    else:
        b_spec = pl.BlockSpec((tn, tk), lambda i, j, k: (j, k)) if tb else pl.BlockSpec((tk, tn), lambda i, j, k: (k, j))
    in_specs = [a_spec, b_spec]
    args = [a, b]
    if add is not None:
        in_specs.append(pl.BlockSpec((tm, tn), lambda i, j, k: (i, j)))
        args.append(add)
    if out_chip:
        out_spec = pl.BlockSpec((None, tm, tn), lambda i, j, k: (j, i, 0))
        out_shape = jax.ShapeDtypeStruct((4, m, tn), out_dtype)
    else:
        out_spec = pl.BlockSpec((tm, tn), lambda i, j, k: (i, j))
        out_shape = jax.ShapeDtypeStruct((m, n), out_dtype)
    res = _pcall(
        body, name=name, grid=(gi, gj, nk), in_specs=in_specs + [_HBM] * nc, out_specs=[out_spec] + [_HBM] * nc,
        out_shape=[out_shape] + riding.out_shape,
        scratch_shapes=([] if nk == 1 else [pltpu.VMEM((tm, tn), F32)]) + riding.sems,
        compiler_params=_cparams(("arbitrary",) * 3 if nc else ("parallel", "parallel", "arbitrary")),
    )(*args, *riding.ins)
    return res[0] if comm is None else (res[0], res[1:])


def _mm_sum(pairs, *, comm, name):
    m, n = pairs[0][0].shape[0], pairs[0][1].shape[1]
    tm, tn = _div(m, 1024, 128), _div(n, 1024, 128)
    tks = [_div(a.shape[1], 1408, 128) for a, _ in pairs]
    nks = [a.shape[1] // tk for (a, _), tk in zip(pairs, tks)]
    offs = [sum(nks[:p]) for p in range(len(pairs))]
    total, npair, nc = sum(nks), len(pairs), comm.n
    gi, gj = m // tm, n // tn

    def body(*refs):
        o_ref, acc = refs[2 * npair + nc], refs[2 * npair + 2 * nc + 1]
        k = pl.program_id(2)
        step = (pl.program_id(0) * gj + pl.program_id(1)) * total + k
        before, after = _comm_phases(comm, refs[2 * npair:2 * npair + nc]
                                     + refs[2 * npair + nc + 1:2 * npair + 2 * nc + 1]
                                     + refs[2 * npair + 2 * nc + 2:], gi * gj * total, step)
        before()
        for p in range(npair):
            def partial_product(p=p):
                part = jnp.dot(refs[2 * p][...], refs[2 * p + 1][...], preferred_element_type=F32)
                if p == 0:
                    @pl.when(k == 0)
                    def _():
                        acc[...] = part

                    @pl.when(k > 0)
                    def _():
                        acc[...] += part
                else:
                    acc[...] += part

            pl.when(jnp.logical_and(k >= offs[p], k < offs[p] + nks[p]))(partial_product)

        @pl.when(k == total - 1)
        def _():
            o_ref[...] = acc[...]

        after()

    def specs(tk, off, nk):
        def kb(k):
            return jnp.clip(k - off, 0, nk - 1)
        return [pl.BlockSpec((tm, tk), lambda i, j, k: (i, kb(k))), pl.BlockSpec((tk, tn), lambda i, j, k: (kb(k), j))]

    in_specs, args = [], []
    for (a, b), tk, off, nk in zip(pairs, tks, offs, nks):
        in_specs += specs(tk, off, nk)
        args += [a, b]
    res = _pcall(
        body, name=name, grid=(gi, gj, total), in_specs=in_specs + [_HBM] * nc,
        out_specs=[pl.BlockSpec((tm, tn), lambda i, j, k: (i, j))] + [_HBM] * nc,
        out_shape=[jax.ShapeDtypeStruct((m, n), F32)] + comm.out_shape,
        scratch_shapes=[pltpu.VMEM((tm, tn), F32)] + comm.sems,
        compiler_params=_cparams(("arbitrary",) * 3),
    )(*args, *comm.ins)
    return res[0], res[1:]


def _row_spec(br, w, cb):
    return pl.BlockSpec((br, w), lambda i: (i, cb))


def _const_spec(shape):
    return pl.BlockSpec(shape, lambda i: (0,) * len(shape))


def _rows_fwd(fn, rows, consts, outs, *, name, br, acc_shape=None):
    s = rows[0][0].shape[0]
    nr, nc = len(rows), len(consts)
    kept = [k for k, o in enumerate(outs) if o is not None]

    def body(*refs):
        xs = [r[...].astype(F32) for r in refs[:nr]]
        cs = [c[...] for c in refs[nr:nr + nc]]
        res = fn(*xs, *cs)
        orefs = refs[nr + nc:]
        for j, k in enumerate(kept):
            orefs[j][...] = res[k].astype(orefs[j].dtype)
        if acc_shape is not None:
            acc_ref = orefs[len(kept)]

            @pl.when(pl.program_id(0) == 0)
            def _():
                acc_ref[...] = jnp.zeros_like(acc_ref)

            acc_ref[...] += res[len(outs)]

    in_specs = [_row_spec(br, w, cb) for (_, w, cb) in rows] + [_const_spec(c.shape) for c in consts]
    out_specs = [_row_spec(br, outs[k][0], 0) for k in kept]
    out_shape = [jax.ShapeDtypeStruct((s, outs[k][0]), outs[k][1]) for k in kept]
    if acc_shape is not None:
        out_specs.append(_const_spec(acc_shape))
        out_shape.append(jax.ShapeDtypeStruct(acc_shape, F32))
    return _pcall(
        body, name=name, grid=(pl.cdiv(s, br),), in_specs=in_specs, out_specs=out_specs, out_shape=out_shape,
        compiler_params=_cparams(("arbitrary",)),
    )(*[r[0] for r in rows], *consts)


def _rows_bwd(fn, rows, consts, cots, *, wrt_rows, wrt_consts, drow_dtypes, name, br, unit_cot=False, comm=None):
    comm = _NOTHING if comm is None else comm
    ncomm = comm.n
    nout = len(wrt_rows) + len(wrt_consts)
    s = rows[0][0].shape[0]
    nr, nc = len(rows), len(consts)
    flat_cots = [c for lst in cots for c in lst]
    ncot = len(flat_cots)

    def body(*refs):
        xs = [r[...].astype(F32) for r in refs[:nr]]
        cs = [c[...] for c in refs[nr:nr + nc]]
        cvals = [c[...].astype(F32) for c in refs[nr + nc:nr + nc + ncot]]
        orefs = refs[nr + nc + ncot + ncomm:]
        before, after = _comm_phases(comm, refs[nr + nc + ncot:nr + nc + ncot + ncomm] + orefs[nout:], s // br)
        before()

        def g(*d):
            xs2, cs2 = list(xs), list(cs)
            for j, k in enumerate(wrt_rows):
                xs2[k] = d[j]
            for j, k in enumerate(wrt_consts):
                cs2[k] = d[len(wrt_rows) + j]
            return tuple(fn(*xs2, *cs2))

        prim = [xs[k] for k in wrt_rows] + [cs[k] for k in wrt_consts]
        outs, vjp = jax.vjp(g, *prim)
        ct = []
        pos = 0
        for o, lst in zip(outs, cots):
            if unit_cot:
                ct.append(jnp.ones_like(o))
                continue
            acc = jnp.zeros_like(o)
            for _ in lst:
                acc = acc + cvals[pos]
                pos += 1
            ct.append(acc)
        grads = vjp(tuple(ct))
        for j in range(len(wrt_rows)):
            orefs[j][...] = grads[j].astype(orefs[j].dtype)

        @pl.when(pl.program_id(0) == 0)
        def _():
            for j in range(len(wrt_consts)):
                oref = orefs[len(wrt_rows) + j]
                oref[...] = jnp.zeros_like(oref)

        for j in range(len(wrt_consts)):
            orefs[len(wrt_rows) + j][...] += grads[len(wrt_rows) + j]
        after()

    in_specs = ([_row_spec(br, w, cb) for (_, w, cb) in rows] + [_const_spec(c.shape) for c in consts]
                + [_row_spec(br, w, cb) for (_, w, cb) in flat_cots] + [_HBM] * ncomm)
    out_specs = ([_row_spec(br, rows[k][1], 0) for k in wrt_rows] + [_const_spec(consts[k].shape) for k in wrt_consts]
                 + [_HBM] * ncomm)
    out_shape = ([jax.ShapeDtypeStruct((s, rows[k][1]), dt) for k, dt in zip(wrt_rows, drow_dtypes)]
                 + [jax.ShapeDtypeStruct(consts[k].shape, F32) for k in wrt_consts] + comm.out_shape)
    return _pcall(
        body, name=name, grid=(s // br,), in_specs=in_specs, out_specs=out_specs, out_shape=out_shape,
        scratch_shapes=comm.sems, compiler_params=_cparams(("arbitrary",)),
    )(*[r[0] for r in rows], *consts, *[c[0] for c in flat_cots], *comm.ins)


def _rms(x, w):
    return x * lax.rsqrt(jnp.mean(x * x, axis=-1, keepdims=True) + RMS_EPS) * w


def _softplus(x):
    return jnp.maximum(x, 0.0) + jnp.log(1.0 + jnp.exp(-jnp.abs(x)))


def _f_pre(x, nw, sc, sh):
    return _rms(x, nw) * (1.0 + sc) + sh, x


def _f_pre2(x, o, gt, nw, sc, sh):
    x1 = x + gt * o
    return x1, _rms(x1, nw) * (1.0 + sc) + sh


def _f_fin(x1, f, tgt, gt, nfw):
    y = _rms(x1 + gt * f, nfw)
    return (0.5 * jnp.mean(jnp.square(y - tgt), axis=-1, keepdims=True),)


def _f_comb(o1, o2, o3, l1, l2, l3):
    m = lax.stop_gradient(jnp.maximum(jnp.maximum(l1, l2), l3))
    e1, e2, e3 = jnp.exp(l1 - m), jnp.exp(l2 - m), jnp.exp(l3 - m)
    return ((e1 * o1 + e2 * o2 + e3 * o3) / (e1 + e2 + e3),)


def _f_rwpre(zs, w0, a0, k_k, k_a, wl, e, et):
    r, k, v, zl = zs[:, 0:D], zs[:, D:2 * D], zs[:, 2 * D:3 * D], zs[:, 3 * D:N_RWP]
    lane = lax.broadcasted_iota(jnp.int32, zl.shape, 1)
    t = jnp.where(lane < 64, jnp.tanh(zl), jnp.where(lane < 128, zl, jnp.where(lane < 288, jax.nn.sigmoid(zl), 0.0)))
    lo = _nn(t[:, 0:128], wl[0:128, 0:2 * D])
    g = _nn(t[:, 128:N_LORA], wl[128:N_LORA, 2 * D:3 * D])
    w_log = -_softplus(-(w0 + lo[:, 0:D])) - 0.5
    lw = -jnp.exp(w_log)
    a = jax.nn.sigmoid(a0 + lo[:, D:2 * D])
    k_mod = k * (1.0 + (a - 1.0) * k_a)
    kk = k * k_k
    kk = kk / jnp.maximum(jnp.sqrt(_hsum(kk * kk, e, et)), 1e-12)
    return r, lw, k_mod, v, -kk, kk * a, g


def _f_rwpost(y, r, v, k_mod, g, lnx_w, lnx_b, r_k, e, et):
    mean = _hsum(y, e, et) * (1.0 / 64)
    yc = y - mean
    var = _hsum(yc * yc, e, et) * (1.0 / 64)
    yn = yc * lax.rsqrt(var + GN_EPS) * lnx_w + lnx_b
    bonus = _hsum(r * k_mod * r_k, e, et) * v
    return ((yn + bonus) * g,)


def _f_mix(gi, ya, yr, bg):
    gate = jax.nn.sigmoid(gi + bg)
    return (gate[:, 0:D] * ya + gate[:, D:2 * D] * yr,)


def _f_adamw(w, g, m, v):
    m = ADAM_B1 * m + (1.0 - ADAM_B1) * g
    v = ADAM_B2 * v + (1.0 - ADAM_B2) * jnp.square(g)
    m_hat = m / (1.0 - ADAM_B1 ** ADAM_STEP)
    v_hat = v / (1.0 - ADAM_B2 ** ADAM_STEP)
    return -ADAM_LR * (m_hat / (jnp.sqrt(v_hat) + ADAM_EPS) + ADAM_WD * w), m, v


def _down(x, k):
    row = lax.broadcasted_iota(jnp.int32, x.shape, 0)
    return jnp.where(row < k, 0.0, pltpu.roll(x, k, 0))


def _up(x, k):
    n = x.shape[0]
    row = lax.broadcasted_iota(jnp.int32, x.shape, 0)
    return jnp.where(row >= n - k, 0.0, pltpu.roll(x, n - k, 0))


def _col_spec(s, w, off=0):
    return pl.BlockSpec((s, w), lambda j: (0, j + off))


def _shift_fwd(z, mu):
    s, n = z.shape

    def body(z_ref, mu_ref, o_ref):
        zz = z_ref[...]
        o_ref[...] = zz + (_down(zz, 1) - zz) * mu_ref[...]

    return _pcall(
        body, name="shift_fwd", grid=(n // 128,), in_specs=[_col_spec(s, 128), _col_spec(1, 128)],
        out_specs=_col_spec(s, 128), out_shape=jax.ShapeDtypeStruct((s, n), F32),
        compiler_params=_cparams(("parallel",)),
    )(z, mu)


def _shift_bwd(z, mu, dzs):
    s, n = z.shape

    def body(z_ref, mu_ref, d_ref, dz_ref, dmu_ref):
        zz, d, m = z_ref[...], d_ref[...], mu_ref[...]
        dm = d * m
        dz_ref[...] = (d - dm + _up(dm, 1)).astype(dz_ref.dtype)
        dmu_ref[...] = jnp.sum(d * (_down(zz, 1) - zz), axis=0, keepdims=True)

    return _pcall(
        body, name="shift_bwd", grid=(n // 128,), in_specs=[_col_spec(s, 128), _col_spec(1, 128), _col_spec(s, 128)],
        out_specs=[_col_spec(s, 128), _col_spec(1, 128)],
        out_shape=[jax.ShapeDtypeStruct((s, n), BF16), jax.ShapeDtypeStruct((1, n), F32)],
        compiler_params=_cparams(("parallel",)),
    )(z, mu, dzs)


def _conv3(x, w_ref, b_ref):
    return b_ref[...] + w_ref[0:1, :] * _down(x, 2) + w_ref[1:2, :] * _down(x, 1) + w_ref[2:3, :] * x


def _conv_fwd(u, cw, cb):
    s = u.shape[0]
    nb = D_FF // 128

    def body(ug_ref, uv_ref, wg_ref, wv_ref, bg_ref, bv_ref, o_ref):
        gate = _conv3(ug_ref[...], wg_ref, bg_ref)
        val = _conv3(uv_ref[...], wv_ref, bv_ref)
        o_ref[...] = (gate * jax.nn.sigmoid(gate) * val).astype(o_ref.dtype)

    return _pcall(
        body, name="conv_fwd", grid=(nb,),
        in_specs=[_col_spec(s, 128), _col_spec(s, 128, nb), _col_spec(3, 128), _col_spec(3, 128, nb),
                  _col_spec(1, 128), _col_spec(1, 128, nb)],
        out_specs=_col_spec(s, 128), out_shape=jax.ShapeDtypeStruct((s, D_FF), BF16),
        compiler_params=_cparams(("parallel",)),
    )(u, u, cw, cw, cb, cb)


def _conv_bwd(u, cw, cb, dact):
    s = u.shape[0]
    nb = D_FF // 128

    def half(x, d, w_ref, du_ref, dw_ref, db_ref):
        x1, x2 = _down(x, 1), _down(x, 2)
        du_ref[...] = (w_ref[2:3, :] * d + w_ref[1:2, :] * _up(d, 1) + w_ref[0:1, :] * _up(d, 2)).astype(du_ref.dtype)
        dw_ref[0:1, :] = jnp.sum(d * x2, axis=0, keepdims=True)
        dw_ref[1:2, :] = jnp.sum(d * x1, axis=0, keepdims=True)
        dw_ref[2:3, :] = jnp.sum(d * x, axis=0, keepdims=True)
        db_ref[...] = jnp.sum(d, axis=0, keepdims=True)

    def body(ug_ref, uv_ref, wg_ref, wv_ref, bg_ref, bv_ref, da_ref,
             dug_ref, duv_ref, dwg_ref, dwv_ref, dbg_ref, dbv_ref):
        ug, uv, da = ug_ref[...], uv_ref[...], da_ref[...]
        gate = _conv3(ug, wg_ref, bg_ref)
        val = _conv3(uv, wv_ref, bv_ref)
        sg = jax.nn.sigmoid(gate)
        dgate = da * val * sg * (1.0 + gate * (1.0 - sg))
        dval = da * gate * sg
        half(ug, dgate, wg_ref, dug_ref, dwg_ref, dbg_ref)
        half(uv, dval, wv_ref, duv_ref, dwv_ref, dbv_ref)

    dug, duv, dwg, dwv, dbg, dbv = _pcall(
        body, name="conv_bwd", grid=(nb,),
        in_specs=[_col_spec(s, 128), _col_spec(s, 128, nb), _col_spec(3, 128), _col_spec(3, 128, nb),
                  _col_spec(1, 128), _col_spec(1, 128, nb), _col_spec(s, 128)],
        out_specs=[_col_spec(s, 128), _col_spec(s, 128), _col_spec(3, 128), _col_spec(3, 128),
                   _col_spec(1, 128), _col_spec(1, 128)],
        out_shape=[jax.ShapeDtypeStruct((s, D_FF), BF16), jax.ShapeDtypeStruct((s, D_FF), BF16),
                   jax.ShapeDtypeStruct((3, D_FF), F32), jax.ShapeDtypeStruct((3, D_FF), F32),
                   jax.ShapeDtypeStruct((1, D_FF), F32), jax.ShapeDtypeStruct((1, D_FF), F32)],
        compiler_params=_cparams(("parallel",)),
    )(u, u, cw, cw, cb, cb, dact)
    return (jnp.concatenate([dug, duv], axis=1), jnp.concatenate([dwg, dwv], axis=1),
            jnp.concatenate([dbg, dbv], axis=1))


ATT_BATCH = 4


def _att_batch(q, kp, kc, vp, vc, first):
    ma = lax.broadcasted_iota(jnp.int32, (1, ATT_BLOCK, 128), 2) < 64
    qs = jnp.concatenate([jnp.where(ma, q, 0.0), jnp.where(ma, 0.0, q)], axis=1)
    qi = lax.broadcasted_iota(jnp.int32, (1, 2 * ATT_BLOCK, ATT_BLOCK), 1) & (ATT_BLOCK - 1)
    kj = lax.broadcasted_iota(jnp.int32, (1, 2 * ATT_BLOCK, ATT_BLOCK), 2)
    okp = kj >= qi + jnp.where(first, 2 * ATT_BLOCK, 0)
    okc = kj <= qi
    sp = jnp.where(okp, _bnt(qs, kp) * 0.125, NEG)
    sc = jnp.where(okc, _bnt(qs, kc) * 0.125, NEG)
    m = lax.stop_gradient(jnp.maximum(jnp.max(sp, axis=-1, keepdims=True), jnp.max(sc, axis=-1, keepdims=True)))
    pp, pc = jnp.exp(sp - m), jnp.exp(sc - m)
    den = jnp.sum(pp, axis=-1, keepdims=True) + jnp.sum(pc, axis=-1, keepdims=True)
    o_s = (_bnn(pp, vp) + _bnn(pc, vc)) / den
    l_s = jnp.broadcast_to(m + jnp.log(den), o_s.shape)
    return (jnp.where(ma, o_s[:, :ATT_BLOCK], o_s[:, ATT_BLOCK:]), jnp.where(ma, l_s[:, :ATT_BLOCK], l_s[:, ATT_BLOCK:]))


def _att_pairs_per_step(dil):
    return ATT_BATCH if dil == 1 else 1


def _att_specs(g, dil):
    rows, pp = ATT_BLOCK * dil, _att_pairs_per_step(dil)

    def cur(slot):
        return pl.BlockSpec((rows, 128 * pp), lambda n, p: (n, (g * 3 + slot) * (4 // pp) + p))

    def prev(slot):
        return pl.BlockSpec((rows, 128 * pp), lambda n, p: (jnp.maximum(n - 1, 0), (g * 3 + slot) * (4 // pp) + p))

    return [cur(0), prev(1), cur(1), prev(2), cur(2)]


def _att_out_spec(dil):
    return pl.BlockSpec((ATT_BLOCK * dil, 128 * _att_pairs_per_step(dil)), lambda n, p: (n, p))


def _att_grid(s, dil):
    return (s // (ATT_BLOCK * dil), 4 // _att_pairs_per_step(dil))


def _att_windows(i, dil):
    if dil == 1:
        return [(pl.ds(0, ATT_BLOCK), pl.ds(128 * j, 128)) for j in range(ATT_BATCH)]
    return [(pl.ds(i * ATT_BATCH + j, ATT_BLOCK, stride=dil), pl.ds(0, 128)) for j in range(ATT_BATCH)]


def _att_fwd(att_in, g, dil):
    s = att_in.shape[0]

    def body(q_ref, kp_ref, kc_ref, vp_ref, vc_ref, o_ref, l_ref):
        first = pl.program_id(0) == 0

        def one(i, carry):
            win = _att_windows(i, dil)
            vals = [jnp.stack([ref[w] for w in win]) for ref in (q_ref, kp_ref, kc_ref, vp_ref, vc_ref)]
            o, l = _att_batch(*vals, first)
            for j, w in enumerate(win):
                o_ref[w] = o[j]
                l_ref[w] = l[j]
            return carry

        lax.fori_loop(0, max(1, dil // ATT_BATCH), one, 0)

    return _pcall(
        body, name=f"att_fwd{g}", grid=_att_grid(s, dil), in_specs=_att_specs(g, dil),
        out_specs=[_att_out_spec(dil)] * 2, out_shape=[jax.ShapeDtypeStruct((s, ATT_WIDTH), F32)] * 2,
        compiler_params=_cparams(("parallel", "parallel")),
    )(att_in, att_in, att_in, att_in, att_in)


def _att_bwd(att_in, g, dil, do, dl, acc):
    s = att_in.shape[0]

    def body(q_ref, kp_ref, kc_ref, vp_ref, vc_ref, do_ref, dl_ref, dq_ref, dkp_ref, dkc_ref, dvp_ref, dvc_ref):
        first = pl.program_id(0) == 0

        def one(i, carry):
            win = _att_windows(i, dil)
            vals = [jnp.stack([ref[w] for w in win]) for ref in (q_ref, kp_ref, kc_ref, vp_ref, vc_ref)]
            _, vjp = jax.vjp(lambda *a: _att_batch(*a, first), *vals)
            grads = vjp((jnp.stack([do_ref[w] for w in win]), jnp.stack([dl_ref[w] for w in win])))
            for ref, gr in zip((dq_ref, dkp_ref, dkc_ref, dvp_ref, dvc_ref), grads):
                for j, w in enumerate(win):
                    ref[w] = gr[j]
            return carry

        lax.fori_loop(0, max(1, dil // ATT_BATCH), one, 0)

    dq, dkp, dkc, dvp, dvc = _pcall(
        body, name=f"att_bwd{g}", grid=_att_grid(s, dil), in_specs=_att_specs(g, dil) + [_att_out_spec(dil)] * 2,
        out_specs=[_att_out_spec(dil)] * 5, out_shape=[jax.ShapeDtypeStruct((s, ATT_WIDTH), F32)] * 5,
        compiler_params=_cparams(("parallel", "parallel")),
    )(att_in, att_in, att_in, att_in, att_in, do, dl)

    unit, rb = ATT_BLOCK * dil, 1024
    steps = s // rb
    within = unit < rb

    def shifted(cur_ref, next_ref, has_next):
        nxt = jnp.where(has_next, next_ref[...], 0.0)
        return jnp.concatenate([cur_ref[unit:, :], nxt], axis=0) if within else nxt

    def cbody(dq_ref, dkc_ref, dkp_ref, dkn_ref, dvc_ref, dvp_ref, dvn_ref, *rest):
        o_ref = rest[-1]
        has_next = pl.program_id(0) + (1 if within else unit // rb) < steps
        o_ref[:, 0:ATT_WIDTH] = dq_ref[...].astype(BF16)
        o_ref[:, ATT_WIDTH:2 * ATT_WIDTH] = (dkc_ref[...] + shifted(dkp_ref, dkn_ref, has_next)).astype(BF16)
        o_ref[:, 2 * ATT_WIDTH:3 * ATT_WIDTH] = (dvc_ref[...] + shifted(dvp_ref, dvn_ref, has_next)).astype(BF16)

    cur = pl.BlockSpec((rb, ATT_WIDTH), lambda i: (i, 0))
    if within:
        nxt = pl.BlockSpec((unit, ATT_WIDTH), lambda i: (jnp.minimum((i + 1) * (rb // unit), s // unit - 1), 0))
    else:
        nxt = pl.BlockSpec((rb, ATT_WIDTH), lambda i: (jnp.minimum(i + unit // rb, steps - 1), 0))
    carried = [] if acc is None else [acc]
    return _pcall(
        cbody, name=f"att_bwd_sum{g}", grid=(steps,),
        in_specs=[cur, cur, cur, nxt, cur, cur, nxt] + [pl.BlockSpec(memory_space=pl.ANY)] * len(carried),
        out_specs=pl.BlockSpec((rb, 3 * ATT_WIDTH), lambda i: (i, g)),
        out_shape=jax.ShapeDtypeStruct((s, N_ATT), BF16), input_output_aliases={7: 0} if carried else {},
        compiler_params=_cparams(("parallel",)),
    )(dq, dkc, dkp, dkp, dvc, dvp, dvp, *carried)


def _unit_lower_inverse_impl(n):
    eye = (lax.broadcasted_iota(jnp.int32, (1,) + n.shape[1:], 1)
           == lax.broadcasted_iota(jnp.int32, (1,) + n.shape[1:], 2))
    t = jnp.where(eye, 1.0, 0.0) + n
    pw = n
    for _ in range(5):
        pw = _bnn(pw, pw)
        t = t + _bnn(t, pw)
    return t


@jax.custom_vjp
def _unit_lower_inverse(n):
    return _unit_lower_inverse_impl(n)


def _unit_lower_inverse_fwd(n):
    t = _unit_lower_inverse_impl(n)
    return t, t


_unit_lower_inverse.defvjp(_unit_lower_inverse_fwd, lambda t, g: (_bnt(_btn(t, g), t),))


@jax.custom_vjp
def _known_inverse(n, t):
    return t


_known_inverse.defvjp(lambda n, t: (t, t), lambda t, g: (_bnt(_btn(t, g), t), jnp.zeros_like(t)))


def _scan_chunk(r, lw, k, v, a, b, s0, inverse):
    c = SCAN_CHUNK
    p = s0.shape[0]
    ri = lax.broadcasted_iota(jnp.int32, (c, c), 0)
    ci = lax.broadcasted_iota(jnp.int32, (c, c), 1)
    cum = jnp.dot((ci <= ri).astype(F32), lw, precision=HI, preferred_element_type=F32)
    tot = jnp.sum(lw, axis=0, keepdims=True)
    ma = (lax.broadcasted_iota(jnp.int32, (c, 128 * p), 1) & 127) < 64

    def pairs(x):
        return jnp.concatenate([x[None, :, 128 * j:128 * (j + 1)] for j in range(p)], axis=0)

    def stack(x):
        return jnp.concatenate([pairs(jnp.where(ma, x, 0.0)), pairs(jnp.where(ma, 0.0, x))], axis=1)

    einv, eend = jnp.exp(-cum), jnp.exp(tot - cum)
    ra, aa = stack(r * jnp.exp(cum)), stack(a * jnp.exp(cum - lw))
    bi, ki, be, ke, vs = stack(b * einv), stack(k * einv), stack(b * eend), stack(k * eend), stack(v)
    r2 = lax.broadcasted_iota(jnp.int32, (1, 2 * c, 2 * c), 1)
    c2 = lax.broadcasted_iota(jnp.int32, (1, 2 * c, 2 * c), 2)
    same = (r2 >= c) == (c2 >= c)
    strict = jnp.logical_and(same, c2 < r2)
    incl = jnp.logical_and(same, c2 <= r2)
    s0 = jnp.where(same, s0, 0.0)
    prod = _bnt(jnp.concatenate([aa, ra], axis=1), jnp.concatenate([bi, ki], axis=1))
    a_ab = jnp.where(strict, prod[:, :2 * c, :2 * c], 0.0)
    a_ak = jnp.where(strict, prod[:, :2 * c, 2 * c:], 0.0)
    a_rb = jnp.where(incl, prod[:, 2 * c:, :2 * c], 0.0)
    a_rk = jnp.where(incl, prod[:, 2 * c:, 2 * c:], 0.0)
    t = inverse(a_ab)
    u = _bnn(t, _bnt(aa, s0) + _bnn(a_ak, vs))
    uv = jnp.concatenate([u, vs], axis=1)
    ys = _bnt(ra, s0) + _bnn(jnp.concatenate([a_rb, a_rk], axis=2), uv)
    s1 = s0 * pairs(jnp.exp(tot)) + _btn(uv, jnp.concatenate([be, ke], axis=1))
    y3 = ys[:, :c] + ys[:, c:]
    return (jnp.concatenate([y3[j] for j in range(p)], axis=1), s1), t


def _scan_specs(rev, n):
    def at(i):
        return n - 1 - i if rev else i

    def cm(cb):
        return pl.BlockSpec((SCAN_CHUNK, D), lambda i: (at(i), cb))

    return cm, pl.BlockSpec((1, SCAN_PAIRS, 128, 128), lambda i: (at(i), 0, 0, 0))


def _comm_phases(comm, refs, n, step=None):
    k = comm.n
    srcs, outs, sems = refs[:k], refs[k:2 * k], refs[2 * k:]
    i = pl.program_id(0) if step is None else step

    def before():
        @pl.when(i == 0)
        def _():
            comm.first(srcs, outs, sems)

    def after():
        if comm.mid is not None:
            @pl.when(i == (3 * n) // 4)
            def _():
                comm.mid(srcs, outs, sems)

        @pl.when(i == n - 1)
        def _():
            comm.last(srcs, outs, sems)

    return before, after


def _scan_fwd(zs, lw, km, aa, bb, comm):
    s = zs.shape[0]
    n = s // SCAN_CHUNK
    cm, st = _scan_specs(False, n)
    k = comm.n

    def body(*refs):
        r_ref, lw_ref, k_ref, v_ref, a_ref, b_ref = refs[:6]
        y_ref, s0_ref, t_ref = refs[6 + k:9 + k]
        state = refs[9 + 2 * k]
        before, after = _comm_phases(comm, refs[6:6 + k] + refs[9 + k:9 + 2 * k] + refs[10 + 2 * k:], n)
        before()

        @pl.when(pl.program_id(0) == 0)
        def _():
            state[...] = jnp.zeros_like(state)

        s0 = state[...]
        s0_ref[0] = s0
        (y, s1), t = _scan_chunk(*[ref[...] for ref in (r_ref, lw_ref, k_ref, v_ref, a_ref, b_ref)], s0,
                                 _unit_lower_inverse)
        y_ref[...] = y
        t_ref[0] = t.astype(BF16)
        state[...] = s1
        after()

    per_chunk = (n, SCAN_PAIRS, 128, 128)
    res = _pcall(
        body, name="scan_fwd", grid=(n,), in_specs=[cm(0), cm(0), cm(0), cm(2), cm(0), cm(0)] + [_HBM] * k,
        out_specs=[cm(0), st, st] + [_HBM] * k,
        out_shape=[jax.ShapeDtypeStruct((s, D), F32), jax.ShapeDtypeStruct(per_chunk, F32),
                   jax.ShapeDtypeStruct(per_chunk, BF16)] + comm.out_shape,
        scratch_shapes=[pltpu.VMEM((SCAN_PAIRS, 128, 128), F32)] + comm.sems,
        compiler_params=_cparams(("arbitrary",)),
    )(zs, lw, km, zs, aa, bb, *comm.ins)
    return res[0], res[1], res[2], res[3:]


def _scan_bwd(zs, lw, km, aa, bb, s0s, ts, dy, comm):
    s = zs.shape[0]
    n = s // SCAN_CHUNK
    cm, st = _scan_specs(True, n)
    k = comm.n

    def body(*refs):
        r_ref, lw_ref, k_ref, v_ref, a_ref, b_ref, s0_ref, t_ref, dy_ref = refs[:9]
        douts = refs[9 + k:15 + k]
        dstate = refs[15 + 2 * k]
        before, after = _comm_phases(comm, refs[9:9 + k] + refs[15 + k:15 + 2 * k] + refs[16 + 2 * k:], n)
        before()

        @pl.when(pl.program_id(0) == 0)
        def _():
            dstate[...] = jnp.zeros_like(dstate)

        t = t_ref[0].astype(F32)
        prim = [ref[...] for ref in (r_ref, lw_ref, k_ref, v_ref, a_ref, b_ref)] + [s0_ref[0]]
        _, vjp, _ = jax.vjp(lambda *p: _scan_chunk(*p, lambda nil: _known_inverse(nil, t)), *prim, has_aux=True)
        grads = vjp((dy_ref[...], dstate[...]))
        for ref, gr in zip(douts, grads[:6]):
            ref[...] = gr
        dstate[...] = grads[6]
        after()

    res = _pcall(
        body, name="scan_bwd", grid=(n,),
        in_specs=[cm(0), cm(0), cm(0), cm(2), cm(0), cm(0), st, st, cm(0)] + [_HBM] * k,
        out_specs=[cm(0)] * 6 + [_HBM] * k, out_shape=[jax.ShapeDtypeStruct((s, D), F32)] * 6 + comm.out_shape,
        scratch_shapes=[pltpu.VMEM((SCAN_PAIRS, 128, 128), F32)] + comm.sems,
        compiler_params=_cparams(("arbitrary",)),
    )(zs, lw, km, zs, aa, bb, s0s, ts, dy, *comm.ins)
    return res[:6], res[6:]


_HBM = pl.BlockSpec(memory_space=pltpu.HBM)


def _me():
    return lax.axis_index("x"), lax.axis_index("y"), lax.axis_index("c")


def _allgather8(src, name):
    def body(src_ref, out_ref, ssem, rsem, lsem):
        x, y, c = _me()
        me = 4 * x + 2 * y + c
        local = pltpu.make_async_copy(src_ref, out_ref.at[me], lsem)
        local.start()
        peers = []
        for k in range(1, 8):
            peers.append(((1 - x) if k & 4 else x, (1 - y) if k & 2 else y, (1 - c) if k & 1 else c))
        sends = []
        for k, peer in enumerate(peers):
            cp = pltpu.make_async_remote_copy(src_ref, out_ref.at[me], ssem.at[k], rsem.at[k], device_id=peer,
                                              device_id_type=MESH)
            cp.start()
            sends.append(cp)
        for k, (px, py, pc) in enumerate(peers):
            pltpu.make_async_remote_copy(src_ref, out_ref.at[4 * px + 2 * py + pc], ssem.at[k], rsem.at[k],
                                         device_id=(px, py, pc), device_id_type=MESH).wait_recv()
        for cp in sends:
            cp.wait_send()
        local.wait()

    return _pcall(
        body, name=name, in_specs=[_HBM], out_specs=_HBM, out_shape=jax.ShapeDtypeStruct((8,) + src.shape, src.dtype),
        scratch_shapes=[pltpu.SemaphoreType.DMA((7,)), pltpu.SemaphoreType.DMA((7,)), pltpu.SemaphoreType.DMA],
    )(src)


def _other_chips(x, y):
    return [(1 - x, y), (x, 1 - y), (1 - x, 1 - y)]


def _remote(src, dst, ssem, rsem, to):
    return pltpu.make_async_remote_copy(src, dst, ssem, rsem, device_id=to, device_id_type=MESH)


class _GatherWeights:
    def __init__(self, shards):
        self.ins = list(shards)
        n = self.n = len(shards)
        self.out_shape = [jax.ShapeDtypeStruct((4,) + t.shape, t.dtype) for t in shards]
        self.sems = [pltpu.SemaphoreType.DMA((6 * n,)), pltpu.SemaphoreType.DMA((6 * n,)),
                     pltpu.SemaphoreType.DMA((n,)), pltpu.SemaphoreType.DMA((n,))]

    def _copies(self, srcs, outs, sems):
        ssem, rsem, lsem, osem = sems
        x, y, c = _me()
        me = 2 * x + y
        own, ici, landed, passed, passed_in = [], [], [], [], []
        for a in range(self.n):
            h = self.ins[a].shape[0] // 2
            mine, other = pl.ds(c * h, h), pl.ds((1 - c) * h, h)
            own.append(_remote(srcs[a], outs[a].at[me], lsem.at[a], osem.at[a], (x, y, 1 - c)))
            for k, (px, py) in enumerate(_other_chips(x, y)):
                s1, r1, s2, r2 = ssem.at[6 * a + k], rsem.at[6 * a + k], ssem.at[6 * a + 3 + k], rsem.at[6 * a + 3 + k]
                got, got_sib = outs[a].at[2 * px + py, mine], outs[a].at[2 * px + py, other]
                ici.append(_remote(srcs[a].at[mine], outs[a].at[me, mine], s1, r1, (px, py, c)))
                landed.append(_remote(got, got, s1, r1, (px, py, c)))
                passed.append(_remote(got, got, s2, r2, (x, y, 1 - c)))
                passed_in.append(_remote(got_sib, got_sib, s2, r2, (x, y, 1 - c)))
        return own, ici, landed, passed, passed_in

    def first(self, srcs, outs, sems):
        own, ici, _, _, _ = self._copies(srcs, outs, sems)
        for cp in own + ici:
            cp.start()

    def mid(self, srcs, outs, sems):
        _, _, landed, passed, _ = self._copies(srcs, outs, sems)
        for arrived, onward in zip(landed, passed):
            arrived.wait_recv()
            onward.start()

    def last(self, srcs, outs, sems):
        own, ici, _, passed, passed_in = self._copies(srcs, outs, sems)
        for cp in passed_in:
            cp.wait_recv()
        for cp in ici + passed:
            cp.wait_send()
        for cp in own:
            cp.wait()


class _ScatterToChips:
    def __init__(self, parts):
        self.ins = list(parts)
        n = self.n = len(parts)
        self.out_shape = [jax.ShapeDtypeStruct(t.shape, t.dtype) for t in parts]
        self.sems = [pltpu.SemaphoreType.DMA((3 * n,)), pltpu.SemaphoreType.DMA((3 * n,)), pltpu.SemaphoreType.DMA((n,))]

    def _copies(self, srcs, outs, sems):
        ssem, rsem, lsem = sems
        x, y, c = _me()
        me = 2 * x + y
        own, out, landed = [], [], []
        for a in range(self.n):
            own.append(pltpu.make_async_copy(srcs[a].at[me], outs[a].at[me], lsem.at[a]))
            for k, (px, py) in enumerate(_other_chips(x, y)):
                dst = outs[a].at[2 * px + py]
                out.append(_remote(srcs[a].at[2 * px + py], outs[a].at[me], ssem.at[3 * a + k], rsem.at[3 * a + k],
                                   (px, py, c)))
                landed.append(_remote(dst, dst, ssem.at[3 * a + k], rsem.at[3 * a + k], (px, py, c)))
        return own, out, landed

    def first(self, srcs, outs, sems):
        own, out, _ = self._copies(srcs, outs, sems)
        for cp in own + out:
            cp.start()

    mid = None

    def last(self, srcs, outs, sems):
        own, out, landed = self._copies(srcs, outs, sems)
        for cp in landed:
            cp.wait_recv()
        for cp in own:
            cp.wait()
        for cp in out:
            cp.wait_send()


def _run_comm(comm, name):
    n = comm.n

    def body(*refs):
        srcs, outs, sems = refs[:n], refs[n:2 * n], refs[2 * n:]
        comm.first(srcs, outs, sems)
        if comm.mid is not None:
            comm.mid(srcs, outs, sems)
        comm.last(srcs, outs, sems)

    return _pcall(body, name=name, in_specs=[_HBM] * n, out_specs=[_HBM] * n, out_shape=comm.out_shape,
                  scratch_shapes=comm.sems)(*comm.ins)


class _NoComm:
    n, ins, out_shape, sems, mid = 0, [], [], [], None

    def first(self, srcs, outs, sems):
        pass

    def last(self, srcs, outs, sems):
        pass


_NOTHING = _NoComm()


class _SiblingHalves:
    mid = None

    def __init__(self, grads):
        self.ins = list(grads)
        n = self.n = len(grads)
        self.out_shape = [jax.ShapeDtypeStruct((4, t.shape[1] // 2, t.shape[2]), t.dtype) for t in grads]
        self.sems = [pltpu.SemaphoreType.DMA((n,)), pltpu.SemaphoreType.DMA((n,))]

    def _copies(self, srcs, outs, sems):
        ssem, rsem = sems
        x, y, c = _me()
        copies = []
        for a in range(self.n):
            h = self.ins[a].shape[1] // 2
            copies.append(_remote(srcs[a].at[:, pl.ds((1 - c) * h, h)], outs[a], ssem.at[a], rsem.at[a], (x, y, 1 - c)))
        return copies

    def first(self, srcs, outs, sems):
        for cp in self._copies(srcs, outs, sems):
            cp.start()

    def last(self, srcs, outs, sems):
        for cp in self._copies(srcs, outs, sems):
            cp.wait()


def _reduce_finish(reds, name):
    n = len(reds)

    def body(*refs):
        outs = refs[n:2 * n]
        ssem, rsem = refs[2 * n:]
        x, y, c = _me()
        copies = []
        for a in range(n):
            h = reds[a].shape[0] // 2
            mine = outs[a].at[pl.ds(c * h, h)]
            copies.append(_remote(mine, mine, ssem.at[a], rsem.at[a], (x, y, 1 - c)))
        for cp in copies:
            cp.start()
        for a in range(n):
            h = reds[a].shape[0] // 2
            dst = outs[a].at[pl.ds((1 - c) * h, h)]
            _remote(dst, dst, ssem.at[a], rsem.at[a], (x, y, 1 - c)).wait_recv()
        for cp in copies:
            cp.wait_send()

    return _pcall(
        body, name=name, in_specs=[_HBM] * n, out_specs=[_HBM] * n,
        out_shape=[jax.ShapeDtypeStruct(t.shape, t.dtype) for t in reds],
        input_output_aliases={a: a for a in range(n)},
        scratch_shapes=[pltpu.SemaphoreType.DMA((n,)), pltpu.SemaphoreType.DMA((n,))],
    )(*reds)


def _half_sum(fn, full, halves, out_full, out_dtype, core, name):
    p, h, c = (halves[0].shape if halves else (full[0].shape[0], full[0].shape[1] // 2, full[0].shape[2]))
    br = _div(h, max(16, (1 << 19) // (p * c)), 16)
    nb = h // br
    mine3 = pl.BlockSpec((p, br, c), lambda i, core_ref: (0, core_ref[0] * nb + i, 0))
    half3 = pl.BlockSpec((p, br, c), lambda i, core_ref: (0, i, 0))

    def body(core_ref, *refs):
        refs[-1][...] = fn(*[t[...].astype(F32) for t in refs[:-1]]).astype(out_dtype)

    if out_full:
        out_spec = pl.BlockSpec((br, c), lambda i, core_ref: (core_ref[0] * nb + i, 0))
        out_shape = jax.ShapeDtypeStruct((2 * h, c), out_dtype)
    else:
        out_spec, out_shape = half3, jax.ShapeDtypeStruct((p, h, c), out_dtype)
    return _pcall(
        body, name=name,
        grid_spec=pltpu.PrefetchScalarGridSpec(
            num_scalar_prefetch=1, grid=(nb,), in_specs=[mine3] * len(full) + [half3] * len(halves),
            out_specs=out_spec),
        out_shape=out_shape, compiler_params=_cparams(("parallel",)),
    )(core, *full, *halves)


def _ada_fwd(c_all, w, b):
    def body(c_ref, w_ref, b_ref, o_ref):
        o_ref[...] = jnp.dot(c_ref[...], w_ref[...], precision=HI, preferred_element_type=F32) + b_ref[...]

    return _pcall(body, name="ada_fwd", out_shape=jax.ShapeDtypeStruct((c_all.shape[0], w.shape[1]), F32),
                  compiler_params=pltpu.CompilerParams(vmem_limit_bytes=VMEM_LIMIT))(c_all, w, b)


def _ada_bwd(c_all_t, d):
    def body(c_ref, d_ref, o_ref):
        o_ref[...] = jnp.dot(c_ref[...], d_ref[...], precision=HI, preferred_element_type=F32)

    return _pcall(body, name="ada_bwd", out_shape=jax.ShapeDtypeStruct((c_all_t.shape[0], d.shape[1]), F32),
                  compiler_params=pltpu.CompilerParams(vmem_limit_bytes=VMEM_LIMIT))(c_all_t, d)


def _sum_lead(x, name):
    p, r, n = x.shape
    br = _div(r, 512, 8)

    def body(x_ref, o_ref):
        acc = x_ref[0]
        for j in range(1, p):
            acc = acc + x_ref[j]
        o_ref[...] = acc

    return _pcall(
        body, name=name, grid=(r // br,), in_specs=[pl.BlockSpec((p, br, n), lambda i: (0, i, 0))],
        out_specs=pl.BlockSpec((br, n), lambda i: (i, 0)), out_shape=jax.ShapeDtypeStruct((r, n), F32),
        compiler_params=_cparams(("parallel",)),
    )(x)


def _adamw(w, g, m, v, name):
    shape = w.shape
    cols = shape[-1]
    w2, g2, m2, v2 = [t.reshape(-1, cols) for t in (w, g, m, v)]
    rows = w2.shape[0]
    pref = max(8, (1 << 19) // cols // 8 * 8)
    br = _div(rows, pref, 8)
    if rows // br > 64:
        br = pref
    outs = _rows_fwd(_f_adamw, [(t, cols, 0) for t in (w2, g2, m2, v2)], [], [(cols, F32)] * 3, name=name, br=br)
    return [o.reshape(shape) for o in outs]


_BIG = (("w_in", 1), ("w_up", 1), ("w_down", 0), ("w_o", 0), ("w_rwkv_out", 0), ("w_att_out", 1), ("w2", 1), ("a2", 1),
        ("g2", 1))


_NEEDED_FIRST = ("w_in", "w_att_out", "w2", "a2", "g2")
_NEEDED_LATER = ("w_up", "w_down", "w_o", "w_rwkv_out")
_DONE_EARLY = ("w_up", "w_down", "w_o", "w_rwkv_out", "w_att_out")
_DONE_LATE = ("w_in", "w2", "a2", "g2")


def _cols_joined(t):
    return jnp.concatenate([t[j] for j in range(4)], axis=1)


def _cols_split(t):
    n = t.shape[1] // 4
    return jnp.stack([t[:, j * n:(j + 1) * n] for j in range(4)])


W_IN_SHARD = (N_ATT + N_RW + N_GATE) // 4
W_IN_PAD = 2560


def _row_window(parts, lo, hi):
    out, pos = [], 0
    for t, w in parts:
        a, b = max(lo, pos), min(hi, pos + w)
        if a < b:
            out.append(t[a - pos:b - pos])
        pos += w
    return out[0] if len(out) == 1 else jnp.concatenate(out, axis=0)


def _rows_joined(t):
    return t.reshape(4 * t.shape[1], t.shape[2])


def _rows_split(t):
    return t.reshape(4, t.shape[0] // 4, t.shape[1])


def _step_to_scan(x, tgt, ada, wts):
    sh1, sc1, gt1, sh2, sc2, gt2 = ada
    br = 256
    grp = lax.broadcasted_iota(jnp.int32, (D, 128), 0) // 64 == lax.broadcasted_iota(jnp.int32, (D, 128), 1)
    e = grp.astype(F32)
    et = e.T
    w_in = [(wts["w_in"][j], W_IN_SHARD) for j in range(4)]
    w_att = _row_window(w_in, 0, N_ATT)
    w_rw = jnp.concatenate([_row_window(w_in, N_ATT, N_ATT + N_RW), jnp.zeros((N_RWP - N_RW, D), BF16)], axis=0)
    w_gate = _row_window(w_in, N_ATT + N_RW, N_ATT + N_RW + N_GATE)
    mu = jnp.pad(wts["mu_shift"], ((0, 0), (0, N_RWP - N_RW)))
    wl = jnp.zeros((N_LORA, 3 * D), F32)
    wl = wl.at[0:64, 0:D].set(_cols_joined(wts["w2"]).astype(F32))
    wl = wl.at[64:128, D:2 * D].set(_cols_joined(wts["a2"]).astype(F32))
    wl = wl.at[128:288, 2 * D:3 * D].set(_cols_joined(wts["g2"]).astype(F32))
    pre1_c = [wts["norm1_w"], sc1, sh1]
    (h1,) = _rows_fwd(_f_pre, [(x, D, 0)], pre1_c, [(D, BF16), None], name="pre1_fwd", br=br)
    att_in = _mm(h1, w_att, tb=True, name="mm_att_in")
    z = _mm(h1, w_rw, tb=True, name="mm_rw_in")
    gate_in = _mm(h1, w_gate, tb=True, name="mm_gate_in")
    att_o, att_l = [], []
    for g, (_, dil) in enumerate(ATT_PATTERNS):
        o, l = _att_fwd(att_in, g, dil)
        att_o.append(o)
        att_l.append(l)
    comb_rows = [(t, ATT_WIDTH, 0) for t in att_o + att_l]
    (att,) = _rows_fwd(_f_comb, comb_rows, [], [(ATT_WIDTH, BF16)], name="comb_fwd", br=br)
    y_att = _mm(att, wts["w_att_out"], b_chip=True, name="mm_att_out")
    zs = _shift_fwd(z, mu)
    rwpre_c = [wts["w0"], wts["a0"], wts["k_k"], wts["k_a"], wl, e, et]
    lw, km, aa, bb, gg = _rows_fwd(_f_rwpre, [(zs, N_RWP, 0)], rwpre_c,
                                   [None, (D, F32), (D, F32), None, (D, F32), (D, F32), (D, F32)],
                                   name="rwpre_fwd", br=br)
    return dict(x=x, tgt=tgt, wts=wts, br=br, e=e, et=et, gt1=gt1, sc2=sc2, sh2=sh2, gt2=gt2, w_att=w_att, w_rw=w_rw,
                w_gate=w_gate, mu=mu, pre1_c=pre1_c, h1=h1, att_in=att_in, z=z, gate_in=gate_in, comb_rows=comb_rows,
                att=att, y_att=y_att, zs=zs, rwpre_c=rwpre_c, lw=lw, km=km, aa=aa, bb=bb, gg=gg)


def _step_between_scans(st, y_raw, late):
    x, tgt, wts, br, e, et = st["x"], st["tgt"], st["wts"], st["br"], st["e"], st["et"]
    zs, km, gg, gate_in, y_att, att = st["zs"], st["km"], st["gg"], st["gate_in"], st["y_att"], st["att"]
    comb_rows, att_in = st["comb_rows"], st["att_in"]
    gt1, sc2, sh2, gt2 = st["gt1"], st["sc2"], st["sh2"], st["gt2"]
    w_up, w_ao = late["w_up"], wts["w_att_out"]
    w_down, w_o, w_ro = _rows_joined(late["w_down"]), _rows_joined(late["w_o"]), _rows_joined(late["w_rwkv_out"])
    post_rows = [(y_raw, D, 0), (zs, D, 0), (zs, D, 2), (km, D, 0), (gg, D, 0)]
    post_c = [wts["lnx_w"], wts["lnx_b"], wts["r_k"], e, et]
    (rw_out,) = _rows_fwd(_f_rwpost, post_rows, post_c, [(D, BF16)], name="rwpost_fwd", br=br)
    y_rw = _mm(rw_out, w_ro, name="mm_rw_out")
    mix_rows = [(gate_in, N_GATE, 0), (y_att, D, 0), (y_rw, D, 0)]
    (mix,) = _rows_fwd(_f_mix, mix_rows, [wts["b_gate"]], [(D, BF16)], name="mix_fwd", br=br)
    o = _mm(mix, w_o, name="mm_o")
    pre2_c = [gt1, wts["norm2_w"], sc2, sh2]
    x1, h2 = _rows_fwd(_f_pre2, [(x, D, 0), (o, D, 0)], pre2_c, [(D, F32), (D, BF16)], name="pre2_fwd", br=br)
    u = _mm(h2, w_up, b_chip=True, name="mm_up")
    act = _conv_fwd(u, wts["conv_w"], wts["conv_b"])
    f = _mm(act, w_down, name="mm_down")
    fin_rows = [(x1, D, 0), (f, D, 0), (tgt, D, 0)]
    fin_c = [gt2, wts["norm_f_w"]]

    def fin_fwd(*a):
        (l,) = _f_fin(*a)
        return (jnp.broadcast_to(jnp.sum(l, axis=0, keepdims=True), (8, 128)),)

    (loss_acc,) = _rows_fwd(fin_fwd, fin_rows, fin_c, [], name="fin_fwd", br=br, acc_shape=(8, 128))

    gw = {}
    dx1a, df, d_gt2, gw["norm_f_w"] = _rows_bwd(
        _f_fin, fin_rows, fin_c, [[]], wrt_rows=[0, 1], wrt_consts=[0, 1], drow_dtypes=[F32, BF16],
        name="fin_bwd", br=br, unit_cot=True)
    dact = _mm(df, w_down, tb=True, name="mm_dact")
    gw["w_down"] = _rows_split(_mm(act, df, ta=True, name="mm_dw_down"))
    du, gw["conv_w"], gw["conv_b"] = _conv_bwd(u, wts["conv_w"], wts["conv_b"], dact)
    dh2 = _mm(du, w_up, tb=True, b_chip=True, name="mm_dh2")
    gw["w_up"] = _mm(h2, du, ta=True, out_chip=True, name="mm_dw_up")
    dxa, do, d_gt1, gw["norm2_w"], d_sc2, d_sh2 = _rows_bwd(
        _f_pre2, [(x, D, 0), (o, D, 0)], pre2_c, [[(dx1a, D, 0)], [(dh2, D, 0)]], wrt_rows=[0, 1],
        wrt_consts=[0, 1, 2, 3], drow_dtypes=[F32, BF16], name="pre2_bwd", br=br)
    dmix = _mm(do, w_o, tb=True, name="mm_dmix")
    gw["w_o"] = _rows_split(_mm(mix, do, ta=True, name="mm_dw_o"))
    dgate, dya, dyr, gw["b_gate"] = _rows_bwd(
        _f_mix, mix_rows, [wts["b_gate"]], [[(dmix, D, 0)]], wrt_rows=[0, 1, 2], wrt_consts=[0],
        drow_dtypes=[BF16] * 3, name="mix_bwd", br=br)
    datt = _mm(dya, w_ao, tb=True, b_chip=True, name="mm_datt")
    gw["w_att_out"] = _mm(att, dya, ta=True, out_chip=True, name="mm_dw_att_out")
    drw = _mm(dyr, w_ro, tb=True, name="mm_drw")
    gw["w_rwkv_out"] = _rows_split(_mm(rw_out, dyr, ta=True, name="mm_dw_rw_out"))
    dcomb = _rows_bwd(_f_comb, comb_rows, [], [[(datt, ATT_WIDTH, 0)]], wrt_rows=list(range(6)), wrt_consts=[],
                      drow_dtypes=[F32] * 6, name="comb_bwd", br=br)
    datt_in = None
    for g, (_, dil) in enumerate(ATT_PATTERNS):
        datt_in = _att_bwd(att_in, g, dil, dcomb[g], dcomb[3 + g], datt_in)
    dy_raw, dr_p, dv_p, dkm_p, dgg, gw["lnx_w"], gw["lnx_b"], gw["r_k"], *recv_early = _rows_bwd(
        _f_rwpost, post_rows, post_c, [[(drw, D, 0)]], wrt_rows=[0, 1, 2, 3, 4], wrt_consts=[0, 1, 2],
        drow_dtypes=[F32] * 5, name="rwpost_bwd", br=br, comm=_SiblingHalves([gw[n] for n in _DONE_EARLY]))
    st.update(loss=loss_acc[0, 0], gw=gw, dxa=dxa, dgate=dgate, datt_in=datt_in,
              dy_raw=dy_raw, dr_p=dr_p, dv_p=dv_p, dkm_p=dkm_p, dgg=dgg, d_ada_late=(d_gt1, d_sh2, d_sc2, d_gt2),
              recv_early=recv_early)
    return st


def _chip_parts(grads, recv, names, core):
    return [_half_sum(lambda a, b: a + b, [g], [r], False, BF16, core, "reduce_add2_" + n)
            for g, r, n in zip(grads, recv, names)]


def _step_after_scan(st, scan_grads, core):
    x, br, gw, h1, zs = st["x"], st["br"], st["gw"], st["h1"], st["zs"]
    dr_s, dlw, dkm_s, dv_s, daa, dbb = scan_grads
    pre_cots = [[(st["dr_p"], D, 0), (dr_s, D, 0)], [(dlw, D, 0)], [(st["dkm_p"], D, 0), (dkm_s, D, 0)],
                [(st["dv_p"], D, 0), (dv_s, D, 0)], [(daa, D, 0)], [(dbb, D, 0)], [(st["dgg"], D, 0)]]
    dzs, gw["w0"], gw["a0"], gw["k_k"], gw["k_a"], dwl = _rows_bwd(
        _f_rwpre, [(zs, N_RWP, 0)], st["rwpre_c"], pre_cots, wrt_rows=[0], wrt_consts=[0, 1, 2, 3, 4],
        drow_dtypes=[F32], name="rwpre_bwd", br=128)
    gw["w2"], gw["a2"] = _cols_split(dwl[0:64, 0:D]), _cols_split(dwl[64:128, D:2 * D])
    gw["g2"] = _cols_split(dwl[128:288, 2 * D:3 * D])
    dz, dmu = _shift_bwd(st["z"], st["mu"], dzs)
    gw["mu_shift"] = dmu[:, :N_RW]
    datt_in, dgate = st["datt_in"], st["dgate"]
    dw_in = [(_mm(datt_in, h1, ta=True, name="mm_dw_att"), N_ATT), (_mm(dz, h1, ta=True, name="mm_dw_rw"), N_RW),
             (_mm(dgate, h1, ta=True, name="mm_dw_gate"), N_GATE)]
    slabs = []
    for j in range(4):
        slabs += [_row_window(dw_in, j * W_IN_SHARD, (j + 1) * W_IN_SHARD), jnp.zeros((W_IN_PAD - W_IN_SHARD, D), F32)]
    gw["w_in"] = jnp.concatenate(slabs, axis=0).reshape(4, W_IN_PAD, D)
    late = [gw[n] for n in _DONE_LATE]
    parts = _chip_parts(late, _run_comm(_SiblingHalves(late), "reduce_sib_late"), _DONE_LATE, core)
    dh1, slots_late = _mm_sum([(datt_in, st["w_att"]), (dz, st["w_rw"]), (dgate, st["w_gate"])],
                              comm=_ScatterToChips(parts), name="mm_dh1")
    grad_x, gw["norm1_w"], d_sc1, d_sh1 = _rows_bwd(
        _f_pre, [(x, D, 0)], st["pre1_c"], [[(dh1, D, 0)], [(st["dxa"], D, 0)]], wrt_rows=[0], wrt_consts=[0, 1, 2],
        drow_dtypes=[F32], name="pre1_bwd", br=br)
    d_gt1, d_sh2, d_sc2, d_gt2 = st["d_ada_late"]
    return st["loss"], grad_x, (d_sh1, d_sc1, d_gt1, d_sh2, d_sc2, d_gt2), gw, slots_late


_SMALL = ("b_ada", "norm1_w", "b_gate", "mu_shift", "w0", "a0", "k_k", "k_a", "r_k", "lnx_w", "lnx_b", "norm2_w",
          "conv_b", "norm_f_w")
_NAMES = ("w_ada", "b_ada", "norm1_w", "w_in", "b_gate", "mu_shift", "w0", "w2", "a0", "a2", "g2", "k_k", "k_a", "r_k",
          "lnx_w", "lnx_b", "w_att_out", "w_rwkv_out", "w_o", "norm2_w", "w_up", "conv_w", "conv_b", "w_down",
          "norm_f_w")


def kernel(x, c, w_ada, b_ada, norm1_w, w_in, b_gate, mu_shift, w0, w2, a0, a2, g2, k_k, k_a, r_k, lnx_w, lnx_b, w_att_out, w_rwkv_out, w_o, norm2_w, w_up, conv_w, conv_b, w_down, norm_f_w, loss_target, m_w_ada, m_b_ada, m_norm1_w, m_w_in, m_b_gate, m_mu_shift, m_w0, m_w2, m_a0, m_a2, m_g2, m_k_k, m_k_a, m_r_k, m_lnx_w, m_lnx_b, m_w_att_out, m_w_rwkv_out, m_w_o, m_norm2_w, m_w_up, m_conv_w, m_conv_b, m_w_down, m_norm_f_w, v_w_ada, v_b_ada, v_norm1_w, v_w_in, v_b_gate, v_mu_shift, v_w0, v_w2, v_a0, v_a2, v_g2, v_k_k, v_k_a, v_r_k, v_lnx_w, v_lnx_b, v_w_att_out, v_w_rwkv_out, v_w_o, v_norm2_w, v_w_up, v_conv_w, v_conv_b, v_w_down, v_norm_f_w):
    args = dict(locals())
    p, pm, pv = {}, {}, {}
    for name in _NAMES:
        for dst, key in ((p, name), (pm, "m_" + name), (pv, "v_" + name)):
            t = args[key]
            if name == "w_in":
                dst[name] = jnp.swapaxes(t, 1, 2)[0]
            else:
                dst[name] = t.reshape(1, -1) if name in ("r_k", "norm_f_w") else t.reshape(t.shape[-2], t.shape[-1])
    xi, yi, ci = _me()
    chip = 2 * xi + yi
    dev = 4 * xi + 2 * yi + ci
    x2, tgt = x[0], loss_target[0]

    n_cw = 3 * (2 * D_FF // 4)
    vec = jnp.concatenate([c.reshape(-1), p["conv_w"].reshape(-1), jnp.zeros((8 * D - D - n_cw,), F32)]).reshape(8, D)
    g0 = _allgather8(vec, "gather_c").reshape(8, 8 * D)
    c_all = g0[:, :D]
    conv_w_full = jnp.concatenate([g0[2 * j, D:D + n_cw].reshape(3, -1) for j in range(4)], axis=1)
    n_ada = 6 * D // 4
    b_ada_sh = lax.dynamic_slice(p["b_ada"], (0, chip * n_ada), (1, n_ada))
    ada_sh = _ada_fwd(c_all, p["w_ada"], b_ada_sh)
    ga = _allgather8(ada_sh, "gather_ada")
    ada_all = jnp.concatenate([ga[2 * j] for j in range(4)], axis=1)
    ada_row = lax.dynamic_slice(ada_all, (dev, 0), (1, 6 * D))
    ada = [ada_row[:, j * D:(j + 1) * D] for j in range(6)]

    big = [n for n, _ in _BIG]
    shard = {n: p[n].astype(BF16) for n in big}
    shard["w_in"] = jnp.pad(shard["w_in"], ((0, W_IN_PAD - W_IN_SHARD), (0, 0)))
    wts = dict(zip(_NEEDED_FIRST, _run_comm(_GatherWeights([shard[n] for n in _NEEDED_FIRST]), "gather_w")))
    for n in _SMALL:
        wts[n] = p[n]
    wts["conv_w"] = conv_w_full
    core = ci.reshape(1).astype(jnp.int32)

    st = _step_to_scan(x2, tgt, ada, wts)
    y_raw, s0s, inverses, late = _scan_fwd(st["zs"], st["lw"], st["km"], st["aa"], st["bb"],
                                           _GatherWeights([shard[n] for n in _NEEDED_LATER]))
    st = _step_between_scans(st, y_raw, dict(zip(_NEEDED_LATER, late)))
    early = _chip_parts([st["gw"][n] for n in _DONE_EARLY], st["recv_early"], _DONE_EARLY, core)
    scan_grads, slots_early = _scan_bwd(st["zs"], st["lw"], st["km"], st["aa"], st["bb"], s0s, inverses,
                                        st["dy_raw"], _ScatterToChips(early))
    loss_part, grad_x, d_ada, gw, slots_late = _step_after_scan(st, scan_grads, core)

    small = [jnp.concatenate(d_ada, axis=1)] + [gw[n] for n in _SMALL[1:]] + [gw["conv_w"], loss_part.reshape(1, 1)]
    sizes = [t.size for t in small]
    flat = jnp.concatenate([t.reshape(-1) for t in small])
    npad = (-flat.shape[0]) % (8 * D)
    srows = (flat.shape[0] + npad) // D
    flat = jnp.concatenate([flat, jnp.zeros((npad,), F32)]).reshape(srows, D)
    parts = _allgather8(flat, "gather_small")
    tot = _sum_lead(parts, "sum_small").reshape(-1)
    pieces, pos = [], 0
    for sz in sizes:
        pieces.append(tot[pos:pos + sz])
        pos += sz
    grads = {}
    for n, piece in zip(_SMALL, pieces[:len(_SMALL)]):
        grads[n] = piece.reshape(p[n].shape)
    conv_w_grad = pieces[len(_SMALL)].reshape(3, 2 * D_FF)
    grads["conv_w"] = lax.dynamic_slice(conv_w_grad, (0, chip * (n_cw // 3)), (3, n_cw // 3))
    loss = pieces[-1][0]
    d_ada_all = parts[:, :6].reshape(8, 6 * D)
    grads["w_ada"] = _ada_bwd(c_all.T, lax.dynamic_slice(d_ada_all, (0, chip * n_ada), (8, n_ada)))

    order = _DONE_EARLY + _DONE_LATE
    reds = [_half_sum(lambda t: t[0] + t[1] + t[2] + t[3], [], [t], True, F32, core, "reduce_add4_" + n)
            for n, t in zip(order, list(slots_early) + list(slots_late))]
    for n, g in zip(order, _reduce_finish(reds, "reduce_sib2")):
        grads[n] = g

    outs_g, outs_d, outs_m, outs_v = [], [], [], []
    grads["w_in"] = grads["w_in"][:W_IN_SHARD]
    for name in _NAMES:
        g = grads[name]
        d, m, v = _adamw(p[name], g, pm[name], pv[name], "adamw_" + name)
        shape = args[name].shape
        for outs, t in ((outs_g, g), (outs_d, d), (outs_m, m), (outs_v, v)):
            outs.append(jnp.swapaxes(t[None], 1, 2) if name == "w_in" else t.reshape(shape))
    return (loss, grad_x.reshape(x.shape), *outs_g, *outs_d, *outs_m, *outs_v)
```

```python
import functools

import jax
import jax.numpy as jnp
from jax import lax
from jax.experimental import pallas as pl
from jax.experimental.pallas import tpu as pltpu

F32 = jnp.float32
BF16 = jnp.bfloat16
HI = lax.Precision.HIGHEST
MESH = pl.DeviceIdType.MESH

D = 1024
ATT_PATTERNS = ((128, 1), (512, 4), (2048, 16))
ATT_BLOCK = 128
ATT_WIDTH = 512
N_ATT = 3 * 3 * ATT_WIDTH
N_RW = 3 * D + 64 + 64 + 160
N_RWP = 3456
N_LORA = N_RWP - 3 * D
N_GATE = 2 * D
D_FF = 2816
RMS_EPS = 1e-6
GN_EPS = 64e-5
SCAN_CHUNK = 64
SCAN_PAIRS = 8
NEG = -1e30
VMEM_LIMIT = 48 * 1024 * 1024

ADAM_LR, ADAM_B1, ADAM_B2, ADAM_EPS, ADAM_WD, ADAM_STEP = 0.001, 0.9, 0.999, 1e-08, 0.01, 10


def _pcall(body, **kw):
    return pl.pallas_call(body, **kw)


def _cparams(sem):
    return pltpu.CompilerParams(dimension_semantics=sem, vmem_limit_bytes=VMEM_LIMIT)


def _div(n, pref, mult):
    best = None
    d = mult
    while d <= min(n, pref):
        if n % d == 0:
            best = d
        d += mult
    return best if best else n


def _dg(a, b, ca, cb):
    return lax.dot_general(a.astype(BF16), b.astype(BF16), (((ca,), (cb,)), ((), ())), preferred_element_type=F32)


@jax.custom_vjp
def _nn(a, b):
    return _dg(a, b, 1, 0)


@jax.custom_vjp
def _nt(a, b):
    return _dg(a, b, 1, 1)


@jax.custom_vjp
def _tn(a, b):
    return _dg(a, b, 0, 0)


_nn.defvjp(lambda a, b: (_nn(a, b), (a, b)), lambda res, g: (_nt(g, res[1]), _tn(res[0], g)))
_nt.defvjp(lambda a, b: (_nt(a, b), (a, b)), lambda res, g: (_nn(g, res[1]), _tn(g, res[0])))
_tn.defvjp(lambda a, b: (_tn(a, b), (a, b)), lambda res, g: (_nt(res[1], g), _nn(res[0], g)))


def _bdg(a, b, ca, cb):
    return lax.dot_general(a.astype(BF16), b.astype(BF16), (((ca,), (cb,)), ((0,), (0,))), preferred_element_type=F32)


@jax.custom_vjp
def _bnn(a, b):
    return _bdg(a, b, 2, 1)


@jax.custom_vjp
def _bnt(a, b):
    return _bdg(a, b, 2, 2)


@jax.custom_vjp
def _btn(a, b):
    return _bdg(a, b, 1, 1)


_bnn.defvjp(lambda a, b: (_bnn(a, b), (a, b)), lambda res, g: (_bnt(g, res[1]), _btn(res[0], g)))
_bnt.defvjp(lambda a, b: (_bnt(a, b), (a, b)), lambda res, g: (_bnn(g, res[1]), _btn(g, res[0])))
_btn.defvjp(lambda a, b: (_btn(a, b), (a, b)), lambda res, g: (_bnt(res[1], g), _bnn(res[0], g)))


def _split2(x):
    hi = x.astype(BF16)
    lo = (x - hi.astype(F32)).astype(BF16)
    return hi, lo


def _hsum_impl(x, e, et):
    eb, etb = e.astype(BF16), et.astype(BF16)
    s = jnp.dot(x.astype(BF16), eb, preferred_element_type=F32)
    shi, slo = _split2(s)
    return jnp.dot(shi, etb, preferred_element_type=F32) + jnp.dot(slo, etb, preferred_element_type=F32)


@jax.custom_vjp
def _hsum(x, e, et):
    return _hsum_impl(x, e, et)


_hsum.defvjp(lambda x, e, et: (_hsum_impl(x, e, et), (e, et)),
             lambda res, g: (_hsum_impl(g, res[0], res[1]), jnp.zeros_like(res[0]), jnp.zeros_like(res[1])))


def _mm(a, b, *, ta=False, tb=False, out_dtype=F32, add=None, b_chip=False, out_chip=False, comm=None, name):
    riding = _NOTHING if comm is None else comm
    nc = riding.n
    if ta:
        kdim, m = a.shape
    else:
        m, kdim = a.shape
    if b_chip:
        n = b.shape[1] if tb else 4 * b.shape[2]
    else:
        n = b.shape[0] if tb else b.shape[1]
    tm, tn, tk = _div(m, 1536, 128), _div(n, 1536, 128), _div(kdim, 1408, 128)
    if b_chip and tb:
        tk = kdim // 4
    if (b_chip and not tb) or out_chip:
        tn = n // 4
    nk = kdim // tk
    ca, cb = (0 if ta else 1), (1 if tb else 0)

    nin = 2 if add is None else 3
    gi, gj = m // tm, n // tn

    def body(*refs):
        a_ref, b_ref = refs[0], refs[1]
        add_ref = None if add is None else refs[2]
        o_ref = refs[nin + nc]
        step = (pl.program_id(0) * gj + pl.program_id(1)) * nk + pl.program_id(2)
        before, after = _comm_phases(riding, refs[nin:nin + nc] + refs[nin + nc + 1:nin + 2 * nc + 1]
                                     + refs[nin + 2 * nc + 1 + (nk > 1):], gi * gj * nk, step)
        before()
        part = lax.dot_general(a_ref[...], b_ref[...], (((ca,), (cb,)), ((), ())), preferred_element_type=F32)

        def finish(r):
            if add_ref is not None:
                r = r + add_ref[...]
            o_ref[...] = r.astype(o_ref.dtype)

        if nk == 1:
            finish(part)
            after()
            return
        acc = refs[nin + 2 * nc + 1]
        k = pl.program_id(2)

        @pl.when(k == 0)
        def _():
            acc[...] = part

        @pl.when(k > 0)
        def _():
            acc[...] += part

        @pl.when(k == nk - 1)
        def _():
            finish(acc[...])

        after()

    a_spec = pl.BlockSpec((tk, tm), lambda i, j, k: (k, i)) if ta else pl.BlockSpec((tm, tk), lambda i, j, k: (i, k))
    if b_chip:
        b_spec = (pl.BlockSpec((None, tn, tk), lambda i, j, k: (k, j, 0)) if tb
                  else pl.BlockSpec((None, tk, tn), lambda i, j, k: (j, k, 0)))
    else:
        b_spec = pl.BlockSpec((tn, tk), lambda i, j, k: (j, k)) if tb else pl.BlockSpec((tk, tn), lambda i, j, k: (k, j))
    in_specs = [a_spec, b_spec]
    args = [a, b]
    if add is not None:
        in_specs.append(pl.BlockSpec((tm, tn), lambda i, j, k: (i, j)))
        args.append(add)
    if out_chip:
        out_spec = pl.BlockSpec((None, tm, tn), lambda i, j, k: (j, i, 0))
        out_shape = jax.ShapeDtypeStruct((4, m, tn), out_dtype)
    else:
        out_spec = pl.BlockSpec((tm, tn), lambda i, j, k: (i, j))
        out_shape = jax.ShapeDtypeStruct((m, n), out_dtype)
    res = _pcall(
        body, name=name, grid=(gi, gj, nk), in_specs=in_specs + [_HBM] * nc, out_specs=[out_spec] + [_HBM] * nc,
        out_shape=[out_shape] + riding.out_shape,
        scratch_shapes=([] if nk == 1 else [pltpu.VMEM((tm, tn), F32)]) + riding.sems,
        compiler_params=_cparams(("arbitrary",) * 3 if nc else ("parallel", "parallel", "arbitrary")),
    )(*args, *riding.ins)
    return res[0] if comm is None else (res[0], res[1:])


def _mm_sum(pairs, *, comm, name):
    m, n = pairs[0][0].shape[0], pairs[0][1].shape[1]
    tm, tn = _div(m, 1024, 128), _div(n, 1024, 128)
    tks = [_div(a.shape[1], 1408, 128) for a, _ in pairs]
    nks = [a.shape[1] // tk for (a, _), tk in zip(pairs, tks)]
    offs = [sum(nks[:p]) for p in range(len(pairs))]
    total, npair, nc = sum(nks), len(pairs), comm.n
    gi, gj = m // tm, n // tn

    def body(*refs):
        o_ref, acc = refs[2 * npair + nc], refs[2 * npair + 2 * nc + 1]
        k = pl.program_id(2)
        step = (pl.program_id(0) * gj + pl.program_id(1)) * total + k
        before, after = _comm_phases(comm, refs[2 * npair:2 * npair + nc]
                                     + refs[2 * npair + nc + 1:2 * npair + 2 * nc + 1]
                                     + refs[2 * npair + 2 * nc + 2:], gi * gj * total, step)
        before()
        for p in range(npair):
            def partial_product(p=p):
                part = jnp.dot(refs[2 * p][...], refs[2 * p + 1][...], preferred_element_type=F32)
                if p == 0:
                    @pl.when(k == 0)
                    def _():
                        acc[...] = part

                    @pl.when(k > 0)
                    def _():
                        acc[...] += part
                else:
                    acc[...] += part

            pl.when(jnp.logical_and(k >= offs[p], k < offs[p] + nks[p]))(partial_product)

        @pl.when(k == total - 1)
        def _():
            o_ref[...] = acc[...]

        after()

    def specs(tk, off, nk):
        def kb(k):
            return jnp.clip(k - off, 0, nk - 1)
        return [pl.BlockSpec((tm, tk), lambda i, j, k: (i, kb(k))), pl.BlockSpec((tk, tn), lambda i, j, k: (kb(k), j))]

    in_specs, args = [], []
    for (a, b), tk, off, nk in zip(pairs, tks, offs, nks):
        in_specs += specs(tk, off, nk)
        args += [a, b]
    res = _pcall(
        body, name=name, grid=(gi, gj, total), in_specs=in_specs + [_HBM] * nc,
        out_specs=[pl.BlockSpec((tm, tn), lambda i, j, k: (i, j))] + [_HBM] * nc,
        out_shape=[jax.ShapeDtypeStruct((m, n), F32)] + comm.out_shape,
        scratch_shapes=[pltpu.VMEM((tm, tn), F32)] + comm.sems,
        compiler_params=_cparams(("arbitrary",) * 3),
    )(*args, *comm.ins)
    return res[0], res[1:]


def _row_spec(br, w, cb):
    return pl.BlockSpec((br, w), lambda i: (i, cb))


def _const_spec(shape):
    return pl.BlockSpec(shape, lambda i: (0,) * len(shape))


def _rows_fwd(fn, rows, consts, outs, *, name, br, acc_shape=None):
    s = rows[0][0].shape[0]
    nr, nc = len(rows), len(consts)
    kept = [k for k, o in enumerate(outs) if o is not None]

    def body(*refs):
        xs = [r[...].astype(F32) for r in refs[:nr]]
        cs = [c[...] for c in refs[nr:nr + nc]]
        res = fn(*xs, *cs)
        orefs = refs[nr + nc:]
        for j, k in enumerate(kept):
            orefs[j][...] = res[k].astype(orefs[j].dtype)
        if acc_shape is not None:
            acc_ref = orefs[len(kept)]

            @pl.when(pl.program_id(0) == 0)
            def _():
                acc_ref[...] = jnp.zeros_like(acc_ref)

            acc_ref[...] += res[len(outs)]

    in_specs = [_row_spec(br, w, cb) for (_, w, cb) in rows] + [_const_spec(c.shape) for c in consts]
    out_specs = [_row_spec(br, outs[k][0], 0) for k in kept]
    out_shape = [jax.ShapeDtypeStruct((s, outs[k][0]), outs[k][1]) for k in kept]
    if acc_shape is not None:
        out_specs.append(_const_spec(acc_shape))
        out_shape.append(jax.ShapeDtypeStruct(acc_shape, F32))
    return _pcall(
        body, name=name, grid=(pl.cdiv(s, br),), in_specs=in_specs, out_specs=out_specs, out_shape=out_shape,
        compiler_params=_cparams(("arbitrary",)),
    )(*[r[0] for r in rows], *consts)


def _rows_bwd(fn, rows, consts, cots, *, wrt_rows, wrt_consts, drow_dtypes, name, br, unit_cot=False, comm=None):
    comm = _NOTHING if comm is None else comm
    ncomm = comm.n
    nout = len(wrt_rows) + len(wrt_consts)
    s = rows[0][0].shape[0]
    nr, nc = len(rows), len(consts)
    flat_cots = [c for lst in cots for c in lst]
    ncot = len(flat_cots)

    def body(*refs):
        xs = [r[...].astype(F32) for r in refs[:nr]]
        cs = [c[...] for c in refs[nr:nr + nc]]
        cvals = [c[...].astype(F32) for c in refs[nr + nc:nr + nc + ncot]]
        orefs = refs[nr + nc + ncot + ncomm:]
        before, after = _comm_phases(comm, refs[nr + nc + ncot:nr + nc + ncot + ncomm] + orefs[nout:], s // br)
        before()

        def g(*d):
            xs2, cs2 = list(xs), list(cs)
            for j, k in enumerate(wrt_rows):
                xs2[k] = d[j]
            for j, k in enumerate(wrt_consts):
                cs2[k] = d[len(wrt_rows) + j]
            return tuple(fn(*xs2, *cs2))

        prim = [xs[k] for k in wrt_rows] + [cs[k] for k in wrt_consts]
        outs, vjp = jax.vjp(g, *prim)
        ct = []
        pos = 0
        for o, lst in zip(outs, cots):
            if unit_cot:
                ct.append(jnp.ones_like(o))
                continue
            acc = jnp.zeros_like(o)
            for _ in lst:
                acc = acc + cvals[pos]
                pos += 1
            ct.append(acc)
        grads = vjp(tuple(ct))
        for j in range(len(wrt_rows)):
            orefs[j][...] = grads[j].astype(orefs[j].dtype)

        @pl.when(pl.program_id(0) == 0)
        def _():
            for j in range(len(wrt_consts)):
                oref = orefs[len(wrt_rows) + j]
                oref[...] = jnp.zeros_like(oref)

        for j in range(len(wrt_consts)):
            orefs[len(wrt_rows) + j][...] += grads[len(wrt_rows) + j]
        after()

    in_specs = ([_row_spec(br, w, cb) for (_, w, cb) in rows] + [_const_spec(c.shape) for c in consts]
                + [_row_spec(br, w, cb) for (_, w, cb) in flat_cots] + [_HBM] * ncomm)
    out_specs = ([_row_spec(br, rows[k][1], 0) for k in wrt_rows] + [_const_spec(consts[k].shape) for k in wrt_consts]
                 + [_HBM] * ncomm)
    out_shape = ([jax.ShapeDtypeStruct((s, rows[k][1]), dt) for k, dt in zip(wrt_rows, drow_dtypes)]
                 + [jax.ShapeDtypeStruct(consts[k].shape, F32) for k in wrt_consts] + comm.out_shape)
    return _pcall(
        body, name=name, grid=(s // br,), in_specs=in_specs, out_specs=out_specs, out_shape=out_shape,
        scratch_shapes=comm.sems, compiler_params=_cparams(("arbitrary",)),
    )(*[r[0] for r in rows], *consts, *[c[0] for c in flat_cots], *comm.ins)


def _rms(x, w):
    return x * lax.rsqrt(jnp.mean(x * x, axis=-1, keepdims=True) + RMS_EPS) * w


def _softplus(x):
    return jnp.maximum(x, 0.0) + jnp.log(1.0 + jnp.exp(-jnp.abs(x)))


def _f_pre(x, nw, sc, sh):
    return _rms(x, nw) * (1.0 + sc) + sh, x


def _f_pre2(x, o, gt, nw, sc, sh):
    x1 = x + gt * o
    return x1, _rms(x1, nw) * (1.0 + sc) + sh


def _f_fin(x1, f, tgt, gt, nfw):
    y = _rms(x1 + gt * f, nfw)
    return (0.5 * jnp.mean(jnp.square(y - tgt), axis=-1, keepdims=True),)


def _f_comb(o1, o2, o3, l1, l2, l3):
    m = lax.stop_gradient(jnp.maximum(jnp.maximum(l1, l2), l3))
    e1, e2, e3 = jnp.exp(l1 - m), jnp.exp(l2 - m), jnp.exp(l3 - m)
    return ((e1 * o1 + e2 * o2 + e3 * o3) / (e1 + e2 + e3),)


def _f_rwpre(zs, w0, a0, k_k, k_a, wl, e, et):
    r, k, v, zl = zs[:, 0:D], zs[:, D:2 * D], zs[:, 2 * D:3 * D], zs[:, 3 * D:N_RWP]
    lane = lax.broadcasted_iota(jnp.int32, zl.shape, 1)
    t = jnp.where(lane < 64, jnp.tanh(zl), jnp.where(lane < 128, zl, jnp.where(lane < 288, jax.nn.sigmoid(zl), 0.0)))
    lo = _nn(t[:, 0:128], wl[0:128, 0:2 * D])
    g = _nn(t[:, 128:N_LORA], wl[128:N_LORA, 2 * D:3 * D])
    w_log = -_softplus(-(w0 + lo[:, 0:D])) - 0.5
    lw = -jnp.exp(w_log)
    a = jax.nn.sigmoid(a0 + lo[:, D:2 * D])
    k_mod = k * (1.0 + (a - 1.0) * k_a)
    kk = k * k_k
    kk = kk / jnp.maximum(jnp.sqrt(_hsum(kk * kk, e, et)), 1e-12)
    return r, lw, k_mod, v, -kk, kk * a, g


def _f_rwpost(y, r, v, k_mod, g, lnx_w, lnx_b, r_k, e, et):
    mean = _hsum(y, e, et) * (1.0 / 64)
    yc = y - mean
    var = _hsum(yc * yc, e, et) * (1.0 / 64)
    yn = yc * lax.rsqrt(var + GN_EPS) * lnx_w + lnx_b
    bonus = _hsum(r * k_mod * r_k, e, et) * v
    return ((yn + bonus) * g,)


def _f_mix(gi, ya, yr, bg):
    gate = jax.nn.sigmoid(gi + bg)
    return (gate[:, 0:D] * ya + gate[:, D:2 * D] * yr,)


def _f_adamw(w, g, m, v):
    m = ADAM_B1 * m + (1.0 - ADAM_B1) * g
    v = ADAM_B2 * v + (1.0 - ADAM_B2) * jnp.square(g)
    m_hat = m / (1.0 - ADAM_B1 ** ADAM_STEP)
    v_hat = v / (1.0 - ADAM_B2 ** ADAM_STEP)
    return -ADAM_LR * (m_hat / (jnp.sqrt(v_hat) + ADAM_EPS) + ADAM_WD * w), m, v


def _down(x, k):
    row = lax.broadcasted_iota(jnp.int32, x.shape, 0)
    return jnp.where(row < k, 0.0, pltpu.roll(x, k, 0))


def _up(x, k):
    n = x.shape[0]
    row = lax.broadcasted_iota(jnp.int32, x.shape, 0)
    return jnp.where(row >= n - k, 0.0, pltpu.roll(x, n - k, 0))


def _col_spec(s, w, off=0):
    return pl.BlockSpec((s, w), lambda j: (0, j + off))


def _shift_fwd(z, mu):
    s, n = z.shape

    def body(z_ref, mu_ref, o_ref):
        zz = z_ref[...]
        o_ref[...] = zz + (_down(zz, 1) - zz) * mu_ref[...]

    return _pcall(
        body, name="shift_fwd", grid=(n // 128,), in_specs=[_col_spec(s, 128), _col_spec(1, 128)],
        out_specs=_col_spec(s, 128), out_shape=jax.ShapeDtypeStruct((s, n), F32),
        compiler_params=_cparams(("parallel",)),
    )(z, mu)


def _shift_bwd(z, mu, dzs):
    s, n = z.shape

    def body(z_ref, mu_ref, d_ref, dz_ref, dmu_ref):
        zz, d, m = z_ref[...], d_ref[...], mu_ref[...]
        dm = d * m
        dz_ref[...] = (d - dm + _up(dm, 1)).astype(dz_ref.dtype)
        dmu_ref[...] = jnp.sum(d * (_down(zz, 1) - zz), axis=0, keepdims=True)

    return _pcall(
        body, name="shift_bwd", grid=(n // 128,), in_specs=[_col_spec(s, 128), _col_spec(1, 128), _col_spec(s, 128)],
        out_specs=[_col_spec(s, 128), _col_spec(1, 128)],
        out_shape=[jax.ShapeDtypeStruct((s, n), BF16), jax.ShapeDtypeStruct((1, n), F32)],
        compiler_params=_cparams(("parallel",)),
    )(z, mu, dzs)


def _conv3(x, w_ref, b_ref):
    return b_ref[...] + w_ref[0:1, :] * _down(x, 2) + w_ref[1:2, :] * _down(x, 1) + w_ref[2:3, :] * x


def _conv_fwd(u, cw, cb):
    s = u.shape[0]
    nb = D_FF // 128

    def body(ug_ref, uv_ref, wg_ref, wv_ref, bg_ref, bv_ref, o_ref):
        gate = _conv3(ug_ref[...], wg_ref, bg_ref)
        val = _conv3(uv_ref[...], wv_ref, bv_ref)
        o_ref[...] = (gate * jax.nn.sigmoid(gate) * val).astype(o_ref.dtype)

    return _pcall(
        body, name="conv_fwd", grid=(nb,),
        in_specs=[_col_spec(s, 128), _col_spec(s, 128, nb), _col_spec(3, 128), _col_spec(3, 128, nb),
                  _col_spec(1, 128), _col_spec(1, 128, nb)],
        out_specs=_col_spec(s, 128), out_shape=jax.ShapeDtypeStruct((s, D_FF), BF16),
        compiler_params=_cparams(("parallel",)),
    )(u, u, cw, cw, cb, cb)


def _conv_bwd(u, cw, cb, dact):
    s = u.shape[0]
    nb = D_FF // 128

    def half(x, d, w_ref, du_ref, dw_ref, db_ref):
        x1, x2 = _down(x, 1), _down(x, 2)
        du_ref[...] = (w_ref[2:3, :] * d + w_ref[1:2, :] * _up(d, 1) + w_ref[0:1, :] * _up(d, 2)).astype(du_ref.dtype)
        dw_ref[0:1, :] = jnp.sum(d * x2, axis=0, keepdims=True)
        dw_ref[1:2, :] = jnp.sum(d * x1, axis=0, keepdims=True)
        dw_ref[2:3, :] = jnp.sum(d * x, axis=0, keepdims=True)
        db_ref[...] = jnp.sum(d, axis=0, keepdims=True)

    def body(ug_ref, uv_ref, wg_ref, wv_ref, bg_ref, bv_ref, da_ref,
             dug_ref, duv_ref, dwg_ref, dwv_ref, dbg_ref, dbv_ref):
        ug, uv, da = ug_ref[...], uv_ref[...], da_ref[...]
        gate = _conv3(ug, wg_ref, bg_ref)
        val = _conv3(uv, wv_ref, bv_ref)
        sg = jax.nn.sigmoid(gate)
        dgate = da * val * sg * (1.0 + gate * (1.0 - sg))
        dval = da * gate * sg
        half(ug, dgate, wg_ref, dug_ref, dwg_ref, dbg_ref)
        half(uv, dval, wv_ref, duv_ref, dwv_ref, dbv_ref)

    dug, duv, dwg, dwv, dbg, dbv = _pcall(
        body, name="conv_bwd", grid=(nb,),
        in_specs=[_col_spec(s, 128), _col_spec(s, 128, nb), _col_spec(3, 128), _col_spec(3, 128, nb),
                  _col_spec(1, 128), _col_spec(1, 128, nb), _col_spec(s, 128)],
        out_specs=[_col_spec(s, 128), _col_spec(s, 128), _col_spec(3, 128), _col_spec(3, 128),
                   _col_spec(1, 128), _col_spec(1, 128)],
        out_shape=[jax.ShapeDtypeStruct((s, D_FF), BF16), jax.ShapeDtypeStruct((s, D_FF), BF16),
                   jax.ShapeDtypeStruct((3, D_FF), F32), jax.ShapeDtypeStruct((3, D_FF), F32),
                   jax.ShapeDtypeStruct((1, D_FF), F32), jax.ShapeDtypeStruct((1, D_FF), F32)],
        compiler_params=_cparams(("parallel",)),
    )(u, u, cw, cw, cb, cb, dact)
    return (jnp.concatenate([dug, duv], axis=1), jnp.concatenate([dwg, dwv], axis=1),
            jnp.concatenate([dbg, dbv], axis=1))


ATT_BATCH = 4


def _att_batch(q, kp, kc, vp, vc, first):
    ma = lax.broadcasted_iota(jnp.int32, (1, ATT_BLOCK, 128), 2) < 64

    def diag(x):
        return jnp.concatenate([jnp.where(ma, x, 0.0), jnp.where(ma, 0.0, x)], axis=1)

    qi = lax.broadcasted_iota(jnp.int32, (1, ATT_BLOCK, 2 * ATT_BLOCK), 1)
    kj = lax.broadcasted_iota(jnp.int32, (1, ATT_BLOCK, 2 * ATT_BLOCK), 2) & (ATT_BLOCK - 1)
    okp = kj >= qi + jnp.where(first, 2 * ATT_BLOCK, 0)
    okc = kj <= qi
    sp = jnp.where(okp, _bnt(q, diag(kp)) * 0.125, NEG)
    sc = jnp.where(okc, _bnt(q, diag(kc)) * 0.125, NEG)

    def per_head(fn, x):
        return fn(x[..., :ATT_BLOCK]), fn(x[..., ATT_BLOCK:])

    def spread(ab):
        return jnp.concatenate([jnp.broadcast_to(t, t.shape[:2] + (ATT_BLOCK,)) for t in ab], axis=-1)

    row_max = functools.partial(jnp.max, axis=-1, keepdims=True)
    row_sum = functools.partial(jnp.sum, axis=-1, keepdims=True)
    m = [lax.stop_gradient(jnp.maximum(a, b)) for a, b in zip(per_head(row_max, sp), per_head(row_max, sc))]
    pp, pc = jnp.exp(sp - spread(m)), jnp.exp(sc - spread(m))
    den = [a + b for a, b in zip(per_head(row_sum, pp), per_head(row_sum, pc))]
    num = _bnn(pp, diag(vp)) + _bnn(pc, diag(vc))
    out = num / jnp.where(ma, den[0], den[1])
    lse = jnp.where(ma, m[0] + jnp.log(den[0]), m[1] + jnp.log(den[1]))
    return out, jnp.broadcast_to(lse, out.shape)


def _att_pairs_per_step(dil):
    return 4 if dil == 1 else 1


def _att_residues(dil):
    return min(dil, ATT_BATCH // _att_pairs_per_step(dil))


def _att_specs(g, dil):
    rows, pp = ATT_BLOCK * dil, _att_pairs_per_step(dil)

    def cur(slot):
        return pl.BlockSpec((rows, 128 * pp), lambda n, p: (n, (g * 3 + slot) * (4 // pp) + p))

    def prev(slot):
        return pl.BlockSpec((rows, 128 * pp), lambda n, p: (jnp.maximum(n - 1, 0), (g * 3 + slot) * (4 // pp) + p))

    return [cur(0), prev(1), cur(1), prev(2), cur(2)]


def _att_out_spec(dil):
    return pl.BlockSpec((ATT_BLOCK * dil, 128 * _att_pairs_per_step(dil)), lambda n, p: (n, p))


def _att_grid(s, dil):
    return (s // (ATT_BLOCK * dil), 4 // _att_pairs_per_step(dil))


def _att_windows(i, dil):
    res = _att_residues(dil)

    def rows(r):
        return pl.ds(i * res + r, ATT_BLOCK, stride=dil) if dil > 1 else pl.ds(0, ATT_BLOCK)

    return [(rows(r), pl.ds(128 * j, 128)) for j in range(_att_pairs_per_step(dil)) for r in range(res)]


def _att_fwd(att_in, g, dil):
    s = att_in.shape[0]

    def body(q_ref, kp_ref, kc_ref, vp_ref, vc_ref, o_ref, l_ref):
        first = pl.program_id(0) == 0

        def one(i, carry):
            win = _att_windows(i, dil)
            vals = [jnp.stack([ref[w] for w in win]) for ref in (q_ref, kp_ref, kc_ref, vp_ref, vc_ref)]
            o, l = _att_batch(*vals, first)
            for j, w in enumerate(win):
                o_ref[w] = o[j]
                l_ref[w] = l[j]
            return carry

        lax.fori_loop(0, dil // _att_residues(dil), one, 0)

    return _pcall(
        body, name=f"att_fwd{g}", grid=_att_grid(s, dil), in_specs=_att_specs(g, dil),
        out_specs=[_att_out_spec(dil)] * 2, out_shape=[jax.ShapeDtypeStruct((s, ATT_WIDTH), F32)] * 2,
        compiler_params=_cparams(("parallel", "parallel")),
    )(att_in, att_in, att_in, att_in, att_in)


def _att_bwd(att_in, g, dil, do, dl, acc):
    s = att_in.shape[0]

    def body(q_ref, kp_ref, kc_ref, vp_ref, vc_ref, do_ref, dl_ref, dq_ref, dkp_ref, dkc_ref, dvp_ref, dvc_ref):
        first = pl.program_id(0) == 0

        def one(i, carry):
            win = _att_windows(i, dil)
            vals = [jnp.stack([ref[w] for w in win]) for ref in (q_ref, kp_ref, kc_ref, vp_ref, vc_ref)]
            _, vjp = jax.vjp(lambda *a: _att_batch(*a, first), *vals)
            grads = vjp((jnp.stack([do_ref[w] for w in win]), jnp.stack([dl_ref[w] for w in win])))
            for ref, gr in zip((dq_ref, dkp_ref, dkc_ref, dvp_ref, dvc_ref), grads):
                for j, w in enumerate(win):
                    ref[w] = gr[j]
            return carry

        lax.fori_loop(0, dil // _att_residues(dil), one, 0)

    dq, dkp, dkc, dvp, dvc = _pcall(
        body, name=f"att_bwd{g}", grid=_att_grid(s, dil), in_specs=_att_specs(g, dil) + [_att_out_spec(dil)] * 2,
        out_specs=[_att_out_spec(dil)] * 5, out_shape=[jax.ShapeDtypeStruct((s, ATT_WIDTH), F32)] * 5,
        compiler_params=_cparams(("parallel", "parallel")),
    )(att_in, att_in, att_in, att_in, att_in, do, dl)

    unit, rb = ATT_BLOCK * dil, 1024
    steps = s // rb
    within = unit < rb

    def shifted(cur_ref, next_ref, has_next):
        nxt = jnp.where(has_next, next_ref[...], 0.0)
        return jnp.concatenate([cur_ref[unit:, :], nxt], axis=0) if within else nxt

    def cbody(dq_ref, dkc_ref, dkp_ref, dkn_ref, dvc_ref, dvp_ref, dvn_ref, *rest):
        o_ref = rest[-1]
        has_next = pl.program_id(0) + (1 if within else unit // rb) < steps
        o_ref[:, 0:ATT_WIDTH] = dq_ref[...].astype(BF16)
        o_ref[:, ATT_WIDTH:2 * ATT_WIDTH] = (dkc_ref[...] + shifted(dkp_ref, dkn_ref, has_next)).astype(BF16)
        o_ref[:, 2 * ATT_WIDTH:3 * ATT_WIDTH] = (dvc_ref[...] + shifted(dvp_ref, dvn_ref, has_next)).astype(BF16)

    cur = pl.BlockSpec((rb, ATT_WIDTH), lambda i: (i, 0))
    if within:
        nxt = pl.BlockSpec((unit, ATT_WIDTH), lambda i: (jnp.minimum((i + 1) * (rb // unit), s // unit - 1), 0))
    else:
        nxt = pl.BlockSpec((rb, ATT_WIDTH), lambda i: (jnp.minimum(i + unit // rb, steps - 1), 0))
    carried = [] if acc is None else [acc]
    return _pcall(
        cbody, name=f"att_bwd_sum{g}", grid=(steps,),
        in_specs=[cur, cur, cur, nxt, cur, cur, nxt] + [pl.BlockSpec(memory_space=pl.ANY)] * len(carried),
        out_specs=pl.BlockSpec((rb, 3 * ATT_WIDTH), lambda i: (i, g)),
        out_shape=jax.ShapeDtypeStruct((s, N_ATT), BF16), input_output_aliases={7: 0} if carried else {},
        compiler_params=_cparams(("parallel",)),
    )(dq, dkc, dkp, dkp, dvc, dvp, dvp, *carried)


def _unit_lower_inverse_impl(n):
    eye = (lax.broadcasted_iota(jnp.int32, (1,) + n.shape[1:], 1)
           == lax.broadcasted_iota(jnp.int32, (1,) + n.shape[1:], 2))
    t = jnp.where(eye, 1.0, 0.0) + n
    pw = n
    for _ in range(5):
        pw = _bnn(pw, pw)
        t = t + _bnn(t, pw)
    return t


@jax.custom_vjp
def _unit_lower_inverse(n):
    return _unit_lower_inverse_impl(n)


def _unit_lower_inverse_fwd(n):
    t = _unit_lower_inverse_impl(n)
    return t, t


_unit_lower_inverse.defvjp(_unit_lower_inverse_fwd, lambda t, g: (_bnt(_btn(t, g), t),))


@jax.custom_vjp
def _known_inverse(n, t):
    return t


_known_inverse.defvjp(lambda n, t: (t, t), lambda t, g: (_bnt(_btn(t, g), t), jnp.zeros_like(t)))


def _scan_chunk(r, lw, k, v, a, b, s0, inverse):
    c = SCAN_CHUNK
    p = s0.shape[0]
    ri = lax.broadcasted_iota(jnp.int32, (c, c), 0)
    ci = lax.broadcasted_iota(jnp.int32, (c, c), 1)
    cum = jnp.dot((ci <= ri).astype(F32), lw, precision=HI, preferred_element_type=F32)
    tot = jnp.sum(lw, axis=0, keepdims=True)
    ma = (lax.broadcasted_iota(jnp.int32, (c, 128 * p), 1) & 127) < 64

    def pairs(x):
        return jnp.concatenate([x[None, :, 128 * j:128 * (j + 1)] for j in range(p)], axis=0)

    def stack(x):
        return jnp.concatenate([pairs(jnp.where(ma, x, 0.0)), pairs(jnp.where(ma, 0.0, x))], axis=1)

    einv, eend = jnp.exp(-cum), jnp.exp(tot - cum)
    ra, aa = stack(r * jnp.exp(cum)), stack(a * jnp.exp(cum - lw))
    bi, ki, be, ke, vs = stack(b * einv), stack(k * einv), stack(b * eend), stack(k * eend), stack(v)
    r2 = lax.broadcasted_iota(jnp.int32, (1, 2 * c, 2 * c), 1)
    c2 = lax.broadcasted_iota(jnp.int32, (1, 2 * c, 2 * c), 2)
    same = (r2 >= c) == (c2 >= c)
    strict = jnp.logical_and(same, c2 < r2)
    incl = jnp.logical_and(same, c2 <= r2)
    s0 = jnp.where(same, s0, 0.0)
    prod = _bnt(jnp.concatenate([aa, ra], axis=1), jnp.concatenate([bi, ki], axis=1))
    a_ab = jnp.where(strict, prod[:, :2 * c, :2 * c], 0.0)
    a_ak = jnp.where(strict, prod[:, :2 * c, 2 * c:], 0.0)
    a_rb = jnp.where(incl, prod[:, 2 * c:, :2 * c], 0.0)
    a_rk = jnp.where(incl, prod[:, 2 * c:, 2 * c:], 0.0)
    t = inverse(a_ab)
    u = _bnn(t, _bnt(aa, s0) + _bnn(a_ak, vs))
    uv = jnp.concatenate([u, vs], axis=1)
    ys = _bnt(ra, s0) + _bnn(jnp.concatenate([a_rb, a_rk], axis=2), uv)
    s1 = s0 * pairs(jnp.exp(tot)) + _btn(uv, jnp.concatenate([be, ke], axis=1))
    y3 = ys[:, :c] + ys[:, c:]
    return (jnp.concatenate([y3[j] for j in range(p)], axis=1), s1), t


def _scan_specs(rev, n):
    def at(i):
        return n - 1 - i if rev else i

    def cm(cb):
        return pl.BlockSpec((SCAN_CHUNK, D), lambda i: (at(i), cb))

    return cm, pl.BlockSpec((1, SCAN_PAIRS, 128, 128), lambda i: (at(i), 0, 0, 0))


def _comm_phases(comm, refs, n, step=None):
    k = comm.n
    srcs, outs, sems = refs[:k], refs[k:2 * k], refs[2 * k:]
    i = pl.program_id(0) if step is None else step

    def before():
        @pl.when(i == 0)
        def _():
            comm.first(srcs, outs, sems)

    def after():
        if comm.mid is not None:
            @pl.when(i == (3 * n) // 4)
            def _():
                comm.mid(srcs, outs, sems)

        @pl.when(i == n - 1)
        def _():
            comm.last(srcs, outs, sems)

    return before, after


def _scan_fwd(zs, lw, km, aa, bb, comm):
    s = zs.shape[0]
    n = s // SCAN_CHUNK
    cm, st = _scan_specs(False, n)
    k = comm.n

    def body(*refs):
        r_ref, lw_ref, k_ref, v_ref, a_ref, b_ref = refs[:6]
        y_ref, s0_ref, t_ref = refs[6 + k:9 + k]
        state = refs[9 + 2 * k]
        before, after = _comm_phases(comm, refs[6:6 + k] + refs[9 + k:9 + 2 * k] + refs[10 + 2 * k:], n)
        before()

        @pl.when(pl.program_id(0) == 0)
        def _():
            state[...] = jnp.zeros_like(state)

        s0 = state[...]
        s0_ref[0] = s0
        (y, s1), t = _scan_chunk(*[ref[...] for ref in (r_ref, lw_ref, k_ref, v_ref, a_ref, b_ref)], s0,
                                 _unit_lower_inverse)
        y_ref[...] = y
        t_ref[0] = t.astype(BF16)
        state[...] = s1
        after()

    per_chunk = (n, SCAN_PAIRS, 128, 128)
    res = _pcall(
        body, name="scan_fwd", grid=(n,), in_specs=[cm(0), cm(0), cm(0), cm(2), cm(0), cm(0)] + [_HBM] * k,
        out_specs=[cm(0), st, st] + [_HBM] * k,
        out_shape=[jax.ShapeDtypeStruct((s, D), F32), jax.ShapeDtypeStruct(per_chunk, F32),
                   jax.ShapeDtypeStruct(per_chunk, BF16)] + comm.out_shape,
        scratch_shapes=[pltpu.VMEM((SCAN_PAIRS, 128, 128), F32)] + comm.sems,
        compiler_params=_cparams(("arbitrary",)),
    )(zs, lw, km, zs, aa, bb, *comm.ins)
    return res[0], res[1], res[2], res[3:]


def _scan_bwd(zs, lw, km, aa, bb, s0s, ts, dy, comm):
    s = zs.shape[0]
    n = s // SCAN_CHUNK
    cm, st = _scan_specs(True, n)
    k = comm.n

    def body(*refs):
        r_ref, lw_ref, k_ref, v_ref, a_ref, b_ref, s0_ref, t_ref, dy_ref = refs[:9]
        douts = refs[9 + k:15 + k]
        dstate = refs[15 + 2 * k]
        before, after = _comm_phases(comm, refs[9:9 + k] + refs[15 + k:15 + 2 * k] + refs[16 + 2 * k:], n)
        before()

        @pl.when(pl.program_id(0) == 0)
        def _():
            dstate[...] = jnp.zeros_like(dstate)

        t = t_ref[0].astype(F32)
        prim = [ref[...] for ref in (r_ref, lw_ref, k_ref, v_ref, a_ref, b_ref)] + [s0_ref[0]]
        _, vjp, _ = jax.vjp(lambda *p: _scan_chunk(*p, lambda nil: _known_inverse(nil, t)), *prim, has_aux=True)
        grads = vjp((dy_ref[...], dstate[...]))
        for ref, gr in zip(douts, grads[:6]):
            ref[...] = gr
        dstate[...] = grads[6]
        after()

    res = _pcall(
        body, name="scan_bwd", grid=(n,),
        in_specs=[cm(0), cm(0), cm(0), cm(2), cm(0), cm(0), st, st, cm(0)] + [_HBM] * k,
        out_specs=[cm(0)] * 6 + [_HBM] * k, out_shape=[jax.ShapeDtypeStruct((s, D), F32)] * 6 + comm.out_shape,
        scratch_shapes=[pltpu.VMEM((SCAN_PAIRS, 128, 128), F32)] + comm.sems,
        compiler_params=_cparams(("arbitrary",)),
    )(zs, lw, km, zs, aa, bb, s0s, ts, dy, *comm.ins)
    return res[:6], res[6:]


_HBM = pl.BlockSpec(memory_space=pltpu.HBM)


def _me():
    return lax.axis_index("x"), lax.axis_index("y"), lax.axis_index("c")


def _allgather8(src, name):
    def body(src_ref, out_ref, ssem, rsem, lsem):
        x, y, c = _me()
        me = 4 * x + 2 * y + c
        local = pltpu.make_async_copy(src_ref, out_ref.at[me], lsem)
        local.start()
        peers = []
        for k in range(1, 8):
            peers.append(((1 - x) if k & 4 else x, (1 - y) if k & 2 else y, (1 - c) if k & 1 else c))
        sends = []
        for k, peer in enumerate(peers):
            cp = pltpu.make_async_remote_copy(src_ref, out_ref.at[me], ssem.at[k], rsem.at[k], device_id=peer,
                                              device_id_type=MESH)
            cp.start()
            sends.append(cp)
        for k, (px, py, pc) in enumerate(peers):
            pltpu.make_async_remote_copy(src_ref, out_ref.at[4 * px + 2 * py + pc], ssem.at[k], rsem.at[k],
                                         device_id=(px, py, pc), device_id_type=MESH).wait_recv()
        for cp in sends:
            cp.wait_send()
        local.wait()

    return _pcall(
        body, name=name, in_specs=[_HBM], out_specs=_HBM, out_shape=jax.ShapeDtypeStruct((8,) + src.shape, src.dtype),
        scratch_shapes=[pltpu.SemaphoreType.DMA((7,)), pltpu.SemaphoreType.DMA((7,)), pltpu.SemaphoreType.DMA],
    )(src)


def _other_chips(x, y):
    return [(1 - x, y), (x, 1 - y), (1 - x, 1 - y)]


def _remote(src, dst, ssem, rsem, to):
    return pltpu.make_async_remote_copy(src, dst, ssem, rsem, device_id=to, device_id_type=MESH)


class _GatherWeights:
    def __init__(self, shards):
        self.ins = list(shards)
        n = self.n = len(shards)
        self.out_shape = [jax.ShapeDtypeStruct((4,) + t.shape, t.dtype) for t in shards]
        self.sems = [pltpu.SemaphoreType.DMA((6 * n,)), pltpu.SemaphoreType.DMA((6 * n,)),
                     pltpu.SemaphoreType.DMA((n,)), pltpu.SemaphoreType.DMA((n,))]

    def _copies(self, srcs, outs, sems):
        ssem, rsem, lsem, osem = sems
        x, y, c = _me()
        me = 2 * x + y
        own, ici, landed, passed, passed_in = [], [], [], [], []
        for a in range(self.n):
            h = self.ins[a].shape[0] // 2
            mine, other = pl.ds(c * h, h), pl.ds((1 - c) * h, h)
            own.append(_remote(srcs[a], outs[a].at[me], lsem.at[a], osem.at[a], (x, y, 1 - c)))
            for k, (px, py) in enumerate(_other_chips(x, y)):
                s1, r1, s2, r2 = ssem.at[6 * a + k], rsem.at[6 * a + k], ssem.at[6 * a + 3 + k], rsem.at[6 * a + 3 + k]
                got, got_sib = outs[a].at[2 * px + py, mine], outs[a].at[2 * px + py, other]
                ici.append(_remote(srcs[a].at[mine], outs[a].at[me, mine], s1, r1, (px, py, c)))
                landed.append(_remote(got, got, s1, r1, (px, py, c)))
                passed.append(_remote(got, got, s2, r2, (x, y, 1 - c)))
                passed_in.append(_remote(got_sib, got_sib, s2, r2, (x, y, 1 - c)))
        return own, ici, landed, passed, passed_in

    def first(self, srcs, outs, sems):
        own, ici, _, _, _ = self._copies(srcs, outs, sems)
        for cp in own + ici:
            cp.start()

    def mid(self, srcs, outs, sems):
        _, _, landed, passed, _ = self._copies(srcs, outs, sems)
        for arrived, onward in zip(landed, passed):
            arrived.wait_recv()
            onward.start()

    def last(self, srcs, outs, sems):
        own, ici, _, passed, passed_in = self._copies(srcs, outs, sems)
        for cp in passed_in:
            cp.wait_recv()
        for cp in ici + passed:
            cp.wait_send()
        for cp in own:
            cp.wait()


class _ScatterToChips:
    def __init__(self, parts):
        self.ins = list(parts)
        n = self.n = len(parts)
        self.out_shape = [jax.ShapeDtypeStruct(t.shape, t.dtype) for t in parts]
        self.sems = [pltpu.SemaphoreType.DMA((3 * n,)), pltpu.SemaphoreType.DMA((3 * n,)), pltpu.SemaphoreType.DMA((n,))]

    def _copies(self, srcs, outs, sems):
        ssem, rsem, lsem = sems
        x, y, c = _me()
        me = 2 * x + y
        own, out, landed = [], [], []
        for a in range(self.n):
            own.append(pltpu.make_async_copy(srcs[a].at[me], outs[a].at[me], lsem.at[a]))
            for k, (px, py) in enumerate(_other_chips(x, y)):
                dst = outs[a].at[2 * px + py]
                out.append(_remote(srcs[a].at[2 * px + py], outs[a].at[me], ssem.at[3 * a + k], rsem.at[3 * a + k],
                                   (px, py, c)))
                landed.append(_remote(dst, dst, ssem.at[3 * a + k], rsem.at[3 * a + k], (px, py, c)))
        return own, out, landed

    def first(self, srcs, outs, sems):
        own, out, _ = self._copies(srcs, outs, sems)
        for cp in own + out:
            cp.start()

    mid = None

    def last(self, srcs, outs, sems):
        own, out, landed = self._copies(srcs, outs, sems)
        for cp in landed:
            cp.wait_recv()
        for cp in own:
            cp.wait()
        for cp in out:
            cp.wait_send()


def _run_comm(comm, name):
    n = comm.n

    def body(*refs):
        srcs, outs, sems = refs[:n], refs[n:2 * n], refs[2 * n:]
        comm.first(srcs, outs, sems)
        if comm.mid is not None:
            comm.mid(srcs, outs, sems)
        comm.last(srcs, outs, sems)

    return _pcall(body, name=name, in_specs=[_HBM] * n, out_specs=[_HBM] * n, out_shape=comm.out_shape,
                  scratch_shapes=comm.sems)(*comm.ins)


class _NoComm:
    n, ins, out_shape, sems, mid = 0, [], [], [], None

    def first(self, srcs, outs, sems):
        pass

    def last(self, srcs, outs, sems):
        pass


_NOTHING = _NoComm()


class _SiblingHalves:
    mid = None

    def __init__(self, grads):
        self.ins = list(grads)
        n = self.n = len(grads)
        self.out_shape = [jax.ShapeDtypeStruct((4, t.shape[1] // 2, t.shape[2]), t.dtype) for t in grads]
        self.sems = [pltpu.SemaphoreType.DMA((n,)), pltpu.SemaphoreType.DMA((n,))]

    def _copies(self, srcs, outs, sems):
        ssem, rsem = sems
        x, y, c = _me()
        copies = []
        for a in range(self.n):
            h = self.ins[a].shape[1] // 2
            copies.append(_remote(srcs[a].at[:, pl.ds((1 - c) * h, h)], outs[a], ssem.at[a], rsem.at[a], (x, y, 1 - c)))
        return copies

    def first(self, srcs, outs, sems):
        for cp in self._copies(srcs, outs, sems):
            cp.start()

    def last(self, srcs, outs, sems):
        for cp in self._copies(srcs, outs, sems):
            cp.wait()


def _reduce_finish(reds, name):
    n = len(reds)

    def body(*refs):
        outs = refs[n:2 * n]
        ssem, rsem = refs[2 * n:]
        x, y, c = _me()
        copies = []
        for a in range(n):
            h = reds[a].shape[0] // 2
            mine = outs[a].at[pl.ds(c * h, h)]
            copies.append(_remote(mine, mine, ssem.at[a], rsem.at[a], (x, y, 1 - c)))
        for cp in copies:
            cp.start()
        for a in range(n):
            h = reds[a].shape[0] // 2
            dst = outs[a].at[pl.ds((1 - c) * h, h)]
            _remote(dst, dst, ssem.at[a], rsem.at[a], (x, y, 1 - c)).wait_recv()
        for cp in copies:
            cp.wait_send()

    return _pcall(
        body, name=name, in_specs=[_HBM] * n, out_specs=[_HBM] * n,
        out_shape=[jax.ShapeDtypeStruct(t.shape, t.dtype) for t in reds],
        input_output_aliases={a: a for a in range(n)},
        scratch_shapes=[pltpu.SemaphoreType.DMA((n,)), pltpu.SemaphoreType.DMA((n,))],
    )(*reds)


def _half_sum(fn, full, halves, out_full, out_dtype, core, name):
    p, h, c = (halves[0].shape if halves else (full[0].shape[0], full[0].shape[1] // 2, full[0].shape[2]))
    br = _div(h, max(16, (1 << 19) // (p * c)), 16)
    nb = h // br
    mine3 = pl.BlockSpec((p, br, c), lambda i, core_ref: (0, core_ref[0] * nb + i, 0))
    half3 = pl.BlockSpec((p, br, c), lambda i, core_ref: (0, i, 0))

    def body(core_ref, *refs):
        refs[-1][...] = fn(*[t[...].astype(F32) for t in refs[:-1]]).astype(out_dtype)

    if out_full:
        out_spec = pl.BlockSpec((br, c), lambda i, core_ref: (core_ref[0] * nb + i, 0))
        out_shape = jax.ShapeDtypeStruct((2 * h, c), out_dtype)
    else:
        out_spec, out_shape = half3, jax.ShapeDtypeStruct((p, h, c), out_dtype)
    return _pcall(
        body, name=name,
        grid_spec=pltpu.PrefetchScalarGridSpec(
            num_scalar_prefetch=1, grid=(nb,), in_specs=[mine3] * len(full) + [half3] * len(halves),
            out_specs=out_spec),
        out_shape=out_shape, compiler_params=_cparams(("parallel",)),
    )(core, *full, *halves)


def _ada_fwd(c_all, w, b):
    def body(c_ref, w_ref, b_ref, o_ref):
        o_ref[...] = jnp.dot(c_ref[...], w_ref[...], precision=HI, preferred_element_type=F32) + b_ref[...]

    return _pcall(body, name="ada_fwd", out_shape=jax.ShapeDtypeStruct((c_all.shape[0], w.shape[1]), F32),
                  compiler_params=pltpu.CompilerParams(vmem_limit_bytes=VMEM_LIMIT))(c_all, w, b)


def _ada_bwd(c_all_t, d):
    def body(c_ref, d_ref, o_ref):
        o_ref[...] = jnp.dot(c_ref[...], d_ref[...], precision=HI, preferred_element_type=F32)

    return _pcall(body, name="ada_bwd", out_shape=jax.ShapeDtypeStruct((c_all_t.shape[0], d.shape[1]), F32),
                  compiler_params=pltpu.CompilerParams(vmem_limit_bytes=VMEM_LIMIT))(c_all_t, d)


def _sum_lead(x, name):
    p, r, n = x.shape
    br = _div(r, 512, 8)

    def body(x_ref, o_ref):
        acc = x_ref[0]
        for j in range(1, p):
            acc = acc + x_ref[j]
        o_ref[...] = acc

    return _pcall(
        body, name=name, grid=(r // br,), in_specs=[pl.BlockSpec((p, br, n), lambda i: (0, i, 0))],
        out_specs=pl.BlockSpec((br, n), lambda i: (i, 0)), out_shape=jax.ShapeDtypeStruct((r, n), F32),
        compiler_params=_cparams(("parallel",)),
    )(x)


def _adamw(w, g, m, v, name):
    shape = w.shape
    cols = shape[-1]
    w2, g2, m2, v2 = [t.reshape(-1, cols) for t in (w, g, m, v)]
    rows = w2.shape[0]
    pref = max(8, (1 << 19) // cols // 8 * 8)
    br = _div(rows, pref, 8)
    if rows // br > 64:
        br = pref
    outs = _rows_fwd(_f_adamw, [(t, cols, 0) for t in (w2, g2, m2, v2)], [], [(cols, F32)] * 3, name=name, br=br)
    return [o.reshape(shape) for o in outs]


_BIG = (("w_in", 1), ("w_up", 1), ("w_down", 0), ("w_o", 0), ("w_rwkv_out", 0), ("w_att_out", 1), ("w2", 1), ("a2", 1),
        ("g2", 1))


_NEEDED_FIRST = ("w_in", "w_att_out", "w2", "a2", "g2")
_NEEDED_LATER = ("w_up", "w_down", "w_o", "w_rwkv_out")
_DONE_EARLY = ("w_up", "w_down", "w_o", "w_rwkv_out", "w_att_out")
_DONE_LATE = ("w_in", "w2", "a2", "g2")


def _cols_joined(t):
    return jnp.concatenate([t[j] for j in range(4)], axis=1)


def _cols_split(t):
    n = t.shape[1] // 4
    return jnp.stack([t[:, j * n:(j + 1) * n] for j in range(4)])


W_IN_SHARD = (N_ATT + N_RW + N_GATE) // 4
W_IN_PAD = 2560


def _row_window(parts, lo, hi):
    out, pos = [], 0
    for t, w in parts:
        a, b = max(lo, pos), min(hi, pos + w)
        if a < b:
            out.append(t[a - pos:b - pos])
        pos += w
    return out[0] if len(out) == 1 else jnp.concatenate(out, axis=0)


def _rows_joined(t):
    return t.reshape(4 * t.shape[1], t.shape[2])


def _rows_split(t):
    return t.reshape(4, t.shape[0] // 4, t.shape[1])


def _step_to_scan(x, tgt, ada, wts):
    sh1, sc1, gt1, sh2, sc2, gt2 = ada
    br = 256
    grp = lax.broadcasted_iota(jnp.int32, (D, 128), 0) // 64 == lax.broadcasted_iota(jnp.int32, (D, 128), 1)
    e = grp.astype(F32)
    et = e.T
    w_in = [(wts["w_in"][j], W_IN_SHARD) for j in range(4)]
    w_att = _row_window(w_in, 0, N_ATT)
    w_rw = jnp.concatenate([_row_window(w_in, N_ATT, N_ATT + N_RW), jnp.zeros((N_RWP - N_RW, D), BF16)], axis=0)
    w_gate = _row_window(w_in, N_ATT + N_RW, N_ATT + N_RW + N_GATE)
    mu = jnp.pad(wts["mu_shift"], ((0, 0), (0, N_RWP - N_RW)))
    wl = jnp.zeros((N_LORA, 3 * D), F32)
    wl = wl.at[0:64, 0:D].set(_cols_joined(wts["w2"]).astype(F32))
    wl = wl.at[64:128, D:2 * D].set(_cols_joined(wts["a2"]).astype(F32))
    wl = wl.at[128:288, 2 * D:3 * D].set(_cols_joined(wts["g2"]).astype(F32))
    pre1_c = [wts["norm1_w"], sc1, sh1]
    (h1,) = _rows_fwd(_f_pre, [(x, D, 0)], pre1_c, [(D, BF16), None], name="pre1_fwd", br=2 * br)
    att_in = _mm(h1, w_att, tb=True, name="mm_att_in")
    z = _mm(h1, w_rw, tb=True, name="mm_rw_in")
    gate_in = _mm(h1, w_gate, tb=True, name="mm_gate_in")
    att_o, att_l = [], []
    for g, (_, dil) in enumerate(ATT_PATTERNS):
        o, l = _att_fwd(att_in, g, dil)
        att_o.append(o)
        att_l.append(l)
    comb_rows = [(t, ATT_WIDTH, 0) for t in att_o + att_l]
    (att,) = _rows_fwd(_f_comb, comb_rows, [], [(ATT_WIDTH, BF16)], name="comb_fwd", br=2 * br)
    y_att = _mm(att, wts["w_att_out"], b_chip=True, name="mm_att_out")
    zs = _shift_fwd(z, mu)
    rwpre_c = [wts["w0"], wts["a0"], wts["k_k"], wts["k_a"], wl, e, et]
    lw, km, aa, bb, gg = _rows_fwd(_f_rwpre, [(zs, N_RWP, 0)], rwpre_c,
                                   [None, (D, F32), (D, F32), None, (D, F32), (D, F32), (D, F32)],
                                   name="rwpre_fwd", br=br)
    return dict(x=x, tgt=tgt, wts=wts, br=br, e=e, et=et, gt1=gt1, sc2=sc2, sh2=sh2, gt2=gt2, w_att=w_att, w_rw=w_rw,
                w_gate=w_gate, mu=mu, pre1_c=pre1_c, h1=h1, att_in=att_in, z=z, gate_in=gate_in, comb_rows=comb_rows,
                att=att, y_att=y_att, zs=zs, rwpre_c=rwpre_c, lw=lw, km=km, aa=aa, bb=bb, gg=gg)


def _step_between_scans(st, y_raw, late):
    x, tgt, wts, br, e, et = st["x"], st["tgt"], st["wts"], st["br"], st["e"], st["et"]
    zs, km, gg, gate_in, y_att, att = st["zs"], st["km"], st["gg"], st["gate_in"], st["y_att"], st["att"]
    comb_rows, att_in = st["comb_rows"], st["att_in"]
    gt1, sc2, sh2, gt2 = st["gt1"], st["sc2"], st["sh2"], st["gt2"]
    w_up, w_ao = late["w_up"], wts["w_att_out"]
    w_down, w_o, w_ro = _rows_joined(late["w_down"]), _rows_joined(late["w_o"]), _rows_joined(late["w_rwkv_out"])
    post_rows = [(y_raw, D, 0), (zs, D, 0), (zs, D, 2), (km, D, 0), (gg, D, 0)]
    post_c = [wts["lnx_w"], wts["lnx_b"], wts["r_k"], e, et]
    (rw_out,) = _rows_fwd(_f_rwpost, post_rows, post_c, [(D, BF16)], name="rwpost_fwd", br=br)
    y_rw = _mm(rw_out, w_ro, name="mm_rw_out")
    mix_rows = [(gate_in, N_GATE, 0), (y_att, D, 0), (y_rw, D, 0)]
    (mix,) = _rows_fwd(_f_mix, mix_rows, [wts["b_gate"]], [(D, BF16)], name="mix_fwd", br=2 * br)
    o = _mm(mix, w_o, name="mm_o")
    pre2_c = [gt1, wts["norm2_w"], sc2, sh2]
    x1, h2 = _rows_fwd(_f_pre2, [(x, D, 0), (o, D, 0)], pre2_c, [(D, F32), (D, BF16)], name="pre2_fwd", br=2 * br)
    u = _mm(h2, w_up, b_chip=True, name="mm_up")
    act = _conv_fwd(u, wts["conv_w"], wts["conv_b"])
    f = _mm(act, w_down, name="mm_down")
    fin_rows = [(x1, D, 0), (f, D, 0), (tgt, D, 0)]
    fin_c = [gt2, wts["norm_f_w"]]

    def fin_fwd(*a):
        (l,) = _f_fin(*a)
        return (jnp.broadcast_to(jnp.sum(l, axis=0, keepdims=True), (8, 128)),)

    (loss_acc,) = _rows_fwd(fin_fwd, fin_rows, fin_c, [], name="fin_fwd", br=2 * br, acc_shape=(8, 128))

    gw = {}
    dx1a, df, d_gt2, gw["norm_f_w"] = _rows_bwd(
        _f_fin, fin_rows, fin_c, [[]], wrt_rows=[0, 1], wrt_consts=[0, 1], drow_dtypes=[F32, BF16],
        name="fin_bwd", br=2 * br, unit_cot=True)
    dact = _mm(df, w_down, tb=True, name="mm_dact")
    gw["w_down"] = _rows_split(_mm(act, df, ta=True, name="mm_dw_down"))
    du, gw["conv_w"], gw["conv_b"] = _conv_bwd(u, wts["conv_w"], wts["conv_b"], dact)
    dh2 = _mm(du, w_up, tb=True, b_chip=True, name="mm_dh2")
    gw["w_up"] = _mm(h2, du, ta=True, out_chip=True, name="mm_dw_up")
    dxa, do, d_gt1, gw["norm2_w"], d_sc2, d_sh2 = _rows_bwd(
        _f_pre2, [(x, D, 0), (o, D, 0)], pre2_c, [[(dx1a, D, 0)], [(dh2, D, 0)]], wrt_rows=[0, 1],
        wrt_consts=[0, 1, 2, 3], drow_dtypes=[F32, BF16], name="pre2_bwd", br=2 * br)
    dmix = _mm(do, w_o, tb=True, name="mm_dmix")
    gw["w_o"] = _rows_split(_mm(mix, do, ta=True, name="mm_dw_o"))
    dgate, dya, dyr, gw["b_gate"] = _rows_bwd(
        _f_mix, mix_rows, [wts["b_gate"]], [[(dmix, D, 0)]], wrt_rows=[0, 1, 2], wrt_consts=[0],
        drow_dtypes=[BF16] * 3, name="mix_bwd", br=2 * br)
    datt = _mm(dya, w_ao, tb=True, b_chip=True, name="mm_datt")
    gw["w_att_out"] = _mm(att, dya, ta=True, out_chip=True, name="mm_dw_att_out")
    drw = _mm(dyr, w_ro, tb=True, name="mm_drw")
    gw["w_rwkv_out"] = _rows_split(_mm(rw_out, dyr, ta=True, name="mm_dw_rw_out"))
    dcomb = _rows_bwd(_f_comb, comb_rows, [], [[(datt, ATT_WIDTH, 0)]], wrt_rows=list(range(6)), wrt_consts=[],
                      drow_dtypes=[F32] * 6, name="comb_bwd", br=2 * br)
    datt_in = None
    for g, (_, dil) in enumerate(ATT_PATTERNS):
        datt_in = _att_bwd(att_in, g, dil, dcomb[g], dcomb[3 + g], datt_in)
    dy_raw, dr_p, dv_p, dkm_p, dgg, gw["lnx_w"], gw["lnx_b"], gw["r_k"], *recv_early = _rows_bwd(
        _f_rwpost, post_rows, post_c, [[(drw, D, 0)]], wrt_rows=[0, 1, 2, 3, 4], wrt_consts=[0, 1, 2],
        drow_dtypes=[F32] * 5, name="rwpost_bwd", br=br, comm=_SiblingHalves([gw[n] for n in _DONE_EARLY]))
    st.update(loss=loss_acc[0, 0], gw=gw, dxa=dxa, dgate=dgate, datt_in=datt_in,
              dy_raw=dy_raw, dr_p=dr_p, dv_p=dv_p, dkm_p=dkm_p, dgg=dgg, d_ada_late=(d_gt1, d_sh2, d_sc2, d_gt2),
              recv_early=recv_early)
    return st


def _chip_parts(grads, recv, names, core):
    return [_half_sum(lambda a, b: a + b, [g], [r], False, BF16, core, "reduce_add2_" + n)
            for g, r, n in zip(grads, recv, names)]


def _step_after_scan(st, scan_grads, core):
    x, br, gw, h1, zs = st["x"], st["br"], st["gw"], st["h1"], st["zs"]
    dr_s, dlw, dkm_s, dv_s, daa, dbb = scan_grads
    pre_cots = [[(st["dr_p"], D, 0), (dr_s, D, 0)], [(dlw, D, 0)], [(st["dkm_p"], D, 0), (dkm_s, D, 0)],
                [(st["dv_p"], D, 0), (dv_s, D, 0)], [(daa, D, 0)], [(dbb, D, 0)], [(st["dgg"], D, 0)]]
    dzs, gw["w0"], gw["a0"], gw["k_k"], gw["k_a"], dwl = _rows_bwd(
        _f_rwpre, [(zs, N_RWP, 0)], st["rwpre_c"], pre_cots, wrt_rows=[0], wrt_consts=[0, 1, 2, 3, 4],
        drow_dtypes=[F32], name="rwpre_bwd", br=128)
    gw["w2"], gw["a2"] = _cols_split(dwl[0:64, 0:D]), _cols_split(dwl[64:128, D:2 * D])
    gw["g2"] = _cols_split(dwl[128:288, 2 * D:3 * D])
    dz, dmu = _shift_bwd(st["z"], st["mu"], dzs)
    gw["mu_shift"] = dmu[:, :N_RW]
    datt_in, dgate = st["datt_in"], st["dgate"]
    dw_in = [(_mm(datt_in, h1, ta=True, name="mm_dw_att"), N_ATT), (_mm(dz, h1, ta=True, name="mm_dw_rw"), N_RW),
             (_mm(dgate, h1, ta=True, name="mm_dw_gate"), N_GATE)]
    slabs = []
    for j in range(4):
        slabs += [_row_window(dw_in, j * W_IN_SHARD, (j + 1) * W_IN_SHARD), jnp.zeros((W_IN_PAD - W_IN_SHARD, D), F32)]
    gw["w_in"] = jnp.concatenate(slabs, axis=0).reshape(4, W_IN_PAD, D)
    late = [gw[n] for n in _DONE_LATE]
    parts = _chip_parts(late, _run_comm(_SiblingHalves(late), "reduce_sib_late"), _DONE_LATE, core)
    dh1, slots_late = _mm_sum([(datt_in, st["w_att"]), (dz, st["w_rw"]), (dgate, st["w_gate"])],
                              comm=_ScatterToChips(parts), name="mm_dh1")
    grad_x, gw["norm1_w"], d_sc1, d_sh1 = _rows_bwd(
        _f_pre, [(x, D, 0)], st["pre1_c"], [[(dh1, D, 0)], [(st["dxa"], D, 0)]], wrt_rows=[0], wrt_consts=[0, 1, 2],
        drow_dtypes=[F32], name="pre1_bwd", br=2 * br)
    d_gt1, d_sh2, d_sc2, d_gt2 = st["d_ada_late"]
    return st["loss"], grad_x, (d_sh1, d_sc1, d_gt1, d_sh2, d_sc2, d_gt2), gw, slots_late


_SMALL = ("b_ada", "norm1_w", "b_gate", "mu_shift", "w0", "a0", "k_k", "k_a", "r_k", "lnx_w", "lnx_b", "norm2_w",
          "conv_b", "norm_f_w")
_NAMES = ("w_ada", "b_ada", "norm1_w", "w_in", "b_gate", "mu_shift", "w0", "w2", "a0", "a2", "g2", "k_k", "k_a", "r_k",
          "lnx_w", "lnx_b", "w_att_out", "w_rwkv_out", "w_o", "norm2_w", "w_up", "conv_w", "conv_b", "w_down",
          "norm_f_w")


def kernel(x, c, w_ada, b_ada, norm1_w, w_in, b_gate, mu_shift, w0, w2, a0, a2, g2, k_k, k_a, r_k, lnx_w, lnx_b, w_att_out, w_rwkv_out, w_o, norm2_w, w_up, conv_w, conv_b, w_down, norm_f_w, loss_target, m_w_ada, m_b_ada, m_norm1_w, m_w_in, m_b_gate, m_mu_shift, m_w0, m_w2, m_a0, m_a2, m_g2, m_k_k, m_k_a, m_r_k, m_lnx_w, m_lnx_b, m_w_att_out, m_w_rwkv_out, m_w_o, m_norm2_w, m_w_up, m_conv_w, m_conv_b, m_w_down, m_norm_f_w, v_w_ada, v_b_ada, v_norm1_w, v_w_in, v_b_gate, v_mu_shift, v_w0, v_w2, v_a0, v_a2, v_g2, v_k_k, v_k_a, v_r_k, v_lnx_w, v_lnx_b, v_w_att_out, v_w_rwkv_out, v_w_o, v_norm2_w, v_w_up, v_conv_w, v_conv_b, v_w_down, v_norm_f_w):
    args = dict(locals())
    p, pm, pv = {}, {}, {}
    for name in _NAMES:
        for dst, key in ((p, name), (pm, "m_" + name), (pv, "v_" + name)):
            t = args[key]
            if name == "w_in":
                dst[name] = jnp.swapaxes(t, 1, 2)[0]
            else:
                dst[name] = t.reshape(1, -1) if name in ("r_k", "norm_f_w") else t.reshape(t.shape[-2], t.shape[-1])
    xi, yi, ci = _me()
    chip = 2 * xi + yi
    dev = 4 * xi + 2 * yi + ci
    x2, tgt = x[0], loss_target[0]

    n_cw = 3 * (2 * D_FF // 4)
    vec = jnp.concatenate([c.reshape(-1), p["conv_w"].reshape(-1), jnp.zeros((8 * D - D - n_cw,), F32)]).reshape(8, D)
    g0 = _allgather8(vec, "gather_c").reshape(8, 8 * D)
    c_all = g0[:, :D]
    conv_w_full = jnp.concatenate([g0[2 * j, D:D + n_cw].reshape(3, -1) for j in range(4)], axis=1)
    n_ada = 6 * D // 4
    b_ada_sh = lax.dynamic_slice(p["b_ada"], (0, chip * n_ada), (1, n_ada))
    ada_sh = _ada_fwd(c_all, p["w_ada"], b_ada_sh)
    ga = _allgather8(ada_sh, "gather_ada")
    ada_all = jnp.concatenate([ga[2 * j] for j in range(4)], axis=1)
    ada_row = lax.dynamic_slice(ada_all, (dev, 0), (1, 6 * D))
    ada = [ada_row[:, j * D:(j + 1) * D] for j in range(6)]

    big = [n for n, _ in _BIG]
    shard = {n: p[n].astype(BF16) for n in big}
    shard["w_in"] = jnp.pad(shard["w_in"], ((0, W_IN_PAD - W_IN_SHARD), (0, 0)))
    wts = dict(zip(_NEEDED_FIRST, _run_comm(_GatherWeights([shard[n] for n in _NEEDED_FIRST]), "gather_w")))
    for n in _SMALL:
        wts[n] = p[n]
    wts["conv_w"] = conv_w_full
    core = ci.reshape(1).astype(jnp.int32)

    st = _step_to_scan(x2, tgt, ada, wts)
    y_raw, s0s, inverses, late = _scan_fwd(st["zs"], st["lw"], st["km"], st["aa"], st["bb"],
                                           _GatherWeights([shard[n] for n in _NEEDED_LATER]))
    st = _step_between_scans(st, y_raw, dict(zip(_NEEDED_LATER, late)))
    early = _chip_parts([st["gw"][n] for n in _DONE_EARLY], st["recv_early"], _DONE_EARLY, core)
    scan_grads, slots_early = _scan_bwd(st["zs"], st["lw"], st["km"], st["aa"], st["bb"], s0s, inverses,
                                        st["dy_raw"], _ScatterToChips(early))
    loss_part, grad_x, d_ada, gw, slots_late = _step_after_scan(st, scan_grads, core)

    small = [jnp.concatenate(d_ada, axis=1)] + [gw[n] for n in _SMALL[1:]] + [gw["conv_w"], loss_part.reshape(1, 1)]
    sizes = [t.size for t in small]
    flat = jnp.concatenate([t.reshape(-1) for t in small])
    npad = (-flat.shape[0]) % (8 * D)
    srows = (flat.shape[0] + npad) // D
    flat = jnp.concatenate([flat, jnp.zeros((npad,), F32)]).reshape(srows, D)
    parts = _allgather8(flat, "gather_small")
    tot = _sum_lead(parts, "sum_small").reshape(-1)
    pieces, pos = [], 0
    for sz in sizes:
        pieces.append(tot[pos:pos + sz])
        pos += sz
    grads = {}
    for n, piece in zip(_SMALL, pieces[:len(_SMALL)]):
        grads[n] = piece.reshape(p[n].shape)
    conv_w_grad = pieces[len(_SMALL)].reshape(3, 2 * D_FF)
    grads["conv_w"] = lax.dynamic_slice(conv_w_grad, (0, chip * (n_cw // 3)), (3, n_cw // 3))
    loss = pieces[-1][0]
    d_ada_all = parts[:, :6].reshape(8, 6 * D)
    grads["w_ada"] = _ada_bwd(c_all.T, lax.dynamic_slice(d_ada_all, (0, chip * n_ada), (8, n_ada)))

    order = _DONE_EARLY + _DONE_LATE
    reds = [_half_sum(lambda t: t[0] + t[1] + t[2] + t[3], [], [t], True, F32, core, "reduce_add4_" + n)
            for n, t in zip(order, list(slots_early) + list(slots_late))]
    for n, g in zip(order, _reduce_finish(reds, "reduce_sib2")):
        grads[n] = g

    outs_g, outs_d, outs_m, outs_v = [], [], [], []
    grads["w_in"] = grads["w_in"][:W_IN_SHARD]
    for name in _NAMES:
        g = grads[name]
        d, m, v = _adamw(p[name], g, pm[name], pv[name], "adamw_" + name)
        shape = args[name].shape
        for outs, t in ((outs_g, g), (outs_d, d), (outs_m, m), (outs_v, v)):
            outs.append(jnp.swapaxes(t[None], 1, 2) if name == "w_in" else t.reshape(shape))
    return (loss, grad_x.reshape(x.shape), *outs_g, *outs_d, *outs_m, *outs_v)
```

```python
import functools

import jax
import jax.numpy as jnp
from jax import lax
from jax.experimental import pallas as pl
from jax.experimental.pallas import tpu as pltpu

F32 = jnp.float32
BF16 = jnp.bfloat16
HI = lax.Precision.HIGHEST
MESH = pl.DeviceIdType.MESH

D = 1024
ATT_PATTERNS = ((128, 1), (512, 4), (2048, 16))
ATT_BLOCK = 128
ATT_WIDTH = 512
N_ATT = 3 * 3 * ATT_WIDTH
N_RW = 3 * D + 64 + 64 + 160
N_RWP = 3456
N_LORA = N_RWP - 3 * D
N_GATE = 2 * D
D_FF = 2816
RMS_EPS = 1e-6
GN_EPS = 64e-5
SCAN_CHUNK = 64
SCAN_PAIRS = 8
NEG = -1e30
VMEM_LIMIT = 48 * 1024 * 1024

ADAM_LR, ADAM_B1, ADAM_B2, ADAM_EPS, ADAM_WD, ADAM_STEP = 0.001, 0.9, 0.999, 1e-08, 0.01, 10


def _pcall(body, **kw):
    return pl.pallas_call(body, **kw)


def _cparams(sem):
    return pltpu.CompilerParams(dimension_semantics=sem, vmem_limit_bytes=VMEM_LIMIT)


def _div(n, pref, mult):
    best = None
    d = mult
    while d <= min(n, pref):
        if n % d == 0:
            best = d
        d += mult
    return best if best else n


def _dg(a, b, ca, cb):
    return lax.dot_general(a.astype(BF16), b.astype(BF16), (((ca,), (cb,)), ((), ())), preferred_element_type=F32)


@jax.custom_vjp
def _nn(a, b):
    return _dg(a, b, 1, 0)


@jax.custom_vjp
def _nt(a, b):
    return _dg(a, b, 1, 1)


@jax.custom_vjp
def _tn(a, b):
    return _dg(a, b, 0, 0)


_nn.defvjp(lambda a, b: (_nn(a, b), (a, b)), lambda res, g: (_nt(g, res[1]), _tn(res[0], g)))
_nt.defvjp(lambda a, b: (_nt(a, b), (a, b)), lambda res, g: (_nn(g, res[1]), _tn(g, res[0])))
_tn.defvjp(lambda a, b: (_tn(a, b), (a, b)), lambda res, g: (_nt(res[1], g), _nn(res[0], g)))


def _bdg(a, b, ca, cb):
    return lax.dot_general(a.astype(BF16), b.astype(BF16), (((ca,), (cb,)), ((0,), (0,))), preferred_element_type=F32)


@jax.custom_vjp
def _bnn(a, b):
    return _bdg(a, b, 2, 1)


@jax.custom_vjp
def _bnt(a, b):
    return _bdg(a, b, 2, 2)


@jax.custom_vjp
def _btn(a, b):
    return _bdg(a, b, 1, 1)


_bnn.defvjp(lambda a, b: (_bnn(a, b), (a, b)), lambda res, g: (_bnt(g, res[1]), _btn(res[0], g)))
_bnt.defvjp(lambda a, b: (_bnt(a, b), (a, b)), lambda res, g: (_bnn(g, res[1]), _btn(g, res[0])))
_btn.defvjp(lambda a, b: (_btn(a, b), (a, b)), lambda res, g: (_bnt(res[1], g), _bnn(res[0], g)))


def _split2(x):
    hi = x.astype(BF16)
    lo = (x - hi.astype(F32)).astype(BF16)
    return hi, lo


def _hsum_impl(x, e, et):
    eb, etb = e.astype(BF16), et.astype(BF16)
    s = jnp.dot(x.astype(BF16), eb, preferred_element_type=F32)
    shi, slo = _split2(s)
    return jnp.dot(shi, etb, preferred_element_type=F32) + jnp.dot(slo, etb, preferred_element_type=F32)


@jax.custom_vjp
def _hsum(x, e, et):
    return _hsum_impl(x, e, et)


_hsum.defvjp(lambda x, e, et: (_hsum_impl(x, e, et), (e, et)),
             lambda res, g: (_hsum_impl(g, res[0], res[1]), jnp.zeros_like(res[0]), jnp.zeros_like(res[1])))


def _mm(a, b, *, ta=False, tb=False, out_dtype=F32, add=None, b_chip=False, out_chip=False, comm=None, name):
    riding = _NOTHING if comm is None else comm
    nc = riding.n
    if ta:
        kdim, m = a.shape
    else:
        m, kdim = a.shape
    if b_chip:
        n = b.shape[1] if tb else 4 * b.shape[2]
    else:
        n = b.shape[0] if tb else b.shape[1]
    tm, tn, tk = _div(m, 1536, 128), _div(n, 1536, 128), _div(kdim, 1408, 128)
    if b_chip and tb:
        tk = kdim // 4
    if (b_chip and not tb) or out_chip:
        tn = n // 4
    nk = kdim // tk
    ca, cb = (0 if ta else 1), (1 if tb else 0)

    nin = 2 if add is None else 3
    gi, gj = m // tm, n // tn

    def body(*refs):
        a_ref, b_ref = refs[0], refs[1]
        add_ref = None if add is None else refs[2]
        o_ref = refs[nin + nc]
        step = (pl.program_id(0) * gj + pl.program_id(1)) * nk + pl.program_id(2)
        before, after = _comm_phases(riding, refs[nin:nin + nc] + refs[nin + nc + 1:nin + 2 * nc + 1]
                                     + refs[nin + 2 * nc + 1 + (nk > 1):], gi * gj * nk, step)
        before()
        part = lax.dot_general(a_ref[...], b_ref[...], (((ca,), (cb,)), ((), ())), preferred_element_type=F32)

        def finish(r):
            if add_ref is not None:
                r = r + add_ref[...]
            o_ref[...] = r.astype(o_ref.dtype)

        if nk == 1:
            finish(part)
            after()
            return
        acc = refs[nin + 2 * nc + 1]
        k = pl.program_id(2)

        @pl.when(k == 0)
        def _():
            acc[...] = part

        @pl.when(k > 0)
        def _():
            acc[...] += part

        @pl.when(k == nk - 1)
        def _():
            finish(acc[...])

        after()

    a_spec = pl.BlockSpec((tk, tm), lambda i, j, k: (k, i)) if ta else pl.BlockSpec((tm, tk), lambda i, j, k: (i, k))
    if b_chip:
        b_spec = (pl.BlockSpec((None, tn, tk), lambda i, j, k: (k, j, 0)) if tb
                  else pl.BlockSpec((None, tk, tn), lambda i, j, k: (j, k, 0)))
    else:
        b_spec = pl.BlockSpec((tn, tk), lambda i, j, k: (j, k)) if tb else pl.BlockSpec((tk, tn), lambda i, j, k: (k, j))
    in_specs = [a_spec, b_spec]
    args = [a, b]
    if add is not None:
        in_specs.append(pl.BlockSpec((tm, tn), lambda i, j, k: (i, j)))
        args.append(add)
    if out_chip:
        out_spec = pl.BlockSpec((None, tm, tn), lambda i, j, k: (j, i, 0))
        out_shape = jax.ShapeDtypeStruct((4, m, tn), out_dtype)
    else:
        out_spec = pl.BlockSpec((tm, tn), lambda i, j, k: (i, j))
        out_shape = jax.ShapeDtypeStruct((m, n), out_dtype)
    res = _pcall(
        body, name=name, grid=(gi, gj, nk), in_specs=in_specs + [_HBM] * nc, out_specs=[out_spec] + [_HBM] * nc,
        out_shape=[out_shape] + riding.out_shape,
        scratch_shapes=([] if nk == 1 else [pltpu.VMEM((tm, tn), F32)]) + riding.sems,
        compiler_params=_cparams(("arbitrary",) * 3 if nc else ("parallel", "parallel", "arbitrary")),
    )(*args, *riding.ins)
    return res[0] if comm is None else (res[0], res[1:])


def _mm_sum(pairs, *, comm, name):
    m, n = pairs[0][0].shape[0], pairs[0][1].shape[1]
    tm, tn = _div(m, 1024, 128), _div(n, 1024, 128)
    tks = [_div(a.shape[1], 1408, 128) for a, _ in pairs]
    nks = [a.shape[1] // tk for (a, _), tk in zip(pairs, tks)]
    offs = [sum(nks[:p]) for p in range(len(pairs))]
    total, npair, nc = sum(nks), len(pairs), comm.n
    gi, gj = m // tm, n // tn

    def body(*refs):
        o_ref, acc = refs[2 * npair + nc], refs[2 * npair + 2 * nc + 1]
        k = pl.program_id(2)
        step = (pl.program_id(0) * gj + pl.program_id(1)) * total + k
        before, after = _comm_phases(comm, refs[2 * npair:2 * npair + nc]
                                     + refs[2 * npair + nc + 1:2 * npair + 2 * nc + 1]
                                     + refs[2 * npair + 2 * nc + 2:], gi * gj * total, step)
        before()
        for p in range(npair):
            def partial_product(p=p):
                part = jnp.dot(refs[2 * p][...], refs[2 * p + 1][...], preferred_element_type=F32)
                if p == 0:
                    @pl.when(k == 0)
                    def _():
                        acc[...] = part

                    @pl.when(k > 0)
                    def _():
                        acc[...] += part
                else:
                    acc[...] += part

            pl.when(jnp.logical_and(k >= offs[p], k < offs[p] + nks[p]))(partial_product)

        @pl.when(k == total - 1)
        def _():
            o_ref[...] = acc[...]

        after()

    def specs(tk, off, nk):
        def kb(k):
            return jnp.clip(k - off, 0, nk - 1)
        return [pl.BlockSpec((tm, tk), lambda i, j, k: (i, kb(k))), pl.BlockSpec((tk, tn), lambda i, j, k: (kb(k), j))]

    in_specs, args = [], []
    for (a, b), tk, off, nk in zip(pairs, tks, offs, nks):
        in_specs += specs(tk, off, nk)
        args += [a, b]
    res = _pcall(
        body, name=name, grid=(gi, gj, total), in_specs=in_specs + [_HBM] * nc,
        out_specs=[pl.BlockSpec((tm, tn), lambda i, j, k: (i, j))] + [_HBM] * nc,
        out_shape=[jax.ShapeDtypeStruct((m, n), F32)] + comm.out_shape,
        scratch_shapes=[pltpu.VMEM((tm, tn), F32)] + comm.sems,
        compiler_params=_cparams(("arbitrary",) * 3),
    )(*args, *comm.ins)
    return res[0], res[1:]


def _row_spec(br, w, cb):
    return pl.BlockSpec((br, w), lambda i: (i, cb))


def _const_spec(shape):
    return pl.BlockSpec(shape, lambda i: (0,) * len(shape))


def _rows_fwd(fn, rows, consts, outs, *, name, br, acc_shape=None, halo=None):
    s = rows[0][0].shape[0]
    nr, nc = len(rows), len(consts)
    kept = [k for k, o in enumerate(outs) if o is not None]

    def body(*refs):
        xs = [r[...].astype(F32) for r in refs[:nr]]
        cs = [c[...] for c in refs[nr:nr + nc]]
        if halo is not None:
            cs.append(jnp.where(pl.program_id(0) == 0, 0.0, refs[nr + nc][...].astype(F32)))
        res = fn(*xs, *cs)
        orefs = refs[nr + nc + (halo is not None):]
        for j, k in enumerate(kept):
            orefs[j][...] = res[k].astype(orefs[j].dtype)
        if acc_shape is not None:
            acc_ref = orefs[len(kept)]

            @pl.when(pl.program_id(0) == 0)
            def _():
                acc_ref[...] = jnp.zeros_like(acc_ref)

            acc_ref[...] += res[len(outs)]

    in_specs = [_row_spec(br, w, cb) for (_, w, cb) in rows] + [_const_spec(c.shape) for c in consts]
    args = [r[0] for r in rows] + list(consts)
    if halo is not None:
        harr, hw, hcb = rows[halo]
        in_specs.append(pl.BlockSpec((8, hw), lambda i: (jnp.maximum(i * (br // 8) - 1, 0), hcb)))
        args.append(harr)
    out_specs = [_row_spec(br, outs[k][0], 0) for k in kept]
    out_shape = [jax.ShapeDtypeStruct((s, outs[k][0]), outs[k][1]) for k in kept]
    if acc_shape is not None:
        out_specs.append(_const_spec(acc_shape))
        out_shape.append(jax.ShapeDtypeStruct(acc_shape, F32))
    return _pcall(
        body, name=name, grid=(pl.cdiv(s, br),), in_specs=in_specs, out_specs=out_specs, out_shape=out_shape,
        compiler_params=_cparams(("arbitrary",)),
    )(*args)


def _rows_bwd(fn, rows, consts, cots, *, wrt_rows, wrt_consts, drow_dtypes, name, br, unit_cot=False, comm=None):
    comm = _NOTHING if comm is None else comm
    ncomm = comm.n
    nout = len(wrt_rows) + len(wrt_consts)
    s = rows[0][0].shape[0]
    nr, nc = len(rows), len(consts)
    flat_cots = [c for lst in cots for c in lst]
    ncot = len(flat_cots)

    def body(*refs):
        xs = [r[...].astype(F32) for r in refs[:nr]]
        cs = [c[...] for c in refs[nr:nr + nc]]
        cvals = [c[...].astype(F32) for c in refs[nr + nc:nr + nc + ncot]]
        orefs = refs[nr + nc + ncot + ncomm:]
        before, after = _comm_phases(comm, refs[nr + nc + ncot:nr + nc + ncot + ncomm] + orefs[nout:], s // br)
        before()

        def g(*d):
            xs2, cs2 = list(xs), list(cs)
            for j, k in enumerate(wrt_rows):
                xs2[k] = d[j]
            for j, k in enumerate(wrt_consts):
                cs2[k] = d[len(wrt_rows) + j]
            return tuple(fn(*xs2, *cs2))

        prim = [xs[k] for k in wrt_rows] + [cs[k] for k in wrt_consts]
        outs, vjp = jax.vjp(g, *prim)
        ct = []
        pos = 0
        for o, lst in zip(outs, cots):
            if unit_cot:
                ct.append(jnp.ones_like(o))
                continue
            acc = jnp.zeros_like(o)
            for _ in lst:
                acc = acc + cvals[pos]
                pos += 1
            ct.append(acc)
        grads = vjp(tuple(ct))
        for j in range(len(wrt_rows)):
            orefs[j][...] = grads[j].astype(orefs[j].dtype)

        @pl.when(pl.program_id(0) == 0)
        def _():
            for j in range(len(wrt_consts)):
                oref = orefs[len(wrt_rows) + j]
                oref[...] = jnp.zeros_like(oref)

        for j in range(len(wrt_consts)):
            orefs[len(wrt_rows) + j][...] += grads[len(wrt_rows) + j]
        after()

    in_specs = ([_row_spec(br, w, cb) for (_, w, cb) in rows] + [_const_spec(c.shape) for c in consts]
                + [_row_spec(br, w, cb) for (_, w, cb) in flat_cots] + [_HBM] * ncomm)
    out_specs = ([_row_spec(br, rows[k][1], 0) for k in wrt_rows] + [_const_spec(consts[k].shape) for k in wrt_consts]
                 + [_HBM] * ncomm)
    out_shape = ([jax.ShapeDtypeStruct((s, rows[k][1]), dt) for k, dt in zip(wrt_rows, drow_dtypes)]
                 + [jax.ShapeDtypeStruct(consts[k].shape, F32) for k in wrt_consts] + comm.out_shape)
    return _pcall(
        body, name=name, grid=(s // br,), in_specs=in_specs, out_specs=out_specs, out_shape=out_shape,
        scratch_shapes=comm.sems, compiler_params=_cparams(("arbitrary",)),
    )(*[r[0] for r in rows], *consts, *[c[0] for c in flat_cots], *comm.ins)


def _rms(x, w):
    return x * lax.rsqrt(jnp.mean(x * x, axis=-1, keepdims=True) + RMS_EPS) * w


def _softplus(x):
    return jnp.maximum(x, 0.0) + jnp.log(1.0 + jnp.exp(-jnp.abs(x)))


def _f_pre(x, nw, sc, sh):
    return _rms(x, nw) * (1.0 + sc) + sh, x


def _f_pre2(x, o, gt, nw, sc, sh):
    x1 = x + gt * o
    return x1, _rms(x1, nw) * (1.0 + sc) + sh


def _f_fin(x1, f, tgt, gt, nfw):
    y = _rms(x1 + gt * f, nfw)
    return (0.5 * jnp.mean(jnp.square(y - tgt), axis=-1, keepdims=True),)


def _f_comb(o1, o2, o3, l1, l2, l3):
    m = lax.stop_gradient(jnp.maximum(jnp.maximum(l1, l2), l3))
    e1, e2, e3 = jnp.exp(l1 - m), jnp.exp(l2 - m), jnp.exp(l3 - m)
    return ((e1 * o1 + e2 * o2 + e3 * o3) / (e1 + e2 + e3),)


def _f_rwpre(zs, w0, a0, k_k, k_a, wl, e, et):
    r, k, v, zl = zs[:, 0:D], zs[:, D:2 * D], zs[:, 2 * D:3 * D], zs[:, 3 * D:N_RWP]
    lane = lax.broadcasted_iota(jnp.int32, zl.shape, 1)
    t = jnp.where(lane < 64, jnp.tanh(zl), jnp.where(lane < 128, zl, jnp.where(lane < 288, jax.nn.sigmoid(zl), 0.0)))
    lo = _nn(t[:, 0:128], wl[0:128, 0:2 * D])
    g = _nn(t[:, 128:N_LORA], wl[128:N_LORA, 2 * D:3 * D])
    w_log = -_softplus(-(w0 + lo[:, 0:D])) - 0.5
    lw = -jnp.exp(w_log)
    a = jax.nn.sigmoid(a0 + lo[:, D:2 * D])
    k_mod = k * (1.0 + (a - 1.0) * k_a)
    kk = k * k_k
    kk = kk / jnp.maximum(jnp.sqrt(_hsum(kk * kk, e, et)), 1e-12)
    return r, lw, k_mod, v, -kk, kk * a, g


def _f_rwpost(y, r, v, k_mod, g, lnx_w, lnx_b, r_k, e, et):
    mean = _hsum(y, e, et) * (1.0 / 64)
    yc = y - mean
    var = _hsum(yc * yc, e, et) * (1.0 / 64)
    yn = yc * lax.rsqrt(var + GN_EPS) * lnx_w + lnx_b
    bonus = _hsum(r * k_mod * r_k, e, et) * v
    return ((yn + bonus) * g,)


def _f_mix(gi, ya, yr, bg):
    gate = jax.nn.sigmoid(gi + bg)
    return (gate[:, 0:D] * ya + gate[:, D:2 * D] * yr,)


def _f_adamw(w, g, m, v):
    m = ADAM_B1 * m + (1.0 - ADAM_B1) * g
    v = ADAM_B2 * v + (1.0 - ADAM_B2) * jnp.square(g)
    m_hat = m / (1.0 - ADAM_B1 ** ADAM_STEP)
    v_hat = v / (1.0 - ADAM_B2 ** ADAM_STEP)
    return -ADAM_LR * (m_hat / (jnp.sqrt(v_hat) + ADAM_EPS) + ADAM_WD * w), m, v


def _down(x, k):
    row = lax.broadcasted_iota(jnp.int32, x.shape, 0)
    return jnp.where(row < k, 0.0, pltpu.roll(x, k, 0))


def _up(x, k):
    n = x.shape[0]
    row = lax.broadcasted_iota(jnp.int32, x.shape, 0)
    return jnp.where(row >= n - k, 0.0, pltpu.roll(x, n - k, 0))


def _col_spec(s, w, off=0):
    return pl.BlockSpec((s, w), lambda j: (0, j + off))


def _shift_bwd(z, mu, dzs):
    s, n = z.shape

    def body(z_ref, mu_ref, d_ref, dz_ref, dmu_ref):
        zz, d, m = z_ref[...], d_ref[...], mu_ref[...]
        dm = d * m
        dz_ref[...] = (d - dm + _up(dm, 1)).astype(dz_ref.dtype)
        dmu_ref[...] = jnp.sum(d * (_down(zz, 1) - zz), axis=0, keepdims=True)

    return _pcall(
        body, name="shift_bwd", grid=(n // 128,), in_specs=[_col_spec(s, 128), _col_spec(1, 128), _col_spec(s, 128)],
        out_specs=[_col_spec(s, 128), _col_spec(1, 128)],
        out_shape=[jax.ShapeDtypeStruct((s, n), BF16), jax.ShapeDtypeStruct((1, n), F32)],
        compiler_params=_cparams(("parallel",)),
    )(z, mu, dzs)


def _conv3(x, w_ref, b_ref):
    return b_ref[...] + w_ref[0:1, :] * _down(x, 2) + w_ref[1:2, :] * _down(x, 1) + w_ref[2:3, :] * x


def _conv_fwd(u, cw, cb):
    s = u.shape[0]
    nb = D_FF // 128

    def body(ug_ref, uv_ref, wg_ref, wv_ref, bg_ref, bv_ref, o_ref):
        gate = _conv3(ug_ref[...], wg_ref, bg_ref)
        val = _conv3(uv_ref[...], wv_ref, bv_ref)
        o_ref[...] = (gate * jax.nn.sigmoid(gate) * val).astype(o_ref.dtype)

    return _pcall(
        body, name="conv_fwd", grid=(nb,),
        in_specs=[_col_spec(s, 128), _col_spec(s, 128, nb), _col_spec(3, 128), _col_spec(3, 128, nb),
                  _col_spec(1, 128), _col_spec(1, 128, nb)],
        out_specs=_col_spec(s, 128), out_shape=jax.ShapeDtypeStruct((s, D_FF), BF16),
        compiler_params=_cparams(("parallel",)),
    )(u, u, cw, cw, cb, cb)


def _conv_bwd(u, cw, cb, dact):
    s = u.shape[0]
    nb = D_FF // 128

    def half(x, d, w_ref, du_ref, dw_ref, db_ref):
        x1, x2 = _down(x, 1), _down(x, 2)
        du_ref[...] = (w_ref[2:3, :] * d + w_ref[1:2, :] * _up(d, 1) + w_ref[0:1, :] * _up(d, 2)).astype(du_ref.dtype)
        dw_ref[0:1, :] = jnp.sum(d * x2, axis=0, keepdims=True)
        dw_ref[1:2, :] = jnp.sum(d * x1, axis=0, keepdims=True)
        dw_ref[2:3, :] = jnp.sum(d * x, axis=0, keepdims=True)
        db_ref[...] = jnp.sum(d, axis=0, keepdims=True)

    def body(ug_ref, uv_ref, wg_ref, wv_ref, bg_ref, bv_ref, da_ref,
             dug_ref, duv_ref, dwg_ref, dwv_ref, dbg_ref, dbv_ref):
        ug, uv, da = ug_ref[...], uv_ref[...], da_ref[...]
        gate = _conv3(ug, wg_ref, bg_ref)
        val = _conv3(uv, wv_ref, bv_ref)
        sg = jax.nn.sigmoid(gate)
        dgate = da * val * sg * (1.0 + gate * (1.0 - sg))
        dval = da * gate * sg
        half(ug, dgate, wg_ref, dug_ref, dwg_ref, dbg_ref)
        half(uv, dval, wv_ref, duv_ref, dwv_ref, dbv_ref)

    dug, duv, dwg, dwv, dbg, dbv = _pcall(
        body, name="conv_bwd", grid=(nb,),
        in_specs=[_col_spec(s, 128), _col_spec(s, 128, nb), _col_spec(3, 128), _col_spec(3, 128, nb),
                  _col_spec(1, 128), _col_spec(1, 128, nb), _col_spec(s, 128)],
        out_specs=[_col_spec(s, 128), _col_spec(s, 128), _col_spec(3, 128), _col_spec(3, 128),
                   _col_spec(1, 128), _col_spec(1, 128)],
        out_shape=[jax.ShapeDtypeStruct((s, D_FF), BF16), jax.ShapeDtypeStruct((s, D_FF), BF16),
                   jax.ShapeDtypeStruct((3, D_FF), F32), jax.ShapeDtypeStruct((3, D_FF), F32),
                   jax.ShapeDtypeStruct((1, D_FF), F32), jax.ShapeDtypeStruct((1, D_FF), F32)],
        compiler_params=_cparams(("parallel",)),
    )(u, u, cw, cw, cb, cb, dact)
    return (jnp.concatenate([dug, duv], axis=1), jnp.concatenate([dwg, dwv], axis=1),
            jnp.concatenate([dbg, dbv], axis=1))


ATT_BATCH = 4


def _att_batch(q, kp, kc, vp, vc, first):
    ma = lax.broadcasted_iota(jnp.int32, (1, ATT_BLOCK, 128), 2) < 64

    def diag(x):
        return jnp.concatenate([jnp.where(ma, x, 0.0), jnp.where(ma, 0.0, x)], axis=1)

    qi = lax.broadcasted_iota(jnp.int32, (1, ATT_BLOCK, 2 * ATT_BLOCK), 1)
    kj = lax.broadcasted_iota(jnp.int32, (1, ATT_BLOCK, 2 * ATT_BLOCK), 2) & (ATT_BLOCK - 1)
    okp = kj >= qi + jnp.where(first, 2 * ATT_BLOCK, 0)
    okc = kj <= qi
    sp = jnp.where(okp, _bnt(q, diag(kp)) * 0.125, NEG)
    sc = jnp.where(okc, _bnt(q, diag(kc)) * 0.125, NEG)

    def per_head(fn, x):
        return fn(x[..., :ATT_BLOCK]), fn(x[..., ATT_BLOCK:])

    def spread(ab):
        return jnp.concatenate([jnp.broadcast_to(t, t.shape[:2] + (ATT_BLOCK,)) for t in ab], axis=-1)

    row_max = functools.partial(jnp.max, axis=-1, keepdims=True)
    row_sum = functools.partial(jnp.sum, axis=-1, keepdims=True)
    m = [lax.stop_gradient(jnp.maximum(a, b)) for a, b in zip(per_head(row_max, sp), per_head(row_max, sc))]
    pp, pc = jnp.exp(sp - spread(m)), jnp.exp(sc - spread(m))
    den = [a + b for a, b in zip(per_head(row_sum, pp), per_head(row_sum, pc))]
    num = _bnn(pp, diag(vp)) + _bnn(pc, diag(vc))
    out = num / jnp.where(ma, den[0], den[1])
    lse = jnp.where(ma, m[0] + jnp.log(den[0]), m[1] + jnp.log(den[1]))
    return out, jnp.broadcast_to(lse, out.shape)


def _att_pairs_per_step(dil):
    return 4 if dil == 1 else 1


def _att_residues(dil):
    return min(dil, ATT_BATCH // _att_pairs_per_step(dil))


def _att_specs(g, dil):
    rows, pp = ATT_BLOCK * dil, _att_pairs_per_step(dil)

    def cur(slot):
        return pl.BlockSpec((rows, 128 * pp), lambda n, p: (n, (g * 3 + slot) * (4 // pp) + p))

    def prev(slot):
        return pl.BlockSpec((rows, 128 * pp), lambda n, p: (jnp.maximum(n - 1, 0), (g * 3 + slot) * (4 // pp) + p))

    return [cur(0), prev(1), cur(1), prev(2), cur(2)]


def _att_out_spec(dil):
    return pl.BlockSpec((ATT_BLOCK * dil, 128 * _att_pairs_per_step(dil)), lambda n, p: (n, p))


def _att_grid(s, dil):
    return (s // (ATT_BLOCK * dil), 4 // _att_pairs_per_step(dil))


def _att_windows(i, dil):
    res = _att_residues(dil)

    def rows(r):
        return pl.ds(i * res + r, ATT_BLOCK, stride=dil) if dil > 1 else pl.ds(0, ATT_BLOCK)

    return [(rows(r), pl.ds(128 * j, 128)) for j in range(_att_pairs_per_step(dil)) for r in range(res)]


def _att_fwd(att_in, g, dil):
    s = att_in.shape[0]

    def body(q_ref, kp_ref, kc_ref, vp_ref, vc_ref, o_ref, l_ref):
        first = pl.program_id(0) == 0

        def one(i, carry):
            win = _att_windows(i, dil)
            vals = [jnp.stack([ref[w] for w in win]) for ref in (q_ref, kp_ref, kc_ref, vp_ref, vc_ref)]
            o, l = _att_batch(*vals, first)
            for j, w in enumerate(win):
                o_ref[w] = o[j]
                l_ref[w] = l[j]
            return carry

        lax.fori_loop(0, dil // _att_residues(dil), one, 0)

    return _pcall(
        body, name=f"att_fwd{g}", grid=_att_grid(s, dil), in_specs=_att_specs(g, dil),
        out_specs=[_att_out_spec(dil)] * 2, out_shape=[jax.ShapeDtypeStruct((s, ATT_WIDTH), F32)] * 2,
        compiler_params=_cparams(("parallel", "parallel")),
    )(att_in, att_in, att_in, att_in, att_in)


def _att_bwd(att_in, g, dil, do, dl, acc):
    s = att_in.shape[0]

    def body(q_ref, kp_ref, kc_ref, vp_ref, vc_ref, do_ref, dl_ref, dq_ref, dkp_ref, dkc_ref, dvp_ref, dvc_ref):
        first = pl.program_id(0) == 0

        def one(i, carry):
            win = _att_windows(i, dil)
            vals = [jnp.stack([ref[w] for w in win]) for ref in (q_ref, kp_ref, kc_ref, vp_ref, vc_ref)]
            _, vjp = jax.vjp(lambda *a: _att_batch(*a, first), *vals)
            grads = vjp((jnp.stack([do_ref[w] for w in win]), jnp.stack([dl_ref[w] for w in win])))
            for ref, gr in zip((dq_ref, dkp_ref, dkc_ref, dvp_ref, dvc_ref), grads):
                for j, w in enumerate(win):
                    ref[w] = gr[j]
            return carry

        lax.fori_loop(0, dil // _att_residues(dil), one, 0)

    dq, dkp, dkc, dvp, dvc = _pcall(
        body, name=f"att_bwd{g}", grid=_att_grid(s, dil), in_specs=_att_specs(g, dil) + [_att_out_spec(dil)] * 2,
        out_specs=[_att_out_spec(dil)] * 5, out_shape=[jax.ShapeDtypeStruct((s, ATT_WIDTH), F32)] * 5,
        compiler_params=_cparams(("parallel", "parallel")),
    )(att_in, att_in, att_in, att_in, att_in, do, dl)

    unit, rb = ATT_BLOCK * dil, 1024
    steps = s // rb
    within = unit < rb

    def shifted(cur_ref, next_ref, has_next):
        nxt = jnp.where(has_next, next_ref[...], 0.0)
        return jnp.concatenate([cur_ref[unit:, :], nxt], axis=0) if within else nxt

    def cbody(dq_ref, dkc_ref, dkp_ref, dkn_ref, dvc_ref, dvp_ref, dvn_ref, *rest):
        o_ref = rest[-1]
        has_next = pl.program_id(0) + (1 if within else unit // rb) < steps
        o_ref[:, 0:ATT_WIDTH] = dq_ref[...].astype(BF16)
        o_ref[:, ATT_WIDTH:2 * ATT_WIDTH] = (dkc_ref[...] + shifted(dkp_ref, dkn_ref, has_next)).astype(BF16)
        o_ref[:, 2 * ATT_WIDTH:3 * ATT_WIDTH] = (dvc_ref[...] + shifted(dvp_ref, dvn_ref, has_next)).astype(BF16)

    cur = pl.BlockSpec((rb, ATT_WIDTH), lambda i: (i, 0))
    if within:
        nxt = pl.BlockSpec((unit, ATT_WIDTH), lambda i: (jnp.minimum((i + 1) * (rb // unit), s // unit - 1), 0))
    else:
        nxt = pl.BlockSpec((rb, ATT_WIDTH), lambda i: (jnp.minimum(i + unit // rb, steps - 1), 0))
    carried = [] if acc is None else [acc]
    return _pcall(
        cbody, name=f"att_bwd_sum{g}", grid=(steps,),
        in_specs=[cur, cur, cur, nxt, cur, cur, nxt] + [pl.BlockSpec(memory_space=pl.ANY)] * len(carried),
        out_specs=pl.BlockSpec((rb, 3 * ATT_WIDTH), lambda i: (i, g)),
        out_shape=jax.ShapeDtypeStruct((s, N_ATT), BF16), input_output_aliases={7: 0} if carried else {},
        compiler_params=_cparams(("parallel",)),
    )(dq, dkc, dkp, dkp, dvc, dvp, dvp, *carried)


def _unit_lower_inverse_impl(n):
    eye = (lax.broadcasted_iota(jnp.int32, (1,) + n.shape[1:], 1)
           == lax.broadcasted_iota(jnp.int32, (1,) + n.shape[1:], 2))
    t = jnp.where(eye, 1.0, 0.0) + n
    pw = n
    for _ in range(5):
        pw = _bnn(pw, pw)
        t = t + _bnn(t, pw)
    return t


@jax.custom_vjp
def _unit_lower_inverse(n):
    return _unit_lower_inverse_impl(n)


def _unit_lower_inverse_fwd(n):
    t = _unit_lower_inverse_impl(n)
    return t, t


_unit_lower_inverse.defvjp(_unit_lower_inverse_fwd, lambda t, g: (_bnt(_btn(t, g), t),))


@jax.custom_vjp
def _known_inverse(n, t):
    return t


_known_inverse.defvjp(lambda n, t: (t, t), lambda t, g: (_bnt(_btn(t, g), t), jnp.zeros_like(t)))


def _scan_chunk(r, lw, k, v, a, b, s0, inverse):
    c = SCAN_CHUNK
    p = s0.shape[0]
    ri = lax.broadcasted_iota(jnp.int32, (c, c), 0)
    ci = lax.broadcasted_iota(jnp.int32, (c, c), 1)
    cum = jnp.dot((ci <= ri).astype(F32), lw, precision=HI, preferred_element_type=F32)
    tot = jnp.sum(lw, axis=0, keepdims=True)
    ma = (lax.broadcasted_iota(jnp.int32, (c, 128 * p), 1) & 127) < 64

    def pairs(x):
        return jnp.concatenate([x[None, :, 128 * j:128 * (j + 1)] for j in range(p)], axis=0)

    def stack(x):
        return jnp.concatenate([pairs(jnp.where(ma, x, 0.0)), pairs(jnp.where(ma, 0.0, x))], axis=1)

    einv, eend = jnp.exp(-cum), jnp.exp(tot - cum)
    ra, aa = stack(r * jnp.exp(cum)), stack(a * jnp.exp(cum - lw))
    bi, ki, be, ke, vs = stack(b * einv), stack(k * einv), stack(b * eend), stack(k * eend), stack(v)
    r2 = lax.broadcasted_iota(jnp.int32, (1, 2 * c, 2 * c), 1)
    c2 = lax.broadcasted_iota(jnp.int32, (1, 2 * c, 2 * c), 2)
    same = (r2 >= c) == (c2 >= c)
    strict = jnp.logical_and(same, c2 < r2)
    incl = jnp.logical_and(same, c2 <= r2)
    s0 = jnp.where(same, s0, 0.0)
    prod = _bnt(jnp.concatenate([aa, ra], axis=1), jnp.concatenate([bi, ki], axis=1))
    a_ab = jnp.where(strict, prod[:, :2 * c, :2 * c], 0.0)
    a_ak = jnp.where(strict, prod[:, :2 * c, 2 * c:], 0.0)
    a_rb = jnp.where(incl, prod[:, 2 * c:, :2 * c], 0.0)
    a_rk = jnp.where(incl, prod[:, 2 * c:, 2 * c:], 0.0)
    t = inverse(a_ab)
    u = _bnn(t, _bnt(aa, s0) + _bnn(a_ak, vs))
    uv = jnp.concatenate([u, vs], axis=1)
    ys = _bnt(ra, s0) + _bnn(jnp.concatenate([a_rb, a_rk], axis=2), uv)
    s1 = s0 * pairs(jnp.exp(tot)) + _btn(uv, jnp.concatenate([be, ke], axis=1))
    y3 = ys[:, :c] + ys[:, c:]
    return (jnp.concatenate([y3[j] for j in range(p)], axis=1), s1), t


def _scan_specs(rev, n):
    def at(i):
        return n - 1 - i if rev else i

    def cm(cb):
        return pl.BlockSpec((SCAN_CHUNK, D), lambda i: (at(i), cb))

    return cm, pl.BlockSpec((1, SCAN_PAIRS, 128, 128), lambda i: (at(i), 0, 0, 0))


def _comm_phases(comm, refs, n, step=None):
    k = comm.n
    srcs, outs, sems = refs[:k], refs[k:2 * k], refs[2 * k:]
    i = pl.program_id(0) if step is None else step

    def before():
        @pl.when(i == 0)
        def _():
            comm.first(srcs, outs, sems)

    def after():
        if comm.mid is not None:
            @pl.when(i == (3 * n) // 4)
            def _():
                comm.mid(srcs, outs, sems)

        @pl.when(i == n - 1)
        def _():
            comm.last(srcs, outs, sems)

    return before, after


def _scan_fwd(zs, lw, km, aa, bb, comm):
    s = zs.shape[0]
    n = s // SCAN_CHUNK
    cm, st = _scan_specs(False, n)
    k = comm.n

    def body(*refs):
        r_ref, lw_ref, k_ref, v_ref, a_ref, b_ref = refs[:6]
        y_ref, s0_ref, t_ref = refs[6 + k:9 + k]
        state = refs[9 + 2 * k]
        before, after = _comm_phases(comm, refs[6:6 + k] + refs[9 + k:9 + 2 * k] + refs[10 + 2 * k:], n)
        before()

        @pl.when(pl.program_id(0) == 0)
        def _():
            state[...] = jnp.zeros_like(state)

        s0 = state[...]
        s0_ref[0] = s0
        (y, s1), t = _scan_chunk(*[ref[...] for ref in (r_ref, lw_ref, k_ref, v_ref, a_ref, b_ref)], s0,
                                 _unit_lower_inverse)
        y_ref[...] = y
        t_ref[0] = t.astype(BF16)
        state[...] = s1
        after()

    per_chunk = (n, SCAN_PAIRS, 128, 128)
    res = _pcall(
        body, name="scan_fwd", grid=(n,), in_specs=[cm(0), cm(0), cm(0), cm(2), cm(0), cm(0)] + [_HBM] * k,
        out_specs=[cm(0), st, st] + [_HBM] * k,
        out_shape=[jax.ShapeDtypeStruct((s, D), F32), jax.ShapeDtypeStruct(per_chunk, F32),
                   jax.ShapeDtypeStruct(per_chunk, BF16)] + comm.out_shape,
        scratch_shapes=[pltpu.VMEM((SCAN_PAIRS, 128, 128), F32)] + comm.sems,
        compiler_params=_cparams(("arbitrary",)),
    )(zs, lw, km, zs, aa, bb, *comm.ins)
    return res[0], res[1], res[2], res[3:]


def _scan_bwd(zs, lw, km, aa, bb, s0s, ts, dy, comm):
    s = zs.shape[0]
    n = s // SCAN_CHUNK
    cm, st = _scan_specs(True, n)
    k = comm.n

    def body(*refs):
        r_ref, lw_ref, k_ref, v_ref, a_ref, b_ref, s0_ref, t_ref, dy_ref = refs[:9]
        douts = refs[9 + k:15 + k]
        dstate = refs[15 + 2 * k]
        before, after = _comm_phases(comm, refs[9:9 + k] + refs[15 + k:15 + 2 * k] + refs[16 + 2 * k:], n)
        before()

        @pl.when(pl.program_id(0) == 0)
        def _():
            dstate[...] = jnp.zeros_like(dstate)

        t = t_ref[0].astype(F32)
        prim = [ref[...] for ref in (r_ref, lw_ref, k_ref, v_ref, a_ref, b_ref)] + [s0_ref[0]]
        _, vjp, _ = jax.vjp(lambda *p: _scan_chunk(*p, lambda nil: _known_inverse(nil, t)), *prim, has_aux=True)
        grads = vjp((dy_ref[...], dstate[...]))
        for ref, gr in zip(douts, grads[:6]):
            ref[...] = gr
        dstate[...] = grads[6]
        after()

    res = _pcall(
        body, name="scan_bwd", grid=(n,),
        in_specs=[cm(0), cm(0), cm(0), cm(2), cm(0), cm(0), st, st, cm(0)] + [_HBM] * k,
        out_specs=[cm(0)] * 6 + [_HBM] * k, out_shape=[jax.ShapeDtypeStruct((s, D), F32)] * 6 + comm.out_shape,
        scratch_shapes=[pltpu.VMEM((SCAN_PAIRS, 128, 128), F32)] + comm.sems,
        compiler_params=_cparams(("arbitrary",)),
    )(zs, lw, km, zs, aa, bb, s0s, ts, dy, *comm.ins)
    return res[:6], res[6:]


_HBM = pl.BlockSpec(memory_space=pltpu.HBM)


def _me():
    return lax.axis_index("x"), lax.axis_index("y"), lax.axis_index("c")


def _allgather8(src, name):
    def body(src_ref, out_ref, ssem, rsem, lsem):
        x, y, c = _me()
        me = 4 * x + 2 * y + c
        local = pltpu.make_async_copy(src_ref, out_ref.at[me], lsem)
        local.start()
        peers = []
        for k in range(1, 8):
            peers.append(((1 - x) if k & 4 else x, (1 - y) if k & 2 else y, (1 - c) if k & 1 else c))
        sends = []
        for k, peer in enumerate(peers):
            cp = pltpu.make_async_remote_copy(src_ref, out_ref.at[me], ssem.at[k], rsem.at[k], device_id=peer,
                                              device_id_type=MESH)
            cp.start()
            sends.append(cp)
        for k, (px, py, pc) in enumerate(peers):
            pltpu.make_async_remote_copy(src_ref, out_ref.at[4 * px + 2 * py + pc], ssem.at[k], rsem.at[k],
                                         device_id=(px, py, pc), device_id_type=MESH).wait_recv()
        for cp in sends:
            cp.wait_send()
        local.wait()

    return _pcall(
        body, name=name, in_specs=[_HBM], out_specs=_HBM, out_shape=jax.ShapeDtypeStruct((8,) + src.shape, src.dtype),
        scratch_shapes=[pltpu.SemaphoreType.DMA((7,)), pltpu.SemaphoreType.DMA((7,)), pltpu.SemaphoreType.DMA],
    )(src)


def _other_chips(x, y):
    return [(1 - x, y), (x, 1 - y), (1 - x, 1 - y)]


def _remote(src, dst, ssem, rsem, to):
    return pltpu.make_async_remote_copy(src, dst, ssem, rsem, device_id=to, device_id_type=MESH)


class _GatherWeights:
    def __init__(self, shards):
        self.ins = list(shards)
        n = self.n = len(shards)
        self.out_shape = [jax.ShapeDtypeStruct((4,) + t.shape, t.dtype) for t in shards]
        self.sems = [pltpu.SemaphoreType.DMA((6 * n,)), pltpu.SemaphoreType.DMA((6 * n,)),
                     pltpu.SemaphoreType.DMA((n,)), pltpu.SemaphoreType.DMA((n,))]

    def _copies(self, srcs, outs, sems):
        ssem, rsem, lsem, osem = sems
        x, y, c = _me()
        me = 2 * x + y
        own, ici, landed, passed, passed_in = [], [], [], [], []
        for a in range(self.n):
            h = self.ins[a].shape[0] // 2
            mine, other = pl.ds(c * h, h), pl.ds((1 - c) * h, h)
            own.append(_remote(srcs[a], outs[a].at[me], lsem.at[a], osem.at[a], (x, y, 1 - c)))
            for k, (px, py) in enumerate(_other_chips(x, y)):
                s1, r1, s2, r2 = ssem.at[6 * a + k], rsem.at[6 * a + k], ssem.at[6 * a + 3 + k], rsem.at[6 * a + 3 + k]
                got, got_sib = outs[a].at[2 * px + py, mine], outs[a].at[2 * px + py, other]
                ici.append(_remote(srcs[a].at[mine], outs[a].at[me, mine], s1, r1, (px, py, c)))
                landed.append(_remote(got, got, s1, r1, (px, py, c)))
                passed.append(_remote(got, got, s2, r2, (x, y, 1 - c)))
                passed_in.append(_remote(got_sib, got_sib, s2, r2, (x, y, 1 - c)))
        return own, ici, landed, passed, passed_in

    def first(self, srcs, outs, sems):
        own, ici, _, _, _ = self._copies(srcs, outs, sems)
        for cp in own + ici:
            cp.start()

    def mid(self, srcs, outs, sems):
        _, _, landed, passed, _ = self._copies(srcs, outs, sems)
        for arrived, onward in zip(landed, passed):
            arrived.wait_recv()
            onward.start()

    def last(self, srcs, outs, sems):
        own, ici, _, passed, passed_in = self._copies(srcs, outs, sems)
        for cp in passed_in:
            cp.wait_recv()
        for cp in ici + passed:
            cp.wait_send()
        for cp in own:
            cp.wait()


class _ScatterToChips:
    def __init__(self, parts):
        self.ins = list(parts)
        n = self.n = len(parts)
        self.out_shape = [jax.ShapeDtypeStruct(t.shape, t.dtype) for t in parts]
        self.sems = [pltpu.SemaphoreType.DMA((3 * n,)), pltpu.SemaphoreType.DMA((3 * n,)), pltpu.SemaphoreType.DMA((n,))]

    def _copies(self, srcs, outs, sems):
        ssem, rsem, lsem = sems
        x, y, c = _me()
        me = 2 * x + y
        own, out, landed = [], [], []
        for a in range(self.n):
            own.append(pltpu.make_async_copy(srcs[a].at[me], outs[a].at[me], lsem.at[a]))
            for k, (px, py) in enumerate(_other_chips(x, y)):
                dst = outs[a].at[2 * px + py]
                out.append(_remote(srcs[a].at[2 * px + py], outs[a].at[me], ssem.at[3 * a + k], rsem.at[3 * a + k],
                                   (px, py, c)))
                landed.append(_remote(dst, dst, ssem.at[3 * a + k], rsem.at[3 * a + k], (px, py, c)))
        return own, out, landed

    def first(self, srcs, outs, sems):
        own, out, _ = self._copies(srcs, outs, sems)
        for cp in own + out:
            cp.start()

    mid = None

    def last(self, srcs, outs, sems):
        own, out, landed = self._copies(srcs, outs, sems)
        for cp in landed:
            cp.wait_recv()
        for cp in own:
            cp.wait()
        for cp in out:
            cp.wait_send()


def _run_comm(comm, name):
    n = comm.n

    def body(*refs):
        srcs, outs, sems = refs[:n], refs[n:2 * n], refs[2 * n:]
        comm.first(srcs, outs, sems)
        if comm.mid is not None:
            comm.mid(srcs, outs, sems)
        comm.last(srcs, outs, sems)

    return _pcall(body, name=name, in_specs=[_HBM] * n, out_specs=[_HBM] * n, out_shape=comm.out_shape,
                  scratch_shapes=comm.sems)(*comm.ins)


class _NoComm:
    n, ins, out_shape, sems, mid = 0, [], [], [], None

    def first(self, srcs, outs, sems):
        pass

    def last(self, srcs, outs, sems):
        pass


_NOTHING = _NoComm()


class _SiblingHalves:
    mid = None

    def __init__(self, grads):
        self.ins = list(grads)
        n = self.n = len(grads)
        self.out_shape = [jax.ShapeDtypeStruct((4, t.shape[1] // 2, t.shape[2]), t.dtype) for t in grads]
        self.sems = [pltpu.SemaphoreType.DMA((n,)), pltpu.SemaphoreType.DMA((n,))]

    def _copies(self, srcs, outs, sems):
        ssem, rsem = sems
        x, y, c = _me()
        copies = []
        for a in range(self.n):
            h = self.ins[a].shape[1] // 2
            copies.append(_remote(srcs[a].at[:, pl.ds((1 - c) * h, h)], outs[a], ssem.at[a], rsem.at[a], (x, y, 1 - c)))
        return copies

    def first(self, srcs, outs, sems):
        for cp in self._copies(srcs, outs, sems):
            cp.start()

    def last(self, srcs, outs, sems):
        for cp in self._copies(srcs, outs, sems):
            cp.wait()


def _reduce_finish(reds, name):
    n = len(reds)

    def body(*refs):
        outs = refs[n:2 * n]
        ssem, rsem = refs[2 * n:]
        x, y, c = _me()
        copies = []
        for a in range(n):
            h = reds[a].shape[0] // 2
            mine = outs[a].at[pl.ds(c * h, h)]
            copies.append(_remote(mine, mine, ssem.at[a], rsem.at[a], (x, y, 1 - c)))
        for cp in copies:
            cp.start()
        for a in range(n):
            h = reds[a].shape[0] // 2
            dst = outs[a].at[pl.ds((1 - c) * h, h)]
            _remote(dst, dst, ssem.at[a], rsem.at[a], (x, y, 1 - c)).wait_recv()
        for cp in copies:
            cp.wait_send()

    return _pcall(
        body, name=name, in_specs=[_HBM] * n, out_specs=[_HBM] * n,
        out_shape=[jax.ShapeDtypeStruct(t.shape, t.dtype) for t in reds],
        input_output_aliases={a: a for a in range(n)},
        scratch_shapes=[pltpu.SemaphoreType.DMA((n,)), pltpu.SemaphoreType.DMA((n,))],
    )(*reds)


def _half_sum(fn, full, halves, out_full, out_dtype, core, name):
    p, h, c = (halves[0].shape if halves else (full[0].shape[0], full[0].shape[1] // 2, full[0].shape[2]))
    br = _div(h, max(16, (1 << 19) // (p * c)), 16)
    nb = h // br
    mine3 = pl.BlockSpec((p, br, c), lambda i, core_ref: (0, core_ref[0] * nb + i, 0))
    half3 = pl.BlockSpec((p, br, c), lambda i, core_ref: (0, i, 0))

    def body(core_ref, *refs):
        refs[-1][...] = fn(*[t[...].astype(F32) for t in refs[:-1]]).astype(out_dtype)

    if out_full:
        out_spec = pl.BlockSpec((br, c), lambda i, core_ref: (core_ref[0] * nb + i, 0))
        out_shape = jax.ShapeDtypeStruct((2 * h, c), out_dtype)
    else:
        out_spec, out_shape = half3, jax.ShapeDtypeStruct((p, h, c), out_dtype)
    return _pcall(
        body, name=name,
        grid_spec=pltpu.PrefetchScalarGridSpec(
            num_scalar_prefetch=1, grid=(nb,), in_specs=[mine3] * len(full) + [half3] * len(halves),
            out_specs=out_spec),
        out_shape=out_shape, compiler_params=_cparams(("parallel",)),
    )(core, *full, *halves)


def _ada_fwd(c_all, w, b):
    def body(c_ref, w_ref, b_ref, o_ref):
        o_ref[...] = jnp.dot(c_ref[...], w_ref[...], precision=HI, preferred_element_type=F32) + b_ref[...]

    return _pcall(body, name="ada_fwd", out_shape=jax.ShapeDtypeStruct((c_all.shape[0], w.shape[1]), F32),
                  compiler_params=pltpu.CompilerParams(vmem_limit_bytes=VMEM_LIMIT))(c_all, w, b)


def _ada_bwd(c_all_t, d):
    def body(c_ref, d_ref, o_ref):
        o_ref[...] = jnp.dot(c_ref[...], d_ref[...], precision=HI, preferred_element_type=F32)

    return _pcall(body, name="ada_bwd", out_shape=jax.ShapeDtypeStruct((c_all_t.shape[0], d.shape[1]), F32),
                  compiler_params=pltpu.CompilerParams(vmem_limit_bytes=VMEM_LIMIT))(c_all_t, d)


def _sum_lead(x, name):
    p, r, n = x.shape
    br = _div(r, 512, 8)

    def body(x_ref, o_ref):
        acc = x_ref[0]
        for j in range(1, p):
            acc = acc + x_ref[j]
        o_ref[...] = acc

    return _pcall(
        body, name=name, grid=(r // br,), in_specs=[pl.BlockSpec((p, br, n), lambda i: (0, i, 0))],
        out_specs=pl.BlockSpec((br, n), lambda i: (i, 0)), out_shape=jax.ShapeDtypeStruct((r, n), F32),
        compiler_params=_cparams(("parallel",)),
    )(x)


def _adamw(w, g, m, v, name):
    shape = w.shape
    cols = shape[-1]
    w2, g2, m2, v2 = [t.reshape(-1, cols) for t in (w, g, m, v)]
    rows = w2.shape[0]
    pref = max(8, (1 << 19) // cols // 8 * 8)
    br = _div(rows, pref, 8)
    if rows // br > 64:
        br = pref
    outs = _rows_fwd(_f_adamw, [(t, cols, 0) for t in (w2, g2, m2, v2)], [], [(cols, F32)] * 3, name=name, br=br)
    return [o.reshape(shape) for o in outs]


_BIG = (("w_in", 1), ("w_up", 1), ("w_down", 0), ("w_o", 0), ("w_rwkv_out", 0), ("w_att_out", 1), ("w2", 1), ("a2", 1),
        ("g2", 1))


_NEEDED_FIRST = ("w_in", "w_att_out", "w2", "a2", "g2")
_NEEDED_LATER = ("w_up", "w_down", "w_o", "w_rwkv_out")
_DONE_EARLY = ("w_up", "w_down", "w_o", "w_rwkv_out", "w_att_out")
_DONE_LATE = ("w_in", "w2", "a2", "g2")


def _cols_joined(t):
    return jnp.concatenate([t[j] for j in range(4)], axis=1)


def _cols_split(t):
    n = t.shape[1] // 4
    return jnp.stack([t[:, j * n:(j + 1) * n] for j in range(4)])


W_IN_SHARD = (N_ATT + N_RW + N_GATE) // 4
W_IN_PAD = 2560


def _row_window(parts, lo, hi):
    out, pos = [], 0
    for t, w in parts:
        a, b = max(lo, pos), min(hi, pos + w)
        if a < b:
            out.append(t[a - pos:b - pos])
        pos += w
    return out[0] if len(out) == 1 else jnp.concatenate(out, axis=0)


def _rows_joined(t):
    return t.reshape(4 * t.shape[1], t.shape[2])


def _rows_split(t):
    return t.reshape(4, t.shape[0] // 4, t.shape[1])


def _step_to_scan(x, tgt, ada, wts):
    sh1, sc1, gt1, sh2, sc2, gt2 = ada
    br = 256
    grp = lax.broadcasted_iota(jnp.int32, (D, 128), 0) // 64 == lax.broadcasted_iota(jnp.int32, (D, 128), 1)
    e = grp.astype(F32)
    et = e.T
    w_in = [(wts["w_in"][j], W_IN_SHARD) for j in range(4)]
    w_att = _row_window(w_in, 0, N_ATT)
    w_rw = jnp.concatenate([_row_window(w_in, N_ATT, N_ATT + N_RW), jnp.zeros((N_RWP - N_RW, D), BF16)], axis=0)
    w_gate = _row_window(w_in, N_ATT + N_RW, N_ATT + N_RW + N_GATE)
    mu = jnp.pad(wts["mu_shift"], ((0, 0), (0, N_RWP - N_RW)))
    wl = jnp.zeros((N_LORA, 3 * D), F32)
    wl = wl.at[0:64, 0:D].set(_cols_joined(wts["w2"]).astype(F32))
    wl = wl.at[64:128, D:2 * D].set(_cols_joined(wts["a2"]).astype(F32))
    wl = wl.at[128:288, 2 * D:3 * D].set(_cols_joined(wts["g2"]).astype(F32))
    pre1_c = [wts["norm1_w"], sc1, sh1]
    (h1,) = _rows_fwd(_f_pre, [(x, D, 0)], pre1_c, [(D, BF16), None], name="pre1_fwd", br=2 * br)
    att_in = _mm(h1, w_att, tb=True, name="mm_att_in")
    z = _mm(h1, w_rw, tb=True, name="mm_rw_in")
    gate_in = _mm(h1, w_gate, tb=True, name="mm_gate_in")
    att_o, att_l = [], []
    for g, (_, dil) in enumerate(ATT_PATTERNS):
        o, l = _att_fwd(att_in, g, dil)
        att_o.append(o)
        att_l.append(l)
    comb_rows = [(t, ATT_WIDTH, 0) for t in att_o + att_l]
    (att,) = _rows_fwd(_f_comb, comb_rows, [], [(ATT_WIDTH, BF16)], name="comb_fwd", br=2 * br)
    y_att = _mm(att, wts["w_att_out"], b_chip=True, name="mm_att_out")
    rwpre_c = [wts["w0"], wts["a0"], wts["k_k"], wts["k_a"], wl, e, et]

    def shift_and_rwpre(zz, *rest):
        consts, mu_row, before = rest[:-2], rest[-2], rest[-1]
        last = jnp.sum(jnp.where(lax.broadcasted_iota(jnp.int32, before.shape, 0) == 7, before, 0.0), axis=0,
                       keepdims=True)
        row = lax.broadcasted_iota(jnp.int32, zz.shape, 0)
        zprev = jnp.where(row == 0, last, pltpu.roll(zz, 1, 0))
        shifted = zz + (zprev - zz) * mu_row
        return (shifted,) + tuple(_f_rwpre(shifted, *consts))

    zs, lw, km, aa, bb, gg = _rows_fwd(
        shift_and_rwpre, [(z, N_RWP, 0)], rwpre_c + [mu],
        [(N_RWP, F32), None, (D, F32), (D, F32), None, (D, F32), (D, F32), (D, F32)], name="rwpre_fwd", br=br, halo=0)
    return dict(x=x, tgt=tgt, wts=wts, br=br, e=e, et=et, gt1=gt1, sc2=sc2, sh2=sh2, gt2=gt2, w_att=w_att, w_rw=w_rw,
                w_gate=w_gate, mu=mu, pre1_c=pre1_c, h1=h1, att_in=att_in, z=z, gate_in=gate_in, comb_rows=comb_rows,
                att=att, y_att=y_att, zs=zs, rwpre_c=rwpre_c, lw=lw, km=km, aa=aa, bb=bb, gg=gg)


def _step_between_scans(st, y_raw, late):
    x, tgt, wts, br, e, et = st["x"], st["tgt"], st["wts"], st["br"], st["e"], st["et"]
    zs, km, gg, gate_in, y_att, att = st["zs"], st["km"], st["gg"], st["gate_in"], st["y_att"], st["att"]
    comb_rows, att_in = st["comb_rows"], st["att_in"]
    gt1, sc2, sh2, gt2 = st["gt1"], st["sc2"], st["sh2"], st["gt2"]
    w_up, w_ao = late["w_up"], wts["w_att_out"]
    w_down, w_o, w_ro = _rows_joined(late["w_down"]), _rows_joined(late["w_o"]), _rows_joined(late["w_rwkv_out"])
    post_rows = [(y_raw, D, 0), (zs, D, 0), (zs, D, 2), (km, D, 0), (gg, D, 0)]
    post_c = [wts["lnx_w"], wts["lnx_b"], wts["r_k"], e, et]
    (rw_out,) = _rows_fwd(_f_rwpost, post_rows, post_c, [(D, BF16)], name="rwpost_fwd", br=br)
    y_rw = _mm(rw_out, w_ro, name="mm_rw_out")
    mix_rows = [(gate_in, N_GATE, 0), (y_att, D, 0), (y_rw, D, 0)]
    (mix,) = _rows_fwd(_f_mix, mix_rows, [wts["b_gate"]], [(D, BF16)], name="mix_fwd", br=2 * br)
    o = _mm(mix, w_o, name="mm_o")
    pre2_c = [gt1, wts["norm2_w"], sc2, sh2]
    x1, h2 = _rows_fwd(_f_pre2, [(x, D, 0), (o, D, 0)], pre2_c, [(D, F32), (D, BF16)], name="pre2_fwd", br=2 * br)
    u = _mm(h2, w_up, b_chip=True, name="mm_up")
    act = _conv_fwd(u, wts["conv_w"], wts["conv_b"])
    f = _mm(act, w_down, name="mm_down")
    fin_rows = [(x1, D, 0), (f, D, 0), (tgt, D, 0)]
    fin_c = [gt2, wts["norm_f_w"]]

    def fin_fwd(*a):
        (l,) = _f_fin(*a)
        return (jnp.broadcast_to(jnp.sum(l, axis=0, keepdims=True), (8, 128)),)

    (loss_acc,) = _rows_fwd(fin_fwd, fin_rows, fin_c, [], name="fin_fwd", br=2 * br, acc_shape=(8, 128))

    gw = {}
    dx1a, df, d_gt2, gw["norm_f_w"] = _rows_bwd(
        _f_fin, fin_rows, fin_c, [[]], wrt_rows=[0, 1], wrt_consts=[0, 1], drow_dtypes=[F32, BF16],
        name="fin_bwd", br=2 * br, unit_cot=True)
    dact = _mm(df, w_down, tb=True, name="mm_dact")
    gw["w_down"] = _rows_split(_mm(act, df, ta=True, name="mm_dw_down"))
    du, gw["conv_w"], gw["conv_b"] = _conv_bwd(u, wts["conv_w"], wts["conv_b"], dact)
    dh2 = _mm(du, w_up, tb=True, b_chip=True, name="mm_dh2")
    gw["w_up"] = _mm(h2, du, ta=True, out_chip=True, name="mm_dw_up")
    dxa, do, d_gt1, gw["norm2_w"], d_sc2, d_sh2 = _rows_bwd(
        _f_pre2, [(x, D, 0), (o, D, 0)], pre2_c, [[(dx1a, D, 0)], [(dh2, D, 0)]], wrt_rows=[0, 1],
        wrt_consts=[0, 1, 2, 3], drow_dtypes=[F32, BF16], name="pre2_bwd", br=2 * br)
    dmix = _mm(do, w_o, tb=True, name="mm_dmix")
    gw["w_o"] = _rows_split(_mm(mix, do, ta=True, name="mm_dw_o"))
    dgate, dya, dyr, gw["b_gate"] = _rows_bwd(
        _f_mix, mix_rows, [wts["b_gate"]], [[(dmix, D, 0)]], wrt_rows=[0, 1, 2], wrt_consts=[0],
        drow_dtypes=[BF16] * 3, name="mix_bwd", br=2 * br)
    datt = _mm(dya, w_ao, tb=True, b_chip=True, name="mm_datt")
    gw["w_att_out"] = _mm(att, dya, ta=True, out_chip=True, name="mm_dw_att_out")
    drw = _mm(dyr, w_ro, tb=True, name="mm_drw")
    gw["w_rwkv_out"] = _rows_split(_mm(rw_out, dyr, ta=True, name="mm_dw_rw_out"))
    dcomb = _rows_bwd(_f_comb, comb_rows, [], [[(datt, ATT_WIDTH, 0)]], wrt_rows=list(range(6)), wrt_consts=[],
                      drow_dtypes=[F32] * 6, name="comb_bwd", br=2 * br)
    datt_in = None
    for g, (_, dil) in enumerate(ATT_PATTERNS):
        datt_in = _att_bwd(att_in, g, dil, dcomb[g], dcomb[3 + g], datt_in)
    dy_raw, dr_p, dv_p, dkm_p, dgg, gw["lnx_w"], gw["lnx_b"], gw["r_k"], *recv_early = _rows_bwd(
        _f_rwpost, post_rows, post_c, [[(drw, D, 0)]], wrt_rows=[0, 1, 2, 3, 4], wrt_consts=[0, 1, 2],
        drow_dtypes=[F32] * 5, name="rwpost_bwd", br=br, comm=_SiblingHalves([gw[n] for n in _DONE_EARLY]))
    st.update(loss=loss_acc[0, 0], gw=gw, dxa=dxa, dgate=dgate, datt_in=datt_in,
              dy_raw=dy_raw, dr_p=dr_p, dv_p=dv_p, dkm_p=dkm_p, dgg=dgg, d_ada_late=(d_gt1, d_sh2, d_sc2, d_gt2),
              recv_early=recv_early)
    return st


def _chip_parts(grads, recv, names, core):
    return [_half_sum(lambda a, b: a + b, [g], [r], False, BF16, core, "reduce_add2_" + n)
            for g, r, n in zip(grads, recv, names)]


def _step_after_scan(st, scan_grads, core):
    x, br, gw, h1, zs = st["x"], st["br"], st["gw"], st["h1"], st["zs"]
    dr_s, dlw, dkm_s, dv_s, daa, dbb = scan_grads
    pre_cots = [[(st["dr_p"], D, 0), (dr_s, D, 0)], [(dlw, D, 0)], [(st["dkm_p"], D, 0), (dkm_s, D, 0)],
                [(st["dv_p"], D, 0), (dv_s, D, 0)], [(daa, D, 0)], [(dbb, D, 0)], [(st["dgg"], D, 0)]]
    dzs, gw["w0"], gw["a0"], gw["k_k"], gw["k_a"], dwl = _rows_bwd(
        _f_rwpre, [(zs, N_RWP, 0)], st["rwpre_c"], pre_cots, wrt_rows=[0], wrt_consts=[0, 1, 2, 3, 4],
        drow_dtypes=[F32], name="rwpre_bwd", br=128)
    gw["w2"], gw["a2"] = _cols_split(dwl[0:64, 0:D]), _cols_split(dwl[64:128, D:2 * D])
    gw["g2"] = _cols_split(dwl[128:288, 2 * D:3 * D])
    dz, dmu = _shift_bwd(st["z"], st["mu"], dzs)
    gw["mu_shift"] = dmu[:, :N_RW]
    datt_in, dgate = st["datt_in"], st["dgate"]
    dw_in = [(_mm(datt_in, h1, ta=True, name="mm_dw_att"), N_ATT), (_mm(dz, h1, ta=True, name="mm_dw_rw"), N_RW),
             (_mm(dgate, h1, ta=True, name="mm_dw_gate"), N_GATE)]
    slabs = []
    for j in range(4):
        slabs += [_row_window(dw_in, j * W_IN_SHARD, (j + 1) * W_IN_SHARD), jnp.zeros((W_IN_PAD - W_IN_SHARD, D), F32)]
    gw["w_in"] = jnp.concatenate(slabs, axis=0).reshape(4, W_IN_PAD, D)
    late = [gw[n] for n in _DONE_LATE]
    parts = _chip_parts(late, _run_comm(_SiblingHalves(late), "reduce_sib_late"), _DONE_LATE, core)
    dh1, slots_late = _mm_sum([(datt_in, st["w_att"]), (dz, st["w_rw"]), (dgate, st["w_gate"])],
                              comm=_ScatterToChips(parts), name="mm_dh1")
    grad_x, gw["norm1_w"], d_sc1, d_sh1 = _rows_bwd(
        _f_pre, [(x, D, 0)], st["pre1_c"], [[(dh1, D, 0)], [(st["dxa"], D, 0)]], wrt_rows=[0], wrt_consts=[0, 1, 2],
        drow_dtypes=[F32], name="pre1_bwd", br=2 * br)
    d_gt1, d_sh2, d_sc2, d_gt2 = st["d_ada_late"]
    return st["loss"], grad_x, (d_sh1, d_sc1, d_gt1, d_sh2, d_sc2, d_gt2), gw, slots_late


_SMALL = ("b_ada", "norm1_w", "b_gate", "mu_shift", "w0", "a0", "k_k", "k_a", "r_k", "lnx_w", "lnx_b", "norm2_w",
          "conv_b", "norm_f_w")
_NAMES = ("w_ada", "b_ada", "norm1_w", "w_in", "b_gate", "mu_shift", "w0", "w2", "a0", "a2", "g2", "k_k", "k_a", "r_k",
          "lnx_w", "lnx_b", "w_att_out", "w_rwkv_out", "w_o", "norm2_w", "w_up", "conv_w", "conv_b", "w_down",
          "norm_f_w")


def kernel(x, c, w_ada, b_ada, norm1_w, w_in, b_gate, mu_shift, w0, w2, a0, a2, g2, k_k, k_a, r_k, lnx_w, lnx_b, w_att_out, w_rwkv_out, w_o, norm2_w, w_up, conv_w, conv_b, w_down, norm_f_w, loss_target, m_w_ada, m_b_ada, m_norm1_w, m_w_in, m_b_gate, m_mu_shift, m_w0, m_w2, m_a0, m_a2, m_g2, m_k_k, m_k_a, m_r_k, m_lnx_w, m_lnx_b, m_w_att_out, m_w_rwkv_out, m_w_o, m_norm2_w, m_w_up, m_conv_w, m_conv_b, m_w_down, m_norm_f_w, v_w_ada, v_b_ada, v_norm1_w, v_w_in, v_b_gate, v_mu_shift, v_w0, v_w2, v_a0, v_a2, v_g2, v_k_k, v_k_a, v_r_k, v_lnx_w, v_lnx_b, v_w_att_out, v_w_rwkv_out, v_w_o, v_norm2_w, v_w_up, v_conv_w, v_conv_b, v_w_down, v_norm_f_w):
    args = dict(locals())
    p, pm, pv = {}, {}, {}
    for name in _NAMES:
        for dst, key in ((p, name), (pm, "m_" + name), (pv, "v_" + name)):
            t = args[key]
            if name == "w_in":
                dst[name] = jnp.swapaxes(t, 1, 2)[0]
            else:
                dst[name] = t.reshape(1, -1) if name in ("r_k", "norm_f_w") else t.reshape(t.shape[-2], t.shape[-1])
    xi, yi, ci = _me()
    chip = 2 * xi + yi
    dev = 4 * xi + 2 * yi + ci
    x2, tgt = x[0], loss_target[0]

    n_cw = 3 * (2 * D_FF // 4)
    vec = jnp.concatenate([c.reshape(-1), p["conv_w"].reshape(-1), jnp.zeros((8 * D - D - n_cw,), F32)]).reshape(8, D)
    g0 = _allgather8(vec, "gather_c").reshape(8, 8 * D)
    c_all = g0[:, :D]
    conv_w_full = jnp.concatenate([g0[2 * j, D:D + n_cw].reshape(3, -1) for j in range(4)], axis=1)
    n_ada = 6 * D // 4
    b_ada_sh = lax.dynamic_slice(p["b_ada"], (0, chip * n_ada), (1, n_ada))
    ada_sh = _ada_fwd(c_all, p["w_ada"], b_ada_sh)
    ga = _allgather8(ada_sh, "gather_ada")
    ada_all = jnp.concatenate([ga[2 * j] for j in range(4)], axis=1)
    ada_row = lax.dynamic_slice(ada_all, (dev, 0), (1, 6 * D))
    ada = [ada_row[:, j * D:(j + 1) * D] for j in range(6)]

    big = [n for n, _ in _BIG]
    shard = {n: p[n].astype(BF16) for n in big}
    shard["w_in"] = jnp.pad(shard["w_in"], ((0, W_IN_PAD - W_IN_SHARD), (0, 0)))
    wts = dict(zip(_NEEDED_FIRST, _run_comm(_GatherWeights([shard[n] for n in _NEEDED_FIRST]), "gather_w")))
    for n in _SMALL:
        wts[n] = p[n]
    wts["conv_w"] = conv_w_full
    core = ci.reshape(1).astype(jnp.int32)

    st = _step_to_scan(x2, tgt, ada, wts)
    y_raw, s0s, inverses, late = _scan_fwd(st["zs"], st["lw"], st["km"], st["aa"], st["bb"],
                                           _GatherWeights([shard[n] for n in _NEEDED_LATER]))
    st = _step_between_scans(st, y_raw, dict(zip(_NEEDED_LATER, late)))
    early = _chip_parts([st["gw"][n] for n in _DONE_EARLY], st["recv_early"], _DONE_EARLY, core)
    scan_grads, slots_early = _scan_bwd(st["zs"], st["lw"], st["km"], st["aa"], st["bb"], s0s, inverses,
                                        st["dy_raw"], _ScatterToChips(early))
    loss_part, grad_x, d_ada, gw, slots_late = _step_after_scan(st, scan_grads, core)

    small = [jnp.concatenate(d_ada, axis=1)] + [gw[n] for n in _SMALL[1:]] + [gw["conv_w"], loss_part.reshape(1, 1)]
    sizes = [t.size for t in small]
    flat = jnp.concatenate([t.reshape(-1) for t in small])
    npad = (-flat.shape[0]) % (8 * D)
    srows = (flat.shape[0] + npad) // D
    flat = jnp.concatenate([flat, jnp.zeros((npad,), F32)]).reshape(srows, D)
    parts = _allgather8(flat, "gather_small")
    tot = _sum_lead(parts, "sum_small").reshape(-1)
    pieces, pos = [], 0
    for sz in sizes:
        pieces.append(tot[pos:pos + sz])
        pos += sz
    grads = {}
    for n, piece in zip(_SMALL, pieces[:len(_SMALL)]):
        grads[n] = piece.reshape(p[n].shape)
    conv_w_grad = pieces[len(_SMALL)].reshape(3, 2 * D_FF)
    grads["conv_w"] = lax.dynamic_slice(conv_w_grad, (0, chip * (n_cw // 3)), (3, n_cw // 3))
    loss = pieces[-1][0]
    d_ada_all = parts[:, :6].reshape(8, 6 * D)
    grads["w_ada"] = _ada_bwd(c_all.T, lax.dynamic_slice(d_ada_all, (0, chip * n_ada), (8, n_ada)))

    order = _DONE_EARLY + _DONE_LATE
    reds = [_half_sum(lambda t: t[0] + t[1] + t[2] + t[3], [], [t], True, F32, core, "reduce_add4_" + n)
            for n, t in zip(order, list(slots_early) + list(slots_late))]
    for n, g in zip(order, _reduce_finish(reds, "reduce_sib2")):
        grads[n] = g

    outs_g, outs_d, outs_m, outs_v = [], [], [], []
    grads["w_in"] = grads["w_in"][:W_IN_SHARD]
    for name in _NAMES:
        g = grads[name]
        d, m, v = _adamw(p[name], g, pm[name], pv[name], "adamw_" + name)
        shape = args[name].shape
        for outs, t in ((outs_g, g), (outs_d, d), (outs_m, m), (outs_v, v)):
            outs.append(jnp.swapaxes(t[None], 1, 2) if name == "w_in" else t.reshape(shape))
    return (loss, grad_x.reshape(x.shape), *outs_g, *outs_d, *outs_m, *outs_v)
```

```python
import functools

import jax
import jax.numpy as jnp
from jax import lax
from jax.experimental import pallas as pl
from jax.experimental.pallas import tpu as pltpu

F32 = jnp.float32
BF16 = jnp.bfloat16
HI = lax.Precision.HIGHEST
MESH = pl.DeviceIdType.MESH

D = 1024
ATT_PATTERNS = ((128, 1), (512, 4), (2048, 16))
ATT_BLOCK = 128
ATT_WIDTH = 512
N_ATT = 3 * 3 * ATT_WIDTH
N_RW = 3 * D + 64 + 64 + 160
N_RWP = 3456
N_LORA = N_RWP - 3 * D
N_GATE = 2 * D
D_FF = 2816
RMS_EPS = 1e-6
GN_EPS = 64e-5
SCAN_CHUNK = 64
SCAN_PAIRS = 8
NEG = -1e30
VMEM_LIMIT = 48 * 1024 * 1024

ADAM_LR, ADAM_B1, ADAM_B2, ADAM_EPS, ADAM_WD, ADAM_STEP = 0.001, 0.9, 0.999, 1e-08, 0.01, 10


def _pcall(body, **kw):
    return pl.pallas_call(body, **kw)


def _cparams(sem):
    return pltpu.CompilerParams(dimension_semantics=sem, vmem_limit_bytes=VMEM_LIMIT)


def _div(n, pref, mult):
    best = None
    d = mult
    while d <= min(n, pref):
        if n % d == 0:
            best = d
        d += mult
    return best if best else n


def _dg(a, b, ca, cb):
    return lax.dot_general(a.astype(BF16), b.astype(BF16), (((ca,), (cb,)), ((), ())), preferred_element_type=F32)


@jax.custom_vjp
def _nn(a, b):
    return _dg(a, b, 1, 0)


@jax.custom_vjp
def _nt(a, b):
    return _dg(a, b, 1, 1)


@jax.custom_vjp
def _tn(a, b):
    return _dg(a, b, 0, 0)


_nn.defvjp(lambda a, b: (_nn(a, b), (a, b)), lambda res, g: (_nt(g, res[1]), _tn(res[0], g)))
_nt.defvjp(lambda a, b: (_nt(a, b), (a, b)), lambda res, g: (_nn(g, res[1]), _tn(g, res[0])))
_tn.defvjp(lambda a, b: (_tn(a, b), (a, b)), lambda res, g: (_nt(res[1], g), _nn(res[0], g)))


def _bdg(a, b, ca, cb):
    return lax.dot_general(a.astype(BF16), b.astype(BF16), (((ca,), (cb,)), ((0,), (0,))), preferred_element_type=F32)


@jax.custom_vjp
def _bnn(a, b):
    return _bdg(a, b, 2, 1)


@jax.custom_vjp
def _bnt(a, b):
    return _bdg(a, b, 2, 2)


@jax.custom_vjp
def _btn(a, b):
    return _bdg(a, b, 1, 1)


_bnn.defvjp(lambda a, b: (_bnn(a, b), (a, b)), lambda res, g: (_bnt(g, res[1]), _btn(res[0], g)))
_bnt.defvjp(lambda a, b: (_bnt(a, b), (a, b)), lambda res, g: (_bnn(g, res[1]), _btn(g, res[0])))
_btn.defvjp(lambda a, b: (_btn(a, b), (a, b)), lambda res, g: (_bnt(res[1], g), _bnn(res[0], g)))


def _split2(x):
    hi = x.astype(BF16)
    lo = (x - hi.astype(F32)).astype(BF16)
    return hi, lo


def _hsum_impl(x, e, et):
    eb, etb = e.astype(BF16), et.astype(BF16)
    s = jnp.dot(x.astype(BF16), eb, preferred_element_type=F32)
    shi, slo = _split2(s)
    return jnp.dot(shi, etb, preferred_element_type=F32) + jnp.dot(slo, etb, preferred_element_type=F32)


@jax.custom_vjp
def _hsum(x, e, et):
    return _hsum_impl(x, e, et)


_hsum.defvjp(lambda x, e, et: (_hsum_impl(x, e, et), (e, et)),
             lambda res, g: (_hsum_impl(g, res[0], res[1]), jnp.zeros_like(res[0]), jnp.zeros_like(res[1])))


def _mm(a, b, *, ta=False, tb=False, out_dtype=F32, add=None, b_chip=False, out_chip=False, comm=None, name):
    riding = _NOTHING if comm is None else comm
    nc = riding.n
    if ta:
        kdim, m = a.shape
    else:
        m, kdim = a.shape
    if b_chip:
        n = b.shape[1] if tb else 4 * b.shape[2]
    else:
        n = b.shape[0] if tb else b.shape[1]
    tm, tn, tk = _div(m, 1536, 128), _div(n, 1536, 128), _div(kdim, 1408, 128)
    if b_chip and tb:
        tk = kdim // 4
    if (b_chip and not tb) or out_chip:
        tn = n // 4
    nk = kdim // tk
    ca, cb = (0 if ta else 1), (1 if tb else 0)

    nin = 2 if add is None else 3
    gi, gj = m // tm, n // tn

    def body(*refs):
        a_ref, b_ref = refs[0], refs[1]
        add_ref = None if add is None else refs[2]
        o_ref = refs[nin + nc]
        step = (pl.program_id(0) * gj + pl.program_id(1)) * nk + pl.program_id(2)
        before, after = _comm_phases(riding, refs[nin:nin + nc] + refs[nin + nc + 1:nin + 2 * nc + 1]
                                     + refs[nin + 2 * nc + 1 + (nk > 1):], gi * gj * nk, step)
        before()
        part = lax.dot_general(a_ref[...], b_ref[...], (((ca,), (cb,)), ((), ())), preferred_element_type=F32)

        def finish(r):
            if add_ref is not None:
                r = r + add_ref[...]
            o_ref[...] = r.astype(o_ref.dtype)

        if nk == 1:
            finish(part)
            after()
            return
        acc = refs[nin + 2 * nc + 1]
        k = pl.program_id(2)

        @pl.when(k == 0)
        def _():
            acc[...] = part

        @pl.when(k > 0)
        def _():
            acc[...] += part

        @pl.when(k == nk - 1)
        def _():
            finish(acc[...])

        after()

    a_spec = pl.BlockSpec((tk, tm), lambda i, j, k: (k, i)) if ta else pl.BlockSpec((tm, tk), lambda i, j, k: (i, k))
    if b_chip:
        b_spec = (pl.BlockSpec((None, tn, tk), lambda i, j, k: (k, j, 0)) if tb
                  else pl.BlockSpec((None, tk, tn), lambda i, j, k: (j, k, 0)))
    else:
        b_spec = pl.BlockSpec((tn, tk), lambda i, j, k: (j, k)) if tb else pl.BlockSpec((tk, tn), lambda i, j, k: (k, j))
    in_specs = [a_spec, b_spec]
    args = [a, b]
    if add is not None:
        in_specs.append(pl.BlockSpec((tm, tn), lambda i, j, k: (i, j)))
        args.append(add)
    if out_chip:
        out_spec = pl.BlockSpec((None, tm, tn), lambda i, j, k: (j, i, 0))
        out_shape = jax.ShapeDtypeStruct((4, m, tn), out_dtype)
    else:
        out_spec = pl.BlockSpec((tm, tn), lambda i, j, k: (i, j))
        out_shape = jax.ShapeDtypeStruct((m, n), out_dtype)
    res = _pcall(
        body, name=name, grid=(gi, gj, nk), in_specs=in_specs + [_HBM] * nc, out_specs=[out_spec] + [_HBM] * nc,
        out_shape=[out_shape] + riding.out_shape,
        scratch_shapes=([] if nk == 1 else [pltpu.VMEM((tm, tn), F32)]) + riding.sems,
        compiler_params=_cparams(("arbitrary",) * 3 if nc else ("parallel", "parallel", "arbitrary")),
    )(*args, *riding.ins)
    return res[0] if comm is None else (res[0], res[1:])


def _mm_sum(pairs, *, comm, name):
    m, n = pairs[0][0].shape[0], pairs[0][1].shape[1]
    tm, tn = _div(m, 1024, 128), _div(n, 1024, 128)
    tks = [_div(a.shape[1], 1408, 128) for a, _ in pairs]
    nks = [a.shape[1] // tk for (a, _), tk in zip(pairs, tks)]
    offs = [sum(nks[:p]) for p in range(len(pairs))]
    total, npair, nc = sum(nks), len(pairs), comm.n
    gi, gj = m // tm, n // tn

    def body(*refs):
        o_ref, acc = refs[2 * npair + nc], refs[2 * npair + 2 * nc + 1]
        k = pl.program_id(2)
        step = (pl.program_id(0) * gj + pl.program_id(1)) * total + k
        before, after = _comm_phases(comm, refs[2 * npair:2 * npair + nc]
                                     + refs[2 * npair + nc + 1:2 * npair + 2 * nc + 1]
                                     + refs[2 * npair + 2 * nc + 2:], gi * gj * total, step)
        before()
        for p in range(npair):
            def partial_product(p=p):
                part = jnp.dot(refs[2 * p][...], refs[2 * p + 1][...], preferred_element_type=F32)
                if p == 0:
                    @pl.when(k == 0)
                    def _():
                        acc[...] = part

                    @pl.when(k > 0)
                    def _():
                        acc[...] += part
                else:
                    acc[...] += part

            pl.when(jnp.logical_and(k >= offs[p], k < offs[p] + nks[p]))(partial_product)

        @pl.when(k == total - 1)
        def _():
            o_ref[...] = acc[...]

        after()

    def specs(tk, off, nk):
        def kb(k):
            return jnp.clip(k - off, 0, nk - 1)
        return [pl.BlockSpec((tm, tk), lambda i, j, k: (i, kb(k))), pl.BlockSpec((tk, tn), lambda i, j, k: (kb(k), j))]

    in_specs, args = [], []
    for (a, b), tk, off, nk in zip(pairs, tks, offs, nks):
        in_specs += specs(tk, off, nk)
        args += [a, b]
    res = _pcall(
        body, name=name, grid=(gi, gj, total), in_specs=in_specs + [_HBM] * nc,
        out_specs=[pl.BlockSpec((tm, tn), lambda i, j, k: (i, j))] + [_HBM] * nc,
        out_shape=[jax.ShapeDtypeStruct((m, n), F32)] + comm.out_shape,
        scratch_shapes=[pltpu.VMEM((tm, tn), F32)] + comm.sems,
        compiler_params=_cparams(("arbitrary",) * 3),
    )(*args, *comm.ins)
    return res[0], res[1:]


def _row_spec(br, w, cb):
    return pl.BlockSpec((br, w), lambda i: (i, cb))


def _const_spec(shape):
    return pl.BlockSpec(shape, lambda i: (0,) * len(shape))


def _rows_fwd(fn, rows, consts, outs, *, name, br, acc_shape=None, halo=None):
    s = rows[0][0].shape[0]
    nr, nc = len(rows), len(consts)
    kept = [k for k, o in enumerate(outs) if o is not None]

    def body(*refs):
        xs = [r[...].astype(F32) for r in refs[:nr]]
        cs = [c[...] for c in refs[nr:nr + nc]]
        if halo is not None:
            cs.append(jnp.where(pl.program_id(0) == 0, 0.0, refs[nr + nc][...].astype(F32)))
        res = fn(*xs, *cs)
        orefs = refs[nr + nc + (halo is not None):]
        for j, k in enumerate(kept):
            orefs[j][...] = res[k].astype(orefs[j].dtype)
        if acc_shape is not None:
            acc_ref = orefs[len(kept)]

            @pl.when(pl.program_id(0) == 0)
            def _():
                acc_ref[...] = jnp.zeros_like(acc_ref)

            acc_ref[...] += res[len(outs)]

    in_specs = [_row_spec(br, w, cb) for (_, w, cb) in rows] + [_const_spec(c.shape) for c in consts]
    args = [r[0] for r in rows] + list(consts)
    if halo is not None:
        harr, hw, hcb = rows[halo]
        in_specs.append(pl.BlockSpec((8, hw), lambda i: (jnp.maximum(i * (br // 8) - 1, 0), hcb)))
        args.append(harr)
    out_specs = [_row_spec(br, outs[k][0], 0) for k in kept]
    out_shape = [jax.ShapeDtypeStruct((s, outs[k][0]), outs[k][1]) for k in kept]
    if acc_shape is not None:
        out_specs.append(_const_spec(acc_shape))
        out_shape.append(jax.ShapeDtypeStruct(acc_shape, F32))
    return _pcall(
        body, name=name, grid=(pl.cdiv(s, br),), in_specs=in_specs, out_specs=out_specs, out_shape=out_shape,
        compiler_params=_cparams(("arbitrary",)),
    )(*args)


def _rows_bwd(fn, rows, consts, cots, *, wrt_rows, wrt_consts, drow_dtypes, name, br, unit_cot=False, comm=None):
    comm = _NOTHING if comm is None else comm
    ncomm = comm.n
    nout = len(wrt_rows) + len(wrt_consts)
    s = rows[0][0].shape[0]
    nr, nc = len(rows), len(consts)
    flat_cots = [c for lst in cots for c in lst]
    ncot = len(flat_cots)

    def body(*refs):
        xs = [r[...].astype(F32) for r in refs[:nr]]
        cs = [c[...] for c in refs[nr:nr + nc]]
        cvals = [c[...].astype(F32) for c in refs[nr + nc:nr + nc + ncot]]
        orefs = refs[nr + nc + ncot + ncomm:]
        before, after = _comm_phases(comm, refs[nr + nc + ncot:nr + nc + ncot + ncomm] + orefs[nout:], s // br)
        before()

        def g(*d):
            xs2, cs2 = list(xs), list(cs)
            for j, k in enumerate(wrt_rows):
                xs2[k] = d[j]
            for j, k in enumerate(wrt_consts):
                cs2[k] = d[len(wrt_rows) + j]
            return tuple(fn(*xs2, *cs2))

        prim = [xs[k] for k in wrt_rows] + [cs[k] for k in wrt_consts]
        outs, vjp = jax.vjp(g, *prim)
        ct = []
        pos = 0
        for o, lst in zip(outs, cots):
            if unit_cot:
                ct.append(jnp.ones_like(o))
                continue
            acc = jnp.zeros_like(o)
            for _ in lst:
                acc = acc + cvals[pos]
                pos += 1
            ct.append(acc)
        grads = vjp(tuple(ct))
        for j in range(len(wrt_rows)):
            orefs[j][...] = grads[j].astype(orefs[j].dtype)

        @pl.when(pl.program_id(0) == 0)
        def _():
            for j in range(len(wrt_consts)):
                oref = orefs[len(wrt_rows) + j]
                oref[...] = jnp.zeros_like(oref)

        for j in range(len(wrt_consts)):
            orefs[len(wrt_rows) + j][...] += grads[len(wrt_rows) + j]
        after()

    in_specs = ([_row_spec(br, w, cb) for (_, w, cb) in rows] + [_const_spec(c.shape) for c in consts]
                + [_row_spec(br, w, cb) for (_, w, cb) in flat_cots] + [_HBM] * ncomm)
    out_specs = ([_row_spec(br, rows[k][1], 0) for k in wrt_rows] + [_const_spec(consts[k].shape) for k in wrt_consts]
                 + [_HBM] * ncomm)
    out_shape = ([jax.ShapeDtypeStruct((s, rows[k][1]), dt) for k, dt in zip(wrt_rows, drow_dtypes)]
                 + [jax.ShapeDtypeStruct(consts[k].shape, F32) for k in wrt_consts] + comm.out_shape)
    return _pcall(
        body, name=name, grid=(s // br,), in_specs=in_specs, out_specs=out_specs, out_shape=out_shape,
        scratch_shapes=comm.sems, compiler_params=_cparams(("arbitrary",)),
    )(*[r[0] for r in rows], *consts, *[c[0] for c in flat_cots], *comm.ins)


def _rms(x, w):
    return x * lax.rsqrt(jnp.mean(x * x, axis=-1, keepdims=True) + RMS_EPS) * w


def _softplus(x):
    return jnp.maximum(x, 0.0) + jnp.log(1.0 + jnp.exp(-jnp.abs(x)))


def _f_pre(x, nw, sc, sh):
    return _rms(x, nw) * (1.0 + sc) + sh, x


def _f_pre2(x, o, gt, nw, sc, sh):
    x1 = x + gt * o
    return x1, _rms(x1, nw) * (1.0 + sc) + sh


def _f_fin(x1, f, tgt, gt, nfw):
    y = _rms(x1 + gt * f, nfw)
    return (0.5 * jnp.mean(jnp.square(y - tgt), axis=-1, keepdims=True),)


def _f_comb(o1, o2, o3, l1, l2, l3):
    m = lax.stop_gradient(jnp.maximum(jnp.maximum(l1, l2), l3))
    e1, e2, e3 = jnp.exp(l1 - m), jnp.exp(l2 - m), jnp.exp(l3 - m)
    return ((e1 * o1 + e2 * o2 + e3 * o3) / (e1 + e2 + e3),)


def _f_rwpre(zs, w0, a0, k_k, k_a, wl, e, et):
    r, k, v, zl = zs[:, 0:D], zs[:, D:2 * D], zs[:, 2 * D:3 * D], zs[:, 3 * D:N_RWP]
    lane = lax.broadcasted_iota(jnp.int32, zl.shape, 1)
    t = jnp.where(lane < 64, jnp.tanh(zl), jnp.where(lane < 128, zl, jnp.where(lane < 288, jax.nn.sigmoid(zl), 0.0)))
    lo = _nn(t[:, 0:128], wl[0:128, 0:2 * D])
    g = _nn(t[:, 128:N_LORA], wl[128:N_LORA, 2 * D:3 * D])
    w_log = -_softplus(-(w0 + lo[:, 0:D])) - 0.5
    lw = -jnp.exp(w_log)
    a = jax.nn.sigmoid(a0 + lo[:, D:2 * D])
    k_mod = k * (1.0 + (a - 1.0) * k_a)
    kk = k * k_k
    kk = kk / jnp.maximum(jnp.sqrt(_hsum(kk * kk, e, et)), 1e-12)
    return r, lw, k_mod, v, -kk, kk * a, g


def _f_rwpost(y, r, v, k_mod, g, lnx_w, lnx_b, r_k, e, et):
    mean = _hsum(y, e, et) * (1.0 / 64)
    yc = y - mean
    var = _hsum(yc * yc, e, et) * (1.0 / 64)
    yn = yc * lax.rsqrt(var + GN_EPS) * lnx_w + lnx_b
    bonus = _hsum(r * k_mod * r_k, e, et) * v
    return ((yn + bonus) * g,)


def _f_mix(gi, ya, yr, bg):
    gate = jax.nn.sigmoid(gi + bg)
    return (gate[:, 0:D] * ya + gate[:, D:2 * D] * yr,)


def _f_adamw(w, g, m, v):
    m = ADAM_B1 * m + (1.0 - ADAM_B1) * g
    v = ADAM_B2 * v + (1.0 - ADAM_B2) * jnp.square(g)
    m_hat = m / (1.0 - ADAM_B1 ** ADAM_STEP)
    v_hat = v / (1.0 - ADAM_B2 ** ADAM_STEP)
    return -ADAM_LR * (m_hat / (jnp.sqrt(v_hat) + ADAM_EPS) + ADAM_WD * w), m, v


def _down(x, k):
    row = lax.broadcasted_iota(jnp.int32, x.shape, 0)
    return jnp.where(row < k, 0.0, pltpu.roll(x, k, 0))


def _up(x, k):
    n = x.shape[0]
    row = lax.broadcasted_iota(jnp.int32, x.shape, 0)
    return jnp.where(row >= n - k, 0.0, pltpu.roll(x, n - k, 0))


def _col_spec(s, w, off=0):
    return pl.BlockSpec((s, w), lambda j: (0, j + off))


def _shift_bwd(z, mu, dzs):
    s, n = z.shape

    def body(z_ref, mu_ref, d_ref, dz_ref, dmu_ref):
        zz, d, m = z_ref[...], d_ref[...], mu_ref[...]
        dm = d * m
        dz_ref[...] = (d - dm + _up(dm, 1)).astype(dz_ref.dtype)
        dmu_ref[...] = jnp.sum(d * (_down(zz, 1) - zz), axis=0, keepdims=True)

    return _pcall(
        body, name="shift_bwd", grid=(n // 128,), in_specs=[_col_spec(s, 128), _col_spec(1, 128), _col_spec(s, 128)],
        out_specs=[_col_spec(s, 128), _col_spec(1, 128)],
        out_shape=[jax.ShapeDtypeStruct((s, n), BF16), jax.ShapeDtypeStruct((1, n), F32)],
        compiler_params=_cparams(("parallel",)),
    )(z, mu, dzs)


def _conv3(x, w_ref, b_ref):
    return b_ref[...] + w_ref[0:1, :] * _down(x, 2) + w_ref[1:2, :] * _down(x, 1) + w_ref[2:3, :] * x


def _conv_fwd(u, cw, cb):
    s = u.shape[0]
    nb = D_FF // 128

    def body(ug_ref, uv_ref, wg_ref, wv_ref, bg_ref, bv_ref, o_ref):
        gate = _conv3(ug_ref[...], wg_ref, bg_ref)
        val = _conv3(uv_ref[...], wv_ref, bv_ref)
        o_ref[...] = (gate * jax.nn.sigmoid(gate) * val).astype(o_ref.dtype)

    return _pcall(
        body, name="conv_fwd", grid=(nb,),
        in_specs=[_col_spec(s, 128), _col_spec(s, 128, nb), _col_spec(3, 128), _col_spec(3, 128, nb),
                  _col_spec(1, 128), _col_spec(1, 128, nb)],
        out_specs=_col_spec(s, 128), out_shape=jax.ShapeDtypeStruct((s, D_FF), BF16),
        compiler_params=_cparams(("parallel",)),
    )(u, u, cw, cw, cb, cb)


def _conv_bwd(u, cw, cb, dact):
    s = u.shape[0]
    nb = D_FF // 128

    def half(x, d, w_ref, du_ref, dw_ref, db_ref):
        x1, x2 = _down(x, 1), _down(x, 2)
        du_ref[...] = (w_ref[2:3, :] * d + w_ref[1:2, :] * _up(d, 1) + w_ref[0:1, :] * _up(d, 2)).astype(du_ref.dtype)
        dw_ref[0:1, :] = jnp.sum(d * x2, axis=0, keepdims=True)
        dw_ref[1:2, :] = jnp.sum(d * x1, axis=0, keepdims=True)
        dw_ref[2:3, :] = jnp.sum(d * x, axis=0, keepdims=True)
        db_ref[...] = jnp.sum(d, axis=0, keepdims=True)

    def body(ug_ref, uv_ref, wg_ref, wv_ref, bg_ref, bv_ref, da_ref,
             dug_ref, duv_ref, dwg_ref, dwv_ref, dbg_ref, dbv_ref):
        ug, uv, da = ug_ref[...], uv_ref[...], da_ref[...]
        gate = _conv3(ug, wg_ref, bg_ref)
        val = _conv3(uv, wv_ref, bv_ref)
        sg = jax.nn.sigmoid(gate)
        dgate = da * val * sg * (1.0 + gate * (1.0 - sg))
        dval = da * gate * sg
        half(ug, dgate, wg_ref, dug_ref, dwg_ref, dbg_ref)
        half(uv, dval, wv_ref, duv_ref, dwv_ref, dbv_ref)

    dug, duv, dwg, dwv, dbg, dbv = _pcall(
        body, name="conv_bwd", grid=(nb,),
        in_specs=[_col_spec(s, 128), _col_spec(s, 128, nb), _col_spec(3, 128), _col_spec(3, 128, nb),
                  _col_spec(1, 128), _col_spec(1, 128, nb), _col_spec(s, 128)],
        out_specs=[_col_spec(s, 128), _col_spec(s, 128), _col_spec(3, 128), _col_spec(3, 128),
                   _col_spec(1, 128), _col_spec(1, 128)],
        out_shape=[jax.ShapeDtypeStruct((s, D_FF), BF16), jax.ShapeDtypeStruct((s, D_FF), BF16),
                   jax.ShapeDtypeStruct((3, D_FF), F32), jax.ShapeDtypeStruct((3, D_FF), F32),
                   jax.ShapeDtypeStruct((1, D_FF), F32), jax.ShapeDtypeStruct((1, D_FF), F32)],
        compiler_params=_cparams(("parallel",)),
    )(u, u, cw, cw, cb, cb, dact)
    return (jnp.concatenate([dug, duv], axis=1), jnp.concatenate([dwg, dwv], axis=1),
            jnp.concatenate([dbg, dbv], axis=1))


ATT_BATCH = 4


def _att_batch(q, kp, kc, vp, vc, first):
    ma = lax.broadcasted_iota(jnp.int32, (1, ATT_BLOCK, 128), 2) < 64

    def diag(x):
        return jnp.concatenate([jnp.where(ma, x, 0.0), jnp.where(ma, 0.0, x)], axis=1)

    qi = lax.broadcasted_iota(jnp.int32, (1, ATT_BLOCK, 2 * ATT_BLOCK), 1)
    kj = lax.broadcasted_iota(jnp.int32, (1, ATT_BLOCK, 2 * ATT_BLOCK), 2) & (ATT_BLOCK - 1)
    okp = kj >= qi + jnp.where(first, 2 * ATT_BLOCK, 0)
    okc = kj <= qi
    sp = jnp.where(okp, _bnt(q, diag(kp)) * 0.125, NEG)
    sc = jnp.where(okc, _bnt(q, diag(kc)) * 0.125, NEG)

    def per_head(fn, x):
        return fn(x[..., :ATT_BLOCK]), fn(x[..., ATT_BLOCK:])

    def spread(ab):
        return jnp.concatenate([jnp.broadcast_to(t, t.shape[:2] + (ATT_BLOCK,)) for t in ab], axis=-1)

    row_max = functools.partial(jnp.max, axis=-1, keepdims=True)
    row_sum = functools.partial(jnp.sum, axis=-1, keepdims=True)
    m = [lax.stop_gradient(jnp.maximum(a, b)) for a, b in zip(per_head(row_max, sp), per_head(row_max, sc))]
    pp, pc = jnp.exp(sp - spread(m)), jnp.exp(sc - spread(m))
    den = [a + b for a, b in zip(per_head(row_sum, pp), per_head(row_sum, pc))]
    num = _bnn(pp, diag(vp)) + _bnn(pc, diag(vc))
    out = num / jnp.where(ma, den[0], den[1])
    lse = jnp.where(ma, m[0] + jnp.log(den[0]), m[1] + jnp.log(den[1]))
    return out, jnp.broadcast_to(lse, out.shape)


def _att_pairs_per_step(dil):
    return 4 if dil == 1 else 1


def _att_residues(dil):
    return min(dil, ATT_BATCH // _att_pairs_per_step(dil))


def _att_specs(g, dil):
    rows, pp = ATT_BLOCK * dil, _att_pairs_per_step(dil)

    def cur(slot):
        return pl.BlockSpec((rows, 128 * pp), lambda n, p: (n, (g * 3 + slot) * (4 // pp) + p))

    def prev(slot):
        return pl.BlockSpec((rows, 128 * pp), lambda n, p: (jnp.maximum(n - 1, 0), (g * 3 + slot) * (4 // pp) + p))

    return [cur(0), prev(1), cur(1), prev(2), cur(2)]


def _att_out_spec(dil):
    return pl.BlockSpec((ATT_BLOCK * dil, 128 * _att_pairs_per_step(dil)), lambda n, p: (n, p))


def _att_grid(s, dil):
    return (s // (ATT_BLOCK * dil), 4 // _att_pairs_per_step(dil))


def _att_windows(i, dil):
    res = _att_residues(dil)

    def rows(r):
        return pl.ds(i * res + r, ATT_BLOCK, stride=dil) if dil > 1 else pl.ds(0, ATT_BLOCK)

    return [(rows(r), pl.ds(128 * j, 128)) for j in range(_att_pairs_per_step(dil)) for r in range(res)]


def _att_fwd(att_in, g, dil):
    s = att_in.shape[0]

    def body(q_ref, kp_ref, kc_ref, vp_ref, vc_ref, o_ref, l_ref):
        first = pl.program_id(0) == 0

        def one(i, carry):
            win = _att_windows(i, dil)
            vals = [jnp.stack([ref[w] for w in win]) for ref in (q_ref, kp_ref, kc_ref, vp_ref, vc_ref)]
            o, l = _att_batch(*vals, first)
            for j, w in enumerate(win):
                o_ref[w] = o[j]
                l_ref[w] = l[j]
            return carry

        lax.fori_loop(0, dil // _att_residues(dil), one, 0)

    return _pcall(
        body, name=f"att_fwd{g}", grid=_att_grid(s, dil), in_specs=_att_specs(g, dil),
        out_specs=[_att_out_spec(dil)] * 2, out_shape=[jax.ShapeDtypeStruct((s, ATT_WIDTH), F32)] * 2,
        compiler_params=_cparams(("parallel", "parallel")),
    )(att_in, att_in, att_in, att_in, att_in)


def _att_bwd(att_in, g, dil, do, dl, acc):
    s = att_in.shape[0]

    def body(q_ref, kp_ref, kc_ref, vp_ref, vc_ref, do_ref, dl_ref, dq_ref, dkp_ref, dkc_ref, dvp_ref, dvc_ref):
        first = pl.program_id(0) == 0

        def one(i, carry):
            win = _att_windows(i, dil)
            vals = [jnp.stack([ref[w] for w in win]) for ref in (q_ref, kp_ref, kc_ref, vp_ref, vc_ref)]
            _, vjp = jax.vjp(lambda *a: _att_batch(*a, first), *vals)
            grads = vjp((jnp.stack([do_ref[w] for w in win]), jnp.stack([dl_ref[w] for w in win])))
            for ref, gr in zip((dq_ref, dkp_ref, dkc_ref, dvp_ref, dvc_ref), grads):
                for j, w in enumerate(win):
                    ref[w] = gr[j]
            return carry

        lax.fori_loop(0, dil // _att_residues(dil), one, 0)

    dq, dkp, dkc, dvp, dvc = _pcall(
        body, name=f"att_bwd{g}", grid=_att_grid(s, dil), in_specs=_att_specs(g, dil) + [_att_out_spec(dil)] * 2,
        out_specs=[_att_out_spec(dil)] * 5, out_shape=[jax.ShapeDtypeStruct((s, ATT_WIDTH), F32)] * 5,
        compiler_params=_cparams(("parallel", "parallel")),
    )(att_in, att_in, att_in, att_in, att_in, do, dl)

    unit, rb = ATT_BLOCK * dil, 1024
    steps = s // rb
    within = unit < rb

    def shifted(cur_ref, next_ref, has_next):
        nxt = jnp.where(has_next, next_ref[...], 0.0)
        return jnp.concatenate([cur_ref[unit:, :], nxt], axis=0) if within else nxt

    def cbody(dq_ref, dkc_ref, dkp_ref, dkn_ref, dvc_ref, dvp_ref, dvn_ref, *rest):
        o_ref = rest[-1]
        has_next = pl.program_id(0) + (1 if within else unit // rb) < steps
        o_ref[:, 0:ATT_WIDTH] = dq_ref[...].astype(BF16)
        o_ref[:, ATT_WIDTH:2 * ATT_WIDTH] = (dkc_ref[...] + shifted(dkp_ref, dkn_ref, has_next)).astype(BF16)
        o_ref[:, 2 * ATT_WIDTH:3 * ATT_WIDTH] = (dvc_ref[...] + shifted(dvp_ref, dvn_ref, has_next)).astype(BF16)

    cur = pl.BlockSpec((rb, ATT_WIDTH), lambda i: (i, 0))
    if within:
        nxt = pl.BlockSpec((unit, ATT_WIDTH), lambda i: (jnp.minimum((i + 1) * (rb // unit), s // unit - 1), 0))
    else:
        nxt = pl.BlockSpec((rb, ATT_WIDTH), lambda i: (jnp.minimum(i + unit // rb, steps - 1), 0))
    carried = [] if acc is None else [acc]
    return _pcall(
        cbody, name=f"att_bwd_sum{g}", grid=(steps,),
        in_specs=[cur, cur, cur, nxt, cur, cur, nxt] + [pl.BlockSpec(memory_space=pl.ANY)] * len(carried),
        out_specs=pl.BlockSpec((rb, 3 * ATT_WIDTH), lambda i: (i, g)),
        out_shape=jax.ShapeDtypeStruct((s, N_ATT), BF16), input_output_aliases={7: 0} if carried else {},
        compiler_params=_cparams(("parallel",)),
    )(dq, dkc, dkp, dkp, dvc, dvp, dvp, *carried)


def _unit_lower_inverse_impl(n):
    eye = (lax.broadcasted_iota(jnp.int32, (1,) + n.shape[1:], 1)
           == lax.broadcasted_iota(jnp.int32, (1,) + n.shape[1:], 2))
    t = jnp.where(eye, 1.0, 0.0) + n
    pw = n
    for _ in range(5):
        pw = _bnn(pw, pw)
        t = t + _bnn(t, pw)
    return t


@jax.custom_vjp
def _unit_lower_inverse(n):
    return _unit_lower_inverse_impl(n)


def _unit_lower_inverse_fwd(n):
    t = _unit_lower_inverse_impl(n)
    return t, t


_unit_lower_inverse.defvjp(_unit_lower_inverse_fwd, lambda t, g: (_bnt(_btn(t, g), t),))


@jax.custom_vjp
def _known_inverse(n, t):
    return t


_known_inverse.defvjp(lambda n, t: (t, t), lambda t, g: (_bnt(_btn(t, g), t), jnp.zeros_like(t)))


def _scan_chunk(r, lw, k, v, a, b, s0, inverse):
    c = SCAN_CHUNK
    p = s0.shape[0]
    ri = lax.broadcasted_iota(jnp.int32, (c, c), 0)
    ci = lax.broadcasted_iota(jnp.int32, (c, c), 1)
    cum = jnp.dot((ci <= ri).astype(F32), lw, precision=HI, preferred_element_type=F32)
    tot = jnp.sum(lw, axis=0, keepdims=True)
    ma = (lax.broadcasted_iota(jnp.int32, (c, 128 * p), 1) & 127) < 64

    def pairs(x):
        return jnp.concatenate([x[None, :, 128 * j:128 * (j + 1)] for j in range(p)], axis=0)

    def stack(x):
        return jnp.concatenate([pairs(jnp.where(ma, x, 0.0)), pairs(jnp.where(ma, 0.0, x))], axis=1)

    einv, eend = jnp.exp(-cum), jnp.exp(tot - cum)
    ra, aa = stack(r * jnp.exp(cum)), stack(a * jnp.exp(cum - lw))
    bi, ki, be, ke, vs = stack(b * einv), stack(k * einv), stack(b * eend), stack(k * eend), stack(v)
    r2 = lax.broadcasted_iota(jnp.int32, (1, 2 * c, 2 * c), 1)
    c2 = lax.broadcasted_iota(jnp.int32, (1, 2 * c, 2 * c), 2)
    same = (r2 >= c) == (c2 >= c)
    strict = jnp.logical_and(same, c2 < r2)
    incl = jnp.logical_and(same, c2 <= r2)
    s0 = jnp.where(same, s0, 0.0)
    prod = _bnt(jnp.concatenate([aa, ra], axis=1), jnp.concatenate([bi, ki], axis=1))
    a_ab = jnp.where(strict, prod[:, :2 * c, :2 * c], 0.0)
    a_ak = jnp.where(strict, prod[:, :2 * c, 2 * c:], 0.0)
    a_rb = jnp.where(incl, prod[:, 2 * c:, :2 * c], 0.0)
    a_rk = jnp.where(incl, prod[:, 2 * c:, 2 * c:], 0.0)
    t = inverse(a_ab)
    u = _bnn(t, _bnt(aa, s0) + _bnn(a_ak, vs))
    uv = jnp.concatenate([u, vs], axis=1)
    ys = _bnt(ra, s0) + _bnn(jnp.concatenate([a_rb, a_rk], axis=2), uv)
    s1 = s0 * pairs(jnp.exp(tot)) + _btn(uv, jnp.concatenate([be, ke], axis=1))
    y3 = ys[:, :c] + ys[:, c:]
    return (jnp.concatenate([y3[j] for j in range(p)], axis=1), s1), t


def _scan_specs(rev, n):
    def at(i):
        return n - 1 - i if rev else i

    def cm(cb):
        return pl.BlockSpec((SCAN_CHUNK, D), lambda i: (at(i), cb))

    return cm, pl.BlockSpec((1, SCAN_PAIRS, 128, 128), lambda i: (at(i), 0, 0, 0))


def _comm_phases(comm, refs, n, step=None):
    k = comm.n
    srcs, outs, sems = refs[:k], refs[k:2 * k], refs[2 * k:]
    i = pl.program_id(0) if step is None else step

    def before():
        @pl.when(i == 0)
        def _():
            comm.first(srcs, outs, sems)

    def after():
        if comm.mid is not None:
            @pl.when(i == (3 * n) // 4)
            def _():
                comm.mid(srcs, outs, sems)

        @pl.when(i == n - 1)
        def _():
            comm.last(srcs, outs, sems)

    return before, after


def _scan_fwd(zs, lw, km, aa, bb, comm):
    s = zs.shape[0]
    n = s // SCAN_CHUNK
    cm, st = _scan_specs(False, n)
    k = comm.n

    def body(*refs):
        r_ref, lw_ref, k_ref, v_ref, a_ref, b_ref = refs[:6]
        y_ref, s0_ref, t_ref = refs[6 + k:9 + k]
        state = refs[9 + 2 * k]
        before, after = _comm_phases(comm, refs[6:6 + k] + refs[9 + k:9 + 2 * k] + refs[10 + 2 * k:], n)
        before()

        @pl.when(pl.program_id(0) == 0)
        def _():
            state[...] = jnp.zeros_like(state)

        s0 = state[...]
        s0_ref[0] = s0
        (y, s1), t = _scan_chunk(*[ref[...] for ref in (r_ref, lw_ref, k_ref, v_ref, a_ref, b_ref)], s0,
                                 _unit_lower_inverse)
        y_ref[...] = y
        t_ref[0] = t.astype(BF16)
        state[...] = s1
        after()

    per_chunk = (n, SCAN_PAIRS, 128, 128)
    res = _pcall(
        body, name="scan_fwd", grid=(n,), in_specs=[cm(0), cm(0), cm(0), cm(2), cm(0), cm(0)] + [_HBM] * k,
        out_specs=[cm(0), st, st] + [_HBM] * k,
        out_shape=[jax.ShapeDtypeStruct((s, D), F32), jax.ShapeDtypeStruct(per_chunk, F32),
                   jax.ShapeDtypeStruct(per_chunk, BF16)] + comm.out_shape,
        scratch_shapes=[pltpu.VMEM((SCAN_PAIRS, 128, 128), F32)] + comm.sems,
        compiler_params=_cparams(("arbitrary",)),
    )(zs, lw, km, zs, aa, bb, *comm.ins)
    return res[0], res[1], res[2], res[3:]


def _scan_bwd(zs, lw, km, aa, bb, s0s, ts, dy, comm):
    s = zs.shape[0]
    n = s // SCAN_CHUNK
    cm, st = _scan_specs(True, n)
    k = comm.n

    def body(*refs):
        r_ref, lw_ref, k_ref, v_ref, a_ref, b_ref, s0_ref, t_ref, dy_ref = refs[:9]
        douts = refs[9 + k:15 + k]
        dstate = refs[15 + 2 * k]
        before, after = _comm_phases(comm, refs[9:9 + k] + refs[15 + k:15 + 2 * k] + refs[16 + 2 * k:], n)
        before()

        @pl.when(pl.program_id(0) == 0)
        def _():
            dstate[...] = jnp.zeros_like(dstate)

        t = t_ref[0].astype(F32)
        prim = [ref[...] for ref in (r_ref, lw_ref, k_ref, v_ref, a_ref, b_ref)] + [s0_ref[0]]
        _, vjp, _ = jax.vjp(lambda *p: _scan_chunk(*p, lambda nil: _known_inverse(nil, t)), *prim, has_aux=True)
        grads = vjp((dy_ref[...], dstate[...]))
        for ref, gr in zip(douts, grads[:6]):
            ref[...] = gr
        dstate[...] = grads[6]
        after()

    res = _pcall(
        body, name="scan_bwd", grid=(n,),
        in_specs=[cm(0), cm(0), cm(0), cm(2), cm(0), cm(0), st, st, cm(0)] + [_HBM] * k,
        out_specs=[cm(0)] * 6 + [_HBM] * k, out_shape=[jax.ShapeDtypeStruct((s, D), F32)] * 6 + comm.out_shape,
        scratch_shapes=[pltpu.VMEM((SCAN_PAIRS, 128, 128), F32)] + comm.sems,
        compiler_params=_cparams(("arbitrary",)),
    )(zs, lw, km, zs, aa, bb, s0s, ts, dy, *comm.ins)
    return res[:6], res[6:]


_HBM = pl.BlockSpec(memory_space=pltpu.HBM)


def _me():
    return lax.axis_index("x"), lax.axis_index("y"), lax.axis_index("c")


def _allgather8(src, name):
    def body(src_ref, out_ref, ssem, rsem, lsem):
        x, y, c = _me()
        me = 4 * x + 2 * y + c
        local = pltpu.make_async_copy(src_ref, out_ref.at[me], lsem)
        local.start()
        peers = []
        for k in range(1, 8):
            peers.append(((1 - x) if k & 4 else x, (1 - y) if k & 2 else y, (1 - c) if k & 1 else c))
        sends = []
        for k, peer in enumerate(peers):
            cp = pltpu.make_async_remote_copy(src_ref, out_ref.at[me], ssem.at[k], rsem.at[k], device_id=peer,
                                              device_id_type=MESH)
            cp.start()
            sends.append(cp)
        for k, (px, py, pc) in enumerate(peers):
            pltpu.make_async_remote_copy(src_ref, out_ref.at[4 * px + 2 * py + pc], ssem.at[k], rsem.at[k],
                                         device_id=(px, py, pc), device_id_type=MESH).wait_recv()
        for cp in sends:
            cp.wait_send()
        local.wait()

    return _pcall(
        body, name=name, in_specs=[_HBM], out_specs=_HBM, out_shape=jax.ShapeDtypeStruct((8,) + src.shape, src.dtype),
        scratch_shapes=[pltpu.SemaphoreType.DMA((7,)), pltpu.SemaphoreType.DMA((7,)), pltpu.SemaphoreType.DMA],
    )(src)


def _other_chips(x, y):
    return [(1 - x, y), (x, 1 - y), (1 - x, 1 - y)]


def _remote(src, dst, ssem, rsem, to):
    return pltpu.make_async_remote_copy(src, dst, ssem, rsem, device_id=to, device_id_type=MESH)


class _GatherWeights:
    def __init__(self, shards):
        self.ins = list(shards)
        n = self.n = len(shards)
        self.out_shape = [jax.ShapeDtypeStruct((4,) + t.shape, t.dtype) for t in shards]
        self.sems = [pltpu.SemaphoreType.DMA((6 * n,)), pltpu.SemaphoreType.DMA((6 * n,)),
                     pltpu.SemaphoreType.DMA((n,)), pltpu.SemaphoreType.DMA((n,))]

    def _copies(self, srcs, outs, sems):
        ssem, rsem, lsem, osem = sems
        x, y, c = _me()
        me = 2 * x + y
        own, ici, landed, passed, passed_in = [], [], [], [], []
        for a in range(self.n):
            h = self.ins[a].shape[0] // 2
            mine, other = pl.ds(c * h, h), pl.ds((1 - c) * h, h)
            own.append(_remote(srcs[a], outs[a].at[me], lsem.at[a], osem.at[a], (x, y, 1 - c)))
            for k, (px, py) in enumerate(_other_chips(x, y)):
                s1, r1, s2, r2 = ssem.at[6 * a + k], rsem.at[6 * a + k], ssem.at[6 * a + 3 + k], rsem.at[6 * a + 3 + k]
                got, got_sib = outs[a].at[2 * px + py, mine], outs[a].at[2 * px + py, other]
                ici.append(_remote(srcs[a].at[mine], outs[a].at[me, mine], s1, r1, (px, py, c)))
                landed.append(_remote(got, got, s1, r1, (px, py, c)))
                passed.append(_remote(got, got, s2, r2, (x, y, 1 - c)))
                passed_in.append(_remote(got_sib, got_sib, s2, r2, (x, y, 1 - c)))
        return own, ici, landed, passed, passed_in

    def first(self, srcs, outs, sems):
        own, ici, _, _, _ = self._copies(srcs, outs, sems)
        for cp in own + ici:
            cp.start()

    def mid(self, srcs, outs, sems):
        _, _, landed, passed, _ = self._copies(srcs, outs, sems)
        for arrived, onward in zip(landed, passed):
            arrived.wait_recv()
            onward.start()

    def last(self, srcs, outs, sems):
        own, ici, _, passed, passed_in = self._copies(srcs, outs, sems)
        for cp in passed_in:
            cp.wait_recv()
        for cp in ici + passed:
            cp.wait_send()
        for cp in own:
            cp.wait()


class _ScatterToChips:
    def __init__(self, parts):
        self.ins = list(parts)
        n = self.n = len(parts)
        self.out_shape = [jax.ShapeDtypeStruct(t.shape, t.dtype) for t in parts]
        self.sems = [pltpu.SemaphoreType.DMA((3 * n,)), pltpu.SemaphoreType.DMA((3 * n,)), pltpu.SemaphoreType.DMA((n,))]

    def _copies(self, srcs, outs, sems):
        ssem, rsem, lsem = sems
        x, y, c = _me()
        me = 2 * x + y
        own, out, landed = [], [], []
        for a in range(self.n):
            own.append(pltpu.make_async_copy(srcs[a].at[me], outs[a].at[me], lsem.at[a]))
            for k, (px, py) in enumerate(_other_chips(x, y)):
                dst = outs[a].at[2 * px + py]
                out.append(_remote(srcs[a].at[2 * px + py], outs[a].at[me], ssem.at[3 * a + k], rsem.at[3 * a + k],
                                   (px, py, c)))
                landed.append(_remote(dst, dst, ssem.at[3 * a + k], rsem.at[3 * a + k], (px, py, c)))
        return own, out, landed

    def first(self, srcs, outs, sems):
        own, out, _ = self._copies(srcs, outs, sems)
        for cp in own + out:
            cp.start()

    mid = None

    def last(self, srcs, outs, sems):
        own, out, landed = self._copies(srcs, outs, sems)
        for cp in landed:
            cp.wait_recv()
        for cp in own:
            cp.wait()
        for cp in out:
            cp.wait_send()


def _run_comm(comm, name):
    n = comm.n

    def body(*refs):
        srcs, outs, sems = refs[:n], refs[n:2 * n], refs[2 * n:]
        comm.first(srcs, outs, sems)
        if comm.mid is not None:
            comm.mid(srcs, outs, sems)
        comm.last(srcs, outs, sems)

    return _pcall(body, name=name, in_specs=[_HBM] * n, out_specs=[_HBM] * n, out_shape=comm.out_shape,
                  scratch_shapes=comm.sems)(*comm.ins)


class _NoComm:
    n, ins, out_shape, sems, mid = 0, [], [], [], None

    def first(self, srcs, outs, sems):
        pass

    def last(self, srcs, outs, sems):
        pass


_NOTHING = _NoComm()


class _SiblingHalves:
    mid = None

    def __init__(self, grads):
        self.ins = list(grads)
        n = self.n = len(grads)
        self.out_shape = [jax.ShapeDtypeStruct((4, t.shape[1] // 2, t.shape[2]), t.dtype) for t in grads]
        self.sems = [pltpu.SemaphoreType.DMA((n,)), pltpu.SemaphoreType.DMA((n,))]

    def _copies(self, srcs, outs, sems):
        ssem, rsem = sems
        x, y, c = _me()
        copies = []
        for a in range(self.n):
            h = self.ins[a].shape[1] // 2
            copies.append(_remote(srcs[a].at[:, pl.ds((1 - c) * h, h)], outs[a], ssem.at[a], rsem.at[a], (x, y, 1 - c)))
        return copies

    def first(self, srcs, outs, sems):
        for cp in self._copies(srcs, outs, sems):
            cp.start()

    def last(self, srcs, outs, sems):
        for cp in self._copies(srcs, outs, sems):
            cp.wait()


def _reduce_finish(reds, name):
    n = len(reds)

    def body(*refs):
        outs = refs[n:2 * n]
        ssem, rsem = refs[2 * n:]
        x, y, c = _me()
        copies = []
        for a in range(n):
            h = reds[a].shape[0] // 2
            mine = outs[a].at[pl.ds(c * h, h)]
            copies.append(_remote(mine, mine, ssem.at[a], rsem.at[a], (x, y, 1 - c)))
        for cp in copies:
            cp.start()
        for a in range(n):
            h = reds[a].shape[0] // 2
            dst = outs[a].at[pl.ds((1 - c) * h, h)]
            _remote(dst, dst, ssem.at[a], rsem.at[a], (x, y, 1 - c)).wait_recv()
        for cp in copies:
            cp.wait_send()

    return _pcall(
        body, name=name, in_specs=[_HBM] * n, out_specs=[_HBM] * n,
        out_shape=[jax.ShapeDtypeStruct(t.shape, t.dtype) for t in reds],
        input_output_aliases={a: a for a in range(n)},
        scratch_shapes=[pltpu.SemaphoreType.DMA((n,)), pltpu.SemaphoreType.DMA((n,))],
    )(*reds)


def _half_sum(fn, full, halves, out_full, out_dtype, core, name):
    p, h, c = (halves[0].shape if halves else (full[0].shape[0], full[0].shape[1] // 2, full[0].shape[2]))
    br = _div(h, max(16, (1 << 19) // (p * c)), 16)
    nb = h // br
    mine3 = pl.BlockSpec((p, br, c), lambda i, core_ref: (0, core_ref[0] * nb + i, 0))
    half3 = pl.BlockSpec((p, br, c), lambda i, core_ref: (0, i, 0))

    def body(core_ref, *refs):
        refs[-1][...] = fn(*[t[...].astype(F32) for t in refs[:-1]]).astype(out_dtype)

    if out_full:
        out_spec = pl.BlockSpec((br, c), lambda i, core_ref: (core_ref[0] * nb + i, 0))
        out_shape = jax.ShapeDtypeStruct((2 * h, c), out_dtype)
    else:
        out_spec, out_shape = half3, jax.ShapeDtypeStruct((p, h, c), out_dtype)
    return _pcall(
        body, name=name,
        grid_spec=pltpu.PrefetchScalarGridSpec(
            num_scalar_prefetch=1, grid=(nb,), in_specs=[mine3] * len(full) + [half3] * len(halves),
            out_specs=out_spec),
        out_shape=out_shape, compiler_params=_cparams(("parallel",)),
    )(core, *full, *halves)


def _ada_fwd(c_all, w, b):
    def body(c_ref, w_ref, b_ref, o_ref):
        o_ref[...] = jnp.dot(c_ref[...], w_ref[...], precision=HI, preferred_element_type=F32) + b_ref[...]

    return _pcall(body, name="ada_fwd", out_shape=jax.ShapeDtypeStruct((c_all.shape[0], w.shape[1]), F32),
                  compiler_params=pltpu.CompilerParams(vmem_limit_bytes=VMEM_LIMIT))(c_all, w, b)


def _ada_bwd(c_all_t, d):
    def body(c_ref, d_ref, o_ref):
        o_ref[...] = jnp.dot(c_ref[...], d_ref[...], precision=HI, preferred_element_type=F32)

    return _pcall(body, name="ada_bwd", out_shape=jax.ShapeDtypeStruct((c_all_t.shape[0], d.shape[1]), F32),
                  compiler_params=pltpu.CompilerParams(vmem_limit_bytes=VMEM_LIMIT))(c_all_t, d)


def _sum_lead(x, name):
    p, r, n = x.shape
    br = _div(r, 512, 8)

    def body(x_ref, o_ref):
        acc = x_ref[0]
        for j in range(1, p):
            acc = acc + x_ref[j]
        o_ref[...] = acc

    return _pcall(
        body, name=name, grid=(r // br,), in_specs=[pl.BlockSpec((p, br, n), lambda i: (0, i, 0))],
        out_specs=pl.BlockSpec((br, n), lambda i: (i, 0)), out_shape=jax.ShapeDtypeStruct((r, n), F32),
        compiler_params=_cparams(("parallel",)),
    )(x)


def _adamw(w, g, m, v, name):
    shape = w.shape
    cols = shape[-1]
    w2, g2, m2, v2 = [t.reshape(-1, cols) for t in (w, g, m, v)]
    rows = w2.shape[0]
    pref = max(8, (1 << 19) // cols // 8 * 8)
    br = _div(rows, pref, 8)
    if rows // br > 64:
        br = pref
    outs = _rows_fwd(_f_adamw, [(t, cols, 0) for t in (w2, g2, m2, v2)], [], [(cols, F32)] * 3, name=name, br=br)
    return [o.reshape(shape) for o in outs]


_BIG = (("w_in", 1), ("w_up", 1), ("w_down", 0), ("w_o", 0), ("w_rwkv_out", 0), ("w_att_out", 1), ("w2", 1), ("a2", 1),
        ("g2", 1))


_NEEDED_FIRST = ("w_in", "w_att_out", "w2", "a2", "g2")
_NEEDED_LATER = ("w_up", "w_down", "w_o", "w_rwkv_out")
_DONE_EARLY = ("w_up", "w_down", "w_o", "w_rwkv_out", "w_att_out")
_DONE_LATE = ("w_in", "w2", "a2", "g2")


def _cols_joined(t):
    return jnp.concatenate([t[j] for j in range(4)], axis=1)


def _cols_split(t):
    n = t.shape[1] // 4
    return jnp.stack([t[:, j * n:(j + 1) * n] for j in range(4)])


W_IN_SHARD = (N_ATT + N_RW + N_GATE) // 4
W_IN_PAD = 2560


def _row_window(parts, lo, hi):
    out, pos = [], 0
    for t, w in parts:
        a, b = max(lo, pos), min(hi, pos + w)
        if a < b:
            out.append(t[a - pos:b - pos])
        pos += w
    return out[0] if len(out) == 1 else jnp.concatenate(out, axis=0)


def _rows_joined(t):
    return t.reshape(4 * t.shape[1], t.shape[2])


def _rows_split(t):
    return t.reshape(4, t.shape[0] // 4, t.shape[1])


def _step_to_scan(x, tgt, ada, wts):
    sh1, sc1, gt1, sh2, sc2, gt2 = ada
    br = 256
    grp = lax.broadcasted_iota(jnp.int32, (D, 128), 0) // 64 == lax.broadcasted_iota(jnp.int32, (D, 128), 1)
    e = grp.astype(F32)
    et = e.T
    w_in = [(wts["w_in"][j], W_IN_SHARD) for j in range(4)]
    w_att = _row_window(w_in, 0, N_ATT)
    w_rw = jnp.concatenate([_row_window(w_in, N_ATT, N_ATT + N_RW), jnp.zeros((N_RWP - N_RW, D), BF16)], axis=0)
    w_gate = _row_window(w_in, N_ATT + N_RW, N_ATT + N_RW + N_GATE)
    mu = jnp.pad(wts["mu_shift"], ((0, 0), (0, N_RWP - N_RW)))
    wl = jnp.zeros((N_LORA, 3 * D), F32)
    wl = wl.at[0:64, 0:D].set(_cols_joined(wts["w2"]).astype(F32))
    wl = wl.at[64:128, D:2 * D].set(_cols_joined(wts["a2"]).astype(F32))
    wl = wl.at[128:288, 2 * D:3 * D].set(_cols_joined(wts["g2"]).astype(F32))
    pre1_c = [wts["norm1_w"], sc1, sh1]
    (h1,) = _rows_fwd(_f_pre, [(x, D, 0)], pre1_c, [(D, BF16), None], name="pre1_fwd", br=2 * br)
    att_in = _mm(h1, w_att, tb=True, name="mm_att_in")
    z = _mm(h1, w_rw, tb=True, name="mm_rw_in")
    gate_in = _mm(h1, w_gate, tb=True, name="mm_gate_in")
    att_o, att_l = [], []
    for g, (_, dil) in enumerate(ATT_PATTERNS):
        o, l = _att_fwd(att_in, g, dil)
        att_o.append(o)
        att_l.append(l)
    comb_rows = [(t, ATT_WIDTH, 0) for t in att_o + att_l]
    (att,) = _rows_fwd(_f_comb, comb_rows, [], [(ATT_WIDTH, BF16)], name="comb_fwd", br=2 * br)
    y_att = _mm(att, wts["w_att_out"], b_chip=True, name="mm_att_out")
    rwpre_c = [wts["w0"], wts["a0"], wts["k_k"], wts["k_a"], wl, e, et]

    def shift_and_rwpre(zz, *rest):
        consts, mu_row, before = rest[:-2], rest[-2], rest[-1]
        last = jnp.sum(jnp.where(lax.broadcasted_iota(jnp.int32, before.shape, 0) == 7, before, 0.0), axis=0,
                       keepdims=True)
        row = lax.broadcasted_iota(jnp.int32, zz.shape, 0)
        zprev = jnp.where(row == 0, last, pltpu.roll(zz, 1, 0))
        shifted = zz + (zprev - zz) * mu_row
        return (shifted,) + tuple(_f_rwpre(shifted, *consts))

    zs, lw, km, aa, bb, gg = _rows_fwd(
        shift_and_rwpre, [(z, N_RWP, 0)], rwpre_c + [mu],
        [(N_RWP, F32), None, (D, F32), (D, F32), None, (D, F32), (D, F32), (D, F32)], name="rwpre_fwd", br=br, halo=0)
    return dict(x=x, tgt=tgt, wts=wts, br=br, e=e, et=et, gt1=gt1, sc2=sc2, sh2=sh2, gt2=gt2, w_att=w_att, w_rw=w_rw,
                w_gate=w_gate, mu=mu, pre1_c=pre1_c, h1=h1, att_in=att_in, z=z, gate_in=gate_in, comb_rows=comb_rows,
                att=att, y_att=y_att, zs=zs, rwpre_c=rwpre_c, lw=lw, km=km, aa=aa, bb=bb, gg=gg)


def _step_between_scans(st, y_raw, late):
    x, tgt, wts, br, e, et = st["x"], st["tgt"], st["wts"], st["br"], st["e"], st["et"]
    zs, km, gg, gate_in, y_att, att = st["zs"], st["km"], st["gg"], st["gate_in"], st["y_att"], st["att"]
    comb_rows, att_in = st["comb_rows"], st["att_in"]
    gt1, sc2, sh2, gt2 = st["gt1"], st["sc2"], st["sh2"], st["gt2"]
    w_up, w_ao = late["w_up"], wts["w_att_out"]
    w_down, w_o, w_ro = _rows_joined(late["w_down"]), _rows_joined(late["w_o"]), _rows_joined(late["w_rwkv_out"])
    post_rows = [(y_raw, D, 0), (zs, D, 0), (zs, D, 2), (km, D, 0), (gg, D, 0)]
    post_c = [wts["lnx_w"], wts["lnx_b"], wts["r_k"], e, et]
    (rw_out,) = _rows_fwd(_f_rwpost, post_rows, post_c, [(D, BF16)], name="rwpost_fwd", br=br)
    y_rw = _mm(rw_out, w_ro, name="mm_rw_out")
    mix_rows = [(gate_in, N_GATE, 0), (y_att, D, 0), (y_rw, D, 0)]
    (mix,) = _rows_fwd(_f_mix, mix_rows, [wts["b_gate"]], [(D, BF16)], name="mix_fwd", br=2 * br)
    o = _mm(mix, w_o, name="mm_o")
    pre2_c = [gt1, wts["norm2_w"], sc2, sh2]
    x1, h2 = _rows_fwd(_f_pre2, [(x, D, 0), (o, D, 0)], pre2_c, [(D, F32), (D, BF16)], name="pre2_fwd", br=2 * br)
    u = _mm(h2, w_up, b_chip=True, name="mm_up")
    act = _conv_fwd(u, wts["conv_w"], wts["conv_b"])
    f = _mm(act, w_down, name="mm_down")
    fin_rows = [(x1, D, 0), (f, D, 0), (tgt, D, 0)]
    fin_c = [gt2, wts["norm_f_w"]]

    def fin_fwd(*a):
        (l,) = _f_fin(*a)
        return (jnp.broadcast_to(jnp.sum(l, axis=0, keepdims=True), (8, 128)),)

    (loss_acc,) = _rows_fwd(fin_fwd, fin_rows, fin_c, [], name="fin_fwd", br=2 * br, acc_shape=(8, 128))

    gw = {}
    dx1a, df, d_gt2, gw["norm_f_w"] = _rows_bwd(
        _f_fin, fin_rows, fin_c, [[]], wrt_rows=[0, 1], wrt_consts=[0, 1], drow_dtypes=[F32, BF16],
        name="fin_bwd", br=2 * br, unit_cot=True)
    dact = _mm(df, w_down, tb=True, name="mm_dact")
    gw["w_down"] = _rows_split(_mm(act, df, ta=True, out_dtype=BF16, name="mm_dw_down"))
    du, gw["conv_w"], gw["conv_b"] = _conv_bwd(u, wts["conv_w"], wts["conv_b"], dact)
    dh2 = _mm(du, w_up, tb=True, b_chip=True, name="mm_dh2")
    gw["w_up"] = _mm(h2, du, ta=True, out_chip=True, out_dtype=BF16, name="mm_dw_up")
    dxa, do, d_gt1, gw["norm2_w"], d_sc2, d_sh2 = _rows_bwd(
        _f_pre2, [(x, D, 0), (o, D, 0)], pre2_c, [[(dx1a, D, 0)], [(dh2, D, 0)]], wrt_rows=[0, 1],
        wrt_consts=[0, 1, 2, 3], drow_dtypes=[F32, BF16], name="pre2_bwd", br=2 * br)
    dmix = _mm(do, w_o, tb=True, name="mm_dmix")
    gw["w_o"] = _rows_split(_mm(mix, do, ta=True, out_dtype=BF16, name="mm_dw_o"))
    dgate, dya, dyr, gw["b_gate"] = _rows_bwd(
        _f_mix, mix_rows, [wts["b_gate"]], [[(dmix, D, 0)]], wrt_rows=[0, 1, 2], wrt_consts=[0],
        drow_dtypes=[BF16] * 3, name="mix_bwd", br=2 * br)
    datt = _mm(dya, w_ao, tb=True, b_chip=True, name="mm_datt")
    gw["w_att_out"] = _mm(att, dya, ta=True, out_chip=True, out_dtype=BF16, name="mm_dw_att_out")
    drw = _mm(dyr, w_ro, tb=True, name="mm_drw")
    gw["w_rwkv_out"] = _rows_split(_mm(rw_out, dyr, ta=True, out_dtype=BF16, name="mm_dw_rw_out"))
    dcomb = _rows_bwd(_f_comb, comb_rows, [], [[(datt, ATT_WIDTH, 0)]], wrt_rows=list(range(6)), wrt_consts=[],
                      drow_dtypes=[F32] * 6, name="comb_bwd", br=2 * br)
    datt_in = None
    for g, (_, dil) in enumerate(ATT_PATTERNS):
        datt_in = _att_bwd(att_in, g, dil, dcomb[g], dcomb[3 + g], datt_in)
    dy_raw, dr_p, dv_p, dkm_p, dgg, gw["lnx_w"], gw["lnx_b"], gw["r_k"], *recv_early = _rows_bwd(
        _f_rwpost, post_rows, post_c, [[(drw, D, 0)]], wrt_rows=[0, 1, 2, 3, 4], wrt_consts=[0, 1, 2],
        drow_dtypes=[F32] * 5, name="rwpost_bwd", br=br, comm=_SiblingHalves([gw[n] for n in _DONE_EARLY]))
    st.update(loss=loss_acc[0, 0], gw=gw, dxa=dxa, dgate=dgate, datt_in=datt_in,
              dy_raw=dy_raw, dr_p=dr_p, dv_p=dv_p, dkm_p=dkm_p, dgg=dgg, d_ada_late=(d_gt1, d_sh2, d_sc2, d_gt2),
              recv_early=recv_early)
    return st


def _chip_parts(grads, recv, names, core):
    return [_half_sum(lambda a, b: a + b, [g], [r], False, BF16, core, "reduce_add2_" + n)
            for g, r, n in zip(grads, recv, names)]


def _step_after_scan(st, scan_grads, core):
    x, br, gw, h1, zs = st["x"], st["br"], st["gw"], st["h1"], st["zs"]
    dr_s, dlw, dkm_s, dv_s, daa, dbb = scan_grads
    pre_cots = [[(st["dr_p"], D, 0), (dr_s, D, 0)], [(dlw, D, 0)], [(st["dkm_p"], D, 0), (dkm_s, D, 0)],
                [(st["dv_p"], D, 0), (dv_s, D, 0)], [(daa, D, 0)], [(dbb, D, 0)], [(st["dgg"], D, 0)]]
    dzs, gw["w0"], gw["a0"], gw["k_k"], gw["k_a"], dwl = _rows_bwd(
        _f_rwpre, [(zs, N_RWP, 0)], st["rwpre_c"], pre_cots, wrt_rows=[0], wrt_consts=[0, 1, 2, 3, 4],
        drow_dtypes=[F32], name="rwpre_bwd", br=128)
    gw["w2"], gw["a2"] = _cols_split(dwl[0:64, 0:D]), _cols_split(dwl[64:128, D:2 * D])
    gw["g2"] = _cols_split(dwl[128:288, 2 * D:3 * D])
    dz, dmu = _shift_bwd(st["z"], st["mu"], dzs)
    gw["mu_shift"] = dmu[:, :N_RW]
    datt_in, dgate = st["datt_in"], st["dgate"]
    dw_in = [(_mm(datt_in, h1, ta=True, out_dtype=BF16, name="mm_dw_att"), N_ATT),
             (_mm(dz, h1, ta=True, out_dtype=BF16, name="mm_dw_rw"), N_RW),
             (_mm(dgate, h1, ta=True, out_dtype=BF16, name="mm_dw_gate"), N_GATE)]
    slabs = []
    for j in range(4):
        slabs += [_row_window(dw_in, j * W_IN_SHARD, (j + 1) * W_IN_SHARD), jnp.zeros((W_IN_PAD - W_IN_SHARD, D), BF16)]
    gw["w_in"] = jnp.concatenate(slabs, axis=0).reshape(4, W_IN_PAD, D)
    late = [gw[n] for n in _DONE_LATE]
    parts = _chip_parts(late, _run_comm(_SiblingHalves(late), "reduce_sib_late"), _DONE_LATE, core)
    dh1, slots_late = _mm_sum([(datt_in, st["w_att"]), (dz, st["w_rw"]), (dgate, st["w_gate"])],
                              comm=_ScatterToChips(parts), name="mm_dh1")
    grad_x, gw["norm1_w"], d_sc1, d_sh1 = _rows_bwd(
        _f_pre, [(x, D, 0)], st["pre1_c"], [[(dh1, D, 0)], [(st["dxa"], D, 0)]], wrt_rows=[0], wrt_consts=[0, 1, 2],
        drow_dtypes=[F32], name="pre1_bwd", br=2 * br)
    d_gt1, d_sh2, d_sc2, d_gt2 = st["d_ada_late"]
    return st["loss"], grad_x, (d_sh1, d_sc1, d_gt1, d_sh2, d_sc2, d_gt2), gw, slots_late


_SMALL = ("b_ada", "norm1_w", "b_gate", "mu_shift", "w0", "a0", "k_k", "k_a", "r_k", "lnx_w", "lnx_b", "norm2_w",
          "conv_b", "norm_f_w")
_NAMES = ("w_ada", "b_ada", "norm1_w", "w_in", "b_gate", "mu_shift", "w0", "w2", "a0", "a2", "g2", "k_k", "k_a", "r_k",
          "lnx_w", "lnx_b", "w_att_out", "w_rwkv_out", "w_o", "norm2_w", "w_up", "conv_w", "conv_b", "w_down",
          "norm_f_w")


def kernel(x, c, w_ada, b_ada, norm1_w, w_in, b_gate, mu_shift, w0, w2, a0, a2, g2, k_k, k_a, r_k, lnx_w, lnx_b, w_att_out, w_rwkv_out, w_o, norm2_w, w_up, conv_w, conv_b, w_down, norm_f_w, loss_target, m_w_ada, m_b_ada, m_norm1_w, m_w_in, m_b_gate, m_mu_shift, m_w0, m_w2, m_a0, m_a2, m_g2, m_k_k, m_k_a, m_r_k, m_lnx_w, m_lnx_b, m_w_att_out, m_w_rwkv_out, m_w_o, m_norm2_w, m_w_up, m_conv_w, m_conv_b, m_w_down, m_norm_f_w, v_w_ada, v_b_ada, v_norm1_w, v_w_in, v_b_gate, v_mu_shift, v_w0, v_w2, v_a0, v_a2, v_g2, v_k_k, v_k_a, v_r_k, v_lnx_w, v_lnx_b, v_w_att_out, v_w_rwkv_out, v_w_o, v_norm2_w, v_w_up, v_conv_w, v_conv_b, v_w_down, v_norm_f_w):
    args = dict(locals())
    p, pm, pv = {}, {}, {}
    for name in _NAMES:
        for dst, key in ((p, name), (pm, "m_" + name), (pv, "v_" + name)):
            t = args[key]
            if name == "w_in":
                dst[name] = jnp.swapaxes(t, 1, 2)[0]
            else:
                dst[name] = t.reshape(1, -1) if name in ("r_k", "norm_f_w") else t.reshape(t.shape[-2], t.shape[-1])
    xi, yi, ci = _me()
    chip = 2 * xi + yi
    dev = 4 * xi + 2 * yi + ci
    x2, tgt = x[0], loss_target[0]

    n_cw = 3 * (2 * D_FF // 4)
    vec = jnp.concatenate([c.reshape(-1), p["conv_w"].reshape(-1), jnp.zeros((8 * D - D - n_cw,), F32)]).reshape(8, D)
    g0 = _allgather8(vec, "gather_c").reshape(8, 8 * D)
    c_all = g0[:, :D]
    conv_w_full = jnp.concatenate([g0[2 * j, D:D + n_cw].reshape(3, -1) for j in range(4)], axis=1)
    n_ada = 6 * D // 4
    b_ada_sh = lax.dynamic_slice(p["b_ada"], (0, chip * n_ada), (1, n_ada))
    ada_sh = _ada_fwd(c_all, p["w_ada"], b_ada_sh)
    ga = _allgather8(ada_sh, "gather_ada")
    ada_all = jnp.concatenate([ga[2 * j] for j in range(4)], axis=1)
    ada_row = lax.dynamic_slice(ada_all, (dev, 0), (1, 6 * D))
    ada = [ada_row[:, j * D:(j + 1) * D] for j in range(6)]

    big = [n for n, _ in _BIG]
    shard = {n: p[n].astype(BF16) for n in big}
    shard["w_in"] = jnp.pad(shard["w_in"], ((0, W_IN_PAD - W_IN_SHARD), (0, 0)))
    wts = dict(zip(_NEEDED_FIRST, _run_comm(_GatherWeights([shard[n] for n in _NEEDED_FIRST]), "gather_w")))
    for n in _SMALL:
        wts[n] = p[n]
    wts["conv_w"] = conv_w_full
    core = ci.reshape(1).astype(jnp.int32)

    st = _step_to_scan(x2, tgt, ada, wts)
    y_raw, s0s, inverses, late = _scan_fwd(st["zs"], st["lw"], st["km"], st["aa"], st["bb"],
                                           _GatherWeights([shard[n] for n in _NEEDED_LATER]))
    st = _step_between_scans(st, y_raw, dict(zip(_NEEDED_LATER, late)))
    early = _chip_parts([st["gw"][n] for n in _DONE_EARLY], st["recv_early"], _DONE_EARLY, core)
    scan_grads, slots_early = _scan_bwd(st["zs"], st["lw"], st["km"], st["aa"], st["bb"], s0s, inverses,
                                        st["dy_raw"], _ScatterToChips(early))
    loss_part, grad_x, d_ada, gw, slots_late = _step_after_scan(st, scan_grads, core)

    small = [jnp.concatenate(d_ada, axis=1)] + [gw[n] for n in _SMALL[1:]] + [gw["conv_w"], loss_part.reshape(1, 1)]
    sizes = [t.size for t in small]
    flat = jnp.concatenate([t.reshape(-1) for t in small])
    npad = (-flat.shape[0]) % (8 * D)
    srows = (flat.shape[0] + npad) // D
    flat = jnp.concatenate([flat, jnp.zeros((npad,), F32)]).reshape(srows, D)
    parts = _allgather8(flat, "gather_small")
    tot = _sum_lead(parts, "sum_small").reshape(-1)
    pieces, pos = [], 0
    for sz in sizes:
        pieces.append(tot[pos:pos + sz])
        pos += sz
    grads = {}
    for n, piece in zip(_SMALL, pieces[:len(_SMALL)]):
        grads[n] = piece.reshape(p[n].shape)
    conv_w_grad = pieces[len(_SMALL)].reshape(3, 2 * D_FF)
    grads["conv_w"] = lax.dynamic_slice(conv_w_grad, (0, chip * (n_cw // 3)), (3, n_cw // 3))
    loss = pieces[-1][0]
    d_ada_all = parts[:, :6].reshape(8, 6 * D)
    grads["w_ada"] = _ada_bwd(c_all.T, lax.dynamic_slice(d_ada_all, (0, chip * n_ada), (8, n_ada)))

    order = _DONE_EARLY + _DONE_LATE
    reds = [_half_sum(lambda t: t[0] + t[1] + t[2] + t[3], [], [t], True, F32, core, "reduce_add4_" + n)
            for n, t in zip(order, list(slots_early) + list(slots_late))]
    for n, g in zip(order, _reduce_finish(reds, "reduce_sib2")):
        grads[n] = g

    outs_g, outs_d, outs_m, outs_v = [], [], [], []
    grads["w_in"] = grads["w_in"][:W_IN_SHARD]
    for name in _NAMES:
        g = grads[name]
        d, m, v = _adamw(p[name], g, pm[name], pv[name], "adamw_" + name)
        shape = args[name].shape
        for outs, t in ((outs_g, g), (outs_d, d), (outs_m, m), (outs_v, v)):
            outs.append(jnp.swapaxes(t[None], 1, 2) if name == "w_in" else t.reshape(shape))
    return (loss, grad_x.reshape(x.shape), *outs_g, *outs_d, *outs_m, *outs_v)
```

```python
import functools

import jax
import jax.numpy as jnp
from jax import lax
from jax.experimental import pallas as pl
from jax.experimental.pallas import tpu as pltpu

F32 = jnp.float32
BF16 = jnp.bfloat16
HI = lax.Precision.HIGHEST
MESH = pl.DeviceIdType.MESH

D = 1024
ATT_PATTERNS = ((128, 1), (512, 4), (2048, 16))
ATT_BLOCK = 128
ATT_WIDTH = 512
N_ATT = 3 * 3 * ATT_WIDTH
N_RW = 3 * D + 64 + 64 + 160
N_RWP = 3456
N_LORA = N_RWP - 3 * D
N_GATE = 2 * D
D_FF = 2816
RMS_EPS = 1e-6
GN_EPS = 64e-5
SCAN_CHUNK = 64
SCAN_PAIRS = 8
NEG = -1e30
VMEM_LIMIT = 48 * 1024 * 1024
HALO = 16

ADAM_LR, ADAM_B1, ADAM_B2, ADAM_EPS, ADAM_WD, ADAM_STEP = 0.001, 0.9, 0.999, 1e-08, 0.01, 10


def _pcall(body, **kw):
    return pl.pallas_call(body, **kw)


def _cparams(sem):
    return pltpu.CompilerParams(dimension_semantics=sem, vmem_limit_bytes=VMEM_LIMIT)


def _div(n, pref, mult):
    best = None
    d = mult
    while d <= min(n, pref):
        if n % d == 0:
            best = d
        d += mult
    return best if best else n


def _dg(a, b, ca, cb):
    return lax.dot_general(a.astype(BF16), b.astype(BF16), (((ca,), (cb,)), ((), ())), preferred_element_type=F32)


@jax.custom_vjp
def _nn(a, b):
    return _dg(a, b, 1, 0)


@jax.custom_vjp
def _nt(a, b):
    return _dg(a, b, 1, 1)


@jax.custom_vjp
def _tn(a, b):
    return _dg(a, b, 0, 0)


_nn.defvjp(lambda a, b: (_nn(a, b), (a, b)), lambda res, g: (_nt(g, res[1]), _tn(res[0], g)))
_nt.defvjp(lambda a, b: (_nt(a, b), (a, b)), lambda res, g: (_nn(g, res[1]), _tn(g, res[0])))
_tn.defvjp(lambda a, b: (_tn(a, b), (a, b)), lambda res, g: (_nt(res[1], g), _nn(res[0], g)))


def _bdg(a, b, ca, cb):
    return lax.dot_general(a.astype(BF16), b.astype(BF16), (((ca,), (cb,)), ((0,), (0,))), preferred_element_type=F32)


@jax.custom_vjp
def _bnn(a, b):
    return _bdg(a, b, 2, 1)


@jax.custom_vjp
def _bnt(a, b):
    return _bdg(a, b, 2, 2)


@jax.custom_vjp
def _btn(a, b):
    return _bdg(a, b, 1, 1)


_bnn.defvjp(lambda a, b: (_bnn(a, b), (a, b)), lambda res, g: (_bnt(g, res[1]), _btn(res[0], g)))
_bnt.defvjp(lambda a, b: (_bnt(a, b), (a, b)), lambda res, g: (_bnn(g, res[1]), _btn(g, res[0])))
_btn.defvjp(lambda a, b: (_btn(a, b), (a, b)), lambda res, g: (_bnt(res[1], g), _bnn(res[0], g)))


def _split2(x):
    hi = x.astype(BF16)
    lo = (x - hi.astype(F32)).astype(BF16)
    return hi, lo


def _hsum_impl(x, e, et):
    eb, etb = e.astype(BF16), et.astype(BF16)
    s = jnp.dot(x.astype(BF16), eb, preferred_element_type=F32)
    shi, slo = _split2(s)
    return jnp.dot(shi, etb, preferred_element_type=F32) + jnp.dot(slo, etb, preferred_element_type=F32)


@jax.custom_vjp
def _hsum(x, e, et):
    return _hsum_impl(x, e, et)


_hsum.defvjp(lambda x, e, et: (_hsum_impl(x, e, et), (e, et)),
             lambda res, g: (_hsum_impl(g, res[0], res[1]), jnp.zeros_like(res[0]), jnp.zeros_like(res[1])))


def _mm(a, b, *, ta=False, tb=False, out_dtype=F32, add=None, b_chip=False, out_chip=False, comm=None, name):
    riding = _NOTHING if comm is None else comm
    nc = riding.n
    if ta:
        kdim, m = a.shape
    else:
        m, kdim = a.shape
    if b_chip:
        n = b.shape[1] if tb else 4 * b.shape[2]
    else:
        n = b.shape[0] if tb else b.shape[1]
    tm, tn, tk = _div(m, 1536, 128), _div(n, 1536, 128), _div(kdim, 1408, 128)
    if b_chip and tb:
        tk = kdim // 4
    if (b_chip and not tb) or out_chip:
        tn = n // 4
    nk = kdim // tk
    ca, cb = (0 if ta else 1), (1 if tb else 0)

    nin = 2 if add is None else 3
    gi, gj = m // tm, n // tn

    def body(*refs):
        a_ref, b_ref = refs[0], refs[1]
        add_ref = None if add is None else refs[2]
        o_ref = refs[nin + nc]
        step = (pl.program_id(0) * gj + pl.program_id(1)) * nk + pl.program_id(2)
        before, after = _comm_phases(riding, refs[nin:nin + nc] + refs[nin + nc + 1:nin + 2 * nc + 1]
                                     + refs[nin + 2 * nc + 1 + (nk > 1):], gi * gj * nk, step)
        before()
        part = lax.dot_general(a_ref[...], b_ref[...], (((ca,), (cb,)), ((), ())), preferred_element_type=F32)

        def finish(r):
            if add_ref is not None:
                r = r + add_ref[...]
            o_ref[...] = r.astype(o_ref.dtype)

        if nk == 1:
            finish(part)
            after()
            return
        acc = refs[nin + 2 * nc + 1]
        k = pl.program_id(2)

        @pl.when(k == 0)
        def _():
            acc[...] = part

        @pl.when(k > 0)
        def _():
            acc[...] += part

        @pl.when(k == nk - 1)
        def _():
            finish(acc[...])

        after()

    a_spec = pl.BlockSpec((tk, tm), lambda i, j, k: (k, i)) if ta else pl.BlockSpec((tm, tk), lambda i, j, k: (i, k))
    if b_chip:
        b_spec = (pl.BlockSpec((None, tn, tk), lambda i, j, k: (k, j, 0)) if tb
                  else pl.BlockSpec((None, tk, tn), lambda i, j, k: (j, k, 0)))
    else:
        b_spec = pl.BlockSpec((tn, tk), lambda i, j, k: (j, k)) if tb else pl.BlockSpec((tk, tn), lambda i, j, k: (k, j))
    in_specs = [a_spec, b_spec]
    args = [a, b]
    if add is not None:
        in_specs.append(pl.BlockSpec((tm, tn), lambda i, j, k: (i, j)))
        args.append(add)
    if out_chip:
        out_spec = pl.BlockSpec((None, tm, tn), lambda i, j, k: (j, i, 0))
        out_shape = jax.ShapeDtypeStruct((4, m, tn), out_dtype)
    else:
        out_spec = pl.BlockSpec((tm, tn), lambda i, j, k: (i, j))
        out_shape = jax.ShapeDtypeStruct((m, n), out_dtype)
    res = _pcall(
        body, name=name, grid=(gi, gj, nk), in_specs=in_specs + [_HBM] * nc, out_specs=[out_spec] + [_HBM] * nc,
        out_shape=[out_shape] + riding.out_shape,
        scratch_shapes=([] if nk == 1 else [pltpu.VMEM((tm, tn), F32)]) + riding.sems,
        compiler_params=_cparams(("arbitrary",) * 3 if nc else ("parallel", "parallel", "arbitrary")),
    )(*args, *riding.ins)
    return res[0] if comm is None else (res[0], res[1:])


def _mm_sum(pairs, *, comm, name):
    m, n = pairs[0][0].shape[0], pairs[0][1].shape[1]
    tm, tn = _div(m, 1024, 128), _div(n, 1024, 128)
    tks = [_div(a.shape[1], 1408, 128) for a, _ in pairs]
    nks = [a.shape[1] // tk for (a, _), tk in zip(pairs, tks)]
    offs = [sum(nks[:p]) for p in range(len(pairs))]
    total, npair, nc = sum(nks), len(pairs), comm.n
    gi, gj = m // tm, n // tn

    def body(*refs):
        o_ref, acc = refs[2 * npair + nc], refs[2 * npair + 2 * nc + 1]
        k = pl.program_id(2)
        step = (pl.program_id(0) * gj + pl.program_id(1)) * total + k
        before, after = _comm_phases(comm, refs[2 * npair:2 * npair + nc]
                                     + refs[2 * npair + nc + 1:2 * npair + 2 * nc + 1]
                                     + refs[2 * npair + 2 * nc + 2:], gi * gj * total, step)
        before()
        for p in range(npair):
            def partial_product(p=p):
                part = jnp.dot(refs[2 * p][...], refs[2 * p + 1][...], preferred_element_type=F32)
                if p == 0:
                    @pl.when(k == 0)
                    def _():
                        acc[...] = part

                    @pl.when(k > 0)
                    def _():
                        acc[...] += part
                else:
                    acc[...] += part

            pl.when(jnp.logical_and(k >= offs[p], k < offs[p] + nks[p]))(partial_product)

        @pl.when(k == total - 1)
        def _():
            o_ref[...] = acc[...].astype(o_ref.dtype)

        after()

    def specs(tk, off, nk):
        def kb(k):
            return jnp.clip(k - off, 0, nk - 1)
        return [pl.BlockSpec((tm, tk), lambda i, j, k: (i, kb(k))), pl.BlockSpec((tk, tn), lambda i, j, k: (kb(k), j))]

    in_specs, args = [], []
    for (a, b), tk, off, nk in zip(pairs, tks, offs, nks):
        in_specs += specs(tk, off, nk)
        args += [a, b]
    res = _pcall(
        body, name=name, grid=(gi, gj, total), in_specs=in_specs + [_HBM] * nc,
        out_specs=[pl.BlockSpec((tm, tn), lambda i, j, k: (i, j))] + [_HBM] * nc,
        out_shape=[jax.ShapeDtypeStruct((m, n), BF16)] + comm.out_shape,
        scratch_shapes=[pltpu.VMEM((tm, tn), F32)] + comm.sems,
        compiler_params=_cparams(("arbitrary",) * 3),
    )(*args, *comm.ins)
    return res[0], res[1:]


def _row_spec(br, w, cb):
    return pl.BlockSpec((br, w), lambda i: (i, cb))


def _const_spec(shape):
    return pl.BlockSpec(shape, lambda i: (0,) * len(shape))


def _rows_fwd(fn, rows, consts, outs, *, name, br, acc_shape=None, halo=None):
    s = rows[0][0].shape[0]
    nr, nc = len(rows), len(consts)
    kept = [k for k, o in enumerate(outs) if o is not None]

    def body(*refs):
        xs = [r[...].astype(F32) for r in refs[:nr]]
        cs = [c[...] for c in refs[nr:nr + nc]]
        if halo is not None:
            cs.append(jnp.where(pl.program_id(0) == 0, 0.0, refs[nr + nc][...].astype(F32)))
        res = fn(*xs, *cs)
        orefs = refs[nr + nc + (halo is not None):]
        for j, k in enumerate(kept):
            orefs[j][...] = res[k].astype(orefs[j].dtype)
        if acc_shape is not None:
            acc_ref = orefs[len(kept)]

            @pl.when(pl.program_id(0) == 0)
            def _():
                acc_ref[...] = jnp.zeros_like(acc_ref)

            acc_ref[...] += res[len(outs)]

    in_specs = [_row_spec(br, w, cb) for (_, w, cb) in rows] + [_const_spec(c.shape) for c in consts]
    args = [r[0] for r in rows] + list(consts)
    if halo is not None:
        harr, hw, hcb = rows[halo]
        in_specs.append(pl.BlockSpec((HALO, hw), lambda i: (jnp.maximum(i * (br // HALO) - 1, 0), hcb)))
        args.append(harr)
    out_specs = [_row_spec(br, outs[k][0], 0) for k in kept]
    out_shape = [jax.ShapeDtypeStruct((s, outs[k][0]), outs[k][1]) for k in kept]
    if acc_shape is not None:
        out_specs.append(_const_spec(acc_shape))
        out_shape.append(jax.ShapeDtypeStruct(acc_shape, F32))
    return _pcall(
        body, name=name, grid=(pl.cdiv(s, br),), in_specs=in_specs, out_specs=out_specs, out_shape=out_shape,
        compiler_params=_cparams(("arbitrary",)),
    )(*args)


def _rows_bwd(fn, rows, consts, cots, *, wrt_rows, wrt_consts, drow_dtypes, name, br, unit_cot=False, comm=None):
    comm = _NOTHING if comm is None else comm
    ncomm = comm.n
    nout = len(wrt_rows) + len(wrt_consts)
    s = rows[0][0].shape[0]
    nr, nc = len(rows), len(consts)
    flat_cots = [c for lst in cots for c in lst]
    ncot = len(flat_cots)

    def body(*refs):
        xs = [r[...].astype(F32) for r in refs[:nr]]
        cs = [c[...] for c in refs[nr:nr + nc]]
        cvals = [c[...].astype(F32) for c in refs[nr + nc:nr + nc + ncot]]
        orefs = refs[nr + nc + ncot + ncomm:]
        before, after = _comm_phases(comm, refs[nr + nc + ncot:nr + nc + ncot + ncomm] + orefs[nout:], s // br)
        before()

        def g(*d):
            xs2, cs2 = list(xs), list(cs)
            for j, k in enumerate(wrt_rows):
                xs2[k] = d[j]
            for j, k in enumerate(wrt_consts):
                cs2[k] = d[len(wrt_rows) + j]
            return tuple(fn(*xs2, *cs2))

        prim = [xs[k] for k in wrt_rows] + [cs[k] for k in wrt_consts]
        outs, vjp = jax.vjp(g, *prim)
        ct = []
        pos = 0
        for o, lst in zip(outs, cots):
            if unit_cot:
                ct.append(jnp.ones_like(o))
                continue
            acc = jnp.zeros_like(o)
            for _ in lst:
                acc = acc + cvals[pos]
                pos += 1
            ct.append(acc)
        grads = vjp(tuple(ct))
        for j in range(len(wrt_rows)):
            orefs[j][...] = grads[j].astype(orefs[j].dtype)

        @pl.when(pl.program_id(0) == 0)
        def _():
            for j in range(len(wrt_consts)):
                oref = orefs[len(wrt_rows) + j]
                oref[...] = jnp.zeros_like(oref)

        for j in range(len(wrt_consts)):
            orefs[len(wrt_rows) + j][...] += grads[len(wrt_rows) + j]
        after()

    in_specs = ([_row_spec(br, w, cb) for (_, w, cb) in rows] + [_const_spec(c.shape) for c in consts]
                + [_row_spec(br, w, cb) for (_, w, cb) in flat_cots] + [_HBM] * ncomm)
    out_specs = ([_row_spec(br, rows[k][1], 0) for k in wrt_rows] + [_const_spec(consts[k].shape) for k in wrt_consts]
                 + [_HBM] * ncomm)
    out_shape = ([jax.ShapeDtypeStruct((s, rows[k][1]), dt) for k, dt in zip(wrt_rows, drow_dtypes)]
                 + [jax.ShapeDtypeStruct(consts[k].shape, F32) for k in wrt_consts] + comm.out_shape)
    return _pcall(
        body, name=name, grid=(s // br,), in_specs=in_specs, out_specs=out_specs, out_shape=out_shape,
        scratch_shapes=comm.sems, compiler_params=_cparams(("arbitrary",)),
    )(*[r[0] for r in rows], *consts, *[c[0] for c in flat_cots], *comm.ins)


def _rms(x, w):
    return x * lax.rsqrt(jnp.mean(x * x, axis=-1, keepdims=True) + RMS_EPS) * w


def _softplus(x):
    return jnp.maximum(x, 0.0) + jnp.log(1.0 + jnp.exp(-jnp.abs(x)))


def _f_pre(x, nw, sc, sh):
    return _rms(x, nw) * (1.0 + sc) + sh, x


def _f_pre2(x, o, gt, nw, sc, sh):
    x1 = x + gt * o
    return x1, _rms(x1, nw) * (1.0 + sc) + sh


def _f_fin(x1, f, tgt, gt, nfw):
    y = _rms(x1 + gt * f, nfw)
    return (0.5 * jnp.mean(jnp.square(y - tgt), axis=-1, keepdims=True),)


def _f_comb(o1, o2, o3, l1, l2, l3):
    m = lax.stop_gradient(jnp.maximum(jnp.maximum(l1, l2), l3))
    e1, e2, e3 = jnp.exp(l1 - m), jnp.exp(l2 - m), jnp.exp(l3 - m)
    return ((e1 * o1 + e2 * o2 + e3 * o3) / (e1 + e2 + e3),)


def _f_rwpre(zs, w0, a0, k_k, k_a, wl, e, et):
    r, k, v, zl = zs[:, 0:D], zs[:, D:2 * D], zs[:, 2 * D:3 * D], zs[:, 3 * D:N_RWP]
    lane = lax.broadcasted_iota(jnp.int32, zl.shape, 1)
    t = jnp.where(lane < 64, jnp.tanh(zl), jnp.where(lane < 128, zl, jnp.where(lane < 288, jax.nn.sigmoid(zl), 0.0)))
    lo = _nn(t[:, 0:128], wl[0:128, 0:2 * D])
    g = _nn(t[:, 128:N_LORA], wl[128:N_LORA, 2 * D:3 * D])
    w_log = -_softplus(-(w0 + lo[:, 0:D])) - 0.5
    lw = -jnp.exp(w_log)
    a = jax.nn.sigmoid(a0 + lo[:, D:2 * D])
    k_mod = k * (1.0 + (a - 1.0) * k_a)
    kk = k * k_k
    kk = kk / jnp.maximum(jnp.sqrt(_hsum(kk * kk, e, et)), 1e-12)
    return r, lw, k_mod, v, -kk, kk * a, g


def _f_rwpost(y, r, v, k_mod, g, lnx_w, lnx_b, r_k, e, et):
    mean = _hsum(y, e, et) * (1.0 / 64)
    yc = y - mean
    var = _hsum(yc * yc, e, et) * (1.0 / 64)
    yn = yc * lax.rsqrt(var + GN_EPS) * lnx_w + lnx_b
    bonus = _hsum(r * k_mod * r_k, e, et) * v
    return ((yn + bonus) * g,)


def _f_mix(gi, ya, yr, bg):
    gate = jax.nn.sigmoid(gi + bg)
    return (gate[:, 0:D] * ya + gate[:, D:2 * D] * yr,)


def _f_adamw(w, g, m, v):
    m = ADAM_B1 * m + (1.0 - ADAM_B1) * g
    v = ADAM_B2 * v + (1.0 - ADAM_B2) * jnp.square(g)
    m_hat = m / (1.0 - ADAM_B1 ** ADAM_STEP)
    v_hat = v / (1.0 - ADAM_B2 ** ADAM_STEP)
    return -ADAM_LR * (m_hat / (jnp.sqrt(v_hat) + ADAM_EPS) + ADAM_WD * w), m, v


def _down(x, k):
    row = lax.broadcasted_iota(jnp.int32, x.shape, 0)
    return jnp.where(row < k, 0.0, pltpu.roll(x, k, 0))


def _up(x, k):
    n = x.shape[0]
    row = lax.broadcasted_iota(jnp.int32, x.shape, 0)
    return jnp.where(row >= n - k, 0.0, pltpu.roll(x, n - k, 0))


def _col_spec(s, w, off=0):
    return pl.BlockSpec((s, w), lambda j: (0, j + off))


def _shift_bwd(z, mu, dzs):
    s, n = z.shape

    def body(z_ref, mu_ref, d_ref, dz_ref, dmu_ref):
        zz, d, m = z_ref[...].astype(F32), d_ref[...], mu_ref[...]
        dm = d * m
        dz_ref[...] = (d - dm + _up(dm, 1)).astype(dz_ref.dtype)
        dmu_ref[...] = jnp.sum(d * (_down(zz, 1) - zz), axis=0, keepdims=True)

    return _pcall(
        body, name="shift_bwd", grid=(n // 128,), in_specs=[_col_spec(s, 128), _col_spec(1, 128), _col_spec(s, 128)],
        out_specs=[_col_spec(s, 128), _col_spec(1, 128)],
        out_shape=[jax.ShapeDtypeStruct((s, n), BF16), jax.ShapeDtypeStruct((1, n), F32)],
        compiler_params=_cparams(("parallel",)),
    )(z, mu, dzs)


def _conv3(x, w_ref, b_ref):
    return b_ref[...] + w_ref[0:1, :] * _down(x, 2) + w_ref[1:2, :] * _down(x, 1) + w_ref[2:3, :] * x


def _conv_fwd(u, cw, cb):
    s = u.shape[0]
    nb = D_FF // 128

    def body(ug_ref, uv_ref, wg_ref, wv_ref, bg_ref, bv_ref, o_ref):
        gate = _conv3(ug_ref[...], wg_ref, bg_ref)
        val = _conv3(uv_ref[...], wv_ref, bv_ref)
        o_ref[...] = (gate * jax.nn.sigmoid(gate) * val).astype(o_ref.dtype)

    return _pcall(
        body, name="conv_fwd", grid=(nb,),
        in_specs=[_col_spec(s, 128), _col_spec(s, 128, nb), _col_spec(3, 128), _col_spec(3, 128, nb),
                  _col_spec(1, 128), _col_spec(1, 128, nb)],
        out_specs=_col_spec(s, 128), out_shape=jax.ShapeDtypeStruct((s, D_FF), BF16),
        compiler_params=_cparams(("parallel",)),
    )(u, u, cw, cw, cb, cb)


def _conv_bwd(u, cw, cb, dact):
    s = u.shape[0]
    nb = D_FF // 128

    def half(x, d, w_ref, du_ref, dw_ref, db_ref):
        x1, x2 = _down(x, 1), _down(x, 2)
        du_ref[...] = (w_ref[2:3, :] * d + w_ref[1:2, :] * _up(d, 1) + w_ref[0:1, :] * _up(d, 2)).astype(du_ref.dtype)
        dw_ref[0:1, :] = jnp.sum(d * x2, axis=0, keepdims=True)
        dw_ref[1:2, :] = jnp.sum(d * x1, axis=0, keepdims=True)
        dw_ref[2:3, :] = jnp.sum(d * x, axis=0, keepdims=True)
        db_ref[...] = jnp.sum(d, axis=0, keepdims=True)

    def body(ug_ref, uv_ref, wg_ref, wv_ref, bg_ref, bv_ref, da_ref,
             dug_ref, duv_ref, dwg_ref, dwv_ref, dbg_ref, dbv_ref):
        ug, uv, da = ug_ref[...], uv_ref[...], da_ref[...]
        gate = _conv3(ug, wg_ref, bg_ref)
        val = _conv3(uv, wv_ref, bv_ref)
        sg = jax.nn.sigmoid(gate)
        dgate = da * val * sg * (1.0 + gate * (1.0 - sg))
        dval = da * gate * sg
        half(ug, dgate, wg_ref, dug_ref, dwg_ref, dbg_ref)
        half(uv, dval, wv_ref, duv_ref, dwv_ref, dbv_ref)

    dug, duv, dwg, dwv, dbg, dbv = _pcall(
        body, name="conv_bwd", grid=(nb,),
        in_specs=[_col_spec(s, 128), _col_spec(s, 128, nb), _col_spec(3, 128), _col_spec(3, 128, nb),
                  _col_spec(1, 128), _col_spec(1, 128, nb), _col_spec(s, 128)],
        out_specs=[_col_spec(s, 128), _col_spec(s, 128), _col_spec(3, 128), _col_spec(3, 128),
                   _col_spec(1, 128), _col_spec(1, 128)],
        out_shape=[jax.ShapeDtypeStruct((s, D_FF), BF16), jax.ShapeDtypeStruct((s, D_FF), BF16),
                   jax.ShapeDtypeStruct((3, D_FF), F32), jax.ShapeDtypeStruct((3, D_FF), F32),
                   jax.ShapeDtypeStruct((1, D_FF), F32), jax.ShapeDtypeStruct((1, D_FF), F32)],
        compiler_params=_cparams(("parallel",)),
    )(u, u, cw, cw, cb, cb, dact)
    return (jnp.concatenate([dug, duv], axis=1), jnp.concatenate([dwg, dwv], axis=1),
            jnp.concatenate([dbg, dbv], axis=1))


ATT_BATCH = 4


def _att_batch(q, kp, kc, vp, vc, first):
    ma = lax.broadcasted_iota(jnp.int32, (1, ATT_BLOCK, 128), 2) < 64

    def diag(x):
        return jnp.concatenate([jnp.where(ma, x, 0.0), jnp.where(ma, 0.0, x)], axis=1)

    qi = lax.broadcasted_iota(jnp.int32, (1, ATT_BLOCK, 2 * ATT_BLOCK), 1)
    kj = lax.broadcasted_iota(jnp.int32, (1, ATT_BLOCK, 2 * ATT_BLOCK), 2) & (ATT_BLOCK - 1)
    okp = kj >= qi + jnp.where(first, 2 * ATT_BLOCK, 0)
    okc = kj <= qi
    sp = jnp.where(okp, _bnt(q, diag(kp)) * 0.125, NEG)
    sc = jnp.where(okc, _bnt(q, diag(kc)) * 0.125, NEG)

    def per_head(fn, x):
        return fn(x[..., :ATT_BLOCK]), fn(x[..., ATT_BLOCK:])

    def spread(ab):
        return jnp.concatenate([jnp.broadcast_to(t, t.shape[:2] + (ATT_BLOCK,)) for t in ab], axis=-1)

    row_max = functools.partial(jnp.max, axis=-1, keepdims=True)
    row_sum = functools.partial(jnp.sum, axis=-1, keepdims=True)
    m = [lax.stop_gradient(jnp.maximum(a, b)) for a, b in zip(per_head(row_max, sp), per_head(row_max, sc))]
    pp, pc = jnp.exp(sp - spread(m)), jnp.exp(sc - spread(m))
    den = [a + b for a, b in zip(per_head(row_sum, pp), per_head(row_sum, pc))]
    num = _bnn(pp, diag(vp)) + _bnn(pc, diag(vc))
    out = num / jnp.where(ma, den[0], den[1])
    lse = jnp.where(ma, m[0] + jnp.log(den[0]), m[1] + jnp.log(den[1]))
    return out, jnp.broadcast_to(lse, out.shape)


def _att_pairs_per_step(dil):
    return 4 if dil == 1 else 1


def _att_residues(dil):
    return min(dil, ATT_BATCH // _att_pairs_per_step(dil))


def _att_specs(g, dil):
    rows, pp = ATT_BLOCK * dil, _att_pairs_per_step(dil)

    def cur(slot):
        return pl.BlockSpec((rows, 128 * pp), lambda n, p: (n, (g * 3 + slot) * (4 // pp) + p))

    def prev(slot):
        return pl.BlockSpec((rows, 128 * pp), lambda n, p: (jnp.maximum(n - 1, 0), (g * 3 + slot) * (4 // pp) + p))

    return [cur(0), prev(1), cur(1), prev(2), cur(2)]


def _att_out_spec(dil):
    return pl.BlockSpec((ATT_BLOCK * dil, 128 * _att_pairs_per_step(dil)), lambda n, p: (n, p))


def _att_grid(s, dil):
    return (s // (ATT_BLOCK * dil), 4 // _att_pairs_per_step(dil))


def _att_windows(i, dil):
    res = _att_residues(dil)

    def rows(r):
        return pl.ds(i * res + r, ATT_BLOCK, stride=dil) if dil > 1 else pl.ds(0, ATT_BLOCK)

    return [(rows(r), pl.ds(128 * j, 128)) for j in range(_att_pairs_per_step(dil)) for r in range(res)]


def _att_fwd(att_in, g, dil):
    s = att_in.shape[0]

    def body(q_ref, kp_ref, kc_ref, vp_ref, vc_ref, o_ref, l_ref):
        first = pl.program_id(0) == 0

        def one(i, carry):
            win = _att_windows(i, dil)
            vals = [jnp.stack([ref[w] for w in win]) for ref in (q_ref, kp_ref, kc_ref, vp_ref, vc_ref)]
            o, l = _att_batch(*vals, first)
            for j, w in enumerate(win):
                o_ref[w] = o[j]
                l_ref[w] = l[j]
            return carry

        lax.fori_loop(0, dil // _att_residues(dil), one, 0)

    return _pcall(
        body, name=f"att_fwd{g}", grid=_att_grid(s, dil), in_specs=_att_specs(g, dil),
        out_specs=[_att_out_spec(dil)] * 2, out_shape=[jax.ShapeDtypeStruct((s, ATT_WIDTH), F32)] * 2,
        compiler_params=_cparams(("parallel", "parallel")),
    )(att_in, att_in, att_in, att_in, att_in)


def _att_bwd(att_in, g, dil, do, dl, acc):
    s = att_in.shape[0]

    def body(q_ref, kp_ref, kc_ref, vp_ref, vc_ref, do_ref, dl_ref, dq_ref, dkp_ref, dkc_ref, dvp_ref, dvc_ref):
        first = pl.program_id(0) == 0

        def one(i, carry):
            win = _att_windows(i, dil)
            vals = [jnp.stack([ref[w] for w in win]) for ref in (q_ref, kp_ref, kc_ref, vp_ref, vc_ref)]
            _, vjp = jax.vjp(lambda *a: _att_batch(*a, first), *vals)
            grads = vjp((jnp.stack([do_ref[w] for w in win]), jnp.stack([dl_ref[w] for w in win])))
            for ref, gr in zip((dq_ref, dkp_ref, dkc_ref, dvp_ref, dvc_ref), grads):
                for j, w in enumerate(win):
                    ref[w] = gr[j]
            return carry

        lax.fori_loop(0, dil // _att_residues(dil), one, 0)

    dq, dkp, dkc, dvp, dvc = _pcall(
        body, name=f"att_bwd{g}", grid=_att_grid(s, dil), in_specs=_att_specs(g, dil) + [_att_out_spec(dil)] * 2,
        out_specs=[_att_out_spec(dil)] * 5, out_shape=[jax.ShapeDtypeStruct((s, ATT_WIDTH), F32)] * 5,
        compiler_params=_cparams(("parallel", "parallel")),
    )(att_in, att_in, att_in, att_in, att_in, do, dl)

    unit, rb = ATT_BLOCK * dil, 1024
    steps = s // rb
    within = unit < rb

    def shifted(cur_ref, next_ref, has_next):
        nxt = jnp.where(has_next, next_ref[...], 0.0)
        return jnp.concatenate([cur_ref[unit:, :], nxt], axis=0) if within else nxt

    def cbody(dq_ref, dkc_ref, dkp_ref, dkn_ref, dvc_ref, dvp_ref, dvn_ref, *rest):
        o_ref = rest[-1]
        has_next = pl.program_id(0) + (1 if within else unit // rb) < steps
        o_ref[:, 0:ATT_WIDTH] = dq_ref[...].astype(BF16)
        o_ref[:, ATT_WIDTH:2 * ATT_WIDTH] = (dkc_ref[...] + shifted(dkp_ref, dkn_ref, has_next)).astype(BF16)
        o_ref[:, 2 * ATT_WIDTH:3 * ATT_WIDTH] = (dvc_ref[...] + shifted(dvp_ref, dvn_ref, has_next)).astype(BF16)

    cur = pl.BlockSpec((rb, ATT_WIDTH), lambda i: (i, 0))
    if within:
        nxt = pl.BlockSpec((unit, ATT_WIDTH), lambda i: (jnp.minimum((i + 1) * (rb // unit), s // unit - 1), 0))
    else:
        nxt = pl.BlockSpec((rb, ATT_WIDTH), lambda i: (jnp.minimum(i + unit // rb, steps - 1), 0))
    carried = [] if acc is None else [acc]
    return _pcall(
        cbody, name=f"att_bwd_sum{g}", grid=(steps,),
        in_specs=[cur, cur, cur, nxt, cur, cur, nxt] + [pl.BlockSpec(memory_space=pl.ANY)] * len(carried),
        out_specs=pl.BlockSpec((rb, 3 * ATT_WIDTH), lambda i: (i, g)),
        out_shape=jax.ShapeDtypeStruct((s, N_ATT), BF16), input_output_aliases={7: 0} if carried else {},
        compiler_params=_cparams(("parallel",)),
    )(dq, dkc, dkp, dkp, dvc, dvp, dvp, *carried)


def _unit_lower_inverse_impl(n):
    eye = (lax.broadcasted_iota(jnp.int32, (1,) + n.shape[1:], 1)
           == lax.broadcasted_iota(jnp.int32, (1,) + n.shape[1:], 2))
    t = jnp.where(eye, 1.0, 0.0) + n
    pw = n
    for _ in range(5):
        pw = _bnn(pw, pw)
        t = t + _bnn(t, pw)
    return t


@jax.custom_vjp
def _unit_lower_inverse(n):
    return _unit_lower_inverse_impl(n)


def _unit_lower_inverse_fwd(n):
    t = _unit_lower_inverse_impl(n)
    return t, t


_unit_lower_inverse.defvjp(_unit_lower_inverse_fwd, lambda t, g: (_bnt(_btn(t, g), t),))


@jax.custom_vjp
def _known_inverse(n, t):
    return t


_known_inverse.defvjp(lambda n, t: (t, t), lambda t, g: (_bnt(_btn(t, g), t), jnp.zeros_like(t)))


def _scan_chunk(r, lw, k, v, a, b, s0, inverse):
    c = SCAN_CHUNK
    p = s0.shape[0]
    ri = lax.broadcasted_iota(jnp.int32, (c, c), 0)
    ci = lax.broadcasted_iota(jnp.int32, (c, c), 1)
    cum = jnp.dot((ci <= ri).astype(F32), lw, precision=HI, preferred_element_type=F32)
    tot = jnp.sum(lw, axis=0, keepdims=True)
    ma = (lax.broadcasted_iota(jnp.int32, (c, 128 * p), 1) & 127) < 64

    def pairs(x):
        return jnp.concatenate([x[None, :, 128 * j:128 * (j + 1)] for j in range(p)], axis=0)

    def stack(x):
        return jnp.concatenate([pairs(jnp.where(ma, x, 0.0)), pairs(jnp.where(ma, 0.0, x))], axis=1)

    einv, eend = jnp.exp(-cum), jnp.exp(tot - cum)
    ra, aa = stack(r * jnp.exp(cum)), stack(a * jnp.exp(cum - lw))
    bi, ki, be, ke, vs = stack(b * einv), stack(k * einv), stack(b * eend), stack(k * eend), stack(v)
    r2 = lax.broadcasted_iota(jnp.int32, (1, 2 * c, 2 * c), 1)
    c2 = lax.broadcasted_iota(jnp.int32, (1, 2 * c, 2 * c), 2)
    same = (r2 >= c) == (c2 >= c)
    strict = jnp.logical_and(same, c2 < r2)
    incl = jnp.logical_and(same, c2 <= r2)
    s0 = jnp.where(same, s0, 0.0)
    prod = _bnt(jnp.concatenate([aa, ra], axis=1), jnp.concatenate([bi, ki], axis=1))
    a_ab = jnp.where(strict, prod[:, :2 * c, :2 * c], 0.0)
    a_ak = jnp.where(strict, prod[:, :2 * c, 2 * c:], 0.0)
    a_rb = jnp.where(incl, prod[:, 2 * c:, :2 * c], 0.0)
    a_rk = jnp.where(incl, prod[:, 2 * c:, 2 * c:], 0.0)
    t = inverse(a_ab)
    u = _bnn(t, _bnt(aa, s0) + _bnn(a_ak, vs))
    uv = jnp.concatenate([u, vs], axis=1)
    ys = _bnt(ra, s0) + _bnn(jnp.concatenate([a_rb, a_rk], axis=2), uv)
    s1 = s0 * pairs(jnp.exp(tot)) + _btn(uv, jnp.concatenate([be, ke], axis=1))
    y3 = ys[:, :c] + ys[:, c:]
    return (jnp.concatenate([y3[j] for j in range(p)], axis=1), s1), t


def _scan_specs(rev, n):
    def at(i):
        return n - 1 - i if rev else i

    def cm(cb):
        return pl.BlockSpec((SCAN_CHUNK, D), lambda i: (at(i), cb))

    return cm, pl.BlockSpec((1, SCAN_PAIRS, 128, 128), lambda i: (at(i), 0, 0, 0))


def _comm_phases(comm, refs, n, step=None):
    k = comm.n
    srcs, outs, sems = refs[:k], refs[k:2 * k], refs[2 * k:]
    i = pl.program_id(0) if step is None else step

    def before():
        @pl.when(i == 0)
        def _():
            comm.first(srcs, outs, sems)

    def after():
        if comm.mid is not None:
            @pl.when(i == (3 * n) // 4)
            def _():
                comm.mid(srcs, outs, sems)

        @pl.when(i == n - 1)
        def _():
            comm.last(srcs, outs, sems)

    return before, after


def _scan_fwd(zs, lw, km, aa, bb, comm):
    s = zs.shape[0]
    n = s // SCAN_CHUNK
    cm, st = _scan_specs(False, n)
    k = comm.n

    def body(*refs):
        r_ref, lw_ref, k_ref, v_ref, a_ref, b_ref = refs[:6]
        y_ref, s0_ref, t_ref = refs[6 + k:9 + k]
        state = refs[9 + 2 * k]
        before, after = _comm_phases(comm, refs[6:6 + k] + refs[9 + k:9 + 2 * k] + refs[10 + 2 * k:], n)
        before()

        @pl.when(pl.program_id(0) == 0)
        def _():
            state[...] = jnp.zeros_like(state)

        s0 = state[...]
        s0_ref[0] = s0
        (y, s1), t = _scan_chunk(*[ref[...] for ref in (r_ref, lw_ref, k_ref, v_ref, a_ref, b_ref)], s0,
                                 _unit_lower_inverse)
        y_ref[...] = y
        t_ref[0] = t.astype(BF16)
        state[...] = s1
        after()

    per_chunk = (n, SCAN_PAIRS, 128, 128)
    res = _pcall(
        body, name="scan_fwd", grid=(n,), in_specs=[cm(0), cm(0), cm(0), cm(2), cm(0), cm(0)] + [_HBM] * k,
        out_specs=[cm(0), st, st] + [_HBM] * k,
        out_shape=[jax.ShapeDtypeStruct((s, D), F32), jax.ShapeDtypeStruct(per_chunk, F32),
                   jax.ShapeDtypeStruct(per_chunk, BF16)] + comm.out_shape,
        scratch_shapes=[pltpu.VMEM((SCAN_PAIRS, 128, 128), F32)] + comm.sems,
        compiler_params=_cparams(("arbitrary",)),
    )(zs, lw, km, zs, aa, bb, *comm.ins)
    return res[0], res[1], res[2], res[3:]


def _scan_bwd(zs, lw, km, aa, bb, s0s, ts, dy, comm):
    s = zs.shape[0]
    n = s // SCAN_CHUNK
    cm, st = _scan_specs(True, n)
    k = comm.n

    def body(*refs):
        r_ref, lw_ref, k_ref, v_ref, a_ref, b_ref, s0_ref, t_ref, dy_ref = refs[:9]
        douts = refs[9 + k:15 + k]
        dstate = refs[15 + 2 * k]
        before, after = _comm_phases(comm, refs[9:9 + k] + refs[15 + k:15 + 2 * k] + refs[16 + 2 * k:], n)
        before()

        @pl.when(pl.program_id(0) == 0)
        def _():
            dstate[...] = jnp.zeros_like(dstate)

        t = t_ref[0].astype(F32)
        prim = [ref[...] for ref in (r_ref, lw_ref, k_ref, v_ref, a_ref, b_ref)] + [s0_ref[0]]
        _, vjp, _ = jax.vjp(lambda *p: _scan_chunk(*p, lambda nil: _known_inverse(nil, t)), *prim, has_aux=True)
        grads = vjp((dy_ref[...], dstate[...]))
        for ref, gr in zip(douts, grads[:6]):
            ref[...] = gr
        dstate[...] = grads[6]
        after()

    res = _pcall(
        body, name="scan_bwd", grid=(n,),
        in_specs=[cm(0), cm(0), cm(0), cm(2), cm(0), cm(0), st, st, cm(0)] + [_HBM] * k,
        out_specs=[cm(0)] * 6 + [_HBM] * k, out_shape=[jax.ShapeDtypeStruct((s, D), F32)] * 6 + comm.out_shape,
        scratch_shapes=[pltpu.VMEM((SCAN_PAIRS, 128, 128), F32)] + comm.sems,
        compiler_params=_cparams(("arbitrary",)),
    )(zs, lw, km, zs, aa, bb, s0s, ts, dy, *comm.ins)
    return res[:6], res[6:]


_HBM = pl.BlockSpec(memory_space=pltpu.HBM)


def _me():
    return lax.axis_index("x"), lax.axis_index("y"), lax.axis_index("c")


def _allgather8(src, name):
    def body(src_ref, out_ref, ssem, rsem, lsem):
        x, y, c = _me()
        me = 4 * x + 2 * y + c
        local = pltpu.make_async_copy(src_ref, out_ref.at[me], lsem)
        local.start()
        peers = []
        for k in range(1, 8):
            peers.append(((1 - x) if k & 4 else x, (1 - y) if k & 2 else y, (1 - c) if k & 1 else c))
        sends = []
        for k, peer in enumerate(peers):
            cp = pltpu.make_async_remote_copy(src_ref, out_ref.at[me], ssem.at[k], rsem.at[k], device_id=peer,
                                              device_id_type=MESH)
            cp.start()
            sends.append(cp)
        for k, (px, py, pc) in enumerate(peers):
            pltpu.make_async_remote_copy(src_ref, out_ref.at[4 * px + 2 * py + pc], ssem.at[k], rsem.at[k],
                                         device_id=(px, py, pc), device_id_type=MESH).wait_recv()
        for cp in sends:
            cp.wait_send()
        local.wait()

    return _pcall(
        body, name=name, in_specs=[_HBM], out_specs=_HBM, out_shape=jax.ShapeDtypeStruct((8,) + src.shape, src.dtype),
        scratch_shapes=[pltpu.SemaphoreType.DMA((7,)), pltpu.SemaphoreType.DMA((7,)), pltpu.SemaphoreType.DMA],
    )(src)


def _other_chips(x, y):
    return [(1 - x, y), (x, 1 - y), (1 - x, 1 - y)]


def _remote(src, dst, ssem, rsem, to):
    return pltpu.make_async_remote_copy(src, dst, ssem, rsem, device_id=to, device_id_type=MESH)


class _GatherWeights:
    def __init__(self, shards):
        self.ins = list(shards)
        n = self.n = len(shards)
        self.out_shape = [jax.ShapeDtypeStruct((4,) + t.shape, t.dtype) for t in shards]
        self.sems = [pltpu.SemaphoreType.DMA((6 * n,)), pltpu.SemaphoreType.DMA((6 * n,)),
                     pltpu.SemaphoreType.DMA((n,)), pltpu.SemaphoreType.DMA((n,))]

    def _copies(self, srcs, outs, sems):
        ssem, rsem, lsem, osem = sems
        x, y, c = _me()
        me = 2 * x + y
        own, ici, landed, passed, passed_in = [], [], [], [], []
        for a in range(self.n):
            h = self.ins[a].shape[0] // 2
            mine, other = pl.ds(c * h, h), pl.ds((1 - c) * h, h)
            own.append(_remote(srcs[a], outs[a].at[me], lsem.at[a], osem.at[a], (x, y, 1 - c)))
            for k, (px, py) in enumerate(_other_chips(x, y)):
                s1, r1, s2, r2 = ssem.at[6 * a + k], rsem.at[6 * a + k], ssem.at[6 * a + 3 + k], rsem.at[6 * a + 3 + k]
                got, got_sib = outs[a].at[2 * px + py, mine], outs[a].at[2 * px + py, other]
                ici.append(_remote(srcs[a].at[mine], outs[a].at[me, mine], s1, r1, (px, py, c)))
                landed.append(_remote(got, got, s1, r1, (px, py, c)))
                passed.append(_remote(got, got, s2, r2, (x, y, 1 - c)))
                passed_in.append(_remote(got_sib, got_sib, s2, r2, (x, y, 1 - c)))
        return own, ici, landed, passed, passed_in

    def first(self, srcs, outs, sems):
        own, ici, _, _, _ = self._copies(srcs, outs, sems)
        for cp in own + ici:
            cp.start()

    def mid(self, srcs, outs, sems):
        _, _, landed, passed, _ = self._copies(srcs, outs, sems)
        for arrived, onward in zip(landed, passed):
            arrived.wait_recv()
            onward.start()

    def last(self, srcs, outs, sems):
        own, ici, _, passed, passed_in = self._copies(srcs, outs, sems)
        for cp in passed_in:
            cp.wait_recv()
        for cp in ici + passed:
            cp.wait_send()
        for cp in own:
            cp.wait()


class _ScatterToChips:
    def __init__(self, parts):
        self.ins = list(parts)
        n = self.n = len(parts)
        self.out_shape = [jax.ShapeDtypeStruct(t.shape, t.dtype) for t in parts]
        self.sems = [pltpu.SemaphoreType.DMA((3 * n,)), pltpu.SemaphoreType.DMA((3 * n,)), pltpu.SemaphoreType.DMA((n,))]

    def _copies(self, srcs, outs, sems):
        ssem, rsem, lsem = sems
        x, y, c = _me()
        me = 2 * x + y
        own, out, landed = [], [], []
        for a in range(self.n):
            own.append(pltpu.make_async_copy(srcs[a].at[me], outs[a].at[me], lsem.at[a]))
            for k, (px, py) in enumerate(_other_chips(x, y)):
                dst = outs[a].at[2 * px + py]
                out.append(_remote(srcs[a].at[2 * px + py], outs[a].at[me], ssem.at[3 * a + k], rsem.at[3 * a + k],
                                   (px, py, c)))
                landed.append(_remote(dst, dst, ssem.at[3 * a + k], rsem.at[3 * a + k], (px, py, c)))
        return own, out, landed

    def first(self, srcs, outs, sems):
        own, out, _ = self._copies(srcs, outs, sems)
        for cp in own + out:
            cp.start()

    mid = None

    def last(self, srcs, outs, sems):
        own, out, landed = self._copies(srcs, outs, sems)
        for cp in landed:
            cp.wait_recv()
        for cp in own:
            cp.wait()
        for cp in out:
            cp.wait_send()


def _run_comm(comm, name):
    n = comm.n

    def body(*refs):
        srcs, outs, sems = refs[:n], refs[n:2 * n], refs[2 * n:]
        comm.first(srcs, outs, sems)
        if comm.mid is not None:
            comm.mid(srcs, outs, sems)
        comm.last(srcs, outs, sems)

    return _pcall(body, name=name, in_specs=[_HBM] * n, out_specs=[_HBM] * n, out_shape=comm.out_shape,
                  scratch_shapes=comm.sems)(*comm.ins)


class _NoComm:
    n, ins, out_shape, sems, mid = 0, [], [], [], None

    def first(self, srcs, outs, sems):
        pass

    def last(self, srcs, outs, sems):
        pass


_NOTHING = _NoComm()


class _SiblingHalves:
    mid = None

    def __init__(self, grads):
        self.ins = list(grads)
        n = self.n = len(grads)
        self.out_shape = [jax.ShapeDtypeStruct((4, t.shape[1] // 2, t.shape[2]), t.dtype) for t in grads]
        self.sems = [pltpu.SemaphoreType.DMA((n,)), pltpu.SemaphoreType.DMA((n,))]

    def _copies(self, srcs, outs, sems):
        ssem, rsem = sems
        x, y, c = _me()
        copies = []
        for a in range(self.n):
            h = self.ins[a].shape[1] // 2
            copies.append(_remote(srcs[a].at[:, pl.ds((1 - c) * h, h)], outs[a], ssem.at[a], rsem.at[a], (x, y, 1 - c)))
        return copies

    def first(self, srcs, outs, sems):
        for cp in self._copies(srcs, outs, sems):
            cp.start()

    def last(self, srcs, outs, sems):
        for cp in self._copies(srcs, outs, sems):
            cp.wait()


def _reduce_finish(reds, name):
    n = len(reds)

    def body(*refs):
        outs = refs[n:2 * n]
        ssem, rsem = refs[2 * n:]
        x, y, c = _me()
        copies = []
        for a in range(n):
            h = reds[a].shape[0] // 2
            mine = outs[a].at[pl.ds(c * h, h)]
            copies.append(_remote(mine, mine, ssem.at[a], rsem.at[a], (x, y, 1 - c)))
        for cp in copies:
            cp.start()
        for a in range(n):
            h = reds[a].shape[0] // 2
            dst = outs[a].at[pl.ds((1 - c) * h, h)]
            _remote(dst, dst, ssem.at[a], rsem.at[a], (x, y, 1 - c)).wait_recv()
        for cp in copies:
            cp.wait_send()

    return _pcall(
        body, name=name, in_specs=[_HBM] * n, out_specs=[_HBM] * n,
        out_shape=[jax.ShapeDtypeStruct(t.shape, t.dtype) for t in reds],
        input_output_aliases={a: a for a in range(n)},
        scratch_shapes=[pltpu.SemaphoreType.DMA((n,)), pltpu.SemaphoreType.DMA((n,))],
    )(*reds)


def _half_sum(fn, full, halves, out_full, out_dtype, core, name):
    p, h, c = (halves[0].shape if halves else (full[0].shape[0], full[0].shape[1] // 2, full[0].shape[2]))
    br = _div(h, max(16, (1 << 19) // (p * c)), 16)
    nb = h // br
    mine3 = pl.BlockSpec((p, br, c), lambda i, core_ref: (0, core_ref[0] * nb + i, 0))
    half3 = pl.BlockSpec((p, br, c), lambda i, core_ref: (0, i, 0))

    def body(core_ref, *refs):
        refs[-1][...] = fn(*[t[...].astype(F32) for t in refs[:-1]]).astype(out_dtype)

    if out_full:
        out_spec = pl.BlockSpec((br, c), lambda i, core_ref: (core_ref[0] * nb + i, 0))
        out_shape = jax.ShapeDtypeStruct((2 * h, c), out_dtype)
    else:
        out_spec, out_shape = half3, jax.ShapeDtypeStruct((p, h, c), out_dtype)
    return _pcall(
        body, name=name,
        grid_spec=pltpu.PrefetchScalarGridSpec(
            num_scalar_prefetch=1, grid=(nb,), in_specs=[mine3] * len(full) + [half3] * len(halves),
            out_specs=out_spec),
        out_shape=out_shape, compiler_params=_cparams(("parallel",)),
    )(core, *full, *halves)


def _ada_fwd(c_all, w, b):
    def body(c_ref, w_ref, b_ref, o_ref):
        o_ref[...] = jnp.dot(c_ref[...], w_ref[...], precision=HI, preferred_element_type=F32) + b_ref[...]

    return _pcall(body, name="ada_fwd", out_shape=jax.ShapeDtypeStruct((c_all.shape[0], w.shape[1]), F32),
                  compiler_params=pltpu.CompilerParams(vmem_limit_bytes=VMEM_LIMIT))(c_all, w, b)


def _ada_bwd(c_all_t, d):
    def body(c_ref, d_ref, o_ref):
        o_ref[...] = jnp.dot(c_ref[...], d_ref[...], precision=HI, preferred_element_type=F32)

    return _pcall(body, name="ada_bwd", out_shape=jax.ShapeDtypeStruct((c_all_t.shape[0], d.shape[1]), F32),
                  compiler_params=pltpu.CompilerParams(vmem_limit_bytes=VMEM_LIMIT))(c_all_t, d)


def _sum_lead(x, name):
    p, r, n = x.shape
    br = _div(r, 512, 8)

    def body(x_ref, o_ref):
        acc = x_ref[0]
        for j in range(1, p):
            acc = acc + x_ref[j]
        o_ref[...] = acc

    return _pcall(
        body, name=name, grid=(r // br,), in_specs=[pl.BlockSpec((p, br, n), lambda i: (0, i, 0))],
        out_specs=pl.BlockSpec((br, n), lambda i: (i, 0)), out_shape=jax.ShapeDtypeStruct((r, n), F32),
        compiler_params=_cparams(("parallel",)),
    )(x)


def _adamw(w, g, m, v, name):
    shape = w.shape
    cols = shape[-1]
    w2, g2, m2, v2 = [t.reshape(-1, cols) for t in (w, g, m, v)]
    rows = w2.shape[0]
    pref = max(8, (1 << 19) // cols // 8 * 8)
    br = _div(rows, pref, 8)
    if rows // br > 64:
        br = pref
    outs = _rows_fwd(_f_adamw, [(t, cols, 0) for t in (w2, g2, m2, v2)], [], [(cols, F32)] * 3, name=name, br=br)
    return [o.reshape(shape) for o in outs]


_BIG = (("w_in", 1), ("w_up", 1), ("w_down", 0), ("w_o", 0), ("w_rwkv_out", 0), ("w_att_out", 1), ("w2", 1), ("a2", 1),
        ("g2", 1))


_NEEDED_FIRST = ("w_in", "w_att_out", "w2", "a2", "g2")
_NEEDED_LATER = ("w_up", "w_down", "w_o", "w_rwkv_out")
_DONE_EARLY = ("w_up", "w_down", "w_o", "w_rwkv_out", "w_att_out")
_DONE_LATE = ("w_in", "w2", "a2", "g2")


def _cols_joined(t):
    return jnp.concatenate([t[j] for j in range(4)], axis=1)


def _cols_split(t):
    n = t.shape[1] // 4
    return jnp.stack([t[:, j * n:(j + 1) * n] for j in range(4)])


W_IN_SHARD = (N_ATT + N_RW + N_GATE) // 4
W_IN_PAD = 2560


def _row_window(parts, lo, hi):
    out, pos = [], 0
    for t, w in parts:
        a, b = max(lo, pos), min(hi, pos + w)
        if a < b:
            out.append(t[a - pos:b - pos])
        pos += w
    return out[0] if len(out) == 1 else jnp.concatenate(out, axis=0)


def _rows_joined(t):
    return t.reshape(4 * t.shape[1], t.shape[2])


def _rows_split(t):
    return t.reshape(4, t.shape[0] // 4, t.shape[1])


def _step_to_scan(x, tgt, ada, wts):
    sh1, sc1, gt1, sh2, sc2, gt2 = ada
    br = 256
    grp = lax.broadcasted_iota(jnp.int32, (D, 128), 0) // 64 == lax.broadcasted_iota(jnp.int32, (D, 128), 1)
    e = grp.astype(F32)
    et = e.T
    w_in = [(wts["w_in"][j], W_IN_SHARD) for j in range(4)]
    w_att = _row_window(w_in, 0, N_ATT)
    w_rw = jnp.concatenate([_row_window(w_in, N_ATT, N_ATT + N_RW), jnp.zeros((N_RWP - N_RW, D), BF16)], axis=0)
    w_gate = _row_window(w_in, N_ATT + N_RW, N_ATT + N_RW + N_GATE)
    mu = jnp.pad(wts["mu_shift"], ((0, 0), (0, N_RWP - N_RW)))
    wl = jnp.zeros((N_LORA, 3 * D), F32)
    wl = wl.at[0:64, 0:D].set(_cols_joined(wts["w2"]).astype(F32))
    wl = wl.at[64:128, D:2 * D].set(_cols_joined(wts["a2"]).astype(F32))
    wl = wl.at[128:288, 2 * D:3 * D].set(_cols_joined(wts["g2"]).astype(F32))
    pre1_c = [wts["norm1_w"], sc1, sh1]
    (h1,) = _rows_fwd(_f_pre, [(x, D, 0)], pre1_c, [(D, BF16), None], name="pre1_fwd", br=2 * br)
    att_in = _mm(h1, w_att, tb=True, name="mm_att_in")
    z = _mm(h1, w_rw, tb=True, out_dtype=BF16, name="mm_rw_in")
    gate_in = _mm(h1, w_gate, tb=True, out_dtype=BF16, name="mm_gate_in")
    att_o, att_l = [], []
    for g, (_, dil) in enumerate(ATT_PATTERNS):
        o, l = _att_fwd(att_in, g, dil)
        att_o.append(o)
        att_l.append(l)
    comb_rows = [(t, ATT_WIDTH, 0) for t in att_o + att_l]
    (att,) = _rows_fwd(_f_comb, comb_rows, [], [(ATT_WIDTH, BF16)], name="comb_fwd", br=2 * br)
    y_att = _mm(att, wts["w_att_out"], b_chip=True, out_dtype=BF16, name="mm_att_out")
    rwpre_c = [wts["w0"], wts["a0"], wts["k_k"], wts["k_a"], wl, e, et]

    def shift_and_rwpre(zz, *rest):
        consts, mu_row, before = rest[:-2], rest[-2], rest[-1]
        last = jnp.sum(jnp.where(lax.broadcasted_iota(jnp.int32, before.shape, 0) == HALO - 1, before, 0.0), axis=0,
                       keepdims=True)
        row = lax.broadcasted_iota(jnp.int32, zz.shape, 0)
        zprev = jnp.where(row == 0, last, pltpu.roll(zz, 1, 0))
        shifted = zz + (zprev - zz) * mu_row
        return (shifted,) + tuple(_f_rwpre(shifted, *consts))

    zs, lw, km, aa, bb, gg = _rows_fwd(
        shift_and_rwpre, [(z, N_RWP, 0)], rwpre_c + [mu],
        [(N_RWP, F32), None, (D, F32), (D, F32), None, (D, F32), (D, F32), (D, F32)], name="rwpre_fwd", br=br, halo=0)
    return dict(x=x, tgt=tgt, wts=wts, br=br, e=e, et=et, gt1=gt1, sc2=sc2, sh2=sh2, gt2=gt2, w_att=w_att, w_rw=w_rw,
                w_gate=w_gate, mu=mu, pre1_c=pre1_c, h1=h1, att_in=att_in, z=z, gate_in=gate_in, comb_rows=comb_rows,
                att=att, y_att=y_att, zs=zs, rwpre_c=rwpre_c, lw=lw, km=km, aa=aa, bb=bb, gg=gg)


def _step_between_scans(st, y_raw, late):
    x, tgt, wts, br, e, et = st["x"], st["tgt"], st["wts"], st["br"], st["e"], st["et"]
    zs, km, gg, gate_in, y_att, att = st["zs"], st["km"], st["gg"], st["gate_in"], st["y_att"], st["att"]
    comb_rows, att_in = st["comb_rows"], st["att_in"]
    gt1, sc2, sh2, gt2 = st["gt1"], st["sc2"], st["sh2"], st["gt2"]
    w_up, w_ao = late["w_up"], wts["w_att_out"]
    w_down, w_o, w_ro = _rows_joined(late["w_down"]), _rows_joined(late["w_o"]), _rows_joined(late["w_rwkv_out"])
    post_rows = [(y_raw, D, 0), (zs, D, 0), (zs, D, 2), (km, D, 0), (gg, D, 0)]
    post_c = [wts["lnx_w"], wts["lnx_b"], wts["r_k"], e, et]
    (rw_out,) = _rows_fwd(_f_rwpost, post_rows, post_c, [(D, BF16)], name="rwpost_fwd", br=br)
    y_rw = _mm(rw_out, w_ro, out_dtype=BF16, name="mm_rw_out")
    mix_rows = [(gate_in, N_GATE, 0), (y_att, D, 0), (y_rw, D, 0)]
    (mix,) = _rows_fwd(_f_mix, mix_rows, [wts["b_gate"]], [(D, BF16)], name="mix_fwd", br=2 * br)
    o = _mm(mix, w_o, out_dtype=BF16, name="mm_o")
    pre2_c = [gt1, wts["norm2_w"], sc2, sh2]
    x1, h2 = _rows_fwd(_f_pre2, [(x, D, 0), (o, D, 0)], pre2_c, [(D, F32), (D, BF16)], name="pre2_fwd", br=2 * br)
    u = _mm(h2, w_up, b_chip=True, name="mm_up")
    act = _conv_fwd(u, wts["conv_w"], wts["conv_b"])
    f = _mm(act, w_down, out_dtype=BF16, name="mm_down")
    fin_rows = [(x1, D, 0), (f, D, 0), (tgt, D, 0)]
    fin_c = [gt2, wts["norm_f_w"]]

    def fin_fwd(*a):
        (l,) = _f_fin(*a)
        return (jnp.broadcast_to(jnp.sum(l, axis=0, keepdims=True), (8, 128)),)

    (loss_acc,) = _rows_fwd(fin_fwd, fin_rows, fin_c, [], name="fin_fwd", br=2 * br, acc_shape=(8, 128))

    gw = {}
    dx1a, df, d_gt2, gw["norm_f_w"] = _rows_bwd(
        _f_fin, fin_rows, fin_c, [[]], wrt_rows=[0, 1], wrt_consts=[0, 1], drow_dtypes=[F32, BF16],
        name="fin_bwd", br=2 * br, unit_cot=True)
    dact = _mm(df, w_down, tb=True, name="mm_dact")
    gw["w_down"] = _rows_split(_mm(act, df, ta=True, out_dtype=BF16, name="mm_dw_down"))
    du, gw["conv_w"], gw["conv_b"] = _conv_bwd(u, wts["conv_w"], wts["conv_b"], dact)
    dh2 = _mm(du, w_up, tb=True, b_chip=True, out_dtype=BF16, name="mm_dh2")
    gw["w_up"] = _mm(h2, du, ta=True, out_chip=True, out_dtype=BF16, name="mm_dw_up")
    dxa, do, d_gt1, gw["norm2_w"], d_sc2, d_sh2 = _rows_bwd(
        _f_pre2, [(x, D, 0), (o, D, 0)], pre2_c, [[(dx1a, D, 0)], [(dh2, D, 0)]], wrt_rows=[0, 1],
        wrt_consts=[0, 1, 2, 3], drow_dtypes=[F32, BF16], name="pre2_bwd", br=2 * br)
    dmix = _mm(do, w_o, tb=True, out_dtype=BF16, name="mm_dmix")
    gw["w_o"] = _rows_split(_mm(mix, do, ta=True, out_dtype=BF16, name="mm_dw_o"))
    dgate, dya, dyr, gw["b_gate"] = _rows_bwd(
        _f_mix, mix_rows, [wts["b_gate"]], [[(dmix, D, 0)]], wrt_rows=[0, 1, 2], wrt_consts=[0],
        drow_dtypes=[BF16] * 3, name="mix_bwd", br=2 * br)
    datt = _mm(dya, w_ao, tb=True, b_chip=True, out_dtype=BF16, name="mm_datt")
    gw["w_att_out"] = _mm(att, dya, ta=True, out_chip=True, out_dtype=BF16, name="mm_dw_att_out")
    drw = _mm(dyr, w_ro, tb=True, out_dtype=BF16, name="mm_drw")
    gw["w_rwkv_out"] = _rows_split(_mm(rw_out, dyr, ta=True, out_dtype=BF16, name="mm_dw_rw_out"))
    dcomb = _rows_bwd(_f_comb, comb_rows, [], [[(datt, ATT_WIDTH, 0)]], wrt_rows=list(range(6)), wrt_consts=[],
                      drow_dtypes=[F32] * 6, name="comb_bwd", br=2 * br)
    datt_in = None
    for g, (_, dil) in enumerate(ATT_PATTERNS):
        datt_in = _att_bwd(att_in, g, dil, dcomb[g], dcomb[3 + g], datt_in)
    dy_raw, dr_p, dv_p, dkm_p, dgg, gw["lnx_w"], gw["lnx_b"], gw["r_k"], *recv_early = _rows_bwd(
        _f_rwpost, post_rows, post_c, [[(drw, D, 0)]], wrt_rows=[0, 1, 2, 3, 4], wrt_consts=[0, 1, 2],
        drow_dtypes=[F32] * 5, name="rwpost_bwd", br=br, comm=_SiblingHalves([gw[n] for n in _DONE_EARLY]))
    st.update(loss=loss_acc[0, 0], gw=gw, dxa=dxa, dgate=dgate, datt_in=datt_in,
              dy_raw=dy_raw, dr_p=dr_p, dv_p=dv_p, dkm_p=dkm_p, dgg=dgg, d_ada_late=(d_gt1, d_sh2, d_sc2, d_gt2),
              recv_early=recv_early)
    return st


def _chip_parts(grads, recv, names, core):
    return [_half_sum(lambda a, b: a + b, [g], [r], False, BF16, core, "reduce_add2_" + n)
            for g, r, n in zip(grads, recv, names)]


def _step_after_scan(st, scan_grads, core):
    x, br, gw, h1, zs = st["x"], st["br"], st["gw"], st["h1"], st["zs"]
    dr_s, dlw, dkm_s, dv_s, daa, dbb = scan_grads
    pre_cots = [[(st["dr_p"], D, 0), (dr_s, D, 0)], [(dlw, D, 0)], [(st["dkm_p"], D, 0), (dkm_s, D, 0)],
                [(st["dv_p"], D, 0), (dv_s, D, 0)], [(daa, D, 0)], [(dbb, D, 0)], [(st["dgg"], D, 0)]]
    dzs, gw["w0"], gw["a0"], gw["k_k"], gw["k_a"], dwl = _rows_bwd(
        _f_rwpre, [(zs, N_RWP, 0)], st["rwpre_c"], pre_cots, wrt_rows=[0], wrt_consts=[0, 1, 2, 3, 4],
        drow_dtypes=[F32], name="rwpre_bwd", br=128)
    gw["w2"], gw["a2"] = _cols_split(dwl[0:64, 0:D]), _cols_split(dwl[64:128, D:2 * D])
    gw["g2"] = _cols_split(dwl[128:288, 2 * D:3 * D])
    dz, dmu = _shift_bwd(st["z"], st["mu"], dzs)
    gw["mu_shift"] = dmu[:, :N_RW]
    datt_in, dgate = st["datt_in"], st["dgate"]
    dw_in = [(_mm(datt_in, h1, ta=True, out_dtype=BF16, name="mm_dw_att"), N_ATT),
             (_mm(dz, h1, ta=True, out_dtype=BF16, name="mm_dw_rw"), N_RW),
             (_mm(dgate, h1, ta=True, out_dtype=BF16, name="mm_dw_gate"), N_GATE)]
    slabs = []
    for j in range(4):
        slabs += [_row_window(dw_in, j * W_IN_SHARD, (j + 1) * W_IN_SHARD), jnp.zeros((W_IN_PAD - W_IN_SHARD, D), BF16)]
    gw["w_in"] = jnp.concatenate(slabs, axis=0).reshape(4, W_IN_PAD, D)
    late = [gw[n] for n in _DONE_LATE]
    parts = _chip_parts(late, _run_comm(_SiblingHalves(late), "reduce_sib_late"), _DONE_LATE, core)
    dh1, slots_late = _mm_sum([(datt_in, st["w_att"]), (dz, st["w_rw"]), (dgate, st["w_gate"])],
                              comm=_ScatterToChips(parts), name="mm_dh1")
    grad_x, gw["norm1_w"], d_sc1, d_sh1 = _rows_bwd(
        _f_pre, [(x, D, 0)], st["pre1_c"], [[(dh1, D, 0)], [(st["dxa"], D, 0)]], wrt_rows=[0], wrt_consts=[0, 1, 2],
        drow_dtypes=[F32], name="pre1_bwd", br=2 * br)
    d_gt1, d_sh2, d_sc2, d_gt2 = st["d_ada_late"]
    return st["loss"], grad_x, (d_sh1, d_sc1, d_gt1, d_sh2, d_sc2, d_gt2), gw, slots_late


_SMALL = ("b_ada", "norm1_w", "b_gate", "mu_shift", "w0", "a0", "k_k", "k_a", "r_k", "lnx_w", "lnx_b", "norm2_w",
          "conv_b", "norm_f_w")
_NAMES = ("w_ada", "b_ada", "norm1_w", "w_in", "b_gate", "mu_shift", "w0", "w2", "a0", "a2", "g2", "k_k", "k_a", "r_k",
          "lnx_w", "lnx_b", "w_att_out", "w_rwkv_out", "w_o", "norm2_w", "w_up", "conv_w", "conv_b", "w_down",
          "norm_f_w")


def kernel(x, c, w_ada, b_ada, norm1_w, w_in, b_gate, mu_shift, w0, w2, a0, a2, g2, k_k, k_a, r_k, lnx_w, lnx_b, w_att_out, w_rwkv_out, w_o, norm2_w, w_up, conv_w, conv_b, w_down, norm_f_w, loss_target, m_w_ada, m_b_ada, m_norm1_w, m_w_in, m_b_gate, m_mu_shift, m_w0, m_w2, m_a0, m_a2, m_g2, m_k_k, m_k_a, m_r_k, m_lnx_w, m_lnx_b, m_w_att_out, m_w_rwkv_out, m_w_o, m_norm2_w, m_w_up, m_conv_w, m_conv_b, m_w_down, m_norm_f_w, v_w_ada, v_b_ada, v_norm1_w, v_w_in, v_b_gate, v_mu_shift, v_w0, v_w2, v_a0, v_a2, v_g2, v_k_k, v_k_a, v_r_k, v_lnx_w, v_lnx_b, v_w_att_out, v_w_rwkv_out, v_w_o, v_norm2_w, v_w_up, v_conv_w, v_conv_b, v_w_down, v_norm_f_w):
    args = dict(locals())
    p, pm, pv = {}, {}, {}
    for name in _NAMES:
        for dst, key in ((p, name), (pm, "m_" + name), (pv, "v_" + name)):
            t = args[key]
            if name == "w_in":
                dst[name] = jnp.swapaxes(t, 1, 2)[0]
            else:
                dst[name] = t.reshape(1, -1) if name in ("r_k", "norm_f_w") else t.reshape(t.shape[-2], t.shape[-1])
    xi, yi, ci = _me()
    chip = 2 * xi + yi
    dev = 4 * xi + 2 * yi + ci
    x2, tgt = x[0], loss_target[0]

    n_cw = 3 * (2 * D_FF // 4)
    vec = jnp.concatenate([c.reshape(-1), p["conv_w"].reshape(-1), jnp.zeros((8 * D - D - n_cw,), F32)]).reshape(8, D)
    g0 = _allgather8(vec, "gather_c").reshape(8, 8 * D)
    c_all = g0[:, :D]
    conv_w_full = jnp.concatenate([g0[2 * j, D:D + n_cw].reshape(3, -1) for j in range(4)], axis=1)
    n_ada = 6 * D // 4
    b_ada_sh = lax.dynamic_slice(p["b_ada"], (0, chip * n_ada), (1, n_ada))
    ada_sh = _ada_fwd(c_all, p["w_ada"], b_ada_sh)
    ga = _allgather8(ada_sh, "gather_ada")
    ada_all = jnp.concatenate([ga[2 * j] for j in range(4)], axis=1)
    ada_row = lax.dynamic_slice(ada_all, (dev, 0), (1, 6 * D))
    ada = [ada_row[:, j * D:(j + 1) * D] for j in range(6)]

    big = [n for n, _ in _BIG]
    shard = {n: p[n].astype(BF16) for n in big}
    shard["w_in"] = jnp.pad(shard["w_in"], ((0, W_IN_PAD - W_IN_SHARD), (0, 0)))
    wts = dict(zip(_NEEDED_FIRST, _run_comm(_GatherWeights([shard[n] for n in _NEEDED_FIRST]), "gather_w")))
    for n in _SMALL:
        wts[n] = p[n]
    wts["conv_w"] = conv_w_full
    core = ci.reshape(1).astype(jnp.int32)

    st = _step_to_scan(x2, tgt, ada, wts)
    y_raw, s0s, inverses, late = _scan_fwd(st["zs"], st["lw"], st["km"], st["aa"], st["bb"],
                                           _GatherWeights([shard[n] for n in _NEEDED_LATER]))
    st = _step_between_scans(st, y_raw, dict(zip(_NEEDED_LATER, late)))
    early = _chip_parts([st["gw"][n] for n in _DONE_EARLY], st["recv_early"], _DONE_EARLY, core)
    scan_grads, slots_early = _scan_bwd(st["zs"], st["lw"], st["km"], st["aa"], st["bb"], s0s, inverses,
                                        st["dy_raw"], _ScatterToChips(early))
    loss_part, grad_x, d_ada, gw, slots_late = _step_after_scan(st, scan_grads, core)

    small = [jnp.concatenate(d_ada, axis=1)] + [gw[n] for n in _SMALL[1:]] + [gw["conv_w"], loss_part.reshape(1, 1)]
    sizes = [t.size for t in small]
    flat = jnp.concatenate([t.reshape(-1) for t in small])
    npad = (-flat.shape[0]) % (8 * D)
    srows = (flat.shape[0] + npad) // D
    flat = jnp.concatenate([flat, jnp.zeros((npad,), F32)]).reshape(srows, D)
    parts = _allgather8(flat, "gather_small")
    tot = _sum_lead(parts, "sum_small").reshape(-1)
    pieces, pos = [], 0
    for sz in sizes:
        pieces.append(tot[pos:pos + sz])
        pos += sz
    grads = {}
    for n, piece in zip(_SMALL, pieces[:len(_SMALL)]):
        grads[n] = piece.reshape(p[n].shape)
    conv_w_grad = pieces[len(_SMALL)].reshape(3, 2 * D_FF)
    grads["conv_w"] = lax.dynamic_slice(conv_w_grad, (0, chip * (n_cw // 3)), (3, n_cw // 3))
    loss = pieces[-1][0]
    d_ada_all = parts[:, :6].reshape(8, 6 * D)
    grads["w_ada"] = _ada_bwd(c_all.T, lax.dynamic_slice(d_ada_all, (0, chip * n_ada), (8, n_ada)))

    order = _DONE_EARLY + _DONE_LATE
    reds = [_half_sum(lambda t: t[0] + t[1] + t[2] + t[3], [], [t], True, F32, core, "reduce_add4_" + n)
            for n, t in zip(order, list(slots_early) + list(slots_late))]
    for n, g in zip(order, _reduce_finish(reds, "reduce_sib2")):
        grads[n] = g

    outs_g, outs_d, outs_m, outs_v = [], [], [], []
    grads["w_in"] = grads["w_in"][:W_IN_SHARD]
    for name in _NAMES:
        g = grads[name]
        d, m, v = _adamw(p[name], g, pm[name], pv[name], "adamw_" + name)
        shape = args[name].shape
        for outs, t in ((outs_g, g), (outs_d, d), (outs_m, m), (outs_v, v)):
            outs.append(jnp.swapaxes(t[None], 1, 2) if name == "w_in" else t.reshape(shape))
    return (loss, grad_x.reshape(x.shape), *outs_g, *outs_d, *outs_m, *outs_v)
```

```python
import functools

import jax
import jax.numpy as jnp
from jax import lax
from jax.experimental import pallas as pl
from jax.experimental.pallas import tpu as pltpu

F32 = jnp.float32
BF16 = jnp.bfloat16
HI = lax.Precision.HIGHEST
MESH = pl.DeviceIdType.MESH

D = 1024
ATT_PATTERNS = ((128, 1), (512, 4), (2048, 16))
ATT_BLOCK = 128
ATT_WIDTH = 512
N_ATT = 3 * 3 * ATT_WIDTH
N_RW = 3 * D + 64 + 64 + 160
N_RWP = 3456
N_LORA = N_RWP - 3 * D
N_GATE = 2 * D
D_FF = 2816
RMS_EPS = 1e-6
GN_EPS = 64e-5
SCAN_CHUNK = 64
SCAN_PAIRS = 8
NEG = -1e30
VMEM_LIMIT = 48 * 1024 * 1024
HALO = 16

ADAM_LR, ADAM_B1, ADAM_B2, ADAM_EPS, ADAM_WD, ADAM_STEP = 0.001, 0.9, 0.999, 1e-08, 0.01, 10


def _pcall(body, **kw):
    return pl.pallas_call(body, **kw)


def _cparams(sem):
    return pltpu.CompilerParams(dimension_semantics=sem, vmem_limit_bytes=VMEM_LIMIT)


def _div(n, pref, mult):
    best = None
    d = mult
    while d <= min(n, pref):
        if n % d == 0:
            best = d
        d += mult
    return best if best else n


def _dg(a, b, ca, cb):
    return lax.dot_general(a.astype(BF16), b.astype(BF16), (((ca,), (cb,)), ((), ())), preferred_element_type=F32)


@jax.custom_vjp
def _nn(a, b):
    return _dg(a, b, 1, 0)


@jax.custom_vjp
def _nt(a, b):
    return _dg(a, b, 1, 1)


@jax.custom_vjp
def _tn(a, b):
    return _dg(a, b, 0, 0)


_nn.defvjp(lambda a, b: (_nn(a, b), (a, b)), lambda res, g: (_nt(g, res[1]), _tn(res[0], g)))
_nt.defvjp(lambda a, b: (_nt(a, b), (a, b)), lambda res, g: (_nn(g, res[1]), _tn(g, res[0])))
_tn.defvjp(lambda a, b: (_tn(a, b), (a, b)), lambda res, g: (_nt(res[1], g), _nn(res[0], g)))


def _bdg(a, b, ca, cb):
    return lax.dot_general(a.astype(BF16), b.astype(BF16), (((ca,), (cb,)), ((0,), (0,))), preferred_element_type=F32)


@jax.custom_vjp
def _bnn(a, b):
    return _bdg(a, b, 2, 1)


@jax.custom_vjp
def _bnt(a, b):
    return _bdg(a, b, 2, 2)


@jax.custom_vjp
def _btn(a, b):
    return _bdg(a, b, 1, 1)


_bnn.defvjp(lambda a, b: (_bnn(a, b), (a, b)), lambda res, g: (_bnt(g, res[1]), _btn(res[0], g)))
_bnt.defvjp(lambda a, b: (_bnt(a, b), (a, b)), lambda res, g: (_bnn(g, res[1]), _btn(g, res[0])))
_btn.defvjp(lambda a, b: (_btn(a, b), (a, b)), lambda res, g: (_bnt(res[1], g), _bnn(res[0], g)))


def _split2(x):
    hi = x.astype(BF16)
    lo = (x - hi.astype(F32)).astype(BF16)
    return hi, lo


def _hsum_impl(x, e, et):
    eb, etb = e.astype(BF16), et.astype(BF16)
    s = jnp.dot(x.astype(BF16), eb, preferred_element_type=F32)
    shi, slo = _split2(s)
    return jnp.dot(shi, etb, preferred_element_type=F32) + jnp.dot(slo, etb, preferred_element_type=F32)


@jax.custom_vjp
def _hsum(x, e, et):
    return _hsum_impl(x, e, et)


_hsum.defvjp(lambda x, e, et: (_hsum_impl(x, e, et), (e, et)),
             lambda res, g: (_hsum_impl(g, res[0], res[1]), jnp.zeros_like(res[0]), jnp.zeros_like(res[1])))


def _mm(a, b, *, ta=False, tb=False, out_dtype=F32, add=None, b_chip=False, out_chip=False, comm=None, name):
    riding = _NOTHING if comm is None else comm
    nc = riding.n
    if ta:
        kdim, m = a.shape
    else:
        m, kdim = a.shape
    if b_chip:
        n = b.shape[1] if tb else 4 * b.shape[2]
    else:
        n = b.shape[0] if tb else b.shape[1]
    tm, tn, tk = _div(m, 1536, 128), _div(n, 1536, 128), _div(kdim, 2048 if ta else 1408, 128)
    if b_chip and tb:
        tk = kdim // 4
    if (b_chip and not tb) or out_chip:
        tn = n // 4
    nk = kdim // tk
    ca, cb = (0 if ta else 1), (1 if tb else 0)

    nin = 2 if add is None else 3
    gi, gj = m // tm, n // tn

    def body(*refs):
        a_ref, b_ref = refs[0], refs[1]
        add_ref = None if add is None else refs[2]
        o_ref = refs[nin + nc]
        step = (pl.program_id(0) * gj + pl.program_id(1)) * nk + pl.program_id(2)
        before, after = _comm_phases(riding, refs[nin:nin + nc] + refs[nin + nc + 1:nin + 2 * nc + 1]
                                     + refs[nin + 2 * nc + 1 + (nk > 1):], gi * gj * nk, step)
        before()
        part = lax.dot_general(a_ref[...], b_ref[...], (((ca,), (cb,)), ((), ())), preferred_element_type=F32)

        def finish(r):
            if add_ref is not None:
                r = r + add_ref[...]
            o_ref[...] = r.astype(o_ref.dtype)

        if nk == 1:
            finish(part)
            after()
            return
        acc = refs[nin + 2 * nc + 1]
        k = pl.program_id(2)

        @pl.when(k == 0)
        def _():
            acc[...] = part

        @pl.when(k > 0)
        def _():
            acc[...] += part

        @pl.when(k == nk - 1)
        def _():
            finish(acc[...])

        after()

    a_spec = pl.BlockSpec((tk, tm), lambda i, j, k: (k, i)) if ta else pl.BlockSpec((tm, tk), lambda i, j, k: (i, k))
    if b_chip:
        b_spec = (pl.BlockSpec((None, tn, tk), lambda i, j, k: (k, j, 0)) if tb
                  else pl.BlockSpec((None, tk, tn), lambda i, j, k: (j, k, 0)))
    else:
        b_spec = pl.BlockSpec((tn, tk), lambda i, j, k: (j, k)) if tb else pl.BlockSpec((tk, tn), lambda i, j, k: (k, j))
    in_specs = [a_spec, b_spec]
    args = [a, b]
    if add is not None:
        in_specs.append(pl.BlockSpec((tm, tn), lambda i, j, k: (i, j)))
        args.append(add)
    if out_chip:
        out_spec = pl.BlockSpec((None, tm, tn), lambda i, j, k: (j, i, 0))
        out_shape = jax.ShapeDtypeStruct((4, m, tn), out_dtype)
    else:
        out_spec = pl.BlockSpec((tm, tn), lambda i, j, k: (i, j))
        out_shape = jax.ShapeDtypeStruct((m, n), out_dtype)
    res = _pcall(
        body, name=name, grid=(gi, gj, nk), in_specs=in_specs + [_HBM] * nc, out_specs=[out_spec] + [_HBM] * nc,
        out_shape=[out_shape] + riding.out_shape,
        scratch_shapes=([] if nk == 1 else [pltpu.VMEM((tm, tn), F32)]) + riding.sems,
        compiler_params=_cparams(("arbitrary",) * 3 if nc else ("parallel", "parallel", "arbitrary")),
    )(*args, *riding.ins)
    return res[0] if comm is None else (res[0], res[1:])


def _mm_sum(pairs, *, comm, name):
    m, n = pairs[0][0].shape[0], pairs[0][1].shape[1]
    tm, tn = _div(m, 1024, 128), _div(n, 1024, 128)
    tks = [_div(a.shape[1], 1408, 128) for a, _ in pairs]
    nks = [a.shape[1] // tk for (a, _), tk in zip(pairs, tks)]
    offs = [sum(nks[:p]) for p in range(len(pairs))]
    total, npair, nc = sum(nks), len(pairs), comm.n
    gi, gj = m // tm, n // tn

    def body(*refs):
        o_ref, acc = refs[2 * npair + nc], refs[2 * npair + 2 * nc + 1]
        k = pl.program_id(2)
        step = (pl.program_id(0) * gj + pl.program_id(1)) * total + k
        before, after = _comm_phases(comm, refs[2 * npair:2 * npair + nc]
                                     + refs[2 * npair + nc + 1:2 * npair + 2 * nc + 1]
                                     + refs[2 * npair + 2 * nc + 2:], gi * gj * total, step)
        before()
        for p in range(npair):
            def partial_product(p=p):
                part = jnp.dot(refs[2 * p][...], refs[2 * p + 1][...], preferred_element_type=F32)
                if p == 0:
                    @pl.when(k == 0)
                    def _():
                        acc[...] = part

                    @pl.when(k > 0)
                    def _():
                        acc[...] += part
                else:
                    acc[...] += part

            pl.when(jnp.logical_and(k >= offs[p], k < offs[p] + nks[p]))(partial_product)

        @pl.when(k == total - 1)
        def _():
            o_ref[...] = acc[...].astype(o_ref.dtype)

        after()

    def specs(tk, off, nk):
        def kb(k):
            return jnp.clip(k - off, 0, nk - 1)
        return [pl.BlockSpec((tm, tk), lambda i, j, k: (i, kb(k))), pl.BlockSpec((tk, tn), lambda i, j, k: (kb(k), j))]

    in_specs, args = [], []
    for (a, b), tk, off, nk in zip(pairs, tks, offs, nks):
        in_specs += specs(tk, off, nk)
        args += [a, b]
    res = _pcall(
        body, name=name, grid=(gi, gj, total), in_specs=in_specs + [_HBM] * nc,
        out_specs=[pl.BlockSpec((tm, tn), lambda i, j, k: (i, j))] + [_HBM] * nc,
        out_shape=[jax.ShapeDtypeStruct((m, n), BF16)] + comm.out_shape,
        scratch_shapes=[pltpu.VMEM((tm, tn), F32)] + comm.sems,
        compiler_params=_cparams(("arbitrary",) * 3),
    )(*args, *comm.ins)
    return res[0], res[1:]


def _row_spec(br, w, cb):
    return pl.BlockSpec((br, w), lambda i: (i, cb))


def _const_spec(shape):
    return pl.BlockSpec(shape, lambda i: (0,) * len(shape))


def _rows_fwd(fn, rows, consts, outs, *, name, br, acc_shape=None, halo=None):
    s = rows[0][0].shape[0]
    nr, nc = len(rows), len(consts)
    kept = [k for k, o in enumerate(outs) if o is not None]

    def body(*refs):
        xs = [r[...].astype(F32) for r in refs[:nr]]
        cs = [c[...] for c in refs[nr:nr + nc]]
        if halo is not None:
            cs.append(jnp.where(pl.program_id(0) == 0, 0.0, refs[nr + nc][...].astype(F32)))
        res = fn(*xs, *cs)
        orefs = refs[nr + nc + (halo is not None):]
        for j, k in enumerate(kept):
            orefs[j][...] = res[k].astype(orefs[j].dtype)
        if acc_shape is not None:
            acc_ref = orefs[len(kept)]

            @pl.when(pl.program_id(0) == 0)
            def _():
                acc_ref[...] = jnp.zeros_like(acc_ref)

            acc_ref[...] += res[len(outs)]

    in_specs = [_row_spec(br, w, cb) for (_, w, cb) in rows] + [_const_spec(c.shape) for c in consts]
    args = [r[0] for r in rows] + list(consts)
    if halo is not None:
        harr, hw, hcb = rows[halo]
        in_specs.append(pl.BlockSpec((HALO, hw), lambda i: (jnp.maximum(i * (br // HALO) - 1, 0), hcb)))
        args.append(harr)
    out_specs = [_row_spec(br, outs[k][0], 0) for k in kept]
    out_shape = [jax.ShapeDtypeStruct((s, outs[k][0]), outs[k][1]) for k in kept]
    if acc_shape is not None:
        out_specs.append(_const_spec(acc_shape))
        out_shape.append(jax.ShapeDtypeStruct(acc_shape, F32))
    return _pcall(
        body, name=name, grid=(pl.cdiv(s, br),), in_specs=in_specs, out_specs=out_specs, out_shape=out_shape,
        compiler_params=_cparams(("arbitrary",)),
    )(*args)


def _rows_bwd(fn, rows, consts, cots, *, wrt_rows, wrt_consts, drow_dtypes, name, br, unit_cot=False, comm=None):
    comm = _NOTHING if comm is None else comm
    ncomm = comm.n
    nout = len(wrt_rows) + len(wrt_consts)
    s = rows[0][0].shape[0]
    nr, nc = len(rows), len(consts)
    flat_cots = [c for lst in cots for c in lst]
    ncot = len(flat_cots)

    def body(*refs):
        xs = [r[...].astype(F32) for r in refs[:nr]]
        cs = [c[...] for c in refs[nr:nr + nc]]
        cvals = [c[...].astype(F32) for c in refs[nr + nc:nr + nc + ncot]]
        orefs = refs[nr + nc + ncot + ncomm:]
        before, after = _comm_phases(comm, refs[nr + nc + ncot:nr + nc + ncot + ncomm] + orefs[nout:], s // br)
        before()

        def g(*d):
            xs2, cs2 = list(xs), list(cs)
            for j, k in enumerate(wrt_rows):
                xs2[k] = d[j]
            for j, k in enumerate(wrt_consts):
                cs2[k] = d[len(wrt_rows) + j]
            return tuple(fn(*xs2, *cs2))

        prim = [xs[k] for k in wrt_rows] + [cs[k] for k in wrt_consts]
        outs, vjp = jax.vjp(g, *prim)
        ct = []
        pos = 0
        for o, lst in zip(outs, cots):
            if unit_cot:
                ct.append(jnp.ones_like(o))
                continue
            acc = jnp.zeros_like(o)
            for _ in lst:
                acc = acc + cvals[pos]
                pos += 1
            ct.append(acc)
        grads = vjp(tuple(ct))
        for j in range(len(wrt_rows)):
            orefs[j][...] = grads[j].astype(orefs[j].dtype)

        @pl.when(pl.program_id(0) == 0)
        def _():
            for j in range(len(wrt_consts)):
                oref = orefs[len(wrt_rows) + j]
                oref[...] = jnp.zeros_like(oref)

        for j in range(len(wrt_consts)):
            orefs[len(wrt_rows) + j][...] += grads[len(wrt_rows) + j]
        after()

    in_specs = ([_row_spec(br, w, cb) for (_, w, cb) in rows] + [_const_spec(c.shape) for c in consts]
                + [_row_spec(br, w, cb) for (_, w, cb) in flat_cots] + [_HBM] * ncomm)
    out_specs = ([_row_spec(br, rows[k][1], 0) for k in wrt_rows] + [_const_spec(consts[k].shape) for k in wrt_consts]
                 + [_HBM] * ncomm)
    out_shape = ([jax.ShapeDtypeStruct((s, rows[k][1]), dt) for k, dt in zip(wrt_rows, drow_dtypes)]
                 + [jax.ShapeDtypeStruct(consts[k].shape, F32) for k in wrt_consts] + comm.out_shape)
    return _pcall(
        body, name=name, grid=(s // br,), in_specs=in_specs, out_specs=out_specs, out_shape=out_shape,
        scratch_shapes=comm.sems, compiler_params=_cparams(("arbitrary",)),
    )(*[r[0] for r in rows], *consts, *[c[0] for c in flat_cots], *comm.ins)


def _rms(x, w):
    return x * lax.rsqrt(jnp.mean(x * x, axis=-1, keepdims=True) + RMS_EPS) * w


def _softplus(x):
    return jnp.maximum(x, 0.0) + jnp.log(1.0 + jnp.exp(-jnp.abs(x)))


def _f_pre(x, nw, sc, sh):
    return _rms(x, nw) * (1.0 + sc) + sh, x


def _f_pre2(x, o, gt, nw, sc, sh):
    x1 = x + gt * o
    return x1, _rms(x1, nw) * (1.0 + sc) + sh


def _f_fin(x1, f, tgt, gt, nfw):
    y = _rms(x1 + gt * f, nfw)
    return (0.5 * jnp.mean(jnp.square(y - tgt), axis=-1, keepdims=True),)


def _f_comb(o1, o2, o3, l1, l2, l3):
    m = lax.stop_gradient(jnp.maximum(jnp.maximum(l1, l2), l3))
    e1, e2, e3 = jnp.exp(l1 - m), jnp.exp(l2 - m), jnp.exp(l3 - m)
    return ((e1 * o1 + e2 * o2 + e3 * o3) / (e1 + e2 + e3),)


def _f_rwpre(zs, w0, a0, k_k, k_a, wl, e, et):
    r, k, v, zl = zs[:, 0:D], zs[:, D:2 * D], zs[:, 2 * D:3 * D], zs[:, 3 * D:N_RWP]
    lane = lax.broadcasted_iota(jnp.int32, zl.shape, 1)
    t = jnp.where(lane < 64, jnp.tanh(zl), jnp.where(lane < 128, zl, jnp.where(lane < 288, jax.nn.sigmoid(zl), 0.0)))
    lo = _nn(t[:, 0:128], wl[0:128, 0:2 * D])
    g = _nn(t[:, 128:N_LORA], wl[128:N_LORA, 2 * D:3 * D])
    w_log = -_softplus(-(w0 + lo[:, 0:D])) - 0.5
    lw = -jnp.exp(w_log)
    a = jax.nn.sigmoid(a0 + lo[:, D:2 * D])
    k_mod = k * (1.0 + (a - 1.0) * k_a)
    kk = k * k_k
    kk = kk / jnp.maximum(jnp.sqrt(_hsum(kk * kk, e, et)), 1e-12)
    return r, lw, k_mod, v, -kk, kk * a, g


def _f_rwpost(y, r, v, k_mod, g, lnx_w, lnx_b, r_k, e, et):
    mean = _hsum(y, e, et) * (1.0 / 64)
    yc = y - mean
    var = _hsum(yc * yc, e, et) * (1.0 / 64)
    yn = yc * lax.rsqrt(var + GN_EPS) * lnx_w + lnx_b
    bonus = _hsum(r * k_mod * r_k, e, et) * v
    return ((yn + bonus) * g,)


def _f_mix(gi, ya, yr, bg):
    gate = jax.nn.sigmoid(gi + bg)
    return (gate[:, 0:D] * ya + gate[:, D:2 * D] * yr,)


def _f_adamw(w, g, m, v):
    m = ADAM_B1 * m + (1.0 - ADAM_B1) * g
    v = ADAM_B2 * v + (1.0 - ADAM_B2) * jnp.square(g)
    m_hat = m / (1.0 - ADAM_B1 ** ADAM_STEP)
    v_hat = v / (1.0 - ADAM_B2 ** ADAM_STEP)
    return -ADAM_LR * (m_hat / (jnp.sqrt(v_hat) + ADAM_EPS) + ADAM_WD * w), m, v


def _down(x, k):
    row = lax.broadcasted_iota(jnp.int32, x.shape, 0)
    return jnp.where(row < k, 0.0, pltpu.roll(x, k, 0))


def _up(x, k):
    n = x.shape[0]
    row = lax.broadcasted_iota(jnp.int32, x.shape, 0)
    return jnp.where(row >= n - k, 0.0, pltpu.roll(x, n - k, 0))


def _col_spec(s, w, off=0):
    return pl.BlockSpec((s, w), lambda j: (0, j + off))


def _shift_bwd(z, mu, dzs):
    s, n = z.shape

    def body(z_ref, mu_ref, d_ref, dz_ref, dmu_ref):
        zz, d, m = z_ref[...].astype(F32), d_ref[...], mu_ref[...]
        dm = d * m
        dz_ref[...] = (d - dm + _up(dm, 1)).astype(dz_ref.dtype)
        dmu_ref[...] = jnp.sum(d * (_down(zz, 1) - zz), axis=0, keepdims=True)

    return _pcall(
        body, name="shift_bwd", grid=(n // 128,), in_specs=[_col_spec(s, 128), _col_spec(1, 128), _col_spec(s, 128)],
        out_specs=[_col_spec(s, 128), _col_spec(1, 128)],
        out_shape=[jax.ShapeDtypeStruct((s, n), BF16), jax.ShapeDtypeStruct((1, n), F32)],
        compiler_params=_cparams(("parallel",)),
    )(z, mu, dzs)


def _conv3(x, w_ref, b_ref):
    return b_ref[...] + w_ref[0:1, :] * _down(x, 2) + w_ref[1:2, :] * _down(x, 1) + w_ref[2:3, :] * x


def _conv_fwd(u, cw, cb):
    s = u.shape[0]
    nb = D_FF // 128

    def body(ug_ref, uv_ref, wg_ref, wv_ref, bg_ref, bv_ref, o_ref):
        gate = _conv3(ug_ref[...], wg_ref, bg_ref)
        val = _conv3(uv_ref[...], wv_ref, bv_ref)
        o_ref[...] = (gate * jax.nn.sigmoid(gate) * val).astype(o_ref.dtype)

    return _pcall(
        body, name="conv_fwd", grid=(nb,),
        in_specs=[_col_spec(s, 128), _col_spec(s, 128, nb), _col_spec(3, 128), _col_spec(3, 128, nb),
                  _col_spec(1, 128), _col_spec(1, 128, nb)],
        out_specs=_col_spec(s, 128), out_shape=jax.ShapeDtypeStruct((s, D_FF), BF16),
        compiler_params=_cparams(("parallel",)),
    )(u, u, cw, cw, cb, cb)


def _conv_bwd(u, cw, cb, dact):
    s = u.shape[0]
    nb = D_FF // 128

    def half(x, d, w_ref, du_ref, dw_ref, db_ref):
        x1, x2 = _down(x, 1), _down(x, 2)
        du_ref[...] = (w_ref[2:3, :] * d + w_ref[1:2, :] * _up(d, 1) + w_ref[0:1, :] * _up(d, 2)).astype(du_ref.dtype)
        dw_ref[0:1, :] = jnp.sum(d * x2, axis=0, keepdims=True)
        dw_ref[1:2, :] = jnp.sum(d * x1, axis=0, keepdims=True)
        dw_ref[2:3, :] = jnp.sum(d * x, axis=0, keepdims=True)
        db_ref[...] = jnp.sum(d, axis=0, keepdims=True)

    def body(ug_ref, uv_ref, wg_ref, wv_ref, bg_ref, bv_ref, da_ref,
             dug_ref, duv_ref, dwg_ref, dwv_ref, dbg_ref, dbv_ref):
        ug, uv, da = ug_ref[...], uv_ref[...], da_ref[...]
        gate = _conv3(ug, wg_ref, bg_ref)
        val = _conv3(uv, wv_ref, bv_ref)
        sg = jax.nn.sigmoid(gate)
        dgate = da * val * sg * (1.0 + gate * (1.0 - sg))
        dval = da * gate * sg
        half(ug, dgate, wg_ref, dug_ref, dwg_ref, dbg_ref)
        half(uv, dval, wv_ref, duv_ref, dwv_ref, dbv_ref)

    dug, duv, dwg, dwv, dbg, dbv = _pcall(
        body, name="conv_bwd", grid=(nb,),
        in_specs=[_col_spec(s, 128), _col_spec(s, 128, nb), _col_spec(3, 128), _col_spec(3, 128, nb),
                  _col_spec(1, 128), _col_spec(1, 128, nb), _col_spec(s, 128)],
        out_specs=[_col_spec(s, 128), _col_spec(s, 128), _col_spec(3, 128), _col_spec(3, 128),
                   _col_spec(1, 128), _col_spec(1, 128)],
        out_shape=[jax.ShapeDtypeStruct((s, D_FF), BF16), jax.ShapeDtypeStruct((s, D_FF), BF16),
                   jax.ShapeDtypeStruct((3, D_FF), F32), jax.ShapeDtypeStruct((3, D_FF), F32),
                   jax.ShapeDtypeStruct((1, D_FF), F32), jax.ShapeDtypeStruct((1, D_FF), F32)],
        compiler_params=_cparams(("parallel",)),
    )(u, u, cw, cw, cb, cb, dact)
    return (jnp.concatenate([dug, duv], axis=1), jnp.concatenate([dwg, dwv], axis=1),
            jnp.concatenate([dbg, dbv], axis=1))


ATT_BATCH = 4


def _att_batch(q, kp, kc, vp, vc, first):
    ma = lax.broadcasted_iota(jnp.int32, (1, ATT_BLOCK, 128), 2) < 64

    def diag(x):
        return jnp.concatenate([jnp.where(ma, x, 0.0), jnp.where(ma, 0.0, x)], axis=1)

    qi = lax.broadcasted_iota(jnp.int32, (1, ATT_BLOCK, 2 * ATT_BLOCK), 1)
    kj = lax.broadcasted_iota(jnp.int32, (1, ATT_BLOCK, 2 * ATT_BLOCK), 2) & (ATT_BLOCK - 1)
    okp = kj >= qi + jnp.where(first, 2 * ATT_BLOCK, 0)
    okc = kj <= qi
    sp = jnp.where(okp, _bnt(q, diag(kp)) * 0.125, NEG)
    sc = jnp.where(okc, _bnt(q, diag(kc)) * 0.125, NEG)

    def per_head(fn, x):
        return fn(x[..., :ATT_BLOCK]), fn(x[..., ATT_BLOCK:])

    def spread(ab):
        return jnp.concatenate([jnp.broadcast_to(t, t.shape[:2] + (ATT_BLOCK,)) for t in ab], axis=-1)

    row_max = functools.partial(jnp.max, axis=-1, keepdims=True)
    row_sum = functools.partial(jnp.sum, axis=-1, keepdims=True)
    m = [lax.stop_gradient(jnp.maximum(a, b)) for a, b in zip(per_head(row_max, sp), per_head(row_max, sc))]
    pp, pc = jnp.exp(sp - spread(m)), jnp.exp(sc - spread(m))
    den = [a + b for a, b in zip(per_head(row_sum, pp), per_head(row_sum, pc))]
    num = _bnn(pp, diag(vp)) + _bnn(pc, diag(vc))
    out = num / jnp.where(ma, den[0], den[1])
    lse = jnp.where(ma, m[0] + jnp.log(den[0]), m[1] + jnp.log(den[1]))
    return out, jnp.broadcast_to(lse, out.shape)


def _att_pairs_per_step(dil):
    return 4 if dil == 1 else 1


def _att_residues(dil):
    return min(dil, ATT_BATCH // _att_pairs_per_step(dil))


def _att_specs(g, dil):
    rows, pp = ATT_BLOCK * dil, _att_pairs_per_step(dil)

    def cur(slot):
        return pl.BlockSpec((rows, 128 * pp), lambda n, p: (n, (g * 3 + slot) * (4 // pp) + p))

    def prev(slot):
        return pl.BlockSpec((rows, 128 * pp), lambda n, p: (jnp.maximum(n - 1, 0), (g * 3 + slot) * (4 // pp) + p))

    return [cur(0), prev(1), cur(1), prev(2), cur(2)]


def _att_out_spec(dil):
    return pl.BlockSpec((ATT_BLOCK * dil, 128 * _att_pairs_per_step(dil)), lambda n, p: (n, p))


def _att_grid(s, dil):
    return (s // (ATT_BLOCK * dil), 4 // _att_pairs_per_step(dil))


def _att_windows(i, dil):
    res = _att_residues(dil)

    def rows(r):
        return pl.ds(i * res + r, ATT_BLOCK, stride=dil) if dil > 1 else pl.ds(0, ATT_BLOCK)

    return [(rows(r), pl.ds(128 * j, 128)) for j in range(_att_pairs_per_step(dil)) for r in range(res)]


def _att_fwd(att_in, g, dil):
    s = att_in.shape[0]

    def body(q_ref, kp_ref, kc_ref, vp_ref, vc_ref, o_ref, l_ref):
        first = pl.program_id(0) == 0

        def one(i, carry):
            win = _att_windows(i, dil)
            vals = [jnp.stack([ref[w] for w in win]) for ref in (q_ref, kp_ref, kc_ref, vp_ref, vc_ref)]
            o, l = _att_batch(*vals, first)
            for j, w in enumerate(win):
                o_ref[w] = o[j]
                l_ref[w] = l[j]
            return carry

        lax.fori_loop(0, dil // _att_residues(dil), one, 0)

    return _pcall(
        body, name=f"att_fwd{g}", grid=_att_grid(s, dil), in_specs=_att_specs(g, dil),
        out_specs=[_att_out_spec(dil)] * 2, out_shape=[jax.ShapeDtypeStruct((s, ATT_WIDTH), F32)] * 2,
        compiler_params=_cparams(("parallel", "parallel")),
    )(att_in, att_in, att_in, att_in, att_in)


def _att_bwd(att_in, g, dil, do, dl, acc):
    s = att_in.shape[0]

    def body(q_ref, kp_ref, kc_ref, vp_ref, vc_ref, do_ref, dl_ref, dq_ref, dkp_ref, dkc_ref, dvp_ref, dvc_ref):
        first = pl.program_id(0) == 0

        def one(i, carry):
            win = _att_windows(i, dil)
            vals = [jnp.stack([ref[w] for w in win]) for ref in (q_ref, kp_ref, kc_ref, vp_ref, vc_ref)]
            _, vjp = jax.vjp(lambda *a: _att_batch(*a, first), *vals)
            grads = vjp((jnp.stack([do_ref[w] for w in win]), jnp.stack([dl_ref[w] for w in win])))
            for ref, gr in zip((dq_ref, dkp_ref, dkc_ref, dvp_ref, dvc_ref), grads):
                for j, w in enumerate(win):
                    ref[w] = gr[j]
            return carry

        lax.fori_loop(0, dil // _att_residues(dil), one, 0)

    dq, dkp, dkc, dvp, dvc = _pcall(
        body, name=f"att_bwd{g}", grid=_att_grid(s, dil), in_specs=_att_specs(g, dil) + [_att_out_spec(dil)] * 2,
        out_specs=[_att_out_spec(dil)] * 5, out_shape=[jax.ShapeDtypeStruct((s, ATT_WIDTH), F32)] * 5,
        compiler_params=_cparams(("parallel", "parallel")),
    )(att_in, att_in, att_in, att_in, att_in, do, dl)

    unit, rb = ATT_BLOCK * dil, 1024
    steps = s // rb
    within = unit < rb

    def shifted(cur_ref, next_ref, has_next):
        nxt = jnp.where(has_next, next_ref[...], 0.0)
        return jnp.concatenate([cur_ref[unit:, :], nxt], axis=0) if within else nxt

    def cbody(dq_ref, dkc_ref, dkp_ref, dkn_ref, dvc_ref, dvp_ref, dvn_ref, *rest):
        o_ref = rest[-1]
        has_next = pl.program_id(0) + (1 if within else unit // rb) < steps
        o_ref[:, 0:ATT_WIDTH] = dq_ref[...].astype(BF16)
        o_ref[:, ATT_WIDTH:2 * ATT_WIDTH] = (dkc_ref[...] + shifted(dkp_ref, dkn_ref, has_next)).astype(BF16)
        o_ref[:, 2 * ATT_WIDTH:3 * ATT_WIDTH] = (dvc_ref[...] + shifted(dvp_ref, dvn_ref, has_next)).astype(BF16)

    cur = pl.BlockSpec((rb, ATT_WIDTH), lambda i: (i, 0))
    if within:
        nxt = pl.BlockSpec((unit, ATT_WIDTH), lambda i: (jnp.minimum((i + 1) * (rb // unit), s // unit - 1), 0))
    else:
        nxt = pl.BlockSpec((rb, ATT_WIDTH), lambda i: (jnp.minimum(i + unit // rb, steps - 1), 0))
    carried = [] if acc is None else [acc]
    return _pcall(
        cbody, name=f"att_bwd_sum{g}", grid=(steps,),
        in_specs=[cur, cur, cur, nxt, cur, cur, nxt] + [pl.BlockSpec(memory_space=pl.ANY)] * len(carried),
        out_specs=pl.BlockSpec((rb, 3 * ATT_WIDTH), lambda i: (i, g)),
        out_shape=jax.ShapeDtypeStruct((s, N_ATT), BF16), input_output_aliases={7: 0} if carried else {},
        compiler_params=_cparams(("parallel",)),
    )(dq, dkc, dkp, dkp, dvc, dvp, dvp, *carried)


def _unit_lower_inverse_impl(n):
    eye = (lax.broadcasted_iota(jnp.int32, (1,) + n.shape[1:], 1)
           == lax.broadcasted_iota(jnp.int32, (1,) + n.shape[1:], 2))
    t = jnp.where(eye, 1.0, 0.0) + n
    pw = n
    for _ in range(5):
        pw = _bnn(pw, pw)
        t = t + _bnn(t, pw)
    return t


@jax.custom_vjp
def _unit_lower_inverse(n):
    return _unit_lower_inverse_impl(n)


def _unit_lower_inverse_fwd(n):
    t = _unit_lower_inverse_impl(n)
    return t, t


_unit_lower_inverse.defvjp(_unit_lower_inverse_fwd, lambda t, g: (_bnt(_btn(t, g), t),))


@jax.custom_vjp
def _known_inverse(n, t):
    return t


_known_inverse.defvjp(lambda n, t: (t, t), lambda t, g: (_bnt(_btn(t, g), t), jnp.zeros_like(t)))


def _scan_chunk(r, lw, k, v, a, b, s0, inverse):
    c = SCAN_CHUNK
    p = s0.shape[0]
    ri = lax.broadcasted_iota(jnp.int32, (c, c), 0)
    ci = lax.broadcasted_iota(jnp.int32, (c, c), 1)
    cum = jnp.dot((ci <= ri).astype(F32), lw, precision=HI, preferred_element_type=F32)
    tot = jnp.sum(lw, axis=0, keepdims=True)
    ma = (lax.broadcasted_iota(jnp.int32, (c, 128 * p), 1) & 127) < 64

    def pairs(x):
        return jnp.concatenate([x[None, :, 128 * j:128 * (j + 1)] for j in range(p)], axis=0)

    def stack(x):
        return jnp.concatenate([pairs(jnp.where(ma, x, 0.0)), pairs(jnp.where(ma, 0.0, x))], axis=1)

    einv, eend = jnp.exp(-cum), jnp.exp(tot - cum)
    ra, aa = stack(r * jnp.exp(cum)), stack(a * jnp.exp(cum - lw))
    bi, ki, be, ke, vs = stack(b * einv), stack(k * einv), stack(b * eend), stack(k * eend), stack(v)
    r2 = lax.broadcasted_iota(jnp.int32, (1, 2 * c, 2 * c), 1)
    c2 = lax.broadcasted_iota(jnp.int32, (1, 2 * c, 2 * c), 2)
    same = (r2 >= c) == (c2 >= c)
    strict = jnp.logical_and(same, c2 < r2)
    incl = jnp.logical_and(same, c2 <= r2)
    s0 = jnp.where(same, s0, 0.0)
    prod = _bnt(jnp.concatenate([aa, ra], axis=1), jnp.concatenate([bi, ki], axis=1))
    a_ab = jnp.where(strict, prod[:, :2 * c, :2 * c], 0.0)
    a_ak = jnp.where(strict, prod[:, :2 * c, 2 * c:], 0.0)
    a_rb = jnp.where(incl, prod[:, 2 * c:, :2 * c], 0.0)
    a_rk = jnp.where(incl, prod[:, 2 * c:, 2 * c:], 0.0)
    t = inverse(a_ab)
    u = _bnn(t, _bnt(aa, s0) + _bnn(a_ak, vs))
    uv = jnp.concatenate([u, vs], axis=1)
    ys = _bnt(ra, s0) + _bnn(jnp.concatenate([a_rb, a_rk], axis=2), uv)
    s1 = s0 * pairs(jnp.exp(tot)) + _btn(uv, jnp.concatenate([be, ke], axis=1))
    y3 = ys[:, :c] + ys[:, c:]
    return (jnp.concatenate([y3[j] for j in range(p)], axis=1), s1), t


def _scan_specs(rev, n):
    def at(i):
        return n - 1 - i if rev else i

    def cm(cb):
        return pl.BlockSpec((SCAN_CHUNK, D), lambda i: (at(i), cb))

    return cm, pl.BlockSpec((1, SCAN_PAIRS, 128, 128), lambda i: (at(i), 0, 0, 0))


def _comm_phases(comm, refs, n, step=None):
    k = comm.n
    srcs, outs, sems = refs[:k], refs[k:2 * k], refs[2 * k:]
    i = pl.program_id(0) if step is None else step

    def before():
        @pl.when(i == 0)
        def _():
            comm.first(srcs, outs, sems)

    def after():
        if comm.mid is not None:
            @pl.when(i == (3 * n) // 4)
            def _():
                comm.mid(srcs, outs, sems)

        @pl.when(i == n - 1)
        def _():
            comm.last(srcs, outs, sems)

    return before, after


def _scan_fwd(zs, lw, km, aa, bb, comm):
    s = zs.shape[0]
    n = s // SCAN_CHUNK
    cm, st = _scan_specs(False, n)
    k = comm.n

    def body(*refs):
        r_ref, lw_ref, k_ref, v_ref, a_ref, b_ref = refs[:6]
        y_ref, s0_ref, t_ref = refs[6 + k:9 + k]
        state = refs[9 + 2 * k]
        before, after = _comm_phases(comm, refs[6:6 + k] + refs[9 + k:9 + 2 * k] + refs[10 + 2 * k:], n)
        before()

        @pl.when(pl.program_id(0) == 0)
        def _():
            state[...] = jnp.zeros_like(state)

        s0 = state[...]
        s0_ref[0] = s0
        (y, s1), t = _scan_chunk(*[ref[...] for ref in (r_ref, lw_ref, k_ref, v_ref, a_ref, b_ref)], s0,
                                 _unit_lower_inverse)
        y_ref[...] = y
        t_ref[0] = t.astype(BF16)
        state[...] = s1
        after()

    per_chunk = (n, SCAN_PAIRS, 128, 128)
    res = _pcall(
        body, name="scan_fwd", grid=(n,), in_specs=[cm(0), cm(0), cm(0), cm(2), cm(0), cm(0)] + [_HBM] * k,
        out_specs=[cm(0), st, st] + [_HBM] * k,
        out_shape=[jax.ShapeDtypeStruct((s, D), F32), jax.ShapeDtypeStruct(per_chunk, F32),
                   jax.ShapeDtypeStruct(per_chunk, BF16)] + comm.out_shape,
        scratch_shapes=[pltpu.VMEM((SCAN_PAIRS, 128, 128), F32)] + comm.sems,
        compiler_params=_cparams(("arbitrary",)),
    )(zs, lw, km, zs, aa, bb, *comm.ins)
    return res[0], res[1], res[2], res[3:]


def _scan_bwd(zs, lw, km, aa, bb, s0s, ts, dy, comm):
    s = zs.shape[0]
    n = s // SCAN_CHUNK
    cm, st = _scan_specs(True, n)
    k = comm.n

    def body(*refs):
        r_ref, lw_ref, k_ref, v_ref, a_ref, b_ref, s0_ref, t_ref, dy_ref = refs[:9]
        douts = refs[9 + k:15 + k]
        dstate = refs[15 + 2 * k]
        before, after = _comm_phases(comm, refs[9:9 + k] + refs[15 + k:15 + 2 * k] + refs[16 + 2 * k:], n)
        before()

        @pl.when(pl.program_id(0) == 0)
        def _():
            dstate[...] = jnp.zeros_like(dstate)

        t = t_ref[0].astype(F32)
        prim = [ref[...] for ref in (r_ref, lw_ref, k_ref, v_ref, a_ref, b_ref)] + [s0_ref[0]]
        _, vjp, _ = jax.vjp(lambda *p: _scan_chunk(*p, lambda nil: _known_inverse(nil, t)), *prim, has_aux=True)
        grads = vjp((dy_ref[...], dstate[...]))
        for ref, gr in zip(douts, grads[:6]):
            ref[...] = gr
        dstate[...] = grads[6]
        after()

    res = _pcall(
        body, name="scan_bwd", grid=(n,),
        in_specs=[cm(0), cm(0), cm(0), cm(2), cm(0), cm(0), st, st, cm(0)] + [_HBM] * k,
        out_specs=[cm(0)] * 6 + [_HBM] * k, out_shape=[jax.ShapeDtypeStruct((s, D), F32)] * 6 + comm.out_shape,
        scratch_shapes=[pltpu.VMEM((SCAN_PAIRS, 128, 128), F32)] + comm.sems,
        compiler_params=_cparams(("arbitrary",)),
    )(zs, lw, km, zs, aa, bb, s0s, ts, dy, *comm.ins)
    return res[:6], res[6:]


_HBM = pl.BlockSpec(memory_space=pltpu.HBM)


def _me():
    return lax.axis_index("x"), lax.axis_index("y"), lax.axis_index("c")


def _allgather8(src, name):
    def body(src_ref, out_ref, ssem, rsem, lsem):
        x, y, c = _me()
        me = 4 * x + 2 * y + c
        local = pltpu.make_async_copy(src_ref, out_ref.at[me], lsem)
        local.start()
        peers = []
        for k in range(1, 8):
            peers.append(((1 - x) if k & 4 else x, (1 - y) if k & 2 else y, (1 - c) if k & 1 else c))
        sends = []
        for k, peer in enumerate(peers):
            cp = pltpu.make_async_remote_copy(src_ref, out_ref.at[me], ssem.at[k], rsem.at[k], device_id=peer,
                                              device_id_type=MESH)
            cp.start()
            sends.append(cp)
        for k, (px, py, pc) in enumerate(peers):
            pltpu.make_async_remote_copy(src_ref, out_ref.at[4 * px + 2 * py + pc], ssem.at[k], rsem.at[k],
                                         device_id=(px, py, pc), device_id_type=MESH).wait_recv()
        for cp in sends:
            cp.wait_send()
        local.wait()

    return _pcall(
        body, name=name, in_specs=[_HBM], out_specs=_HBM, out_shape=jax.ShapeDtypeStruct((8,) + src.shape, src.dtype),
        scratch_shapes=[pltpu.SemaphoreType.DMA((7,)), pltpu.SemaphoreType.DMA((7,)), pltpu.SemaphoreType.DMA],
    )(src)


def _other_chips(x, y):
    return [(1 - x, y), (x, 1 - y), (1 - x, 1 - y)]


def _remote(src, dst, ssem, rsem, to):
    return pltpu.make_async_remote_copy(src, dst, ssem, rsem, device_id=to, device_id_type=MESH)


class _GatherWeights:
    def __init__(self, shards):
        self.ins = list(shards)
        n = self.n = len(shards)
        self.out_shape = [jax.ShapeDtypeStruct((4,) + t.shape, t.dtype) for t in shards]
        self.sems = [pltpu.SemaphoreType.DMA((6 * n,)), pltpu.SemaphoreType.DMA((6 * n,)),
                     pltpu.SemaphoreType.DMA((n,)), pltpu.SemaphoreType.DMA((n,))]

    def _copies(self, srcs, outs, sems):
        ssem, rsem, lsem, osem = sems
        x, y, c = _me()
        me = 2 * x + y
        own, ici, landed, passed, passed_in = [], [], [], [], []
        for a in range(self.n):
            h = self.ins[a].shape[0] // 2
            mine, other = pl.ds(c * h, h), pl.ds((1 - c) * h, h)
            own.append(_remote(srcs[a], outs[a].at[me], lsem.at[a], osem.at[a], (x, y, 1 - c)))
            for k, (px, py) in enumerate(_other_chips(x, y)):
                s1, r1, s2, r2 = ssem.at[6 * a + k], rsem.at[6 * a + k], ssem.at[6 * a + 3 + k], rsem.at[6 * a + 3 + k]
                got, got_sib = outs[a].at[2 * px + py, mine], outs[a].at[2 * px + py, other]
                ici.append(_remote(srcs[a].at[mine], outs[a].at[me, mine], s1, r1, (px, py, c)))
                landed.append(_remote(got, got, s1, r1, (px, py, c)))
                passed.append(_remote(got, got, s2, r2, (x, y, 1 - c)))
                passed_in.append(_remote(got_sib, got_sib, s2, r2, (x, y, 1 - c)))
        return own, ici, landed, passed, passed_in

    def first(self, srcs, outs, sems):
        own, ici, _, _, _ = self._copies(srcs, outs, sems)
        for cp in own + ici:
            cp.start()

    def mid(self, srcs, outs, sems):
        _, _, landed, passed, _ = self._copies(srcs, outs, sems)
        for arrived, onward in zip(landed, passed):
            arrived.wait_recv()
            onward.start()

    def last(self, srcs, outs, sems):
        own, ici, _, passed, passed_in = self._copies(srcs, outs, sems)
        for cp in passed_in:
            cp.wait_recv()
        for cp in ici + passed:
            cp.wait_send()
        for cp in own:
            cp.wait()


class _ScatterToChips:
    def __init__(self, parts):
        self.ins = list(parts)
        n = self.n = len(parts)
        self.out_shape = [jax.ShapeDtypeStruct(t.shape, t.dtype) for t in parts]
        self.sems = [pltpu.SemaphoreType.DMA((3 * n,)), pltpu.SemaphoreType.DMA((3 * n,)), pltpu.SemaphoreType.DMA((n,))]

    def _copies(self, srcs, outs, sems):
        ssem, rsem, lsem = sems
        x, y, c = _me()
        me = 2 * x + y
        own, out, landed = [], [], []
        for a in range(self.n):
            own.append(pltpu.make_async_copy(srcs[a].at[me], outs[a].at[me], lsem.at[a]))
            for k, (px, py) in enumerate(_other_chips(x, y)):
                dst = outs[a].at[2 * px + py]
                out.append(_remote(srcs[a].at[2 * px + py], outs[a].at[me], ssem.at[3 * a + k], rsem.at[3 * a + k],
                                   (px, py, c)))
                landed.append(_remote(dst, dst, ssem.at[3 * a + k], rsem.at[3 * a + k], (px, py, c)))
        return own, out, landed

    def first(self, srcs, outs, sems):
        own, out, _ = self._copies(srcs, outs, sems)
        for cp in own + out:
            cp.start()

    mid = None

    def last(self, srcs, outs, sems):
        own, out, landed = self._copies(srcs, outs, sems)
        for cp in landed:
            cp.wait_recv()
        for cp in own:
            cp.wait()
        for cp in out:
            cp.wait_send()


def _run_comm(comm, name):
    n = comm.n

    def body(*refs):
        srcs, outs, sems = refs[:n], refs[n:2 * n], refs[2 * n:]
        comm.first(srcs, outs, sems)
        if comm.mid is not None:
            comm.mid(srcs, outs, sems)
        comm.last(srcs, outs, sems)

    return _pcall(body, name=name, in_specs=[_HBM] * n, out_specs=[_HBM] * n, out_shape=comm.out_shape,
                  scratch_shapes=comm.sems)(*comm.ins)


class _NoComm:
    n, ins, out_shape, sems, mid = 0, [], [], [], None

    def first(self, srcs, outs, sems):
        pass

    def last(self, srcs, outs, sems):
        pass


_NOTHING = _NoComm()


class _SiblingHalves:
    mid = None

    def __init__(self, grads):
        self.ins = list(grads)
        n = self.n = len(grads)
        self.out_shape = [jax.ShapeDtypeStruct((4, t.shape[1] // 2, t.shape[2]), t.dtype) for t in grads]
        self.sems = [pltpu.SemaphoreType.DMA((n,)), pltpu.SemaphoreType.DMA((n,))]

    def _copies(self, srcs, outs, sems):
        ssem, rsem = sems
        x, y, c = _me()
        copies = []
        for a in range(self.n):
            h = self.ins[a].shape[1] // 2
            copies.append(_remote(srcs[a].at[:, pl.ds((1 - c) * h, h)], outs[a], ssem.at[a], rsem.at[a], (x, y, 1 - c)))
        return copies

    def first(self, srcs, outs, sems):
        for cp in self._copies(srcs, outs, sems):
            cp.start()

    def last(self, srcs, outs, sems):
        for cp in self._copies(srcs, outs, sems):
            cp.wait()


def _reduce_finish(reds, name):
    n = len(reds)

    def body(*refs):
        outs = refs[n:2 * n]
        ssem, rsem = refs[2 * n:]
        x, y, c = _me()
        copies = []
        for a in range(n):
            h = reds[a].shape[0] // 2
            mine = outs[a].at[pl.ds(c * h, h)]
            copies.append(_remote(mine, mine, ssem.at[a], rsem.at[a], (x, y, 1 - c)))
        for cp in copies:
            cp.start()
        for a in range(n):
            h = reds[a].shape[0] // 2
            dst = outs[a].at[pl.ds((1 - c) * h, h)]
            _remote(dst, dst, ssem.at[a], rsem.at[a], (x, y, 1 - c)).wait_recv()
        for cp in copies:
            cp.wait_send()

    return _pcall(
        body, name=name, in_specs=[_HBM] * n, out_specs=[_HBM] * n,
        out_shape=[jax.ShapeDtypeStruct(t.shape, t.dtype) for t in reds],
        input_output_aliases={a: a for a in range(n)},
        scratch_shapes=[pltpu.SemaphoreType.DMA((n,)), pltpu.SemaphoreType.DMA((n,))],
    )(*reds)


def _half_sum(fn, full, halves, out_full, out_dtype, core, name):
    p, h, c = (halves[0].shape if halves else (full[0].shape[0], full[0].shape[1] // 2, full[0].shape[2]))
    br = _div(h, max(16, (1 << 19) // (p * c)), 16)
    nb = h // br
    mine3 = pl.BlockSpec((p, br, c), lambda i, core_ref: (0, core_ref[0] * nb + i, 0))
    half3 = pl.BlockSpec((p, br, c), lambda i, core_ref: (0, i, 0))

    def body(core_ref, *refs):
        refs[-1][...] = fn(*[t[...].astype(F32) for t in refs[:-1]]).astype(out_dtype)

    if out_full:
        out_spec = pl.BlockSpec((br, c), lambda i, core_ref: (core_ref[0] * nb + i, 0))
        out_shape = jax.ShapeDtypeStruct((2 * h, c), out_dtype)
    else:
        out_spec, out_shape = half3, jax.ShapeDtypeStruct((p, h, c), out_dtype)
    return _pcall(
        body, name=name,
        grid_spec=pltpu.PrefetchScalarGridSpec(
            num_scalar_prefetch=1, grid=(nb,), in_specs=[mine3] * len(full) + [half3] * len(halves),
            out_specs=out_spec),
        out_shape=out_shape, compiler_params=_cparams(("parallel",)),
    )(core, *full, *halves)


def _ada_fwd(c_all, w, b):
    def body(c_ref, w_ref, b_ref, o_ref):
        o_ref[...] = jnp.dot(c_ref[...], w_ref[...], precision=HI, preferred_element_type=F32) + b_ref[...]

    return _pcall(body, name="ada_fwd", out_shape=jax.ShapeDtypeStruct((c_all.shape[0], w.shape[1]), F32),
                  compiler_params=pltpu.CompilerParams(vmem_limit_bytes=VMEM_LIMIT))(c_all, w, b)


def _ada_bwd(c_all_t, d):
    def body(c_ref, d_ref, o_ref):
        o_ref[...] = jnp.dot(c_ref[...], d_ref[...], precision=HI, preferred_element_type=F32)

    return _pcall(body, name="ada_bwd", out_shape=jax.ShapeDtypeStruct((c_all_t.shape[0], d.shape[1]), F32),
                  compiler_params=pltpu.CompilerParams(vmem_limit_bytes=VMEM_LIMIT))(c_all_t, d)


def _sum_lead(x, name):
    p, r, n = x.shape
    br = _div(r, 512, 8)

    def body(x_ref, o_ref):
        acc = x_ref[0]
        for j in range(1, p):
            acc = acc + x_ref[j]
        o_ref[...] = acc

    return _pcall(
        body, name=name, grid=(r // br,), in_specs=[pl.BlockSpec((p, br, n), lambda i: (0, i, 0))],
        out_specs=pl.BlockSpec((br, n), lambda i: (i, 0)), out_shape=jax.ShapeDtypeStruct((r, n), F32),
        compiler_params=_cparams(("parallel",)),
    )(x)


def _adamw(w, g, m, v, name):
    shape = w.shape
    cols = shape[-1]
    w2, g2, m2, v2 = [t.reshape(-1, cols) for t in (w, g, m, v)]
    rows = w2.shape[0]
    pref = max(8, (1 << 19) // cols // 8 * 8)
    br = _div(rows, pref, 8)
    if rows // br > 64:
        br = pref
    outs = _rows_fwd(_f_adamw, [(t, cols, 0) for t in (w2, g2, m2, v2)], [], [(cols, F32)] * 3, name=name, br=br)
    return [o.reshape(shape) for o in outs]


_BIG = (("w_in", 1), ("w_up", 1), ("w_down", 0), ("w_o", 0), ("w_rwkv_out", 0), ("w_att_out", 1), ("w2", 1), ("a2", 1),
        ("g2", 1))


_NEEDED_FIRST = ("w_in", "w_att_out", "w2", "a2", "g2")
_NEEDED_LATER = ("w_up", "w_down", "w_o", "w_rwkv_out")
_DONE_EARLY = ("w_up", "w_down", "w_o", "w_rwkv_out", "w_att_out")
_DONE_LATE = ("w_in", "w2", "a2", "g2")


def _cols_joined(t):
    return jnp.concatenate([t[j] for j in range(4)], axis=1)


def _cols_split(t):
    n = t.shape[1] // 4
    return jnp.stack([t[:, j * n:(j + 1) * n] for j in range(4)])


W_IN_SHARD = (N_ATT + N_RW + N_GATE) // 4
W_IN_PAD = 2560


def _row_window(parts, lo, hi):
    out, pos = [], 0
    for t, w in parts:
        a, b = max(lo, pos), min(hi, pos + w)
        if a < b:
            out.append(t[a - pos:b - pos])
        pos += w
    return out[0] if len(out) == 1 else jnp.concatenate(out, axis=0)


def _rows_joined(t):
    return t.reshape(4 * t.shape[1], t.shape[2])


def _rows_split(t):
    return t.reshape(4, t.shape[0] // 4, t.shape[1])


def _step_to_scan(x, tgt, ada, wts):
    sh1, sc1, gt1, sh2, sc2, gt2 = ada
    br = 256
    grp = lax.broadcasted_iota(jnp.int32, (D, 128), 0) // 64 == lax.broadcasted_iota(jnp.int32, (D, 128), 1)
    e = grp.astype(F32)
    et = e.T
    w_in = [(wts["w_in"][j], W_IN_SHARD) for j in range(4)]
    w_att = _row_window(w_in, 0, N_ATT)
    w_rw = jnp.concatenate([_row_window(w_in, N_ATT, N_ATT + N_RW), jnp.zeros((N_RWP - N_RW, D), BF16)], axis=0)
    w_gate = _row_window(w_in, N_ATT + N_RW, N_ATT + N_RW + N_GATE)
    mu = jnp.pad(wts["mu_shift"], ((0, 0), (0, N_RWP - N_RW)))
    wl = jnp.zeros((N_LORA, 3 * D), F32)
    wl = wl.at[0:64, 0:D].set(_cols_joined(wts["w2"]).astype(F32))
    wl = wl.at[64:128, D:2 * D].set(_cols_joined(wts["a2"]).astype(F32))
    wl = wl.at[128:288, 2 * D:3 * D].set(_cols_joined(wts["g2"]).astype(F32))
    pre1_c = [wts["norm1_w"], sc1, sh1]
    (h1,) = _rows_fwd(_f_pre, [(x, D, 0)], pre1_c, [(D, BF16), None], name="pre1_fwd", br=2 * br)
    att_in = _mm(h1, w_att, tb=True, name="mm_att_in")
    z = _mm(h1, w_rw, tb=True, out_dtype=BF16, name="mm_rw_in")
    gate_in = _mm(h1, w_gate, tb=True, out_dtype=BF16, name="mm_gate_in")
    att_o, att_l = [], []
    for g, (_, dil) in enumerate(ATT_PATTERNS):
        o, l = _att_fwd(att_in, g, dil)
        att_o.append(o)
        att_l.append(l)
    comb_rows = [(t, ATT_WIDTH, 0) for t in att_o + att_l]
    (att,) = _rows_fwd(_f_comb, comb_rows, [], [(ATT_WIDTH, BF16)], name="comb_fwd", br=2 * br)
    y_att = _mm(att, wts["w_att_out"], b_chip=True, out_dtype=BF16, name="mm_att_out")
    rwpre_c = [wts["w0"], wts["a0"], wts["k_k"], wts["k_a"], wl, e, et]

    def shift_and_rwpre(zz, *rest):
        consts, mu_row, before = rest[:-2], rest[-2], rest[-1]
        last = jnp.sum(jnp.where(lax.broadcasted_iota(jnp.int32, before.shape, 0) == HALO - 1, before, 0.0), axis=0,
                       keepdims=True)
        row = lax.broadcasted_iota(jnp.int32, zz.shape, 0)
        zprev = jnp.where(row == 0, last, pltpu.roll(zz, 1, 0))
        shifted = zz + (zprev - zz) * mu_row
        return (shifted,) + tuple(_f_rwpre(shifted, *consts))

    zs, lw, km, aa, bb, gg = _rows_fwd(
        shift_and_rwpre, [(z, N_RWP, 0)], rwpre_c + [mu],
        [(N_RWP, F32), None, (D, F32), (D, F32), None, (D, F32), (D, F32), (D, F32)], name="rwpre_fwd", br=br, halo=0)
    return dict(x=x, tgt=tgt, wts=wts, br=br, e=e, et=et, gt1=gt1, sc2=sc2, sh2=sh2, gt2=gt2, w_att=w_att, w_rw=w_rw,
                w_gate=w_gate, mu=mu, pre1_c=pre1_c, h1=h1, att_in=att_in, z=z, gate_in=gate_in, comb_rows=comb_rows,
                att=att, y_att=y_att, zs=zs, rwpre_c=rwpre_c, lw=lw, km=km, aa=aa, bb=bb, gg=gg)


def _step_between_scans(st, y_raw, late):
    x, tgt, wts, br, e, et = st["x"], st["tgt"], st["wts"], st["br"], st["e"], st["et"]
    zs, km, gg, gate_in, y_att, att = st["zs"], st["km"], st["gg"], st["gate_in"], st["y_att"], st["att"]
    comb_rows, att_in = st["comb_rows"], st["att_in"]
    gt1, sc2, sh2, gt2 = st["gt1"], st["sc2"], st["sh2"], st["gt2"]
    w_up, w_ao = late["w_up"], wts["w_att_out"]
    w_down, w_o, w_ro = _rows_joined(late["w_down"]), _rows_joined(late["w_o"]), _rows_joined(late["w_rwkv_out"])
    post_rows = [(y_raw, D, 0), (zs, D, 0), (zs, D, 2), (km, D, 0), (gg, D, 0)]
    post_c = [wts["lnx_w"], wts["lnx_b"], wts["r_k"], e, et]
    (rw_out,) = _rows_fwd(_f_rwpost, post_rows, post_c, [(D, BF16)], name="rwpost_fwd", br=br)
    y_rw = _mm(rw_out, w_ro, out_dtype=BF16, name="mm_rw_out")
    mix_rows = [(gate_in, N_GATE, 0), (y_att, D, 0), (y_rw, D, 0)]
    (mix,) = _rows_fwd(_f_mix, mix_rows, [wts["b_gate"]], [(D, BF16)], name="mix_fwd", br=2 * br)
    o = _mm(mix, w_o, out_dtype=BF16, name="mm_o")
    pre2_c = [gt1, wts["norm2_w"], sc2, sh2]
    x1, h2 = _rows_fwd(_f_pre2, [(x, D, 0), (o, D, 0)], pre2_c, [(D, F32), (D, BF16)], name="pre2_fwd", br=2 * br)
    u = _mm(h2, w_up, b_chip=True, name="mm_up")
    act = _conv_fwd(u, wts["conv_w"], wts["conv_b"])
    f = _mm(act, w_down, out_dtype=BF16, name="mm_down")
    fin_rows = [(x1, D, 0), (f, D, 0), (tgt, D, 0)]
    fin_c = [gt2, wts["norm_f_w"]]

    def fin_fwd(*a):
        (l,) = _f_fin(*a)
        return (jnp.broadcast_to(jnp.sum(l, axis=0, keepdims=True), (8, 128)),)

    (loss_acc,) = _rows_fwd(fin_fwd, fin_rows, fin_c, [], name="fin_fwd", br=2 * br, acc_shape=(8, 128))

    gw = {}
    dx1a, df, d_gt2, gw["norm_f_w"] = _rows_bwd(
        _f_fin, fin_rows, fin_c, [[]], wrt_rows=[0, 1], wrt_consts=[0, 1], drow_dtypes=[F32, BF16],
        name="fin_bwd", br=2 * br, unit_cot=True)
    dact = _mm(df, w_down, tb=True, name="mm_dact")
    gw["w_down"] = _rows_split(_mm(act, df, ta=True, out_dtype=BF16, name="mm_dw_down"))
    du, gw["conv_w"], gw["conv_b"] = _conv_bwd(u, wts["conv_w"], wts["conv_b"], dact)
    dh2 = _mm(du, w_up, tb=True, b_chip=True, out_dtype=BF16, name="mm_dh2")
    gw["w_up"] = _mm(h2, du, ta=True, out_chip=True, out_dtype=BF16, name="mm_dw_up")
    dxa, do, d_gt1, gw["norm2_w"], d_sc2, d_sh2 = _rows_bwd(
        _f_pre2, [(x, D, 0), (o, D, 0)], pre2_c, [[(dx1a, D, 0)], [(dh2, D, 0)]], wrt_rows=[0, 1],
        wrt_consts=[0, 1, 2, 3], drow_dtypes=[F32, BF16], name="pre2_bwd", br=2 * br)
    dmix = _mm(do, w_o, tb=True, out_dtype=BF16, name="mm_dmix")
    gw["w_o"] = _rows_split(_mm(mix, do, ta=True, out_dtype=BF16, name="mm_dw_o"))
    dgate, dya, dyr, gw["b_gate"] = _rows_bwd(
        _f_mix, mix_rows, [wts["b_gate"]], [[(dmix, D, 0)]], wrt_rows=[0, 1, 2], wrt_consts=[0],
        drow_dtypes=[BF16] * 3, name="mix_bwd", br=2 * br)
    datt = _mm(dya, w_ao, tb=True, b_chip=True, out_dtype=BF16, name="mm_datt")
    gw["w_att_out"] = _mm(att, dya, ta=True, out_chip=True, out_dtype=BF16, name="mm_dw_att_out")
    drw = _mm(dyr, w_ro, tb=True, out_dtype=BF16, name="mm_drw")
    gw["w_rwkv_out"] = _rows_split(_mm(rw_out, dyr, ta=True, out_dtype=BF16, name="mm_dw_rw_out"))
    dcomb = _rows_bwd(_f_comb, comb_rows, [], [[(datt, ATT_WIDTH, 0)]], wrt_rows=list(range(6)), wrt_consts=[],
                      drow_dtypes=[F32] * 6, name="comb_bwd", br=2 * br)
    datt_in = None
    for g, (_, dil) in enumerate(ATT_PATTERNS):
        datt_in = _att_bwd(att_in, g, dil, dcomb[g], dcomb[3 + g], datt_in)
    dy_raw, dr_p, dv_p, dkm_p, dgg, gw["lnx_w"], gw["lnx_b"], gw["r_k"], *recv_early = _rows_bwd(
        _f_rwpost, post_rows, post_c, [[(drw, D, 0)]], wrt_rows=[0, 1, 2, 3, 4], wrt_consts=[0, 1, 2],
        drow_dtypes=[F32] * 5, name="rwpost_bwd", br=br, comm=_SiblingHalves([gw[n] for n in _DONE_EARLY]))
    st.update(loss=loss_acc[0, 0], gw=gw, dxa=dxa, dgate=dgate, datt_in=datt_in,
              dy_raw=dy_raw, dr_p=dr_p, dv_p=dv_p, dkm_p=dkm_p, dgg=dgg, d_ada_late=(d_gt1, d_sh2, d_sc2, d_gt2),
              recv_early=recv_early)
    return st


def _chip_parts(grads, recv, names, core):
    return [_half_sum(lambda a, b: a + b, [g], [r], False, BF16, core, "reduce_add2_" + n)
            for g, r, n in zip(grads, recv, names)]


def _step_after_scan(st, scan_grads, core):
    x, br, gw, h1, zs = st["x"], st["br"], st["gw"], st["h1"], st["zs"]
    dr_s, dlw, dkm_s, dv_s, daa, dbb = scan_grads
    pre_cots = [[(st["dr_p"], D, 0), (dr_s, D, 0)], [(dlw, D, 0)], [(st["dkm_p"], D, 0), (dkm_s, D, 0)],
                [(st["dv_p"], D, 0), (dv_s, D, 0)], [(daa, D, 0)], [(dbb, D, 0)], [(st["dgg"], D, 0)]]
    dzs, gw["w0"], gw["a0"], gw["k_k"], gw["k_a"], dwl = _rows_bwd(
        _f_rwpre, [(zs, N_RWP, 0)], st["rwpre_c"], pre_cots, wrt_rows=[0], wrt_consts=[0, 1, 2, 3, 4],
        drow_dtypes=[F32], name="rwpre_bwd", br=128)
    gw["w2"], gw["a2"] = _cols_split(dwl[0:64, 0:D]), _cols_split(dwl[64:128, D:2 * D])
    gw["g2"] = _cols_split(dwl[128:288, 2 * D:3 * D])
    dz, dmu = _shift_bwd(st["z"], st["mu"], dzs)
    gw["mu_shift"] = dmu[:, :N_RW]
    datt_in, dgate = st["datt_in"], st["dgate"]
    dw_in = [(_mm(datt_in, h1, ta=True, out_dtype=BF16, name="mm_dw_att"), N_ATT),
             (_mm(dz, h1, ta=True, out_dtype=BF16, name="mm_dw_rw"), N_RW),
             (_mm(dgate, h1, ta=True, out_dtype=BF16, name="mm_dw_gate"), N_GATE)]
    slabs = []
    for j in range(4):
        slabs += [_row_window(dw_in, j * W_IN_SHARD, (j + 1) * W_IN_SHARD), jnp.zeros((W_IN_PAD - W_IN_SHARD, D), BF16)]
    gw["w_in"] = jnp.concatenate(slabs, axis=0).reshape(4, W_IN_PAD, D)
    late = [gw[n] for n in _DONE_LATE]
    parts = _chip_parts(late, _run_comm(_SiblingHalves(late), "reduce_sib_late"), _DONE_LATE, core)
    dh1, slots_late = _mm_sum([(datt_in, st["w_att"]), (dz, st["w_rw"]), (dgate, st["w_gate"])],
                              comm=_ScatterToChips(parts), name="mm_dh1")
    grad_x, gw["norm1_w"], d_sc1, d_sh1 = _rows_bwd(
        _f_pre, [(x, D, 0)], st["pre1_c"], [[(dh1, D, 0)], [(st["dxa"], D, 0)]], wrt_rows=[0], wrt_consts=[0, 1, 2],
        drow_dtypes=[F32], name="pre1_bwd", br=2 * br)
    d_gt1, d_sh2, d_sc2, d_gt2 = st["d_ada_late"]
    return st["loss"], grad_x, (d_sh1, d_sc1, d_gt1, d_sh2, d_sc2, d_gt2), gw, slots_late


_SMALL = ("b_ada", "norm1_w", "b_gate", "mu_shift", "w0", "a0", "k_k", "k_a", "r_k", "lnx_w", "lnx_b", "norm2_w",
          "conv_b", "norm_f_w")
_NAMES = ("w_ada", "b_ada", "norm1_w", "w_in", "b_gate", "mu_shift", "w0", "w2", "a0", "a2", "g2", "k_k", "k_a", "r_k",
          "lnx_w", "lnx_b", "w_att_out", "w_rwkv_out", "w_o", "norm2_w", "w_up", "conv_w", "conv_b", "w_down",
          "norm_f_w")


def kernel(x, c, w_ada, b_ada, norm1_w, w_in, b_gate, mu_shift, w0, w2, a0, a2, g2, k_k, k_a, r_k, lnx_w, lnx_b, w_att_out, w_rwkv_out, w_o, norm2_w, w_up, conv_w, conv_b, w_down, norm_f_w, loss_target, m_w_ada, m_b_ada, m_norm1_w, m_w_in, m_b_gate, m_mu_shift, m_w0, m_w2, m_a0, m_a2, m_g2, m_k_k, m_k_a, m_r_k, m_lnx_w, m_lnx_b, m_w_att_out, m_w_rwkv_out, m_w_o, m_norm2_w, m_w_up, m_conv_w, m_conv_b, m_w_down, m_norm_f_w, v_w_ada, v_b_ada, v_norm1_w, v_w_in, v_b_gate, v_mu_shift, v_w0, v_w2, v_a0, v_a2, v_g2, v_k_k, v_k_a, v_r_k, v_lnx_w, v_lnx_b, v_w_att_out, v_w_rwkv_out, v_w_o, v_norm2_w, v_w_up, v_conv_w, v_conv_b, v_w_down, v_norm_f_w):
    args = dict(locals())
    p, pm, pv = {}, {}, {}
    for name in _NAMES:
        for dst, key in ((p, name), (pm, "m_" + name), (pv, "v_" + name)):
            t = args[key]
            if name == "w_in":
                dst[name] = jnp.swapaxes(t, 1, 2)[0]
            else:
                dst[name] = t.reshape(1, -1) if name in ("r_k", "norm_f_w") else t.reshape(t.shape[-2], t.shape[-1])
    xi, yi, ci = _me()
    chip = 2 * xi + yi
    dev = 4 * xi + 2 * yi + ci
    x2, tgt = x[0], loss_target[0]

    n_cw = 3 * (2 * D_FF // 4)
    vec = jnp.concatenate([c.reshape(-1), p["conv_w"].reshape(-1), jnp.zeros((8 * D - D - n_cw,), F32)]).reshape(8, D)
    g0 = _allgather8(vec, "gather_c").reshape(8, 8 * D)
    c_all = g0[:, :D]
    conv_w_full = jnp.concatenate([g0[2 * j, D:D + n_cw].reshape(3, -1) for j in range(4)], axis=1)
    n_ada = 6 * D // 4
    b_ada_sh = lax.dynamic_slice(p["b_ada"], (0, chip * n_ada), (1, n_ada))
    ada_sh = _ada_fwd(c_all, p["w_ada"], b_ada_sh)
    ga = _allgather8(ada_sh, "gather_ada")
    ada_all = jnp.concatenate([ga[2 * j] for j in range(4)], axis=1)
    ada_row = lax.dynamic_slice(ada_all, (dev, 0), (1, 6 * D))
    ada = [ada_row[:, j * D:(j + 1) * D] for j in range(6)]

    big = [n for n, _ in _BIG]
    shard = {n: p[n].astype(BF16) for n in big}
    shard["w_in"] = jnp.pad(shard["w_in"], ((0, W_IN_PAD - W_IN_SHARD), (0, 0)))
    wts = dict(zip(_NEEDED_FIRST, _run_comm(_GatherWeights([shard[n] for n in _NEEDED_FIRST]), "gather_w")))
    for n in _SMALL:
        wts[n] = p[n]
    wts["conv_w"] = conv_w_full
    core = ci.reshape(1).astype(jnp.int32)

    st = _step_to_scan(x2, tgt, ada, wts)
    y_raw, s0s, inverses, late = _scan_fwd(st["zs"], st["lw"], st["km"], st["aa"], st["bb"],
                                           _GatherWeights([shard[n] for n in _NEEDED_LATER]))
    st = _step_between_scans(st, y_raw, dict(zip(_NEEDED_LATER, late)))
    early = _chip_parts([st["gw"][n] for n in _DONE_EARLY], st["recv_early"], _DONE_EARLY, core)
    scan_grads, slots_early = _scan_bwd(st["zs"], st["lw"], st["km"], st["aa"], st["bb"], s0s, inverses,
                                        st["dy_raw"], _ScatterToChips(early))
    loss_part, grad_x, d_ada, gw, slots_late = _step_after_scan(st, scan_grads, core)

    small = [jnp.concatenate(d_ada, axis=1)] + [gw[n] for n in _SMALL[1:]] + [gw["conv_w"], loss_part.reshape(1, 1)]
    sizes = [t.size for t in small]
    flat = jnp.concatenate([t.reshape(-1) for t in small])
    npad = (-flat.shape[0]) % (8 * D)
    srows = (flat.shape[0] + npad) // D
    flat = jnp.concatenate([flat, jnp.zeros((npad,), F32)]).reshape(srows, D)
    parts = _allgather8(flat, "gather_small")
    tot = _sum_lead(parts, "sum_small").reshape(-1)
    pieces, pos = [], 0
    for sz in sizes:
        pieces.append(tot[pos:pos + sz])
        pos += sz
    grads = {}
    for n, piece in zip(_SMALL, pieces[:len(_SMALL)]):
        grads[n] = piece.reshape(p[n].shape)
    conv_w_grad = pieces[len(_SMALL)].reshape(3, 2 * D_FF)
    grads["conv_w"] = lax.dynamic_slice(conv_w_grad, (0, chip * (n_cw // 3)), (3, n_cw // 3))
    loss = pieces[-1][0]
    d_ada_all = parts[:, :6].reshape(8, 6 * D)
    grads["w_ada"] = _ada_bwd(c_all.T, lax.dynamic_slice(d_ada_all, (0, chip * n_ada), (8, n_ada)))

    order = _DONE_EARLY + _DONE_LATE
    reds = [_half_sum(lambda t: t[0] + t[1] + t[2] + t[3], [], [t], True, F32, core, "reduce_add4_" + n)
            for n, t in zip(order, list(slots_early) + list(slots_late))]
    for n, g in zip(order, _reduce_finish(reds, "reduce_sib2")):
        grads[n] = g

    outs_g, outs_d, outs_m, outs_v = [], [], [], []
    grads["w_in"] = grads["w_in"][:W_IN_SHARD]
    for name in _NAMES:
        g = grads[name]
        d, m, v = _adamw(p[name], g, pm[name], pv[name], "adamw_" + name)
        shape = args[name].shape
        for outs, t in ((outs_g, g), (outs_d, d), (outs_m, m), (outs_v, v)):
            outs.append(jnp.swapaxes(t[None], 1, 2) if name == "w_in" else t.reshape(shape))
    return (loss, grad_x.reshape(x.shape), *outs_g, *outs_d, *outs_m, *outs_v)
```

```python
import functools
import math

import jax
import jax.numpy as jnp
from jax import lax
from jax.experimental import pallas as pl
from jax.experimental.pallas import tpu as pltpu

F32 = jnp.float32
BF16 = jnp.bfloat16
HI = lax.Precision.HIGHEST
MESH = pl.DeviceIdType.MESH

D = 1024
ATT_PATTERNS = ((128, 1), (512, 4), (2048, 16))
ATT_BLOCK = 128
ATT_WIDTH = 512
N_ATT = 3 * 3 * ATT_WIDTH
N_RW = 3 * D + 64 + 64 + 160
N_RWP = 3456
N_LORA = N_RWP - 3 * D
N_GATE = 2 * D
D_FF = 2816
RMS_EPS = 1e-6
GN_EPS = 64e-5
SCAN_CHUNK = 64
SCAN_PAIRS = 8
NEG = -1e30
VMEM_LIMIT = 48 * 1024 * 1024
HALO = 16

ADAM_LR, ADAM_B1, ADAM_B2, ADAM_EPS, ADAM_WD, ADAM_STEP = 0.001, 0.9, 0.999, 1e-08, 0.01, 10


def _pcall(body, **kw):
    return pl.pallas_call(body, **kw)


def _cparams(sem):
    return pltpu.CompilerParams(dimension_semantics=sem, vmem_limit_bytes=VMEM_LIMIT)


def _div(n, pref, mult):
    best = None
    d = mult
    while d <= min(n, pref):
        if n % d == 0:
            best = d
        d += mult
    return best if best else n


def _dg(a, b, ca, cb):
    return lax.dot_general(a.astype(BF16), b.astype(BF16), (((ca,), (cb,)), ((), ())), preferred_element_type=F32)


@jax.custom_vjp
def _nn(a, b):
    return _dg(a, b, 1, 0)


@jax.custom_vjp
def _nt(a, b):
    return _dg(a, b, 1, 1)


@jax.custom_vjp
def _tn(a, b):
    return _dg(a, b, 0, 0)


_nn.defvjp(lambda a, b: (_nn(a, b), (a, b)), lambda res, g: (_nt(g, res[1]), _tn(res[0], g)))
_nt.defvjp(lambda a, b: (_nt(a, b), (a, b)), lambda res, g: (_nn(g, res[1]), _tn(g, res[0])))
_tn.defvjp(lambda a, b: (_tn(a, b), (a, b)), lambda res, g: (_nt(res[1], g), _nn(res[0], g)))


def _bdg(a, b, ca, cb):
    return lax.dot_general(a.astype(BF16), b.astype(BF16), (((ca,), (cb,)), ((0,), (0,))), preferred_element_type=F32)


@jax.custom_vjp
def _bnn(a, b):
    return _bdg(a, b, 2, 1)


@jax.custom_vjp
def _bnt(a, b):
    return _bdg(a, b, 2, 2)


@jax.custom_vjp
def _btn(a, b):
    return _bdg(a, b, 1, 1)


_bnn.defvjp(lambda a, b: (_bnn(a, b), (a, b)), lambda res, g: (_bnt(g, res[1]), _btn(res[0], g)))
_bnt.defvjp(lambda a, b: (_bnt(a, b), (a, b)), lambda res, g: (_bnn(g, res[1]), _btn(g, res[0])))
_btn.defvjp(lambda a, b: (_btn(a, b), (a, b)), lambda res, g: (_bnt(res[1], g), _bnn(res[0], g)))


def _hsum_impl(x, e, et):
    eb, etb = e.astype(BF16), et.astype(BF16)
    s = jnp.dot(x.astype(BF16), eb, preferred_element_type=F32)
    return jnp.dot(s.astype(BF16), etb, preferred_element_type=F32)


@jax.custom_vjp
def _hsum(x, e, et):
    return _hsum_impl(x, e, et)


_hsum.defvjp(lambda x, e, et: (_hsum_impl(x, e, et), (e, et)),
             lambda res, g: (_hsum_impl(g, res[0], res[1]), jnp.zeros_like(res[0]), jnp.zeros_like(res[1])))


def _mm(a, b, *, ta=False, tb=False, out_dtype=F32, add=None, b_chip=False, out_chip=False, comm=None, name):
    riding = _NOTHING if comm is None else comm
    nc = riding.n
    if ta:
        kdim, m = a.shape
    else:
        m, kdim = a.shape
    if b_chip:
        n = b.shape[1] if tb else 4 * b.shape[2]
    else:
        n = b.shape[0] if tb else b.shape[1]
    tm, tn, tk = _div(m, 1536, 128), _div(n, 1536, 128), _div(kdim, 2048 if ta else 1408, 128)
    if b_chip and tb:
        tk = kdim // 4
    if (b_chip and not tb) or out_chip:
        tn = n // 4
    nk = kdim // tk
    ca, cb = (0 if ta else 1), (1 if tb else 0)

    nin = 2 if add is None else 3
    gi, gj = m // tm, n // tn

    def body(*refs):
        a_ref, b_ref = refs[0], refs[1]
        add_ref = None if add is None else refs[2]
        o_ref = refs[nin + nc]
        step = (pl.program_id(0) * gj + pl.program_id(1)) * nk + pl.program_id(2)
        before, after = _comm_phases(riding, refs[nin:nin + nc] + refs[nin + nc + 1:nin + 2 * nc + 1]
                                     + refs[nin + 2 * nc + 1 + (nk > 1):], gi * gj * nk, step)
        before()
        part = lax.dot_general(a_ref[...], b_ref[...], (((ca,), (cb,)), ((), ())), preferred_element_type=F32)

        def finish(r):
            if add_ref is not None:
                r = r + add_ref[...]
            o_ref[...] = r.astype(o_ref.dtype)

        if nk == 1:
            finish(part)
            after()
            return
        acc = refs[nin + 2 * nc + 1]
        k = pl.program_id(2)

        @pl.when(k == 0)
        def _():
            acc[...] = part

        @pl.when(k > 0)
        def _():
            acc[...] += part

        @pl.when(k == nk - 1)
        def _():
            finish(acc[...])

        after()

    a_spec = pl.BlockSpec((tk, tm), lambda i, j, k: (k, i)) if ta else pl.BlockSpec((tm, tk), lambda i, j, k: (i, k))
    if b_chip:
        b_spec = (pl.BlockSpec((None, tn, tk), lambda i, j, k: (k, j, 0)) if tb
                  else pl.BlockSpec((None, tk, tn), lambda i, j, k: (j, k, 0)))
    else:
        b_spec = pl.BlockSpec((tn, tk), lambda i, j, k: (j, k)) if tb else pl.BlockSpec((tk, tn), lambda i, j, k: (k, j))
    in_specs = [a_spec, b_spec]
    args = [a, b]
    if add is not None:
        in_specs.append(pl.BlockSpec((tm, tn), lambda i, j, k: (i, j)))
        args.append(add)
    if out_chip:
        out_spec = pl.BlockSpec((None, tm, tn), lambda i, j, k: (j, i, 0))
        out_shape = jax.ShapeDtypeStruct((4, m, tn), out_dtype)
    else:
        out_spec = pl.BlockSpec((tm, tn), lambda i, j, k: (i, j))
        out_shape = jax.ShapeDtypeStruct((m, n), out_dtype)
    res = _pcall(
        body, name=name, grid=(gi, gj, nk), in_specs=in_specs + [_HBM] * nc, out_specs=[out_spec] + [_HBM] * nc,
        out_shape=[out_shape] + riding.out_shape,
        scratch_shapes=([] if nk == 1 else [pltpu.VMEM((tm, tn), F32)]) + riding.sems,
        compiler_params=_cparams(("arbitrary",) * 3 if nc else ("parallel", "parallel", "arbitrary")),
    )(*args, *riding.ins)
    return res[0] if comm is None else (res[0], res[1:])


def _mm_sum(pairs, *, comm, name):
    m, n = pairs[0][0].shape[0], pairs[0][1].shape[1]
    tm, tn = _div(m, 1024, 128), _div(n, 1024, 128)
    tks = [_div(a.shape[1], 1408, 128) for a, _ in pairs]
    nks = [a.shape[1] // tk for (a, _), tk in zip(pairs, tks)]
    offs = [sum(nks[:p]) for p in range(len(pairs))]
    total, npair, nc = sum(nks), len(pairs), comm.n
    gi, gj = m // tm, n // tn

    def body(*refs):
        o_ref, acc = refs[2 * npair + nc], refs[2 * npair + 2 * nc + 1]
        k = pl.program_id(2)
        step = (pl.program_id(0) * gj + pl.program_id(1)) * total + k
        before, after = _comm_phases(comm, refs[2 * npair:2 * npair + nc]
                                     + refs[2 * npair + nc + 1:2 * npair + 2 * nc + 1]
                                     + refs[2 * npair + 2 * nc + 2:], gi * gj * total, step)
        before()
        for p in range(npair):
            def partial_product(p=p):
                part = jnp.dot(refs[2 * p][...], refs[2 * p + 1][...], preferred_element_type=F32)
                if p == 0:
                    @pl.when(k == 0)
                    def _():
                        acc[...] = part

                    @pl.when(k > 0)
                    def _():
                        acc[...] += part
                else:
                    acc[...] += part

            pl.when(jnp.logical_and(k >= offs[p], k < offs[p] + nks[p]))(partial_product)

        @pl.when(k == total - 1)
        def _():
            o_ref[...] = acc[...].astype(o_ref.dtype)

        after()

    def specs(tk, off, nk):
        def kb(k):
            return jnp.clip(k - off, 0, nk - 1)
        return [pl.BlockSpec((tm, tk), lambda i, j, k: (i, kb(k))), pl.BlockSpec((tk, tn), lambda i, j, k: (kb(k), j))]

    in_specs, args = [], []
    for (a, b), tk, off, nk in zip(pairs, tks, offs, nks):
        in_specs += specs(tk, off, nk)
        args += [a, b]
    res = _pcall(
        body, name=name, grid=(gi, gj, total), in_specs=in_specs + [_HBM] * nc,
        out_specs=[pl.BlockSpec((tm, tn), lambda i, j, k: (i, j))] + [_HBM] * nc,
        out_shape=[jax.ShapeDtypeStruct((m, n), BF16)] + comm.out_shape,
        scratch_shapes=[pltpu.VMEM((tm, tn), F32)] + comm.sems,
        compiler_params=_cparams(("arbitrary",) * 3),
    )(*args, *comm.ins)
    return res[0], res[1:]


def _row_spec(br, w, cb):
    return pl.BlockSpec((br, w), lambda i: (i, cb))


def _const_spec(shape):
    return pl.BlockSpec(shape, lambda i: (0,) * len(shape))


def _rows_fwd(fn, rows, consts, outs, *, name, br, acc_shape=None, halo=None):
    s = rows[0][0].shape[0]
    nr, nc = len(rows), len(consts)
    kept = [k for k, o in enumerate(outs) if o is not None]

    def body(*refs):
        xs = [r[...].astype(F32) for r in refs[:nr]]
        cs = [c[...] for c in refs[nr:nr + nc]]
        if halo is not None:
            cs.append(jnp.where(pl.program_id(0) == 0, 0.0, refs[nr + nc][...].astype(F32)))
        res = fn(*xs, *cs)
        orefs = refs[nr + nc + (halo is not None):]
        for j, k in enumerate(kept):
            orefs[j][...] = res[k].astype(orefs[j].dtype)
        if acc_shape is not None:
            acc_ref = orefs[len(kept)]

            @pl.when(pl.program_id(0) == 0)
            def _():
                acc_ref[...] = jnp.zeros_like(acc_ref)

            acc_ref[...] += res[len(outs)]

    in_specs = [_row_spec(br, w, cb) for (_, w, cb) in rows] + [_const_spec(c.shape) for c in consts]
    args = [r[0] for r in rows] + list(consts)
    if halo is not None:
        harr, hw, hcb = rows[halo]
        in_specs.append(pl.BlockSpec((HALO, hw), lambda i: (jnp.maximum(i * (br // HALO) - 1, 0), hcb)))
        args.append(harr)
    out_specs = [_row_spec(br, outs[k][0], 0) for k in kept]
    out_shape = [jax.ShapeDtypeStruct((s, outs[k][0]), outs[k][1]) for k in kept]
    if acc_shape is not None:
        out_specs.append(_const_spec(acc_shape))
        out_shape.append(jax.ShapeDtypeStruct(acc_shape, F32))
    return _pcall(
        body, name=name, grid=(pl.cdiv(s, br),), in_specs=in_specs, out_specs=out_specs, out_shape=out_shape,
        compiler_params=_cparams(("arbitrary",)),
    )(*args)


def _rows_bwd(fn, rows, consts, cots, *, wrt_rows, wrt_consts, drow_dtypes, name, br, unit_cot=False, comm=None):
    comm = _NOTHING if comm is None else comm
    ncomm = comm.n
    nout = len(wrt_rows) + len(wrt_consts)
    s = rows[0][0].shape[0]
    nr, nc = len(rows), len(consts)
    flat_cots = [c for lst in cots for c in lst]
    ncot = len(flat_cots)

    def body(*refs):
        xs = [r[...].astype(F32) for r in refs[:nr]]
        cs = [c[...] for c in refs[nr:nr + nc]]
        cvals = [c[...].astype(F32) for c in refs[nr + nc:nr + nc + ncot]]
        orefs = refs[nr + nc + ncot + ncomm:]
        before, after = _comm_phases(comm, refs[nr + nc + ncot:nr + nc + ncot + ncomm] + orefs[nout:], s // br)
        before()

        def g(*d):
            xs2, cs2 = list(xs), list(cs)
            for j, k in enumerate(wrt_rows):
                xs2[k] = d[j]
            for j, k in enumerate(wrt_consts):
                cs2[k] = d[len(wrt_rows) + j]
            return tuple(fn(*xs2, *cs2))

        prim = [xs[k] for k in wrt_rows] + [cs[k] for k in wrt_consts]
        outs, vjp = jax.vjp(g, *prim)
        ct = []
        pos = 0
        for o, lst in zip(outs, cots):
            if unit_cot:
                ct.append(jnp.ones_like(o))
                continue
            acc = jnp.zeros_like(o)
            for _ in lst:
                acc = acc + cvals[pos]
                pos += 1
            ct.append(acc)
        grads = vjp(tuple(ct))
        for j in range(len(wrt_rows)):
            orefs[j][...] = grads[j].astype(orefs[j].dtype)

        @pl.when(pl.program_id(0) == 0)
        def _():
            for j in range(len(wrt_consts)):
                oref = orefs[len(wrt_rows) + j]
                oref[...] = jnp.zeros_like(oref)

        for j in range(len(wrt_consts)):
            orefs[len(wrt_rows) + j][...] += grads[len(wrt_rows) + j]
        after()

    in_specs = ([_row_spec(br, w, cb) for (_, w, cb) in rows] + [_const_spec(c.shape) for c in consts]
                + [_row_spec(br, w, cb) for (_, w, cb) in flat_cots] + [_HBM] * ncomm)
    out_specs = ([_row_spec(br, rows[k][1], 0) for k in wrt_rows] + [_const_spec(consts[k].shape) for k in wrt_consts]
                 + [_HBM] * ncomm)
    out_shape = ([jax.ShapeDtypeStruct((s, rows[k][1]), dt) for k, dt in zip(wrt_rows, drow_dtypes)]
                 + [jax.ShapeDtypeStruct(consts[k].shape, F32) for k in wrt_consts] + comm.out_shape)
    return _pcall(
        body, name=name, grid=(s // br,), in_specs=in_specs, out_specs=out_specs, out_shape=out_shape,
        scratch_shapes=comm.sems, compiler_params=_cparams(("arbitrary",)),
    )(*[r[0] for r in rows], *consts, *[c[0] for c in flat_cots], *comm.ins)


def _rms(x, w):
    return x * lax.rsqrt(jnp.mean(x * x, axis=-1, keepdims=True) + RMS_EPS) * w


def _f_pre(x, nw, sc, sh):
    return _rms(x, nw) * (1.0 + sc) + sh, x


def _f_pre2(x, o, gt, nw, sc, sh):
    x1 = x + gt * o
    return x1, _rms(x1, nw) * (1.0 + sc) + sh


def _f_fin(x1, f, tgt, gt, nfw):
    y = _rms(x1 + gt * f, nfw)
    return (0.5 * jnp.mean(jnp.square(y - tgt), axis=-1, keepdims=True),)


def _f_comb(o1, o2, o3, l1, l2, l3):
    m = lax.stop_gradient(jnp.maximum(jnp.maximum(l1, l2), l3))
    e1, e2, e3 = jnp.exp(l1 - m), jnp.exp(l2 - m), jnp.exp(l3 - m)
    return ((e1 * o1 + e2 * o2 + e3 * o3) / (e1 + e2 + e3),)


def _f_rwpre(zs, w0, a0, k_k, k_a, wl, e, et):
    r, k, v, zl = zs[:, 0:D], zs[:, D:2 * D], zs[:, 2 * D:3 * D], zs[:, 3 * D:N_RWP]
    lane = lax.broadcasted_iota(jnp.int32, zl.shape, 1)
    t = jnp.where(lane < 64, jnp.tanh(zl), jnp.where(lane < 128, zl, jnp.where(lane < 288, jax.nn.sigmoid(zl), 0.0)))
    lo = _nn(t[:, 0:128], wl[0:128, 0:2 * D])
    g = _nn(t[:, 128:N_LORA], wl[128:N_LORA, 2 * D:3 * D])
    lw = -math.exp(-0.5) * jax.nn.sigmoid(w0 + lo[:, 0:D])
    a = jax.nn.sigmoid(a0 + lo[:, D:2 * D])
    k_mod = k * (1.0 + (a - 1.0) * k_a)
    kk = k * k_k
    kk = kk / jnp.maximum(jnp.sqrt(_hsum(kk * kk, e, et)), 1e-12)
    return r, lw, k_mod, v, -kk, kk * a, g


def _f_rwpost(y, r, v, k_mod, g, lnx_w, lnx_b, r_k, e, et):
    mean = _hsum(y, e, et) * (1.0 / 64)
    yc = y - mean
    var = _hsum(yc * yc, e, et) * (1.0 / 64)
    yn = yc * lax.rsqrt(var + GN_EPS) * lnx_w + lnx_b
    bonus = _hsum(r * k_mod * r_k, e, et) * v
    return ((yn + bonus) * g,)


def _f_mix(gi, ya, yr, bg):
    gate = jax.nn.sigmoid(gi + bg)
    return (gate[:, 0:D] * ya + gate[:, D:2 * D] * yr,)


def _f_adamw(w, g, m, v):
    m = ADAM_B1 * m + (1.0 - ADAM_B1) * g
    v = ADAM_B2 * v + (1.0 - ADAM_B2) * jnp.square(g)
    m_hat = m / (1.0 - ADAM_B1 ** ADAM_STEP)
    v_hat = v / (1.0 - ADAM_B2 ** ADAM_STEP)
    return -ADAM_LR * (m_hat / (jnp.sqrt(v_hat) + ADAM_EPS) + ADAM_WD * w), m, v


def _down(x, k):
    row = lax.broadcasted_iota(jnp.int32, x.shape, 0)
    return jnp.where(row < k, 0.0, pltpu.roll(x, k, 0))


def _up(x, k):
    n = x.shape[0]
    row = lax.broadcasted_iota(jnp.int32, x.shape, 0)
    return jnp.where(row >= n - k, 0.0, pltpu.roll(x, n - k, 0))


def _col_spec(s, w, off=0):
    return pl.BlockSpec((s, w), lambda j: (0, j + off))


def _shift_bwd(z, mu, dzs):
    s, n = z.shape

    def body(z_ref, mu_ref, d_ref, dz_ref, dmu_ref):
        zz, d, m = z_ref[...].astype(F32), d_ref[...], mu_ref[...]
        dm = d * m
        dz_ref[...] = (d - dm + _up(dm, 1)).astype(dz_ref.dtype)
        dmu_ref[...] = jnp.sum(d * (_down(zz, 1) - zz), axis=0, keepdims=True)

    return _pcall(
        body, name="shift_bwd", grid=(n // 128,), in_specs=[_col_spec(s, 128), _col_spec(1, 128), _col_spec(s, 128)],
        out_specs=[_col_spec(s, 128), _col_spec(1, 128)],
        out_shape=[jax.ShapeDtypeStruct((s, n), BF16), jax.ShapeDtypeStruct((1, n), F32)],
        compiler_params=_cparams(("parallel",)),
    )(z, mu, dzs)


def _conv3(x, w_ref, b_ref):
    return b_ref[...] + w_ref[0:1, :] * _down(x, 2) + w_ref[1:2, :] * _down(x, 1) + w_ref[2:3, :] * x


def _conv_fwd(u, cw, cb):
    s = u.shape[0]
    nb = D_FF // 128

    def body(ug_ref, uv_ref, wg_ref, wv_ref, bg_ref, bv_ref, o_ref):
        gate = _conv3(ug_ref[...], wg_ref, bg_ref)
        val = _conv3(uv_ref[...], wv_ref, bv_ref)
        o_ref[...] = (gate * jax.nn.sigmoid(gate) * val).astype(o_ref.dtype)

    return _pcall(
        body, name="conv_fwd", grid=(nb,),
        in_specs=[_col_spec(s, 128), _col_spec(s, 128, nb), _col_spec(3, 128), _col_spec(3, 128, nb),
                  _col_spec(1, 128), _col_spec(1, 128, nb)],
        out_specs=_col_spec(s, 128), out_shape=jax.ShapeDtypeStruct((s, D_FF), BF16),
        compiler_params=_cparams(("parallel",)),
    )(u, u, cw, cw, cb, cb)


def _conv_bwd(u, cw, cb, dact):
    s = u.shape[0]
    nb = D_FF // 128

    def half(x, d, w_ref, du_ref, dw_ref, db_ref):
        x1, x2 = _down(x, 1), _down(x, 2)
        du_ref[...] = (w_ref[2:3, :] * d + w_ref[1:2, :] * _up(d, 1) + w_ref[0:1, :] * _up(d, 2)).astype(du_ref.dtype)
        dw_ref[0:1, :] = jnp.sum(d * x2, axis=0, keepdims=True)
        dw_ref[1:2, :] = jnp.sum(d * x1, axis=0, keepdims=True)
        dw_ref[2:3, :] = jnp.sum(d * x, axis=0, keepdims=True)
        db_ref[...] = jnp.sum(d, axis=0, keepdims=True)

    def body(ug_ref, uv_ref, wg_ref, wv_ref, bg_ref, bv_ref, da_ref,
             dug_ref, duv_ref, dwg_ref, dwv_ref, dbg_ref, dbv_ref):
        ug, uv, da = ug_ref[...], uv_ref[...], da_ref[...]
        gate = _conv3(ug, wg_ref, bg_ref)
        val = _conv3(uv, wv_ref, bv_ref)
        sg = jax.nn.sigmoid(gate)
        dgate = da * val * sg * (1.0 + gate * (1.0 - sg))
        dval = da * gate * sg
        half(ug, dgate, wg_ref, dug_ref, dwg_ref, dbg_ref)
        half(uv, dval, wv_ref, duv_ref, dwv_ref, dbv_ref)

    dug, duv, dwg, dwv, dbg, dbv = _pcall(
        body, name="conv_bwd", grid=(nb,),
        in_specs=[_col_spec(s, 128), _col_spec(s, 128, nb), _col_spec(3, 128), _col_spec(3, 128, nb),
                  _col_spec(1, 128), _col_spec(1, 128, nb), _col_spec(s, 128)],
        out_specs=[_col_spec(s, 128), _col_spec(s, 128), _col_spec(3, 128), _col_spec(3, 128),
                   _col_spec(1, 128), _col_spec(1, 128)],
        out_shape=[jax.ShapeDtypeStruct((s, D_FF), BF16), jax.ShapeDtypeStruct((s, D_FF), BF16),
                   jax.ShapeDtypeStruct((3, D_FF), F32), jax.ShapeDtypeStruct((3, D_FF), F32),
                   jax.ShapeDtypeStruct((1, D_FF), F32), jax.ShapeDtypeStruct((1, D_FF), F32)],
        compiler_params=_cparams(("parallel",)),
    )(u, u, cw, cw, cb, cb, dact)
    return (jnp.concatenate([dug, duv], axis=1), jnp.concatenate([dwg, dwv], axis=1),
            jnp.concatenate([dbg, dbv], axis=1))


ATT_BATCH = 4


def _att_batch(q, kp, kc, vp, vc, first):
    ma = lax.broadcasted_iota(jnp.int32, (1, ATT_BLOCK, 128), 2) < 64

    def diag(x):
        return jnp.concatenate([jnp.where(ma, x, 0.0), jnp.where(ma, 0.0, x)], axis=1)

    qi = lax.broadcasted_iota(jnp.int32, (1, ATT_BLOCK, 2 * ATT_BLOCK), 1)
    kj = lax.broadcasted_iota(jnp.int32, (1, ATT_BLOCK, 2 * ATT_BLOCK), 2) & (ATT_BLOCK - 1)
    okp = kj >= qi + jnp.where(first, 2 * ATT_BLOCK, 0)
    okc = kj <= qi
    sp = jnp.where(okp, _bnt(q, diag(kp)) * 0.125, NEG)
    sc = jnp.where(okc, _bnt(q, diag(kc)) * 0.125, NEG)

    def per_head(fn, x):
        return fn(x[..., :ATT_BLOCK]), fn(x[..., ATT_BLOCK:])

    def spread(ab):
        return jnp.concatenate([jnp.broadcast_to(t, t.shape[:2] + (ATT_BLOCK,)) for t in ab], axis=-1)

    row_max = functools.partial(jnp.max, axis=-1, keepdims=True)
    row_sum = functools.partial(jnp.sum, axis=-1, keepdims=True)
    m = [lax.stop_gradient(jnp.maximum(a, b)) for a, b in zip(per_head(row_max, sp), per_head(row_max, sc))]
    pp, pc = jnp.exp(sp - spread(m)), jnp.exp(sc - spread(m))
    den = [a + b for a, b in zip(per_head(row_sum, pp), per_head(row_sum, pc))]
    num = _bnn(pp, diag(vp)) + _bnn(pc, diag(vc))
    out = num / jnp.where(ma, den[0], den[1])
    lse = jnp.where(ma, m[0] + jnp.log(den[0]), m[1] + jnp.log(den[1]))
    return out, jnp.broadcast_to(lse, out.shape)


def _att_pairs_per_step(dil):
    return 4 if dil == 1 else 1


def _att_residues(dil):
    return min(dil, ATT_BATCH // _att_pairs_per_step(dil))


def _att_specs(g, dil):
    rows, pp = ATT_BLOCK * dil, _att_pairs_per_step(dil)

    def cur(slot):
        return pl.BlockSpec((rows, 128 * pp), lambda n, p: (n, (g * 3 + slot) * (4 // pp) + p))

    def prev(slot):
        return pl.BlockSpec((rows, 128 * pp), lambda n, p: (jnp.maximum(n - 1, 0), (g * 3 + slot) * (4 // pp) + p))

    return [cur(0), prev(1), cur(1), prev(2), cur(2)]


def _att_out_spec(dil):
    return pl.BlockSpec((ATT_BLOCK * dil, 128 * _att_pairs_per_step(dil)), lambda n, p: (n, p))


def _att_grid(s, dil):
    return (s // (ATT_BLOCK * dil), 4 // _att_pairs_per_step(dil))


def _att_windows(i, dil):
    res = _att_residues(dil)

    def rows(r):
        return pl.ds(i * res + r, ATT_BLOCK, stride=dil) if dil > 1 else pl.ds(0, ATT_BLOCK)

    return [(rows(r), pl.ds(128 * j, 128)) for j in range(_att_pairs_per_step(dil)) for r in range(res)]


def _att_fwd(att_in, g, dil):
    s = att_in.shape[0]

    def body(q_ref, kp_ref, kc_ref, vp_ref, vc_ref, o_ref, l_ref):
        first = pl.program_id(0) == 0

        def one(i, carry):
            win = _att_windows(i, dil)
            vals = [jnp.stack([ref[w] for w in win]) for ref in (q_ref, kp_ref, kc_ref, vp_ref, vc_ref)]
            o, l = _att_batch(*vals, first)
            for j, w in enumerate(win):
                o_ref[w] = o[j]
                l_ref[w] = l[j]
            return carry

        lax.fori_loop(0, dil // _att_residues(dil), one, 0)

    return _pcall(
        body, name=f"att_fwd{g}", grid=_att_grid(s, dil), in_specs=_att_specs(g, dil),
        out_specs=[_att_out_spec(dil)] * 2, out_shape=[jax.ShapeDtypeStruct((s, ATT_WIDTH), F32)] * 2,
        compiler_params=_cparams(("parallel", "parallel")),
    )(att_in, att_in, att_in, att_in, att_in)


def _att_bwd(att_in, g, dil, do, dl, acc):
    s = att_in.shape[0]

    def body(q_ref, kp_ref, kc_ref, vp_ref, vc_ref, do_ref, dl_ref, dq_ref, dkp_ref, dkc_ref, dvp_ref, dvc_ref):
        first = pl.program_id(0) == 0

        def one(i, carry):
            win = _att_windows(i, dil)
            vals = [jnp.stack([ref[w] for w in win]) for ref in (q_ref, kp_ref, kc_ref, vp_ref, vc_ref)]
            _, vjp = jax.vjp(lambda *a: _att_batch(*a, first), *vals)
            grads = vjp((jnp.stack([do_ref[w] for w in win]), jnp.stack([dl_ref[w] for w in win])))
            for ref, gr in zip((dq_ref, dkp_ref, dkc_ref, dvp_ref, dvc_ref), grads):
                for j, w in enumerate(win):
                    ref[w] = gr[j]
            return carry

        lax.fori_loop(0, dil // _att_residues(dil), one, 0)

    dq, dkp, dkc, dvp, dvc = _pcall(
        body, name=f"att_bwd{g}", grid=_att_grid(s, dil), in_specs=_att_specs(g, dil) + [_att_out_spec(dil)] * 2,
        out_specs=[_att_out_spec(dil)] * 5, out_shape=[jax.ShapeDtypeStruct((s, ATT_WIDTH), F32)] * 5,
        compiler_params=_cparams(("parallel", "parallel")),
    )(att_in, att_in, att_in, att_in, att_in, do, dl)

    unit, rb = ATT_BLOCK * dil, 1024
    steps = s // rb
    within = unit < rb

    def shifted(cur_ref, next_ref, has_next):
        nxt = jnp.where(has_next, next_ref[...], 0.0)
        return jnp.concatenate([cur_ref[unit:, :], nxt], axis=0) if within else nxt

    def cbody(dq_ref, dkc_ref, dkp_ref, dkn_ref, dvc_ref, dvp_ref, dvn_ref, *rest):
        o_ref = rest[-1]
        has_next = pl.program_id(0) + (1 if within else unit // rb) < steps
        o_ref[:, 0:ATT_WIDTH] = dq_ref[...].astype(BF16)
        o_ref[:, ATT_WIDTH:2 * ATT_WIDTH] = (dkc_ref[...] + shifted(dkp_ref, dkn_ref, has_next)).astype(BF16)
        o_ref[:, 2 * ATT_WIDTH:3 * ATT_WIDTH] = (dvc_ref[...] + shifted(dvp_ref, dvn_ref, has_next)).astype(BF16)

    cur = pl.BlockSpec((rb, ATT_WIDTH), lambda i: (i, 0))
    if within:
        nxt = pl.BlockSpec((unit, ATT_WIDTH), lambda i: (jnp.minimum((i + 1) * (rb // unit), s // unit - 1), 0))
    else:
        nxt = pl.BlockSpec((rb, ATT_WIDTH), lambda i: (jnp.minimum(i + unit // rb, steps - 1), 0))
    carried = [] if acc is None else [acc]
    return _pcall(
        cbody, name=f"att_bwd_sum{g}", grid=(steps,),
        in_specs=[cur, cur, cur, nxt, cur, cur, nxt] + [pl.BlockSpec(memory_space=pl.ANY)] * len(carried),
        out_specs=pl.BlockSpec((rb, 3 * ATT_WIDTH), lambda i: (i, g)),
        out_shape=jax.ShapeDtypeStruct((s, N_ATT), BF16), input_output_aliases={7: 0} if carried else {},
        compiler_params=_cparams(("parallel",)),
    )(dq, dkc, dkp, dkp, dvc, dvp, dvp, *carried)


def _unit_lower_inverse_impl(n):
    eye = (lax.broadcasted_iota(jnp.int32, (1,) + n.shape[1:], 1)
           == lax.broadcasted_iota(jnp.int32, (1,) + n.shape[1:], 2))
    t = jnp.where(eye, 1.0, 0.0) + n
    pw = n
    for _ in range(5):
        pw = _bnn(pw, pw)
        t = t + _bnn(t, pw)
    return t


@jax.custom_vjp
def _unit_lower_inverse(n):
    return _unit_lower_inverse_impl(n)


def _unit_lower_inverse_fwd(n):
    t = _unit_lower_inverse_impl(n)
    return t, t


_unit_lower_inverse.defvjp(_unit_lower_inverse_fwd, lambda t, g: (_bnt(_btn(t, g), t),))


@jax.custom_vjp
def _known_inverse(n, t):
    return t


_known_inverse.defvjp(lambda n, t: (t, t), lambda t, g: (_bnt(_btn(t, g), t), jnp.zeros_like(t)))


def _scan_chunk(r, lw, k, v, a, b, s0, inverse):
    c = SCAN_CHUNK
    p = s0.shape[0]
    ri = lax.broadcasted_iota(jnp.int32, (c, c), 0)
    ci = lax.broadcasted_iota(jnp.int32, (c, c), 1)
    cum = jnp.dot((ci <= ri).astype(F32), lw, precision=HI, preferred_element_type=F32)
    tot = jnp.sum(lw, axis=0, keepdims=True)
    ma = (lax.broadcasted_iota(jnp.int32, (c, 128 * p), 1) & 127) < 64

    def pairs(x):
        return jnp.concatenate([x[None, :, 128 * j:128 * (j + 1)] for j in range(p)], axis=0)

    def stack(x):
        return jnp.concatenate([pairs(jnp.where(ma, x, 0.0)), pairs(jnp.where(ma, 0.0, x))], axis=1)

    einv, eend = jnp.exp(-cum), jnp.exp(tot - cum)
    ra, aa = stack(r * jnp.exp(cum)), stack(a * jnp.exp(cum - lw))
    bi, ki, be, ke, vs = stack(b * einv), stack(k * einv), stack(b * eend), stack(k * eend), stack(v)
    r2 = lax.broadcasted_iota(jnp.int32, (1, 2 * c, 2 * c), 1)
    c2 = lax.broadcasted_iota(jnp.int32, (1, 2 * c, 2 * c), 2)
    same = (r2 >= c) == (c2 >= c)
    strict = jnp.logical_and(same, c2 < r2)
    incl = jnp.logical_and(same, c2 <= r2)
    s0 = jnp.where(same, s0, 0.0)
    prod = _bnt(jnp.concatenate([aa, ra], axis=1), jnp.concatenate([bi, ki], axis=1))
    a_ab = jnp.where(strict, prod[:, :2 * c, :2 * c], 0.0)
    a_ak = jnp.where(strict, prod[:, :2 * c, 2 * c:], 0.0)
    a_rb = jnp.where(incl, prod[:, 2 * c:, :2 * c], 0.0)
    a_rk = jnp.where(incl, prod[:, 2 * c:, 2 * c:], 0.0)
    t = inverse(a_ab)
    u = _bnn(t, _bnt(aa, s0) + _bnn(a_ak, vs))
    uv = jnp.concatenate([u, vs], axis=1)
    ys = _bnt(ra, s0) + _bnn(jnp.concatenate([a_rb, a_rk], axis=2), uv)
    s1 = s0 * pairs(jnp.exp(tot)) + _btn(uv, jnp.concatenate([be, ke], axis=1))
    y3 = ys[:, :c] + ys[:, c:]
    return (jnp.concatenate([y3[j] for j in range(p)], axis=1), s1), t


def _scan_specs(rev, n):
    def at(i):
        return n - 1 - i if rev else i

    def cm(cb):
        return pl.BlockSpec((SCAN_CHUNK, D), lambda i: (at(i), cb))

    return cm, pl.BlockSpec((1, SCAN_PAIRS, 128, 128), lambda i: (at(i), 0, 0, 0))


def _comm_phases(comm, refs, n, step=None):
    k = comm.n
    srcs, outs, sems = refs[:k], refs[k:2 * k], refs[2 * k:]
    i = pl.program_id(0) if step is None else step

    def before():
        @pl.when(i == 0)
        def _():
            comm.first(srcs, outs, sems)

    def after():
        if comm.mid is not None:
            @pl.when(i == (3 * n) // 4)
            def _():
                comm.mid(srcs, outs, sems)

        @pl.when(i == n - 1)
        def _():
            comm.last(srcs, outs, sems)

    return before, after


def _scan_fwd(zs, lw, km, aa, bb, comm):
    s = zs.shape[0]
    n = s // SCAN_CHUNK
    cm, st = _scan_specs(False, n)
    k = comm.n

    def body(*refs):
        r_ref, lw_ref, k_ref, v_ref, a_ref, b_ref = refs[:6]
        y_ref, s0_ref, t_ref = refs[6 + k:9 + k]
        state = refs[9 + 2 * k]
        before, after = _comm_phases(comm, refs[6:6 + k] + refs[9 + k:9 + 2 * k] + refs[10 + 2 * k:], n)
        before()

        @pl.when(pl.program_id(0) == 0)
        def _():
            state[...] = jnp.zeros_like(state)

        s0 = state[...]
        s0_ref[0] = s0
        (y, s1), t = _scan_chunk(*[ref[...] for ref in (r_ref, lw_ref, k_ref, v_ref, a_ref, b_ref)], s0,
                                 _unit_lower_inverse)
        y_ref[...] = y
        t_ref[0] = t.astype(BF16)
        state[...] = s1
        after()

    per_chunk = (n, SCAN_PAIRS, 128, 128)
    res = _pcall(
        body, name="scan_fwd", grid=(n,), in_specs=[cm(0), cm(0), cm(0), cm(2), cm(0), cm(0)] + [_HBM] * k,
        out_specs=[cm(0), st, st] + [_HBM] * k,
        out_shape=[jax.ShapeDtypeStruct((s, D), F32), jax.ShapeDtypeStruct(per_chunk, F32),
                   jax.ShapeDtypeStruct(per_chunk, BF16)] + comm.out_shape,
        scratch_shapes=[pltpu.VMEM((SCAN_PAIRS, 128, 128), F32)] + comm.sems,
        compiler_params=_cparams(("arbitrary",)),
    )(zs, lw, km, zs, aa, bb, *comm.ins)
    return res[0], res[1], res[2], res[3:]


def _scan_bwd(zs, lw, km, aa, bb, s0s, ts, dy, comm):
    s = zs.shape[0]
    n = s // SCAN_CHUNK
    cm, st = _scan_specs(True, n)
    k = comm.n

    def body(*refs):
        r_ref, lw_ref, k_ref, v_ref, a_ref, b_ref, s0_ref, t_ref, dy_ref = refs[:9]
        douts = refs[9 + k:15 + k]
        dstate = refs[15 + 2 * k]
        before, after = _comm_phases(comm, refs[9:9 + k] + refs[15 + k:15 + 2 * k] + refs[16 + 2 * k:], n)
        before()

        @pl.when(pl.program_id(0) == 0)
        def _():
            dstate[...] = jnp.zeros_like(dstate)

        t = t_ref[0].astype(F32)
        prim = [ref[...] for ref in (r_ref, lw_ref, k_ref, v_ref, a_ref, b_ref)] + [s0_ref[0]]
        _, vjp, _ = jax.vjp(lambda *p: _scan_chunk(*p, lambda nil: _known_inverse(nil, t)), *prim, has_aux=True)
        grads = vjp((dy_ref[...], dstate[...]))
        for ref, gr in zip(douts, grads[:6]):
            ref[...] = gr
        dstate[...] = grads[6]
        after()

    res = _pcall(
        body, name="scan_bwd", grid=(n,),
        in_specs=[cm(0), cm(0), cm(0), cm(2), cm(0), cm(0), st, st, cm(0)] + [_HBM] * k,
        out_specs=[cm(0)] * 6 + [_HBM] * k, out_shape=[jax.ShapeDtypeStruct((s, D), F32)] * 6 + comm.out_shape,
        scratch_shapes=[pltpu.VMEM((SCAN_PAIRS, 128, 128), F32)] + comm.sems,
        compiler_params=_cparams(("arbitrary",)),
    )(zs, lw, km, zs, aa, bb, s0s, ts, dy, *comm.ins)
    return res[:6], res[6:]


_HBM = pl.BlockSpec(memory_space=pltpu.HBM)


def _me():
    return lax.axis_index("x"), lax.axis_index("y"), lax.axis_index("c")


def _allgather8(src, name):
    def body(src_ref, out_ref, ssem, rsem, lsem):
        x, y, c = _me()
        me = 4 * x + 2 * y + c
        local = pltpu.make_async_copy(src_ref, out_ref.at[me], lsem)
        local.start()
        peers = []
        for k in range(1, 8):
            peers.append(((1 - x) if k & 4 else x, (1 - y) if k & 2 else y, (1 - c) if k & 1 else c))
        sends = []
        for k, peer in enumerate(peers):
            cp = pltpu.make_async_remote_copy(src_ref, out_ref.at[me], ssem.at[k], rsem.at[k], device_id=peer,
                                              device_id_type=MESH)
            cp.start()
            sends.append(cp)
        for k, (px, py, pc) in enumerate(peers):
            pltpu.make_async_remote_copy(src_ref, out_ref.at[4 * px + 2 * py + pc], ssem.at[k], rsem.at[k],
                                         device_id=(px, py, pc), device_id_type=MESH).wait_recv()
        for cp in sends:
            cp.wait_send()
        local.wait()

    return _pcall(
        body, name=name, in_specs=[_HBM], out_specs=_HBM, out_shape=jax.ShapeDtypeStruct((8,) + src.shape, src.dtype),
        scratch_shapes=[pltpu.SemaphoreType.DMA((7,)), pltpu.SemaphoreType.DMA((7,)), pltpu.SemaphoreType.DMA],
    )(src)


def _other_chips(x, y):
    return [(1 - x, y), (x, 1 - y), (1 - x, 1 - y)]


def _remote(src, dst, ssem, rsem, to):
    return pltpu.make_async_remote_copy(src, dst, ssem, rsem, device_id=to, device_id_type=MESH)


class _GatherWeights:
    def __init__(self, shards):
        self.ins = list(shards)
        n = self.n = len(shards)
        self.out_shape = [jax.ShapeDtypeStruct((4,) + t.shape, t.dtype) for t in shards]
        self.sems = [pltpu.SemaphoreType.DMA((6 * n,)), pltpu.SemaphoreType.DMA((6 * n,)),
                     pltpu.SemaphoreType.DMA((n,)), pltpu.SemaphoreType.DMA((n,))]

    def _copies(self, srcs, outs, sems):
        ssem, rsem, lsem, osem = sems
        x, y, c = _me()
        me = 2 * x + y
        own, ici, landed, passed, passed_in = [], [], [], [], []
        for a in range(self.n):
            h = self.ins[a].shape[0] // 2
            mine, other = pl.ds(c * h, h), pl.ds((1 - c) * h, h)
            own.append(_remote(srcs[a], outs[a].at[me], lsem.at[a], osem.at[a], (x, y, 1 - c)))
            for k, (px, py) in enumerate(_other_chips(x, y)):
                s1, r1, s2, r2 = ssem.at[6 * a + k], rsem.at[6 * a + k], ssem.at[6 * a + 3 + k], rsem.at[6 * a + 3 + k]
                got, got_sib = outs[a].at[2 * px + py, mine], outs[a].at[2 * px + py, other]
                ici.append(_remote(srcs[a].at[mine], outs[a].at[me, mine], s1, r1, (px, py, c)))
                landed.append(_remote(got, got, s1, r1, (px, py, c)))
                passed.append(_remote(got, got, s2, r2, (x, y, 1 - c)))
                passed_in.append(_remote(got_sib, got_sib, s2, r2, (x, y, 1 - c)))
        return own, ici, landed, passed, passed_in

    def first(self, srcs, outs, sems):
        own, ici, _, _, _ = self._copies(srcs, outs, sems)
        for cp in own + ici:
            cp.start()

    def mid(self, srcs, outs, sems):
        _, _, landed, passed, _ = self._copies(srcs, outs, sems)
        for arrived, onward in zip(landed, passed):
            arrived.wait_recv()
            onward.start()

    def last(self, srcs, outs, sems):
        own, ici, _, passed, passed_in = self._copies(srcs, outs, sems)
        for cp in passed_in:
            cp.wait_recv()
        for cp in ici + passed:
            cp.wait_send()
        for cp in own:
            cp.wait()


class _ScatterToChips:
    def __init__(self, parts):
        self.ins = list(parts)
        n = self.n = len(parts)
        self.out_shape = [jax.ShapeDtypeStruct(t.shape, t.dtype) for t in parts]
        self.sems = [pltpu.SemaphoreType.DMA((3 * n,)), pltpu.SemaphoreType.DMA((3 * n,)), pltpu.SemaphoreType.DMA((n,))]

    def _copies(self, srcs, outs, sems):
        ssem, rsem, lsem = sems
        x, y, c = _me()
        me = 2 * x + y
        own, out, landed = [], [], []
        for a in range(self.n):
            own.append(pltpu.make_async_copy(srcs[a].at[me], outs[a].at[me], lsem.at[a]))
            for k, (px, py) in enumerate(_other_chips(x, y)):
                dst = outs[a].at[2 * px + py]
                out.append(_remote(srcs[a].at[2 * px + py], outs[a].at[me], ssem.at[3 * a + k], rsem.at[3 * a + k],
                                   (px, py, c)))
                landed.append(_remote(dst, dst, ssem.at[3 * a + k], rsem.at[3 * a + k], (px, py, c)))
        return own, out, landed

    def first(self, srcs, outs, sems):
        own, out, _ = self._copies(srcs, outs, sems)
        for cp in own + out:
            cp.start()

    mid = None

    def last(self, srcs, outs, sems):
        own, out, landed = self._copies(srcs, outs, sems)
        for cp in landed:
            cp.wait_recv()
        for cp in own:
            cp.wait()
        for cp in out:
            cp.wait_send()


def _run_comm(comm, name):
    n = comm.n

    def body(*refs):
        srcs, outs, sems = refs[:n], refs[n:2 * n], refs[2 * n:]
        comm.first(srcs, outs, sems)
        if comm.mid is not None:
            comm.mid(srcs, outs, sems)
        comm.last(srcs, outs, sems)

    return _pcall(body, name=name, in_specs=[_HBM] * n, out_specs=[_HBM] * n, out_shape=comm.out_shape,
                  scratch_shapes=comm.sems)(*comm.ins)


class _NoComm:
    n, ins, out_shape, sems, mid = 0, [], [], [], None

    def first(self, srcs, outs, sems):
        pass

    def last(self, srcs, outs, sems):
        pass


_NOTHING = _NoComm()


class _SiblingHalves:
    mid = None

    def __init__(self, grads):
        self.ins = list(grads)
        n = self.n = len(grads)
        self.out_shape = [jax.ShapeDtypeStruct((4, t.shape[1] // 2, t.shape[2]), t.dtype) for t in grads]
        self.sems = [pltpu.SemaphoreType.DMA((n,)), pltpu.SemaphoreType.DMA((n,))]

    def _copies(self, srcs, outs, sems):
        ssem, rsem = sems
        x, y, c = _me()
        copies = []
        for a in range(self.n):
            h = self.ins[a].shape[1] // 2
            copies.append(_remote(srcs[a].at[:, pl.ds((1 - c) * h, h)], outs[a], ssem.at[a], rsem.at[a], (x, y, 1 - c)))
        return copies

    def first(self, srcs, outs, sems):
        for cp in self._copies(srcs, outs, sems):
            cp.start()

    def last(self, srcs, outs, sems):
        for cp in self._copies(srcs, outs, sems):
            cp.wait()


def _reduce_finish(reds, name):
    n = len(reds)

    def body(*refs):
        outs = refs[n:2 * n]
        ssem, rsem = refs[2 * n:]
        x, y, c = _me()
        copies = []
        for a in range(n):
            h = reds[a].shape[0] // 2
            mine = outs[a].at[pl.ds(c * h, h)]
            copies.append(_remote(mine, mine, ssem.at[a], rsem.at[a], (x, y, 1 - c)))
        for cp in copies:
            cp.start()
        for a in range(n):
            h = reds[a].shape[0] // 2
            dst = outs[a].at[pl.ds((1 - c) * h, h)]
            _remote(dst, dst, ssem.at[a], rsem.at[a], (x, y, 1 - c)).wait_recv()
        for cp in copies:
            cp.wait_send()

    return _pcall(
        body, name=name, in_specs=[_HBM] * n, out_specs=[_HBM] * n,
        out_shape=[jax.ShapeDtypeStruct(t.shape, t.dtype) for t in reds],
        input_output_aliases={a: a for a in range(n)},
        scratch_shapes=[pltpu.SemaphoreType.DMA((n,)), pltpu.SemaphoreType.DMA((n,))],
    )(*reds)


def _half_sum(fn, full, halves, out_full, out_dtype, core, name):
    p, h, c = (halves[0].shape if halves else (full[0].shape[0], full[0].shape[1] // 2, full[0].shape[2]))
    br = _div(h, max(16, (1 << 19) // (p * c)), 16)
    nb = h // br
    mine3 = pl.BlockSpec((p, br, c), lambda i, core_ref: (0, core_ref[0] * nb + i, 0))
    half3 = pl.BlockSpec((p, br, c), lambda i, core_ref: (0, i, 0))

    def body(core_ref, *refs):
        refs[-1][...] = fn(*[t[...].astype(F32) for t in refs[:-1]]).astype(out_dtype)

    if out_full:
        out_spec = pl.BlockSpec((br, c), lambda i, core_ref: (core_ref[0] * nb + i, 0))
        out_shape = jax.ShapeDtypeStruct((2 * h, c), out_dtype)
    else:
        out_spec, out_shape = half3, jax.ShapeDtypeStruct((p, h, c), out_dtype)
    return _pcall(
        body, name=name,
        grid_spec=pltpu.PrefetchScalarGridSpec(
            num_scalar_prefetch=1, grid=(nb,), in_specs=[mine3] * len(full) + [half3] * len(halves),
            out_specs=out_spec),
        out_shape=out_shape, compiler_params=_cparams(("parallel",)),
    )(core, *full, *halves)


def _ada_fwd(c_all, w, b):
    def body(c_ref, w_ref, b_ref, o_ref):
        o_ref[...] = jnp.dot(c_ref[...], w_ref[...], precision=HI, preferred_element_type=F32) + b_ref[...]

    return _pcall(body, name="ada_fwd", out_shape=jax.ShapeDtypeStruct((c_all.shape[0], w.shape[1]), F32),
                  compiler_params=pltpu.CompilerParams(vmem_limit_bytes=VMEM_LIMIT))(c_all, w, b)


def _ada_bwd(c_all_t, d):
    def body(c_ref, d_ref, o_ref):
        o_ref[...] = jnp.dot(c_ref[...], d_ref[...], precision=HI, preferred_element_type=F32)

    return _pcall(body, name="ada_bwd", out_shape=jax.ShapeDtypeStruct((c_all_t.shape[0], d.shape[1]), F32),
                  compiler_params=pltpu.CompilerParams(vmem_limit_bytes=VMEM_LIMIT))(c_all_t, d)


def _sum_lead(x, name):
    p, r, n = x.shape
    br = _div(r, 512, 8)

    def body(x_ref, o_ref):
        acc = x_ref[0]
        for j in range(1, p):
            acc = acc + x_ref[j]
        o_ref[...] = acc

    return _pcall(
        body, name=name, grid=(r // br,), in_specs=[pl.BlockSpec((p, br, n), lambda i: (0, i, 0))],
        out_specs=pl.BlockSpec((br, n), lambda i: (i, 0)), out_shape=jax.ShapeDtypeStruct((r, n), F32),
        compiler_params=_cparams(("parallel",)),
    )(x)


def _adamw(w, g, m, v, name):
    shape = w.shape
    cols = shape[-1]
    w2, g2, m2, v2 = [t.reshape(-1, cols) for t in (w, g, m, v)]
    rows = w2.shape[0]
    pref = max(8, (1 << 19) // cols // 8 * 8)
    br = _div(rows, pref, 8)
    if rows // br > 64:
        br = pref
    outs = _rows_fwd(_f_adamw, [(t, cols, 0) for t in (w2, g2, m2, v2)], [], [(cols, F32)] * 3, name=name, br=br)
    return [o.reshape(shape) for o in outs]


_BIG = (("w_in", 1), ("w_up", 1), ("w_down", 0), ("w_o", 0), ("w_rwkv_out", 0), ("w_att_out", 1), ("w2", 1), ("a2", 1),
        ("g2", 1))


_NEEDED_FIRST = ("w_in", "w_att_out", "w2", "a2", "g2")
_NEEDED_LATER = ("w_up", "w_down", "w_o", "w_rwkv_out")
_DONE_EARLY = ("w_up", "w_down", "w_o", "w_rwkv_out", "w_att_out")
_DONE_LATE = ("w_in", "w2", "a2", "g2")


def _cols_joined(t):
    return jnp.concatenate([t[j] for j in range(4)], axis=1)


def _cols_split(t):
    n = t.shape[1] // 4
    return jnp.stack([t[:, j * n:(j + 1) * n] for j in range(4)])


W_IN_SHARD = (N_ATT + N_RW + N_GATE) // 4
W_IN_PAD = 2560


def _row_window(parts, lo, hi):
    out, pos = [], 0
    for t, w in parts:
        a, b = max(lo, pos), min(hi, pos + w)
        if a < b:
            out.append(t[a - pos:b - pos])
        pos += w
    return out[0] if len(out) == 1 else jnp.concatenate(out, axis=0)


def _rows_joined(t):
    return t.reshape(4 * t.shape[1], t.shape[2])


def _rows_split(t):
    return t.reshape(4, t.shape[0] // 4, t.shape[1])


def _step_to_scan(x, tgt, ada, wts):
    sh1, sc1, gt1, sh2, sc2, gt2 = ada
    br = 256
    grp = lax.broadcasted_iota(jnp.int32, (D, 128), 0) // 64 == lax.broadcasted_iota(jnp.int32, (D, 128), 1)
    e = grp.astype(F32)
    et = e.T
    w_in = [(wts["w_in"][j], W_IN_SHARD) for j in range(4)]
    w_att = _row_window(w_in, 0, N_ATT)
    w_rw = jnp.concatenate([_row_window(w_in, N_ATT, N_ATT + N_RW), jnp.zeros((N_RWP - N_RW, D), BF16)], axis=0)
    w_gate = _row_window(w_in, N_ATT + N_RW, N_ATT + N_RW + N_GATE)
    mu = jnp.pad(wts["mu_shift"], ((0, 0), (0, N_RWP - N_RW)))
    wl = jnp.zeros((N_LORA, 3 * D), F32)
    wl = wl.at[0:64, 0:D].set(_cols_joined(wts["w2"]).astype(F32))
    wl = wl.at[64:128, D:2 * D].set(_cols_joined(wts["a2"]).astype(F32))
    wl = wl.at[128:288, 2 * D:3 * D].set(_cols_joined(wts["g2"]).astype(F32))
    pre1_c = [wts["norm1_w"], sc1, sh1]
    (h1,) = _rows_fwd(_f_pre, [(x, D, 0)], pre1_c, [(D, BF16), None], name="pre1_fwd", br=2 * br)
    att_in = _mm(h1, w_att, tb=True, name="mm_att_in")
    z = _mm(h1, w_rw, tb=True, out_dtype=BF16, name="mm_rw_in")
    gate_in = _mm(h1, w_gate, tb=True, out_dtype=BF16, name="mm_gate_in")
    att_o, att_l = [], []
    for g, (_, dil) in enumerate(ATT_PATTERNS):
        o, l = _att_fwd(att_in, g, dil)
        att_o.append(o)
        att_l.append(l)
    comb_rows = [(t, ATT_WIDTH, 0) for t in att_o + att_l]
    (att,) = _rows_fwd(_f_comb, comb_rows, [], [(ATT_WIDTH, BF16)], name="comb_fwd", br=2 * br)
    y_att = _mm(att, wts["w_att_out"], b_chip=True, out_dtype=BF16, name="mm_att_out")
    rwpre_c = [wts["w0"], wts["a0"], wts["k_k"], wts["k_a"], wl, e, et]

    def shift_and_rwpre(zz, *rest):
        consts, mu_row, before = rest[:-2], rest[-2], rest[-1]
        last = jnp.sum(jnp.where(lax.broadcasted_iota(jnp.int32, before.shape, 0) == HALO - 1, before, 0.0), axis=0,
                       keepdims=True)
        row = lax.broadcasted_iota(jnp.int32, zz.shape, 0)
        zprev = jnp.where(row == 0, last, pltpu.roll(zz, 1, 0))
        shifted = zz + (zprev - zz) * mu_row
        return (shifted,) + tuple(_f_rwpre(shifted, *consts))

    zs, lw, km, aa, bb, gg = _rows_fwd(
        shift_and_rwpre, [(z, N_RWP, 0)], rwpre_c + [mu],
        [(N_RWP, F32), None, (D, F32), (D, F32), None, (D, F32), (D, F32), (D, F32)], name="rwpre_fwd", br=br, halo=0)
    return dict(x=x, tgt=tgt, wts=wts, br=br, e=e, et=et, gt1=gt1, sc2=sc2, sh2=sh2, gt2=gt2, w_att=w_att, w_rw=w_rw,
                w_gate=w_gate, mu=mu, pre1_c=pre1_c, h1=h1, att_in=att_in, z=z, gate_in=gate_in, comb_rows=comb_rows,
                att=att, y_att=y_att, zs=zs, rwpre_c=rwpre_c, lw=lw, km=km, aa=aa, bb=bb, gg=gg)


def _step_between_scans(st, y_raw, late):
    x, tgt, wts, br, e, et = st["x"], st["tgt"], st["wts"], st["br"], st["e"], st["et"]
    zs, km, gg, gate_in, y_att, att = st["zs"], st["km"], st["gg"], st["gate_in"], st["y_att"], st["att"]
    comb_rows, att_in = st["comb_rows"], st["att_in"]
    gt1, sc2, sh2, gt2 = st["gt1"], st["sc2"], st["sh2"], st["gt2"]
    w_up, w_ao = late["w_up"], wts["w_att_out"]
    w_down, w_o, w_ro = _rows_joined(late["w_down"]), _rows_joined(late["w_o"]), _rows_joined(late["w_rwkv_out"])
    post_rows = [(y_raw, D, 0), (zs, D, 0), (zs, D, 2), (km, D, 0), (gg, D, 0)]
    post_c = [wts["lnx_w"], wts["lnx_b"], wts["r_k"], e, et]
    (rw_out,) = _rows_fwd(_f_rwpost, post_rows, post_c, [(D, BF16)], name="rwpost_fwd", br=br)
    y_rw = _mm(rw_out, w_ro, out_dtype=BF16, name="mm_rw_out")
    mix_rows = [(gate_in, N_GATE, 0), (y_att, D, 0), (y_rw, D, 0)]
    (mix,) = _rows_fwd(_f_mix, mix_rows, [wts["b_gate"]], [(D, BF16)], name="mix_fwd", br=2 * br)
    o = _mm(mix, w_o, out_dtype=BF16, name="mm_o")
    pre2_c = [gt1, wts["norm2_w"], sc2, sh2]
    x1, h2 = _rows_fwd(_f_pre2, [(x, D, 0), (o, D, 0)], pre2_c, [(D, F32), (D, BF16)], name="pre2_fwd", br=2 * br)
    u = _mm(h2, w_up, b_chip=True, name="mm_up")
    act = _conv_fwd(u, wts["conv_w"], wts["conv_b"])
    f = _mm(act, w_down, out_dtype=BF16, name="mm_down")
    fin_rows = [(x1, D, 0), (f, D, 0), (tgt, D, 0)]
    fin_c = [gt2, wts["norm_f_w"]]

    def fin_fwd(*a):
        (l,) = _f_fin(*a)
        return (jnp.broadcast_to(jnp.sum(l, axis=0, keepdims=True), (8, 128)),)

    (loss_acc,) = _rows_fwd(fin_fwd, fin_rows, fin_c, [], name="fin_fwd", br=2 * br, acc_shape=(8, 128))

    gw = {}
    dx1a, df, d_gt2, gw["norm_f_w"] = _rows_bwd(
        _f_fin, fin_rows, fin_c, [[]], wrt_rows=[0, 1], wrt_consts=[0, 1], drow_dtypes=[F32, BF16],
        name="fin_bwd", br=2 * br, unit_cot=True)
    dact = _mm(df, w_down, tb=True, name="mm_dact")
    gw["w_down"] = _rows_split(_mm(act, df, ta=True, out_dtype=BF16, name="mm_dw_down"))
    du, gw["conv_w"], gw["conv_b"] = _conv_bwd(u, wts["conv_w"], wts["conv_b"], dact)
    dh2 = _mm(du, w_up, tb=True, b_chip=True, out_dtype=BF16, name="mm_dh2")
    gw["w_up"] = _mm(h2, du, ta=True, out_chip=True, out_dtype=BF16, name="mm_dw_up")
    dxa, do, d_gt1, gw["norm2_w"], d_sc2, d_sh2 = _rows_bwd(
        _f_pre2, [(x, D, 0), (o, D, 0)], pre2_c, [[(dx1a, D, 0)], [(dh2, D, 0)]], wrt_rows=[0, 1],
        wrt_consts=[0, 1, 2, 3], drow_dtypes=[F32, BF16], name="pre2_bwd", br=2 * br)
    dmix = _mm(do, w_o, tb=True, out_dtype=BF16, name="mm_dmix")
    gw["w_o"] = _rows_split(_mm(mix, do, ta=True, out_dtype=BF16, name="mm_dw_o"))
    dgate, dya, dyr, gw["b_gate"] = _rows_bwd(
        _f_mix, mix_rows, [wts["b_gate"]], [[(dmix, D, 0)]], wrt_rows=[0, 1, 2], wrt_consts=[0],
        drow_dtypes=[BF16] * 3, name="mix_bwd", br=2 * br)
    datt = _mm(dya, w_ao, tb=True, b_chip=True, out_dtype=BF16, name="mm_datt")
    gw["w_att_out"] = _mm(att, dya, ta=True, out_chip=True, out_dtype=BF16, name="mm_dw_att_out")
    drw = _mm(dyr, w_ro, tb=True, out_dtype=BF16, name="mm_drw")
    gw["w_rwkv_out"] = _rows_split(_mm(rw_out, dyr, ta=True, out_dtype=BF16, name="mm_dw_rw_out"))
    dcomb = _rows_bwd(_f_comb, comb_rows, [], [[(datt, ATT_WIDTH, 0)]], wrt_rows=list(range(6)), wrt_consts=[],
                      drow_dtypes=[F32] * 6, name="comb_bwd", br=2 * br)
    datt_in = None
    for g, (_, dil) in enumerate(ATT_PATTERNS):
        datt_in = _att_bwd(att_in, g, dil, dcomb[g], dcomb[3 + g], datt_in)
    dy_raw, dr_p, dv_p, dkm_p, dgg, gw["lnx_w"], gw["lnx_b"], gw["r_k"], *recv_early = _rows_bwd(
        _f_rwpost, post_rows, post_c, [[(drw, D, 0)]], wrt_rows=[0, 1, 2, 3, 4], wrt_consts=[0, 1, 2],
        drow_dtypes=[F32] * 5, name="rwpost_bwd", br=br, comm=_SiblingHalves([gw[n] for n in _DONE_EARLY]))
    st.update(loss=loss_acc[0, 0], gw=gw, dxa=dxa, dgate=dgate, datt_in=datt_in,
              dy_raw=dy_raw, dr_p=dr_p, dv_p=dv_p, dkm_p=dkm_p, dgg=dgg, d_ada_late=(d_gt1, d_sh2, d_sc2, d_gt2),
              recv_early=recv_early)
    return st


def _chip_parts(grads, recv, names, core):
    return [_half_sum(lambda a, b: a + b, [g], [r], False, BF16, core, "reduce_add2_" + n)
            for g, r, n in zip(grads, recv, names)]


def _step_after_scan(st, scan_grads, core):
    x, br, gw, h1, zs = st["x"], st["br"], st["gw"], st["h1"], st["zs"]
    dr_s, dlw, dkm_s, dv_s, daa, dbb = scan_grads
    pre_cots = [[(st["dr_p"], D, 0), (dr_s, D, 0)], [(dlw, D, 0)], [(st["dkm_p"], D, 0), (dkm_s, D, 0)],
                [(st["dv_p"], D, 0), (dv_s, D, 0)], [(daa, D, 0)], [(dbb, D, 0)], [(st["dgg"], D, 0)]]
    dzs, gw["w0"], gw["a0"], gw["k_k"], gw["k_a"], dwl = _rows_bwd(
        _f_rwpre, [(zs, N_RWP, 0)], st["rwpre_c"], pre_cots, wrt_rows=[0], wrt_consts=[0, 1, 2, 3, 4],
        drow_dtypes=[F32], name="rwpre_bwd", br=128)
    gw["w2"], gw["a2"] = _cols_split(dwl[0:64, 0:D]), _cols_split(dwl[64:128, D:2 * D])
    gw["g2"] = _cols_split(dwl[128:288, 2 * D:3 * D])
    dz, dmu = _shift_bwd(st["z"], st["mu"], dzs)
    gw["mu_shift"] = dmu[:, :N_RW]
    datt_in, dgate = st["datt_in"], st["dgate"]
    dw_in = [(_mm(datt_in, h1, ta=True, out_dtype=BF16, name="mm_dw_att"), N_ATT),
             (_mm(dz, h1, ta=True, out_dtype=BF16, name="mm_dw_rw"), N_RW),
             (_mm(dgate, h1, ta=True, out_dtype=BF16, name="mm_dw_gate"), N_GATE)]
    slabs = []
    for j in range(4):
        slabs += [_row_window(dw_in, j * W_IN_SHARD, (j + 1) * W_IN_SHARD), jnp.zeros((W_IN_PAD - W_IN_SHARD, D), BF16)]
    gw["w_in"] = jnp.concatenate(slabs, axis=0).reshape(4, W_IN_PAD, D)
    late = [gw[n] for n in _DONE_LATE]
    parts = _chip_parts(late, _run_comm(_SiblingHalves(late), "reduce_sib_late"), _DONE_LATE, core)
    dh1, slots_late = _mm_sum([(datt_in, st["w_att"]), (dz, st["w_rw"]), (dgate, st["w_gate"])],
                              comm=_ScatterToChips(parts), name="mm_dh1")
    grad_x, gw["norm1_w"], d_sc1, d_sh1 = _rows_bwd(
        _f_pre, [(x, D, 0)], st["pre1_c"], [[(dh1, D, 0)], [(st["dxa"], D, 0)]], wrt_rows=[0], wrt_consts=[0, 1, 2],
        drow_dtypes=[F32], name="pre1_bwd", br=2 * br)
    d_gt1, d_sh2, d_sc2, d_gt2 = st["d_ada_late"]
    return st["loss"], grad_x, (d_sh1, d_sc1, d_gt1, d_sh2, d_sc2, d_gt2), gw, slots_late


_SMALL = ("b_ada", "norm1_w", "b_gate", "mu_shift", "w0", "a0", "k_k", "k_a", "r_k", "lnx_w", "lnx_b", "norm2_w",
          "conv_b", "norm_f_w")
_NAMES = ("w_ada", "b_ada", "norm1_w", "w_in", "b_gate", "mu_shift", "w0", "w2", "a0", "a2", "g2", "k_k", "k_a", "r_k",
          "lnx_w", "lnx_b", "w_att_out", "w_rwkv_out", "w_o", "norm2_w", "w_up", "conv_w", "conv_b", "w_down",
          "norm_f_w")


def kernel(x, c, w_ada, b_ada, norm1_w, w_in, b_gate, mu_shift, w0, w2, a0, a2, g2, k_k, k_a, r_k, lnx_w, lnx_b, w_att_out, w_rwkv_out, w_o, norm2_w, w_up, conv_w, conv_b, w_down, norm_f_w, loss_target, m_w_ada, m_b_ada, m_norm1_w, m_w_in, m_b_gate, m_mu_shift, m_w0, m_w2, m_a0, m_a2, m_g2, m_k_k, m_k_a, m_r_k, m_lnx_w, m_lnx_b, m_w_att_out, m_w_rwkv_out, m_w_o, m_norm2_w, m_w_up, m_conv_w, m_conv_b, m_w_down, m_norm_f_w, v_w_ada, v_b_ada, v_norm1_w, v_w_in, v_b_gate, v_mu_shift, v_w0, v_w2, v_a0, v_a2, v_g2, v_k_k, v_k_a, v_r_k, v_lnx_w, v_lnx_b, v_w_att_out, v_w_rwkv_out, v_w_o, v_norm2_w, v_w_up, v_conv_w, v_conv_b, v_w_down, v_norm_f_w):
    args = dict(locals())
    p, pm, pv = {}, {}, {}
    for name in _NAMES:
        for dst, key in ((p, name), (pm, "m_" + name), (pv, "v_" + name)):
            t = args[key]
            if name == "w_in":
                dst[name] = jnp.swapaxes(t, 1, 2)[0]
            else:
                dst[name] = t.reshape(1, -1) if name in ("r_k", "norm_f_w") else t.reshape(t.shape[-2], t.shape[-1])
    xi, yi, ci = _me()
    chip = 2 * xi + yi
    dev = 4 * xi + 2 * yi + ci
    x2, tgt = x[0], loss_target[0]

    n_cw = 3 * (2 * D_FF // 4)
    vec = jnp.concatenate([c.reshape(-1), p["conv_w"].reshape(-1), jnp.zeros((8 * D - D - n_cw,), F32)]).reshape(8, D)
    g0 = _allgather8(vec, "gather_c").reshape(8, 8 * D)
    c_all = g0[:, :D]
    conv_w_full = jnp.concatenate([g0[2 * j, D:D + n_cw].reshape(3, -1) for j in range(4)], axis=1)
    n_ada = 6 * D // 4
    b_ada_sh = lax.dynamic_slice(p["b_ada"], (0, chip * n_ada), (1, n_ada))
    ada_sh = _ada_fwd(c_all, p["w_ada"], b_ada_sh)
    ga = _allgather8(ada_sh, "gather_ada")
    ada_all = jnp.concatenate([ga[2 * j] for j in range(4)], axis=1)
    ada_row = lax.dynamic_slice(ada_all, (dev, 0), (1, 6 * D))
    ada = [ada_row[:, j * D:(j + 1) * D] for j in range(6)]

    big = [n for n, _ in _BIG]
    shard = {n: p[n].astype(BF16) for n in big}
    shard["w_in"] = jnp.pad(shard["w_in"], ((0, W_IN_PAD - W_IN_SHARD), (0, 0)))
    wts = dict(zip(_NEEDED_FIRST, _run_comm(_GatherWeights([shard[n] for n in _NEEDED_FIRST]), "gather_w")))
    for n in _SMALL:
        wts[n] = p[n]
    wts["conv_w"] = conv_w_full
    core = ci.reshape(1).astype(jnp.int32)

    st = _step_to_scan(x2, tgt, ada, wts)
    y_raw, s0s, inverses, late = _scan_fwd(st["zs"], st["lw"], st["km"], st["aa"], st["bb"],
                                           _GatherWeights([shard[n] for n in _NEEDED_LATER]))
    st = _step_between_scans(st, y_raw, dict(zip(_NEEDED_LATER, late)))
    early = _chip_parts([st["gw"][n] for n in _DONE_EARLY], st["recv_early"], _DONE_EARLY, core)
    scan_grads, slots_early = _scan_bwd(st["zs"], st["lw"], st["km"], st["aa"], st["bb"], s0s, inverses,
                                        st["dy_raw"], _ScatterToChips(early))
    loss_part, grad_x, d_ada, gw, slots_late = _step_after_scan(st, scan_grads, core)

    small = [jnp.concatenate(d_ada, axis=1)] + [gw[n] for n in _SMALL[1:]] + [gw["conv_w"], loss_part.reshape(1, 1)]
    sizes = [t.size for t in small]
    flat = jnp.concatenate([t.reshape(-1) for t in small])
    npad = (-flat.shape[0]) % (8 * D)
    srows = (flat.shape[0] + npad) // D
    flat = jnp.concatenate([flat, jnp.zeros((npad,), F32)]).reshape(srows, D)
    parts = _allgather8(flat, "gather_small")
    tot = _sum_lead(parts, "sum_small").reshape(-1)
    pieces, pos = [], 0
    for sz in sizes:
        pieces.append(tot[pos:pos + sz])
        pos += sz
    grads = {}
    for n, piece in zip(_SMALL, pieces[:len(_SMALL)]):
        grads[n] = piece.reshape(p[n].shape)
    conv_w_grad = pieces[len(_SMALL)].reshape(3, 2 * D_FF)
    grads["conv_w"] = lax.dynamic_slice(conv_w_grad, (0, chip * (n_cw // 3)), (3, n_cw // 3))
    loss = pieces[-1][0]
    d_ada_all = parts[:, :6].reshape(8, 6 * D)
    grads["w_ada"] = _ada_bwd(c_all.T, lax.dynamic_slice(d_ada_all, (0, chip * n_ada), (8, n_ada)))

    order = _DONE_EARLY + _DONE_LATE
    reds = [_half_sum(lambda t: t[0] + t[1] + t[2] + t[3], [], [t], True, F32, core, "reduce_add4_" + n)
            for n, t in zip(order, list(slots_early) + list(slots_late))]
    for n, g in zip(order, _reduce_finish(reds, "reduce_sib2")):
        grads[n] = g

    outs_g, outs_d, outs_m, outs_v = [], [], [], []
    grads["w_in"] = grads["w_in"][:W_IN_SHARD]
    for name in _NAMES:
        g = grads[name]
        d, m, v = _adamw(p[name], g, pm[name], pv[name], "adamw_" + name)
        shape = args[name].shape
        for outs, t in ((outs_g, g), (outs_d, d), (outs_m, m), (outs_v, v)):
            outs.append(jnp.swapaxes(t[None], 1, 2) if name == "w_in" else t.reshape(shape))
    return (loss, grad_x.reshape(x.shape), *outs_g, *outs_d, *outs_m, *outs_v)
```

```python
import functools
import math

import jax
import jax.numpy as jnp
from jax import lax
from jax.experimental import pallas as pl
from jax.experimental.pallas import tpu as pltpu

F32 = jnp.float32
BF16 = jnp.bfloat16
HI = lax.Precision.HIGHEST
MESH = pl.DeviceIdType.MESH

D = 1024
ATT_PATTERNS = ((128, 1), (512, 4), (2048, 16))
ATT_BLOCK = 128
ATT_WIDTH = 512
N_ATT = 3 * 3 * ATT_WIDTH
N_RW = 3 * D + 64 + 64 + 160
N_RWP = 3456
N_LORA = N_RWP - 3 * D
N_GATE = 2 * D
D_FF = 2816
RMS_EPS = 1e-6
GN_EPS = 64e-5
SCAN_CHUNK = 64
SCAN_PAIRS = 8
NEG = -1e30
VMEM_LIMIT = 48 * 1024 * 1024
HALO = 16

ADAM_LR, ADAM_B1, ADAM_B2, ADAM_EPS, ADAM_WD, ADAM_STEP = 0.001, 0.9, 0.999, 1e-08, 0.01, 10


def _pcall(body, **kw):
    return pl.pallas_call(body, **kw)


def _cparams(sem):
    return pltpu.CompilerParams(dimension_semantics=sem, vmem_limit_bytes=VMEM_LIMIT)


def _div(n, pref, mult):
    best = None
    d = mult
    while d <= min(n, pref):
        if n % d == 0:
            best = d
        d += mult
    return best if best else n


def _dg(a, b, ca, cb):
    return lax.dot_general(a.astype(BF16), b.astype(BF16), (((ca,), (cb,)), ((), ())), preferred_element_type=F32)


@jax.custom_vjp
def _nn(a, b):
    return _dg(a, b, 1, 0)


@jax.custom_vjp
def _nt(a, b):
    return _dg(a, b, 1, 1)


@jax.custom_vjp
def _tn(a, b):
    return _dg(a, b, 0, 0)


_nn.defvjp(lambda a, b: (_nn(a, b), (a, b)), lambda res, g: (_nt(g, res[1]), _tn(res[0], g)))
_nt.defvjp(lambda a, b: (_nt(a, b), (a, b)), lambda res, g: (_nn(g, res[1]), _tn(g, res[0])))
_tn.defvjp(lambda a, b: (_tn(a, b), (a, b)), lambda res, g: (_nt(res[1], g), _nn(res[0], g)))


def _bdg(a, b, ca, cb):
    return lax.dot_general(a.astype(BF16), b.astype(BF16), (((ca,), (cb,)), ((0,), (0,))), preferred_element_type=F32)


@jax.custom_vjp
def _bnn(a, b):
    return _bdg(a, b, 2, 1)


@jax.custom_vjp
def _bnt(a, b):
    return _bdg(a, b, 2, 2)


@jax.custom_vjp
def _btn(a, b):
    return _bdg(a, b, 1, 1)


_bnn.defvjp(lambda a, b: (_bnn(a, b), (a, b)), lambda res, g: (_bnt(g, res[1]), _btn(res[0], g)))
_bnt.defvjp(lambda a, b: (_bnt(a, b), (a, b)), lambda res, g: (_bnn(g, res[1]), _btn(g, res[0])))
_btn.defvjp(lambda a, b: (_btn(a, b), (a, b)), lambda res, g: (_bnt(res[1], g), _bnn(res[0], g)))


def _hsum_impl(x, e, et):
    eb, etb = e.astype(BF16), et.astype(BF16)
    s = jnp.dot(x.astype(BF16), eb, preferred_element_type=F32)
    return jnp.dot(s.astype(BF16), etb, preferred_element_type=F32)


@jax.custom_vjp
def _hsum(x, e, et):
    return _hsum_impl(x, e, et)


_hsum.defvjp(lambda x, e, et: (_hsum_impl(x, e, et), (e, et)),
             lambda res, g: (_hsum_impl(g, res[0], res[1]), jnp.zeros_like(res[0]), jnp.zeros_like(res[1])))


def _mm(a, b, *, ta=False, tb=False, out_dtype=F32, add=None, b_chip=False, out_chip=False, comm=None, name):
    riding = _NOTHING if comm is None else comm
    nc = riding.n
    if ta:
        kdim, m = a.shape
    else:
        m, kdim = a.shape
    if b_chip:
        n = b.shape[1] if tb else 4 * b.shape[2]
    else:
        n = b.shape[0] if tb else b.shape[1]
    tm, tn, tk = _div(m, 1536, 128), _div(n, 1536, 128), _div(kdim, 2048 if ta else 1408, 128)
    if b_chip and tb:
        tk = kdim // 4
    if (b_chip and not tb) or out_chip:
        tn = n // 4
    nk = kdim // tk
    ca, cb = (0 if ta else 1), (1 if tb else 0)

    nin = 2 if add is None else 3
    gi, gj = m // tm, n // tn

    def body(*refs):
        a_ref, b_ref = refs[0], refs[1]
        add_ref = None if add is None else refs[2]
        o_ref = refs[nin + nc]
        step = (pl.program_id(0) * gj + pl.program_id(1)) * nk + pl.program_id(2)
        before, after = _comm_phases(riding, refs[nin:nin + nc] + refs[nin + nc + 1:nin + 2 * nc + 1]
                                     + refs[nin + 2 * nc + 1 + (nk > 1):], gi * gj * nk, step)
        before()
        part = lax.dot_general(a_ref[...], b_ref[...], (((ca,), (cb,)), ((), ())), preferred_element_type=F32)

        def finish(r):
            if add_ref is not None:
                r = r + add_ref[...]
            o_ref[...] = r.astype(o_ref.dtype)

        if nk == 1:
            finish(part)
            after()
            return
        acc = refs[nin + 2 * nc + 1]
        k = pl.program_id(2)

        @pl.when(k == 0)
        def _():
            acc[...] = part

        @pl.when(k > 0)
        def _():
            acc[...] += part

        @pl.when(k == nk - 1)
        def _():
            finish(acc[...])

        after()

    a_spec = pl.BlockSpec((tk, tm), lambda i, j, k: (k, i)) if ta else pl.BlockSpec((tm, tk), lambda i, j, k: (i, k))
    if b_chip:
        b_spec = (pl.BlockSpec((None, tn, tk), lambda i, j, k: (k, j, 0)) if tb
                  else pl.BlockSpec((None, tk, tn), lambda i, j, k: (j, k, 0)))
    else:
        b_spec = pl.BlockSpec((tn, tk), lambda i, j, k: (j, k)) if tb else pl.BlockSpec((tk, tn), lambda i, j, k: (k, j))
    in_specs = [a_spec, b_spec]
    args = [a, b]
    if add is not None:
        in_specs.append(pl.BlockSpec((tm, tn), lambda i, j, k: (i, j)))
        args.append(add)
    if out_chip:
        out_spec = pl.BlockSpec((None, tm, tn), lambda i, j, k: (j, i, 0))
        out_shape = jax.ShapeDtypeStruct((4, m, tn), out_dtype)
    else:
        out_spec = pl.BlockSpec((tm, tn), lambda i, j, k: (i, j))
        out_shape = jax.ShapeDtypeStruct((m, n), out_dtype)
    res = _pcall(
        body, name=name, grid=(gi, gj, nk), in_specs=in_specs + [_HBM] * nc, out_specs=[out_spec] + [_HBM] * nc,
        out_shape=[out_shape] + riding.out_shape,
        scratch_shapes=([] if nk == 1 else [pltpu.VMEM((tm, tn), F32)]) + riding.sems,
        compiler_params=_cparams(("arbitrary",) * 3 if nc else ("parallel", "parallel", "arbitrary")),
    )(*args, *riding.ins)
    return res[0] if comm is None else (res[0], res[1:])


def _mm_sum(pairs, *, comm, name):
    m, n = pairs[0][0].shape[0], pairs[0][1].shape[1]
    tm, tn = _div(m, 1024, 128), _div(n, 1024, 128)
    tks = [_div(a.shape[1], 1408, 128) for a, _ in pairs]
    nks = [a.shape[1] // tk for (a, _), tk in zip(pairs, tks)]
    offs = [sum(nks[:p]) for p in range(len(pairs))]
    total, npair, nc = sum(nks), len(pairs), comm.n
    gi, gj = m // tm, n // tn

    def body(*refs):
        o_ref, acc = refs[2 * npair + nc], refs[2 * npair + 2 * nc + 1]
        k = pl.program_id(2)
        step = (pl.program_id(0) * gj + pl.program_id(1)) * total + k
        before, after = _comm_phases(comm, refs[2 * npair:2 * npair + nc]
                                     + refs[2 * npair + nc + 1:2 * npair + 2 * nc + 1]
                                     + refs[2 * npair + 2 * nc + 2:], gi * gj * total, step)
        before()
        for p in range(npair):
            def partial_product(p=p):
                part = jnp.dot(refs[2 * p][...], refs[2 * p + 1][...], preferred_element_type=F32)
                if p == 0:
                    @pl.when(k == 0)
                    def _():
                        acc[...] = part

                    @pl.when(k > 0)
                    def _():
                        acc[...] += part
                else:
                    acc[...] += part

            pl.when(jnp.logical_and(k >= offs[p], k < offs[p] + nks[p]))(partial_product)

        @pl.when(k == total - 1)
        def _():
            o_ref[...] = acc[...].astype(o_ref.dtype)

        after()

    def specs(tk, off, nk):
        def kb(k):
            return jnp.clip(k - off, 0, nk - 1)
        return [pl.BlockSpec((tm, tk), lambda i, j, k: (i, kb(k))), pl.BlockSpec((tk, tn), lambda i, j, k: (kb(k), j))]

    in_specs, args = [], []
    for (a, b), tk, off, nk in zip(pairs, tks, offs, nks):
        in_specs += specs(tk, off, nk)
        args += [a, b]
    res = _pcall(
        body, name=name, grid=(gi, gj, total), in_specs=in_specs + [_HBM] * nc,
        out_specs=[pl.BlockSpec((tm, tn), lambda i, j, k: (i, j))] + [_HBM] * nc,
        out_shape=[jax.ShapeDtypeStruct((m, n), BF16)] + comm.out_shape,
        scratch_shapes=[pltpu.VMEM((tm, tn), F32)] + comm.sems,
        compiler_params=_cparams(("arbitrary",) * 3),
    )(*args, *comm.ins)
    return res[0], res[1:]


def _row_spec(br, w, cb):
    return pl.BlockSpec((br, w), lambda i: (i, cb))


def _const_spec(shape):
    return pl.BlockSpec(shape, lambda i: (0,) * len(shape))


def _rows_fwd(fn, rows, consts, outs, *, name, br, acc_shape=None, halo=None):
    s = rows[0][0].shape[0]
    nr, nc = len(rows), len(consts)
    kept = [k for k, o in enumerate(outs) if o is not None]

    def body(*refs):
        xs = [r[...].astype(F32) for r in refs[:nr]]
        cs = [c[...] for c in refs[nr:nr + nc]]
        if halo is not None:
            cs.append(jnp.where(pl.program_id(0) == 0, 0.0, refs[nr + nc][...].astype(F32)))
        res = fn(*xs, *cs)
        orefs = refs[nr + nc + (halo is not None):]
        for j, k in enumerate(kept):
            orefs[j][...] = res[k].astype(orefs[j].dtype)
        if acc_shape is not None:
            acc_ref = orefs[len(kept)]

            @pl.when(pl.program_id(0) == 0)
            def _():
                acc_ref[...] = jnp.zeros_like(acc_ref)

            acc_ref[...] += res[len(outs)]

    in_specs = [_row_spec(br, w, cb) for (_, w, cb) in rows] + [_const_spec(c.shape) for c in consts]
    args = [r[0] for r in rows] + list(consts)
    if halo is not None:
        harr, hw, hcb = rows[halo]
        in_specs.append(pl.BlockSpec((HALO, hw), lambda i: (jnp.maximum(i * (br // HALO) - 1, 0), hcb)))
        args.append(harr)
    out_specs = [_row_spec(br, outs[k][0], 0) for k in kept]
    out_shape = [jax.ShapeDtypeStruct((s, outs[k][0]), outs[k][1]) for k in kept]
    if acc_shape is not None:
        out_specs.append(_const_spec(acc_shape))
        out_shape.append(jax.ShapeDtypeStruct(acc_shape, F32))
    return _pcall(
        body, name=name, grid=(pl.cdiv(s, br),), in_specs=in_specs, out_specs=out_specs, out_shape=out_shape,
        compiler_params=_cparams(("arbitrary",)),
    )(*args)


def _rows_bwd(fn, rows, consts, cots, *, wrt_rows, wrt_consts, drow_dtypes, name, br, unit_cot=False, comm=None):
    comm = _NOTHING if comm is None else comm
    ncomm = comm.n
    nout = len(wrt_rows) + len(wrt_consts)
    s = rows[0][0].shape[0]
    nr, nc = len(rows), len(consts)
    flat_cots = [c for lst in cots for c in lst]
    ncot = len(flat_cots)

    def body(*refs):
        xs = [r[...].astype(F32) for r in refs[:nr]]
        cs = [c[...] for c in refs[nr:nr + nc]]
        cvals = [c[...].astype(F32) for c in refs[nr + nc:nr + nc + ncot]]
        orefs = refs[nr + nc + ncot + ncomm:]
        before, after = _comm_phases(comm, refs[nr + nc + ncot:nr + nc + ncot + ncomm] + orefs[nout:], s // br)
        before()

        def g(*d):
            xs2, cs2 = list(xs), list(cs)
            for j, k in enumerate(wrt_rows):
                xs2[k] = d[j]
            for j, k in enumerate(wrt_consts):
                cs2[k] = d[len(wrt_rows) + j]
            return tuple(fn(*xs2, *cs2))

        prim = [xs[k] for k in wrt_rows] + [cs[k] for k in wrt_consts]
        outs, vjp = jax.vjp(g, *prim)
        ct = []
        pos = 0
        for o, lst in zip(outs, cots):
            if unit_cot:
                ct.append(jnp.ones_like(o))
                continue
            acc = jnp.zeros_like(o)
            for _ in lst:
                acc = acc + cvals[pos]
                pos += 1
            ct.append(acc)
        grads = vjp(tuple(ct))
        for j in range(len(wrt_rows)):
            orefs[j][...] = grads[j].astype(orefs[j].dtype)

        @pl.when(pl.program_id(0) == 0)
        def _():
            for j in range(len(wrt_consts)):
                oref = orefs[len(wrt_rows) + j]
                oref[...] = jnp.zeros_like(oref)

        for j in range(len(wrt_consts)):
            orefs[len(wrt_rows) + j][...] += grads[len(wrt_rows) + j]
        after()

    in_specs = ([_row_spec(br, w, cb) for (_, w, cb) in rows] + [_const_spec(c.shape) for c in consts]
                + [_row_spec(br, w, cb) for (_, w, cb) in flat_cots] + [_HBM] * ncomm)
    out_specs = ([_row_spec(br, rows[k][1], 0) for k in wrt_rows] + [_const_spec(consts[k].shape) for k in wrt_consts]
                 + [_HBM] * ncomm)
    out_shape = ([jax.ShapeDtypeStruct((s, rows[k][1]), dt) for k, dt in zip(wrt_rows, drow_dtypes)]
                 + [jax.ShapeDtypeStruct(consts[k].shape, F32) for k in wrt_consts] + comm.out_shape)
    return _pcall(
        body, name=name, grid=(s // br,), in_specs=in_specs, out_specs=out_specs, out_shape=out_shape,
        scratch_shapes=comm.sems, compiler_params=_cparams(("arbitrary",)),
    )(*[r[0] for r in rows], *consts, *[c[0] for c in flat_cots], *comm.ins)


def _rms(x, w):
    return x * lax.rsqrt(jnp.mean(x * x, axis=-1, keepdims=True) + RMS_EPS) * w


def _f_pre(x, nw, sc, sh):
    return _rms(x, nw) * (1.0 + sc) + sh, x


def _f_pre2(x, o, gt, nw, sc, sh):
    x1 = x + gt * o
    return x1, _rms(x1, nw) * (1.0 + sc) + sh


def _f_fin(x1, f, tgt, gt, nfw):
    y = _rms(x1 + gt * f, nfw)
    return (0.5 * jnp.mean(jnp.square(y - tgt), axis=-1, keepdims=True),)


def _f_comb(o1, o2, o3, l1, l2, l3):
    m = lax.stop_gradient(jnp.maximum(jnp.maximum(l1, l2), l3))
    e1, e2, e3 = jnp.exp(l1 - m), jnp.exp(l2 - m), jnp.exp(l3 - m)
    return ((e1 * o1 + e2 * o2 + e3 * o3) / (e1 + e2 + e3),)


def _f_rwpre(zs, w0, a0, k_k, k_a, wl, e, et):
    r, k, v, zl = zs[:, 0:D], zs[:, D:2 * D], zs[:, 2 * D:3 * D], zs[:, 3 * D:N_RWP]
    lane = lax.broadcasted_iota(jnp.int32, zl.shape, 1)
    t = jnp.where(lane < 64, jnp.tanh(zl), jnp.where(lane < 128, zl, jnp.where(lane < 288, jax.nn.sigmoid(zl), 0.0)))
    lo = _nn(t[:, 0:128], wl[0:128, 0:2 * D])
    g = _nn(t[:, 128:N_LORA], wl[128:N_LORA, 2 * D:3 * D])
    lw = -math.exp(-0.5) * jax.nn.sigmoid(w0 + lo[:, 0:D])
    a = jax.nn.sigmoid(a0 + lo[:, D:2 * D])
    k_mod = k * (1.0 + (a - 1.0) * k_a)
    kk = k * k_k
    kk = kk / jnp.maximum(jnp.sqrt(_hsum(kk * kk, e, et)), 1e-12)
    return r, lw, k_mod, v, -kk, kk * a, g


def _f_rwpost(y, r, v, k_mod, g, lnx_w, lnx_b, r_k, e, et):
    mean = _hsum(y, e, et) * (1.0 / 64)
    yc = y - mean
    var = _hsum(yc * yc, e, et) * (1.0 / 64)
    yn = yc * lax.rsqrt(var + GN_EPS) * lnx_w + lnx_b
    bonus = _hsum(r * k_mod * r_k, e, et) * v
    return ((yn + bonus) * g,)


def _f_mix(gi, ya, yr, bg):
    gate = jax.nn.sigmoid(gi + bg)
    return (gate[:, 0:D] * ya + gate[:, D:2 * D] * yr,)


def _f_adamw(w, g, m, v):
    m = ADAM_B1 * m + (1.0 - ADAM_B1) * g
    v = ADAM_B2 * v + (1.0 - ADAM_B2) * jnp.square(g)
    m_hat = m / (1.0 - ADAM_B1 ** ADAM_STEP)
    v_hat = v / (1.0 - ADAM_B2 ** ADAM_STEP)
    return -ADAM_LR * (m_hat / (jnp.sqrt(v_hat) + ADAM_EPS) + ADAM_WD * w), m, v


def _down(x, k):
    row = lax.broadcasted_iota(jnp.int32, x.shape, 0)
    return jnp.where(row < k, 0.0, pltpu.roll(x, k, 0))


def _up(x, k):
    n = x.shape[0]
    row = lax.broadcasted_iota(jnp.int32, x.shape, 0)
    return jnp.where(row >= n - k, 0.0, pltpu.roll(x, n - k, 0))


def _col_spec(s, w, off=0):
    return pl.BlockSpec((s, w), lambda j: (0, j + off))


def _shift_bwd(z, mu, dzs):
    s, n = z.shape

    def body(z_ref, mu_ref, d_ref, dz_ref, dmu_ref):
        zz, d, m = z_ref[...].astype(F32), d_ref[...], mu_ref[...]
        dm = d * m
        dz_ref[...] = (d - dm + _up(dm, 1)).astype(dz_ref.dtype)
        dmu_ref[...] = jnp.sum(d * (_down(zz, 1) - zz), axis=0, keepdims=True)

    return _pcall(
        body, name="shift_bwd", grid=(n // 128,), in_specs=[_col_spec(s, 128), _col_spec(1, 128), _col_spec(s, 128)],
        out_specs=[_col_spec(s, 128), _col_spec(1, 128)],
        out_shape=[jax.ShapeDtypeStruct((s, n), BF16), jax.ShapeDtypeStruct((1, n), F32)],
        compiler_params=_cparams(("parallel",)),
    )(z, mu, dzs)


def _conv3(x, w_ref, b_ref):
    return b_ref[...] + w_ref[0:1, :] * _down(x, 2) + w_ref[1:2, :] * _down(x, 1) + w_ref[2:3, :] * x


def _conv_fwd(u, cw, cb):
    s = u.shape[0]
    nb = D_FF // 128

    def body(ug_ref, uv_ref, wg_ref, wv_ref, bg_ref, bv_ref, o_ref):
        gate = _conv3(ug_ref[...], wg_ref, bg_ref)
        val = _conv3(uv_ref[...], wv_ref, bv_ref)
        o_ref[...] = (gate * jax.nn.sigmoid(gate) * val).astype(o_ref.dtype)

    return _pcall(
        body, name="conv_fwd", grid=(nb,),
        in_specs=[_col_spec(s, 128), _col_spec(s, 128, nb), _col_spec(3, 128), _col_spec(3, 128, nb),
                  _col_spec(1, 128), _col_spec(1, 128, nb)],
        out_specs=_col_spec(s, 128), out_shape=jax.ShapeDtypeStruct((s, D_FF), BF16),
        compiler_params=_cparams(("parallel",)),
    )(u, u, cw, cw, cb, cb)


def _conv_bwd(u, cw, cb, dact):
    s = u.shape[0]
    nb = D_FF // 128

    def half(x, d, w_ref, du_ref, dw_ref, db_ref):
        x1, x2 = _down(x, 1), _down(x, 2)
        du_ref[...] = (w_ref[2:3, :] * d + w_ref[1:2, :] * _up(d, 1) + w_ref[0:1, :] * _up(d, 2)).astype(du_ref.dtype)
        dw_ref[0:1, :] = jnp.sum(d * x2, axis=0, keepdims=True)
        dw_ref[1:2, :] = jnp.sum(d * x1, axis=0, keepdims=True)
        dw_ref[2:3, :] = jnp.sum(d * x, axis=0, keepdims=True)
        db_ref[...] = jnp.sum(d, axis=0, keepdims=True)

    def body(ug_ref, uv_ref, wg_ref, wv_ref, bg_ref, bv_ref, da_ref,
             dug_ref, duv_ref, dwg_ref, dwv_ref, dbg_ref, dbv_ref):
        ug, uv, da = ug_ref[...], uv_ref[...], da_ref[...]
        gate = _conv3(ug, wg_ref, bg_ref)
        val = _conv3(uv, wv_ref, bv_ref)
        sg = jax.nn.sigmoid(gate)
        dgate = da * val * sg * (1.0 + gate * (1.0 - sg))
        dval = da * gate * sg
        half(ug, dgate, wg_ref, dug_ref, dwg_ref, dbg_ref)
        half(uv, dval, wv_ref, duv_ref, dwv_ref, dbv_ref)

    dug, duv, dwg, dwv, dbg, dbv = _pcall(
        body, name="conv_bwd", grid=(nb,),
        in_specs=[_col_spec(s, 128), _col_spec(s, 128, nb), _col_spec(3, 128), _col_spec(3, 128, nb),
                  _col_spec(1, 128), _col_spec(1, 128, nb), _col_spec(s, 128)],
        out_specs=[_col_spec(s, 128), _col_spec(s, 128), _col_spec(3, 128), _col_spec(3, 128),
                   _col_spec(1, 128), _col_spec(1, 128)],
        out_shape=[jax.ShapeDtypeStruct((s, D_FF), BF16), jax.ShapeDtypeStruct((s, D_FF), BF16),
                   jax.ShapeDtypeStruct((3, D_FF), F32), jax.ShapeDtypeStruct((3, D_FF), F32),
                   jax.ShapeDtypeStruct((1, D_FF), F32), jax.ShapeDtypeStruct((1, D_FF), F32)],
        compiler_params=_cparams(("parallel",)),
    )(u, u, cw, cw, cb, cb, dact)
    return (jnp.concatenate([dug, duv], axis=1), jnp.concatenate([dwg, dwv], axis=1),
            jnp.concatenate([dbg, dbv], axis=1))


ATT_BATCH = 4


def _att_batch(q, kp, kc, vp, vc, first):
    ma = lax.broadcasted_iota(jnp.int32, (1, ATT_BLOCK, 128), 2) < 64

    def diag(x):
        return jnp.concatenate([jnp.where(ma, x, 0.0), jnp.where(ma, 0.0, x)], axis=1)

    qi = lax.broadcasted_iota(jnp.int32, (1, ATT_BLOCK, 2 * ATT_BLOCK), 1)
    kj = lax.broadcasted_iota(jnp.int32, (1, ATT_BLOCK, 2 * ATT_BLOCK), 2) & (ATT_BLOCK - 1)
    okp = kj >= qi + jnp.where(first, 2 * ATT_BLOCK, 0)
    okc = kj <= qi
    sp = jnp.where(okp, _bnt(q, diag(kp)) * 0.125, NEG)
    sc = jnp.where(okc, _bnt(q, diag(kc)) * 0.125, NEG)

    def per_head(fn, x):
        return fn(x[..., :ATT_BLOCK]), fn(x[..., ATT_BLOCK:])

    def spread(ab):
        return jnp.concatenate([jnp.broadcast_to(t, t.shape[:2] + (ATT_BLOCK,)) for t in ab], axis=-1)

    row_max = functools.partial(jnp.max, axis=-1, keepdims=True)
    row_sum = functools.partial(jnp.sum, axis=-1, keepdims=True)
    m = [lax.stop_gradient(jnp.maximum(a, b)) for a, b in zip(per_head(row_max, sp), per_head(row_max, sc))]
    pp, pc = jnp.exp(sp - spread(m)), jnp.exp(sc - spread(m))
    den = [a + b for a, b in zip(per_head(row_sum, pp), per_head(row_sum, pc))]
    num = _bnn(pp, diag(vp)) + _bnn(pc, diag(vc))
    out = num / jnp.where(ma, den[0], den[1])
    lse = jnp.where(ma, m[0] + jnp.log(den[0]), m[1] + jnp.log(den[1]))
    return out, jnp.broadcast_to(lse, out.shape)


def _att_pairs_per_step(dil):
    return 4 if dil == 1 else 1


def _att_residues(dil):
    return min(dil, ATT_BATCH // _att_pairs_per_step(dil))


def _att_specs(g, dil):
    rows, pp = ATT_BLOCK * dil, _att_pairs_per_step(dil)

    def cur(slot):
        return pl.BlockSpec((rows, 128 * pp), lambda n, p: (n, (g * 3 + slot) * (4 // pp) + p))

    def prev(slot):
        return pl.BlockSpec((rows, 128 * pp), lambda n, p: (jnp.maximum(n - 1, 0), (g * 3 + slot) * (4 // pp) + p))

    return [cur(0), prev(1), cur(1), prev(2), cur(2)]


def _att_out_spec(dil):
    return pl.BlockSpec((ATT_BLOCK * dil, 128 * _att_pairs_per_step(dil)), lambda n, p: (n, p))


def _att_grid(s, dil):
    return (s // (ATT_BLOCK * dil), 4 // _att_pairs_per_step(dil))


def _att_windows(i, dil):
    res = _att_residues(dil)

    def rows(r):
        return pl.ds(i * res + r, ATT_BLOCK, stride=dil) if dil > 1 else pl.ds(0, ATT_BLOCK)

    return [(rows(r), pl.ds(128 * j, 128)) for j in range(_att_pairs_per_step(dil)) for r in range(res)]


def _att_fwd(att_in, g, dil):
    s = att_in.shape[0]

    def body(q_ref, kp_ref, kc_ref, vp_ref, vc_ref, o_ref, l_ref):
        first = pl.program_id(0) == 0

        def one(i, carry):
            win = _att_windows(i, dil)
            vals = [jnp.stack([ref[w] for w in win]) for ref in (q_ref, kp_ref, kc_ref, vp_ref, vc_ref)]
            o, l = _att_batch(*vals, first)
            for j, w in enumerate(win):
                o_ref[w] = o[j]
                l_ref[w] = l[j]
            return carry

        lax.fori_loop(0, dil // _att_residues(dil), one, 0)

    return _pcall(
        body, name=f"att_fwd{g}", grid=_att_grid(s, dil), in_specs=_att_specs(g, dil),
        out_specs=[_att_out_spec(dil)] * 2, out_shape=[jax.ShapeDtypeStruct((s, ATT_WIDTH), F32)] * 2,
        compiler_params=_cparams(("parallel", "parallel")),
    )(att_in, att_in, att_in, att_in, att_in)


def _att_bwd(att_in, g, dil, do, dl, acc):
    s = att_in.shape[0]

    def body(q_ref, kp_ref, kc_ref, vp_ref, vc_ref, do_ref, dl_ref, dq_ref, dkp_ref, dkc_ref, dvp_ref, dvc_ref):
        first = pl.program_id(0) == 0

        def one(i, carry):
            win = _att_windows(i, dil)
            vals = [jnp.stack([ref[w] for w in win]) for ref in (q_ref, kp_ref, kc_ref, vp_ref, vc_ref)]
            _, vjp = jax.vjp(lambda *a: _att_batch(*a, first), *vals)
            grads = vjp((jnp.stack([do_ref[w] for w in win]), jnp.stack([dl_ref[w] for w in win])))
            for ref, gr in zip((dq_ref, dkp_ref, dkc_ref, dvp_ref, dvc_ref), grads):
                for j, w in enumerate(win):
                    ref[w] = gr[j]
            return carry

        lax.fori_loop(0, dil // _att_residues(dil), one, 0)

    dq, dkp, dkc, dvp, dvc = _pcall(
        body, name=f"att_bwd{g}", grid=_att_grid(s, dil), in_specs=_att_specs(g, dil) + [_att_out_spec(dil)] * 2,
        out_specs=[_att_out_spec(dil)] * 5, out_shape=[jax.ShapeDtypeStruct((s, ATT_WIDTH), F32)] * 5,
        compiler_params=_cparams(("parallel", "parallel")),
    )(att_in, att_in, att_in, att_in, att_in, do, dl)

    unit, rb = ATT_BLOCK * dil, 1024
    steps = s // rb
    within = unit < rb

    def shifted(cur_ref, next_ref, has_next):
        nxt = jnp.where(has_next, next_ref[...], 0.0)
        return jnp.concatenate([cur_ref[unit:, :], nxt], axis=0) if within else nxt

    def cbody(dq_ref, dkc_ref, dkp_ref, dkn_ref, dvc_ref, dvp_ref, dvn_ref, *rest):
        o_ref = rest[-1]
        has_next = pl.program_id(0) + (1 if within else unit // rb) < steps
        o_ref[:, 0:ATT_WIDTH] = dq_ref[...].astype(BF16)
        o_ref[:, ATT_WIDTH:2 * ATT_WIDTH] = (dkc_ref[...] + shifted(dkp_ref, dkn_ref, has_next)).astype(BF16)
        o_ref[:, 2 * ATT_WIDTH:3 * ATT_WIDTH] = (dvc_ref[...] + shifted(dvp_ref, dvn_ref, has_next)).astype(BF16)

    cur = pl.BlockSpec((rb, ATT_WIDTH), lambda i: (i, 0))
    if within:
        nxt = pl.BlockSpec((unit, ATT_WIDTH), lambda i: (jnp.minimum((i + 1) * (rb // unit), s // unit - 1), 0))
    else:
        nxt = pl.BlockSpec((rb, ATT_WIDTH), lambda i: (jnp.minimum(i + unit // rb, steps - 1), 0))
    carried = [] if acc is None else [acc]
    return _pcall(
        cbody, name=f"att_bwd_sum{g}", grid=(steps,),
        in_specs=[cur, cur, cur, nxt, cur, cur, nxt] + [pl.BlockSpec(memory_space=pl.ANY)] * len(carried),
        out_specs=pl.BlockSpec((rb, 3 * ATT_WIDTH), lambda i: (i, g)),
        out_shape=jax.ShapeDtypeStruct((s, N_ATT), BF16), input_output_aliases={7: 0} if carried else {},
        compiler_params=_cparams(("parallel",)),
    )(dq, dkc, dkp, dkp, dvc, dvp, dvp, *carried)


def _cumsum_rows_impl(x):
    row = lax.broadcasted_iota(jnp.int32, x.shape, 0)
    shift = 1
    while shift < x.shape[0]:
        x = x + jnp.where(row >= shift, pltpu.roll(x, shift, 0), 0.0)
        shift *= 2
    return x


@jax.custom_vjp
def _cumsum_rows(x):
    return _cumsum_rows_impl(x)


_cumsum_rows.defvjp(lambda x: (_cumsum_rows_impl(x), None),
                    lambda _, g: (jnp.sum(g, axis=0, keepdims=True) - _cumsum_rows_impl(g) + g,))


def _unit_lower_inverse_impl(n):
    eye = (lax.broadcasted_iota(jnp.int32, (1,) + n.shape[1:], 1)
           == lax.broadcasted_iota(jnp.int32, (1,) + n.shape[1:], 2))
    t = jnp.where(eye, 1.0, 0.0) + n
    pw = n
    for _ in range(5):
        pw = _bnn(pw, pw)
        t = t + _bnn(t, pw)
    return t


@jax.custom_vjp
def _unit_lower_inverse(n):
    return _unit_lower_inverse_impl(n)


def _unit_lower_inverse_fwd(n):
    t = _unit_lower_inverse_impl(n)
    return t, t


_unit_lower_inverse.defvjp(_unit_lower_inverse_fwd, lambda t, g: (_bnt(_btn(t, g), t),))


@jax.custom_vjp
def _known_inverse(n, t):
    return t


_known_inverse.defvjp(lambda n, t: (t, t), lambda t, g: (_bnt(_btn(t, g), t), jnp.zeros_like(t)))


def _scan_chunk(r, lw, k, v, a, b, s0, inverse):
    c = SCAN_CHUNK
    p = s0.shape[0]
    cum = _cumsum_rows(lw)
    tot = jnp.sum(lw, axis=0, keepdims=True)
    ma = (lax.broadcasted_iota(jnp.int32, (c, 128 * p), 1) & 127) < 64

    def pairs(x):
        return jnp.concatenate([x[None, :, 128 * j:128 * (j + 1)] for j in range(p)], axis=0)

    def stack(x):
        return jnp.concatenate([pairs(jnp.where(ma, x, 0.0)), pairs(jnp.where(ma, 0.0, x))], axis=1)

    einv, eend = jnp.exp(-cum), jnp.exp(tot - cum)
    ra, aa = stack(r * jnp.exp(cum)), stack(a * jnp.exp(cum - lw))
    bi, ki, be, ke, vs = stack(b * einv), stack(k * einv), stack(b * eend), stack(k * eend), stack(v)
    r2 = lax.broadcasted_iota(jnp.int32, (1, 2 * c, 2 * c), 1)
    c2 = lax.broadcasted_iota(jnp.int32, (1, 2 * c, 2 * c), 2)
    same = (r2 >= c) == (c2 >= c)
    strict = jnp.logical_and(same, c2 < r2)
    incl = jnp.logical_and(same, c2 <= r2)
    s0 = jnp.where(same, s0, 0.0)
    prod = _bnt(jnp.concatenate([aa, ra], axis=1), jnp.concatenate([bi, ki], axis=1))
    a_ab = jnp.where(strict, prod[:, :2 * c, :2 * c], 0.0)
    a_ak = jnp.where(strict, prod[:, :2 * c, 2 * c:], 0.0)
    a_rb = jnp.where(incl, prod[:, 2 * c:, :2 * c], 0.0)
    a_rk = jnp.where(incl, prod[:, 2 * c:, 2 * c:], 0.0)
    t = inverse(a_ab)
    u = _bnn(t, _bnt(aa, s0) + _bnn(a_ak, vs))
    uv = jnp.concatenate([u, vs], axis=1)
    ys = _bnt(ra, s0) + _bnn(jnp.concatenate([a_rb, a_rk], axis=2), uv)
    s1 = s0 * pairs(jnp.exp(tot)) + _btn(uv, jnp.concatenate([be, ke], axis=1))
    y3 = ys[:, :c] + ys[:, c:]
    return (jnp.concatenate([y3[j] for j in range(p)], axis=1), s1), t


def _scan_specs(rev, n):
    def at(i):
        return n - 1 - i if rev else i

    def cm(cb):
        return pl.BlockSpec((SCAN_CHUNK, D), lambda i: (at(i), cb))

    return cm, pl.BlockSpec((1, SCAN_PAIRS, 128, 128), lambda i: (at(i), 0, 0, 0))


def _comm_phases(comm, refs, n, step=None):
    k = comm.n
    srcs, outs, sems = refs[:k], refs[k:2 * k], refs[2 * k:]
    i = pl.program_id(0) if step is None else step

    def before():
        @pl.when(i == 0)
        def _():
            comm.first(srcs, outs, sems)

    def after():
        if comm.mid is not None:
            @pl.when(i == (3 * n) // 4)
            def _():
                comm.mid(srcs, outs, sems)

        @pl.when(i == n - 1)
        def _():
            comm.last(srcs, outs, sems)

    return before, after


def _scan_fwd(zs, lw, km, aa, bb, comm):
    s = zs.shape[0]
    n = s // SCAN_CHUNK
    cm, st = _scan_specs(False, n)
    k = comm.n

    def body(*refs):
        r_ref, lw_ref, k_ref, v_ref, a_ref, b_ref = refs[:6]
        y_ref, s0_ref, t_ref = refs[6 + k:9 + k]
        state = refs[9 + 2 * k]
        before, after = _comm_phases(comm, refs[6:6 + k] + refs[9 + k:9 + 2 * k] + refs[10 + 2 * k:], n)
        before()

        @pl.when(pl.program_id(0) == 0)
        def _():
            state[...] = jnp.zeros_like(state)

        s0 = state[...]
        s0_ref[0] = s0
        (y, s1), t = _scan_chunk(*[ref[...] for ref in (r_ref, lw_ref, k_ref, v_ref, a_ref, b_ref)], s0,
                                 _unit_lower_inverse)
        y_ref[...] = y
        t_ref[0] = t.astype(BF16)
        state[...] = s1
        after()

    per_chunk = (n, SCAN_PAIRS, 128, 128)
    res = _pcall(
        body, name="scan_fwd", grid=(n,), in_specs=[cm(0), cm(0), cm(0), cm(2), cm(0), cm(0)] + [_HBM] * k,
        out_specs=[cm(0), st, st] + [_HBM] * k,
        out_shape=[jax.ShapeDtypeStruct((s, D), F32), jax.ShapeDtypeStruct(per_chunk, F32),
                   jax.ShapeDtypeStruct(per_chunk, BF16)] + comm.out_shape,
        scratch_shapes=[pltpu.VMEM((SCAN_PAIRS, 128, 128), F32)] + comm.sems,
        compiler_params=_cparams(("arbitrary",)),
    )(zs, lw, km, zs, aa, bb, *comm.ins)
    return res[0], res[1], res[2], res[3:]


def _scan_bwd(zs, lw, km, aa, bb, s0s, ts, dy, comm):
    s = zs.shape[0]
    n = s // SCAN_CHUNK
    cm, st = _scan_specs(True, n)
    k = comm.n

    def body(*refs):
        r_ref, lw_ref, k_ref, v_ref, a_ref, b_ref, s0_ref, t_ref, dy_ref = refs[:9]
        douts = refs[9 + k:15 + k]
        dstate = refs[15 + 2 * k]
        before, after = _comm_phases(comm, refs[9:9 + k] + refs[15 + k:15 + 2 * k] + refs[16 + 2 * k:], n)
        before()

        @pl.when(pl.program_id(0) == 0)
        def _():
            dstate[...] = jnp.zeros_like(dstate)

        t = t_ref[0].astype(F32)
        prim = [ref[...] for ref in (r_ref, lw_ref, k_ref, v_ref, a_ref, b_ref)] + [s0_ref[0]]
        _, vjp, _ = jax.vjp(lambda *p: _scan_chunk(*p, lambda nil: _known_inverse(nil, t)), *prim, has_aux=True)
        grads = vjp((dy_ref[...], dstate[...]))
        for ref, gr in zip(douts, grads[:6]):
            ref[...] = gr
        dstate[...] = grads[6]
        after()

    res = _pcall(
        body, name="scan_bwd", grid=(n,),
        in_specs=[cm(0), cm(0), cm(0), cm(2), cm(0), cm(0), st, st, cm(0)] + [_HBM] * k,
        out_specs=[cm(0)] * 6 + [_HBM] * k, out_shape=[jax.ShapeDtypeStruct((s, D), F32)] * 6 + comm.out_shape,
        scratch_shapes=[pltpu.VMEM((SCAN_PAIRS, 128, 128), F32)] + comm.sems,
        compiler_params=_cparams(("arbitrary",)),
    )(zs, lw, km, zs, aa, bb, s0s, ts, dy, *comm.ins)
    return res[:6], res[6:]


_HBM = pl.BlockSpec(memory_space=pltpu.HBM)


def _me():
    return lax.axis_index("x"), lax.axis_index("y"), lax.axis_index("c")


def _allgather8(src, name):
    def body(src_ref, out_ref, ssem, rsem, lsem):
        x, y, c = _me()
        me = 4 * x + 2 * y + c
        local = pltpu.make_async_copy(src_ref, out_ref.at[me], lsem)
        local.start()
        peers = []
        for k in range(1, 8):
            peers.append(((1 - x) if k & 4 else x, (1 - y) if k & 2 else y, (1 - c) if k & 1 else c))
        sends = []
        for k, peer in enumerate(peers):
            cp = pltpu.make_async_remote_copy(src_ref, out_ref.at[me], ssem.at[k], rsem.at[k], device_id=peer,
                                              device_id_type=MESH)
            cp.start()
            sends.append(cp)
        for k, (px, py, pc) in enumerate(peers):
            pltpu.make_async_remote_copy(src_ref, out_ref.at[4 * px + 2 * py + pc], ssem.at[k], rsem.at[k],
                                         device_id=(px, py, pc), device_id_type=MESH).wait_recv()
        for cp in sends:
            cp.wait_send()
        local.wait()

    return _pcall(
        body, name=name, in_specs=[_HBM], out_specs=_HBM, out_shape=jax.ShapeDtypeStruct((8,) + src.shape, src.dtype),
        scratch_shapes=[pltpu.SemaphoreType.DMA((7,)), pltpu.SemaphoreType.DMA((7,)), pltpu.SemaphoreType.DMA],
    )(src)


def _other_chips(x, y):
    return [(1 - x, y), (x, 1 - y), (1 - x, 1 - y)]


def _remote(src, dst, ssem, rsem, to):
    return pltpu.make_async_remote_copy(src, dst, ssem, rsem, device_id=to, device_id_type=MESH)


class _GatherWeights:
    def __init__(self, shards):
        self.ins = list(shards)
        n = self.n = len(shards)
        self.out_shape = [jax.ShapeDtypeStruct((4,) + t.shape, t.dtype) for t in shards]
        self.sems = [pltpu.SemaphoreType.DMA((6 * n,)), pltpu.SemaphoreType.DMA((6 * n,)),
                     pltpu.SemaphoreType.DMA((n,)), pltpu.SemaphoreType.DMA((n,))]

    def _copies(self, srcs, outs, sems):
        ssem, rsem, lsem, osem = sems
        x, y, c = _me()
        me = 2 * x + y
        own, ici, landed, passed, passed_in = [], [], [], [], []
        for a in range(self.n):
            h = self.ins[a].shape[0] // 2
            mine, other = pl.ds(c * h, h), pl.ds((1 - c) * h, h)
            own.append(_remote(srcs[a], outs[a].at[me], lsem.at[a], osem.at[a], (x, y, 1 - c)))
            for k, (px, py) in enumerate(_other_chips(x, y)):
                s1, r1, s2, r2 = ssem.at[6 * a + k], rsem.at[6 * a + k], ssem.at[6 * a + 3 + k], rsem.at[6 * a + 3 + k]
                got, got_sib = outs[a].at[2 * px + py, mine], outs[a].at[2 * px + py, other]
                ici.append(_remote(srcs[a].at[mine], outs[a].at[me, mine], s1, r1, (px, py, c)))
                landed.append(_remote(got, got, s1, r1, (px, py, c)))
                passed.append(_remote(got, got, s2, r2, (x, y, 1 - c)))
                passed_in.append(_remote(got_sib, got_sib, s2, r2, (x, y, 1 - c)))
        return own, ici, landed, passed, passed_in

    def first(self, srcs, outs, sems):
        own, ici, _, _, _ = self._copies(srcs, outs, sems)
        for cp in own + ici:
            cp.start()

    def mid(self, srcs, outs, sems):
        _, _, landed, passed, _ = self._copies(srcs, outs, sems)
        for arrived, onward in zip(landed, passed):
            arrived.wait_recv()
            onward.start()

    def last(self, srcs, outs, sems):
        own, ici, _, passed, passed_in = self._copies(srcs, outs, sems)
        for cp in passed_in:
            cp.wait_recv()
        for cp in ici + passed:
            cp.wait_send()
        for cp in own:
            cp.wait()


class _ScatterToChips:
    def __init__(self, parts):
        self.ins = list(parts)
        n = self.n = len(parts)
        self.out_shape = [jax.ShapeDtypeStruct(t.shape, t.dtype) for t in parts]
        self.sems = [pltpu.SemaphoreType.DMA((3 * n,)), pltpu.SemaphoreType.DMA((3 * n,)), pltpu.SemaphoreType.DMA((n,))]

    def _copies(self, srcs, outs, sems):
        ssem, rsem, lsem = sems
        x, y, c = _me()
        me = 2 * x + y
        own, out, landed = [], [], []
        for a in range(self.n):
            own.append(pltpu.make_async_copy(srcs[a].at[me], outs[a].at[me], lsem.at[a]))
            for k, (px, py) in enumerate(_other_chips(x, y)):
                dst = outs[a].at[2 * px + py]
                out.append(_remote(srcs[a].at[2 * px + py], outs[a].at[me], ssem.at[3 * a + k], rsem.at[3 * a + k],
                                   (px, py, c)))
                landed.append(_remote(dst, dst, ssem.at[3 * a + k], rsem.at[3 * a + k], (px, py, c)))
        return own, out, landed

    def first(self, srcs, outs, sems):
        own, out, _ = self._copies(srcs, outs, sems)
        for cp in own + out:
            cp.start()

    mid = None

    def last(self, srcs, outs, sems):
        own, out, landed = self._copies(srcs, outs, sems)
        for cp in landed:
            cp.wait_recv()
        for cp in own:
            cp.wait()
        for cp in out:
            cp.wait_send()


def _run_comm(comm, name):
    n = comm.n

    def body(*refs):
        srcs, outs, sems = refs[:n], refs[n:2 * n], refs[2 * n:]
        comm.first(srcs, outs, sems)
        if comm.mid is not None:
            comm.mid(srcs, outs, sems)
        comm.last(srcs, outs, sems)

    return _pcall(body, name=name, in_specs=[_HBM] * n, out_specs=[_HBM] * n, out_shape=comm.out_shape,
                  scratch_shapes=comm.sems)(*comm.ins)


class _NoComm:
    n, ins, out_shape, sems, mid = 0, [], [], [], None

    def first(self, srcs, outs, sems):
        pass

    def last(self, srcs, outs, sems):
        pass


_NOTHING = _NoComm()


class _SiblingHalves:
    mid = None

    def __init__(self, grads):
        self.ins = list(grads)
        n = self.n = len(grads)
        self.out_shape = [jax.ShapeDtypeStruct((4, t.shape[1] // 2, t.shape[2]), t.dtype) for t in grads]
        self.sems = [pltpu.SemaphoreType.DMA((n,)), pltpu.SemaphoreType.DMA((n,))]

    def _copies(self, srcs, outs, sems):
        ssem, rsem = sems
        x, y, c = _me()
        copies = []
        for a in range(self.n):
            h = self.ins[a].shape[1] // 2
            copies.append(_remote(srcs[a].at[:, pl.ds((1 - c) * h, h)], outs[a], ssem.at[a], rsem.at[a], (x, y, 1 - c)))
        return copies

    def first(self, srcs, outs, sems):
        for cp in self._copies(srcs, outs, sems):
            cp.start()

    def last(self, srcs, outs, sems):
        for cp in self._copies(srcs, outs, sems):
            cp.wait()


def _reduce_finish(reds, name):
    n = len(reds)

    def body(*refs):
        outs = refs[n:2 * n]
        ssem, rsem = refs[2 * n:]
        x, y, c = _me()
        copies = []
        for a in range(n):
            h = reds[a].shape[0] // 2
            mine = outs[a].at[pl.ds(c * h, h)]
            copies.append(_remote(mine, mine, ssem.at[a], rsem.at[a], (x, y, 1 - c)))
        for cp in copies:
            cp.start()
        for a in range(n):
            h = reds[a].shape[0] // 2
            dst = outs[a].at[pl.ds((1 - c) * h, h)]
            _remote(dst, dst, ssem.at[a], rsem.at[a], (x, y, 1 - c)).wait_recv()
        for cp in copies:
            cp.wait_send()

    return _pcall(
        body, name=name, in_specs=[_HBM] * n, out_specs=[_HBM] * n,
        out_shape=[jax.ShapeDtypeStruct(t.shape, t.dtype) for t in reds],
        input_output_aliases={a: a for a in range(n)},
        scratch_shapes=[pltpu.SemaphoreType.DMA((n,)), pltpu.SemaphoreType.DMA((n,))],
    )(*reds)


def _half_sum(fn, full, halves, out_full, out_dtype, core, name):
    p, h, c = (halves[0].shape if halves else (full[0].shape[0], full[0].shape[1] // 2, full[0].shape[2]))
    br = _div(h, max(16, (1 << 19) // (p * c)), 16)
    nb = h // br
    mine3 = pl.BlockSpec((p, br, c), lambda i, core_ref: (0, core_ref[0] * nb + i, 0))
    half3 = pl.BlockSpec((p, br, c), lambda i, core_ref: (0, i, 0))

    def body(core_ref, *refs):
        refs[-1][...] = fn(*[t[...].astype(F32) for t in refs[:-1]]).astype(out_dtype)

    if out_full:
        out_spec = pl.BlockSpec((br, c), lambda i, core_ref: (core_ref[0] * nb + i, 0))
        out_shape = jax.ShapeDtypeStruct((2 * h, c), out_dtype)
    else:
        out_spec, out_shape = half3, jax.ShapeDtypeStruct((p, h, c), out_dtype)
    return _pcall(
        body, name=name,
        grid_spec=pltpu.PrefetchScalarGridSpec(
            num_scalar_prefetch=1, grid=(nb,), in_specs=[mine3] * len(full) + [half3] * len(halves),
            out_specs=out_spec),
        out_shape=out_shape, compiler_params=_cparams(("parallel",)),
    )(core, *full, *halves)


def _ada_fwd(c_all, w, b):
    def body(c_ref, w_ref, b_ref, o_ref):
        o_ref[...] = jnp.dot(c_ref[...], w_ref[...], precision=HI, preferred_element_type=F32) + b_ref[...]

    return _pcall(body, name="ada_fwd", out_shape=jax.ShapeDtypeStruct((c_all.shape[0], w.shape[1]), F32),
                  compiler_params=pltpu.CompilerParams(vmem_limit_bytes=VMEM_LIMIT))(c_all, w, b)


def _ada_bwd(c_all_t, d):
    def body(c_ref, d_ref, o_ref):
        o_ref[...] = jnp.dot(c_ref[...], d_ref[...], precision=HI, preferred_element_type=F32)

    return _pcall(body, name="ada_bwd", out_shape=jax.ShapeDtypeStruct((c_all_t.shape[0], d.shape[1]), F32),
                  compiler_params=pltpu.CompilerParams(vmem_limit_bytes=VMEM_LIMIT))(c_all_t, d)


def _sum_lead(x, name):
    p, r, n = x.shape
    br = _div(r, 512, 8)

    def body(x_ref, o_ref):
        acc = x_ref[0]
        for j in range(1, p):
            acc = acc + x_ref[j]
        o_ref[...] = acc

    return _pcall(
        body, name=name, grid=(r // br,), in_specs=[pl.BlockSpec((p, br, n), lambda i: (0, i, 0))],
        out_specs=pl.BlockSpec((br, n), lambda i: (i, 0)), out_shape=jax.ShapeDtypeStruct((r, n), F32),
        compiler_params=_cparams(("parallel",)),
    )(x)


def _adamw(w, g, m, v, name):
    shape = w.shape
    cols = shape[-1]
    w2, g2, m2, v2 = [t.reshape(-1, cols) for t in (w, g, m, v)]
    rows = w2.shape[0]
    pref = max(8, (1 << 19) // cols // 8 * 8)
    br = _div(rows, pref, 8)
    if rows // br > 64:
        br = pref
    outs = _rows_fwd(_f_adamw, [(t, cols, 0) for t in (w2, g2, m2, v2)], [], [(cols, F32)] * 3, name=name, br=br)
    return [o.reshape(shape) for o in outs]


_BIG = (("w_in", 1), ("w_up", 1), ("w_down", 0), ("w_o", 0), ("w_rwkv_out", 0), ("w_att_out", 1), ("w2", 1), ("a2", 1),
        ("g2", 1))


_NEEDED_FIRST = ("w_in", "w_att_out", "w2", "a2", "g2")
_NEEDED_LATER = ("w_up", "w_down", "w_o", "w_rwkv_out")
_DONE_EARLY = ("w_up", "w_down", "w_o", "w_rwkv_out", "w_att_out")
_DONE_LATE = ("w_in", "w2", "a2", "g2")


def _cols_joined(t):
    return jnp.concatenate([t[j] for j in range(4)], axis=1)


def _cols_split(t):
    n = t.shape[1] // 4
    return jnp.stack([t[:, j * n:(j + 1) * n] for j in range(4)])


W_IN_SHARD = (N_ATT + N_RW + N_GATE) // 4
W_IN_PAD = 2560


def _row_window(parts, lo, hi):
    out, pos = [], 0
    for t, w in parts:
        a, b = max(lo, pos), min(hi, pos + w)
        if a < b:
            out.append(t[a - pos:b - pos])
        pos += w
    return out[0] if len(out) == 1 else jnp.concatenate(out, axis=0)


def _rows_joined(t):
    return t.reshape(4 * t.shape[1], t.shape[2])


def _rows_split(t):
    return t.reshape(4, t.shape[0] // 4, t.shape[1])


def _step_to_scan(x, tgt, ada, wts):
    sh1, sc1, gt1, sh2, sc2, gt2 = ada
    br = 256
    grp = lax.broadcasted_iota(jnp.int32, (D, 128), 0) // 64 == lax.broadcasted_iota(jnp.int32, (D, 128), 1)
    e = grp.astype(F32)
    et = e.T
    w_in = [(wts["w_in"][j], W_IN_SHARD) for j in range(4)]
    w_att = _row_window(w_in, 0, N_ATT)
    w_rw = jnp.concatenate([_row_window(w_in, N_ATT, N_ATT + N_RW), jnp.zeros((N_RWP - N_RW, D), BF16)], axis=0)
    w_gate = _row_window(w_in, N_ATT + N_RW, N_ATT + N_RW + N_GATE)
    mu = jnp.pad(wts["mu_shift"], ((0, 0), (0, N_RWP - N_RW)))
    wl = jnp.zeros((N_LORA, 3 * D), F32)
    wl = wl.at[0:64, 0:D].set(_cols_joined(wts["w2"]).astype(F32))
    wl = wl.at[64:128, D:2 * D].set(_cols_joined(wts["a2"]).astype(F32))
    wl = wl.at[128:288, 2 * D:3 * D].set(_cols_joined(wts["g2"]).astype(F32))
    pre1_c = [wts["norm1_w"], sc1, sh1]
    (h1,) = _rows_fwd(_f_pre, [(x, D, 0)], pre1_c, [(D, BF16), None], name="pre1_fwd", br=2 * br)
    att_in = _mm(h1, w_att, tb=True, name="mm_att_in")
    z = _mm(h1, w_rw, tb=True, out_dtype=BF16, name="mm_rw_in")
    gate_in = _mm(h1, w_gate, tb=True, out_dtype=BF16, name="mm_gate_in")
    att_o, att_l = [], []
    for g, (_, dil) in enumerate(ATT_PATTERNS):
        o, l = _att_fwd(att_in, g, dil)
        att_o.append(o)
        att_l.append(l)
    comb_rows = [(t, ATT_WIDTH, 0) for t in att_o + att_l]
    (att,) = _rows_fwd(_f_comb, comb_rows, [], [(ATT_WIDTH, BF16)], name="comb_fwd", br=2 * br)
    y_att = _mm(att, wts["w_att_out"], b_chip=True, out_dtype=BF16, name="mm_att_out")
    rwpre_c = [wts["w0"], wts["a0"], wts["k_k"], wts["k_a"], wl, e, et]

    def shift_and_rwpre(zz, *rest):
        consts, mu_row, before = rest[:-2], rest[-2], rest[-1]
        last = jnp.sum(jnp.where(lax.broadcasted_iota(jnp.int32, before.shape, 0) == HALO - 1, before, 0.0), axis=0,
                       keepdims=True)
        row = lax.broadcasted_iota(jnp.int32, zz.shape, 0)
        zprev = jnp.where(row == 0, last, pltpu.roll(zz, 1, 0))
        shifted = zz + (zprev - zz) * mu_row
        return (shifted,) + tuple(_f_rwpre(shifted, *consts))

    zs, lw, km, aa, bb, gg = _rows_fwd(
        shift_and_rwpre, [(z, N_RWP, 0)], rwpre_c + [mu],
        [(N_RWP, F32), None, (D, F32), (D, F32), None, (D, F32), (D, F32), (D, F32)], name="rwpre_fwd", br=br, halo=0)
    return dict(x=x, tgt=tgt, wts=wts, br=br, e=e, et=et, gt1=gt1, sc2=sc2, sh2=sh2, gt2=gt2, w_att=w_att, w_rw=w_rw,
                w_gate=w_gate, mu=mu, pre1_c=pre1_c, h1=h1, att_in=att_in, z=z, gate_in=gate_in, comb_rows=comb_rows,
                att=att, y_att=y_att, zs=zs, rwpre_c=rwpre_c, lw=lw, km=km, aa=aa, bb=bb, gg=gg)


def _step_between_scans(st, y_raw, late):
    x, tgt, wts, br, e, et = st["x"], st["tgt"], st["wts"], st["br"], st["e"], st["et"]
    zs, km, gg, gate_in, y_att, att = st["zs"], st["km"], st["gg"], st["gate_in"], st["y_att"], st["att"]
    comb_rows, att_in = st["comb_rows"], st["att_in"]
    gt1, sc2, sh2, gt2 = st["gt1"], st["sc2"], st["sh2"], st["gt2"]
    w_up, w_ao = late["w_up"], wts["w_att_out"]
    w_down, w_o, w_ro = _rows_joined(late["w_down"]), _rows_joined(late["w_o"]), _rows_joined(late["w_rwkv_out"])
    post_rows = [(y_raw, D, 0), (zs, D, 0), (zs, D, 2), (km, D, 0), (gg, D, 0)]
    post_c = [wts["lnx_w"], wts["lnx_b"], wts["r_k"], e, et]
    (rw_out,) = _rows_fwd(_f_rwpost, post_rows, post_c, [(D, BF16)], name="rwpost_fwd", br=br)
    y_rw = _mm(rw_out, w_ro, out_dtype=BF16, name="mm_rw_out")
    mix_rows = [(gate_in, N_GATE, 0), (y_att, D, 0), (y_rw, D, 0)]
    (mix,) = _rows_fwd(_f_mix, mix_rows, [wts["b_gate"]], [(D, BF16)], name="mix_fwd", br=2 * br)
    o = _mm(mix, w_o, out_dtype=BF16, name="mm_o")
    pre2_c = [gt1, wts["norm2_w"], sc2, sh2]
    x1, h2 = _rows_fwd(_f_pre2, [(x, D, 0), (o, D, 0)], pre2_c, [(D, F32), (D, BF16)], name="pre2_fwd", br=2 * br)
    u = _mm(h2, w_up, b_chip=True, name="mm_up")
    act = _conv_fwd(u, wts["conv_w"], wts["conv_b"])
    f = _mm(act, w_down, out_dtype=BF16, name="mm_down")
    fin_rows = [(x1, D, 0), (f, D, 0), (tgt, D, 0)]
    fin_c = [gt2, wts["norm_f_w"]]

    def fin_fwd(*a):
        (l,) = _f_fin(*a)
        return (jnp.broadcast_to(jnp.sum(l, axis=0, keepdims=True), (8, 128)),)

    (loss_acc,) = _rows_fwd(fin_fwd, fin_rows, fin_c, [], name="fin_fwd", br=2 * br, acc_shape=(8, 128))

    gw = {}
    dx1a, df, d_gt2, gw["norm_f_w"] = _rows_bwd(
        _f_fin, fin_rows, fin_c, [[]], wrt_rows=[0, 1], wrt_consts=[0, 1], drow_dtypes=[F32, BF16],
        name="fin_bwd", br=2 * br, unit_cot=True)
    dact = _mm(df, w_down, tb=True, name="mm_dact")
    gw["w_down"] = _rows_split(_mm(act, df, ta=True, out_dtype=BF16, name="mm_dw_down"))
    du, gw["conv_w"], gw["conv_b"] = _conv_bwd(u, wts["conv_w"], wts["conv_b"], dact)
    dh2 = _mm(du, w_up, tb=True, b_chip=True, out_dtype=BF16, name="mm_dh2")
    gw["w_up"] = _mm(h2, du, ta=True, out_chip=True, out_dtype=BF16, name="mm_dw_up")
    dxa, do, d_gt1, gw["norm2_w"], d_sc2, d_sh2 = _rows_bwd(
        _f_pre2, [(x, D, 0), (o, D, 0)], pre2_c, [[(dx1a, D, 0)], [(dh2, D, 0)]], wrt_rows=[0, 1],
        wrt_consts=[0, 1, 2, 3], drow_dtypes=[F32, BF16], name="pre2_bwd", br=2 * br)
    dmix = _mm(do, w_o, tb=True, out_dtype=BF16, name="mm_dmix")
    gw["w_o"] = _rows_split(_mm(mix, do, ta=True, out_dtype=BF16, name="mm_dw_o"))
    dgate, dya, dyr, gw["b_gate"] = _rows_bwd(
        _f_mix, mix_rows, [wts["b_gate"]], [[(dmix, D, 0)]], wrt_rows=[0, 1, 2], wrt_consts=[0],
        drow_dtypes=[BF16] * 3, name="mix_bwd", br=2 * br)
    datt = _mm(dya, w_ao, tb=True, b_chip=True, out_dtype=BF16, name="mm_datt")
    gw["w_att_out"] = _mm(att, dya, ta=True, out_chip=True, out_dtype=BF16, name="mm_dw_att_out")
    drw = _mm(dyr, w_ro, tb=True, out_dtype=BF16, name="mm_drw")
    gw["w_rwkv_out"] = _rows_split(_mm(rw_out, dyr, ta=True, out_dtype=BF16, name="mm_dw_rw_out"))
    dcomb = _rows_bwd(_f_comb, comb_rows, [], [[(datt, ATT_WIDTH, 0)]], wrt_rows=list(range(6)), wrt_consts=[],
                      drow_dtypes=[F32] * 6, name="comb_bwd", br=2 * br)
    datt_in = None
    for g, (_, dil) in enumerate(ATT_PATTERNS):
        datt_in = _att_bwd(att_in, g, dil, dcomb[g], dcomb[3 + g], datt_in)
    dy_raw, dr_p, dv_p, dkm_p, dgg, gw["lnx_w"], gw["lnx_b"], gw["r_k"], *recv_early = _rows_bwd(
        _f_rwpost, post_rows, post_c, [[(drw, D, 0)]], wrt_rows=[0, 1, 2, 3, 4], wrt_consts=[0, 1, 2],
        drow_dtypes=[F32] * 5, name="rwpost_bwd", br=br, comm=_SiblingHalves([gw[n] for n in _DONE_EARLY]))
    st.update(loss=loss_acc[0, 0], gw=gw, dxa=dxa, dgate=dgate, datt_in=datt_in,
              dy_raw=dy_raw, dr_p=dr_p, dv_p=dv_p, dkm_p=dkm_p, dgg=dgg, d_ada_late=(d_gt1, d_sh2, d_sc2, d_gt2),
              recv_early=recv_early)
    return st


def _chip_parts(grads, recv, names, core):
    return [_half_sum(lambda a, b: a + b, [g], [r], False, BF16, core, "reduce_add2_" + n)
            for g, r, n in zip(grads, recv, names)]


def _step_after_scan(st, scan_grads, core):
    x, br, gw, h1, zs = st["x"], st["br"], st["gw"], st["h1"], st["zs"]
    dr_s, dlw, dkm_s, dv_s, daa, dbb = scan_grads
    pre_cots = [[(st["dr_p"], D, 0), (dr_s, D, 0)], [(dlw, D, 0)], [(st["dkm_p"], D, 0), (dkm_s, D, 0)],
                [(st["dv_p"], D, 0), (dv_s, D, 0)], [(daa, D, 0)], [(dbb, D, 0)], [(st["dgg"], D, 0)]]
    dzs, gw["w0"], gw["a0"], gw["k_k"], gw["k_a"], dwl = _rows_bwd(
        _f_rwpre, [(zs, N_RWP, 0)], st["rwpre_c"], pre_cots, wrt_rows=[0], wrt_consts=[0, 1, 2, 3, 4],
        drow_dtypes=[F32], name="rwpre_bwd", br=128)
    gw["w2"], gw["a2"] = _cols_split(dwl[0:64, 0:D]), _cols_split(dwl[64:128, D:2 * D])
    gw["g2"] = _cols_split(dwl[128:288, 2 * D:3 * D])
    dz, dmu = _shift_bwd(st["z"], st["mu"], dzs)
    gw["mu_shift"] = dmu[:, :N_RW]
    datt_in, dgate = st["datt_in"], st["dgate"]
    dw_in = [(_mm(datt_in, h1, ta=True, out_dtype=BF16, name="mm_dw_att"), N_ATT),
             (_mm(dz, h1, ta=True, out_dtype=BF16, name="mm_dw_rw"), N_RW),
             (_mm(dgate, h1, ta=True, out_dtype=BF16, name="mm_dw_gate"), N_GATE)]
    slabs = []
    for j in range(4):
        slabs += [_row_window(dw_in, j * W_IN_SHARD, (j + 1) * W_IN_SHARD), jnp.zeros((W_IN_PAD - W_IN_SHARD, D), BF16)]
    gw["w_in"] = jnp.concatenate(slabs, axis=0).reshape(4, W_IN_PAD, D)
    late = [gw[n] for n in _DONE_LATE]
    parts = _chip_parts(late, _run_comm(_SiblingHalves(late), "reduce_sib_late"), _DONE_LATE, core)
    dh1, slots_late = _mm_sum([(datt_in, st["w_att"]), (dz, st["w_rw"]), (dgate, st["w_gate"])],
                              comm=_ScatterToChips(parts), name="mm_dh1")
    grad_x, gw["norm1_w"], d_sc1, d_sh1 = _rows_bwd(
        _f_pre, [(x, D, 0)], st["pre1_c"], [[(dh1, D, 0)], [(st["dxa"], D, 0)]], wrt_rows=[0], wrt_consts=[0, 1, 2],
        drow_dtypes=[F32], name="pre1_bwd", br=2 * br)
    d_gt1, d_sh2, d_sc2, d_gt2 = st["d_ada_late"]
    return st["loss"], grad_x, (d_sh1, d_sc1, d_gt1, d_sh2, d_sc2, d_gt2), gw, slots_late


_SMALL = ("b_ada", "norm1_w", "b_gate", "mu_shift", "w0", "a0", "k_k", "k_a", "r_k", "lnx_w", "lnx_b", "norm2_w",
          "conv_b", "norm_f_w")
_NAMES = ("w_ada", "b_ada", "norm1_w", "w_in", "b_gate", "mu_shift", "w0", "w2", "a0", "a2", "g2", "k_k", "k_a", "r_k",
          "lnx_w", "lnx_b", "w_att_out", "w_rwkv_out", "w_o", "norm2_w", "w_up", "conv_w", "conv_b", "w_down",
          "norm_f_w")


def kernel(x, c, w_ada, b_ada, norm1_w, w_in, b_gate, mu_shift, w0, w2, a0, a2, g2, k_k, k_a, r_k, lnx_w, lnx_b, w_att_out, w_rwkv_out, w_o, norm2_w, w_up, conv_w, conv_b, w_down, norm_f_w, loss_target, m_w_ada, m_b_ada, m_norm1_w, m_w_in, m_b_gate, m_mu_shift, m_w0, m_w2, m_a0, m_a2, m_g2, m_k_k, m_k_a, m_r_k, m_lnx_w, m_lnx_b, m_w_att_out, m_w_rwkv_out, m_w_o, m_norm2_w, m_w_up, m_conv_w, m_conv_b, m_w_down, m_norm_f_w, v_w_ada, v_b_ada, v_norm1_w, v_w_in, v_b_gate, v_mu_shift, v_w0, v_w2, v_a0, v_a2, v_g2, v_k_k, v_k_a, v_r_k, v_lnx_w, v_lnx_b, v_w_att_out, v_w_rwkv_out, v_w_o, v_norm2_w, v_w_up, v_conv_w, v_conv_b, v_w_down, v_norm_f_w):
    args = dict(locals())
    p, pm, pv = {}, {}, {}
    for name in _NAMES:
        for dst, key in ((p, name), (pm, "m_" + name), (pv, "v_" + name)):
            t = args[key]
            if name == "w_in":
                dst[name] = jnp.swapaxes(t, 1, 2)[0]
            else:
                dst[name] = t.reshape(1, -1) if name in ("r_k", "norm_f_w") else t.reshape(t.shape[-2], t.shape[-1])
    xi, yi, ci = _me()
    chip = 2 * xi + yi
    dev = 4 * xi + 2 * yi + ci
    x2, tgt = x[0], loss_target[0]

    n_cw = 3 * (2 * D_FF // 4)
    vec = jnp.concatenate([c.reshape(-1), p["conv_w"].reshape(-1), jnp.zeros((8 * D - D - n_cw,), F32)]).reshape(8, D)
    g0 = _allgather8(vec, "gather_c").reshape(8, 8 * D)
    c_all = g0[:, :D]
    conv_w_full = jnp.concatenate([g0[2 * j, D:D + n_cw].reshape(3, -1) for j in range(4)], axis=1)
    n_ada = 6 * D // 4
    b_ada_sh = lax.dynamic_slice(p["b_ada"], (0, chip * n_ada), (1, n_ada))
    ada_sh = _ada_fwd(c_all, p["w_ada"], b_ada_sh)
    ga = _allgather8(ada_sh, "gather_ada")
    ada_all = jnp.concatenate([ga[2 * j] for j in range(4)], axis=1)
    ada_row = lax.dynamic_slice(ada_all, (dev, 0), (1, 6 * D))
    ada = [ada_row[:, j * D:(j + 1) * D] for j in range(6)]

    big = [n for n, _ in _BIG]
    shard = {n: p[n].astype(BF16) for n in big}
    shard["w_in"] = jnp.pad(shard["w_in"], ((0, W_IN_PAD - W_IN_SHARD), (0, 0)))
    wts = dict(zip(_NEEDED_FIRST, _run_comm(_GatherWeights([shard[n] for n in _NEEDED_FIRST]), "gather_w")))
    for n in _SMALL:
        wts[n] = p[n]
    wts["conv_w"] = conv_w_full
    core = ci.reshape(1).astype(jnp.int32)

    st = _step_to_scan(x2, tgt, ada, wts)
    y_raw, s0s, inverses, late = _scan_fwd(st["zs"], st["lw"], st["km"], st["aa"], st["bb"],
                                           _GatherWeights([shard[n] for n in _NEEDED_LATER]))
    st = _step_between_scans(st, y_raw, dict(zip(_NEEDED_LATER, late)))
    early = _chip_parts([st["gw"][n] for n in _DONE_EARLY], st["recv_early"], _DONE_EARLY, core)
    scan_grads, slots_early = _scan_bwd(st["zs"], st["lw"], st["km"], st["aa"], st["bb"], s0s, inverses,
                                        st["dy_raw"], _ScatterToChips(early))
    loss_part, grad_x, d_ada, gw, slots_late = _step_after_scan(st, scan_grads, core)

    small = [jnp.concatenate(d_ada, axis=1)] + [gw[n] for n in _SMALL[1:]] + [gw["conv_w"], loss_part.reshape(1, 1)]
    sizes = [t.size for t in small]
    flat = jnp.concatenate([t.reshape(-1) for t in small])
    npad = (-flat.shape[0]) % (8 * D)
    srows = (flat.shape[0] + npad) // D
    flat = jnp.concatenate([flat, jnp.zeros((npad,), F32)]).reshape(srows, D)
    parts = _allgather8(flat, "gather_small")
    tot = _sum_lead(parts, "sum_small").reshape(-1)
    pieces, pos = [], 0
    for sz in sizes:
        pieces.append(tot[pos:pos + sz])
        pos += sz
    grads = {}
    for n, piece in zip(_SMALL, pieces[:len(_SMALL)]):
        grads[n] = piece.reshape(p[n].shape)
    conv_w_grad = pieces[len(_SMALL)].reshape(3, 2 * D_FF)
    grads["conv_w"] = lax.dynamic_slice(conv_w_grad, (0, chip * (n_cw // 3)), (3, n_cw // 3))
    loss = pieces[-1][0]
    d_ada_all = parts[:, :6].reshape(8, 6 * D)
    grads["w_ada"] = _ada_bwd(c_all.T, lax.dynamic_slice(d_ada_all, (0, chip * n_ada), (8, n_ada)))

    order = _DONE_EARLY + _DONE_LATE
    reds = [_half_sum(lambda t: t[0] + t[1] + t[2] + t[3], [], [t], True, F32, core, "reduce_add4_" + n)
            for n, t in zip(order, list(slots_early) + list(slots_late))]
    for n, g in zip(order, _reduce_finish(reds, "reduce_sib2")):
        grads[n] = g

    outs_g, outs_d, outs_m, outs_v = [], [], [], []
    grads["w_in"] = grads["w_in"][:W_IN_SHARD]
    for name in _NAMES:
        g = grads[name]
        d, m, v = _adamw(p[name], g, pm[name], pv[name], "adamw_" + name)
        shape = args[name].shape
        for outs, t in ((outs_g, g), (outs_d, d), (outs_m, m), (outs_v, v)):
            outs.append(jnp.swapaxes(t[None], 1, 2) if name == "w_in" else t.reshape(shape))
    return (loss, grad_x.reshape(x.shape), *outs_g, *outs_d, *outs_m, *outs_v)
```

```python
import functools
import math

import jax
import jax.numpy as jnp
from jax import lax
from jax.experimental import pallas as pl
from jax.experimental.pallas import tpu as pltpu

F32 = jnp.float32
BF16 = jnp.bfloat16
HI = lax.Precision.HIGHEST
MESH = pl.DeviceIdType.MESH

D = 1024
ATT_PATTERNS = ((128, 1), (512, 4), (2048, 16))
ATT_BLOCK = 128
ATT_WIDTH = 512
N_ATT = 3 * 3 * ATT_WIDTH
N_RW = 3 * D + 64 + 64 + 160
N_RWP = 3456
N_LORA = N_RWP - 3 * D
N_GATE = 2 * D
D_FF = 2816
RMS_EPS = 1e-6
GN_EPS = 64e-5
SCAN_CHUNK = 64
SCAN_PAIRS = 8
NEG = -1e30
VMEM_LIMIT = 48 * 1024 * 1024
HALO = 16

ADAM_LR, ADAM_B1, ADAM_B2, ADAM_EPS, ADAM_WD, ADAM_STEP = 0.001, 0.9, 0.999, 1e-08, 0.01, 10


def _pcall(body, **kw):
    return pl.pallas_call(body, **kw)


def _cparams(sem):
    return pltpu.CompilerParams(dimension_semantics=sem, vmem_limit_bytes=VMEM_LIMIT)


def _div(n, pref, mult):
    best = None
    d = mult
    while d <= min(n, pref):
        if n % d == 0:
            best = d
        d += mult
    return best if best else n


def _dg(a, b, ca, cb):
    return lax.dot_general(a.astype(BF16), b.astype(BF16), (((ca,), (cb,)), ((), ())), preferred_element_type=F32)


@jax.custom_vjp
def _nn(a, b):
    return _dg(a, b, 1, 0)


@jax.custom_vjp
def _nt(a, b):
    return _dg(a, b, 1, 1)


@jax.custom_vjp
def _tn(a, b):
    return _dg(a, b, 0, 0)


_nn.defvjp(lambda a, b: (_nn(a, b), (a, b)), lambda res, g: (_nt(g, res[1]), _tn(res[0], g)))
_nt.defvjp(lambda a, b: (_nt(a, b), (a, b)), lambda res, g: (_nn(g, res[1]), _tn(g, res[0])))
_tn.defvjp(lambda a, b: (_tn(a, b), (a, b)), lambda res, g: (_nt(res[1], g), _nn(res[0], g)))


def _bdg(a, b, ca, cb):
    return lax.dot_general(a.astype(BF16), b.astype(BF16), (((ca,), (cb,)), ((0,), (0,))), preferred_element_type=F32)


@jax.custom_vjp
def _bnn(a, b):
    return _bdg(a, b, 2, 1)


@jax.custom_vjp
def _bnt(a, b):
    return _bdg(a, b, 2, 2)


@jax.custom_vjp
def _btn(a, b):
    return _bdg(a, b, 1, 1)


_bnn.defvjp(lambda a, b: (_bnn(a, b), (a, b)), lambda res, g: (_bnt(g, res[1]), _btn(res[0], g)))
_bnt.defvjp(lambda a, b: (_bnt(a, b), (a, b)), lambda res, g: (_bnn(g, res[1]), _btn(g, res[0])))
_btn.defvjp(lambda a, b: (_btn(a, b), (a, b)), lambda res, g: (_bnt(res[1], g), _bnn(res[0], g)))


def _hsum_impl(x, e, et):
    eb, etb = e.astype(BF16), et.astype(BF16)
    s = jnp.dot(x.astype(BF16), eb, preferred_element_type=F32)
    return jnp.dot(s.astype(BF16), etb, preferred_element_type=F32)


@jax.custom_vjp
def _hsum(x, e, et):
    return _hsum_impl(x, e, et)


_hsum.defvjp(lambda x, e, et: (_hsum_impl(x, e, et), (e, et)),
             lambda res, g: (_hsum_impl(g, res[0], res[1]), jnp.zeros_like(res[0]), jnp.zeros_like(res[1])))


def _mm(a, b, *, ta=False, tb=False, out_dtype=F32, add=None, b_chip=False, out_chip=False, comm=None, name):
    riding = _NOTHING if comm is None else comm
    nc = riding.n
    if ta:
        kdim, m = a.shape
    else:
        m, kdim = a.shape
    if b_chip:
        n = b.shape[1] if tb else 4 * b.shape[2]
    else:
        n = b.shape[0] if tb else b.shape[1]
    tm, tn, tk = _div(m, 1536, 128), _div(n, 1536, 128), _div(kdim, 2048 if ta else 1408, 128)
    if b_chip and tb:
        tk = kdim // 4
    if (b_chip and not tb) or out_chip:
        tn = n // 4
    nk = kdim // tk
    ca, cb = (0 if ta else 1), (1 if tb else 0)

    nin = 2 if add is None else 3
    gi, gj = m // tm, n // tn

    def body(*refs):
        a_ref, b_ref = refs[0], refs[1]
        add_ref = None if add is None else refs[2]
        o_ref = refs[nin + nc]
        step = (pl.program_id(0) * gj + pl.program_id(1)) * nk + pl.program_id(2)
        before, after = _comm_phases(riding, refs[nin:nin + nc] + refs[nin + nc + 1:nin + 2 * nc + 1]
                                     + refs[nin + 2 * nc + 1 + (nk > 1):], gi * gj * nk, step)
        before()
        part = lax.dot_general(a_ref[...], b_ref[...], (((ca,), (cb,)), ((), ())), preferred_element_type=F32)

        def finish(r):
            if add_ref is not None:
                r = r + add_ref[...]
            o_ref[...] = r.astype(o_ref.dtype)

        if nk == 1:
            finish(part)
            after()
            return
        acc = refs[nin + 2 * nc + 1]
        k = pl.program_id(2)

        @pl.when(k == 0)
        def _():
            acc[...] = part

        @pl.when(k > 0)
        def _():
            acc[...] += part

        @pl.when(k == nk - 1)
        def _():
            finish(acc[...])

        after()

    a_spec = pl.BlockSpec((tk, tm), lambda i, j, k: (k, i)) if ta else pl.BlockSpec((tm, tk), lambda i, j, k: (i, k))
    if b_chip:
        b_spec = (pl.BlockSpec((None, tn, tk), lambda i, j, k: (k, j, 0)) if tb
                  else pl.BlockSpec((None, tk, tn), lambda i, j, k: (j, k, 0)))
    else:
        b_spec = pl.BlockSpec((tn, tk), lambda i, j, k: (j, k)) if tb else pl.BlockSpec((tk, tn), lambda i, j, k: (k, j))
    in_specs = [a_spec, b_spec]
    args = [a, b]
    if add is not None:
        in_specs.append(pl.BlockSpec((tm, tn), lambda i, j, k: (i, j)))
        args.append(add)
    if out_chip:
        out_spec = pl.BlockSpec((None, tm, tn), lambda i, j, k: (j, i, 0))
        out_shape = jax.ShapeDtypeStruct((4, m, tn), out_dtype)
    else:
        out_spec = pl.BlockSpec((tm, tn), lambda i, j, k: (i, j))
        out_shape = jax.ShapeDtypeStruct((m, n), out_dtype)
    res = _pcall(
        body, name=name, grid=(gi, gj, nk), in_specs=in_specs + [_HBM] * nc, out_specs=[out_spec] + [_HBM] * nc,
        out_shape=[out_shape] + riding.out_shape,
        scratch_shapes=([] if nk == 1 else [pltpu.VMEM((tm, tn), F32)]) + riding.sems,
        compiler_params=_cparams(("arbitrary",) * 3 if nc else ("parallel", "parallel", "arbitrary")),
    )(*args, *riding.ins)
    return res[0] if comm is None else (res[0], res[1:])


def _mm_sum(pairs, *, comm, name):
    m, n = pairs[0][0].shape[0], pairs[0][1].shape[1]
    tm, tn = _div(m, 1024, 128), _div(n, 1024, 128)
    tks = [_div(a.shape[1], 1408, 128) for a, _ in pairs]
    nks = [a.shape[1] // tk for (a, _), tk in zip(pairs, tks)]
    offs = [sum(nks[:p]) for p in range(len(pairs))]
    total, npair, nc = sum(nks), len(pairs), comm.n
    gi, gj = m // tm, n // tn

    def body(*refs):
        o_ref, acc = refs[2 * npair + nc], refs[2 * npair + 2 * nc + 1]
        k = pl.program_id(2)
        step = (pl.program_id(0) * gj + pl.program_id(1)) * total + k
        before, after = _comm_phases(comm, refs[2 * npair:2 * npair + nc]
                                     + refs[2 * npair + nc + 1:2 * npair + 2 * nc + 1]
                                     + refs[2 * npair + 2 * nc + 2:], gi * gj * total, step)
        before()
        for p in range(npair):
            def partial_product(p=p):
                part = jnp.dot(refs[2 * p][...], refs[2 * p + 1][...], preferred_element_type=F32)
                if p == 0:
                    @pl.when(k == 0)
                    def _():
                        acc[...] = part

                    @pl.when(k > 0)
                    def _():
                        acc[...] += part
                else:
                    acc[...] += part

            pl.when(jnp.logical_and(k >= offs[p], k < offs[p] + nks[p]))(partial_product)

        @pl.when(k == total - 1)
        def _():
            o_ref[...] = acc[...].astype(o_ref.dtype)

        after()

    def specs(tk, off, nk):
        def kb(k):
            return jnp.clip(k - off, 0, nk - 1)
        return [pl.BlockSpec((tm, tk), lambda i, j, k: (i, kb(k))), pl.BlockSpec((tk, tn), lambda i, j, k: (kb(k), j))]

    in_specs, args = [], []
    for (a, b), tk, off, nk in zip(pairs, tks, offs, nks):
        in_specs += specs(tk, off, nk)
        args += [a, b]
    res = _pcall(
        body, name=name, grid=(gi, gj, total), in_specs=in_specs + [_HBM] * nc,
        out_specs=[pl.BlockSpec((tm, tn), lambda i, j, k: (i, j))] + [_HBM] * nc,
        out_shape=[jax.ShapeDtypeStruct((m, n), BF16)] + comm.out_shape,
        scratch_shapes=[pltpu.VMEM((tm, tn), F32)] + comm.sems,
        compiler_params=_cparams(("arbitrary",) * 3),
    )(*args, *comm.ins)
    return res[0], res[1:]


def _row_spec(br, w, cb):
    return pl.BlockSpec((br, w), lambda i: (i, cb))


def _const_spec(shape):
    return pl.BlockSpec(shape, lambda i: (0,) * len(shape))


def _rows_fwd(fn, rows, consts, outs, *, name, br, acc_shape=None, halo=None):
    s = rows[0][0].shape[0]
    nr, nc = len(rows), len(consts)
    kept = [k for k, o in enumerate(outs) if o is not None]

    def body(*refs):
        xs = [r[...].astype(F32) for r in refs[:nr]]
        cs = [c[...] for c in refs[nr:nr + nc]]
        if halo is not None:
            cs.append(jnp.where(pl.program_id(0) == 0, 0.0, refs[nr + nc][...].astype(F32)))
        res = fn(*xs, *cs)
        orefs = refs[nr + nc + (halo is not None):]
        for j, k in enumerate(kept):
            orefs[j][...] = res[k].astype(orefs[j].dtype)
        if acc_shape is not None:
            acc_ref = orefs[len(kept)]

            @pl.when(pl.program_id(0) == 0)
            def _():
                acc_ref[...] = jnp.zeros_like(acc_ref)

            acc_ref[...] += res[len(outs)]

    in_specs = [_row_spec(br, w, cb) for (_, w, cb) in rows] + [_const_spec(c.shape) for c in consts]
    args = [r[0] for r in rows] + list(consts)
    if halo is not None:
        harr, hw, hcb = rows[halo]
        in_specs.append(pl.BlockSpec((HALO, hw), lambda i: (jnp.maximum(i * (br // HALO) - 1, 0), hcb)))
        args.append(harr)
    out_specs = [_row_spec(br, outs[k][0], 0) for k in kept]
    out_shape = [jax.ShapeDtypeStruct((s, outs[k][0]), outs[k][1]) for k in kept]
    if acc_shape is not None:
        out_specs.append(_const_spec(acc_shape))
        out_shape.append(jax.ShapeDtypeStruct(acc_shape, F32))
    return _pcall(
        body, name=name, grid=(pl.cdiv(s, br),), in_specs=in_specs, out_specs=out_specs, out_shape=out_shape,
        compiler_params=_cparams(("arbitrary",)),
    )(*args)


def _rows_bwd(fn, rows, consts, cots, *, wrt_rows, wrt_consts, drow_dtypes, name, br, unit_cot=False, comm=None):
    comm = _NOTHING if comm is None else comm
    ncomm = comm.n
    nout = len(wrt_rows) + len(wrt_consts)
    s = rows[0][0].shape[0]
    nr, nc = len(rows), len(consts)
    flat_cots = [c for lst in cots for c in lst]
    ncot = len(flat_cots)

    def body(*refs):
        xs = [r[...].astype(F32) for r in refs[:nr]]
        cs = [c[...] for c in refs[nr:nr + nc]]
        cvals = [c[...].astype(F32) for c in refs[nr + nc:nr + nc + ncot]]
        orefs = refs[nr + nc + ncot + ncomm:]
        before, after = _comm_phases(comm, refs[nr + nc + ncot:nr + nc + ncot + ncomm] + orefs[nout:], s // br)
        before()

        def g(*d):
            xs2, cs2 = list(xs), list(cs)
            for j, k in enumerate(wrt_rows):
                xs2[k] = d[j]
            for j, k in enumerate(wrt_consts):
                cs2[k] = d[len(wrt_rows) + j]
            return tuple(fn(*xs2, *cs2))

        prim = [xs[k] for k in wrt_rows] + [cs[k] for k in wrt_consts]
        outs, vjp = jax.vjp(g, *prim)
        ct = []
        pos = 0
        for o, lst in zip(outs, cots):
            if unit_cot:
                ct.append(jnp.ones_like(o))
                continue
            acc = jnp.zeros_like(o)
            for _ in lst:
                acc = acc + cvals[pos]
                pos += 1
            ct.append(acc)
        grads = vjp(tuple(ct))
        for j in range(len(wrt_rows)):
            orefs[j][...] = grads[j].astype(orefs[j].dtype)

        @pl.when(pl.program_id(0) == 0)
        def _():
            for j in range(len(wrt_consts)):
                oref = orefs[len(wrt_rows) + j]
                oref[...] = jnp.zeros_like(oref)

        for j in range(len(wrt_consts)):
            orefs[len(wrt_rows) + j][...] += grads[len(wrt_rows) + j]
        after()

    in_specs = ([_row_spec(br, w, cb) for (_, w, cb) in rows] + [_const_spec(c.shape) for c in consts]
                + [_row_spec(br, w, cb) for (_, w, cb) in flat_cots] + [_HBM] * ncomm)
    out_specs = ([_row_spec(br, rows[k][1], 0) for k in wrt_rows] + [_const_spec(consts[k].shape) for k in wrt_consts]
                 + [_HBM] * ncomm)
    out_shape = ([jax.ShapeDtypeStruct((s, rows[k][1]), dt) for k, dt in zip(wrt_rows, drow_dtypes)]
                 + [jax.ShapeDtypeStruct(consts[k].shape, F32) for k in wrt_consts] + comm.out_shape)
    return _pcall(
        body, name=name, grid=(s // br,), in_specs=in_specs, out_specs=out_specs, out_shape=out_shape,
        scratch_shapes=comm.sems, compiler_params=_cparams(("arbitrary",)),
    )(*[r[0] for r in rows], *consts, *[c[0] for c in flat_cots], *comm.ins)


def _rms(x, w):
    return x * lax.rsqrt(jnp.mean(x * x, axis=-1, keepdims=True) + RMS_EPS) * w


def _f_pre(x, nw, sc, sh):
    return _rms(x, nw) * (1.0 + sc) + sh, x


def _f_pre2(x, o, gt, nw, sc, sh):
    x1 = x + gt * o
    return x1, _rms(x1, nw) * (1.0 + sc) + sh


def _f_fin(x1, f, tgt, gt, nfw):
    y = _rms(x1 + gt * f, nfw)
    return (0.5 * jnp.mean(jnp.square(y - tgt), axis=-1, keepdims=True),)


def _f_comb(o1, o2, o3, l1, l2, l3):
    m = lax.stop_gradient(jnp.maximum(jnp.maximum(l1, l2), l3))
    e1, e2, e3 = jnp.exp(l1 - m), jnp.exp(l2 - m), jnp.exp(l3 - m)
    return ((e1 * o1 + e2 * o2 + e3 * o3) / (e1 + e2 + e3),)


def _f_rwpre(zs, w0, a0, k_k, k_a, wl, e, et):
    r, k, v, zl = zs[:, 0:D], zs[:, D:2 * D], zs[:, 2 * D:3 * D], zs[:, 3 * D:N_RWP]
    lane = lax.broadcasted_iota(jnp.int32, zl.shape, 1)
    t = jnp.where(lane < 64, jnp.tanh(zl), jnp.where(lane < 128, zl, jnp.where(lane < 288, jax.nn.sigmoid(zl), 0.0)))
    lo = _nn(t[:, 0:128], wl[0:128, 0:2 * D])
    g = _nn(t[:, 128:N_LORA], wl[128:N_LORA, 2 * D:3 * D])
    lw = -math.exp(-0.5) * jax.nn.sigmoid(w0 + lo[:, 0:D])
    a = jax.nn.sigmoid(a0 + lo[:, D:2 * D])
    k_mod = k * (1.0 + (a - 1.0) * k_a)
    kk = k * k_k
    kk = kk / jnp.maximum(jnp.sqrt(_hsum(kk * kk, e, et)), 1e-12)
    return r, lw, k_mod, v, -kk, kk * a, g


def _f_rwpost(y, r, v, k_mod, g, lnx_w, lnx_b, r_k, e, et):
    mean = _hsum(y, e, et) * (1.0 / 64)
    yc = y - mean
    var = _hsum(yc * yc, e, et) * (1.0 / 64)
    yn = yc * lax.rsqrt(var + GN_EPS) * lnx_w + lnx_b
    bonus = _hsum(r * k_mod * r_k, e, et) * v
    return ((yn + bonus) * g,)


def _f_mix(gi, ya, yr, bg):
    gate = jax.nn.sigmoid(gi + bg)
    return (gate[:, 0:D] * ya + gate[:, D:2 * D] * yr,)


def _f_adamw(w, g, m, v):
    m = ADAM_B1 * m + (1.0 - ADAM_B1) * g
    v = ADAM_B2 * v + (1.0 - ADAM_B2) * jnp.square(g)
    m_hat = m / (1.0 - ADAM_B1 ** ADAM_STEP)
    v_hat = v / (1.0 - ADAM_B2 ** ADAM_STEP)
    return -ADAM_LR * (m_hat / (jnp.sqrt(v_hat) + ADAM_EPS) + ADAM_WD * w), m, v


def _down(x, k):
    row = lax.broadcasted_iota(jnp.int32, x.shape, 0)
    return jnp.where(row < k, 0.0, pltpu.roll(x, k, 0))


def _up(x, k):
    n = x.shape[0]
    row = lax.broadcasted_iota(jnp.int32, x.shape, 0)
    return jnp.where(row >= n - k, 0.0, pltpu.roll(x, n - k, 0))


def _col_spec(s, w, off=0):
    return pl.BlockSpec((s, w), lambda j: (0, j + off))


def _shift_bwd(z, mu, dzs):
    s, n = z.shape

    def body(z_ref, mu_ref, d_ref, dz_ref, dmu_ref):
        zz, d, m = z_ref[...].astype(F32), d_ref[...], mu_ref[...]
        dm = d * m
        dz_ref[...] = (d - dm + _up(dm, 1)).astype(dz_ref.dtype)
        dmu_ref[...] = jnp.sum(d * (_down(zz, 1) - zz), axis=0, keepdims=True)

    return _pcall(
        body, name="shift_bwd", grid=(n // 128,), in_specs=[_col_spec(s, 128), _col_spec(1, 128), _col_spec(s, 128)],
        out_specs=[_col_spec(s, 128), _col_spec(1, 128)],
        out_shape=[jax.ShapeDtypeStruct((s, n), BF16), jax.ShapeDtypeStruct((1, n), F32)],
        compiler_params=_cparams(("parallel",)),
    )(z, mu, dzs)


def _conv3(x, w_ref, b_ref):
    return b_ref[...] + w_ref[0:1, :] * _down(x, 2) + w_ref[1:2, :] * _down(x, 1) + w_ref[2:3, :] * x


def _conv_fwd(u, cw, cb):
    s = u.shape[0]
    nb = D_FF // 128

    def body(ug_ref, uv_ref, wg_ref, wv_ref, bg_ref, bv_ref, o_ref):
        gate = _conv3(ug_ref[...], wg_ref, bg_ref)
        val = _conv3(uv_ref[...], wv_ref, bv_ref)
        o_ref[...] = (gate * jax.nn.sigmoid(gate) * val).astype(o_ref.dtype)

    return _pcall(
        body, name="conv_fwd", grid=(nb,),
        in_specs=[_col_spec(s, 128), _col_spec(s, 128, nb), _col_spec(3, 128), _col_spec(3, 128, nb),
                  _col_spec(1, 128), _col_spec(1, 128, nb)],
        out_specs=_col_spec(s, 128), out_shape=jax.ShapeDtypeStruct((s, D_FF), BF16),
        compiler_params=_cparams(("parallel",)),
    )(u, u, cw, cw, cb, cb)


def _conv_bwd(u, cw, cb, dact):
    s = u.shape[0]
    nb = D_FF // 128

    def half(x, d, w_ref, du_ref, dw_ref, db_ref):
        x1, x2 = _down(x, 1), _down(x, 2)
        du_ref[...] = (w_ref[2:3, :] * d + w_ref[1:2, :] * _up(d, 1) + w_ref[0:1, :] * _up(d, 2)).astype(du_ref.dtype)
        dw_ref[0:1, :] = jnp.sum(d * x2, axis=0, keepdims=True)
        dw_ref[1:2, :] = jnp.sum(d * x1, axis=0, keepdims=True)
        dw_ref[2:3, :] = jnp.sum(d * x, axis=0, keepdims=True)
        db_ref[...] = jnp.sum(d, axis=0, keepdims=True)

    def body(ug_ref, uv_ref, wg_ref, wv_ref, bg_ref, bv_ref, da_ref,
             dug_ref, duv_ref, dwg_ref, dwv_ref, dbg_ref, dbv_ref):
        ug, uv, da = ug_ref[...], uv_ref[...], da_ref[...]
        gate = _conv3(ug, wg_ref, bg_ref)
        val = _conv3(uv, wv_ref, bv_ref)
        sg = jax.nn.sigmoid(gate)
        dgate = da * val * sg * (1.0 + gate * (1.0 - sg))
        dval = da * gate * sg
        half(ug, dgate, wg_ref, dug_ref, dwg_ref, dbg_ref)
        half(uv, dval, wv_ref, duv_ref, dwv_ref, dbv_ref)

    dug, duv, dwg, dwv, dbg, dbv = _pcall(
        body, name="conv_bwd", grid=(nb,),
        in_specs=[_col_spec(s, 128), _col_spec(s, 128, nb), _col_spec(3, 128), _col_spec(3, 128, nb),
                  _col_spec(1, 128), _col_spec(1, 128, nb), _col_spec(s, 128)],
        out_specs=[_col_spec(s, 128), _col_spec(s, 128), _col_spec(3, 128), _col_spec(3, 128),
                   _col_spec(1, 128), _col_spec(1, 128)],
        out_shape=[jax.ShapeDtypeStruct((s, D_FF), BF16), jax.ShapeDtypeStruct((s, D_FF), BF16),
                   jax.ShapeDtypeStruct((3, D_FF), F32), jax.ShapeDtypeStruct((3, D_FF), F32),
                   jax.ShapeDtypeStruct((1, D_FF), F32), jax.ShapeDtypeStruct((1, D_FF), F32)],
        compiler_params=_cparams(("parallel",)),
    )(u, u, cw, cw, cb, cb, dact)
    return (jnp.concatenate([dug, duv], axis=1), jnp.concatenate([dwg, dwv], axis=1),
            jnp.concatenate([dbg, dbv], axis=1))


ATT_BATCH = 4


def _att_batch(q, kp, kc, vp, vc, first):
    ma = lax.broadcasted_iota(jnp.int32, (1, ATT_BLOCK, 128), 2) < 64

    def diag(x):
        return jnp.concatenate([jnp.where(ma, x, 0.0), jnp.where(ma, 0.0, x)], axis=1)

    qi = lax.broadcasted_iota(jnp.int32, (1, ATT_BLOCK, 2 * ATT_BLOCK), 1)
    kj = lax.broadcasted_iota(jnp.int32, (1, ATT_BLOCK, 2 * ATT_BLOCK), 2) & (ATT_BLOCK - 1)
    okp = kj >= qi + jnp.where(first, 2 * ATT_BLOCK, 0)
    okc = kj <= qi
    sp = jnp.where(okp, _bnt(q, diag(kp)) * 0.125, NEG)
    sc = jnp.where(okc, _bnt(q, diag(kc)) * 0.125, NEG)

    def per_head(fn, x):
        return fn(x[..., :ATT_BLOCK]), fn(x[..., ATT_BLOCK:])

    def spread(ab):
        return jnp.concatenate([jnp.broadcast_to(t, t.shape[:2] + (ATT_BLOCK,)) for t in ab], axis=-1)

    row_max = functools.partial(jnp.max, axis=-1, keepdims=True)
    row_sum = functools.partial(jnp.sum, axis=-1, keepdims=True)
    m = [lax.stop_gradient(jnp.maximum(a, b)) for a, b in zip(per_head(row_max, sp), per_head(row_max, sc))]
    pp, pc = jnp.exp(sp - spread(m)), jnp.exp(sc - spread(m))
    den = [a + b for a, b in zip(per_head(row_sum, pp), per_head(row_sum, pc))]
    num = _bnn(pp, diag(vp)) + _bnn(pc, diag(vc))
    out = num / jnp.where(ma, den[0], den[1])
    lse = jnp.where(ma, m[0] + jnp.log(den[0]), m[1] + jnp.log(den[1]))
    return out, jnp.broadcast_to(lse, out.shape)


def _att_pairs_per_step(dil):
    return 4 if dil == 1 else 1


def _att_residues(dil):
    return min(dil, ATT_BATCH // _att_pairs_per_step(dil))


def _att_specs(g, dil):
    rows, pp = ATT_BLOCK * dil, _att_pairs_per_step(dil)

    def cur(slot):
        return pl.BlockSpec((rows, 128 * pp), lambda n, p: (n, (g * 3 + slot) * (4 // pp) + p))

    def prev(slot):
        return pl.BlockSpec((rows, 128 * pp), lambda n, p: (jnp.maximum(n - 1, 0), (g * 3 + slot) * (4 // pp) + p))

    return [cur(0), prev(1), cur(1), prev(2), cur(2)]


def _att_out_spec(dil):
    return pl.BlockSpec((ATT_BLOCK * dil, 128 * _att_pairs_per_step(dil)), lambda n, p: (n, p))


def _att_grid(s, dil):
    return (s // (ATT_BLOCK * dil), 4 // _att_pairs_per_step(dil))


def _att_windows(i, dil):
    res = _att_residues(dil)

    def rows(r):
        return pl.ds(i * res + r, ATT_BLOCK, stride=dil) if dil > 1 else pl.ds(0, ATT_BLOCK)

    return [(rows(r), pl.ds(128 * j, 128)) for j in range(_att_pairs_per_step(dil)) for r in range(res)]


def _att_fwd(att_in, g, dil):
    s = att_in.shape[0]

    def body(q_ref, kp_ref, kc_ref, vp_ref, vc_ref, o_ref, l_ref):
        first = pl.program_id(0) == 0

        def one(i, carry):
            win = _att_windows(i, dil)
            vals = [jnp.stack([ref[w] for w in win]) for ref in (q_ref, kp_ref, kc_ref, vp_ref, vc_ref)]
            o, l = _att_batch(*vals, first)
            for j, w in enumerate(win):
                o_ref[w] = o[j]
                l_ref[w] = l[j]
            return carry

        lax.fori_loop(0, dil // _att_residues(dil), one, 0)

    return _pcall(
        body, name=f"att_fwd{g}", grid=_att_grid(s, dil), in_specs=_att_specs(g, dil),
        out_specs=[_att_out_spec(dil)] * 2, out_shape=[jax.ShapeDtypeStruct((s, ATT_WIDTH), F32)] * 2,
        compiler_params=_cparams(("parallel", "parallel")),
    )(att_in, att_in, att_in, att_in, att_in)


def _att_bwd(att_in, g, dil, do, dl, acc):
    s = att_in.shape[0]

    def body(q_ref, kp_ref, kc_ref, vp_ref, vc_ref, do_ref, dl_ref, dq_ref, dkp_ref, dkc_ref, dvp_ref, dvc_ref):
        first = pl.program_id(0) == 0

        def one(i, carry):
            win = _att_windows(i, dil)
            vals = [jnp.stack([ref[w] for w in win]) for ref in (q_ref, kp_ref, kc_ref, vp_ref, vc_ref)]
            _, vjp = jax.vjp(lambda *a: _att_batch(*a, first), *vals)
            grads = vjp((jnp.stack([do_ref[w] for w in win]), jnp.stack([dl_ref[w] for w in win])))
            for ref, gr in zip((dq_ref, dkp_ref, dkc_ref, dvp_ref, dvc_ref), grads):
                for j, w in enumerate(win):
                    ref[w] = gr[j]
            return carry

        lax.fori_loop(0, dil // _att_residues(dil), one, 0)

    dq, dkp, dkc, dvp, dvc = _pcall(
        body, name=f"att_bwd{g}", grid=_att_grid(s, dil), in_specs=_att_specs(g, dil) + [_att_out_spec(dil)] * 2,
        out_specs=[_att_out_spec(dil)] * 5, out_shape=[jax.ShapeDtypeStruct((s, ATT_WIDTH), F32)] * 5,
        compiler_params=_cparams(("parallel", "parallel")),
    )(att_in, att_in, att_in, att_in, att_in, do, dl)

    unit, rb = ATT_BLOCK * dil, 1024
    steps = s // rb
    within = unit < rb

    def shifted(cur_ref, next_ref, has_next):
        nxt = jnp.where(has_next, next_ref[...], 0.0)
        return jnp.concatenate([cur_ref[unit:, :], nxt], axis=0) if within else nxt

    def cbody(dq_ref, dkc_ref, dkp_ref, dkn_ref, dvc_ref, dvp_ref, dvn_ref, *rest):
        o_ref = rest[-1]
        has_next = pl.program_id(0) + (1 if within else unit // rb) < steps
        o_ref[:, 0:ATT_WIDTH] = dq_ref[...].astype(BF16)
        o_ref[:, ATT_WIDTH:2 * ATT_WIDTH] = (dkc_ref[...] + shifted(dkp_ref, dkn_ref, has_next)).astype(BF16)
        o_ref[:, 2 * ATT_WIDTH:3 * ATT_WIDTH] = (dvc_ref[...] + shifted(dvp_ref, dvn_ref, has_next)).astype(BF16)

    cur = pl.BlockSpec((rb, ATT_WIDTH), lambda i: (i, 0))
    if within:
        nxt = pl.BlockSpec((unit, ATT_WIDTH), lambda i: (jnp.minimum((i + 1) * (rb // unit), s // unit - 1), 0))
    else:
        nxt = pl.BlockSpec((rb, ATT_WIDTH), lambda i: (jnp.minimum(i + unit // rb, steps - 1), 0))
    carried = [] if acc is None else [acc]
    return _pcall(
        cbody, name=f"att_bwd_sum{g}", grid=(steps,),
        in_specs=[cur, cur, cur, nxt, cur, cur, nxt] + [pl.BlockSpec(memory_space=pl.ANY)] * len(carried),
        out_specs=pl.BlockSpec((rb, 3 * ATT_WIDTH), lambda i: (i, g)),
        out_shape=jax.ShapeDtypeStruct((s, N_ATT), BF16), input_output_aliases={7: 0} if carried else {},
        compiler_params=_cparams(("parallel",)),
    )(dq, dkc, dkp, dkp, dvc, dvp, dvp, *carried)


def _cumsum_rows_impl(x):
    row = lax.broadcasted_iota(jnp.int32, x.shape, 0)
    shift = 1
    while shift < x.shape[0]:
        x = x + jnp.where(row >= shift, pltpu.roll(x, shift, 0), 0.0)
        shift *= 2
    return x


@jax.custom_vjp
def _cumsum_rows(x):
    return _cumsum_rows_impl(x)


_cumsum_rows.defvjp(lambda x: (_cumsum_rows_impl(x), None),
                    lambda _, g: (jnp.sum(g, axis=0, keepdims=True) - _cumsum_rows_impl(g) + g,))


def _unit_lower_inverse_impl(n):
    eye = (lax.broadcasted_iota(jnp.int32, (1,) + n.shape[1:], 1)
           == lax.broadcasted_iota(jnp.int32, (1,) + n.shape[1:], 2))
    t = jnp.where(eye, 1.0, 0.0) + n
    pw = n
    for _ in range(5):
        pw = _bnn(pw, pw)
        t = t + _bnn(t, pw)
    return t


@jax.custom_vjp
def _unit_lower_inverse(n):
    return _unit_lower_inverse_impl(n)


def _unit_lower_inverse_fwd(n):
    t = _unit_lower_inverse_impl(n)
    return t, t


_unit_lower_inverse.defvjp(_unit_lower_inverse_fwd, lambda t, g: (_bnt(_btn(t, g), t),))


@jax.custom_vjp
def _known_inverse(n, t):
    return t


_known_inverse.defvjp(lambda n, t: (t, t), lambda t, g: (_bnt(_btn(t, g), t), jnp.zeros_like(t)))


def _scan_chunk(r, lw, k, v, a, b, s0, inverse):
    c = SCAN_CHUNK
    p = s0.shape[0]
    cum = _cumsum_rows(lw)
    tot = jnp.sum(lw, axis=0, keepdims=True)
    ma = (lax.broadcasted_iota(jnp.int32, (c, 128 * p), 1) & 127) < 64

    def pairs(x):
        return jnp.concatenate([x[None, :, 128 * j:128 * (j + 1)] for j in range(p)], axis=0)

    def stack(x):
        return jnp.concatenate([pairs(jnp.where(ma, x, 0.0)), pairs(jnp.where(ma, 0.0, x))], axis=1)

    einv, eend = jnp.exp(-cum), jnp.exp(tot - cum)
    ra, aa = stack(r * jnp.exp(cum)), stack(a * jnp.exp(cum - lw))
    bi, ki, be, ke, vs = stack(b * einv), stack(k * einv), stack(b * eend), stack(k * eend), stack(v)
    r2 = lax.broadcasted_iota(jnp.int32, (1, 2 * c, 2 * c), 1)
    c2 = lax.broadcasted_iota(jnp.int32, (1, 2 * c, 2 * c), 2)
    same = (r2 >= c) == (c2 >= c)
    strict = jnp.logical_and(same, c2 < r2)
    incl = jnp.logical_and(same, c2 <= r2)
    s0 = jnp.where(same, s0, 0.0)
    prod = _bnt(jnp.concatenate([aa, ra], axis=1), jnp.concatenate([bi, ki], axis=1))
    a_ab = jnp.where(strict, prod[:, :2 * c, :2 * c], 0.0)
    a_ak = jnp.where(strict, prod[:, :2 * c, 2 * c:], 0.0)
    a_rb = jnp.where(incl, prod[:, 2 * c:, :2 * c], 0.0)
    a_rk = jnp.where(incl, prod[:, 2 * c:, 2 * c:], 0.0)
    t = inverse(a_ab)
    u = _bnn(t, _bnt(aa, s0) + _bnn(a_ak, vs))
    uv = jnp.concatenate([u, vs], axis=1)
    ys = _bnt(ra, s0) + _bnn(jnp.concatenate([a_rb, a_rk], axis=2), uv)
    s1 = s0 * pairs(jnp.exp(tot)) + _btn(uv, jnp.concatenate([be, ke], axis=1))
    y3 = ys[:, :c] + ys[:, c:]
    return (jnp.concatenate([y3[j] for j in range(p)], axis=1), s1), t


def _scan_specs(rev, n):
    def at(i):
        return n - 1 - i if rev else i

    def cm(cb):
        return pl.BlockSpec((SCAN_CHUNK, D), lambda i: (at(i), cb))

    return cm, pl.BlockSpec((1, SCAN_PAIRS, 128, 128), lambda i: (at(i), 0, 0, 0))


def _comm_phases(comm, refs, n, step=None):
    k = comm.n
    srcs, outs, sems = refs[:k], refs[k:2 * k], refs[2 * k:]
    i = pl.program_id(0) if step is None else step

    def before():
        @pl.when(i == 0)
        def _():
            comm.first(srcs, outs, sems)

    def after():
        if comm.mid is not None:
            @pl.when(i == (3 * n) // 4)
            def _():
                comm.mid(srcs, outs, sems)

        @pl.when(i == n - 1)
        def _():
            comm.last(srcs, outs, sems)

    return before, after


def _scan_fwd(zs, lw, km, aa, bb, comm):
    s = zs.shape[0]
    n = s // SCAN_CHUNK
    cm, st = _scan_specs(False, n)
    k = comm.n

    def body(*refs):
        r_ref, lw_ref, k_ref, v_ref, a_ref, b_ref = refs[:6]
        y_ref, s0_ref, t_ref = refs[6 + k:9 + k]
        state = refs[9 + 2 * k]
        before, after = _comm_phases(comm, refs[6:6 + k] + refs[9 + k:9 + 2 * k] + refs[10 + 2 * k:], n)
        before()

        @pl.when(pl.program_id(0) == 0)
        def _():
            state[...] = jnp.zeros_like(state)

        s0 = state[...]
        s0_ref[0] = s0
        (y, s1), t = _scan_chunk(*[ref[...] for ref in (r_ref, lw_ref, k_ref, v_ref, a_ref, b_ref)], s0,
                                 _unit_lower_inverse)
        y_ref[...] = y
        t_ref[0] = t.astype(BF16)
        state[...] = s1
        after()

    per_chunk = (n, SCAN_PAIRS, 128, 128)
    res = _pcall(
        body, name="scan_fwd", grid=(n,), in_specs=[cm(0), cm(0), cm(0), cm(2), cm(0), cm(0)] + [_HBM] * k,
        out_specs=[cm(0), st, st] + [_HBM] * k,
        out_shape=[jax.ShapeDtypeStruct((s, D), F32), jax.ShapeDtypeStruct(per_chunk, F32),
                   jax.ShapeDtypeStruct(per_chunk, BF16)] + comm.out_shape,
        scratch_shapes=[pltpu.VMEM((SCAN_PAIRS, 128, 128), F32)] + comm.sems,
        compiler_params=_cparams(("arbitrary",)),
    )(zs, lw, km, zs, aa, bb, *comm.ins)
    return res[0], res[1], res[2], res[3:]


def _scan_bwd(zs, lw, km, aa, bb, s0s, ts, dy, comm):
    s = zs.shape[0]
    n = s // SCAN_CHUNK
    cm, st = _scan_specs(True, n)
    k = comm.n

    def body(*refs):
        r_ref, lw_ref, k_ref, v_ref, a_ref, b_ref, s0_ref, t_ref, dy_ref = refs[:9]
        douts = refs[9 + k:15 + k]
        dstate = refs[15 + 2 * k]
        before, after = _comm_phases(comm, refs[9:9 + k] + refs[15 + k:15 + 2 * k] + refs[16 + 2 * k:], n)
        before()

        @pl.when(pl.program_id(0) == 0)
        def _():
            dstate[...] = jnp.zeros_like(dstate)

        t = t_ref[0].astype(F32)
        prim = [ref[...] for ref in (r_ref, lw_ref, k_ref, v_ref, a_ref, b_ref)] + [s0_ref[0]]
        _, vjp, _ = jax.vjp(lambda *p: _scan_chunk(*p, lambda nil: _known_inverse(nil, t)), *prim, has_aux=True)
        grads = vjp((dy_ref[...], dstate[...]))
        for ref, gr in zip(douts, grads[:6]):
            ref[...] = gr
        dstate[...] = grads[6]
        after()

    res = _pcall(
        body, name="scan_bwd", grid=(n,),
        in_specs=[cm(0), cm(0), cm(0), cm(2), cm(0), cm(0), st, st, cm(0)] + [_HBM] * k,
        out_specs=[cm(0)] * 6 + [_HBM] * k, out_shape=[jax.ShapeDtypeStruct((s, D), F32)] * 6 + comm.out_shape,
        scratch_shapes=[pltpu.VMEM((SCAN_PAIRS, 128, 128), F32)] + comm.sems,
        compiler_params=_cparams(("arbitrary",)),
    )(zs, lw, km, zs, aa, bb, s0s, ts, dy, *comm.ins)
    return res[:6], res[6:]


_HBM = pl.BlockSpec(memory_space=pltpu.HBM)


def _me():
    return lax.axis_index("x"), lax.axis_index("y"), lax.axis_index("c")


def _allgather8(src, name):
    def body(src_ref, out_ref, ssem, rsem, lsem):
        x, y, c = _me()
        me = 4 * x + 2 * y + c
        local = pltpu.make_async_copy(src_ref, out_ref.at[me], lsem)
        local.start()
        peers = []
        for k in range(1, 8):
            peers.append(((1 - x) if k & 4 else x, (1 - y) if k & 2 else y, (1 - c) if k & 1 else c))
        sends = []
        for k, peer in enumerate(peers):
            cp = pltpu.make_async_remote_copy(src_ref, out_ref.at[me], ssem.at[k], rsem.at[k], device_id=peer,
                                              device_id_type=MESH)
            cp.start()
            sends.append(cp)
        for k, (px, py, pc) in enumerate(peers):
            pltpu.make_async_remote_copy(src_ref, out_ref.at[4 * px + 2 * py + pc], ssem.at[k], rsem.at[k],
                                         device_id=(px, py, pc), device_id_type=MESH).wait_recv()
        for cp in sends:
            cp.wait_send()
        local.wait()

    return _pcall(
        body, name=name, in_specs=[_HBM], out_specs=_HBM, out_shape=jax.ShapeDtypeStruct((8,) + src.shape, src.dtype),
        scratch_shapes=[pltpu.SemaphoreType.DMA((7,)), pltpu.SemaphoreType.DMA((7,)), pltpu.SemaphoreType.DMA],
    )(src)


def _other_chips(x, y):
    return [(1 - x, y), (x, 1 - y), (1 - x, 1 - y)]


def _remote(src, dst, ssem, rsem, to):
    return pltpu.make_async_remote_copy(src, dst, ssem, rsem, device_id=to, device_id_type=MESH)


class _GatherWeights:
    def __init__(self, shards):
        self.ins = list(shards)
        n = self.n = len(shards)
        self.out_shape = [jax.ShapeDtypeStruct((4,) + t.shape, t.dtype) for t in shards]
        self.sems = [pltpu.SemaphoreType.DMA((6 * n,)), pltpu.SemaphoreType.DMA((6 * n,)),
                     pltpu.SemaphoreType.DMA((n,)), pltpu.SemaphoreType.DMA((n,))]

    def _copies(self, srcs, outs, sems):
        ssem, rsem, lsem, osem = sems
        x, y, c = _me()
        me = 2 * x + y
        own, ici, landed, passed, passed_in = [], [], [], [], []
        for a in range(self.n):
            h = self.ins[a].shape[0] // 2
            mine, other = pl.ds(c * h, h), pl.ds((1 - c) * h, h)
            own.append(_remote(srcs[a], outs[a].at[me], lsem.at[a], osem.at[a], (x, y, 1 - c)))
            for k, (px, py) in enumerate(_other_chips(x, y)):
                s1, r1, s2, r2 = ssem.at[6 * a + k], rsem.at[6 * a + k], ssem.at[6 * a + 3 + k], rsem.at[6 * a + 3 + k]
                got, got_sib = outs[a].at[2 * px + py, mine], outs[a].at[2 * px + py, other]
                ici.append(_remote(srcs[a].at[mine], outs[a].at[me, mine], s1, r1, (px, py, c)))
                landed.append(_remote(got, got, s1, r1, (px, py, c)))
                passed.append(_remote(got, got, s2, r2, (x, y, 1 - c)))
                passed_in.append(_remote(got_sib, got_sib, s2, r2, (x, y, 1 - c)))
        return own, ici, landed, passed, passed_in

    def first(self, srcs, outs, sems):
        own, ici, _, _, _ = self._copies(srcs, outs, sems)
        for cp in own + ici:
            cp.start()

    def mid(self, srcs, outs, sems):
        _, _, landed, passed, _ = self._copies(srcs, outs, sems)
        for arrived, onward in zip(landed, passed):
            arrived.wait_recv()
            onward.start()

    def last(self, srcs, outs, sems):
        own, ici, _, passed, passed_in = self._copies(srcs, outs, sems)
        for cp in passed_in:
            cp.wait_recv()
        for cp in ici + passed:
            cp.wait_send()
        for cp in own:
            cp.wait()


class _ScatterToChips:
    def __init__(self, parts):
        self.ins = list(parts)
        n = self.n = len(parts)
        self.out_shape = [jax.ShapeDtypeStruct(t.shape, t.dtype) for t in parts]
        self.sems = [pltpu.SemaphoreType.DMA((3 * n,)), pltpu.SemaphoreType.DMA((3 * n,)), pltpu.SemaphoreType.DMA((n,))]

    def _copies(self, srcs, outs, sems):
        ssem, rsem, lsem = sems
        x, y, c = _me()
        me = 2 * x + y
        own, out, landed = [], [], []
        for a in range(self.n):
            own.append(pltpu.make_async_copy(srcs[a].at[me], outs[a].at[me], lsem.at[a]))
            for k, (px, py) in enumerate(_other_chips(x, y)):
                dst = outs[a].at[2 * px + py]
                out.append(_remote(srcs[a].at[2 * px + py], outs[a].at[me], ssem.at[3 * a + k], rsem.at[3 * a + k],
                                   (px, py, c)))
                landed.append(_remote(dst, dst, ssem.at[3 * a + k], rsem.at[3 * a + k], (px, py, c)))
        return own, out, landed

    def first(self, srcs, outs, sems):
        own, out, _ = self._copies(srcs, outs, sems)
        for cp in own + out:
            cp.start()

    mid = None

    def last(self, srcs, outs, sems):
        own, out, landed = self._copies(srcs, outs, sems)
        for cp in landed:
            cp.wait_recv()
        for cp in own:
            cp.wait()
        for cp in out:
            cp.wait_send()


def _run_comm(comm, name):
    n = comm.n

    def body(*refs):
        srcs, outs, sems = refs[:n], refs[n:2 * n], refs[2 * n:]
        comm.first(srcs, outs, sems)
        if comm.mid is not None:
            comm.mid(srcs, outs, sems)
        comm.last(srcs, outs, sems)

    return _pcall(body, name=name, in_specs=[_HBM] * n, out_specs=[_HBM] * n, out_shape=comm.out_shape,
                  scratch_shapes=comm.sems)(*comm.ins)


class _NoComm:
    n, ins, out_shape, sems, mid = 0, [], [], [], None

    def first(self, srcs, outs, sems):
        pass

    def last(self, srcs, outs, sems):
        pass


_NOTHING = _NoComm()


class _SiblingHalves:
    mid = None

    def __init__(self, grads):
        self.ins = list(grads)
        n = self.n = len(grads)
        self.out_shape = [jax.ShapeDtypeStruct((4, t.shape[1] // 2, t.shape[2]), t.dtype) for t in grads]
        self.sems = [pltpu.SemaphoreType.DMA((n,)), pltpu.SemaphoreType.DMA((n,))]

    def _copies(self, srcs, outs, sems):
        ssem, rsem = sems
        x, y, c = _me()
        copies = []
        for a in range(self.n):
            h = self.ins[a].shape[1] // 2
            copies.append(_remote(srcs[a].at[:, pl.ds((1 - c) * h, h)], outs[a], ssem.at[a], rsem.at[a], (x, y, 1 - c)))
        return copies

    def first(self, srcs, outs, sems):
        for cp in self._copies(srcs, outs, sems):
            cp.start()

    def last(self, srcs, outs, sems):
        for cp in self._copies(srcs, outs, sems):
            cp.wait()


def _reduce_finish(reds, name):
    n = len(reds)

    def body(*refs):
        outs = refs[n:2 * n]
        ssem, rsem = refs[2 * n:]
        x, y, c = _me()
        copies = []
        for a in range(n):
            h = reds[a].shape[0] // 2
            mine = outs[a].at[pl.ds(c * h, h)]
            copies.append(_remote(mine, mine, ssem.at[a], rsem.at[a], (x, y, 1 - c)))
        for cp in copies:
            cp.start()
        for a in range(n):
            h = reds[a].shape[0] // 2
            dst = outs[a].at[pl.ds((1 - c) * h, h)]
            _remote(dst, dst, ssem.at[a], rsem.at[a], (x, y, 1 - c)).wait_recv()
        for cp in copies:
            cp.wait_send()

    return _pcall(
        body, name=name, in_specs=[_HBM] * n, out_specs=[_HBM] * n,
        out_shape=[jax.ShapeDtypeStruct(t.shape, t.dtype) for t in reds],
        input_output_aliases={a: a for a in range(n)},
        scratch_shapes=[pltpu.SemaphoreType.DMA((n,)), pltpu.SemaphoreType.DMA((n,))],
    )(*reds)


def _half_sum(fn, full, halves, out_full, out_dtype, core, name):
    p, h, c = (halves[0].shape if halves else (full[0].shape[0], full[0].shape[1] // 2, full[0].shape[2]))
    br = _div(h, max(16, (1 << 19) // (p * c)), 16)
    nb = h // br
    mine3 = pl.BlockSpec((p, br, c), lambda i, core_ref: (0, core_ref[0] * nb + i, 0))
    half3 = pl.BlockSpec((p, br, c), lambda i, core_ref: (0, i, 0))

    def body(core_ref, *refs):
        refs[-1][...] = fn(*[t[...].astype(F32) for t in refs[:-1]]).astype(out_dtype)

    if out_full:
        out_spec = pl.BlockSpec((br, c), lambda i, core_ref: (core_ref[0] * nb + i, 0))
        out_shape = jax.ShapeDtypeStruct((2 * h, c), out_dtype)
    else:
        out_spec, out_shape = half3, jax.ShapeDtypeStruct((p, h, c), out_dtype)
    return _pcall(
        body, name=name,
        grid_spec=pltpu.PrefetchScalarGridSpec(
            num_scalar_prefetch=1, grid=(nb,), in_specs=[mine3] * len(full) + [half3] * len(halves),
            out_specs=out_spec),
        out_shape=out_shape, compiler_params=_cparams(("parallel",)),
    )(core, *full, *halves)


def _ada_fwd(c_all, w, b):
    def body(c_ref, w_ref, b_ref, o_ref):
        o_ref[...] = jnp.dot(c_ref[...], w_ref[...], precision=HI, preferred_element_type=F32) + b_ref[...]

    return _pcall(body, name="ada_fwd", out_shape=jax.ShapeDtypeStruct((c_all.shape[0], w.shape[1]), F32),
                  compiler_params=pltpu.CompilerParams(vmem_limit_bytes=VMEM_LIMIT))(c_all, w, b)


def _ada_bwd(c_all_t, d):
    def body(c_ref, d_ref, o_ref):
        o_ref[...] = jnp.dot(c_ref[...], d_ref[...], precision=HI, preferred_element_type=F32)

    return _pcall(body, name="ada_bwd", out_shape=jax.ShapeDtypeStruct((c_all_t.shape[0], d.shape[1]), F32),
                  compiler_params=pltpu.CompilerParams(vmem_limit_bytes=VMEM_LIMIT))(c_all_t, d)


def _sum_lead(x, name):
    p, r, n = x.shape
    br = _div(r, 512, 8)

    def body(x_ref, o_ref):
        acc = x_ref[0]
        for j in range(1, p):
            acc = acc + x_ref[j]
        o_ref[...] = acc

    return _pcall(
        body, name=name, grid=(r // br,), in_specs=[pl.BlockSpec((p, br, n), lambda i: (0, i, 0))],
        out_specs=pl.BlockSpec((br, n), lambda i: (i, 0)), out_shape=jax.ShapeDtypeStruct((r, n), F32),
        compiler_params=_cparams(("parallel",)),
    )(x)


def _adamw(w, g, m, v, name):
    shape = w.shape
    cols = shape[-1]
    w2, g2, m2, v2 = [t.reshape(-1, cols) for t in (w, g, m, v)]
    rows = w2.shape[0]
    pref = max(8, (1 << 19) // cols // 8 * 8)
    br = _div(rows, pref, 8)
    if rows // br > 64:
        br = pref
    outs = _rows_fwd(_f_adamw, [(t, cols, 0) for t in (w2, g2, m2, v2)], [], [(cols, F32)] * 3, name=name, br=br)
    return [o.reshape(shape) for o in outs]


_BIG = (("w_in", 1), ("w_up", 1), ("w_down", 0), ("w_o", 0), ("w_rwkv_out", 0), ("w_att_out", 1), ("w2", 1), ("a2", 1),
        ("g2", 1))


_NEEDED_FIRST = ("w_in", "w_att_out", "w2", "a2", "g2")
_NEEDED_LATER = ("w_up", "w_down", "w_o", "w_rwkv_out")
_DONE_EARLY = ("w_up", "w_down", "w_o", "w_rwkv_out", "w_att_out")
_DONE_LATE = ("w_in", "w2", "a2", "g2")


def _cols_joined(t):
    return jnp.concatenate([t[j] for j in range(4)], axis=1)


def _cols_split(t):
    n = t.shape[1] // 4
    return jnp.stack([t[:, j * n:(j + 1) * n] for j in range(4)])


W_IN_SHARD = (N_ATT + N_RW + N_GATE) // 4
W_IN_PAD = 2560


def _row_window(parts, lo, hi):
    out, pos = [], 0
    for t, w in parts:
        a, b = max(lo, pos), min(hi, pos + w)
        if a < b:
            out.append(t[a - pos:b - pos])
        pos += w
    return out[0] if len(out) == 1 else jnp.concatenate(out, axis=0)


def _rows_joined(t):
    return t.reshape(4 * t.shape[1], t.shape[2])


def _rows_split(t):
    return t.reshape(4, t.shape[0] // 4, t.shape[1])


def _step_to_scan(x, tgt, ada, wts):
    sh1, sc1, gt1, sh2, sc2, gt2 = ada
    br = 256
    grp = lax.broadcasted_iota(jnp.int32, (D, 128), 0) // 64 == lax.broadcasted_iota(jnp.int32, (D, 128), 1)
    e = grp.astype(F32)
    et = e.T
    w_in = [(wts["w_in"][j], W_IN_SHARD) for j in range(4)]
    w_att = _row_window(w_in, 0, N_ATT)
    w_rw = jnp.concatenate([_row_window(w_in, N_ATT, N_ATT + N_RW), jnp.zeros((N_RWP - N_RW, D), BF16)], axis=0)
    w_gate = _row_window(w_in, N_ATT + N_RW, N_ATT + N_RW + N_GATE)
    mu = jnp.pad(wts["mu_shift"], ((0, 0), (0, N_RWP - N_RW)))
    wl = jnp.zeros((N_LORA, 3 * D), F32)
    wl = wl.at[0:64, 0:D].set(_cols_joined(wts["w2"]).astype(F32))
    wl = wl.at[64:128, D:2 * D].set(_cols_joined(wts["a2"]).astype(F32))
    wl = wl.at[128:288, 2 * D:3 * D].set(_cols_joined(wts["g2"]).astype(F32))
    pre1_c = [wts["norm1_w"], sc1, sh1]
    (h1,) = _rows_fwd(_f_pre, [(x, D, 0)], pre1_c, [(D, BF16), None], name="pre1_fwd", br=2 * br)
    att_in = _mm(h1, w_att, tb=True, name="mm_att_in")
    z = _mm(h1, w_rw, tb=True, out_dtype=BF16, name="mm_rw_in")
    gate_in = _mm(h1, w_gate, tb=True, out_dtype=BF16, name="mm_gate_in")
    att_o, att_l = [], []
    for g, (_, dil) in enumerate(ATT_PATTERNS):
        o, l = _att_fwd(att_in, g, dil)
        att_o.append(o)
        att_l.append(l)
    comb_rows = [(t, ATT_WIDTH, 0) for t in att_o + att_l]
    (att,) = _rows_fwd(_f_comb, comb_rows, [], [(ATT_WIDTH, BF16)], name="comb_fwd", br=2 * br)
    w_ao = _cols_joined(wts["w_att_out"])
    y_att = _mm(att, w_ao, out_dtype=BF16, name="mm_att_out")
    rwpre_c = [wts["w0"], wts["a0"], wts["k_k"], wts["k_a"], wl, e, et]

    def shift_and_rwpre(zz, *rest):
        consts, mu_row, before = rest[:-2], rest[-2], rest[-1]
        last = jnp.sum(jnp.where(lax.broadcasted_iota(jnp.int32, before.shape, 0) == HALO - 1, before, 0.0), axis=0,
                       keepdims=True)
        row = lax.broadcasted_iota(jnp.int32, zz.shape, 0)
        zprev = jnp.where(row == 0, last, pltpu.roll(zz, 1, 0))
        shifted = zz + (zprev - zz) * mu_row
        return (shifted,) + tuple(_f_rwpre(shifted, *consts))

    zs, lw, km, aa, bb, gg = _rows_fwd(
        shift_and_rwpre, [(z, N_RWP, 0)], rwpre_c + [mu],
        [(N_RWP, F32), None, (D, F32), (D, F32), None, (D, F32), (D, F32), (D, F32)], name="rwpre_fwd", br=br, halo=0)
    return dict(x=x, tgt=tgt, wts=wts, br=br, e=e, et=et, gt1=gt1, sc2=sc2, sh2=sh2, gt2=gt2, w_att=w_att, w_rw=w_rw,
                w_ao=w_ao,
                w_gate=w_gate, mu=mu, pre1_c=pre1_c, h1=h1, att_in=att_in, z=z, gate_in=gate_in, comb_rows=comb_rows,
                att=att, y_att=y_att, zs=zs, rwpre_c=rwpre_c, lw=lw, km=km, aa=aa, bb=bb, gg=gg)


def _step_between_scans(st, y_raw, late):
    x, tgt, wts, br, e, et = st["x"], st["tgt"], st["wts"], st["br"], st["e"], st["et"]
    zs, km, gg, gate_in, y_att, att = st["zs"], st["km"], st["gg"], st["gate_in"], st["y_att"], st["att"]
    comb_rows, att_in = st["comb_rows"], st["att_in"]
    gt1, sc2, sh2, gt2 = st["gt1"], st["sc2"], st["sh2"], st["gt2"]
    w_up, w_ao = late["w_up"], st["w_ao"]
    w_down, w_o, w_ro = _rows_joined(late["w_down"]), _rows_joined(late["w_o"]), _rows_joined(late["w_rwkv_out"])
    post_rows = [(y_raw, D, 0), (zs, D, 0), (zs, D, 2), (km, D, 0), (gg, D, 0)]
    post_c = [wts["lnx_w"], wts["lnx_b"], wts["r_k"], e, et]
    (rw_out,) = _rows_fwd(_f_rwpost, post_rows, post_c, [(D, BF16)], name="rwpost_fwd", br=br)
    y_rw = _mm(rw_out, w_ro, out_dtype=BF16, name="mm_rw_out")
    mix_rows = [(gate_in, N_GATE, 0), (y_att, D, 0), (y_rw, D, 0)]
    (mix,) = _rows_fwd(_f_mix, mix_rows, [wts["b_gate"]], [(D, BF16)], name="mix_fwd", br=2 * br)
    o = _mm(mix, w_o, out_dtype=BF16, name="mm_o")
    pre2_c = [gt1, wts["norm2_w"], sc2, sh2]
    x1, h2 = _rows_fwd(_f_pre2, [(x, D, 0), (o, D, 0)], pre2_c, [(D, F32), (D, BF16)], name="pre2_fwd", br=2 * br)
    u = _mm(h2, w_up, b_chip=True, name="mm_up")
    act = _conv_fwd(u, wts["conv_w"], wts["conv_b"])
    f = _mm(act, w_down, out_dtype=BF16, name="mm_down")
    fin_rows = [(x1, D, 0), (f, D, 0), (tgt, D, 0)]
    fin_c = [gt2, wts["norm_f_w"]]

    def fin_fwd(*a):
        (l,) = _f_fin(*a)
        return (jnp.broadcast_to(jnp.sum(l, axis=0, keepdims=True), (8, 128)),)

    (loss_acc,) = _rows_fwd(fin_fwd, fin_rows, fin_c, [], name="fin_fwd", br=2 * br, acc_shape=(8, 128))

    gw = {}
    dx1a, df, d_gt2, gw["norm_f_w"] = _rows_bwd(
        _f_fin, fin_rows, fin_c, [[]], wrt_rows=[0, 1], wrt_consts=[0, 1], drow_dtypes=[F32, BF16],
        name="fin_bwd", br=2 * br, unit_cot=True)
    dact = _mm(df, w_down, tb=True, name="mm_dact")
    gw["w_down"] = _rows_split(_mm(act, df, ta=True, out_dtype=BF16, name="mm_dw_down"))
    du, gw["conv_w"], gw["conv_b"] = _conv_bwd(u, wts["conv_w"], wts["conv_b"], dact)
    dh2 = _mm(du, w_up, tb=True, b_chip=True, out_dtype=BF16, name="mm_dh2")
    gw["w_up"] = _mm(h2, du, ta=True, out_chip=True, out_dtype=BF16, name="mm_dw_up")
    dxa, do, d_gt1, gw["norm2_w"], d_sc2, d_sh2 = _rows_bwd(
        _f_pre2, [(x, D, 0), (o, D, 0)], pre2_c, [[(dx1a, D, 0)], [(dh2, D, 0)]], wrt_rows=[0, 1],
        wrt_consts=[0, 1, 2, 3], drow_dtypes=[F32, BF16], name="pre2_bwd", br=2 * br)
    dmix = _mm(do, w_o, tb=True, out_dtype=BF16, name="mm_dmix")
    gw["w_o"] = _rows_split(_mm(mix, do, ta=True, out_dtype=BF16, name="mm_dw_o"))
    dgate, dya, dyr, gw["b_gate"] = _rows_bwd(
        _f_mix, mix_rows, [wts["b_gate"]], [[(dmix, D, 0)]], wrt_rows=[0, 1, 2], wrt_consts=[0],
        drow_dtypes=[BF16] * 3, name="mix_bwd", br=2 * br)
    datt = _mm(dya, w_ao, tb=True, out_dtype=BF16, name="mm_datt")
    gw["w_att_out"] = _mm(att, dya, ta=True, out_chip=True, out_dtype=BF16, name="mm_dw_att_out")
    drw = _mm(dyr, w_ro, tb=True, out_dtype=BF16, name="mm_drw")
    gw["w_rwkv_out"] = _rows_split(_mm(rw_out, dyr, ta=True, out_dtype=BF16, name="mm_dw_rw_out"))
    dcomb = _rows_bwd(_f_comb, comb_rows, [], [[(datt, ATT_WIDTH, 0)]], wrt_rows=list(range(6)), wrt_consts=[],
                      drow_dtypes=[F32] * 6, name="comb_bwd", br=2 * br)
    datt_in = None
    for g, (_, dil) in enumerate(ATT_PATTERNS):
        datt_in = _att_bwd(att_in, g, dil, dcomb[g], dcomb[3 + g], datt_in)
    dy_raw, dr_p, dv_p, dkm_p, dgg, gw["lnx_w"], gw["lnx_b"], gw["r_k"], *recv_early = _rows_bwd(
        _f_rwpost, post_rows, post_c, [[(drw, D, 0)]], wrt_rows=[0, 1, 2, 3, 4], wrt_consts=[0, 1, 2],
        drow_dtypes=[F32] * 5, name="rwpost_bwd", br=br, comm=_SiblingHalves([gw[n] for n in _DONE_EARLY]))
    st.update(loss=loss_acc[0, 0], gw=gw, dxa=dxa, dgate=dgate, datt_in=datt_in,
              dy_raw=dy_raw, dr_p=dr_p, dv_p=dv_p, dkm_p=dkm_p, dgg=dgg, d_ada_late=(d_gt1, d_sh2, d_sc2, d_gt2),
              recv_early=recv_early)
    return st


def _chip_parts(grads, recv, names, core):
    return [_half_sum(lambda a, b: a + b, [g], [r], False, BF16, core, "reduce_add2_" + n)
            for g, r, n in zip(grads, recv, names)]


def _step_after_scan(st, scan_grads, core):
    x, br, gw, h1, zs = st["x"], st["br"], st["gw"], st["h1"], st["zs"]
    dr_s, dlw, dkm_s, dv_s, daa, dbb = scan_grads
    pre_cots = [[(st["dr_p"], D, 0), (dr_s, D, 0)], [(dlw, D, 0)], [(st["dkm_p"], D, 0), (dkm_s, D, 0)],
                [(st["dv_p"], D, 0), (dv_s, D, 0)], [(daa, D, 0)], [(dbb, D, 0)], [(st["dgg"], D, 0)]]
    dzs, gw["w0"], gw["a0"], gw["k_k"], gw["k_a"], dwl = _rows_bwd(
        _f_rwpre, [(zs, N_RWP, 0)], st["rwpre_c"], pre_cots, wrt_rows=[0], wrt_consts=[0, 1, 2, 3, 4],
        drow_dtypes=[F32], name="rwpre_bwd", br=128)
    gw["w2"], gw["a2"] = _cols_split(dwl[0:64, 0:D]), _cols_split(dwl[64:128, D:2 * D])
    gw["g2"] = _cols_split(dwl[128:288, 2 * D:3 * D])
    dz, dmu = _shift_bwd(st["z"], st["mu"], dzs)
    gw["mu_shift"] = dmu[:, :N_RW]
    datt_in, dgate = st["datt_in"], st["dgate"]
    dw_in = [(_mm(datt_in, h1, ta=True, out_dtype=BF16, name="mm_dw_att"), N_ATT),
             (_mm(dz, h1, ta=True, out_dtype=BF16, name="mm_dw_rw"), N_RW),
             (_mm(dgate, h1, ta=True, out_dtype=BF16, name="mm_dw_gate"), N_GATE)]
    slabs = []
    for j in range(4):
        slabs += [_row_window(dw_in, j * W_IN_SHARD, (j + 1) * W_IN_SHARD), jnp.zeros((W_IN_PAD - W_IN_SHARD, D), BF16)]
    gw["w_in"] = jnp.concatenate(slabs, axis=0).reshape(4, W_IN_PAD, D)
    late = [gw[n] for n in _DONE_LATE]
    parts = _chip_parts(late, _run_comm(_SiblingHalves(late), "reduce_sib_late"), _DONE_LATE, core)
    dh1, slots_late = _mm_sum([(datt_in, st["w_att"]), (dz, st["w_rw"]), (dgate, st["w_gate"])],
                              comm=_ScatterToChips(parts), name="mm_dh1")
    grad_x, gw["norm1_w"], d_sc1, d_sh1 = _rows_bwd(
        _f_pre, [(x, D, 0)], st["pre1_c"], [[(dh1, D, 0)], [(st["dxa"], D, 0)]], wrt_rows=[0], wrt_consts=[0, 1, 2],
        drow_dtypes=[F32], name="pre1_bwd", br=2 * br)
    d_gt1, d_sh2, d_sc2, d_gt2 = st["d_ada_late"]
    return st["loss"], grad_x, (d_sh1, d_sc1, d_gt1, d_sh2, d_sc2, d_gt2), gw, slots_late


_SMALL = ("b_ada", "norm1_w", "b_gate", "mu_shift", "w0", "a0", "k_k", "k_a", "r_k", "lnx_w", "lnx_b", "norm2_w",
          "conv_b", "norm_f_w")
_NAMES = ("w_ada", "b_ada", "norm1_w", "w_in", "b_gate", "mu_shift", "w0", "w2", "a0", "a2", "g2", "k_k", "k_a", "r_k",
          "lnx_w", "lnx_b", "w_att_out", "w_rwkv_out", "w_o", "norm2_w", "w_up", "conv_w", "conv_b", "w_down",
          "norm_f_w")


def kernel(x, c, w_ada, b_ada, norm1_w, w_in, b_gate, mu_shift, w0, w2, a0, a2, g2, k_k, k_a, r_k, lnx_w, lnx_b, w_att_out, w_rwkv_out, w_o, norm2_w, w_up, conv_w, conv_b, w_down, norm_f_w, loss_target, m_w_ada, m_b_ada, m_norm1_w, m_w_in, m_b_gate, m_mu_shift, m_w0, m_w2, m_a0, m_a2, m_g2, m_k_k, m_k_a, m_r_k, m_lnx_w, m_lnx_b, m_w_att_out, m_w_rwkv_out, m_w_o, m_norm2_w, m_w_up, m_conv_w, m_conv_b, m_w_down, m_norm_f_w, v_w_ada, v_b_ada, v_norm1_w, v_w_in, v_b_gate, v_mu_shift, v_w0, v_w2, v_a0, v_a2, v_g2, v_k_k, v_k_a, v_r_k, v_lnx_w, v_lnx_b, v_w_att_out, v_w_rwkv_out, v_w_o, v_norm2_w, v_w_up, v_conv_w, v_conv_b, v_w_down, v_norm_f_w):
    args = dict(locals())
    p, pm, pv = {}, {}, {}
    for name in _NAMES:
        for dst, key in ((p, name), (pm, "m_" + name), (pv, "v_" + name)):
            t = args[key]
            if name == "w_in":
                dst[name] = jnp.swapaxes(t, 1, 2)[0]
            else:
                dst[name] = t.reshape(1, -1) if name in ("r_k", "norm_f_w") else t.reshape(t.shape[-2], t.shape[-1])
    xi, yi, ci = _me()
    chip = 2 * xi + yi
    dev = 4 * xi + 2 * yi + ci
    x2, tgt = x[0], loss_target[0]

    n_cw = 3 * (2 * D_FF // 4)
    vec = jnp.concatenate([c.reshape(-1), p["conv_w"].reshape(-1), jnp.zeros((8 * D - D - n_cw,), F32)]).reshape(8, D)
    g0 = _allgather8(vec, "gather_c").reshape(8, 8 * D)
    c_all = g0[:, :D]
    conv_w_full = jnp.concatenate([g0[2 * j, D:D + n_cw].reshape(3, -1) for j in range(4)], axis=1)
    n_ada = 6 * D // 4
    b_ada_sh = lax.dynamic_slice(p["b_ada"], (0, chip * n_ada), (1, n_ada))
    ada_sh = _ada_fwd(c_all, p["w_ada"], b_ada_sh)
    ga = _allgather8(ada_sh, "gather_ada")
    ada_all = jnp.concatenate([ga[2 * j] for j in range(4)], axis=1)
    ada_row = lax.dynamic_slice(ada_all, (dev, 0), (1, 6 * D))
    ada = [ada_row[:, j * D:(j + 1) * D] for j in range(6)]

    big = [n for n, _ in _BIG]
    shard = {n: p[n].astype(BF16) for n in big}
    shard["w_in"] = jnp.pad(shard["w_in"], ((0, W_IN_PAD - W_IN_SHARD), (0, 0)))
    wts = dict(zip(_NEEDED_FIRST, _run_comm(_GatherWeights([shard[n] for n in _NEEDED_FIRST]), "gather_w")))
    for n in _SMALL:
        wts[n] = p[n]
    wts["conv_w"] = conv_w_full
    core = ci.reshape(1).astype(jnp.int32)

    st = _step_to_scan(x2, tgt, ada, wts)
    y_raw, s0s, inverses, late = _scan_fwd(st["zs"], st["lw"], st["km"], st["aa"], st["bb"],
                                           _GatherWeights([shard[n] for n in _NEEDED_LATER]))
    st = _step_between_scans(st, y_raw, dict(zip(_NEEDED_LATER, late)))
    early = _chip_parts([st["gw"][n] for n in _DONE_EARLY], st["recv_early"], _DONE_EARLY, core)
    scan_grads, slots_early = _scan_bwd(st["zs"], st["lw"], st["km"], st["aa"], st["bb"], s0s, inverses,
                                        st["dy_raw"], _ScatterToChips(early))
    loss_part, grad_x, d_ada, gw, slots_late = _step_after_scan(st, scan_grads, core)

    small = [jnp.concatenate(d_ada, axis=1)] + [gw[n] for n in _SMALL[1:]] + [gw["conv_w"], loss_part.reshape(1, 1)]
    sizes = [t.size for t in small]
    flat = jnp.concatenate([t.reshape(-1) for t in small])
    npad = (-flat.shape[0]) % (8 * D)
    srows = (flat.shape[0] + npad) // D
    flat = jnp.concatenate([flat, jnp.zeros((npad,), F32)]).reshape(srows, D)
    parts = _allgather8(flat, "gather_small")
    tot = _sum_lead(parts, "sum_small").reshape(-1)
    pieces, pos = [], 0
    for sz in sizes:
        pieces.append(tot[pos:pos + sz])
        pos += sz
    grads = {}
    for n, piece in zip(_SMALL, pieces[:len(_SMALL)]):
        grads[n] = piece.reshape(p[n].shape)
    conv_w_grad = pieces[len(_SMALL)].reshape(3, 2 * D_FF)
    grads["conv_w"] = lax.dynamic_slice(conv_w_grad, (0, chip * (n_cw // 3)), (3, n_cw // 3))
    loss = pieces[-1][0]
    d_ada_all = parts[:, :6].reshape(8, 6 * D)
    grads["w_ada"] = _ada_bwd(c_all.T, lax.dynamic_slice(d_ada_all, (0, chip * n_ada), (8, n_ada)))

    order = _DONE_EARLY + _DONE_LATE
    reds = [_half_sum(lambda t: t[0] + t[1] + t[2] + t[3], [], [t], True, F32, core, "reduce_add4_" + n)
            for n, t in zip(order, list(slots_early) + list(slots_late))]
    for n, g in zip(order, _reduce_finish(reds, "reduce_sib2")):
        grads[n] = g

    outs_g, outs_d, outs_m, outs_v = [], [], [], []
    grads["w_in"] = grads["w_in"][:W_IN_SHARD]
    for name in _NAMES:
        g = grads[name]
        d, m, v = _adamw(p[name], g, pm[name], pv[name], "adamw_" + name)
        shape = args[name].shape
        for outs, t in ((outs_g, g), (outs_d, d), (outs_m, m), (outs_v, v)):
            outs.append(jnp.swapaxes(t[None], 1, 2) if name == "w_in" else t.reshape(shape))
    return (loss, grad_x.reshape(x.shape), *outs_g, *outs_d, *outs_m, *outs_v)
```

```python
import functools
import math

import jax
import jax.numpy as jnp
import numpy as np
from jax import lax
from jax.experimental import pallas as pl
from jax.experimental.pallas import tpu as pltpu

F32 = jnp.float32
BF16 = jnp.bfloat16
HI = lax.Precision.HIGHEST
MESH = pl.DeviceIdType.MESH

D = 1024
ATT_PATTERNS = ((128, 1), (512, 4), (2048, 16))
ATT_BLOCK = 128
ATT_WIDTH = 512
N_ATT = 3 * 3 * ATT_WIDTH
N_RW = 3 * D + 64 + 64 + 160
N_RWP = 3456
N_LORA = N_RWP - 3 * D
N_GATE = 2 * D
D_FF = 2816
RMS_EPS = 1e-6
GN_EPS = 64e-5
SCAN_CHUNK = 64
SCAN_PAIRS = 8
NEG = -1e30
VMEM_LIMIT = 48 * 1024 * 1024
HALO = 16

ADAM_LR, ADAM_B1, ADAM_B2, ADAM_EPS, ADAM_WD, ADAM_STEP = 0.001, 0.9, 0.999, 1e-08, 0.01, 10


def _pcall(body, **kw):
    return pl.pallas_call(body, **kw)


def _cparams(sem):
    return pltpu.CompilerParams(dimension_semantics=sem, vmem_limit_bytes=VMEM_LIMIT)


def _div(n, pref, mult):
    best = None
    d = mult
    while d <= min(n, pref):
        if n % d == 0:
            best = d
        d += mult
    return best if best else n


def _dg(a, b, ca, cb):
    return lax.dot_general(a.astype(BF16), b.astype(BF16), (((ca,), (cb,)), ((), ())), preferred_element_type=F32)


@jax.custom_vjp
def _nn(a, b):
    return _dg(a, b, 1, 0)


@jax.custom_vjp
def _nt(a, b):
    return _dg(a, b, 1, 1)


@jax.custom_vjp
def _tn(a, b):
    return _dg(a, b, 0, 0)


_nn.defvjp(lambda a, b: (_nn(a, b), (a, b)), lambda res, g: (_nt(g, res[1]), _tn(res[0], g)))
_nt.defvjp(lambda a, b: (_nt(a, b), (a, b)), lambda res, g: (_nn(g, res[1]), _tn(g, res[0])))
_tn.defvjp(lambda a, b: (_tn(a, b), (a, b)), lambda res, g: (_nt(res[1], g), _nn(res[0], g)))


def _bdg(a, b, ca, cb):
    return lax.dot_general(a.astype(BF16), b.astype(BF16), (((ca,), (cb,)), ((0,), (0,))), preferred_element_type=F32)


@jax.custom_vjp
def _bnn(a, b):
    return _bdg(a, b, 2, 1)


@jax.custom_vjp
def _bnt(a, b):
    return _bdg(a, b, 2, 2)


@jax.custom_vjp
def _btn(a, b):
    return _bdg(a, b, 1, 1)


_bnn.defvjp(lambda a, b: (_bnn(a, b), (a, b)), lambda res, g: (_bnt(g, res[1]), _btn(res[0], g)))
_bnt.defvjp(lambda a, b: (_bnt(a, b), (a, b)), lambda res, g: (_bnn(g, res[1]), _btn(g, res[0])))
_btn.defvjp(lambda a, b: (_btn(a, b), (a, b)), lambda res, g: (_bnt(res[1], g), _bnn(res[0], g)))


def _hsum_impl(x, e, et):
    eb, etb = e.astype(BF16), et.astype(BF16)
    s = jnp.dot(x.astype(BF16), eb, preferred_element_type=F32)
    return jnp.dot(s.astype(BF16), etb, preferred_element_type=F32)


@jax.custom_vjp
def _hsum(x, e, et):
    return _hsum_impl(x, e, et)


_hsum.defvjp(lambda x, e, et: (_hsum_impl(x, e, et), (e, et)),
             lambda res, g: (_hsum_impl(g, res[0], res[1]), jnp.zeros_like(res[0]), jnp.zeros_like(res[1])))


def _mm(a, b, *, ta=False, tb=False, out_dtype=F32, add=None, b_chip=False, out_chip=False, comm=None, name):
    riding = _NOTHING if comm is None else comm
    nc = riding.n
    if ta:
        kdim, m = a.shape
    else:
        m, kdim = a.shape
    if b_chip:
        n = b.shape[1] if tb else 4 * b.shape[2]
    else:
        n = b.shape[0] if tb else b.shape[1]
    tm, tn, tk = _div(m, 1536, 128), _div(n, 1536, 128), _div(kdim, 2048 if ta else 1408, 128)
    if b_chip and tb:
        tk = kdim // 4
    if (b_chip and not tb) or out_chip:
        tn = n // 4
    nk = kdim // tk
    ca, cb = (0 if ta else 1), (1 if tb else 0)

    nin = 2 if add is None else 3
    gi, gj = m // tm, n // tn

    def body(*refs):
        a_ref, b_ref = refs[0], refs[1]
        add_ref = None if add is None else refs[2]
        o_ref = refs[nin + nc]
        step = (pl.program_id(0) * gj + pl.program_id(1)) * nk + pl.program_id(2)
        before, after = _comm_phases(riding, refs[nin:nin + nc] + refs[nin + nc + 1:nin + 2 * nc + 1]
                                     + refs[nin + 2 * nc + 1 + (nk > 1):], gi * gj * nk, step)
        before()
        part = lax.dot_general(a_ref[...], b_ref[...], (((ca,), (cb,)), ((), ())), preferred_element_type=F32)

        def finish(r):
            if add_ref is not None:
                r = r + add_ref[...]
            o_ref[...] = r.astype(o_ref.dtype)

        if nk == 1:
            finish(part)
            after()
            return
        acc = refs[nin + 2 * nc + 1]
        k = pl.program_id(2)

        @pl.when(k == 0)
        def _():
            acc[...] = part

        @pl.when(k > 0)
        def _():
            acc[...] += part

        @pl.when(k == nk - 1)
        def _():
            finish(acc[...])

        after()

    a_spec = pl.BlockSpec((tk, tm), lambda i, j, k: (k, i)) if ta else pl.BlockSpec((tm, tk), lambda i, j, k: (i, k))
    if b_chip:
        b_spec = (pl.BlockSpec((None, tn, tk), lambda i, j, k: (k, j, 0)) if tb
                  else pl.BlockSpec((None, tk, tn), lambda i, j, k: (j, k, 0)))
    else:
        b_spec = pl.BlockSpec((tn, tk), lambda i, j, k: (j, k)) if tb else pl.BlockSpec((tk, tn), lambda i, j, k: (k, j))
    in_specs = [a_spec, b_spec]
    args = [a, b]
    if add is not None:
        in_specs.append(pl.BlockSpec((tm, tn), lambda i, j, k: (i, j)))
        args.append(add)
    if out_chip:
        out_spec = pl.BlockSpec((None, tm, tn), lambda i, j, k: (j, i, 0))
        out_shape = jax.ShapeDtypeStruct((4, m, tn), out_dtype)
    else:
        out_spec = pl.BlockSpec((tm, tn), lambda i, j, k: (i, j))
        out_shape = jax.ShapeDtypeStruct((m, n), out_dtype)
    res = _pcall(
        body, name=name, grid=(gi, gj, nk), in_specs=in_specs + [_HBM] * nc, out_specs=[out_spec] + [_HBM] * nc,
        out_shape=[out_shape] + riding.out_shape,
        scratch_shapes=([] if nk == 1 else [pltpu.VMEM((tm, tn), F32)]) + riding.sems,
        compiler_params=_cparams(("arbitrary",) * 3 if nc else ("parallel", "parallel", "arbitrary")),
    )(*args, *riding.ins)
    return res[0] if comm is None else (res[0], res[1:])


def _mm_sum(pairs, *, comm, name):
    m, n = pairs[0][0].shape[0], pairs[0][1].shape[1]
    tm, tn = _div(m, 1024, 128), _div(n, 1024, 128)
    tks = [_div(a.shape[1], 1408, 128) for a, _ in pairs]
    nks = [a.shape[1] // tk for (a, _), tk in zip(pairs, tks)]
    offs = [sum(nks[:p]) for p in range(len(pairs))]
    total, npair, nc = sum(nks), len(pairs), comm.n
    gi, gj = m // tm, n // tn

    def body(*refs):
        o_ref, acc = refs[2 * npair + nc], refs[2 * npair + 2 * nc + 1]
        k = pl.program_id(2)
        step = (pl.program_id(0) * gj + pl.program_id(1)) * total + k
        before, after = _comm_phases(comm, refs[2 * npair:2 * npair + nc]
                                     + refs[2 * npair + nc + 1:2 * npair + 2 * nc + 1]
                                     + refs[2 * npair + 2 * nc + 2:], gi * gj * total, step)
        before()
        for p in range(npair):
            def partial_product(p=p):
                part = jnp.dot(refs[2 * p][...], refs[2 * p + 1][...], preferred_element_type=F32)
                if p == 0:
                    @pl.when(k == 0)
                    def _():
                        acc[...] = part

                    @pl.when(k > 0)
                    def _():
                        acc[...] += part
                else:
                    acc[...] += part

            pl.when(jnp.logical_and(k >= offs[p], k < offs[p] + nks[p]))(partial_product)

        @pl.when(k == total - 1)
        def _():
            o_ref[...] = acc[...].astype(o_ref.dtype)

        after()

    def specs(tk, off, nk):
        def kb(k):
            return jnp.clip(k - off, 0, nk - 1)
        return [pl.BlockSpec((tm, tk), lambda i, j, k: (i, kb(k))), pl.BlockSpec((tk, tn), lambda i, j, k: (kb(k), j))]

    in_specs, args = [], []
    for (a, b), tk, off, nk in zip(pairs, tks, offs, nks):
        in_specs += specs(tk, off, nk)
        args += [a, b]
    res = _pcall(
        body, name=name, grid=(gi, gj, total), in_specs=in_specs + [_HBM] * nc,
        out_specs=[pl.BlockSpec((tm, tn), lambda i, j, k: (i, j))] + [_HBM] * nc,
        out_shape=[jax.ShapeDtypeStruct((m, n), BF16)] + comm.out_shape,
        scratch_shapes=[pltpu.VMEM((tm, tn), F32)] + comm.sems,
        compiler_params=_cparams(("arbitrary",) * 3),
    )(*args, *comm.ins)
    return res[0], res[1:]


def _row_spec(br, w, cb):
    return pl.BlockSpec((br, w), lambda i: (i, cb))


def _const_spec(shape):
    return pl.BlockSpec(shape, lambda i: (0,) * len(shape))


def _rows_fwd(fn, rows, consts, outs, *, name, br, acc_shape=None, halo=None):
    s = rows[0][0].shape[0]
    nr, nc = len(rows), len(consts)
    kept = [k for k, o in enumerate(outs) if o is not None]

    def body(*refs):
        xs = [r[...].astype(F32) for r in refs[:nr]]
        cs = [c[...] for c in refs[nr:nr + nc]]
        if halo is not None:
            cs.append(jnp.where(pl.program_id(0) == 0, 0.0, refs[nr + nc][...].astype(F32)))
        res = fn(*xs, *cs)
        orefs = refs[nr + nc + (halo is not None):]
        for j, k in enumerate(kept):
            orefs[j][...] = res[k].astype(orefs[j].dtype)
        if acc_shape is not None:
            acc_ref = orefs[len(kept)]

            @pl.when(pl.program_id(0) == 0)
            def _():
                acc_ref[...] = jnp.zeros_like(acc_ref)

            acc_ref[...] += res[len(outs)]

    in_specs = [_row_spec(br, w, cb) for (_, w, cb) in rows] + [_const_spec(c.shape) for c in consts]
    args = [r[0] for r in rows] + list(consts)
    if halo is not None:
        harr, hw, hcb = rows[halo]
        in_specs.append(pl.BlockSpec((HALO, hw), lambda i: (jnp.maximum(i * (br // HALO) - 1, 0), hcb)))
        args.append(harr)
    out_specs = [_row_spec(br, outs[k][0], 0) for k in kept]
    out_shape = [jax.ShapeDtypeStruct((s, outs[k][0]), outs[k][1]) for k in kept]
    if acc_shape is not None:
        out_specs.append(_const_spec(acc_shape))
        out_shape.append(jax.ShapeDtypeStruct(acc_shape, F32))
    return _pcall(
        body, name=name, grid=(pl.cdiv(s, br),), in_specs=in_specs, out_specs=out_specs, out_shape=out_shape,
        compiler_params=_cparams(("arbitrary",)),
    )(*args)


def _rows_bwd(fn, rows, consts, cots, *, wrt_rows, wrt_consts, drow_dtypes, name, br, unit_cot=False, comm=None):
    comm = _NOTHING if comm is None else comm
    ncomm = comm.n
    nout = len(wrt_rows) + len(wrt_consts)
    s = rows[0][0].shape[0]
    nr, nc = len(rows), len(consts)
    flat_cots = [c for lst in cots for c in lst]
    ncot = len(flat_cots)

    def body(*refs):
        xs = [r[...].astype(F32) for r in refs[:nr]]
        cs = [c[...] for c in refs[nr:nr + nc]]
        cvals = [c[...].astype(F32) for c in refs[nr + nc:nr + nc + ncot]]
        orefs = refs[nr + nc + ncot + ncomm:]
        before, after = _comm_phases(comm, refs[nr + nc + ncot:nr + nc + ncot + ncomm] + orefs[nout:], s // br)
        before()

        def g(*d):
            xs2, cs2 = list(xs), list(cs)
            for j, k in enumerate(wrt_rows):
                xs2[k] = d[j]
            for j, k in enumerate(wrt_consts):
                cs2[k] = d[len(wrt_rows) + j]
            return tuple(fn(*xs2, *cs2))

        prim = [xs[k] for k in wrt_rows] + [cs[k] for k in wrt_consts]
        outs, vjp = jax.vjp(g, *prim)
        ct = []
        pos = 0
        for o, lst in zip(outs, cots):
            if unit_cot:
                ct.append(jnp.ones_like(o))
                continue
            acc = jnp.zeros_like(o)
            for _ in lst:
                acc = acc + cvals[pos]
                pos += 1
            ct.append(acc)
        grads = vjp(tuple(ct))
        for j in range(len(wrt_rows)):
            orefs[j][...] = grads[j].astype(orefs[j].dtype)

        @pl.when(pl.program_id(0) == 0)
        def _():
            for j in range(len(wrt_consts)):
                oref = orefs[len(wrt_rows) + j]
                oref[...] = jnp.zeros_like(oref)

        for j in range(len(wrt_consts)):
            orefs[len(wrt_rows) + j][...] += grads[len(wrt_rows) + j]
        after()

    in_specs = ([_row_spec(br, w, cb) for (_, w, cb) in rows] + [_const_spec(c.shape) for c in consts]
                + [_row_spec(br, w, cb) for (_, w, cb) in flat_cots] + [_HBM] * ncomm)
    out_specs = ([_row_spec(br, rows[k][1], 0) for k in wrt_rows] + [_const_spec(consts[k].shape) for k in wrt_consts]
                 + [_HBM] * ncomm)
    out_shape = ([jax.ShapeDtypeStruct((s, rows[k][1]), dt) for k, dt in zip(wrt_rows, drow_dtypes)]
                 + [jax.ShapeDtypeStruct(consts[k].shape, F32) for k in wrt_consts] + comm.out_shape)
    return _pcall(
        body, name=name, grid=(s // br,), in_specs=in_specs, out_specs=out_specs, out_shape=out_shape,
        scratch_shapes=comm.sems, compiler_params=_cparams(("arbitrary",)),
    )(*[r[0] for r in rows], *consts, *[c[0] for c in flat_cots], *comm.ins)


def _rms(x, w):
    return x * lax.rsqrt(jnp.mean(x * x, axis=-1, keepdims=True) + RMS_EPS) * w


def _f_pre(x, nw, sc, sh):
    return _rms(x, nw) * (1.0 + sc) + sh, x


def _f_pre2(x, o, gt, nw, sc, sh):
    x1 = x + gt * o
    return x1, _rms(x1, nw) * (1.0 + sc) + sh


def _f_fin(x1, f, tgt, gt, nfw):
    y = _rms(x1 + gt * f, nfw)
    return (0.5 * jnp.mean(jnp.square(y - tgt), axis=-1, keepdims=True),)


def _f_comb(o1, o2, o3, l1, l2, l3):
    m = lax.stop_gradient(jnp.maximum(jnp.maximum(l1, l2), l3))
    e1, e2, e3 = jnp.exp(l1 - m), jnp.exp(l2 - m), jnp.exp(l3 - m)
    return ((e1 * o1 + e2 * o2 + e3 * o3) / (e1 + e2 + e3),)


def _f_rwpre(zs, w0, a0, k_k, k_a, wl, e, et):
    r, k, v, zl = zs[:, 0:D], zs[:, D:2 * D], zs[:, 2 * D:3 * D], zs[:, 3 * D:N_RWP]
    lane = lax.broadcasted_iota(jnp.int32, zl.shape, 1)
    t = jnp.where(lane < 64, jnp.tanh(zl), jnp.where(lane < 128, zl, jnp.where(lane < 288, jax.nn.sigmoid(zl), 0.0)))
    lo = _nn(t[:, 0:128], wl[0:128, 0:2 * D])
    g = _nn(t[:, 128:N_LORA], wl[128:N_LORA, 2 * D:3 * D])
    lw = -math.exp(-0.5) * jax.nn.sigmoid(w0 + lo[:, 0:D])
    a = jax.nn.sigmoid(a0 + lo[:, D:2 * D])
    k_mod = k * (1.0 + (a - 1.0) * k_a)
    kk = k * k_k
    kk = kk / jnp.maximum(jnp.sqrt(_hsum(kk * kk, e, et)), 1e-12)
    return r, lw, k_mod, v, -kk, kk * a, g


def _f_rwpost(y, r, v, k_mod, g, lnx_w, lnx_b, r_k, e, et):
    mean = _hsum(y, e, et) * (1.0 / 64)
    yc = y - mean
    var = _hsum(yc * yc, e, et) * (1.0 / 64)
    yn = yc * lax.rsqrt(var + GN_EPS) * lnx_w + lnx_b
    bonus = _hsum(r * k_mod * r_k, e, et) * v
    return ((yn + bonus) * g,)


def _f_mix(gi, ya, yr, bg):
    gate = jax.nn.sigmoid(gi + bg)
    return (gate[:, 0:D] * ya + gate[:, D:2 * D] * yr,)


def _f_adamw(w, g, m, v):
    m = ADAM_B1 * m + (1.0 - ADAM_B1) * g
    v = ADAM_B2 * v + (1.0 - ADAM_B2) * jnp.square(g)
    m_hat = m / (1.0 - ADAM_B1 ** ADAM_STEP)
    v_hat = v / (1.0 - ADAM_B2 ** ADAM_STEP)
    return -ADAM_LR * (m_hat / (jnp.sqrt(v_hat) + ADAM_EPS) + ADAM_WD * w), m, v


def _down(x, k):
    row = lax.broadcasted_iota(jnp.int32, x.shape, 0)
    return jnp.where(row < k, 0.0, pltpu.roll(x, k, 0))


def _up(x, k):
    n = x.shape[0]
    row = lax.broadcasted_iota(jnp.int32, x.shape, 0)
    return jnp.where(row >= n - k, 0.0, pltpu.roll(x, n - k, 0))


def _col_spec(s, w, off=0):
    return pl.BlockSpec((s, w), lambda j: (0, j + off))


def _shift_bwd(z, mu, dzs):
    s, n = z.shape

    def body(z_ref, mu_ref, d_ref, dz_ref, dmu_ref):
        zz, d, m = z_ref[...].astype(F32), d_ref[...], mu_ref[...]
        dm = d * m
        dz_ref[...] = (d - dm + _up(dm, 1)).astype(dz_ref.dtype)
        dmu_ref[...] = jnp.sum(d * (_down(zz, 1) - zz), axis=0, keepdims=True)

    return _pcall(
        body, name="shift_bwd", grid=(n // 128,), in_specs=[_col_spec(s, 128), _col_spec(1, 128), _col_spec(s, 128)],
        out_specs=[_col_spec(s, 128), _col_spec(1, 128)],
        out_shape=[jax.ShapeDtypeStruct((s, n), BF16), jax.ShapeDtypeStruct((1, n), F32)],
        compiler_params=_cparams(("parallel",)),
    )(z, mu, dzs)


def _conv3(x, w_ref, b_ref):
    return b_ref[...] + w_ref[0:1, :] * _down(x, 2) + w_ref[1:2, :] * _down(x, 1) + w_ref[2:3, :] * x


def _conv_fwd(u, cw, cb):
    s = u.shape[0]
    nb = D_FF // 128

    def body(ug_ref, uv_ref, wg_ref, wv_ref, bg_ref, bv_ref, o_ref):
        gate = _conv3(ug_ref[...], wg_ref, bg_ref)
        val = _conv3(uv_ref[...], wv_ref, bv_ref)
        o_ref[...] = (gate * jax.nn.sigmoid(gate) * val).astype(o_ref.dtype)

    return _pcall(
        body, name="conv_fwd", grid=(nb,),
        in_specs=[_col_spec(s, 128), _col_spec(s, 128, nb), _col_spec(3, 128), _col_spec(3, 128, nb),
                  _col_spec(1, 128), _col_spec(1, 128, nb)],
        out_specs=_col_spec(s, 128), out_shape=jax.ShapeDtypeStruct((s, D_FF), BF16),
        compiler_params=_cparams(("parallel",)),
    )(u, u, cw, cw, cb, cb)


def _conv_bwd(u, cw, cb, dact):
    s = u.shape[0]
    nb = D_FF // 128

    def half(x, d, w_ref, du_ref, dw_ref, db_ref):
        x1, x2 = _down(x, 1), _down(x, 2)
        du_ref[...] = (w_ref[2:3, :] * d + w_ref[1:2, :] * _up(d, 1) + w_ref[0:1, :] * _up(d, 2)).astype(du_ref.dtype)
        dw_ref[0:1, :] = jnp.sum(d * x2, axis=0, keepdims=True)
        dw_ref[1:2, :] = jnp.sum(d * x1, axis=0, keepdims=True)
        dw_ref[2:3, :] = jnp.sum(d * x, axis=0, keepdims=True)
        db_ref[...] = jnp.sum(d, axis=0, keepdims=True)

    def body(ug_ref, uv_ref, wg_ref, wv_ref, bg_ref, bv_ref, da_ref,
             dug_ref, duv_ref, dwg_ref, dwv_ref, dbg_ref, dbv_ref):
        ug, uv, da = ug_ref[...], uv_ref[...], da_ref[...]
        gate = _conv3(ug, wg_ref, bg_ref)
        val = _conv3(uv, wv_ref, bv_ref)
        sg = jax.nn.sigmoid(gate)
        dgate = da * val * sg * (1.0 + gate * (1.0 - sg))
        dval = da * gate * sg
        half(ug, dgate, wg_ref, dug_ref, dwg_ref, dbg_ref)
        half(uv, dval, wv_ref, duv_ref, dwv_ref, dbv_ref)

    dug, duv, dwg, dwv, dbg, dbv = _pcall(
        body, name="conv_bwd", grid=(nb,),
        in_specs=[_col_spec(s, 128), _col_spec(s, 128, nb), _col_spec(3, 128), _col_spec(3, 128, nb),
                  _col_spec(1, 128), _col_spec(1, 128, nb), _col_spec(s, 128)],
        out_specs=[_col_spec(s, 128), _col_spec(s, 128), _col_spec(3, 128), _col_spec(3, 128),
                   _col_spec(1, 128), _col_spec(1, 128)],
        out_shape=[jax.ShapeDtypeStruct((s, D_FF), BF16), jax.ShapeDtypeStruct((s, D_FF), BF16),
                   jax.ShapeDtypeStruct((3, D_FF), F32), jax.ShapeDtypeStruct((3, D_FF), F32),
                   jax.ShapeDtypeStruct((1, D_FF), F32), jax.ShapeDtypeStruct((1, D_FF), F32)],
        compiler_params=_cparams(("parallel",)),
    )(u, u, cw, cw, cb, cb, dact)
    return (jnp.concatenate([dug, duv], axis=1), jnp.concatenate([dwg, dwv], axis=1),
            jnp.concatenate([dbg, dbv], axis=1))


ATT_BATCH = 4


def _att_batch(q, kp, kc, vp, vc, first):
    ma = lax.broadcasted_iota(jnp.int32, (1, ATT_BLOCK, 128), 2) < 64

    def diag(x):
        return jnp.concatenate([jnp.where(ma, x, 0.0), jnp.where(ma, 0.0, x)], axis=1)

    qi = lax.broadcasted_iota(jnp.int32, (1, ATT_BLOCK, 2 * ATT_BLOCK), 1)
    kj = lax.broadcasted_iota(jnp.int32, (1, ATT_BLOCK, 2 * ATT_BLOCK), 2) & (ATT_BLOCK - 1)
    okp = kj >= qi + jnp.where(first, 2 * ATT_BLOCK, 0)
    okc = kj <= qi
    sp = jnp.where(okp, _bnt(q, diag(kp)) * 0.125, NEG)
    sc = jnp.where(okc, _bnt(q, diag(kc)) * 0.125, NEG)

    def per_head(fn, x):
        return fn(x[..., :ATT_BLOCK]), fn(x[..., ATT_BLOCK:])

    def spread(ab):
        return jnp.concatenate([jnp.broadcast_to(t, t.shape[:2] + (ATT_BLOCK,)) for t in ab], axis=-1)

    row_max = functools.partial(jnp.max, axis=-1, keepdims=True)
    row_sum = functools.partial(jnp.sum, axis=-1, keepdims=True)
    m = [lax.stop_gradient(jnp.maximum(a, b)) for a, b in zip(per_head(row_max, sp), per_head(row_max, sc))]
    pp, pc = jnp.exp(sp - spread(m)), jnp.exp(sc - spread(m))
    den = [a + b for a, b in zip(per_head(row_sum, pp), per_head(row_sum, pc))]
    num = _bnn(pp, diag(vp)) + _bnn(pc, diag(vc))
    out = num / jnp.where(ma, den[0], den[1])
    lse = jnp.where(ma, m[0] + jnp.log(den[0]), m[1] + jnp.log(den[1]))
    return out, jnp.broadcast_to(lse, out.shape)


def _att_pairs_per_step(dil):
    return 4 if dil == 1 else 1


def _att_residues(dil):
    return min(dil, ATT_BATCH // _att_pairs_per_step(dil))


def _att_specs(g, dil):
    rows, pp = ATT_BLOCK * dil, _att_pairs_per_step(dil)

    def cur(slot):
        return pl.BlockSpec((rows, 128 * pp), lambda n, p: (n, (g * 3 + slot) * (4 // pp) + p))

    def prev(slot):
        return pl.BlockSpec((rows, 128 * pp), lambda n, p: (jnp.maximum(n - 1, 0), (g * 3 + slot) * (4 // pp) + p))

    return [cur(0), prev(1), cur(1), prev(2), cur(2)]


def _att_out_spec(dil):
    return pl.BlockSpec((ATT_BLOCK * dil, 128 * _att_pairs_per_step(dil)), lambda n, p: (n, p))


def _att_grid(s, dil):
    return (s // (ATT_BLOCK * dil), 4 // _att_pairs_per_step(dil))


def _att_windows(i, dil):
    res = _att_residues(dil)

    def rows(r):
        return pl.ds(i * res + r, ATT_BLOCK, stride=dil) if dil > 1 else pl.ds(0, ATT_BLOCK)

    return [(rows(r), pl.ds(128 * j, 128)) for j in range(_att_pairs_per_step(dil)) for r in range(res)]


def _att_fwd(att_in, g, dil):
    s = att_in.shape[0]

    def body(q_ref, kp_ref, kc_ref, vp_ref, vc_ref, o_ref, l_ref):
        first = pl.program_id(0) == 0

        def one(i, carry):
            win = _att_windows(i, dil)
            vals = [jnp.stack([ref[w] for w in win]) for ref in (q_ref, kp_ref, kc_ref, vp_ref, vc_ref)]
            o, l = _att_batch(*vals, first)
            for j, w in enumerate(win):
                o_ref[w] = o[j]
                l_ref[w] = l[j]
            return carry

        lax.fori_loop(0, dil // _att_residues(dil), one, 0)

    return _pcall(
        body, name=f"att_fwd{g}", grid=_att_grid(s, dil), in_specs=_att_specs(g, dil),
        out_specs=[_att_out_spec(dil)] * 2, out_shape=[jax.ShapeDtypeStruct((s, ATT_WIDTH), F32)] * 2,
        compiler_params=_cparams(("parallel", "parallel")),
    )(att_in, att_in, att_in, att_in, att_in)


def _att_bwd(att_in, g, dil, do, dl, acc):
    s = att_in.shape[0]

    def body(q_ref, kp_ref, kc_ref, vp_ref, vc_ref, do_ref, dl_ref, dq_ref, dkp_ref, dkc_ref, dvp_ref, dvc_ref):
        first = pl.program_id(0) == 0

        def one(i, carry):
            win = _att_windows(i, dil)
            vals = [jnp.stack([ref[w] for w in win]) for ref in (q_ref, kp_ref, kc_ref, vp_ref, vc_ref)]
            _, vjp = jax.vjp(lambda *a: _att_batch(*a, first), *vals)
            grads = vjp((jnp.stack([do_ref[w] for w in win]), jnp.stack([dl_ref[w] for w in win])))
            for ref, gr in zip((dq_ref, dkp_ref, dkc_ref, dvp_ref, dvc_ref), grads):
                for j, w in enumerate(win):
                    ref[w] = gr[j]
            return carry

        lax.fori_loop(0, dil // _att_residues(dil), one, 0)

    dq, dkp, dkc, dvp, dvc = _pcall(
        body, name=f"att_bwd{g}", grid=_att_grid(s, dil), in_specs=_att_specs(g, dil) + [_att_out_spec(dil)] * 2,
        out_specs=[_att_out_spec(dil)] * 5, out_shape=[jax.ShapeDtypeStruct((s, ATT_WIDTH), F32)] * 5,
        compiler_params=_cparams(("parallel", "parallel")),
    )(att_in, att_in, att_in, att_in, att_in, do, dl)

    unit, rb = ATT_BLOCK * dil, 1024
    steps = s // rb
    within = unit < rb

    def shifted(cur_ref, next_ref, has_next):
        nxt = jnp.where(has_next, next_ref[...], 0.0)
        return jnp.concatenate([cur_ref[unit:, :], nxt], axis=0) if within else nxt

    def cbody(dq_ref, dkc_ref, dkp_ref, dkn_ref, dvc_ref, dvp_ref, dvn_ref, *rest):
        o_ref = rest[-1]
        has_next = pl.program_id(0) + (1 if within else unit // rb) < steps
        o_ref[:, 0:ATT_WIDTH] = dq_ref[...].astype(BF16)
        o_ref[:, ATT_WIDTH:2 * ATT_WIDTH] = (dkc_ref[...] + shifted(dkp_ref, dkn_ref, has_next)).astype(BF16)
        o_ref[:, 2 * ATT_WIDTH:3 * ATT_WIDTH] = (dvc_ref[...] + shifted(dvp_ref, dvn_ref, has_next)).astype(BF16)

    cur = pl.BlockSpec((rb, ATT_WIDTH), lambda i: (i, 0))
    if within:
        nxt = pl.BlockSpec((unit, ATT_WIDTH), lambda i: (jnp.minimum((i + 1) * (rb // unit), s // unit - 1), 0))
    else:
        nxt = pl.BlockSpec((rb, ATT_WIDTH), lambda i: (jnp.minimum(i + unit // rb, steps - 1), 0))
    carried = [] if acc is None else [acc]
    return _pcall(
        cbody, name=f"att_bwd_sum{g}", grid=(steps,),
        in_specs=[cur, cur, cur, nxt, cur, cur, nxt] + [pl.BlockSpec(memory_space=pl.ANY)] * len(carried),
        out_specs=pl.BlockSpec((rb, 3 * ATT_WIDTH), lambda i: (i, g)),
        out_shape=jax.ShapeDtypeStruct((s, N_ATT), BF16), input_output_aliases={7: 0} if carried else {},
        compiler_params=_cparams(("parallel",)),
    )(dq, dkc, dkp, dkp, dvc, dvp, dvp, *carried)


def _cumsum_rows_impl(x):
    row = lax.broadcasted_iota(jnp.int32, x.shape, 0)
    shift = 1
    while shift < x.shape[0]:
        x = x + jnp.where(row >= shift, pltpu.roll(x, shift, 0), 0.0)
        shift *= 2
    return x


@jax.custom_vjp
def _cumsum_rows(x):
    return _cumsum_rows_impl(x)


_cumsum_rows.defvjp(lambda x: (_cumsum_rows_impl(x), None),
                    lambda _, g: (jnp.sum(g, axis=0, keepdims=True) - _cumsum_rows_impl(g) + g,))


def _unit_lower_inverse_impl(n):
    eye = (lax.broadcasted_iota(jnp.int32, (1,) + n.shape[1:], 1)
           == lax.broadcasted_iota(jnp.int32, (1,) + n.shape[1:], 2))
    t = jnp.where(eye, 1.0, 0.0) + n
    pw = n
    for _ in range(5):
        pw = _bnn(pw, pw)
        t = t + _bnn(t, pw)
    return t


@jax.custom_vjp
def _unit_lower_inverse(n):
    return _unit_lower_inverse_impl(n)


def _unit_lower_inverse_fwd(n):
    t = _unit_lower_inverse_impl(n)
    return t, t


_unit_lower_inverse.defvjp(_unit_lower_inverse_fwd, lambda t, g: (_bnt(_btn(t, g), t),))


@jax.custom_vjp
def _known_inverse(n, t):
    return t


_known_inverse.defvjp(lambda n, t: (t, t), lambda t, g: (_bnt(_btn(t, g), t), jnp.zeros_like(t)))


def _scan_chunk(r, lw, k, v, a, b, s0, inverse):
    c = SCAN_CHUNK
    p = s0.shape[0]
    cum = _cumsum_rows(lw)
    tot = jnp.sum(lw, axis=0, keepdims=True)
    ma = (lax.broadcasted_iota(jnp.int32, (c, 128 * p), 1) & 127) < 64

    def pairs(x):
        return jnp.concatenate([x[None, :, 128 * j:128 * (j + 1)] for j in range(p)], axis=0)

    def stack(x):
        return jnp.concatenate([pairs(jnp.where(ma, x, 0.0)), pairs(jnp.where(ma, 0.0, x))], axis=1)

    einv, eend = jnp.exp(-cum), jnp.exp(tot - cum)
    ra, aa = stack(r * jnp.exp(cum)), stack(a * jnp.exp(cum - lw))
    bi, ki, be, ke, vs = stack(b * einv), stack(k * einv), stack(b * eend), stack(k * eend), stack(v)
    r2 = lax.broadcasted_iota(jnp.int32, (1, 2 * c, 2 * c), 1)
    c2 = lax.broadcasted_iota(jnp.int32, (1, 2 * c, 2 * c), 2)
    same = (r2 >= c) == (c2 >= c)
    strict = jnp.logical_and(same, c2 < r2)
    incl = jnp.logical_and(same, c2 <= r2)
    s0 = jnp.where(same, s0, 0.0)
    prod = _bnt(jnp.concatenate([aa, ra], axis=1), jnp.concatenate([bi, ki], axis=1))
    a_ab = jnp.where(strict, prod[:, :2 * c, :2 * c], 0.0)
    a_ak = jnp.where(strict, prod[:, :2 * c, 2 * c:], 0.0)
    a_rb = jnp.where(incl, prod[:, 2 * c:, :2 * c], 0.0)
    a_rk = jnp.where(incl, prod[:, 2 * c:, 2 * c:], 0.0)
    t = inverse(a_ab)
    u = _bnn(t, _bnt(aa, s0) + _bnn(a_ak, vs))
    uv = jnp.concatenate([u, vs], axis=1)
    ys = _bnt(ra, s0) + _bnn(jnp.concatenate([a_rb, a_rk], axis=2), uv)
    s1 = s0 * pairs(jnp.exp(tot)) + _btn(uv, jnp.concatenate([be, ke], axis=1))
    y3 = ys[:, :c] + ys[:, c:]
    return (jnp.concatenate([y3[j] for j in range(p)], axis=1), s1), t


def _scan_specs(rev, n):
    def at(i):
        return n - 1 - i if rev else i

    def cm(cb):
        return pl.BlockSpec((SCAN_CHUNK, D), lambda i: (at(i), cb))

    return cm, pl.BlockSpec((1, SCAN_PAIRS, 128, 128), lambda i: (at(i), 0, 0, 0))


def _comm_phases(comm, refs, n, step=None):
    k = comm.n
    srcs, outs, sems = refs[:k], refs[k:2 * k], refs[2 * k:]
    i = pl.program_id(0) if step is None else step

    def before():
        @pl.when(i == 0)
        def _():
            comm.first(srcs, outs, sems)

    def after():
        if comm.mid is not None:
            @pl.when(i == (3 * n) // 4)
            def _():
                comm.mid(srcs, outs, sems)

        @pl.when(i == n - 1)
        def _():
            comm.last(srcs, outs, sems)

    return before, after


def _scan_fwd(zs, lw, km, aa, bb, comm):
    s = zs.shape[0]
    n = s // SCAN_CHUNK
    cm, st = _scan_specs(False, n)
    k = comm.n

    def body(*refs):
        r_ref, lw_ref, k_ref, v_ref, a_ref, b_ref = refs[:6]
        y_ref, s0_ref, t_ref = refs[6 + k:9 + k]
        state = refs[9 + 2 * k]
        before, after = _comm_phases(comm, refs[6:6 + k] + refs[9 + k:9 + 2 * k] + refs[10 + 2 * k:], n)
        before()

        @pl.when(pl.program_id(0) == 0)
        def _():
            state[...] = jnp.zeros_like(state)

        s0 = state[...]
        s0_ref[0] = s0
        (y, s1), t = _scan_chunk(*[ref[...] for ref in (r_ref, lw_ref, k_ref, v_ref, a_ref, b_ref)], s0,
                                 _unit_lower_inverse)
        y_ref[...] = y
        t_ref[0] = t.astype(BF16)
        state[...] = s1
        after()

    per_chunk = (n, SCAN_PAIRS, 128, 128)
    res = _pcall(
        body, name="scan_fwd", grid=(n,), in_specs=[cm(0), cm(0), cm(0), cm(2), cm(0), cm(0)] + [_HBM] * k,
        out_specs=[cm(0), st, st] + [_HBM] * k,
        out_shape=[jax.ShapeDtypeStruct((s, D), F32), jax.ShapeDtypeStruct(per_chunk, F32),
                   jax.ShapeDtypeStruct(per_chunk, BF16)] + comm.out_shape,
        scratch_shapes=[pltpu.VMEM((SCAN_PAIRS, 128, 128), F32)] + comm.sems,
        compiler_params=_cparams(("arbitrary",)),
    )(zs, lw, km, zs, aa, bb, *comm.ins)
    return res[0], res[1], res[2], res[3:]


def _scan_bwd(zs, lw, km, aa, bb, s0s, ts, dy, comm):
    s = zs.shape[0]
    n = s // SCAN_CHUNK
    cm, st = _scan_specs(True, n)
    k = comm.n

    def body(*refs):
        r_ref, lw_ref, k_ref, v_ref, a_ref, b_ref, s0_ref, t_ref, dy_ref = refs[:9]
        douts = refs[9 + k:15 + k]
        dstate = refs[15 + 2 * k]
        before, after = _comm_phases(comm, refs[9:9 + k] + refs[15 + k:15 + 2 * k] + refs[16 + 2 * k:], n)
        before()

        @pl.when(pl.program_id(0) == 0)
        def _():
            dstate[...] = jnp.zeros_like(dstate)

        t = t_ref[0].astype(F32)
        prim = [ref[...] for ref in (r_ref, lw_ref, k_ref, v_ref, a_ref, b_ref)] + [s0_ref[0]]
        _, vjp, _ = jax.vjp(lambda *p: _scan_chunk(*p, lambda nil: _known_inverse(nil, t)), *prim, has_aux=True)
        grads = vjp((dy_ref[...], dstate[...]))
        for ref, gr in zip(douts, grads[:6]):
            ref[...] = gr
        dstate[...] = grads[6]
        after()

    res = _pcall(
        body, name="scan_bwd", grid=(n,),
        in_specs=[cm(0), cm(0), cm(0), cm(2), cm(0), cm(0), st, st, cm(0)] + [_HBM] * k,
        out_specs=[cm(0)] * 6 + [_HBM] * k, out_shape=[jax.ShapeDtypeStruct((s, D), F32)] * 6 + comm.out_shape,
        scratch_shapes=[pltpu.VMEM((SCAN_PAIRS, 128, 128), F32)] + comm.sems,
        compiler_params=_cparams(("arbitrary",)),
    )(zs, lw, km, zs, aa, bb, s0s, ts, dy, *comm.ins)
    return res[:6], res[6:]


_HBM = pl.BlockSpec(memory_space=pltpu.HBM)


def _me():
    return lax.axis_index("x"), lax.axis_index("y"), lax.axis_index("c")


def _allgather8(src, name):
    def body(src_ref, out_ref, ssem, rsem, lsem):
        x, y, c = _me()
        me = 4 * x + 2 * y + c
        local = pltpu.make_async_copy(src_ref, out_ref.at[me], lsem)
        local.start()
        peers = []
        for k in range(1, 8):
            peers.append(((1 - x) if k & 4 else x, (1 - y) if k & 2 else y, (1 - c) if k & 1 else c))
        sends = []
        for k, peer in enumerate(peers):
            cp = pltpu.make_async_remote_copy(src_ref, out_ref.at[me], ssem.at[k], rsem.at[k], device_id=peer,
                                              device_id_type=MESH)
            cp.start()
            sends.append(cp)
        for k, (px, py, pc) in enumerate(peers):
            pltpu.make_async_remote_copy(src_ref, out_ref.at[4 * px + 2 * py + pc], ssem.at[k], rsem.at[k],
                                         device_id=(px, py, pc), device_id_type=MESH).wait_recv()
        for cp in sends:
            cp.wait_send()
        local.wait()

    return _pcall(
        body, name=name, in_specs=[_HBM], out_specs=_HBM, out_shape=jax.ShapeDtypeStruct((8,) + src.shape, src.dtype),
        scratch_shapes=[pltpu.SemaphoreType.DMA((7,)), pltpu.SemaphoreType.DMA((7,)), pltpu.SemaphoreType.DMA],
    )(src)


def _other_chips(x, y):
    return [(1 - x, y), (x, 1 - y), (1 - x, 1 - y)]


def _remote(src, dst, ssem, rsem, to):
    return pltpu.make_async_remote_copy(src, dst, ssem, rsem, device_id=to, device_id_type=MESH)


class _GatherWeights:
    def __init__(self, shards):
        self.ins = list(shards)
        n = self.n = len(shards)
        self.out_shape = [jax.ShapeDtypeStruct((4,) + t.shape, t.dtype) for t in shards]
        self.sems = [pltpu.SemaphoreType.DMA((6 * n,)), pltpu.SemaphoreType.DMA((6 * n,)),
                     pltpu.SemaphoreType.DMA((n,)), pltpu.SemaphoreType.DMA((n,))]

    def _copies(self, srcs, outs, sems):
        ssem, rsem, lsem, osem = sems
        x, y, c = _me()
        me = 2 * x + y
        own, ici, landed, passed, passed_in = [], [], [], [], []
        for a in range(self.n):
            h = self.ins[a].shape[0] // 2
            mine, other = pl.ds(c * h, h), pl.ds((1 - c) * h, h)
            own.append(_remote(srcs[a], outs[a].at[me], lsem.at[a], osem.at[a], (x, y, 1 - c)))
            for k, (px, py) in enumerate(_other_chips(x, y)):
                s1, r1, s2, r2 = ssem.at[6 * a + k], rsem.at[6 * a + k], ssem.at[6 * a + 3 + k], rsem.at[6 * a + 3 + k]
                got, got_sib = outs[a].at[2 * px + py, mine], outs[a].at[2 * px + py, other]
                ici.append(_remote(srcs[a].at[mine], outs[a].at[me, mine], s1, r1, (px, py, c)))
                landed.append(_remote(got, got, s1, r1, (px, py, c)))
                passed.append(_remote(got, got, s2, r2, (x, y, 1 - c)))
                passed_in.append(_remote(got_sib, got_sib, s2, r2, (x, y, 1 - c)))
        return own, ici, landed, passed, passed_in

    def first(self, srcs, outs, sems):
        own, ici, _, _, _ = self._copies(srcs, outs, sems)
        for cp in own + ici:
            cp.start()

    def mid(self, srcs, outs, sems):
        _, _, landed, passed, _ = self._copies(srcs, outs, sems)
        for arrived, onward in zip(landed, passed):
            arrived.wait_recv()
            onward.start()

    def last(self, srcs, outs, sems):
        own, ici, _, passed, passed_in = self._copies(srcs, outs, sems)
        for cp in passed_in:
            cp.wait_recv()
        for cp in ici + passed:
            cp.wait_send()
        for cp in own:
            cp.wait()


class _ScatterToChips:
    def __init__(self, parts):
        self.ins = list(parts)
        n = self.n = len(parts)
        self.out_shape = [jax.ShapeDtypeStruct(t.shape, t.dtype) for t in parts]
        self.sems = [pltpu.SemaphoreType.DMA((3 * n,)), pltpu.SemaphoreType.DMA((3 * n,)), pltpu.SemaphoreType.DMA((n,))]

    def _copies(self, srcs, outs, sems):
        ssem, rsem, lsem = sems
        x, y, c = _me()
        me = 2 * x + y
        own, out, landed = [], [], []
        for a in range(self.n):
            own.append(pltpu.make_async_copy(srcs[a].at[me], outs[a].at[me], lsem.at[a]))
            for k, (px, py) in enumerate(_other_chips(x, y)):
                dst = outs[a].at[2 * px + py]
                out.append(_remote(srcs[a].at[2 * px + py], outs[a].at[me], ssem.at[3 * a + k], rsem.at[3 * a + k],
                                   (px, py, c)))
                landed.append(_remote(dst, dst, ssem.at[3 * a + k], rsem.at[3 * a + k], (px, py, c)))
        return own, out, landed

    def first(self, srcs, outs, sems):
        own, out, _ = self._copies(srcs, outs, sems)
        for cp in own + out:
            cp.start()

    mid = None

    def last(self, srcs, outs, sems):
        own, out, landed = self._copies(srcs, outs, sems)
        for cp in landed:
            cp.wait_recv()
        for cp in own:
            cp.wait()
        for cp in out:
            cp.wait_send()


def _run_comm(comm, name):
    n = comm.n

    def body(*refs):
        srcs, outs, sems = refs[:n], refs[n:2 * n], refs[2 * n:]
        comm.first(srcs, outs, sems)
        if comm.mid is not None:
            comm.mid(srcs, outs, sems)
        comm.last(srcs, outs, sems)

    return _pcall(body, name=name, in_specs=[_HBM] * n, out_specs=[_HBM] * n, out_shape=comm.out_shape,
                  scratch_shapes=comm.sems)(*comm.ins)


class _NoComm:
    n, ins, out_shape, sems, mid = 0, [], [], [], None

    def first(self, srcs, outs, sems):
        pass

    def last(self, srcs, outs, sems):
        pass


_NOTHING = _NoComm()


class _SiblingHalves:
    mid = None

    def __init__(self, grads):
        self.ins = list(grads)
        n = self.n = len(grads)
        self.out_shape = [jax.ShapeDtypeStruct((4, t.shape[1] // 2, t.shape[2]), t.dtype) for t in grads]
        self.sems = [pltpu.SemaphoreType.DMA((n,)), pltpu.SemaphoreType.DMA((n,))]

    def _copies(self, srcs, outs, sems):
        ssem, rsem = sems
        x, y, c = _me()
        copies = []
        for a in range(self.n):
            h = self.ins[a].shape[1] // 2
            copies.append(_remote(srcs[a].at[:, pl.ds((1 - c) * h, h)], outs[a], ssem.at[a], rsem.at[a], (x, y, 1 - c)))
        return copies

    def first(self, srcs, outs, sems):
        for cp in self._copies(srcs, outs, sems):
            cp.start()

    def last(self, srcs, outs, sems):
        for cp in self._copies(srcs, outs, sems):
            cp.wait()


def _reduce_finish(reds, name):
    n = len(reds)

    def body(*refs):
        outs = refs[n:2 * n]
        ssem, rsem = refs[2 * n:]
        x, y, c = _me()
        copies = []
        for a in range(n):
            h = reds[a].shape[0] // 2
            mine = outs[a].at[pl.ds(c * h, h)]
            copies.append(_remote(mine, mine, ssem.at[a], rsem.at[a], (x, y, 1 - c)))
        for cp in copies:
            cp.start()
        for a in range(n):
            h = reds[a].shape[0] // 2
            dst = outs[a].at[pl.ds((1 - c) * h, h)]
            _remote(dst, dst, ssem.at[a], rsem.at[a], (x, y, 1 - c)).wait_recv()
        for cp in copies:
            cp.wait_send()

    return _pcall(
        body, name=name, in_specs=[_HBM] * n, out_specs=[_HBM] * n,
        out_shape=[jax.ShapeDtypeStruct(t.shape, t.dtype) for t in reds],
        input_output_aliases={a: a for a in range(n)},
        scratch_shapes=[pltpu.SemaphoreType.DMA((n,)), pltpu.SemaphoreType.DMA((n,))],
    )(*reds)


def _half_sum(fn, full, halves, out_full, out_dtype, core, name):
    p, h, c = (halves[0].shape if halves else (full[0].shape[0], full[0].shape[1] // 2, full[0].shape[2]))
    br = _div(h, max(16, (1 << 19) // (p * c)), 16)
    nb = h // br
    mine3 = pl.BlockSpec((p, br, c), lambda i, core_ref: (0, core_ref[0] * nb + i, 0))
    half3 = pl.BlockSpec((p, br, c), lambda i, core_ref: (0, i, 0))

    def body(core_ref, *refs):
        refs[-1][...] = fn(*[t[...].astype(F32) for t in refs[:-1]]).astype(out_dtype)

    if out_full:
        out_spec = pl.BlockSpec((br, c), lambda i, core_ref: (core_ref[0] * nb + i, 0))
        out_shape = jax.ShapeDtypeStruct((2 * h, c), out_dtype)
    else:
        out_spec, out_shape = half3, jax.ShapeDtypeStruct((p, h, c), out_dtype)
    return _pcall(
        body, name=name,
        grid_spec=pltpu.PrefetchScalarGridSpec(
            num_scalar_prefetch=1, grid=(nb,), in_specs=[mine3] * len(full) + [half3] * len(halves),
            out_specs=out_spec),
        out_shape=out_shape, compiler_params=_cparams(("parallel",)),
    )(core, *full, *halves)


def _ada_fwd(c_all, w, b):
    def body(c_ref, w_ref, b_ref, o_ref):
        o_ref[...] = jnp.dot(c_ref[...], w_ref[...], precision=HI, preferred_element_type=F32) + b_ref[...]

    return _pcall(body, name="ada_fwd", out_shape=jax.ShapeDtypeStruct((c_all.shape[0], w.shape[1]), F32),
                  compiler_params=pltpu.CompilerParams(vmem_limit_bytes=VMEM_LIMIT))(c_all, w, b)


def _ada_bwd(c_all_t, d):
    def body(c_ref, d_ref, o_ref):
        o_ref[...] = jnp.dot(c_ref[...], d_ref[...], precision=HI, preferred_element_type=F32)

    return _pcall(body, name="ada_bwd", out_shape=jax.ShapeDtypeStruct((c_all_t.shape[0], d.shape[1]), F32),
                  compiler_params=pltpu.CompilerParams(vmem_limit_bytes=VMEM_LIMIT))(c_all_t, d)


def _sum_lead(x, name):
    p, r, n = x.shape
    br = _div(r, 512, 8)

    def body(x_ref, o_ref):
        acc = x_ref[0]
        for j in range(1, p):
            acc = acc + x_ref[j]
        o_ref[...] = acc

    return _pcall(
        body, name=name, grid=(r // br,), in_specs=[pl.BlockSpec((p, br, n), lambda i: (0, i, 0))],
        out_specs=pl.BlockSpec((br, n), lambda i: (i, 0)), out_shape=jax.ShapeDtypeStruct((r, n), F32),
        compiler_params=_cparams(("parallel",)),
    )(x)


def _adamw(w, g, m, v, name):
    shape = w.shape
    cols = shape[-1]
    w2, g2, m2, v2 = [t.reshape(-1, cols) for t in (w, g, m, v)]
    rows = w2.shape[0]
    pref = max(8, (1 << 19) // cols // 8 * 8)
    br = _div(rows, pref, 8)
    if rows // br > 64:
        br = pref
    outs = _rows_fwd(_f_adamw, [(t, cols, 0) for t in (w2, g2, m2, v2)], [], [(cols, F32)] * 3, name=name, br=br)
    return [o.reshape(shape) for o in outs]


_BIG = (("w_in", 1), ("w_up", 1), ("w_down", 0), ("w_o", 0), ("w_rwkv_out", 0), ("w_att_out", 1), ("w2", 1), ("a2", 1),
        ("g2", 1))


_NEEDED_FIRST = ("w_in", "w_att_out", "w2", "a2", "g2")
_NEEDED_LATER = ("w_up", "w_down", "w_o", "w_rwkv_out")
_DONE_EARLY = ("w_up", "w_down", "w_o", "w_rwkv_out", "w_att_out")
_DONE_LATE = ("w_in", "w2", "a2", "g2")


def _cols_joined(t):
    return jnp.concatenate([t[j] for j in range(4)], axis=1)


def _cols_split(t):
    n = t.shape[1] // 4
    return jnp.stack([t[:, j * n:(j + 1) * n] for j in range(4)])


W_IN_SHARD = (N_ATT + N_RW + N_GATE) // 4
W_IN_PAD = 2560


def _row_window(parts, lo, hi):
    out, pos = [], 0
    for t, w in parts:
        a, b = max(lo, pos), min(hi, pos + w)
        if a < b:
            out.append(t[a - pos:b - pos])
        pos += w
    return out[0] if len(out) == 1 else jnp.concatenate(out, axis=0)


def _rows_joined(t):
    return t.reshape(4 * t.shape[1], t.shape[2])


def _rows_split(t):
    return t.reshape(4, t.shape[0] // 4, t.shape[1])


def _step_to_scan(x, tgt, ada, wts):
    sh1, sc1, gt1, sh2, sc2, gt2 = ada
    br = 256
    grp = (np.arange(D)[:, None] // 64 == np.arange(128)[None, :]).astype(np.float32)
    e, et = jnp.asarray(grp), jnp.asarray(grp.T)
    w_in = [(wts["w_in"][j], W_IN_SHARD) for j in range(4)]
    w_att = _row_window(w_in, 0, N_ATT)
    w_rw = jnp.concatenate([_row_window(w_in, N_ATT, N_ATT + N_RW), jnp.zeros((N_RWP - N_RW, D), BF16)], axis=0)
    w_gate = _row_window(w_in, N_ATT + N_RW, N_ATT + N_RW + N_GATE)
    mu = jnp.pad(wts["mu_shift"], ((0, 0), (0, N_RWP - N_RW)))
    wl = jnp.zeros((N_LORA, 3 * D), F32)
    wl = wl.at[0:64, 0:D].set(_cols_joined(wts["w2"]).astype(F32))
    wl = wl.at[64:128, D:2 * D].set(_cols_joined(wts["a2"]).astype(F32))
    wl = wl.at[128:288, 2 * D:3 * D].set(_cols_joined(wts["g2"]).astype(F32))
    pre1_c = [wts["norm1_w"], sc1, sh1]
    (h1,) = _rows_fwd(_f_pre, [(x, D, 0)], pre1_c, [(D, BF16), None], name="pre1_fwd", br=2 * br)
    att_in = _mm(h1, w_att, tb=True, name="mm_att_in")
    z = _mm(h1, w_rw, tb=True, out_dtype=BF16, name="mm_rw_in")
    gate_in = _mm(h1, w_gate, tb=True, out_dtype=BF16, name="mm_gate_in")
    att_o, att_l = [], []
    for g, (_, dil) in enumerate(ATT_PATTERNS):
        o, l = _att_fwd(att_in, g, dil)
        att_o.append(o)
        att_l.append(l)
    comb_rows = [(t, ATT_WIDTH, 0) for t in att_o + att_l]
    (att,) = _rows_fwd(_f_comb, comb_rows, [], [(ATT_WIDTH, BF16)], name="comb_fwd", br=2 * br)
    w_ao = _cols_joined(wts["w_att_out"])
    y_att = _mm(att, w_ao, out_dtype=BF16, name="mm_att_out")
    rwpre_c = [wts["w0"], wts["a0"], wts["k_k"], wts["k_a"], wl, e, et]

    def shift_and_rwpre(zz, *rest):
        consts, mu_row, before = rest[:-2], rest[-2], rest[-1]
        last = jnp.sum(jnp.where(lax.broadcasted_iota(jnp.int32, before.shape, 0) == HALO - 1, before, 0.0), axis=0,
                       keepdims=True)
        row = lax.broadcasted_iota(jnp.int32, zz.shape, 0)
        zprev = jnp.where(row == 0, last, pltpu.roll(zz, 1, 0))
        shifted = zz + (zprev - zz) * mu_row
        return (shifted,) + tuple(_f_rwpre(shifted, *consts))

    zs, lw, km, aa, bb, gg = _rows_fwd(
        shift_and_rwpre, [(z, N_RWP, 0)], rwpre_c + [mu],
        [(N_RWP, F32), None, (D, F32), (D, F32), None, (D, F32), (D, F32), (D, F32)], name="rwpre_fwd", br=br, halo=0)
    return dict(x=x, tgt=tgt, wts=wts, br=br, e=e, et=et, gt1=gt1, sc2=sc2, sh2=sh2, gt2=gt2, w_att=w_att, w_rw=w_rw,
                w_ao=w_ao,
                w_gate=w_gate, mu=mu, pre1_c=pre1_c, h1=h1, att_in=att_in, z=z, gate_in=gate_in, comb_rows=comb_rows,
                att=att, y_att=y_att, zs=zs, rwpre_c=rwpre_c, lw=lw, km=km, aa=aa, bb=bb, gg=gg)


def _step_between_scans(st, y_raw, late):
    x, tgt, wts, br, e, et = st["x"], st["tgt"], st["wts"], st["br"], st["e"], st["et"]
    zs, km, gg, gate_in, y_att, att = st["zs"], st["km"], st["gg"], st["gate_in"], st["y_att"], st["att"]
    comb_rows, att_in = st["comb_rows"], st["att_in"]
    gt1, sc2, sh2, gt2 = st["gt1"], st["sc2"], st["sh2"], st["gt2"]
    w_up, w_ao = late["w_up"], st["w_ao"]
    w_down, w_o, w_ro = _rows_joined(late["w_down"]), _rows_joined(late["w_o"]), _rows_joined(late["w_rwkv_out"])
    post_rows = [(y_raw, D, 0), (zs, D, 0), (zs, D, 2), (km, D, 0), (gg, D, 0)]
    post_c = [wts["lnx_w"], wts["lnx_b"], wts["r_k"], e, et]
    (rw_out,) = _rows_fwd(_f_rwpost, post_rows, post_c, [(D, BF16)], name="rwpost_fwd", br=br)
    y_rw = _mm(rw_out, w_ro, out_dtype=BF16, name="mm_rw_out")
    mix_rows = [(gate_in, N_GATE, 0), (y_att, D, 0), (y_rw, D, 0)]
    (mix,) = _rows_fwd(_f_mix, mix_rows, [wts["b_gate"]], [(D, BF16)], name="mix_fwd", br=2 * br)
    o = _mm(mix, w_o, out_dtype=BF16, name="mm_o")
    pre2_c = [gt1, wts["norm2_w"], sc2, sh2]
    x1, h2 = _rows_fwd(_f_pre2, [(x, D, 0), (o, D, 0)], pre2_c, [(D, F32), (D, BF16)], name="pre2_fwd", br=2 * br)
    u = _mm(h2, w_up, b_chip=True, name="mm_up")
    act = _conv_fwd(u, wts["conv_w"], wts["conv_b"])
    f = _mm(act, w_down, out_dtype=BF16, name="mm_down")
    fin_rows = [(x1, D, 0), (f, D, 0), (tgt, D, 0)]
    fin_c = [gt2, wts["norm_f_w"]]

    def fin_fwd(*a):
        (l,) = _f_fin(*a)
        return (jnp.broadcast_to(jnp.sum(l, axis=0, keepdims=True), (8, 128)),)

    (loss_acc,) = _rows_fwd(fin_fwd, fin_rows, fin_c, [], name="fin_fwd", br=2 * br, acc_shape=(8, 128))

    gw = {}
    dx1a, df, d_gt2, gw["norm_f_w"] = _rows_bwd(
        _f_fin, fin_rows, fin_c, [[]], wrt_rows=[0, 1], wrt_consts=[0, 1], drow_dtypes=[F32, BF16],
        name="fin_bwd", br=2 * br, unit_cot=True)
    dact = _mm(df, w_down, tb=True, name="mm_dact")
    gw["w_down"] = _rows_split(_mm(act, df, ta=True, out_dtype=BF16, name="mm_dw_down"))
    du, gw["conv_w"], gw["conv_b"] = _conv_bwd(u, wts["conv_w"], wts["conv_b"], dact)
    dh2 = _mm(du, w_up, tb=True, b_chip=True, out_dtype=BF16, name="mm_dh2")
    gw["w_up"] = _mm(h2, du, ta=True, out_chip=True, out_dtype=BF16, name="mm_dw_up")
    dxa, do, d_gt1, gw["norm2_w"], d_sc2, d_sh2 = _rows_bwd(
        _f_pre2, [(x, D, 0), (o, D, 0)], pre2_c, [[(dx1a, D, 0)], [(dh2, D, 0)]], wrt_rows=[0, 1],
        wrt_consts=[0, 1, 2, 3], drow_dtypes=[F32, BF16], name="pre2_bwd", br=2 * br)
    dmix = _mm(do, w_o, tb=True, out_dtype=BF16, name="mm_dmix")
    gw["w_o"] = _rows_split(_mm(mix, do, ta=True, out_dtype=BF16, name="mm_dw_o"))
    dgate, dya, dyr, gw["b_gate"] = _rows_bwd(
        _f_mix, mix_rows, [wts["b_gate"]], [[(dmix, D, 0)]], wrt_rows=[0, 1, 2], wrt_consts=[0],
        drow_dtypes=[BF16] * 3, name="mix_bwd", br=2 * br)
    datt = _mm(dya, w_ao, tb=True, out_dtype=BF16, name="mm_datt")
    gw["w_att_out"] = _mm(att, dya, ta=True, out_chip=True, out_dtype=BF16, name="mm_dw_att_out")
    drw = _mm(dyr, w_ro, tb=True, out_dtype=BF16, name="mm_drw")
    gw["w_rwkv_out"] = _rows_split(_mm(rw_out, dyr, ta=True, out_dtype=BF16, name="mm_dw_rw_out"))
    dcomb = _rows_bwd(_f_comb, comb_rows, [], [[(datt, ATT_WIDTH, 0)]], wrt_rows=list(range(6)), wrt_consts=[],
                      drow_dtypes=[F32] * 6, name="comb_bwd", br=2 * br)
    datt_in = None
    for g, (_, dil) in enumerate(ATT_PATTERNS):
        datt_in = _att_bwd(att_in, g, dil, dcomb[g], dcomb[3 + g], datt_in)
    dy_raw, dr_p, dv_p, dkm_p, dgg, gw["lnx_w"], gw["lnx_b"], gw["r_k"], *recv_early = _rows_bwd(
        _f_rwpost, post_rows, post_c, [[(drw, D, 0)]], wrt_rows=[0, 1, 2, 3, 4], wrt_consts=[0, 1, 2],
        drow_dtypes=[F32] * 5, name="rwpost_bwd", br=br, comm=_SiblingHalves([gw[n] for n in _DONE_EARLY]))
    st.update(loss=loss_acc[0, 0], gw=gw, dxa=dxa, dgate=dgate, datt_in=datt_in,
              dy_raw=dy_raw, dr_p=dr_p, dv_p=dv_p, dkm_p=dkm_p, dgg=dgg, d_ada_late=(d_gt1, d_sh2, d_sc2, d_gt2),
              recv_early=recv_early)
    return st


def _chip_parts(grads, recv, names, core):
    return [_half_sum(lambda a, b: a + b, [g], [r], False, BF16, core, "reduce_add2_" + n)
            for g, r, n in zip(grads, recv, names)]


def _step_after_scan(st, scan_grads, core):
    x, br, gw, h1, zs = st["x"], st["br"], st["gw"], st["h1"], st["zs"]
    dr_s, dlw, dkm_s, dv_s, daa, dbb = scan_grads
    pre_cots = [[(st["dr_p"], D, 0), (dr_s, D, 0)], [(dlw, D, 0)], [(st["dkm_p"], D, 0), (dkm_s, D, 0)],
                [(st["dv_p"], D, 0), (dv_s, D, 0)], [(daa, D, 0)], [(dbb, D, 0)], [(st["dgg"], D, 0)]]
    dzs, gw["w0"], gw["a0"], gw["k_k"], gw["k_a"], dwl = _rows_bwd(
        _f_rwpre, [(zs, N_RWP, 0)], st["rwpre_c"], pre_cots, wrt_rows=[0], wrt_consts=[0, 1, 2, 3, 4],
        drow_dtypes=[F32], name="rwpre_bwd", br=128)
    gw["w2"], gw["a2"] = _cols_split(dwl[0:64, 0:D]), _cols_split(dwl[64:128, D:2 * D])
    gw["g2"] = _cols_split(dwl[128:288, 2 * D:3 * D])
    dz, dmu = _shift_bwd(st["z"], st["mu"], dzs)
    gw["mu_shift"] = dmu[:, :N_RW]
    datt_in, dgate = st["datt_in"], st["dgate"]
    dw_in = [(_mm(datt_in, h1, ta=True, out_dtype=BF16, name="mm_dw_att"), N_ATT),
             (_mm(dz, h1, ta=True, out_dtype=BF16, name="mm_dw_rw"), N_RW),
             (_mm(dgate, h1, ta=True, out_dtype=BF16, name="mm_dw_gate"), N_GATE)]
    slabs = []
    for j in range(4):
        slabs += [_row_window(dw_in, j * W_IN_SHARD, (j + 1) * W_IN_SHARD), jnp.zeros((W_IN_PAD - W_IN_SHARD, D), BF16)]
    gw["w_in"] = jnp.concatenate(slabs, axis=0).reshape(4, W_IN_PAD, D)
    late = [gw[n] for n in _DONE_LATE]
    parts = _chip_parts(late, _run_comm(_SiblingHalves(late), "reduce_sib_late"), _DONE_LATE, core)
    dh1, slots_late = _mm_sum([(datt_in, st["w_att"]), (dz, st["w_rw"]), (dgate, st["w_gate"])],
                              comm=_ScatterToChips(parts), name="mm_dh1")
    grad_x, gw["norm1_w"], d_sc1, d_sh1 = _rows_bwd(
        _f_pre, [(x, D, 0)], st["pre1_c"], [[(dh1, D, 0)], [(st["dxa"], D, 0)]], wrt_rows=[0], wrt_consts=[0, 1, 2],
        drow_dtypes=[F32], name="pre1_bwd", br=2 * br)
    d_gt1, d_sh2, d_sc2, d_gt2 = st["d_ada_late"]
    return st["loss"], grad_x, (d_sh1, d_sc1, d_gt1, d_sh2, d_sc2, d_gt2), gw, slots_late


_SMALL = ("b_ada", "norm1_w", "b_gate", "mu_shift", "w0", "a0", "k_k", "k_a", "r_k", "lnx_w", "lnx_b", "norm2_w",
          "conv_b", "norm_f_w")
_NAMES = ("w_ada", "b_ada", "norm1_w", "w_in", "b_gate", "mu_shift", "w0", "w2", "a0", "a2", "g2", "k_k", "k_a", "r_k",
          "lnx_w", "lnx_b", "w_att_out", "w_rwkv_out", "w_o", "norm2_w", "w_up", "conv_w", "conv_b", "w_down",
          "norm_f_w")


def kernel(x, c, w_ada, b_ada, norm1_w, w_in, b_gate, mu_shift, w0, w2, a0, a2, g2, k_k, k_a, r_k, lnx_w, lnx_b, w_att_out, w_rwkv_out, w_o, norm2_w, w_up, conv_w, conv_b, w_down, norm_f_w, loss_target, m_w_ada, m_b_ada, m_norm1_w, m_w_in, m_b_gate, m_mu_shift, m_w0, m_w2, m_a0, m_a2, m_g2, m_k_k, m_k_a, m_r_k, m_lnx_w, m_lnx_b, m_w_att_out, m_w_rwkv_out, m_w_o, m_norm2_w, m_w_up, m_conv_w, m_conv_b, m_w_down, m_norm_f_w, v_w_ada, v_b_ada, v_norm1_w, v_w_in, v_b_gate, v_mu_shift, v_w0, v_w2, v_a0, v_a2, v_g2, v_k_k, v_k_a, v_r_k, v_lnx_w, v_lnx_b, v_w_att_out, v_w_rwkv_out, v_w_o, v_norm2_w, v_w_up, v_conv_w, v_conv_b, v_w_down, v_norm_f_w):
    args = dict(locals())
    p, pm, pv = {}, {}, {}
    for name in _NAMES:
        for dst, key in ((p, name), (pm, "m_" + name), (pv, "v_" + name)):
            t = args[key]
            if name == "w_in":
                dst[name] = jnp.swapaxes(t, 1, 2)[0]
            else:
                dst[name] = t.reshape(1, -1) if name in ("r_k", "norm_f_w") else t.reshape(t.shape[-2], t.shape[-1])
    xi, yi, ci = _me()
    chip = 2 * xi + yi
    dev = 4 * xi + 2 * yi + ci
    x2, tgt = x[0], loss_target[0]

    n_cw = 3 * (2 * D_FF // 4)
    vec = jnp.concatenate([c.reshape(-1), p["conv_w"].reshape(-1), jnp.zeros((8 * D - D - n_cw,), F32)]).reshape(8, D)
    g0 = _allgather8(vec, "gather_c").reshape(8, 8 * D)
    c_all = g0[:, :D]
    conv_w_full = jnp.concatenate([g0[2 * j, D:D + n_cw].reshape(3, -1) for j in range(4)], axis=1)
    n_ada = 6 * D // 4
    b_ada_sh = lax.dynamic_slice(p["b_ada"], (0, chip * n_ada), (1, n_ada))
    ada_sh = _ada_fwd(c_all, p["w_ada"], b_ada_sh)
    ga = _allgather8(ada_sh, "gather_ada")
    ada_all = jnp.concatenate([ga[2 * j] for j in range(4)], axis=1)
    ada_row = lax.dynamic_slice(ada_all, (dev, 0), (1, 6 * D))
    ada = [ada_row[:, j * D:(j + 1) * D] for j in range(6)]

    big = [n for n, _ in _BIG]
    shard = {n: p[n].astype(BF16) for n in big}
    shard["w_in"] = jnp.pad(shard["w_in"], ((0, W_IN_PAD - W_IN_SHARD), (0, 0)))
    wts = dict(zip(_NEEDED_FIRST, _run_comm(_GatherWeights([shard[n] for n in _NEEDED_FIRST]), "gather_w")))
    for n in _SMALL:
        wts[n] = p[n]
    wts["conv_w"] = conv_w_full
    core = ci.reshape(1).astype(jnp.int32)

    st = _step_to_scan(x2, tgt, ada, wts)
    y_raw, s0s, inverses, late = _scan_fwd(st["zs"], st["lw"], st["km"], st["aa"], st["bb"],
                                           _GatherWeights([shard[n] for n in _NEEDED_LATER]))
    st = _step_between_scans(st, y_raw, dict(zip(_NEEDED_LATER, late)))
    early = _chip_parts([st["gw"][n] for n in _DONE_EARLY], st["recv_early"], _DONE_EARLY, core)
    scan_grads, slots_early = _scan_bwd(st["zs"], st["lw"], st["km"], st["aa"], st["bb"], s0s, inverses,
                                        st["dy_raw"], _ScatterToChips(early))
    loss_part, grad_x, d_ada, gw, slots_late = _step_after_scan(st, scan_grads, core)

    small = [jnp.concatenate(d_ada, axis=1)] + [gw[n] for n in _SMALL[1:]] + [gw["conv_w"], loss_part.reshape(1, 1)]
    sizes = [t.size for t in small]
    flat = jnp.concatenate([t.reshape(-1) for t in small])
    npad = (-flat.shape[0]) % (8 * D)
    srows = (flat.shape[0] + npad) // D
    flat = jnp.concatenate([flat, jnp.zeros((npad,), F32)]).reshape(srows, D)
    parts = _allgather8(flat, "gather_small")
    tot = _sum_lead(parts, "sum_small").reshape(-1)
    pieces, pos = [], 0
    for sz in sizes:
        pieces.append(tot[pos:pos + sz])
        pos += sz
    grads = {}
    for n, piece in zip(_SMALL, pieces[:len(_SMALL)]):
        grads[n] = piece.reshape(p[n].shape)
    conv_w_grad = pieces[len(_SMALL)].reshape(3, 2 * D_FF)
    grads["conv_w"] = lax.dynamic_slice(conv_w_grad, (0, chip * (n_cw // 3)), (3, n_cw // 3))
    loss = pieces[-1][0]
    d_ada_all = parts[:, :6].reshape(8, 6 * D)
    grads["w_ada"] = _ada_bwd(c_all.T, lax.dynamic_slice(d_ada_all, (0, chip * n_ada), (8, n_ada)))

    order = _DONE_EARLY + _DONE_LATE
    reds = [_half_sum(lambda t: t[0] + t[1] + t[2] + t[3], [], [t], True, F32, core, "reduce_add4_" + n)
            for n, t in zip(order, list(slots_early) + list(slots_late))]
    for n, g in zip(order, _reduce_finish(reds, "reduce_sib2")):
        grads[n] = g

    outs_g, outs_d, outs_m, outs_v = [], [], [], []
    grads["w_in"] = grads["w_in"][:W_IN_SHARD]
    for name in _NAMES:
        g = grads[name]
        d, m, v = _adamw(p[name], g, pm[name], pv[name], "adamw_" + name)
        shape = args[name].shape
        for outs, t in ((outs_g, g), (outs_d, d), (outs_m, m), (outs_v, v)):
            outs.append(jnp.swapaxes(t[None], 1, 2) if name == "w_in" else t.reshape(shape))
    return (loss, grad_x.reshape(x.shape), *outs_g, *outs_d, *outs_m, *outs_v)
```

```python
import functools
import math

import jax
import jax.numpy as jnp
from jax import lax
from jax.experimental import pallas as pl
from jax.experimental.pallas import tpu as pltpu

F32 = jnp.float32
BF16 = jnp.bfloat16
HI = lax.Precision.HIGHEST
MESH = pl.DeviceIdType.MESH

D = 1024
ATT_PATTERNS = ((128, 1), (512, 4), (2048, 16))
ATT_BLOCK = 128
ATT_WIDTH = 512
N_ATT = 3 * 3 * ATT_WIDTH
N_RW = 3 * D + 64 + 64 + 160
N_RWP = 3456
N_LORA = N_RWP - 3 * D
N_GATE = 2 * D
D_FF = 2816
RMS_EPS = 1e-6
GN_EPS = 64e-5
SCAN_CHUNK = 64
SCAN_PAIRS = 8
NEG = -1e30
VMEM_LIMIT = 48 * 1024 * 1024
HALO = 16

ADAM_LR, ADAM_B1, ADAM_B2, ADAM_EPS, ADAM_WD, ADAM_STEP = 0.001, 0.9, 0.999, 1e-08, 0.01, 10


def _pcall(body, **kw):
    return pl.pallas_call(body, **kw)


def _cparams(sem):
    return pltpu.CompilerParams(dimension_semantics=sem, vmem_limit_bytes=VMEM_LIMIT)


def _div(n, pref, mult):
    best = None
    d = mult
    while d <= min(n, pref):
        if n % d == 0:
            best = d
        d += mult
    return best if best else n


def _dg(a, b, ca, cb):
    return lax.dot_general(a.astype(BF16), b.astype(BF16), (((ca,), (cb,)), ((), ())), preferred_element_type=F32)


@jax.custom_vjp
def _nn(a, b):
    return _dg(a, b, 1, 0)


@jax.custom_vjp
def _nt(a, b):
    return _dg(a, b, 1, 1)


@jax.custom_vjp
def _tn(a, b):
    return _dg(a, b, 0, 0)


_nn.defvjp(lambda a, b: (_nn(a, b), (a, b)), lambda res, g: (_nt(g, res[1]), _tn(res[0], g)))
_nt.defvjp(lambda a, b: (_nt(a, b), (a, b)), lambda res, g: (_nn(g, res[1]), _tn(g, res[0])))
_tn.defvjp(lambda a, b: (_tn(a, b), (a, b)), lambda res, g: (_nt(res[1], g), _nn(res[0], g)))


def _bdg(a, b, ca, cb):
    return lax.dot_general(a.astype(BF16), b.astype(BF16), (((ca,), (cb,)), ((0,), (0,))), preferred_element_type=F32)


@jax.custom_vjp
def _bnn(a, b):
    return _bdg(a, b, 2, 1)


@jax.custom_vjp
def _bnt(a, b):
    return _bdg(a, b, 2, 2)


@jax.custom_vjp
def _btn(a, b):
    return _bdg(a, b, 1, 1)


_bnn.defvjp(lambda a, b: (_bnn(a, b), (a, b)), lambda res, g: (_bnt(g, res[1]), _btn(res[0], g)))
_bnt.defvjp(lambda a, b: (_bnt(a, b), (a, b)), lambda res, g: (_bnn(g, res[1]), _btn(g, res[0])))
_btn.defvjp(lambda a, b: (_btn(a, b), (a, b)), lambda res, g: (_bnt(res[1], g), _bnn(res[0], g)))


def _hsum_impl(x, e, et):
    eb, etb = e.astype(BF16), et.astype(BF16)
    s = jnp.dot(x.astype(BF16), eb, preferred_element_type=F32)
    return jnp.dot(s.astype(BF16), etb, preferred_element_type=F32)


@jax.custom_vjp
def _hsum(x, e, et):
    return _hsum_impl(x, e, et)


_hsum.defvjp(lambda x, e, et: (_hsum_impl(x, e, et), (e, et)),
             lambda res, g: (_hsum_impl(g, res[0], res[1]), jnp.zeros_like(res[0]), jnp.zeros_like(res[1])))


def _mm(a, b, *, ta=False, tb=False, out_dtype=F32, add=None, b_chip=False, out_chip=False, comm=None, name):
    riding = _NOTHING if comm is None else comm
    nc = riding.n
    if ta:
        kdim, m = a.shape
    else:
        m, kdim = a.shape
    if b_chip:
        n = b.shape[1] if tb else 4 * b.shape[2]
    else:
        n = b.shape[0] if tb else b.shape[1]
    tm, tn, tk = _div(m, 1536, 128), _div(n, 1536, 128), _div(kdim, 2048 if ta else 1408, 128)
    if b_chip and tb:
        tk = kdim // 4
    if (b_chip and not tb) or out_chip:
        tn = n // 4
    nk = kdim // tk
    ca, cb = (0 if ta else 1), (1 if tb else 0)

    nin = 2 if add is None else 3
    gi, gj = m // tm, n // tn

    def body(*refs):
        a_ref, b_ref = refs[0], refs[1]
        add_ref = None if add is None else refs[2]
        o_ref = refs[nin + nc]
        step = (pl.program_id(0) * gj + pl.program_id(1)) * nk + pl.program_id(2)
        before, after = _comm_phases(riding, refs[nin:nin + nc] + refs[nin + nc + 1:nin + 2 * nc + 1]
                                     + refs[nin + 2 * nc + 1 + (nk > 1):], gi * gj * nk, step)
        before()
        part = lax.dot_general(a_ref[...], b_ref[...], (((ca,), (cb,)), ((), ())), preferred_element_type=F32)

        def finish(r):
            if add_ref is not None:
                r = r + add_ref[...]
            o_ref[...] = r.astype(o_ref.dtype)

        if nk == 1:
            finish(part)
            after()
            return
        acc = refs[nin + 2 * nc + 1]
        k = pl.program_id(2)

        @pl.when(k == 0)
        def _():
            acc[...] = part

        @pl.when(k > 0)
        def _():
            acc[...] += part

        @pl.when(k == nk - 1)
        def _():
            finish(acc[...])

        after()

    a_spec = pl.BlockSpec((tk, tm), lambda i, j, k: (k, i)) if ta else pl.BlockSpec((tm, tk), lambda i, j, k: (i, k))
    if b_chip:
        b_spec = (pl.BlockSpec((None, tn, tk), lambda i, j, k: (k, j, 0)) if tb
                  else pl.BlockSpec((None, tk, tn), lambda i, j, k: (j, k, 0)))
    else:
        b_spec = pl.BlockSpec((tn, tk), lambda i, j, k: (j, k)) if tb else pl.BlockSpec((tk, tn), lambda i, j, k: (k, j))
    in_specs = [a_spec, b_spec]
    args = [a, b]
    if add is not None:
        in_specs.append(pl.BlockSpec((tm, tn), lambda i, j, k: (i, j)))
        args.append(add)
    if out_chip:
        out_spec = pl.BlockSpec((None, tm, tn), lambda i, j, k: (j, i, 0))
        out_shape = jax.ShapeDtypeStruct((4, m, tn), out_dtype)
    else:
        out_spec = pl.BlockSpec((tm, tn), lambda i, j, k: (i, j))
        out_shape = jax.ShapeDtypeStruct((m, n), out_dtype)
    res = _pcall(
        body, name=name, grid=(gi, gj, nk), in_specs=in_specs + [_HBM] * nc, out_specs=[out_spec] + [_HBM] * nc,
        out_shape=[out_shape] + riding.out_shape,
        scratch_shapes=([] if nk == 1 else [pltpu.VMEM((tm, tn), F32)]) + riding.sems,
        compiler_params=_cparams(("arbitrary",) * 3 if nc else ("parallel", "parallel", "arbitrary")),
    )(*args, *riding.ins)
    return res[0] if comm is None else (res[0], res[1:])


def _mm_sum(pairs, *, comm, name):
    m, n = pairs[0][0].shape[0], pairs[0][1].shape[1]
    tm, tn = _div(m, 1024, 128), _div(n, 1024, 128)
    tks = [_div(a.shape[1], 1408, 128) for a, _ in pairs]
    nks = [a.shape[1] // tk for (a, _), tk in zip(pairs, tks)]
    offs = [sum(nks[:p]) for p in range(len(pairs))]
    total, npair, nc = sum(nks), len(pairs), comm.n
    gi, gj = m // tm, n // tn

    def body(*refs):
        o_ref, acc = refs[2 * npair + nc], refs[2 * npair + 2 * nc + 1]
        k = pl.program_id(2)
        step = (pl.program_id(0) * gj + pl.program_id(1)) * total + k
        before, after = _comm_phases(comm, refs[2 * npair:2 * npair + nc]
                                     + refs[2 * npair + nc + 1:2 * npair + 2 * nc + 1]
                                     + refs[2 * npair + 2 * nc + 2:], gi * gj * total, step)
        before()
        for p in range(npair):
            def partial_product(p=p):
                part = jnp.dot(refs[2 * p][...], refs[2 * p + 1][...], preferred_element_type=F32)
                if p == 0:
                    @pl.when(k == 0)
                    def _():
                        acc[...] = part

                    @pl.when(k > 0)
                    def _():
                        acc[...] += part
                else:
                    acc[...] += part

            pl.when(jnp.logical_and(k >= offs[p], k < offs[p] + nks[p]))(partial_product)

        @pl.when(k == total - 1)
        def _():
            o_ref[...] = acc[...].astype(o_ref.dtype)

        after()

    def specs(tk, off, nk):
        def kb(k):
            return jnp.clip(k - off, 0, nk - 1)
        return [pl.BlockSpec((tm, tk), lambda i, j, k: (i, kb(k))), pl.BlockSpec((tk, tn), lambda i, j, k: (kb(k), j))]

    in_specs, args = [], []
    for (a, b), tk, off, nk in zip(pairs, tks, offs, nks):
        in_specs += specs(tk, off, nk)
        args += [a, b]
    res = _pcall(
        body, name=name, grid=(gi, gj, total), in_specs=in_specs + [_HBM] * nc,
        out_specs=[pl.BlockSpec((tm, tn), lambda i, j, k: (i, j))] + [_HBM] * nc,
        out_shape=[jax.ShapeDtypeStruct((m, n), BF16)] + comm.out_shape,
        scratch_shapes=[pltpu.VMEM((tm, tn), F32)] + comm.sems,
        compiler_params=_cparams(("arbitrary",) * 3),
    )(*args, *comm.ins)
    return res[0], res[1:]


def _row_spec(br, w, cb):
    return pl.BlockSpec((br, w), lambda i: (i, cb))


def _const_spec(shape):
    return pl.BlockSpec(shape, lambda i: (0,) * len(shape))


def _rows_fwd(fn, rows, consts, outs, *, name, br, acc_shape=None, halo=None):
    s = rows[0][0].shape[0]
    nr, nc = len(rows), len(consts)
    kept = [k for k, o in enumerate(outs) if o is not None]

    def body(*refs):
        xs = [r[...].astype(F32) for r in refs[:nr]]
        cs = [c[...] for c in refs[nr:nr + nc]]
        if halo is not None:
            cs.append(jnp.where(pl.program_id(0) == 0, 0.0, refs[nr + nc][...].astype(F32)))
        res = fn(*xs, *cs)
        orefs = refs[nr + nc + (halo is not None):]
        for j, k in enumerate(kept):
            orefs[j][...] = res[k].astype(orefs[j].dtype)
        if acc_shape is not None:
            acc_ref = orefs[len(kept)]

            @pl.when(pl.program_id(0) == 0)
            def _():
                acc_ref[...] = jnp.zeros_like(acc_ref)

            acc_ref[...] += res[len(outs)]

    in_specs = [_row_spec(br, w, cb) for (_, w, cb) in rows] + [_const_spec(c.shape) for c in consts]
    args = [r[0] for r in rows] + list(consts)
    if halo is not None:
        harr, hw, hcb = rows[halo]
        in_specs.append(pl.BlockSpec((HALO, hw), lambda i: (jnp.maximum(i * (br // HALO) - 1, 0), hcb)))
        args.append(harr)
    out_specs = [_row_spec(br, outs[k][0], 0) for k in kept]
    out_shape = [jax.ShapeDtypeStruct((s, outs[k][0]), outs[k][1]) for k in kept]
    if acc_shape is not None:
        out_specs.append(_const_spec(acc_shape))
        out_shape.append(jax.ShapeDtypeStruct(acc_shape, F32))
    return _pcall(
        body, name=name, grid=(pl.cdiv(s, br),), in_specs=in_specs, out_specs=out_specs, out_shape=out_shape,
        compiler_params=_cparams(("arbitrary",)),
    )(*args)


def _rows_bwd(fn, rows, consts, cots, *, wrt_rows, wrt_consts, drow_dtypes, name, br, unit_cot=False, comm=None):
    comm = _NOTHING if comm is None else comm
    ncomm = comm.n
    nout = len(wrt_rows) + len(wrt_consts)
    s = rows[0][0].shape[0]
    nr, nc = len(rows), len(consts)
    flat_cots = [c for lst in cots for c in lst]
    ncot = len(flat_cots)

    def body(*refs):
        xs = [r[...].astype(F32) for r in refs[:nr]]
        cs = [c[...] for c in refs[nr:nr + nc]]
        cvals = [c[...].astype(F32) for c in refs[nr + nc:nr + nc + ncot]]
        orefs = refs[nr + nc + ncot + ncomm:]
        before, after = _comm_phases(comm, refs[nr + nc + ncot:nr + nc + ncot + ncomm] + orefs[nout:], s // br)
        before()

        def g(*d):
            xs2, cs2 = list(xs), list(cs)
            for j, k in enumerate(wrt_rows):
                xs2[k] = d[j]
            for j, k in enumerate(wrt_consts):
                cs2[k] = d[len(wrt_rows) + j]
            return tuple(fn(*xs2, *cs2))

        prim = [xs[k] for k in wrt_rows] + [cs[k] for k in wrt_consts]
        outs, vjp = jax.vjp(g, *prim)
        ct = []
        pos = 0
        for o, lst in zip(outs, cots):
            if unit_cot:
                ct.append(jnp.ones_like(o))
                continue
            acc = jnp.zeros_like(o)
            for _ in lst:
                acc = acc + cvals[pos]
                pos += 1
            ct.append(acc)
        grads = vjp(tuple(ct))
        for j in range(len(wrt_rows)):
            orefs[j][...] = grads[j].astype(orefs[j].dtype)

        @pl.when(pl.program_id(0) == 0)
        def _():
            for j in range(len(wrt_consts)):
                oref = orefs[len(wrt_rows) + j]
                oref[...] = jnp.zeros_like(oref)

        for j in range(len(wrt_consts)):
            orefs[len(wrt_rows) + j][...] += grads[len(wrt_rows) + j]
        after()

    in_specs = ([_row_spec(br, w, cb) for (_, w, cb) in rows] + [_const_spec(c.shape) for c in consts]
                + [_row_spec(br, w, cb) for (_, w, cb) in flat_cots] + [_HBM] * ncomm)
    out_specs = ([_row_spec(br, rows[k][1], 0) for k in wrt_rows] + [_const_spec(consts[k].shape) for k in wrt_consts]
                 + [_HBM] * ncomm)
    out_shape = ([jax.ShapeDtypeStruct((s, rows[k][1]), dt) for k, dt in zip(wrt_rows, drow_dtypes)]
                 + [jax.ShapeDtypeStruct(consts[k].shape, F32) for k in wrt_consts] + comm.out_shape)
    return _pcall(
        body, name=name, grid=(s // br,), in_specs=in_specs, out_specs=out_specs, out_shape=out_shape,
        scratch_shapes=comm.sems, compiler_params=_cparams(("arbitrary",)),
    )(*[r[0] for r in rows], *consts, *[c[0] for c in flat_cots], *comm.ins)


def _rms(x, w):
    return x * lax.rsqrt(jnp.mean(x * x, axis=-1, keepdims=True) + RMS_EPS) * w


def _f_pre(x, nw, sc, sh):
    return _rms(x, nw) * (1.0 + sc) + sh, x


def _f_pre2(x, o, gt, nw, sc, sh):
    x1 = x + gt * o
    return x1, _rms(x1, nw) * (1.0 + sc) + sh


def _f_fin(x1, f, tgt, gt, nfw):
    y = _rms(x1 + gt * f, nfw)
    return (0.5 * jnp.mean(jnp.square(y - tgt), axis=-1, keepdims=True),)


def _f_comb(o1, o2, o3, l1, l2, l3):
    m = lax.stop_gradient(jnp.maximum(jnp.maximum(l1, l2), l3))
    e1, e2, e3 = jnp.exp(l1 - m), jnp.exp(l2 - m), jnp.exp(l3 - m)
    return ((e1 * o1 + e2 * o2 + e3 * o3) / (e1 + e2 + e3),)


def _f_rwpre(zs, w0, a0, k_k, k_a, wl, e, et):
    r, k, v, zl = zs[:, 0:D], zs[:, D:2 * D], zs[:, 2 * D:3 * D], zs[:, 3 * D:N_RWP]
    lane = lax.broadcasted_iota(jnp.int32, zl.shape, 1)
    t = jnp.where(lane < 64, jnp.tanh(zl), jnp.where(lane < 128, zl, jnp.where(lane < 288, jax.nn.sigmoid(zl), 0.0)))
    lo = _nn(t[:, 0:128], wl[0:128, 0:2 * D])
    g = _nn(t[:, 128:N_LORA], wl[128:N_LORA, 2 * D:3 * D])
    lw = -math.exp(-0.5) * jax.nn.sigmoid(w0 + lo[:, 0:D])
    a = jax.nn.sigmoid(a0 + lo[:, D:2 * D])
    k_mod = k * (1.0 + (a - 1.0) * k_a)
    kk = k * k_k
    kk = kk / jnp.maximum(jnp.sqrt(_hsum(kk * kk, e, et)), 1e-12)
    return r, lw, k_mod, v, -kk, kk * a, g


def _f_rwpost(y, r, v, k_mod, g, lnx_w, lnx_b, r_k, e, et):
    mean = _hsum(y, e, et) * (1.0 / 64)
    yc = y - mean
    var = _hsum(yc * yc, e, et) * (1.0 / 64)
    yn = yc * lax.rsqrt(var + GN_EPS) * lnx_w + lnx_b
    bonus = _hsum(r * k_mod * r_k, e, et) * v
    return ((yn + bonus) * g,)


def _f_mix(gi, ya, yr, bg):
    gate = jax.nn.sigmoid(gi + bg)
    return (gate[:, 0:D] * ya + gate[:, D:2 * D] * yr,)


def _f_adamw(w, g, m, v):
    m = ADAM_B1 * m + (1.0 - ADAM_B1) * g
    v = ADAM_B2 * v + (1.0 - ADAM_B2) * jnp.square(g)
    m_hat = m / (1.0 - ADAM_B1 ** ADAM_STEP)
    v_hat = v / (1.0 - ADAM_B2 ** ADAM_STEP)
    return -ADAM_LR * (m_hat / (jnp.sqrt(v_hat) + ADAM_EPS) + ADAM_WD * w), m, v


def _down(x, k):
    row = lax.broadcasted_iota(jnp.int32, x.shape, 0)
    return jnp.where(row < k, 0.0, pltpu.roll(x, k, 0))


def _up(x, k):
    n = x.shape[0]
    row = lax.broadcasted_iota(jnp.int32, x.shape, 0)
    return jnp.where(row >= n - k, 0.0, pltpu.roll(x, n - k, 0))


def _col_spec(s, w, off=0):
    return pl.BlockSpec((s, w), lambda j: (0, j + off))


def _rwpre_shift_bwd(zs, z, mu, consts, cots, *, br):
    s, w = zs.shape
    n = s // br
    flat = [c for lst in cots for c in lst]
    nc, ncot, nwrt = len(consts), len(flat), 5

    def this(i):
        return jnp.minimum(i, n - 1)

    def last(i):
        return jnp.maximum(i - 1, 0)

    def body(*refs):
        zs_ref, z_ref, zh_ref, mu_ref = refs[:4]
        c_refs, cot_refs = refs[4:4 + nc], refs[4 + nc:4 + nc + ncot]
        dz_ref, dmu_ref = refs[4 + nc + ncot:6 + nc + ncot]
        dc_refs = refs[6 + nc + ncot:6 + nc + ncot + nwrt]
        kept = refs[-1]
        i = pl.program_id(0)

        @pl.when(i == 0)
        def _():
            dmu_ref[...] = jnp.zeros_like(dmu_ref)
            for ref in dc_refs:
                ref[...] = jnp.zeros_like(ref)

        cs = [c[...] for c in c_refs]

        def g(zz, *d):
            return tuple(_f_rwpre(zz, *d, *cs[nwrt:]))

        outs, vjp = jax.vjp(g, zs_ref[...], *cs[:nwrt])
        cts, pos = [], 0
        for o, lst in zip(outs, cots):
            acc = jnp.zeros_like(o)
            for _ in lst:
                acc = acc + cot_refs[pos][...].astype(F32)
                pos += 1
            cts.append(acc)
        grads = vjp(tuple(cts))
        dzs_new = grads[0]

        @pl.when(i < n)
        def _():
            for ref, gr in zip(dc_refs, grads[1:]):
                ref[...] += gr

        @pl.when(i > 0)
        def _():
            d, m = kept[...], mu_ref[...]
            row = lax.broadcasted_iota(jnp.int32, d.shape, 0)
            head = jnp.sum(jnp.where(row == 0, dzs_new, 0.0), axis=0, keepdims=True)
            head = jnp.where(i < n, head, 0.0)
            dm = d * m
            after = jnp.where(row == br - 1, head * m, pltpu.roll(dm, br - 1, 0))
            dz_ref[...] = (d - dm + after).astype(dz_ref.dtype)
            zz, halo = z_ref[...].astype(F32), zh_ref[...].astype(F32)
            tail = jnp.sum(jnp.where(lax.broadcasted_iota(jnp.int32, halo.shape, 0) == HALO - 1, halo, 0.0), axis=0,
                           keepdims=True)
            before = jnp.where(row == 0, jnp.where(i > 1, tail, 0.0), pltpu.roll(zz, 1, 0))
            dmu_ref[...] += jnp.sum(d * (before - zz), axis=0, keepdims=True)

        kept[...] = dzs_new

    in_specs = ([pl.BlockSpec((br, w), lambda i: (this(i), 0)), pl.BlockSpec((br, w), lambda i: (last(i), 0)),
                 pl.BlockSpec((HALO, w), lambda i: (jnp.maximum(last(i) * (br // HALO) - 1, 0), 0)),
                 _const_spec(mu.shape)] + [_const_spec(c.shape) for c in consts]
                + [pl.BlockSpec((br, cw), lambda i, cb=cb: (this(i), cb)) for (_, cw, cb) in flat])
    out_specs = ([pl.BlockSpec((br, w), lambda i: (last(i), 0)), _const_spec(mu.shape)]
                 + [_const_spec(consts[k].shape) for k in range(nwrt)])
    out_shape = ([jax.ShapeDtypeStruct((s, w), BF16), jax.ShapeDtypeStruct(mu.shape, F32)]
                 + [jax.ShapeDtypeStruct(consts[k].shape, F32) for k in range(nwrt)])
    return _pcall(
        body, name="rwpre_shift_bwd", grid=(n + 1,), in_specs=in_specs, out_specs=out_specs, out_shape=out_shape,
        scratch_shapes=[pltpu.VMEM((br, w), F32)], compiler_params=_cparams(("arbitrary",)),
    )(zs, z, z, mu, *consts, *[c[0] for c in flat])


def _conv3(x, w_ref, b_ref):
    return b_ref[...] + w_ref[0:1, :] * _down(x, 2) + w_ref[1:2, :] * _down(x, 1) + w_ref[2:3, :] * x


def _conv_fwd(u, cw, cb):
    s = u.shape[0]
    nb = D_FF // 128

    def body(ug_ref, uv_ref, wg_ref, wv_ref, bg_ref, bv_ref, o_ref):
        gate = _conv3(ug_ref[...], wg_ref, bg_ref)
        val = _conv3(uv_ref[...], wv_ref, bv_ref)
        o_ref[...] = (gate * jax.nn.sigmoid(gate) * val).astype(o_ref.dtype)

    return _pcall(
        body, name="conv_fwd", grid=(nb,),
        in_specs=[_col_spec(s, 128), _col_spec(s, 128, nb), _col_spec(3, 128), _col_spec(3, 128, nb),
                  _col_spec(1, 128), _col_spec(1, 128, nb)],
        out_specs=_col_spec(s, 128), out_shape=jax.ShapeDtypeStruct((s, D_FF), BF16),
        compiler_params=_cparams(("parallel",)),
    )(u, u, cw, cw, cb, cb)


def _conv_bwd(u, cw, cb, dact):
    s = u.shape[0]
    nb = D_FF // 128

    def half(x, d, w_ref, du_ref, dw_ref, db_ref):
        x1, x2 = _down(x, 1), _down(x, 2)
        du_ref[...] = (w_ref[2:3, :] * d + w_ref[1:2, :] * _up(d, 1) + w_ref[0:1, :] * _up(d, 2)).astype(du_ref.dtype)
        dw_ref[0:1, :] = jnp.sum(d * x2, axis=0, keepdims=True)
        dw_ref[1:2, :] = jnp.sum(d * x1, axis=0, keepdims=True)
        dw_ref[2:3, :] = jnp.sum(d * x, axis=0, keepdims=True)
        db_ref[...] = jnp.sum(d, axis=0, keepdims=True)

    def body(ug_ref, uv_ref, wg_ref, wv_ref, bg_ref, bv_ref, da_ref,
             dug_ref, duv_ref, dwg_ref, dwv_ref, dbg_ref, dbv_ref):
        ug, uv, da = ug_ref[...], uv_ref[...], da_ref[...]
        gate = _conv3(ug, wg_ref, bg_ref)
        val = _conv3(uv, wv_ref, bv_ref)
        sg = jax.nn.sigmoid(gate)
        dgate = da * val * sg * (1.0 + gate * (1.0 - sg))
        dval = da * gate * sg
        half(ug, dgate, wg_ref, dug_ref, dwg_ref, dbg_ref)
        half(uv, dval, wv_ref, duv_ref, dwv_ref, dbv_ref)

    dug, duv, dwg, dwv, dbg, dbv = _pcall(
        body, name="conv_bwd", grid=(nb,),
        in_specs=[_col_spec(s, 128), _col_spec(s, 128, nb), _col_spec(3, 128), _col_spec(3, 128, nb),
                  _col_spec(1, 128), _col_spec(1, 128, nb), _col_spec(s, 128)],
        out_specs=[_col_spec(s, 128), _col_spec(s, 128), _col_spec(3, 128), _col_spec(3, 128),
                   _col_spec(1, 128), _col_spec(1, 128)],
        out_shape=[jax.ShapeDtypeStruct((s, D_FF), BF16), jax.ShapeDtypeStruct((s, D_FF), BF16),
                   jax.ShapeDtypeStruct((3, D_FF), F32), jax.ShapeDtypeStruct((3, D_FF), F32),
                   jax.ShapeDtypeStruct((1, D_FF), F32), jax.ShapeDtypeStruct((1, D_FF), F32)],
        compiler_params=_cparams(("parallel",)),
    )(u, u, cw, cw, cb, cb, dact)
    return (jnp.concatenate([dug, duv], axis=1), jnp.concatenate([dwg, dwv], axis=1),
            jnp.concatenate([dbg, dbv], axis=1))


ATT_BATCH = 4


def _att_batch(q, kp, kc, vp, vc, first):
    ma = lax.broadcasted_iota(jnp.int32, (1, ATT_BLOCK, 128), 2) < 64

    def diag(x):
        return jnp.concatenate([jnp.where(ma, x, 0.0), jnp.where(ma, 0.0, x)], axis=1)

    qi = lax.broadcasted_iota(jnp.int32, (1, ATT_BLOCK, 2 * ATT_BLOCK), 1)
    kj = lax.broadcasted_iota(jnp.int32, (1, ATT_BLOCK, 2 * ATT_BLOCK), 2) & (ATT_BLOCK - 1)
    okp = kj >= qi + jnp.where(first, 2 * ATT_BLOCK, 0)
    okc = kj <= qi
    sp = jnp.where(okp, _bnt(q, diag(kp)) * 0.125, NEG)
    sc = jnp.where(okc, _bnt(q, diag(kc)) * 0.125, NEG)

    def per_head(fn, x):
        return fn(x[..., :ATT_BLOCK]), fn(x[..., ATT_BLOCK:])

    def spread(ab):
        return jnp.concatenate([jnp.broadcast_to(t, t.shape[:2] + (ATT_BLOCK,)) for t in ab], axis=-1)

    row_max = functools.partial(jnp.max, axis=-1, keepdims=True)
    row_sum = functools.partial(jnp.sum, axis=-1, keepdims=True)
    m = [lax.stop_gradient(jnp.maximum(a, b)) for a, b in zip(per_head(row_max, sp), per_head(row_max, sc))]
    pp, pc = jnp.exp(sp - spread(m)), jnp.exp(sc - spread(m))
    den = [a + b for a, b in zip(per_head(row_sum, pp), per_head(row_sum, pc))]
    num = _bnn(pp, diag(vp)) + _bnn(pc, diag(vc))
    out = num / jnp.where(ma, den[0], den[1])
    lse = jnp.where(ma, m[0] + jnp.log(den[0]), m[1] + jnp.log(den[1]))
    return out, jnp.broadcast_to(lse, out.shape)


def _att_pairs_per_step(dil):
    return 4 if dil == 1 else 1


def _att_residues(dil):
    return min(dil, ATT_BATCH // _att_pairs_per_step(dil))


def _att_specs(g, dil):
    rows, pp = ATT_BLOCK * dil, _att_pairs_per_step(dil)

    def cur(slot):
        return pl.BlockSpec((rows, 128 * pp), lambda n, p: (n, (g * 3 + slot) * (4 // pp) + p))

    def prev(slot):
        return pl.BlockSpec((rows, 128 * pp), lambda n, p: (jnp.maximum(n - 1, 0), (g * 3 + slot) * (4 // pp) + p))

    return [cur(0), prev(1), cur(1), prev(2), cur(2)]


def _att_out_spec(dil):
    return pl.BlockSpec((ATT_BLOCK * dil, 128 * _att_pairs_per_step(dil)), lambda n, p: (n, p))


def _att_grid(s, dil):
    return (s // (ATT_BLOCK * dil), 4 // _att_pairs_per_step(dil))


def _att_windows(i, dil):
    res = _att_residues(dil)

    def rows(r):
        return pl.ds(i * res + r, ATT_BLOCK, stride=dil) if dil > 1 else pl.ds(0, ATT_BLOCK)

    return [(rows(r), pl.ds(128 * j, 128)) for j in range(_att_pairs_per_step(dil)) for r in range(res)]


def _att_fwd(att_in, g, dil):
    s = att_in.shape[0]

    def body(q_ref, kp_ref, kc_ref, vp_ref, vc_ref, o_ref, l_ref):
        first = pl.program_id(0) == 0

        def one(i, carry):
            win = _att_windows(i, dil)
            vals = [jnp.stack([ref[w] for w in win]) for ref in (q_ref, kp_ref, kc_ref, vp_ref, vc_ref)]
            o, l = _att_batch(*vals, first)
            for j, w in enumerate(win):
                o_ref[w] = o[j]
                l_ref[w] = l[j]
            return carry

        lax.fori_loop(0, dil // _att_residues(dil), one, 0)

    return _pcall(
        body, name=f"att_fwd{g}", grid=_att_grid(s, dil), in_specs=_att_specs(g, dil),
        out_specs=[_att_out_spec(dil)] * 2, out_shape=[jax.ShapeDtypeStruct((s, ATT_WIDTH), F32)] * 2,
        compiler_params=_cparams(("parallel", "parallel")),
    )(att_in, att_in, att_in, att_in, att_in)


def _att_bwd(att_in, g, dil, do, dl, acc):
    s = att_in.shape[0]

    def body(q_ref, kp_ref, kc_ref, vp_ref, vc_ref, do_ref, dl_ref, dq_ref, dkp_ref, dkc_ref, dvp_ref, dvc_ref):
        first = pl.program_id(0) == 0

        def one(i, carry):
            win = _att_windows(i, dil)
            vals = [jnp.stack([ref[w] for w in win]) for ref in (q_ref, kp_ref, kc_ref, vp_ref, vc_ref)]
            _, vjp = jax.vjp(lambda *a: _att_batch(*a, first), *vals)
            grads = vjp((jnp.stack([do_ref[w] for w in win]), jnp.stack([dl_ref[w] for w in win])))
            for ref, gr in zip((dq_ref, dkp_ref, dkc_ref, dvp_ref, dvc_ref), grads):
                for j, w in enumerate(win):
                    ref[w] = gr[j]
            return carry

        lax.fori_loop(0, dil // _att_residues(dil), one, 0)

    dq, dkp, dkc, dvp, dvc = _pcall(
        body, name=f"att_bwd{g}", grid=_att_grid(s, dil), in_specs=_att_specs(g, dil) + [_att_out_spec(dil)] * 2,
        out_specs=[_att_out_spec(dil)] * 5, out_shape=[jax.ShapeDtypeStruct((s, ATT_WIDTH), F32)] * 5,
        compiler_params=_cparams(("parallel", "parallel")),
    )(att_in, att_in, att_in, att_in, att_in, do, dl)

    unit, rb = ATT_BLOCK * dil, 1024
    steps = s // rb
    within = unit < rb

    def shifted(cur_ref, next_ref, has_next):
        nxt = jnp.where(has_next, next_ref[...], 0.0)
        return jnp.concatenate([cur_ref[unit:, :], nxt], axis=0) if within else nxt

    def cbody(dq_ref, dkc_ref, dkp_ref, dkn_ref, dvc_ref, dvp_ref, dvn_ref, *rest):
        o_ref = rest[-1]
        has_next = pl.program_id(0) + (1 if within else unit // rb) < steps
        o_ref[:, 0:ATT_WIDTH] = dq_ref[...].astype(BF16)
        o_ref[:, ATT_WIDTH:2 * ATT_WIDTH] = (dkc_ref[...] + shifted(dkp_ref, dkn_ref, has_next)).astype(BF16)
        o_ref[:, 2 * ATT_WIDTH:3 * ATT_WIDTH] = (dvc_ref[...] + shifted(dvp_ref, dvn_ref, has_next)).astype(BF16)

    cur = pl.BlockSpec((rb, ATT_WIDTH), lambda i: (i, 0))
    if within:
        nxt = pl.BlockSpec((unit, ATT_WIDTH), lambda i: (jnp.minimum((i + 1) * (rb // unit), s // unit - 1), 0))
    else:
        nxt = pl.BlockSpec((rb, ATT_WIDTH), lambda i: (jnp.minimum(i + unit // rb, steps - 1), 0))
    carried = [] if acc is None else [acc]
    return _pcall(
        cbody, name=f"att_bwd_sum{g}", grid=(steps,),
        in_specs=[cur, cur, cur, nxt, cur, cur, nxt] + [pl.BlockSpec(memory_space=pl.ANY)] * len(carried),
        out_specs=pl.BlockSpec((rb, 3 * ATT_WIDTH), lambda i: (i, g)),
        out_shape=jax.ShapeDtypeStruct((s, N_ATT), BF16), input_output_aliases={7: 0} if carried else {},
        compiler_params=_cparams(("parallel",)),
    )(dq, dkc, dkp, dkp, dvc, dvp, dvp, *carried)


def _cumsum_rows_impl(x):
    row = lax.broadcasted_iota(jnp.int32, x.shape, 0)
    shift = 1
    while shift < x.shape[0]:
        x = x + jnp.where(row >= shift, pltpu.roll(x, shift, 0), 0.0)
        shift *= 2
    return x


@jax.custom_vjp
def _cumsum_rows(x):
    return _cumsum_rows_impl(x)


_cumsum_rows.defvjp(lambda x: (_cumsum_rows_impl(x), None),
                    lambda _, g: (jnp.sum(g, axis=0, keepdims=True) - _cumsum_rows_impl(g) + g,))


def _unit_lower_inverse_impl(n):
    eye = (lax.broadcasted_iota(jnp.int32, (1,) + n.shape[1:], 1)
           == lax.broadcasted_iota(jnp.int32, (1,) + n.shape[1:], 2))
    t = jnp.where(eye, 1.0, 0.0) + n
    pw = n
    for _ in range(5):
        pw = _bnn(pw, pw)
        t = t + _bnn(t, pw)
    return t


@jax.custom_vjp
def _unit_lower_inverse(n):
    return _unit_lower_inverse_impl(n)


def _unit_lower_inverse_fwd(n):
    t = _unit_lower_inverse_impl(n)
    return t, t


_unit_lower_inverse.defvjp(_unit_lower_inverse_fwd, lambda t, g: (_bnt(_btn(t, g), t),))


@jax.custom_vjp
def _known_inverse(n, t):
    return t


_known_inverse.defvjp(lambda n, t: (t, t), lambda t, g: (_bnt(_btn(t, g), t), jnp.zeros_like(t)))


def _scan_chunk(r, lw, k, v, a, b, s0, inverse):
    c = SCAN_CHUNK
    p = s0.shape[0]
    cum = _cumsum_rows(lw)
    tot = jnp.sum(lw, axis=0, keepdims=True)
    ma = (lax.broadcasted_iota(jnp.int32, (c, 128 * p), 1) & 127) < 64

    def pairs(x):
        return jnp.concatenate([x[None, :, 128 * j:128 * (j + 1)] for j in range(p)], axis=0)

    def stack(x):
        return jnp.concatenate([pairs(jnp.where(ma, x, 0.0)), pairs(jnp.where(ma, 0.0, x))], axis=1)

    einv, eend = jnp.exp(-cum), jnp.exp(tot - cum)
    ra, aa = stack(r * jnp.exp(cum)), stack(a * jnp.exp(cum - lw))
    bi, ki, be, ke, vs = stack(b * einv), stack(k * einv), stack(b * eend), stack(k * eend), stack(v)
    r2 = lax.broadcasted_iota(jnp.int32, (1, 2 * c, 2 * c), 1)
    c2 = lax.broadcasted_iota(jnp.int32, (1, 2 * c, 2 * c), 2)
    same = (r2 >= c) == (c2 >= c)
    strict = jnp.logical_and(same, c2 < r2)
    incl = jnp.logical_and(same, c2 <= r2)
    s0 = jnp.where(same, s0, 0.0)
    prod = _bnt(jnp.concatenate([aa, ra], axis=1), jnp.concatenate([bi, ki], axis=1))
    a_ab = jnp.where(strict, prod[:, :2 * c, :2 * c], 0.0)
    a_ak = jnp.where(strict, prod[:, :2 * c, 2 * c:], 0.0)
    a_rb = jnp.where(incl, prod[:, 2 * c:, :2 * c], 0.0)
    a_rk = jnp.where(incl, prod[:, 2 * c:, 2 * c:], 0.0)
    t = inverse(a_ab)
    u = _bnn(t, _bnt(aa, s0) + _bnn(a_ak, vs))
    uv = jnp.concatenate([u, vs], axis=1)
    ys = _bnt(ra, s0) + _bnn(jnp.concatenate([a_rb, a_rk], axis=2), uv)
    s1 = s0 * pairs(jnp.exp(tot)) + _btn(uv, jnp.concatenate([be, ke], axis=1))
    y3 = ys[:, :c] + ys[:, c:]
    return (jnp.concatenate([y3[j] for j in range(p)], axis=1), s1), t


def _scan_specs(rev, n):
    def at(i):
        return n - 1 - i if rev else i

    def cm(cb):
        return pl.BlockSpec((SCAN_CHUNK, D), lambda i: (at(i), cb))

    return cm, pl.BlockSpec((1, SCAN_PAIRS, 128, 128), lambda i: (at(i), 0, 0, 0))


def _comm_phases(comm, refs, n, step=None):
    k = comm.n
    srcs, outs, sems = refs[:k], refs[k:2 * k], refs[2 * k:]
    i = pl.program_id(0) if step is None else step

    def before():
        @pl.when(i == 0)
        def _():
            comm.first(srcs, outs, sems)

    def after():
        if comm.mid is not None:
            @pl.when(i == (3 * n) // 4)
            def _():
                comm.mid(srcs, outs, sems)

        @pl.when(i == n - 1)
        def _():
            comm.last(srcs, outs, sems)

    return before, after


def _scan_fwd(zs, lw, km, aa, bb, comm):
    s = zs.shape[0]
    n = s // SCAN_CHUNK
    cm, st = _scan_specs(False, n)
    k = comm.n

    def body(*refs):
        r_ref, lw_ref, k_ref, v_ref, a_ref, b_ref = refs[:6]
        y_ref, s0_ref, t_ref = refs[6 + k:9 + k]
        state = refs[9 + 2 * k]
        before, after = _comm_phases(comm, refs[6:6 + k] + refs[9 + k:9 + 2 * k] + refs[10 + 2 * k:], n)
        before()

        @pl.when(pl.program_id(0) == 0)
        def _():
            state[...] = jnp.zeros_like(state)

        s0 = state[...]
        s0_ref[0] = s0
        (y, s1), t = _scan_chunk(*[ref[...] for ref in (r_ref, lw_ref, k_ref, v_ref, a_ref, b_ref)], s0,
                                 _unit_lower_inverse)
        y_ref[...] = y
        t_ref[0] = t.astype(BF16)
        state[...] = s1
        after()

    per_chunk = (n, SCAN_PAIRS, 128, 128)
    res = _pcall(
        body, name="scan_fwd", grid=(n,), in_specs=[cm(0), cm(0), cm(0), cm(2), cm(0), cm(0)] + [_HBM] * k,
        out_specs=[cm(0), st, st] + [_HBM] * k,
        out_shape=[jax.ShapeDtypeStruct((s, D), F32), jax.ShapeDtypeStruct(per_chunk, F32),
                   jax.ShapeDtypeStruct(per_chunk, BF16)] + comm.out_shape,
        scratch_shapes=[pltpu.VMEM((SCAN_PAIRS, 128, 128), F32)] + comm.sems,
        compiler_params=_cparams(("arbitrary",)),
    )(zs, lw, km, zs, aa, bb, *comm.ins)
    return res[0], res[1], res[2], res[3:]


def _scan_bwd(zs, lw, km, aa, bb, s0s, ts, dy, comm):
    s = zs.shape[0]
    n = s // SCAN_CHUNK
    cm, st = _scan_specs(True, n)
    k = comm.n

    def body(*refs):
        r_ref, lw_ref, k_ref, v_ref, a_ref, b_ref, s0_ref, t_ref, dy_ref = refs[:9]
        douts = refs[9 + k:15 + k]
        dstate = refs[15 + 2 * k]
        before, after = _comm_phases(comm, refs[9:9 + k] + refs[15 + k:15 + 2 * k] + refs[16 + 2 * k:], n)
        before()

        @pl.when(pl.program_id(0) == 0)
        def _():
            dstate[...] = jnp.zeros_like(dstate)

        t = t_ref[0].astype(F32)
        prim = [ref[...] for ref in (r_ref, lw_ref, k_ref, v_ref, a_ref, b_ref)] + [s0_ref[0]]
        _, vjp, _ = jax.vjp(lambda *p: _scan_chunk(*p, lambda nil: _known_inverse(nil, t)), *prim, has_aux=True)
        grads = vjp((dy_ref[...], dstate[...]))
        for ref, gr in zip(douts, grads[:6]):
            ref[...] = gr
        dstate[...] = grads[6]
        after()

    res = _pcall(
        body, name="scan_bwd", grid=(n,),
        in_specs=[cm(0), cm(0), cm(0), cm(2), cm(0), cm(0), st, st, cm(0)] + [_HBM] * k,
        out_specs=[cm(0)] * 6 + [_HBM] * k, out_shape=[jax.ShapeDtypeStruct((s, D), F32)] * 6 + comm.out_shape,
        scratch_shapes=[pltpu.VMEM((SCAN_PAIRS, 128, 128), F32)] + comm.sems,
        compiler_params=_cparams(("arbitrary",)),
    )(zs, lw, km, zs, aa, bb, s0s, ts, dy, *comm.ins)
    return res[:6], res[6:]


_HBM = pl.BlockSpec(memory_space=pltpu.HBM)


def _me():
    return lax.axis_index("x"), lax.axis_index("y"), lax.axis_index("c")


def _allgather8(src, name):
    def body(src_ref, out_ref, ssem, rsem, lsem):
        x, y, c = _me()
        me = 4 * x + 2 * y + c
        local = pltpu.make_async_copy(src_ref, out_ref.at[me], lsem)
        local.start()
        peers = []
        for k in range(1, 8):
            peers.append(((1 - x) if k & 4 else x, (1 - y) if k & 2 else y, (1 - c) if k & 1 else c))
        sends = []
        for k, peer in enumerate(peers):
            cp = pltpu.make_async_remote_copy(src_ref, out_ref.at[me], ssem.at[k], rsem.at[k], device_id=peer,
                                              device_id_type=MESH)
            cp.start()
            sends.append(cp)
        for k, (px, py, pc) in enumerate(peers):
            pltpu.make_async_remote_copy(src_ref, out_ref.at[4 * px + 2 * py + pc], ssem.at[k], rsem.at[k],
                                         device_id=(px, py, pc), device_id_type=MESH).wait_recv()
        for cp in sends:
            cp.wait_send()
        local.wait()

    return _pcall(
        body, name=name, in_specs=[_HBM], out_specs=_HBM, out_shape=jax.ShapeDtypeStruct((8,) + src.shape, src.dtype),
        scratch_shapes=[pltpu.SemaphoreType.DMA((7,)), pltpu.SemaphoreType.DMA((7,)), pltpu.SemaphoreType.DMA],
    )(src)


def _other_chips(x, y):
    return [(1 - x, y), (x, 1 - y), (1 - x, 1 - y)]


def _remote(src, dst, ssem, rsem, to):
    return pltpu.make_async_remote_copy(src, dst, ssem, rsem, device_id=to, device_id_type=MESH)


class _GatherWeights:
    def __init__(self, shards):
        self.ins = list(shards)
        n = self.n = len(shards)
        self.out_shape = [jax.ShapeDtypeStruct((4,) + t.shape, t.dtype) for t in shards]
        self.sems = [pltpu.SemaphoreType.DMA((6 * n,)), pltpu.SemaphoreType.DMA((6 * n,)),
                     pltpu.SemaphoreType.DMA((n,)), pltpu.SemaphoreType.DMA((n,))]

    def _copies(self, srcs, outs, sems):
        ssem, rsem, lsem, osem = sems
        x, y, c = _me()
        me = 2 * x + y
        own, ici, landed, passed, passed_in = [], [], [], [], []
        for a in range(self.n):
            h = self.ins[a].shape[0] // 2
            mine, other = pl.ds(c * h, h), pl.ds((1 - c) * h, h)
            own.append(_remote(srcs[a], outs[a].at[me], lsem.at[a], osem.at[a], (x, y, 1 - c)))
            for k, (px, py) in enumerate(_other_chips(x, y)):
                s1, r1, s2, r2 = ssem.at[6 * a + k], rsem.at[6 * a + k], ssem.at[6 * a + 3 + k], rsem.at[6 * a + 3 + k]
                got, got_sib = outs[a].at[2 * px + py, mine], outs[a].at[2 * px + py, other]
                ici.append(_remote(srcs[a].at[mine], outs[a].at[me, mine], s1, r1, (px, py, c)))
                landed.append(_remote(got, got, s1, r1, (px, py, c)))
                passed.append(_remote(got, got, s2, r2, (x, y, 1 - c)))
                passed_in.append(_remote(got_sib, got_sib, s2, r2, (x, y, 1 - c)))
        return own, ici, landed, passed, passed_in

    def first(self, srcs, outs, sems):
        own, ici, _, _, _ = self._copies(srcs, outs, sems)
        for cp in own + ici:
            cp.start()

    def mid(self, srcs, outs, sems):
        _, _, landed, passed, _ = self._copies(srcs, outs, sems)
        for arrived, onward in zip(landed, passed):
            arrived.wait_recv()
            onward.start()

    def last(self, srcs, outs, sems):
        own, ici, _, passed, passed_in = self._copies(srcs, outs, sems)
        for cp in passed_in:
            cp.wait_recv()
        for cp in ici + passed:
            cp.wait_send()
        for cp in own:
            cp.wait()


class _ScatterToChips:
    def __init__(self, parts):
        self.ins = list(parts)
        n = self.n = len(parts)
        self.out_shape = [jax.ShapeDtypeStruct(t.shape, t.dtype) for t in parts]
        self.sems = [pltpu.SemaphoreType.DMA((3 * n,)), pltpu.SemaphoreType.DMA((3 * n,)), pltpu.SemaphoreType.DMA((n,))]

    def _copies(self, srcs, outs, sems):
        ssem, rsem, lsem = sems
        x, y, c = _me()
        me = 2 * x + y
        own, out, landed = [], [], []
        for a in range(self.n):
            own.append(pltpu.make_async_copy(srcs[a].at[me], outs[a].at[me], lsem.at[a]))
            for k, (px, py) in enumerate(_other_chips(x, y)):
                dst = outs[a].at[2 * px + py]
                out.append(_remote(srcs[a].at[2 * px + py], outs[a].at[me], ssem.at[3 * a + k], rsem.at[3 * a + k],
                                   (px, py, c)))
                landed.append(_remote(dst, dst, ssem.at[3 * a + k], rsem.at[3 * a + k], (px, py, c)))
        return own, out, landed

    def first(self, srcs, outs, sems):
        own, out, _ = self._copies(srcs, outs, sems)
        for cp in own + out:
            cp.start()

    mid = None

    def last(self, srcs, outs, sems):
        own, out, landed = self._copies(srcs, outs, sems)
        for cp in landed:
            cp.wait_recv()
        for cp in own:
            cp.wait()
        for cp in out:
            cp.wait_send()


def _run_comm(comm, name):
    n = comm.n

    def body(*refs):
        srcs, outs, sems = refs[:n], refs[n:2 * n], refs[2 * n:]
        comm.first(srcs, outs, sems)
        if comm.mid is not None:
            comm.mid(srcs, outs, sems)
        comm.last(srcs, outs, sems)

    return _pcall(body, name=name, in_specs=[_HBM] * n, out_specs=[_HBM] * n, out_shape=comm.out_shape,
                  scratch_shapes=comm.sems)(*comm.ins)


class _NoComm:
    n, ins, out_shape, sems, mid = 0, [], [], [], None

    def first(self, srcs, outs, sems):
        pass

    def last(self, srcs, outs, sems):
        pass


_NOTHING = _NoComm()


class _SiblingHalves:
    mid = None

    def __init__(self, grads):
        self.ins = list(grads)
        n = self.n = len(grads)
        self.out_shape = [jax.ShapeDtypeStruct((4, t.shape[1] // 2, t.shape[2]), t.dtype) for t in grads]
        self.sems = [pltpu.SemaphoreType.DMA((n,)), pltpu.SemaphoreType.DMA((n,))]

    def _copies(self, srcs, outs, sems):
        ssem, rsem = sems
        x, y, c = _me()
        copies = []
        for a in range(self.n):
            h = self.ins[a].shape[1] // 2
            copies.append(_remote(srcs[a].at[:, pl.ds((1 - c) * h, h)], outs[a], ssem.at[a], rsem.at[a], (x, y, 1 - c)))
        return copies

    def first(self, srcs, outs, sems):
        for cp in self._copies(srcs, outs, sems):
            cp.start()

    def last(self, srcs, outs, sems):
        for cp in self._copies(srcs, outs, sems):
            cp.wait()


def _reduce_finish(reds, name):
    n = len(reds)

    def body(*refs):
        outs = refs[n:2 * n]
        ssem, rsem = refs[2 * n:]
        x, y, c = _me()
        copies = []
        for a in range(n):
            h = reds[a].shape[0] // 2
            mine = outs[a].at[pl.ds(c * h, h)]
            copies.append(_remote(mine, mine, ssem.at[a], rsem.at[a], (x, y, 1 - c)))
        for cp in copies:
            cp.start()
        for a in range(n):
            h = reds[a].shape[0] // 2
            dst = outs[a].at[pl.ds((1 - c) * h, h)]
            _remote(dst, dst, ssem.at[a], rsem.at[a], (x, y, 1 - c)).wait_recv()
        for cp in copies:
            cp.wait_send()

    return _pcall(
        body, name=name, in_specs=[_HBM] * n, out_specs=[_HBM] * n,
        out_shape=[jax.ShapeDtypeStruct(t.shape, t.dtype) for t in reds],
        input_output_aliases={a: a for a in range(n)},
        scratch_shapes=[pltpu.SemaphoreType.DMA((n,)), pltpu.SemaphoreType.DMA((n,))],
    )(*reds)


def _half_sum(fn, full, halves, out_full, out_dtype, core, name):
    p, h, c = (halves[0].shape if halves else (full[0].shape[0], full[0].shape[1] // 2, full[0].shape[2]))
    br = _div(h, max(16, (1 << 19) // (p * c)), 16)
    nb = h // br
    mine3 = pl.BlockSpec((p, br, c), lambda i, core_ref: (0, core_ref[0] * nb + i, 0))
    half3 = pl.BlockSpec((p, br, c), lambda i, core_ref: (0, i, 0))

    def body(core_ref, *refs):
        refs[-1][...] = fn(*[t[...].astype(F32) for t in refs[:-1]]).astype(out_dtype)

    if out_full:
        out_spec = pl.BlockSpec((br, c), lambda i, core_ref: (core_ref[0] * nb + i, 0))
        out_shape = jax.ShapeDtypeStruct((2 * h, c), out_dtype)
    else:
        out_spec, out_shape = half3, jax.ShapeDtypeStruct((p, h, c), out_dtype)
    return _pcall(
        body, name=name,
        grid_spec=pltpu.PrefetchScalarGridSpec(
            num_scalar_prefetch=1, grid=(nb,), in_specs=[mine3] * len(full) + [half3] * len(halves),
            out_specs=out_spec),
        out_shape=out_shape, compiler_params=_cparams(("parallel",)),
    )(core, *full, *halves)


def _ada_fwd(c_all, w, b):
    def body(c_ref, w_ref, b_ref, o_ref):
        o_ref[...] = jnp.dot(c_ref[...], w_ref[...], precision=HI, preferred_element_type=F32) + b_ref[...]

    return _pcall(body, name="ada_fwd", out_shape=jax.ShapeDtypeStruct((c_all.shape[0], w.shape[1]), F32),
                  compiler_params=pltpu.CompilerParams(vmem_limit_bytes=VMEM_LIMIT))(c_all, w, b)


def _ada_bwd(c_all_t, d):
    def body(c_ref, d_ref, o_ref):
        o_ref[...] = jnp.dot(c_ref[...], d_ref[...], precision=HI, preferred_element_type=F32)

    return _pcall(body, name="ada_bwd", out_shape=jax.ShapeDtypeStruct((c_all_t.shape[0], d.shape[1]), F32),
                  compiler_params=pltpu.CompilerParams(vmem_limit_bytes=VMEM_LIMIT))(c_all_t, d)


def _sum_lead(x, name):
    p, r, n = x.shape
    br = _div(r, 512, 8)

    def body(x_ref, o_ref):
        acc = x_ref[0]
        for j in range(1, p):
            acc = acc + x_ref[j]
        o_ref[...] = acc

    return _pcall(
        body, name=name, grid=(r // br,), in_specs=[pl.BlockSpec((p, br, n), lambda i: (0, i, 0))],
        out_specs=pl.BlockSpec((br, n), lambda i: (i, 0)), out_shape=jax.ShapeDtypeStruct((r, n), F32),
        compiler_params=_cparams(("parallel",)),
    )(x)


def _adamw(w, g, m, v, name):
    shape = w.shape
    cols = shape[-1]
    w2, g2, m2, v2 = [t.reshape(-1, cols) for t in (w, g, m, v)]
    rows = w2.shape[0]
    pref = max(8, (1 << 19) // cols // 8 * 8)
    br = _div(rows, pref, 8)
    if rows // br > 64:
        br = pref
    outs = _rows_fwd(_f_adamw, [(t, cols, 0) for t in (w2, g2, m2, v2)], [], [(cols, F32)] * 3, name=name, br=br)
    return [o.reshape(shape) for o in outs]


_BIG = (("w_in", 1), ("w_up", 1), ("w_down", 0), ("w_o", 0), ("w_rwkv_out", 0), ("w_att_out", 1), ("w2", 1), ("a2", 1),
        ("g2", 1))


_NEEDED_FIRST = ("w_in", "w_att_out", "w2", "a2", "g2")
_NEEDED_LATER = ("w_up", "w_down", "w_o", "w_rwkv_out")
_DONE_EARLY = ("w_up", "w_down", "w_o", "w_rwkv_out", "w_att_out")
_DONE_LATE = ("w_in", "w2", "a2", "g2")


def _cols_joined(t):
    return jnp.concatenate([t[j] for j in range(4)], axis=1)


def _cols_split(t):
    n = t.shape[1] // 4
    return jnp.stack([t[:, j * n:(j + 1) * n] for j in range(4)])


W_IN_SHARD = (N_ATT + N_RW + N_GATE) // 4
W_IN_PAD = 2560


def _row_window(parts, lo, hi):
    out, pos = [], 0
    for t, w in parts:
        a, b = max(lo, pos), min(hi, pos + w)
        if a < b:
            out.append(t[a - pos:b - pos])
        pos += w
    return out[0] if len(out) == 1 else jnp.concatenate(out, axis=0)


def _rows_joined(t):
    return t.reshape(4 * t.shape[1], t.shape[2])


def _rows_split(t):
    return t.reshape(4, t.shape[0] // 4, t.shape[1])


def _step_to_scan(x, tgt, ada, wts):
    sh1, sc1, gt1, sh2, sc2, gt2 = ada
    br = 256
    grp = lax.broadcasted_iota(jnp.int32, (D, 128), 0) // 64 == lax.broadcasted_iota(jnp.int32, (D, 128), 1)
    e = grp.astype(F32)
    et = e.T
    w_in = [(wts["w_in"][j], W_IN_SHARD) for j in range(4)]
    w_att = _row_window(w_in, 0, N_ATT)
    w_rw = jnp.concatenate([_row_window(w_in, N_ATT, N_ATT + N_RW), jnp.zeros((N_RWP - N_RW, D), BF16)], axis=0)
    w_gate = _row_window(w_in, N_ATT + N_RW, N_ATT + N_RW + N_GATE)
    mu = jnp.pad(wts["mu_shift"], ((0, 0), (0, N_RWP - N_RW)))
    wl = jnp.zeros((N_LORA, 3 * D), F32)
    wl = wl.at[0:64, 0:D].set(_cols_joined(wts["w2"]).astype(F32))
    wl = wl.at[64:128, D:2 * D].set(_cols_joined(wts["a2"]).astype(F32))
    wl = wl.at[128:288, 2 * D:3 * D].set(_cols_joined(wts["g2"]).astype(F32))
    pre1_c = [wts["norm1_w"], sc1, sh1]
    (h1,) = _rows_fwd(_f_pre, [(x, D, 0)], pre1_c, [(D, BF16), None], name="pre1_fwd", br=2 * br)
    att_in = _mm(h1, w_att, tb=True, name="mm_att_in")
    z = _mm(h1, w_rw, tb=True, out_dtype=BF16, name="mm_rw_in")
    gate_in = _mm(h1, w_gate, tb=True, out_dtype=BF16, name="mm_gate_in")
    att_o, att_l = [], []
    for g, (_, dil) in enumerate(ATT_PATTERNS):
        o, l = _att_fwd(att_in, g, dil)
        att_o.append(o)
        att_l.append(l)
    comb_rows = [(t, ATT_WIDTH, 0) for t in att_o + att_l]
    (att,) = _rows_fwd(_f_comb, comb_rows, [], [(ATT_WIDTH, BF16)], name="comb_fwd", br=2 * br)
    w_ao = _cols_joined(wts["w_att_out"])
    y_att = _mm(att, w_ao, out_dtype=BF16, name="mm_att_out")
    rwpre_c = [wts["w0"], wts["a0"], wts["k_k"], wts["k_a"], wl, e, et]

    def shift_and_rwpre(zz, *rest):
        consts, mu_row, before = rest[:-2], rest[-2], rest[-1]
        last = jnp.sum(jnp.where(lax.broadcasted_iota(jnp.int32, before.shape, 0) == HALO - 1, before, 0.0), axis=0,
                       keepdims=True)
        row = lax.broadcasted_iota(jnp.int32, zz.shape, 0)
        zprev = jnp.where(row == 0, last, pltpu.roll(zz, 1, 0))
        shifted = zz + (zprev - zz) * mu_row
        return (shifted,) + tuple(_f_rwpre(shifted, *consts))

    zs, lw, km, aa, bb, gg = _rows_fwd(
        shift_and_rwpre, [(z, N_RWP, 0)], rwpre_c + [mu],
        [(N_RWP, F32), None, (D, F32), (D, F32), None, (D, F32), (D, F32), (D, F32)], name="rwpre_fwd", br=br, halo=0)
    return dict(x=x, tgt=tgt, wts=wts, br=br, e=e, et=et, gt1=gt1, sc2=sc2, sh2=sh2, gt2=gt2, w_att=w_att, w_rw=w_rw,
                w_ao=w_ao,
                w_gate=w_gate, mu=mu, pre1_c=pre1_c, h1=h1, att_in=att_in, z=z, gate_in=gate_in, comb_rows=comb_rows,
                att=att, y_att=y_att, zs=zs, rwpre_c=rwpre_c, lw=lw, km=km, aa=aa, bb=bb, gg=gg)


def _step_between_scans(st, y_raw, late):
    x, tgt, wts, br, e, et = st["x"], st["tgt"], st["wts"], st["br"], st["e"], st["et"]
    zs, km, gg, gate_in, y_att, att = st["zs"], st["km"], st["gg"], st["gate_in"], st["y_att"], st["att"]
    comb_rows, att_in = st["comb_rows"], st["att_in"]
    gt1, sc2, sh2, gt2 = st["gt1"], st["sc2"], st["sh2"], st["gt2"]
    w_up, w_ao = late["w_up"], st["w_ao"]
    w_down, w_o, w_ro = _rows_joined(late["w_down"]), _rows_joined(late["w_o"]), _rows_joined(late["w_rwkv_out"])
    post_rows = [(y_raw, D, 0), (zs, D, 0), (zs, D, 2), (km, D, 0), (gg, D, 0)]
    post_c = [wts["lnx_w"], wts["lnx_b"], wts["r_k"], e, et]
    (rw_out,) = _rows_fwd(_f_rwpost, post_rows, post_c, [(D, BF16)], name="rwpost_fwd", br=br)
    y_rw = _mm(rw_out, w_ro, out_dtype=BF16, name="mm_rw_out")
    mix_rows = [(gate_in, N_GATE, 0), (y_att, D, 0), (y_rw, D, 0)]
    (mix,) = _rows_fwd(_f_mix, mix_rows, [wts["b_gate"]], [(D, BF16)], name="mix_fwd", br=2 * br)
    o = _mm(mix, w_o, out_dtype=BF16, name="mm_o")
    pre2_c = [gt1, wts["norm2_w"], sc2, sh2]
    x1, h2 = _rows_fwd(_f_pre2, [(x, D, 0), (o, D, 0)], pre2_c, [(D, F32), (D, BF16)], name="pre2_fwd", br=2 * br)
    u = _mm(h2, w_up, b_chip=True, name="mm_up")
    act = _conv_fwd(u, wts["conv_w"], wts["conv_b"])
    f = _mm(act, w_down, out_dtype=BF16, name="mm_down")
    fin_rows = [(x1, D, 0), (f, D, 0), (tgt, D, 0)]
    fin_c = [gt2, wts["norm_f_w"]]

    def fin_fwd(*a):
        (l,) = _f_fin(*a)
        return (jnp.broadcast_to(jnp.sum(l, axis=0, keepdims=True), (8, 128)),)

    (loss_acc,) = _rows_fwd(fin_fwd, fin_rows, fin_c, [], name="fin_fwd", br=2 * br, acc_shape=(8, 128))

    gw = {}
    dx1a, df, d_gt2, gw["norm_f_w"] = _rows_bwd(
        _f_fin, fin_rows, fin_c, [[]], wrt_rows=[0, 1], wrt_consts=[0, 1], drow_dtypes=[F32, BF16],
        name="fin_bwd", br=2 * br, unit_cot=True)
    dact = _mm(df, w_down, tb=True, name="mm_dact")
    gw["w_down"] = _rows_split(_mm(act, df, ta=True, out_dtype=BF16, name="mm_dw_down"))
    du, gw["conv_w"], gw["conv_b"] = _conv_bwd(u, wts["conv_w"], wts["conv_b"], dact)
    dh2 = _mm(du, w_up, tb=True, b_chip=True, out_dtype=BF16, name="mm_dh2")
    gw["w_up"] = _mm(h2, du, ta=True, out_chip=True, out_dtype=BF16, name="mm_dw_up")
    dxa, do, d_gt1, gw["norm2_w"], d_sc2, d_sh2 = _rows_bwd(
        _f_pre2, [(x, D, 0), (o, D, 0)], pre2_c, [[(dx1a, D, 0)], [(dh2, D, 0)]], wrt_rows=[0, 1],
        wrt_consts=[0, 1, 2, 3], drow_dtypes=[F32, BF16], name="pre2_bwd", br=2 * br)
    dmix = _mm(do, w_o, tb=True, out_dtype=BF16, name="mm_dmix")
    gw["w_o"] = _rows_split(_mm(mix, do, ta=True, out_dtype=BF16, name="mm_dw_o"))
    dgate, dya, dyr, gw["b_gate"] = _rows_bwd(
        _f_mix, mix_rows, [wts["b_gate"]], [[(dmix, D, 0)]], wrt_rows=[0, 1, 2], wrt_consts=[0],
        drow_dtypes=[BF16] * 3, name="mix_bwd", br=2 * br)
    datt = _mm(dya, w_ao, tb=True, out_dtype=BF16, name="mm_datt")
    gw["w_att_out"] = _mm(att, dya, ta=True, out_chip=True, out_dtype=BF16, name="mm_dw_att_out")
    drw = _mm(dyr, w_ro, tb=True, out_dtype=BF16, name="mm_drw")
    gw["w_rwkv_out"] = _rows_split(_mm(rw_out, dyr, ta=True, out_dtype=BF16, name="mm_dw_rw_out"))
    dcomb = _rows_bwd(_f_comb, comb_rows, [], [[(datt, ATT_WIDTH, 0)]], wrt_rows=list(range(6)), wrt_consts=[],
                      drow_dtypes=[F32] * 6, name="comb_bwd", br=2 * br)
    datt_in = None
    for g, (_, dil) in enumerate(ATT_PATTERNS):
        datt_in = _att_bwd(att_in, g, dil, dcomb[g], dcomb[3 + g], datt_in)
    dy_raw, dr_p, dv_p, dkm_p, dgg, gw["lnx_w"], gw["lnx_b"], gw["r_k"], *recv_early = _rows_bwd(
        _f_rwpost, post_rows, post_c, [[(drw, D, 0)]], wrt_rows=[0, 1, 2, 3, 4], wrt_consts=[0, 1, 2],
        drow_dtypes=[F32] * 5, name="rwpost_bwd", br=br, comm=_SiblingHalves([gw[n] for n in _DONE_EARLY]))
    st.update(loss=loss_acc[0, 0], gw=gw, dxa=dxa, dgate=dgate, datt_in=datt_in,
              dy_raw=dy_raw, dr_p=dr_p, dv_p=dv_p, dkm_p=dkm_p, dgg=dgg, d_ada_late=(d_gt1, d_sh2, d_sc2, d_gt2),
              recv_early=recv_early)
    return st


def _chip_parts(grads, recv, names, core):
    return [_half_sum(lambda a, b: a + b, [g], [r], False, BF16, core, "reduce_add2_" + n)
            for g, r, n in zip(grads, recv, names)]


def _step_after_scan(st, scan_grads, core):
    x, br, gw, h1, zs = st["x"], st["br"], st["gw"], st["h1"], st["zs"]
    dr_s, dlw, dkm_s, dv_s, daa, dbb = scan_grads
    pre_cots = [[(st["dr_p"], D, 0), (dr_s, D, 0)], [(dlw, D, 0)], [(st["dkm_p"], D, 0), (dkm_s, D, 0)],
                [(st["dv_p"], D, 0), (dv_s, D, 0)], [(daa, D, 0)], [(dbb, D, 0)], [(st["dgg"], D, 0)]]
    dz, dmu, gw["w0"], gw["a0"], gw["k_k"], gw["k_a"], dwl = _rwpre_shift_bwd(
        zs, st["z"], st["mu"], st["rwpre_c"], pre_cots, br=128)
    gw["w2"], gw["a2"] = _cols_split(dwl[0:64, 0:D]), _cols_split(dwl[64:128, D:2 * D])
    gw["g2"] = _cols_split(dwl[128:288, 2 * D:3 * D])
    gw["mu_shift"] = dmu[:, :N_RW]
    datt_in, dgate = st["datt_in"], st["dgate"]
    dw_in = [(_mm(datt_in, h1, ta=True, out_dtype=BF16, name="mm_dw_att"), N_ATT),
             (_mm(dz, h1, ta=True, out_dtype=BF16, name="mm_dw_rw"), N_RW),
             (_mm(dgate, h1, ta=True, out_dtype=BF16, name="mm_dw_gate"), N_GATE)]
    slabs = []
    for j in range(4):
        slabs += [_row_window(dw_in, j * W_IN_SHARD, (j + 1) * W_IN_SHARD), jnp.zeros((W_IN_PAD - W_IN_SHARD, D), BF16)]
    gw["w_in"] = jnp.concatenate(slabs, axis=0).reshape(4, W_IN_PAD, D)
    late = [gw[n] for n in _DONE_LATE]
    parts = _chip_parts(late, _run_comm(_SiblingHalves(late), "reduce_sib_late"), _DONE_LATE, core)
    dh1, slots_late = _mm_sum([(datt_in, st["w_att"]), (dz, st["w_rw"]), (dgate, st["w_gate"])],
                              comm=_ScatterToChips(parts), name="mm_dh1")
    grad_x, gw["norm1_w"], d_sc1, d_sh1 = _rows_bwd(
        _f_pre, [(x, D, 0)], st["pre1_c"], [[(dh1, D, 0)], [(st["dxa"], D, 0)]], wrt_rows=[0], wrt_consts=[0, 1, 2],
        drow_dtypes=[F32], name="pre1_bwd", br=2 * br)
    d_gt1, d_sh2, d_sc2, d_gt2 = st["d_ada_late"]
    return st["loss"], grad_x, (d_sh1, d_sc1, d_gt1, d_sh2, d_sc2, d_gt2), gw, slots_late


_SMALL = ("b_ada", "norm1_w", "b_gate", "mu_shift", "w0", "a0", "k_k", "k_a", "r_k", "lnx_w", "lnx_b", "norm2_w",
          "conv_b", "norm_f_w")
_NAMES = ("w_ada", "b_ada", "norm1_w", "w_in", "b_gate", "mu_shift", "w0", "w2", "a0", "a2", "g2", "k_k", "k_a", "r_k",
          "lnx_w", "lnx_b", "w_att_out", "w_rwkv_out", "w_o", "norm2_w", "w_up", "conv_w", "conv_b", "w_down",
          "norm_f_w")


def kernel(x, c, w_ada, b_ada, norm1_w, w_in, b_gate, mu_shift, w0, w2, a0, a2, g2, k_k, k_a, r_k, lnx_w, lnx_b, w_att_out, w_rwkv_out, w_o, norm2_w, w_up, conv_w, conv_b, w_down, norm_f_w, loss_target, m_w_ada, m_b_ada, m_norm1_w, m_w_in, m_b_gate, m_mu_shift, m_w0, m_w2, m_a0, m_a2, m_g2, m_k_k, m_k_a, m_r_k, m_lnx_w, m_lnx_b, m_w_att_out, m_w_rwkv_out, m_w_o, m_norm2_w, m_w_up, m_conv_w, m_conv_b, m_w_down, m_norm_f_w, v_w_ada, v_b_ada, v_norm1_w, v_w_in, v_b_gate, v_mu_shift, v_w0, v_w2, v_a0, v_a2, v_g2, v_k_k, v_k_a, v_r_k, v_lnx_w, v_lnx_b, v_w_att_out, v_w_rwkv_out, v_w_o, v_norm2_w, v_w_up, v_conv_w, v_conv_b, v_w_down, v_norm_f_w):
    args = dict(locals())
    p, pm, pv = {}, {}, {}
    for name in _NAMES:
        for dst, key in ((p, name), (pm, "m_" + name), (pv, "v_" + name)):
            t = args[key]
            if name == "w_in":
                dst[name] = jnp.swapaxes(t, 1, 2)[0]
            else:
                dst[name] = t.reshape(1, -1) if name in ("r_k", "norm_f_w") else t.reshape(t.shape[-2], t.shape[-1])
    xi, yi, ci = _me()
    chip = 2 * xi + yi
    dev = 4 * xi + 2 * yi + ci
    x2, tgt = x[0], loss_target[0]

    n_cw = 3 * (2 * D_FF // 4)
    vec = jnp.concatenate([c.reshape(-1), p["conv_w"].reshape(-1), jnp.zeros((8 * D - D - n_cw,), F32)]).reshape(8, D)
    g0 = _allgather8(vec, "gather_c").reshape(8, 8 * D)
    c_all = g0[:, :D]
    conv_w_full = jnp.concatenate([g0[2 * j, D:D + n_cw].reshape(3, -1) for j in range(4)], axis=1)
    n_ada = 6 * D // 4
    b_ada_sh = lax.dynamic_slice(p["b_ada"], (0, chip * n_ada), (1, n_ada))
    ada_sh = _ada_fwd(c_all, p["w_ada"], b_ada_sh)
    ga = _allgather8(ada_sh, "gather_ada")
    ada_all = jnp.concatenate([ga[2 * j] for j in range(4)], axis=1)
    ada_row = lax.dynamic_slice(ada_all, (dev, 0), (1, 6 * D))
    ada = [ada_row[:, j * D:(j + 1) * D] for j in range(6)]

    big = [n for n, _ in _BIG]
    shard = {n: p[n].astype(BF16) for n in big}
    shard["w_in"] = jnp.pad(shard["w_in"], ((0, W_IN_PAD - W_IN_SHARD), (0, 0)))
    wts = dict(zip(_NEEDED_FIRST, _run_comm(_GatherWeights([shard[n] for n in _NEEDED_FIRST]), "gather_w")))
    for n in _SMALL:
        wts[n] = p[n]
    wts["conv_w"] = conv_w_full
    core = ci.reshape(1).astype(jnp.int32)

    st = _step_to_scan(x2, tgt, ada, wts)
    y_raw, s0s, inverses, late = _scan_fwd(st["zs"], st["lw"], st["km"], st["aa"], st["bb"],
                                           _GatherWeights([shard[n] for n in _NEEDED_LATER]))
    st = _step_between_scans(st, y_raw, dict(zip(_NEEDED_LATER, late)))
    early = _chip_parts([st["gw"][n] for n in _DONE_EARLY], st["recv_early"], _DONE_EARLY, core)
    scan_grads, slots_early = _scan_bwd(st["zs"], st["lw"], st["km"], st["aa"], st["bb"], s0s, inverses,
                                        st["dy_raw"], _ScatterToChips(early))
    loss_part, grad_x, d_ada, gw, slots_late = _step_after_scan(st, scan_grads, core)

    small = [jnp.concatenate(d_ada, axis=1)] + [gw[n] for n in _SMALL[1:]] + [gw["conv_w"], loss_part.reshape(1, 1)]
    sizes = [t.size for t in small]
    flat = jnp.concatenate([t.reshape(-1) for t in small])
    npad = (-flat.shape[0]) % (8 * D)
    srows = (flat.shape[0] + npad) // D
    flat = jnp.concatenate([flat, jnp.zeros((npad,), F32)]).reshape(srows, D)
    parts = _allgather8(flat, "gather_small")
    tot = _sum_lead(parts, "sum_small").reshape(-1)
    pieces, pos = [], 0
    for sz in sizes:
        pieces.append(tot[pos:pos + sz])
        pos += sz
    grads = {}
    for n, piece in zip(_SMALL, pieces[:len(_SMALL)]):
        grads[n] = piece.reshape(p[n].shape)
    conv_w_grad = pieces[len(_SMALL)].reshape(3, 2 * D_FF)
    grads["conv_w"] = lax.dynamic_slice(conv_w_grad, (0, chip * (n_cw // 3)), (3, n_cw // 3))
    loss = pieces[-1][0]
    d_ada_all = parts[:, :6].reshape(8, 6 * D)
    grads["w_ada"] = _ada_bwd(c_all.T, lax.dynamic_slice(d_ada_all, (0, chip * n_ada), (8, n_ada)))

    order = _DONE_EARLY + _DONE_LATE
    reds = [_half_sum(lambda t: t[0] + t[1] + t[2] + t[3], [], [t], True, F32, core, "reduce_add4_" + n)
            for n, t in zip(order, list(slots_early) + list(slots_late))]
    for n, g in zip(order, _reduce_finish(reds, "reduce_sib2")):
        grads[n] = g

    outs_g, outs_d, outs_m, outs_v = [], [], [], []
    grads["w_in"] = grads["w_in"][:W_IN_SHARD]
    for name in _NAMES:
        g = grads[name]
        d, m, v = _adamw(p[name], g, pm[name], pv[name], "adamw_" + name)
        shape = args[name].shape
        for outs, t in ((outs_g, g), (outs_d, d), (outs_m, m), (outs_v, v)):
            outs.append(jnp.swapaxes(t[None], 1, 2) if name == "w_in" else t.reshape(shape))
    return (loss, grad_x.reshape(x.shape), *outs_g, *outs_d, *outs_m, *outs_v)
```

```python
import functools
import math

import jax
import jax.numpy as jnp
from jax import lax
from jax.experimental import pallas as pl
from jax.experimental.pallas import tpu as pltpu

F32 = jnp.float32
BF16 = jnp.bfloat16
HI = lax.Precision.HIGHEST
MESH = pl.DeviceIdType.MESH

D = 1024
ATT_PATTERNS = ((128, 1), (512, 4), (2048, 16))
ATT_BLOCK = 128
ATT_WIDTH = 512
N_ATT = 3 * 3 * ATT_WIDTH
N_RW = 3 * D + 64 + 64 + 160
N_RWP = 3456
N_LORA = N_RWP - 3 * D
N_GATE = 2 * D
D_FF = 2816
RMS_EPS = 1e-6
GN_EPS = 64e-5
SCAN_CHUNK = 64
SCAN_PAIRS = 8
NEG = -1e30
VMEM_LIMIT = 48 * 1024 * 1024
HALO = 16

ADAM_LR, ADAM_B1, ADAM_B2, ADAM_EPS, ADAM_WD, ADAM_STEP = 0.001, 0.9, 0.999, 1e-08, 0.01, 10


def _pcall(body, **kw):
    call = pl.pallas_call(body, **kw)

    def run(*args):
        return call(*[pltpu.with_memory_space_constraint(a, pltpu.HBM) if jnp.issubdtype(a.dtype, jnp.floating)
                      else a for a in args])

    return run


def _cparams(sem):
    return pltpu.CompilerParams(dimension_semantics=sem, vmem_limit_bytes=VMEM_LIMIT)


def _div(n, pref, mult):
    best = None
    d = mult
    while d <= min(n, pref):
        if n % d == 0:
            best = d
        d += mult
    return best if best else n


def _dg(a, b, ca, cb):
    return lax.dot_general(a.astype(BF16), b.astype(BF16), (((ca,), (cb,)), ((), ())), preferred_element_type=F32)


@jax.custom_vjp
def _nn(a, b):
    return _dg(a, b, 1, 0)


@jax.custom_vjp
def _nt(a, b):
    return _dg(a, b, 1, 1)


@jax.custom_vjp
def _tn(a, b):
    return _dg(a, b, 0, 0)


_nn.defvjp(lambda a, b: (_nn(a, b), (a, b)), lambda res, g: (_nt(g, res[1]), _tn(res[0], g)))
_nt.defvjp(lambda a, b: (_nt(a, b), (a, b)), lambda res, g: (_nn(g, res[1]), _tn(g, res[0])))
_tn.defvjp(lambda a, b: (_tn(a, b), (a, b)), lambda res, g: (_nt(res[1], g), _nn(res[0], g)))


def _bdg(a, b, ca, cb):
    return lax.dot_general(a.astype(BF16), b.astype(BF16), (((ca,), (cb,)), ((0,), (0,))), preferred_element_type=F32)


@jax.custom_vjp
def _bnn(a, b):
    return _bdg(a, b, 2, 1)


@jax.custom_vjp
def _bnt(a, b):
    return _bdg(a, b, 2, 2)


@jax.custom_vjp
def _btn(a, b):
    return _bdg(a, b, 1, 1)


_bnn.defvjp(lambda a, b: (_bnn(a, b), (a, b)), lambda res, g: (_bnt(g, res[1]), _btn(res[0], g)))
_bnt.defvjp(lambda a, b: (_bnt(a, b), (a, b)), lambda res, g: (_bnn(g, res[1]), _btn(g, res[0])))
_btn.defvjp(lambda a, b: (_btn(a, b), (a, b)), lambda res, g: (_bnt(res[1], g), _bnn(res[0], g)))


def _hsum_impl(x, e, et):
    eb, etb = e.astype(BF16), et.astype(BF16)
    s = jnp.dot(x.astype(BF16), eb, preferred_element_type=F32)
    return jnp.dot(s.astype(BF16), etb, preferred_element_type=F32)


@jax.custom_vjp
def _hsum(x, e, et):
    return _hsum_impl(x, e, et)


_hsum.defvjp(lambda x, e, et: (_hsum_impl(x, e, et), (e, et)),
             lambda res, g: (_hsum_impl(g, res[0], res[1]), jnp.zeros_like(res[0]), jnp.zeros_like(res[1])))


def _mm(a, b, *, ta=False, tb=False, out_dtype=F32, add=None, b_chip=False, out_chip=False, comm=None, name):
    riding = _NOTHING if comm is None else comm
    nc = riding.n
    if ta:
        kdim, m = a.shape
    else:
        m, kdim = a.shape
    if b_chip:
        n = b.shape[1] if tb else 4 * b.shape[2]
    else:
        n = b.shape[0] if tb else b.shape[1]
    tm, tn, tk = _div(m, 1536, 128), _div(n, 1536, 128), _div(kdim, 2048 if ta else 1408, 128)
    if b_chip and tb:
        tk = kdim // 4
    if (b_chip and not tb) or out_chip:
        tn = n // 4
    nk = kdim // tk
    ca, cb = (0 if ta else 1), (1 if tb else 0)

    nin = 2 if add is None else 3
    gi, gj = m // tm, n // tn

    def body(*refs):
        a_ref, b_ref = refs[0], refs[1]
        add_ref = None if add is None else refs[2]
        o_ref = refs[nin + nc]
        step = (pl.program_id(0) * gj + pl.program_id(1)) * nk + pl.program_id(2)
        before, after = _comm_phases(riding, refs[nin:nin + nc] + refs[nin + nc + 1:nin + 2 * nc + 1]
                                     + refs[nin + 2 * nc + 1 + (nk > 1):], gi * gj * nk, step)
        before()
        part = lax.dot_general(a_ref[...], b_ref[...], (((ca,), (cb,)), ((), ())), preferred_element_type=F32)

        def finish(r):
            if add_ref is not None:
                r = r + add_ref[...]
            o_ref[...] = r.astype(o_ref.dtype)

        if nk == 1:
            finish(part)
            after()
            return
        acc = refs[nin + 2 * nc + 1]
        k = pl.program_id(2)

        @pl.when(k == 0)
        def _():
            acc[...] = part

        @pl.when(k > 0)
        def _():
            acc[...] += part

        @pl.when(k == nk - 1)
        def _():
            finish(acc[...])

        after()

    a_spec = pl.BlockSpec((tk, tm), lambda i, j, k: (k, i)) if ta else pl.BlockSpec((tm, tk), lambda i, j, k: (i, k))
    if b_chip:
        b_spec = (pl.BlockSpec((None, tn, tk), lambda i, j, k: (k, j, 0)) if tb
                  else pl.BlockSpec((None, tk, tn), lambda i, j, k: (j, k, 0)))
    else:
        b_spec = pl.BlockSpec((tn, tk), lambda i, j, k: (j, k)) if tb else pl.BlockSpec((tk, tn), lambda i, j, k: (k, j))
    in_specs = [a_spec, b_spec]
    args = [a, b]
    if add is not None:
        in_specs.append(pl.BlockSpec((tm, tn), lambda i, j, k: (i, j)))
        args.append(add)
    if out_chip:
        out_spec = pl.BlockSpec((None, tm, tn), lambda i, j, k: (j, i, 0))
        out_shape = jax.ShapeDtypeStruct((4, m, tn), out_dtype)
    else:
        out_spec = pl.BlockSpec((tm, tn), lambda i, j, k: (i, j))
        out_shape = jax.ShapeDtypeStruct((m, n), out_dtype)
    res = _pcall(
        body, name=name, grid=(gi, gj, nk), in_specs=in_specs + [_HBM] * nc, out_specs=[out_spec] + [_HBM] * nc,
        out_shape=[out_shape] + riding.out_shape,
        scratch_shapes=([] if nk == 1 else [pltpu.VMEM((tm, tn), F32)]) + riding.sems,
        compiler_params=_cparams(("arbitrary",) * 3 if nc else ("parallel", "parallel", "arbitrary")),
    )(*args, *riding.ins)
    return res[0] if comm is None else (res[0], res[1:])


def _mm_sum(pairs, *, comm, name):
    m, n = pairs[0][0].shape[0], pairs[0][1].shape[1]
    tm, tn = _div(m, 1024, 128), _div(n, 1024, 128)
    tks = [_div(a.shape[1], 1408, 128) for a, _ in pairs]
    nks = [a.shape[1] // tk for (a, _), tk in zip(pairs, tks)]
    offs = [sum(nks[:p]) for p in range(len(pairs))]
    total, npair, nc = sum(nks), len(pairs), comm.n
    gi, gj = m // tm, n // tn

    def body(*refs):
        o_ref, acc = refs[2 * npair + nc], refs[2 * npair + 2 * nc + 1]
        k = pl.program_id(2)
        step = (pl.program_id(0) * gj + pl.program_id(1)) * total + k
        before, after = _comm_phases(comm, refs[2 * npair:2 * npair + nc]
                                     + refs[2 * npair + nc + 1:2 * npair + 2 * nc + 1]
                                     + refs[2 * npair + 2 * nc + 2:], gi * gj * total, step)
        before()
        for p in range(npair):
            def partial_product(p=p):
                part = jnp.dot(refs[2 * p][...], refs[2 * p + 1][...], preferred_element_type=F32)
                if p == 0:
                    @pl.when(k == 0)
                    def _():
                        acc[...] = part

                    @pl.when(k > 0)
                    def _():
                        acc[...] += part
                else:
                    acc[...] += part

            pl.when(jnp.logical_and(k >= offs[p], k < offs[p] + nks[p]))(partial_product)

        @pl.when(k == total - 1)
        def _():
            o_ref[...] = acc[...].astype(o_ref.dtype)

        after()

    def specs(tk, off, nk):
        def kb(k):
            return jnp.clip(k - off, 0, nk - 1)
        return [pl.BlockSpec((tm, tk), lambda i, j, k: (i, kb(k))), pl.BlockSpec((tk, tn), lambda i, j, k: (kb(k), j))]

    in_specs, args = [], []
    for (a, b), tk, off, nk in zip(pairs, tks, offs, nks):
        in_specs += specs(tk, off, nk)
        args += [a, b]
    res = _pcall(
        body, name=name, grid=(gi, gj, total), in_specs=in_specs + [_HBM] * nc,
        out_specs=[pl.BlockSpec((tm, tn), lambda i, j, k: (i, j))] + [_HBM] * nc,
        out_shape=[jax.ShapeDtypeStruct((m, n), BF16)] + comm.out_shape,
        scratch_shapes=[pltpu.VMEM((tm, tn), F32)] + comm.sems,
        compiler_params=_cparams(("arbitrary",) * 3),
    )(*args, *comm.ins)
    return res[0], res[1:]


def _row_spec(br, w, cb):
    return pl.BlockSpec((br, w), lambda i: (i, cb))


def _const_spec(shape):
    return pl.BlockSpec(shape, lambda i: (0,) * len(shape))


def _rows_fwd(fn, rows, consts, outs, *, name, br, acc_shape=None, halo=None):
    s = rows[0][0].shape[0]
    nr, nc = len(rows), len(consts)
    kept = [k for k, o in enumerate(outs) if o is not None]

    def body(*refs):
        xs = [r[...].astype(F32) for r in refs[:nr]]
        cs = [c[...] for c in refs[nr:nr + nc]]
        if halo is not None:
            cs.append(jnp.where(pl.program_id(0) == 0, 0.0, refs[nr + nc][...].astype(F32)))
        res = fn(*xs, *cs)
        orefs = refs[nr + nc + (halo is not None):]
        for j, k in enumerate(kept):
            orefs[j][...] = res[k].astype(orefs[j].dtype)
        if acc_shape is not None:
            acc_ref = orefs[len(kept)]

            @pl.when(pl.program_id(0) == 0)
            def _():
                acc_ref[...] = jnp.zeros_like(acc_ref)

            acc_ref[...] += res[len(outs)]

    in_specs = [_row_spec(br, w, cb) for (_, w, cb) in rows] + [_const_spec(c.shape) for c in consts]
    args = [r[0] for r in rows] + list(consts)
    if halo is not None:
        harr, hw, hcb = rows[halo]
        in_specs.append(pl.BlockSpec((HALO, hw), lambda i: (jnp.maximum(i * (br // HALO) - 1, 0), hcb)))
        args.append(harr)
    out_specs = [_row_spec(br, outs[k][0], 0) for k in kept]
    out_shape = [jax.ShapeDtypeStruct((s, outs[k][0]), outs[k][1]) for k in kept]
    if acc_shape is not None:
        out_specs.append(_const_spec(acc_shape))
        out_shape.append(jax.ShapeDtypeStruct(acc_shape, F32))
    return _pcall(
        body, name=name, grid=(pl.cdiv(s, br),), in_specs=in_specs, out_specs=out_specs, out_shape=out_shape,
        compiler_params=_cparams(("arbitrary",)),
    )(*args)


def _rows_bwd(fn, rows, consts, cots, *, wrt_rows, wrt_consts, drow_dtypes, name, br, unit_cot=False, comm=None):
    comm = _NOTHING if comm is None else comm
    ncomm = comm.n
    nout = len(wrt_rows) + len(wrt_consts)
    s = rows[0][0].shape[0]
    nr, nc = len(rows), len(consts)
    flat_cots = [c for lst in cots for c in lst]
    ncot = len(flat_cots)

    def body(*refs):
        xs = [r[...].astype(F32) for r in refs[:nr]]
        cs = [c[...] for c in refs[nr:nr + nc]]
        cvals = [c[...].astype(F32) for c in refs[nr + nc:nr + nc + ncot]]
        orefs = refs[nr + nc + ncot + ncomm:]
        before, after = _comm_phases(comm, refs[nr + nc + ncot:nr + nc + ncot + ncomm] + orefs[nout:], s // br)
        before()

        def g(*d):
            xs2, cs2 = list(xs), list(cs)
            for j, k in enumerate(wrt_rows):
                xs2[k] = d[j]
            for j, k in enumerate(wrt_consts):
                cs2[k] = d[len(wrt_rows) + j]
            return tuple(fn(*xs2, *cs2))

        prim = [xs[k] for k in wrt_rows] + [cs[k] for k in wrt_consts]
        outs, vjp = jax.vjp(g, *prim)
        ct = []
        pos = 0
        for o, lst in zip(outs, cots):
            if unit_cot:
                ct.append(jnp.ones_like(o))
                continue
            acc = jnp.zeros_like(o)
            for _ in lst:
                acc = acc + cvals[pos]
                pos += 1
            ct.append(acc)
        grads = vjp(tuple(ct))
        for j in range(len(wrt_rows)):
            orefs[j][...] = grads[j].astype(orefs[j].dtype)

        @pl.when(pl.program_id(0) == 0)
        def _():
            for j in range(len(wrt_consts)):
                oref = orefs[len(wrt_rows) + j]
                oref[...] = jnp.zeros_like(oref)

        for j in range(len(wrt_consts)):
            orefs[len(wrt_rows) + j][...] += grads[len(wrt_rows) + j]
        after()

    in_specs = ([_row_spec(br, w, cb) for (_, w, cb) in rows] + [_const_spec(c.shape) for c in consts]
                + [_row_spec(br, w, cb) for (_, w, cb) in flat_cots] + [_HBM] * ncomm)
    out_specs = ([_row_spec(br, rows[k][1], 0) for k in wrt_rows] + [_const_spec(consts[k].shape) for k in wrt_consts]
                 + [_HBM] * ncomm)
    out_shape = ([jax.ShapeDtypeStruct((s, rows[k][1]), dt) for k, dt in zip(wrt_rows, drow_dtypes)]
                 + [jax.ShapeDtypeStruct(consts[k].shape, F32) for k in wrt_consts] + comm.out_shape)
    return _pcall(
        body, name=name, grid=(s // br,), in_specs=in_specs, out_specs=out_specs, out_shape=out_shape,
        scratch_shapes=comm.sems, compiler_params=_cparams(("arbitrary",)),
    )(*[r[0] for r in rows], *consts, *[c[0] for c in flat_cots], *comm.ins)


def _rms(x, w):
    return x * lax.rsqrt(jnp.mean(x * x, axis=-1, keepdims=True) + RMS_EPS) * w


def _f_pre(x, nw, sc, sh):
    return _rms(x, nw) * (1.0 + sc) + sh, x


def _f_pre2(x, o, gt, nw, sc, sh):
    x1 = x + gt * o
    return x1, _rms(x1, nw) * (1.0 + sc) + sh


def _f_fin(x1, f, tgt, gt, nfw):
    y = _rms(x1 + gt * f, nfw)
    return (0.5 * jnp.mean(jnp.square(y - tgt), axis=-1, keepdims=True),)


def _f_comb(o1, o2, o3, l1, l2, l3):
    m = lax.stop_gradient(jnp.maximum(jnp.maximum(l1, l2), l3))
    e1, e2, e3 = jnp.exp(l1 - m), jnp.exp(l2 - m), jnp.exp(l3 - m)
    return ((e1 * o1 + e2 * o2 + e3 * o3) / (e1 + e2 + e3),)


def _f_rwpre(zs, w0, a0, k_k, k_a, wl, e, et):
    r, k, v, zl = zs[:, 0:D], zs[:, D:2 * D], zs[:, 2 * D:3 * D], zs[:, 3 * D:N_RWP]
    lane = lax.broadcasted_iota(jnp.int32, zl.shape, 1)
    t = jnp.where(lane < 64, jnp.tanh(zl), jnp.where(lane < 128, zl, jnp.where(lane < 288, jax.nn.sigmoid(zl), 0.0)))
    lo = _nn(t[:, 0:128], wl[0:128, 0:2 * D])
    g = _nn(t[:, 128:N_LORA], wl[128:N_LORA, 2 * D:3 * D])
    lw = -math.exp(-0.5) * jax.nn.sigmoid(w0 + lo[:, 0:D])
    a = jax.nn.sigmoid(a0 + lo[:, D:2 * D])
    k_mod = k * (1.0 + (a - 1.0) * k_a)
    kk = k * k_k
    kk = kk / jnp.maximum(jnp.sqrt(_hsum(kk * kk, e, et)), 1e-12)
    return r, lw, k_mod, v, -kk, kk * a, g


def _f_rwpost(y, r, v, k_mod, g, lnx_w, lnx_b, r_k, e, et):
    mean = _hsum(y, e, et) * (1.0 / 64)
    yc = y - mean
    var = _hsum(yc * yc, e, et) * (1.0 / 64)
    yn = yc * lax.rsqrt(var + GN_EPS) * lnx_w + lnx_b
    bonus = _hsum(r * k_mod * r_k, e, et) * v
    return ((yn + bonus) * g,)


def _f_mix(gi, ya, yr, bg):
    gate = jax.nn.sigmoid(gi + bg)
    return (gate[:, 0:D] * ya + gate[:, D:2 * D] * yr,)


def _f_adamw(w, g, m, v):
    m = ADAM_B1 * m + (1.0 - ADAM_B1) * g
    v = ADAM_B2 * v + (1.0 - ADAM_B2) * jnp.square(g)
    m_hat = m / (1.0 - ADAM_B1 ** ADAM_STEP)
    v_hat = v / (1.0 - ADAM_B2 ** ADAM_STEP)
    return -ADAM_LR * (m_hat / (jnp.sqrt(v_hat) + ADAM_EPS) + ADAM_WD * w), m, v


def _down(x, k):
    row = lax.broadcasted_iota(jnp.int32, x.shape, 0)
    return jnp.where(row < k, 0.0, pltpu.roll(x, k, 0))


def _up(x, k):
    n = x.shape[0]
    row = lax.broadcasted_iota(jnp.int32, x.shape, 0)
    return jnp.where(row >= n - k, 0.0, pltpu.roll(x, n - k, 0))


def _col_spec(s, w, off=0):
    return pl.BlockSpec((s, w), lambda j: (0, j + off))


def _rwpre_shift_bwd(zs, z, mu, consts, cots, *, br):
    s, w = zs.shape
    n = s // br
    flat = [c for lst in cots for c in lst]
    nc, ncot, nwrt = len(consts), len(flat), 5

    def this(i):
        return jnp.minimum(i, n - 1)

    def last(i):
        return jnp.maximum(i - 1, 0)

    def body(*refs):
        zs_ref, z_ref, zh_ref, mu_ref = refs[:4]
        c_refs, cot_refs = refs[4:4 + nc], refs[4 + nc:4 + nc + ncot]
        dz_ref, dmu_ref = refs[4 + nc + ncot:6 + nc + ncot]
        dc_refs = refs[6 + nc + ncot:6 + nc + ncot + nwrt]
        kept = refs[-1]
        i = pl.program_id(0)

        @pl.when(i == 0)
        def _():
            dmu_ref[...] = jnp.zeros_like(dmu_ref)
            for ref in dc_refs:
                ref[...] = jnp.zeros_like(ref)

        cs = [c[...] for c in c_refs]

        def g(zz, *d):
            return tuple(_f_rwpre(zz, *d, *cs[nwrt:]))

        outs, vjp = jax.vjp(g, zs_ref[...], *cs[:nwrt])
        cts, pos = [], 0
        for o, lst in zip(outs, cots):
            acc = jnp.zeros_like(o)
            for _ in lst:
                acc = acc + cot_refs[pos][...].astype(F32)
                pos += 1
            cts.append(acc)
        grads = vjp(tuple(cts))
        dzs_new = grads[0]

        @pl.when(i < n)
        def _():
            for ref, gr in zip(dc_refs, grads[1:]):
                ref[...] += gr

        @pl.when(i > 0)
        def _():
            d, m = kept[...], mu_ref[...]
            row = lax.broadcasted_iota(jnp.int32, d.shape, 0)
            head = jnp.sum(jnp.where(row == 0, dzs_new, 0.0), axis=0, keepdims=True)
            head = jnp.where(i < n, head, 0.0)
            dm = d * m
            after = jnp.where(row == br - 1, head * m, pltpu.roll(dm, br - 1, 0))
            dz_ref[...] = (d - dm + after).astype(dz_ref.dtype)
            zz, halo = z_ref[...].astype(F32), zh_ref[...].astype(F32)
            tail = jnp.sum(jnp.where(lax.broadcasted_iota(jnp.int32, halo.shape, 0) == HALO - 1, halo, 0.0), axis=0,
                           keepdims=True)
            before = jnp.where(row == 0, jnp.where(i > 1, tail, 0.0), pltpu.roll(zz, 1, 0))
            dmu_ref[...] += jnp.sum(d * (before - zz), axis=0, keepdims=True)

        kept[...] = dzs_new

    in_specs = ([pl.BlockSpec((br, w), lambda i: (this(i), 0)), pl.BlockSpec((br, w), lambda i: (last(i), 0)),
                 pl.BlockSpec((HALO, w), lambda i: (jnp.maximum(last(i) * (br // HALO) - 1, 0), 0)),
                 _const_spec(mu.shape)] + [_const_spec(c.shape) for c in consts]
                + [pl.BlockSpec((br, cw), lambda i, cb=cb: (this(i), cb)) for (_, cw, cb) in flat])
    out_specs = ([pl.BlockSpec((br, w), lambda i: (last(i), 0)), _const_spec(mu.shape)]
                 + [_const_spec(consts[k].shape) for k in range(nwrt)])
    out_shape = ([jax.ShapeDtypeStruct((s, w), BF16), jax.ShapeDtypeStruct(mu.shape, F32)]
                 + [jax.ShapeDtypeStruct(consts[k].shape, F32) for k in range(nwrt)])
    return _pcall(
        body, name="rwpre_shift_bwd", grid=(n + 1,), in_specs=in_specs, out_specs=out_specs, out_shape=out_shape,
        scratch_shapes=[pltpu.VMEM((br, w), F32)], compiler_params=_cparams(("arbitrary",)),
    )(zs, z, z, mu, *consts, *[c[0] for c in flat])


def _conv3(x, w_ref, b_ref):
    return b_ref[...] + w_ref[0:1, :] * _down(x, 2) + w_ref[1:2, :] * _down(x, 1) + w_ref[2:3, :] * x


def _conv_fwd(u, cw, cb):
    s = u.shape[0]
    nb = D_FF // 128

    def body(ug_ref, uv_ref, wg_ref, wv_ref, bg_ref, bv_ref, o_ref):
        gate = _conv3(ug_ref[...], wg_ref, bg_ref)
        val = _conv3(uv_ref[...], wv_ref, bv_ref)
        o_ref[...] = (gate * jax.nn.sigmoid(gate) * val).astype(o_ref.dtype)

    return _pcall(
        body, name="conv_fwd", grid=(nb,),
        in_specs=[_col_spec(s, 128), _col_spec(s, 128, nb), _col_spec(3, 128), _col_spec(3, 128, nb),
                  _col_spec(1, 128), _col_spec(1, 128, nb)],
        out_specs=_col_spec(s, 128), out_shape=jax.ShapeDtypeStruct((s, D_FF), BF16),
        compiler_params=_cparams(("parallel",)),
    )(u, u, cw, cw, cb, cb)


def _conv_bwd(u, cw, cb, dact):
    s = u.shape[0]
    nb = D_FF // 128

    def half(x, d, w_ref, du_ref, dw_ref, db_ref):
        x1, x2 = _down(x, 1), _down(x, 2)
        du_ref[...] = (w_ref[2:3, :] * d + w_ref[1:2, :] * _up(d, 1) + w_ref[0:1, :] * _up(d, 2)).astype(du_ref.dtype)
        dw_ref[0:1, :] = jnp.sum(d * x2, axis=0, keepdims=True)
        dw_ref[1:2, :] = jnp.sum(d * x1, axis=0, keepdims=True)
        dw_ref[2:3, :] = jnp.sum(d * x, axis=0, keepdims=True)
        db_ref[...] = jnp.sum(d, axis=0, keepdims=True)

    def body(ug_ref, uv_ref, wg_ref, wv_ref, bg_ref, bv_ref, da_ref,
             dug_ref, duv_ref, dwg_ref, dwv_ref, dbg_ref, dbv_ref):
        ug, uv, da = ug_ref[...], uv_ref[...], da_ref[...]
        gate = _conv3(ug, wg_ref, bg_ref)
        val = _conv3(uv, wv_ref, bv_ref)
        sg = jax.nn.sigmoid(gate)
        dgate = da * val * sg * (1.0 + gate * (1.0 - sg))
        dval = da * gate * sg
        half(ug, dgate, wg_ref, dug_ref, dwg_ref, dbg_ref)
        half(uv, dval, wv_ref, duv_ref, dwv_ref, dbv_ref)

    dug, duv, dwg, dwv, dbg, dbv = _pcall(
        body, name="conv_bwd", grid=(nb,),
        in_specs=[_col_spec(s, 128), _col_spec(s, 128, nb), _col_spec(3, 128), _col_spec(3, 128, nb),
                  _col_spec(1, 128), _col_spec(1, 128, nb), _col_spec(s, 128)],
        out_specs=[_col_spec(s, 128), _col_spec(s, 128), _col_spec(3, 128), _col_spec(3, 128),
                   _col_spec(1, 128), _col_spec(1, 128)],
        out_shape=[jax.ShapeDtypeStruct((s, D_FF), BF16), jax.ShapeDtypeStruct((s, D_FF), BF16),
                   jax.ShapeDtypeStruct((3, D_FF), F32), jax.ShapeDtypeStruct((3, D_FF), F32),
                   jax.ShapeDtypeStruct((1, D_FF), F32), jax.ShapeDtypeStruct((1, D_FF), F32)],
        compiler_params=_cparams(("parallel",)),
    )(u, u, cw, cw, cb, cb, dact)
    return (jnp.concatenate([dug, duv], axis=1), jnp.concatenate([dwg, dwv], axis=1),
            jnp.concatenate([dbg, dbv], axis=1))


ATT_BATCH = 4


def _att_batch(q, kp, kc, vp, vc, first):
    ma = lax.broadcasted_iota(jnp.int32, (1, ATT_BLOCK, 128), 2) < 64

    def diag(x):
        return jnp.concatenate([jnp.where(ma, x, 0.0), jnp.where(ma, 0.0, x)], axis=1)

    qi = lax.broadcasted_iota(jnp.int32, (1, ATT_BLOCK, 2 * ATT_BLOCK), 1)
    kj = lax.broadcasted_iota(jnp.int32, (1, ATT_BLOCK, 2 * ATT_BLOCK), 2) & (ATT_BLOCK - 1)
    okp = kj >= qi + jnp.where(first, 2 * ATT_BLOCK, 0)
    okc = kj <= qi
    sp = jnp.where(okp, _bnt(q, diag(kp)) * 0.125, NEG)
    sc = jnp.where(okc, _bnt(q, diag(kc)) * 0.125, NEG)

    def per_head(fn, x):
        return fn(x[..., :ATT_BLOCK]), fn(x[..., ATT_BLOCK:])

    def spread(ab):
        return jnp.concatenate([jnp.broadcast_to(t, t.shape[:2] + (ATT_BLOCK,)) for t in ab], axis=-1)

    row_max = functools.partial(jnp.max, axis=-1, keepdims=True)
    row_sum = functools.partial(jnp.sum, axis=-1, keepdims=True)
    m = [lax.stop_gradient(jnp.maximum(a, b)) for a, b in zip(per_head(row_max, sp), per_head(row_max, sc))]
    pp, pc = jnp.exp(sp - spread(m)), jnp.exp(sc - spread(m))
    den = [a + b for a, b in zip(per_head(row_sum, pp), per_head(row_sum, pc))]
    num = _bnn(pp, diag(vp)) + _bnn(pc, diag(vc))
    out = num / jnp.where(ma, den[0], den[1])
    lse = jnp.where(ma, m[0] + jnp.log(den[0]), m[1] + jnp.log(den[1]))
    return out, jnp.broadcast_to(lse, out.shape)


def _att_pairs_per_step(dil):
    return 4 if dil == 1 else 1


def _att_residues(dil):
    return min(dil, ATT_BATCH // _att_pairs_per_step(dil))


def _att_specs(g, dil):
    rows, pp = ATT_BLOCK * dil, _att_pairs_per_step(dil)

    def cur(slot):
        return pl.BlockSpec((rows, 128 * pp), lambda n, p: (n, (g * 3 + slot) * (4 // pp) + p))

    def prev(slot):
        return pl.BlockSpec((rows, 128 * pp), lambda n, p: (jnp.maximum(n - 1, 0), (g * 3 + slot) * (4 // pp) + p))

    return [cur(0), prev(1), cur(1), prev(2), cur(2)]


def _att_out_spec(dil):
    return pl.BlockSpec((ATT_BLOCK * dil, 128 * _att_pairs_per_step(dil)), lambda n, p: (n, p))


def _att_grid(s, dil):
    return (s // (ATT_BLOCK * dil), 4 // _att_pairs_per_step(dil))


def _att_windows(i, dil):
    res = _att_residues(dil)

    def rows(r):
        return pl.ds(i * res + r, ATT_BLOCK, stride=dil) if dil > 1 else pl.ds(0, ATT_BLOCK)

    return [(rows(r), pl.ds(128 * j, 128)) for j in range(_att_pairs_per_step(dil)) for r in range(res)]


def _att_fwd(att_in, g, dil):
    s = att_in.shape[0]

    def body(q_ref, kp_ref, kc_ref, vp_ref, vc_ref, o_ref, l_ref):
        first = pl.program_id(0) == 0

        def one(i, carry):
            win = _att_windows(i, dil)
            vals = [jnp.stack([ref[w] for w in win]) for ref in (q_ref, kp_ref, kc_ref, vp_ref, vc_ref)]
            o, l = _att_batch(*vals, first)
            for j, w in enumerate(win):
                o_ref[w] = o[j]
                l_ref[w] = l[j]
            return carry

        lax.fori_loop(0, dil // _att_residues(dil), one, 0)

    return _pcall(
        body, name=f"att_fwd{g}", grid=_att_grid(s, dil), in_specs=_att_specs(g, dil),
        out_specs=[_att_out_spec(dil)] * 2, out_shape=[jax.ShapeDtypeStruct((s, ATT_WIDTH), F32)] * 2,
        compiler_params=_cparams(("parallel", "parallel")),
    )(att_in, att_in, att_in, att_in, att_in)


def _att_bwd(att_in, g, dil, do, dl, acc):
    s = att_in.shape[0]

    def body(q_ref, kp_ref, kc_ref, vp_ref, vc_ref, do_ref, dl_ref, dq_ref, dkp_ref, dkc_ref, dvp_ref, dvc_ref):
        first = pl.program_id(0) == 0

        def one(i, carry):
            win = _att_windows(i, dil)
            vals = [jnp.stack([ref[w] for w in win]) for ref in (q_ref, kp_ref, kc_ref, vp_ref, vc_ref)]
            _, vjp = jax.vjp(lambda *a: _att_batch(*a, first), *vals)
            grads = vjp((jnp.stack([do_ref[w] for w in win]), jnp.stack([dl_ref[w] for w in win])))
            for ref, gr in zip((dq_ref, dkp_ref, dkc_ref, dvp_ref, dvc_ref), grads):
                for j, w in enumerate(win):
                    ref[w] = gr[j]
            return carry

        lax.fori_loop(0, dil // _att_residues(dil), one, 0)

    dq, dkp, dkc, dvp, dvc = _pcall(
        body, name=f"att_bwd{g}", grid=_att_grid(s, dil), in_specs=_att_specs(g, dil) + [_att_out_spec(dil)] * 2,
        out_specs=[_att_out_spec(dil)] * 5, out_shape=[jax.ShapeDtypeStruct((s, ATT_WIDTH), F32)] * 5,
        compiler_params=_cparams(("parallel", "parallel")),
    )(att_in, att_in, att_in, att_in, att_in, do, dl)

    unit, rb = ATT_BLOCK * dil, 1024
    steps = s // rb
    within = unit < rb

    def shifted(cur_ref, next_ref, has_next):
        nxt = jnp.where(has_next, next_ref[...], 0.0)
        return jnp.concatenate([cur_ref[unit:, :], nxt], axis=0) if within else nxt

    def cbody(dq_ref, dkc_ref, dkp_ref, dkn_ref, dvc_ref, dvp_ref, dvn_ref, *rest):
        o_ref = rest[-1]
        has_next = pl.program_id(0) + (1 if within else unit // rb) < steps
        o_ref[:, 0:ATT_WIDTH] = dq_ref[...].astype(BF16)
        o_ref[:, ATT_WIDTH:2 * ATT_WIDTH] = (dkc_ref[...] + shifted(dkp_ref, dkn_ref, has_next)).astype(BF16)
        o_ref[:, 2 * ATT_WIDTH:3 * ATT_WIDTH] = (dvc_ref[...] + shifted(dvp_ref, dvn_ref, has_next)).astype(BF16)

    cur = pl.BlockSpec((rb, ATT_WIDTH), lambda i: (i, 0))
    if within:
        nxt = pl.BlockSpec((unit, ATT_WIDTH), lambda i: (jnp.minimum((i + 1) * (rb // unit), s // unit - 1), 0))
    else:
        nxt = pl.BlockSpec((rb, ATT_WIDTH), lambda i: (jnp.minimum(i + unit // rb, steps - 1), 0))
    carried = [] if acc is None else [acc]
    return _pcall(
        cbody, name=f"att_bwd_sum{g}", grid=(steps,),
        in_specs=[cur, cur, cur, nxt, cur, cur, nxt] + [pl.BlockSpec(memory_space=pl.ANY)] * len(carried),
        out_specs=pl.BlockSpec((rb, 3 * ATT_WIDTH), lambda i: (i, g)),
        out_shape=jax.ShapeDtypeStruct((s, N_ATT), BF16), input_output_aliases={7: 0} if carried else {},
        compiler_params=_cparams(("parallel",)),
    )(dq, dkc, dkp, dkp, dvc, dvp, dvp, *carried)


def _cumsum_rows_impl(x):
    row = lax.broadcasted_iota(jnp.int32, x.shape, 0)
    shift = 1
    while shift < x.shape[0]:
        x = x + jnp.where(row >= shift, pltpu.roll(x, shift, 0), 0.0)
        shift *= 2
    return x


@jax.custom_vjp
def _cumsum_rows(x):
    return _cumsum_rows_impl(x)


_cumsum_rows.defvjp(lambda x: (_cumsum_rows_impl(x), None),
                    lambda _, g: (jnp.sum(g, axis=0, keepdims=True) - _cumsum_rows_impl(g) + g,))


def _unit_lower_inverse_impl(n):
    eye = (lax.broadcasted_iota(jnp.int32, (1,) + n.shape[1:], 1)
           == lax.broadcasted_iota(jnp.int32, (1,) + n.shape[1:], 2))
    t = jnp.where(eye, 1.0, 0.0) + n
    pw = n
    for _ in range(5):
        pw = _bnn(pw, pw)
        t = t + _bnn(t, pw)
    return t


@jax.custom_vjp
def _unit_lower_inverse(n):
    return _unit_lower_inverse_impl(n)


def _unit_lower_inverse_fwd(n):
    t = _unit_lower_inverse_impl(n)
    return t, t


_unit_lower_inverse.defvjp(_unit_lower_inverse_fwd, lambda t, g: (_bnt(_btn(t, g), t),))


@jax.custom_vjp
def _known_inverse(n, t):
    return t


_known_inverse.defvjp(lambda n, t: (t, t), lambda t, g: (_bnt(_btn(t, g), t), jnp.zeros_like(t)))


def _scan_chunk(r, lw, k, v, a, b, s0, inverse):
    c = SCAN_CHUNK
    p = s0.shape[0]
    cum = _cumsum_rows(lw)
    tot = jnp.sum(lw, axis=0, keepdims=True)
    ma = (lax.broadcasted_iota(jnp.int32, (c, 128 * p), 1) & 127) < 64

    def pairs(x):
        return jnp.concatenate([x[None, :, 128 * j:128 * (j + 1)] for j in range(p)], axis=0)

    def stack(x):
        return jnp.concatenate([pairs(jnp.where(ma, x, 0.0)), pairs(jnp.where(ma, 0.0, x))], axis=1)

    einv, eend = jnp.exp(-cum), jnp.exp(tot - cum)
    ra, aa = stack(r * jnp.exp(cum)), stack(a * jnp.exp(cum - lw))
    bi, ki, be, ke, vs = stack(b * einv), stack(k * einv), stack(b * eend), stack(k * eend), stack(v)
    r2 = lax.broadcasted_iota(jnp.int32, (1, 2 * c, 2 * c), 1)
    c2 = lax.broadcasted_iota(jnp.int32, (1, 2 * c, 2 * c), 2)
    same = (r2 >= c) == (c2 >= c)
    strict = jnp.logical_and(same, c2 < r2)
    incl = jnp.logical_and(same, c2 <= r2)
    s0 = jnp.where(same, s0, 0.0)
    prod = _bnt(jnp.concatenate([aa, ra], axis=1), jnp.concatenate([bi, ki], axis=1))
    a_ab = jnp.where(strict, prod[:, :2 * c, :2 * c], 0.0)
    a_ak = jnp.where(strict, prod[:, :2 * c, 2 * c:], 0.0)
    a_rb = jnp.where(incl, prod[:, 2 * c:, :2 * c], 0.0)
    a_rk = jnp.where(incl, prod[:, 2 * c:, 2 * c:], 0.0)
    t = inverse(a_ab)
    u = _bnn(t, _bnt(aa, s0) + _bnn(a_ak, vs))
    uv = jnp.concatenate([u, vs], axis=1)
    ys = _bnt(ra, s0) + _bnn(jnp.concatenate([a_rb, a_rk], axis=2), uv)
    s1 = s0 * pairs(jnp.exp(tot)) + _btn(uv, jnp.concatenate([be, ke], axis=1))
    y3 = ys[:, :c] + ys[:, c:]
    return (jnp.concatenate([y3[j] for j in range(p)], axis=1), s1), t


def _scan_specs(rev, n):
    def at(i):
        return n - 1 - i if rev else i

    def cm(cb):
        return pl.BlockSpec((SCAN_CHUNK, D), lambda i: (at(i), cb))

    return cm, pl.BlockSpec((1, SCAN_PAIRS, 128, 128), lambda i: (at(i), 0, 0, 0))


def _comm_phases(comm, refs, n, step=None):
    k = comm.n
    srcs, outs, sems = refs[:k], refs[k:2 * k], refs[2 * k:]
    i = pl.program_id(0) if step is None else step

    def before():
        @pl.when(i == 0)
        def _():
            comm.first(srcs, outs, sems)

    def after():
        if comm.mid is not None:
            @pl.when(i == (3 * n) // 4)
            def _():
                comm.mid(srcs, outs, sems)

        @pl.when(i == n - 1)
        def _():
            comm.last(srcs, outs, sems)

    return before, after


def _scan_fwd(zs, lw, km, aa, bb, comm):
    s = zs.shape[0]
    n = s // SCAN_CHUNK
    cm, st = _scan_specs(False, n)
    k = comm.n

    def body(*refs):
        r_ref, lw_ref, k_ref, v_ref, a_ref, b_ref = refs[:6]
        y_ref, s0_ref, t_ref = refs[6 + k:9 + k]
        state = refs[9 + 2 * k]
        before, after = _comm_phases(comm, refs[6:6 + k] + refs[9 + k:9 + 2 * k] + refs[10 + 2 * k:], n)
        before()

        @pl.when(pl.program_id(0) == 0)
        def _():
            state[...] = jnp.zeros_like(state)

        s0 = state[...]
        s0_ref[0] = s0
        (y, s1), t = _scan_chunk(*[ref[...] for ref in (r_ref, lw_ref, k_ref, v_ref, a_ref, b_ref)], s0,
                                 _unit_lower_inverse)
        y_ref[...] = y
        t_ref[0] = t.astype(BF16)
        state[...] = s1
        after()

    per_chunk = (n, SCAN_PAIRS, 128, 128)
    res = _pcall(
        body, name="scan_fwd", grid=(n,), in_specs=[cm(0), cm(0), cm(0), cm(2), cm(0), cm(0)] + [_HBM] * k,
        out_specs=[cm(0), st, st] + [_HBM] * k,
        out_shape=[jax.ShapeDtypeStruct((s, D), F32), jax.ShapeDtypeStruct(per_chunk, F32),
                   jax.ShapeDtypeStruct(per_chunk, BF16)] + comm.out_shape,
        scratch_shapes=[pltpu.VMEM((SCAN_PAIRS, 128, 128), F32)] + comm.sems,
        compiler_params=_cparams(("arbitrary",)),
    )(zs, lw, km, zs, aa, bb, *comm.ins)
    return res[0], res[1], res[2], res[3:]


def _scan_bwd(zs, lw, km, aa, bb, s0s, ts, dy, comm):
    s = zs.shape[0]
    n = s // SCAN_CHUNK
    cm, st = _scan_specs(True, n)
    k = comm.n

    def body(*refs):
        r_ref, lw_ref, k_ref, v_ref, a_ref, b_ref, s0_ref, t_ref, dy_ref = refs[:9]
        douts = refs[9 + k:15 + k]
        dstate = refs[15 + 2 * k]
        before, after = _comm_phases(comm, refs[9:9 + k] + refs[15 + k:15 + 2 * k] + refs[16 + 2 * k:], n)
        before()

        @pl.when(pl.program_id(0) == 0)
        def _():
            dstate[...] = jnp.zeros_like(dstate)

        t = t_ref[0].astype(F32)
        prim = [ref[...] for ref in (r_ref, lw_ref, k_ref, v_ref, a_ref, b_ref)] + [s0_ref[0]]
        _, vjp, _ = jax.vjp(lambda *p: _scan_chunk(*p, lambda nil: _known_inverse(nil, t)), *prim, has_aux=True)
        grads = vjp((dy_ref[...], dstate[...]))
        for ref, gr in zip(douts, grads[:6]):
            ref[...] = gr
        dstate[...] = grads[6]
        after()

    res = _pcall(
        body, name="scan_bwd", grid=(n,),
        in_specs=[cm(0), cm(0), cm(0), cm(2), cm(0), cm(0), st, st, cm(0)] + [_HBM] * k,
        out_specs=[cm(0)] * 6 + [_HBM] * k, out_shape=[jax.ShapeDtypeStruct((s, D), F32)] * 6 + comm.out_shape,
        scratch_shapes=[pltpu.VMEM((SCAN_PAIRS, 128, 128), F32)] + comm.sems,
        compiler_params=_cparams(("arbitrary",)),
    )(zs, lw, km, zs, aa, bb, s0s, ts, dy, *comm.ins)
    return res[:6], res[6:]


_HBM = pl.BlockSpec(memory_space=pltpu.HBM)


def _me():
    return lax.axis_index("x"), lax.axis_index("y"), lax.axis_index("c")


def _allgather8(src, name):
    def body(src_ref, out_ref, ssem, rsem, lsem):
        x, y, c = _me()
        me = 4 * x + 2 * y + c
        local = pltpu.make_async_copy(src_ref, out_ref.at[me], lsem)
        local.start()
        peers = []
        for k in range(1, 8):
            peers.append(((1 - x) if k & 4 else x, (1 - y) if k & 2 else y, (1 - c) if k & 1 else c))
        sends = []
        for k, peer in enumerate(peers):
            cp = pltpu.make_async_remote_copy(src_ref, out_ref.at[me], ssem.at[k], rsem.at[k], device_id=peer,
                                              device_id_type=MESH)
            cp.start()
            sends.append(cp)
        for k, (px, py, pc) in enumerate(peers):
            pltpu.make_async_remote_copy(src_ref, out_ref.at[4 * px + 2 * py + pc], ssem.at[k], rsem.at[k],
                                         device_id=(px, py, pc), device_id_type=MESH).wait_recv()
        for cp in sends:
            cp.wait_send()
        local.wait()

    return _pcall(
        body, name=name, in_specs=[_HBM], out_specs=_HBM, out_shape=jax.ShapeDtypeStruct((8,) + src.shape, src.dtype),
        scratch_shapes=[pltpu.SemaphoreType.DMA((7,)), pltpu.SemaphoreType.DMA((7,)), pltpu.SemaphoreType.DMA],
    )(src)


def _other_chips(x, y):
    return [(1 - x, y), (x, 1 - y), (1 - x, 1 - y)]


def _remote(src, dst, ssem, rsem, to):
    return pltpu.make_async_remote_copy(src, dst, ssem, rsem, device_id=to, device_id_type=MESH)


class _GatherWeights:
    def __init__(self, shards):
        self.ins = list(shards)
        n = self.n = len(shards)
        self.out_shape = [jax.ShapeDtypeStruct((4,) + t.shape, t.dtype) for t in shards]
        self.sems = [pltpu.SemaphoreType.DMA((6 * n,)), pltpu.SemaphoreType.DMA((6 * n,)),
                     pltpu.SemaphoreType.DMA((n,)), pltpu.SemaphoreType.DMA((n,))]

    def _copies(self, srcs, outs, sems):
        ssem, rsem, lsem, osem = sems
        x, y, c = _me()
        me = 2 * x + y
        own, ici, landed, passed, passed_in = [], [], [], [], []
        for a in range(self.n):
            h = self.ins[a].shape[0] // 2
            mine, other = pl.ds(c * h, h), pl.ds((1 - c) * h, h)
            own.append(_remote(srcs[a], outs[a].at[me], lsem.at[a], osem.at[a], (x, y, 1 - c)))
            for k, (px, py) in enumerate(_other_chips(x, y)):
                s1, r1, s2, r2 = ssem.at[6 * a + k], rsem.at[6 * a + k], ssem.at[6 * a + 3 + k], rsem.at[6 * a + 3 + k]
                got, got_sib = outs[a].at[2 * px + py, mine], outs[a].at[2 * px + py, other]
                ici.append(_remote(srcs[a].at[mine], outs[a].at[me, mine], s1, r1, (px, py, c)))
                landed.append(_remote(got, got, s1, r1, (px, py, c)))
                passed.append(_remote(got, got, s2, r2, (x, y, 1 - c)))
                passed_in.append(_remote(got_sib, got_sib, s2, r2, (x, y, 1 - c)))
        return own, ici, landed, passed, passed_in

    def first(self, srcs, outs, sems):
        own, ici, _, _, _ = self._copies(srcs, outs, sems)
        for cp in own + ici:
            cp.start()

    def mid(self, srcs, outs, sems):
        _, _, landed, passed, _ = self._copies(srcs, outs, sems)
        for arrived, onward in zip(landed, passed):
            arrived.wait_recv()
            onward.start()

    def last(self, srcs, outs, sems):
        own, ici, _, passed, passed_in = self._copies(srcs, outs, sems)
        for cp in passed_in:
            cp.wait_recv()
        for cp in ici + passed:
            cp.wait_send()
        for cp in own:
            cp.wait()


class _ScatterToChips:
    def __init__(self, parts):
        self.ins = list(parts)
        n = self.n = len(parts)
        self.out_shape = [jax.ShapeDtypeStruct(t.shape, t.dtype) for t in parts]
        self.sems = [pltpu.SemaphoreType.DMA((3 * n,)), pltpu.SemaphoreType.DMA((3 * n,)), pltpu.SemaphoreType.DMA((n,))]

    def _copies(self, srcs, outs, sems):
        ssem, rsem, lsem = sems
        x, y, c = _me()
        me = 2 * x + y
        own, out, landed = [], [], []
        for a in range(self.n):
            own.append(pltpu.make_async_copy(srcs[a].at[me], outs[a].at[me], lsem.at[a]))
            for k, (px, py) in enumerate(_other_chips(x, y)):
                dst = outs[a].at[2 * px + py]
                out.append(_remote(srcs[a].at[2 * px + py], outs[a].at[me], ssem.at[3 * a + k], rsem.at[3 * a + k],
                                   (px, py, c)))
                landed.append(_remote(dst, dst, ssem.at[3 * a + k], rsem.at[3 * a + k], (px, py, c)))
        return own, out, landed

    def first(self, srcs, outs, sems):
        own, out, _ = self._copies(srcs, outs, sems)
        for cp in own + out:
            cp.start()

    mid = None

    def last(self, srcs, outs, sems):
        own, out, landed = self._copies(srcs, outs, sems)
        for cp in landed:
            cp.wait_recv()
        for cp in own:
            cp.wait()
        for cp in out:
            cp.wait_send()


def _run_comm(comm, name):
    n = comm.n

    def body(*refs):
        srcs, outs, sems = refs[:n], refs[n:2 * n], refs[2 * n:]
        comm.first(srcs, outs, sems)
        if comm.mid is not None:
            comm.mid(srcs, outs, sems)
        comm.last(srcs, outs, sems)

    return _pcall(body, name=name, in_specs=[_HBM] * n, out_specs=[_HBM] * n, out_shape=comm.out_shape,
                  scratch_shapes=comm.sems)(*comm.ins)


class _NoComm:
    n, ins, out_shape, sems, mid = 0, [], [], [], None

    def first(self, srcs, outs, sems):
        pass

    def last(self, srcs, outs, sems):
        pass


_NOTHING = _NoComm()


class _SiblingHalves:
    mid = None

    def __init__(self, grads):
        self.ins = list(grads)
        n = self.n = len(grads)
        self.out_shape = [jax.ShapeDtypeStruct((4, t.shape[1] // 2, t.shape[2]), t.dtype) for t in grads]
        self.sems = [pltpu.SemaphoreType.DMA((n,)), pltpu.SemaphoreType.DMA((n,))]

    def _copies(self, srcs, outs, sems):
        ssem, rsem = sems
        x, y, c = _me()
        copies = []
        for a in range(self.n):
            h = self.ins[a].shape[1] // 2
            copies.append(_remote(srcs[a].at[:, pl.ds((1 - c) * h, h)], outs[a], ssem.at[a], rsem.at[a], (x, y, 1 - c)))
        return copies

    def first(self, srcs, outs, sems):
        for cp in self._copies(srcs, outs, sems):
            cp.start()

    def last(self, srcs, outs, sems):
        for cp in self._copies(srcs, outs, sems):
            cp.wait()


def _reduce_finish(reds, name):
    n = len(reds)

    def body(*refs):
        outs = refs[n:2 * n]
        ssem, rsem = refs[2 * n:]
        x, y, c = _me()
        copies = []
        for a in range(n):
            h = reds[a].shape[0] // 2
            mine = outs[a].at[pl.ds(c * h, h)]
            copies.append(_remote(mine, mine, ssem.at[a], rsem.at[a], (x, y, 1 - c)))
        for cp in copies:
            cp.start()
        for a in range(n):
            h = reds[a].shape[0] // 2
            dst = outs[a].at[pl.ds((1 - c) * h, h)]
            _remote(dst, dst, ssem.at[a], rsem.at[a], (x, y, 1 - c)).wait_recv()
        for cp in copies:
            cp.wait_send()

    return _pcall(
        body, name=name, in_specs=[_HBM] * n, out_specs=[_HBM] * n,
        out_shape=[jax.ShapeDtypeStruct(t.shape, t.dtype) for t in reds],
        input_output_aliases={a: a for a in range(n)},
        scratch_shapes=[pltpu.SemaphoreType.DMA((n,)), pltpu.SemaphoreType.DMA((n,))],
    )(*reds)


def _half_sum(fn, full, halves, out_full, out_dtype, core, name):
    p, h, c = (halves[0].shape if halves else (full[0].shape[0], full[0].shape[1] // 2, full[0].shape[2]))
    br = _div(h, max(16, (1 << 19) // (p * c)), 16)
    nb = h // br
    mine3 = pl.BlockSpec((p, br, c), lambda i, core_ref: (0, core_ref[0] * nb + i, 0))
    half3 = pl.BlockSpec((p, br, c), lambda i, core_ref: (0, i, 0))

    def body(core_ref, *refs):
        refs[-1][...] = fn(*[t[...].astype(F32) for t in refs[:-1]]).astype(out_dtype)

    if out_full:
        out_spec = pl.BlockSpec((br, c), lambda i, core_ref: (core_ref[0] * nb + i, 0))
        out_shape = jax.ShapeDtypeStruct((2 * h, c), out_dtype)
    else:
        out_spec, out_shape = half3, jax.ShapeDtypeStruct((p, h, c), out_dtype)
    return _pcall(
        body, name=name,
        grid_spec=pltpu.PrefetchScalarGridSpec(
            num_scalar_prefetch=1, grid=(nb,), in_specs=[mine3] * len(full) + [half3] * len(halves),
            out_specs=out_spec),
        out_shape=out_shape, compiler_params=_cparams(("parallel",)),
    )(core, *full, *halves)


def _ada_fwd(c_all, w, b):
    def body(c_ref, w_ref, b_ref, o_ref):
        o_ref[...] = jnp.dot(c_ref[...], w_ref[...], precision=HI, preferred_element_type=F32) + b_ref[...]

    return _pcall(body, name="ada_fwd", out_shape=jax.ShapeDtypeStruct((c_all.shape[0], w.shape[1]), F32),
                  compiler_params=pltpu.CompilerParams(vmem_limit_bytes=VMEM_LIMIT))(c_all, w, b)


def _ada_bwd(c_all_t, d):
    def body(c_ref, d_ref, o_ref):
        o_ref[...] = jnp.dot(c_ref[...], d_ref[...], precision=HI, preferred_element_type=F32)

    return _pcall(body, name="ada_bwd", out_shape=jax.ShapeDtypeStruct((c_all_t.shape[0], d.shape[1]), F32),
                  compiler_params=pltpu.CompilerParams(vmem_limit_bytes=VMEM_LIMIT))(c_all_t, d)


def _sum_lead(x, name):
    p, r, n = x.shape
    br = _div(r, 512, 8)

    def body(x_ref, o_ref):
        acc = x_ref[0]
        for j in range(1, p):
            acc = acc + x_ref[j]
        o_ref[...] = acc

    return _pcall(
        body, name=name, grid=(r // br,), in_specs=[pl.BlockSpec((p, br, n), lambda i: (0, i, 0))],
        out_specs=pl.BlockSpec((br, n), lambda i: (i, 0)), out_shape=jax.ShapeDtypeStruct((r, n), F32),
        compiler_params=_cparams(("parallel",)),
    )(x)


def _adamw(w, g, m, v, name):
    shape = w.shape
    cols = shape[-1]
    w2, g2, m2, v2 = [t.reshape(-1, cols) for t in (w, g, m, v)]
    rows = w2.shape[0]
    pref = max(8, (1 << 19) // cols // 8 * 8)
    br = _div(rows, pref, 8)
    if rows // br > 64:
        br = pref
    outs = _rows_fwd(_f_adamw, [(t, cols, 0) for t in (w2, g2, m2, v2)], [], [(cols, F32)] * 3, name=name, br=br)
    return [o.reshape(shape) for o in outs]


_BIG = (("w_in", 1), ("w_up", 1), ("w_down", 0), ("w_o", 0), ("w_rwkv_out", 0), ("w_att_out", 1), ("w2", 1), ("a2", 1),
        ("g2", 1))


_NEEDED_FIRST = ("w_in", "w_att_out", "w2", "a2", "g2")
_NEEDED_LATER = ("w_up", "w_down", "w_o", "w_rwkv_out")
_DONE_EARLY = ("w_up", "w_down", "w_o", "w_rwkv_out", "w_att_out")
_DONE_LATE = ("w_in", "w2", "a2", "g2")


def _cols_joined(t):
    return jnp.concatenate([t[j] for j in range(4)], axis=1)


def _cols_split(t):
    n = t.shape[1] // 4
    return jnp.stack([t[:, j * n:(j + 1) * n] for j in range(4)])


W_IN_SHARD = (N_ATT + N_RW + N_GATE) // 4
W_IN_PAD = 2560


def _row_window(parts, lo, hi):
    out, pos = [], 0
    for t, w in parts:
        a, b = max(lo, pos), min(hi, pos + w)
        if a < b:
            out.append(t[a - pos:b - pos])
        pos += w
    return out[0] if len(out) == 1 else jnp.concatenate(out, axis=0)


def _rows_joined(t):
    return t.reshape(4 * t.shape[1], t.shape[2])


def _rows_split(t):
    return t.reshape(4, t.shape[0] // 4, t.shape[1])


def _step_to_scan(x, tgt, ada, wts):
    sh1, sc1, gt1, sh2, sc2, gt2 = ada
    br = 256
    grp = lax.broadcasted_iota(jnp.int32, (D, 128), 0) // 64 == lax.broadcasted_iota(jnp.int32, (D, 128), 1)
    e = grp.astype(F32)
    et = e.T
    w_in = [(wts["w_in"][j], W_IN_SHARD) for j in range(4)]
    w_att = _row_window(w_in, 0, N_ATT)
    w_rw = jnp.concatenate([_row_window(w_in, N_ATT, N_ATT + N_RW), jnp.zeros((N_RWP - N_RW, D), BF16)], axis=0)
    w_gate = _row_window(w_in, N_ATT + N_RW, N_ATT + N_RW + N_GATE)
    mu = jnp.pad(wts["mu_shift"], ((0, 0), (0, N_RWP - N_RW)))
    wl = jnp.zeros((N_LORA, 3 * D), F32)
    wl = wl.at[0:64, 0:D].set(_cols_joined(wts["w2"]).astype(F32))
    wl = wl.at[64:128, D:2 * D].set(_cols_joined(wts["a2"]).astype(F32))
    wl = wl.at[128:288, 2 * D:3 * D].set(_cols_joined(wts["g2"]).astype(F32))
    pre1_c = [wts["norm1_w"], sc1, sh1]
    (h1,) = _rows_fwd(_f_pre, [(x, D, 0)], pre1_c, [(D, BF16), None], name="pre1_fwd", br=2 * br)
    att_in = _mm(h1, w_att, tb=True, name="mm_att_in")
    z = _mm(h1, w_rw, tb=True, out_dtype=BF16, name="mm_rw_in")
    gate_in = _mm(h1, w_gate, tb=True, out_dtype=BF16, name="mm_gate_in")
    att_o, att_l = [], []
    for g, (_, dil) in enumerate(ATT_PATTERNS):
        o, l = _att_fwd(att_in, g, dil)
        att_o.append(o)
        att_l.append(l)
    comb_rows = [(t, ATT_WIDTH, 0) for t in att_o + att_l]
    (att,) = _rows_fwd(_f_comb, comb_rows, [], [(ATT_WIDTH, BF16)], name="comb_fwd", br=2 * br)
    w_ao = _cols_joined(wts["w_att_out"])
    y_att = _mm(att, w_ao, out_dtype=BF16, name="mm_att_out")
    rwpre_c = [wts["w0"], wts["a0"], wts["k_k"], wts["k_a"], wl, e, et]

    def shift_and_rwpre(zz, *rest):
        consts, mu_row, before = rest[:-2], rest[-2], rest[-1]
        last = jnp.sum(jnp.where(lax.broadcasted_iota(jnp.int32, before.shape, 0) == HALO - 1, before, 0.0), axis=0,
                       keepdims=True)
        row = lax.broadcasted_iota(jnp.int32, zz.shape, 0)
        zprev = jnp.where(row == 0, last, pltpu.roll(zz, 1, 0))
        shifted = zz + (zprev - zz) * mu_row
        return (shifted,) + tuple(_f_rwpre(shifted, *consts))

    zs, lw, km, aa, bb, gg = _rows_fwd(
        shift_and_rwpre, [(z, N_RWP, 0)], rwpre_c + [mu],
        [(N_RWP, F32), None, (D, F32), (D, F32), None, (D, F32), (D, F32), (D, F32)], name="rwpre_fwd", br=br, halo=0)
    return dict(x=x, tgt=tgt, wts=wts, br=br, e=e, et=et, gt1=gt1, sc2=sc2, sh2=sh2, gt2=gt2, w_att=w_att, w_rw=w_rw,
                w_ao=w_ao,
                w_gate=w_gate, mu=mu, pre1_c=pre1_c, h1=h1, att_in=att_in, z=z, gate_in=gate_in, comb_rows=comb_rows,
                att=att, y_att=y_att, zs=zs, rwpre_c=rwpre_c, lw=lw, km=km, aa=aa, bb=bb, gg=gg)


def _step_between_scans(st, y_raw, late):
    x, tgt, wts, br, e, et = st["x"], st["tgt"], st["wts"], st["br"], st["e"], st["et"]
    zs, km, gg, gate_in, y_att, att = st["zs"], st["km"], st["gg"], st["gate_in"], st["y_att"], st["att"]
    comb_rows, att_in = st["comb_rows"], st["att_in"]
    gt1, sc2, sh2, gt2 = st["gt1"], st["sc2"], st["sh2"], st["gt2"]
    w_up, w_ao = late["w_up"], st["w_ao"]
    w_down, w_o, w_ro = _rows_joined(late["w_down"]), _rows_joined(late["w_o"]), _rows_joined(late["w_rwkv_out"])
    post_rows = [(y_raw, D, 0), (zs, D, 0), (zs, D, 2), (km, D, 0), (gg, D, 0)]
    post_c = [wts["lnx_w"], wts["lnx_b"], wts["r_k"], e, et]
    (rw_out,) = _rows_fwd(_f_rwpost, post_rows, post_c, [(D, BF16)], name="rwpost_fwd", br=br)
    y_rw = _mm(rw_out, w_ro, out_dtype=BF16, name="mm_rw_out")
    mix_rows = [(gate_in, N_GATE, 0), (y_att, D, 0), (y_rw, D, 0)]
    (mix,) = _rows_fwd(_f_mix, mix_rows, [wts["b_gate"]], [(D, BF16)], name="mix_fwd", br=2 * br)
    o = _mm(mix, w_o, out_dtype=BF16, name="mm_o")
    pre2_c = [gt1, wts["norm2_w"], sc2, sh2]
    x1, h2 = _rows_fwd(_f_pre2, [(x, D, 0), (o, D, 0)], pre2_c, [(D, F32), (D, BF16)], name="pre2_fwd", br=2 * br)
    u = _mm(h2, w_up, b_chip=True, name="mm_up")
    act = _conv_fwd(u, wts["conv_w"], wts["conv_b"])
    f = _mm(act, w_down, out_dtype=BF16, name="mm_down")
    fin_rows = [(x1, D, 0), (f, D, 0), (tgt, D, 0)]
    fin_c = [gt2, wts["norm_f_w"]]

    def fin_fwd(*a):
        (l,) = _f_fin(*a)
        return (jnp.broadcast_to(jnp.sum(l, axis=0, keepdims=True), (8, 128)),)

    (loss_acc,) = _rows_fwd(fin_fwd, fin_rows, fin_c, [], name="fin_fwd", br=2 * br, acc_shape=(8, 128))

    gw = {}
    dx1a, df, d_gt2, gw["norm_f_w"] = _rows_bwd(
        _f_fin, fin_rows, fin_c, [[]], wrt_rows=[0, 1], wrt_consts=[0, 1], drow_dtypes=[F32, BF16],
        name="fin_bwd", br=2 * br, unit_cot=True)
    dact = _mm(df, w_down, tb=True, name="mm_dact")
    gw["w_down"] = _rows_split(_mm(act, df, ta=True, out_dtype=BF16, name="mm_dw_down"))
    du, gw["conv_w"], gw["conv_b"] = _conv_bwd(u, wts["conv_w"], wts["conv_b"], dact)
    dh2 = _mm(du, w_up, tb=True, b_chip=True, out_dtype=BF16, name="mm_dh2")
    gw["w_up"] = _mm(h2, du, ta=True, out_chip=True, out_dtype=BF16, name="mm_dw_up")
    dxa, do, d_gt1, gw["norm2_w"], d_sc2, d_sh2 = _rows_bwd(
        _f_pre2, [(x, D, 0), (o, D, 0)], pre2_c, [[(dx1a, D, 0)], [(dh2, D, 0)]], wrt_rows=[0, 1],
        wrt_consts=[0, 1, 2, 3], drow_dtypes=[F32, BF16], name="pre2_bwd", br=2 * br)
    dmix = _mm(do, w_o, tb=True, out_dtype=BF16, name="mm_dmix")
    gw["w_o"] = _rows_split(_mm(mix, do, ta=True, out_dtype=BF16, name="mm_dw_o"))
    dgate, dya, dyr, gw["b_gate"] = _rows_bwd(
        _f_mix, mix_rows, [wts["b_gate"]], [[(dmix, D, 0)]], wrt_rows=[0, 1, 2], wrt_consts=[0],
        drow_dtypes=[BF16] * 3, name="mix_bwd", br=2 * br)
    datt = _mm(dya, w_ao, tb=True, out_dtype=BF16, name="mm_datt")
    gw["w_att_out"] = _mm(att, dya, ta=True, out_chip=True, out_dtype=BF16, name="mm_dw_att_out")
    drw = _mm(dyr, w_ro, tb=True, out_dtype=BF16, name="mm_drw")
    gw["w_rwkv_out"] = _rows_split(_mm(rw_out, dyr, ta=True, out_dtype=BF16, name="mm_dw_rw_out"))
    dcomb = _rows_bwd(_f_comb, comb_rows, [], [[(datt, ATT_WIDTH, 0)]], wrt_rows=list(range(6)), wrt_consts=[],
                      drow_dtypes=[F32] * 6, name="comb_bwd", br=2 * br)
    datt_in = None
    for g, (_, dil) in enumerate(ATT_PATTERNS):
        datt_in = _att_bwd(att_in, g, dil, dcomb[g], dcomb[3 + g], datt_in)
    dy_raw, dr_p, dv_p, dkm_p, dgg, gw["lnx_w"], gw["lnx_b"], gw["r_k"], *recv_early = _rows_bwd(
        _f_rwpost, post_rows, post_c, [[(drw, D, 0)]], wrt_rows=[0, 1, 2, 3, 4], wrt_consts=[0, 1, 2],
        drow_dtypes=[F32] * 5, name="rwpost_bwd", br=br, comm=_SiblingHalves([gw[n] for n in _DONE_EARLY]))
    st.update(loss=loss_acc[0, 0], gw=gw, dxa=dxa, dgate=dgate, datt_in=datt_in,
              dy_raw=dy_raw, dr_p=dr_p, dv_p=dv_p, dkm_p=dkm_p, dgg=dgg, d_ada_late=(d_gt1, d_sh2, d_sc2, d_gt2),
              recv_early=recv_early)
    return st


def _chip_parts(grads, recv, names, core):
    return [_half_sum(lambda a, b: a + b, [g], [r], False, BF16, core, "reduce_add2_" + n)
            for g, r, n in zip(grads, recv, names)]


def _step_after_scan(st, scan_grads, core):
    x, br, gw, h1, zs = st["x"], st["br"], st["gw"], st["h1"], st["zs"]
    dr_s, dlw, dkm_s, dv_s, daa, dbb = scan_grads
    pre_cots = [[(st["dr_p"], D, 0), (dr_s, D, 0)], [(dlw, D, 0)], [(st["dkm_p"], D, 0), (dkm_s, D, 0)],
                [(st["dv_p"], D, 0), (dv_s, D, 0)], [(daa, D, 0)], [(dbb, D, 0)], [(st["dgg"], D, 0)]]
    dz, dmu, gw["w0"], gw["a0"], gw["k_k"], gw["k_a"], dwl = _rwpre_shift_bwd(
        zs, st["z"], st["mu"], st["rwpre_c"], pre_cots, br=128)
    gw["w2"], gw["a2"] = _cols_split(dwl[0:64, 0:D]), _cols_split(dwl[64:128, D:2 * D])
    gw["g2"] = _cols_split(dwl[128:288, 2 * D:3 * D])
    gw["mu_shift"] = dmu[:, :N_RW]
    datt_in, dgate = st["datt_in"], st["dgate"]
    dw_in = [(_mm(datt_in, h1, ta=True, out_dtype=BF16, name="mm_dw_att"), N_ATT),
             (_mm(dz, h1, ta=True, out_dtype=BF16, name="mm_dw_rw"), N_RW),
             (_mm(dgate, h1, ta=True, out_dtype=BF16, name="mm_dw_gate"), N_GATE)]
    slabs = []
    for j in range(4):
        slabs += [_row_window(dw_in, j * W_IN_SHARD, (j + 1) * W_IN_SHARD), jnp.zeros((W_IN_PAD - W_IN_SHARD, D), BF16)]
    gw["w_in"] = jnp.concatenate(slabs, axis=0).reshape(4, W_IN_PAD, D)
    late = [gw[n] for n in _DONE_LATE]
    parts = _chip_parts(late, _run_comm(_SiblingHalves(late), "reduce_sib_late"), _DONE_LATE, core)
    dh1, slots_late = _mm_sum([(datt_in, st["w_att"]), (dz, st["w_rw"]), (dgate, st["w_gate"])],
                              comm=_ScatterToChips(parts), name="mm_dh1")
    grad_x, gw["norm1_w"], d_sc1, d_sh1 = _rows_bwd(
        _f_pre, [(x, D, 0)], st["pre1_c"], [[(dh1, D, 0)], [(st["dxa"], D, 0)]], wrt_rows=[0], wrt_consts=[0, 1, 2],
        drow_dtypes=[F32], name="pre1_bwd", br=2 * br)
    d_gt1, d_sh2, d_sc2, d_gt2 = st["d_ada_late"]
    return st["loss"], grad_x, (d_sh1, d_sc1, d_gt1, d_sh2, d_sc2, d_gt2), gw, slots_late


_SMALL = ("b_ada", "norm1_w", "b_gate", "mu_shift", "w0", "a0", "k_k", "k_a", "r_k", "lnx_w", "lnx_b", "norm2_w",
          "conv_b", "norm_f_w")
_NAMES = ("w_ada", "b_ada", "norm1_w", "w_in", "b_gate", "mu_shift", "w0", "w2", "a0", "a2", "g2", "k_k", "k_a", "r_k",
          "lnx_w", "lnx_b", "w_att_out", "w_rwkv_out", "w_o", "norm2_w", "w_up", "conv_w", "conv_b", "w_down",
          "norm_f_w")


def kernel(x, c, w_ada, b_ada, norm1_w, w_in, b_gate, mu_shift, w0, w2, a0, a2, g2, k_k, k_a, r_k, lnx_w, lnx_b, w_att_out, w_rwkv_out, w_o, norm2_w, w_up, conv_w, conv_b, w_down, norm_f_w, loss_target, m_w_ada, m_b_ada, m_norm1_w, m_w_in, m_b_gate, m_mu_shift, m_w0, m_w2, m_a0, m_a2, m_g2, m_k_k, m_k_a, m_r_k, m_lnx_w, m_lnx_b, m_w_att_out, m_w_rwkv_out, m_w_o, m_norm2_w, m_w_up, m_conv_w, m_conv_b, m_w_down, m_norm_f_w, v_w_ada, v_b_ada, v_norm1_w, v_w_in, v_b_gate, v_mu_shift, v_w0, v_w2, v_a0, v_a2, v_g2, v_k_k, v_k_a, v_r_k, v_lnx_w, v_lnx_b, v_w_att_out, v_w_rwkv_out, v_w_o, v_norm2_w, v_w_up, v_conv_w, v_conv_b, v_w_down, v_norm_f_w):
    args = dict(locals())
    p, pm, pv = {}, {}, {}
    for name in _NAMES:
        for dst, key in ((p, name), (pm, "m_" + name), (pv, "v_" + name)):
            t = args[key]
            if name == "w_in":
                dst[name] = jnp.swapaxes(t, 1, 2)[0]
            else:
                dst[name] = t.reshape(1, -1) if name in ("r_k", "norm_f_w") else t.reshape(t.shape[-2], t.shape[-1])
    xi, yi, ci = _me()
    chip = 2 * xi + yi
    dev = 4 * xi + 2 * yi + ci
    x2, tgt = x[0], loss_target[0]

    n_cw = 3 * (2 * D_FF // 4)
    vec = jnp.concatenate([c.reshape(-1), p["conv_w"].reshape(-1), jnp.zeros((8 * D - D - n_cw,), F32)]).reshape(8, D)
    g0 = _allgather8(vec, "gather_c").reshape(8, 8 * D)
    c_all = g0[:, :D]
    conv_w_full = jnp.concatenate([g0[2 * j, D:D + n_cw].reshape(3, -1) for j in range(4)], axis=1)
    n_ada = 6 * D // 4
    b_ada_sh = lax.dynamic_slice(p["b_ada"], (0, chip * n_ada), (1, n_ada))
    ada_sh = _ada_fwd(c_all, p["w_ada"], b_ada_sh)
    ga = _allgather8(ada_sh, "gather_ada")
    ada_all = jnp.concatenate([ga[2 * j] for j in range(4)], axis=1)
    ada_row = lax.dynamic_slice(ada_all, (dev, 0), (1, 6 * D))
    ada = [ada_row[:, j * D:(j + 1) * D] for j in range(6)]

    big = [n for n, _ in _BIG]
    shard = {n: p[n].astype(BF16) for n in big}
    shard["w_in"] = jnp.pad(shard["w_in"], ((0, W_IN_PAD - W_IN_SHARD), (0, 0)))
    wts = dict(zip(_NEEDED_FIRST, _run_comm(_GatherWeights([shard[n] for n in _NEEDED_FIRST]), "gather_w")))
    for n in _SMALL:
        wts[n] = p[n]
    wts["conv_w"] = conv_w_full
    core = ci.reshape(1).astype(jnp.int32)

    st = _step_to_scan(x2, tgt, ada, wts)
    y_raw, s0s, inverses, late = _scan_fwd(st["zs"], st["lw"], st["km"], st["aa"], st["bb"],
                                           _GatherWeights([shard[n] for n in _NEEDED_LATER]))
    st = _step_between_scans(st, y_raw, dict(zip(_NEEDED_LATER, late)))
    early = _chip_parts([st["gw"][n] for n in _DONE_EARLY], st["recv_early"], _DONE_EARLY, core)
    scan_grads, slots_early = _scan_bwd(st["zs"], st["lw"], st["km"], st["aa"], st["bb"], s0s, inverses,
                                        st["dy_raw"], _ScatterToChips(early))
    loss_part, grad_x, d_ada, gw, slots_late = _step_after_scan(st, scan_grads, core)

    small = [jnp.concatenate(d_ada, axis=1)] + [gw[n] for n in _SMALL[1:]] + [gw["conv_w"], loss_part.reshape(1, 1)]
    sizes = [t.size for t in small]
    flat = jnp.concatenate([t.reshape(-1) for t in small])
    npad = (-flat.shape[0]) % (8 * D)
    srows = (flat.shape[0] + npad) // D
    flat = jnp.concatenate([flat, jnp.zeros((npad,), F32)]).reshape(srows, D)
    parts = _allgather8(flat, "gather_small")
    tot = _sum_lead(parts, "sum_small").reshape(-1)
    pieces, pos = [], 0
    for sz in sizes:
        pieces.append(tot[pos:pos + sz])
        pos += sz
    grads = {}
    for n, piece in zip(_SMALL, pieces[:len(_SMALL)]):
        grads[n] = piece.reshape(p[n].shape)
    conv_w_grad = pieces[len(_SMALL)].reshape(3, 2 * D_FF)
    grads["conv_w"] = lax.dynamic_slice(conv_w_grad, (0, chip * (n_cw // 3)), (3, n_cw // 3))
    loss = pieces[-1][0]
    d_ada_all = parts[:, :6].reshape(8, 6 * D)
    grads["w_ada"] = _ada_bwd(c_all.T, lax.dynamic_slice(d_ada_all, (0, chip * n_ada), (8, n_ada)))

    order = _DONE_EARLY + _DONE_LATE
    reds = [_half_sum(lambda t: t[0] + t[1] + t[2] + t[3], [], [t], True, F32, core, "reduce_add4_" + n)
            for n, t in zip(order, list(slots_early) + list(slots_late))]
    for n, g in zip(order, _reduce_finish(reds, "reduce_sib2")):
        grads[n] = g

    outs_g, outs_d, outs_m, outs_v = [], [], [], []
    grads["w_in"] = grads["w_in"][:W_IN_SHARD]
    for name in _NAMES:
        g = grads[name]
        d, m, v = _adamw(p[name], g, pm[name], pv[name], "adamw_" + name)
        shape = args[name].shape
        for outs, t in ((outs_g, g), (outs_d, d), (outs_m, m), (outs_v, v)):
            outs.append(jnp.swapaxes(t[None], 1, 2) if name == "w_in" else t.reshape(shape))
    return (loss, grad_x.reshape(x.shape), *outs_g, *outs_d, *outs_m, *outs_v)
```

```python
import functools
import math

import jax
import jax.numpy as jnp
from jax import lax
from jax.experimental import pallas as pl
from jax.experimental.pallas import tpu as pltpu

F32 = jnp.float32
BF16 = jnp.bfloat16
HI = lax.Precision.HIGHEST
MESH = pl.DeviceIdType.MESH

D = 1024
ATT_PATTERNS = ((128, 1), (512, 4), (2048, 16))
ATT_BLOCK = 128
ATT_WIDTH = 512
N_ATT = 3 * 3 * ATT_WIDTH
N_RW = 3 * D + 64 + 64 + 160
N_RWP = 3456
N_LORA = N_RWP - 3 * D
N_GATE = 2 * D
D_FF = 2816
RMS_EPS = 1e-6
GN_EPS = 64e-5
SCAN_CHUNK = 64
SCAN_PAIRS = 8
NEG = -1e30
VMEM_LIMIT = 48 * 1024 * 1024
SMALL_OPERAND = 16384
HALO = 16

ADAM_LR, ADAM_B1, ADAM_B2, ADAM_EPS, ADAM_WD, ADAM_STEP = 0.001, 0.9, 0.999, 1e-08, 0.01, 10


def _pcall(body, **kw):
    call = pl.pallas_call(body, **kw)

    def run(*args):
        return call(*[pltpu.with_memory_space_constraint(a, pltpu.HBM)
                      if jnp.issubdtype(a.dtype, jnp.floating) and a.size <= SMALL_OPERAND else a for a in args])

    return run


def _cparams(sem):
    return pltpu.CompilerParams(dimension_semantics=sem, vmem_limit_bytes=VMEM_LIMIT)


def _div(n, pref, mult):
    best = None
    d = mult
    while d <= min(n, pref):
        if n % d == 0:
            best = d
        d += mult
    return best if best else n


def _dg(a, b, ca, cb):
    return lax.dot_general(a.astype(BF16), b.astype(BF16), (((ca,), (cb,)), ((), ())), preferred_element_type=F32)


@jax.custom_vjp
def _nn(a, b):
    return _dg(a, b, 1, 0)


@jax.custom_vjp
def _nt(a, b):
    return _dg(a, b, 1, 1)


@jax.custom_vjp
def _tn(a, b):
    return _dg(a, b, 0, 0)


_nn.defvjp(lambda a, b: (_nn(a, b), (a, b)), lambda res, g: (_nt(g, res[1]), _tn(res[0], g)))
_nt.defvjp(lambda a, b: (_nt(a, b), (a, b)), lambda res, g: (_nn(g, res[1]), _tn(g, res[0])))
_tn.defvjp(lambda a, b: (_tn(a, b), (a, b)), lambda res, g: (_nt(res[1], g), _nn(res[0], g)))


def _bdg(a, b, ca, cb):
    return lax.dot_general(a.astype(BF16), b.astype(BF16), (((ca,), (cb,)), ((0,), (0,))), preferred_element_type=F32)


@jax.custom_vjp
def _bnn(a, b):
    return _bdg(a, b, 2, 1)


@jax.custom_vjp
def _bnt(a, b):
    return _bdg(a, b, 2, 2)


@jax.custom_vjp
def _btn(a, b):
    return _bdg(a, b, 1, 1)


_bnn.defvjp(lambda a, b: (_bnn(a, b), (a, b)), lambda res, g: (_bnt(g, res[1]), _btn(res[0], g)))
_bnt.defvjp(lambda a, b: (_bnt(a, b), (a, b)), lambda res, g: (_bnn(g, res[1]), _btn(g, res[0])))
_btn.defvjp(lambda a, b: (_btn(a, b), (a, b)), lambda res, g: (_bnt(res[1], g), _bnn(res[0], g)))


def _hsum_impl(x, e, et):
    eb, etb = e.astype(BF16), et.astype(BF16)
    s = jnp.dot(x.astype(BF16), eb, preferred_element_type=F32)
    return jnp.dot(s.astype(BF16), etb, preferred_element_type=F32)


@jax.custom_vjp
def _hsum(x, e, et):
    return _hsum_impl(x, e, et)


_hsum.defvjp(lambda x, e, et: (_hsum_impl(x, e, et), (e, et)),
             lambda res, g: (_hsum_impl(g, res[0], res[1]), jnp.zeros_like(res[0]), jnp.zeros_like(res[1])))


def _mm(a, b, *, ta=False, tb=False, out_dtype=F32, add=None, b_chip=False, out_chip=False, comm=None, name):
    riding = _NOTHING if comm is None else comm
    nc = riding.n
    if ta:
        kdim, m = a.shape
    else:
        m, kdim = a.shape
    if b_chip:
        n = b.shape[1] if tb else 4 * b.shape[2]
    else:
        n = b.shape[0] if tb else b.shape[1]
    tm, tn, tk = _div(m, 1536, 128), _div(n, 1536, 128), _div(kdim, 2048 if ta else 1408, 128)
    if b_chip and tb:
        tk = kdim // 4
    if (b_chip and not tb) or out_chip:
        tn = n // 4
    nk = kdim // tk
    ca, cb = (0 if ta else 1), (1 if tb else 0)

    nin = 2 if add is None else 3
    gi, gj = m // tm, n // tn

    def body(*refs):
        a_ref, b_ref = refs[0], refs[1]
        add_ref = None if add is None else refs[2]
        o_ref = refs[nin + nc]
        step = (pl.program_id(0) * gj + pl.program_id(1)) * nk + pl.program_id(2)
        before, after = _comm_phases(riding, refs[nin:nin + nc] + refs[nin + nc + 1:nin + 2 * nc + 1]
                                     + refs[nin + 2 * nc + 1 + (nk > 1):], gi * gj * nk, step)
        before()
        part = lax.dot_general(a_ref[...], b_ref[...], (((ca,), (cb,)), ((), ())), preferred_element_type=F32)

        def finish(r):
            if add_ref is not None:
                r = r + add_ref[...]
            o_ref[...] = r.astype(o_ref.dtype)

        if nk == 1:
            finish(part)
            after()
            return
        acc = refs[nin + 2 * nc + 1]
        k = pl.program_id(2)

        @pl.when(k == 0)
        def _():
            acc[...] = part

        @pl.when(k > 0)
        def _():
            acc[...] += part

        @pl.when(k == nk - 1)
        def _():
            finish(acc[...])

        after()

    a_spec = pl.BlockSpec((tk, tm), lambda i, j, k: (k, i)) if ta else pl.BlockSpec((tm, tk), lambda i, j, k: (i, k))
    if b_chip:
        b_spec = (pl.BlockSpec((None, tn, tk), lambda i, j, k: (k, j, 0)) if tb
                  else pl.BlockSpec((None, tk, tn), lambda i, j, k: (j, k, 0)))
    else:
        b_spec = pl.BlockSpec((tn, tk), lambda i, j, k: (j, k)) if tb else pl.BlockSpec((tk, tn), lambda i, j, k: (k, j))
    in_specs = [a_spec, b_spec]
    args = [a, b]
    if add is not None:
        in_specs.append(pl.BlockSpec((tm, tn), lambda i, j, k: (i, j)))
        args.append(add)
    if out_chip:
        out_spec = pl.BlockSpec((None, tm, tn), lambda i, j, k: (j, i, 0))
        out_shape = jax.ShapeDtypeStruct((4, m, tn), out_dtype)
    else:
        out_spec = pl.BlockSpec((tm, tn), lambda i, j, k: (i, j))
        out_shape = jax.ShapeDtypeStruct((m, n), out_dtype)
    res = _pcall(
        body, name=name, grid=(gi, gj, nk), in_specs=in_specs + [_HBM] * nc, out_specs=[out_spec] + [_HBM] * nc,
        out_shape=[out_shape] + riding.out_shape,
        scratch_shapes=([] if nk == 1 else [pltpu.VMEM((tm, tn), F32)]) + riding.sems,
        compiler_params=_cparams(("arbitrary",) * 3 if nc else ("parallel", "parallel", "arbitrary")),
    )(*args, *riding.ins)
    return res[0] if comm is None else (res[0], res[1:])


def _mm_sum(pairs, *, comm, name):
    m, n = pairs[0][0].shape[0], pairs[0][1].shape[1]
    tm, tn = _div(m, 1024, 128), _div(n, 1024, 128)
    tks = [_div(a.shape[1], 1408, 128) for a, _ in pairs]
    nks = [a.shape[1] // tk for (a, _), tk in zip(pairs, tks)]
    offs = [sum(nks[:p]) for p in range(len(pairs))]
    total, npair, nc = sum(nks), len(pairs), comm.n
    gi, gj = m // tm, n // tn

    def body(*refs):
        o_ref, acc = refs[2 * npair + nc], refs[2 * npair + 2 * nc + 1]
        k = pl.program_id(2)
        step = (pl.program_id(0) * gj + pl.program_id(1)) * total + k
        before, after = _comm_phases(comm, refs[2 * npair:2 * npair + nc]
                                     + refs[2 * npair + nc + 1:2 * npair + 2 * nc + 1]
                                     + refs[2 * npair + 2 * nc + 2:], gi * gj * total, step)
        before()
        for p in range(npair):
            def partial_product(p=p):
                part = jnp.dot(refs[2 * p][...], refs[2 * p + 1][...], preferred_element_type=F32)
                if p == 0:
                    @pl.when(k == 0)
                    def _():
                        acc[...] = part

                    @pl.when(k > 0)
                    def _():
                        acc[...] += part
                else:
                    acc[...] += part

            pl.when(jnp.logical_and(k >= offs[p], k < offs[p] + nks[p]))(partial_product)

        @pl.when(k == total - 1)
        def _():
            o_ref[...] = acc[...].astype(o_ref.dtype)

        after()

    def specs(tk, off, nk):
        def kb(k):
            return jnp.clip(k - off, 0, nk - 1)
        return [pl.BlockSpec((tm, tk), lambda i, j, k: (i, kb(k))), pl.BlockSpec((tk, tn), lambda i, j, k: (kb(k), j))]

    in_specs, args = [], []
    for (a, b), tk, off, nk in zip(pairs, tks, offs, nks):
        in_specs += specs(tk, off, nk)
        args += [a, b]
    res = _pcall(
        body, name=name, grid=(gi, gj, total), in_specs=in_specs + [_HBM] * nc,
        out_specs=[pl.BlockSpec((tm, tn), lambda i, j, k: (i, j))] + [_HBM] * nc,
        out_shape=[jax.ShapeDtypeStruct((m, n), BF16)] + comm.out_shape,
        scratch_shapes=[pltpu.VMEM((tm, tn), F32)] + comm.sems,
        compiler_params=_cparams(("arbitrary",) * 3),
    )(*args, *comm.ins)
    return res[0], res[1:]


def _row_spec(br, w, cb):
    return pl.BlockSpec((br, w), lambda i: (i, cb))


def _const_spec(shape):
    return pl.BlockSpec(shape, lambda i: (0,) * len(shape))


def _rows_fwd(fn, rows, consts, outs, *, name, br, acc_shape=None, halo=None):
    s = rows[0][0].shape[0]
    nr, nc = len(rows), len(consts)
    kept = [k for k, o in enumerate(outs) if o is not None]

    def body(*refs):
        xs = [r[...].astype(F32) for r in refs[:nr]]
        cs = [c[...] for c in refs[nr:nr + nc]]
        if halo is not None:
            cs.append(jnp.where(pl.program_id(0) == 0, 0.0, refs[nr + nc][...].astype(F32)))
        res = fn(*xs, *cs)
        orefs = refs[nr + nc + (halo is not None):]
        for j, k in enumerate(kept):
            orefs[j][...] = res[k].astype(orefs[j].dtype)
        if acc_shape is not None:
            acc_ref = orefs[len(kept)]

            @pl.when(pl.program_id(0) == 0)
            def _():
                acc_ref[...] = jnp.zeros_like(acc_ref)

            acc_ref[...] += res[len(outs)]

    in_specs = [_row_spec(br, w, cb) for (_, w, cb) in rows] + [_const_spec(c.shape) for c in consts]
    args = [r[0] for r in rows] + list(consts)
    if halo is not None:
        harr, hw, hcb = rows[halo]
        in_specs.append(pl.BlockSpec((HALO, hw), lambda i: (jnp.maximum(i * (br // HALO) - 1, 0), hcb)))
        args.append(harr)
    out_specs = [_row_spec(br, outs[k][0], 0) for k in kept]
    out_shape = [jax.ShapeDtypeStruct((s, outs[k][0]), outs[k][1]) for k in kept]
    if acc_shape is not None:
        out_specs.append(_const_spec(acc_shape))
        out_shape.append(jax.ShapeDtypeStruct(acc_shape, F32))
    return _pcall(
        body, name=name, grid=(pl.cdiv(s, br),), in_specs=in_specs, out_specs=out_specs, out_shape=out_shape,
        compiler_params=_cparams(("arbitrary",)),
    )(*args)


def _rows_bwd(fn, rows, consts, cots, *, wrt_rows, wrt_consts, drow_dtypes, name, br, unit_cot=False, comm=None):
    comm = _NOTHING if comm is None else comm
    ncomm = comm.n
    nout = len(wrt_rows) + len(wrt_consts)
    s = rows[0][0].shape[0]
    nr, nc = len(rows), len(consts)
    flat_cots = [c for lst in cots for c in lst]
    ncot = len(flat_cots)

    def body(*refs):
        xs = [r[...].astype(F32) for r in refs[:nr]]
        cs = [c[...] for c in refs[nr:nr + nc]]
        cvals = [c[...].astype(F32) for c in refs[nr + nc:nr + nc + ncot]]
        orefs = refs[nr + nc + ncot + ncomm:]
        before, after = _comm_phases(comm, refs[nr + nc + ncot:nr + nc + ncot + ncomm] + orefs[nout:], s // br)
        before()

        def g(*d):
            xs2, cs2 = list(xs), list(cs)
            for j, k in enumerate(wrt_rows):
                xs2[k] = d[j]
            for j, k in enumerate(wrt_consts):
                cs2[k] = d[len(wrt_rows) + j]
            return tuple(fn(*xs2, *cs2))

        prim = [xs[k] for k in wrt_rows] + [cs[k] for k in wrt_consts]
        outs, vjp = jax.vjp(g, *prim)
        ct = []
        pos = 0
        for o, lst in zip(outs, cots):
            if unit_cot:
                ct.append(jnp.ones_like(o))
                continue
            acc = jnp.zeros_like(o)
            for _ in lst:
                acc = acc + cvals[pos]
                pos += 1
            ct.append(acc)
        grads = vjp(tuple(ct))
        for j in range(len(wrt_rows)):
            orefs[j][...] = grads[j].astype(orefs[j].dtype)

        @pl.when(pl.program_id(0) == 0)
        def _():
            for j in range(len(wrt_consts)):
                oref = orefs[len(wrt_rows) + j]
                oref[...] = jnp.zeros_like(oref)

        for j in range(len(wrt_consts)):
            orefs[len(wrt_rows) + j][...] += grads[len(wrt_rows) + j]
        after()

    in_specs = ([_row_spec(br, w, cb) for (_, w, cb) in rows] + [_const_spec(c.shape) for c in consts]
                + [_row_spec(br, w, cb) for (_, w, cb) in flat_cots] + [_HBM] * ncomm)
    out_specs = ([_row_spec(br, rows[k][1], 0) for k in wrt_rows] + [_const_spec(consts[k].shape) for k in wrt_consts]
                 + [_HBM] * ncomm)
    out_shape = ([jax.ShapeDtypeStruct((s, rows[k][1]), dt) for k, dt in zip(wrt_rows, drow_dtypes)]
                 + [jax.ShapeDtypeStruct(consts[k].shape, F32) for k in wrt_consts] + comm.out_shape)
    return _pcall(
        body, name=name, grid=(s // br,), in_specs=in_specs, out_specs=out_specs, out_shape=out_shape,
        scratch_shapes=comm.sems, compiler_params=_cparams(("arbitrary",)),
    )(*[r[0] for r in rows], *consts, *[c[0] for c in flat_cots], *comm.ins)


def _rms(x, w):
    return x * lax.rsqrt(jnp.mean(x * x, axis=-1, keepdims=True) + RMS_EPS) * w


def _f_pre(x, nw, sc, sh):
    return _rms(x, nw) * (1.0 + sc) + sh, x


def _f_pre2(x, o, gt, nw, sc, sh):
    x1 = x + gt * o
    return x1, _rms(x1, nw) * (1.0 + sc) + sh


def _f_fin(x1, f, tgt, gt, nfw):
    y = _rms(x1 + gt * f, nfw)
    return (0.5 * jnp.mean(jnp.square(y - tgt), axis=-1, keepdims=True),)


def _f_comb(o1, o2, o3, l1, l2, l3):
    m = lax.stop_gradient(jnp.maximum(jnp.maximum(l1, l2), l3))
    e1, e2, e3 = jnp.exp(l1 - m), jnp.exp(l2 - m), jnp.exp(l3 - m)
    return ((e1 * o1 + e2 * o2 + e3 * o3) / (e1 + e2 + e3),)


def _f_rwpre(zs, w0, a0, k_k, k_a, wl, e, et):
    r, k, v, zl = zs[:, 0:D], zs[:, D:2 * D], zs[:, 2 * D:3 * D], zs[:, 3 * D:N_RWP]
    lane = lax.broadcasted_iota(jnp.int32, zl.shape, 1)
    t = jnp.where(lane < 64, jnp.tanh(zl), jnp.where(lane < 128, zl, jnp.where(lane < 288, jax.nn.sigmoid(zl), 0.0)))
    lo = _nn(t[:, 0:128], wl[0:128, 0:2 * D])
    g = _nn(t[:, 128:N_LORA], wl[128:N_LORA, 2 * D:3 * D])
    lw = -math.exp(-0.5) * jax.nn.sigmoid(w0 + lo[:, 0:D])
    a = jax.nn.sigmoid(a0 + lo[:, D:2 * D])
    k_mod = k * (1.0 + (a - 1.0) * k_a)
    kk = k * k_k
    kk = kk / jnp.maximum(jnp.sqrt(_hsum(kk * kk, e, et)), 1e-12)
    return r, lw, k_mod, v, -kk, kk * a, g


def _f_rwpost(y, r, v, k_mod, g, lnx_w, lnx_b, r_k, e, et):
    mean = _hsum(y, e, et) * (1.0 / 64)
    yc = y - mean
    var = _hsum(yc * yc, e, et) * (1.0 / 64)
    yn = yc * lax.rsqrt(var + GN_EPS) * lnx_w + lnx_b
    bonus = _hsum(r * k_mod * r_k, e, et) * v
    return ((yn + bonus) * g,)


def _f_mix(gi, ya, yr, bg):
    gate = jax.nn.sigmoid(gi + bg)
    return (gate[:, 0:D] * ya + gate[:, D:2 * D] * yr,)


def _f_adamw(w, g, m, v):
    m = ADAM_B1 * m + (1.0 - ADAM_B1) * g
    v = ADAM_B2 * v + (1.0 - ADAM_B2) * jnp.square(g)
    m_hat = m / (1.0 - ADAM_B1 ** ADAM_STEP)
    v_hat = v / (1.0 - ADAM_B2 ** ADAM_STEP)
    return -ADAM_LR * (m_hat / (jnp.sqrt(v_hat) + ADAM_EPS) + ADAM_WD * w), m, v


def _down(x, k):
    row = lax.broadcasted_iota(jnp.int32, x.shape, 0)
    return jnp.where(row < k, 0.0, pltpu.roll(x, k, 0))


def _up(x, k):
    n = x.shape[0]
    row = lax.broadcasted_iota(jnp.int32, x.shape, 0)
    return jnp.where(row >= n - k, 0.0, pltpu.roll(x, n - k, 0))


def _col_spec(s, w, off=0):
    return pl.BlockSpec((s, w), lambda j: (0, j + off))


def _rwpre_shift_bwd(zs, z, mu, consts, cots, *, br):
    s, w = zs.shape
    n = s // br
    flat = [c for lst in cots for c in lst]
    nc, ncot, nwrt = len(consts), len(flat), 5

    def this(i):
        return jnp.minimum(i, n - 1)

    def last(i):
        return jnp.maximum(i - 1, 0)

    def body(*refs):
        zs_ref, z_ref, zh_ref, mu_ref = refs[:4]
        c_refs, cot_refs = refs[4:4 + nc], refs[4 + nc:4 + nc + ncot]
        dz_ref, dmu_ref = refs[4 + nc + ncot:6 + nc + ncot]
        dc_refs = refs[6 + nc + ncot:6 + nc + ncot + nwrt]
        kept = refs[-1]
        i = pl.program_id(0)

        @pl.when(i == 0)
        def _():
            dmu_ref[...] = jnp.zeros_like(dmu_ref)
            for ref in dc_refs:
                ref[...] = jnp.zeros_like(ref)

        cs = [c[...] for c in c_refs]

        def g(zz, *d):
            return tuple(_f_rwpre(zz, *d, *cs[nwrt:]))

        outs, vjp = jax.vjp(g, zs_ref[...], *cs[:nwrt])
        cts, pos = [], 0
        for o, lst in zip(outs, cots):
            acc = jnp.zeros_like(o)
            for _ in lst:
                acc = acc + cot_refs[pos][...].astype(F32)
                pos += 1
            cts.append(acc)
        grads = vjp(tuple(cts))
        dzs_new = grads[0]

        @pl.when(i < n)
        def _():
            for ref, gr in zip(dc_refs, grads[1:]):
                ref[...] += gr

        @pl.when(i > 0)
        def _():
            d, m = kept[...], mu_ref[...]
            row = lax.broadcasted_iota(jnp.int32, d.shape, 0)
            head = jnp.sum(jnp.where(row == 0, dzs_new, 0.0), axis=0, keepdims=True)
            head = jnp.where(i < n, head, 0.0)
            dm = d * m
            after = jnp.where(row == br - 1, head * m, pltpu.roll(dm, br - 1, 0))
            dz_ref[...] = (d - dm + after).astype(dz_ref.dtype)
            zz, halo = z_ref[...].astype(F32), zh_ref[...].astype(F32)
            tail = jnp.sum(jnp.where(lax.broadcasted_iota(jnp.int32, halo.shape, 0) == HALO - 1, halo, 0.0), axis=0,
                           keepdims=True)
            before = jnp.where(row == 0, jnp.where(i > 1, tail, 0.0), pltpu.roll(zz, 1, 0))
            dmu_ref[...] += jnp.sum(d * (before - zz), axis=0, keepdims=True)

        kept[...] = dzs_new

    in_specs = ([pl.BlockSpec((br, w), lambda i: (this(i), 0)), pl.BlockSpec((br, w), lambda i: (last(i), 0)),
                 pl.BlockSpec((HALO, w), lambda i: (jnp.maximum(last(i) * (br // HALO) - 1, 0), 0)),
                 _const_spec(mu.shape)] + [_const_spec(c.shape) for c in consts]
                + [pl.BlockSpec((br, cw), lambda i, cb=cb: (this(i), cb)) for (_, cw, cb) in flat])
    out_specs = ([pl.BlockSpec((br, w), lambda i: (last(i), 0)), _const_spec(mu.shape)]
                 + [_const_spec(consts[k].shape) for k in range(nwrt)])
    out_shape = ([jax.ShapeDtypeStruct((s, w), BF16), jax.ShapeDtypeStruct(mu.shape, F32)]
                 + [jax.ShapeDtypeStruct(consts[k].shape, F32) for k in range(nwrt)])
    return _pcall(
        body, name="rwpre_shift_bwd", grid=(n + 1,), in_specs=in_specs, out_specs=out_specs, out_shape=out_shape,
        scratch_shapes=[pltpu.VMEM((br, w), F32)], compiler_params=_cparams(("arbitrary",)),
    )(zs, z, z, mu, *consts, *[c[0] for c in flat])


def _conv3(x, w_ref, b_ref):
    return b_ref[...] + w_ref[0:1, :] * _down(x, 2) + w_ref[1:2, :] * _down(x, 1) + w_ref[2:3, :] * x


def _conv_fwd(u, cw, cb):
    s = u.shape[0]
    nb = D_FF // 128

    def body(ug_ref, uv_ref, wg_ref, wv_ref, bg_ref, bv_ref, o_ref):
        gate = _conv3(ug_ref[...], wg_ref, bg_ref)
        val = _conv3(uv_ref[...], wv_ref, bv_ref)
        o_ref[...] = (gate * jax.nn.sigmoid(gate) * val).astype(o_ref.dtype)

    return _pcall(
        body, name="conv_fwd", grid=(nb,),
        in_specs=[_col_spec(s, 128), _col_spec(s, 128, nb), _col_spec(3, 128), _col_spec(3, 128, nb),
                  _col_spec(1, 128), _col_spec(1, 128, nb)],
        out_specs=_col_spec(s, 128), out_shape=jax.ShapeDtypeStruct((s, D_FF), BF16),
        compiler_params=_cparams(("parallel",)),
    )(u, u, cw, cw, cb, cb)


def _conv_bwd(u, cw, cb, dact):
    s = u.shape[0]
    nb = D_FF // 128

    def half(x, d, w_ref, du_ref, dw_ref, db_ref):
        x1, x2 = _down(x, 1), _down(x, 2)
        du_ref[...] = (w_ref[2:3, :] * d + w_ref[1:2, :] * _up(d, 1) + w_ref[0:1, :] * _up(d, 2)).astype(du_ref.dtype)
        dw_ref[0:1, :] = jnp.sum(d * x2, axis=0, keepdims=True)
        dw_ref[1:2, :] = jnp.sum(d * x1, axis=0, keepdims=True)
        dw_ref[2:3, :] = jnp.sum(d * x, axis=0, keepdims=True)
        db_ref[...] = jnp.sum(d, axis=0, keepdims=True)

    def body(ug_ref, uv_ref, wg_ref, wv_ref, bg_ref, bv_ref, da_ref,
             dug_ref, duv_ref, dwg_ref, dwv_ref, dbg_ref, dbv_ref):
        ug, uv, da = ug_ref[...], uv_ref[...], da_ref[...]
        gate = _conv3(ug, wg_ref, bg_ref)
        val = _conv3(uv, wv_ref, bv_ref)
        sg = jax.nn.sigmoid(gate)
        dgate = da * val * sg * (1.0 + gate * (1.0 - sg))
        dval = da * gate * sg
        half(ug, dgate, wg_ref, dug_ref, dwg_ref, dbg_ref)
        half(uv, dval, wv_ref, duv_ref, dwv_ref, dbv_ref)

    dug, duv, dwg, dwv, dbg, dbv = _pcall(
        body, name="conv_bwd", grid=(nb,),
        in_specs=[_col_spec(s, 128), _col_spec(s, 128, nb), _col_spec(3, 128), _col_spec(3, 128, nb),
                  _col_spec(1, 128), _col_spec(1, 128, nb), _col_spec(s, 128)],
        out_specs=[_col_spec(s, 128), _col_spec(s, 128), _col_spec(3, 128), _col_spec(3, 128),
                   _col_spec(1, 128), _col_spec(1, 128)],
        out_shape=[jax.ShapeDtypeStruct((s, D_FF), BF16), jax.ShapeDtypeStruct((s, D_FF), BF16),
                   jax.ShapeDtypeStruct((3, D_FF), F32), jax.ShapeDtypeStruct((3, D_FF), F32),
                   jax.ShapeDtypeStruct((1, D_FF), F32), jax.ShapeDtypeStruct((1, D_FF), F32)],
        compiler_params=_cparams(("parallel",)),
    )(u, u, cw, cw, cb, cb, dact)
    return (jnp.concatenate([dug, duv], axis=1), jnp.concatenate([dwg, dwv], axis=1),
            jnp.concatenate([dbg, dbv], axis=1))


ATT_BATCH = 4


def _att_batch(q, kp, kc, vp, vc, first):
    ma = lax.broadcasted_iota(jnp.int32, (1, ATT_BLOCK, 128), 2) < 64

    def diag(x):
        return jnp.concatenate([jnp.where(ma, x, 0.0), jnp.where(ma, 0.0, x)], axis=1)

    qi = lax.broadcasted_iota(jnp.int32, (1, ATT_BLOCK, 2 * ATT_BLOCK), 1)
    kj = lax.broadcasted_iota(jnp.int32, (1, ATT_BLOCK, 2 * ATT_BLOCK), 2) & (ATT_BLOCK - 1)
    okp = kj >= qi + jnp.where(first, 2 * ATT_BLOCK, 0)
    okc = kj <= qi
    sp = jnp.where(okp, _bnt(q, diag(kp)) * 0.125, NEG)
    sc = jnp.where(okc, _bnt(q, diag(kc)) * 0.125, NEG)

    def per_head(fn, x):
        return fn(x[..., :ATT_BLOCK]), fn(x[..., ATT_BLOCK:])

    def spread(ab):
        return jnp.concatenate([jnp.broadcast_to(t, t.shape[:2] + (ATT_BLOCK,)) for t in ab], axis=-1)

    row_max = functools.partial(jnp.max, axis=-1, keepdims=True)
    row_sum = functools.partial(jnp.sum, axis=-1, keepdims=True)
    m = [lax.stop_gradient(jnp.maximum(a, b)) for a, b in zip(per_head(row_max, sp), per_head(row_max, sc))]
    pp, pc = jnp.exp(sp - spread(m)), jnp.exp(sc - spread(m))
    den = [a + b for a, b in zip(per_head(row_sum, pp), per_head(row_sum, pc))]
    num = _bnn(pp, diag(vp)) + _bnn(pc, diag(vc))
    out = num / jnp.where(ma, den[0], den[1])
    lse = jnp.where(ma, m[0] + jnp.log(den[0]), m[1] + jnp.log(den[1]))
    return out, jnp.broadcast_to(lse, out.shape)


def _att_pairs_per_step(dil):
    return 4 if dil == 1 else 1


def _att_residues(dil):
    return min(dil, ATT_BATCH // _att_pairs_per_step(dil))


def _att_specs(g, dil):
    rows, pp = ATT_BLOCK * dil, _att_pairs_per_step(dil)

    def cur(slot):
        return pl.BlockSpec((rows, 128 * pp), lambda n, p: (n, (g * 3 + slot) * (4 // pp) + p))

    def prev(slot):
        return pl.BlockSpec((rows, 128 * pp), lambda n, p: (jnp.maximum(n - 1, 0), (g * 3 + slot) * (4 // pp) + p))

    return [cur(0), prev(1), cur(1), prev(2), cur(2)]


def _att_out_spec(dil):
    return pl.BlockSpec((ATT_BLOCK * dil, 128 * _att_pairs_per_step(dil)), lambda n, p: (n, p))


def _att_grid(s, dil):
    return (s // (ATT_BLOCK * dil), 4 // _att_pairs_per_step(dil))


def _att_windows(i, dil):
    res = _att_residues(dil)

    def rows(r):
        return pl.ds(i * res + r, ATT_BLOCK, stride=dil) if dil > 1 else pl.ds(0, ATT_BLOCK)

    return [(rows(r), pl.ds(128 * j, 128)) for j in range(_att_pairs_per_step(dil)) for r in range(res)]


def _att_fwd(att_in, g, dil):
    s = att_in.shape[0]

    def body(q_ref, kp_ref, kc_ref, vp_ref, vc_ref, o_ref, l_ref):
        first = pl.program_id(0) == 0

        def one(i, carry):
            win = _att_windows(i, dil)
            vals = [jnp.stack([ref[w] for w in win]) for ref in (q_ref, kp_ref, kc_ref, vp_ref, vc_ref)]
            o, l = _att_batch(*vals, first)
            for j, w in enumerate(win):
                o_ref[w] = o[j]
                l_ref[w] = l[j]
            return carry

        lax.fori_loop(0, dil // _att_residues(dil), one, 0)

    return _pcall(
        body, name=f"att_fwd{g}", grid=_att_grid(s, dil), in_specs=_att_specs(g, dil),
        out_specs=[_att_out_spec(dil)] * 2, out_shape=[jax.ShapeDtypeStruct((s, ATT_WIDTH), F32)] * 2,
        compiler_params=_cparams(("parallel", "parallel")),
    )(att_in, att_in, att_in, att_in, att_in)


def _att_bwd(att_in, g, dil, do, dl, acc):
    s = att_in.shape[0]

    def body(q_ref, kp_ref, kc_ref, vp_ref, vc_ref, do_ref, dl_ref, dq_ref, dkp_ref, dkc_ref, dvp_ref, dvc_ref):
        first = pl.program_id(0) == 0

        def one(i, carry):
            win = _att_windows(i, dil)
            vals = [jnp.stack([ref[w] for w in win]) for ref in (q_ref, kp_ref, kc_ref, vp_ref, vc_ref)]
            _, vjp = jax.vjp(lambda *a: _att_batch(*a, first), *vals)
            grads = vjp((jnp.stack([do_ref[w] for w in win]), jnp.stack([dl_ref[w] for w in win])))
            for ref, gr in zip((dq_ref, dkp_ref, dkc_ref, dvp_ref, dvc_ref), grads):
                for j, w in enumerate(win):
                    ref[w] = gr[j]
            return carry

        lax.fori_loop(0, dil // _att_residues(dil), one, 0)

    dq, dkp, dkc, dvp, dvc = _pcall(
        body, name=f"att_bwd{g}", grid=_att_grid(s, dil), in_specs=_att_specs(g, dil) + [_att_out_spec(dil)] * 2,
        out_specs=[_att_out_spec(dil)] * 5, out_shape=[jax.ShapeDtypeStruct((s, ATT_WIDTH), F32)] * 5,
        compiler_params=_cparams(("parallel", "parallel")),
    )(att_in, att_in, att_in, att_in, att_in, do, dl)

    unit, rb = ATT_BLOCK * dil, 1024
    steps = s // rb
    within = unit < rb

    def shifted(cur_ref, next_ref, has_next):
        nxt = jnp.where(has_next, next_ref[...], 0.0)
        return jnp.concatenate([cur_ref[unit:, :], nxt], axis=0) if within else nxt

    def cbody(dq_ref, dkc_ref, dkp_ref, dkn_ref, dvc_ref, dvp_ref, dvn_ref, *rest):
        o_ref = rest[-1]
        has_next = pl.program_id(0) + (1 if within else unit // rb) < steps
        o_ref[:, 0:ATT_WIDTH] = dq_ref[...].astype(BF16)
        o_ref[:, ATT_WIDTH:2 * ATT_WIDTH] = (dkc_ref[...] + shifted(dkp_ref, dkn_ref, has_next)).astype(BF16)
        o_ref[:, 2 * ATT_WIDTH:3 * ATT_WIDTH] = (dvc_ref[...] + shifted(dvp_ref, dvn_ref, has_next)).astype(BF16)

    cur = pl.BlockSpec((rb, ATT_WIDTH), lambda i: (i, 0))
    if within:
        nxt = pl.BlockSpec((unit, ATT_WIDTH), lambda i: (jnp.minimum((i + 1) * (rb // unit), s // unit - 1), 0))
    else:
        nxt = pl.BlockSpec((rb, ATT_WIDTH), lambda i: (jnp.minimum(i + unit // rb, steps - 1), 0))
    carried = [] if acc is None else [acc]
    return _pcall(
        cbody, name=f"att_bwd_sum{g}", grid=(steps,),
        in_specs=[cur, cur, cur, nxt, cur, cur, nxt] + [pl.BlockSpec(memory_space=pl.ANY)] * len(carried),
        out_specs=pl.BlockSpec((rb, 3 * ATT_WIDTH), lambda i: (i, g)),
        out_shape=jax.ShapeDtypeStruct((s, N_ATT), BF16), input_output_aliases={7: 0} if carried else {},
        compiler_params=_cparams(("parallel",)),
    )(dq, dkc, dkp, dkp, dvc, dvp, dvp, *carried)


def _cumsum_rows_impl(x):
    row = lax.broadcasted_iota(jnp.int32, x.shape, 0)
    shift = 1
    while shift < x.shape[0]:
        x = x + jnp.where(row >= shift, pltpu.roll(x, shift, 0), 0.0)
        shift *= 2
    return x


@jax.custom_vjp
def _cumsum_rows(x):
    return _cumsum_rows_impl(x)


_cumsum_rows.defvjp(lambda x: (_cumsum_rows_impl(x), None),
                    lambda _, g: (jnp.sum(g, axis=0, keepdims=True) - _cumsum_rows_impl(g) + g,))


def _unit_lower_inverse_impl(n):
    eye = (lax.broadcasted_iota(jnp.int32, (1,) + n.shape[1:], 1)
           == lax.broadcasted_iota(jnp.int32, (1,) + n.shape[1:], 2))
    t = jnp.where(eye, 1.0, 0.0) + n
    pw = n
    for _ in range(5):
        pw = _bnn(pw, pw)
        t = t + _bnn(t, pw)
    return t


@jax.custom_vjp
def _unit_lower_inverse(n):
    return _unit_lower_inverse_impl(n)


def _unit_lower_inverse_fwd(n):
    t = _unit_lower_inverse_impl(n)
    return t, t


_unit_lower_inverse.defvjp(_unit_lower_inverse_fwd, lambda t, g: (_bnt(_btn(t, g), t),))


@jax.custom_vjp
def _known_inverse(n, t):
    return t


_known_inverse.defvjp(lambda n, t: (t, t), lambda t, g: (_bnt(_btn(t, g), t), jnp.zeros_like(t)))


def _scan_chunk(r, lw, k, v, a, b, s0, inverse):
    c = SCAN_CHUNK
    p = s0.shape[0]
    cum = _cumsum_rows(lw)
    tot = jnp.sum(lw, axis=0, keepdims=True)
    ma = (lax.broadcasted_iota(jnp.int32, (c, 128 * p), 1) & 127) < 64

    def pairs(x):
        return jnp.concatenate([x[None, :, 128 * j:128 * (j + 1)] for j in range(p)], axis=0)

    def stack(x):
        return jnp.concatenate([pairs(jnp.where(ma, x, 0.0)), pairs(jnp.where(ma, 0.0, x))], axis=1)

    einv, eend = jnp.exp(-cum), jnp.exp(tot - cum)
    ra, aa = stack(r * jnp.exp(cum)), stack(a * jnp.exp(cum - lw))
    bi, ki, be, ke, vs = stack(b * einv), stack(k * einv), stack(b * eend), stack(k * eend), stack(v)
    r2 = lax.broadcasted_iota(jnp.int32, (1, 2 * c, 2 * c), 1)
    c2 = lax.broadcasted_iota(jnp.int32, (1, 2 * c, 2 * c), 2)
    same = (r2 >= c) == (c2 >= c)
    strict = jnp.logical_and(same, c2 < r2)
    incl = jnp.logical_and(same, c2 <= r2)
    s0 = jnp.where(same, s0, 0.0)
    prod = _bnt(jnp.concatenate([aa, ra], axis=1), jnp.concatenate([bi, ki], axis=1))
    a_ab = jnp.where(strict, prod[:, :2 * c, :2 * c], 0.0)
    a_ak = jnp.where(strict, prod[:, :2 * c, 2 * c:], 0.0)
    a_rb = jnp.where(incl, prod[:, 2 * c:, :2 * c], 0.0)
    a_rk = jnp.where(incl, prod[:, 2 * c:, 2 * c:], 0.0)
    t = inverse(a_ab)
    u = _bnn(t, _bnt(aa, s0) + _bnn(a_ak, vs))
    uv = jnp.concatenate([u, vs], axis=1)
    ys = _bnt(ra, s0) + _bnn(jnp.concatenate([a_rb, a_rk], axis=2), uv)
    s1 = s0 * pairs(jnp.exp(tot)) + _btn(uv, jnp.concatenate([be, ke], axis=1))
    y3 = ys[:, :c] + ys[:, c:]
    return (jnp.concatenate([y3[j] for j in range(p)], axis=1), s1), t


def _scan_specs(rev, n):
    def at(i):
        return n - 1 - i if rev else i

    def cm(cb):
        return pl.BlockSpec((SCAN_CHUNK, D), lambda i: (at(i), cb))

    return cm, pl.BlockSpec((1, SCAN_PAIRS, 128, 128), lambda i: (at(i), 0, 0, 0))


def _comm_phases(comm, refs, n, step=None):
    k = comm.n
    srcs, outs, sems = refs[:k], refs[k:2 * k], refs[2 * k:]
    i = pl.program_id(0) if step is None else step

    def before():
        @pl.when(i == 0)
        def _():
            comm.first(srcs, outs, sems)

    def after():
        if comm.mid is not None:
            @pl.when(i == (3 * n) // 4)
            def _():
                comm.mid(srcs, outs, sems)

        @pl.when(i == n - 1)
        def _():
            comm.last(srcs, outs, sems)

    return before, after


def _scan_fwd(zs, lw, km, aa, bb, comm):
    s = zs.shape[0]
    n = s // SCAN_CHUNK
    cm, st = _scan_specs(False, n)
    k = comm.n

    def body(*refs):
        r_ref, lw_ref, k_ref, v_ref, a_ref, b_ref = refs[:6]
        y_ref, s0_ref, t_ref = refs[6 + k:9 + k]
        state = refs[9 + 2 * k]
        before, after = _comm_phases(comm, refs[6:6 + k] + refs[9 + k:9 + 2 * k] + refs[10 + 2 * k:], n)
        before()

        @pl.when(pl.program_id(0) == 0)
        def _():
            state[...] = jnp.zeros_like(state)

        s0 = state[...]
        s0_ref[0] = s0
        (y, s1), t = _scan_chunk(*[ref[...] for ref in (r_ref, lw_ref, k_ref, v_ref, a_ref, b_ref)], s0,
                                 _unit_lower_inverse)
        y_ref[...] = y
        t_ref[0] = t.astype(BF16)
        state[...] = s1
        after()

    per_chunk = (n, SCAN_PAIRS, 128, 128)
    res = _pcall(
        body, name="scan_fwd", grid=(n,), in_specs=[cm(0), cm(0), cm(0), cm(2), cm(0), cm(0)] + [_HBM] * k,
        out_specs=[cm(0), st, st] + [_HBM] * k,
        out_shape=[jax.ShapeDtypeStruct((s, D), F32), jax.ShapeDtypeStruct(per_chunk, F32),
                   jax.ShapeDtypeStruct(per_chunk, BF16)] + comm.out_shape,
        scratch_shapes=[pltpu.VMEM((SCAN_PAIRS, 128, 128), F32)] + comm.sems,
        compiler_params=_cparams(("arbitrary",)),
    )(zs, lw, km, zs, aa, bb, *comm.ins)
    return res[0], res[1], res[2], res[3:]


def _scan_bwd(zs, lw, km, aa, bb, s0s, ts, dy, comm):
    s = zs.shape[0]
    n = s // SCAN_CHUNK
    cm, st = _scan_specs(True, n)
    k = comm.n

    def body(*refs):
        r_ref, lw_ref, k_ref, v_ref, a_ref, b_ref, s0_ref, t_ref, dy_ref = refs[:9]
        douts = refs[9 + k:15 + k]
        dstate = refs[15 + 2 * k]
        before, after = _comm_phases(comm, refs[9:9 + k] + refs[15 + k:15 + 2 * k] + refs[16 + 2 * k:], n)
        before()

        @pl.when(pl.program_id(0) == 0)
        def _():
            dstate[...] = jnp.zeros_like(dstate)

        t = t_ref[0].astype(F32)
        prim = [ref[...] for ref in (r_ref, lw_ref, k_ref, v_ref, a_ref, b_ref)] + [s0_ref[0]]
        _, vjp, _ = jax.vjp(lambda *p: _scan_chunk(*p, lambda nil: _known_inverse(nil, t)), *prim, has_aux=True)
        grads = vjp((dy_ref[...], dstate[...]))
        for ref, gr in zip(douts, grads[:6]):
            ref[...] = gr
        dstate[...] = grads[6]
        after()

    res = _pcall(
        body, name="scan_bwd", grid=(n,),
        in_specs=[cm(0), cm(0), cm(0), cm(2), cm(0), cm(0), st, st, cm(0)] + [_HBM] * k,
        out_specs=[cm(0)] * 6 + [_HBM] * k, out_shape=[jax.ShapeDtypeStruct((s, D), F32)] * 6 + comm.out_shape,
        scratch_shapes=[pltpu.VMEM((SCAN_PAIRS, 128, 128), F32)] + comm.sems,
        compiler_params=_cparams(("arbitrary",)),
    )(zs, lw, km, zs, aa, bb, s0s, ts, dy, *comm.ins)
    return res[:6], res[6:]


_HBM = pl.BlockSpec(memory_space=pltpu.HBM)


def _me():
    return lax.axis_index("x"), lax.axis_index("y"), lax.axis_index("c")


def _allgather8(src, name):
    def body(src_ref, out_ref, ssem, rsem, lsem):
        x, y, c = _me()
        me = 4 * x + 2 * y + c
        local = pltpu.make_async_copy(src_ref, out_ref.at[me], lsem)
        local.start()
        peers = []
        for k in range(1, 8):
            peers.append(((1 - x) if k & 4 else x, (1 - y) if k & 2 else y, (1 - c) if k & 1 else c))
        sends = []
        for k, peer in enumerate(peers):
            cp = pltpu.make_async_remote_copy(src_ref, out_ref.at[me], ssem.at[k], rsem.at[k], device_id=peer,
                                              device_id_type=MESH)
            cp.start()
            sends.append(cp)
        for k, (px, py, pc) in enumerate(peers):
            pltpu.make_async_remote_copy(src_ref, out_ref.at[4 * px + 2 * py + pc], ssem.at[k], rsem.at[k],
                                         device_id=(px, py, pc), device_id_type=MESH).wait_recv()
        for cp in sends:
            cp.wait_send()
        local.wait()

    return _pcall(
        body, name=name, in_specs=[_HBM], out_specs=_HBM, out_shape=jax.ShapeDtypeStruct((8,) + src.shape, src.dtype),
        scratch_shapes=[pltpu.SemaphoreType.DMA((7,)), pltpu.SemaphoreType.DMA((7,)), pltpu.SemaphoreType.DMA],
    )(src)


def _other_chips(x, y):
    return [(1 - x, y), (x, 1 - y), (1 - x, 1 - y)]


def _remote(src, dst, ssem, rsem, to):
    return pltpu.make_async_remote_copy(src, dst, ssem, rsem, device_id=to, device_id_type=MESH)


class _GatherWeights:
    def __init__(self, shards):
        self.ins = list(shards)
        n = self.n = len(shards)
        self.out_shape = [jax.ShapeDtypeStruct((4,) + t.shape, t.dtype) for t in shards]
        self.sems = [pltpu.SemaphoreType.DMA((6 * n,)), pltpu.SemaphoreType.DMA((6 * n,)),
                     pltpu.SemaphoreType.DMA((n,)), pltpu.SemaphoreType.DMA((n,))]

    def _copies(self, srcs, outs, sems):
        ssem, rsem, lsem, osem = sems
        x, y, c = _me()
        me = 2 * x + y
        own, ici, landed, passed, passed_in = [], [], [], [], []
        for a in range(self.n):
            h = self.ins[a].shape[0] // 2
            mine, other = pl.ds(c * h, h), pl.ds((1 - c) * h, h)
            own.append(_remote(srcs[a], outs[a].at[me], lsem.at[a], osem.at[a], (x, y, 1 - c)))
            for k, (px, py) in enumerate(_other_chips(x, y)):
                s1, r1, s2, r2 = ssem.at[6 * a + k], rsem.at[6 * a + k], ssem.at[6 * a + 3 + k], rsem.at[6 * a + 3 + k]
                got, got_sib = outs[a].at[2 * px + py, mine], outs[a].at[2 * px + py, other]
                ici.append(_remote(srcs[a].at[mine], outs[a].at[me, mine], s1, r1, (px, py, c)))
                landed.append(_remote(got, got, s1, r1, (px, py, c)))
                passed.append(_remote(got, got, s2, r2, (x, y, 1 - c)))
                passed_in.append(_remote(got_sib, got_sib, s2, r2, (x, y, 1 - c)))
        return own, ici, landed, passed, passed_in

    def first(self, srcs, outs, sems):
        own, ici, _, _, _ = self._copies(srcs, outs, sems)
        for cp in own + ici:
            cp.start()

    def mid(self, srcs, outs, sems):
        _, _, landed, passed, _ = self._copies(srcs, outs, sems)
        for arrived, onward in zip(landed, passed):
            arrived.wait_recv()
            onward.start()

    def last(self, srcs, outs, sems):
        own, ici, _, passed, passed_in = self._copies(srcs, outs, sems)
        for cp in passed_in:
            cp.wait_recv()
        for cp in ici + passed:
            cp.wait_send()
        for cp in own:
            cp.wait()


class _ScatterToChips:
    def __init__(self, parts):
        self.ins = list(parts)
        n = self.n = len(parts)
        self.out_shape = [jax.ShapeDtypeStruct(t.shape, t.dtype) for t in parts]
        self.sems = [pltpu.SemaphoreType.DMA((3 * n,)), pltpu.SemaphoreType.DMA((3 * n,)), pltpu.SemaphoreType.DMA((n,))]

    def _copies(self, srcs, outs, sems):
        ssem, rsem, lsem = sems
        x, y, c = _me()
        me = 2 * x + y
        own, out, landed = [], [], []
        for a in range(self.n):
            own.append(pltpu.make_async_copy(srcs[a].at[me], outs[a].at[me], lsem.at[a]))
            for k, (px, py) in enumerate(_other_chips(x, y)):
                dst = outs[a].at[2 * px + py]
                out.append(_remote(srcs[a].at[2 * px + py], outs[a].at[me], ssem.at[3 * a + k], rsem.at[3 * a + k],
                                   (px, py, c)))
                landed.append(_remote(dst, dst, ssem.at[3 * a + k], rsem.at[3 * a + k], (px, py, c)))
        return own, out, landed

    def first(self, srcs, outs, sems):
        own, out, _ = self._copies(srcs, outs, sems)
        for cp in own + out:
            cp.start()

    mid = None

    def last(self, srcs, outs, sems):
        own, out, landed = self._copies(srcs, outs, sems)
        for cp in landed:
            cp.wait_recv()
        for cp in own:
            cp.wait()
        for cp in out:
            cp.wait_send()


def _run_comm(comm, name):
    n = comm.n

    def body(*refs):
        srcs, outs, sems = refs[:n], refs[n:2 * n], refs[2 * n:]
        comm.first(srcs, outs, sems)
        if comm.mid is not None:
            comm.mid(srcs, outs, sems)
        comm.last(srcs, outs, sems)

    return _pcall(body, name=name, in_specs=[_HBM] * n, out_specs=[_HBM] * n, out_shape=comm.out_shape,
                  scratch_shapes=comm.sems)(*comm.ins)


class _NoComm:
    n, ins, out_shape, sems, mid = 0, [], [], [], None

    def first(self, srcs, outs, sems):
        pass

    def last(self, srcs, outs, sems):
        pass


_NOTHING = _NoComm()


class _SiblingHalves:
    mid = None

    def __init__(self, grads):
        self.ins = list(grads)
        n = self.n = len(grads)
        self.out_shape = [jax.ShapeDtypeStruct((4, t.shape[1] // 2, t.shape[2]), t.dtype) for t in grads]
        self.sems = [pltpu.SemaphoreType.DMA((n,)), pltpu.SemaphoreType.DMA((n,))]

    def _copies(self, srcs, outs, sems):
        ssem, rsem = sems
        x, y, c = _me()
        copies = []
        for a in range(self.n):
            h = self.ins[a].shape[1] // 2
            copies.append(_remote(srcs[a].at[:, pl.ds((1 - c) * h, h)], outs[a], ssem.at[a], rsem.at[a], (x, y, 1 - c)))
        return copies

    def first(self, srcs, outs, sems):
        for cp in self._copies(srcs, outs, sems):
            cp.start()

    def last(self, srcs, outs, sems):
        for cp in self._copies(srcs, outs, sems):
            cp.wait()


def _reduce_finish(reds, name):
    n = len(reds)

    def body(*refs):
        outs = refs[n:2 * n]
        ssem, rsem = refs[2 * n:]
        x, y, c = _me()
        copies = []
        for a in range(n):
            h = reds[a].shape[0] // 2
            mine = outs[a].at[pl.ds(c * h, h)]
            copies.append(_remote(mine, mine, ssem.at[a], rsem.at[a], (x, y, 1 - c)))
        for cp in copies:
            cp.start()
        for a in range(n):
            h = reds[a].shape[0] // 2
            dst = outs[a].at[pl.ds((1 - c) * h, h)]
            _remote(dst, dst, ssem.at[a], rsem.at[a], (x, y, 1 - c)).wait_recv()
        for cp in copies:
            cp.wait_send()

    return _pcall(
        body, name=name, in_specs=[_HBM] * n, out_specs=[_HBM] * n,
        out_shape=[jax.ShapeDtypeStruct(t.shape, t.dtype) for t in reds],
        input_output_aliases={a: a for a in range(n)},
        scratch_shapes=[pltpu.SemaphoreType.DMA((n,)), pltpu.SemaphoreType.DMA((n,))],
    )(*reds)


def _half_sum(fn, full, halves, out_full, out_dtype, core, name):
    p, h, c = (halves[0].shape if halves else (full[0].shape[0], full[0].shape[1] // 2, full[0].shape[2]))
    br = _div(h, max(16, (1 << 19) // (p * c)), 16)
    nb = h // br
    mine3 = pl.BlockSpec((p, br, c), lambda i, core_ref: (0, core_ref[0] * nb + i, 0))
    half3 = pl.BlockSpec((p, br, c), lambda i, core_ref: (0, i, 0))

    def body(core_ref, *refs):
        refs[-1][...] = fn(*[t[...].astype(F32) for t in refs[:-1]]).astype(out_dtype)

    if out_full:
        out_spec = pl.BlockSpec((br, c), lambda i, core_ref: (core_ref[0] * nb + i, 0))
        out_shape = jax.ShapeDtypeStruct((2 * h, c), out_dtype)
    else:
        out_spec, out_shape = half3, jax.ShapeDtypeStruct((p, h, c), out_dtype)
    return _pcall(
        body, name=name,
        grid_spec=pltpu.PrefetchScalarGridSpec(
            num_scalar_prefetch=1, grid=(nb,), in_specs=[mine3] * len(full) + [half3] * len(halves),
            out_specs=out_spec),
        out_shape=out_shape, compiler_params=_cparams(("parallel",)),
    )(core, *full, *halves)


def _ada_fwd(c_all, w, b):
    def body(c_ref, w_ref, b_ref, o_ref):
        o_ref[...] = jnp.dot(c_ref[...], w_ref[...], precision=HI, preferred_element_type=F32) + b_ref[...]

    return _pcall(body, name="ada_fwd", out_shape=jax.ShapeDtypeStruct((c_all.shape[0], w.shape[1]), F32),
                  compiler_params=pltpu.CompilerParams(vmem_limit_bytes=VMEM_LIMIT))(c_all, w, b)


def _ada_bwd(c_all_t, d):
    def body(c_ref, d_ref, o_ref):
        o_ref[...] = jnp.dot(c_ref[...], d_ref[...], precision=HI, preferred_element_type=F32)

    return _pcall(body, name="ada_bwd", out_shape=jax.ShapeDtypeStruct((c_all_t.shape[0], d.shape[1]), F32),
                  compiler_params=pltpu.CompilerParams(vmem_limit_bytes=VMEM_LIMIT))(c_all_t, d)


def _sum_lead(x, name):
    p, r, n = x.shape
    br = _div(r, 512, 8)

    def body(x_ref, o_ref):
        acc = x_ref[0]
        for j in range(1, p):
            acc = acc + x_ref[j]
        o_ref[...] = acc

    return _pcall(
        body, name=name, grid=(r // br,), in_specs=[pl.BlockSpec((p, br, n), lambda i: (0, i, 0))],
        out_specs=pl.BlockSpec((br, n), lambda i: (i, 0)), out_shape=jax.ShapeDtypeStruct((r, n), F32),
        compiler_params=_cparams(("parallel",)),
    )(x)


def _adamw(w, g, m, v, name):
    shape = w.shape
    cols = shape[-1]
    w2, g2, m2, v2 = [t.reshape(-1, cols) for t in (w, g, m, v)]
    rows = w2.shape[0]
    pref = max(8, (1 << 19) // cols // 8 * 8)
    br = _div(rows, pref, 8)
    if rows // br > 64:
        br = pref
    outs = _rows_fwd(_f_adamw, [(t, cols, 0) for t in (w2, g2, m2, v2)], [], [(cols, F32)] * 3, name=name, br=br)
    return [o.reshape(shape) for o in outs]


_BIG = (("w_in", 1), ("w_up", 1), ("w_down", 0), ("w_o", 0), ("w_rwkv_out", 0), ("w_att_out", 1), ("w2", 1), ("a2", 1),
        ("g2", 1))


_NEEDED_FIRST = ("w_in", "w_att_out", "w2", "a2", "g2")
_NEEDED_LATER = ("w_up", "w_down", "w_o", "w_rwkv_out")
_DONE_EARLY = ("w_up", "w_down", "w_o", "w_rwkv_out", "w_att_out")
_DONE_LATE = ("w_in", "w2", "a2", "g2")


def _cols_joined(t):
    return jnp.concatenate([t[j] for j in range(4)], axis=1)


def _cols_split(t):
    n = t.shape[1] // 4
    return jnp.stack([t[:, j * n:(j + 1) * n] for j in range(4)])


W_IN_SHARD = (N_ATT + N_RW + N_GATE) // 4
W_IN_PAD = 2560


def _row_window(parts, lo, hi):
    out, pos = [], 0
    for t, w in parts:
        a, b = max(lo, pos), min(hi, pos + w)
        if a < b:
            out.append(t[a - pos:b - pos])
        pos += w
    return out[0] if len(out) == 1 else jnp.concatenate(out, axis=0)


def _rows_joined(t):
    return t.reshape(4 * t.shape[1], t.shape[2])


def _rows_split(t):
    return t.reshape(4, t.shape[0] // 4, t.shape[1])


def _step_to_scan(x, tgt, ada, wts):
    sh1, sc1, gt1, sh2, sc2, gt2 = ada
    br = 256
    grp = lax.broadcasted_iota(jnp.int32, (D, 128), 0) // 64 == lax.broadcasted_iota(jnp.int32, (D, 128), 1)
    e = grp.astype(F32)
    et = e.T
    w_in = [(wts["w_in"][j], W_IN_SHARD) for j in range(4)]
    w_att = _row_window(w_in, 0, N_ATT)
    w_rw = jnp.concatenate([_row_window(w_in, N_ATT, N_ATT + N_RW), jnp.zeros((N_RWP - N_RW, D), BF16)], axis=0)
    w_gate = _row_window(w_in, N_ATT + N_RW, N_ATT + N_RW + N_GATE)
    mu = jnp.pad(wts["mu_shift"], ((0, 0), (0, N_RWP - N_RW)))
    wl = jnp.zeros((N_LORA, 3 * D), F32)
    wl = wl.at[0:64, 0:D].set(_cols_joined(wts["w2"]).astype(F32))
    wl = wl.at[64:128, D:2 * D].set(_cols_joined(wts["a2"]).astype(F32))
    wl = wl.at[128:288, 2 * D:3 * D].set(_cols_joined(wts["g2"]).astype(F32))
    pre1_c = [wts["norm1_w"], sc1, sh1]
    (h1,) = _rows_fwd(_f_pre, [(x, D, 0)], pre1_c, [(D, BF16), None], name="pre1_fwd", br=2 * br)
    att_in = _mm(h1, w_att, tb=True, name="mm_att_in")
    z = _mm(h1, w_rw, tb=True, out_dtype=BF16, name="mm_rw_in")
    gate_in = _mm(h1, w_gate, tb=True, out_dtype=BF16, name="mm_gate_in")
    att_o, att_l = [], []
    for g, (_, dil) in enumerate(ATT_PATTERNS):
        o, l = _att_fwd(att_in, g, dil)
        att_o.append(o)
        att_l.append(l)
    comb_rows = [(t, ATT_WIDTH, 0) for t in att_o + att_l]
    (att,) = _rows_fwd(_f_comb, comb_rows, [], [(ATT_WIDTH, BF16)], name="comb_fwd", br=2 * br)
    w_ao = _cols_joined(wts["w_att_out"])
    y_att = _mm(att, w_ao, out_dtype=BF16, name="mm_att_out")
    rwpre_c = [wts["w0"], wts["a0"], wts["k_k"], wts["k_a"], wl, e, et]

    def shift_and_rwpre(zz, *rest):
        consts, mu_row, before = rest[:-2], rest[-2], rest[-1]
        last = jnp.sum(jnp.where(lax.broadcasted_iota(jnp.int32, before.shape, 0) == HALO - 1, before, 0.0), axis=0,
                       keepdims=True)
        row = lax.broadcasted_iota(jnp.int32, zz.shape, 0)
        zprev = jnp.where(row == 0, last, pltpu.roll(zz, 1, 0))
        shifted = zz + (zprev - zz) * mu_row
        return (shifted,) + tuple(_f_rwpre(shifted, *consts))

    zs, lw, km, aa, bb, gg = _rows_fwd(
        shift_and_rwpre, [(z, N_RWP, 0)], rwpre_c + [mu],
        [(N_RWP, F32), None, (D, F32), (D, F32), None, (D, F32), (D, F32), (D, F32)], name="rwpre_fwd", br=br, halo=0)
    return dict(x=x, tgt=tgt, wts=wts, br=br, e=e, et=et, gt1=gt1, sc2=sc2, sh2=sh2, gt2=gt2, w_att=w_att, w_rw=w_rw,
                w_ao=w_ao,
                w_gate=w_gate, mu=mu, pre1_c=pre1_c, h1=h1, att_in=att_in, z=z, gate_in=gate_in, comb_rows=comb_rows,
                att=att, y_att=y_att, zs=zs, rwpre_c=rwpre_c, lw=lw, km=km, aa=aa, bb=bb, gg=gg)


def _step_between_scans(st, y_raw, late):
    x, tgt, wts, br, e, et = st["x"], st["tgt"], st["wts"], st["br"], st["e"], st["et"]
    zs, km, gg, gate_in, y_att, att = st["zs"], st["km"], st["gg"], st["gate_in"], st["y_att"], st["att"]
    comb_rows, att_in = st["comb_rows"], st["att_in"]
    gt1, sc2, sh2, gt2 = st["gt1"], st["sc2"], st["sh2"], st["gt2"]
    w_up, w_ao = late["w_up"], st["w_ao"]
    w_down, w_o, w_ro = _rows_joined(late["w_down"]), _rows_joined(late["w_o"]), _rows_joined(late["w_rwkv_out"])
    post_rows = [(y_raw, D, 0), (zs, D, 0), (zs, D, 2), (km, D, 0), (gg, D, 0)]
    post_c = [wts["lnx_w"], wts["lnx_b"], wts["r_k"], e, et]
    (rw_out,) = _rows_fwd(_f_rwpost, post_rows, post_c, [(D, BF16)], name="rwpost_fwd", br=br)
    y_rw = _mm(rw_out, w_ro, out_dtype=BF16, name="mm_rw_out")
    mix_rows = [(gate_in, N_GATE, 0), (y_att, D, 0), (y_rw, D, 0)]
    (mix,) = _rows_fwd(_f_mix, mix_rows, [wts["b_gate"]], [(D, BF16)], name="mix_fwd", br=2 * br)
    o = _mm(mix, w_o, out_dtype=BF16, name="mm_o")
    pre2_c = [gt1, wts["norm2_w"], sc2, sh2]
    x1, h2 = _rows_fwd(_f_pre2, [(x, D, 0), (o, D, 0)], pre2_c, [(D, F32), (D, BF16)], name="pre2_fwd", br=2 * br)
    u = _mm(h2, w_up, b_chip=True, name="mm_up")
    act = _conv_fwd(u, wts["conv_w"], wts["conv_b"])
    f = _mm(act, w_down, out_dtype=BF16, name="mm_down")
    fin_rows = [(x1, D, 0), (f, D, 0), (tgt, D, 0)]
    fin_c = [gt2, wts["norm_f_w"]]

    def fin_fwd(*a):
        (l,) = _f_fin(*a)
        return (jnp.broadcast_to(jnp.sum(l, axis=0, keepdims=True), (8, 128)),)

    (loss_acc,) = _rows_fwd(fin_fwd, fin_rows, fin_c, [], name="fin_fwd", br=2 * br, acc_shape=(8, 128))

    gw = {}
    dx1a, df, d_gt2, gw["norm_f_w"] = _rows_bwd(
        _f_fin, fin_rows, fin_c, [[]], wrt_rows=[0, 1], wrt_consts=[0, 1], drow_dtypes=[F32, BF16],
        name="fin_bwd", br=2 * br, unit_cot=True)
    dact = _mm(df, w_down, tb=True, name="mm_dact")
    gw["w_down"] = _rows_split(_mm(act, df, ta=True, out_dtype=BF16, name="mm_dw_down"))
    du, gw["conv_w"], gw["conv_b"] = _conv_bwd(u, wts["conv_w"], wts["conv_b"], dact)
    dh2 = _mm(du, w_up, tb=True, b_chip=True, out_dtype=BF16, name="mm_dh2")
    gw["w_up"] = _mm(h2, du, ta=True, out_chip=True, out_dtype=BF16, name="mm_dw_up")
    dxa, do, d_gt1, gw["norm2_w"], d_sc2, d_sh2 = _rows_bwd(
        _f_pre2, [(x, D, 0), (o, D, 0)], pre2_c, [[(dx1a, D, 0)], [(dh2, D, 0)]], wrt_rows=[0, 1],
        wrt_consts=[0, 1, 2, 3], drow_dtypes=[F32, BF16], name="pre2_bwd", br=2 * br)
    dmix = _mm(do, w_o, tb=True, out_dtype=BF16, name="mm_dmix")
    gw["w_o"] = _rows_split(_mm(mix, do, ta=True, out_dtype=BF16, name="mm_dw_o"))
    dgate, dya, dyr, gw["b_gate"] = _rows_bwd(
        _f_mix, mix_rows, [wts["b_gate"]], [[(dmix, D, 0)]], wrt_rows=[0, 1, 2], wrt_consts=[0],
        drow_dtypes=[BF16] * 3, name="mix_bwd", br=2 * br)
    datt = _mm(dya, w_ao, tb=True, out_dtype=BF16, name="mm_datt")
    gw["w_att_out"] = _mm(att, dya, ta=True, out_chip=True, out_dtype=BF16, name="mm_dw_att_out")
    drw = _mm(dyr, w_ro, tb=True, out_dtype=BF16, name="mm_drw")
    gw["w_rwkv_out"] = _rows_split(_mm(rw_out, dyr, ta=True, out_dtype=BF16, name="mm_dw_rw_out"))
    dcomb = _rows_bwd(_f_comb, comb_rows, [], [[(datt, ATT_WIDTH, 0)]], wrt_rows=list(range(6)), wrt_consts=[],
                      drow_dtypes=[F32] * 6, name="comb_bwd", br=2 * br)
    datt_in = None
    for g, (_, dil) in enumerate(ATT_PATTERNS):
        datt_in = _att_bwd(att_in, g, dil, dcomb[g], dcomb[3 + g], datt_in)
    dy_raw, dr_p, dv_p, dkm_p, dgg, gw["lnx_w"], gw["lnx_b"], gw["r_k"], *recv_early = _rows_bwd(
        _f_rwpost, post_rows, post_c, [[(drw, D, 0)]], wrt_rows=[0, 1, 2, 3, 4], wrt_consts=[0, 1, 2],
        drow_dtypes=[F32] * 5, name="rwpost_bwd", br=br, comm=_SiblingHalves([gw[n] for n in _DONE_EARLY]))
    st.update(loss=loss_acc[0, 0], gw=gw, dxa=dxa, dgate=dgate, datt_in=datt_in,
              dy_raw=dy_raw, dr_p=dr_p, dv_p=dv_p, dkm_p=dkm_p, dgg=dgg, d_ada_late=(d_gt1, d_sh2, d_sc2, d_gt2),
              recv_early=recv_early)
    return st


def _chip_parts(grads, recv, names, core):
    return [_half_sum(lambda a, b: a + b, [g], [r], False, BF16, core, "reduce_add2_" + n)
            for g, r, n in zip(grads, recv, names)]


def _step_after_scan(st, scan_grads, core):
    x, br, gw, h1, zs = st["x"], st["br"], st["gw"], st["h1"], st["zs"]
    dr_s, dlw, dkm_s, dv_s, daa, dbb = scan_grads
    pre_cots = [[(st["dr_p"], D, 0), (dr_s, D, 0)], [(dlw, D, 0)], [(st["dkm_p"], D, 0), (dkm_s, D, 0)],
                [(st["dv_p"], D, 0), (dv_s, D, 0)], [(daa, D, 0)], [(dbb, D, 0)], [(st["dgg"], D, 0)]]
    dz, dmu, gw["w0"], gw["a0"], gw["k_k"], gw["k_a"], dwl = _rwpre_shift_bwd(
        zs, st["z"], st["mu"], st["rwpre_c"], pre_cots, br=128)
    gw["w2"], gw["a2"] = _cols_split(dwl[0:64, 0:D]), _cols_split(dwl[64:128, D:2 * D])
    gw["g2"] = _cols_split(dwl[128:288, 2 * D:3 * D])
    gw["mu_shift"] = dmu[:, :N_RW]
    datt_in, dgate = st["datt_in"], st["dgate"]
    dw_in = [(_mm(datt_in, h1, ta=True, out_dtype=BF16, name="mm_dw_att"), N_ATT),
             (_mm(dz, h1, ta=True, out_dtype=BF16, name="mm_dw_rw"), N_RW),
             (_mm(dgate, h1, ta=True, out_dtype=BF16, name="mm_dw_gate"), N_GATE)]
    slabs = []
    for j in range(4):
        slabs += [_row_window(dw_in, j * W_IN_SHARD, (j + 1) * W_IN_SHARD), jnp.zeros((W_IN_PAD - W_IN_SHARD, D), BF16)]
    gw["w_in"] = jnp.concatenate(slabs, axis=0).reshape(4, W_IN_PAD, D)
    late = [gw[n] for n in _DONE_LATE]
    parts = _chip_parts(late, _run_comm(_SiblingHalves(late), "reduce_sib_late"), _DONE_LATE, core)
    dh1, slots_late = _mm_sum([(datt_in, st["w_att"]), (dz, st["w_rw"]), (dgate, st["w_gate"])],
                              comm=_ScatterToChips(parts), name="mm_dh1")
    grad_x, gw["norm1_w"], d_sc1, d_sh1 = _rows_bwd(
        _f_pre, [(x, D, 0)], st["pre1_c"], [[(dh1, D, 0)], [(st["dxa"], D, 0)]], wrt_rows=[0], wrt_consts=[0, 1, 2],
        drow_dtypes=[F32], name="pre1_bwd", br=2 * br)
    d_gt1, d_sh2, d_sc2, d_gt2 = st["d_ada_late"]
    return st["loss"], grad_x, (d_sh1, d_sc1, d_gt1, d_sh2, d_sc2, d_gt2), gw, slots_late


_SMALL = ("b_ada", "norm1_w", "b_gate", "mu_shift", "w0", "a0", "k_k", "k_a", "r_k", "lnx_w", "lnx_b", "norm2_w",
          "conv_b", "norm_f_w")
_NAMES = ("w_ada", "b_ada", "norm1_w", "w_in", "b_gate", "mu_shift", "w0", "w2", "a0", "a2", "g2", "k_k", "k_a", "r_k",
          "lnx_w", "lnx_b", "w_att_out", "w_rwkv_out", "w_o", "norm2_w", "w_up", "conv_w", "conv_b", "w_down",
          "norm_f_w")


def kernel(x, c, w_ada, b_ada, norm1_w, w_in, b_gate, mu_shift, w0, w2, a0, a2, g2, k_k, k_a, r_k, lnx_w, lnx_b, w_att_out, w_rwkv_out, w_o, norm2_w, w_up, conv_w, conv_b, w_down, norm_f_w, loss_target, m_w_ada, m_b_ada, m_norm1_w, m_w_in, m_b_gate, m_mu_shift, m_w0, m_w2, m_a0, m_a2, m_g2, m_k_k, m_k_a, m_r_k, m_lnx_w, m_lnx_b, m_w_att_out, m_w_rwkv_out, m_w_o, m_norm2_w, m_w_up, m_conv_w, m_conv_b, m_w_down, m_norm_f_w, v_w_ada, v_b_ada, v_norm1_w, v_w_in, v_b_gate, v_mu_shift, v_w0, v_w2, v_a0, v_a2, v_g2, v_k_k, v_k_a, v_r_k, v_lnx_w, v_lnx_b, v_w_att_out, v_w_rwkv_out, v_w_o, v_norm2_w, v_w_up, v_conv_w, v_conv_b, v_w_down, v_norm_f_w):
    args = dict(locals())
    p, pm, pv = {}, {}, {}
    for name in _NAMES:
        for dst, key in ((p, name), (pm, "m_" + name), (pv, "v_" + name)):
            t = args[key]
            if name == "w_in":
                dst[name] = jnp.swapaxes(t, 1, 2)[0]
            else:
                dst[name] = t.reshape(1, -1) if name in ("r_k", "norm_f_w") else t.reshape(t.shape[-2], t.shape[-1])
    xi, yi, ci = _me()
    chip = 2 * xi + yi
    dev = 4 * xi + 2 * yi + ci
    x2, tgt = x[0], loss_target[0]

    n_cw = 3 * (2 * D_FF // 4)
    vec = jnp.concatenate([c.reshape(-1), p["conv_w"].reshape(-1), jnp.zeros((8 * D - D - n_cw,), F32)]).reshape(8, D)
    g0 = _allgather8(vec, "gather_c").reshape(8, 8 * D)
    c_all = g0[:, :D]
    conv_w_full = jnp.concatenate([g0[2 * j, D:D + n_cw].reshape(3, -1) for j in range(4)], axis=1)
    n_ada = 6 * D // 4
    b_ada_sh = lax.dynamic_slice(p["b_ada"], (0, chip * n_ada), (1, n_ada))
    ada_sh = _ada_fwd(c_all, p["w_ada"], b_ada_sh)
    ga = _allgather8(ada_sh, "gather_ada")
    ada_all = jnp.concatenate([ga[2 * j] for j in range(4)], axis=1)
    ada_row = lax.dynamic_slice(ada_all, (dev, 0), (1, 6 * D))
    ada = [ada_row[:, j * D:(j + 1) * D] for j in range(6)]

    big = [n for n, _ in _BIG]
    shard = {n: p[n].astype(BF16) for n in big}
    shard["w_in"] = jnp.pad(shard["w_in"], ((0, W_IN_PAD - W_IN_SHARD), (0, 0)))
    wts = dict(zip(_NEEDED_FIRST, _run_comm(_GatherWeights([shard[n] for n in _NEEDED_FIRST]), "gather_w")))
    for n in _SMALL:
        wts[n] = p[n]
    wts["conv_w"] = conv_w_full
    core = ci.reshape(1).astype(jnp.int32)

    st = _step_to_scan(x2, tgt, ada, wts)
    y_raw, s0s, inverses, late = _scan_fwd(st["zs"], st["lw"], st["km"], st["aa"], st["bb"],
                                           _GatherWeights([shard[n] for n in _NEEDED_LATER]))
    st = _step_between_scans(st, y_raw, dict(zip(_NEEDED_LATER, late)))
    early = _chip_parts([st["gw"][n] for n in _DONE_EARLY], st["recv_early"], _DONE_EARLY, core)
    scan_grads, slots_early = _scan_bwd(st["zs"], st["lw"], st["km"], st["aa"], st["bb"], s0s, inverses,
                                        st["dy_raw"], _ScatterToChips(early))
    loss_part, grad_x, d_ada, gw, slots_late = _step_after_scan(st, scan_grads, core)

    small = [jnp.concatenate(d_ada, axis=1)] + [gw[n] for n in _SMALL[1:]] + [gw["conv_w"], loss_part.reshape(1, 1)]
    sizes = [t.size for t in small]
    flat = jnp.concatenate([t.reshape(-1) for t in small])
    npad = (-flat.shape[0]) % (8 * D)
    srows = (flat.shape[0] + npad) // D
    flat = jnp.concatenate([flat, jnp.zeros((npad,), F32)]).reshape(srows, D)
    parts = _allgather8(flat, "gather_small")
    tot = _sum_lead(parts, "sum_small").reshape(-1)
    pieces, pos = [], 0
    for sz in sizes:
        pieces.append(tot[pos:pos + sz])
        pos += sz
    grads = {}
    for n, piece in zip(_SMALL, pieces[:len(_SMALL)]):
        grads[n] = piece.reshape(p[n].shape)
    conv_w_grad = pieces[len(_SMALL)].reshape(3, 2 * D_FF)
    grads["conv_w"] = lax.dynamic_slice(conv_w_grad, (0, chip * (n_cw // 3)), (3, n_cw // 3))
    loss = pieces[-1][0]
    d_ada_all = parts[:, :6].reshape(8, 6 * D)
    grads["w_ada"] = _ada_bwd(c_all.T, lax.dynamic_slice(d_ada_all, (0, chip * n_ada), (8, n_ada)))

    order = _DONE_EARLY + _DONE_LATE
    reds = [_half_sum(lambda t: t[0] + t[1] + t[2] + t[3], [], [t], True, F32, core, "reduce_add4_" + n)
            for n, t in zip(order, list(slots_early) + list(slots_late))]
    for n, g in zip(order, _reduce_finish(reds, "reduce_sib2")):
        grads[n] = g

    outs_g, outs_d, outs_m, outs_v = [], [], [], []
    grads["w_in"] = grads["w_in"][:W_IN_SHARD]
    for name in _NAMES:
        g = grads[name]
        d, m, v = _adamw(p[name], g, pm[name], pv[name], "adamw_" + name)
        shape = args[name].shape
        for outs, t in ((outs_g, g), (outs_d, d), (outs_m, m), (outs_v, v)):
            outs.append(jnp.swapaxes(t[None], 1, 2) if name == "w_in" else t.reshape(shape))
    return (loss, grad_x.reshape(x.shape), *outs_g, *outs_d, *outs_m, *outs_v)
```

```python
import functools
import math

import jax
import jax.numpy as jnp
from jax import lax
from jax.experimental import pallas as pl
from jax.experimental.pallas import tpu as pltpu

F32 = jnp.float32
BF16 = jnp.bfloat16
HI = lax.Precision.HIGHEST
MESH = pl.DeviceIdType.MESH

D = 1024
ATT_PATTERNS = ((128, 1), (512, 4), (2048, 16))
ATT_BLOCK = 128
ATT_WIDTH = 512
N_ATT = 3 * 3 * ATT_WIDTH
N_RW = 3 * D + 64 + 64 + 160
N_RWP = 3456
N_LORA = N_RWP - 3 * D
N_GATE = 2 * D
D_FF = 2816
RMS_EPS = 1e-6
GN_EPS = 64e-5
SCAN_CHUNK = 64
SCAN_PAIRS = 8
NEG = -1e30
VMEM_LIMIT = 48 * 1024 * 1024
HALO = 16

ADAM_LR, ADAM_B1, ADAM_B2, ADAM_EPS, ADAM_WD, ADAM_STEP = 0.001, 0.9, 0.999, 1e-08, 0.01, 10


def _pcall(body, **kw):
    return pl.pallas_call(body, **kw)


def _cparams(sem):
    return pltpu.CompilerParams(dimension_semantics=sem, vmem_limit_bytes=VMEM_LIMIT)


def _div(n, pref, mult):
    best = None
    d = mult
    while d <= min(n, pref):
        if n % d == 0:
            best = d
        d += mult
    return best if best else n


def _dg(a, b, ca, cb):
    return lax.dot_general(a.astype(BF16), b.astype(BF16), (((ca,), (cb,)), ((), ())), preferred_element_type=F32)


@jax.custom_vjp
def _nn(a, b):
    return _dg(a, b, 1, 0)


@jax.custom_vjp
def _nt(a, b):
    return _dg(a, b, 1, 1)


@jax.custom_vjp
def _tn(a, b):
    return _dg(a, b, 0, 0)


_nn.defvjp(lambda a, b: (_nn(a, b), (a, b)), lambda res, g: (_nt(g, res[1]), _tn(res[0], g)))
_nt.defvjp(lambda a, b: (_nt(a, b), (a, b)), lambda res, g: (_nn(g, res[1]), _tn(g, res[0])))
_tn.defvjp(lambda a, b: (_tn(a, b), (a, b)), lambda res, g: (_nt(res[1], g), _nn(res[0], g)))


def _bdg(a, b, ca, cb):
    return lax.dot_general(a.astype(BF16), b.astype(BF16), (((ca,), (cb,)), ((0,), (0,))), preferred_element_type=F32)


@jax.custom_vjp
def _bnn(a, b):
    return _bdg(a, b, 2, 1)


@jax.custom_vjp
def _bnt(a, b):
    return _bdg(a, b, 2, 2)


@jax.custom_vjp
def _btn(a, b):
    return _bdg(a, b, 1, 1)


_bnn.defvjp(lambda a, b: (_bnn(a, b), (a, b)), lambda res, g: (_bnt(g, res[1]), _btn(res[0], g)))
_bnt.defvjp(lambda a, b: (_bnt(a, b), (a, b)), lambda res, g: (_bnn(g, res[1]), _btn(g, res[0])))
_btn.defvjp(lambda a, b: (_btn(a, b), (a, b)), lambda res, g: (_bnt(res[1], g), _bnn(res[0], g)))


def _hsum_impl(x, e, et):
    eb, etb = e.astype(BF16), et.astype(BF16)
    s = jnp.dot(x.astype(BF16), eb, preferred_element_type=F32)
    return jnp.dot(s.astype(BF16), etb, preferred_element_type=F32)


@jax.custom_vjp
def _hsum(x, e, et):
    return _hsum_impl(x, e, et)


_hsum.defvjp(lambda x, e, et: (_hsum_impl(x, e, et), (e, et)),
             lambda res, g: (_hsum_impl(g, res[0], res[1]), jnp.zeros_like(res[0]), jnp.zeros_like(res[1])))


def _mm(a, b, *, ta=False, tb=False, out_dtype=F32, add=None, b_chip=False, out_chip=False, comm=None, name):
    riding = _NOTHING if comm is None else comm
    nc = riding.n
    if ta:
        kdim, m = a.shape
    else:
        m, kdim = a.shape
    if b_chip:
        n = b.shape[1] if tb else 4 * b.shape[2]
    else:
        n = b.shape[0] if tb else b.shape[1]
    tm, tn, tk = _div(m, 1536, 128), _div(n, 1536, 128), _div(kdim, 2048 if ta else 1408, 128)
    if b_chip and tb:
        tk = kdim // 4
    if (b_chip and not tb) or out_chip:
        tn = n // 4
    nk = kdim // tk
    ca, cb = (0 if ta else 1), (1 if tb else 0)

    nin = 2 if add is None else 3
    gi, gj = m // tm, n // tn

    def body(*refs):
        a_ref, b_ref = refs[0], refs[1]
        add_ref = None if add is None else refs[2]
        o_ref = refs[nin + nc]
        step = (pl.program_id(0) * gj + pl.program_id(1)) * nk + pl.program_id(2)
        before, after = _comm_phases(riding, refs[nin:nin + nc] + refs[nin + nc + 1:nin + 2 * nc + 1]
                                     + refs[nin + 2 * nc + 1 + (nk > 1):], gi * gj * nk, step)
        before()
        part = lax.dot_general(a_ref[...], b_ref[...], (((ca,), (cb,)), ((), ())), preferred_element_type=F32)

        def finish(r):
            if add_ref is not None:
                r = r + add_ref[...]
            o_ref[...] = r.astype(o_ref.dtype)

        if nk == 1:
            finish(part)
            after()
            return
        acc = refs[nin + 2 * nc + 1]
        k = pl.program_id(2)

        @pl.when(k == 0)
        def _():
            acc[...] = part

        @pl.when(k > 0)
        def _():
            acc[...] += part

        @pl.when(k == nk - 1)
        def _():
            finish(acc[...])

        after()

    a_spec = pl.BlockSpec((tk, tm), lambda i, j, k: (k, i)) if ta else pl.BlockSpec((tm, tk), lambda i, j, k: (i, k))
    if b_chip:
        b_spec = (pl.BlockSpec((None, tn, tk), lambda i, j, k: (k, j, 0)) if tb
                  else pl.BlockSpec((None, tk, tn), lambda i, j, k: (j, k, 0)))
    else:
        b_spec = pl.BlockSpec((tn, tk), lambda i, j, k: (j, k)) if tb else pl.BlockSpec((tk, tn), lambda i, j, k: (k, j))
    in_specs = [a_spec, b_spec]
    args = [a, b]
    if add is not None:
        in_specs.append(pl.BlockSpec((tm, tn), lambda i, j, k: (i, j)))
        args.append(add)
    if out_chip:
        out_spec = pl.BlockSpec((None, tm, tn), lambda i, j, k: (j, i, 0))
        out_shape = jax.ShapeDtypeStruct((4, m, tn), out_dtype)
    else:
        out_spec = pl.BlockSpec((tm, tn), lambda i, j, k: (i, j))
        out_shape = jax.ShapeDtypeStruct((m, n), out_dtype)
    res = _pcall(
        body, name=name, grid=(gi, gj, nk), in_specs=in_specs + [_HBM] * nc, out_specs=[out_spec] + [_HBM] * nc,
        out_shape=[out_shape] + riding.out_shape,
        scratch_shapes=([] if nk == 1 else [pltpu.VMEM((tm, tn), F32)]) + riding.sems,
        compiler_params=_cparams(("arbitrary",) * 3 if nc else ("parallel", "parallel", "arbitrary")),
    )(*args, *riding.ins)
    return res[0] if comm is None else (res[0], res[1:])


def _mm_sum(pairs, *, comm, name):
    m, n = pairs[0][0].shape[0], pairs[0][1].shape[1]
    tm, tn = _div(m, 1024, 128), _div(n, 1024, 128)
    tks = [_div(a.shape[1], 1408, 128) for a, _ in pairs]
    nks = [a.shape[1] // tk for (a, _), tk in zip(pairs, tks)]
    offs = [sum(nks[:p]) for p in range(len(pairs))]
    total, npair, nc = sum(nks), len(pairs), comm.n
    gi, gj = m // tm, n // tn

    def body(*refs):
        o_ref, acc = refs[2 * npair + nc], refs[2 * npair + 2 * nc + 1]
        k = pl.program_id(2)
        step = (pl.program_id(0) * gj + pl.program_id(1)) * total + k
        before, after = _comm_phases(comm, refs[2 * npair:2 * npair + nc]
                                     + refs[2 * npair + nc + 1:2 * npair + 2 * nc + 1]
                                     + refs[2 * npair + 2 * nc + 2:], gi * gj * total, step)
        before()
        for p in range(npair):
            def partial_product(p=p):
                part = jnp.dot(refs[2 * p][...], refs[2 * p + 1][...], preferred_element_type=F32)
                if p == 0:
                    @pl.when(k == 0)
                    def _():
                        acc[...] = part

                    @pl.when(k > 0)
                    def _():
                        acc[...] += part
                else:
                    acc[...] += part

            pl.when(jnp.logical_and(k >= offs[p], k < offs[p] + nks[p]))(partial_product)

        @pl.when(k == total - 1)
        def _():
            o_ref[...] = acc[...].astype(o_ref.dtype)

        after()

    def specs(tk, off, nk):
        def kb(k):
            return jnp.clip(k - off, 0, nk - 1)
        return [pl.BlockSpec((tm, tk), lambda i, j, k: (i, kb(k))), pl.BlockSpec((tk, tn), lambda i, j, k: (kb(k), j))]

    in_specs, args = [], []
    for (a, b), tk, off, nk in zip(pairs, tks, offs, nks):
        in_specs += specs(tk, off, nk)
        args += [a, b]
    res = _pcall(
        body, name=name, grid=(gi, gj, total), in_specs=in_specs + [_HBM] * nc,
        out_specs=[pl.BlockSpec((tm, tn), lambda i, j, k: (i, j))] + [_HBM] * nc,
        out_shape=[jax.ShapeDtypeStruct((m, n), BF16)] + comm.out_shape,
        scratch_shapes=[pltpu.VMEM((tm, tn), F32)] + comm.sems,
        compiler_params=_cparams(("arbitrary",) * 3),
    )(*args, *comm.ins)
    return res[0], res[1:]


def _row_spec(br, w, cb):
    return pl.BlockSpec((br, w), lambda i: (i, cb))


def _const_spec(shape):
    return pl.BlockSpec(shape, lambda i: (0,) * len(shape))


def _rows_fwd(fn, rows, consts, outs, *, name, br, acc_shape=None, halo=None):
    s = rows[0][0].shape[0]
    nr, nc = len(rows), len(consts)
    kept = [k for k, o in enumerate(outs) if o is not None]

    def body(*refs):
        xs = [r[...].astype(F32) for r in refs[:nr]]
        cs = [c[...] for c in refs[nr:nr + nc]]
        if halo is not None:
            cs.append(jnp.where(pl.program_id(0) == 0, 0.0, refs[nr + nc][...].astype(F32)))
        res = fn(*xs, *cs)
        orefs = refs[nr + nc + (halo is not None):]
        for j, k in enumerate(kept):
            orefs[j][...] = res[k].astype(orefs[j].dtype)
        if acc_shape is not None:
            acc_ref = orefs[len(kept)]

            @pl.when(pl.program_id(0) == 0)
            def _():
                acc_ref[...] = jnp.zeros_like(acc_ref)

            acc_ref[...] += res[len(outs)]

    in_specs = [_row_spec(br, w, cb) for (_, w, cb) in rows] + [_const_spec(c.shape) for c in consts]
    args = [r[0] for r in rows] + list(consts)
    if halo is not None:
        harr, hw, hcb = rows[halo]
        in_specs.append(pl.BlockSpec((HALO, hw), lambda i: (jnp.maximum(i * (br // HALO) - 1, 0), hcb)))
        args.append(harr)
    out_specs = [_row_spec(br, outs[k][0], 0) for k in kept]
    out_shape = [jax.ShapeDtypeStruct((s, outs[k][0]), outs[k][1]) for k in kept]
    if acc_shape is not None:
        out_specs.append(_const_spec(acc_shape))
        out_shape.append(jax.ShapeDtypeStruct(acc_shape, F32))
    return _pcall(
        body, name=name, grid=(pl.cdiv(s, br),), in_specs=in_specs, out_specs=out_specs, out_shape=out_shape,
        compiler_params=_cparams(("arbitrary",)),
    )(*args)


def _rows_bwd(fn, rows, consts, cots, *, wrt_rows, wrt_consts, drow_dtypes, name, br, unit_cot=False, comm=None):
    comm = _NOTHING if comm is None else comm
    ncomm = comm.n
    nout = len(wrt_rows) + len(wrt_consts)
    s = rows[0][0].shape[0]
    nr, nc = len(rows), len(consts)
    flat_cots = [c for lst in cots for c in lst]
    ncot = len(flat_cots)

    def body(*refs):
        xs = [r[...].astype(F32) for r in refs[:nr]]
        cs = [c[...] for c in refs[nr:nr + nc]]
        cvals = [c[...].astype(F32) for c in refs[nr + nc:nr + nc + ncot]]
        orefs = refs[nr + nc + ncot + ncomm:]
        before, after = _comm_phases(comm, refs[nr + nc + ncot:nr + nc + ncot + ncomm] + orefs[nout:], s // br)
        before()

        def g(*d):
            xs2, cs2 = list(xs), list(cs)
            for j, k in enumerate(wrt_rows):
                xs2[k] = d[j]
            for j, k in enumerate(wrt_consts):
                cs2[k] = d[len(wrt_rows) + j]
            return tuple(fn(*xs2, *cs2))

        prim = [xs[k] for k in wrt_rows] + [cs[k] for k in wrt_consts]
        outs, vjp = jax.vjp(g, *prim)
        ct = []
        pos = 0
        for o, lst in zip(outs, cots):
            if unit_cot:
                ct.append(jnp.ones_like(o))
                continue
            acc = jnp.zeros_like(o)
            for _ in lst:
                acc = acc + cvals[pos]
                pos += 1
            ct.append(acc)
        grads = vjp(tuple(ct))
        for j in range(len(wrt_rows)):
            orefs[j][...] = grads[j].astype(orefs[j].dtype)

        @pl.when(pl.program_id(0) == 0)
        def _():
            for j in range(len(wrt_consts)):
                oref = orefs[len(wrt_rows) + j]
                oref[...] = jnp.zeros_like(oref)

        for j in range(len(wrt_consts)):
            orefs[len(wrt_rows) + j][...] += grads[len(wrt_rows) + j]
        after()

    in_specs = ([_row_spec(br, w, cb) for (_, w, cb) in rows] + [_const_spec(c.shape) for c in consts]
                + [_row_spec(br, w, cb) for (_, w, cb) in flat_cots] + [_HBM] * ncomm)
    out_specs = ([_row_spec(br, rows[k][1], 0) for k in wrt_rows] + [_const_spec(consts[k].shape) for k in wrt_consts]
                 + [_HBM] * ncomm)
    out_shape = ([jax.ShapeDtypeStruct((s, rows[k][1]), dt) for k, dt in zip(wrt_rows, drow_dtypes)]
                 + [jax.ShapeDtypeStruct(consts[k].shape, F32) for k in wrt_consts] + comm.out_shape)
    return _pcall(
        body, name=name, grid=(s // br,), in_specs=in_specs, out_specs=out_specs, out_shape=out_shape,
        scratch_shapes=comm.sems, compiler_params=_cparams(("arbitrary",)),
    )(*[r[0] for r in rows], *consts, *[c[0] for c in flat_cots], *comm.ins)


def _rms(x, w):
    return x * lax.rsqrt(jnp.mean(x * x, axis=-1, keepdims=True) + RMS_EPS) * w


def _f_pre(x, nw, sc, sh):
    return _rms(x, nw) * (1.0 + sc) + sh, x


def _f_pre2(x, o, gt, nw, sc, sh):
    x1 = x + gt * o
    return x1, _rms(x1, nw) * (1.0 + sc) + sh


def _f_fin(x1, f, tgt, gt, nfw):
    y = _rms(x1 + gt * f, nfw)
    return (0.5 * jnp.mean(jnp.square(y - tgt), axis=-1, keepdims=True),)


def _f_comb(o1, o2, o3, l1, l2, l3):
    m = lax.stop_gradient(jnp.maximum(jnp.maximum(l1, l2), l3))
    e1, e2, e3 = jnp.exp(l1 - m), jnp.exp(l2 - m), jnp.exp(l3 - m)
    return ((e1 * o1 + e2 * o2 + e3 * o3) / (e1 + e2 + e3),)


def _f_rwpre(zs, w0, a0, k_k, k_a, wl, e, et):
    r, k, v, zl = zs[:, 0:D], zs[:, D:2 * D], zs[:, 2 * D:3 * D], zs[:, 3 * D:N_RWP]
    lane = lax.broadcasted_iota(jnp.int32, zl.shape, 1)
    t = jnp.where(lane < 64, jnp.tanh(zl), jnp.where(lane < 128, zl, jnp.where(lane < 288, jax.nn.sigmoid(zl), 0.0)))
    lo = _nn(t[:, 0:128], wl[0:128, 0:2 * D])
    g = _nn(t[:, 128:N_LORA], wl[128:N_LORA, 2 * D:3 * D])
    lw = -math.exp(-0.5) * jax.nn.sigmoid(w0 + lo[:, 0:D])
    a = jax.nn.sigmoid(a0 + lo[:, D:2 * D])
    k_mod = k * (1.0 + (a - 1.0) * k_a)
    kk = k * k_k
    kk = kk / jnp.maximum(jnp.sqrt(_hsum(kk * kk, e, et)), 1e-12)
    return r, lw, k_mod, v, -kk, kk * a, g


def _f_rwpost(y, r, v, k_mod, g, lnx_w, lnx_b, r_k, e, et):
    mean = _hsum(y, e, et) * (1.0 / 64)
    yc = y - mean
    var = _hsum(yc * yc, e, et) * (1.0 / 64)
    yn = yc * lax.rsqrt(var + GN_EPS) * lnx_w + lnx_b
    bonus = _hsum(r * k_mod * r_k, e, et) * v
    return ((yn + bonus) * g,)


def _f_mix(gi, ya, yr, bg):
    gate = jax.nn.sigmoid(gi + bg)
    return (gate[:, 0:D] * ya + gate[:, D:2 * D] * yr,)


def _f_adamw(w, g, m, v):
    m = ADAM_B1 * m + (1.0 - ADAM_B1) * g
    v = ADAM_B2 * v + (1.0 - ADAM_B2) * jnp.square(g)
    m_hat = m / (1.0 - ADAM_B1 ** ADAM_STEP)
    v_hat = v / (1.0 - ADAM_B2 ** ADAM_STEP)
    return -ADAM_LR * (m_hat / (jnp.sqrt(v_hat) + ADAM_EPS) + ADAM_WD * w), m, v


def _down(x, k):
    row = lax.broadcasted_iota(jnp.int32, x.shape, 0)
    return jnp.where(row < k, 0.0, pltpu.roll(x, k, 0))


def _up(x, k):
    n = x.shape[0]
    row = lax.broadcasted_iota(jnp.int32, x.shape, 0)
    return jnp.where(row >= n - k, 0.0, pltpu.roll(x, n - k, 0))


def _col_spec(s, w, off=0):
    return pl.BlockSpec((s, w), lambda j: (0, j + off))


def _rwpre_shift_bwd(zs, z, mu, consts, cots, *, br):
    s, w = zs.shape
    n = s // br
    flat = [c for lst in cots for c in lst]
    nc, ncot, nwrt = len(consts), len(flat), 5

    def this(i):
        return jnp.minimum(i, n - 1)

    def last(i):
        return jnp.maximum(i - 1, 0)

    def body(*refs):
        zs_ref, z_ref, zh_ref, mu_ref = refs[:4]
        c_refs, cot_refs = refs[4:4 + nc], refs[4 + nc:4 + nc + ncot]
        dz_ref, dmu_ref = refs[4 + nc + ncot:6 + nc + ncot]
        dc_refs = refs[6 + nc + ncot:6 + nc + ncot + nwrt]
        kept = refs[-1]
        i = pl.program_id(0)

        @pl.when(i == 0)
        def _():
            dmu_ref[...] = jnp.zeros_like(dmu_ref)
            for ref in dc_refs:
                ref[...] = jnp.zeros_like(ref)

        cs = [c[...] for c in c_refs]

        def g(zz, *d):
            return tuple(_f_rwpre(zz, *d, *cs[nwrt:]))

        outs, vjp = jax.vjp(g, zs_ref[...], *cs[:nwrt])
        cts, pos = [], 0
        for o, lst in zip(outs, cots):
            acc = jnp.zeros_like(o)
            for _ in lst:
                acc = acc + cot_refs[pos][...].astype(F32)
                pos += 1
            cts.append(acc)
        grads = vjp(tuple(cts))
        dzs_new = grads[0]

        @pl.when(i < n)
        def _():
            for ref, gr in zip(dc_refs, grads[1:]):
                ref[...] += gr

        @pl.when(i > 0)
        def _():
            d, m = kept[...], mu_ref[...]
            row = lax.broadcasted_iota(jnp.int32, d.shape, 0)
            head = jnp.sum(jnp.where(row == 0, dzs_new, 0.0), axis=0, keepdims=True)
            head = jnp.where(i < n, head, 0.0)
            dm = d * m
            after = jnp.where(row == br - 1, head * m, pltpu.roll(dm, br - 1, 0))
            dz_ref[...] = (d - dm + after).astype(dz_ref.dtype)
            zz, halo = z_ref[...].astype(F32), zh_ref[...].astype(F32)
            tail = jnp.sum(jnp.where(lax.broadcasted_iota(jnp.int32, halo.shape, 0) == HALO - 1, halo, 0.0), axis=0,
                           keepdims=True)
            before = jnp.where(row == 0, jnp.where(i > 1, tail, 0.0), pltpu.roll(zz, 1, 0))
            dmu_ref[...] += jnp.sum(d * (before - zz), axis=0, keepdims=True)

        kept[...] = dzs_new

    in_specs = ([pl.BlockSpec((br, w), lambda i: (this(i), 0)), pl.BlockSpec((br, w), lambda i: (last(i), 0)),
                 pl.BlockSpec((HALO, w), lambda i: (jnp.maximum(last(i) * (br // HALO) - 1, 0), 0)),
                 _const_spec(mu.shape)] + [_const_spec(c.shape) for c in consts]
                + [pl.BlockSpec((br, cw), lambda i, cb=cb: (this(i), cb)) for (_, cw, cb) in flat])
    out_specs = ([pl.BlockSpec((br, w), lambda i: (last(i), 0)), _const_spec(mu.shape)]
                 + [_const_spec(consts[k].shape) for k in range(nwrt)])
    out_shape = ([jax.ShapeDtypeStruct((s, w), BF16), jax.ShapeDtypeStruct(mu.shape, F32)]
                 + [jax.ShapeDtypeStruct(consts[k].shape, F32) for k in range(nwrt)])
    return _pcall(
        body, name="rwpre_shift_bwd", grid=(n + 1,), in_specs=in_specs, out_specs=out_specs, out_shape=out_shape,
        scratch_shapes=[pltpu.VMEM((br, w), F32)], compiler_params=_cparams(("arbitrary",)),
    )(zs, z, z, mu, *consts, *[c[0] for c in flat])


def _conv3(x, w_ref, b_ref):
    return b_ref[...] + w_ref[0:1, :] * _down(x, 2) + w_ref[1:2, :] * _down(x, 1) + w_ref[2:3, :] * x


def _conv_fwd(u, cw, cb):
    s = u.shape[0]
    nb = D_FF // 128

    def body(ug_ref, uv_ref, wg_ref, wv_ref, bg_ref, bv_ref, o_ref):
        gate = _conv3(ug_ref[...], wg_ref, bg_ref)
        val = _conv3(uv_ref[...], wv_ref, bv_ref)
        o_ref[...] = (gate * jax.nn.sigmoid(gate) * val).astype(o_ref.dtype)

    return _pcall(
        body, name="conv_fwd", grid=(nb,),
        in_specs=[_col_spec(s, 128), _col_spec(s, 128, nb), _col_spec(3, 128), _col_spec(3, 128, nb),
                  _col_spec(1, 128), _col_spec(1, 128, nb)],
        out_specs=_col_spec(s, 128), out_shape=jax.ShapeDtypeStruct((s, D_FF), BF16),
        compiler_params=_cparams(("parallel",)),
    )(u, u, cw, cw, cb, cb)


def _conv_bwd(u, cw, cb, dact):
    s = u.shape[0]
    nb = D_FF // 128

    def half(x, d, w_ref, du_ref, dw_ref, db_ref):
        x1, x2 = _down(x, 1), _down(x, 2)
        du_ref[...] = (w_ref[2:3, :] * d + w_ref[1:2, :] * _up(d, 1) + w_ref[0:1, :] * _up(d, 2)).astype(du_ref.dtype)
        dw_ref[0:1, :] = jnp.sum(d * x2, axis=0, keepdims=True)
        dw_ref[1:2, :] = jnp.sum(d * x1, axis=0, keepdims=True)
        dw_ref[2:3, :] = jnp.sum(d * x, axis=0, keepdims=True)
        db_ref[...] = jnp.sum(d, axis=0, keepdims=True)

    def body(ug_ref, uv_ref, wg_ref, wv_ref, bg_ref, bv_ref, da_ref,
             dug_ref, duv_ref, dwg_ref, dwv_ref, dbg_ref, dbv_ref):
        ug, uv, da = ug_ref[...], uv_ref[...], da_ref[...]
        gate = _conv3(ug, wg_ref, bg_ref)
        val = _conv3(uv, wv_ref, bv_ref)
        sg = jax.nn.sigmoid(gate)
        dgate = da * val * sg * (1.0 + gate * (1.0 - sg))
        dval = da * gate * sg
        half(ug, dgate, wg_ref, dug_ref, dwg_ref, dbg_ref)
        half(uv, dval, wv_ref, duv_ref, dwv_ref, dbv_ref)

    dug, duv, dwg, dwv, dbg, dbv = _pcall(
        body, name="conv_bwd", grid=(nb,),
        in_specs=[_col_spec(s, 128), _col_spec(s, 128, nb), _col_spec(3, 128), _col_spec(3, 128, nb),
                  _col_spec(1, 128), _col_spec(1, 128, nb), _col_spec(s, 128)],
        out_specs=[_col_spec(s, 128), _col_spec(s, 128), _col_spec(3, 128), _col_spec(3, 128),
                   _col_spec(1, 128), _col_spec(1, 128)],
        out_shape=[jax.ShapeDtypeStruct((s, D_FF), BF16), jax.ShapeDtypeStruct((s, D_FF), BF16),
                   jax.ShapeDtypeStruct((3, D_FF), F32), jax.ShapeDtypeStruct((3, D_FF), F32),
                   jax.ShapeDtypeStruct((1, D_FF), F32), jax.ShapeDtypeStruct((1, D_FF), F32)],
        compiler_params=_cparams(("parallel",)),
    )(u, u, cw, cw, cb, cb, dact)
    return (jnp.concatenate([dug, duv], axis=1), jnp.concatenate([dwg, dwv], axis=1),
            jnp.concatenate([dbg, dbv], axis=1))


ATT_BATCH = 4


def _att_batch(q, kp, kc, vp, vc, first):
    ma = lax.broadcasted_iota(jnp.int32, (1, ATT_BLOCK, 128), 2) < 64

    def diag(x):
        return jnp.concatenate([jnp.where(ma, x, 0.0), jnp.where(ma, 0.0, x)], axis=1)

    qi = lax.broadcasted_iota(jnp.int32, (1, ATT_BLOCK, 2 * ATT_BLOCK), 1)
    kj = lax.broadcasted_iota(jnp.int32, (1, ATT_BLOCK, 2 * ATT_BLOCK), 2) & (ATT_BLOCK - 1)
    okp = kj >= qi + jnp.where(first, 2 * ATT_BLOCK, 0)
    okc = kj <= qi
    sp = jnp.where(okp, _bnt(q, diag(kp)) * 0.125, NEG)
    sc = jnp.where(okc, _bnt(q, diag(kc)) * 0.125, NEG)

    def per_head(fn, x):
        return fn(x[..., :ATT_BLOCK]), fn(x[..., ATT_BLOCK:])

    def spread(ab):
        return jnp.concatenate([jnp.broadcast_to(t, t.shape[:2] + (ATT_BLOCK,)) for t in ab], axis=-1)

    row_max = functools.partial(jnp.max, axis=-1, keepdims=True)
    row_sum = functools.partial(jnp.sum, axis=-1, keepdims=True)
    m = [lax.stop_gradient(jnp.maximum(a, b)) for a, b in zip(per_head(row_max, sp), per_head(row_max, sc))]
    pp, pc = jnp.exp(sp - spread(m)), jnp.exp(sc - spread(m))
    den = [a + b for a, b in zip(per_head(row_sum, pp), per_head(row_sum, pc))]
    num = _bnn(pp, diag(vp)) + _bnn(pc, diag(vc))
    out = num / jnp.where(ma, den[0], den[1])
    lse = jnp.where(ma, m[0] + jnp.log(den[0]), m[1] + jnp.log(den[1]))
    return out, jnp.broadcast_to(lse, out.shape)


def _att_pairs_per_step(dil):
    return 4 if dil == 1 else 1


def _att_residues(dil):
    return min(dil, ATT_BATCH // _att_pairs_per_step(dil))


def _att_specs(g, dil):
    rows, pp = ATT_BLOCK * dil, _att_pairs_per_step(dil)

    def cur(slot):
        return pl.BlockSpec((rows, 128 * pp), lambda n, p: (n, (g * 3 + slot) * (4 // pp) + p))

    def prev(slot):
        return pl.BlockSpec((rows, 128 * pp), lambda n, p: (jnp.maximum(n - 1, 0), (g * 3 + slot) * (4 // pp) + p))

    return [cur(0), prev(1), cur(1), prev(2), cur(2)]


def _att_out_spec(dil):
    return pl.BlockSpec((ATT_BLOCK * dil, 128 * _att_pairs_per_step(dil)), lambda n, p: (n, p))


def _att_grid(s, dil):
    return (s // (ATT_BLOCK * dil), 4 // _att_pairs_per_step(dil))


def _att_windows(i, dil):
    res = _att_residues(dil)

    def rows(r):
        return pl.ds(i * res + r, ATT_BLOCK, stride=dil) if dil > 1 else pl.ds(0, ATT_BLOCK)

    return [(rows(r), pl.ds(128 * j, 128)) for j in range(_att_pairs_per_step(dil)) for r in range(res)]


def _att_fwd(att_in, g, dil):
    s = att_in.shape[0]

    def body(q_ref, kp_ref, kc_ref, vp_ref, vc_ref, o_ref, l_ref):
        first = pl.program_id(0) == 0

        def one(i, carry):
            win = _att_windows(i, dil)
            vals = [jnp.stack([ref[w] for w in win]) for ref in (q_ref, kp_ref, kc_ref, vp_ref, vc_ref)]
            o, l = _att_batch(*vals, first)
            for j, w in enumerate(win):
                o_ref[w] = o[j]
                l_ref[w] = l[j]
            return carry

        lax.fori_loop(0, dil // _att_residues(dil), one, 0)

    return _pcall(
        body, name=f"att_fwd{g}", grid=_att_grid(s, dil), in_specs=_att_specs(g, dil),
        out_specs=[_att_out_spec(dil)] * 2, out_shape=[jax.ShapeDtypeStruct((s, ATT_WIDTH), F32)] * 2,
        compiler_params=_cparams(("parallel", "parallel")),
    )(att_in, att_in, att_in, att_in, att_in)


def _att_bwd_lagged(att_in, g, dil, do, dl, acc):
    s = att_in.shape[0]
    rows, pp = ATT_BLOCK * dil, _att_pairs_per_step(dil)
    nb, npair, wid = s // rows, 4 // pp, 128 * pp

    def body(q_ref, kp_ref, kc_ref, vp_ref, vc_ref, do_ref, dl_ref, *rest):
        o_ref, lag_q, lag_k, lag_v = rest[-9:-5]
        stage = rest[-5:]
        n, p = pl.program_id(0), pl.program_id(1)
        first = n == 0

        @pl.when(n < nb)
        def _():
            def one(i, carry):
                win = _att_windows(i, dil)
                vals = [jnp.stack([ref[w] for w in win]) for ref in (q_ref, kp_ref, kc_ref, vp_ref, vc_ref)]
                _, vjp = jax.vjp(lambda *a: _att_batch(*a, first), *vals)
                grads = vjp((jnp.stack([do_ref[w] for w in win]), jnp.stack([dl_ref[w] for w in win])))
                for ref, gr in zip(stage, grads):
                    for j, w in enumerate(win):
                        ref[w] = gr[j]
                return carry

            lax.fori_loop(0, dil // _att_residues(dil), one, 0)

        live = n < nb
        for pj in range(npair):
            @pl.when((n > 0) & (p == pj))
            def _(pj=pj):
                c = pj * wid
                o_ref[:, c:c + wid] = lag_q[pj].astype(BF16)
                o_ref[:, ATT_WIDTH + c:ATT_WIDTH + c + wid] = (
                    lag_k[pj] + jnp.where(live, stage[1][...], 0.0)).astype(BF16)
                o_ref[:, 2 * ATT_WIDTH + c:2 * ATT_WIDTH + c + wid] = (
                    lag_v[pj] + jnp.where(live, stage[3][...], 0.0)).astype(BF16)

        @pl.when(live)
        def _():
            lag_q[p] = stage[0][...]
            lag_k[p] = stage[2][...]
            lag_v[p] = stage[4][...]

    def col(slot, n, p):
        return (g * 3 + slot) * npair + jnp.where(n < nb, p, npair - 1)

    def cur(slot):
        return pl.BlockSpec((rows, wid), lambda n, p: (jnp.minimum(n, nb - 1), col(slot, n, p)))

    def prev(slot):
        return pl.BlockSpec((rows, wid), lambda n, p: (jnp.maximum(jnp.minimum(n, nb - 1) - 1, 0), col(slot, n, p)))

    cot = pl.BlockSpec((rows, wid), lambda n, p: (jnp.minimum(n, nb - 1), jnp.where(n < nb, p, npair - 1)))
    carried = [] if acc is None else [acc]
    return _pcall(
        body, name=f"att_bwd{g}", grid=(nb + 1, npair),
        in_specs=[cur(0), prev(1), cur(1), prev(2), cur(2), cot, cot] + [pl.BlockSpec(memory_space=pl.ANY)] * len(carried),
        out_specs=pl.BlockSpec((rows, 3 * ATT_WIDTH), lambda n, p: (jnp.maximum(n - 1, 0), g)),
        out_shape=jax.ShapeDtypeStruct((s, N_ATT), BF16), input_output_aliases={7: 0} if carried else {},
        scratch_shapes=[pltpu.VMEM((npair, rows, wid), F32)] * 3 + [pltpu.VMEM((rows, wid), F32)] * 5,
        compiler_params=_cparams(("arbitrary", "arbitrary")),
    )(att_in, att_in, att_in, att_in, att_in, do, dl, *carried)


def _att_bwd(att_in, g, dil, do, dl, acc):
    if dil <= 4:
        return _att_bwd_lagged(att_in, g, dil, do, dl, acc)
    s = att_in.shape[0]

    def body(q_ref, kp_ref, kc_ref, vp_ref, vc_ref, do_ref, dl_ref, dq_ref, dkp_ref, dkc_ref, dvp_ref, dvc_ref):
        first = pl.program_id(0) == 0

        def one(i, carry):
            win = _att_windows(i, dil)
            vals = [jnp.stack([ref[w] for w in win]) for ref in (q_ref, kp_ref, kc_ref, vp_ref, vc_ref)]
            _, vjp = jax.vjp(lambda *a: _att_batch(*a, first), *vals)
            grads = vjp((jnp.stack([do_ref[w] for w in win]), jnp.stack([dl_ref[w] for w in win])))
            for ref, gr in zip((dq_ref, dkp_ref, dkc_ref, dvp_ref, dvc_ref), grads):
                for j, w in enumerate(win):
                    ref[w] = gr[j]
            return carry

        lax.fori_loop(0, dil // _att_residues(dil), one, 0)

    dq, dkp, dkc, dvp, dvc = _pcall(
        body, name=f"att_bwd{g}", grid=_att_grid(s, dil), in_specs=_att_specs(g, dil) + [_att_out_spec(dil)] * 2,
        out_specs=[_att_out_spec(dil)] * 5, out_shape=[jax.ShapeDtypeStruct((s, ATT_WIDTH), F32)] * 5,
        compiler_params=_cparams(("parallel", "parallel")),
    )(att_in, att_in, att_in, att_in, att_in, do, dl)

    unit, rb = ATT_BLOCK * dil, 1024
    steps = s // rb
    within = unit < rb

    def shifted(cur_ref, next_ref, has_next):
        nxt = jnp.where(has_next, next_ref[...], 0.0)
        return jnp.concatenate([cur_ref[unit:, :], nxt], axis=0) if within else nxt

    def cbody(dq_ref, dkc_ref, dkp_ref, dkn_ref, dvc_ref, dvp_ref, dvn_ref, *rest):
        o_ref = rest[-1]
        has_next = pl.program_id(0) + (1 if within else unit // rb) < steps
        o_ref[:, 0:ATT_WIDTH] = dq_ref[...].astype(BF16)
        o_ref[:, ATT_WIDTH:2 * ATT_WIDTH] = (dkc_ref[...] + shifted(dkp_ref, dkn_ref, has_next)).astype(BF16)
        o_ref[:, 2 * ATT_WIDTH:3 * ATT_WIDTH] = (dvc_ref[...] + shifted(dvp_ref, dvn_ref, has_next)).astype(BF16)

    cur = pl.BlockSpec((rb, ATT_WIDTH), lambda i: (i, 0))
    if within:
        nxt = pl.BlockSpec((unit, ATT_WIDTH), lambda i: (jnp.minimum((i + 1) * (rb // unit), s // unit - 1), 0))
    else:
        nxt = pl.BlockSpec((rb, ATT_WIDTH), lambda i: (jnp.minimum(i + unit // rb, steps - 1), 0))
    carried = [] if acc is None else [acc]
    return _pcall(
        cbody, name=f"att_bwd_sum{g}", grid=(steps,),
        in_specs=[cur, cur, cur, nxt, cur, cur, nxt] + [pl.BlockSpec(memory_space=pl.ANY)] * len(carried),
        out_specs=pl.BlockSpec((rb, 3 * ATT_WIDTH), lambda i: (i, g)),
        out_shape=jax.ShapeDtypeStruct((s, N_ATT), BF16), input_output_aliases={7: 0} if carried else {},
        compiler_params=_cparams(("parallel",)),
    )(dq, dkc, dkp, dkp, dvc, dvp, dvp, *carried)


def _cumsum_rows_impl(x):
    row = lax.broadcasted_iota(jnp.int32, x.shape, 0)
    shift = 1
    while shift < x.shape[0]:
        x = x + jnp.where(row >= shift, pltpu.roll(x, shift, 0), 0.0)
        shift *= 2
    return x


@jax.custom_vjp
def _cumsum_rows(x):
    return _cumsum_rows_impl(x)


_cumsum_rows.defvjp(lambda x: (_cumsum_rows_impl(x), None),
                    lambda _, g: (jnp.sum(g, axis=0, keepdims=True) - _cumsum_rows_impl(g) + g,))


def _unit_lower_inverse_impl(n):
    eye = (lax.broadcasted_iota(jnp.int32, (1,) + n.shape[1:], 1)
           == lax.broadcasted_iota(jnp.int32, (1,) + n.shape[1:], 2))
    t = jnp.where(eye, 1.0, 0.0) + n
    pw = n
    for _ in range(5):
        pw = _bnn(pw, pw)
        t = t + _bnn(t, pw)
    return t


@jax.custom_vjp
def _unit_lower_inverse(n):
    return _unit_lower_inverse_impl(n)


def _unit_lower_inverse_fwd(n):
    t = _unit_lower_inverse_impl(n)
    return t, t


_unit_lower_inverse.defvjp(_unit_lower_inverse_fwd, lambda t, g: (_bnt(_btn(t, g), t),))


@jax.custom_vjp
def _known_inverse(n, t):
    return t


_known_inverse.defvjp(lambda n, t: (t, t), lambda t, g: (_bnt(_btn(t, g), t), jnp.zeros_like(t)))


def _scan_chunk(r, lw, k, v, a, b, s0, inverse):
    c = SCAN_CHUNK
    p = s0.shape[0]
    cum = _cumsum_rows(lw)
    tot = jnp.sum(lw, axis=0, keepdims=True)
    ma = (lax.broadcasted_iota(jnp.int32, (c, 128 * p), 1) & 127) < 64

    def pairs(x):
        return jnp.concatenate([x[None, :, 128 * j:128 * (j + 1)] for j in range(p)], axis=0)

    def stack(x):
        return jnp.concatenate([pairs(jnp.where(ma, x, 0.0)), pairs(jnp.where(ma, 0.0, x))], axis=1)

    einv, eend = jnp.exp(-cum), jnp.exp(tot - cum)
    ra, aa = stack(r * jnp.exp(cum)), stack(a * jnp.exp(cum - lw))
    bi, ki, be, ke, vs = stack(b * einv), stack(k * einv), stack(b * eend), stack(k * eend), stack(v)
    r2 = lax.broadcasted_iota(jnp.int32, (1, 2 * c, 2 * c), 1)
    c2 = lax.broadcasted_iota(jnp.int32, (1, 2 * c, 2 * c), 2)
    same = (r2 >= c) == (c2 >= c)
    strict = jnp.logical_and(same, c2 < r2)
    incl = jnp.logical_and(same, c2 <= r2)
    s0 = jnp.where(same, s0, 0.0)
    prod = _bnt(jnp.concatenate([aa, ra], axis=1), jnp.concatenate([bi, ki], axis=1))
    a_ab = jnp.where(strict, prod[:, :2 * c, :2 * c], 0.0)
    a_ak = jnp.where(strict, prod[:, :2 * c, 2 * c:], 0.0)
    a_rb = jnp.where(incl, prod[:, 2 * c:, :2 * c], 0.0)
    a_rk = jnp.where(incl, prod[:, 2 * c:, 2 * c:], 0.0)
    t = inverse(a_ab)
    u = _bnn(t, _bnt(aa, s0) + _bnn(a_ak, vs))
    uv = jnp.concatenate([u, vs], axis=1)
    ys = _bnt(ra, s0) + _bnn(jnp.concatenate([a_rb, a_rk], axis=2), uv)
    s1 = s0 * pairs(jnp.exp(tot)) + _btn(uv, jnp.concatenate([be, ke], axis=1))
    y3 = ys[:, :c] + ys[:, c:]
    return (jnp.concatenate([y3[j] for j in range(p)], axis=1), s1), t


def _scan_specs(rev, n):
    def at(i):
        return n - 1 - i if rev else i

    def cm(cb):
        return pl.BlockSpec((SCAN_CHUNK, D), lambda i: (at(i), cb))

    return cm, pl.BlockSpec((1, SCAN_PAIRS, 128, 128), lambda i: (at(i), 0, 0, 0))


def _comm_phases(comm, refs, n, step=None):
    k = comm.n
    srcs, outs, sems = refs[:k], refs[k:2 * k], refs[2 * k:]
    i = pl.program_id(0) if step is None else step

    def before():
        @pl.when(i == 0)
        def _():
            comm.first(srcs, outs, sems)

    def after():
        if comm.mid is not None:
            @pl.when(i == (3 * n) // 4)
            def _():
                comm.mid(srcs, outs, sems)

        @pl.when(i == n - 1)
        def _():
            comm.last(srcs, outs, sems)

    return before, after


def _scan_fwd(zs, lw, km, aa, bb, comm):
    s = zs.shape[0]
    n = s // SCAN_CHUNK
    cm, st = _scan_specs(False, n)
    k = comm.n

    def body(*refs):
        r_ref, lw_ref, k_ref, v_ref, a_ref, b_ref = refs[:6]
        y_ref, s0_ref, t_ref = refs[6 + k:9 + k]
        state = refs[9 + 2 * k]
        before, after = _comm_phases(comm, refs[6:6 + k] + refs[9 + k:9 + 2 * k] + refs[10 + 2 * k:], n)
        before()

        @pl.when(pl.program_id(0) == 0)
        def _():
            state[...] = jnp.zeros_like(state)

        s0 = state[...]
        s0_ref[0] = s0
        (y, s1), t = _scan_chunk(*[ref[...] for ref in (r_ref, lw_ref, k_ref, v_ref, a_ref, b_ref)], s0,
                                 _unit_lower_inverse)
        y_ref[...] = y
        t_ref[0] = t.astype(BF16)
        state[...] = s1
        after()

    per_chunk = (n, SCAN_PAIRS, 128, 128)
    res = _pcall(
        body, name="scan_fwd", grid=(n,), in_specs=[cm(0), cm(0), cm(0), cm(2), cm(0), cm(0)] + [_HBM] * k,
        out_specs=[cm(0), st, st] + [_HBM] * k,
        out_shape=[jax.ShapeDtypeStruct((s, D), F32), jax.ShapeDtypeStruct(per_chunk, F32),
                   jax.ShapeDtypeStruct(per_chunk, BF16)] + comm.out_shape,
        scratch_shapes=[pltpu.VMEM((SCAN_PAIRS, 128, 128), F32)] + comm.sems,
        compiler_params=_cparams(("arbitrary",)),
    )(zs, lw, km, zs, aa, bb, *comm.ins)
    return res[0], res[1], res[2], res[3:]


def _scan_bwd(zs, lw, km, aa, bb, s0s, ts, dy, comm):
    s = zs.shape[0]
    n = s // SCAN_CHUNK
    cm, st = _scan_specs(True, n)
    k = comm.n

    def body(*refs):
        r_ref, lw_ref, k_ref, v_ref, a_ref, b_ref, s0_ref, t_ref, dy_ref = refs[:9]
        douts = refs[9 + k:15 + k]
        dstate = refs[15 + 2 * k]
        before, after = _comm_phases(comm, refs[9:9 + k] + refs[15 + k:15 + 2 * k] + refs[16 + 2 * k:], n)
        before()

        @pl.when(pl.program_id(0) == 0)
        def _():
            dstate[...] = jnp.zeros_like(dstate)

        t = t_ref[0].astype(F32)
        prim = [ref[...] for ref in (r_ref, lw_ref, k_ref, v_ref, a_ref, b_ref)] + [s0_ref[0]]
        _, vjp, _ = jax.vjp(lambda *p: _scan_chunk(*p, lambda nil: _known_inverse(nil, t)), *prim, has_aux=True)
        grads = vjp((dy_ref[...], dstate[...]))
        for ref, gr in zip(douts, grads[:6]):
            ref[...] = gr
        dstate[...] = grads[6]
        after()

    res = _pcall(
        body, name="scan_bwd", grid=(n,),
        in_specs=[cm(0), cm(0), cm(0), cm(2), cm(0), cm(0), st, st, cm(0)] + [_HBM] * k,
        out_specs=[cm(0)] * 6 + [_HBM] * k, out_shape=[jax.ShapeDtypeStruct((s, D), F32)] * 6 + comm.out_shape,
        scratch_shapes=[pltpu.VMEM((SCAN_PAIRS, 128, 128), F32)] + comm.sems,
        compiler_params=_cparams(("arbitrary",)),
    )(zs, lw, km, zs, aa, bb, s0s, ts, dy, *comm.ins)
    return res[:6], res[6:]


_HBM = pl.BlockSpec(memory_space=pltpu.HBM)


def _me():
    return lax.axis_index("x"), lax.axis_index("y"), lax.axis_index("c")


def _allgather8(src, name):
    def body(src_ref, out_ref, ssem, rsem, lsem):
        x, y, c = _me()
        me = 4 * x + 2 * y + c
        local = pltpu.make_async_copy(src_ref, out_ref.at[me], lsem)
        local.start()
        peers = []
        for k in range(1, 8):
            peers.append(((1 - x) if k & 4 else x, (1 - y) if k & 2 else y, (1 - c) if k & 1 else c))
        sends = []
        for k, peer in enumerate(peers):
            cp = pltpu.make_async_remote_copy(src_ref, out_ref.at[me], ssem.at[k], rsem.at[k], device_id=peer,
                                              device_id_type=MESH)
            cp.start()
            sends.append(cp)
        for k, (px, py, pc) in enumerate(peers):
            pltpu.make_async_remote_copy(src_ref, out_ref.at[4 * px + 2 * py + pc], ssem.at[k], rsem.at[k],
                                         device_id=(px, py, pc), device_id_type=MESH).wait_recv()
        for cp in sends:
            cp.wait_send()
        local.wait()

    return _pcall(
        body, name=name, in_specs=[_HBM], out_specs=_HBM, out_shape=jax.ShapeDtypeStruct((8,) + src.shape, src.dtype),
        scratch_shapes=[pltpu.SemaphoreType.DMA((7,)), pltpu.SemaphoreType.DMA((7,)), pltpu.SemaphoreType.DMA],
    )(src)


def _other_chips(x, y):
    return [(1 - x, y), (x, 1 - y), (1 - x, 1 - y)]


def _remote(src, dst, ssem, rsem, to):
    return pltpu.make_async_remote_copy(src, dst, ssem, rsem, device_id=to, device_id_type=MESH)


class _GatherWeights:
    def __init__(self, shards):
        self.ins = list(shards)
        n = self.n = len(shards)
        self.out_shape = [jax.ShapeDtypeStruct((4,) + t.shape, t.dtype) for t in shards]
        self.sems = [pltpu.SemaphoreType.DMA((6 * n,)), pltpu.SemaphoreType.DMA((6 * n,)),
                     pltpu.SemaphoreType.DMA((n,)), pltpu.SemaphoreType.DMA((n,))]

    def _copies(self, srcs, outs, sems):
        ssem, rsem, lsem, osem = sems
        x, y, c = _me()
        me = 2 * x + y
        own, ici, landed, passed, passed_in = [], [], [], [], []
        for a in range(self.n):
            h = self.ins[a].shape[0] // 2
            mine, other = pl.ds(c * h, h), pl.ds((1 - c) * h, h)
            own.append(_remote(srcs[a], outs[a].at[me], lsem.at[a], osem.at[a], (x, y, 1 - c)))
            for k, (px, py) in enumerate(_other_chips(x, y)):
                s1, r1, s2, r2 = ssem.at[6 * a + k], rsem.at[6 * a + k], ssem.at[6 * a + 3 + k], rsem.at[6 * a + 3 + k]
                got, got_sib = outs[a].at[2 * px + py, mine], outs[a].at[2 * px + py, other]
                ici.append(_remote(srcs[a].at[mine], outs[a].at[me, mine], s1, r1, (px, py, c)))
                landed.append(_remote(got, got, s1, r1, (px, py, c)))
                passed.append(_remote(got, got, s2, r2, (x, y, 1 - c)))
                passed_in.append(_remote(got_sib, got_sib, s2, r2, (x, y, 1 - c)))
        return own, ici, landed, passed, passed_in

    def first(self, srcs, outs, sems):
        own, ici, _, _, _ = self._copies(srcs, outs, sems)
        for cp in own + ici:
            cp.start()

    def mid(self, srcs, outs, sems):
        _, _, landed, passed, _ = self._copies(srcs, outs, sems)
        for arrived, onward in zip(landed, passed):
            arrived.wait_recv()
            onward.start()

    def last(self, srcs, outs, sems):
        own, ici, _, passed, passed_in = self._copies(srcs, outs, sems)
        for cp in passed_in:
            cp.wait_recv()
        for cp in ici + passed:
            cp.wait_send()
        for cp in own:
            cp.wait()


class _ScatterToChips:
    def __init__(self, parts):
        self.ins = list(parts)
        n = self.n = len(parts)
        self.out_shape = [jax.ShapeDtypeStruct(t.shape, t.dtype) for t in parts]
        self.sems = [pltpu.SemaphoreType.DMA((3 * n,)), pltpu.SemaphoreType.DMA((3 * n,)), pltpu.SemaphoreType.DMA((n,))]

    def _copies(self, srcs, outs, sems):
        ssem, rsem, lsem = sems
        x, y, c = _me()
        me = 2 * x + y
        own, out, landed = [], [], []
        for a in range(self.n):
            own.append(pltpu.make_async_copy(srcs[a].at[me], outs[a].at[me], lsem.at[a]))
            for k, (px, py) in enumerate(_other_chips(x, y)):
                dst = outs[a].at[2 * px + py]
                out.append(_remote(srcs[a].at[2 * px + py], outs[a].at[me], ssem.at[3 * a + k], rsem.at[3 * a + k],
                                   (px, py, c)))
                landed.append(_remote(dst, dst, ssem.at[3 * a + k], rsem.at[3 * a + k], (px, py, c)))
        return own, out, landed

    def first(self, srcs, outs, sems):
        own, out, _ = self._copies(srcs, outs, sems)
        for cp in own + out:
            cp.start()

    mid = None

    def last(self, srcs, outs, sems):
        own, out, landed = self._copies(srcs, outs, sems)
        for cp in landed:
            cp.wait_recv()
        for cp in own:
            cp.wait()
        for cp in out:
            cp.wait_send()


def _run_comm(comm, name):
    n = comm.n

    def body(*refs):
        srcs, outs, sems = refs[:n], refs[n:2 * n], refs[2 * n:]
        comm.first(srcs, outs, sems)
        if comm.mid is not None:
            comm.mid(srcs, outs, sems)
        comm.last(srcs, outs, sems)

    return _pcall(body, name=name, in_specs=[_HBM] * n, out_specs=[_HBM] * n, out_shape=comm.out_shape,
                  scratch_shapes=comm.sems)(*comm.ins)


class _NoComm:
    n, ins, out_shape, sems, mid = 0, [], [], [], None

    def first(self, srcs, outs, sems):
        pass

    def last(self, srcs, outs, sems):
        pass


_NOTHING = _NoComm()


class _SiblingHalves:
    mid = None

    def __init__(self, grads):
        self.ins = list(grads)
        n = self.n = len(grads)
        self.out_shape = [jax.ShapeDtypeStruct((4, t.shape[1] // 2, t.shape[2]), t.dtype) for t in grads]
        self.sems = [pltpu.SemaphoreType.DMA((n,)), pltpu.SemaphoreType.DMA((n,))]

    def _copies(self, srcs, outs, sems):
        ssem, rsem = sems
        x, y, c = _me()
        copies = []
        for a in range(self.n):
            h = self.ins[a].shape[1] // 2
            copies.append(_remote(srcs[a].at[:, pl.ds((1 - c) * h, h)], outs[a], ssem.at[a], rsem.at[a], (x, y, 1 - c)))
        return copies

    def first(self, srcs, outs, sems):
        for cp in self._copies(srcs, outs, sems):
            cp.start()

    def last(self, srcs, outs, sems):
        for cp in self._copies(srcs, outs, sems):
            cp.wait()


def _reduce_finish(reds, name):
    n = len(reds)

    def body(*refs):
        outs = refs[n:2 * n]
        ssem, rsem = refs[2 * n:]
        x, y, c = _me()
        copies = []
        for a in range(n):
            h = reds[a].shape[0] // 2
            mine = outs[a].at[pl.ds(c * h, h)]
            copies.append(_remote(mine, mine, ssem.at[a], rsem.at[a], (x, y, 1 - c)))
        for cp in copies:
            cp.start()
        for a in range(n):
            h = reds[a].shape[0] // 2
            dst = outs[a].at[pl.ds((1 - c) * h, h)]
            _remote(dst, dst, ssem.at[a], rsem.at[a], (x, y, 1 - c)).wait_recv()
        for cp in copies:
            cp.wait_send()

    return _pcall(
        body, name=name, in_specs=[_HBM] * n, out_specs=[_HBM] * n,
        out_shape=[jax.ShapeDtypeStruct(t.shape, t.dtype) for t in reds],
        input_output_aliases={a: a for a in range(n)},
        scratch_shapes=[pltpu.SemaphoreType.DMA((n,)), pltpu.SemaphoreType.DMA((n,))],
    )(*reds)


def _half_sum(fn, full, halves, out_full, out_dtype, core, name):
    p, h, c = (halves[0].shape if halves else (full[0].shape[0], full[0].shape[1] // 2, full[0].shape[2]))
    br = _div(h, max(16, (1 << 19) // (p * c)), 16)
    nb = h // br
    mine3 = pl.BlockSpec((p, br, c), lambda i, core_ref: (0, core_ref[0] * nb + i, 0))
    half3 = pl.BlockSpec((p, br, c), lambda i, core_ref: (0, i, 0))

    def body(core_ref, *refs):
        refs[-1][...] = fn(*[t[...].astype(F32) for t in refs[:-1]]).astype(out_dtype)

    if out_full:
        out_spec = pl.BlockSpec((br, c), lambda i, core_ref: (core_ref[0] * nb + i, 0))
        out_shape = jax.ShapeDtypeStruct((2 * h, c), out_dtype)
    else:
        out_spec, out_shape = half3, jax.ShapeDtypeStruct((p, h, c), out_dtype)
    return _pcall(
        body, name=name,
        grid_spec=pltpu.PrefetchScalarGridSpec(
            num_scalar_prefetch=1, grid=(nb,), in_specs=[mine3] * len(full) + [half3] * len(halves),
            out_specs=out_spec),
        out_shape=out_shape, compiler_params=_cparams(("parallel",)),
    )(core, *full, *halves)


def _ada_fwd(c_all, w, b):
    def body(c_ref, w_ref, b_ref, o_ref):
        o_ref[...] = jnp.dot(c_ref[...], w_ref[...], precision=HI, preferred_element_type=F32) + b_ref[...]

    return _pcall(body, name="ada_fwd", out_shape=jax.ShapeDtypeStruct((c_all.shape[0], w.shape[1]), F32),
                  compiler_params=pltpu.CompilerParams(vmem_limit_bytes=VMEM_LIMIT))(c_all, w, b)


def _ada_bwd(c_all_t, d):
    def body(c_ref, d_ref, o_ref):
        o_ref[...] = jnp.dot(c_ref[...], d_ref[...], precision=HI, preferred_element_type=F32)

    return _pcall(body, name="ada_bwd", out_shape=jax.ShapeDtypeStruct((c_all_t.shape[0], d.shape[1]), F32),
                  compiler_params=pltpu.CompilerParams(vmem_limit_bytes=VMEM_LIMIT))(c_all_t, d)


def _sum_lead(x, name):
    p, r, n = x.shape
    br = _div(r, 512, 8)

    def body(x_ref, o_ref):
        acc = x_ref[0]
        for j in range(1, p):
            acc = acc + x_ref[j]
        o_ref[...] = acc

    return _pcall(
        body, name=name, grid=(r // br,), in_specs=[pl.BlockSpec((p, br, n), lambda i: (0, i, 0))],
        out_specs=pl.BlockSpec((br, n), lambda i: (i, 0)), out_shape=jax.ShapeDtypeStruct((r, n), F32),
        compiler_params=_cparams(("parallel",)),
    )(x)


def _adamw(w, g, m, v, name):
    shape = w.shape
    cols = shape[-1]
    w2, g2, m2, v2 = [t.reshape(-1, cols) for t in (w, g, m, v)]
    rows = w2.shape[0]
    pref = max(8, (1 << 19) // cols // 8 * 8)
    br = _div(rows, pref, 8)
    if rows // br > 64:
        br = pref
    outs = _rows_fwd(_f_adamw, [(t, cols, 0) for t in (w2, g2, m2, v2)], [], [(cols, F32)] * 3, name=name, br=br)
    return [o.reshape(shape) for o in outs]


_BIG = (("w_in", 1), ("w_up", 1), ("w_down", 0), ("w_o", 0), ("w_rwkv_out", 0), ("w_att_out", 1), ("w2", 1), ("a2", 1),
        ("g2", 1))


_NEEDED_FIRST = ("w_in", "w_att_out", "w2", "a2", "g2")
_NEEDED_LATER = ("w_up", "w_down", "w_o", "w_rwkv_out")
_DONE_EARLY = ("w_up", "w_down", "w_o", "w_rwkv_out", "w_att_out")
_DONE_LATE = ("w_in", "w2", "a2", "g2")


def _cols_joined(t):
    return jnp.concatenate([t[j] for j in range(4)], axis=1)


def _cols_split(t):
    n = t.shape[1] // 4
    return jnp.stack([t[:, j * n:(j + 1) * n] for j in range(4)])


W_IN_SHARD = (N_ATT + N_RW + N_GATE) // 4
W_IN_PAD = 2560


def _row_window(parts, lo, hi):
    out, pos = [], 0
    for t, w in parts:
        a, b = max(lo, pos), min(hi, pos + w)
        if a < b:
            out.append(t[a - pos:b - pos])
        pos += w
    return out[0] if len(out) == 1 else jnp.concatenate(out, axis=0)


def _rows_joined(t):
    return t.reshape(4 * t.shape[1], t.shape[2])


def _rows_split(t):
    return t.reshape(4, t.shape[0] // 4, t.shape[1])


def _step_to_scan(x, tgt, ada, wts):
    sh1, sc1, gt1, sh2, sc2, gt2 = ada
    br = 256
    grp = lax.broadcasted_iota(jnp.int32, (D, 128), 0) // 64 == lax.broadcasted_iota(jnp.int32, (D, 128), 1)
    e = grp.astype(F32)
    et = e.T
    w_in = [(wts["w_in"][j], W_IN_SHARD) for j in range(4)]
    w_att = _row_window(w_in, 0, N_ATT)
    w_rw = jnp.concatenate([_row_window(w_in, N_ATT, N_ATT + N_RW), jnp.zeros((N_RWP - N_RW, D), BF16)], axis=0)
    w_gate = _row_window(w_in, N_ATT + N_RW, N_ATT + N_RW + N_GATE)
    mu = jnp.pad(wts["mu_shift"], ((0, 0), (0, N_RWP - N_RW)))
    wl = jnp.zeros((N_LORA, 3 * D), F32)
    wl = wl.at[0:64, 0:D].set(_cols_joined(wts["w2"]).astype(F32))
    wl = wl.at[64:128, D:2 * D].set(_cols_joined(wts["a2"]).astype(F32))
    wl = wl.at[128:288, 2 * D:3 * D].set(_cols_joined(wts["g2"]).astype(F32))
    pre1_c = [wts["norm1_w"], sc1, sh1]
    (h1,) = _rows_fwd(_f_pre, [(x, D, 0)], pre1_c, [(D, BF16), None], name="pre1_fwd", br=2 * br)
    att_in = _mm(h1, w_att, tb=True, name="mm_att_in")
    z = _mm(h1, w_rw, tb=True, out_dtype=BF16, name="mm_rw_in")
    gate_in = _mm(h1, w_gate, tb=True, out_dtype=BF16, name="mm_gate_in")
    att_o, att_l = [], []
    for g, (_, dil) in enumerate(ATT_PATTERNS):
        o, l = _att_fwd(att_in, g, dil)
        att_o.append(o)
        att_l.append(l)
    comb_rows = [(t, ATT_WIDTH, 0) for t in att_o + att_l]
    (att,) = _rows_fwd(_f_comb, comb_rows, [], [(ATT_WIDTH, BF16)], name="comb_fwd", br=2 * br)
    w_ao = _cols_joined(wts["w_att_out"])
    y_att = _mm(att, w_ao, out_dtype=BF16, name="mm_att_out")
    rwpre_c = [wts["w0"], wts["a0"], wts["k_k"], wts["k_a"], wl, e, et]

    def shift_and_rwpre(zz, *rest):
        consts, mu_row, before = rest[:-2], rest[-2], rest[-1]
        last = jnp.sum(jnp.where(lax.broadcasted_iota(jnp.int32, before.shape, 0) == HALO - 1, before, 0.0), axis=0,
                       keepdims=True)
        row = lax.broadcasted_iota(jnp.int32, zz.shape, 0)
        zprev = jnp.where(row == 0, last, pltpu.roll(zz, 1, 0))
        shifted = zz + (zprev - zz) * mu_row
        return (shifted,) + tuple(_f_rwpre(shifted, *consts))

    zs, lw, km, aa, bb, gg = _rows_fwd(
        shift_and_rwpre, [(z, N_RWP, 0)], rwpre_c + [mu],
        [(N_RWP, F32), None, (D, F32), (D, F32), None, (D, F32), (D, F32), (D, F32)], name="rwpre_fwd", br=br, halo=0)
    return dict(x=x, tgt=tgt, wts=wts, br=br, e=e, et=et, gt1=gt1, sc2=sc2, sh2=sh2, gt2=gt2, w_att=w_att, w_rw=w_rw,
                w_ao=w_ao,
                w_gate=w_gate, mu=mu, pre1_c=pre1_c, h1=h1, att_in=att_in, z=z, gate_in=gate_in, comb_rows=comb_rows,
                att=att, y_att=y_att, zs=zs, rwpre_c=rwpre_c, lw=lw, km=km, aa=aa, bb=bb, gg=gg)


def _step_between_scans(st, y_raw, late):
    x, tgt, wts, br, e, et = st["x"], st["tgt"], st["wts"], st["br"], st["e"], st["et"]
    zs, km, gg, gate_in, y_att, att = st["zs"], st["km"], st["gg"], st["gate_in"], st["y_att"], st["att"]
    comb_rows, att_in = st["comb_rows"], st["att_in"]
    gt1, sc2, sh2, gt2 = st["gt1"], st["sc2"], st["sh2"], st["gt2"]
    w_up, w_ao = late["w_up"], st["w_ao"]
    w_down, w_o, w_ro = _rows_joined(late["w_down"]), _rows_joined(late["w_o"]), _rows_joined(late["w_rwkv_out"])
    post_rows = [(y_raw, D, 0), (zs, D, 0), (zs, D, 2), (km, D, 0), (gg, D, 0)]
    post_c = [wts["lnx_w"], wts["lnx_b"], wts["r_k"], e, et]
    (rw_out,) = _rows_fwd(_f_rwpost, post_rows, post_c, [(D, BF16)], name="rwpost_fwd", br=br)
    y_rw = _mm(rw_out, w_ro, out_dtype=BF16, name="mm_rw_out")
    mix_rows = [(gate_in, N_GATE, 0), (y_att, D, 0), (y_rw, D, 0)]
    (mix,) = _rows_fwd(_f_mix, mix_rows, [wts["b_gate"]], [(D, BF16)], name="mix_fwd", br=2 * br)
    o = _mm(mix, w_o, out_dtype=BF16, name="mm_o")
    pre2_c = [gt1, wts["norm2_w"], sc2, sh2]
    x1, h2 = _rows_fwd(_f_pre2, [(x, D, 0), (o, D, 0)], pre2_c, [(D, F32), (D, BF16)], name="pre2_fwd", br=2 * br)
    u = _mm(h2, w_up, b_chip=True, name="mm_up")
    act = _conv_fwd(u, wts["conv_w"], wts["conv_b"])
    f = _mm(act, w_down, out_dtype=BF16, name="mm_down")
    fin_rows = [(x1, D, 0), (f, D, 0), (tgt, D, 0)]
    fin_c = [gt2, wts["norm_f_w"]]

    def fin_fwd(*a):
        (l,) = _f_fin(*a)
        return (jnp.broadcast_to(jnp.sum(l, axis=0, keepdims=True), (8, 128)),)

    (loss_acc,) = _rows_fwd(fin_fwd, fin_rows, fin_c, [], name="fin_fwd", br=2 * br, acc_shape=(8, 128))

    gw = {}
    dx1a, df, d_gt2, gw["norm_f_w"] = _rows_bwd(
        _f_fin, fin_rows, fin_c, [[]], wrt_rows=[0, 1], wrt_consts=[0, 1], drow_dtypes=[F32, BF16],
        name="fin_bwd", br=2 * br, unit_cot=True)
    dact = _mm(df, w_down, tb=True, name="mm_dact")
    gw["w_down"] = _rows_split(_mm(act, df, ta=True, out_dtype=BF16, name="mm_dw_down"))
    du, gw["conv_w"], gw["conv_b"] = _conv_bwd(u, wts["conv_w"], wts["conv_b"], dact)
    dh2 = _mm(du, w_up, tb=True, b_chip=True, out_dtype=BF16, name="mm_dh2")
    gw["w_up"] = _mm(h2, du, ta=True, out_chip=True, out_dtype=BF16, name="mm_dw_up")
    dxa, do, d_gt1, gw["norm2_w"], d_sc2, d_sh2 = _rows_bwd(
        _f_pre2, [(x, D, 0), (o, D, 0)], pre2_c, [[(dx1a, D, 0)], [(dh2, D, 0)]], wrt_rows=[0, 1],
        wrt_consts=[0, 1, 2, 3], drow_dtypes=[F32, BF16], name="pre2_bwd", br=2 * br)
    dmix = _mm(do, w_o, tb=True, out_dtype=BF16, name="mm_dmix")
    gw["w_o"] = _rows_split(_mm(mix, do, ta=True, out_dtype=BF16, name="mm_dw_o"))
    dgate, dya, dyr, gw["b_gate"] = _rows_bwd(
        _f_mix, mix_rows, [wts["b_gate"]], [[(dmix, D, 0)]], wrt_rows=[0, 1, 2], wrt_consts=[0],
        drow_dtypes=[BF16] * 3, name="mix_bwd", br=2 * br)
    datt = _mm(dya, w_ao, tb=True, out_dtype=BF16, name="mm_datt")
    gw["w_att_out"] = _mm(att, dya, ta=True, out_chip=True, out_dtype=BF16, name="mm_dw_att_out")
    drw = _mm(dyr, w_ro, tb=True, out_dtype=BF16, name="mm_drw")
    gw["w_rwkv_out"] = _rows_split(_mm(rw_out, dyr, ta=True, out_dtype=BF16, name="mm_dw_rw_out"))
    dcomb = _rows_bwd(_f_comb, comb_rows, [], [[(datt, ATT_WIDTH, 0)]], wrt_rows=list(range(6)), wrt_consts=[],
                      drow_dtypes=[F32] * 6, name="comb_bwd", br=2 * br)
    datt_in = None
    for g, (_, dil) in enumerate(ATT_PATTERNS):
        datt_in = _att_bwd(att_in, g, dil, dcomb[g], dcomb[3 + g], datt_in)
    dy_raw, dr_p, dv_p, dkm_p, dgg, gw["lnx_w"], gw["lnx_b"], gw["r_k"], *recv_early = _rows_bwd(
        _f_rwpost, post_rows, post_c, [[(drw, D, 0)]], wrt_rows=[0, 1, 2, 3, 4], wrt_consts=[0, 1, 2],
        drow_dtypes=[F32] * 5, name="rwpost_bwd", br=br, comm=_SiblingHalves([gw[n] for n in _DONE_EARLY]))
    st.update(loss=loss_acc[0, 0], gw=gw, dxa=dxa, dgate=dgate, datt_in=datt_in,
              dy_raw=dy_raw, dr_p=dr_p, dv_p=dv_p, dkm_p=dkm_p, dgg=dgg, d_ada_late=(d_gt1, d_sh2, d_sc2, d_gt2),
              recv_early=recv_early)
    return st


def _chip_parts(grads, recv, names, core):
    return [_half_sum(lambda a, b: a + b, [g], [r], False, BF16, core, "reduce_add2_" + n)
            for g, r, n in zip(grads, recv, names)]


def _step_after_scan(st, scan_grads, core):
    x, br, gw, h1, zs = st["x"], st["br"], st["gw"], st["h1"], st["zs"]
    dr_s, dlw, dkm_s, dv_s, daa, dbb = scan_grads
    pre_cots = [[(st["dr_p"], D, 0), (dr_s, D, 0)], [(dlw, D, 0)], [(st["dkm_p"], D, 0), (dkm_s, D, 0)],
                [(st["dv_p"], D, 0), (dv_s, D, 0)], [(daa, D, 0)], [(dbb, D, 0)], [(st["dgg"], D, 0)]]
    dz, dmu, gw["w0"], gw["a0"], gw["k_k"], gw["k_a"], dwl = _rwpre_shift_bwd(
        zs, st["z"], st["mu"], st["rwpre_c"], pre_cots, br=128)
    gw["w2"], gw["a2"] = _cols_split(dwl[0:64, 0:D]), _cols_split(dwl[64:128, D:2 * D])
    gw["g2"] = _cols_split(dwl[128:288, 2 * D:3 * D])
    gw["mu_shift"] = dmu[:, :N_RW]
    datt_in, dgate = st["datt_in"], st["dgate"]
    dw_in = [(_mm(datt_in, h1, ta=True, out_dtype=BF16, name="mm_dw_att"), N_ATT),
             (_mm(dz, h1, ta=True, out_dtype=BF16, name="mm_dw_rw"), N_RW),
             (_mm(dgate, h1, ta=True, out_dtype=BF16, name="mm_dw_gate"), N_GATE)]
    slabs = []
    for j in range(4):
        slabs += [_row_window(dw_in, j * W_IN_SHARD, (j + 1) * W_IN_SHARD), jnp.zeros((W_IN_PAD - W_IN_SHARD, D), BF16)]
    gw["w_in"] = jnp.concatenate(slabs, axis=0).reshape(4, W_IN_PAD, D)
    late = [gw[n] for n in _DONE_LATE]
    parts = _chip_parts(late, _run_comm(_SiblingHalves(late), "reduce_sib_late"), _DONE_LATE, core)
    dh1, slots_late = _mm_sum([(datt_in, st["w_att"]), (dz, st["w_rw"]), (dgate, st["w_gate"])],
                              comm=_ScatterToChips(parts), name="mm_dh1")
    grad_x, gw["norm1_w"], d_sc1, d_sh1 = _rows_bwd(
        _f_pre, [(x, D, 0)], st["pre1_c"], [[(dh1, D, 0)], [(st["dxa"], D, 0)]], wrt_rows=[0], wrt_consts=[0, 1, 2],
        drow_dtypes=[F32], name="pre1_bwd", br=2 * br)
    d_gt1, d_sh2, d_sc2, d_gt2 = st["d_ada_late"]
    return st["loss"], grad_x, (d_sh1, d_sc1, d_gt1, d_sh2, d_sc2, d_gt2), gw, slots_late


_SMALL = ("b_ada", "norm1_w", "b_gate", "mu_shift", "w0", "a0", "k_k", "k_a", "r_k", "lnx_w", "lnx_b", "norm2_w",
          "conv_b", "norm_f_w")
_NAMES = ("w_ada", "b_ada", "norm1_w", "w_in", "b_gate", "mu_shift", "w0", "w2", "a0", "a2", "g2", "k_k", "k_a", "r_k",
          "lnx_w", "lnx_b", "w_att_out", "w_rwkv_out", "w_o", "norm2_w", "w_up", "conv_w", "conv_b", "w_down",
          "norm_f_w")


def kernel(x, c, w_ada, b_ada, norm1_w, w_in, b_gate, mu_shift, w0, w2, a0, a2, g2, k_k, k_a, r_k, lnx_w, lnx_b, w_att_out, w_rwkv_out, w_o, norm2_w, w_up, conv_w, conv_b, w_down, norm_f_w, loss_target, m_w_ada, m_b_ada, m_norm1_w, m_w_in, m_b_gate, m_mu_shift, m_w0, m_w2, m_a0, m_a2, m_g2, m_k_k, m_k_a, m_r_k, m_lnx_w, m_lnx_b, m_w_att_out, m_w_rwkv_out, m_w_o, m_norm2_w, m_w_up, m_conv_w, m_conv_b, m_w_down, m_norm_f_w, v_w_ada, v_b_ada, v_norm1_w, v_w_in, v_b_gate, v_mu_shift, v_w0, v_w2, v_a0, v_a2, v_g2, v_k_k, v_k_a, v_r_k, v_lnx_w, v_lnx_b, v_w_att_out, v_w_rwkv_out, v_w_o, v_norm2_w, v_w_up, v_conv_w, v_conv_b, v_w_down, v_norm_f_w):
    args = dict(locals())
    p, pm, pv = {}, {}, {}
    for name in _NAMES:
        for dst, key in ((p, name), (pm, "m_" + name), (pv, "v_" + name)):
            t = args[key]
            if name == "w_in":
                dst[name] = jnp.swapaxes(t, 1, 2)[0]
            else:
                dst[name] = t.reshape(1, -1) if name in ("r_k", "norm_f_w") else t.reshape(t.shape[-2], t.shape[-1])
    xi, yi, ci = _me()
    chip = 2 * xi + yi
    dev = 4 * xi + 2 * yi + ci
    x2, tgt = x[0], loss_target[0]

    n_cw = 3 * (2 * D_FF // 4)
    vec = jnp.concatenate([c.reshape(-1), p["conv_w"].reshape(-1), jnp.zeros((8 * D - D - n_cw,), F32)]).reshape(8, D)
    g0 = _allgather8(vec, "gather_c").reshape(8, 8 * D)
    c_all = g0[:, :D]
    conv_w_full = jnp.concatenate([g0[2 * j, D:D + n_cw].reshape(3, -1) for j in range(4)], axis=1)
    n_ada = 6 * D // 4
    b_ada_sh = lax.dynamic_slice(p["b_ada"], (0, chip * n_ada), (1, n_ada))
    ada_sh = _ada_fwd(c_all, p["w_ada"], b_ada_sh)
    ga = _allgather8(ada_sh, "gather_ada")
    ada_all = jnp.concatenate([ga[2 * j] for j in range(4)], axis=1)
    ada_row = lax.dynamic_slice(ada_all, (dev, 0), (1, 6 * D))
    ada = [ada_row[:, j * D:(j + 1) * D] for j in range(6)]

    big = [n for n, _ in _BIG]
    shard = {n: p[n].astype(BF16) for n in big}
    shard["w_in"] = jnp.pad(shard["w_in"], ((0, W_IN_PAD - W_IN_SHARD), (0, 0)))
    wts = dict(zip(_NEEDED_FIRST, _run_comm(_GatherWeights([shard[n] for n in _NEEDED_FIRST]), "gather_w")))
    for n in _SMALL:
        wts[n] = p[n]
    wts["conv_w"] = conv_w_full
    core = ci.reshape(1).astype(jnp.int32)

    st = _step_to_scan(x2, tgt, ada, wts)
    y_raw, s0s, inverses, late = _scan_fwd(st["zs"], st["lw"], st["km"], st["aa"], st["bb"],
                                           _GatherWeights([shard[n] for n in _NEEDED_LATER]))
    st = _step_between_scans(st, y_raw, dict(zip(_NEEDED_LATER, late)))
    early = _chip_parts([st["gw"][n] for n in _DONE_EARLY], st["recv_early"], _DONE_EARLY, core)
    scan_grads, slots_early = _scan_bwd(st["zs"], st["lw"], st["km"], st["aa"], st["bb"], s0s, inverses,
                                        st["dy_raw"], _ScatterToChips(early))
    loss_part, grad_x, d_ada, gw, slots_late = _step_after_scan(st, scan_grads, core)

    small = [jnp.concatenate(d_ada, axis=1)] + [gw[n] for n in _SMALL[1:]] + [gw["conv_w"], loss_part.reshape(1, 1)]
    sizes = [t.size for t in small]
    flat = jnp.concatenate([t.reshape(-1) for t in small])
    npad = (-flat.shape[0]) % (8 * D)
    srows = (flat.shape[0] + npad) // D
    flat = jnp.concatenate([flat, jnp.zeros((npad,), F32)]).reshape(srows, D)
    parts = _allgather8(flat, "gather_small")
    tot = _sum_lead(parts, "sum_small").reshape(-1)
    pieces, pos = [], 0
    for sz in sizes:
        pieces.append(tot[pos:pos + sz])
        pos += sz
    grads = {}
    for n, piece in zip(_SMALL, pieces[:len(_SMALL)]):
        grads[n] = piece.reshape(p[n].shape)
    conv_w_grad = pieces[len(_SMALL)].reshape(3, 2 * D_FF)
    grads["conv_w"] = lax.dynamic_slice(conv_w_grad, (0, chip * (n_cw // 3)), (3, n_cw // 3))
    loss = pieces[-1][0]
    d_ada_all = parts[:, :6].reshape(8, 6 * D)
    grads["w_ada"] = _ada_bwd(c_all.T, lax.dynamic_slice(d_ada_all, (0, chip * n_ada), (8, n_ada)))

    order = _DONE_EARLY + _DONE_LATE
    reds = [_half_sum(lambda t: t[0] + t[1] + t[2] + t[3], [], [t], True, F32, core, "reduce_add4_" + n)
            for n, t in zip(order, list(slots_early) + list(slots_late))]
    for n, g in zip(order, _reduce_finish(reds, "reduce_sib2")):
        grads[n] = g

    outs_g, outs_d, outs_m, outs_v = [], [], [], []
    grads["w_in"] = grads["w_in"][:W_IN_SHARD]
    for name in _NAMES:
        g = grads[name]
        d, m, v = _adamw(p[name], g, pm[name], pv[name], "adamw_" + name)
        shape = args[name].shape
        for outs, t in ((outs_g, g), (outs_d, d), (outs_m, m), (outs_v, v)):
            outs.append(jnp.swapaxes(t[None], 1, 2) if name == "w_in" else t.reshape(shape))
    return (loss, grad_x.reshape(x.shape), *outs_g, *outs_d, *outs_m, *outs_v)
```

```python
import functools
import math

import jax
import jax.numpy as jnp
from jax import lax
from jax.experimental import pallas as pl
from jax.experimental.pallas import tpu as pltpu

F32 = jnp.float32
BF16 = jnp.bfloat16
HI = lax.Precision.HIGHEST
MESH = pl.DeviceIdType.MESH

D = 1024
ATT_PATTERNS = ((128, 1), (512, 4), (2048, 16))
ATT_BLOCK = 128
ATT_WIDTH = 512
N_ATT = 3 * 3 * ATT_WIDTH
N_RW = 3 * D + 64 + 64 + 160
N_RWP = 3456
N_LORA = N_RWP - 3 * D
N_GATE = 2 * D
D_FF = 2816
RMS_EPS = 1e-6
GN_EPS = 64e-5
SCAN_CHUNK = 64
SCAN_PAIRS = 8
NEG = -1e30
VMEM_LIMIT = 48 * 1024 * 1024
HALO = 16

ADAM_LR, ADAM_B1, ADAM_B2, ADAM_EPS, ADAM_WD, ADAM_STEP = 0.001, 0.9, 0.999, 1e-08, 0.01, 10


def _pcall(body, **kw):
    return pl.pallas_call(body, **kw)


def _cparams(sem):
    return pltpu.CompilerParams(dimension_semantics=sem, vmem_limit_bytes=VMEM_LIMIT)


def _div(n, pref, mult):
    best = None
    d = mult
    while d <= min(n, pref):
        if n % d == 0:
            best = d
        d += mult
    return best if best else n


def _dg(a, b, ca, cb):
    return lax.dot_general(a.astype(BF16), b.astype(BF16), (((ca,), (cb,)), ((), ())), preferred_element_type=F32)


@jax.custom_vjp
def _nn(a, b):
    return _dg(a, b, 1, 0)


@jax.custom_vjp
def _nt(a, b):
    return _dg(a, b, 1, 1)


@jax.custom_vjp
def _tn(a, b):
    return _dg(a, b, 0, 0)


_nn.defvjp(lambda a, b: (_nn(a, b), (a, b)), lambda res, g: (_nt(g, res[1]), _tn(res[0], g)))
_nt.defvjp(lambda a, b: (_nt(a, b), (a, b)), lambda res, g: (_nn(g, res[1]), _tn(g, res[0])))
_tn.defvjp(lambda a, b: (_tn(a, b), (a, b)), lambda res, g: (_nt(res[1], g), _nn(res[0], g)))


def _bdg(a, b, ca, cb):
    return lax.dot_general(a.astype(BF16), b.astype(BF16), (((ca,), (cb,)), ((0,), (0,))), preferred_element_type=F32)


@jax.custom_vjp
def _bnn(a, b):
    return _bdg(a, b, 2, 1)


@jax.custom_vjp
def _bnt(a, b):
    return _bdg(a, b, 2, 2)


@jax.custom_vjp
def _btn(a, b):
    return _bdg(a, b, 1, 1)


_bnn.defvjp(lambda a, b: (_bnn(a, b), (a, b)), lambda res, g: (_bnt(g, res[1]), _btn(res[0], g)))
_bnt.defvjp(lambda a, b: (_bnt(a, b), (a, b)), lambda res, g: (_bnn(g, res[1]), _btn(g, res[0])))
_btn.defvjp(lambda a, b: (_btn(a, b), (a, b)), lambda res, g: (_bnt(res[1], g), _bnn(res[0], g)))


def _hsum_impl(x, e, et):
    eb, etb = e.astype(BF16), et.astype(BF16)
    s = jnp.dot(x.astype(BF16), eb, preferred_element_type=F32)
    return jnp.dot(s.astype(BF16), etb, preferred_element_type=F32)


@jax.custom_vjp
def _hsum(x, e, et):
    return _hsum_impl(x, e, et)


_hsum.defvjp(lambda x, e, et: (_hsum_impl(x, e, et), (e, et)),
             lambda res, g: (_hsum_impl(g, res[0], res[1]), jnp.zeros_like(res[0]), jnp.zeros_like(res[1])))


def _mm(a, b, *, ta=False, tb=False, out_dtype=F32, add=None, b_chip=False, out_chip=False, comm=None, name):
    riding = _NOTHING if comm is None else comm
    nc = riding.n
    if ta:
        kdim, m = a.shape
    else:
        m, kdim = a.shape
    if b_chip:
        n = b.shape[1] if tb else 4 * b.shape[2]
    else:
        n = b.shape[0] if tb else b.shape[1]
    tm, tn, tk = _div(m, 1536, 128), _div(n, 1536, 128), _div(kdim, 2048 if ta else 1408, 128)
    if b_chip and tb:
        tk = kdim // 4
    if (b_chip and not tb) or out_chip:
        tn = n // 4
    nk = kdim // tk
    ca, cb = (0 if ta else 1), (1 if tb else 0)

    nin = 2 if add is None else 3
    gi, gj = m // tm, n // tn

    def body(*refs):
        a_ref, b_ref = refs[0], refs[1]
        add_ref = None if add is None else refs[2]
        o_ref = refs[nin + nc]
        step = (pl.program_id(0) * gj + pl.program_id(1)) * nk + pl.program_id(2)
        before, after = _comm_phases(riding, refs[nin:nin + nc] + refs[nin + nc + 1:nin + 2 * nc + 1]
                                     + refs[nin + 2 * nc + 1 + (nk > 1):], gi * gj * nk, step)
        before()
        part = lax.dot_general(a_ref[...], b_ref[...], (((ca,), (cb,)), ((), ())), preferred_element_type=F32)

        def finish(r):
            if add_ref is not None:
                r = r + add_ref[...]
            o_ref[...] = r.astype(o_ref.dtype)

        if nk == 1:
            finish(part)
            after()
            return
        acc = refs[nin + 2 * nc + 1]
        k = pl.program_id(2)

        @pl.when(k == 0)
        def _():
            acc[...] = part

        @pl.when(k > 0)
        def _():
            acc[...] += part

        @pl.when(k == nk - 1)
        def _():
            finish(acc[...])

        after()

    a_spec = pl.BlockSpec((tk, tm), lambda i, j, k: (k, i)) if ta else pl.BlockSpec((tm, tk), lambda i, j, k: (i, k))
    if b_chip:
        b_spec = (pl.BlockSpec((None, tn, tk), lambda i, j, k: (k, j, 0)) if tb
                  else pl.BlockSpec((None, tk, tn), lambda i, j, k: (j, k, 0)))
    else:
        b_spec = pl.BlockSpec((tn, tk), lambda i, j, k: (j, k)) if tb else pl.BlockSpec((tk, tn), lambda i, j, k: (k, j))
    in_specs = [a_spec, b_spec]
    args = [a, b]
    if add is not None:
        in_specs.append(pl.BlockSpec((tm, tn), lambda i, j, k: (i, j)))
        args.append(add)
    if out_chip:
        out_spec = pl.BlockSpec((None, tm, tn), lambda i, j, k: (j, i, 0))
        out_shape = jax.ShapeDtypeStruct((4, m, tn), out_dtype)
    else:
        out_spec = pl.BlockSpec((tm, tn), lambda i, j, k: (i, j))
        out_shape = jax.ShapeDtypeStruct((m, n), out_dtype)
    res = _pcall(
        body, name=name, grid=(gi, gj, nk), in_specs=in_specs + [_HBM] * nc, out_specs=[out_spec] + [_HBM] * nc,
        out_shape=[out_shape] + riding.out_shape,
        scratch_shapes=([] if nk == 1 else [pltpu.VMEM((tm, tn), F32)]) + riding.sems,
        compiler_params=_cparams(("arbitrary",) * 3 if nc else ("parallel", "parallel", "arbitrary")),
    )(*args, *riding.ins)
    return res[0] if comm is None else (res[0], res[1:])


def _mm_sum(pairs, *, comm, name):
    m, n = pairs[0][0].shape[0], pairs[0][1].shape[1]
    tm, tn = _div(m, 1024, 128), _div(n, 1024, 128)
    tks = [_div(a.shape[1], 1408, 128) for a, _ in pairs]
    nks = [a.shape[1] // tk for (a, _), tk in zip(pairs, tks)]
    offs = [sum(nks[:p]) for p in range(len(pairs))]
    total, npair, nc = sum(nks), len(pairs), comm.n
    gi, gj = m // tm, n // tn

    def body(*refs):
        o_ref, acc = refs[2 * npair + nc], refs[2 * npair + 2 * nc + 1]
        k = pl.program_id(2)
        step = (pl.program_id(0) * gj + pl.program_id(1)) * total + k
        before, after = _comm_phases(comm, refs[2 * npair:2 * npair + nc]
                                     + refs[2 * npair + nc + 1:2 * npair + 2 * nc + 1]
                                     + refs[2 * npair + 2 * nc + 2:], gi * gj * total, step)
        before()
        for p in range(npair):
            def partial_product(p=p):
                part = jnp.dot(refs[2 * p][...], refs[2 * p + 1][...], preferred_element_type=F32)
                if p == 0:
                    @pl.when(k == 0)
                    def _():
                        acc[...] = part

                    @pl.when(k > 0)
                    def _():
                        acc[...] += part
                else:
                    acc[...] += part

            pl.when(jnp.logical_and(k >= offs[p], k < offs[p] + nks[p]))(partial_product)

        @pl.when(k == total - 1)
        def _():
            o_ref[...] = acc[...].astype(o_ref.dtype)

        after()

    def specs(tk, off, nk):
        def kb(k):
            return jnp.clip(k - off, 0, nk - 1)
        return [pl.BlockSpec((tm, tk), lambda i, j, k: (i, kb(k))), pl.BlockSpec((tk, tn), lambda i, j, k: (kb(k), j))]

    in_specs, args = [], []
    for (a, b), tk, off, nk in zip(pairs, tks, offs, nks):
        in_specs += specs(tk, off, nk)
        args += [a, b]
    res = _pcall(
        body, name=name, grid=(gi, gj, total), in_specs=in_specs + [_HBM] * nc,
        out_specs=[pl.BlockSpec((tm, tn), lambda i, j, k: (i, j))] + [_HBM] * nc,
        out_shape=[jax.ShapeDtypeStruct((m, n), BF16)] + comm.out_shape,
        scratch_shapes=[pltpu.VMEM((tm, tn), F32)] + comm.sems,
        compiler_params=_cparams(("arbitrary",) * 3),
    )(*args, *comm.ins)
    return res[0], res[1:]


def _row_spec(br, w, cb):
    return pl.BlockSpec((br, w), lambda i: (i, cb))


def _const_spec(shape):
    return pl.BlockSpec(shape, lambda i: (0,) * len(shape))


def _rows_fwd(fn, rows, consts, outs, *, name, br, acc_shape=None, halo=None):
    s = rows[0][0].shape[0]
    nr, nc = len(rows), len(consts)
    kept = [k for k, o in enumerate(outs) if o is not None]

    def body(*refs):
        xs = [r[...].astype(F32) for r in refs[:nr]]
        cs = [c[...] for c in refs[nr:nr + nc]]
        if halo is not None:
            cs.append(jnp.where(pl.program_id(0) == 0, 0.0, refs[nr + nc][...].astype(F32)))
        res = fn(*xs, *cs)
        orefs = refs[nr + nc + (halo is not None):]
        for j, k in enumerate(kept):
            orefs[j][...] = res[k].astype(orefs[j].dtype)
        if acc_shape is not None:
            acc_ref = orefs[len(kept)]

            @pl.when(pl.program_id(0) == 0)
            def _():
                acc_ref[...] = jnp.zeros_like(acc_ref)

            acc_ref[...] += res[len(outs)]

    in_specs = [_row_spec(br, w, cb) for (_, w, cb) in rows] + [_const_spec(c.shape) for c in consts]
    args = [r[0] for r in rows] + list(consts)
    if halo is not None:
        harr, hw, hcb = rows[halo]
        in_specs.append(pl.BlockSpec((HALO, hw), lambda i: (jnp.maximum(i * (br // HALO) - 1, 0), hcb)))
        args.append(harr)
    out_specs = [_row_spec(br, outs[k][0], 0) for k in kept]
    out_shape = [jax.ShapeDtypeStruct((s, outs[k][0]), outs[k][1]) for k in kept]
    if acc_shape is not None:
        out_specs.append(_const_spec(acc_shape))
        out_shape.append(jax.ShapeDtypeStruct(acc_shape, F32))
    return _pcall(
        body, name=name, grid=(pl.cdiv(s, br),), in_specs=in_specs, out_specs=out_specs, out_shape=out_shape,
        compiler_params=_cparams(("arbitrary",)),
    )(*args)


def _rows_bwd(fn, rows, consts, cots, *, wrt_rows, wrt_consts, drow_dtypes, name, br, unit_cot=False, comm=None):
    comm = _NOTHING if comm is None else comm
    ncomm = comm.n
    nout = len(wrt_rows) + len(wrt_consts)
    s = rows[0][0].shape[0]
    nr, nc = len(rows), len(consts)
    flat_cots = [c for lst in cots for c in lst]
    ncot = len(flat_cots)

    def body(*refs):
        xs = [r[...].astype(F32) for r in refs[:nr]]
        cs = [c[...] for c in refs[nr:nr + nc]]
        cvals = [c[...].astype(F32) for c in refs[nr + nc:nr + nc + ncot]]
        orefs = refs[nr + nc + ncot + ncomm:]
        before, after = _comm_phases(comm, refs[nr + nc + ncot:nr + nc + ncot + ncomm] + orefs[nout:], s // br)
        before()

        def g(*d):
            xs2, cs2 = list(xs), list(cs)
            for j, k in enumerate(wrt_rows):
                xs2[k] = d[j]
            for j, k in enumerate(wrt_consts):
                cs2[k] = d[len(wrt_rows) + j]
            return tuple(fn(*xs2, *cs2))

        prim = [xs[k] for k in wrt_rows] + [cs[k] for k in wrt_consts]
        outs, vjp = jax.vjp(g, *prim)
        ct = []
        pos = 0
        for o, lst in zip(outs, cots):
            if unit_cot:
                ct.append(jnp.ones_like(o))
                continue
            acc = jnp.zeros_like(o)
            for _ in lst:
                acc = acc + cvals[pos]
                pos += 1
            ct.append(acc)
        grads = vjp(tuple(ct))
        for j in range(len(wrt_rows)):
            orefs[j][...] = grads[j].astype(orefs[j].dtype)

        @pl.when(pl.program_id(0) == 0)
        def _():
            for j in range(len(wrt_consts)):
                oref = orefs[len(wrt_rows) + j]
                oref[...] = jnp.zeros_like(oref)

        for j in range(len(wrt_consts)):
            orefs[len(wrt_rows) + j][...] += grads[len(wrt_rows) + j]
        after()

    in_specs = ([_row_spec(br, w, cb) for (_, w, cb) in rows] + [_const_spec(c.shape) for c in consts]
                + [_row_spec(br, w, cb) for (_, w, cb) in flat_cots] + [_HBM] * ncomm)
    out_specs = ([_row_spec(br, rows[k][1], 0) for k in wrt_rows] + [_const_spec(consts[k].shape) for k in wrt_consts]
                 + [_HBM] * ncomm)
    out_shape = ([jax.ShapeDtypeStruct((s, rows[k][1]), dt) for k, dt in zip(wrt_rows, drow_dtypes)]
                 + [jax.ShapeDtypeStruct(consts[k].shape, F32) for k in wrt_consts] + comm.out_shape)
    return _pcall(
        body, name=name, grid=(s // br,), in_specs=in_specs, out_specs=out_specs, out_shape=out_shape,
        scratch_shapes=comm.sems, compiler_params=_cparams(("arbitrary",)),
    )(*[r[0] for r in rows], *consts, *[c[0] for c in flat_cots], *comm.ins)


def _rms(x, w):
    return x * lax.rsqrt(jnp.mean(x * x, axis=-1, keepdims=True) + RMS_EPS) * w


def _f_pre(x, nw, sc, sh):
    return _rms(x, nw) * (1.0 + sc) + sh, x


def _f_pre2(x, o, gt, nw, sc, sh):
    x1 = x + gt * o
    return x1, _rms(x1, nw) * (1.0 + sc) + sh


def _f_fin(x1, f, tgt, gt, nfw):
    y = _rms(x1 + gt * f, nfw)
    return (0.5 * jnp.mean(jnp.square(y - tgt), axis=-1, keepdims=True),)


def _f_comb(o1, o2, o3, l1, l2, l3):
    m = lax.stop_gradient(jnp.maximum(jnp.maximum(l1, l2), l3))
    e1, e2, e3 = jnp.exp(l1 - m), jnp.exp(l2 - m), jnp.exp(l3 - m)
    return ((e1 * o1 + e2 * o2 + e3 * o3) / (e1 + e2 + e3),)


def _f_rwpre(zs, w0, a0, k_k, k_a, wl, e, et):
    r, k, v, zl = zs[:, 0:D], zs[:, D:2 * D], zs[:, 2 * D:3 * D], zs[:, 3 * D:N_RWP]
    lane = lax.broadcasted_iota(jnp.int32, zl.shape, 1)
    t = jnp.where(lane < 64, jnp.tanh(zl), jnp.where(lane < 128, zl, jnp.where(lane < 288, jax.nn.sigmoid(zl), 0.0)))
    lo = _nn(t[:, 0:128], wl[0:128, 0:2 * D])
    g = _nn(t[:, 128:N_LORA], wl[128:N_LORA, 2 * D:3 * D])
    lw = -math.exp(-0.5) * jax.nn.sigmoid(w0 + lo[:, 0:D])
    a = jax.nn.sigmoid(a0 + lo[:, D:2 * D])
    k_mod = k * (1.0 + (a - 1.0) * k_a)
    kk = k * k_k
    kk = kk / jnp.maximum(jnp.sqrt(_hsum(kk * kk, e, et)), 1e-12)
    return r, lw, k_mod, v, -kk, kk * a, g


def _f_rwpost(y, r, v, k_mod, g, lnx_w, lnx_b, r_k, e, et):
    mean = _hsum(y, e, et) * (1.0 / 64)
    yc = y - mean
    var = _hsum(yc * yc, e, et) * (1.0 / 64)
    yn = yc * lax.rsqrt(var + GN_EPS) * lnx_w + lnx_b
    bonus = _hsum(r * k_mod * r_k, e, et) * v
    return ((yn + bonus) * g,)


def _f_mix(gi, ya, yr, bg):
    gate = jax.nn.sigmoid(gi + bg)
    return (gate[:, 0:D] * ya + gate[:, D:2 * D] * yr,)


def _f_adamw(w, g, m, v):
    m = ADAM_B1 * m + (1.0 - ADAM_B1) * g
    v = ADAM_B2 * v + (1.0 - ADAM_B2) * jnp.square(g)
    m_hat = m / (1.0 - ADAM_B1 ** ADAM_STEP)
    v_hat = v / (1.0 - ADAM_B2 ** ADAM_STEP)
    return -ADAM_LR * (m_hat / (jnp.sqrt(v_hat) + ADAM_EPS) + ADAM_WD * w), m, v


def _down(x, k):
    row = lax.broadcasted_iota(jnp.int32, x.shape, 0)
    return jnp.where(row < k, 0.0, pltpu.roll(x, k, 0))


def _up(x, k):
    n = x.shape[0]
    row = lax.broadcasted_iota(jnp.int32, x.shape, 0)
    return jnp.where(row >= n - k, 0.0, pltpu.roll(x, n - k, 0))


def _col_spec(s, w, off=0):
    return pl.BlockSpec((s, w), lambda j: (0, j + off))


def _rwpre_shift_bwd(zs, z, mu, consts, cots, *, br):
    s, w = zs.shape
    n = s // br
    flat = [c for lst in cots for c in lst]
    nc, ncot, nwrt = len(consts), len(flat), 5

    def this(i):
        return jnp.minimum(i, n - 1)

    def last(i):
        return jnp.maximum(i - 1, 0)

    def body(*refs):
        zs_ref, z_ref, zh_ref, mu_ref = refs[:4]
        c_refs, cot_refs = refs[4:4 + nc], refs[4 + nc:4 + nc + ncot]
        dz_ref, dmu_ref = refs[4 + nc + ncot:6 + nc + ncot]
        dc_refs = refs[6 + nc + ncot:6 + nc + ncot + nwrt]
        kept = refs[-1]
        i = pl.program_id(0)

        @pl.when(i == 0)
        def _():
            dmu_ref[...] = jnp.zeros_like(dmu_ref)
            for ref in dc_refs:
                ref[...] = jnp.zeros_like(ref)

        cs = [c[...] for c in c_refs]

        def g(zz, *d):
            return tuple(_f_rwpre(zz, *d, *cs[nwrt:]))

        outs, vjp = jax.vjp(g, zs_ref[...], *cs[:nwrt])
        cts, pos = [], 0
        for o, lst in zip(outs, cots):
            acc = jnp.zeros_like(o)
            for _ in lst:
                acc = acc + cot_refs[pos][...].astype(F32)
                pos += 1
            cts.append(acc)
        grads = vjp(tuple(cts))
        dzs_new = grads[0]

        @pl.when(i < n)
        def _():
            for ref, gr in zip(dc_refs, grads[1:]):
                ref[...] += gr

        @pl.when(i > 0)
        def _():
            d, m = kept[...], mu_ref[...]
            row = lax.broadcasted_iota(jnp.int32, d.shape, 0)
            head = jnp.sum(jnp.where(row == 0, dzs_new, 0.0), axis=0, keepdims=True)
            head = jnp.where(i < n, head, 0.0)
            dm = d * m
            after = jnp.where(row == br - 1, head * m, pltpu.roll(dm, br - 1, 0))
            dz_ref[...] = (d - dm + after).astype(dz_ref.dtype)
            zz, halo = z_ref[...].astype(F32), zh_ref[...].astype(F32)
            tail = jnp.sum(jnp.where(lax.broadcasted_iota(jnp.int32, halo.shape, 0) == HALO - 1, halo, 0.0), axis=0,
                           keepdims=True)
            before = jnp.where(row == 0, jnp.where(i > 1, tail, 0.0), pltpu.roll(zz, 1, 0))
            dmu_ref[...] += jnp.sum(d * (before - zz), axis=0, keepdims=True)

        kept[...] = dzs_new

    in_specs = ([pl.BlockSpec((br, w), lambda i: (this(i), 0)), pl.BlockSpec((br, w), lambda i: (last(i), 0)),
                 pl.BlockSpec((HALO, w), lambda i: (jnp.maximum(last(i) * (br // HALO) - 1, 0), 0)),
                 _const_spec(mu.shape)] + [_const_spec(c.shape) for c in consts]
                + [pl.BlockSpec((br, cw), lambda i, cb=cb: (this(i), cb)) for (_, cw, cb) in flat])
    out_specs = ([pl.BlockSpec((br, w), lambda i: (last(i), 0)), _const_spec(mu.shape)]
                 + [_const_spec(consts[k].shape) for k in range(nwrt)])
    out_shape = ([jax.ShapeDtypeStruct((s, w), BF16), jax.ShapeDtypeStruct(mu.shape, F32)]
                 + [jax.ShapeDtypeStruct(consts[k].shape, F32) for k in range(nwrt)])
    return _pcall(
        body, name="rwpre_shift_bwd", grid=(n + 1,), in_specs=in_specs, out_specs=out_specs, out_shape=out_shape,
        scratch_shapes=[pltpu.VMEM((br, w), F32)], compiler_params=_cparams(("arbitrary",)),
    )(zs, z, z, mu, *consts, *[c[0] for c in flat])


def _conv3(x, w_ref, b_ref):
    return b_ref[...] + w_ref[0:1, :] * _down(x, 2) + w_ref[1:2, :] * _down(x, 1) + w_ref[2:3, :] * x


def _conv_fwd(u, cw, cb):
    s = u.shape[0]
    nb = D_FF // 128

    def body(ug_ref, uv_ref, wg_ref, wv_ref, bg_ref, bv_ref, o_ref):
        gate = _conv3(ug_ref[...], wg_ref, bg_ref)
        val = _conv3(uv_ref[...], wv_ref, bv_ref)
        o_ref[...] = (gate * jax.nn.sigmoid(gate) * val).astype(o_ref.dtype)

    return _pcall(
        body, name="conv_fwd", grid=(nb,),
        in_specs=[_col_spec(s, 128), _col_spec(s, 128, nb), _col_spec(3, 128), _col_spec(3, 128, nb),
                  _col_spec(1, 128), _col_spec(1, 128, nb)],
        out_specs=_col_spec(s, 128), out_shape=jax.ShapeDtypeStruct((s, D_FF), BF16),
        compiler_params=_cparams(("parallel",)),
    )(u, u, cw, cw, cb, cb)


def _conv_bwd(u, cw, cb, dact):
    s = u.shape[0]
    nb = D_FF // 128

    def half(x, d, w_ref, du_ref, dw_ref, db_ref):
        x1, x2 = _down(x, 1), _down(x, 2)
        du_ref[...] = (w_ref[2:3, :] * d + w_ref[1:2, :] * _up(d, 1) + w_ref[0:1, :] * _up(d, 2)).astype(du_ref.dtype)
        dw_ref[0:1, :] = jnp.sum(d * x2, axis=0, keepdims=True)
        dw_ref[1:2, :] = jnp.sum(d * x1, axis=0, keepdims=True)
        dw_ref[2:3, :] = jnp.sum(d * x, axis=0, keepdims=True)
        db_ref[...] = jnp.sum(d, axis=0, keepdims=True)

    def body(ug_ref, uv_ref, wg_ref, wv_ref, bg_ref, bv_ref, da_ref,
             dug_ref, duv_ref, dwg_ref, dwv_ref, dbg_ref, dbv_ref):
        ug, uv, da = ug_ref[...], uv_ref[...], da_ref[...]
        gate = _conv3(ug, wg_ref, bg_ref)
        val = _conv3(uv, wv_ref, bv_ref)
        sg = jax.nn.sigmoid(gate)
        dgate = da * val * sg * (1.0 + gate * (1.0 - sg))
        dval = da * gate * sg
        half(ug, dgate, wg_ref, dug_ref, dwg_ref, dbg_ref)
        half(uv, dval, wv_ref, duv_ref, dwv_ref, dbv_ref)

    dug, duv, dwg, dwv, dbg, dbv = _pcall(
        body, name="conv_bwd", grid=(nb,),
        in_specs=[_col_spec(s, 128), _col_spec(s, 128, nb), _col_spec(3, 128), _col_spec(3, 128, nb),
                  _col_spec(1, 128), _col_spec(1, 128, nb), _col_spec(s, 128)],
        out_specs=[_col_spec(s, 128), _col_spec(s, 128), _col_spec(3, 128), _col_spec(3, 128),
                   _col_spec(1, 128), _col_spec(1, 128)],
        out_shape=[jax.ShapeDtypeStruct((s, D_FF), BF16), jax.ShapeDtypeStruct((s, D_FF), BF16),
                   jax.ShapeDtypeStruct((3, D_FF), F32), jax.ShapeDtypeStruct((3, D_FF), F32),
                   jax.ShapeDtypeStruct((1, D_FF), F32), jax.ShapeDtypeStruct((1, D_FF), F32)],
        compiler_params=_cparams(("parallel",)),
    )(u, u, cw, cw, cb, cb, dact)
    return (jnp.concatenate([dug, duv], axis=1), jnp.concatenate([dwg, dwv], axis=1),
            jnp.concatenate([dbg, dbv], axis=1))


ATT_BATCH = 4


def _att_batch(q, kp, kc, vp, vc, first):
    ma = lax.broadcasted_iota(jnp.int32, (1, ATT_BLOCK, 128), 2) < 64

    def diag(x):
        return jnp.concatenate([jnp.where(ma, x, 0.0), jnp.where(ma, 0.0, x)], axis=1)

    qi = lax.broadcasted_iota(jnp.int32, (1, ATT_BLOCK, 2 * ATT_BLOCK), 1)
    kj = lax.broadcasted_iota(jnp.int32, (1, ATT_BLOCK, 2 * ATT_BLOCK), 2) & (ATT_BLOCK - 1)
    okp = kj >= qi + jnp.where(first, 2 * ATT_BLOCK, 0)
    okc = kj <= qi
    sp = jnp.where(okp, _bnt(q, diag(kp)) * 0.125, NEG)
    sc = jnp.where(okc, _bnt(q, diag(kc)) * 0.125, NEG)

    def per_head(fn, x):
        return fn(x[..., :ATT_BLOCK]), fn(x[..., ATT_BLOCK:])

    def spread(ab):
        return jnp.concatenate([jnp.broadcast_to(t, t.shape[:2] + (ATT_BLOCK,)) for t in ab], axis=-1)

    row_max = functools.partial(jnp.max, axis=-1, keepdims=True)
    row_sum = functools.partial(jnp.sum, axis=-1, keepdims=True)
    m = [lax.stop_gradient(jnp.maximum(a, b)) for a, b in zip(per_head(row_max, sp), per_head(row_max, sc))]
    pp, pc = jnp.exp(sp - spread(m)), jnp.exp(sc - spread(m))
    den = [a + b for a, b in zip(per_head(row_sum, pp), per_head(row_sum, pc))]
    num = _bnn(pp, diag(vp)) + _bnn(pc, diag(vc))
    out = num / jnp.where(ma, den[0], den[1])
    lse = jnp.where(ma, m[0] + jnp.log(den[0]), m[1] + jnp.log(den[1]))
    return out, jnp.broadcast_to(lse, out.shape)


def _att_pairs_per_step(dil):
    return 4 if dil == 1 else 1


def _att_residues(dil):
    return min(dil, ATT_BATCH // _att_pairs_per_step(dil))


def _att_specs(g, dil):
    rows, pp = ATT_BLOCK * dil, _att_pairs_per_step(dil)

    def cur(slot):
        return pl.BlockSpec((rows, 128 * pp), lambda n, p: (n, (g * 3 + slot) * (4 // pp) + p))

    def prev(slot):
        return pl.BlockSpec((rows, 128 * pp), lambda n, p: (jnp.maximum(n - 1, 0), (g * 3 + slot) * (4 // pp) + p))

    return [cur(0), prev(1), cur(1), prev(2), cur(2)]


def _att_out_spec(dil):
    return pl.BlockSpec((ATT_BLOCK * dil, 128 * _att_pairs_per_step(dil)), lambda n, p: (n, p))


def _att_grid(s, dil):
    return (s // (ATT_BLOCK * dil), 4 // _att_pairs_per_step(dil))


def _att_windows(i, dil):
    res = _att_residues(dil)

    def rows(r):
        return pl.ds(i * res + r, ATT_BLOCK, stride=dil) if dil > 1 else pl.ds(0, ATT_BLOCK)

    return [(rows(r), pl.ds(128 * j, 128)) for j in range(_att_pairs_per_step(dil)) for r in range(res)]


def _att_fwd(att_in, g, dil):
    s = att_in.shape[0]

    def body(q_ref, kp_ref, kc_ref, vp_ref, vc_ref, o_ref, l_ref):
        first = pl.program_id(0) == 0

        def one(i, carry):
            win = _att_windows(i, dil)
            vals = [jnp.stack([ref[w] for w in win]) for ref in (q_ref, kp_ref, kc_ref, vp_ref, vc_ref)]
            o, l = _att_batch(*vals, first)
            for j, w in enumerate(win):
                o_ref[w] = o[j]
                l_ref[w] = l[j]
            return carry

        lax.fori_loop(0, dil // _att_residues(dil), one, 0)

    return _pcall(
        body, name=f"att_fwd{g}", grid=_att_grid(s, dil), in_specs=_att_specs(g, dil),
        out_specs=[_att_out_spec(dil)] * 2, out_shape=[jax.ShapeDtypeStruct((s, ATT_WIDTH), F32)] * 2,
        compiler_params=_cparams(("parallel", "parallel")),
    )(att_in, att_in, att_in, att_in, att_in)


def _att_bwd(att_in, g, dil, do, dl, acc):
    s = att_in.shape[0]
    rows, pp = ATT_BLOCK * dil, _att_pairs_per_step(dil)
    nb, npair, wid = s // rows, 4 // pp, 128 * pp

    def body(q_ref, kp_ref, kc_ref, vp_ref, vc_ref, do_ref, dl_ref, *rest):
        o_ref, lag_q, lag_k, lag_v = rest[-9:-5]
        stage = rest[-5:]
        n, p = pl.program_id(0), pl.program_id(1)
        first = n == 0

        @pl.when(n < nb)
        def _():
            def one(i, carry):
                win = _att_windows(i, dil)
                vals = [jnp.stack([ref[w] for w in win]) for ref in (q_ref, kp_ref, kc_ref, vp_ref, vc_ref)]
                _, vjp = jax.vjp(lambda *a: _att_batch(*a, first), *vals)
                grads = vjp((jnp.stack([do_ref[w] for w in win]), jnp.stack([dl_ref[w] for w in win])))
                for ref, gr in zip(stage, grads):
                    for j, w in enumerate(win):
                        ref[w] = gr[j]
                return carry

            lax.fori_loop(0, dil // _att_residues(dil), one, 0)

        live = n < nb
        for pj in range(npair):
            @pl.when((n > 0) & (p == pj))
            def _(pj=pj):
                c = pj * wid
                o_ref[:, c:c + wid] = lag_q[pj]
                o_ref[:, ATT_WIDTH + c:ATT_WIDTH + c + wid] = (
                    lag_k[pj] + jnp.where(live, stage[1][...], 0.0)).astype(BF16)
                o_ref[:, 2 * ATT_WIDTH + c:2 * ATT_WIDTH + c + wid] = (
                    lag_v[pj] + jnp.where(live, stage[3][...], 0.0)).astype(BF16)

        @pl.when(live)
        def _():
            lag_q[p] = stage[0][...].astype(BF16)
            lag_k[p] = stage[2][...]
            lag_v[p] = stage[4][...]

    def col(slot, n, p):
        return (g * 3 + slot) * npair + jnp.where(n < nb, p, npair - 1)

    def cur(slot):
        return pl.BlockSpec((rows, wid), lambda n, p: (jnp.minimum(n, nb - 1), col(slot, n, p)))

    def prev(slot):
        return pl.BlockSpec((rows, wid), lambda n, p: (jnp.maximum(jnp.minimum(n, nb - 1) - 1, 0), col(slot, n, p)))

    cot = pl.BlockSpec((rows, wid), lambda n, p: (jnp.minimum(n, nb - 1), jnp.where(n < nb, p, npair - 1)))
    carried = [] if acc is None else [acc]
    return _pcall(
        body, name=f"att_bwd{g}", grid=(nb + 1, npair),
        in_specs=[cur(0), prev(1), cur(1), prev(2), cur(2), cot, cot] + [pl.BlockSpec(memory_space=pl.ANY)] * len(carried),
        out_specs=pl.BlockSpec((rows, 3 * ATT_WIDTH), lambda n, p: (jnp.maximum(n - 1, 0), g)),
        out_shape=jax.ShapeDtypeStruct((s, N_ATT), BF16), input_output_aliases={7: 0} if carried else {},
        scratch_shapes=([pltpu.VMEM((npair, rows, wid), BF16)] + [pltpu.VMEM((npair, rows, wid), F32)] * 2
                        + [pltpu.VMEM((rows, wid), F32)] * 5),
        compiler_params=_cparams(("arbitrary", "arbitrary")),
    )(att_in, att_in, att_in, att_in, att_in, do, dl, *carried)


def _cumsum_rows_impl(x):
    row = lax.broadcasted_iota(jnp.int32, x.shape, 0)
    shift = 1
    while shift < x.shape[0]:
        x = x + jnp.where(row >= shift, pltpu.roll(x, shift, 0), 0.0)
        shift *= 2
    return x


@jax.custom_vjp
def _cumsum_rows(x):
    return _cumsum_rows_impl(x)


_cumsum_rows.defvjp(lambda x: (_cumsum_rows_impl(x), None),
                    lambda _, g: (jnp.sum(g, axis=0, keepdims=True) - _cumsum_rows_impl(g) + g,))


def _unit_lower_inverse_impl(n):
    eye = (lax.broadcasted_iota(jnp.int32, (1,) + n.shape[1:], 1)
           == lax.broadcasted_iota(jnp.int32, (1,) + n.shape[1:], 2))
    t = jnp.where(eye, 1.0, 0.0) + n
    pw = n
    for _ in range(5):
        pw = _bnn(pw, pw)
        t = t + _bnn(t, pw)
    return t


@jax.custom_vjp
def _unit_lower_inverse(n):
    return _unit_lower_inverse_impl(n)


def _unit_lower_inverse_fwd(n):
    t = _unit_lower_inverse_impl(n)
    return t, t


_unit_lower_inverse.defvjp(_unit_lower_inverse_fwd, lambda t, g: (_bnt(_btn(t, g), t),))


@jax.custom_vjp
def _known_inverse(n, t):
    return t


_known_inverse.defvjp(lambda n, t: (t, t), lambda t, g: (_bnt(_btn(t, g), t), jnp.zeros_like(t)))


def _scan_chunk(r, lw, k, v, a, b, s0, inverse):
    c = SCAN_CHUNK
    p = s0.shape[0]
    cum = _cumsum_rows(lw)
    tot = jnp.sum(lw, axis=0, keepdims=True)
    ma = (lax.broadcasted_iota(jnp.int32, (c, 128 * p), 1) & 127) < 64

    def pairs(x):
        return jnp.concatenate([x[None, :, 128 * j:128 * (j + 1)] for j in range(p)], axis=0)

    def stack(x):
        return jnp.concatenate([pairs(jnp.where(ma, x, 0.0)), pairs(jnp.where(ma, 0.0, x))], axis=1)

    einv, eend = jnp.exp(-cum), jnp.exp(tot - cum)
    ra, aa = stack(r * jnp.exp(cum)), stack(a * jnp.exp(cum - lw))
    bi, ki, be, ke, vs = stack(b * einv), stack(k * einv), stack(b * eend), stack(k * eend), stack(v)
    r2 = lax.broadcasted_iota(jnp.int32, (1, 2 * c, 2 * c), 1)
    c2 = lax.broadcasted_iota(jnp.int32, (1, 2 * c, 2 * c), 2)
    same = (r2 >= c) == (c2 >= c)
    strict = jnp.logical_and(same, c2 < r2)
    incl = jnp.logical_and(same, c2 <= r2)
    s0 = jnp.where(same, s0, 0.0)
    prod = _bnt(jnp.concatenate([aa, ra], axis=1), jnp.concatenate([bi, ki], axis=1))
    a_ab = jnp.where(strict, prod[:, :2 * c, :2 * c], 0.0)
    a_ak = jnp.where(strict, prod[:, :2 * c, 2 * c:], 0.0)
    a_rb = jnp.where(incl, prod[:, 2 * c:, :2 * c], 0.0)
    a_rk = jnp.where(incl, prod[:, 2 * c:, 2 * c:], 0.0)
    t = inverse(a_ab)
    u = _bnn(t, _bnt(aa, s0) + _bnn(a_ak, vs))
    uv = jnp.concatenate([u, vs], axis=1)
    ys = _bnt(ra, s0) + _bnn(jnp.concatenate([a_rb, a_rk], axis=2), uv)
    s1 = s0 * pairs(jnp.exp(tot)) + _btn(uv, jnp.concatenate([be, ke], axis=1))
    y3 = ys[:, :c] + ys[:, c:]
    return (jnp.concatenate([y3[j] for j in range(p)], axis=1), s1), t


def _scan_specs(rev, n):
    def at(i):
        return n - 1 - i if rev else i

    def cm(cb):
        return pl.BlockSpec((SCAN_CHUNK, D), lambda i: (at(i), cb))

    return cm, pl.BlockSpec((1, SCAN_PAIRS, 128, 128), lambda i: (at(i), 0, 0, 0))


def _comm_phases(comm, refs, n, step=None):
    k = comm.n
    srcs, outs, sems = refs[:k], refs[k:2 * k], refs[2 * k:]
    i = pl.program_id(0) if step is None else step

    def before():
        @pl.when(i == 0)
        def _():
            comm.first(srcs, outs, sems)

    def after():
        if comm.mid is not None:
            @pl.when(i == (3 * n) // 4)
            def _():
                comm.mid(srcs, outs, sems)

        @pl.when(i == n - 1)
        def _():
            comm.last(srcs, outs, sems)

    return before, after


def _scan_fwd(zs, lw, km, aa, bb, comm):
    s = zs.shape[0]
    n = s // SCAN_CHUNK
    cm, st = _scan_specs(False, n)
    k = comm.n

    def body(*refs):
        r_ref, lw_ref, k_ref, v_ref, a_ref, b_ref = refs[:6]
        y_ref, s0_ref, t_ref = refs[6 + k:9 + k]
        state = refs[9 + 2 * k]
        before, after = _comm_phases(comm, refs[6:6 + k] + refs[9 + k:9 + 2 * k] + refs[10 + 2 * k:], n)
        before()

        @pl.when(pl.program_id(0) == 0)
        def _():
            state[...] = jnp.zeros_like(state)

        s0 = state[...]
        s0_ref[0] = s0
        (y, s1), t = _scan_chunk(*[ref[...] for ref in (r_ref, lw_ref, k_ref, v_ref, a_ref, b_ref)], s0,
                                 _unit_lower_inverse)
        y_ref[...] = y
        t_ref[0] = t.astype(BF16)
        state[...] = s1
        after()

    per_chunk = (n, SCAN_PAIRS, 128, 128)
    res = _pcall(
        body, name="scan_fwd", grid=(n,), in_specs=[cm(0), cm(0), cm(0), cm(2), cm(0), cm(0)] + [_HBM] * k,
        out_specs=[cm(0), st, st] + [_HBM] * k,
        out_shape=[jax.ShapeDtypeStruct((s, D), F32), jax.ShapeDtypeStruct(per_chunk, F32),
                   jax.ShapeDtypeStruct(per_chunk, BF16)] + comm.out_shape,
        scratch_shapes=[pltpu.VMEM((SCAN_PAIRS, 128, 128), F32)] + comm.sems,
        compiler_params=_cparams(("arbitrary",)),
    )(zs, lw, km, zs, aa, bb, *comm.ins)
    return res[0], res[1], res[2], res[3:]


def _scan_bwd(zs, lw, km, aa, bb, s0s, ts, dy, comm):
    s = zs.shape[0]
    n = s // SCAN_CHUNK
    cm, st = _scan_specs(True, n)
    k = comm.n

    def body(*refs):
        r_ref, lw_ref, k_ref, v_ref, a_ref, b_ref, s0_ref, t_ref, dy_ref = refs[:9]
        douts = refs[9 + k:15 + k]
        dstate = refs[15 + 2 * k]
        before, after = _comm_phases(comm, refs[9:9 + k] + refs[15 + k:15 + 2 * k] + refs[16 + 2 * k:], n)
        before()

        @pl.when(pl.program_id(0) == 0)
        def _():
            dstate[...] = jnp.zeros_like(dstate)

        t = t_ref[0].astype(F32)
        prim = [ref[...] for ref in (r_ref, lw_ref, k_ref, v_ref, a_ref, b_ref)] + [s0_ref[0]]
        _, vjp, _ = jax.vjp(lambda *p: _scan_chunk(*p, lambda nil: _known_inverse(nil, t)), *prim, has_aux=True)
        grads = vjp((dy_ref[...], dstate[...]))
        for ref, gr in zip(douts, grads[:6]):
            ref[...] = gr
        dstate[...] = grads[6]
        after()

    res = _pcall(
        body, name="scan_bwd", grid=(n,),
        in_specs=[cm(0), cm(0), cm(0), cm(2), cm(0), cm(0), st, st, cm(0)] + [_HBM] * k,
        out_specs=[cm(0)] * 6 + [_HBM] * k, out_shape=[jax.ShapeDtypeStruct((s, D), F32)] * 6 + comm.out_shape,
        scratch_shapes=[pltpu.VMEM((SCAN_PAIRS, 128, 128), F32)] + comm.sems,
        compiler_params=_cparams(("arbitrary",)),
    )(zs, lw, km, zs, aa, bb, s0s, ts, dy, *comm.ins)
    return res[:6], res[6:]


_HBM = pl.BlockSpec(memory_space=pltpu.HBM)


def _me():
    return lax.axis_index("x"), lax.axis_index("y"), lax.axis_index("c")


def _allgather8(src, name):
    def body(src_ref, out_ref, ssem, rsem, lsem):
        x, y, c = _me()
        me = 4 * x + 2 * y + c
        local = pltpu.make_async_copy(src_ref, out_ref.at[me], lsem)
        local.start()
        peers = []
        for k in range(1, 8):
            peers.append(((1 - x) if k & 4 else x, (1 - y) if k & 2 else y, (1 - c) if k & 1 else c))
        sends = []
        for k, peer in enumerate(peers):
            cp = pltpu.make_async_remote_copy(src_ref, out_ref.at[me], ssem.at[k], rsem.at[k], device_id=peer,
                                              device_id_type=MESH)
            cp.start()
            sends.append(cp)
        for k, (px, py, pc) in enumerate(peers):
            pltpu.make_async_remote_copy(src_ref, out_ref.at[4 * px + 2 * py + pc], ssem.at[k], rsem.at[k],
                                         device_id=(px, py, pc), device_id_type=MESH).wait_recv()
        for cp in sends:
            cp.wait_send()
        local.wait()

    return _pcall(
        body, name=name, in_specs=[_HBM], out_specs=_HBM, out_shape=jax.ShapeDtypeStruct((8,) + src.shape, src.dtype),
        scratch_shapes=[pltpu.SemaphoreType.DMA((7,)), pltpu.SemaphoreType.DMA((7,)), pltpu.SemaphoreType.DMA],
    )(src)


def _other_chips(x, y):
    return [(1 - x, y), (x, 1 - y), (1 - x, 1 - y)]


def _remote(src, dst, ssem, rsem, to):
    return pltpu.make_async_remote_copy(src, dst, ssem, rsem, device_id=to, device_id_type=MESH)


class _GatherWeights:
    def __init__(self, shards):
        self.ins = list(shards)
        n = self.n = len(shards)
        self.out_shape = [jax.ShapeDtypeStruct((4,) + t.shape, t.dtype) for t in shards]
        self.sems = [pltpu.SemaphoreType.DMA((6 * n,)), pltpu.SemaphoreType.DMA((6 * n,)),
                     pltpu.SemaphoreType.DMA((n,)), pltpu.SemaphoreType.DMA((n,))]

    def _copies(self, srcs, outs, sems):
        ssem, rsem, lsem, osem = sems
        x, y, c = _me()
        me = 2 * x + y
        own, ici, landed, passed, passed_in = [], [], [], [], []
        for a in range(self.n):
            h = self.ins[a].shape[0] // 2
            mine, other = pl.ds(c * h, h), pl.ds((1 - c) * h, h)
            own.append(_remote(srcs[a], outs[a].at[me], lsem.at[a], osem.at[a], (x, y, 1 - c)))
            for k, (px, py) in enumerate(_other_chips(x, y)):
                s1, r1, s2, r2 = ssem.at[6 * a + k], rsem.at[6 * a + k], ssem.at[6 * a + 3 + k], rsem.at[6 * a + 3 + k]
                got, got_sib = outs[a].at[2 * px + py, mine], outs[a].at[2 * px + py, other]
                ici.append(_remote(srcs[a].at[mine], outs[a].at[me, mine], s1, r1, (px, py, c)))
                landed.append(_remote(got, got, s1, r1, (px, py, c)))
                passed.append(_remote(got, got, s2, r2, (x, y, 1 - c)))
                passed_in.append(_remote(got_sib, got_sib, s2, r2, (x, y, 1 - c)))
        return own, ici, landed, passed, passed_in

    def first(self, srcs, outs, sems):
        own, ici, _, _, _ = self._copies(srcs, outs, sems)
        for cp in own + ici:
            cp.start()

    def mid(self, srcs, outs, sems):
        _, _, landed, passed, _ = self._copies(srcs, outs, sems)
        for arrived, onward in zip(landed, passed):
            arrived.wait_recv()
            onward.start()

    def last(self, srcs, outs, sems):
        own, ici, _, passed, passed_in = self._copies(srcs, outs, sems)
        for cp in passed_in:
            cp.wait_recv()
        for cp in ici + passed:
            cp.wait_send()
        for cp in own:
            cp.wait()


class _ScatterToChips:
    def __init__(self, parts):
        self.ins = list(parts)
        n = self.n = len(parts)
        self.out_shape = [jax.ShapeDtypeStruct(t.shape, t.dtype) for t in parts]
        self.sems = [pltpu.SemaphoreType.DMA((3 * n,)), pltpu.SemaphoreType.DMA((3 * n,)), pltpu.SemaphoreType.DMA((n,))]

    def _copies(self, srcs, outs, sems):
        ssem, rsem, lsem = sems
        x, y, c = _me()
        me = 2 * x + y
        own, out, landed = [], [], []
        for a in range(self.n):
            own.append(pltpu.make_async_copy(srcs[a].at[me], outs[a].at[me], lsem.at[a]))
            for k, (px, py) in enumerate(_other_chips(x, y)):
                dst = outs[a].at[2 * px + py]
                out.append(_remote(srcs[a].at[2 * px + py], outs[a].at[me], ssem.at[3 * a + k], rsem.at[3 * a + k],
                                   (px, py, c)))
                landed.append(_remote(dst, dst, ssem.at[3 * a + k], rsem.at[3 * a + k], (px, py, c)))
        return own, out, landed

    def first(self, srcs, outs, sems):
        own, out, _ = self._copies(srcs, outs, sems)
        for cp in own + out:
            cp.start()

    mid = None

    def last(self, srcs, outs, sems):
        own, out, landed = self._copies(srcs, outs, sems)
        for cp in landed:
            cp.wait_recv()
        for cp in own:
            cp.wait()
        for cp in out:
            cp.wait_send()


def _run_comm(comm, name):
    n = comm.n

    def body(*refs):
        srcs, outs, sems = refs[:n], refs[n:2 * n], refs[2 * n:]
        comm.first(srcs, outs, sems)
        if comm.mid is not None:
            comm.mid(srcs, outs, sems)
        comm.last(srcs, outs, sems)

    return _pcall(body, name=name, in_specs=[_HBM] * n, out_specs=[_HBM] * n, out_shape=comm.out_shape,
                  scratch_shapes=comm.sems)(*comm.ins)


class _NoComm:
    n, ins, out_shape, sems, mid = 0, [], [], [], None

    def first(self, srcs, outs, sems):
        pass

    def last(self, srcs, outs, sems):
        pass


_NOTHING = _NoComm()


class _SiblingHalves:
    mid = None

    def __init__(self, grads):
        self.ins = list(grads)
        n = self.n = len(grads)
        self.out_shape = [jax.ShapeDtypeStruct((4, t.shape[1] // 2, t.shape[2]), t.dtype) for t in grads]
        self.sems = [pltpu.SemaphoreType.DMA((n,)), pltpu.SemaphoreType.DMA((n,))]

    def _copies(self, srcs, outs, sems):
        ssem, rsem = sems
        x, y, c = _me()
        copies = []
        for a in range(self.n):
            h = self.ins[a].shape[1] // 2
            copies.append(_remote(srcs[a].at[:, pl.ds((1 - c) * h, h)], outs[a], ssem.at[a], rsem.at[a], (x, y, 1 - c)))
        return copies

    def first(self, srcs, outs, sems):
        for cp in self._copies(srcs, outs, sems):
            cp.start()

    def last(self, srcs, outs, sems):
        for cp in self._copies(srcs, outs, sems):
            cp.wait()


def _reduce_finish(reds, name):
    n = len(reds)

    def body(*refs):
        outs = refs[n:2 * n]
        ssem, rsem = refs[2 * n:]
        x, y, c = _me()
        copies = []
        for a in range(n):
            h = reds[a].shape[0] // 2
            mine = outs[a].at[pl.ds(c * h, h)]
            copies.append(_remote(mine, mine, ssem.at[a], rsem.at[a], (x, y, 1 - c)))
        for cp in copies:
            cp.start()
        for a in range(n):
            h = reds[a].shape[0] // 2
            dst = outs[a].at[pl.ds((1 - c) * h, h)]
            _remote(dst, dst, ssem.at[a], rsem.at[a], (x, y, 1 - c)).wait_recv()
        for cp in copies:
            cp.wait_send()

    return _pcall(
        body, name=name, in_specs=[_HBM] * n, out_specs=[_HBM] * n,
        out_shape=[jax.ShapeDtypeStruct(t.shape, t.dtype) for t in reds],
        input_output_aliases={a: a for a in range(n)},
        scratch_shapes=[pltpu.SemaphoreType.DMA((n,)), pltpu.SemaphoreType.DMA((n,))],
    )(*reds)


def _half_sum(fn, full, halves, out_full, out_dtype, core, name):
    p, h, c = (halves[0].shape if halves else (full[0].shape[0], full[0].shape[1] // 2, full[0].shape[2]))
    br = _div(h, max(16, (1 << 19) // (p * c)), 16)
    nb = h // br
    mine3 = pl.BlockSpec((p, br, c), lambda i, core_ref: (0, core_ref[0] * nb + i, 0))
    half3 = pl.BlockSpec((p, br, c), lambda i, core_ref: (0, i, 0))

    def body(core_ref, *refs):
        refs[-1][...] = fn(*[t[...].astype(F32) for t in refs[:-1]]).astype(out_dtype)

    if out_full:
        out_spec = pl.BlockSpec((br, c), lambda i, core_ref: (core_ref[0] * nb + i, 0))
        out_shape = jax.ShapeDtypeStruct((2 * h, c), out_dtype)
    else:
        out_spec, out_shape = half3, jax.ShapeDtypeStruct((p, h, c), out_dtype)
    return _pcall(
        body, name=name,
        grid_spec=pltpu.PrefetchScalarGridSpec(
            num_scalar_prefetch=1, grid=(nb,), in_specs=[mine3] * len(full) + [half3] * len(halves),
            out_specs=out_spec),
        out_shape=out_shape, compiler_params=_cparams(("parallel",)),
    )(core, *full, *halves)


def _ada_fwd(c_all, w, b):
    def body(c_ref, w_ref, b_ref, o_ref):
        o_ref[...] = jnp.dot(c_ref[...], w_ref[...], precision=HI, preferred_element_type=F32) + b_ref[...]

    return _pcall(body, name="ada_fwd", out_shape=jax.ShapeDtypeStruct((c_all.shape[0], w.shape[1]), F32),
                  compiler_params=pltpu.CompilerParams(vmem_limit_bytes=VMEM_LIMIT))(c_all, w, b)


def _ada_bwd(c_all_t, d):
    def body(c_ref, d_ref, o_ref):
        o_ref[...] = jnp.dot(c_ref[...], d_ref[...], precision=HI, preferred_element_type=F32)

    return _pcall(body, name="ada_bwd", out_shape=jax.ShapeDtypeStruct((c_all_t.shape[0], d.shape[1]), F32),
                  compiler_params=pltpu.CompilerParams(vmem_limit_bytes=VMEM_LIMIT))(c_all_t, d)


def _sum_lead(x, name):
    p, r, n = x.shape
    br = _div(r, 512, 8)

    def body(x_ref, o_ref):
        acc = x_ref[0]
        for j in range(1, p):
            acc = acc + x_ref[j]
        o_ref[...] = acc

    return _pcall(
        body, name=name, grid=(r // br,), in_specs=[pl.BlockSpec((p, br, n), lambda i: (0, i, 0))],
        out_specs=pl.BlockSpec((br, n), lambda i: (i, 0)), out_shape=jax.ShapeDtypeStruct((r, n), F32),
        compiler_params=_cparams(("parallel",)),
    )(x)


def _adamw(w, g, m, v, name):
    shape = w.shape
    cols = shape[-1]
    w2, g2, m2, v2 = [t.reshape(-1, cols) for t in (w, g, m, v)]
    rows = w2.shape[0]
    pref = max(8, (1 << 19) // cols // 8 * 8)
    br = _div(rows, pref, 8)
    if rows // br > 64:
        br = pref
    outs = _rows_fwd(_f_adamw, [(t, cols, 0) for t in (w2, g2, m2, v2)], [], [(cols, F32)] * 3, name=name, br=br)
    return [o.reshape(shape) for o in outs]


_BIG = (("w_in", 1), ("w_up", 1), ("w_down", 0), ("w_o", 0), ("w_rwkv_out", 0), ("w_att_out", 1), ("w2", 1), ("a2", 1),
        ("g2", 1))


_NEEDED_FIRST = ("w_in", "w_att_out", "w2", "a2", "g2")
_NEEDED_LATER = ("w_up", "w_down", "w_o", "w_rwkv_out")
_DONE_EARLY = ("w_up", "w_down", "w_o", "w_rwkv_out", "w_att_out")
_DONE_LATE = ("w_in", "w2", "a2", "g2")


def _cols_joined(t):
    return jnp.concatenate([t[j] for j in range(4)], axis=1)


def _cols_split(t):
    n = t.shape[1] // 4
    return jnp.stack([t[:, j * n:(j + 1) * n] for j in range(4)])


W_IN_SHARD = (N_ATT + N_RW + N_GATE) // 4
W_IN_PAD = 2560


def _row_window(parts, lo, hi):
    out, pos = [], 0
    for t, w in parts:
        a, b = max(lo, pos), min(hi, pos + w)
        if a < b:
            out.append(t[a - pos:b - pos])
        pos += w
    return out[0] if len(out) == 1 else jnp.concatenate(out, axis=0)


def _rows_joined(t):
    return t.reshape(4 * t.shape[1], t.shape[2])


def _rows_split(t):
    return t.reshape(4, t.shape[0] // 4, t.shape[1])


def _step_to_scan(x, tgt, ada, wts):
    sh1, sc1, gt1, sh2, sc2, gt2 = ada
    br = 256
    grp = lax.broadcasted_iota(jnp.int32, (D, 128), 0) // 64 == lax.broadcasted_iota(jnp.int32, (D, 128), 1)
    e = grp.astype(F32)
    et = e.T
    w_in = [(wts["w_in"][j], W_IN_SHARD) for j in range(4)]
    w_att = _row_window(w_in, 0, N_ATT)
    w_rw = jnp.concatenate([_row_window(w_in, N_ATT, N_ATT + N_RW), jnp.zeros((N_RWP - N_RW, D), BF16)], axis=0)
    w_gate = _row_window(w_in, N_ATT + N_RW, N_ATT + N_RW + N_GATE)
    mu = jnp.pad(wts["mu_shift"], ((0, 0), (0, N_RWP - N_RW)))
    wl = jnp.zeros((N_LORA, 3 * D), F32)
    wl = wl.at[0:64, 0:D].set(_cols_joined(wts["w2"]).astype(F32))
    wl = wl.at[64:128, D:2 * D].set(_cols_joined(wts["a2"]).astype(F32))
    wl = wl.at[128:288, 2 * D:3 * D].set(_cols_joined(wts["g2"]).astype(F32))
    pre1_c = [wts["norm1_w"], sc1, sh1]
    (h1,) = _rows_fwd(_f_pre, [(x, D, 0)], pre1_c, [(D, BF16), None], name="pre1_fwd", br=2 * br)
    att_in = _mm(h1, w_att, tb=True, name="mm_att_in")
    z = _mm(h1, w_rw, tb=True, out_dtype=BF16, name="mm_rw_in")
    gate_in = _mm(h1, w_gate, tb=True, out_dtype=BF16, name="mm_gate_in")
    att_o, att_l = [], []
    for g, (_, dil) in enumerate(ATT_PATTERNS):
        o, l = _att_fwd(att_in, g, dil)
        att_o.append(o)
        att_l.append(l)
    comb_rows = [(t, ATT_WIDTH, 0) for t in att_o + att_l]
    (att,) = _rows_fwd(_f_comb, comb_rows, [], [(ATT_WIDTH, BF16)], name="comb_fwd", br=2 * br)
    w_ao = _cols_joined(wts["w_att_out"])
    y_att = _mm(att, w_ao, out_dtype=BF16, name="mm_att_out")
    rwpre_c = [wts["w0"], wts["a0"], wts["k_k"], wts["k_a"], wl, e, et]

    def shift_and_rwpre(zz, *rest):
        consts, mu_row, before = rest[:-2], rest[-2], rest[-1]
        last = jnp.sum(jnp.where(lax.broadcasted_iota(jnp.int32, before.shape, 0) == HALO - 1, before, 0.0), axis=0,
                       keepdims=True)
        row = lax.broadcasted_iota(jnp.int32, zz.shape, 0)
        zprev = jnp.where(row == 0, last, pltpu.roll(zz, 1, 0))
        shifted = zz + (zprev - zz) * mu_row
        return (shifted,) + tuple(_f_rwpre(shifted, *consts))

    zs, lw, km, aa, bb, gg = _rows_fwd(
        shift_and_rwpre, [(z, N_RWP, 0)], rwpre_c + [mu],
        [(N_RWP, F32), None, (D, F32), (D, F32), None, (D, F32), (D, F32), (D, F32)], name="rwpre_fwd", br=br, halo=0)
    return dict(x=x, tgt=tgt, wts=wts, br=br, e=e, et=et, gt1=gt1, sc2=sc2, sh2=sh2, gt2=gt2, w_att=w_att, w_rw=w_rw,
                w_ao=w_ao,
                w_gate=w_gate, mu=mu, pre1_c=pre1_c, h1=h1, att_in=att_in, z=z, gate_in=gate_in, comb_rows=comb_rows,
                att=att, y_att=y_att, zs=zs, rwpre_c=rwpre_c, lw=lw, km=km, aa=aa, bb=bb, gg=gg)


def _step_between_scans(st, y_raw, late):
    x, tgt, wts, br, e, et = st["x"], st["tgt"], st["wts"], st["br"], st["e"], st["et"]
    zs, km, gg, gate_in, y_att, att = st["zs"], st["km"], st["gg"], st["gate_in"], st["y_att"], st["att"]
    comb_rows, att_in = st["comb_rows"], st["att_in"]
    gt1, sc2, sh2, gt2 = st["gt1"], st["sc2"], st["sh2"], st["gt2"]
    w_up, w_ao = late["w_up"], st["w_ao"]
    w_down, w_o, w_ro = _rows_joined(late["w_down"]), _rows_joined(late["w_o"]), _rows_joined(late["w_rwkv_out"])
    post_rows = [(y_raw, D, 0), (zs, D, 0), (zs, D, 2), (km, D, 0), (gg, D, 0)]
    post_c = [wts["lnx_w"], wts["lnx_b"], wts["r_k"], e, et]
    (rw_out,) = _rows_fwd(_f_rwpost, post_rows, post_c, [(D, BF16)], name="rwpost_fwd", br=br)
    y_rw = _mm(rw_out, w_ro, out_dtype=BF16, name="mm_rw_out")
    mix_rows = [(gate_in, N_GATE, 0), (y_att, D, 0), (y_rw, D, 0)]
    (mix,) = _rows_fwd(_f_mix, mix_rows, [wts["b_gate"]], [(D, BF16)], name="mix_fwd", br=2 * br)
    o = _mm(mix, w_o, out_dtype=BF16, name="mm_o")
    pre2_c = [gt1, wts["norm2_w"], sc2, sh2]
    x1, h2 = _rows_fwd(_f_pre2, [(x, D, 0), (o, D, 0)], pre2_c, [(D, F32), (D, BF16)], name="pre2_fwd", br=2 * br)
    u = _mm(h2, w_up, b_chip=True, name="mm_up")
    act = _conv_fwd(u, wts["conv_w"], wts["conv_b"])
    f = _mm(act, w_down, out_dtype=BF16, name="mm_down")
    fin_rows = [(x1, D, 0), (f, D, 0), (tgt, D, 0)]
    fin_c = [gt2, wts["norm_f_w"]]

    def fin_fwd(*a):
        (l,) = _f_fin(*a)
        return (jnp.broadcast_to(jnp.sum(l, axis=0, keepdims=True), (8, 128)),)

    (loss_acc,) = _rows_fwd(fin_fwd, fin_rows, fin_c, [], name="fin_fwd", br=2 * br, acc_shape=(8, 128))

    gw = {}
    dx1a, df, d_gt2, gw["norm_f_w"] = _rows_bwd(
        _f_fin, fin_rows, fin_c, [[]], wrt_rows=[0, 1], wrt_consts=[0, 1], drow_dtypes=[F32, BF16],
        name="fin_bwd", br=2 * br, unit_cot=True)
    dact = _mm(df, w_down, tb=True, name="mm_dact")
    gw["w_down"] = _rows_split(_mm(act, df, ta=True, out_dtype=BF16, name="mm_dw_down"))
    du, gw["conv_w"], gw["conv_b"] = _conv_bwd(u, wts["conv_w"], wts["conv_b"], dact)
    dh2 = _mm(du, w_up, tb=True, b_chip=True, out_dtype=BF16, name="mm_dh2")
    gw["w_up"] = _mm(h2, du, ta=True, out_chip=True, out_dtype=BF16, name="mm_dw_up")
    dxa, do, d_gt1, gw["norm2_w"], d_sc2, d_sh2 = _rows_bwd(
        _f_pre2, [(x, D, 0), (o, D, 0)], pre2_c, [[(dx1a, D, 0)], [(dh2, D, 0)]], wrt_rows=[0, 1],
        wrt_consts=[0, 1, 2, 3], drow_dtypes=[F32, BF16], name="pre2_bwd", br=2 * br)
    dmix = _mm(do, w_o, tb=True, out_dtype=BF16, name="mm_dmix")
    gw["w_o"] = _rows_split(_mm(mix, do, ta=True, out_dtype=BF16, name="mm_dw_o"))
    dgate, dya, dyr, gw["b_gate"] = _rows_bwd(
        _f_mix, mix_rows, [wts["b_gate"]], [[(dmix, D, 0)]], wrt_rows=[0, 1, 2], wrt_consts=[0],
        drow_dtypes=[BF16] * 3, name="mix_bwd", br=2 * br)
    datt = _mm(dya, w_ao, tb=True, out_dtype=BF16, name="mm_datt")
    gw["w_att_out"] = _mm(att, dya, ta=True, out_chip=True, out_dtype=BF16, name="mm_dw_att_out")
    drw = _mm(dyr, w_ro, tb=True, out_dtype=BF16, name="mm_drw")
    gw["w_rwkv_out"] = _rows_split(_mm(rw_out, dyr, ta=True, out_dtype=BF16, name="mm_dw_rw_out"))
    dcomb = _rows_bwd(_f_comb, comb_rows, [], [[(datt, ATT_WIDTH, 0)]], wrt_rows=list(range(6)), wrt_consts=[],
                      drow_dtypes=[F32] * 6, name="comb_bwd", br=2 * br)
    datt_in = None
    for g, (_, dil) in enumerate(ATT_PATTERNS):
        datt_in = _att_bwd(att_in, g, dil, dcomb[g], dcomb[3 + g], datt_in)
    dy_raw, dr_p, dv_p, dkm_p, dgg, gw["lnx_w"], gw["lnx_b"], gw["r_k"], *recv_early = _rows_bwd(
        _f_rwpost, post_rows, post_c, [[(drw, D, 0)]], wrt_rows=[0, 1, 2, 3, 4], wrt_consts=[0, 1, 2],
        drow_dtypes=[F32] * 5, name="rwpost_bwd", br=br, comm=_SiblingHalves([gw[n] for n in _DONE_EARLY]))
    st.update(loss=loss_acc[0, 0], gw=gw, dxa=dxa, dgate=dgate, datt_in=datt_in,
              dy_raw=dy_raw, dr_p=dr_p, dv_p=dv_p, dkm_p=dkm_p, dgg=dgg, d_ada_late=(d_gt1, d_sh2, d_sc2, d_gt2),
              recv_early=recv_early)
    return st


def _chip_parts(grads, recv, names, core):
    return [_half_sum(lambda a, b: a + b, [g], [r], False, BF16, core, "reduce_add2_" + n)
            for g, r, n in zip(grads, recv, names)]


def _step_after_scan(st, scan_grads, core):
    x, br, gw, h1, zs = st["x"], st["br"], st["gw"], st["h1"], st["zs"]
    dr_s, dlw, dkm_s, dv_s, daa, dbb = scan_grads
    pre_cots = [[(st["dr_p"], D, 0), (dr_s, D, 0)], [(dlw, D, 0)], [(st["dkm_p"], D, 0), (dkm_s, D, 0)],
                [(st["dv_p"], D, 0), (dv_s, D, 0)], [(daa, D, 0)], [(dbb, D, 0)], [(st["dgg"], D, 0)]]
    dz, dmu, gw["w0"], gw["a0"], gw["k_k"], gw["k_a"], dwl = _rwpre_shift_bwd(
        zs, st["z"], st["mu"], st["rwpre_c"], pre_cots, br=128)
    gw["w2"], gw["a2"] = _cols_split(dwl[0:64, 0:D]), _cols_split(dwl[64:128, D:2 * D])
    gw["g2"] = _cols_split(dwl[128:288, 2 * D:3 * D])
    gw["mu_shift"] = dmu[:, :N_RW]
    datt_in, dgate = st["datt_in"], st["dgate"]
    dw_in = [(_mm(datt_in, h1, ta=True, out_dtype=BF16, name="mm_dw_att"), N_ATT),
             (_mm(dz, h1, ta=True, out_dtype=BF16, name="mm_dw_rw"), N_RW),
             (_mm(dgate, h1, ta=True, out_dtype=BF16, name="mm_dw_gate"), N_GATE)]
    slabs = []
    for j in range(4):
        slabs += [_row_window(dw_in, j * W_IN_SHARD, (j + 1) * W_IN_SHARD), jnp.zeros((W_IN_PAD - W_IN_SHARD, D), BF16)]
    gw["w_in"] = jnp.concatenate(slabs, axis=0).reshape(4, W_IN_PAD, D)
    late = [gw[n] for n in _DONE_LATE]
    parts = _chip_parts(late, _run_comm(_SiblingHalves(late), "reduce_sib_late"), _DONE_LATE, core)
    dh1, slots_late = _mm_sum([(datt_in, st["w_att"]), (dz, st["w_rw"]), (dgate, st["w_gate"])],
                              comm=_ScatterToChips(parts), name="mm_dh1")
    grad_x, gw["norm1_w"], d_sc1, d_sh1 = _rows_bwd(
        _f_pre, [(x, D, 0)], st["pre1_c"], [[(dh1, D, 0)], [(st["dxa"], D, 0)]], wrt_rows=[0], wrt_consts=[0, 1, 2],
        drow_dtypes=[F32], name="pre1_bwd", br=2 * br)
    d_gt1, d_sh2, d_sc2, d_gt2 = st["d_ada_late"]
    return st["loss"], grad_x, (d_sh1, d_sc1, d_gt1, d_sh2, d_sc2, d_gt2), gw, slots_late


_SMALL = ("b_ada", "norm1_w", "b_gate", "mu_shift", "w0", "a0", "k_k", "k_a", "r_k", "lnx_w", "lnx_b", "norm2_w",
          "conv_b", "norm_f_w")
_NAMES = ("w_ada", "b_ada", "norm1_w", "w_in", "b_gate", "mu_shift", "w0", "w2", "a0", "a2", "g2", "k_k", "k_a", "r_k",
          "lnx_w", "lnx_b", "w_att_out", "w_rwkv_out", "w_o", "norm2_w", "w_up", "conv_w", "conv_b", "w_down",
          "norm_f_w")


def kernel(x, c, w_ada, b_ada, norm1_w, w_in, b_gate, mu_shift, w0, w2, a0, a2, g2, k_k, k_a, r_k, lnx_w, lnx_b, w_att_out, w_rwkv_out, w_o, norm2_w, w_up, conv_w, conv_b, w_down, norm_f_w, loss_target, m_w_ada, m_b_ada, m_norm1_w, m_w_in, m_b_gate, m_mu_shift, m_w0, m_w2, m_a0, m_a2, m_g2, m_k_k, m_k_a, m_r_k, m_lnx_w, m_lnx_b, m_w_att_out, m_w_rwkv_out, m_w_o, m_norm2_w, m_w_up, m_conv_w, m_conv_b, m_w_down, m_norm_f_w, v_w_ada, v_b_ada, v_norm1_w, v_w_in, v_b_gate, v_mu_shift, v_w0, v_w2, v_a0, v_a2, v_g2, v_k_k, v_k_a, v_r_k, v_lnx_w, v_lnx_b, v_w_att_out, v_w_rwkv_out, v_w_o, v_norm2_w, v_w_up, v_conv_w, v_conv_b, v_w_down, v_norm_f_w):
    args = dict(locals())
    p, pm, pv = {}, {}, {}
    for name in _NAMES:
        for dst, key in ((p, name), (pm, "m_" + name), (pv, "v_" + name)):
            t = args[key]
            if name == "w_in":
                dst[name] = jnp.swapaxes(t, 1, 2)[0]
            else:
                dst[name] = t.reshape(1, -1) if name in ("r_k", "norm_f_w") else t.reshape(t.shape[-2], t.shape[-1])
    xi, yi, ci = _me()
    chip = 2 * xi + yi
    dev = 4 * xi + 2 * yi + ci
    x2, tgt = x[0], loss_target[0]

    n_cw = 3 * (2 * D_FF // 4)
    vec = jnp.concatenate([c.reshape(-1), p["conv_w"].reshape(-1), jnp.zeros((8 * D - D - n_cw,), F32)]).reshape(8, D)
    g0 = _allgather8(vec, "gather_c").reshape(8, 8 * D)
    c_all = g0[:, :D]
    conv_w_full = jnp.concatenate([g0[2 * j, D:D + n_cw].reshape(3, -1) for j in range(4)], axis=1)
    n_ada = 6 * D // 4
    b_ada_sh = lax.dynamic_slice(p["b_ada"], (0, chip * n_ada), (1, n_ada))
    ada_sh = _ada_fwd(c_all, p["w_ada"], b_ada_sh)
    ga = _allgather8(ada_sh, "gather_ada")
    ada_all = jnp.concatenate([ga[2 * j] for j in range(4)], axis=1)
    ada_row = lax.dynamic_slice(ada_all, (dev, 0), (1, 6 * D))
    ada = [ada_row[:, j * D:(j + 1) * D] for j in range(6)]

    big = [n for n, _ in _BIG]
    shard = {n: p[n].astype(BF16) for n in big}
    shard["w_in"] = jnp.pad(shard["w_in"], ((0, W_IN_PAD - W_IN_SHARD), (0, 0)))
    wts = dict(zip(_NEEDED_FIRST, _run_comm(_GatherWeights([shard[n] for n in _NEEDED_FIRST]), "gather_w")))
    for n in _SMALL:
        wts[n] = p[n]
    wts["conv_w"] = conv_w_full
    core = ci.reshape(1).astype(jnp.int32)

    st = _step_to_scan(x2, tgt, ada, wts)
    y_raw, s0s, inverses, late = _scan_fwd(st["zs"], st["lw"], st["km"], st["aa"], st["bb"],
                                           _GatherWeights([shard[n] for n in _NEEDED_LATER]))
    st = _step_between_scans(st, y_raw, dict(zip(_NEEDED_LATER, late)))
    early = _chip_parts([st["gw"][n] for n in _DONE_EARLY], st["recv_early"], _DONE_EARLY, core)
    scan_grads, slots_early = _scan_bwd(st["zs"], st["lw"], st["km"], st["aa"], st["bb"], s0s, inverses,
                                        st["dy_raw"], _ScatterToChips(early))
    loss_part, grad_x, d_ada, gw, slots_late = _step_after_scan(st, scan_grads, core)

    small = [jnp.concatenate(d_ada, axis=1)] + [gw[n] for n in _SMALL[1:]] + [gw["conv_w"], loss_part.reshape(1, 1)]
    sizes = [t.size for t in small]
    flat = jnp.concatenate([t.reshape(-1) for t in small])
    npad = (-flat.shape[0]) % (8 * D)
    srows = (flat.shape[0] + npad) // D
    flat = jnp.concatenate([flat, jnp.zeros((npad,), F32)]).reshape(srows, D)
    parts = _allgather8(flat, "gather_small")
    tot = _sum_lead(parts, "sum_small").reshape(-1)
    pieces, pos = [], 0
    for sz in sizes:
        pieces.append(tot[pos:pos + sz])
        pos += sz
    grads = {}
    for n, piece in zip(_SMALL, pieces[:len(_SMALL)]):
        grads[n] = piece.reshape(p[n].shape)
    conv_w_grad = pieces[len(_SMALL)].reshape(3, 2 * D_FF)
    grads["conv_w"] = lax.dynamic_slice(conv_w_grad, (0, chip * (n_cw // 3)), (3, n_cw // 3))
    loss = pieces[-1][0]
    d_ada_all = parts[:, :6].reshape(8, 6 * D)
    grads["w_ada"] = _ada_bwd(c_all.T, lax.dynamic_slice(d_ada_all, (0, chip * n_ada), (8, n_ada)))

    order = _DONE_EARLY + _DONE_LATE
    reds = [_half_sum(lambda t: t[0] + t[1] + t[2] + t[3], [], [t], True, F32, core, "reduce_add4_" + n)
            for n, t in zip(order, list(slots_early) + list(slots_late))]
    for n, g in zip(order, _reduce_finish(reds, "reduce_sib2")):
        grads[n] = g

    outs_g, outs_d, outs_m, outs_v = [], [], [], []
    grads["w_in"] = grads["w_in"][:W_IN_SHARD]
    for name in _NAMES:
        g = grads[name]
        d, m, v = _adamw(p[name], g, pm[name], pv[name], "adamw_" + name)
        shape = args[name].shape
        for outs, t in ((outs_g, g), (outs_d, d), (outs_m, m), (outs_v, v)):
            outs.append(jnp.swapaxes(t[None], 1, 2) if name == "w_in" else t.reshape(shape))
    return (loss, grad_x.reshape(x.shape), *outs_g, *outs_d, *outs_m, *outs_v)
```

```python
import functools
import math

import jax
import jax.numpy as jnp
from jax import lax
from jax.experimental import pallas as pl
from jax.experimental.pallas import tpu as pltpu

F32 = jnp.float32
BF16 = jnp.bfloat16
HI = lax.Precision.HIGHEST
MESH = pl.DeviceIdType.MESH

D = 1024
ATT_PATTERNS = ((128, 1), (512, 4), (2048, 16))
ATT_BLOCK = 128
ATT_WIDTH = 512
N_ATT = 3 * 3 * ATT_WIDTH
N_RW = 3 * D + 64 + 64 + 160
N_RWP = 3456
N_LORA = N_RWP - 3 * D
N_GATE = 2 * D
D_FF = 2816
RMS_EPS = 1e-6
GN_EPS = 64e-5
SCAN_CHUNK = 64
SCAN_PAIRS = 8
NEG = -1e30
VMEM_LIMIT = 48 * 1024 * 1024
HALO = 16

ADAM_LR, ADAM_B1, ADAM_B2, ADAM_EPS, ADAM_WD, ADAM_STEP = 0.001, 0.9, 0.999, 1e-08, 0.01, 10


def _pcall(body, **kw):
    return pl.pallas_call(body, **kw)


def _cparams(sem):
    return pltpu.CompilerParams(dimension_semantics=sem, vmem_limit_bytes=VMEM_LIMIT)


def _div(n, pref, mult):
    best = None
    d = mult
    while d <= min(n, pref):
        if n % d == 0:
            best = d
        d += mult
    return best if best else n


def _dg(a, b, ca, cb):
    return lax.dot_general(a.astype(BF16), b.astype(BF16), (((ca,), (cb,)), ((), ())), preferred_element_type=F32)


@jax.custom_vjp
def _nn(a, b):
    return _dg(a, b, 1, 0)


@jax.custom_vjp
def _nt(a, b):
    return _dg(a, b, 1, 1)


@jax.custom_vjp
def _tn(a, b):
    return _dg(a, b, 0, 0)


_nn.defvjp(lambda a, b: (_nn(a, b), (a, b)), lambda res, g: (_nt(g, res[1]), _tn(res[0], g)))
_nt.defvjp(lambda a, b: (_nt(a, b), (a, b)), lambda res, g: (_nn(g, res[1]), _tn(g, res[0])))
_tn.defvjp(lambda a, b: (_tn(a, b), (a, b)), lambda res, g: (_nt(res[1], g), _nn(res[0], g)))


def _bdg(a, b, ca, cb):
    return lax.dot_general(a.astype(BF16), b.astype(BF16), (((ca,), (cb,)), ((0,), (0,))), preferred_element_type=F32)


@jax.custom_vjp
def _bnn(a, b):
    return _bdg(a, b, 2, 1)


@jax.custom_vjp
def _bnt(a, b):
    return _bdg(a, b, 2, 2)


@jax.custom_vjp
def _btn(a, b):
    return _bdg(a, b, 1, 1)


_bnn.defvjp(lambda a, b: (_bnn(a, b), (a, b)), lambda res, g: (_bnt(g, res[1]), _btn(res[0], g)))
_bnt.defvjp(lambda a, b: (_bnt(a, b), (a, b)), lambda res, g: (_bnn(g, res[1]), _btn(g, res[0])))
_btn.defvjp(lambda a, b: (_btn(a, b), (a, b)), lambda res, g: (_bnt(res[1], g), _bnn(res[0], g)))


def _hsum_impl(x, e, et):
    eb, etb = e.astype(BF16), et.astype(BF16)
    s = jnp.dot(x.astype(BF16), eb, preferred_element_type=F32)
    return jnp.dot(s.astype(BF16), etb, preferred_element_type=F32)


@jax.custom_vjp
def _hsum(x, e, et):
    return _hsum_impl(x, e, et)


_hsum.defvjp(lambda x, e, et: (_hsum_impl(x, e, et), (e, et)),
             lambda res, g: (_hsum_impl(g, res[0], res[1]), jnp.zeros_like(res[0]), jnp.zeros_like(res[1])))


def _mm(a, b, *, ta=False, tb=False, out_dtype=F32, add=None, b_chip=False, out_chip=False, comm=None, name):
    riding = _NOTHING if comm is None else comm
    nc = riding.n
    if ta:
        kdim, m = a.shape
    else:
        m, kdim = a.shape
    if b_chip:
        n = b.shape[1] if tb else 4 * b.shape[2]
    else:
        n = b.shape[0] if tb else b.shape[1]
    tm, tn, tk = _div(m, 1536, 128), _div(n, 1536, 128), _div(kdim, 2048 if ta else 1408, 128)
    if b_chip and tb:
        tk = kdim // 4
    if (b_chip and not tb) or out_chip:
        tn = n // 4
    nk = kdim // tk
    ca, cb = (0 if ta else 1), (1 if tb else 0)

    nin = 2 if add is None else 3
    gi, gj = m // tm, n // tn

    def body(*refs):
        a_ref, b_ref = refs[0], refs[1]
        add_ref = None if add is None else refs[2]
        o_ref = refs[nin + nc]
        step = (pl.program_id(0) * gj + pl.program_id(1)) * nk + pl.program_id(2)
        before, after = _comm_phases(riding, refs[nin:nin + nc] + refs[nin + nc + 1:nin + 2 * nc + 1]
                                     + refs[nin + 2 * nc + 1 + (nk > 1):], gi * gj * nk, step)
        before()
        part = lax.dot_general(a_ref[...], b_ref[...], (((ca,), (cb,)), ((), ())), preferred_element_type=F32)

        def finish(r):
            if add_ref is not None:
                r = r + add_ref[...]
            o_ref[...] = r.astype(o_ref.dtype)

        if nk == 1:
            finish(part)
            after()
            return
        acc = refs[nin + 2 * nc + 1]
        k = pl.program_id(2)

        @pl.when(k == 0)
        def _():
            acc[...] = part

        @pl.when(k > 0)
        def _():
            acc[...] += part

        @pl.when(k == nk - 1)
        def _():
            finish(acc[...])

        after()

    a_spec = pl.BlockSpec((tk, tm), lambda i, j, k: (k, i)) if ta else pl.BlockSpec((tm, tk), lambda i, j, k: (i, k))
    if b_chip:
        b_spec = (pl.BlockSpec((None, tn, tk), lambda i, j, k: (k, j, 0)) if tb
                  else pl.BlockSpec((None, tk, tn), lambda i, j, k: (j, k, 0)))
    else:
        b_spec = pl.BlockSpec((tn, tk), lambda i, j, k: (j, k)) if tb else pl.BlockSpec((tk, tn), lambda i, j, k: (k, j))
    in_specs = [a_spec, b_spec]
    args = [a, b]
    if add is not None:
        in_specs.append(pl.BlockSpec((tm, tn), lambda i, j, k: (i, j)))
        args.append(add)
    if out_chip:
        out_spec = pl.BlockSpec((None, tm, tn), lambda i, j, k: (j, i, 0))
        out_shape = jax.ShapeDtypeStruct((4, m, tn), out_dtype)
    else:
        out_spec = pl.BlockSpec((tm, tn), lambda i, j, k: (i, j))
        out_shape = jax.ShapeDtypeStruct((m, n), out_dtype)
    res = _pcall(
        body, name=name, grid=(gi, gj, nk), in_specs=in_specs + [_HBM] * nc, out_specs=[out_spec] + [_HBM] * nc,
        out_shape=[out_shape] + riding.out_shape,
        scratch_shapes=([] if nk == 1 else [pltpu.VMEM((tm, tn), F32)]) + riding.sems,
        compiler_params=_cparams(("arbitrary",) * 3 if nc else ("parallel", "parallel", "arbitrary")),
    )(*args, *riding.ins)
    return res[0] if comm is None else (res[0], res[1:])


def _mm_sum(pairs, *, comm, name):
    m, n = pairs[0][0].shape[0], pairs[0][1].shape[1]
    tm, tn = _div(m, 1024, 128), _div(n, 1024, 128)
    tks = [_div(a.shape[1], 1408, 128) for a, _ in pairs]
    nks = [a.shape[1] // tk for (a, _), tk in zip(pairs, tks)]
    offs = [sum(nks[:p]) for p in range(len(pairs))]
    total, npair, nc = sum(nks), len(pairs), comm.n
    gi, gj = m // tm, n // tn

    def body(*refs):
        o_ref, acc = refs[2 * npair + nc], refs[2 * npair + 2 * nc + 1]
        k = pl.program_id(2)
        step = (pl.program_id(0) * gj + pl.program_id(1)) * total + k
        before, after = _comm_phases(comm, refs[2 * npair:2 * npair + nc]
                                     + refs[2 * npair + nc + 1:2 * npair + 2 * nc + 1]
                                     + refs[2 * npair + 2 * nc + 2:], gi * gj * total, step)
        before()
        for p in range(npair):
            def partial_product(p=p):
                part = jnp.dot(refs[2 * p][...], refs[2 * p + 1][...], preferred_element_type=F32)
                if p == 0:
                    @pl.when(k == 0)
                    def _():
                        acc[...] = part

                    @pl.when(k > 0)
                    def _():
                        acc[...] += part
                else:
                    acc[...] += part

            pl.when(jnp.logical_and(k >= offs[p], k < offs[p] + nks[p]))(partial_product)

        @pl.when(k == total - 1)
        def _():
            o_ref[...] = acc[...].astype(o_ref.dtype)

        after()

    def specs(tk, off, nk):
        def kb(k):
            return jnp.clip(k - off, 0, nk - 1)
        return [pl.BlockSpec((tm, tk), lambda i, j, k: (i, kb(k))), pl.BlockSpec((tk, tn), lambda i, j, k: (kb(k), j))]

    in_specs, args = [], []
    for (a, b), tk, off, nk in zip(pairs, tks, offs, nks):
        in_specs += specs(tk, off, nk)
        args += [a, b]
    res = _pcall(
        body, name=name, grid=(gi, gj, total), in_specs=in_specs + [_HBM] * nc,
        out_specs=[pl.BlockSpec((tm, tn), lambda i, j, k: (i, j))] + [_HBM] * nc,
        out_shape=[jax.ShapeDtypeStruct((m, n), BF16)] + comm.out_shape,
        scratch_shapes=[pltpu.VMEM((tm, tn), F32)] + comm.sems,
        compiler_params=_cparams(("arbitrary",) * 3),
    )(*args, *comm.ins)
    return res[0], res[1:]


def _row_spec(br, w, cb):
    return pl.BlockSpec((br, w), lambda i: (i, cb))


def _const_spec(shape):
    return pl.BlockSpec(shape, lambda i: (0,) * len(shape))


def _rows_fwd(fn, rows, consts, outs, *, name, br, acc_shape=None, halo=None):
    s = rows[0][0].shape[0]
    nr, nc = len(rows), len(consts)
    kept = [k for k, o in enumerate(outs) if o is not None]

    def body(*refs):
        xs = [r[...].astype(F32) for r in refs[:nr]]
        cs = [c[...] for c in refs[nr:nr + nc]]
        if halo is not None:
            cs.append(jnp.where(pl.program_id(0) == 0, 0.0, refs[nr + nc][...].astype(F32)))
        res = fn(*xs, *cs)
        orefs = refs[nr + nc + (halo is not None):]
        for j, k in enumerate(kept):
            orefs[j][...] = res[k].astype(orefs[j].dtype)
        if acc_shape is not None:
            acc_ref = orefs[len(kept)]

            @pl.when(pl.program_id(0) == 0)
            def _():
                acc_ref[...] = jnp.zeros_like(acc_ref)

            acc_ref[...] += res[len(outs)]

    in_specs = [_row_spec(br, w, cb) for (_, w, cb) in rows] + [_const_spec(c.shape) for c in consts]
    args = [r[0] for r in rows] + list(consts)
    if halo is not None:
        harr, hw, hcb = rows[halo]
        in_specs.append(pl.BlockSpec((HALO, hw), lambda i: (jnp.maximum(i * (br // HALO) - 1, 0), hcb)))
        args.append(harr)
    out_specs = [_row_spec(br, outs[k][0], 0) for k in kept]
    out_shape = [jax.ShapeDtypeStruct((s, outs[k][0]), outs[k][1]) for k in kept]
    if acc_shape is not None:
        out_specs.append(_const_spec(acc_shape))
        out_shape.append(jax.ShapeDtypeStruct(acc_shape, F32))
    return _pcall(
        body, name=name, grid=(pl.cdiv(s, br),), in_specs=in_specs, out_specs=out_specs, out_shape=out_shape,
        compiler_params=_cparams(("arbitrary",)),
    )(*args)


def _rows_bwd(fn, rows, consts, cots, *, wrt_rows, wrt_consts, drow_dtypes, name, br, unit_cot=False, comm=None):
    comm = _NOTHING if comm is None else comm
    ncomm = comm.n
    nout = len(wrt_rows) + len(wrt_consts)
    s = rows[0][0].shape[0]
    nr, nc = len(rows), len(consts)
    flat_cots = [c for lst in cots for c in lst]
    ncot = len(flat_cots)

    def body(*refs):
        xs = [r[...].astype(F32) for r in refs[:nr]]
        cs = [c[...] for c in refs[nr:nr + nc]]
        cvals = [c[...].astype(F32) for c in refs[nr + nc:nr + nc + ncot]]
        orefs = refs[nr + nc + ncot + ncomm:]
        before, after = _comm_phases(comm, refs[nr + nc + ncot:nr + nc + ncot + ncomm] + orefs[nout:], s // br)
        before()

        def g(*d):
            xs2, cs2 = list(xs), list(cs)
            for j, k in enumerate(wrt_rows):
                xs2[k] = d[j]
            for j, k in enumerate(wrt_consts):
                cs2[k] = d[len(wrt_rows) + j]
            return tuple(fn(*xs2, *cs2))

        prim = [xs[k] for k in wrt_rows] + [cs[k] for k in wrt_consts]
        outs, vjp = jax.vjp(g, *prim)
        ct = []
        pos = 0
        for o, lst in zip(outs, cots):
            if unit_cot:
                ct.append(jnp.ones_like(o))
                continue
            acc = jnp.zeros_like(o)
            for _ in lst:
                acc = acc + cvals[pos]
                pos += 1
            ct.append(acc)
        grads = vjp(tuple(ct))
        for j in range(len(wrt_rows)):
            orefs[j][...] = grads[j].astype(orefs[j].dtype)

        @pl.when(pl.program_id(0) == 0)
        def _():
            for j in range(len(wrt_consts)):
                oref = orefs[len(wrt_rows) + j]
                oref[...] = jnp.zeros_like(oref)

        for j in range(len(wrt_consts)):
            orefs[len(wrt_rows) + j][...] += grads[len(wrt_rows) + j]
        after()

    in_specs = ([_row_spec(br, w, cb) for (_, w, cb) in rows] + [_const_spec(c.shape) for c in consts]
                + [_row_spec(br, w, cb) for (_, w, cb) in flat_cots] + [_HBM] * ncomm)
    out_specs = ([_row_spec(br, rows[k][1], 0) for k in wrt_rows] + [_const_spec(consts[k].shape) for k in wrt_consts]
                 + [_HBM] * ncomm)
    out_shape = ([jax.ShapeDtypeStruct((s, rows[k][1]), dt) for k, dt in zip(wrt_rows, drow_dtypes)]
                 + [jax.ShapeDtypeStruct(consts[k].shape, F32) for k in wrt_consts] + comm.out_shape)
    return _pcall(
        body, name=name, grid=(s // br,), in_specs=in_specs, out_specs=out_specs, out_shape=out_shape,
        scratch_shapes=comm.sems, compiler_params=_cparams(("arbitrary",)),
    )(*[r[0] for r in rows], *consts, *[c[0] for c in flat_cots], *comm.ins)


def _rms(x, w):
    return x * lax.rsqrt(jnp.mean(x * x, axis=-1, keepdims=True) + RMS_EPS) * w


def _f_pre(x, nw, sc, sh):
    return _rms(x, nw) * (1.0 + sc) + sh, x


def _f_pre2(x, o, gt, nw, sc, sh):
    x1 = x + gt * o
    return x1, _rms(x1, nw) * (1.0 + sc) + sh


def _f_fin(x1, f, tgt, gt, nfw):
    y = _rms(x1 + gt * f, nfw)
    return (0.5 * jnp.mean(jnp.square(y - tgt), axis=-1, keepdims=True),)


def _f_comb(o1, o2, o3, l1, l2, l3):
    m = lax.stop_gradient(jnp.maximum(jnp.maximum(l1, l2), l3))
    e1, e2, e3 = jnp.exp(l1 - m), jnp.exp(l2 - m), jnp.exp(l3 - m)
    return ((e1 * o1 + e2 * o2 + e3 * o3) / (e1 + e2 + e3),)


def _f_rwpre(zs, w0, a0, k_k, k_a, wl, e, et):
    r, k, v, zl = zs[:, 0:D], zs[:, D:2 * D], zs[:, 2 * D:3 * D], zs[:, 3 * D:N_RWP]
    lane = lax.broadcasted_iota(jnp.int32, zl.shape, 1)
    t = jnp.where(lane < 64, jnp.tanh(zl), jnp.where(lane < 128, zl, jnp.where(lane < 288, jax.nn.sigmoid(zl), 0.0)))
    lo = _nn(t[:, 0:128], wl[0:128, 0:2 * D])
    g = _nn(t[:, 128:N_LORA], wl[128:N_LORA, 2 * D:3 * D])
    lw = -math.exp(-0.5) * jax.nn.sigmoid(w0 + lo[:, 0:D])
    a = jax.nn.sigmoid(a0 + lo[:, D:2 * D])
    k_mod = k * (1.0 + (a - 1.0) * k_a)
    kk = k * k_k
    kk = kk / jnp.maximum(jnp.sqrt(_hsum(kk * kk, e, et)), 1e-12)
    return r, lw, k_mod, v, -kk, kk * a, g


def _f_rwpost(y, r, v, k_mod, g, lnx_w, lnx_b, r_k, e, et):
    mean = _hsum(y, e, et) * (1.0 / 64)
    yc = y - mean
    var = _hsum(yc * yc, e, et) * (1.0 / 64)
    yn = yc * lax.rsqrt(var + GN_EPS) * lnx_w + lnx_b
    bonus = _hsum(r * k_mod * r_k, e, et) * v
    return ((yn + bonus) * g,)


def _f_mix(gi, ya, yr, bg):
    gate = jax.nn.sigmoid(gi + bg)
    return (gate[:, 0:D] * ya + gate[:, D:2 * D] * yr,)


def _f_adamw(w, g, m, v):
    m = ADAM_B1 * m + (1.0 - ADAM_B1) * g
    v = ADAM_B2 * v + (1.0 - ADAM_B2) * jnp.square(g)
    m_hat = m / (1.0 - ADAM_B1 ** ADAM_STEP)
    v_hat = v / (1.0 - ADAM_B2 ** ADAM_STEP)
    return -ADAM_LR * (m_hat / (jnp.sqrt(v_hat) + ADAM_EPS) + ADAM_WD * w), m, v


def _down(x, k):
    row = lax.broadcasted_iota(jnp.int32, x.shape, 0)
    return jnp.where(row < k, 0.0, pltpu.roll(x, k, 0))


def _up(x, k):
    n = x.shape[0]
    row = lax.broadcasted_iota(jnp.int32, x.shape, 0)
    return jnp.where(row >= n - k, 0.0, pltpu.roll(x, n - k, 0))


def _col_spec(s, w, off=0):
    return pl.BlockSpec((s, w), lambda j: (0, j + off))


def _rwpre_shift_bwd(zs, z, mu, consts, cots, *, br):
    s, w = zs.shape
    n = s // br
    flat = [c for lst in cots for c in lst]
    nc, ncot, nwrt = len(consts), len(flat), 5

    def this(i):
        return jnp.minimum(i, n - 1)

    def last(i):
        return jnp.maximum(i - 1, 0)

    def body(*refs):
        zs_ref, z_ref, zh_ref, mu_ref = refs[:4]
        c_refs, cot_refs = refs[4:4 + nc], refs[4 + nc:4 + nc + ncot]
        dz_ref, dmu_ref = refs[4 + nc + ncot:6 + nc + ncot]
        dc_refs = refs[6 + nc + ncot:6 + nc + ncot + nwrt]
        kept = refs[-1]
        i = pl.program_id(0)

        @pl.when(i == 0)
        def _():
            dmu_ref[...] = jnp.zeros_like(dmu_ref)
            for ref in dc_refs:
                ref[...] = jnp.zeros_like(ref)

        cs = [c[...] for c in c_refs]

        def g(zz, *d):
            return tuple(_f_rwpre(zz, *d, *cs[nwrt:]))

        outs, vjp = jax.vjp(g, zs_ref[...], *cs[:nwrt])
        cts, pos = [], 0
        for o, lst in zip(outs, cots):
            acc = jnp.zeros_like(o)
            for _ in lst:
                acc = acc + cot_refs[pos][...].astype(F32)
                pos += 1
            cts.append(acc)
        grads = vjp(tuple(cts))
        dzs_new = grads[0]

        @pl.when(i < n)
        def _():
            for ref, gr in zip(dc_refs, grads[1:]):
                ref[...] += gr

        @pl.when(i > 0)
        def _():
            d, m = kept[...], mu_ref[...]
            row = lax.broadcasted_iota(jnp.int32, d.shape, 0)
            head = jnp.sum(jnp.where(row == 0, dzs_new, 0.0), axis=0, keepdims=True)
            head = jnp.where(i < n, head, 0.0)
            dm = d * m
            after = jnp.where(row == br - 1, head * m, pltpu.roll(dm, br - 1, 0))
            dz_ref[...] = (d - dm + after).astype(dz_ref.dtype)
            zz, halo = z_ref[...].astype(F32), zh_ref[...].astype(F32)
            tail = jnp.sum(jnp.where(lax.broadcasted_iota(jnp.int32, halo.shape, 0) == HALO - 1, halo, 0.0), axis=0,
                           keepdims=True)
            before = jnp.where(row == 0, jnp.where(i > 1, tail, 0.0), pltpu.roll(zz, 1, 0))
            dmu_ref[...] += jnp.sum(d * (before - zz), axis=0, keepdims=True)

        kept[...] = dzs_new

    in_specs = ([pl.BlockSpec((br, w), lambda i: (this(i), 0)), pl.BlockSpec((br, w), lambda i: (last(i), 0)),
                 pl.BlockSpec((HALO, w), lambda i: (jnp.maximum(last(i) * (br // HALO) - 1, 0), 0)),
                 _const_spec(mu.shape)] + [_const_spec(c.shape) for c in consts]
                + [pl.BlockSpec((br, cw), lambda i, cb=cb: (this(i), cb)) for (_, cw, cb) in flat])
    out_specs = ([pl.BlockSpec((br, w), lambda i: (last(i), 0)), _const_spec(mu.shape)]
                 + [_const_spec(consts[k].shape) for k in range(nwrt)])
    out_shape = ([jax.ShapeDtypeStruct((s, w), BF16), jax.ShapeDtypeStruct(mu.shape, F32)]
                 + [jax.ShapeDtypeStruct(consts[k].shape, F32) for k in range(nwrt)])
    return _pcall(
        body, name="rwpre_shift_bwd", grid=(n + 1,), in_specs=in_specs, out_specs=out_specs, out_shape=out_shape,
        scratch_shapes=[pltpu.VMEM((br, w), F32)], compiler_params=_cparams(("arbitrary",)),
    )(zs, z, z, mu, *consts, *[c[0] for c in flat])


def _conv3(x, w_ref, b_ref):
    return b_ref[...] + w_ref[0:1, :] * _down(x, 2) + w_ref[1:2, :] * _down(x, 1) + w_ref[2:3, :] * x


def _conv_fwd(u, cw, cb):
    s = u.shape[0]
    nb = D_FF // 128

    def body(ug_ref, uv_ref, wg_ref, wv_ref, bg_ref, bv_ref, o_ref):
        gate = _conv3(ug_ref[...], wg_ref, bg_ref)
        val = _conv3(uv_ref[...], wv_ref, bv_ref)
        o_ref[...] = (gate * jax.nn.sigmoid(gate) * val).astype(o_ref.dtype)

    return _pcall(
        body, name="conv_fwd", grid=(nb,),
        in_specs=[_col_spec(s, 128), _col_spec(s, 128, nb), _col_spec(3, 128), _col_spec(3, 128, nb),
                  _col_spec(1, 128), _col_spec(1, 128, nb)],
        out_specs=_col_spec(s, 128), out_shape=jax.ShapeDtypeStruct((s, D_FF), BF16),
        compiler_params=_cparams(("parallel",)),
    )(u, u, cw, cw, cb, cb)


def _conv_bwd(u, cw, cb, dact):
    s = u.shape[0]
    nb = D_FF // 128

    def half(x, d, w_ref, du_ref, dw_ref, db_ref):
        x1, x2 = _down(x, 1), _down(x, 2)
        du_ref[...] = (w_ref[2:3, :] * d + w_ref[1:2, :] * _up(d, 1) + w_ref[0:1, :] * _up(d, 2)).astype(du_ref.dtype)
        dw_ref[0:1, :] = jnp.sum(d * x2, axis=0, keepdims=True)
        dw_ref[1:2, :] = jnp.sum(d * x1, axis=0, keepdims=True)
        dw_ref[2:3, :] = jnp.sum(d * x, axis=0, keepdims=True)
        db_ref[...] = jnp.sum(d, axis=0, keepdims=True)

    def body(ug_ref, uv_ref, wg_ref, wv_ref, bg_ref, bv_ref, da_ref,
             du_ref, dwg_ref, dwv_ref, dbg_ref, dbv_ref, hold):
        @pl.when(pl.program_id(1) == 0)
        def _():
            ug, uv, da = ug_ref[...], uv_ref[...], da_ref[...]
            gate = _conv3(ug, wg_ref, bg_ref)
            val = _conv3(uv, wv_ref, bv_ref)
            sg = jax.nn.sigmoid(gate)
            dgate = da * val * sg * (1.0 + gate * (1.0 - sg))
            dval = da * gate * sg
            half(ug, dgate, wg_ref, du_ref, dwg_ref, dbg_ref)
            half(uv, dval, wv_ref, hold, dwv_ref, dbv_ref)

        @pl.when(pl.program_id(1) == 1)
        def _():
            du_ref[...] = hold[...]

    def spec(rows, off=0):
        return pl.BlockSpec((rows, 128), lambda j, h: (0, j + off))

    du, dwg, dwv, dbg, dbv = _pcall(
        body, name="conv_bwd", grid=(nb, 2),
        in_specs=[spec(s), spec(s, nb), spec(3), spec(3, nb), spec(1), spec(1, nb), spec(s)],
        out_specs=[pl.BlockSpec((s, 128), lambda j, h: (0, j + h * nb)), spec(3), spec(3), spec(1), spec(1)],
        out_shape=[jax.ShapeDtypeStruct((s, 2 * D_FF), BF16),
                   jax.ShapeDtypeStruct((3, D_FF), F32), jax.ShapeDtypeStruct((3, D_FF), F32),
                   jax.ShapeDtypeStruct((1, D_FF), F32), jax.ShapeDtypeStruct((1, D_FF), F32)],
        scratch_shapes=[pltpu.VMEM((s, 128), BF16)],
        compiler_params=_cparams(("arbitrary", "arbitrary")),
    )(u, u, cw, cw, cb, cb, dact)
    return du, jnp.concatenate([dwg, dwv], axis=1), jnp.concatenate([dbg, dbv], axis=1)


ATT_BATCH = 4


def _att_batch(q, kp, kc, vp, vc, first):
    ma = lax.broadcasted_iota(jnp.int32, (1, ATT_BLOCK, 128), 2) < 64

    def diag(x):
        return jnp.concatenate([jnp.where(ma, x, 0.0), jnp.where(ma, 0.0, x)], axis=1)

    qi = lax.broadcasted_iota(jnp.int32, (1, ATT_BLOCK, 2 * ATT_BLOCK), 1)
    kj = lax.broadcasted_iota(jnp.int32, (1, ATT_BLOCK, 2 * ATT_BLOCK), 2) & (ATT_BLOCK - 1)
    okp = kj >= qi + jnp.where(first, 2 * ATT_BLOCK, 0)
    okc = kj <= qi
    sp = jnp.where(okp, _bnt(q, diag(kp)) * 0.125, NEG)
    sc = jnp.where(okc, _bnt(q, diag(kc)) * 0.125, NEG)

    def per_head(fn, x):
        return fn(x[..., :ATT_BLOCK]), fn(x[..., ATT_BLOCK:])

    def spread(ab):
        return jnp.concatenate([jnp.broadcast_to(t, t.shape[:2] + (ATT_BLOCK,)) for t in ab], axis=-1)

    row_max = functools.partial(jnp.max, axis=-1, keepdims=True)
    row_sum = functools.partial(jnp.sum, axis=-1, keepdims=True)
    m = [lax.stop_gradient(jnp.maximum(a, b)) for a, b in zip(per_head(row_max, sp), per_head(row_max, sc))]
    pp, pc = jnp.exp(sp - spread(m)), jnp.exp(sc - spread(m))
    den = [a + b for a, b in zip(per_head(row_sum, pp), per_head(row_sum, pc))]
    num = _bnn(pp, diag(vp)) + _bnn(pc, diag(vc))
    out = num / jnp.where(ma, den[0], den[1])
    lse = jnp.where(ma, m[0] + jnp.log(den[0]), m[1] + jnp.log(den[1]))
    return out, jnp.broadcast_to(lse, out.shape)


def _att_pairs_per_step(dil):
    return 4 if dil == 1 else 1


def _att_residues(dil):
    return min(dil, ATT_BATCH // _att_pairs_per_step(dil))


def _att_specs(g, dil):
    rows, pp = ATT_BLOCK * dil, _att_pairs_per_step(dil)

    def cur(slot):
        return pl.BlockSpec((rows, 128 * pp), lambda n, p: (n, (g * 3 + slot) * (4 // pp) + p))

    def prev(slot):
        return pl.BlockSpec((rows, 128 * pp), lambda n, p: (jnp.maximum(n - 1, 0), (g * 3 + slot) * (4 // pp) + p))

    return [cur(0), prev(1), cur(1), prev(2), cur(2)]


def _att_out_spec(dil):
    return pl.BlockSpec((ATT_BLOCK * dil, 128 * _att_pairs_per_step(dil)), lambda n, p: (n, p))


def _att_grid(s, dil):
    return (s // (ATT_BLOCK * dil), 4 // _att_pairs_per_step(dil))


def _att_windows(i, dil):
    res = _att_residues(dil)

    def rows(r):
        return pl.ds(i * res + r, ATT_BLOCK, stride=dil) if dil > 1 else pl.ds(0, ATT_BLOCK)

    return [(rows(r), pl.ds(128 * j, 128)) for j in range(_att_pairs_per_step(dil)) for r in range(res)]


def _att_fwd(att_in, g, dil):
    s = att_in.shape[0]

    def body(q_ref, kp_ref, kc_ref, vp_ref, vc_ref, o_ref, l_ref):
        first = pl.program_id(0) == 0

        def one(i, carry):
            win = _att_windows(i, dil)
            vals = [jnp.stack([ref[w] for w in win]) for ref in (q_ref, kp_ref, kc_ref, vp_ref, vc_ref)]
            o, l = _att_batch(*vals, first)
            for j, w in enumerate(win):
                o_ref[w] = o[j]
                l_ref[w] = l[j]
            return carry

        lax.fori_loop(0, dil // _att_residues(dil), one, 0)

    return _pcall(
        body, name=f"att_fwd{g}", grid=_att_grid(s, dil), in_specs=_att_specs(g, dil),
        out_specs=[_att_out_spec(dil)] * 2, out_shape=[jax.ShapeDtypeStruct((s, ATT_WIDTH), F32)] * 2,
        compiler_params=_cparams(("parallel", "parallel")),
    )(att_in, att_in, att_in, att_in, att_in)


def _att_bwd(att_in, g, dil, do, dl, acc):
    s = att_in.shape[0]
    rows, pp = ATT_BLOCK * dil, _att_pairs_per_step(dil)
    nb, npair, wid = s // rows, 4 // pp, 128 * pp

    def body(q_ref, kp_ref, kc_ref, vp_ref, vc_ref, do_ref, dl_ref, *rest):
        o_ref, lag_q, lag_k, lag_v = rest[-9:-5]
        stage = rest[-5:]
        n, p = pl.program_id(0), pl.program_id(1)
        first = n == 0

        @pl.when(n < nb)
        def _():
            def one(i, carry):
                win = _att_windows(i, dil)
                vals = [jnp.stack([ref[w] for w in win]) for ref in (q_ref, kp_ref, kc_ref, vp_ref, vc_ref)]
                _, vjp = jax.vjp(lambda *a: _att_batch(*a, first), *vals)
                grads = vjp((jnp.stack([do_ref[w] for w in win]), jnp.stack([dl_ref[w] for w in win])))
                for ref, gr in zip(stage, grads):
                    for j, w in enumerate(win):
                        ref[w] = gr[j]
                return carry

            lax.fori_loop(0, dil // _att_residues(dil), one, 0)

        live = n < nb
        for pj in range(npair):
            @pl.when((n > 0) & (p == pj))
            def _(pj=pj):
                c = pj * wid
                o_ref[:, c:c + wid] = lag_q[pj]
                o_ref[:, ATT_WIDTH + c:ATT_WIDTH + c + wid] = (
                    lag_k[pj] + jnp.where(live, stage[1][...], 0.0)).astype(BF16)
                o_ref[:, 2 * ATT_WIDTH + c:2 * ATT_WIDTH + c + wid] = (
                    lag_v[pj] + jnp.where(live, stage[3][...], 0.0)).astype(BF16)

        @pl.when(live)
        def _():
            lag_q[p] = stage[0][...].astype(BF16)
            lag_k[p] = stage[2][...]
            lag_v[p] = stage[4][...]

    def col(slot, n, p):
        return (g * 3 + slot) * npair + jnp.where(n < nb, p, npair - 1)

    def cur(slot):
        return pl.BlockSpec((rows, wid), lambda n, p: (jnp.minimum(n, nb - 1), col(slot, n, p)))

    def prev(slot):
        return pl.BlockSpec((rows, wid), lambda n, p: (jnp.maximum(jnp.minimum(n, nb - 1) - 1, 0), col(slot, n, p)))

    cot = pl.BlockSpec((rows, wid), lambda n, p: (jnp.minimum(n, nb - 1), jnp.where(n < nb, p, npair - 1)))
    carried = [] if acc is None else [acc]
    return _pcall(
        body, name=f"att_bwd{g}", grid=(nb + 1, npair),
        in_specs=[cur(0), prev(1), cur(1), prev(2), cur(2), cot, cot] + [pl.BlockSpec(memory_space=pl.ANY)] * len(carried),
        out_specs=pl.BlockSpec((rows, 3 * ATT_WIDTH), lambda n, p: (jnp.maximum(n - 1, 0), g)),
        out_shape=jax.ShapeDtypeStruct((s, N_ATT), BF16), input_output_aliases={7: 0} if carried else {},
        scratch_shapes=([pltpu.VMEM((npair, rows, wid), BF16)] + [pltpu.VMEM((npair, rows, wid), F32)] * 2
                        + [pltpu.VMEM((rows, wid), F32)] * 5),
        compiler_params=_cparams(("arbitrary", "arbitrary")),
    )(att_in, att_in, att_in, att_in, att_in, do, dl, *carried)


def _cumsum_rows_impl(x):
    row = lax.broadcasted_iota(jnp.int32, x.shape, 0)
    shift = 1
    while shift < x.shape[0]:
        x = x + jnp.where(row >= shift, pltpu.roll(x, shift, 0), 0.0)
        shift *= 2
    return x


@jax.custom_vjp
def _cumsum_rows(x):
    return _cumsum_rows_impl(x)


_cumsum_rows.defvjp(lambda x: (_cumsum_rows_impl(x), None),
                    lambda _, g: (jnp.sum(g, axis=0, keepdims=True) - _cumsum_rows_impl(g) + g,))


def _unit_lower_inverse_impl(n):
    eye = (lax.broadcasted_iota(jnp.int32, (1,) + n.shape[1:], 1)
           == lax.broadcasted_iota(jnp.int32, (1,) + n.shape[1:], 2))
    t = jnp.where(eye, 1.0, 0.0) + n
    pw = n
    for _ in range(5):
        pw = _bnn(pw, pw)
        t = t + _bnn(t, pw)
    return t


@jax.custom_vjp
def _unit_lower_inverse(n):
    return _unit_lower_inverse_impl(n)


def _unit_lower_inverse_fwd(n):
    t = _unit_lower_inverse_impl(n)
    return t, t


_unit_lower_inverse.defvjp(_unit_lower_inverse_fwd, lambda t, g: (_bnt(_btn(t, g), t),))


@jax.custom_vjp
def _known_inverse(n, t):
    return t


_known_inverse.defvjp(lambda n, t: (t, t), lambda t, g: (_bnt(_btn(t, g), t), jnp.zeros_like(t)))


def _scan_chunk(r, lw, k, v, a, b, s0, inverse):
    c = SCAN_CHUNK
    p = s0.shape[0]
    cum = _cumsum_rows(lw)
    tot = jnp.sum(lw, axis=0, keepdims=True)
    ma = (lax.broadcasted_iota(jnp.int32, (c, 128 * p), 1) & 127) < 64

    def pairs(x):
        return jnp.concatenate([x[None, :, 128 * j:128 * (j + 1)] for j in range(p)], axis=0)

    def stack(x):
        return jnp.concatenate([pairs(jnp.where(ma, x, 0.0)), pairs(jnp.where(ma, 0.0, x))], axis=1)

    einv, eend = jnp.exp(-cum), jnp.exp(tot - cum)
    ra, aa = stack(r * jnp.exp(cum)), stack(a * jnp.exp(cum - lw))
    bi, ki, be, ke, vs = stack(b * einv), stack(k * einv), stack(b * eend), stack(k * eend), stack(v)
    r2 = lax.broadcasted_iota(jnp.int32, (1, 2 * c, 2 * c), 1)
    c2 = lax.broadcasted_iota(jnp.int32, (1, 2 * c, 2 * c), 2)
    same = (r2 >= c) == (c2 >= c)
    strict = jnp.logical_and(same, c2 < r2)
    incl = jnp.logical_and(same, c2 <= r2)
    s0 = jnp.where(same, s0, 0.0)
    prod = _bnt(jnp.concatenate([aa, ra], axis=1), jnp.concatenate([bi, ki], axis=1))
    a_ab = jnp.where(strict, prod[:, :2 * c, :2 * c], 0.0)
    a_ak = jnp.where(strict, prod[:, :2 * c, 2 * c:], 0.0)
    a_rb = jnp.where(incl, prod[:, 2 * c:, :2 * c], 0.0)
    a_rk = jnp.where(incl, prod[:, 2 * c:, 2 * c:], 0.0)
    t = inverse(a_ab)
    u = _bnn(t, _bnt(aa, s0) + _bnn(a_ak, vs))
    uv = jnp.concatenate([u, vs], axis=1)
    ys = _bnt(ra, s0) + _bnn(jnp.concatenate([a_rb, a_rk], axis=2), uv)
    s1 = s0 * pairs(jnp.exp(tot)) + _btn(uv, jnp.concatenate([be, ke], axis=1))
    y3 = ys[:, :c] + ys[:, c:]
    return (jnp.concatenate([y3[j] for j in range(p)], axis=1), s1), t


def _scan_specs(rev, n):
    def at(i):
        return n - 1 - i if rev else i

    def cm(cb):
        return pl.BlockSpec((SCAN_CHUNK, D), lambda i: (at(i), cb))

    return cm, pl.BlockSpec((1, SCAN_PAIRS, 128, 128), lambda i: (at(i), 0, 0, 0))


def _comm_phases(comm, refs, n, step=None):
    k = comm.n
    srcs, outs, sems = refs[:k], refs[k:2 * k], refs[2 * k:]
    i = pl.program_id(0) if step is None else step

    def before():
        @pl.when(i == 0)
        def _():
            comm.first(srcs, outs, sems)

    def after():
        if comm.mid is not None:
            @pl.when(i == (3 * n) // 4)
            def _():
                comm.mid(srcs, outs, sems)

        @pl.when(i == n - 1)
        def _():
            comm.last(srcs, outs, sems)

    return before, after


def _scan_fwd(zs, lw, km, aa, bb, comm):
    s = zs.shape[0]
    n = s // SCAN_CHUNK
    cm, st = _scan_specs(False, n)
    k = comm.n

    def body(*refs):
        r_ref, lw_ref, k_ref, v_ref, a_ref, b_ref = refs[:6]
        y_ref, s0_ref, t_ref = refs[6 + k:9 + k]
        state = refs[9 + 2 * k]
        before, after = _comm_phases(comm, refs[6:6 + k] + refs[9 + k:9 + 2 * k] + refs[10 + 2 * k:], n)
        before()

        @pl.when(pl.program_id(0) == 0)
        def _():
            state[...] = jnp.zeros_like(state)

        s0 = state[...]
        s0_ref[0] = s0
        (y, s1), t = _scan_chunk(*[ref[...] for ref in (r_ref, lw_ref, k_ref, v_ref, a_ref, b_ref)], s0,
                                 _unit_lower_inverse)
        y_ref[...] = y
        t_ref[0] = t.astype(BF16)
        state[...] = s1
        after()

    per_chunk = (n, SCAN_PAIRS, 128, 128)
    res = _pcall(
        body, name="scan_fwd", grid=(n,), in_specs=[cm(0), cm(0), cm(0), cm(2), cm(0), cm(0)] + [_HBM] * k,
        out_specs=[cm(0), st, st] + [_HBM] * k,
        out_shape=[jax.ShapeDtypeStruct((s, D), F32), jax.ShapeDtypeStruct(per_chunk, F32),
                   jax.ShapeDtypeStruct(per_chunk, BF16)] + comm.out_shape,
        scratch_shapes=[pltpu.VMEM((SCAN_PAIRS, 128, 128), F32)] + comm.sems,
        compiler_params=_cparams(("arbitrary",)),
    )(zs, lw, km, zs, aa, bb, *comm.ins)
    return res[0], res[1], res[2], res[3:]


def _scan_bwd(zs, lw, km, aa, bb, s0s, ts, dy, comm):
    s = zs.shape[0]
    n = s // SCAN_CHUNK
    cm, st = _scan_specs(True, n)
    k = comm.n

    def body(*refs):
        r_ref, lw_ref, k_ref, v_ref, a_ref, b_ref, s0_ref, t_ref, dy_ref = refs[:9]
        douts = refs[9 + k:15 + k]
        dstate = refs[15 + 2 * k]
        before, after = _comm_phases(comm, refs[9:9 + k] + refs[15 + k:15 + 2 * k] + refs[16 + 2 * k:], n)
        before()

        @pl.when(pl.program_id(0) == 0)
        def _():
            dstate[...] = jnp.zeros_like(dstate)

        t = t_ref[0].astype(F32)
        prim = [ref[...] for ref in (r_ref, lw_ref, k_ref, v_ref, a_ref, b_ref)] + [s0_ref[0]]
        _, vjp, _ = jax.vjp(lambda *p: _scan_chunk(*p, lambda nil: _known_inverse(nil, t)), *prim, has_aux=True)
        grads = vjp((dy_ref[...], dstate[...]))
        for ref, gr in zip(douts, grads[:6]):
            ref[...] = gr
        dstate[...] = grads[6]
        after()

    res = _pcall(
        body, name="scan_bwd", grid=(n,),
        in_specs=[cm(0), cm(0), cm(0), cm(2), cm(0), cm(0), st, st, cm(0)] + [_HBM] * k,
        out_specs=[cm(0)] * 6 + [_HBM] * k, out_shape=[jax.ShapeDtypeStruct((s, D), F32)] * 6 + comm.out_shape,
        scratch_shapes=[pltpu.VMEM((SCAN_PAIRS, 128, 128), F32)] + comm.sems,
        compiler_params=_cparams(("arbitrary",)),
    )(zs, lw, km, zs, aa, bb, s0s, ts, dy, *comm.ins)
    return res[:6], res[6:]


_HBM = pl.BlockSpec(memory_space=pltpu.HBM)


def _me():
    return lax.axis_index("x"), lax.axis_index("y"), lax.axis_index("c")


def _allgather8(src, name):
    def body(src_ref, out_ref, ssem, rsem, lsem):
        x, y, c = _me()
        me = 4 * x + 2 * y + c
        local = pltpu.make_async_copy(src_ref, out_ref.at[me], lsem)
        local.start()
        peers = []
        for k in range(1, 8):
            peers.append(((1 - x) if k & 4 else x, (1 - y) if k & 2 else y, (1 - c) if k & 1 else c))
        sends = []
        for k, peer in enumerate(peers):
            cp = pltpu.make_async_remote_copy(src_ref, out_ref.at[me], ssem.at[k], rsem.at[k], device_id=peer,
                                              device_id_type=MESH)
            cp.start()
            sends.append(cp)
        for k, (px, py, pc) in enumerate(peers):
            pltpu.make_async_remote_copy(src_ref, out_ref.at[4 * px + 2 * py + pc], ssem.at[k], rsem.at[k],
                                         device_id=(px, py, pc), device_id_type=MESH).wait_recv()
        for cp in sends:
            cp.wait_send()
        local.wait()

    return _pcall(
        body, name=name, in_specs=[_HBM], out_specs=_HBM, out_shape=jax.ShapeDtypeStruct((8,) + src.shape, src.dtype),
        scratch_shapes=[pltpu.SemaphoreType.DMA((7,)), pltpu.SemaphoreType.DMA((7,)), pltpu.SemaphoreType.DMA],
    )(src)


def _other_chips(x, y):
    return [(1 - x, y), (x, 1 - y), (1 - x, 1 - y)]


def _remote(src, dst, ssem, rsem, to):
    return pltpu.make_async_remote_copy(src, dst, ssem, rsem, device_id=to, device_id_type=MESH)


class _GatherWeights:
    def __init__(self, shards):
        self.ins = list(shards)
        n = self.n = len(shards)
        self.out_shape = [jax.ShapeDtypeStruct((4,) + t.shape, t.dtype) for t in shards]
        self.sems = [pltpu.SemaphoreType.DMA((6 * n,)), pltpu.SemaphoreType.DMA((6 * n,)),
                     pltpu.SemaphoreType.DMA((n,)), pltpu.SemaphoreType.DMA((n,))]

    def _copies(self, srcs, outs, sems):
        ssem, rsem, lsem, osem = sems
        x, y, c = _me()
        me = 2 * x + y
        own, ici, landed, passed, passed_in = [], [], [], [], []
        for a in range(self.n):
            h = self.ins[a].shape[0] // 2
            mine, other = pl.ds(c * h, h), pl.ds((1 - c) * h, h)
            own.append(_remote(srcs[a], outs[a].at[me], lsem.at[a], osem.at[a], (x, y, 1 - c)))
            for k, (px, py) in enumerate(_other_chips(x, y)):
                s1, r1, s2, r2 = ssem.at[6 * a + k], rsem.at[6 * a + k], ssem.at[6 * a + 3 + k], rsem.at[6 * a + 3 + k]
                got, got_sib = outs[a].at[2 * px + py, mine], outs[a].at[2 * px + py, other]
                ici.append(_remote(srcs[a].at[mine], outs[a].at[me, mine], s1, r1, (px, py, c)))
                landed.append(_remote(got, got, s1, r1, (px, py, c)))
                passed.append(_remote(got, got, s2, r2, (x, y, 1 - c)))
                passed_in.append(_remote(got_sib, got_sib, s2, r2, (x, y, 1 - c)))
        return own, ici, landed, passed, passed_in

    def first(self, srcs, outs, sems):
        own, ici, _, _, _ = self._copies(srcs, outs, sems)
        for cp in own + ici:
            cp.start()

    def mid(self, srcs, outs, sems):
        _, _, landed, passed, _ = self._copies(srcs, outs, sems)
        for arrived, onward in zip(landed, passed):
            arrived.wait_recv()
            onward.start()

    def last(self, srcs, outs, sems):
        own, ici, _, passed, passed_in = self._copies(srcs, outs, sems)
        for cp in passed_in:
            cp.wait_recv()
        for cp in ici + passed:
            cp.wait_send()
        for cp in own:
            cp.wait()


class _ScatterToChips:
    def __init__(self, parts):
        self.ins = list(parts)
        n = self.n = len(parts)
        self.out_shape = [jax.ShapeDtypeStruct(t.shape, t.dtype) for t in parts]
        self.sems = [pltpu.SemaphoreType.DMA((3 * n,)), pltpu.SemaphoreType.DMA((3 * n,)), pltpu.SemaphoreType.DMA((n,))]

    def _copies(self, srcs, outs, sems):
        ssem, rsem, lsem = sems
        x, y, c = _me()
        me = 2 * x + y
        own, out, landed = [], [], []
        for a in range(self.n):
            own.append(pltpu.make_async_copy(srcs[a].at[me], outs[a].at[me], lsem.at[a]))
            for k, (px, py) in enumerate(_other_chips(x, y)):
                dst = outs[a].at[2 * px + py]
                out.append(_remote(srcs[a].at[2 * px + py], outs[a].at[me], ssem.at[3 * a + k], rsem.at[3 * a + k],
                                   (px, py, c)))
                landed.append(_remote(dst, dst, ssem.at[3 * a + k], rsem.at[3 * a + k], (px, py, c)))
        return own, out, landed

    def first(self, srcs, outs, sems):
        own, out, _ = self._copies(srcs, outs, sems)
        for cp in own + out:
            cp.start()

    mid = None

    def last(self, srcs, outs, sems):
        own, out, landed = self._copies(srcs, outs, sems)
        for cp in landed:
            cp.wait_recv()
        for cp in own:
            cp.wait()
        for cp in out:
            cp.wait_send()


def _run_comm(comm, name):
    n = comm.n

    def body(*refs):
        srcs, outs, sems = refs[:n], refs[n:2 * n], refs[2 * n:]
        comm.first(srcs, outs, sems)
        if comm.mid is not None:
            comm.mid(srcs, outs, sems)
        comm.last(srcs, outs, sems)

    return _pcall(body, name=name, in_specs=[_HBM] * n, out_specs=[_HBM] * n, out_shape=comm.out_shape,
                  scratch_shapes=comm.sems)(*comm.ins)


class _NoComm:
    n, ins, out_shape, sems, mid = 0, [], [], [], None

    def first(self, srcs, outs, sems):
        pass

    def last(self, srcs, outs, sems):
        pass


_NOTHING = _NoComm()


class _SiblingHalves:
    mid = None

    def __init__(self, grads):
        self.ins = list(grads)
        n = self.n = len(grads)
        self.out_shape = [jax.ShapeDtypeStruct((4, t.shape[1] // 2, t.shape[2]), t.dtype) for t in grads]
        self.sems = [pltpu.SemaphoreType.DMA((n,)), pltpu.SemaphoreType.DMA((n,))]

    def _copies(self, srcs, outs, sems):
        ssem, rsem = sems
        x, y, c = _me()
        copies = []
        for a in range(self.n):
            h = self.ins[a].shape[1] // 2
            copies.append(_remote(srcs[a].at[:, pl.ds((1 - c) * h, h)], outs[a], ssem.at[a], rsem.at[a], (x, y, 1 - c)))
        return copies

    def first(self, srcs, outs, sems):
        for cp in self._copies(srcs, outs, sems):
            cp.start()

    def last(self, srcs, outs, sems):
        for cp in self._copies(srcs, outs, sems):
            cp.wait()


def _reduce_finish(reds, name):
    n = len(reds)

    def body(*refs):
        outs = refs[n:2 * n]
        ssem, rsem = refs[2 * n:]
        x, y, c = _me()
        copies = []
        for a in range(n):
            h = reds[a].shape[0] // 2
            mine = outs[a].at[pl.ds(c * h, h)]
            copies.append(_remote(mine, mine, ssem.at[a], rsem.at[a], (x, y, 1 - c)))
        for cp in copies:
            cp.start()
        for a in range(n):
            h = reds[a].shape[0] // 2
            dst = outs[a].at[pl.ds((1 - c) * h, h)]
            _remote(dst, dst, ssem.at[a], rsem.at[a], (x, y, 1 - c)).wait_recv()
        for cp in copies:
            cp.wait_send()

    return _pcall(
        body, name=name, in_specs=[_HBM] * n, out_specs=[_HBM] * n,
        out_shape=[jax.ShapeDtypeStruct(t.shape, t.dtype) for t in reds],
        input_output_aliases={a: a for a in range(n)},
        scratch_shapes=[pltpu.SemaphoreType.DMA((n,)), pltpu.SemaphoreType.DMA((n,))],
    )(*reds)


def _half_sum(fn, full, halves, out_full, out_dtype, core, name):
    p, h, c = (halves[0].shape if halves else (full[0].shape[0], full[0].shape[1] // 2, full[0].shape[2]))
    br = _div(h, max(16, (1 << 19) // (p * c)), 16)
    nb = h // br
    mine3 = pl.BlockSpec((p, br, c), lambda i, core_ref: (0, core_ref[0] * nb + i, 0))
    half3 = pl.BlockSpec((p, br, c), lambda i, core_ref: (0, i, 0))

    def body(core_ref, *refs):
        refs[-1][...] = fn(*[t[...].astype(F32) for t in refs[:-1]]).astype(out_dtype)

    if out_full:
        out_spec = pl.BlockSpec((br, c), lambda i, core_ref: (core_ref[0] * nb + i, 0))
        out_shape = jax.ShapeDtypeStruct((2 * h, c), out_dtype)
    else:
        out_spec, out_shape = half3, jax.ShapeDtypeStruct((p, h, c), out_dtype)
    return _pcall(
        body, name=name,
        grid_spec=pltpu.PrefetchScalarGridSpec(
            num_scalar_prefetch=1, grid=(nb,), in_specs=[mine3] * len(full) + [half3] * len(halves),
            out_specs=out_spec),
        out_shape=out_shape, compiler_params=_cparams(("parallel",)),
    )(core, *full, *halves)


def _ada_fwd(c_all, w, b):
    def body(c_ref, w_ref, b_ref, o_ref):
        o_ref[...] = jnp.dot(c_ref[...], w_ref[...], precision=HI, preferred_element_type=F32) + b_ref[...]

    return _pcall(body, name="ada_fwd", out_shape=jax.ShapeDtypeStruct((c_all.shape[0], w.shape[1]), F32),
                  compiler_params=pltpu.CompilerParams(vmem_limit_bytes=VMEM_LIMIT))(c_all, w, b)


def _ada_bwd(c_all_t, d):
    def body(c_ref, d_ref, o_ref):
        o_ref[...] = jnp.dot(c_ref[...], d_ref[...], precision=HI, preferred_element_type=F32)

    return _pcall(body, name="ada_bwd", out_shape=jax.ShapeDtypeStruct((c_all_t.shape[0], d.shape[1]), F32),
                  compiler_params=pltpu.CompilerParams(vmem_limit_bytes=VMEM_LIMIT))(c_all_t, d)


def _sum_lead(x, name):
    p, r, n = x.shape
    br = _div(r, 512, 8)

    def body(x_ref, o_ref):
        acc = x_ref[0]
        for j in range(1, p):
            acc = acc + x_ref[j]
        o_ref[...] = acc

    return _pcall(
        body, name=name, grid=(r // br,), in_specs=[pl.BlockSpec((p, br, n), lambda i: (0, i, 0))],
        out_specs=pl.BlockSpec((br, n), lambda i: (i, 0)), out_shape=jax.ShapeDtypeStruct((r, n), F32),
        compiler_params=_cparams(("parallel",)),
    )(x)


def _adamw(w, g, m, v, name):
    shape = w.shape
    cols = shape[-1]
    w2, g2, m2, v2 = [t.reshape(-1, cols) for t in (w, g, m, v)]
    rows = w2.shape[0]
    pref = max(8, (1 << 19) // cols // 8 * 8)
    br = _div(rows, pref, 8)
    if rows // br > 64:
        br = pref
    outs = _rows_fwd(_f_adamw, [(t, cols, 0) for t in (w2, g2, m2, v2)], [], [(cols, F32)] * 3, name=name, br=br)
    return [o.reshape(shape) for o in outs]


_BIG = (("w_in", 1), ("w_up", 1), ("w_down", 0), ("w_o", 0), ("w_rwkv_out", 0), ("w_att_out", 1), ("w2", 1), ("a2", 1),
        ("g2", 1))


_NEEDED_FIRST = ("w_in", "w_att_out", "w2", "a2", "g2")
_NEEDED_LATER = ("w_up", "w_down", "w_o", "w_rwkv_out")
_DONE_EARLY = ("w_up", "w_down", "w_o", "w_rwkv_out", "w_att_out")
_DONE_LATE = ("w_in", "w2", "a2", "g2")


def _cols_joined(t):
    return jnp.concatenate([t[j] for j in range(4)], axis=1)


def _cols_split(t):
    n = t.shape[1] // 4
    return jnp.stack([t[:, j * n:(j + 1) * n] for j in range(4)])


W_IN_SHARD = (N_ATT + N_RW + N_GATE) // 4
W_IN_PAD = 2560


def _row_window(parts, lo, hi):
    out, pos = [], 0
    for t, w in parts:
        a, b = max(lo, pos), min(hi, pos + w)
        if a < b:
            out.append(t[a - pos:b - pos])
        pos += w
    return out[0] if len(out) == 1 else jnp.concatenate(out, axis=0)


def _rows_joined(t):
    return t.reshape(4 * t.shape[1], t.shape[2])


def _rows_split(t):
    return t.reshape(4, t.shape[0] // 4, t.shape[1])


def _step_to_scan(x, tgt, ada, wts):
    sh1, sc1, gt1, sh2, sc2, gt2 = ada
    br = 256
    grp = lax.broadcasted_iota(jnp.int32, (D, 128), 0) // 64 == lax.broadcasted_iota(jnp.int32, (D, 128), 1)
    e = grp.astype(F32)
    et = e.T
    w_in = [(wts["w_in"][j], W_IN_SHARD) for j in range(4)]
    w_att = _row_window(w_in, 0, N_ATT)
    w_rw = jnp.concatenate([_row_window(w_in, N_ATT, N_ATT + N_RW), jnp.zeros((N_RWP - N_RW, D), BF16)], axis=0)
    w_gate = _row_window(w_in, N_ATT + N_RW, N_ATT + N_RW + N_GATE)
    mu = jnp.pad(wts["mu_shift"], ((0, 0), (0, N_RWP - N_RW)))
    wl = jnp.zeros((N_LORA, 3 * D), F32)
    wl = wl.at[0:64, 0:D].set(_cols_joined(wts["w2"]).astype(F32))
    wl = wl.at[64:128, D:2 * D].set(_cols_joined(wts["a2"]).astype(F32))
    wl = wl.at[128:288, 2 * D:3 * D].set(_cols_joined(wts["g2"]).astype(F32))
    pre1_c = [wts["norm1_w"], sc1, sh1]
    (h1,) = _rows_fwd(_f_pre, [(x, D, 0)], pre1_c, [(D, BF16), None], name="pre1_fwd", br=2 * br)
    att_in = _mm(h1, w_att, tb=True, name="mm_att_in")
    z = _mm(h1, w_rw, tb=True, out_dtype=BF16, name="mm_rw_in")
    gate_in = _mm(h1, w_gate, tb=True, out_dtype=BF16, name="mm_gate_in")
    att_o, att_l = [], []
    for g, (_, dil) in enumerate(ATT_PATTERNS):
        o, l = _att_fwd(att_in, g, dil)
        att_o.append(o)
        att_l.append(l)
    comb_rows = [(t, ATT_WIDTH, 0) for t in att_o + att_l]
    (att,) = _rows_fwd(_f_comb, comb_rows, [], [(ATT_WIDTH, BF16)], name="comb_fwd", br=2 * br)
    w_ao = _cols_joined(wts["w_att_out"])
    y_att = _mm(att, w_ao, out_dtype=BF16, name="mm_att_out")
    rwpre_c = [wts["w0"], wts["a0"], wts["k_k"], wts["k_a"], wl, e, et]

    def shift_and_rwpre(zz, *rest):
        consts, mu_row, before = rest[:-2], rest[-2], rest[-1]
        last = jnp.sum(jnp.where(lax.broadcasted_iota(jnp.int32, before.shape, 0) == HALO - 1, before, 0.0), axis=0,
                       keepdims=True)
        row = lax.broadcasted_iota(jnp.int32, zz.shape, 0)
        zprev = jnp.where(row == 0, last, pltpu.roll(zz, 1, 0))
        shifted = zz + (zprev - zz) * mu_row
        return (shifted,) + tuple(_f_rwpre(shifted, *consts))

    zs, lw, km, aa, bb, gg = _rows_fwd(
        shift_and_rwpre, [(z, N_RWP, 0)], rwpre_c + [mu],
        [(N_RWP, F32), None, (D, F32), (D, F32), None, (D, F32), (D, F32), (D, F32)], name="rwpre_fwd", br=br, halo=0)
    return dict(x=x, tgt=tgt, wts=wts, br=br, e=e, et=et, gt1=gt1, sc2=sc2, sh2=sh2, gt2=gt2, w_att=w_att, w_rw=w_rw,
                w_ao=w_ao,
                w_gate=w_gate, mu=mu, pre1_c=pre1_c, h1=h1, att_in=att_in, z=z, gate_in=gate_in, comb_rows=comb_rows,
                att=att, y_att=y_att, zs=zs, rwpre_c=rwpre_c, lw=lw, km=km, aa=aa, bb=bb, gg=gg)


def _step_between_scans(st, y_raw, late):
    x, tgt, wts, br, e, et = st["x"], st["tgt"], st["wts"], st["br"], st["e"], st["et"]
    zs, km, gg, gate_in, y_att, att = st["zs"], st["km"], st["gg"], st["gate_in"], st["y_att"], st["att"]
    comb_rows, att_in = st["comb_rows"], st["att_in"]
    gt1, sc2, sh2, gt2 = st["gt1"], st["sc2"], st["sh2"], st["gt2"]
    w_up, w_ao = late["w_up"], st["w_ao"]
    w_down, w_o, w_ro = _rows_joined(late["w_down"]), _rows_joined(late["w_o"]), _rows_joined(late["w_rwkv_out"])
    post_rows = [(y_raw, D, 0), (zs, D, 0), (zs, D, 2), (km, D, 0), (gg, D, 0)]
    post_c = [wts["lnx_w"], wts["lnx_b"], wts["r_k"], e, et]
    (rw_out,) = _rows_fwd(_f_rwpost, post_rows, post_c, [(D, BF16)], name="rwpost_fwd", br=br)
    y_rw = _mm(rw_out, w_ro, out_dtype=BF16, name="mm_rw_out")
    mix_rows = [(gate_in, N_GATE, 0), (y_att, D, 0), (y_rw, D, 0)]
    (mix,) = _rows_fwd(_f_mix, mix_rows, [wts["b_gate"]], [(D, BF16)], name="mix_fwd", br=2 * br)
    o = _mm(mix, w_o, out_dtype=BF16, name="mm_o")
    pre2_c = [gt1, wts["norm2_w"], sc2, sh2]
    x1, h2 = _rows_fwd(_f_pre2, [(x, D, 0), (o, D, 0)], pre2_c, [(D, F32), (D, BF16)], name="pre2_fwd", br=2 * br)
    u = _mm(h2, w_up, b_chip=True, name="mm_up")
    act = _conv_fwd(u, wts["conv_w"], wts["conv_b"])
    f = _mm(act, w_down, out_dtype=BF16, name="mm_down")
    fin_rows = [(x1, D, 0), (f, D, 0), (tgt, D, 0)]
    fin_c = [gt2, wts["norm_f_w"]]

    def fin_fwd(*a):
        (l,) = _f_fin(*a)
        return (jnp.broadcast_to(jnp.sum(l, axis=0, keepdims=True), (8, 128)),)

    (loss_acc,) = _rows_fwd(fin_fwd, fin_rows, fin_c, [], name="fin_fwd", br=2 * br, acc_shape=(8, 128))

    gw = {}
    dx1a, df, d_gt2, gw["norm_f_w"] = _rows_bwd(
        _f_fin, fin_rows, fin_c, [[]], wrt_rows=[0, 1], wrt_consts=[0, 1], drow_dtypes=[F32, BF16],
        name="fin_bwd", br=2 * br, unit_cot=True)
    dact = _mm(df, w_down, tb=True, name="mm_dact")
    gw["w_down"] = _rows_split(_mm(act, df, ta=True, out_dtype=BF16, name="mm_dw_down"))
    du, gw["conv_w"], gw["conv_b"] = _conv_bwd(u, wts["conv_w"], wts["conv_b"], dact)
    dh2 = _mm(du, w_up, tb=True, b_chip=True, out_dtype=BF16, name="mm_dh2")
    gw["w_up"] = _mm(h2, du, ta=True, out_chip=True, out_dtype=BF16, name="mm_dw_up")
    dxa, do, d_gt1, gw["norm2_w"], d_sc2, d_sh2 = _rows_bwd(
        _f_pre2, [(x, D, 0), (o, D, 0)], pre2_c, [[(dx1a, D, 0)], [(dh2, D, 0)]], wrt_rows=[0, 1],
        wrt_consts=[0, 1, 2, 3], drow_dtypes=[F32, BF16], name="pre2_bwd", br=2 * br)
    dmix = _mm(do, w_o, tb=True, out_dtype=BF16, name="mm_dmix")
    gw["w_o"] = _rows_split(_mm(mix, do, ta=True, out_dtype=BF16, name="mm_dw_o"))
    dgate, dya, dyr, gw["b_gate"] = _rows_bwd(
        _f_mix, mix_rows, [wts["b_gate"]], [[(dmix, D, 0)]], wrt_rows=[0, 1, 2], wrt_consts=[0],
        drow_dtypes=[BF16] * 3, name="mix_bwd", br=2 * br)
    datt = _mm(dya, w_ao, tb=True, out_dtype=BF16, name="mm_datt")
    gw["w_att_out"] = _mm(att, dya, ta=True, out_chip=True, out_dtype=BF16, name="mm_dw_att_out")
    drw = _mm(dyr, w_ro, tb=True, out_dtype=BF16, name="mm_drw")
    gw["w_rwkv_out"] = _rows_split(_mm(rw_out, dyr, ta=True, out_dtype=BF16, name="mm_dw_rw_out"))
    dcomb = _rows_bwd(_f_comb, comb_rows, [], [[(datt, ATT_WIDTH, 0)]], wrt_rows=list(range(6)), wrt_consts=[],
                      drow_dtypes=[F32] * 6, name="comb_bwd", br=2 * br)
    datt_in = None
    for g, (_, dil) in enumerate(ATT_PATTERNS):
        datt_in = _att_bwd(att_in, g, dil, dcomb[g], dcomb[3 + g], datt_in)
    dy_raw, dr_p, dv_p, dkm_p, dgg, gw["lnx_w"], gw["lnx_b"], gw["r_k"], *recv_early = _rows_bwd(
        _f_rwpost, post_rows, post_c, [[(drw, D, 0)]], wrt_rows=[0, 1, 2, 3, 4], wrt_consts=[0, 1, 2],
        drow_dtypes=[F32] * 5, name="rwpost_bwd", br=br, comm=_SiblingHalves([gw[n] for n in _DONE_EARLY]))
    st.update(loss=loss_acc[0, 0], gw=gw, dxa=dxa, dgate=dgate, datt_in=datt_in,
              dy_raw=dy_raw, dr_p=dr_p, dv_p=dv_p, dkm_p=dkm_p, dgg=dgg, d_ada_late=(d_gt1, d_sh2, d_sc2, d_gt2),
              recv_early=recv_early)
    return st


def _chip_parts(grads, recv, names, core):
    return [_half_sum(lambda a, b: a + b, [g], [r], False, BF16, core, "reduce_add2_" + n)
            for g, r, n in zip(grads, recv, names)]


def _step_after_scan(st, scan_grads, core):
    x, br, gw, h1, zs = st["x"], st["br"], st["gw"], st["h1"], st["zs"]
    dr_s, dlw, dkm_s, dv_s, daa, dbb = scan_grads
    pre_cots = [[(st["dr_p"], D, 0), (dr_s, D, 0)], [(dlw, D, 0)], [(st["dkm_p"], D, 0), (dkm_s, D, 0)],
                [(st["dv_p"], D, 0), (dv_s, D, 0)], [(daa, D, 0)], [(dbb, D, 0)], [(st["dgg"], D, 0)]]
    dz, dmu, gw["w0"], gw["a0"], gw["k_k"], gw["k_a"], dwl = _rwpre_shift_bwd(
        zs, st["z"], st["mu"], st["rwpre_c"], pre_cots, br=128)
    gw["w2"], gw["a2"] = _cols_split(dwl[0:64, 0:D]), _cols_split(dwl[64:128, D:2 * D])
    gw["g2"] = _cols_split(dwl[128:288, 2 * D:3 * D])
    gw["mu_shift"] = dmu[:, :N_RW]
    datt_in, dgate = st["datt_in"], st["dgate"]
    dw_in = [(_mm(datt_in, h1, ta=True, out_dtype=BF16, name="mm_dw_att"), N_ATT),
             (_mm(dz, h1, ta=True, out_dtype=BF16, name="mm_dw_rw"), N_RW),
             (_mm(dgate, h1, ta=True, out_dtype=BF16, name="mm_dw_gate"), N_GATE)]
    slabs = []
    for j in range(4):
        slabs += [_row_window(dw_in, j * W_IN_SHARD, (j + 1) * W_IN_SHARD), jnp.zeros((W_IN_PAD - W_IN_SHARD, D), BF16)]
    gw["w_in"] = jnp.concatenate(slabs, axis=0).reshape(4, W_IN_PAD, D)
    late = [gw[n] for n in _DONE_LATE]
    parts = _chip_parts(late, _run_comm(_SiblingHalves(late), "reduce_sib_late"), _DONE_LATE, core)
    dh1, slots_late = _mm_sum([(datt_in, st["w_att"]), (dz, st["w_rw"]), (dgate, st["w_gate"])],
                              comm=_ScatterToChips(parts), name="mm_dh1")
    grad_x, gw["norm1_w"], d_sc1, d_sh1 = _rows_bwd(
        _f_pre, [(x, D, 0)], st["pre1_c"], [[(dh1, D, 0)], [(st["dxa"], D, 0)]], wrt_rows=[0], wrt_consts=[0, 1, 2],
        drow_dtypes=[F32], name="pre1_bwd", br=2 * br)
    d_gt1, d_sh2, d_sc2, d_gt2 = st["d_ada_late"]
    return st["loss"], grad_x, (d_sh1, d_sc1, d_gt1, d_sh2, d_sc2, d_gt2), gw, slots_late


_SMALL = ("b_ada", "norm1_w", "b_gate", "mu_shift", "w0", "a0", "k_k", "k_a", "r_k", "lnx_w", "lnx_b", "norm2_w",
          "conv_b", "norm_f_w")
_NAMES = ("w_ada", "b_ada", "norm1_w", "w_in", "b_gate", "mu_shift", "w0", "w2", "a0", "a2", "g2", "k_k", "k_a", "r_k",
          "lnx_w", "lnx_b", "w_att_out", "w_rwkv_out", "w_o", "norm2_w", "w_up", "conv_w", "conv_b", "w_down",
          "norm_f_w")


def kernel(x, c, w_ada, b_ada, norm1_w, w_in, b_gate, mu_shift, w0, w2, a0, a2, g2, k_k, k_a, r_k, lnx_w, lnx_b, w_att_out, w_rwkv_out, w_o, norm2_w, w_up, conv_w, conv_b, w_down, norm_f_w, loss_target, m_w_ada, m_b_ada, m_norm1_w, m_w_in, m_b_gate, m_mu_shift, m_w0, m_w2, m_a0, m_a2, m_g2, m_k_k, m_k_a, m_r_k, m_lnx_w, m_lnx_b, m_w_att_out, m_w_rwkv_out, m_w_o, m_norm2_w, m_w_up, m_conv_w, m_conv_b, m_w_down, m_norm_f_w, v_w_ada, v_b_ada, v_norm1_w, v_w_in, v_b_gate, v_mu_shift, v_w0, v_w2, v_a0, v_a2, v_g2, v_k_k, v_k_a, v_r_k, v_lnx_w, v_lnx_b, v_w_att_out, v_w_rwkv_out, v_w_o, v_norm2_w, v_w_up, v_conv_w, v_conv_b, v_w_down, v_norm_f_w):
    args = dict(locals())
    p, pm, pv = {}, {}, {}
    for name in _NAMES:
        for dst, key in ((p, name), (pm, "m_" + name), (pv, "v_" + name)):
            t = args[key]
            if name == "w_in":
                dst[name] = jnp.swapaxes(t, 1, 2)[0]
            else:
                dst[name] = t.reshape(1, -1) if name in ("r_k", "norm_f_w") else t.reshape(t.shape[-2], t.shape[-1])
    xi, yi, ci = _me()
    chip = 2 * xi + yi
    dev = 4 * xi + 2 * yi + ci
    x2, tgt = x[0], loss_target[0]

    n_cw = 3 * (2 * D_FF // 4)
    vec = jnp.concatenate([c.reshape(-1), p["conv_w"].reshape(-1), jnp.zeros((8 * D - D - n_cw,), F32)]).reshape(8, D)
    g0 = _allgather8(vec, "gather_c").reshape(8, 8 * D)
    c_all = g0[:, :D]
    conv_w_full = jnp.concatenate([g0[2 * j, D:D + n_cw].reshape(3, -1) for j in range(4)], axis=1)
    n_ada = 6 * D // 4
    b_ada_sh = lax.dynamic_slice(p["b_ada"], (0, chip * n_ada), (1, n_ada))
    ada_sh = _ada_fwd(c_all, p["w_ada"], b_ada_sh)
    ga = _allgather8(ada_sh, "gather_ada")
    ada_all = jnp.concatenate([ga[2 * j] for j in range(4)], axis=1)
    ada_row = lax.dynamic_slice(ada_all, (dev, 0), (1, 6 * D))
    ada = [ada_row[:, j * D:(j + 1) * D] for j in range(6)]

    big = [n for n, _ in _BIG]
    shard = {n: p[n].astype(BF16) for n in big}
    shard["w_in"] = jnp.pad(shard["w_in"], ((0, W_IN_PAD - W_IN_SHARD), (0, 0)))
    wts = dict(zip(_NEEDED_FIRST, _run_comm(_GatherWeights([shard[n] for n in _NEEDED_FIRST]), "gather_w")))
    for n in _SMALL:
        wts[n] = p[n]
    wts["conv_w"] = conv_w_full
    core = ci.reshape(1).astype(jnp.int32)

    st = _step_to_scan(x2, tgt, ada, wts)
    y_raw, s0s, inverses, late = _scan_fwd(st["zs"], st["lw"], st["km"], st["aa"], st["bb"],
                                           _GatherWeights([shard[n] for n in _NEEDED_LATER]))
    st = _step_between_scans(st, y_raw, dict(zip(_NEEDED_LATER, late)))
    early = _chip_parts([st["gw"][n] for n in _DONE_EARLY], st["recv_early"], _DONE_EARLY, core)
    scan_grads, slots_early = _scan_bwd(st["zs"], st["lw"], st["km"], st["aa"], st["bb"], s0s, inverses,
                                        st["dy_raw"], _ScatterToChips(early))
    loss_part, grad_x, d_ada, gw, slots_late = _step_after_scan(st, scan_grads, core)

    small = [jnp.concatenate(d_ada, axis=1)] + [gw[n] for n in _SMALL[1:]] + [gw["conv_w"], loss_part.reshape(1, 1)]
    sizes = [t.size for t in small]
    flat = jnp.concatenate([t.reshape(-1) for t in small])
    npad = (-flat.shape[0]) % (8 * D)
    srows = (flat.shape[0] + npad) // D
    flat = jnp.concatenate([flat, jnp.zeros((npad,), F32)]).reshape(srows, D)
    parts = _allgather8(flat, "gather_small")
    tot = _sum_lead(parts, "sum_small").reshape(-1)
    pieces, pos = [], 0
    for sz in sizes:
        pieces.append(tot[pos:pos + sz])
        pos += sz
    grads = {}
    for n, piece in zip(_SMALL, pieces[:len(_SMALL)]):
        grads[n] = piece.reshape(p[n].shape)
    conv_w_grad = pieces[len(_SMALL)].reshape(3, 2 * D_FF)
    grads["conv_w"] = lax.dynamic_slice(conv_w_grad, (0, chip * (n_cw // 3)), (3, n_cw // 3))
    loss = pieces[-1][0]
    d_ada_all = parts[:, :6].reshape(8, 6 * D)
    grads["w_ada"] = _ada_bwd(c_all.T, lax.dynamic_slice(d_ada_all, (0, chip * n_ada), (8, n_ada)))

    order = _DONE_EARLY + _DONE_LATE
    reds = [_half_sum(lambda t: t[0] + t[1] + t[2] + t[3], [], [t], True, F32, core, "reduce_add4_" + n)
            for n, t in zip(order, list(slots_early) + list(slots_late))]
    for n, g in zip(order, _reduce_finish(reds, "reduce_sib2")):
        grads[n] = g

    outs_g, outs_d, outs_m, outs_v = [], [], [], []
    grads["w_in"] = grads["w_in"][:W_IN_SHARD]
    for name in _NAMES:
        g = grads[name]
        d, m, v = _adamw(p[name], g, pm[name], pv[name], "adamw_" + name)
        shape = args[name].shape
        for outs, t in ((outs_g, g), (outs_d, d), (outs_m, m), (outs_v, v)):
            outs.append(jnp.swapaxes(t[None], 1, 2) if name == "w_in" else t.reshape(shape))
    return (loss, grad_x.reshape(x.shape), *outs_g, *outs_d, *outs_m, *outs_v)
```

```python
import functools
import math

import jax
import jax.numpy as jnp
from jax import lax
from jax.experimental import pallas as pl
from jax.experimental.pallas import tpu as pltpu

F32 = jnp.float32
BF16 = jnp.bfloat16
HI = lax.Precision.HIGHEST
MESH = pl.DeviceIdType.MESH

D = 1024
ATT_PATTERNS = ((128, 1), (512, 4), (2048, 16))
ATT_BLOCK = 128
ATT_WIDTH = 512
N_ATT = 3 * 3 * ATT_WIDTH
N_RW = 3 * D + 64 + 64 + 160
N_RWP = 3456
N_LORA = N_RWP - 3 * D
N_GATE = 2 * D
D_FF = 2816
RMS_EPS = 1e-6
GN_EPS = 64e-5
SCAN_CHUNK = 64
SCAN_PAIRS = 8
NEG = -1e30
VMEM_LIMIT = 48 * 1024 * 1024
HALO = 16

ADAM_LR, ADAM_B1, ADAM_B2, ADAM_EPS, ADAM_WD, ADAM_STEP = 0.001, 0.9, 0.999, 1e-08, 0.01, 10


def _pcall(body, **kw):
    return pl.pallas_call(body, **kw)


def _cparams(sem):
    return pltpu.CompilerParams(dimension_semantics=sem, vmem_limit_bytes=VMEM_LIMIT)


def _div(n, pref, mult):
    best = None
    d = mult
    while d <= min(n, pref):
        if n % d == 0:
            best = d
        d += mult
    return best if best else n


def _dg(a, b, ca, cb):
    return lax.dot_general(a.astype(BF16), b.astype(BF16), (((ca,), (cb,)), ((), ())), preferred_element_type=F32)


@jax.custom_vjp
def _nn(a, b):
    return _dg(a, b, 1, 0)


@jax.custom_vjp
def _nt(a, b):
    return _dg(a, b, 1, 1)


@jax.custom_vjp
def _tn(a, b):
    return _dg(a, b, 0, 0)


_nn.defvjp(lambda a, b: (_nn(a, b), (a, b)), lambda res, g: (_nt(g, res[1]), _tn(res[0], g)))
_nt.defvjp(lambda a, b: (_nt(a, b), (a, b)), lambda res, g: (_nn(g, res[1]), _tn(g, res[0])))
_tn.defvjp(lambda a, b: (_tn(a, b), (a, b)), lambda res, g: (_nt(res[1], g), _nn(res[0], g)))


def _bdg(a, b, ca, cb):
    return lax.dot_general(a.astype(BF16), b.astype(BF16), (((ca,), (cb,)), ((0,), (0,))), preferred_element_type=F32)


@jax.custom_vjp
def _bnn(a, b):
    return _bdg(a, b, 2, 1)


@jax.custom_vjp
def _bnt(a, b):
    return _bdg(a, b, 2, 2)


@jax.custom_vjp
def _btn(a, b):
    return _bdg(a, b, 1, 1)


_bnn.defvjp(lambda a, b: (_bnn(a, b), (a, b)), lambda res, g: (_bnt(g, res[1]), _btn(res[0], g)))
_bnt.defvjp(lambda a, b: (_bnt(a, b), (a, b)), lambda res, g: (_bnn(g, res[1]), _btn(g, res[0])))
_btn.defvjp(lambda a, b: (_btn(a, b), (a, b)), lambda res, g: (_bnt(res[1], g), _bnn(res[0], g)))


def _hsum_impl(x, e, et):
    eb, etb = e.astype(BF16), et.astype(BF16)
    s = jnp.dot(x.astype(BF16), eb, preferred_element_type=F32)
    return jnp.dot(s.astype(BF16), etb, preferred_element_type=F32)


@jax.custom_vjp
def _hsum(x, e, et):
    return _hsum_impl(x, e, et)


_hsum.defvjp(lambda x, e, et: (_hsum_impl(x, e, et), (e, et)),
             lambda res, g: (_hsum_impl(g, res[0], res[1]), jnp.zeros_like(res[0]), jnp.zeros_like(res[1])))


def _mm(a, b, *, ta=False, tb=False, out_dtype=F32, add=None, b_chip=False, out_chip=False, comm=None, name):
    riding = _NOTHING if comm is None else comm
    nc = riding.n
    if ta:
        kdim, m = a.shape
    else:
        m, kdim = a.shape
    if b_chip:
        n = b.shape[1] if tb else 4 * b.shape[2]
    else:
        n = b.shape[0] if tb else b.shape[1]
    tm, tn, tk = _div(m, 1536, 128), _div(n, 1536, 128), _div(kdim, 2048 if ta else 1408, 128)
    if b_chip and tb:
        tk = kdim // 4
    if (b_chip and not tb) or out_chip:
        tn = n // 4
    nk = kdim // tk
    ca, cb = (0 if ta else 1), (1 if tb else 0)

    nin = 2 if add is None else 3
    gi, gj = m // tm, n // tn

    def body(*refs):
        a_ref, b_ref = refs[0], refs[1]
        add_ref = None if add is None else refs[2]
        o_ref = refs[nin + nc]
        step = (pl.program_id(0) * gj + pl.program_id(1)) * nk + pl.program_id(2)
        before, after = _comm_phases(riding, refs[nin:nin + nc] + refs[nin + nc + 1:nin + 2 * nc + 1]
                                     + refs[nin + 2 * nc + 1 + (nk > 1):], gi * gj * nk, step)
        before()
        part = lax.dot_general(a_ref[...], b_ref[...], (((ca,), (cb,)), ((), ())), preferred_element_type=F32)

        def finish(r):
            if add_ref is not None:
                r = r + add_ref[...]
            o_ref[...] = r.astype(o_ref.dtype)

        if nk == 1:
            finish(part)
            after()
            return
        acc = refs[nin + 2 * nc + 1]
        k = pl.program_id(2)

        @pl.when(k == 0)
        def _():
            acc[...] = part

        @pl.when(k > 0)
        def _():
            acc[...] += part

        @pl.when(k == nk - 1)
        def _():
            finish(acc[...])

        after()

    a_spec = pl.BlockSpec((tk, tm), lambda i, j, k: (k, i)) if ta else pl.BlockSpec((tm, tk), lambda i, j, k: (i, k))
    if b_chip:
        b_spec = (pl.BlockSpec((None, tn, tk), lambda i, j, k: (k, j, 0)) if tb
                  else pl.BlockSpec((None, tk, tn), lambda i, j, k: (j, k, 0)))
    else:
        b_spec = pl.BlockSpec((tn, tk), lambda i, j, k: (j, k)) if tb else pl.BlockSpec((tk, tn), lambda i, j, k: (k, j))
    in_specs = [a_spec, b_spec]
    args = [a, b]
    if add is not None:
        in_specs.append(pl.BlockSpec((tm, tn), lambda i, j, k: (i, j)))
        args.append(add)
    if out_chip:
        out_spec = pl.BlockSpec((None, tm, tn), lambda i, j, k: (j, i, 0))
        out_shape = jax.ShapeDtypeStruct((4, m, tn), out_dtype)
    else:
        out_spec = pl.BlockSpec((tm, tn), lambda i, j, k: (i, j))
        out_shape = jax.ShapeDtypeStruct((m, n), out_dtype)
    res = _pcall(
        body, name=name, grid=(gi, gj, nk), in_specs=in_specs + [_HBM] * nc, out_specs=[out_spec] + [_HBM] * nc,
        out_shape=[out_shape] + riding.out_shape,
        scratch_shapes=([] if nk == 1 else [pltpu.VMEM((tm, tn), F32)]) + riding.sems,
        compiler_params=_cparams(("arbitrary",) * 3 if nc else ("parallel", "parallel", "arbitrary")),
    )(*args, *riding.ins)
    return res[0] if comm is None else (res[0], res[1:])


def _mm_sum(pairs, *, comm, name):
    m, n = pairs[0][0].shape[0], pairs[0][1].shape[1]
    tm, tn = _div(m, 1024, 128), _div(n, 1024, 128)
    tks = [_div(a.shape[1], 1408, 128) for a, _ in pairs]
    nks = [a.shape[1] // tk for (a, _), tk in zip(pairs, tks)]
    offs = [sum(nks[:p]) for p in range(len(pairs))]
    total, npair, nc = sum(nks), len(pairs), comm.n
    gi, gj = m // tm, n // tn

    def body(*refs):
        o_ref, acc = refs[2 * npair + nc], refs[2 * npair + 2 * nc + 1]
        k = pl.program_id(2)
        step = (pl.program_id(0) * gj + pl.program_id(1)) * total + k
        before, after = _comm_phases(comm, refs[2 * npair:2 * npair + nc]
                                     + refs[2 * npair + nc + 1:2 * npair + 2 * nc + 1]
                                     + refs[2 * npair + 2 * nc + 2:], gi * gj * total, step)
        before()
        for p in range(npair):
            def partial_product(p=p):
                part = jnp.dot(refs[2 * p][...], refs[2 * p + 1][...], preferred_element_type=F32)
                if p == 0:
                    @pl.when(k == 0)
                    def _():
                        acc[...] = part

                    @pl.when(k > 0)
                    def _():
                        acc[...] += part
                else:
                    acc[...] += part

            pl.when(jnp.logical_and(k >= offs[p], k < offs[p] + nks[p]))(partial_product)

        @pl.when(k == total - 1)
        def _():
            o_ref[...] = acc[...].astype(o_ref.dtype)

        after()

    def specs(tk, off, nk):
        def kb(k):
            return jnp.clip(k - off, 0, nk - 1)
        return [pl.BlockSpec((tm, tk), lambda i, j, k: (i, kb(k))), pl.BlockSpec((tk, tn), lambda i, j, k: (kb(k), j))]

    in_specs, args = [], []
    for (a, b), tk, off, nk in zip(pairs, tks, offs, nks):
        in_specs += specs(tk, off, nk)
        args += [a, b]
    res = _pcall(
        body, name=name, grid=(gi, gj, total), in_specs=in_specs + [_HBM] * nc,
        out_specs=[pl.BlockSpec((tm, tn), lambda i, j, k: (i, j))] + [_HBM] * nc,
        out_shape=[jax.ShapeDtypeStruct((m, n), BF16)] + comm.out_shape,
        scratch_shapes=[pltpu.VMEM((tm, tn), F32)] + comm.sems,
        compiler_params=_cparams(("arbitrary",) * 3),
    )(*args, *comm.ins)
    return res[0], res[1:]


def _row_spec(br, w, cb):
    return pl.BlockSpec((br, w), lambda i: (i, cb))


def _const_spec(shape):
    return pl.BlockSpec(shape, lambda i: (0,) * len(shape))


def _rows_fwd(fn, rows, consts, outs, *, name, br, acc_shape=None, halo=None):
    s = rows[0][0].shape[0]
    nr, nc = len(rows), len(consts)
    kept = [k for k, o in enumerate(outs) if o is not None]

    def body(*refs):
        xs = [r[...].astype(F32) for r in refs[:nr]]
        cs = [c[...] for c in refs[nr:nr + nc]]
        if halo is not None:
            cs.append(jnp.where(pl.program_id(0) == 0, 0.0, refs[nr + nc][...].astype(F32)))
        res = fn(*xs, *cs)
        orefs = refs[nr + nc + (halo is not None):]
        for j, k in enumerate(kept):
            orefs[j][...] = res[k].astype(orefs[j].dtype)
        if acc_shape is not None:
            acc_ref = orefs[len(kept)]

            @pl.when(pl.program_id(0) == 0)
            def _():
                acc_ref[...] = jnp.zeros_like(acc_ref)

            acc_ref[...] += res[len(outs)]

    in_specs = [_row_spec(br, w, cb) for (_, w, cb) in rows] + [_const_spec(c.shape) for c in consts]
    args = [r[0] for r in rows] + list(consts)
    if halo is not None:
        harr, hw, hcb = rows[halo]
        in_specs.append(pl.BlockSpec((HALO, hw), lambda i: (jnp.maximum(i * (br // HALO) - 1, 0), hcb)))
        args.append(harr)
    out_specs = [_row_spec(br, outs[k][0], 0) for k in kept]
    out_shape = [jax.ShapeDtypeStruct((s, outs[k][0]), outs[k][1]) for k in kept]
    if acc_shape is not None:
        out_specs.append(_const_spec(acc_shape))
        out_shape.append(jax.ShapeDtypeStruct(acc_shape, F32))
    return _pcall(
        body, name=name, grid=(pl.cdiv(s, br),), in_specs=in_specs, out_specs=out_specs, out_shape=out_shape,
        compiler_params=_cparams(("arbitrary",)),
    )(*args)


def _rows_bwd(fn, rows, consts, cots, *, wrt_rows, wrt_consts, drow_dtypes, name, br, unit_cot=False, comm=None):
    comm = _NOTHING if comm is None else comm
    ncomm = comm.n
    nout = len(wrt_rows) + len(wrt_consts)
    s = rows[0][0].shape[0]
    nr, nc = len(rows), len(consts)
    flat_cots = [c for lst in cots for c in lst]
    ncot = len(flat_cots)

    def body(*refs):
        xs = [r[...].astype(F32) for r in refs[:nr]]
        cs = [c[...] for c in refs[nr:nr + nc]]
        cvals = [c[...].astype(F32) for c in refs[nr + nc:nr + nc + ncot]]
        orefs = refs[nr + nc + ncot + ncomm:]
        before, after = _comm_phases(comm, refs[nr + nc + ncot:nr + nc + ncot + ncomm] + orefs[nout:], s // br)
        before()

        def g(*d):
            xs2, cs2 = list(xs), list(cs)
            for j, k in enumerate(wrt_rows):
                xs2[k] = d[j]
            for j, k in enumerate(wrt_consts):
                cs2[k] = d[len(wrt_rows) + j]
            return tuple(fn(*xs2, *cs2))

        prim = [xs[k] for k in wrt_rows] + [cs[k] for k in wrt_consts]
        outs, vjp = jax.vjp(g, *prim)
        ct = []
        pos = 0
        for o, lst in zip(outs, cots):
            if unit_cot:
                ct.append(jnp.ones_like(o))
                continue
            acc = jnp.zeros_like(o)
            for _ in lst:
                acc = acc + cvals[pos]
                pos += 1
            ct.append(acc)
        grads = vjp(tuple(ct))
        for j in range(len(wrt_rows)):
            orefs[j][...] = grads[j].astype(orefs[j].dtype)

        @pl.when(pl.program_id(0) == 0)
        def _():
            for j in range(len(wrt_consts)):
                oref = orefs[len(wrt_rows) + j]
                oref[...] = jnp.zeros_like(oref)

        for j in range(len(wrt_consts)):
            orefs[len(wrt_rows) + j][...] += grads[len(wrt_rows) + j]
        after()

    in_specs = ([_row_spec(br, w, cb) for (_, w, cb) in rows] + [_const_spec(c.shape) for c in consts]
                + [_row_spec(br, w, cb) for (_, w, cb) in flat_cots] + [_HBM] * ncomm)
    out_specs = ([_row_spec(br, rows[k][1], 0) for k in wrt_rows] + [_const_spec(consts[k].shape) for k in wrt_consts]
                 + [_HBM] * ncomm)
    out_shape = ([jax.ShapeDtypeStruct((s, rows[k][1]), dt) for k, dt in zip(wrt_rows, drow_dtypes)]
                 + [jax.ShapeDtypeStruct(consts[k].shape, F32) for k in wrt_consts] + comm.out_shape)
    return _pcall(
        body, name=name, grid=(s // br,), in_specs=in_specs, out_specs=out_specs, out_shape=out_shape,
        scratch_shapes=comm.sems, compiler_params=_cparams(("arbitrary",)),
    )(*[r[0] for r in rows], *consts, *[c[0] for c in flat_cots], *comm.ins)


def _rms(x, w):
    return x * lax.rsqrt(jnp.mean(x * x, axis=-1, keepdims=True) + RMS_EPS) * w


def _f_pre(x, nw, sc, sh):
    return _rms(x, nw) * (1.0 + sc) + sh, x


def _f_pre2(x, o, gt, nw, sc, sh):
    x1 = x + gt * o
    return x1, _rms(x1, nw) * (1.0 + sc) + sh


def _f_fin(x1, f, tgt, gt, nfw):
    y = _rms(x1 + gt * f, nfw)
    return (0.5 * jnp.mean(jnp.square(y - tgt), axis=-1, keepdims=True),)


def _f_comb(o1, o2, o3, l1, l2, l3):
    m = lax.stop_gradient(jnp.maximum(jnp.maximum(l1, l2), l3))
    e1, e2, e3 = jnp.exp(l1 - m), jnp.exp(l2 - m), jnp.exp(l3 - m)
    return ((e1 * o1 + e2 * o2 + e3 * o3) / (e1 + e2 + e3),)


def _f_rwpre(zs, w0, a0, k_k, k_a, wl, e, et):
    r, k, v, zl = zs[:, 0:D], zs[:, D:2 * D], zs[:, 2 * D:3 * D], zs[:, 3 * D:N_RWP]
    lane = lax.broadcasted_iota(jnp.int32, zl.shape, 1)
    t = jnp.where(lane < 64, jnp.tanh(zl), jnp.where(lane < 128, zl, jnp.where(lane < 288, jax.nn.sigmoid(zl), 0.0)))
    lo = _nn(t[:, 0:128], wl[0:128, 0:2 * D])
    g = _nn(t[:, 128:N_LORA], wl[128:N_LORA, 2 * D:3 * D])
    lw = -math.exp(-0.5) * jax.nn.sigmoid(w0 + lo[:, 0:D])
    a = jax.nn.sigmoid(a0 + lo[:, D:2 * D])
    k_mod = k * (1.0 + (a - 1.0) * k_a)
    kk = k * k_k
    kk = kk / jnp.maximum(jnp.sqrt(_hsum(kk * kk, e, et)), 1e-12)
    return r, lw, k_mod, v, -kk, kk * a, g


def _f_rwpost(y, r, v, k_mod, g, lnx_w, lnx_b, r_k, e, et):
    mean = _hsum(y, e, et) * (1.0 / 64)
    yc = y - mean
    var = _hsum(yc * yc, e, et) * (1.0 / 64)
    yn = yc * lax.rsqrt(var + GN_EPS) * lnx_w + lnx_b
    bonus = _hsum(r * k_mod * r_k, e, et) * v
    return ((yn + bonus) * g,)


def _f_mix(gi, ya, yr, bg):
    gate = jax.nn.sigmoid(gi + bg)
    return (gate[:, 0:D] * ya + gate[:, D:2 * D] * yr,)


def _f_adamw(w, g, m, v):
    m = ADAM_B1 * m + (1.0 - ADAM_B1) * g
    v = ADAM_B2 * v + (1.0 - ADAM_B2) * jnp.square(g)
    m_hat = m / (1.0 - ADAM_B1 ** ADAM_STEP)
    v_hat = v / (1.0 - ADAM_B2 ** ADAM_STEP)
    return -ADAM_LR * (m_hat / (jnp.sqrt(v_hat) + ADAM_EPS) + ADAM_WD * w), m, v


def _down(x, k):
    row = lax.broadcasted_iota(jnp.int32, x.shape, 0)
    return jnp.where(row < k, 0.0, pltpu.roll(x, k, 0))


def _up(x, k):
    n = x.shape[0]
    row = lax.broadcasted_iota(jnp.int32, x.shape, 0)
    return jnp.where(row >= n - k, 0.0, pltpu.roll(x, n - k, 0))


def _col_spec(s, w, off=0):
    return pl.BlockSpec((s, w), lambda j: (0, j + off))


def _rwpre_shift_bwd(zs, z, mu, consts, cots, *, br):
    s, w = zs.shape
    n = s // br
    flat = [c for lst in cots for c in lst]
    nc, ncot, nwrt = len(consts), len(flat), 5

    def this(i):
        return jnp.minimum(i, n - 1)

    def last(i):
        return jnp.maximum(i - 1, 0)

    def body(*refs):
        zs_ref, z_ref, zh_ref, mu_ref = refs[:4]
        c_refs, cot_refs = refs[4:4 + nc], refs[4 + nc:4 + nc + ncot]
        dz_ref, dmu_ref = refs[4 + nc + ncot:6 + nc + ncot]
        dc_refs = refs[6 + nc + ncot:6 + nc + ncot + nwrt]
        kept = refs[-1]
        i = pl.program_id(0)

        @pl.when(i == 0)
        def _():
            dmu_ref[...] = jnp.zeros_like(dmu_ref)
            for ref in dc_refs:
                ref[...] = jnp.zeros_like(ref)

        cs = [c[...] for c in c_refs]

        def g(zz, *d):
            return tuple(_f_rwpre(zz, *d, *cs[nwrt:]))

        outs, vjp = jax.vjp(g, zs_ref[...], *cs[:nwrt])
        cts, pos = [], 0
        for o, lst in zip(outs, cots):
            acc = jnp.zeros_like(o)
            for _ in lst:
                acc = acc + cot_refs[pos][...].astype(F32)
                pos += 1
            cts.append(acc)
        grads = vjp(tuple(cts))
        dzs_new = grads[0]

        @pl.when(i < n)
        def _():
            for ref, gr in zip(dc_refs, grads[1:]):
                ref[...] += gr

        @pl.when(i > 0)
        def _():
            d, m = kept[...], mu_ref[...]
            row = lax.broadcasted_iota(jnp.int32, d.shape, 0)
            head = jnp.sum(jnp.where(row == 0, dzs_new, 0.0), axis=0, keepdims=True)
            head = jnp.where(i < n, head, 0.0)
            dm = d * m
            after = jnp.where(row == br - 1, head * m, pltpu.roll(dm, br - 1, 0))
            dz_ref[...] = (d - dm + after).astype(dz_ref.dtype)
            zz, halo = z_ref[...].astype(F32), zh_ref[...].astype(F32)
            tail = jnp.sum(jnp.where(lax.broadcasted_iota(jnp.int32, halo.shape, 0) == HALO - 1, halo, 0.0), axis=0,
                           keepdims=True)
            before = jnp.where(row == 0, jnp.where(i > 1, tail, 0.0), pltpu.roll(zz, 1, 0))
            dmu_ref[...] += jnp.sum(d * (before - zz), axis=0, keepdims=True)

        kept[...] = dzs_new

    in_specs = ([pl.BlockSpec((br, w), lambda i: (this(i), 0)), pl.BlockSpec((br, w), lambda i: (last(i), 0)),
                 pl.BlockSpec((HALO, w), lambda i: (jnp.maximum(last(i) * (br // HALO) - 1, 0), 0)),
                 _const_spec(mu.shape)] + [_const_spec(c.shape) for c in consts]
                + [pl.BlockSpec((br, cw), lambda i, cb=cb: (this(i), cb)) for (_, cw, cb) in flat])
    out_specs = ([pl.BlockSpec((br, w), lambda i: (last(i), 0)), _const_spec(mu.shape)]
                 + [_const_spec(consts[k].shape) for k in range(nwrt)])
    out_shape = ([jax.ShapeDtypeStruct((s, w), BF16), jax.ShapeDtypeStruct(mu.shape, F32)]
                 + [jax.ShapeDtypeStruct(consts[k].shape, F32) for k in range(nwrt)])
    return _pcall(
        body, name="rwpre_shift_bwd", grid=(n + 1,), in_specs=in_specs, out_specs=out_specs, out_shape=out_shape,
        scratch_shapes=[pltpu.VMEM((br, w), F32)], compiler_params=_cparams(("arbitrary",)),
    )(zs, z, z, mu, *consts, *[c[0] for c in flat])


def _conv3(x, w_ref, b_ref):
    return b_ref[...] + w_ref[0:1, :] * _down(x, 2) + w_ref[1:2, :] * _down(x, 1) + w_ref[2:3, :] * x


def _conv_fwd(u, cw, cb):
    s = u.shape[0]
    nb = D_FF // 128

    def body(ug_ref, uv_ref, wg_ref, wv_ref, bg_ref, bv_ref, o_ref):
        gate = _conv3(ug_ref[...], wg_ref, bg_ref)
        val = _conv3(uv_ref[...], wv_ref, bv_ref)
        o_ref[...] = (gate * jax.nn.sigmoid(gate) * val).astype(o_ref.dtype)

    return _pcall(
        body, name="conv_fwd", grid=(nb,),
        in_specs=[_col_spec(s, 128), _col_spec(s, 128, nb), _col_spec(3, 128), _col_spec(3, 128, nb),
                  _col_spec(1, 128), _col_spec(1, 128, nb)],
        out_specs=_col_spec(s, 128), out_shape=jax.ShapeDtypeStruct((s, D_FF), BF16),
        compiler_params=_cparams(("parallel",)),
    )(u, u, cw, cw, cb, cb)


def _conv_bwd(u, cw, cb, dact):
    s = u.shape[0]
    nb = D_FF // 128

    def half(x, d, w_ref, du_ref, dw_ref, db_ref):
        x1, x2 = _down(x, 1), _down(x, 2)
        du_ref[...] = (w_ref[2:3, :] * d + w_ref[1:2, :] * _up(d, 1) + w_ref[0:1, :] * _up(d, 2)).astype(du_ref.dtype)
        dw_ref[0:1, :] = jnp.sum(d * x2, axis=0, keepdims=True)
        dw_ref[1:2, :] = jnp.sum(d * x1, axis=0, keepdims=True)
        dw_ref[2:3, :] = jnp.sum(d * x, axis=0, keepdims=True)
        db_ref[...] = jnp.sum(d, axis=0, keepdims=True)

    def body(ug_ref, uv_ref, wg_ref, wv_ref, bg_ref, bv_ref, da_ref,
             du_ref, dwg_ref, dwv_ref, dbg_ref, dbv_ref, hold):
        @pl.when((pl.program_id(1) == 1) & (pl.program_id(0) < nb))
        def _():
            ug, uv, da = ug_ref[...], uv_ref[...], da_ref[...]
            gate = _conv3(ug, wg_ref, bg_ref)
            val = _conv3(uv, wv_ref, bv_ref)
            sg = jax.nn.sigmoid(gate)
            dgate = da * val * sg * (1.0 + gate * (1.0 - sg))
            dval = da * gate * sg
            half(ug, dgate, wg_ref, du_ref, dwg_ref, dbg_ref)
            half(uv, dval, wv_ref, hold, dwv_ref, dbv_ref)

        @pl.when((pl.program_id(1) == 0) & (pl.program_id(0) > 0))
        def _():
            du_ref[...] = hold[...]

    def spec(rows, off=0):
        return pl.BlockSpec((rows, 128), lambda j, h: (0, jnp.minimum(j, nb - 1) + off))

    def du_col(j, h):
        return jnp.where(h == 0, jnp.where(j == 0, 0, nb + j - 1), jnp.where(j < nb, j, 2 * nb - 1))

    du, dwg, dwv, dbg, dbv = _pcall(
        body, name="conv_bwd", grid=(nb + 1, 2),
        in_specs=[spec(s), spec(s, nb), spec(3), spec(3, nb), spec(1), spec(1, nb), spec(s)],
        out_specs=[pl.BlockSpec((s, 128), lambda j, h: (0, du_col(j, h))), spec(3), spec(3), spec(1), spec(1)],
        out_shape=[jax.ShapeDtypeStruct((s, 2 * D_FF), BF16),
                   jax.ShapeDtypeStruct((3, D_FF), F32), jax.ShapeDtypeStruct((3, D_FF), F32),
                   jax.ShapeDtypeStruct((1, D_FF), F32), jax.ShapeDtypeStruct((1, D_FF), F32)],
        scratch_shapes=[pltpu.VMEM((s, 128), BF16)],
        compiler_params=_cparams(("arbitrary", "arbitrary")),
    )(u, u, cw, cw, cb, cb, dact)
    return du, jnp.concatenate([dwg, dwv], axis=1), jnp.concatenate([dbg, dbv], axis=1)


ATT_BATCH = 4


def _att_batch(q, kp, kc, vp, vc, first):
    ma = lax.broadcasted_iota(jnp.int32, (1, ATT_BLOCK, 128), 2) < 64

    def diag(x):
        return jnp.concatenate([jnp.where(ma, x, 0.0), jnp.where(ma, 0.0, x)], axis=1)

    qi = lax.broadcasted_iota(jnp.int32, (1, ATT_BLOCK, 2 * ATT_BLOCK), 1)
    kj = lax.broadcasted_iota(jnp.int32, (1, ATT_BLOCK, 2 * ATT_BLOCK), 2) & (ATT_BLOCK - 1)
    okp = kj >= qi + jnp.where(first, 2 * ATT_BLOCK, 0)
    okc = kj <= qi
    sp = jnp.where(okp, _bnt(q, diag(kp)) * 0.125, NEG)
    sc = jnp.where(okc, _bnt(q, diag(kc)) * 0.125, NEG)

    def per_head(fn, x):
        return fn(x[..., :ATT_BLOCK]), fn(x[..., ATT_BLOCK:])

    def spread(ab):
        return jnp.concatenate([jnp.broadcast_to(t, t.shape[:2] + (ATT_BLOCK,)) for t in ab], axis=-1)

    row_max = functools.partial(jnp.max, axis=-1, keepdims=True)
    row_sum = functools.partial(jnp.sum, axis=-1, keepdims=True)
    m = [lax.stop_gradient(jnp.maximum(a, b)) for a, b in zip(per_head(row_max, sp), per_head(row_max, sc))]
    pp, pc = jnp.exp(sp - spread(m)), jnp.exp(sc - spread(m))
    den = [a + b for a, b in zip(per_head(row_sum, pp), per_head(row_sum, pc))]
    num = _bnn(pp, diag(vp)) + _bnn(pc, diag(vc))
    out = num / jnp.where(ma, den[0], den[1])
    lse = jnp.where(ma, m[0] + jnp.log(den[0]), m[1] + jnp.log(den[1]))
    return out, jnp.broadcast_to(lse, out.shape)


def _att_pairs_per_step(dil):
    return 4 if dil == 1 else 1


def _att_residues(dil):
    return min(dil, ATT_BATCH // _att_pairs_per_step(dil))


def _att_specs(g, dil):
    rows, pp = ATT_BLOCK * dil, _att_pairs_per_step(dil)

    def cur(slot):
        return pl.BlockSpec((rows, 128 * pp), lambda n, p: (n, (g * 3 + slot) * (4 // pp) + p))

    def prev(slot):
        return pl.BlockSpec((rows, 128 * pp), lambda n, p: (jnp.maximum(n - 1, 0), (g * 3 + slot) * (4 // pp) + p))

    return [cur(0), prev(1), cur(1), prev(2), cur(2)]


def _att_out_spec(dil):
    return pl.BlockSpec((ATT_BLOCK * dil, 128 * _att_pairs_per_step(dil)), lambda n, p: (n, p))


def _att_grid(s, dil):
    return (s // (ATT_BLOCK * dil), 4 // _att_pairs_per_step(dil))


def _att_windows(i, dil):
    res = _att_residues(dil)

    def rows(r):
        return pl.ds(i * res + r, ATT_BLOCK, stride=dil) if dil > 1 else pl.ds(0, ATT_BLOCK)

    return [(rows(r), pl.ds(128 * j, 128)) for j in range(_att_pairs_per_step(dil)) for r in range(res)]


def _att_fwd(att_in, g, dil):
    s = att_in.shape[0]

    def body(q_ref, kp_ref, kc_ref, vp_ref, vc_ref, o_ref, l_ref):
        first = pl.program_id(0) == 0

        def one(i, carry):
            win = _att_windows(i, dil)
            vals = [jnp.stack([ref[w] for w in win]) for ref in (q_ref, kp_ref, kc_ref, vp_ref, vc_ref)]
            o, l = _att_batch(*vals, first)
            for j, w in enumerate(win):
                o_ref[w] = o[j]
                l_ref[w] = l[j]
            return carry

        lax.fori_loop(0, dil // _att_residues(dil), one, 0)

    return _pcall(
        body, name=f"att_fwd{g}", grid=_att_grid(s, dil), in_specs=_att_specs(g, dil),
        out_specs=[_att_out_spec(dil)] * 2, out_shape=[jax.ShapeDtypeStruct((s, ATT_WIDTH), F32)] * 2,
        compiler_params=_cparams(("parallel", "parallel")),
    )(att_in, att_in, att_in, att_in, att_in)


def _att_bwd(att_in, g, dil, do, dl, acc):
    s = att_in.shape[0]
    rows, pp = ATT_BLOCK * dil, _att_pairs_per_step(dil)
    nb, npair, wid = s // rows, 4 // pp, 128 * pp

    def body(q_ref, kp_ref, kc_ref, vp_ref, vc_ref, do_ref, dl_ref, *rest):
        o_ref, lag_q, lag_k, lag_v = rest[-9:-5]
        stage = rest[-5:]
        n, p = pl.program_id(0), pl.program_id(1)
        first = n == 0

        @pl.when(n < nb)
        def _():
            def one(i, carry):
                win = _att_windows(i, dil)
                vals = [jnp.stack([ref[w] for w in win]) for ref in (q_ref, kp_ref, kc_ref, vp_ref, vc_ref)]
                _, vjp = jax.vjp(lambda *a: _att_batch(*a, first), *vals)
                grads = vjp((jnp.stack([do_ref[w] for w in win]), jnp.stack([dl_ref[w] for w in win])))
                for ref, gr in zip(stage, grads):
                    for j, w in enumerate(win):
                        ref[w] = gr[j]
                return carry

            lax.fori_loop(0, dil // _att_residues(dil), one, 0)

        live = n < nb
        for pj in range(npair):
            @pl.when((n > 0) & (p == pj))
            def _(pj=pj):
                c = pj * wid
                o_ref[:, c:c + wid] = lag_q[pj]
                o_ref[:, ATT_WIDTH + c:ATT_WIDTH + c + wid] = (
                    lag_k[pj] + jnp.where(live, stage[1][...], 0.0)).astype(BF16)
                o_ref[:, 2 * ATT_WIDTH + c:2 * ATT_WIDTH + c + wid] = (
                    lag_v[pj] + jnp.where(live, stage[3][...], 0.0)).astype(BF16)

        @pl.when(live)
        def _():
            lag_q[p] = stage[0][...].astype(BF16)
            lag_k[p] = stage[2][...]
            lag_v[p] = stage[4][...]

    def col(slot, n, p):
        return (g * 3 + slot) * npair + jnp.where(n < nb, p, npair - 1)

    def cur(slot):
        return pl.BlockSpec((rows, wid), lambda n, p: (jnp.minimum(n, nb - 1), col(slot, n, p)))

    def prev(slot):
        return pl.BlockSpec((rows, wid), lambda n, p: (jnp.maximum(jnp.minimum(n, nb - 1) - 1, 0), col(slot, n, p)))

    cot = pl.BlockSpec((rows, wid), lambda n, p: (jnp.minimum(n, nb - 1), jnp.where(n < nb, p, npair - 1)))
    carried = [] if acc is None else [acc]
    return _pcall(
        body, name=f"att_bwd{g}", grid=(nb + 1, npair),
        in_specs=[cur(0), prev(1), cur(1), prev(2), cur(2), cot, cot] + [pl.BlockSpec(memory_space=pl.ANY)] * len(carried),
        out_specs=pl.BlockSpec((rows, 3 * ATT_WIDTH), lambda n, p: (jnp.maximum(n - 1, 0), g)),
        out_shape=jax.ShapeDtypeStruct((s, N_ATT), BF16), input_output_aliases={7: 0} if carried else {},
        scratch_shapes=([pltpu.VMEM((npair, rows, wid), BF16)] + [pltpu.VMEM((npair, rows, wid), F32)] * 2
                        + [pltpu.VMEM((rows, wid), F32)] * 5),
        compiler_params=_cparams(("arbitrary", "arbitrary")),
    )(att_in, att_in, att_in, att_in, att_in, do, dl, *carried)


def _cumsum_rows_impl(x):
    row = lax.broadcasted_iota(jnp.int32, x.shape, 0)
    shift = 1
    while shift < x.shape[0]:
        x = x + jnp.where(row >= shift, pltpu.roll(x, shift, 0), 0.0)
        shift *= 2
    return x


@jax.custom_vjp
def _cumsum_rows(x):
    return _cumsum_rows_impl(x)


_cumsum_rows.defvjp(lambda x: (_cumsum_rows_impl(x), None),
                    lambda _, g: (jnp.sum(g, axis=0, keepdims=True) - _cumsum_rows_impl(g) + g,))


def _unit_lower_inverse_impl(n):
    eye = (lax.broadcasted_iota(jnp.int32, (1,) + n.shape[1:], 1)
           == lax.broadcasted_iota(jnp.int32, (1,) + n.shape[1:], 2))
    t = jnp.where(eye, 1.0, 0.0) + n
    pw = n
    for _ in range(5):
        pw = _bnn(pw, pw)
        t = t + _bnn(t, pw)
    return t


@jax.custom_vjp
def _unit_lower_inverse(n):
    return _unit_lower_inverse_impl(n)


def _unit_lower_inverse_fwd(n):
    t = _unit_lower_inverse_impl(n)
    return t, t


_unit_lower_inverse.defvjp(_unit_lower_inverse_fwd, lambda t, g: (_bnt(_btn(t, g), t),))


@jax.custom_vjp
def _known_inverse(n, t):
    return t


_known_inverse.defvjp(lambda n, t: (t, t), lambda t, g: (_bnt(_btn(t, g), t), jnp.zeros_like(t)))


def _scan_chunk(r, lw, k, v, a, b, s0, inverse):
    c = SCAN_CHUNK
    p = s0.shape[0]
    cum = _cumsum_rows(lw)
    tot = jnp.sum(lw, axis=0, keepdims=True)
    ma = (lax.broadcasted_iota(jnp.int32, (c, 128 * p), 1) & 127) < 64

    def pairs(x):
        return jnp.concatenate([x[None, :, 128 * j:128 * (j + 1)] for j in range(p)], axis=0)

    def stack(x):
        return jnp.concatenate([pairs(jnp.where(ma, x, 0.0)), pairs(jnp.where(ma, 0.0, x))], axis=1)

    einv, eend = jnp.exp(-cum), jnp.exp(tot - cum)
    ra, aa = stack(r * jnp.exp(cum)), stack(a * jnp.exp(cum - lw))
    bi, ki, be, ke, vs = stack(b * einv), stack(k * einv), stack(b * eend), stack(k * eend), stack(v)
    r2 = lax.broadcasted_iota(jnp.int32, (1, 2 * c, 2 * c), 1)
    c2 = lax.broadcasted_iota(jnp.int32, (1, 2 * c, 2 * c), 2)
    same = (r2 >= c) == (c2 >= c)
    strict = jnp.logical_and(same, c2 < r2)
    incl = jnp.logical_and(same, c2 <= r2)
    s0 = jnp.where(same, s0, 0.0)
    prod = _bnt(jnp.concatenate([aa, ra], axis=1), jnp.concatenate([bi, ki], axis=1))
    a_ab = jnp.where(strict, prod[:, :2 * c, :2 * c], 0.0)
    a_ak = jnp.where(strict, prod[:, :2 * c, 2 * c:], 0.0)
    a_rb = jnp.where(incl, prod[:, 2 * c:, :2 * c], 0.0)
    a_rk = jnp.where(incl, prod[:, 2 * c:, 2 * c:], 0.0)
    t = inverse(a_ab)
    u = _bnn(t, _bnt(aa, s0) + _bnn(a_ak, vs))
    uv = jnp.concatenate([u, vs], axis=1)
    ys = _bnt(ra, s0) + _bnn(jnp.concatenate([a_rb, a_rk], axis=2), uv)
    s1 = s0 * pairs(jnp.exp(tot)) + _btn(uv, jnp.concatenate([be, ke], axis=1))
    y3 = ys[:, :c] + ys[:, c:]
    return (jnp.concatenate([y3[j] for j in range(p)], axis=1), s1), t


def _scan_specs(rev, n):
    def at(i):
        return n - 1 - i if rev else i

    def cm(cb):
        return pl.BlockSpec((SCAN_CHUNK, D), lambda i: (at(i), cb))

    return cm, pl.BlockSpec((1, SCAN_PAIRS, 128, 128), lambda i: (at(i), 0, 0, 0))


def _comm_phases(comm, refs, n, step=None):
    k = comm.n
    srcs, outs, sems = refs[:k], refs[k:2 * k], refs[2 * k:]
    i = pl.program_id(0) if step is None else step

    def before():
        @pl.when(i == 0)
        def _():
            comm.first(srcs, outs, sems)

    def after():
        if comm.mid is not None:
            @pl.when(i == (3 * n) // 4)
            def _():
                comm.mid(srcs, outs, sems)

        @pl.when(i == n - 1)
        def _():
            comm.last(srcs, outs, sems)

    return before, after


def _scan_fwd(zs, lw, km, aa, bb, comm):
    s = zs.shape[0]
    n = s // SCAN_CHUNK
    cm, st = _scan_specs(False, n)
    k = comm.n

    def body(*refs):
        r_ref, lw_ref, k_ref, v_ref, a_ref, b_ref = refs[:6]
        y_ref, s0_ref, t_ref = refs[6 + k:9 + k]
        state = refs[9 + 2 * k]
        before, after = _comm_phases(comm, refs[6:6 + k] + refs[9 + k:9 + 2 * k] + refs[10 + 2 * k:], n)
        before()

        @pl.when(pl.program_id(0) == 0)
        def _():
            state[...] = jnp.zeros_like(state)

        s0 = state[...]
        s0_ref[0] = s0
        (y, s1), t = _scan_chunk(*[ref[...] for ref in (r_ref, lw_ref, k_ref, v_ref, a_ref, b_ref)], s0,
                                 _unit_lower_inverse)
        y_ref[...] = y
        t_ref[0] = t.astype(BF16)
        state[...] = s1
        after()

    per_chunk = (n, SCAN_PAIRS, 128, 128)
    res = _pcall(
        body, name="scan_fwd", grid=(n,), in_specs=[cm(0), cm(0), cm(0), cm(2), cm(0), cm(0)] + [_HBM] * k,
        out_specs=[cm(0), st, st] + [_HBM] * k,
        out_shape=[jax.ShapeDtypeStruct((s, D), F32), jax.ShapeDtypeStruct(per_chunk, F32),
                   jax.ShapeDtypeStruct(per_chunk, BF16)] + comm.out_shape,
        scratch_shapes=[pltpu.VMEM((SCAN_PAIRS, 128, 128), F32)] + comm.sems,
        compiler_params=_cparams(("arbitrary",)),
    )(zs, lw, km, zs, aa, bb, *comm.ins)
    return res[0], res[1], res[2], res[3:]


def _scan_bwd(zs, lw, km, aa, bb, s0s, ts, dy, comm):
    s = zs.shape[0]
    n = s // SCAN_CHUNK
    cm, st = _scan_specs(True, n)
    k = comm.n

    def body(*refs):
        r_ref, lw_ref, k_ref, v_ref, a_ref, b_ref, s0_ref, t_ref, dy_ref = refs[:9]
        douts = refs[9 + k:15 + k]
        dstate = refs[15 + 2 * k]
        before, after = _comm_phases(comm, refs[9:9 + k] + refs[15 + k:15 + 2 * k] + refs[16 + 2 * k:], n)
        before()

        @pl.when(pl.program_id(0) == 0)
        def _():
            dstate[...] = jnp.zeros_like(dstate)

        t = t_ref[0].astype(F32)
        prim = [ref[...] for ref in (r_ref, lw_ref, k_ref, v_ref, a_ref, b_ref)] + [s0_ref[0]]
        _, vjp, _ = jax.vjp(lambda *p: _scan_chunk(*p, lambda nil: _known_inverse(nil, t)), *prim, has_aux=True)
        grads = vjp((dy_ref[...], dstate[...]))
        for ref, gr in zip(douts, grads[:6]):
            ref[...] = gr
        dstate[...] = grads[6]
        after()

    res = _pcall(
        body, name="scan_bwd", grid=(n,),
        in_specs=[cm(0), cm(0), cm(0), cm(2), cm(0), cm(0), st, st, cm(0)] + [_HBM] * k,
        out_specs=[cm(0)] * 6 + [_HBM] * k, out_shape=[jax.ShapeDtypeStruct((s, D), F32)] * 6 + comm.out_shape,
        scratch_shapes=[pltpu.VMEM((SCAN_PAIRS, 128, 128), F32)] + comm.sems,
        compiler_params=_cparams(("arbitrary",)),
    )(zs, lw, km, zs, aa, bb, s0s, ts, dy, *comm.ins)
    return res[:6], res[6:]


_HBM = pl.BlockSpec(memory_space=pltpu.HBM)


def _me():
    return lax.axis_index("x"), lax.axis_index("y"), lax.axis_index("c")


def _allgather8(src, name):
    def body(src_ref, out_ref, ssem, rsem, lsem):
        x, y, c = _me()
        me = 4 * x + 2 * y + c
        local = pltpu.make_async_copy(src_ref, out_ref.at[me], lsem)
        local.start()
        peers = []
        for k in range(1, 8):
            peers.append(((1 - x) if k & 4 else x, (1 - y) if k & 2 else y, (1 - c) if k & 1 else c))
        sends = []
        for k, peer in enumerate(peers):
            cp = pltpu.make_async_remote_copy(src_ref, out_ref.at[me], ssem.at[k], rsem.at[k], device_id=peer,
                                              device_id_type=MESH)
            cp.start()
            sends.append(cp)
        for k, (px, py, pc) in enumerate(peers):
            pltpu.make_async_remote_copy(src_ref, out_ref.at[4 * px + 2 * py + pc], ssem.at[k], rsem.at[k],
                                         device_id=(px, py, pc), device_id_type=MESH).wait_recv()
        for cp in sends:
            cp.wait_send()
        local.wait()

    return _pcall(
        body, name=name, in_specs=[_HBM], out_specs=_HBM, out_shape=jax.ShapeDtypeStruct((8,) + src.shape, src.dtype),
        scratch_shapes=[pltpu.SemaphoreType.DMA((7,)), pltpu.SemaphoreType.DMA((7,)), pltpu.SemaphoreType.DMA],
    )(src)


def _other_chips(x, y):
    return [(1 - x, y), (x, 1 - y), (1 - x, 1 - y)]


def _remote(src, dst, ssem, rsem, to):
    return pltpu.make_async_remote_copy(src, dst, ssem, rsem, device_id=to, device_id_type=MESH)


class _GatherWeights:
    def __init__(self, shards):
        self.ins = list(shards)
        n = self.n = len(shards)
        self.out_shape = [jax.ShapeDtypeStruct((4,) + t.shape, t.dtype) for t in shards]
        self.sems = [pltpu.SemaphoreType.DMA((6 * n,)), pltpu.SemaphoreType.DMA((6 * n,)),
                     pltpu.SemaphoreType.DMA((n,)), pltpu.SemaphoreType.DMA((n,))]

    def _copies(self, srcs, outs, sems):
        ssem, rsem, lsem, osem = sems
        x, y, c = _me()
        me = 2 * x + y
        own, ici, landed, passed, passed_in = [], [], [], [], []
        for a in range(self.n):
            h = self.ins[a].shape[0] // 2
            mine, other = pl.ds(c * h, h), pl.ds((1 - c) * h, h)
            own.append(_remote(srcs[a], outs[a].at[me], lsem.at[a], osem.at[a], (x, y, 1 - c)))
            for k, (px, py) in enumerate(_other_chips(x, y)):
                s1, r1, s2, r2 = ssem.at[6 * a + k], rsem.at[6 * a + k], ssem.at[6 * a + 3 + k], rsem.at[6 * a + 3 + k]
                got, got_sib = outs[a].at[2 * px + py, mine], outs[a].at[2 * px + py, other]
                ici.append(_remote(srcs[a].at[mine], outs[a].at[me, mine], s1, r1, (px, py, c)))
                landed.append(_remote(got, got, s1, r1, (px, py, c)))
                passed.append(_remote(got, got, s2, r2, (x, y, 1 - c)))
                passed_in.append(_remote(got_sib, got_sib, s2, r2, (x, y, 1 - c)))
        return own, ici, landed, passed, passed_in

    def first(self, srcs, outs, sems):
        own, ici, _, _, _ = self._copies(srcs, outs, sems)
        for cp in own + ici:
            cp.start()

    def mid(self, srcs, outs, sems):
        _, _, landed, passed, _ = self._copies(srcs, outs, sems)
        for arrived, onward in zip(landed, passed):
            arrived.wait_recv()
            onward.start()

    def last(self, srcs, outs, sems):
        own, ici, _, passed, passed_in = self._copies(srcs, outs, sems)
        for cp in passed_in:
            cp.wait_recv()
        for cp in ici + passed:
            cp.wait_send()
        for cp in own:
            cp.wait()


class _ScatterToChips:
    def __init__(self, parts):
        self.ins = list(parts)
        n = self.n = len(parts)
        self.out_shape = [jax.ShapeDtypeStruct(t.shape, t.dtype) for t in parts]
        self.sems = [pltpu.SemaphoreType.DMA((3 * n,)), pltpu.SemaphoreType.DMA((3 * n,)), pltpu.SemaphoreType.DMA((n,))]

    def _copies(self, srcs, outs, sems):
        ssem, rsem, lsem = sems
        x, y, c = _me()
        me = 2 * x + y
        own, out, landed = [], [], []
        for a in range(self.n):
            own.append(pltpu.make_async_copy(srcs[a].at[me], outs[a].at[me], lsem.at[a]))
            for k, (px, py) in enumerate(_other_chips(x, y)):
                dst = outs[a].at[2 * px + py]
                out.append(_remote(srcs[a].at[2 * px + py], outs[a].at[me], ssem.at[3 * a + k], rsem.at[3 * a + k],
                                   (px, py, c)))
                landed.append(_remote(dst, dst, ssem.at[3 * a + k], rsem.at[3 * a + k], (px, py, c)))
        return own, out, landed

    def first(self, srcs, outs, sems):
        own, out, _ = self._copies(srcs, outs, sems)
        for cp in own + out:
            cp.start()

    mid = None

    def last(self, srcs, outs, sems):
        own, out, landed = self._copies(srcs, outs, sems)
        for cp in landed:
            cp.wait_recv()
        for cp in own:
            cp.wait()
        for cp in out:
            cp.wait_send()


def _run_comm(comm, name):
    n = comm.n

    def body(*refs):
        srcs, outs, sems = refs[:n], refs[n:2 * n], refs[2 * n:]
        comm.first(srcs, outs, sems)
        if comm.mid is not None:
            comm.mid(srcs, outs, sems)
        comm.last(srcs, outs, sems)

    return _pcall(body, name=name, in_specs=[_HBM] * n, out_specs=[_HBM] * n, out_shape=comm.out_shape,
                  scratch_shapes=comm.sems)(*comm.ins)


class _NoComm:
    n, ins, out_shape, sems, mid = 0, [], [], [], None

    def first(self, srcs, outs, sems):
        pass

    def last(self, srcs, outs, sems):
        pass


_NOTHING = _NoComm()


class _SiblingHalves:
    mid = None

    def __init__(self, grads):
        self.ins = list(grads)
        n = self.n = len(grads)
        self.out_shape = [jax.ShapeDtypeStruct((4, t.shape[1] // 2, t.shape[2]), t.dtype) for t in grads]
        self.sems = [pltpu.SemaphoreType.DMA((n,)), pltpu.SemaphoreType.DMA((n,))]

    def _copies(self, srcs, outs, sems):
        ssem, rsem = sems
        x, y, c = _me()
        copies = []
        for a in range(self.n):
            h = self.ins[a].shape[1] // 2
            copies.append(_remote(srcs[a].at[:, pl.ds((1 - c) * h, h)], outs[a], ssem.at[a], rsem.at[a], (x, y, 1 - c)))
        return copies

    def first(self, srcs, outs, sems):
        for cp in self._copies(srcs, outs, sems):
            cp.start()

    def last(self, srcs, outs, sems):
        for cp in self._copies(srcs, outs, sems):
            cp.wait()


def _reduce_finish(reds, name):
    n = len(reds)

    def body(*refs):
        outs = refs[n:2 * n]
        ssem, rsem = refs[2 * n:]
        x, y, c = _me()
        copies = []
        for a in range(n):
            h = reds[a].shape[0] // 2
            mine = outs[a].at[pl.ds(c * h, h)]
            copies.append(_remote(mine, mine, ssem.at[a], rsem.at[a], (x, y, 1 - c)))
        for cp in copies:
            cp.start()
        for a in range(n):
            h = reds[a].shape[0] // 2
            dst = outs[a].at[pl.ds((1 - c) * h, h)]
            _remote(dst, dst, ssem.at[a], rsem.at[a], (x, y, 1 - c)).wait_recv()
        for cp in copies:
            cp.wait_send()

    return _pcall(
        body, name=name, in_specs=[_HBM] * n, out_specs=[_HBM] * n,
        out_shape=[jax.ShapeDtypeStruct(t.shape, t.dtype) for t in reds],
        input_output_aliases={a: a for a in range(n)},
        scratch_shapes=[pltpu.SemaphoreType.DMA((n,)), pltpu.SemaphoreType.DMA((n,))],
    )(*reds)


def _half_sum(fn, full, halves, out_full, out_dtype, core, name):
    p, h, c = (halves[0].shape if halves else (full[0].shape[0], full[0].shape[1] // 2, full[0].shape[2]))
    br = _div(h, max(16, (1 << 19) // (p * c)), 16)
    nb = h // br
    mine3 = pl.BlockSpec((p, br, c), lambda i, core_ref: (0, core_ref[0] * nb + i, 0))
    half3 = pl.BlockSpec((p, br, c), lambda i, core_ref: (0, i, 0))

    def body(core_ref, *refs):
        refs[-1][...] = fn(*[t[...].astype(F32) for t in refs[:-1]]).astype(out_dtype)

    if out_full:
        out_spec = pl.BlockSpec((br, c), lambda i, core_ref: (core_ref[0] * nb + i, 0))
        out_shape = jax.ShapeDtypeStruct((2 * h, c), out_dtype)
    else:
        out_spec, out_shape = half3, jax.ShapeDtypeStruct((p, h, c), out_dtype)
    return _pcall(
        body, name=name,
        grid_spec=pltpu.PrefetchScalarGridSpec(
            num_scalar_prefetch=1, grid=(nb,), in_specs=[mine3] * len(full) + [half3] * len(halves),
            out_specs=out_spec),
        out_shape=out_shape, compiler_params=_cparams(("parallel",)),
    )(core, *full, *halves)


def _ada_fwd(c_all, w, b):
    def body(c_ref, w_ref, b_ref, o_ref):
        o_ref[...] = jnp.dot(c_ref[...], w_ref[...], precision=HI, preferred_element_type=F32) + b_ref[...]

    return _pcall(body, name="ada_fwd", out_shape=jax.ShapeDtypeStruct((c_all.shape[0], w.shape[1]), F32),
                  compiler_params=pltpu.CompilerParams(vmem_limit_bytes=VMEM_LIMIT))(c_all, w, b)


def _ada_bwd(c_all_t, d):
    def body(c_ref, d_ref, o_ref):
        o_ref[...] = jnp.dot(c_ref[...], d_ref[...], precision=HI, preferred_element_type=F32)

    return _pcall(body, name="ada_bwd", out_shape=jax.ShapeDtypeStruct((c_all_t.shape[0], d.shape[1]), F32),
                  compiler_params=pltpu.CompilerParams(vmem_limit_bytes=VMEM_LIMIT))(c_all_t, d)


def _sum_lead(x, name):
    p, r, n = x.shape
    br = _div(r, 512, 8)

    def body(x_ref, o_ref):
        acc = x_ref[0]
        for j in range(1, p):
            acc = acc + x_ref[j]
        o_ref[...] = acc

    return _pcall(
        body, name=name, grid=(r // br,), in_specs=[pl.BlockSpec((p, br, n), lambda i: (0, i, 0))],
        out_specs=pl.BlockSpec((br, n), lambda i: (i, 0)), out_shape=jax.ShapeDtypeStruct((r, n), F32),
        compiler_params=_cparams(("parallel",)),
    )(x)


def _adamw(w, g, m, v, name):
    shape = w.shape
    cols = shape[-1]
    w2, g2, m2, v2 = [t.reshape(-1, cols) for t in (w, g, m, v)]
    rows = w2.shape[0]
    pref = max(8, (1 << 19) // cols // 8 * 8)
    br = _div(rows, pref, 8)
    if rows // br > 64:
        br = pref
    outs = _rows_fwd(_f_adamw, [(t, cols, 0) for t in (w2, g2, m2, v2)], [], [(cols, F32)] * 3, name=name, br=br)
    return [o.reshape(shape) for o in outs]


_BIG = (("w_in", 1), ("w_up", 1), ("w_down", 0), ("w_o", 0), ("w_rwkv_out", 0), ("w_att_out", 1), ("w2", 1), ("a2", 1),
        ("g2", 1))


_NEEDED_FIRST = ("w_in", "w_att_out", "w2", "a2", "g2")
_NEEDED_LATER = ("w_up", "w_down", "w_o", "w_rwkv_out")
_DONE_EARLY = ("w_up", "w_down", "w_o", "w_rwkv_out", "w_att_out")
_DONE_LATE = ("w_in", "w2", "a2", "g2")


def _cols_joined(t):
    return jnp.concatenate([t[j] for j in range(4)], axis=1)


def _cols_split(t):
    n = t.shape[1] // 4
    return jnp.stack([t[:, j * n:(j + 1) * n] for j in range(4)])


W_IN_SHARD = (N_ATT + N_RW + N_GATE) // 4
W_IN_PAD = 2560


def _row_window(parts, lo, hi):
    out, pos = [], 0
    for t, w in parts:
        a, b = max(lo, pos), min(hi, pos + w)
        if a < b:
            out.append(t[a - pos:b - pos])
        pos += w
    return out[0] if len(out) == 1 else jnp.concatenate(out, axis=0)


def _rows_joined(t):
    return t.reshape(4 * t.shape[1], t.shape[2])


def _rows_split(t):
    return t.reshape(4, t.shape[0] // 4, t.shape[1])


def _step_to_scan(x, tgt, ada, wts):
    sh1, sc1, gt1, sh2, sc2, gt2 = ada
    br = 256
    grp = lax.broadcasted_iota(jnp.int32, (D, 128), 0) // 64 == lax.broadcasted_iota(jnp.int32, (D, 128), 1)
    e = grp.astype(F32)
    et = e.T
    w_in = [(wts["w_in"][j], W_IN_SHARD) for j in range(4)]
    w_att = _row_window(w_in, 0, N_ATT)
    w_rw = jnp.concatenate([_row_window(w_in, N_ATT, N_ATT + N_RW), jnp.zeros((N_RWP - N_RW, D), BF16)], axis=0)
    w_gate = _row_window(w_in, N_ATT + N_RW, N_ATT + N_RW + N_GATE)
    mu = jnp.pad(wts["mu_shift"], ((0, 0), (0, N_RWP - N_RW)))
    wl = jnp.zeros((N_LORA, 3 * D), F32)
    wl = wl.at[0:64, 0:D].set(_cols_joined(wts["w2"]).astype(F32))
    wl = wl.at[64:128, D:2 * D].set(_cols_joined(wts["a2"]).astype(F32))
    wl = wl.at[128:288, 2 * D:3 * D].set(_cols_joined(wts["g2"]).astype(F32))
    pre1_c = [wts["norm1_w"], sc1, sh1]
    (h1,) = _rows_fwd(_f_pre, [(x, D, 0)], pre1_c, [(D, BF16), None], name="pre1_fwd", br=2 * br)
    att_in = _mm(h1, w_att, tb=True, name="mm_att_in")
    z = _mm(h1, w_rw, tb=True, out_dtype=BF16, name="mm_rw_in")
    gate_in = _mm(h1, w_gate, tb=True, out_dtype=BF16, name="mm_gate_in")
    att_o, att_l = [], []
    for g, (_, dil) in enumerate(ATT_PATTERNS):
        o, l = _att_fwd(att_in, g, dil)
        att_o.append(o)
        att_l.append(l)
    comb_rows = [(t, ATT_WIDTH, 0) for t in att_o + att_l]
    (att,) = _rows_fwd(_f_comb, comb_rows, [], [(ATT_WIDTH, BF16)], name="comb_fwd", br=2 * br)
    w_ao = _cols_joined(wts["w_att_out"])
    y_att = _mm(att, w_ao, out_dtype=BF16, name="mm_att_out")
    rwpre_c = [wts["w0"], wts["a0"], wts["k_k"], wts["k_a"], wl, e, et]

    def shift_and_rwpre(zz, *rest):
        consts, mu_row, before = rest[:-2], rest[-2], rest[-1]
        last = jnp.sum(jnp.where(lax.broadcasted_iota(jnp.int32, before.shape, 0) == HALO - 1, before, 0.0), axis=0,
                       keepdims=True)
        row = lax.broadcasted_iota(jnp.int32, zz.shape, 0)
        zprev = jnp.where(row == 0, last, pltpu.roll(zz, 1, 0))
        shifted = zz + (zprev - zz) * mu_row
        return (shifted,) + tuple(_f_rwpre(shifted, *consts))

    zs, lw, km, aa, bb, gg = _rows_fwd(
        shift_and_rwpre, [(z, N_RWP, 0)], rwpre_c + [mu],
        [(N_RWP, F32), None, (D, F32), (D, F32), None, (D, F32), (D, F32), (D, F32)], name="rwpre_fwd", br=br, halo=0)
    return dict(x=x, tgt=tgt, wts=wts, br=br, e=e, et=et, gt1=gt1, sc2=sc2, sh2=sh2, gt2=gt2, w_att=w_att, w_rw=w_rw,
                w_ao=w_ao,
                w_gate=w_gate, mu=mu, pre1_c=pre1_c, h1=h1, att_in=att_in, z=z, gate_in=gate_in, comb_rows=comb_rows,
                att=att, y_att=y_att, zs=zs, rwpre_c=rwpre_c, lw=lw, km=km, aa=aa, bb=bb, gg=gg)


def _step_between_scans(st, y_raw, late):
    x, tgt, wts, br, e, et = st["x"], st["tgt"], st["wts"], st["br"], st["e"], st["et"]
    zs, km, gg, gate_in, y_att, att = st["zs"], st["km"], st["gg"], st["gate_in"], st["y_att"], st["att"]
    comb_rows, att_in = st["comb_rows"], st["att_in"]
    gt1, sc2, sh2, gt2 = st["gt1"], st["sc2"], st["sh2"], st["gt2"]
    w_up, w_ao = late["w_up"], st["w_ao"]
    w_down, w_o, w_ro = _rows_joined(late["w_down"]), _rows_joined(late["w_o"]), _rows_joined(late["w_rwkv_out"])
    post_rows = [(y_raw, D, 0), (zs, D, 0), (zs, D, 2), (km, D, 0), (gg, D, 0)]
    post_c = [wts["lnx_w"], wts["lnx_b"], wts["r_k"], e, et]
    (rw_out,) = _rows_fwd(_f_rwpost, post_rows, post_c, [(D, BF16)], name="rwpost_fwd", br=br)
    y_rw = _mm(rw_out, w_ro, out_dtype=BF16, name="mm_rw_out")
    mix_rows = [(gate_in, N_GATE, 0), (y_att, D, 0), (y_rw, D, 0)]
    (mix,) = _rows_fwd(_f_mix, mix_rows, [wts["b_gate"]], [(D, BF16)], name="mix_fwd", br=2 * br)
    o = _mm(mix, w_o, out_dtype=BF16, name="mm_o")
    pre2_c = [gt1, wts["norm2_w"], sc2, sh2]
    x1, h2 = _rows_fwd(_f_pre2, [(x, D, 0), (o, D, 0)], pre2_c, [(D, F32), (D, BF16)], name="pre2_fwd", br=2 * br)
    u = _mm(h2, w_up, b_chip=True, name="mm_up")
    act = _conv_fwd(u, wts["conv_w"], wts["conv_b"])
    f = _mm(act, w_down, out_dtype=BF16, name="mm_down")
    fin_rows = [(x1, D, 0), (f, D, 0), (tgt, D, 0)]
    fin_c = [gt2, wts["norm_f_w"]]

    def fin_fwd(*a):
        (l,) = _f_fin(*a)
        return (jnp.broadcast_to(jnp.sum(l, axis=0, keepdims=True), (8, 128)),)

    (loss_acc,) = _rows_fwd(fin_fwd, fin_rows, fin_c, [], name="fin_fwd", br=2 * br, acc_shape=(8, 128))

    gw = {}
    dx1a, df, d_gt2, gw["norm_f_w"] = _rows_bwd(
        _f_fin, fin_rows, fin_c, [[]], wrt_rows=[0, 1], wrt_consts=[0, 1], drow_dtypes=[F32, BF16],
        name="fin_bwd", br=2 * br, unit_cot=True)
    dact = _mm(df, w_down, tb=True, name="mm_dact")
    gw["w_down"] = _rows_split(_mm(act, df, ta=True, out_dtype=BF16, name="mm_dw_down"))
    du, gw["conv_w"], gw["conv_b"] = _conv_bwd(u, wts["conv_w"], wts["conv_b"], dact)
    dh2 = _mm(du, w_up, tb=True, b_chip=True, out_dtype=BF16, name="mm_dh2")
    gw["w_up"] = _mm(h2, du, ta=True, out_chip=True, out_dtype=BF16, name="mm_dw_up")
    dxa, do, d_gt1, gw["norm2_w"], d_sc2, d_sh2 = _rows_bwd(
        _f_pre2, [(x, D, 0), (o, D, 0)], pre2_c, [[(dx1a, D, 0)], [(dh2, D, 0)]], wrt_rows=[0, 1],
        wrt_consts=[0, 1, 2, 3], drow_dtypes=[F32, BF16], name="pre2_bwd", br=2 * br)
    dmix = _mm(do, w_o, tb=True, out_dtype=BF16, name="mm_dmix")
    gw["w_o"] = _rows_split(_mm(mix, do, ta=True, out_dtype=BF16, name="mm_dw_o"))
    dgate, dya, dyr, gw["b_gate"] = _rows_bwd(
        _f_mix, mix_rows, [wts["b_gate"]], [[(dmix, D, 0)]], wrt_rows=[0, 1, 2], wrt_consts=[0],
        drow_dtypes=[BF16] * 3, name="mix_bwd", br=2 * br)
    datt = _mm(dya, w_ao, tb=True, out_dtype=BF16, name="mm_datt")
    gw["w_att_out"] = _mm(att, dya, ta=True, out_chip=True, out_dtype=BF16, name="mm_dw_att_out")
    drw = _mm(dyr, w_ro, tb=True, out_dtype=BF16, name="mm_drw")
    gw["w_rwkv_out"] = _rows_split(_mm(rw_out, dyr, ta=True, out_dtype=BF16, name="mm_dw_rw_out"))
    dcomb = _rows_bwd(_f_comb, comb_rows, [], [[(datt, ATT_WIDTH, 0)]], wrt_rows=list(range(6)), wrt_consts=[],
                      drow_dtypes=[F32] * 6, name="comb_bwd", br=2 * br)
    datt_in = None
    for g, (_, dil) in enumerate(ATT_PATTERNS):
        datt_in = _att_bwd(att_in, g, dil, dcomb[g], dcomb[3 + g], datt_in)
    dy_raw, dr_p, dv_p, dkm_p, dgg, gw["lnx_w"], gw["lnx_b"], gw["r_k"], *recv_early = _rows_bwd(
        _f_rwpost, post_rows, post_c, [[(drw, D, 0)]], wrt_rows=[0, 1, 2, 3, 4], wrt_consts=[0, 1, 2],
        drow_dtypes=[F32] * 5, name="rwpost_bwd", br=br, comm=_SiblingHalves([gw[n] for n in _DONE_EARLY]))
    st.update(loss=loss_acc[0, 0], gw=gw, dxa=dxa, dgate=dgate, datt_in=datt_in,
              dy_raw=dy_raw, dr_p=dr_p, dv_p=dv_p, dkm_p=dkm_p, dgg=dgg, d_ada_late=(d_gt1, d_sh2, d_sc2, d_gt2),
              recv_early=recv_early)
    return st


def _chip_parts(grads, recv, names, core):
    return [_half_sum(lambda a, b: a + b, [g], [r], False, BF16, core, "reduce_add2_" + n)
            for g, r, n in zip(grads, recv, names)]


def _step_after_scan(st, scan_grads, core):
    x, br, gw, h1, zs = st["x"], st["br"], st["gw"], st["h1"], st["zs"]
    dr_s, dlw, dkm_s, dv_s, daa, dbb = scan_grads
    pre_cots = [[(st["dr_p"], D, 0), (dr_s, D, 0)], [(dlw, D, 0)], [(st["dkm_p"], D, 0), (dkm_s, D, 0)],
                [(st["dv_p"], D, 0), (dv_s, D, 0)], [(daa, D, 0)], [(dbb, D, 0)], [(st["dgg"], D, 0)]]
    dz, dmu, gw["w0"], gw["a0"], gw["k_k"], gw["k_a"], dwl = _rwpre_shift_bwd(
        zs, st["z"], st["mu"], st["rwpre_c"], pre_cots, br=128)
    gw["w2"], gw["a2"] = _cols_split(dwl[0:64, 0:D]), _cols_split(dwl[64:128, D:2 * D])
    gw["g2"] = _cols_split(dwl[128:288, 2 * D:3 * D])
    gw["mu_shift"] = dmu[:, :N_RW]
    datt_in, dgate = st["datt_in"], st["dgate"]
    dw_in = [(_mm(datt_in, h1, ta=True, out_dtype=BF16, name="mm_dw_att"), N_ATT),
             (_mm(dz, h1, ta=True, out_dtype=BF16, name="mm_dw_rw"), N_RW),
             (_mm(dgate, h1, ta=True, out_dtype=BF16, name="mm_dw_gate"), N_GATE)]
    slabs = []
    for j in range(4):
        slabs += [_row_window(dw_in, j * W_IN_SHARD, (j + 1) * W_IN_SHARD), jnp.zeros((W_IN_PAD - W_IN_SHARD, D), BF16)]
    gw["w_in"] = jnp.concatenate(slabs, axis=0).reshape(4, W_IN_PAD, D)
    late = [gw[n] for n in _DONE_LATE]
    parts = _chip_parts(late, _run_comm(_SiblingHalves(late), "reduce_sib_late"), _DONE_LATE, core)
    dh1, slots_late = _mm_sum([(datt_in, st["w_att"]), (dz, st["w_rw"]), (dgate, st["w_gate"])],
                              comm=_ScatterToChips(parts), name="mm_dh1")
    grad_x, gw["norm1_w"], d_sc1, d_sh1 = _rows_bwd(
        _f_pre, [(x, D, 0)], st["pre1_c"], [[(dh1, D, 0)], [(st["dxa"], D, 0)]], wrt_rows=[0], wrt_consts=[0, 1, 2],
        drow_dtypes=[F32], name="pre1_bwd", br=2 * br)
    d_gt1, d_sh2, d_sc2, d_gt2 = st["d_ada_late"]
    return st["loss"], grad_x, (d_sh1, d_sc1, d_gt1, d_sh2, d_sc2, d_gt2), gw, slots_late


_SMALL = ("b_ada", "norm1_w", "b_gate", "mu_shift", "w0", "a0", "k_k", "k_a", "r_k", "lnx_w", "lnx_b", "norm2_w",
          "conv_b", "norm_f_w")
_NAMES = ("w_ada", "b_ada", "norm1_w", "w_in", "b_gate", "mu_shift", "w0", "w2", "a0", "a2", "g2", "k_k", "k_a", "r_k",
          "lnx_w", "lnx_b", "w_att_out", "w_rwkv_out", "w_o", "norm2_w", "w_up", "conv_w", "conv_b", "w_down",
          "norm_f_w")


def kernel(x, c, w_ada, b_ada, norm1_w, w_in, b_gate, mu_shift, w0, w2, a0, a2, g2, k_k, k_a, r_k, lnx_w, lnx_b, w_att_out, w_rwkv_out, w_o, norm2_w, w_up, conv_w, conv_b, w_down, norm_f_w, loss_target, m_w_ada, m_b_ada, m_norm1_w, m_w_in, m_b_gate, m_mu_shift, m_w0, m_w2, m_a0, m_a2, m_g2, m_k_k, m_k_a, m_r_k, m_lnx_w, m_lnx_b, m_w_att_out, m_w_rwkv_out, m_w_o, m_norm2_w, m_w_up, m_conv_w, m_conv_b, m_w_down, m_norm_f_w, v_w_ada, v_b_ada, v_norm1_w, v_w_in, v_b_gate, v_mu_shift, v_w0, v_w2, v_a0, v_a2, v_g2, v_k_k, v_k_a, v_r_k, v_lnx_w, v_lnx_b, v_w_att_out, v_w_rwkv_out, v_w_o, v_norm2_w, v_w_up, v_conv_w, v_conv_b, v_w_down, v_norm_f_w):
    args = dict(locals())
    p, pm, pv = {}, {}, {}
    for name in _NAMES:
        for dst, key in ((p, name), (pm, "m_" + name), (pv, "v_" + name)):
            t = args[key]
            if name == "w_in":
                dst[name] = jnp.swapaxes(t, 1, 2)[0]
            else:
                dst[name] = t.reshape(1, -1) if name in ("r_k", "norm_f_w") else t.reshape(t.shape[-2], t.shape[-1])
    xi, yi, ci = _me()
    chip = 2 * xi + yi
    dev = 4 * xi + 2 * yi + ci
    x2, tgt = x[0], loss_target[0]

    n_cw = 3 * (2 * D_FF // 4)
    vec = jnp.concatenate([c.reshape(-1), p["conv_w"].reshape(-1), jnp.zeros((8 * D - D - n_cw,), F32)]).reshape(8, D)
    g0 = _allgather8(vec, "gather_c").reshape(8, 8 * D)
    c_all = g0[:, :D]
    conv_w_full = jnp.concatenate([g0[2 * j, D:D + n_cw].reshape(3, -1) for j in range(4)], axis=1)
    n_ada = 6 * D // 4
    b_ada_sh = lax.dynamic_slice(p["b_ada"], (0, chip * n_ada), (1, n_ada))
    ada_sh = _ada_fwd(c_all, p["w_ada"], b_ada_sh)
    ga = _allgather8(ada_sh, "gather_ada")
    ada_all = jnp.concatenate([ga[2 * j] for j in range(4)], axis=1)
    ada_row = lax.dynamic_slice(ada_all, (dev, 0), (1, 6 * D))
    ada = [ada_row[:, j * D:(j + 1) * D] for j in range(6)]

    big = [n for n, _ in _BIG]
    shard = {n: p[n].astype(BF16) for n in big}
    shard["w_in"] = jnp.pad(shard["w_in"], ((0, W_IN_PAD - W_IN_SHARD), (0, 0)))
    wts = dict(zip(_NEEDED_FIRST, _run_comm(_GatherWeights([shard[n] for n in _NEEDED_FIRST]), "gather_w")))
    for n in _SMALL:
        wts[n] = p[n]
    wts["conv_w"] = conv_w_full
    core = ci.reshape(1).astype(jnp.int32)

    st = _step_to_scan(x2, tgt, ada, wts)
    y_raw, s0s, inverses, late = _scan_fwd(st["zs"], st["lw"], st["km"], st["aa"], st["bb"],
                                           _GatherWeights([shard[n] for n in _NEEDED_LATER]))
    st = _step_between_scans(st, y_raw, dict(zip(_NEEDED_LATER, late)))
    early = _chip_parts([st["gw"][n] for n in _DONE_EARLY], st["recv_early"], _DONE_EARLY, core)
    scan_grads, slots_early = _scan_bwd(st["zs"], st["lw"], st["km"], st["aa"], st["bb"], s0s, inverses,
                                        st["dy_raw"], _ScatterToChips(early))
    loss_part, grad_x, d_ada, gw, slots_late = _step_after_scan(st, scan_grads, core)

    small = [jnp.concatenate(d_ada, axis=1)] + [gw[n] for n in _SMALL[1:]] + [gw["conv_w"], loss_part.reshape(1, 1)]
    sizes = [t.size for t in small]
    flat = jnp.concatenate([t.reshape(-1) for t in small])
    npad = (-flat.shape[0]) % (8 * D)
    srows = (flat.shape[0] + npad) // D
    flat = jnp.concatenate([flat, jnp.zeros((npad,), F32)]).reshape(srows, D)
    parts = _allgather8(flat, "gather_small")
    tot = _sum_lead(parts, "sum_small").reshape(-1)
    pieces, pos = [], 0
    for sz in sizes:
        pieces.append(tot[pos:pos + sz])
        pos += sz
    grads = {}
    for n, piece in zip(_SMALL, pieces[:len(_SMALL)]):
        grads[n] = piece.reshape(p[n].shape)
    conv_w_grad = pieces[len(_SMALL)].reshape(3, 2 * D_FF)
    grads["conv_w"] = lax.dynamic_slice(conv_w_grad, (0, chip * (n_cw // 3)), (3, n_cw // 3))
    loss = pieces[-1][0]
    d_ada_all = parts[:, :6].reshape(8, 6 * D)
    grads["w_ada"] = _ada_bwd(c_all.T, lax.dynamic_slice(d_ada_all, (0, chip * n_ada), (8, n_ada)))

    order = _DONE_EARLY + _DONE_LATE
    reds = [_half_sum(lambda t: t[0] + t[1] + t[2] + t[3], [], [t], True, F32, core, "reduce_add4_" + n)
            for n, t in zip(order, list(slots_early) + list(slots_late))]
    for n, g in zip(order, _reduce_finish(reds, "reduce_sib2")):
        grads[n] = g

    outs_g, outs_d, outs_m, outs_v = [], [], [], []
    grads["w_in"] = grads["w_in"][:W_IN_SHARD]
    for name in _NAMES:
        g = grads[name]
        d, m, v = _adamw(p[name], g, pm[name], pv[name], "adamw_" + name)
        shape = args[name].shape
        for outs, t in ((outs_g, g), (outs_d, d), (outs_m, m), (outs_v, v)):
            outs.append(jnp.swapaxes(t[None], 1, 2) if name == "w_in" else t.reshape(shape))
    return (loss, grad_x.reshape(x.shape), *outs_g, *outs_d, *outs_m, *outs_v)
```

```python
import functools
import math

import jax
import jax.numpy as jnp
from jax import lax
from jax.experimental import pallas as pl
from jax.experimental.pallas import tpu as pltpu

F32 = jnp.float32
BF16 = jnp.bfloat16
HI = lax.Precision.HIGHEST
MESH = pl.DeviceIdType.MESH

D = 1024
ATT_PATTERNS = ((128, 1), (512, 4), (2048, 16))
ATT_BLOCK = 128
ATT_WIDTH = 512
N_ATT = 3 * 3 * ATT_WIDTH
N_RW = 3 * D + 64 + 64 + 160
N_RWP = 3456
N_LORA = N_RWP - 3 * D
N_GATE = 2 * D
D_FF = 2816
RMS_EPS = 1e-6
GN_EPS = 64e-5
SCAN_CHUNK = 64
SCAN_PAIRS = 8
NEG = -1e30
VMEM_LIMIT = 48 * 1024 * 1024
HALO = 16

ADAM_LR, ADAM_B1, ADAM_B2, ADAM_EPS, ADAM_WD, ADAM_STEP = 0.001, 0.9, 0.999, 1e-08, 0.01, 10


def _pcall(body, **kw):
    return pl.pallas_call(body, **kw)


def _cparams(sem):
    return pltpu.CompilerParams(dimension_semantics=sem, vmem_limit_bytes=VMEM_LIMIT)


def _div(n, pref, mult):
    best = None
    d = mult
    while d <= min(n, pref):
        if n % d == 0:
            best = d
        d += mult
    return best if best else n


def _dg(a, b, ca, cb):
    return lax.dot_general(a.astype(BF16), b.astype(BF16), (((ca,), (cb,)), ((), ())), preferred_element_type=F32)


@jax.custom_vjp
def _nn(a, b):
    return _dg(a, b, 1, 0)


@jax.custom_vjp
def _nt(a, b):
    return _dg(a, b, 1, 1)


@jax.custom_vjp
def _tn(a, b):
    return _dg(a, b, 0, 0)


_nn.defvjp(lambda a, b: (_nn(a, b), (a, b)), lambda res, g: (_nt(g, res[1]), _tn(res[0], g)))
_nt.defvjp(lambda a, b: (_nt(a, b), (a, b)), lambda res, g: (_nn(g, res[1]), _tn(g, res[0])))
_tn.defvjp(lambda a, b: (_tn(a, b), (a, b)), lambda res, g: (_nt(res[1], g), _nn(res[0], g)))


def _bdg(a, b, ca, cb):
    return lax.dot_general(a.astype(BF16), b.astype(BF16), (((ca,), (cb,)), ((0,), (0,))), preferred_element_type=F32)


@jax.custom_vjp
def _bnn(a, b):
    return _bdg(a, b, 2, 1)


@jax.custom_vjp
def _bnt(a, b):
    return _bdg(a, b, 2, 2)


@jax.custom_vjp
def _btn(a, b):
    return _bdg(a, b, 1, 1)


_bnn.defvjp(lambda a, b: (_bnn(a, b), (a, b)), lambda res, g: (_bnt(g, res[1]), _btn(res[0], g)))
_bnt.defvjp(lambda a, b: (_bnt(a, b), (a, b)), lambda res, g: (_bnn(g, res[1]), _btn(g, res[0])))
_btn.defvjp(lambda a, b: (_btn(a, b), (a, b)), lambda res, g: (_bnt(res[1], g), _bnn(res[0], g)))


def _hsum_impl(x, e, et):
    eb, etb = e.astype(BF16), et.astype(BF16)
    s = jnp.dot(x.astype(BF16), eb, preferred_element_type=F32)
    return jnp.dot(s.astype(BF16), etb, preferred_element_type=F32)


@jax.custom_vjp
def _hsum(x, e, et):
    return _hsum_impl(x, e, et)


_hsum.defvjp(lambda x, e, et: (_hsum_impl(x, e, et), (e, et)),
             lambda res, g: (_hsum_impl(g, res[0], res[1]), jnp.zeros_like(res[0]), jnp.zeros_like(res[1])))


def _mm(a, b, *, ta=False, tb=False, out_dtype=F32, add=None, b_chip=False, out_chip=False, comm=None, name):
    riding = _NOTHING if comm is None else comm
    nc = riding.n
    a_halves = a.ndim == 3
    b_halves = b.ndim == 3 and not b_chip
    if ta:
        kdim, m = a.shape
    elif a_halves:
        m, kdim = a.shape[1], 2 * a.shape[2]
    else:
        m, kdim = a.shape
    if b_chip:
        n = b.shape[1] if tb else 4 * b.shape[2]
    elif b_halves:
        n = 2 * b.shape[2]
    else:
        n = b.shape[0] if tb else b.shape[1]
    tm, tn, tk = _div(m, 1536, 128), _div(n, 1536, 128), _div(kdim, 2048 if ta else 1408, 128)
    if b_chip and tb:
        tk = kdim // 4
    if (b_chip and not tb) or out_chip:
        tn = n // 4
    nk = kdim // tk
    ca, cb = (0 if ta else 1), (1 if tb else 0)

    nin = 2 if add is None else 3
    gi, gj = m // tm, n // tn

    def body(*refs):
        a_ref, b_ref = refs[0], refs[1]
        add_ref = None if add is None else refs[2]
        o_ref = refs[nin + nc]
        step = (pl.program_id(0) * gj + pl.program_id(1)) * nk + pl.program_id(2)
        before, after = _comm_phases(riding, refs[nin:nin + nc] + refs[nin + nc + 1:nin + 2 * nc + 1]
                                     + refs[nin + 2 * nc + 1 + (nk > 1):], gi * gj * nk, step)
        before()
        part = lax.dot_general(a_ref[...], b_ref[...], (((ca,), (cb,)), ((), ())), preferred_element_type=F32)

        def finish(r):
            if add_ref is not None:
                r = r + add_ref[...]
            o_ref[...] = r.astype(o_ref.dtype)

        if nk == 1:
            finish(part)
            after()
            return
        acc = refs[nin + 2 * nc + 1]
        k = pl.program_id(2)

        @pl.when(k == 0)
        def _():
            acc[...] = part

        @pl.when(k > 0)
        def _():
            acc[...] += part

        @pl.when(k == nk - 1)
        def _():
            finish(acc[...])

        after()

    a_spec = pl.BlockSpec((tk, tm), lambda i, j, k: (k, i)) if ta else pl.BlockSpec((tm, tk), lambda i, j, k: (i, k))
    if a_halves:
        a_spec = pl.BlockSpec((None, tm, tk), lambda i, j, k: (k // (nk // 2), i, k % (nk // 2)))
    if b_halves:
        b_spec = pl.BlockSpec((None, tk, tn), lambda i, j, k: (j // (gj // 2), k, j % (gj // 2)))
    elif b_chip:
        b_spec = (pl.BlockSpec((None, tn, tk), lambda i, j, k: (k, j, 0)) if tb
                  else pl.BlockSpec((None, tk, tn), lambda i, j, k: (j, k, 0)))
    else:
        b_spec = pl.BlockSpec((tn, tk), lambda i, j, k: (j, k)) if tb else pl.BlockSpec((tk, tn), lambda i, j, k: (k, j))
    in_specs = [a_spec, b_spec]
    args = [a, b]
    if add is not None:
        in_specs.append(pl.BlockSpec((tm, tn), lambda i, j, k: (i, j)))
        args.append(add)
    if out_chip:
        out_spec = pl.BlockSpec((None, tm, tn), lambda i, j, k: (j, i, 0))
        out_shape = jax.ShapeDtypeStruct((4, m, tn), out_dtype)
    else:
        out_spec = pl.BlockSpec((tm, tn), lambda i, j, k: (i, j))
        out_shape = jax.ShapeDtypeStruct((m, n), out_dtype)
    res = _pcall(
        body, name=name, grid=(gi, gj, nk), in_specs=in_specs + [_HBM] * nc, out_specs=[out_spec] + [_HBM] * nc,
        out_shape=[out_shape] + riding.out_shape,
        scratch_shapes=([] if nk == 1 else [pltpu.VMEM((tm, tn), F32)]) + riding.sems,
        compiler_params=_cparams(("arbitrary",) * 3 if nc else ("parallel", "parallel", "arbitrary")),
    )(*args, *riding.ins)
    return res[0] if comm is None else (res[0], res[1:])


def _mm_sum(pairs, *, comm, name):
    m, n = pairs[0][0].shape[0], pairs[0][1].shape[1]
    tm, tn = _div(m, 1024, 128), _div(n, 1024, 128)
    tks = [_div(a.shape[1], 1408, 128) for a, _ in pairs]
    nks = [a.shape[1] // tk for (a, _), tk in zip(pairs, tks)]
    offs = [sum(nks[:p]) for p in range(len(pairs))]
    total, npair, nc = sum(nks), len(pairs), comm.n
    gi, gj = m // tm, n // tn

    def body(*refs):
        o_ref, acc = refs[2 * npair + nc], refs[2 * npair + 2 * nc + 1]
        k = pl.program_id(2)
        step = (pl.program_id(0) * gj + pl.program_id(1)) * total + k
        before, after = _comm_phases(comm, refs[2 * npair:2 * npair + nc]
                                     + refs[2 * npair + nc + 1:2 * npair + 2 * nc + 1]
                                     + refs[2 * npair + 2 * nc + 2:], gi * gj * total, step)
        before()
        for p in range(npair):
            def partial_product(p=p):
                part = jnp.dot(refs[2 * p][...], refs[2 * p + 1][...], preferred_element_type=F32)
                if p == 0:
                    @pl.when(k == 0)
                    def _():
                        acc[...] = part

                    @pl.when(k > 0)
                    def _():
                        acc[...] += part
                else:
                    acc[...] += part

            pl.when(jnp.logical_and(k >= offs[p], k < offs[p] + nks[p]))(partial_product)

        @pl.when(k == total - 1)
        def _():
            o_ref[...] = acc[...].astype(o_ref.dtype)

        after()

    def specs(tk, off, nk):
        def kb(k):
            return jnp.clip(k - off, 0, nk - 1)
        return [pl.BlockSpec((tm, tk), lambda i, j, k: (i, kb(k))), pl.BlockSpec((tk, tn), lambda i, j, k: (kb(k), j))]

    in_specs, args = [], []
    for (a, b), tk, off, nk in zip(pairs, tks, offs, nks):
        in_specs += specs(tk, off, nk)
        args += [a, b]
    res = _pcall(
        body, name=name, grid=(gi, gj, total), in_specs=in_specs + [_HBM] * nc,
        out_specs=[pl.BlockSpec((tm, tn), lambda i, j, k: (i, j))] + [_HBM] * nc,
        out_shape=[jax.ShapeDtypeStruct((m, n), BF16)] + comm.out_shape,
        scratch_shapes=[pltpu.VMEM((tm, tn), F32)] + comm.sems,
        compiler_params=_cparams(("arbitrary",) * 3),
    )(*args, *comm.ins)
    return res[0], res[1:]


def _row_spec(br, w, cb):
    return pl.BlockSpec((br, w), lambda i: (i, cb))


def _const_spec(shape):
    return pl.BlockSpec(shape, lambda i: (0,) * len(shape))


def _rows_fwd(fn, rows, consts, outs, *, name, br, acc_shape=None, halo=None):
    s = rows[0][0].shape[0]
    nr, nc = len(rows), len(consts)
    kept = [k for k, o in enumerate(outs) if o is not None]

    def body(*refs):
        xs = [r[...].astype(F32) for r in refs[:nr]]
        cs = [c[...] for c in refs[nr:nr + nc]]
        if halo is not None:
            cs.append(jnp.where(pl.program_id(0) == 0, 0.0, refs[nr + nc][...].astype(F32)))
        res = fn(*xs, *cs)
        orefs = refs[nr + nc + (halo is not None):]
        for j, k in enumerate(kept):
            orefs[j][...] = res[k].astype(orefs[j].dtype)
        if acc_shape is not None:
            acc_ref = orefs[len(kept)]

            @pl.when(pl.program_id(0) == 0)
            def _():
                acc_ref[...] = jnp.zeros_like(acc_ref)

            acc_ref[...] += res[len(outs)]

    in_specs = [_row_spec(br, w, cb) for (_, w, cb) in rows] + [_const_spec(c.shape) for c in consts]
    args = [r[0] for r in rows] + list(consts)
    if halo is not None:
        harr, hw, hcb = rows[halo]
        in_specs.append(pl.BlockSpec((HALO, hw), lambda i: (jnp.maximum(i * (br // HALO) - 1, 0), hcb)))
        args.append(harr)
    out_specs = [_row_spec(br, outs[k][0], 0) for k in kept]
    out_shape = [jax.ShapeDtypeStruct((s, outs[k][0]), outs[k][1]) for k in kept]
    if acc_shape is not None:
        out_specs.append(_const_spec(acc_shape))
        out_shape.append(jax.ShapeDtypeStruct(acc_shape, F32))
    return _pcall(
        body, name=name, grid=(pl.cdiv(s, br),), in_specs=in_specs, out_specs=out_specs, out_shape=out_shape,
        compiler_params=_cparams(("arbitrary",)),
    )(*args)


def _rows_bwd(fn, rows, consts, cots, *, wrt_rows, wrt_consts, drow_dtypes, name, br, unit_cot=False, comm=None):
    comm = _NOTHING if comm is None else comm
    ncomm = comm.n
    nout = len(wrt_rows) + len(wrt_consts)
    s = rows[0][0].shape[0]
    nr, nc = len(rows), len(consts)
    flat_cots = [c for lst in cots for c in lst]
    ncot = len(flat_cots)

    def body(*refs):
        xs = [r[...].astype(F32) for r in refs[:nr]]
        cs = [c[...] for c in refs[nr:nr + nc]]
        cvals = [c[...].astype(F32) for c in refs[nr + nc:nr + nc + ncot]]
        orefs = refs[nr + nc + ncot + ncomm:]
        before, after = _comm_phases(comm, refs[nr + nc + ncot:nr + nc + ncot + ncomm] + orefs[nout:], s // br)
        before()

        def g(*d):
            xs2, cs2 = list(xs), list(cs)
            for j, k in enumerate(wrt_rows):
                xs2[k] = d[j]
            for j, k in enumerate(wrt_consts):
                cs2[k] = d[len(wrt_rows) + j]
            return tuple(fn(*xs2, *cs2))

        prim = [xs[k] for k in wrt_rows] + [cs[k] for k in wrt_consts]
        outs, vjp = jax.vjp(g, *prim)
        ct = []
        pos = 0
        for o, lst in zip(outs, cots):
            if unit_cot:
                ct.append(jnp.ones_like(o))
                continue
            acc = jnp.zeros_like(o)
            for _ in lst:
                acc = acc + cvals[pos]
                pos += 1
            ct.append(acc)
        grads = vjp(tuple(ct))
        for j in range(len(wrt_rows)):
            orefs[j][...] = grads[j].astype(orefs[j].dtype)

        @pl.when(pl.program_id(0) == 0)
        def _():
            for j in range(len(wrt_consts)):
                oref = orefs[len(wrt_rows) + j]
                oref[...] = jnp.zeros_like(oref)

        for j in range(len(wrt_consts)):
            orefs[len(wrt_rows) + j][...] += grads[len(wrt_rows) + j]
        after()

    in_specs = ([_row_spec(br, w, cb) for (_, w, cb) in rows] + [_const_spec(c.shape) for c in consts]
                + [_row_spec(br, w, cb) for (_, w, cb) in flat_cots] + [_HBM] * ncomm)
    out_specs = ([_row_spec(br, rows[k][1], 0) for k in wrt_rows] + [_const_spec(consts[k].shape) for k in wrt_consts]
                 + [_HBM] * ncomm)
    out_shape = ([jax.ShapeDtypeStruct((s, rows[k][1]), dt) for k, dt in zip(wrt_rows, drow_dtypes)]
                 + [jax.ShapeDtypeStruct(consts[k].shape, F32) for k in wrt_consts] + comm.out_shape)
    return _pcall(
        body, name=name, grid=(s // br,), in_specs=in_specs, out_specs=out_specs, out_shape=out_shape,
        scratch_shapes=comm.sems, compiler_params=_cparams(("arbitrary",)),
    )(*[r[0] for r in rows], *consts, *[c[0] for c in flat_cots], *comm.ins)


def _rms(x, w):
    return x * lax.rsqrt(jnp.mean(x * x, axis=-1, keepdims=True) + RMS_EPS) * w


def _f_pre(x, nw, sc, sh):
    return _rms(x, nw) * (1.0 + sc) + sh, x


def _f_pre2(x, o, gt, nw, sc, sh):
    x1 = x + gt * o
    return x1, _rms(x1, nw) * (1.0 + sc) + sh


def _f_fin(x1, f, tgt, gt, nfw):
    y = _rms(x1 + gt * f, nfw)
    return (0.5 * jnp.mean(jnp.square(y - tgt), axis=-1, keepdims=True),)


def _f_comb(o1, o2, o3, l1, l2, l3):
    m = lax.stop_gradient(jnp.maximum(jnp.maximum(l1, l2), l3))
    e1, e2, e3 = jnp.exp(l1 - m), jnp.exp(l2 - m), jnp.exp(l3 - m)
    return ((e1 * o1 + e2 * o2 + e3 * o3) / (e1 + e2 + e3),)


def _f_rwpre(zs, w0, a0, k_k, k_a, wl, e, et):
    r, k, v, zl = zs[:, 0:D], zs[:, D:2 * D], zs[:, 2 * D:3 * D], zs[:, 3 * D:N_RWP]
    lane = lax.broadcasted_iota(jnp.int32, zl.shape, 1)
    t = jnp.where(lane < 64, jnp.tanh(zl), jnp.where(lane < 128, zl, jnp.where(lane < 288, jax.nn.sigmoid(zl), 0.0)))
    lo = _nn(t[:, 0:128], wl[0:128, 0:2 * D])
    g = _nn(t[:, 128:N_LORA], wl[128:N_LORA, 2 * D:3 * D])
    lw = -math.exp(-0.5) * jax.nn.sigmoid(w0 + lo[:, 0:D])
    a = jax.nn.sigmoid(a0 + lo[:, D:2 * D])
    k_mod = k * (1.0 + (a - 1.0) * k_a)
    kk = k * k_k
    kk = kk / jnp.maximum(jnp.sqrt(_hsum(kk * kk, e, et)), 1e-12)
    return r, lw, k_mod, v, -kk, kk * a, g


def _f_rwpost(y, r, v, k_mod, g, lnx_w, lnx_b, r_k, e, et):
    mean = _hsum(y, e, et) * (1.0 / 64)
    yc = y - mean
    var = _hsum(yc * yc, e, et) * (1.0 / 64)
    yn = yc * lax.rsqrt(var + GN_EPS) * lnx_w + lnx_b
    bonus = _hsum(r * k_mod * r_k, e, et) * v
    return ((yn + bonus) * g,)


def _f_mix(gi, ya, yr, bg):
    gate = jax.nn.sigmoid(gi + bg)
    return (gate[:, 0:D] * ya + gate[:, D:2 * D] * yr,)


def _f_adamw(w, g, m, v):
    m = ADAM_B1 * m + (1.0 - ADAM_B1) * g
    v = ADAM_B2 * v + (1.0 - ADAM_B2) * jnp.square(g)
    m_hat = m / (1.0 - ADAM_B1 ** ADAM_STEP)
    v_hat = v / (1.0 - ADAM_B2 ** ADAM_STEP)
    return -ADAM_LR * (m_hat / (jnp.sqrt(v_hat) + ADAM_EPS) + ADAM_WD * w), m, v


def _down(x, k):
    row = lax.broadcasted_iota(jnp.int32, x.shape, 0)
    return jnp.where(row < k, 0.0, pltpu.roll(x, k, 0))


def _up(x, k):
    n = x.shape[0]
    row = lax.broadcasted_iota(jnp.int32, x.shape, 0)
    return jnp.where(row >= n - k, 0.0, pltpu.roll(x, n - k, 0))


def _col_spec(s, w, off=0):
    return pl.BlockSpec((s, w), lambda j: (0, j + off))


def _rwpre_shift_bwd(zs, z, mu, consts, cots, *, br):
    s, w = zs.shape
    n = s // br
    flat = [c for lst in cots for c in lst]
    nc, ncot, nwrt = len(consts), len(flat), 5

    def this(i):
        return jnp.minimum(i, n - 1)

    def last(i):
        return jnp.maximum(i - 1, 0)

    def body(*refs):
        zs_ref, z_ref, zh_ref, mu_ref = refs[:4]
        c_refs, cot_refs = refs[4:4 + nc], refs[4 + nc:4 + nc + ncot]
        dz_ref, dmu_ref = refs[4 + nc + ncot:6 + nc + ncot]
        dc_refs = refs[6 + nc + ncot:6 + nc + ncot + nwrt]
        kept = refs[-1]
        i = pl.program_id(0)

        @pl.when(i == 0)
        def _():
            dmu_ref[...] = jnp.zeros_like(dmu_ref)
            for ref in dc_refs:
                ref[...] = jnp.zeros_like(ref)

        cs = [c[...] for c in c_refs]

        def g(zz, *d):
            return tuple(_f_rwpre(zz, *d, *cs[nwrt:]))

        outs, vjp = jax.vjp(g, zs_ref[...], *cs[:nwrt])
        cts, pos = [], 0
        for o, lst in zip(outs, cots):
            acc = jnp.zeros_like(o)
            for _ in lst:
                acc = acc + cot_refs[pos][...].astype(F32)
                pos += 1
            cts.append(acc)
        grads = vjp(tuple(cts))
        dzs_new = grads[0]

        @pl.when(i < n)
        def _():
            for ref, gr in zip(dc_refs, grads[1:]):
                ref[...] += gr

        @pl.when(i > 0)
        def _():
            d, m = kept[...], mu_ref[...]
            row = lax.broadcasted_iota(jnp.int32, d.shape, 0)
            head = jnp.sum(jnp.where(row == 0, dzs_new, 0.0), axis=0, keepdims=True)
            head = jnp.where(i < n, head, 0.0)
            dm = d * m
            after = jnp.where(row == br - 1, head * m, pltpu.roll(dm, br - 1, 0))
            dz_ref[...] = (d - dm + after).astype(dz_ref.dtype)
            zz, halo = z_ref[...].astype(F32), zh_ref[...].astype(F32)
            tail = jnp.sum(jnp.where(lax.broadcasted_iota(jnp.int32, halo.shape, 0) == HALO - 1, halo, 0.0), axis=0,
                           keepdims=True)
            before = jnp.where(row == 0, jnp.where(i > 1, tail, 0.0), pltpu.roll(zz, 1, 0))
            dmu_ref[...] += jnp.sum(d * (before - zz), axis=0, keepdims=True)

        kept[...] = dzs_new

    in_specs = ([pl.BlockSpec((br, w), lambda i: (this(i), 0)), pl.BlockSpec((br, w), lambda i: (last(i), 0)),
                 pl.BlockSpec((HALO, w), lambda i: (jnp.maximum(last(i) * (br // HALO) - 1, 0), 0)),
                 _const_spec(mu.shape)] + [_const_spec(c.shape) for c in consts]
                + [pl.BlockSpec((br, cw), lambda i, cb=cb: (this(i), cb)) for (_, cw, cb) in flat])
    out_specs = ([pl.BlockSpec((br, w), lambda i: (last(i), 0)), _const_spec(mu.shape)]
                 + [_const_spec(consts[k].shape) for k in range(nwrt)])
    out_shape = ([jax.ShapeDtypeStruct((s, w), BF16), jax.ShapeDtypeStruct(mu.shape, F32)]
                 + [jax.ShapeDtypeStruct(consts[k].shape, F32) for k in range(nwrt)])
    return _pcall(
        body, name="rwpre_shift_bwd", grid=(n + 1,), in_specs=in_specs, out_specs=out_specs, out_shape=out_shape,
        scratch_shapes=[pltpu.VMEM((br, w), F32)], compiler_params=_cparams(("arbitrary",)),
    )(zs, z, z, mu, *consts, *[c[0] for c in flat])


def _conv3(x, w_ref, b_ref):
    return b_ref[...] + w_ref[0:1, :] * _down(x, 2) + w_ref[1:2, :] * _down(x, 1) + w_ref[2:3, :] * x


def _conv_fwd(u, cw, cb):
    s = u.shape[0]
    nb = D_FF // 128

    def body(ug_ref, uv_ref, wg_ref, wv_ref, bg_ref, bv_ref, o_ref):
        gate = _conv3(ug_ref[...], wg_ref, bg_ref)
        val = _conv3(uv_ref[...], wv_ref, bv_ref)
        o_ref[...] = (gate * jax.nn.sigmoid(gate) * val).astype(o_ref.dtype)

    return _pcall(
        body, name="conv_fwd", grid=(nb,),
        in_specs=[_col_spec(s, 128), _col_spec(s, 128, nb), _col_spec(3, 128), _col_spec(3, 128, nb),
                  _col_spec(1, 128), _col_spec(1, 128, nb)],
        out_specs=_col_spec(s, 128), out_shape=jax.ShapeDtypeStruct((s, D_FF), BF16),
        compiler_params=_cparams(("parallel",)),
    )(u, u, cw, cw, cb, cb)


def _conv_bwd(u, cw, cb, dact):
    s = u.shape[0]
    nb = D_FF // 128

    def half(x, d, w_ref, du_ref, dw_ref, db_ref):
        x1, x2 = _down(x, 1), _down(x, 2)
        du_ref[...] = (w_ref[2:3, :] * d + w_ref[1:2, :] * _up(d, 1) + w_ref[0:1, :] * _up(d, 2)).astype(du_ref.dtype)
        dw_ref[0:1, :] = jnp.sum(d * x2, axis=0, keepdims=True)
        dw_ref[1:2, :] = jnp.sum(d * x1, axis=0, keepdims=True)
        dw_ref[2:3, :] = jnp.sum(d * x, axis=0, keepdims=True)
        db_ref[...] = jnp.sum(d, axis=0, keepdims=True)

    def body(ug_ref, uv_ref, wg_ref, wv_ref, bg_ref, bv_ref, da_ref,
             du_ref, dwg_ref, dwv_ref, dbg_ref, dbv_ref):
        ug, uv, da = ug_ref[...], uv_ref[...], da_ref[...]
        gate = _conv3(ug, wg_ref, bg_ref)
        val = _conv3(uv, wv_ref, bv_ref)
        sg = jax.nn.sigmoid(gate)
        dgate = da * val * sg * (1.0 + gate * (1.0 - sg))
        dval = da * gate * sg
        half(ug, dgate, wg_ref, du_ref.at[0], dwg_ref, dbg_ref)
        half(uv, dval, wv_ref, du_ref.at[1], dwv_ref, dbv_ref)

    du, dwg, dwv, dbg, dbv = _pcall(
        body, name="conv_bwd", grid=(nb,),
        in_specs=[_col_spec(s, 128), _col_spec(s, 128, nb), _col_spec(3, 128), _col_spec(3, 128, nb),
                  _col_spec(1, 128), _col_spec(1, 128, nb), _col_spec(s, 128)],
        out_specs=[pl.BlockSpec((2, s, 128), lambda j: (0, 0, j)), _col_spec(3, 128), _col_spec(3, 128),
                   _col_spec(1, 128), _col_spec(1, 128)],
        out_shape=[jax.ShapeDtypeStruct((2, s, D_FF), BF16),
                   jax.ShapeDtypeStruct((3, D_FF), F32), jax.ShapeDtypeStruct((3, D_FF), F32),
                   jax.ShapeDtypeStruct((1, D_FF), F32), jax.ShapeDtypeStruct((1, D_FF), F32)],
        compiler_params=_cparams(("parallel",)),
    )(u, u, cw, cw, cb, cb, dact)
    return du, jnp.concatenate([dwg, dwv], axis=1), jnp.concatenate([dbg, dbv], axis=1)


ATT_BATCH = 4


def _att_batch(q, kp, kc, vp, vc, first):
    ma = lax.broadcasted_iota(jnp.int32, (1, ATT_BLOCK, 128), 2) < 64

    def diag(x):
        return jnp.concatenate([jnp.where(ma, x, 0.0), jnp.where(ma, 0.0, x)], axis=1)

    qi = lax.broadcasted_iota(jnp.int32, (1, ATT_BLOCK, 2 * ATT_BLOCK), 1)
    kj = lax.broadcasted_iota(jnp.int32, (1, ATT_BLOCK, 2 * ATT_BLOCK), 2) & (ATT_BLOCK - 1)
    okp = kj >= qi + jnp.where(first, 2 * ATT_BLOCK, 0)
    okc = kj <= qi
    sp = jnp.where(okp, _bnt(q, diag(kp)) * 0.125, NEG)
    sc = jnp.where(okc, _bnt(q, diag(kc)) * 0.125, NEG)

    def per_head(fn, x):
        return fn(x[..., :ATT_BLOCK]), fn(x[..., ATT_BLOCK:])

    def spread(ab):
        return jnp.concatenate([jnp.broadcast_to(t, t.shape[:2] + (ATT_BLOCK,)) for t in ab], axis=-1)

    row_max = functools.partial(jnp.max, axis=-1, keepdims=True)
    row_sum = functools.partial(jnp.sum, axis=-1, keepdims=True)
    m = [lax.stop_gradient(jnp.maximum(a, b)) for a, b in zip(per_head(row_max, sp), per_head(row_max, sc))]
    pp, pc = jnp.exp(sp - spread(m)), jnp.exp(sc - spread(m))
    den = [a + b for a, b in zip(per_head(row_sum, pp), per_head(row_sum, pc))]
    num = _bnn(pp, diag(vp)) + _bnn(pc, diag(vc))
    out = num / jnp.where(ma, den[0], den[1])
    lse = jnp.where(ma, m[0] + jnp.log(den[0]), m[1] + jnp.log(den[1]))
    return out, jnp.broadcast_to(lse, out.shape)


def _att_pairs_per_step(dil):
    return 4 if dil == 1 else 1


def _att_residues(dil):
    return min(dil, ATT_BATCH // _att_pairs_per_step(dil))


def _att_specs(g, dil):
    rows, pp = ATT_BLOCK * dil, _att_pairs_per_step(dil)

    def cur(slot):
        return pl.BlockSpec((rows, 128 * pp), lambda n, p: (n, (g * 3 + slot) * (4 // pp) + p))

    def prev(slot):
        return pl.BlockSpec((rows, 128 * pp), lambda n, p: (jnp.maximum(n - 1, 0), (g * 3 + slot) * (4 // pp) + p))

    return [cur(0), prev(1), cur(1), prev(2), cur(2)]


def _att_out_spec(dil):
    return pl.BlockSpec((ATT_BLOCK * dil, 128 * _att_pairs_per_step(dil)), lambda n, p: (n, p))


def _att_grid(s, dil):
    return (s // (ATT_BLOCK * dil), 4 // _att_pairs_per_step(dil))


def _att_windows(i, dil):
    res = _att_residues(dil)

    def rows(r):
        return pl.ds(i * res + r, ATT_BLOCK, stride=dil) if dil > 1 else pl.ds(0, ATT_BLOCK)

    return [(rows(r), pl.ds(128 * j, 128)) for j in range(_att_pairs_per_step(dil)) for r in range(res)]


def _att_fwd(att_in, g, dil):
    s = att_in.shape[0]

    def body(q_ref, kp_ref, kc_ref, vp_ref, vc_ref, o_ref, l_ref):
        first = pl.program_id(0) == 0

        def one(i, carry):
            win = _att_windows(i, dil)
            vals = [jnp.stack([ref[w] for w in win]) for ref in (q_ref, kp_ref, kc_ref, vp_ref, vc_ref)]
            o, l = _att_batch(*vals, first)
            for j, w in enumerate(win):
                o_ref[w] = o[j]
                l_ref[w] = l[j]
            return carry

        lax.fori_loop(0, dil // _att_residues(dil), one, 0)

    return _pcall(
        body, name=f"att_fwd{g}", grid=_att_grid(s, dil), in_specs=_att_specs(g, dil),
        out_specs=[_att_out_spec(dil)] * 2, out_shape=[jax.ShapeDtypeStruct((s, ATT_WIDTH), F32)] * 2,
        compiler_params=_cparams(("parallel", "parallel")),
    )(att_in, att_in, att_in, att_in, att_in)


def _att_bwd(att_in, g, dil, do, dl, acc):
    s = att_in.shape[0]
    rows, pp = ATT_BLOCK * dil, _att_pairs_per_step(dil)
    nb, npair, wid = s // rows, 4 // pp, 128 * pp

    def body(q_ref, kp_ref, kc_ref, vp_ref, vc_ref, do_ref, dl_ref, *rest):
        o_ref, lag_q, lag_k, lag_v = rest[-9:-5]
        stage = rest[-5:]
        n, p = pl.program_id(0), pl.program_id(1)
        first = n == 0

        @pl.when(n < nb)
        def _():
            def one(i, carry):
                win = _att_windows(i, dil)
                vals = [jnp.stack([ref[w] for w in win]) for ref in (q_ref, kp_ref, kc_ref, vp_ref, vc_ref)]
                _, vjp = jax.vjp(lambda *a: _att_batch(*a, first), *vals)
                grads = vjp((jnp.stack([do_ref[w] for w in win]), jnp.stack([dl_ref[w] for w in win])))
                for ref, gr in zip(stage, grads):
                    for j, w in enumerate(win):
                        ref[w] = gr[j]
                return carry

            lax.fori_loop(0, dil // _att_residues(dil), one, 0)

        live = n < nb
        for pj in range(npair):
            @pl.when((n > 0) & (p == pj))
            def _(pj=pj):
                c = pj * wid
                o_ref[:, c:c + wid] = lag_q[pj]
                o_ref[:, ATT_WIDTH + c:ATT_WIDTH + c + wid] = (
                    lag_k[pj] + jnp.where(live, stage[1][...], 0.0)).astype(BF16)
                o_ref[:, 2 * ATT_WIDTH + c:2 * ATT_WIDTH + c + wid] = (
                    lag_v[pj] + jnp.where(live, stage[3][...], 0.0)).astype(BF16)

        @pl.when(live)
        def _():
            lag_q[p] = stage[0][...].astype(BF16)
            lag_k[p] = stage[2][...]
            lag_v[p] = stage[4][...]

    def col(slot, n, p):
        return (g * 3 + slot) * npair + jnp.where(n < nb, p, npair - 1)

    def cur(slot):
        return pl.BlockSpec((rows, wid), lambda n, p: (jnp.minimum(n, nb - 1), col(slot, n, p)))

    def prev(slot):
        return pl.BlockSpec((rows, wid), lambda n, p: (jnp.maximum(jnp.minimum(n, nb - 1) - 1, 0), col(slot, n, p)))

    cot = pl.BlockSpec((rows, wid), lambda n, p: (jnp.minimum(n, nb - 1), jnp.where(n < nb, p, npair - 1)))
    carried = [] if acc is None else [acc]
    return _pcall(
        body, name=f"att_bwd{g}", grid=(nb + 1, npair),
        in_specs=[cur(0), prev(1), cur(1), prev(2), cur(2), cot, cot] + [pl.BlockSpec(memory_space=pl.ANY)] * len(carried),
        out_specs=pl.BlockSpec((rows, 3 * ATT_WIDTH), lambda n, p: (jnp.maximum(n - 1, 0), g)),
        out_shape=jax.ShapeDtypeStruct((s, N_ATT), BF16), input_output_aliases={7: 0} if carried else {},
        scratch_shapes=([pltpu.VMEM((npair, rows, wid), BF16)] + [pltpu.VMEM((npair, rows, wid), F32)] * 2
                        + [pltpu.VMEM((rows, wid), F32)] * 5),
        compiler_params=_cparams(("arbitrary", "arbitrary")),
    )(att_in, att_in, att_in, att_in, att_in, do, dl, *carried)


def _cumsum_rows_impl(x):
    row = lax.broadcasted_iota(jnp.int32, x.shape, 0)
    shift = 1
    while shift < x.shape[0]:
        x = x + jnp.where(row >= shift, pltpu.roll(x, shift, 0), 0.0)
        shift *= 2
    return x


@jax.custom_vjp
def _cumsum_rows(x):
    return _cumsum_rows_impl(x)


_cumsum_rows.defvjp(lambda x: (_cumsum_rows_impl(x), None),
                    lambda _, g: (jnp.sum(g, axis=0, keepdims=True) - _cumsum_rows_impl(g) + g,))


def _unit_lower_inverse_impl(n):
    eye = (lax.broadcasted_iota(jnp.int32, (1,) + n.shape[1:], 1)
           == lax.broadcasted_iota(jnp.int32, (1,) + n.shape[1:], 2))
    t = jnp.where(eye, 1.0, 0.0) + n
    pw = n
    for _ in range(5):
        pw = _bnn(pw, pw)
        t = t + _bnn(t, pw)
    return t


@jax.custom_vjp
def _unit_lower_inverse(n):
    return _unit_lower_inverse_impl(n)


def _unit_lower_inverse_fwd(n):
    t = _unit_lower_inverse_impl(n)
    return t, t


_unit_lower_inverse.defvjp(_unit_lower_inverse_fwd, lambda t, g: (_bnt(_btn(t, g), t),))


@jax.custom_vjp
def _known_inverse(n, t):
    return t


_known_inverse.defvjp(lambda n, t: (t, t), lambda t, g: (_bnt(_btn(t, g), t), jnp.zeros_like(t)))


def _scan_chunk(r, lw, k, v, a, b, s0, inverse):
    c = SCAN_CHUNK
    p = s0.shape[0]
    cum = _cumsum_rows(lw)
    tot = jnp.sum(lw, axis=0, keepdims=True)
    ma = (lax.broadcasted_iota(jnp.int32, (c, 128 * p), 1) & 127) < 64

    def pairs(x):
        return jnp.concatenate([x[None, :, 128 * j:128 * (j + 1)] for j in range(p)], axis=0)

    def stack(x):
        return jnp.concatenate([pairs(jnp.where(ma, x, 0.0)), pairs(jnp.where(ma, 0.0, x))], axis=1)

    einv, eend = jnp.exp(-cum), jnp.exp(tot - cum)
    ra, aa = stack(r * jnp.exp(cum)), stack(a * jnp.exp(cum - lw))
    bi, ki, be, ke, vs = stack(b * einv), stack(k * einv), stack(b * eend), stack(k * eend), stack(v)
    r2 = lax.broadcasted_iota(jnp.int32, (1, 2 * c, 2 * c), 1)
    c2 = lax.broadcasted_iota(jnp.int32, (1, 2 * c, 2 * c), 2)
    same = (r2 >= c) == (c2 >= c)
    strict = jnp.logical_and(same, c2 < r2)
    incl = jnp.logical_and(same, c2 <= r2)
    s0 = jnp.where(same, s0, 0.0)
    prod = _bnt(jnp.concatenate([aa, ra], axis=1), jnp.concatenate([bi, ki], axis=1))
    a_ab = jnp.where(strict, prod[:, :2 * c, :2 * c], 0.0)
    a_ak = jnp.where(strict, prod[:, :2 * c, 2 * c:], 0.0)
    a_rb = jnp.where(incl, prod[:, 2 * c:, :2 * c], 0.0)
    a_rk = jnp.where(incl, prod[:, 2 * c:, 2 * c:], 0.0)
    t = inverse(a_ab)
    u = _bnn(t, _bnt(aa, s0) + _bnn(a_ak, vs))
    uv = jnp.concatenate([u, vs], axis=1)
    ys = _bnt(ra, s0) + _bnn(jnp.concatenate([a_rb, a_rk], axis=2), uv)
    s1 = s0 * pairs(jnp.exp(tot)) + _btn(uv, jnp.concatenate([be, ke], axis=1))
    y3 = ys[:, :c] + ys[:, c:]
    return (jnp.concatenate([y3[j] for j in range(p)], axis=1), s1), t


def _scan_specs(rev, n):
    def at(i):
        return n - 1 - i if rev else i

    def cm(cb):
        return pl.BlockSpec((SCAN_CHUNK, D), lambda i: (at(i), cb))

    return cm, pl.BlockSpec((1, SCAN_PAIRS, 128, 128), lambda i: (at(i), 0, 0, 0))


def _comm_phases(comm, refs, n, step=None):
    k = comm.n
    srcs, outs, sems = refs[:k], refs[k:2 * k], refs[2 * k:]
    i = pl.program_id(0) if step is None else step

    def before():
        @pl.when(i == 0)
        def _():
            comm.first(srcs, outs, sems)

    def after():
        if comm.mid is not None:
            @pl.when(i == (3 * n) // 4)
            def _():
                comm.mid(srcs, outs, sems)

        @pl.when(i == n - 1)
        def _():
            comm.last(srcs, outs, sems)

    return before, after


def _scan_fwd(zs, lw, km, aa, bb, comm):
    s = zs.shape[0]
    n = s // SCAN_CHUNK
    cm, st = _scan_specs(False, n)
    k = comm.n

    def body(*refs):
        r_ref, lw_ref, k_ref, v_ref, a_ref, b_ref = refs[:6]
        y_ref, s0_ref, t_ref = refs[6 + k:9 + k]
        state = refs[9 + 2 * k]
        before, after = _comm_phases(comm, refs[6:6 + k] + refs[9 + k:9 + 2 * k] + refs[10 + 2 * k:], n)
        before()

        @pl.when(pl.program_id(0) == 0)
        def _():
            state[...] = jnp.zeros_like(state)

        s0 = state[...]
        s0_ref[0] = s0
        (y, s1), t = _scan_chunk(*[ref[...] for ref in (r_ref, lw_ref, k_ref, v_ref, a_ref, b_ref)], s0,
                                 _unit_lower_inverse)
        y_ref[...] = y
        t_ref[0] = t.astype(BF16)
        state[...] = s1
        after()

    per_chunk = (n, SCAN_PAIRS, 128, 128)
    res = _pcall(
        body, name="scan_fwd", grid=(n,), in_specs=[cm(0), cm(0), cm(0), cm(2), cm(0), cm(0)] + [_HBM] * k,
        out_specs=[cm(0), st, st] + [_HBM] * k,
        out_shape=[jax.ShapeDtypeStruct((s, D), F32), jax.ShapeDtypeStruct(per_chunk, F32),
                   jax.ShapeDtypeStruct(per_chunk, BF16)] + comm.out_shape,
        scratch_shapes=[pltpu.VMEM((SCAN_PAIRS, 128, 128), F32)] + comm.sems,
        compiler_params=_cparams(("arbitrary",)),
    )(zs, lw, km, zs, aa, bb, *comm.ins)
    return res[0], res[1], res[2], res[3:]


def _scan_bwd(zs, lw, km, aa, bb, s0s, ts, dy, comm):
    s = zs.shape[0]
    n = s // SCAN_CHUNK
    cm, st = _scan_specs(True, n)
    k = comm.n

    def body(*refs):
        r_ref, lw_ref, k_ref, v_ref, a_ref, b_ref, s0_ref, t_ref, dy_ref = refs[:9]
        douts = refs[9 + k:15 + k]
        dstate = refs[15 + 2 * k]
        before, after = _comm_phases(comm, refs[9:9 + k] + refs[15 + k:15 + 2 * k] + refs[16 + 2 * k:], n)
        before()

        @pl.when(pl.program_id(0) == 0)
        def _():
            dstate[...] = jnp.zeros_like(dstate)

        t = t_ref[0].astype(F32)
        prim = [ref[...] for ref in (r_ref, lw_ref, k_ref, v_ref, a_ref, b_ref)] + [s0_ref[0]]
        _, vjp, _ = jax.vjp(lambda *p: _scan_chunk(*p, lambda nil: _known_inverse(nil, t)), *prim, has_aux=True)
        grads = vjp((dy_ref[...], dstate[...]))
        for ref, gr in zip(douts, grads[:6]):
            ref[...] = gr
        dstate[...] = grads[6]
        after()

    res = _pcall(
        body, name="scan_bwd", grid=(n,),
        in_specs=[cm(0), cm(0), cm(0), cm(2), cm(0), cm(0), st, st, cm(0)] + [_HBM] * k,
        out_specs=[cm(0)] * 6 + [_HBM] * k, out_shape=[jax.ShapeDtypeStruct((s, D), F32)] * 6 + comm.out_shape,
        scratch_shapes=[pltpu.VMEM((SCAN_PAIRS, 128, 128), F32)] + comm.sems,
        compiler_params=_cparams(("arbitrary",)),
    )(zs, lw, km, zs, aa, bb, s0s, ts, dy, *comm.ins)
    return res[:6], res[6:]


_HBM = pl.BlockSpec(memory_space=pltpu.HBM)


def _me():
    return lax.axis_index("x"), lax.axis_index("y"), lax.axis_index("c")


def _allgather8(src, name):
    def body(src_ref, out_ref, ssem, rsem, lsem):
        x, y, c = _me()
        me = 4 * x + 2 * y + c
        local = pltpu.make_async_copy(src_ref, out_ref.at[me], lsem)
        local.start()
        peers = []
        for k in range(1, 8):
            peers.append(((1 - x) if k & 4 else x, (1 - y) if k & 2 else y, (1 - c) if k & 1 else c))
        sends = []
        for k, peer in enumerate(peers):
            cp = pltpu.make_async_remote_copy(src_ref, out_ref.at[me], ssem.at[k], rsem.at[k], device_id=peer,
                                              device_id_type=MESH)
            cp.start()
            sends.append(cp)
        for k, (px, py, pc) in enumerate(peers):
            pltpu.make_async_remote_copy(src_ref, out_ref.at[4 * px + 2 * py + pc], ssem.at[k], rsem.at[k],
                                         device_id=(px, py, pc), device_id_type=MESH).wait_recv()
        for cp in sends:
            cp.wait_send()
        local.wait()

    return _pcall(
        body, name=name, in_specs=[_HBM], out_specs=_HBM, out_shape=jax.ShapeDtypeStruct((8,) + src.shape, src.dtype),
        scratch_shapes=[pltpu.SemaphoreType.DMA((7,)), pltpu.SemaphoreType.DMA((7,)), pltpu.SemaphoreType.DMA],
    )(src)


def _other_chips(x, y):
    return [(1 - x, y), (x, 1 - y), (1 - x, 1 - y)]


def _remote(src, dst, ssem, rsem, to):
    return pltpu.make_async_remote_copy(src, dst, ssem, rsem, device_id=to, device_id_type=MESH)


class _GatherWeights:
    def __init__(self, shards):
        self.ins = list(shards)
        n = self.n = len(shards)
        self.out_shape = [jax.ShapeDtypeStruct((4,) + t.shape, t.dtype) for t in shards]
        self.sems = [pltpu.SemaphoreType.DMA((6 * n,)), pltpu.SemaphoreType.DMA((6 * n,)),
                     pltpu.SemaphoreType.DMA((n,)), pltpu.SemaphoreType.DMA((n,))]

    def _copies(self, srcs, outs, sems):
        ssem, rsem, lsem, osem = sems
        x, y, c = _me()
        me = 2 * x + y
        own, ici, landed, passed, passed_in = [], [], [], [], []
        for a in range(self.n):
            h = self.ins[a].shape[0] // 2
            mine, other = pl.ds(c * h, h), pl.ds((1 - c) * h, h)
            own.append(_remote(srcs[a], outs[a].at[me], lsem.at[a], osem.at[a], (x, y, 1 - c)))
            for k, (px, py) in enumerate(_other_chips(x, y)):
                s1, r1, s2, r2 = ssem.at[6 * a + k], rsem.at[6 * a + k], ssem.at[6 * a + 3 + k], rsem.at[6 * a + 3 + k]
                got, got_sib = outs[a].at[2 * px + py, mine], outs[a].at[2 * px + py, other]
                ici.append(_remote(srcs[a].at[mine], outs[a].at[me, mine], s1, r1, (px, py, c)))
                landed.append(_remote(got, got, s1, r1, (px, py, c)))
                passed.append(_remote(got, got, s2, r2, (x, y, 1 - c)))
                passed_in.append(_remote(got_sib, got_sib, s2, r2, (x, y, 1 - c)))
        return own, ici, landed, passed, passed_in

    def first(self, srcs, outs, sems):
        own, ici, _, _, _ = self._copies(srcs, outs, sems)
        for cp in own + ici:
            cp.start()

    def mid(self, srcs, outs, sems):
        _, _, landed, passed, _ = self._copies(srcs, outs, sems)
        for arrived, onward in zip(landed, passed):
            arrived.wait_recv()
            onward.start()

    def last(self, srcs, outs, sems):
        own, ici, _, passed, passed_in = self._copies(srcs, outs, sems)
        for cp in passed_in:
            cp.wait_recv()
        for cp in ici + passed:
            cp.wait_send()
        for cp in own:
            cp.wait()


class _ScatterToChips:
    def __init__(self, parts):
        self.ins = list(parts)
        n = self.n = len(parts)
        self.out_shape = [jax.ShapeDtypeStruct(t.shape, t.dtype) for t in parts]
        self.sems = [pltpu.SemaphoreType.DMA((3 * n,)), pltpu.SemaphoreType.DMA((3 * n,)), pltpu.SemaphoreType.DMA((n,))]

    def _copies(self, srcs, outs, sems):
        ssem, rsem, lsem = sems
        x, y, c = _me()
        me = 2 * x + y
        own, out, landed = [], [], []
        for a in range(self.n):
            own.append(pltpu.make_async_copy(srcs[a].at[me], outs[a].at[me], lsem.at[a]))
            for k, (px, py) in enumerate(_other_chips(x, y)):
                dst = outs[a].at[2 * px + py]
                out.append(_remote(srcs[a].at[2 * px + py], outs[a].at[me], ssem.at[3 * a + k], rsem.at[3 * a + k],
                                   (px, py, c)))
                landed.append(_remote(dst, dst, ssem.at[3 * a + k], rsem.at[3 * a + k], (px, py, c)))
        return own, out, landed

    def first(self, srcs, outs, sems):
        own, out, _ = self._copies(srcs, outs, sems)
        for cp in own + out:
            cp.start()

    mid = None

    def last(self, srcs, outs, sems):
        own, out, landed = self._copies(srcs, outs, sems)
        for cp in landed:
            cp.wait_recv()
        for cp in own:
            cp.wait()
        for cp in out:
            cp.wait_send()


def _run_comm(comm, name):
    n = comm.n

    def body(*refs):
        srcs, outs, sems = refs[:n], refs[n:2 * n], refs[2 * n:]
        comm.first(srcs, outs, sems)
        if comm.mid is not None:
            comm.mid(srcs, outs, sems)
        comm.last(srcs, outs, sems)

    return _pcall(body, name=name, in_specs=[_HBM] * n, out_specs=[_HBM] * n, out_shape=comm.out_shape,
                  scratch_shapes=comm.sems)(*comm.ins)


class _NoComm:
    n, ins, out_shape, sems, mid = 0, [], [], [], None

    def first(self, srcs, outs, sems):
        pass

    def last(self, srcs, outs, sems):
        pass


_NOTHING = _NoComm()


class _SiblingHalves:
    mid = None

    def __init__(self, grads):
        self.ins = list(grads)
        n = self.n = len(grads)
        self.out_shape = [jax.ShapeDtypeStruct((4, t.shape[1] // 2, t.shape[2]), t.dtype) for t in grads]
        self.sems = [pltpu.SemaphoreType.DMA((n,)), pltpu.SemaphoreType.DMA((n,))]

    def _copies(self, srcs, outs, sems):
        ssem, rsem = sems
        x, y, c = _me()
        copies = []
        for a in range(self.n):
            h = self.ins[a].shape[1] // 2
            copies.append(_remote(srcs[a].at[:, pl.ds((1 - c) * h, h)], outs[a], ssem.at[a], rsem.at[a], (x, y, 1 - c)))
        return copies

    def first(self, srcs, outs, sems):
        for cp in self._copies(srcs, outs, sems):
            cp.start()

    def last(self, srcs, outs, sems):
        for cp in self._copies(srcs, outs, sems):
            cp.wait()


def _reduce_finish(reds, name):
    n = len(reds)

    def body(*refs):
        outs = refs[n:2 * n]
        ssem, rsem = refs[2 * n:]
        x, y, c = _me()
        copies = []
        for a in range(n):
            h = reds[a].shape[0] // 2
            mine = outs[a].at[pl.ds(c * h, h)]
            copies.append(_remote(mine, mine, ssem.at[a], rsem.at[a], (x, y, 1 - c)))
        for cp in copies:
            cp.start()
        for a in range(n):
            h = reds[a].shape[0] // 2
            dst = outs[a].at[pl.ds((1 - c) * h, h)]
            _remote(dst, dst, ssem.at[a], rsem.at[a], (x, y, 1 - c)).wait_recv()
        for cp in copies:
            cp.wait_send()

    return _pcall(
        body, name=name, in_specs=[_HBM] * n, out_specs=[_HBM] * n,
        out_shape=[jax.ShapeDtypeStruct(t.shape, t.dtype) for t in reds],
        input_output_aliases={a: a for a in range(n)},
        scratch_shapes=[pltpu.SemaphoreType.DMA((n,)), pltpu.SemaphoreType.DMA((n,))],
    )(*reds)


def _half_sum(fn, full, halves, out_full, out_dtype, core, name):
    p, h, c = (halves[0].shape if halves else (full[0].shape[0], full[0].shape[1] // 2, full[0].shape[2]))
    br = _div(h, max(16, (1 << 19) // (p * c)), 16)
    nb = h // br
    mine3 = pl.BlockSpec((p, br, c), lambda i, core_ref: (0, core_ref[0] * nb + i, 0))
    half3 = pl.BlockSpec((p, br, c), lambda i, core_ref: (0, i, 0))

    def body(core_ref, *refs):
        refs[-1][...] = fn(*[t[...].astype(F32) for t in refs[:-1]]).astype(out_dtype)

    if out_full:
        out_spec = pl.BlockSpec((br, c), lambda i, core_ref: (core_ref[0] * nb + i, 0))
        out_shape = jax.ShapeDtypeStruct((2 * h, c), out_dtype)
    else:
        out_spec, out_shape = half3, jax.ShapeDtypeStruct((p, h, c), out_dtype)
    return _pcall(
        body, name=name,
        grid_spec=pltpu.PrefetchScalarGridSpec(
            num_scalar_prefetch=1, grid=(nb,), in_specs=[mine3] * len(full) + [half3] * len(halves),
            out_specs=out_spec),
        out_shape=out_shape, compiler_params=_cparams(("parallel",)),
    )(core, *full, *halves)


def _ada_fwd(c_all, w, b):
    def body(c_ref, w_ref, b_ref, o_ref):
        o_ref[...] = jnp.dot(c_ref[...], w_ref[...], precision=HI, preferred_element_type=F32) + b_ref[...]

    return _pcall(body, name="ada_fwd", out_shape=jax.ShapeDtypeStruct((c_all.shape[0], w.shape[1]), F32),
                  compiler_params=pltpu.CompilerParams(vmem_limit_bytes=VMEM_LIMIT))(c_all, w, b)


def _ada_bwd(c_all_t, d):
    def body(c_ref, d_ref, o_ref):
        o_ref[...] = jnp.dot(c_ref[...], d_ref[...], precision=HI, preferred_element_type=F32)

    return _pcall(body, name="ada_bwd", out_shape=jax.ShapeDtypeStruct((c_all_t.shape[0], d.shape[1]), F32),
                  compiler_params=pltpu.CompilerParams(vmem_limit_bytes=VMEM_LIMIT))(c_all_t, d)


def _sum_lead(x, name):
    p, r, n = x.shape
    br = _div(r, 512, 8)

    def body(x_ref, o_ref):
        acc = x_ref[0]
        for j in range(1, p):
            acc = acc + x_ref[j]
        o_ref[...] = acc

    return _pcall(
        body, name=name, grid=(r // br,), in_specs=[pl.BlockSpec((p, br, n), lambda i: (0, i, 0))],
        out_specs=pl.BlockSpec((br, n), lambda i: (i, 0)), out_shape=jax.ShapeDtypeStruct((r, n), F32),
        compiler_params=_cparams(("parallel",)),
    )(x)


def _adamw(w, g, m, v, name):
    shape = w.shape
    cols = shape[-1]
    w2, g2, m2, v2 = [t.reshape(-1, cols) for t in (w, g, m, v)]
    rows = w2.shape[0]
    pref = max(8, (1 << 19) // cols // 8 * 8)
    br = _div(rows, pref, 8)
    if rows // br > 64:
        br = pref
    outs = _rows_fwd(_f_adamw, [(t, cols, 0) for t in (w2, g2, m2, v2)], [], [(cols, F32)] * 3, name=name, br=br)
    return [o.reshape(shape) for o in outs]


_BIG = (("w_in", 1), ("w_up", 1), ("w_down", 0), ("w_o", 0), ("w_rwkv_out", 0), ("w_att_out", 1), ("w2", 1), ("a2", 1),
        ("g2", 1))


_NEEDED_FIRST = ("w_in", "w_att_out", "w2", "a2", "g2")
_NEEDED_LATER = ("w_up", "w_down", "w_o", "w_rwkv_out")
_DONE_EARLY = ("w_up", "w_down", "w_o", "w_rwkv_out", "w_att_out")
_DONE_LATE = ("w_in", "w2", "a2", "g2")


def _cols_joined(t):
    return jnp.concatenate([t[j] for j in range(4)], axis=1)


def _cols_split(t):
    n = t.shape[1] // 4
    return jnp.stack([t[:, j * n:(j + 1) * n] for j in range(4)])


W_IN_SHARD = (N_ATT + N_RW + N_GATE) // 4
W_IN_PAD = 2560


def _row_window(parts, lo, hi):
    out, pos = [], 0
    for t, w in parts:
        a, b = max(lo, pos), min(hi, pos + w)
        if a < b:
            out.append(t[a - pos:b - pos])
        pos += w
    return out[0] if len(out) == 1 else jnp.concatenate(out, axis=0)


def _rows_joined(t):
    return t.reshape(4 * t.shape[1], t.shape[2])


def _rows_split(t):
    return t.reshape(4, t.shape[0] // 4, t.shape[1])


def _step_to_scan(x, tgt, ada, wts):
    sh1, sc1, gt1, sh2, sc2, gt2 = ada
    br = 256
    grp = lax.broadcasted_iota(jnp.int32, (D, 128), 0) // 64 == lax.broadcasted_iota(jnp.int32, (D, 128), 1)
    e = grp.astype(F32)
    et = e.T
    w_in = [(wts["w_in"][j], W_IN_SHARD) for j in range(4)]
    w_att = _row_window(w_in, 0, N_ATT)
    w_rw = jnp.concatenate([_row_window(w_in, N_ATT, N_ATT + N_RW), jnp.zeros((N_RWP - N_RW, D), BF16)], axis=0)
    w_gate = _row_window(w_in, N_ATT + N_RW, N_ATT + N_RW + N_GATE)
    mu = jnp.pad(wts["mu_shift"], ((0, 0), (0, N_RWP - N_RW)))
    wl = jnp.zeros((N_LORA, 3 * D), F32)
    wl = wl.at[0:64, 0:D].set(_cols_joined(wts["w2"]).astype(F32))
    wl = wl.at[64:128, D:2 * D].set(_cols_joined(wts["a2"]).astype(F32))
    wl = wl.at[128:288, 2 * D:3 * D].set(_cols_joined(wts["g2"]).astype(F32))
    pre1_c = [wts["norm1_w"], sc1, sh1]
    (h1,) = _rows_fwd(_f_pre, [(x, D, 0)], pre1_c, [(D, BF16), None], name="pre1_fwd", br=2 * br)
    att_in = _mm(h1, w_att, tb=True, name="mm_att_in")
    z = _mm(h1, w_rw, tb=True, out_dtype=BF16, name="mm_rw_in")
    gate_in = _mm(h1, w_gate, tb=True, out_dtype=BF16, name="mm_gate_in")
    att_o, att_l = [], []
    for g, (_, dil) in enumerate(ATT_PATTERNS):
        o, l = _att_fwd(att_in, g, dil)
        att_o.append(o)
        att_l.append(l)
    comb_rows = [(t, ATT_WIDTH, 0) for t in att_o + att_l]
    (att,) = _rows_fwd(_f_comb, comb_rows, [], [(ATT_WIDTH, BF16)], name="comb_fwd", br=2 * br)
    w_ao = _cols_joined(wts["w_att_out"])
    y_att = _mm(att, w_ao, out_dtype=BF16, name="mm_att_out")
    rwpre_c = [wts["w0"], wts["a0"], wts["k_k"], wts["k_a"], wl, e, et]

    def shift_and_rwpre(zz, *rest):
        consts, mu_row, before = rest[:-2], rest[-2], rest[-1]
        last = jnp.sum(jnp.where(lax.broadcasted_iota(jnp.int32, before.shape, 0) == HALO - 1, before, 0.0), axis=0,
                       keepdims=True)
        row = lax.broadcasted_iota(jnp.int32, zz.shape, 0)
        zprev = jnp.where(row == 0, last, pltpu.roll(zz, 1, 0))
        shifted = zz + (zprev - zz) * mu_row
        return (shifted,) + tuple(_f_rwpre(shifted, *consts))

    zs, lw, km, aa, bb, gg = _rows_fwd(
        shift_and_rwpre, [(z, N_RWP, 0)], rwpre_c + [mu],
        [(N_RWP, F32), None, (D, F32), (D, F32), None, (D, F32), (D, F32), (D, F32)], name="rwpre_fwd", br=br, halo=0)
    return dict(x=x, tgt=tgt, wts=wts, br=br, e=e, et=et, gt1=gt1, sc2=sc2, sh2=sh2, gt2=gt2, w_att=w_att, w_rw=w_rw,
                w_ao=w_ao,
                w_gate=w_gate, mu=mu, pre1_c=pre1_c, h1=h1, att_in=att_in, z=z, gate_in=gate_in, comb_rows=comb_rows,
                att=att, y_att=y_att, zs=zs, rwpre_c=rwpre_c, lw=lw, km=km, aa=aa, bb=bb, gg=gg)


def _step_between_scans(st, y_raw, late):
    x, tgt, wts, br, e, et = st["x"], st["tgt"], st["wts"], st["br"], st["e"], st["et"]
    zs, km, gg, gate_in, y_att, att = st["zs"], st["km"], st["gg"], st["gate_in"], st["y_att"], st["att"]
    comb_rows, att_in = st["comb_rows"], st["att_in"]
    gt1, sc2, sh2, gt2 = st["gt1"], st["sc2"], st["sh2"], st["gt2"]
    w_up, w_ao = late["w_up"], st["w_ao"]
    w_down, w_o, w_ro = _rows_joined(late["w_down"]), _rows_joined(late["w_o"]), _rows_joined(late["w_rwkv_out"])
    post_rows = [(y_raw, D, 0), (zs, D, 0), (zs, D, 2), (km, D, 0), (gg, D, 0)]
    post_c = [wts["lnx_w"], wts["lnx_b"], wts["r_k"], e, et]
    (rw_out,) = _rows_fwd(_f_rwpost, post_rows, post_c, [(D, BF16)], name="rwpost_fwd", br=br)
    y_rw = _mm(rw_out, w_ro, out_dtype=BF16, name="mm_rw_out")
    mix_rows = [(gate_in, N_GATE, 0), (y_att, D, 0), (y_rw, D, 0)]
    (mix,) = _rows_fwd(_f_mix, mix_rows, [wts["b_gate"]], [(D, BF16)], name="mix_fwd", br=2 * br)
    o = _mm(mix, w_o, out_dtype=BF16, name="mm_o")
    pre2_c = [gt1, wts["norm2_w"], sc2, sh2]
    x1, h2 = _rows_fwd(_f_pre2, [(x, D, 0), (o, D, 0)], pre2_c, [(D, F32), (D, BF16)], name="pre2_fwd", br=2 * br)
    u = _mm(h2, w_up, b_chip=True, name="mm_up")
    act = _conv_fwd(u, wts["conv_w"], wts["conv_b"])
    f = _mm(act, w_down, out_dtype=BF16, name="mm_down")
    fin_rows = [(x1, D, 0), (f, D, 0), (tgt, D, 0)]
    fin_c = [gt2, wts["norm_f_w"]]

    def fin_fwd(*a):
        (l,) = _f_fin(*a)
        return (jnp.broadcast_to(jnp.sum(l, axis=0, keepdims=True), (8, 128)),)

    (loss_acc,) = _rows_fwd(fin_fwd, fin_rows, fin_c, [], name="fin_fwd", br=2 * br, acc_shape=(8, 128))

    gw = {}
    dx1a, df, d_gt2, gw["norm_f_w"] = _rows_bwd(
        _f_fin, fin_rows, fin_c, [[]], wrt_rows=[0, 1], wrt_consts=[0, 1], drow_dtypes=[F32, BF16],
        name="fin_bwd", br=2 * br, unit_cot=True)
    dact = _mm(df, w_down, tb=True, name="mm_dact")
    gw["w_down"] = _rows_split(_mm(act, df, ta=True, out_dtype=BF16, name="mm_dw_down"))
    du, gw["conv_w"], gw["conv_b"] = _conv_bwd(u, wts["conv_w"], wts["conv_b"], dact)
    dh2 = _mm(du, w_up, tb=True, b_chip=True, out_dtype=BF16, name="mm_dh2")
    gw["w_up"] = _mm(h2, du, ta=True, out_chip=True, out_dtype=BF16, name="mm_dw_up")
    dxa, do, d_gt1, gw["norm2_w"], d_sc2, d_sh2 = _rows_bwd(
        _f_pre2, [(x, D, 0), (o, D, 0)], pre2_c, [[(dx1a, D, 0)], [(dh2, D, 0)]], wrt_rows=[0, 1],
        wrt_consts=[0, 1, 2, 3], drow_dtypes=[F32, BF16], name="pre2_bwd", br=2 * br)
    dmix = _mm(do, w_o, tb=True, out_dtype=BF16, name="mm_dmix")
    gw["w_o"] = _rows_split(_mm(mix, do, ta=True, out_dtype=BF16, name="mm_dw_o"))
    dgate, dya, dyr, gw["b_gate"] = _rows_bwd(
        _f_mix, mix_rows, [wts["b_gate"]], [[(dmix, D, 0)]], wrt_rows=[0, 1, 2], wrt_consts=[0],
        drow_dtypes=[BF16] * 3, name="mix_bwd", br=2 * br)
    datt = _mm(dya, w_ao, tb=True, out_dtype=BF16, name="mm_datt")
    gw["w_att_out"] = _mm(att, dya, ta=True, out_chip=True, out_dtype=BF16, name="mm_dw_att_out")
    drw = _mm(dyr, w_ro, tb=True, out_dtype=BF16, name="mm_drw")
    gw["w_rwkv_out"] = _rows_split(_mm(rw_out, dyr, ta=True, out_dtype=BF16, name="mm_dw_rw_out"))
    dcomb = _rows_bwd(_f_comb, comb_rows, [], [[(datt, ATT_WIDTH, 0)]], wrt_rows=list(range(6)), wrt_consts=[],
                      drow_dtypes=[F32] * 6, name="comb_bwd", br=2 * br)
    datt_in = None
    for g, (_, dil) in enumerate(ATT_PATTERNS):
        datt_in = _att_bwd(att_in, g, dil, dcomb[g], dcomb[3 + g], datt_in)
    dy_raw, dr_p, dv_p, dkm_p, dgg, gw["lnx_w"], gw["lnx_b"], gw["r_k"], *recv_early = _rows_bwd(
        _f_rwpost, post_rows, post_c, [[(drw, D, 0)]], wrt_rows=[0, 1, 2, 3, 4], wrt_consts=[0, 1, 2],
        drow_dtypes=[F32] * 5, name="rwpost_bwd", br=br, comm=_SiblingHalves([gw[n] for n in _DONE_EARLY]))
    st.update(loss=loss_acc[0, 0], gw=gw, dxa=dxa, dgate=dgate, datt_in=datt_in,
              dy_raw=dy_raw, dr_p=dr_p, dv_p=dv_p, dkm_p=dkm_p, dgg=dgg, d_ada_late=(d_gt1, d_sh2, d_sc2, d_gt2),
              recv_early=recv_early)
    return st


def _chip_parts(grads, recv, names, core):
    return [_half_sum(lambda a, b: a + b, [g], [r], False, BF16, core, "reduce_add2_" + n)
            for g, r, n in zip(grads, recv, names)]


def _step_after_scan(st, scan_grads, core):
    x, br, gw, h1, zs = st["x"], st["br"], st["gw"], st["h1"], st["zs"]
    dr_s, dlw, dkm_s, dv_s, daa, dbb = scan_grads
    pre_cots = [[(st["dr_p"], D, 0), (dr_s, D, 0)], [(dlw, D, 0)], [(st["dkm_p"], D, 0), (dkm_s, D, 0)],
                [(st["dv_p"], D, 0), (dv_s, D, 0)], [(daa, D, 0)], [(dbb, D, 0)], [(st["dgg"], D, 0)]]
    dz, dmu, gw["w0"], gw["a0"], gw["k_k"], gw["k_a"], dwl = _rwpre_shift_bwd(
        zs, st["z"], st["mu"], st["rwpre_c"], pre_cots, br=128)
    gw["w2"], gw["a2"] = _cols_split(dwl[0:64, 0:D]), _cols_split(dwl[64:128, D:2 * D])
    gw["g2"] = _cols_split(dwl[128:288, 2 * D:3 * D])
    gw["mu_shift"] = dmu[:, :N_RW]
    datt_in, dgate = st["datt_in"], st["dgate"]
    dw_in = [(_mm(datt_in, h1, ta=True, out_dtype=BF16, name="mm_dw_att"), N_ATT),
             (_mm(dz, h1, ta=True, out_dtype=BF16, name="mm_dw_rw"), N_RW),
             (_mm(dgate, h1, ta=True, out_dtype=BF16, name="mm_dw_gate"), N_GATE)]
    slabs = []
    for j in range(4):
        slabs += [_row_window(dw_in, j * W_IN_SHARD, (j + 1) * W_IN_SHARD), jnp.zeros((W_IN_PAD - W_IN_SHARD, D), BF16)]
    gw["w_in"] = jnp.concatenate(slabs, axis=0).reshape(4, W_IN_PAD, D)
    late = [gw[n] for n in _DONE_LATE]
    parts = _chip_parts(late, _run_comm(_SiblingHalves(late), "reduce_sib_late"), _DONE_LATE, core)
    dh1, slots_late = _mm_sum([(datt_in, st["w_att"]), (dz, st["w_rw"]), (dgate, st["w_gate"])],
                              comm=_ScatterToChips(parts), name="mm_dh1")
    grad_x, gw["norm1_w"], d_sc1, d_sh1 = _rows_bwd(
        _f_pre, [(x, D, 0)], st["pre1_c"], [[(dh1, D, 0)], [(st["dxa"], D, 0)]], wrt_rows=[0], wrt_consts=[0, 1, 2],
        drow_dtypes=[F32], name="pre1_bwd", br=2 * br)
    d_gt1, d_sh2, d_sc2, d_gt2 = st["d_ada_late"]
    return st["loss"], grad_x, (d_sh1, d_sc1, d_gt1, d_sh2, d_sc2, d_gt2), gw, slots_late


_SMALL = ("b_ada", "norm1_w", "b_gate", "mu_shift", "w0", "a0", "k_k", "k_a", "r_k", "lnx_w", "lnx_b", "norm2_w",
          "conv_b", "norm_f_w")
_NAMES = ("w_ada", "b_ada", "norm1_w", "w_in", "b_gate", "mu_shift", "w0", "w2", "a0", "a2", "g2", "k_k", "k_a", "r_k",
          "lnx_w", "lnx_b", "w_att_out", "w_rwkv_out", "w_o", "norm2_w", "w_up", "conv_w", "conv_b", "w_down",
          "norm_f_w")


def kernel(x, c, w_ada, b_ada, norm1_w, w_in, b_gate, mu_shift, w0, w2, a0, a2, g2, k_k, k_a, r_k, lnx_w, lnx_b, w_att_out, w_rwkv_out, w_o, norm2_w, w_up, conv_w, conv_b, w_down, norm_f_w, loss_target, m_w_ada, m_b_ada, m_norm1_w, m_w_in, m_b_gate, m_mu_shift, m_w0, m_w2, m_a0, m_a2, m_g2, m_k_k, m_k_a, m_r_k, m_lnx_w, m_lnx_b, m_w_att_out, m_w_rwkv_out, m_w_o, m_norm2_w, m_w_up, m_conv_w, m_conv_b, m_w_down, m_norm_f_w, v_w_ada, v_b_ada, v_norm1_w, v_w_in, v_b_gate, v_mu_shift, v_w0, v_w2, v_a0, v_a2, v_g2, v_k_k, v_k_a, v_r_k, v_lnx_w, v_lnx_b, v_w_att_out, v_w_rwkv_out, v_w_o, v_norm2_w, v_w_up, v_conv_w, v_conv_b, v_w_down, v_norm_f_w):
    args = dict(locals())
    p, pm, pv = {}, {}, {}
    for name in _NAMES:
        for dst, key in ((p, name), (pm, "m_" + name), (pv, "v_" + name)):
            t = args[key]
            if name == "w_in":
                dst[name] = jnp.swapaxes(t, 1, 2)[0]
            else:
                dst[name] = t.reshape(1, -1) if name in ("r_k", "norm_f_w") else t.reshape(t.shape[-2], t.shape[-1])
    xi, yi, ci = _me()
    chip = 2 * xi + yi
    dev = 4 * xi + 2 * yi + ci
    x2, tgt = x[0], loss_target[0]

    n_cw = 3 * (2 * D_FF // 4)
    vec = jnp.concatenate([c.reshape(-1), p["conv_w"].reshape(-1), jnp.zeros((8 * D - D - n_cw,), F32)]).reshape(8, D)
    g0 = _allgather8(vec, "gather_c").reshape(8, 8 * D)
    c_all = g0[:, :D]
    conv_w_full = jnp.concatenate([g0[2 * j, D:D + n_cw].reshape(3, -1) for j in range(4)], axis=1)
    n_ada = 6 * D // 4
    b_ada_sh = lax.dynamic_slice(p["b_ada"], (0, chip * n_ada), (1, n_ada))
    ada_sh = _ada_fwd(c_all, p["w_ada"], b_ada_sh)
    ga = _allgather8(ada_sh, "gather_ada")
    ada_all = jnp.concatenate([ga[2 * j] for j in range(4)], axis=1)
    ada_row = lax.dynamic_slice(ada_all, (dev, 0), (1, 6 * D))
    ada = [ada_row[:, j * D:(j + 1) * D] for j in range(6)]

    big = [n for n, _ in _BIG]
    shard = {n: p[n].astype(BF16) for n in big}
    shard["w_in"] = jnp.pad(shard["w_in"], ((0, W_IN_PAD - W_IN_SHARD), (0, 0)))
    wts = dict(zip(_NEEDED_FIRST, _run_comm(_GatherWeights([shard[n] for n in _NEEDED_FIRST]), "gather_w")))
    for n in _SMALL:
        wts[n] = p[n]
    wts["conv_w"] = conv_w_full
    core = ci.reshape(1).astype(jnp.int32)

    st = _step_to_scan(x2, tgt, ada, wts)
    y_raw, s0s, inverses, late = _scan_fwd(st["zs"], st["lw"], st["km"], st["aa"], st["bb"],
                                           _GatherWeights([shard[n] for n in _NEEDED_LATER]))
    st = _step_between_scans(st, y_raw, dict(zip(_NEEDED_LATER, late)))
    early = _chip_parts([st["gw"][n] for n in _DONE_EARLY], st["recv_early"], _DONE_EARLY, core)
    scan_grads, slots_early = _scan_bwd(st["zs"], st["lw"], st["km"], st["aa"], st["bb"], s0s, inverses,
                                        st["dy_raw"], _ScatterToChips(early))
    loss_part, grad_x, d_ada, gw, slots_late = _step_after_scan(st, scan_grads, core)

    small = [jnp.concatenate(d_ada, axis=1)] + [gw[n] for n in _SMALL[1:]] + [gw["conv_w"], loss_part.reshape(1, 1)]
    sizes = [t.size for t in small]
    flat = jnp.concatenate([t.reshape(-1) for t in small])
    npad = (-flat.shape[0]) % (8 * D)
    srows = (flat.shape[0] + npad) // D
    flat = jnp.concatenate([flat, jnp.zeros((npad,), F32)]).reshape(srows, D)
    parts = _allgather8(flat, "gather_small")
    tot = _sum_lead(parts, "sum_small").reshape(-1)
    pieces, pos = [], 0
    for sz in sizes:
        pieces.append(tot[pos:pos + sz])
        pos += sz
    grads = {}
    for n, piece in zip(_SMALL, pieces[:len(_SMALL)]):
        grads[n] = piece.reshape(p[n].shape)
    conv_w_grad = pieces[len(_SMALL)].reshape(3, 2 * D_FF)
    grads["conv_w"] = lax.dynamic_slice(conv_w_grad, (0, chip * (n_cw // 3)), (3, n_cw // 3))
    loss = pieces[-1][0]
    d_ada_all = parts[:, :6].reshape(8, 6 * D)
    grads["w_ada"] = _ada_bwd(c_all.T, lax.dynamic_slice(d_ada_all, (0, chip * n_ada), (8, n_ada)))

    order = _DONE_EARLY + _DONE_LATE
    reds = [_half_sum(lambda t: t[0] + t[1] + t[2] + t[3], [], [t], True, F32, core, "reduce_add4_" + n)
            for n, t in zip(order, list(slots_early) + list(slots_late))]
    for n, g in zip(order, _reduce_finish(reds, "reduce_sib2")):
        grads[n] = g

    outs_g, outs_d, outs_m, outs_v = [], [], [], []
    grads["w_in"] = grads["w_in"][:W_IN_SHARD]
    for name in _NAMES:
        g = grads[name]
        d, m, v = _adamw(p[name], g, pm[name], pv[name], "adamw_" + name)
        shape = args[name].shape
        for outs, t in ((outs_g, g), (outs_d, d), (outs_m, m), (outs_v, v)):
            outs.append(jnp.swapaxes(t[None], 1, 2) if name == "w_in" else t.reshape(shape))
    return (loss, grad_x.reshape(x.shape), *outs_g, *outs_d, *outs_m, *outs_v)
```
